```python
import math
import jax, jax.numpy as jnp
from jax import lax
import numpy as np

D_MODEL = 1024
BATCH = 8
SEQ = 8192
DEPTH = 1

MEM_LEN = 256
N_BRANCH = 3
CONV_WIDTH = D_MODEL // 2
CONV_TAPS = 3
SSM_WIDTH = D_MODEL // 2
SSM_GROUP = 16
SSM_GROUPS = SSM_WIDTH // SSM_GROUP
SSM_STATE = 64
XATTN_HEADS = 4
XATTN_HEAD_DIM = 128
XATTN_WIDTH = XATTN_HEADS * XATTN_HEAD_DIM
D_FF = 4 * D_MODEL
GATE_COLS = N_BRANCH * D_MODEL
IN_COLS = GATE_COLS + 3 * CONV_WIDTH + SSM_WIDTH + XATTN_WIDTH
ALPHA = (2.0 * DEPTH) ** 0.25
BETA = (8.0 * DEPTH) ** -0.25
LN_EPS = 1e-5
DT_MIN = 1e-3
DT_MAX = 1e-1

kernel_name = "hybrid_gated_conv_s5_xattn_deepnorm"


def layer_norm(x, g, b):
    xf = x.astype(jnp.float32)
    mu = jnp.mean(xf, axis=-1, keepdims=True)
    var = jnp.mean(jnp.square(xf - mu), axis=-1, keepdims=True)
    y = (xf - mu) * lax.rsqrt(var + LN_EPS) * g.astype(jnp.float32) + b.astype(jnp.float32)
    return y.astype(x.dtype)


def causal_dwconv(z, w):
    s = z.shape[1]
    zp = jnp.pad(z, ((0, 0), (CONV_TAPS - 1, 0), (0, 0)))
    y = w[0] * zp[:, 0:s]
    for k in range(1, CONV_TAPS):
        y = y + w[k] * zp[:, k:k + s]
    return y


def _complex_affine_combine(e1, e2):
    a1r, a1i, b1r, b1i = e1
    a2r, a2i, b2r, b2i = e2
    ar = a1r * a2r - a1i * a2i
    ai = a1r * a2i + a1i * a2r
    br = a2r * b1r - a2i * b1i + b2r
    bi = a2r * b1i + a2i * b1r + b2i
    return (ar, ai, br, bi)


def s5_mixer(u, lam_re, lam_im, log_dt, b_re, b_im, c_re, c_im, d_skip):
    bsz, s, _ = u.shape
    f32 = jnp.float32
    uf = u.astype(f32).reshape(bsz, s, SSM_GROUPS, SSM_GROUP)
    lr = lam_re.astype(f32)
    li = lam_im.astype(f32)
    dt = jnp.exp(log_dt.astype(f32))[:, None]
    mag = jnp.exp(lr * dt)
    abar_r = mag * jnp.cos(li * dt)
    abar_i = mag * jnp.sin(li * dt)
    den = lr * lr + li * li
    nr = abar_r - 1.0
    ni = abar_i
    kr = (nr * lr + ni * li) / den
    ki = (ni * lr - nr * li) / den
    br_ = b_re.astype(f32)
    bi_ = b_im.astype(f32)
    bbar_r = kr[..., None] * br_ - ki[..., None] * bi_
    bbar_i = kr[..., None] * bi_ + ki[..., None] * br_
    bu_r = jnp.einsum("bsgh,gph->bsgp", uf, bbar_r)
    bu_i = jnp.einsum("bsgh,gph->bsgp", uf, bbar_i)
    a_r = jnp.broadcast_to(abar_r, bu_r.shape)
    a_i = jnp.broadcast_to(abar_i, bu_r.shape)
    _, _, st_r, st_i = lax.associative_scan(_complex_affine_combine, (a_r, a_i, bu_r, bu_i), axis=1)
    y = (jnp.einsum("bsgp,ghp->bsgh", st_r, c_re.astype(f32))
         - jnp.einsum("bsgp,ghp->bsgh", st_i, c_im.astype(f32)))
    y = y.reshape(bsz, s, SSM_WIDTH) + d_skip.astype(f32) * uf.reshape(bsz, s, SSM_WIDTH)
    return y.astype(u.dtype)


def memory_cross_attention(q, mem, w_kv, w_xo):
    bsz, s, _ = q.shape
    kv = jnp.einsum("bmd,dn->bmn", mem, w_kv)
    k, v = jnp.split(kv, 2, axis=-1)
    qh = q.reshape(bsz, s, XATTN_HEADS, XATTN_HEAD_DIM)
    kh = k.reshape(bsz, -1, XATTN_HEADS, XATTN_HEAD_DIM)
    vh = v.reshape(bsz, -1, XATTN_HEADS, XATTN_HEAD_DIM)
    scores = jnp.einsum("bshd,bmhd->bhsm", qh, kh).astype(jnp.float32) * (XATTN_HEAD_DIM ** -0.5)
    probs = jax.nn.softmax(scores, axis=-1).astype(vh.dtype)
    o = jnp.einsum("bhsm,bmhd->bshd", probs, vh).reshape(bsz, s, XATTN_WIDTH)
    return jnp.einsum("bsc,cd->bsd", o, w_xo)


def _fwd_setup_inputs(seed: int = 0) -> dict:
    key = jax.random.key(seed)
    ks = jax.random.split(key, 32)
    f32 = jnp.float32
    L, D = DEPTH, D_MODEL

    def nrm(k, shape, std):
        return jax.random.normal(k, shape, f32) * std

    x = jax.random.normal(ks[0], (BATCH, SEQ, D), f32)
    mem = jax.random.normal(ks[1], (BATCH, MEM_LEN, D), f32)
    w_in = nrm(ks[2], (L, D, IN_COLS), D ** -0.5)
    b_gate = nrm(ks[3], (L, GATE_COLS), 0.02)
    conv_w = nrm(ks[4], (L, CONV_TAPS, CONV_WIDTH), CONV_TAPS ** -0.5)
    w_conv_out = nrm(ks[5], (L, CONV_WIDTH, D), CONV_WIDTH ** -0.5)
    n_idx = jnp.arange(SSM_STATE, dtype=f32)
    ssm_lam_re = -0.5 + nrm(ks[6], (L, SSM_GROUPS, SSM_STATE), 0.01)
    ssm_lam_im = math.pi * n_idx + nrm(ks[7], (L, SSM_GROUPS, SSM_STATE), 0.01)
    ssm_log_dt = jax.random.uniform(ks[8], (L, SSM_GROUPS), f32, math.log(DT_MIN), math.log(DT_MAX))
    bstd = (2.0 * SSM_GROUP) ** -0.5
    ssm_b_re = nrm(ks[9], (L, SSM_GROUPS, SSM_STATE, SSM_GROUP), bstd)
    ssm_b_im = nrm(ks[10], (L, SSM_GROUPS, SSM_STATE, SSM_GROUP), bstd)
    cstd = (2.0 * SSM_STATE) ** -0.5
    ssm_c_re = nrm(ks[11], (L, SSM_GROUPS, SSM_GROUP, SSM_STATE), cstd)
    ssm_c_im = nrm(ks[12], (L, SSM_GROUPS, SSM_GROUP, SSM_STATE), cstd)
    ssm_d = nrm(ks[13], (L, SSM_WIDTH), 1.0)
    w_glu = nrm(ks[14], (L, SSM_WIDTH, 2 * D), SSM_WIDTH ** -0.5)
    w_k = nrm(ks[15], (L, D, XATTN_WIDTH), D ** -0.5)
    w_v = nrm(ks[16], (L, D, XATTN_WIDTH), BETA * D ** -0.5)
    w_kv = jnp.concatenate([w_k, w_v], axis=-1)
    w_xattn_out = nrm(ks[17], (L, XATTN_WIDTH, D), XATTN_WIDTH ** -0.5)
    w_out = nrm(ks[18], (L, D, D), BETA * D ** -0.5)
    ln1_g = 1.0 + nrm(ks[19], (L, D), 0.02)
    ln1_b = nrm(ks[20], (L, D), 0.02)
    w_up = nrm(ks[21], (L, D, D_FF), BETA * D ** -0.5)
    b_up = nrm(ks[22], (L, D_FF), 0.02)
    w_down = nrm(ks[23], (L, D_FF, D), BETA * D_FF ** -0.5)
    b_down = nrm(ks[24], (L, D), 0.02)
    ln2_g = 1.0 + nrm(ks[25], (L, D), 0.02)
    ln2_b = nrm(ks[26], (L, D), 0.02)
    return {"x": x, "mem": mem, "w_in": w_in, "b_gate": b_gate, "conv_w": conv_w,
            "w_conv_out": w_conv_out, "ssm_lam_re": ssm_lam_re, "ssm_lam_im": ssm_lam_im,
            "ssm_log_dt": ssm_log_dt, "ssm_b_re": ssm_b_re, "ssm_b_im": ssm_b_im,
            "ssm_c_re": ssm_c_re, "ssm_c_im": ssm_c_im, "ssm_d": ssm_d, "w_glu": w_glu,
            "w_kv": w_kv, "w_xattn_out": w_xattn_out, "w_out": w_out, "ln1_g": ln1_g,
            "ln1_b": ln1_b, "w_up": w_up, "b_up": b_up, "w_down": w_down, "b_down": b_down,
            "ln2_g": ln2_g, "ln2_b": ln2_b}


def _fwd_reference(x, mem, w_in, b_gate, conv_w, w_conv_out, ssm_lam_re, ssm_lam_im, ssm_log_dt,
              ssm_b_re, ssm_b_im, ssm_c_re, ssm_c_im, ssm_d, w_glu, w_kv, w_xattn_out, w_out,
              ln1_g, ln1_b, w_up, b_up, w_down, b_down, ln2_g, ln2_b):
    bsz, s, d = x.shape
    splits = [GATE_COLS, GATE_COLS + 3 * CONV_WIDTH, GATE_COLS + 3 * CONV_WIDTH + SSM_WIDTH]
    for l in range(DEPTH):
        proj = jnp.einsum("bsd,dn->bsn", x, w_in[l])
        gate_pre, conv_in, u, q = jnp.split(proj, splits, axis=-1)
        gates = jax.nn.sigmoid(gate_pre + b_gate[l]).reshape(bsz, s, N_BRANCH, d)

        cb, cc, ch = jnp.split(conv_in, 3, axis=-1)
        y_a = jnp.einsum("bsc,cd->bsd", cb * causal_dwconv(cc * ch, conv_w[l]), w_conv_out[l])

        y_s = jax.nn.gelu(s5_mixer(u, ssm_lam_re[l], ssm_lam_im[l], ssm_log_dt[l], ssm_b_re[l],
                                   ssm_b_im[l], ssm_c_re[l], ssm_c_im[l], ssm_d[l]))
        glu_a, glu_b = jnp.split(jnp.einsum("bsc,cn->bsn", y_s, w_glu[l]), 2, axis=-1)
        y_b = glu_a * jax.nn.sigmoid(glu_b)

        y_c = memory_cross_attention(q, mem, w_kv[l], w_xattn_out[l])

        merged = gates[:, :, 0] * y_a + gates[:, :, 1] * y_b + gates[:, :, 2] * y_c
        x = layer_norm(ALPHA * x + jnp.einsum("bsd,de->bse", merged, w_out[l]), ln1_g[l], ln1_b[l])

        hdn = jnp.square(jax.nn.relu(jnp.einsum("bsd,df->bsf", x, w_up[l]) + b_up[l]))
        x = layer_norm(ALPHA * x + jnp.einsum("bsf,fd->bsd", hdn, w_down[l]) + b_down[l],
                       ln2_g[l], ln2_b[l])
    return x


import jax as _jax
import jax.numpy as _jnp

TWIN_FORMAT = 'train_step'
FWD_PARAMS = ['x', 'mem', 'w_in', 'b_gate', 'conv_w', 'w_conv_out', 'ssm_lam_re', 'ssm_lam_im', 'ssm_log_dt', 'ssm_b_re', 'ssm_b_im', 'ssm_c_re', 'ssm_c_im', 'ssm_d', 'w_glu', 'w_kv', 'w_xattn_out', 'w_out', 'ln1_g', 'ln1_b', 'w_up', 'b_up', 'w_down', 'b_down', 'ln2_g', 'ln2_b']
TWIN_WEIGHTS = ['w_in', 'b_gate', 'conv_w', 'w_conv_out', 'ssm_lam_re', 'ssm_lam_im', 'ssm_log_dt', 'ssm_b_re', 'ssm_b_im', 'ssm_c_re', 'ssm_c_im', 'ssm_d', 'w_glu', 'w_kv', 'w_xattn_out', 'w_out', 'ln1_g', 'ln1_b', 'w_up', 'b_up', 'w_down', 'b_down', 'ln2_g', 'ln2_b']
TWIN_DIFF_INPUT = 'x'
TWIN_INPUTS = ['x', 'mem', 'w_in', 'b_gate', 'conv_w', 'w_conv_out', 'ssm_lam_re', 'ssm_lam_im', 'ssm_log_dt', 'ssm_b_re', 'ssm_b_im', 'ssm_c_re', 'ssm_c_im', 'ssm_d', 'w_glu', 'w_kv', 'w_xattn_out', 'w_out', 'ln1_g', 'ln1_b', 'w_up', 'b_up', 'w_down', 'b_down', 'ln2_g', 'ln2_b', 'loss_target', 'm_w_in', 'm_b_gate', 'm_conv_w', 'm_w_conv_out', 'm_ssm_lam_re', 'm_ssm_lam_im', 'm_ssm_log_dt', 'm_ssm_b_re', 'm_ssm_b_im', 'm_ssm_c_re', 'm_ssm_c_im', 'm_ssm_d', 'm_w_glu', 'm_w_kv', 'm_w_xattn_out', 'm_w_out', 'm_ln1_g', 'm_ln1_b', 'm_w_up', 'm_b_up', 'm_w_down', 'm_b_down', 'm_ln2_g', 'm_ln2_b', 'v_w_in', 'v_b_gate', 'v_conv_w', 'v_w_conv_out', 'v_ssm_lam_re', 'v_ssm_lam_im', 'v_ssm_log_dt', 'v_ssm_b_re', 'v_ssm_b_im', 'v_ssm_c_re', 'v_ssm_c_im', 'v_ssm_d', 'v_w_glu', 'v_w_kv', 'v_w_xattn_out', 'v_w_out', 'v_ln1_g', 'v_ln1_b', 'v_w_up', 'v_b_up', 'v_w_down', 'v_b_down', 'v_ln2_g', 'v_ln2_b']
TWIN_OUTPUTS = ['loss', 'grad_x', 'grad_w_in', 'grad_b_gate', 'grad_conv_w', 'grad_w_conv_out', 'grad_ssm_lam_re', 'grad_ssm_lam_im', 'grad_ssm_log_dt', 'grad_ssm_b_re', 'grad_ssm_b_im', 'grad_ssm_c_re', 'grad_ssm_c_im', 'grad_ssm_d', 'grad_w_glu', 'grad_w_kv', 'grad_w_xattn_out', 'grad_w_out', 'grad_ln1_g', 'grad_ln1_b', 'grad_w_up', 'grad_b_up', 'grad_w_down', 'grad_b_down', 'grad_ln2_g', 'grad_ln2_b', 'delta_w_in', 'delta_b_gate', 'delta_conv_w', 'delta_w_conv_out', 'delta_ssm_lam_re', 'delta_ssm_lam_im', 'delta_ssm_log_dt', 'delta_ssm_b_re', 'delta_ssm_b_im', 'delta_ssm_c_re', 'delta_ssm_c_im', 'delta_ssm_d', 'delta_w_glu', 'delta_w_kv', 'delta_w_xattn_out', 'delta_w_out', 'delta_ln1_g', 'delta_ln1_b', 'delta_w_up', 'delta_b_up', 'delta_w_down', 'delta_b_down', 'delta_ln2_g', 'delta_ln2_b', 'new_m_w_in', 'new_m_b_gate', 'new_m_conv_w', 'new_m_w_conv_out', 'new_m_ssm_lam_re', 'new_m_ssm_lam_im', 'new_m_ssm_log_dt', 'new_m_ssm_b_re', 'new_m_ssm_b_im', 'new_m_ssm_c_re', 'new_m_ssm_c_im', 'new_m_ssm_d', 'new_m_w_glu', 'new_m_w_kv', 'new_m_w_xattn_out', 'new_m_w_out', 'new_m_ln1_g', 'new_m_ln1_b', 'new_m_w_up', 'new_m_b_up', 'new_m_w_down', 'new_m_b_down', 'new_m_ln2_g', 'new_m_ln2_b', 'new_v_w_in', 'new_v_b_gate', 'new_v_conv_w', 'new_v_w_conv_out', 'new_v_ssm_lam_re', 'new_v_ssm_lam_im', 'new_v_ssm_log_dt', 'new_v_ssm_b_re', 'new_v_ssm_b_im', 'new_v_ssm_c_re', 'new_v_ssm_c_im', 'new_v_ssm_d', 'new_v_w_glu', 'new_v_w_kv', 'new_v_w_xattn_out', 'new_v_w_out', 'new_v_ln1_g', 'new_v_ln1_b', 'new_v_w_up', 'new_v_b_up', 'new_v_w_down', 'new_v_b_down', 'new_v_ln2_g', 'new_v_ln2_b']
TWIN_LEAF_KINDS = {'loss': 'loss', 'grad_x': 'grad_x', 'grad_w_in': 'grad_w', 'grad_b_gate': 'grad_w', 'grad_conv_w': 'grad_w', 'grad_w_conv_out': 'grad_w', 'grad_ssm_lam_re': 'grad_w', 'grad_ssm_lam_im': 'grad_w', 'grad_ssm_log_dt': 'grad_w', 'grad_ssm_b_re': 'grad_w', 'grad_ssm_b_im': 'grad_w', 'grad_ssm_c_re': 'grad_w', 'grad_ssm_c_im': 'grad_w', 'grad_ssm_d': 'grad_w', 'grad_w_glu': 'grad_w', 'grad_w_kv': 'grad_w', 'grad_w_xattn_out': 'grad_w', 'grad_w_out': 'grad_w', 'grad_ln1_g': 'grad_w', 'grad_ln1_b': 'grad_w', 'grad_w_up': 'grad_w', 'grad_b_up': 'grad_w', 'grad_w_down': 'grad_w', 'grad_b_down': 'grad_w', 'grad_ln2_g': 'grad_w', 'grad_ln2_b': 'grad_w', 'delta_w_in': 'delta_w', 'delta_b_gate': 'delta_w', 'delta_conv_w': 'delta_w', 'delta_w_conv_out': 'delta_w', 'delta_ssm_lam_re': 'delta_w', 'delta_ssm_lam_im': 'delta_w', 'delta_ssm_log_dt': 'delta_w', 'delta_ssm_b_re': 'delta_w', 'delta_ssm_b_im': 'delta_w', 'delta_ssm_c_re': 'delta_w', 'delta_ssm_c_im': 'delta_w', 'delta_ssm_d': 'delta_w', 'delta_w_glu': 'delta_w', 'delta_w_kv': 'delta_w', 'delta_w_xattn_out': 'delta_w', 'delta_w_out': 'delta_w', 'delta_ln1_g': 'delta_w', 'delta_ln1_b': 'delta_w', 'delta_w_up': 'delta_w', 'delta_b_up': 'delta_w', 'delta_w_down': 'delta_w', 'delta_b_down': 'delta_w', 'delta_ln2_g': 'delta_w', 'delta_ln2_b': 'delta_w', 'new_m_w_in': 'new_m', 'new_m_b_gate': 'new_m', 'new_m_conv_w': 'new_m', 'new_m_w_conv_out': 'new_m', 'new_m_ssm_lam_re': 'new_m', 'new_m_ssm_lam_im': 'new_m', 'new_m_ssm_log_dt': 'new_m', 'new_m_ssm_b_re': 'new_m', 'new_m_ssm_b_im': 'new_m', 'new_m_ssm_c_re': 'new_m', 'new_m_ssm_c_im': 'new_m', 'new_m_ssm_d': 'new_m', 'new_m_w_glu': 'new_m', 'new_m_w_kv': 'new_m', 'new_m_w_xattn_out': 'new_m', 'new_m_w_out': 'new_m', 'new_m_ln1_g': 'new_m', 'new_m_ln1_b': 'new_m', 'new_m_w_up': 'new_m', 'new_m_b_up': 'new_m', 'new_m_w_down': 'new_m', 'new_m_b_down': 'new_m', 'new_m_ln2_g': 'new_m', 'new_m_ln2_b': 'new_m', 'new_v_w_in': 'new_v', 'new_v_b_gate': 'new_v', 'new_v_conv_w': 'new_v', 'new_v_w_conv_out': 'new_v', 'new_v_ssm_lam_re': 'new_v', 'new_v_ssm_lam_im': 'new_v', 'new_v_ssm_log_dt': 'new_v', 'new_v_ssm_b_re': 'new_v', 'new_v_ssm_b_im': 'new_v', 'new_v_ssm_c_re': 'new_v', 'new_v_ssm_c_im': 'new_v', 'new_v_ssm_d': 'new_v', 'new_v_w_glu': 'new_v', 'new_v_w_kv': 'new_v', 'new_v_w_xattn_out': 'new_v', 'new_v_w_out': 'new_v', 'new_v_ln1_g': 'new_v', 'new_v_ln1_b': 'new_v', 'new_v_w_up': 'new_v', 'new_v_b_up': 'new_v', 'new_v_w_down': 'new_v', 'new_v_b_down': 'new_v', 'new_v_ln2_g': 'new_v', 'new_v_ln2_b': 'new_v'}


def _forward(args):
    return _fwd_reference(*[args[k] for k in FWD_PARAMS])


def _output_shape():
    def fwd():
        inp = _fwd_setup_inputs(0)
        return _fwd_reference(*[inp[k] for k in FWD_PARAMS])
    out = _jax.eval_shape(fwd)
    return out.shape, out.dtype

N_MICROBATCH = 1
ADAM_LR = 0.001
ADAM_B1 = 0.9
ADAM_B2 = 0.999
ADAM_EPS = 1e-08
ADAM_WD = 0.01
ADAM_STEP = 10
PER_EXAMPLE_BATCH_AXIS = {'x': 0, 'mem': 0, 'loss_target': 0}
SHARED_INPUTS = []
_WEIGHT_DTYPES = {'w_in': _jnp.float32, 'b_gate': _jnp.float32, 'conv_w': _jnp.float32, 'w_conv_out': _jnp.float32, 'ssm_lam_re': _jnp.float32, 'ssm_lam_im': _jnp.float32, 'ssm_log_dt': _jnp.float32, 'ssm_b_re': _jnp.float32, 'ssm_b_im': _jnp.float32, 'ssm_c_re': _jnp.float32, 'ssm_c_im': _jnp.float32, 'ssm_d': _jnp.float32, 'w_glu': _jnp.float32, 'w_kv': _jnp.float32, 'w_xattn_out': _jnp.float32, 'w_out': _jnp.float32, 'ln1_g': _jnp.float32, 'ln1_b': _jnp.float32, 'w_up': _jnp.float32, 'b_up': _jnp.float32, 'w_down': _jnp.float32, 'b_down': _jnp.float32, 'ln2_g': _jnp.float32, 'ln2_b': _jnp.float32}
MOMENT_SCALE = {'w_in': 5.065542e-02, 'b_gate': 1.656905e-02, 'conv_w': 9.763337e-02, 'w_conv_out': 6.490995e-02, 'ssm_lam_re': 1.560222e-03, 'ssm_lam_im': 1.704592e-03, 'ssm_log_dt': 1.823225e+00, 'ssm_b_re': 1.129068e-03, 'ssm_b_im': 1.096102e-03, 'ssm_c_re': 2.176240e-03, 'ssm_c_im': 2.208187e-03, 'ssm_d': 5.928632e-02, 'w_glu': 2.954075e-02, 'w_kv': 8.922514e-03, 'w_xattn_out': 4.736112e-03, 'w_out': 1.263135e-01, 'ln1_g': 2.257888e+00, 'ln1_b': 1.036711e+00, 'w_up': 5.177943e-02, 'b_up': 1.155212e-01, 'w_down': 1.450804e-01, 'b_down': 8.697393e-01, 'ln2_g': 6.404039e+01, 'ln2_b': 6.250480e+00}


def _to_microbatches(a, axis):
    t = _jnp.moveaxis(a, axis, 0)
    t = t.reshape((N_MICROBATCH, t.shape[0] // N_MICROBATCH) + t.shape[1:])
    return _jnp.moveaxis(t, 1, axis + 1)


def setup_inputs(seed: int = 0) -> dict:
    inp = _fwd_setup_inputs(seed)
    key = _jax.random.fold_in(_jax.random.key(seed), 7919)
    shape, _ = _output_shape()
    out = dict(inp)
    out["loss_target"] = _jax.random.normal(_jax.random.fold_in(key, 0), shape, _jnp.float32)
    for i, name in enumerate(TWIN_WEIGHTS):
        w = inp[name].astype(_jnp.float32)
        if MOMENT_SCALE is None:
            s = _jnp.sqrt(_jnp.mean(_jnp.square(w)) + 1e-30)
        else:
            s = MOMENT_SCALE[name]
        km, kv = _jax.random.split(_jax.random.fold_in(key, i + 1))
        out[name] = w
        out["m_" + name] = s * _jax.random.normal(km, w.shape, _jnp.float32)
        out["v_" + name] = (s * s) * _jax.random.uniform(kv, w.shape, _jnp.float32, 0.5, 1.5)
    if N_MICROBATCH > 1:
        for name, axis in PER_EXAMPLE_BATCH_AXIS.items():
            out[name] = _to_microbatches(out[name], axis)
    return {'x': out['x'], 'mem': out['mem'], 'w_in': out['w_in'], 'b_gate': out['b_gate'], 'conv_w': out['conv_w'], 'w_conv_out': out['w_conv_out'], 'ssm_lam_re': out['ssm_lam_re'], 'ssm_lam_im': out['ssm_lam_im'], 'ssm_log_dt': out['ssm_log_dt'], 'ssm_b_re': out['ssm_b_re'], 'ssm_b_im': out['ssm_b_im'], 'ssm_c_re': out['ssm_c_re'], 'ssm_c_im': out['ssm_c_im'], 'ssm_d': out['ssm_d'], 'w_glu': out['w_glu'], 'w_kv': out['w_kv'], 'w_xattn_out': out['w_xattn_out'], 'w_out': out['w_out'], 'ln1_g': out['ln1_g'], 'ln1_b': out['ln1_b'], 'w_up': out['w_up'], 'b_up': out['b_up'], 'w_down': out['w_down'], 'b_down': out['b_down'], 'ln2_g': out['ln2_g'], 'ln2_b': out['ln2_b'], 'loss_target': out['loss_target'], 'm_w_in': out['m_w_in'], 'm_b_gate': out['m_b_gate'], 'm_conv_w': out['m_conv_w'], 'm_w_conv_out': out['m_w_conv_out'], 'm_ssm_lam_re': out['m_ssm_lam_re'], 'm_ssm_lam_im': out['m_ssm_lam_im'], 'm_ssm_log_dt': out['m_ssm_log_dt'], 'm_ssm_b_re': out['m_ssm_b_re'], 'm_ssm_b_im': out['m_ssm_b_im'], 'm_ssm_c_re': out['m_ssm_c_re'], 'm_ssm_c_im': out['m_ssm_c_im'], 'm_ssm_d': out['m_ssm_d'], 'm_w_glu': out['m_w_glu'], 'm_w_kv': out['m_w_kv'], 'm_w_xattn_out': out['m_w_xattn_out'], 'm_w_out': out['m_w_out'], 'm_ln1_g': out['m_ln1_g'], 'm_ln1_b': out['m_ln1_b'], 'm_w_up': out['m_w_up'], 'm_b_up': out['m_b_up'], 'm_w_down': out['m_w_down'], 'm_b_down': out['m_b_down'], 'm_ln2_g': out['m_ln2_g'], 'm_ln2_b': out['m_ln2_b'], 'v_w_in': out['v_w_in'], 'v_b_gate': out['v_b_gate'], 'v_conv_w': out['v_conv_w'], 'v_w_conv_out': out['v_w_conv_out'], 'v_ssm_lam_re': out['v_ssm_lam_re'], 'v_ssm_lam_im': out['v_ssm_lam_im'], 'v_ssm_log_dt': out['v_ssm_log_dt'], 'v_ssm_b_re': out['v_ssm_b_re'], 'v_ssm_b_im': out['v_ssm_b_im'], 'v_ssm_c_re': out['v_ssm_c_re'], 'v_ssm_c_im': out['v_ssm_c_im'], 'v_ssm_d': out['v_ssm_d'], 'v_w_glu': out['v_w_glu'], 'v_w_kv': out['v_w_kv'], 'v_w_xattn_out': out['v_w_xattn_out'], 'v_w_out': out['v_w_out'], 'v_ln1_g': out['v_ln1_g'], 'v_ln1_b': out['v_ln1_b'], 'v_w_up': out['v_w_up'], 'v_b_up': out['v_b_up'], 'v_w_down': out['v_w_down'], 'v_b_down': out['v_b_down'], 'v_ln2_g': out['v_ln2_g'], 'v_ln2_b': out['v_ln2_b']}


def _loss(weights, diff, rest, loss_target):
    with _jax.named_scope("forward"):
        args = {**rest, TWIN_DIFF_INPUT: diff, **{k: w.astype(_WEIGHT_DTYPES[k]) for k, w in weights.items()}}
        y = _forward(args)
    with _jax.named_scope("loss_head"):
        err = _jnp.square(y.astype(_jnp.float32) - loss_target)
        return 0.5 * _jnp.sum(_jnp.mean(err, axis=-1)) if err.ndim else 0.5 * err


def _adamw(w, g, m, v):
    m = ADAM_B1 * m + (1.0 - ADAM_B1) * g
    v = ADAM_B2 * v + (1.0 - ADAM_B2) * _jnp.square(g)
    m_hat = m / (1.0 - ADAM_B1 ** ADAM_STEP)
    v_hat = v / (1.0 - ADAM_B2 ** ADAM_STEP)
    delta = -ADAM_LR * (m_hat / (_jnp.sqrt(v_hat) + ADAM_EPS) + ADAM_WD * w)
    return delta, m, v


def reference(x, mem, w_in, b_gate, conv_w, w_conv_out, ssm_lam_re, ssm_lam_im, ssm_log_dt, ssm_b_re, ssm_b_im, ssm_c_re, ssm_c_im, ssm_d, w_glu, w_kv, w_xattn_out, w_out, ln1_g, ln1_b, w_up, b_up, w_down, b_down, ln2_g, ln2_b, loss_target, m_w_in, m_b_gate, m_conv_w, m_w_conv_out, m_ssm_lam_re, m_ssm_lam_im, m_ssm_log_dt, m_ssm_b_re, m_ssm_b_im, m_ssm_c_re, m_ssm_c_im, m_ssm_d, m_w_glu, m_w_kv, m_w_xattn_out, m_w_out, m_ln1_g, m_ln1_b, m_w_up, m_b_up, m_w_down, m_b_down, m_ln2_g, m_ln2_b, v_w_in, v_b_gate, v_conv_w, v_w_conv_out, v_ssm_lam_re, v_ssm_lam_im, v_ssm_log_dt, v_ssm_b_re, v_ssm_b_im, v_ssm_c_re, v_ssm_c_im, v_ssm_d, v_w_glu, v_w_kv, v_w_xattn_out, v_w_out, v_ln1_g, v_ln1_b, v_w_up, v_b_up, v_w_down, v_b_down, v_ln2_g, v_ln2_b):
    given = dict(x=x, mem=mem, w_in=w_in, b_gate=b_gate, conv_w=conv_w, w_conv_out=w_conv_out, ssm_lam_re=ssm_lam_re, ssm_lam_im=ssm_lam_im, ssm_log_dt=ssm_log_dt, ssm_b_re=ssm_b_re, ssm_b_im=ssm_b_im, ssm_c_re=ssm_c_re, ssm_c_im=ssm_c_im, ssm_d=ssm_d, w_glu=w_glu, w_kv=w_kv, w_xattn_out=w_xattn_out, w_out=w_out, ln1_g=ln1_g, ln1_b=ln1_b, w_up=w_up, b_up=b_up, w_down=w_down, b_down=b_down, ln2_g=ln2_g, ln2_b=ln2_b, loss_target=loss_target, m_w_in=m_w_in, m_b_gate=m_b_gate, m_conv_w=m_conv_w, m_w_conv_out=m_w_conv_out, m_ssm_lam_re=m_ssm_lam_re, m_ssm_lam_im=m_ssm_lam_im, m_ssm_log_dt=m_ssm_log_dt, m_ssm_b_re=m_ssm_b_re, m_ssm_b_im=m_ssm_b_im, m_ssm_c_re=m_ssm_c_re, m_ssm_c_im=m_ssm_c_im, m_ssm_d=m_ssm_d, m_w_glu=m_w_glu, m_w_kv=m_w_kv, m_w_xattn_out=m_w_xattn_out, m_w_out=m_w_out, m_ln1_g=m_ln1_g, m_ln1_b=m_ln1_b, m_w_up=m_w_up, m_b_up=m_b_up, m_w_down=m_w_down, m_b_down=m_b_down, m_ln2_g=m_ln2_g, m_ln2_b=m_ln2_b, v_w_in=v_w_in, v_b_gate=v_b_gate, v_conv_w=v_conv_w, v_w_conv_out=v_w_conv_out, v_ssm_lam_re=v_ssm_lam_re, v_ssm_lam_im=v_ssm_lam_im, v_ssm_log_dt=v_ssm_log_dt, v_ssm_b_re=v_ssm_b_re, v_ssm_b_im=v_ssm_b_im, v_ssm_c_re=v_ssm_c_re, v_ssm_c_im=v_ssm_c_im, v_ssm_d=v_ssm_d, v_w_glu=v_w_glu, v_w_kv=v_w_kv, v_w_xattn_out=v_w_xattn_out, v_w_out=v_w_out, v_ln1_g=v_ln1_g, v_ln1_b=v_ln1_b, v_w_up=v_w_up, v_b_up=v_b_up, v_w_down=v_w_down, v_b_down=v_b_down, v_ln2_g=v_ln2_g, v_ln2_b=v_ln2_b)
    weights = {n: given[n] for n in TWIN_WEIGHTS}
    shared = {n: given[n] for n in SHARED_INPUTS}
    per_example = {n: given[n] for n in ['x', 'mem']}
    grad_fn = _jax.value_and_grad(_loss, argnums=(0, 1))

    def one_microbatch(ex, loss_target):
        ex = dict(ex)
        diff = ex.pop(TWIN_DIFF_INPUT)
        return grad_fn(weights, diff, {**shared, **ex}, loss_target)

    if N_MICROBATCH == 1:
        loss, (grad_w, grad_x) = one_microbatch(per_example, given["loss_target"])
    else:
        def body(carry, xs):
            loss_sum, grad_sum = carry
            l_k, (gw_k, gx_k) = one_microbatch(xs[0], xs[1])
            with _jax.named_scope("update"):
                return (loss_sum + l_k, _jax.tree.map(_jnp.add, grad_sum, gw_k)), gx_k

        init = (_jnp.zeros((), _jnp.float32), _jax.tree.map(_jnp.zeros_like, weights))
        (loss, grad_w), grad_x = _jax.lax.scan(body, init, (per_example, given["loss_target"]))
    with _jax.named_scope("update"):
        delta_w, new_m, new_v = {}, {}, {}
        for n in TWIN_WEIGHTS:
            delta_w[n], new_m[n], new_v[n] = _adamw(weights[n], grad_w[n], given["m_" + n], given["v_" + n])
    return (loss, grad_x, *[grad_w[n] for n in TWIN_WEIGHTS], *[delta_w[n] for n in TWIN_WEIGHTS],
            *[new_m[n] for n in TWIN_WEIGHTS], *[new_v[n] for n in TWIN_WEIGHTS])
```

```python
import functools
import math

import jax
import jax.numpy as jnp
from jax import lax
from jax.experimental import pallas as pl
from jax.experimental.pallas import tpu as pltpu

f32 = jnp.float32
bf16 = jnp.bfloat16

D_MODEL = 1024
CONV_WIDTH = 512
SSM_WIDTH = 512
SSM_GROUP = 16
SSM_GROUPS = 32
SSM_STATE = 64
N_STATES = SSM_GROUPS * SSM_STATE
XATTN_HEADS = 4
XATTN_HEAD_DIM = 128
XATTN_WIDTH = 512
D_FF = 4096
GATE_COLS = 3 * D_MODEL
IN_COLS = GATE_COLS + 3 * CONV_WIDTH + SSM_WIDTH + XATTN_WIDTH
ALPHA = 2.0 ** 0.25
LN_EPS = 1e-5
ADAM_LR = 0.001
ADAM_B1 = 0.9
ADAM_B2 = 0.999
ADAM_EPS = 1e-08
ADAM_WD = 0.01
ADAM_STEP = 10

N_SEG = 8
SCAN_ROWS = 256
LANE_STRIP = 512
TM = 512
VMEM_LIMIT = 48 * 1024 * 1024
MESH = pl.DeviceIdType.MESH

NT_DIMS = (((1,), (1,)), ((), ()))
TN_DIMS = (((0,), (0,)), ((), ()))


def _params():
    return pltpu.CompilerParams(vmem_limit_bytes=VMEM_LIMIT)


def _full(shape):
    n = len(shape)
    return pl.BlockSpec(shape, lambda *_: (0,) * n)


def _rows(tm, w, cb=0):
    return pl.BlockSpec((tm, w), lambda i: (i, cb))


def _sig(x):
    return 1.0 / (1.0 + jnp.exp(-x))


def mm_fwd(a, w3, name, *, tm=TM, tn=None, pre=None, bias=None):
    M, K = a.shape
    J, _, n = w3.shape
    tm = min(tm, M)
    tn = tn or n
    nl = n // tn

    def body(*refs):
        a_ref, w_ref = refs[0], refs[1]
        o_ref = refs[-1]
        av = a_ref[...]
        if pre is not None:
            av = pre(av)
        acc = jnp.dot(av.astype(bf16), w_ref[0], preferred_element_type=f32)
        if bias is not None:
            acc = acc + refs[2][...]
        o_ref[...] = acc

    in_specs = [pl.BlockSpec((tm, K), lambda i, j, l: (i, 0)),
                pl.BlockSpec((1, K, tn), lambda i, j, l: (j, 0, l))]
    args = [a, w3]
    if bias is not None:
        in_specs.append(pl.BlockSpec((1, tn), lambda i, j, l: (0, j * nl + l)))
        args.append(bias)
    return pl.pallas_call(
        body, name=name, grid=(M // tm, J, nl), in_specs=in_specs,
        out_specs=pl.BlockSpec((tm, tn), lambda i, j, l: (i, j * nl + l)),
        out_shape=jax.ShapeDtypeStruct((M, J * n), f32), compiler_params=_params())(*args)


def mm_bwd_x(dy, w3, name, *, tm=TM, tn=None, epi=None, extras=(), colsum=False):
    M = dy.shape[0]
    J, K, n = w3.shape
    tm = min(tm, M)
    tn = tn or n
    nl = n // tn
    nex = len(extras)

    def body(*refs):
        dy_ref, w_ref = refs[0], refs[1]
        ex = refs[2:2 + nex]
        o_ref = refs[2 + nex]
        i, j, l = pl.program_id(0), pl.program_id(1), pl.program_id(2)
        first = jnp.logical_and(j == 0, l == 0)
        last = jnp.logical_and(j == J - 1, l == nl - 1)
        part = lax.dot_general(dy_ref[...].astype(bf16), w_ref[0], NT_DIMS, preferred_element_type=f32)

        @pl.when(first)
        def _():
            o_ref[...] = part

        @pl.when(jnp.logical_not(first))
        def _():
            o_ref[...] += part

        if epi is not None or colsum:
            @pl.when(last)
            def _():
                v = o_ref[...]
                if epi is not None:
                    v = epi(v, *[e[...] for e in ex])
                    o_ref[...] = v
                if colsum:
                    s_ref = refs[3 + nex]
                    cs = jnp.sum(v, axis=0, keepdims=True)

                    @pl.when(i == 0)
                    def _():
                        s_ref[...] = cs

                    @pl.when(i > 0)
                    def _():
                        s_ref[...] += cs

    in_specs = [pl.BlockSpec((tm, tn), lambda i, j, l: (i, j * nl + l)),
                pl.BlockSpec((1, K, tn), lambda i, j, l: (j, 0, l))]
    in_specs += [pl.BlockSpec((tm, K), lambda i, j, l: (i, 0)) for _ in extras]
    out_specs = [pl.BlockSpec((tm, K), lambda i, j, l: (i, 0))]
    out_shape = [jax.ShapeDtypeStruct((M, K), f32)]
    if colsum:
        out_specs.append(pl.BlockSpec((1, K), lambda i, j, l: (0, 0)))
        out_shape.append(jax.ShapeDtypeStruct((1, K), f32))
    res = pl.pallas_call(body, name=name, grid=(M // tm, J, nl), in_specs=in_specs, out_specs=out_specs,
                         out_shape=out_shape, compiler_params=_params())(dy, w3, *extras)
    return res if colsum else res[0]


def mm_bwd_w(a, dy, J, name, *, tm=TM, tn=None, pre=None):
    M, K = a.shape
    n = dy.shape[1] // J
    tm = min(tm, M)
    tn = tn or n
    nl = n // tn
    ns = M // tm

    def body(a_ref, dy_ref, o_ref, acc_ref):
        s = pl.program_id(2)
        av = a_ref[...]
        if pre is not None:
            av = pre(av)
        part = lax.dot_general(av.astype(bf16), dy_ref[...].astype(bf16), TN_DIMS, preferred_element_type=f32)

        @pl.when(s == 0)
        def _():
            acc_ref[...] = part

        @pl.when(s > 0)
        def _():
            acc_ref[...] += part

        @pl.when(s == ns - 1)
        def _():
            o_ref[0] = acc_ref[...].astype(bf16)

    return pl.pallas_call(
        body, name=name, grid=(J, nl, ns),
        in_specs=[pl.BlockSpec((tm, K), lambda j, l, s: (s, 0)),
                  pl.BlockSpec((tm, tn), lambda j, l, s: (s, j * nl + l))],
        out_specs=pl.BlockSpec((1, K, tn), lambda j, l, s: (j, 0, l)),
        out_shape=jax.ShapeDtypeStruct((J, K, n), bf16),
        scratch_shapes=[pltpu.VMEM((K, tn), f32)], compiler_params=_params())(a, dy)


def _relu2(v):
    r = jnp.maximum(v, 0.0)
    return r * r


def _shift_down(z, k, halo):
    r = lax.broadcasted_iota(jnp.int32, z.shape, 0)
    y = pltpu.roll(z, k, 0)
    for q in range(k):
        y = jnp.where(r == q, halo[8 - k + q:8 - k + q + 1, :], y)
    return y


def _shift_up(z, k, halo):
    tm = z.shape[0]
    r = lax.broadcasted_iota(jnp.int32, z.shape, 0)
    y = pltpu.roll(z, tm - k, 0)
    for q in range(k):
        y = jnp.where(r == tm - k + q, halo[q:q + 1, :], y)
    return y


def _prev_halo(tm, cb):
    return pl.BlockSpec((8, CONV_WIDTH), lambda i: (jnp.maximum(i * (tm // 8) - 1, 0), cb))


def _next_halo(tm, cb, nrow8):
    return pl.BlockSpec((8, CONV_WIDTH), lambda i: (jnp.minimum((i + 1) * (tm // 8), nrow8 - 1), cb))


def conv_fwd(proj, conv_w, tm=TM):
    S = proj.shape[0]

    def body(cb_ref, cc_ref, ch_ref, cch_ref, chh_ref, w_ref, a_ref):
        i = pl.program_id(0)
        z = cc_ref[...] * ch_ref[...]
        zh = jnp.where(i == 0, 0.0, cch_ref[...] * chh_ref[...])
        w = w_ref[...]
        dwz = w[0:1, :] * _shift_down(z, 2, zh) + w[1:2, :] * _shift_down(z, 1, zh) + w[2:3, :] * z
        a_ref[...] = cb_ref[...] * dwz

    return pl.pallas_call(
        body, name="conv_fwd", grid=(S // tm,),
        in_specs=[_rows(tm, CONV_WIDTH, 6), _rows(tm, CONV_WIDTH, 7), _rows(tm, CONV_WIDTH, 8),
                  _prev_halo(tm, 7), _prev_halo(tm, 8), _full((3, CONV_WIDTH))],
        out_specs=_rows(tm, CONV_WIDTH),
        out_shape=jax.ShapeDtypeStruct((S, CONV_WIDTH), f32), compiler_params=_params())(
            proj, proj, proj, proj, proj, conv_w)


def conv_bwd(da, proj, conv_w, tm=TM):
    S = proj.shape[0]
    nt = S // tm

    def body(da_ref, cb_ref, cc_ref, ch_ref, cch_ref, chh_ref, dan_ref, cbn_ref, w_ref, o_ref, dw_ref):
        i = pl.program_id(0)
        cc, ch, cb, dav = cc_ref[...], ch_ref[...], cb_ref[...], da_ref[...]
        z = cc * ch
        zh = jnp.where(i == 0, 0.0, cch_ref[...] * chh_ref[...])
        w = w_ref[...]
        z1 = _shift_down(z, 1, zh)
        z2 = _shift_down(z, 2, zh)
        dwz = w[0:1, :] * z2 + w[1:2, :] * z1 + w[2:3, :] * z
        g = dav * cb
        gn = jnp.where(i == nt - 1, 0.0, dan_ref[...] * cbn_ref[...])
        dz = w[2:3, :] * g + w[1:2, :] * _shift_up(g, 1, gn) + w[0:1, :] * _shift_up(g, 2, gn)
        o_ref[:, 0:CONV_WIDTH] = dav * dwz
        o_ref[:, CONV_WIDTH:2 * CONV_WIDTH] = dz * ch
        o_ref[:, 2 * CONV_WIDTH:3 * CONV_WIDTH] = dz * cc

        @pl.when(i == 0)
        def _():
            dw_ref[...] = jnp.zeros_like(dw_ref)

        dw_ref[0:1, :] += jnp.sum(g * z2, axis=0, keepdims=True)
        dw_ref[1:2, :] += jnp.sum(g * z1, axis=0, keepdims=True)
        dw_ref[2:3, :] += jnp.sum(g * z, axis=0, keepdims=True)

    return pl.pallas_call(
        body, name="conv_bwd", grid=(nt,),
        in_specs=[_rows(tm, CONV_WIDTH), _rows(tm, CONV_WIDTH, 6), _rows(tm, CONV_WIDTH, 7), _rows(tm, CONV_WIDTH, 8),
                  _prev_halo(tm, 7), _prev_halo(tm, 8),
                  pl.BlockSpec((8, CONV_WIDTH), lambda i: (jnp.minimum((i + 1) * (tm // 8), S // 8 - 1), 0)),
                  _next_halo(tm, 6, S // 8), _full((3, CONV_WIDTH))],
        out_specs=[_rows(tm, 3 * CONV_WIDTH), _full((3, CONV_WIDTH))],
        out_shape=[jax.ShapeDtypeStruct((S, 3 * CONV_WIDTH), f32), jax.ShapeDtypeStruct((3, CONV_WIDTH), f32)],
        compiler_params=_params())(da, proj, proj, proj, proj, proj, da, proj, conv_w)


def _cmul(ar, ai, br, bi):
    return ar * br - ai * bi, ar * bi + ai * br


def ssm_prep(lr, li, ldt, bt_re, bt_im):
    def body(lr_ref, li_ref, ldt_ref, br_ref, bi_ref, ar_ref, ai_ref, kr_ref, ki_ref, bbr_ref, bbi_ref):
        lrv, liv = lr_ref[...], li_ref[...]
        dt = jnp.exp(ldt_ref[...])
        mag = jnp.exp(lrv * dt)
        ar = mag * jnp.cos(liv * dt)
        ai = mag * jnp.sin(liv * dt)
        den = lrv * lrv + liv * liv
        nr = ar - 1.0
        kr = (nr * lrv + ai * liv) / den
        ki = (ai * lrv - nr * liv) / den
        ar_ref[...], ai_ref[...], kr_ref[...], ki_ref[...] = ar, ai, kr, ki
        bbr_ref[...] = kr * br_ref[...] - ki * bi_ref[...]
        bbi_ref[...] = kr * bi_ref[...] + ki * br_ref[...]

    v = jax.ShapeDtypeStruct((1, N_STATES), f32)
    m = jax.ShapeDtypeStruct((SSM_GROUP, N_STATES), f32)
    return pl.pallas_call(body, name="ssm_prep", out_shape=[v, v, v, v, m, m])(lr, li, ldt, bt_re, bt_im)


def _pow_segment(ar, ai, seg_len):
    pr, pi = ar, ai
    for _ in range(int(math.log2(seg_len))):
        pr, pi = _cmul(pr, pi, pr, pi)
    return pr, pi


def ssm_scan_fwd(u_perm, bd_r, bd_i, ar, ai, *, final=None):
    S = u_perm.shape[0]
    R = SCAN_ROWS
    nblk = S // R
    seg_len = S // N_SEG
    nstrip = N_STATES // LANE_STRIP
    store = final is not None

    def body(*refs):
        if store:
            (u_ref, bdr_ref, bdi_ref, ar_ref, ai_ref, cdr_ref, cdi_ref, d_ref, ir_ref, ii_ref,
             y_ref, sr_ref, si_ref, bur, bui, car, cai) = refs
        else:
            u_ref, bdr_ref, bdi_ref, ar_ref, ai_ref, or_ref, oi_ref, bur, bui, car, cai = refs
        i = pl.program_id(0)
        u = u_ref[...]
        ub = u.astype(bf16)
        for cb in range(4):
            us = ub[:, cb * 128:(cb + 1) * 128]
            bur[:, cb * 512:(cb + 1) * 512] = jnp.dot(us, bdr_ref[cb], preferred_element_type=f32)
            bui[:, cb * 512:(cb + 1) * 512] = jnp.dot(us, bdi_ref[cb], preferred_element_type=f32)

        @pl.when(i == 0)
        def _():
            if store:
                car[...] = ir_ref[...]
                cai[...] = ii_ref[...]
            else:
                car[...] = jnp.zeros_like(car)
                cai[...] = jnp.zeros_like(cai)

        for ls in range(nstrip):
            lanes = pl.ds(ls * LANE_STRIP, LANE_STRIP)
            a_r = jnp.broadcast_to(ar_ref[:, lanes], (N_SEG, LANE_STRIP))
            a_i = jnp.broadcast_to(ai_ref[:, lanes], (N_SEG, LANE_STRIP))

            def step(t, carry, lanes=lanes, a_r=a_r, a_i=a_i):
                s_r, s_i = carry
                row = pl.multiple_of(t * 8, 8)
                n_r = a_r * s_r - a_i * s_i + bur[pl.ds(row, 8), lanes]
                n_i = a_r * s_i + a_i * s_r + bui[pl.ds(row, 8), lanes]
                if store:
                    sr_ref[pl.ds(row, 8), lanes] = n_r
                    si_ref[pl.ds(row, 8), lanes] = n_i
                return n_r, n_i

            e_r, e_i = lax.fori_loop(0, R // 8, step, (car[:, lanes], cai[:, lanes]), unroll=4)
            car[:, lanes] = e_r
            cai[:, lanes] = e_i

        if store:
            for cb in range(4):
                st_r = sr_ref[:, cb * 512:(cb + 1) * 512].astype(bf16)
                st_i = si_ref[:, cb * 512:(cb + 1) * 512].astype(bf16)
                y = (jnp.dot(st_r, cdr_ref[cb], preferred_element_type=f32)
                     - jnp.dot(st_i, cdi_ref[cb], preferred_element_type=f32))
                cols = slice(cb * 128, (cb + 1) * 128)
                y_ref[:, cols] = y + d_ref[:, cols] * u[:, cols]
        else:
            @pl.when(i == nblk - 1)
            def _():
                p_r, p_i = _pow_segment(ar_ref[...], ai_ref[...], seg_len)
                t_r, t_i = car[0:1, :], cai[0:1, :]
                or_ref[0:1, :] = jnp.zeros((1, N_STATES), f32)
                oi_ref[0:1, :] = jnp.zeros((1, N_STATES), f32)
                for j in range(1, N_SEG):
                    or_ref[j:j + 1, :] = t_r
                    oi_ref[j:j + 1, :] = t_i
                    m_r, m_i = _cmul(p_r, p_i, t_r, t_i)
                    t_r, t_i = car[j:j + 1, :] + m_r, cai[j:j + 1, :] + m_i

    blk = lambda w: pl.BlockSpec((R, w), lambda i: (i, 0))
    in_specs = [blk(SSM_WIDTH), _full((4, 128, 512)), _full((4, 128, 512)), _full((1, N_STATES)), _full((1, N_STATES))]
    args = [u_perm, bd_r, bd_i, ar, ai]
    scratch = [pltpu.VMEM((R, N_STATES), f32), pltpu.VMEM((R, N_STATES), f32),
               pltpu.VMEM((N_SEG, N_STATES), f32), pltpu.VMEM((N_SEG, N_STATES), f32)]
    if store:
        in_specs += [_full((4, 512, 128)), _full((4, 512, 128)), _full((1, SSM_WIDTH)),
                     _full((N_SEG, N_STATES)), _full((N_SEG, N_STATES))]
        args += list(final)
        out_specs = [blk(SSM_WIDTH), blk(N_STATES), blk(N_STATES)]
        out_shape = [jax.ShapeDtypeStruct((S, SSM_WIDTH), f32), jax.ShapeDtypeStruct((S, N_STATES), f32),
                     jax.ShapeDtypeStruct((S, N_STATES), f32)]
        name = "ssm_scan_fwd"
    else:
        out_specs = [_full((N_SEG, N_STATES)), _full((N_SEG, N_STATES))]
        out_shape = [jax.ShapeDtypeStruct((N_SEG, N_STATES), f32)] * 2
        name = "ssm_scan_fwd_carry"
    return pl.pallas_call(body, name=name, grid=(nblk,), in_specs=in_specs, out_specs=out_specs,
                          out_shape=out_shape, scratch_shapes=scratch, compiler_params=_params())(*args)


def ssm_scan_bwd(dy_perm, cdt_r, cdt_i, ar, ai, *, final=None):
    S = dy_perm.shape[0]
    R = SCAN_ROWS
    nblk = S // R
    seg_len = S // N_SEG
    nstrip = N_STATES // LANE_STRIP
    store = final is not None

    def body(*refs):
        if store:
            (dy_ref, cdr_ref, cdi_ref, ar_ref, ai_ref, u_ref, sr_ref, si_ref, bdr_ref, bdi_ref, d_ref, ir_ref, ii_ref,
             du_ref, dbr_ref, dbi_ref, dcr_ref, dci_ref, dar_ref, dai_ref, dd_ref, dsr, dsi, lmr, lmi, car, cai) = refs
        else:
            dy_ref, cdr_ref, cdi_ref, ar_ref, ai_ref, or_ref, oi_ref, dsr, dsi, car, cai = refs
        i = pl.program_id(0)
        dy = dy_ref[...]
        dyb = dy.astype(bf16)
        for cb in range(4):
            ds_ = dyb[:, cb * 128:(cb + 1) * 128]
            dsr[:, cb * 512:(cb + 1) * 512] = jnp.dot(ds_, cdr_ref[cb], preferred_element_type=f32)
            dsi[:, cb * 512:(cb + 1) * 512] = -jnp.dot(ds_, cdi_ref[cb], preferred_element_type=f32)

        @pl.when(i == 0)
        def _():
            if store:
                car[...] = ir_ref[...]
                cai[...] = ii_ref[...]
                dar_ref[...] = jnp.zeros_like(dar_ref)
                dai_ref[...] = jnp.zeros_like(dai_ref)
                dbr_ref[...] = jnp.zeros_like(dbr_ref)
                dbi_ref[...] = jnp.zeros_like(dbi_ref)
                dcr_ref[...] = jnp.zeros_like(dcr_ref)
                dci_ref[...] = jnp.zeros_like(dci_ref)
                dd_ref[...] = jnp.zeros_like(dd_ref)
            else:
                car[...] = jnp.zeros_like(car)
                cai[...] = jnp.zeros_like(cai)

        for ls in range(nstrip):
            lanes = pl.ds(ls * LANE_STRIP, LANE_STRIP)
            a_r = jnp.broadcast_to(ar_ref[:, lanes], (N_SEG, LANE_STRIP))
            a_i = jnp.broadcast_to(ai_ref[:, lanes], (N_SEG, LANE_STRIP))
            zero = jnp.zeros((N_SEG, LANE_STRIP), f32)

            def step(k, carry, lanes=lanes, a_r=a_r, a_i=a_i):
                l_r, l_i, g_r, g_i = carry
                row = pl.multiple_of((R // 8 - 1 - k) * 8, 8)
                if store:
                    s_r = sr_ref[pl.ds(row, 8), lanes]
                    s_i = si_ref[pl.ds(row, 8), lanes]
                    g_r = g_r + l_r * s_r + l_i * s_i
                    g_i = g_i + l_i * s_r - l_r * s_i
                n_r = dsr[pl.ds(row, 8), lanes] + a_r * l_r + a_i * l_i
                n_i = dsi[pl.ds(row, 8), lanes] + a_r * l_i - a_i * l_r
                if store:
                    lmr[pl.ds(row, 8), lanes] = n_r
                    lmi[pl.ds(row, 8), lanes] = n_i
                return n_r, n_i, g_r, g_i

            e_r, e_i, g_r, g_i = lax.fori_loop(0, R // 8, step, (car[:, lanes], cai[:, lanes], zero, zero), unroll=4)
            car[:, lanes] = e_r
            cai[:, lanes] = e_i
            if store:
                dar_ref[:, lanes] += g_r
                dai_ref[:, lanes] += g_i

        if store:
            u = u_ref[...]
            ub = u.astype(bf16)
            for cb in range(4):
                cols = slice(cb * 128, (cb + 1) * 128)
                st = slice(cb * 512, (cb + 1) * 512)
                l_r = lmr[:, st].astype(bf16)
                l_i = lmi[:, st].astype(bf16)
                du = (jnp.dot(l_r, bdr_ref[cb], preferred_element_type=f32)
                      + jnp.dot(l_i, bdi_ref[cb], preferred_element_type=f32))
                du_ref[:, cols] = du + d_ref[:, cols] * dy[:, cols]
                dbr_ref[cb] += lax.dot_general(l_r, ub[:, cols], TN_DIMS, preferred_element_type=f32)
                dbi_ref[cb] += lax.dot_general(l_i, ub[:, cols], TN_DIMS, preferred_element_type=f32)
                dcr_ref[cb] += lax.dot_general(dyb[:, cols], sr_ref[:, st].astype(bf16), TN_DIMS,
                                               preferred_element_type=f32)
                dci_ref[cb] += lax.dot_general(dyb[:, cols], si_ref[:, st].astype(bf16), TN_DIMS,
                                               preferred_element_type=f32)
            dd_ref[...] += jnp.sum(dy * u, axis=0, keepdims=True)
        else:
            @pl.when(i == nblk - 1)
            def _():
                p_r, p_i = _pow_segment(ar_ref[...], ai_ref[...], seg_len)
                p_i = -p_i
                t_r, t_i = car[N_SEG - 1:N_SEG, :], cai[N_SEG - 1:N_SEG, :]
                or_ref[N_SEG - 1:N_SEG, :] = jnp.zeros((1, N_STATES), f32)
                oi_ref[N_SEG - 1:N_SEG, :] = jnp.zeros((1, N_STATES), f32)
                for j in range(N_SEG - 2, -1, -1):
                    or_ref[j:j + 1, :] = t_r
                    oi_ref[j:j + 1, :] = t_i
                    m_r, m_i = _cmul(p_r, p_i, t_r, t_i)
                    t_r, t_i = car[j:j + 1, :] + m_r, cai[j:j + 1, :] + m_i

    blk = lambda w: pl.BlockSpec((R, w), lambda i: (nblk - 1 - i, 0))
    in_specs = [blk(SSM_WIDTH), _full((4, 128, 512)), _full((4, 128, 512)), _full((1, N_STATES)), _full((1, N_STATES))]
    args = [dy_perm, cdt_r, cdt_i, ar, ai]
    seg = jax.ShapeDtypeStruct((N_SEG, N_STATES), f32)
    if store:
        in_specs += [blk(SSM_WIDTH), blk(N_STATES), blk(N_STATES), _full((4, 512, 128)), _full((4, 512, 128)),
                     _full((1, SSM_WIDTH)), _full((N_SEG, N_STATES)), _full((N_SEG, N_STATES))]
        args += list(final)
        out_specs = [blk(SSM_WIDTH), _full((4, 512, 128)), _full((4, 512, 128)), _full((4, 128, 512)),
                     _full((4, 128, 512)), _full((N_SEG, N_STATES)), _full((N_SEG, N_STATES)), _full((1, SSM_WIDTH))]
        b_acc = jax.ShapeDtypeStruct((4, 512, 128), f32)
        c_acc = jax.ShapeDtypeStruct((4, 128, 512), f32)
        out_shape = [jax.ShapeDtypeStruct((S, SSM_WIDTH), f32), b_acc, b_acc, c_acc, c_acc, seg, seg,
                     jax.ShapeDtypeStruct((1, SSM_WIDTH), f32)]
        scratch = [pltpu.VMEM((R, N_STATES), f32)] * 4 + [pltpu.VMEM((N_SEG, N_STATES), f32)] * 2
        name = "ssm_scan_bwd"
    else:
        out_specs = [_full((N_SEG, N_STATES)), _full((N_SEG, N_STATES))]
        out_shape = [seg, seg]
        scratch = [pltpu.VMEM((R, N_STATES), f32)] * 2 + [pltpu.VMEM((N_SEG, N_STATES), f32)] * 2
        name = "ssm_scan_bwd_carry"
    return pl.pallas_call(body, name=name, grid=(nblk,), in_specs=in_specs, out_specs=out_specs,
                          out_shape=out_shape, scratch_shapes=scratch, compiler_params=_params())(*args)


def ssm_param_bwd(dbb_r, dbb_i, bt_re, bt_im, kr, ki, ar, ai, lr, li, ldt, da_r, da_i):
    def body(dbr_ref, dbi_ref, br_ref, bi_ref, kr_ref, ki_ref, ar_ref, ai_ref, lr_ref, li_ref, ldt_ref, dar_ref,
             dai_ref, obr_ref, obi_ref, olr_ref, oli_ref, odt_ref):
        dbr, dbi, b_r, b_i = dbr_ref[...], dbi_ref[...], br_ref[...], bi_ref[...]
        k_r, k_i, a_r, a_i = kr_ref[...], ki_ref[...], ar_ref[...], ai_ref[...]
        l_r, l_i = lr_ref[...], li_ref[...]
        dt = jnp.exp(ldt_ref[...])
        obr_ref[...] = k_r * dbr + k_i * dbi
        obi_ref[...] = k_r * dbi - k_i * dbr
        gk_r = jnp.sum(dbr * b_r + dbi * b_i, axis=0, keepdims=True)
        gk_i = jnp.sum(dbi * b_r - dbr * b_i, axis=0, keepdims=True)
        ga_r = jnp.sum(dar_ref[...], axis=0, keepdims=True)
        ga_i = jnp.sum(dai_ref[...], axis=0, keepdims=True)
        den = l_r * l_r + l_i * l_i
        c_r, c_i = l_r / den, l_i / den
        m_r, m_i = _cmul(c_r, c_i, gk_r, gk_i)
        g_r, g_i = ga_r + m_r, ga_i + m_i
        t1_r, t1_i = _cmul(dt * a_r, -dt * a_i, g_r, g_i)
        q_r, q_i = _cmul(k_r, k_i, c_r, -c_i)
        t2_r, t2_i = _cmul(-q_r, q_i, gk_r, gk_i)
        olr_ref[...] = t1_r + t2_r
        oli_ref[...] = t1_i + t2_i
        w_r, w_i = _cmul(l_r, l_i, a_r, a_i)
        odt_ref[...] = dt * (w_r * g_r + w_i * g_i)

    v = jax.ShapeDtypeStruct((1, N_STATES), f32)
    m = jax.ShapeDtypeStruct((SSM_GROUP, N_STATES), f32)
    return pl.pallas_call(body, name="ssm_param_bwd", out_shape=[m, m, v, v, v])(
        dbb_r, dbb_i, bt_re, bt_im, kr, ki, ar, ai, lr, li, ldt, da_r, da_i)


def group_sum(v):
    def body(v_ref, o_ref):
        o_ref[...] = jnp.sum(v_ref[...], axis=-1, keepdims=True)
    return pl.pallas_call(body, name="group_sum", out_shape=jax.ShapeDtypeStruct((v.shape[0], 1), f32))(v)


GELU_K = math.sqrt(2.0 / math.pi)
GELU_C = 0.044715


def gelu_fwd(y, tm=TM):
    def body(y_ref, o_ref):
        v = y_ref[...]
        o_ref[...] = 0.5 * v * (1.0 + jnp.tanh(GELU_K * (v + GELU_C * v * v * v)))
    S, W = y.shape
    return pl.pallas_call(body, name="gelu_fwd", grid=(S // tm,), in_specs=[_rows(tm, W)], out_specs=_rows(tm, W),
                          out_shape=jax.ShapeDtypeStruct((S, W), f32), compiler_params=_params())(y)


def gelu_bwd(dy, y, tm=TM):
    def body(dy_ref, y_ref, o_ref):
        v = y_ref[...]
        t = jnp.tanh(GELU_K * (v + GELU_C * v * v * v))
        dg = 0.5 * (1.0 + t) + 0.5 * v * (1.0 - t * t) * GELU_K * (1.0 + 3.0 * GELU_C * v * v)
        o_ref[...] = dy_ref[...] * dg
    S, W = y.shape
    return pl.pallas_call(body, name="gelu_bwd", grid=(S // tm,), in_specs=[_rows(tm, W), _rows(tm, W)],
                          out_specs=_rows(tm, W), out_shape=jax.ShapeDtypeStruct((S, W), f32),
                          compiler_params=_params())(dy, y)


ATTN_SCALE = XATTN_HEAD_DIM ** -0.5


def _attn_probs(q_h, k_h):
    s = lax.dot_general(q_h, k_h, NT_DIMS, preferred_element_type=f32) * ATTN_SCALE
    e = jnp.exp(s - jnp.max(s, axis=-1, keepdims=True))
    return e / jnp.sum(e, axis=-1, keepdims=True)


def attn_fwd(proj, kv, tm=TM):
    S = proj.shape[0]
    M = kv.shape[0]

    def body(q_ref, kv_ref, o_ref):
        for h in range(XATTN_HEADS):
            cols = slice(h * XATTN_HEAD_DIM, (h + 1) * XATTN_HEAD_DIM)
            q_h = q_ref[:, cols].astype(bf16)
            k_h = kv_ref[:, cols].astype(bf16)
            v_h = kv_ref[:, XATTN_WIDTH + h * XATTN_HEAD_DIM:XATTN_WIDTH + (h + 1) * XATTN_HEAD_DIM].astype(bf16)
            p = _attn_probs(q_h, k_h)
            o_ref[:, cols] = jnp.dot(p.astype(bf16), v_h, preferred_element_type=f32)

    return pl.pallas_call(
        body, name="attn_fwd", grid=(S // tm,), in_specs=[_rows(tm, XATTN_WIDTH, 10), _full((M, 2 * XATTN_WIDTH))],
        out_specs=_rows(tm, XATTN_WIDTH), out_shape=jax.ShapeDtypeStruct((S, XATTN_WIDTH), f32),
        compiler_params=_params())(proj, kv)


def attn_bwd(do, proj, kv, tm=TM):
    S = proj.shape[0]
    M = kv.shape[0]

    def body(do_ref, q_ref, kv_ref, dq_ref, dkv_ref):
        i = pl.program_id(0)

        @pl.when(i == 0)
        def _():
            dkv_ref[...] = jnp.zeros_like(dkv_ref)

        for h in range(XATTN_HEADS):
            cols = slice(h * XATTN_HEAD_DIM, (h + 1) * XATTN_HEAD_DIM)
            vcols = slice(XATTN_WIDTH + h * XATTN_HEAD_DIM, XATTN_WIDTH + (h + 1) * XATTN_HEAD_DIM)
            q_h = q_ref[:, cols].astype(bf16)
            k_h = kv_ref[:, cols].astype(bf16)
            v_h = kv_ref[:, vcols].astype(bf16)
            do_h = do_ref[:, cols].astype(bf16)
            p = _attn_probs(q_h, k_h)
            dp = lax.dot_general(do_h, v_h, NT_DIMS, preferred_element_type=f32)
            ds = (p * (dp - jnp.sum(dp * p, axis=-1, keepdims=True)) * ATTN_SCALE).astype(bf16)
            dq_ref[:, cols] = jnp.dot(ds, k_h, preferred_element_type=f32)
            dkv_ref[:, cols] += lax.dot_general(ds, q_h, TN_DIMS, preferred_element_type=f32)
            dkv_ref[:, vcols] += lax.dot_general(p.astype(bf16), do_h, TN_DIMS, preferred_element_type=f32)

    return pl.pallas_call(
        body, name="attn_bwd", grid=(S // tm,),
        in_specs=[_rows(tm, XATTN_WIDTH), _rows(tm, XATTN_WIDTH, 10), _full((M, 2 * XATTN_WIDTH))],
        out_specs=[_rows(tm, XATTN_WIDTH), _full((M, 2 * XATTN_WIDTH))],
        out_shape=[jax.ShapeDtypeStruct((S, XATTN_WIDTH), f32), jax.ShapeDtypeStruct((M, 2 * XATTN_WIDTH), f32)],
        compiler_params=_params())(do, proj, kv)


def merge_fwd(proj, b_gate, y_a, glu, y_c, tm=TM):
    S = proj.shape[0]

    def body(g0_ref, g1_ref, g2_ref, b_ref, ya_ref, ga_ref, gb_ref, yc_ref, o_ref):
        b = b_ref[...]
        g0 = _sig(g0_ref[...] + b[:, 0:D_MODEL])
        g1 = _sig(g1_ref[...] + b[:, D_MODEL:2 * D_MODEL])
        g2 = _sig(g2_ref[...] + b[:, 2 * D_MODEL:3 * D_MODEL])
        y_b = ga_ref[...] * _sig(gb_ref[...])
        o_ref[...] = g0 * ya_ref[...] + g1 * y_b + g2 * yc_ref[...]

    return pl.pallas_call(
        body, name="merge_fwd", grid=(S // tm,),
        in_specs=[_rows(tm, D_MODEL, 0), _rows(tm, D_MODEL, 1), _rows(tm, D_MODEL, 2), _full((1, GATE_COLS)),
                  _rows(tm, D_MODEL), _rows(tm, D_MODEL, 0), _rows(tm, D_MODEL, 1), _rows(tm, D_MODEL)],
        out_specs=_rows(tm, D_MODEL), out_shape=jax.ShapeDtypeStruct((S, D_MODEL), f32),
        compiler_params=_params())(proj, proj, proj, b_gate, y_a, glu, glu, y_c)


def merge_bwd(dm, proj, b_gate, y_a, glu, y_c, tm=256):
    S = proj.shape[0]

    def body(dm_ref, g0_ref, g1_ref, g2_ref, b_ref, ya_ref, ga_ref, gb_ref, yc_ref,
             dg_ref, dya_ref, dgl_ref, dyc_ref, dbg_ref):
        i = pl.program_id(0)
        b = b_ref[...]
        dmv = dm_ref[...]
        ga, sb = ga_ref[...], _sig(gb_ref[...])
        ys = (ya_ref[...], ga * sb, yc_ref[...])
        gs = tuple(_sig(r[...] + b[:, k * D_MODEL:(k + 1) * D_MODEL]) for k, r in enumerate((g0_ref, g1_ref, g2_ref)))

        @pl.when(i == 0)
        def _():
            dbg_ref[...] = jnp.zeros_like(dbg_ref)

        for k in range(3):
            dpre = dmv * ys[k] * gs[k] * (1.0 - gs[k])
            dg_ref[:, k * D_MODEL:(k + 1) * D_MODEL] = dpre
            dbg_ref[:, k * D_MODEL:(k + 1) * D_MODEL] += jnp.sum(dpre, axis=0, keepdims=True)
        dya_ref[...] = dmv * gs[0]
        dyc_ref[...] = dmv * gs[2]
        dyb = dmv * gs[1]
        dgl_ref[:, 0:D_MODEL] = dyb * sb
        dgl_ref[:, D_MODEL:2 * D_MODEL] = dyb * ga * sb * (1.0 - sb)

    return pl.pallas_call(
        body, name="merge_bwd", grid=(S // tm,),
        in_specs=[_rows(tm, D_MODEL), _rows(tm, D_MODEL, 0), _rows(tm, D_MODEL, 1), _rows(tm, D_MODEL, 2),
                  _full((1, GATE_COLS)), _rows(tm, D_MODEL), _rows(tm, D_MODEL, 0), _rows(tm, D_MODEL, 1),
                  _rows(tm, D_MODEL)],
        out_specs=[_rows(tm, GATE_COLS), _rows(tm, D_MODEL), _rows(tm, 2 * D_MODEL), _rows(tm, D_MODEL),
                   _full((1, GATE_COLS))],
        out_shape=[jax.ShapeDtypeStruct((S, GATE_COLS), f32), jax.ShapeDtypeStruct((S, D_MODEL), f32),
                   jax.ShapeDtypeStruct((S, 2 * D_MODEL), f32), jax.ShapeDtypeStruct((S, D_MODEL), f32),
                   jax.ShapeDtypeStruct((1, GATE_COLS), f32)],
        compiler_params=_params())(dm, proj, proj, proj, b_gate, y_a, glu, glu, y_c)


def _ln_stats(r):
    mu = jnp.mean(r, axis=-1, keepdims=True)
    xc = r - mu
    var = jnp.mean(xc * xc, axis=-1, keepdims=True)
    rstd = lax.rsqrt(var + LN_EPS)
    return xc * rstd, rstd


def _ln_bwd(dy, xhat, rstd, g):
    dxh = dy * g
    return rstd * (dxh - jnp.mean(dxh, axis=-1, keepdims=True) - xhat * jnp.mean(dxh * xhat, axis=-1, keepdims=True))


def ln1_fwd(x, t1, g, b, tm=TM):
    S = x.shape[0]

    def body(x_ref, t_ref, g_ref, b_ref, o_ref):
        xhat, _ = _ln_stats(ALPHA * x_ref[...] + t_ref[...])
        o_ref[...] = xhat * g_ref[...] + b_ref[...]

    return pl.pallas_call(
        body, name="ln1_fwd", grid=(S // tm,),
        in_specs=[_rows(tm, D_MODEL), _rows(tm, D_MODEL), _full((1, D_MODEL)), _full((1, D_MODEL))],
        out_specs=_rows(tm, D_MODEL), out_shape=jax.ShapeDtypeStruct((S, D_MODEL), f32),
        compiler_params=_params())(x, t1, g, b)


def ln2_loss_bwd(h1, t2, b_down, g, b, target, tm=TM):
    S = h1.shape[0]

    def body(h_ref, t_ref, bd_ref, g_ref, b_ref, y_ref, dr_ref, l_ref, dg_ref, db_ref, dbd_ref):
        i = pl.program_id(0)
        xhat, rstd = _ln_stats(ALPHA * h_ref[...] + t_ref[...] + bd_ref[...])
        gv = g_ref[...]
        err = xhat * gv + b_ref[...] - y_ref[...]
        dout = err * (1.0 / D_MODEL)
        dr = _ln_bwd(dout, xhat, rstd, gv)
        dr_ref[...] = dr
        part = 0.5 * jnp.sum(jnp.sum(err * err, axis=-1, keepdims=True) * (1.0 / D_MODEL), axis=0, keepdims=True)

        @pl.when(i == 0)
        def _():
            l_ref[...] = jnp.zeros_like(l_ref)
            dg_ref[...] = jnp.zeros_like(dg_ref)
            db_ref[...] = jnp.zeros_like(db_ref)
            dbd_ref[...] = jnp.zeros_like(dbd_ref)

        l_ref[...] += jnp.broadcast_to(part, l_ref.shape)
        dg_ref[...] += jnp.sum(dout * xhat, axis=0, keepdims=True)
        db_ref[...] += jnp.sum(dout, axis=0, keepdims=True)
        dbd_ref[...] += jnp.sum(dr, axis=0, keepdims=True)

    vec = jax.ShapeDtypeStruct((1, D_MODEL), f32)
    return pl.pallas_call(
        body, name="ln2_loss_bwd", grid=(S // tm,),
        in_specs=[_rows(tm, D_MODEL), _rows(tm, D_MODEL), _full((1, D_MODEL)), _full((1, D_MODEL)),
                  _full((1, D_MODEL)), _rows(tm, D_MODEL)],
        out_specs=[_rows(tm, D_MODEL), _full((1, 128)), _full((1, D_MODEL)), _full((1, D_MODEL)),
                   _full((1, D_MODEL))],
        out_shape=[jax.ShapeDtypeStruct((S, D_MODEL), f32), jax.ShapeDtypeStruct((1, 128), f32), vec, vec, vec],
        compiler_params=_params())(h1, t2, b_down, g, b, target)


def ln1_bwd(x, t1, dr2, t3, g, tm=TM):
    S = x.shape[0]

    def body(x_ref, t_ref, dr2_ref, t3_ref, g_ref, dr_ref, dg_ref, db_ref):
        i = pl.program_id(0)
        xhat, rstd = _ln_stats(ALPHA * x_ref[...] + t_ref[...])
        dh = ALPHA * dr2_ref[...] + t3_ref[...]
        dr_ref[...] = _ln_bwd(dh, xhat, rstd, g_ref[...])

        @pl.when(i == 0)
        def _():
            dg_ref[...] = jnp.zeros_like(dg_ref)
            db_ref[...] = jnp.zeros_like(db_ref)

        dg_ref[...] += jnp.sum(dh * xhat, axis=0, keepdims=True)
        db_ref[...] += jnp.sum(dh, axis=0, keepdims=True)

    vec = jax.ShapeDtypeStruct((1, D_MODEL), f32)
    return pl.pallas_call(
        body, name="ln1_bwd", grid=(S // tm,),
        in_specs=[_rows(tm, D_MODEL)] * 4 + [_full((1, D_MODEL))],
        out_specs=[_rows(tm, D_MODEL), _full((1, D_MODEL)), _full((1, D_MODEL))],
        out_shape=[jax.ShapeDtypeStruct((S, D_MODEL), f32), vec, vec],
        compiler_params=_params())(x, t1, dr2, t3, g)


HBM_SPEC = pl.BlockSpec(memory_space=pl.ANY)


def _xy_peers(x, y):
    return [(1 - x, y), (x, 1 - y), (1 - x, 1 - y)]


def gather_weights(shard, name):
    def body(in_ref, out_ref, send_sems, recv_sems, local_sem):
        x, y, c = lax.axis_index("x"), lax.axis_index("y"), lax.axis_index("c")
        me = 2 * x + y
        mine = pltpu.make_async_copy(in_ref, out_ref.at[me], local_sem)
        mine.start()
        sends = []
        for k, (px, py) in enumerate(_xy_peers(x, y)):
            cp = pltpu.make_async_remote_copy(src_ref=in_ref, dst_ref=out_ref.at[me], send_sem=send_sems.at[k],
                                              recv_sem=recv_sems.at[k], device_id=(px, py, c), device_id_type=MESH)
            cp.start()
            sends.append(cp)
        for k, (px, py) in enumerate(_xy_peers(x, y)):
            pltpu.make_async_remote_copy(src_ref=in_ref, dst_ref=out_ref.at[2 * px + py], send_sem=send_sems.at[k],
                                         recv_sem=recv_sems.at[k], device_id=(px, py, c),
                                         device_id_type=MESH).wait_recv()
        for cp in sends:
            cp.wait_send()
        mine.wait()

    return pl.pallas_call(
        body, name=name, in_specs=[HBM_SPEC], out_specs=HBM_SPEC,
        out_shape=jax.ShapeDtypeStruct((4,) + shard.shape, shard.dtype),
        scratch_shapes=[pltpu.SemaphoreType.DMA((3,)), pltpu.SemaphoreType.DMA((3,)), pltpu.SemaphoreType.DMA(())],
    )(shard)


def exchange_grads(g_all, small):
    def body(g_ref, s_ref, recv_ref, srecv_ref, send_sems, recv_sems, ssend_sems, srecv_sems, local_sems):
        x, y, c = lax.axis_index("x"), lax.axis_index("y"), lax.axis_index("c")
        me = 2 * x + y
        me8 = 4 * x + 2 * y + c
        own = pltpu.make_async_copy(g_ref.at[me], recv_ref.at[me], local_sems.at[0])
        own_s = pltpu.make_async_copy(s_ref, srecv_ref.at[me8], local_sems.at[1])
        own.start()
        own_s.start()
        sends = []
        for k, (px, py) in enumerate(_xy_peers(x, y)):
            cp = pltpu.make_async_remote_copy(src_ref=g_ref.at[2 * px + py], dst_ref=recv_ref.at[me],
                                              send_sem=send_sems.at[k], recv_sem=recv_sems.at[k],
                                              device_id=(px, py, c), device_id_type=MESH)
            cp.start()
            sends.append(cp)
        flips = [(fx, fy, fc) for fx in (0, 1) for fy in (0, 1) for fc in (0, 1)][1:]
        for k, (fx, fy, fc) in enumerate(flips):
            cp = pltpu.make_async_remote_copy(src_ref=s_ref, dst_ref=srecv_ref.at[me8], send_sem=ssend_sems.at[k],
                                              recv_sem=srecv_sems.at[k],
                                              device_id=(x ^ fx, y ^ fy, c ^ fc), device_id_type=MESH)
            cp.start()
            sends.append(cp)
        for k, (px, py) in enumerate(_xy_peers(x, y)):
            pltpu.make_async_remote_copy(src_ref=g_ref.at[me], dst_ref=recv_ref.at[2 * px + py],
                                         send_sem=send_sems.at[k], recv_sem=recv_sems.at[k],
                                         device_id=(px, py, c), device_id_type=MESH).wait_recv()
        for k, (fx, fy, fc) in enumerate(flips):
            src8 = 4 * (x ^ fx) + 2 * (y ^ fy) + (c ^ fc)
            pltpu.make_async_remote_copy(src_ref=s_ref, dst_ref=srecv_ref.at[src8], send_sem=ssend_sems.at[k],
                                         recv_sem=srecv_sems.at[k], device_id=(x ^ fx, y ^ fy, c ^ fc),
                                         device_id_type=MESH).wait_recv()
        for cp in sends:
            cp.wait_send()
        own.wait()
        own_s.wait()

    return pl.pallas_call(
        body, name="exchange_grads", in_specs=[HBM_SPEC, HBM_SPEC], out_specs=[HBM_SPEC, HBM_SPEC],
        out_shape=[jax.ShapeDtypeStruct(g_all.shape, g_all.dtype),
                   jax.ShapeDtypeStruct((8,) + small.shape, small.dtype)],
        scratch_shapes=[pltpu.SemaphoreType.DMA((3,)), pltpu.SemaphoreType.DMA((3,)), pltpu.SemaphoreType.DMA((7,)),
                        pltpu.SemaphoreType.DMA((7,)), pltpu.SemaphoreType.DMA((2,))],
    )(g_all, small)


def swap_sibling(p):
    def body(p_ref, q_ref, send_sem, recv_sem):
        x, y, c = lax.axis_index("x"), lax.axis_index("y"), lax.axis_index("c")
        cp = pltpu.make_async_remote_copy(src_ref=p_ref, dst_ref=q_ref, send_sem=send_sem, recv_sem=recv_sem,
                                          device_id=(x, y, 1 - c), device_id_type=MESH)
        cp.start()
        cp.wait()

    return pl.pallas_call(
        body, name="swap_sibling", in_specs=[HBM_SPEC], out_specs=HBM_SPEC,
        out_shape=jax.ShapeDtypeStruct(p.shape, p.dtype),
        scratch_shapes=[pltpu.SemaphoreType.DMA(()), pltpu.SemaphoreType.DMA(())],
    )(p)


def sum_slots(recv, tm=560):
    n, R, C = recv.shape

    def body(r_ref, o_ref):
        acc = r_ref[0].astype(f32)
        for k in range(1, n):
            acc = acc + r_ref[k].astype(f32)
        o_ref[...] = acc

    return pl.pallas_call(
        body, name="sum_slots", grid=(R // tm,), in_specs=[pl.BlockSpec((n, tm, C), lambda i: (0, i, 0))],
        out_specs=_rows(tm, C), out_shape=jax.ShapeDtypeStruct((R, C), f32), compiler_params=_params())(recv)


def _adamw(w, g, m, v):
    m = ADAM_B1 * m + (1.0 - ADAM_B1) * g
    v = ADAM_B2 * v + (1.0 - ADAM_B2) * (g * g)
    m_hat = m / (1.0 - ADAM_B1 ** ADAM_STEP)
    v_hat = v / (1.0 - ADAM_B2 ** ADAM_STEP)
    delta = -ADAM_LR * (m_hat / (jnp.sqrt(v_hat) + ADAM_EPS) + ADAM_WD * w)
    return delta, m, v


def adam_pair(p, q, w, m, v, tm=280):
    R, C = w.shape

    def body(p_ref, q_ref, w_ref, m_ref, v_ref, g_ref, d_ref, nm_ref, nv_ref):
        g = p_ref[...] + q_ref[...]
        g_ref[...] = g
        d_ref[...], nm_ref[...], nv_ref[...] = _adamw(w_ref[...], g, m_ref[...], v_ref[...])

    o = jax.ShapeDtypeStruct((R, C), f32)
    return pl.pallas_call(body, name="adam_pair", grid=(R // tm,), in_specs=[_rows(tm, C)] * 5,
                          out_specs=[_rows(tm, C)] * 4, out_shape=[o] * 4, compiler_params=_params())(p, q, w, m, v)


def adam_slots(recv, w, m, v, name):
    n = recv.shape[0]

    def body(r_ref, w_ref, m_ref, v_ref, g_ref, d_ref, nm_ref, nv_ref):
        g = r_ref[0]
        for k in range(1, n):
            g = g + r_ref[k]
        g_ref[...] = g
        d_ref[...], nm_ref[...], nv_ref[...] = _adamw(w_ref[...], g, m_ref[...], v_ref[...])

    o = jax.ShapeDtypeStruct(w.shape, f32)
    return pl.pallas_call(body, name=name, out_shape=[o] * 4, compiler_params=_params())(recv, w, m, v)


BIG = ("w_in", "w_conv_out", "w_glu", "w_kv", "w_xattn_out", "w_out", "w_up", "w_down")
SMALL = ("b_gate", "ssm_lam_re", "ssm_lam_im", "ssm_log_dt", "ssm_b_re", "ssm_b_im", "ssm_c_re", "ssm_c_im", "ssm_d",
         "ln1_g", "ln1_b", "b_up", "b_down", "ln2_g", "ln2_b")
PACK_COLS = 1024


def _pack_rows(arrs):
    return jnp.concatenate([a.reshape(-1, PACK_COLS) for a in arrs], axis=0)


def _pad_flat(a, mult=1024):
    a = a.reshape(-1)
    return jnp.pad(a, (0, (-a.shape[0]) % mult))


def _perm(a):
    S, W = a.shape
    return a.reshape(N_SEG, S // N_SEG, W).transpose(1, 0, 2).reshape(S, W)


def _unperm(a):
    S, W = a.shape
    return a.reshape(S // N_SEG, N_SEG, W).transpose(1, 0, 2).reshape(S, W)


def _state_rows(a):
    return a.transpose(2, 0, 1).reshape(SSM_GROUP, N_STATES)


def _block_diag_b(bt):
    b4 = bt.reshape(SSM_GROUP, 4, 8, SSM_STATE)
    eye = jnp.eye(8, dtype=bt.dtype)
    return jnp.einsum("hcgp,gk->cghkp", b4, eye).reshape(4, 128, 512)


def _block_diag_c(c):
    c4 = c.reshape(4, 8, SSM_GROUP, SSM_STATE)
    eye = jnp.eye(8, dtype=c.dtype)
    return jnp.einsum("cghp,gk->cgpkh", c4, eye).reshape(4, 512, 128)


def _diag_b(acc):
    a = acc.reshape(4, 8, SSM_STATE, 8, SSM_GROUP)
    eye = jnp.eye(8, dtype=acc.dtype)
    return jnp.einsum("cgpkh,gk->hcgp", a, eye).reshape(SSM_GROUP, N_STATES)


def _diag_c(acc):
    a = acc.reshape(4, 8, SSM_GROUP, 8, SSM_STATE)
    eye = jnp.eye(8, dtype=acc.dtype)
    return jnp.einsum("cghkp,gk->cghp", a, eye).reshape(SSM_GROUPS, SSM_GROUP, SSM_STATE)


def kernel(x, mem, w_in, b_gate, conv_w, w_conv_out, ssm_lam_re, ssm_lam_im, ssm_log_dt, ssm_b_re, ssm_b_im, ssm_c_re, ssm_c_im, ssm_d, w_glu, w_kv, w_xattn_out, w_out, ln1_g, ln1_b, w_up, b_up, w_down, b_down, ln2_g, ln2_b, loss_target, m_w_in, m_b_gate, m_conv_w, m_w_conv_out, m_ssm_lam_re, m_ssm_lam_im, m_ssm_log_dt, m_ssm_b_re, m_ssm_b_im, m_ssm_c_re, m_ssm_c_im, m_ssm_d, m_w_glu, m_w_kv, m_w_xattn_out, m_w_out, m_ln1_g, m_ln1_b, m_w_up, m_b_up, m_w_down, m_b_down, m_ln2_g, m_ln2_b, v_w_in, v_b_gate, v_conv_w, v_w_conv_out, v_ssm_lam_re, v_ssm_lam_im, v_ssm_log_dt, v_ssm_b_re, v_ssm_b_im, v_ssm_c_re, v_ssm_c_im, v_ssm_d, v_w_glu, v_w_kv, v_w_xattn_out, v_w_out, v_ln1_g, v_ln1_b, v_w_up, v_b_up, v_w_down, v_b_down, v_ln2_g, v_ln2_b):
    W = dict(w_in=w_in, b_gate=b_gate, conv_w=conv_w, w_conv_out=w_conv_out, ssm_lam_re=ssm_lam_re,
             ssm_lam_im=ssm_lam_im, ssm_log_dt=ssm_log_dt, ssm_b_re=ssm_b_re, ssm_b_im=ssm_b_im, ssm_c_re=ssm_c_re,
             ssm_c_im=ssm_c_im, ssm_d=ssm_d, w_glu=w_glu, w_kv=w_kv, w_xattn_out=w_xattn_out, w_out=w_out,
             ln1_g=ln1_g, ln1_b=ln1_b, w_up=w_up, b_up=b_up, w_down=w_down, b_down=b_down, ln2_g=ln2_g, ln2_b=ln2_b)
    MOM = dict(w_in=m_w_in, b_gate=m_b_gate, conv_w=m_conv_w, w_conv_out=m_w_conv_out, ssm_lam_re=m_ssm_lam_re,
               ssm_lam_im=m_ssm_lam_im, ssm_log_dt=m_ssm_log_dt, ssm_b_re=m_ssm_b_re, ssm_b_im=m_ssm_b_im,
               ssm_c_re=m_ssm_c_re, ssm_c_im=m_ssm_c_im, ssm_d=m_ssm_d, w_glu=m_w_glu, w_kv=m_w_kv,
               w_xattn_out=m_w_xattn_out, w_out=m_w_out, ln1_g=m_ln1_g, ln1_b=m_ln1_b, w_up=m_w_up, b_up=m_b_up,
               w_down=m_w_down, b_down=m_b_down, ln2_g=m_ln2_g, ln2_b=m_ln2_b)
    VEL = dict(w_in=v_w_in, b_gate=v_b_gate, conv_w=v_conv_w, w_conv_out=v_w_conv_out, ssm_lam_re=v_ssm_lam_re,
               ssm_lam_im=v_ssm_lam_im, ssm_log_dt=v_ssm_log_dt, ssm_b_re=v_ssm_b_re, ssm_b_im=v_ssm_b_im,
               ssm_c_re=v_ssm_c_re, ssm_c_im=v_ssm_c_im, ssm_d=v_ssm_d, w_glu=v_w_glu, w_kv=v_w_kv,
               w_xattn_out=v_w_xattn_out, w_out=v_w_out, ln1_g=v_ln1_g, ln1_b=v_ln1_b, w_up=v_w_up, b_up=v_b_up,
               w_down=v_w_down, b_down=v_b_down, ln2_g=v_ln2_g, ln2_b=v_ln2_b)
    names = list(W)
    xy = 2 * lax.axis_index("x") + lax.axis_index("y")

    xs = x[0]
    S = xs.shape[0]
    mems = mem[0]
    tgt = loss_target[0]

    shard_shapes = {n: W[n].shape[1:] for n in BIG}
    gathered = gather_weights(_pack_rows([W[n][0].astype(bf16) for n in BIG]), "gather_weights")
    G3 = {}
    row = 0
    for n in BIG:
        k, c = shard_shapes[n]
        nr = k * c // PACK_COLS
        G3[n] = gathered[:, row:row + nr, :].reshape(4, k, c)
        row += nr
    rows_big = row
    win3, wco3, wglu3, wxo3, wup3 = G3["w_in"], G3["w_conv_out"], G3["w_glu"], G3["w_xattn_out"], G3["w_up"]
    wkv3 = G3["w_kv"].reshape(1, D_MODEL, 2 * XATTN_WIDTH)
    wout3 = G3["w_out"].reshape(1, D_MODEL, D_MODEL)
    wdn3 = G3["w_down"].reshape(1, D_FF, D_MODEL)

    convw_full = gather_weights(conv_w[0], "gather_conv_w").transpose(1, 0, 2).reshape(3, CONV_WIDTH)

    lr = ssm_lam_re.reshape(1, N_STATES)
    li = ssm_lam_im.reshape(1, N_STATES)
    ldt = jnp.repeat(ssm_log_dt.reshape(SSM_GROUPS), SSM_STATE).reshape(1, N_STATES)
    bt_re, bt_im = _state_rows(ssm_b_re[0]), _state_rows(ssm_b_im[0])
    ar, ai, kr, ki, bbt_r, bbt_i = ssm_prep(lr, li, ldt, bt_re, bt_im)
    bd_r, bd_i = _block_diag_b(bbt_r).astype(bf16), _block_diag_b(bbt_i).astype(bf16)
    cd_r, cd_i = _block_diag_c(ssm_c_re[0]).astype(bf16), _block_diag_c(ssm_c_im[0]).astype(bf16)
    bdt_r, bdt_i = bd_r.transpose(0, 2, 1), bd_i.transpose(0, 2, 1)
    cdt_r, cdt_i = cd_r.transpose(0, 2, 1), cd_i.transpose(0, 2, 1)
    d_skip = ssm_d.reshape(1, SSM_WIDTH)

    proj = mm_fwd(xs, win3, "proj")
    a_conv = conv_fwd(proj, convw_full)
    y_a = mm_fwd(a_conv, wco3, "conv_out")
    u_perm = _perm(proj[:, GATE_COLS + 3 * CONV_WIDTH:GATE_COLS + 3 * CONV_WIDTH + SSM_WIDTH])
    init_r, init_i = ssm_scan_fwd(u_perm, bd_r, bd_i, ar, ai)
    ysm_perm, st_r, st_i = ssm_scan_fwd(u_perm, bd_r, bd_i, ar, ai, final=(cd_r, cd_i, d_skip, init_r, init_i))
    ysm = _unperm(ysm_perm)
    y_s = gelu_fwd(ysm)
    glu = mm_fwd(y_s, wglu3, "glu")
    kv = mm_fwd(mems, wkv3, "kv")
    o_att = attn_fwd(proj, kv)
    y_c = mm_fwd(o_att, wxo3, "xattn_out")
    merged = merge_fwd(proj, b_gate, y_a, glu, y_c)
    t1 = mm_fwd(merged, wout3, "w_out")
    h1 = ln1_fwd(xs, t1, ln1_g, ln1_b)
    up = mm_fwd(h1, wup3, "w_up", bias=b_up)
    t2 = mm_fwd(up, wdn3, "w_down", pre=_relu2, tn=512)

    dr2, loss_part, d_ln2_g, d_ln2_b, d_b_down = ln2_loss_bwd(h1, t2, b_down, ln2_g, ln2_b, tgt)
    g_w_down = mm_bwd_w(up, dr2, 1, "dw_down", pre=_relu2, tn=512)
    dup, d_b_up = mm_bwd_x(dr2, wdn3, "dup", tm=256, extras=(up,), colsum=True,
                           epi=lambda acc, upv: acc * (2.0 * jnp.maximum(upv, 0.0)))
    g_w_up = mm_bwd_w(h1, dup, 4, "dw_up")
    t3 = mm_bwd_x(dup, wup3, "dh1")
    dr1, d_ln1_g, d_ln1_b = ln1_bwd(xs, t1, dr2, t3, ln1_g)
    g_w_out = mm_bwd_w(merged, dr1, 1, "dw_out")
    dmerged = mm_bwd_x(dr1, wout3, "dmerged")
    dgate, dy_a, dglu, dy_c, d_b_gate = merge_bwd(dmerged, proj, b_gate, y_a, glu, y_c)

    g_w_co = mm_bwd_w(a_conv, dy_a, 4, "dw_conv_out")
    da_conv = mm_bwd_x(dy_a, wco3, "da_conv")
    dconv, d_conv_w = conv_bwd(da_conv, proj, convw_full)

    g_w_glu = mm_bwd_w(y_s, dglu, 4, "dw_glu")
    dy_s = mm_bwd_x(dglu, wglu3, "dy_s")
    dysm_perm = _perm(gelu_bwd(dy_s, ysm))
    linit_r, linit_i = ssm_scan_bwd(dysm_perm, cdt_r, cdt_i, ar, ai)
    du_perm, dbacc_r, dbacc_i, dcacc_r, dcacc_i, da_r, da_i, d_ssm_d = ssm_scan_bwd(
        dysm_perm, cdt_r, cdt_i, ar, ai, final=(u_perm, st_r, st_i, bdt_r, bdt_i, d_skip, linit_r, linit_i))
    du = _unperm(du_perm)
    dbt_re, dbt_im, d_lr, d_li, d_ldt_state = ssm_param_bwd(
        _diag_b(dbacc_r), _diag_b(dbacc_i), bt_re, bt_im, kr, ki, ar, ai, lr, li, ldt, da_r, da_i)
    d_log_dt = group_sum(d_ldt_state.reshape(SSM_GROUPS, SSM_STATE))
    d_b_re = dbt_re.reshape(SSM_GROUP, SSM_GROUPS, SSM_STATE).transpose(1, 2, 0)
    d_b_im = dbt_im.reshape(SSM_GROUP, SSM_GROUPS, SSM_STATE).transpose(1, 2, 0)
    d_c_re = _diag_c(dcacc_r)
    d_c_im = -_diag_c(dcacc_i)

    g_w_xo = mm_bwd_w(o_att, dy_c, 4, "dw_xattn_out")
    do_att = mm_bwd_x(dy_c, wxo3, "do_att")
    dq, dkv = attn_bwd(do_att, proj, kv)
    g_w_kv = mm_bwd_w(mems, dkv, 1, "dw_kv")

    dproj = jnp.concatenate([dgate, dconv, du, dq], axis=1)
    g_w_in = mm_bwd_w(xs, dproj, 4, "dw_in")
    dx = mm_bwd_x(dproj, win3, "dx", extras=(dr1,), epi=lambda acc, d: acc + ALPHA * d)

    g_big = {"w_in": g_w_in, "w_conv_out": g_w_co, "w_glu": g_w_glu, "w_kv": g_w_kv.reshape(4, 256, D_MODEL),
             "w_xattn_out": g_w_xo, "w_out": g_w_out.reshape(4, 256, D_MODEL), "w_up": g_w_up,
             "w_down": g_w_down.reshape(4, D_MODEL, D_MODEL)}
    g_all = jnp.concatenate([g_big[n].reshape(4, -1, PACK_COLS) for n in BIG], axis=1)
    g_small = {"b_gate": d_b_gate, "ssm_lam_re": d_lr, "ssm_lam_im": d_li, "ssm_log_dt": d_log_dt, "ssm_b_re": d_b_re,
               "ssm_b_im": d_b_im, "ssm_c_re": d_c_re, "ssm_c_im": d_c_im, "ssm_d": d_ssm_d, "ln1_g": d_ln1_g,
               "ln1_b": d_ln1_b, "b_up": d_b_up, "b_down": d_b_down, "ln2_g": d_ln2_g, "ln2_b": d_ln2_b}
    small_names = SMALL + ("conv_w",)
    g_small["conv_w"] = d_conv_w
    sizes = {n: (g_small[n].size + 1023) // 1024 * 1024 for n in small_names}
    pack = lambda d: jnp.concatenate([_pad_flat(d[n]) for n in small_names]).reshape(-1, 128)
    conv_zero = jnp.zeros((3, CONV_WIDTH), f32)
    recv, srecv = exchange_grads(g_all, pack(g_small))
    part = sum_slots(recv)
    other = swap_sibling(part)
    big_w = _pack_rows([W[n][0] for n in BIG])
    big_m = _pack_rows([MOM[n][0] for n in BIG])
    big_v = _pack_rows([VEL[n][0] for n in BIG])
    gb, db, mb, vb = adam_pair(part, other, big_w, big_m, big_v)
    gs, ds_, ms, vs = adam_slots(srecv, pack({**{n: W[n] for n in SMALL}, "conv_w": conv_zero}),
                                 pack({**{n: MOM[n] for n in SMALL}, "conv_w": conv_zero}),
                                 pack({**{n: VEL[n] for n in SMALL}, "conv_w": conv_zero}), "adam_small")

    def unpack_big(buf):
        out, r = {}, 0
        for n in BIG:
            k, c = shard_shapes[n]
            nr = k * c // PACK_COLS
            out[n] = buf[r:r + nr].reshape(1, k, c)
            r += nr
        return out

    def unpack_small(buf):
        flat = buf.reshape(-1)
        out, r = {}, 0
        for n in small_names:
            ref = g_small[n] if n == "conv_w" else W[n]
            out[n] = flat[r:r + ref.size].reshape(ref.shape)
            r += sizes[n]
        return out

    res = [unpack_big(b) for b in (gb, db, mb, vb)]
    res_s = [unpack_small(b) for b in (gs, ds_, ms, vs)]
    g_conv = lax.dynamic_slice(res_s[0]["conv_w"], (0, xy * 128), (3, 128))
    conv_slots = g_conv.reshape(1, 3, 128)
    cg, cd, cm, cv = adam_slots(conv_slots, conv_w[0], m_conv_w[0], v_conv_w[0], "adam_conv")
    conv_res = [cg, cd, cm, cv]

    loss = lax.psum(loss_part[0, 0], ("x", "y", "c"))
    outs = [loss, dx.reshape(x.shape)]
    for k in range(4):
        for n in names:
            if n == "conv_w":
                outs.append(conv_res[k].reshape(conv_w.shape))
            elif n in BIG:
                outs.append(res[k][n])
            else:
                outs.append(res_s[k][n])
    return tuple(outs)
```

```python
import functools
import math

import jax
import jax.numpy as jnp
from jax import lax
from jax.experimental import pallas as pl
from jax.experimental.pallas import tpu as pltpu

f32 = jnp.float32
bf16 = jnp.bfloat16

D_MODEL = 1024
CONV_WIDTH = 512
SSM_WIDTH = 512
SSM_GROUP = 16
SSM_GROUPS = 32
SSM_STATE = 64
N_STATES = SSM_GROUPS * SSM_STATE
XATTN_HEADS = 4
XATTN_HEAD_DIM = 128
XATTN_WIDTH = 512
D_FF = 4096
GATE_COLS = 3 * D_MODEL
IN_COLS = GATE_COLS + 3 * CONV_WIDTH + SSM_WIDTH + XATTN_WIDTH
ALPHA = 2.0 ** 0.25
LN_EPS = 1e-5
ADAM_LR = 0.001
ADAM_B1 = 0.9
ADAM_B2 = 0.999
ADAM_EPS = 1e-08
ADAM_WD = 0.01
ADAM_STEP = 10

N_SEG = 8
SCAN_ROWS = 256
LANE_STRIP = 512
TM = 512
VMEM_LIMIT = 48 * 1024 * 1024
MESH = pl.DeviceIdType.MESH

NT_DIMS = (((1,), (1,)), ((), ()))
TN_DIMS = (((0,), (0,)), ((), ()))


def _params():
    return pltpu.CompilerParams(vmem_limit_bytes=VMEM_LIMIT)


def _full(shape):
    n = len(shape)
    return pl.BlockSpec(shape, lambda *_: (0,) * n)


def _rows(tm, w, cb=0):
    return pl.BlockSpec((tm, w), lambda i: (i, cb))


def _sig(x):
    return 1.0 / (1.0 + jnp.exp(-x))


def _bf(v):
    return v if v.dtype == bf16 else v.astype(bf16)


def mm_fwd(a, w3, name, *, tm=1024, tn=None, bias=None, outs=(f32,), second=None):
    M, K = a.shape
    J, _, n = w3.shape
    tm = min(tm, M)
    tn = tn or n
    nl = n // tn
    nb = 0 if bias is None else 1

    def body(*refs):
        a_ref, w_ref = refs[0], refs[1]
        acc = jnp.dot(_bf(a_ref[...]), w_ref[0], preferred_element_type=f32)
        if bias is not None:
            acc = acc + refs[2][...]
        refs[2 + nb][...] = acc.astype(outs[0])
        if len(outs) > 1:
            refs[3 + nb][...] = second(acc).astype(outs[1])

    in_specs = [pl.BlockSpec((tm, K), lambda j, l, i: (i, 0)),
                pl.BlockSpec((1, K, tn), lambda j, l, i: (j, 0, l))]
    args = [a, w3]
    if bias is not None:
        in_specs.append(pl.BlockSpec((1, tn), lambda j, l, i: (0, j * nl + l)))
        args.append(bias)
    res = pl.pallas_call(
        body, name=name, grid=(J, nl, M // tm), in_specs=in_specs,
        out_specs=[pl.BlockSpec((tm, tn), lambda j, l, i: (i, j * nl + l)) for _ in outs],
        out_shape=[jax.ShapeDtypeStruct((M, J * n), dt) for dt in outs], compiler_params=_params())(*args)
    return res if len(outs) > 1 else res[0]


def mm_bwd_x(dy, w3, name, *, tm=TM, epi=None, extras=(), colsum=False, out_dtype=f32):
    M = dy.shape[0]
    J, K, n = w3.shape
    tm = min(tm, M)
    nex = len(extras)

    def body(*refs):
        dy_ref, w_hbm = refs[0], refs[1]
        ex = refs[2:2 + nex]
        o_ref = refs[2 + nex]
        w_ref = refs[-1]
        i = pl.program_id(0)

        @pl.when(i == 0)
        def _():
            pltpu.sync_copy(w_hbm, w_ref)

        acc = None
        for j in range(J):
            part = lax.dot_general(_bf(dy_ref[:, j * n:(j + 1) * n]), w_ref[j], NT_DIMS, preferred_element_type=f32)
            acc = part if acc is None else acc + part
        if epi is not None:
            acc = epi(acc, *[e[...] for e in ex])
        o_ref[...] = acc.astype(out_dtype)
        if colsum:
            s_ref = refs[3 + nex]
            cs = jnp.sum(acc, axis=0, keepdims=True)

            @pl.when(i == 0)
            def _():
                s_ref[...] = cs

            @pl.when(i > 0)
            def _():
                s_ref[...] += cs

    in_specs = [pl.BlockSpec((tm, J * n), lambda i: (i, 0)), pl.BlockSpec(memory_space=pl.ANY)]
    in_specs += [pl.BlockSpec((tm, K), lambda i: (i, 0)) for _ in extras]
    out_specs = [pl.BlockSpec((tm, K), lambda i: (i, 0))]
    out_shape = [jax.ShapeDtypeStruct((M, K), out_dtype)]
    if colsum:
        out_specs.append(pl.BlockSpec((1, K), lambda i: (0, 0)))
        out_shape.append(jax.ShapeDtypeStruct((1, K), f32))
    res = pl.pallas_call(body, name=name, grid=(M // tm,), in_specs=in_specs, out_specs=out_specs,
                         out_shape=out_shape, scratch_shapes=[pltpu.VMEM((J, K, n), bf16)],
                         compiler_params=_params())(dy, w3, *extras)
    return res if colsum else res[0]


def mm_bwd_w(a, dy, J, name, *, tm=2048, tn=None, tk=None):
    M, K = a.shape
    n = dy.shape[1] // J
    tm = min(tm, M)
    tn = tn or n
    tk = tk or K
    nl = n // tn
    nk = K // tk
    ns = M // tm

    def body(a_ref, dy_ref, o_ref, acc_ref):
        s = pl.program_id(3)
        part = lax.dot_general(_bf(a_ref[...]), _bf(dy_ref[...]), TN_DIMS, preferred_element_type=f32)

        @pl.when(s == 0)
        def _():
            acc_ref[...] = part

        @pl.when(s > 0)
        def _():
            acc_ref[...] += part

        @pl.when(s == ns - 1)
        def _():
            o_ref[0] = acc_ref[...].astype(bf16)

    return pl.pallas_call(
        body, name=name, grid=(J, nl, nk, ns),
        in_specs=[pl.BlockSpec((tm, tk), lambda j, l, k, s: (s, k)),
                  pl.BlockSpec((tm, tn), lambda j, l, k, s: (s, j * nl + l))],
        out_specs=pl.BlockSpec((1, tk, tn), lambda j, l, k, s: (j, k, l)),
        out_shape=jax.ShapeDtypeStruct((J, K, n), bf16),
        scratch_shapes=[pltpu.VMEM((tk, tn), f32)], compiler_params=_params())(a, dy)


def _relu2(v):
    r = jnp.maximum(v, 0.0)
    return r * r


def _shift_down(z, k, halo):
    r = lax.broadcasted_iota(jnp.int32, z.shape, 0)
    y = pltpu.roll(z, k, 0)
    for q in range(k):
        y = jnp.where(r == q, halo[8 - k + q:8 - k + q + 1, :], y)
    return y


def _shift_up(z, k, halo):
    tm = z.shape[0]
    r = lax.broadcasted_iota(jnp.int32, z.shape, 0)
    y = pltpu.roll(z, tm - k, 0)
    for q in range(k):
        y = jnp.where(r == tm - k + q, halo[q:q + 1, :], y)
    return y


def _prev_halo(tm, cb):
    return pl.BlockSpec((8, CONV_WIDTH), lambda i: (jnp.maximum(i * (tm // 8) - 1, 0), cb))


def _next_halo(tm, cb, nrow8):
    return pl.BlockSpec((8, CONV_WIDTH), lambda i: (jnp.minimum((i + 1) * (tm // 8), nrow8 - 1), cb))


def conv_fwd(proj, conv_w, tm=TM):
    S = proj.shape[0]

    def body(cb_ref, cc_ref, ch_ref, cch_ref, chh_ref, w_ref, a_ref):
        i = pl.program_id(0)
        z = cc_ref[...] * ch_ref[...]
        zh = jnp.where(i == 0, 0.0, cch_ref[...] * chh_ref[...])
        w = w_ref[...]
        dwz = w[0:1, :] * _shift_down(z, 2, zh) + w[1:2, :] * _shift_down(z, 1, zh) + w[2:3, :] * z
        a_ref[...] = (cb_ref[...] * dwz).astype(bf16)

    return pl.pallas_call(
        body, name="conv_fwd", grid=(S // tm,),
        in_specs=[_rows(tm, CONV_WIDTH, 6), _rows(tm, CONV_WIDTH, 7), _rows(tm, CONV_WIDTH, 8),
                  _prev_halo(tm, 7), _prev_halo(tm, 8), _full((3, CONV_WIDTH))],
        out_specs=_rows(tm, CONV_WIDTH),
        out_shape=jax.ShapeDtypeStruct((S, CONV_WIDTH), bf16), compiler_params=_params())(
            proj, proj, proj, proj, proj, conv_w)


def conv_bwd(da, proj, conv_w, dproj, tm=TM):
    S = proj.shape[0]
    nt = S // tm

    def body(da_ref, cb_ref, cc_ref, ch_ref, cch_ref, chh_ref, dan_ref, cbn_ref, w_ref, _, o_ref, dw_ref):
        i = pl.program_id(0)
        cc, ch, cb, dav = cc_ref[...], ch_ref[...], cb_ref[...], da_ref[...]
        z = cc * ch
        zh = jnp.where(i == 0, 0.0, cch_ref[...] * chh_ref[...])
        w = w_ref[...]
        z1 = _shift_down(z, 1, zh)
        z2 = _shift_down(z, 2, zh)
        dwz = w[0:1, :] * z2 + w[1:2, :] * z1 + w[2:3, :] * z
        g = dav * cb
        gn = jnp.where(i == nt - 1, 0.0, dan_ref[...] * cbn_ref[...])
        dz = w[2:3, :] * g + w[1:2, :] * _shift_up(g, 1, gn) + w[0:1, :] * _shift_up(g, 2, gn)
        o_ref[:, 0:CONV_WIDTH] = (dav * dwz).astype(bf16)
        o_ref[:, CONV_WIDTH:2 * CONV_WIDTH] = (dz * ch).astype(bf16)
        o_ref[:, 2 * CONV_WIDTH:3 * CONV_WIDTH] = (dz * cc).astype(bf16)

        @pl.when(i == 0)
        def _():
            dw_ref[...] = jnp.zeros_like(dw_ref)

        dw_ref[0:1, :] += jnp.sum(g * z2, axis=0, keepdims=True)
        dw_ref[1:2, :] += jnp.sum(g * z1, axis=0, keepdims=True)
        dw_ref[2:3, :] += jnp.sum(g * z, axis=0, keepdims=True)

    return pl.pallas_call(
        body, name="conv_bwd", grid=(nt,),
        in_specs=[_rows(tm, CONV_WIDTH), _rows(tm, CONV_WIDTH, 6), _rows(tm, CONV_WIDTH, 7), _rows(tm, CONV_WIDTH, 8),
                  _prev_halo(tm, 7), _prev_halo(tm, 8),
                  pl.BlockSpec((8, CONV_WIDTH), lambda i: (jnp.minimum((i + 1) * (tm // 8), S // 8 - 1), 0)),
                  _next_halo(tm, 6, S // 8), _full((3, CONV_WIDTH)), pl.BlockSpec(memory_space=pl.ANY)],
        out_specs=[_rows(tm, 3 * CONV_WIDTH, GATE_COLS // (3 * CONV_WIDTH)), _full((3, CONV_WIDTH))],
        out_shape=[jax.ShapeDtypeStruct(dproj.shape, bf16), jax.ShapeDtypeStruct((3, CONV_WIDTH), f32)],
        input_output_aliases={9: 0},
        compiler_params=_params())(da, proj, proj, proj, proj, proj, da, proj, conv_w, dproj)


def _cmul(ar, ai, br, bi):
    return ar * br - ai * bi, ar * bi + ai * br


def ssm_prep(lr, li, ldt, bt_re, bt_im):
    def body(lr_ref, li_ref, ldt_ref, br_ref, bi_ref, ar_ref, ai_ref, kr_ref, ki_ref, bbr_ref, bbi_ref):
        lrv, liv = lr_ref[...], li_ref[...]
        dt = jnp.exp(ldt_ref[...])
        mag = jnp.exp(lrv * dt)
        ar = mag * jnp.cos(liv * dt)
        ai = mag * jnp.sin(liv * dt)
        den = lrv * lrv + liv * liv
        nr = ar - 1.0
        kr = (nr * lrv + ai * liv) / den
        ki = (ai * lrv - nr * liv) / den
        ar_ref[...], ai_ref[...], kr_ref[...], ki_ref[...] = ar, ai, kr, ki
        bbr_ref[...] = kr * br_ref[...] - ki * bi_ref[...]
        bbi_ref[...] = kr * bi_ref[...] + ki * br_ref[...]

    v = jax.ShapeDtypeStruct((1, N_STATES), f32)
    m = jax.ShapeDtypeStruct((SSM_GROUP, N_STATES), f32)
    return pl.pallas_call(body, name="ssm_prep", out_shape=[v, v, v, v, m, m])(lr, li, ldt, bt_re, bt_im)


def _pow_segment(ar, ai, seg_len):
    pr, pi = ar, ai
    for _ in range(int(math.log2(seg_len))):
        pr, pi = _cmul(pr, pi, pr, pi)
    return pr, pi


GELU_K = math.sqrt(2.0 / math.pi)
GELU_C = 0.044715


def _gelu(v):
    return 0.5 * v * (1.0 + jnp.tanh(GELU_K * (v + GELU_C * v * v * v)))


def _gelu_grad(v):
    t = jnp.tanh(GELU_K * (v + GELU_C * v * v * v))
    return 0.5 * (1.0 + t) + 0.5 * v * (1.0 - t * t) * GELU_K * (1.0 + 3.0 * GELU_C * v * v)


def ssm_scan_fwd(u_perm, bd_r, bd_i, ar, ai, *, final=None):
    S = u_perm.shape[0]
    R = SCAN_ROWS
    nblk = S // R
    seg_len = S // N_SEG
    nstrip = N_STATES // LANE_STRIP
    store = final is not None

    def body(*refs):
        if store:
            (u_ref, bdr_ref, bdi_ref, ar_ref, ai_ref, cdr_ref, cdi_ref, d_ref, ir_ref, ii_ref,
             y_ref, ys_ref, sr_ref, si_ref, bur, bui, car, cai) = refs
        else:
            u_ref, bdr_ref, bdi_ref, ar_ref, ai_ref, or_ref, oi_ref, bur, bui, car, cai = refs
        i = pl.program_id(0)
        u = u_ref[...]
        ub = u.astype(bf16)
        for cb in range(4):
            us = ub[:, cb * 128:(cb + 1) * 128]
            bur[:, cb * 512:(cb + 1) * 512] = jnp.dot(us, bdr_ref[cb], preferred_element_type=f32)
            bui[:, cb * 512:(cb + 1) * 512] = jnp.dot(us, bdi_ref[cb], preferred_element_type=f32)

        @pl.when(i == 0)
        def _():
            if store:
                car[...] = ir_ref[...]
                cai[...] = ii_ref[...]
            else:
                car[...] = jnp.zeros_like(car)
                cai[...] = jnp.zeros_like(cai)

        for ls in range(nstrip):
            lanes = pl.ds(ls * LANE_STRIP, LANE_STRIP)
            a_r = jnp.broadcast_to(ar_ref[:, lanes], (N_SEG, LANE_STRIP))
            a_i = jnp.broadcast_to(ai_ref[:, lanes], (N_SEG, LANE_STRIP))

            def step(t, carry, lanes=lanes, a_r=a_r, a_i=a_i):
                s_r, s_i = carry
                row = pl.multiple_of(t * 8, 8)
                n_r = a_r * s_r - a_i * s_i + bur[pl.ds(row, 8), lanes]
                n_i = a_r * s_i + a_i * s_r + bui[pl.ds(row, 8), lanes]
                if store:
                    sr_ref[pl.ds(row, 8), lanes] = n_r
                    si_ref[pl.ds(row, 8), lanes] = n_i
                return n_r, n_i

            e_r, e_i = lax.fori_loop(0, R // 8, step, (car[:, lanes], cai[:, lanes]), unroll=4)
            car[:, lanes] = e_r
            cai[:, lanes] = e_i

        if store:
            for cb in range(4):
                st_r = sr_ref[:, cb * 512:(cb + 1) * 512].astype(bf16)
                st_i = si_ref[:, cb * 512:(cb + 1) * 512].astype(bf16)
                y = (jnp.dot(st_r, cdr_ref[cb], preferred_element_type=f32)
                     - jnp.dot(st_i, cdi_ref[cb], preferred_element_type=f32))
                cols = slice(cb * 128, (cb + 1) * 128)
                y = y + d_ref[:, cols] * u[:, cols]
                y_ref[:, cols] = y
                ys_ref[:, cols] = _gelu(y).astype(bf16)
        else:
            @pl.when(i == nblk - 1)
            def _():
                p_r, p_i = _pow_segment(ar_ref[...], ai_ref[...], seg_len)
                t_r, t_i = car[0:1, :], cai[0:1, :]
                or_ref[0:1, :] = jnp.zeros((1, N_STATES), f32)
                oi_ref[0:1, :] = jnp.zeros((1, N_STATES), f32)
                for j in range(1, N_SEG):
                    or_ref[j:j + 1, :] = t_r
                    oi_ref[j:j + 1, :] = t_i
                    m_r, m_i = _cmul(p_r, p_i, t_r, t_i)
                    t_r, t_i = car[j:j + 1, :] + m_r, cai[j:j + 1, :] + m_i

    blk = lambda w: pl.BlockSpec((R, w), lambda i: (i, 0))
    in_specs = [blk(SSM_WIDTH), _full((4, 128, 512)), _full((4, 128, 512)), _full((1, N_STATES)), _full((1, N_STATES))]
    args = [u_perm, bd_r, bd_i, ar, ai]
    scratch = [pltpu.VMEM((R, N_STATES), f32), pltpu.VMEM((R, N_STATES), f32),
               pltpu.VMEM((N_SEG, N_STATES), f32), pltpu.VMEM((N_SEG, N_STATES), f32)]
    if store:
        in_specs += [_full((4, 512, 128)), _full((4, 512, 128)), _full((1, SSM_WIDTH)),
                     _full((N_SEG, N_STATES)), _full((N_SEG, N_STATES))]
        args += list(final)
        out_specs = [blk(SSM_WIDTH), blk(SSM_WIDTH), blk(N_STATES), blk(N_STATES)]
        out_shape = [jax.ShapeDtypeStruct((S, SSM_WIDTH), f32), jax.ShapeDtypeStruct((S, SSM_WIDTH), bf16),
                     jax.ShapeDtypeStruct((S, N_STATES), f32), jax.ShapeDtypeStruct((S, N_STATES), f32)]
        name = "ssm_scan_fwd"
    else:
        out_specs = [_full((N_SEG, N_STATES)), _full((N_SEG, N_STATES))]
        out_shape = [jax.ShapeDtypeStruct((N_SEG, N_STATES), f32)] * 2
        name = "ssm_scan_fwd_carry"
    return pl.pallas_call(body, name=name, grid=(nblk,), in_specs=in_specs, out_specs=out_specs,
                          out_shape=out_shape, scratch_shapes=scratch, compiler_params=_params())(*args)


def ssm_scan_bwd(dys_perm, y_perm, cdt_r, cdt_i, ar, ai, *, final=None):
    S = dys_perm.shape[0]
    R = SCAN_ROWS
    nblk = S // R
    seg_len = S // N_SEG
    nstrip = N_STATES // LANE_STRIP
    store = final is not None

    def body(*refs):
        if store:
            (dys_ref, y_ref, cdr_ref, cdi_ref, ar_ref, ai_ref, u_ref, sr_ref, si_ref, bdr_ref, bdi_ref, d_ref, ir_ref,
             ii_ref, du_ref, dbr_ref, dbi_ref, dcr_ref, dci_ref, dar_ref, dai_ref, dd_ref, dsr, dsi, lmr, lmi, car,
             cai) = refs
        else:
            dys_ref, y_ref, cdr_ref, cdi_ref, ar_ref, ai_ref, or_ref, oi_ref, dsr, dsi, car, cai = refs
        i = pl.program_id(0)
        dy = dys_ref[...] * _gelu_grad(y_ref[...])
        dyb = dy.astype(bf16)
        for cb in range(4):
            ds_ = dyb[:, cb * 128:(cb + 1) * 128]
            dsr[:, cb * 512:(cb + 1) * 512] = jnp.dot(ds_, cdr_ref[cb], preferred_element_type=f32)
            dsi[:, cb * 512:(cb + 1) * 512] = -jnp.dot(ds_, cdi_ref[cb], preferred_element_type=f32)

        @pl.when(i == 0)
        def _():
            if store:
                car[...] = ir_ref[...]
                cai[...] = ii_ref[...]
                dar_ref[...] = jnp.zeros_like(dar_ref)
                dai_ref[...] = jnp.zeros_like(dai_ref)
                dbr_ref[...] = jnp.zeros_like(dbr_ref)
                dbi_ref[...] = jnp.zeros_like(dbi_ref)
                dcr_ref[...] = jnp.zeros_like(dcr_ref)
                dci_ref[...] = jnp.zeros_like(dci_ref)
                dd_ref[...] = jnp.zeros_like(dd_ref)
            else:
                car[...] = jnp.zeros_like(car)
                cai[...] = jnp.zeros_like(cai)

        for ls in range(nstrip):
            lanes = pl.ds(ls * LANE_STRIP, LANE_STRIP)
            a_r = jnp.broadcast_to(ar_ref[:, lanes], (N_SEG, LANE_STRIP))
            a_i = jnp.broadcast_to(ai_ref[:, lanes], (N_SEG, LANE_STRIP))
            zero = jnp.zeros((N_SEG, LANE_STRIP), f32)

            def step(k, carry, lanes=lanes, a_r=a_r, a_i=a_i):
                l_r, l_i, g_r, g_i = carry
                row = pl.multiple_of((R // 8 - 1 - k) * 8, 8)
                if store:
                    s_r = sr_ref[pl.ds(row, 8), lanes]
                    s_i = si_ref[pl.ds(row, 8), lanes]
                    g_r = g_r + l_r * s_r + l_i * s_i
                    g_i = g_i + l_i * s_r - l_r * s_i
                n_r = dsr[pl.ds(row, 8), lanes] + a_r * l_r + a_i * l_i
                n_i = dsi[pl.ds(row, 8), lanes] + a_r * l_i - a_i * l_r
                if store:
                    lmr[pl.ds(row, 8), lanes] = n_r
                    lmi[pl.ds(row, 8), lanes] = n_i
                return n_r, n_i, g_r, g_i

            e_r, e_i, g_r, g_i = lax.fori_loop(0, R // 8, step, (car[:, lanes], cai[:, lanes], zero, zero), unroll=4)
            car[:, lanes] = e_r
            cai[:, lanes] = e_i
            if store:
                dar_ref[:, lanes] += g_r
                dai_ref[:, lanes] += g_i

        if store:
            u = u_ref[...]
            ub = u.astype(bf16)
            for cb in range(4):
                cols = slice(cb * 128, (cb + 1) * 128)
                st = slice(cb * 512, (cb + 1) * 512)
                l_r = lmr[:, st].astype(bf16)
                l_i = lmi[:, st].astype(bf16)
                du = (jnp.dot(l_r, bdr_ref[cb], preferred_element_type=f32)
                      + jnp.dot(l_i, bdi_ref[cb], preferred_element_type=f32))
                du_ref[:, cols] = (du + d_ref[:, cols] * dy[:, cols]).astype(bf16)
                dbr_ref[cb] += lax.dot_general(l_r, ub[:, cols], TN_DIMS, preferred_element_type=f32)
                dbi_ref[cb] += lax.dot_general(l_i, ub[:, cols], TN_DIMS, preferred_element_type=f32)
                dcr_ref[cb] += lax.dot_general(dyb[:, cols], sr_ref[:, st].astype(bf16), TN_DIMS,
                                               preferred_element_type=f32)
                dci_ref[cb] += lax.dot_general(dyb[:, cols], si_ref[:, st].astype(bf16), TN_DIMS,
                                               preferred_element_type=f32)
            dd_ref[...] += jnp.sum(dy * u, axis=0, keepdims=True)
        else:
            @pl.when(i == nblk - 1)
            def _():
                p_r, p_i = _pow_segment(ar_ref[...], ai_ref[...], seg_len)
                p_i = -p_i
                t_r, t_i = car[N_SEG - 1:N_SEG, :], cai[N_SEG - 1:N_SEG, :]
                or_ref[N_SEG - 1:N_SEG, :] = jnp.zeros((1, N_STATES), f32)
                oi_ref[N_SEG - 1:N_SEG, :] = jnp.zeros((1, N_STATES), f32)
                for j in range(N_SEG - 2, -1, -1):
                    or_ref[j:j + 1, :] = t_r
                    oi_ref[j:j + 1, :] = t_i
                    m_r, m_i = _cmul(p_r, p_i, t_r, t_i)
                    t_r, t_i = car[j:j + 1, :] + m_r, cai[j:j + 1, :] + m_i

    blk = lambda w: pl.BlockSpec((R, w), lambda i: (nblk - 1 - i, 0))
    in_specs = [blk(SSM_WIDTH), blk(SSM_WIDTH), _full((4, 128, 512)), _full((4, 128, 512)), _full((1, N_STATES)),
                _full((1, N_STATES))]
    args = [dys_perm, y_perm, cdt_r, cdt_i, ar, ai]
    seg = jax.ShapeDtypeStruct((N_SEG, N_STATES), f32)
    if store:
        in_specs += [blk(SSM_WIDTH), blk(N_STATES), blk(N_STATES), _full((4, 512, 128)), _full((4, 512, 128)),
                     _full((1, SSM_WIDTH)), _full((N_SEG, N_STATES)), _full((N_SEG, N_STATES))]
        args += list(final)
        out_specs = [blk(SSM_WIDTH), _full((4, 512, 128)), _full((4, 512, 128)), _full((4, 128, 512)),
                     _full((4, 128, 512)), _full((N_SEG, N_STATES)), _full((N_SEG, N_STATES)), _full((1, SSM_WIDTH))]
        b_acc = jax.ShapeDtypeStruct((4, 512, 128), f32)
        c_acc = jax.ShapeDtypeStruct((4, 128, 512), f32)
        out_shape = [jax.ShapeDtypeStruct((S, SSM_WIDTH), bf16), b_acc, b_acc, c_acc, c_acc, seg, seg,
                     jax.ShapeDtypeStruct((1, SSM_WIDTH), f32)]
        scratch = [pltpu.VMEM((R, N_STATES), f32)] * 4 + [pltpu.VMEM((N_SEG, N_STATES), f32)] * 2
        name = "ssm_scan_bwd"
    else:
        out_specs = [_full((N_SEG, N_STATES)), _full((N_SEG, N_STATES))]
        out_shape = [seg, seg]
        scratch = [pltpu.VMEM((R, N_STATES), f32)] * 2 + [pltpu.VMEM((N_SEG, N_STATES), f32)] * 2
        name = "ssm_scan_bwd_carry"
    return pl.pallas_call(body, name=name, grid=(nblk,), in_specs=in_specs, out_specs=out_specs,
                          out_shape=out_shape, scratch_shapes=scratch, compiler_params=_params())(*args)


def ssm_param_bwd(dbb_r, dbb_i, bt_re, bt_im, kr, ki, ar, ai, lr, li, ldt, da_r, da_i):
    def body(dbr_ref, dbi_ref, br_ref, bi_ref, kr_ref, ki_ref, ar_ref, ai_ref, lr_ref, li_ref, ldt_ref, dar_ref,
             dai_ref, obr_ref, obi_ref, olr_ref, oli_ref, odt_ref):
        dbr, dbi, b_r, b_i = dbr_ref[...], dbi_ref[...], br_ref[...], bi_ref[...]
        k_r, k_i, a_r, a_i = kr_ref[...], ki_ref[...], ar_ref[...], ai_ref[...]
        l_r, l_i = lr_ref[...], li_ref[...]
        dt = jnp.exp(ldt_ref[...])
        obr_ref[...] = k_r * dbr + k_i * dbi
        obi_ref[...] = k_r * dbi - k_i * dbr
        gk_r = jnp.sum(dbr * b_r + dbi * b_i, axis=0, keepdims=True)
        gk_i = jnp.sum(dbi * b_r - dbr * b_i, axis=0, keepdims=True)
        ga_r = jnp.sum(dar_ref[...], axis=0, keepdims=True)
        ga_i = jnp.sum(dai_ref[...], axis=0, keepdims=True)
        den = l_r * l_r + l_i * l_i
        c_r, c_i = l_r / den, l_i / den
        m_r, m_i = _cmul(c_r, c_i, gk_r, gk_i)
        g_r, g_i = ga_r + m_r, ga_i + m_i
        t1_r, t1_i = _cmul(dt * a_r, -dt * a_i, g_r, g_i)
        q_r, q_i = _cmul(k_r, k_i, c_r, -c_i)
        t2_r, t2_i = _cmul(-q_r, q_i, gk_r, gk_i)
        olr_ref[...] = t1_r + t2_r
        oli_ref[...] = t1_i + t2_i
        w_r, w_i = _cmul(l_r, l_i, a_r, a_i)
        odt_ref[...] = dt * (w_r * g_r + w_i * g_i)

    v = jax.ShapeDtypeStruct((1, N_STATES), f32)
    m = jax.ShapeDtypeStruct((SSM_GROUP, N_STATES), f32)
    return pl.pallas_call(body, name="ssm_param_bwd", out_shape=[m, m, v, v, v])(
        dbb_r, dbb_i, bt_re, bt_im, kr, ki, ar, ai, lr, li, ldt, da_r, da_i)


def group_sum(v):
    def body(v_ref, o_ref):
        o_ref[...] = jnp.sum(v_ref[...], axis=-1, keepdims=True)
    return pl.pallas_call(body, name="group_sum", out_shape=jax.ShapeDtypeStruct((v.shape[0], 1), f32))(v)


ATTN_SCALE = XATTN_HEAD_DIM ** -0.5


def _attn_probs(q_h, k_h):
    s = lax.dot_general(q_h, k_h, NT_DIMS, preferred_element_type=f32) * ATTN_SCALE
    e = jnp.exp(s - jnp.max(s, axis=-1, keepdims=True))
    return e / jnp.sum(e, axis=-1, keepdims=True)


def attn_fwd(proj, kv, tm=TM):
    S = proj.shape[0]
    M = kv.shape[0]

    def body(q_ref, kv_ref, o_ref):
        for h in range(XATTN_HEADS):
            cols = slice(h * XATTN_HEAD_DIM, (h + 1) * XATTN_HEAD_DIM)
            q_h = q_ref[:, cols].astype(bf16)
            k_h = kv_ref[:, cols]
            v_h = kv_ref[:, XATTN_WIDTH + h * XATTN_HEAD_DIM:XATTN_WIDTH + (h + 1) * XATTN_HEAD_DIM]
            p = _attn_probs(q_h, k_h)
            o_ref[:, cols] = jnp.dot(p.astype(bf16), v_h, preferred_element_type=f32).astype(bf16)

    return pl.pallas_call(
        body, name="attn_fwd", grid=(S // tm,), in_specs=[_rows(tm, XATTN_WIDTH, 10), _full((M, 2 * XATTN_WIDTH))],
        out_specs=_rows(tm, XATTN_WIDTH), out_shape=jax.ShapeDtypeStruct((S, XATTN_WIDTH), bf16),
        compiler_params=_params())(proj, kv)


def attn_bwd(do, proj, kv, dproj, tm=TM):
    S = proj.shape[0]
    M = kv.shape[0]

    def body(do_ref, q_ref, kv_ref, _, dq_ref, dkv_ref):
        i = pl.program_id(0)

        @pl.when(i == 0)
        def _():
            dkv_ref[...] = jnp.zeros_like(dkv_ref)

        for h in range(XATTN_HEADS):
            cols = slice(h * XATTN_HEAD_DIM, (h + 1) * XATTN_HEAD_DIM)
            vcols = slice(XATTN_WIDTH + h * XATTN_HEAD_DIM, XATTN_WIDTH + (h + 1) * XATTN_HEAD_DIM)
            q_h = q_ref[:, cols].astype(bf16)
            k_h = kv_ref[:, cols]
            v_h = kv_ref[:, vcols]
            do_h = do_ref[:, cols]
            p = _attn_probs(q_h, k_h)
            dp = lax.dot_general(do_h, v_h, NT_DIMS, preferred_element_type=f32)
            ds = (p * (dp - jnp.sum(dp * p, axis=-1, keepdims=True)) * ATTN_SCALE).astype(bf16)
            dq_ref[:, cols] = jnp.dot(ds, k_h, preferred_element_type=f32).astype(bf16)
            dkv_ref[:, cols] += lax.dot_general(ds, q_h, TN_DIMS, preferred_element_type=f32)
            dkv_ref[:, vcols] += lax.dot_general(p.astype(bf16), do_h, TN_DIMS, preferred_element_type=f32)

    return pl.pallas_call(
        body, name="attn_bwd", grid=(S // tm,),
        in_specs=[_rows(tm, XATTN_WIDTH), _rows(tm, XATTN_WIDTH, 10), _full((M, 2 * XATTN_WIDTH)),
                  pl.BlockSpec(memory_space=pl.ANY)],
        out_specs=[_rows(tm, XATTN_WIDTH, 10), _full((M, 2 * XATTN_WIDTH))],
        out_shape=[jax.ShapeDtypeStruct(dproj.shape, bf16), jax.ShapeDtypeStruct((M, 2 * XATTN_WIDTH), f32)],
        input_output_aliases={3: 0},
        compiler_params=_params())(do, proj, kv, dproj)


def merge_fwd(proj, b_gate, y_a, glu, y_c, tm=TM):
    S = proj.shape[0]

    def body(g0_ref, g1_ref, g2_ref, b_ref, ya_ref, ga_ref, gb_ref, yc_ref, o_ref):
        b = b_ref[...]
        g0 = _sig(g0_ref[...] + b[:, 0:D_MODEL])
        g1 = _sig(g1_ref[...] + b[:, D_MODEL:2 * D_MODEL])
        g2 = _sig(g2_ref[...] + b[:, 2 * D_MODEL:3 * D_MODEL])
        y_b = ga_ref[...] * _sig(gb_ref[...])
        o_ref[...] = (g0 * ya_ref[...] + g1 * y_b + g2 * yc_ref[...]).astype(bf16)

    return pl.pallas_call(
        body, name="merge_fwd", grid=(S // tm,),
        in_specs=[_rows(tm, D_MODEL, 0), _rows(tm, D_MODEL, 1), _rows(tm, D_MODEL, 2), _full((1, GATE_COLS)),
                  _rows(tm, D_MODEL), _rows(tm, D_MODEL, 0), _rows(tm, D_MODEL, 1), _rows(tm, D_MODEL)],
        out_specs=_rows(tm, D_MODEL), out_shape=jax.ShapeDtypeStruct((S, D_MODEL), bf16),
        compiler_params=_params())(proj, proj, proj, b_gate, y_a, glu, glu, y_c)


def merge_bwd(dm, proj, b_gate, y_a, glu, y_c, tm=256):
    S = proj.shape[0]

    def body(dm_ref, g0_ref, g1_ref, g2_ref, b_ref, ya_ref, ga_ref, gb_ref, yc_ref,
             dg_ref, dya_ref, dgl_ref, dyc_ref, dbg_ref):
        i = pl.program_id(0)
        b = b_ref[...]
        dmv = dm_ref[...]
        ga, sb = ga_ref[...], _sig(gb_ref[...])
        ys = (ya_ref[...], ga * sb, yc_ref[...])
        gs = tuple(_sig(r[...] + b[:, k * D_MODEL:(k + 1) * D_MODEL]) for k, r in enumerate((g0_ref, g1_ref, g2_ref)))

        @pl.when(i == 0)
        def _():
            dbg_ref[...] = jnp.zeros_like(dbg_ref)

        for k in range(3):
            dpre = dmv * ys[k] * gs[k] * (1.0 - gs[k])
            dg_ref[:, k * D_MODEL:(k + 1) * D_MODEL] = dpre.astype(bf16)
            dbg_ref[:, k * D_MODEL:(k + 1) * D_MODEL] += jnp.sum(dpre, axis=0, keepdims=True)
        dya_ref[...] = (dmv * gs[0]).astype(bf16)
        dyc_ref[...] = (dmv * gs[2]).astype(bf16)
        dyb = dmv * gs[1]
        dgl_ref[:, 0:D_MODEL] = (dyb * sb).astype(bf16)
        dgl_ref[:, D_MODEL:2 * D_MODEL] = (dyb * ga * sb * (1.0 - sb)).astype(bf16)

    return pl.pallas_call(
        body, name="merge_bwd", grid=(S // tm,),
        in_specs=[_rows(tm, D_MODEL), _rows(tm, D_MODEL, 0), _rows(tm, D_MODEL, 1), _rows(tm, D_MODEL, 2),
                  _full((1, GATE_COLS)), _rows(tm, D_MODEL), _rows(tm, D_MODEL, 0), _rows(tm, D_MODEL, 1),
                  _rows(tm, D_MODEL)],
        out_specs=[_rows(tm, GATE_COLS), _rows(tm, D_MODEL), _rows(tm, 2 * D_MODEL), _rows(tm, D_MODEL),
                   _full((1, GATE_COLS))],
        out_shape=[jax.ShapeDtypeStruct((S, IN_COLS), bf16), jax.ShapeDtypeStruct((S, D_MODEL), bf16),
                   jax.ShapeDtypeStruct((S, 2 * D_MODEL), bf16), jax.ShapeDtypeStruct((S, D_MODEL), bf16),
                   jax.ShapeDtypeStruct((1, GATE_COLS), f32)],
        compiler_params=_params())(dm, proj, proj, proj, b_gate, y_a, glu, glu, y_c)


def _ln_stats(r):
    mu = jnp.mean(r, axis=-1, keepdims=True)
    xc = r - mu
    var = jnp.mean(xc * xc, axis=-1, keepdims=True)
    rstd = lax.rsqrt(var + LN_EPS)
    return xc * rstd, rstd


def _ln_bwd(dy, xhat, rstd, g):
    dxh = dy * g
    return rstd * (dxh - jnp.mean(dxh, axis=-1, keepdims=True) - xhat * jnp.mean(dxh * xhat, axis=-1, keepdims=True))


def ln1_fwd(x, t1, g, b, tm=TM):
    S = x.shape[0]

    def body(x_ref, t_ref, g_ref, b_ref, o_ref, ob_ref):
        xhat, _ = _ln_stats(ALPHA * x_ref[...] + t_ref[...])
        h = xhat * g_ref[...] + b_ref[...]
        o_ref[...] = h
        ob_ref[...] = h.astype(bf16)

    return pl.pallas_call(
        body, name="ln1_fwd", grid=(S // tm,),
        in_specs=[_rows(tm, D_MODEL), _rows(tm, D_MODEL), _full((1, D_MODEL)), _full((1, D_MODEL))],
        out_specs=[_rows(tm, D_MODEL), _rows(tm, D_MODEL)],
        out_shape=[jax.ShapeDtypeStruct((S, D_MODEL), f32), jax.ShapeDtypeStruct((S, D_MODEL), bf16)],
        compiler_params=_params())(x, t1, g, b)


def ln2_loss_bwd(h1, t2, b_down, g, b, target, tm=TM):
    S = h1.shape[0]

    def body(h_ref, t_ref, bd_ref, g_ref, b_ref, y_ref, dr_ref, drb_ref, l_ref, dg_ref, db_ref, dbd_ref):
        i = pl.program_id(0)
        xhat, rstd = _ln_stats(ALPHA * h_ref[...] + t_ref[...] + bd_ref[...])
        gv = g_ref[...]
        err = xhat * gv + b_ref[...] - y_ref[...]
        dout = err * (1.0 / D_MODEL)
        dr = _ln_bwd(dout, xhat, rstd, gv)
        dr_ref[...] = dr
        drb_ref[...] = dr.astype(bf16)
        part = 0.5 * jnp.sum(jnp.sum(err * err, axis=-1, keepdims=True) * (1.0 / D_MODEL), axis=0, keepdims=True)

        @pl.when(i == 0)
        def _():
            l_ref[...] = jnp.zeros_like(l_ref)
            dg_ref[...] = jnp.zeros_like(dg_ref)
            db_ref[...] = jnp.zeros_like(db_ref)
            dbd_ref[...] = jnp.zeros_like(dbd_ref)

        l_ref[...] += jnp.broadcast_to(part, l_ref.shape)
        dg_ref[...] += jnp.sum(dout * xhat, axis=0, keepdims=True)
        db_ref[...] += jnp.sum(dout, axis=0, keepdims=True)
        dbd_ref[...] += jnp.sum(dr, axis=0, keepdims=True)

    vec = jax.ShapeDtypeStruct((1, D_MODEL), f32)
    return pl.pallas_call(
        body, name="ln2_loss_bwd", grid=(S // tm,),
        in_specs=[_rows(tm, D_MODEL), _rows(tm, D_MODEL), _full((1, D_MODEL)), _full((1, D_MODEL)),
                  _full((1, D_MODEL)), _rows(tm, D_MODEL)],
        out_specs=[_rows(tm, D_MODEL), _rows(tm, D_MODEL), _full((1, 128)), _full((1, D_MODEL)), _full((1, D_MODEL)),
                   _full((1, D_MODEL))],
        out_shape=[jax.ShapeDtypeStruct((S, D_MODEL), f32), jax.ShapeDtypeStruct((S, D_MODEL), bf16),
                   jax.ShapeDtypeStruct((1, 128), f32), vec, vec, vec],
        compiler_params=_params())(h1, t2, b_down, g, b, target)


def ln1_bwd(x, t1, dr2, t3, g, tm=TM):
    S = x.shape[0]

    def body(x_ref, t_ref, dr2_ref, t3_ref, g_ref, dr_ref, drb_ref, dg_ref, db_ref):
        i = pl.program_id(0)
        xhat, rstd = _ln_stats(ALPHA * x_ref[...] + t_ref[...])
        dh = ALPHA * dr2_ref[...] + t3_ref[...]
        dr = _ln_bwd(dh, xhat, rstd, g_ref[...])
        dr_ref[...] = dr
        drb_ref[...] = dr.astype(bf16)

        @pl.when(i == 0)
        def _():
            dg_ref[...] = jnp.zeros_like(dg_ref)
            db_ref[...] = jnp.zeros_like(db_ref)

        dg_ref[...] += jnp.sum(dh * xhat, axis=0, keepdims=True)
        db_ref[...] += jnp.sum(dh, axis=0, keepdims=True)

    vec = jax.ShapeDtypeStruct((1, D_MODEL), f32)
    return pl.pallas_call(
        body, name="ln1_bwd", grid=(S // tm,),
        in_specs=[_rows(tm, D_MODEL)] * 4 + [_full((1, D_MODEL))],
        out_specs=[_rows(tm, D_MODEL), _rows(tm, D_MODEL), _full((1, D_MODEL)), _full((1, D_MODEL))],
        out_shape=[jax.ShapeDtypeStruct((S, D_MODEL), f32), jax.ShapeDtypeStruct((S, D_MODEL), bf16), vec, vec],
        compiler_params=_params())(x, t1, dr2, t3, g)


HBM_SPEC = pl.BlockSpec(memory_space=pl.ANY)


def _xy_peers(x, y):
    return [(1 - x, y), (x, 1 - y), (1 - x, 1 - y)]


def gather_weights(shard, name):
    def body(in_ref, out_ref, send_sems, recv_sems, local_sem):
        x, y, c = lax.axis_index("x"), lax.axis_index("y"), lax.axis_index("c")
        me = 2 * x + y
        mine = pltpu.make_async_copy(in_ref, out_ref.at[me], local_sem)
        mine.start()
        sends = []
        for k, (px, py) in enumerate(_xy_peers(x, y)):
            cp = pltpu.make_async_remote_copy(src_ref=in_ref, dst_ref=out_ref.at[me], send_sem=send_sems.at[k],
                                              recv_sem=recv_sems.at[k], device_id=(px, py, c), device_id_type=MESH)
            cp.start()
            sends.append(cp)
        for k, (px, py) in enumerate(_xy_peers(x, y)):
            pltpu.make_async_remote_copy(src_ref=in_ref, dst_ref=out_ref.at[2 * px + py], send_sem=send_sems.at[k],
                                         recv_sem=recv_sems.at[k], device_id=(px, py, c),
                                         device_id_type=MESH).wait_recv()
        for cp in sends:
            cp.wait_send()
        mine.wait()

    return pl.pallas_call(
        body, name=name, in_specs=[HBM_SPEC], out_specs=HBM_SPEC,
        out_shape=jax.ShapeDtypeStruct((4,) + shard.shape, shard.dtype),
        scratch_shapes=[pltpu.SemaphoreType.DMA((3,)), pltpu.SemaphoreType.DMA((3,)), pltpu.SemaphoreType.DMA(())],
    )(shard)


def exchange_grads(g_all, small):
    def body(g_ref, s_ref, recv_ref, srecv_ref, send_sems, recv_sems, ssend_sems, srecv_sems, local_sems):
        x, y, c = lax.axis_index("x"), lax.axis_index("y"), lax.axis_index("c")
        me = 2 * x + y
        me8 = 4 * x + 2 * y + c
        own = pltpu.make_async_copy(g_ref.at[me], recv_ref.at[me], local_sems.at[0])
        own_s = pltpu.make_async_copy(s_ref, srecv_ref.at[me8], local_sems.at[1])
        own.start()
        own_s.start()
        sends = []
        for k, (px, py) in enumerate(_xy_peers(x, y)):
            cp = pltpu.make_async_remote_copy(src_ref=g_ref.at[2 * px + py], dst_ref=recv_ref.at[me],
                                              send_sem=send_sems.at[k], recv_sem=recv_sems.at[k],
                                              device_id=(px, py, c), device_id_type=MESH)
            cp.start()
            sends.append(cp)
        flips = [(fx, fy, fc) for fx in (0, 1) for fy in (0, 1) for fc in (0, 1)][1:]
        for k, (fx, fy, fc) in enumerate(flips):
            cp = pltpu.make_async_remote_copy(src_ref=s_ref, dst_ref=srecv_ref.at[me8], send_sem=ssend_sems.at[k],
                                              recv_sem=srecv_sems.at[k],
                                              device_id=(x ^ fx, y ^ fy, c ^ fc), device_id_type=MESH)
            cp.start()
            sends.append(cp)
        for k, (px, py) in enumerate(_xy_peers(x, y)):
            pltpu.make_async_remote_copy(src_ref=g_ref.at[me], dst_ref=recv_ref.at[2 * px + py],
                                         send_sem=send_sems.at[k], recv_sem=recv_sems.at[k],
                                         device_id=(px, py, c), device_id_type=MESH).wait_recv()
        for k, (fx, fy, fc) in enumerate(flips):
            src8 = 4 * (x ^ fx) + 2 * (y ^ fy) + (c ^ fc)
            pltpu.make_async_remote_copy(src_ref=s_ref, dst_ref=srecv_ref.at[src8], send_sem=ssend_sems.at[k],
                                         recv_sem=srecv_sems.at[k], device_id=(x ^ fx, y ^ fy, c ^ fc),
                                         device_id_type=MESH).wait_recv()
        for cp in sends:
            cp.wait_send()
        own.wait()
        own_s.wait()

    return pl.pallas_call(
        body, name="exchange_grads", in_specs=[HBM_SPEC, HBM_SPEC], out_specs=[HBM_SPEC, HBM_SPEC],
        out_shape=[jax.ShapeDtypeStruct(g_all.shape, g_all.dtype),
                   jax.ShapeDtypeStruct((8,) + small.shape, small.dtype)],
        scratch_shapes=[pltpu.SemaphoreType.DMA((3,)), pltpu.SemaphoreType.DMA((3,)), pltpu.SemaphoreType.DMA((7,)),
                        pltpu.SemaphoreType.DMA((7,)), pltpu.SemaphoreType.DMA((2,))],
    )(g_all, small)


def swap_sibling(p):
    def body(p_ref, q_ref, send_sem, recv_sem):
        x, y, c = lax.axis_index("x"), lax.axis_index("y"), lax.axis_index("c")
        cp = pltpu.make_async_remote_copy(src_ref=p_ref, dst_ref=q_ref, send_sem=send_sem, recv_sem=recv_sem,
                                          device_id=(x, y, 1 - c), device_id_type=MESH)
        cp.start()
        cp.wait()

    return pl.pallas_call(
        body, name="swap_sibling", in_specs=[HBM_SPEC], out_specs=HBM_SPEC,
        out_shape=jax.ShapeDtypeStruct(p.shape, p.dtype),
        scratch_shapes=[pltpu.SemaphoreType.DMA(()), pltpu.SemaphoreType.DMA(())],
    )(p)


def sum_slots(recv, tm=560):
    n, R, C = recv.shape

    def body(r_ref, o_ref):
        acc = r_ref[0].astype(f32)
        for k in range(1, n):
            acc = acc + r_ref[k].astype(f32)
        o_ref[...] = acc

    return pl.pallas_call(
        body, name="sum_slots", grid=(R // tm,), in_specs=[pl.BlockSpec((n, tm, C), lambda i: (0, i, 0))],
        out_specs=_rows(tm, C), out_shape=jax.ShapeDtypeStruct((R, C), f32), compiler_params=_params())(recv)


def _adamw(w, g, m, v):
    m = ADAM_B1 * m + (1.0 - ADAM_B1) * g
    v = ADAM_B2 * v + (1.0 - ADAM_B2) * (g * g)
    m_hat = m / (1.0 - ADAM_B1 ** ADAM_STEP)
    v_hat = v / (1.0 - ADAM_B2 ** ADAM_STEP)
    delta = -ADAM_LR * (m_hat / (jnp.sqrt(v_hat) + ADAM_EPS) + ADAM_WD * w)
    return delta, m, v


def adam_pair(p, q, w, m, v, tm=280):
    R, C = w.shape

    def body(p_ref, q_ref, w_ref, m_ref, v_ref, g_ref, d_ref, nm_ref, nv_ref):
        g = p_ref[...] + q_ref[...]
        g_ref[...] = g
        d_ref[...], nm_ref[...], nv_ref[...] = _adamw(w_ref[...], g, m_ref[...], v_ref[...])

    o = jax.ShapeDtypeStruct((R, C), f32)
    return pl.pallas_call(body, name="adam_pair", grid=(R // tm,), in_specs=[_rows(tm, C)] * 5,
                          out_specs=[_rows(tm, C)] * 4, out_shape=[o] * 4, compiler_params=_params())(p, q, w, m, v)


def adam_slots(recv, w, m, v, name):
    n = recv.shape[0]

    def body(r_ref, w_ref, m_ref, v_ref, g_ref, d_ref, nm_ref, nv_ref):
        g = r_ref[0]
        for k in range(1, n):
            g = g + r_ref[k]
        g_ref[...] = g
        d_ref[...], nm_ref[...], nv_ref[...] = _adamw(w_ref[...], g, m_ref[...], v_ref[...])

    o = jax.ShapeDtypeStruct(w.shape, f32)
    return pl.pallas_call(body, name=name, out_shape=[o] * 4, compiler_params=_params())(recv, w, m, v)


BIG = ("w_in", "w_conv_out", "w_glu", "w_kv", "w_xattn_out", "w_out", "w_up", "w_down")
SMALL = ("b_gate", "ssm_lam_re", "ssm_lam_im", "ssm_log_dt", "ssm_b_re", "ssm_b_im", "ssm_c_re", "ssm_c_im", "ssm_d",
         "ln1_g", "ln1_b", "b_up", "b_down", "ln2_g", "ln2_b")
PACK_COLS = 1024


def _pack_rows(arrs):
    return jnp.concatenate([a.reshape(-1, PACK_COLS) for a in arrs], axis=0)


def _pad_flat(a, mult=1024):
    a = a.reshape(-1)
    return jnp.pad(a, (0, (-a.shape[0]) % mult))


def _perm(a):
    S, W = a.shape
    return a.reshape(N_SEG, S // N_SEG, W).transpose(1, 0, 2).reshape(S, W)


def _unperm(a):
    S, W = a.shape
    return a.reshape(S // N_SEG, N_SEG, W).transpose(1, 0, 2).reshape(S, W)


def _state_rows(a):
    return a.transpose(2, 0, 1).reshape(SSM_GROUP, N_STATES)


def _block_diag_b(bt):
    b4 = bt.reshape(SSM_GROUP, 4, 8, SSM_STATE)
    eye = jnp.eye(8, dtype=bt.dtype)
    return jnp.einsum("hcgp,gk->cghkp", b4, eye).reshape(4, 128, 512)


def _block_diag_c(c):
    c4 = c.reshape(4, 8, SSM_GROUP, SSM_STATE)
    eye = jnp.eye(8, dtype=c.dtype)
    return jnp.einsum("cghp,gk->cgpkh", c4, eye).reshape(4, 512, 128)


def _diag_b(acc):
    a = acc.reshape(4, 8, SSM_STATE, 8, SSM_GROUP)
    eye = jnp.eye(8, dtype=acc.dtype)
    return jnp.einsum("cgpkh,gk->hcgp", a, eye).reshape(SSM_GROUP, N_STATES)


def _diag_c(acc):
    a = acc.reshape(4, 8, SSM_GROUP, 8, SSM_STATE)
    eye = jnp.eye(8, dtype=acc.dtype)
    return jnp.einsum("cghkp,gk->cghp", a, eye).reshape(SSM_GROUPS, SSM_GROUP, SSM_STATE)


def kernel(x, mem, w_in, b_gate, conv_w, w_conv_out, ssm_lam_re, ssm_lam_im, ssm_log_dt, ssm_b_re, ssm_b_im, ssm_c_re, ssm_c_im, ssm_d, w_glu, w_kv, w_xattn_out, w_out, ln1_g, ln1_b, w_up, b_up, w_down, b_down, ln2_g, ln2_b, loss_target, m_w_in, m_b_gate, m_conv_w, m_w_conv_out, m_ssm_lam_re, m_ssm_lam_im, m_ssm_log_dt, m_ssm_b_re, m_ssm_b_im, m_ssm_c_re, m_ssm_c_im, m_ssm_d, m_w_glu, m_w_kv, m_w_xattn_out, m_w_out, m_ln1_g, m_ln1_b, m_w_up, m_b_up, m_w_down, m_b_down, m_ln2_g, m_ln2_b, v_w_in, v_b_gate, v_conv_w, v_w_conv_out, v_ssm_lam_re, v_ssm_lam_im, v_ssm_log_dt, v_ssm_b_re, v_ssm_b_im, v_ssm_c_re, v_ssm_c_im, v_ssm_d, v_w_glu, v_w_kv, v_w_xattn_out, v_w_out, v_ln1_g, v_ln1_b, v_w_up, v_b_up, v_w_down, v_b_down, v_ln2_g, v_ln2_b):
    W = dict(w_in=w_in, b_gate=b_gate, conv_w=conv_w, w_conv_out=w_conv_out, ssm_lam_re=ssm_lam_re,
             ssm_lam_im=ssm_lam_im, ssm_log_dt=ssm_log_dt, ssm_b_re=ssm_b_re, ssm_b_im=ssm_b_im, ssm_c_re=ssm_c_re,
             ssm_c_im=ssm_c_im, ssm_d=ssm_d, w_glu=w_glu, w_kv=w_kv, w_xattn_out=w_xattn_out, w_out=w_out,
             ln1_g=ln1_g, ln1_b=ln1_b, w_up=w_up, b_up=b_up, w_down=w_down, b_down=b_down, ln2_g=ln2_g, ln2_b=ln2_b)
    MOM = dict(w_in=m_w_in, b_gate=m_b_gate, conv_w=m_conv_w, w_conv_out=m_w_conv_out, ssm_lam_re=m_ssm_lam_re,
               ssm_lam_im=m_ssm_lam_im, ssm_log_dt=m_ssm_log_dt, ssm_b_re=m_ssm_b_re, ssm_b_im=m_ssm_b_im,
               ssm_c_re=m_ssm_c_re, ssm_c_im=m_ssm_c_im, ssm_d=m_ssm_d, w_glu=m_w_glu, w_kv=m_w_kv,
               w_xattn_out=m_w_xattn_out, w_out=m_w_out, ln1_g=m_ln1_g, ln1_b=m_ln1_b, w_up=m_w_up, b_up=m_b_up,
               w_down=m_w_down, b_down=m_b_down, ln2_g=m_ln2_g, ln2_b=m_ln2_b)
    VEL = dict(w_in=v_w_in, b_gate=v_b_gate, conv_w=v_conv_w, w_conv_out=v_w_conv_out, ssm_lam_re=v_ssm_lam_re,
               ssm_lam_im=v_ssm_lam_im, ssm_log_dt=v_ssm_log_dt, ssm_b_re=v_ssm_b_re, ssm_b_im=v_ssm_b_im,
               ssm_c_re=v_ssm_c_re, ssm_c_im=v_ssm_c_im, ssm_d=v_ssm_d, w_glu=v_w_glu, w_kv=v_w_kv,
               w_xattn_out=v_w_xattn_out, w_out=v_w_out, ln1_g=v_ln1_g, ln1_b=v_ln1_b, w_up=v_w_up, b_up=v_b_up,
               w_down=v_w_down, b_down=v_b_down, ln2_g=v_ln2_g, ln2_b=v_ln2_b)
    names = list(W)
    xy = 2 * lax.axis_index("x") + lax.axis_index("y")

    xs = x[0]
    S = xs.shape[0]
    mems = mem[0]
    tgt = loss_target[0]

    shard_shapes = {n: W[n].shape[1:] for n in BIG}
    gathered = gather_weights(_pack_rows([W[n][0].astype(bf16) for n in BIG]), "gather_weights")
    G3 = {}
    row = 0
    for n in BIG:
        k, c = shard_shapes[n]
        nr = k * c // PACK_COLS
        G3[n] = gathered[:, row:row + nr, :].reshape(4, k, c)
        row += nr
    rows_big = row
    win3, wco3, wglu3, wxo3, wup3 = G3["w_in"], G3["w_conv_out"], G3["w_glu"], G3["w_xattn_out"], G3["w_up"]
    wkv3 = G3["w_kv"].reshape(1, D_MODEL, 2 * XATTN_WIDTH)
    wout3 = G3["w_out"].reshape(1, D_MODEL, D_MODEL)
    wdn3 = G3["w_down"].reshape(1, D_FF, D_MODEL)

    convw_full = gather_weights(conv_w[0], "gather_conv_w").transpose(1, 0, 2).reshape(3, CONV_WIDTH)

    lr = ssm_lam_re.reshape(1, N_STATES)
    li = ssm_lam_im.reshape(1, N_STATES)
    ldt = jnp.repeat(ssm_log_dt.reshape(SSM_GROUPS), SSM_STATE).reshape(1, N_STATES)
    bt_re, bt_im = _state_rows(ssm_b_re[0]), _state_rows(ssm_b_im[0])
    ar, ai, kr, ki, bbt_r, bbt_i = ssm_prep(lr, li, ldt, bt_re, bt_im)
    bd_r, bd_i = _block_diag_b(bbt_r).astype(bf16), _block_diag_b(bbt_i).astype(bf16)
    cd_r, cd_i = _block_diag_c(ssm_c_re[0]).astype(bf16), _block_diag_c(ssm_c_im[0]).astype(bf16)
    bdt_r, bdt_i = bd_r.transpose(0, 2, 1), bd_i.transpose(0, 2, 1)
    cdt_r, cdt_i = cd_r.transpose(0, 2, 1), cd_i.transpose(0, 2, 1)
    d_skip = ssm_d.reshape(1, SSM_WIDTH)

    x_bf = xs.astype(bf16)
    mem_bf = mems.astype(bf16)
    u_col = GATE_COLS + 3 * CONV_WIDTH
    proj = mm_fwd(x_bf, win3, "proj")
    a_conv = conv_fwd(proj, convw_full)
    y_a = mm_fwd(a_conv, wco3, "conv_out")
    u_perm = _perm(proj[:, u_col:u_col + SSM_WIDTH])
    init_r, init_i = ssm_scan_fwd(u_perm, bd_r, bd_i, ar, ai)
    ysm_perm, ys_perm, st_r, st_i = ssm_scan_fwd(u_perm, bd_r, bd_i, ar, ai,
                                                 final=(cd_r, cd_i, d_skip, init_r, init_i))
    y_s = _unperm(ys_perm)
    glu = mm_fwd(y_s, wglu3, "glu")
    kv = mm_fwd(mem_bf, wkv3, "kv", outs=(bf16,))
    o_att = attn_fwd(proj, kv)
    y_c = mm_fwd(o_att, wxo3, "xattn_out")
    merged = merge_fwd(proj, b_gate, y_a, glu, y_c)
    t1 = mm_fwd(merged, wout3, "w_out")
    h1, h1_bf = ln1_fwd(xs, t1, ln1_g, ln1_b)
    up, hdn = mm_fwd(h1_bf, wup3, "w_up", bias=b_up, outs=(f32, bf16), second=_relu2)
    t2 = mm_fwd(hdn, wdn3, "w_down", tn=512)

    dr2, dr2_bf, loss_part, d_ln2_g, d_ln2_b, d_b_down = ln2_loss_bwd(h1, t2, b_down, ln2_g, ln2_b, tgt)
    g_w_down = mm_bwd_w(hdn, dr2_bf, 1, "dw_down", tk=1024)
    dup, d_b_up = mm_bwd_x(dr2_bf, wdn3, "dup", tm=256, extras=(up,), colsum=True, out_dtype=bf16,
                           epi=lambda acc, upv: acc * (2.0 * jnp.maximum(upv, 0.0)))
    g_w_up = mm_bwd_w(h1_bf, dup, 4, "dw_up")
    t3 = mm_bwd_x(dup, wup3, "dh1")
    dr1, dr1_bf, d_ln1_g, d_ln1_b = ln1_bwd(xs, t1, dr2, t3, ln1_g)
    g_w_out = mm_bwd_w(merged, dr1_bf, 1, "dw_out")
    dmerged = mm_bwd_x(dr1_bf, wout3, "dmerged", tm=1024)
    dproj, dy_a, dglu, dy_c, d_b_gate = merge_bwd(dmerged, proj, b_gate, y_a, glu, y_c)

    g_w_co = mm_bwd_w(a_conv, dy_a, 4, "dw_conv_out")
    da_conv = mm_bwd_x(dy_a, wco3, "da_conv", tm=1024)
    dproj, d_conv_w = conv_bwd(da_conv, proj, convw_full, dproj)

    g_w_glu = mm_bwd_w(y_s, dglu, 4, "dw_glu")
    dys_perm = _perm(mm_bwd_x(dglu, wglu3, "dy_s", tm=1024))
    linit_r, linit_i = ssm_scan_bwd(dys_perm, ysm_perm, cdt_r, cdt_i, ar, ai)
    du_perm, dbacc_r, dbacc_i, dcacc_r, dcacc_i, da_r, da_i, d_ssm_d = ssm_scan_bwd(
        dys_perm, ysm_perm, cdt_r, cdt_i, ar, ai,
        final=(u_perm, st_r, st_i, bdt_r, bdt_i, d_skip, linit_r, linit_i))
    dbt_re, dbt_im, d_lr, d_li, d_ldt_state = ssm_param_bwd(
        _diag_b(dbacc_r), _diag_b(dbacc_i), bt_re, bt_im, kr, ki, ar, ai, lr, li, ldt, da_r, da_i)
    d_log_dt = group_sum(d_ldt_state.reshape(SSM_GROUPS, SSM_STATE))
    d_b_re = dbt_re.reshape(SSM_GROUP, SSM_GROUPS, SSM_STATE).transpose(1, 2, 0)
    d_b_im = dbt_im.reshape(SSM_GROUP, SSM_GROUPS, SSM_STATE).transpose(1, 2, 0)
    d_c_re = _diag_c(dcacc_r)
    d_c_im = -_diag_c(dcacc_i)

    g_w_xo = mm_bwd_w(o_att, dy_c, 4, "dw_xattn_out")
    do_att = mm_bwd_x(dy_c, wxo3, "do_att", tm=1024, out_dtype=bf16)
    dproj, dkv = attn_bwd(do_att, proj, kv, dproj)
    g_w_kv = mm_bwd_w(mem_bf, dkv, 1, "dw_kv")

    dproj = lax.dynamic_update_slice(dproj, _unperm(du_perm), (0, u_col))
    g_w_in = mm_bwd_w(x_bf, dproj, 4, "dw_in")
    dx = mm_bwd_x(dproj, win3, "dx", extras=(dr1,), epi=lambda acc, d: acc + ALPHA * d)

    g_big = {"w_in": g_w_in, "w_conv_out": g_w_co, "w_glu": g_w_glu, "w_kv": g_w_kv.reshape(4, 256, D_MODEL),
             "w_xattn_out": g_w_xo, "w_out": g_w_out.reshape(4, 256, D_MODEL), "w_up": g_w_up,
             "w_down": g_w_down.reshape(4, D_MODEL, D_MODEL)}
    g_all = jnp.concatenate([g_big[n].reshape(4, -1, PACK_COLS) for n in BIG], axis=1)
    g_small = {"b_gate": d_b_gate, "ssm_lam_re": d_lr, "ssm_lam_im": d_li, "ssm_log_dt": d_log_dt, "ssm_b_re": d_b_re,
               "ssm_b_im": d_b_im, "ssm_c_re": d_c_re, "ssm_c_im": d_c_im, "ssm_d": d_ssm_d, "ln1_g": d_ln1_g,
               "ln1_b": d_ln1_b, "b_up": d_b_up, "b_down": d_b_down, "ln2_g": d_ln2_g, "ln2_b": d_ln2_b}
    small_names = SMALL + ("conv_w",)
    g_small["conv_w"] = d_conv_w
    sizes = {n: (g_small[n].size + 1023) // 1024 * 1024 for n in small_names}
    pack = lambda d: jnp.concatenate([_pad_flat(d[n]) for n in small_names]).reshape(-1, 128)
    conv_zero = jnp.zeros((3, CONV_WIDTH), f32)
    recv, srecv = exchange_grads(g_all, pack(g_small))
    part = sum_slots(recv)
    other = swap_sibling(part)
    big_w = _pack_rows([W[n][0] for n in BIG])
    big_m = _pack_rows([MOM[n][0] for n in BIG])
    big_v = _pack_rows([VEL[n][0] for n in BIG])
    gb, db, mb, vb = adam_pair(part, other, big_w, big_m, big_v)
    gs, ds_, ms, vs = adam_slots(srecv, pack({**{n: W[n] for n in SMALL}, "conv_w": conv_zero}),
                                 pack({**{n: MOM[n] for n in SMALL}, "conv_w": conv_zero}),
                                 pack({**{n: VEL[n] for n in SMALL}, "conv_w": conv_zero}), "adam_small")

    def unpack_big(buf):
        out, r = {}, 0
        for n in BIG:
            k, c = shard_shapes[n]
            nr = k * c // PACK_COLS
            out[n] = buf[r:r + nr].reshape(1, k, c)
            r += nr
        return out

    def unpack_small(buf):
        flat = buf.reshape(-1)
        out, r = {}, 0
        for n in small_names:
            ref = g_small[n] if n == "conv_w" else W[n]
            out[n] = flat[r:r + ref.size].reshape(ref.shape)
            r += sizes[n]
        return out

    res = [unpack_big(b) for b in (gb, db, mb, vb)]
    res_s = [unpack_small(b) for b in (gs, ds_, ms, vs)]
    g_conv = lax.dynamic_slice(res_s[0]["conv_w"], (0, xy * 128), (3, 128))
    conv_slots = g_conv.reshape(1, 3, 128)
    cg, cd, cm, cv = adam_slots(conv_slots, conv_w[0], m_conv_w[0], v_conv_w[0], "adam_conv")
    conv_res = [cg, cd, cm, cv]

    loss = lax.psum(loss_part[0, 0], ("x", "y", "c"))
    outs = [loss, dx.reshape(x.shape)]
    for k in range(4):
        for n in names:
            if n == "conv_w":
                outs.append(conv_res[k].reshape(conv_w.shape))
            elif n in BIG:
                outs.append(res[k][n])
            else:
                outs.append(res_s[k][n])
    return tuple(outs)
```

```python
import functools
import math

import jax
import jax.numpy as jnp
from jax import lax
from jax.experimental import pallas as pl
from jax.experimental.pallas import tpu as pltpu

f32 = jnp.float32
bf16 = jnp.bfloat16

D_MODEL = 1024
CONV_WIDTH = 512
SSM_WIDTH = 512
SSM_GROUP = 16
SSM_GROUPS = 32
SSM_STATE = 64
N_STATES = SSM_GROUPS * SSM_STATE
XATTN_HEADS = 4
XATTN_HEAD_DIM = 128
XATTN_WIDTH = 512
D_FF = 4096
GATE_COLS = 3 * D_MODEL
IN_COLS = GATE_COLS + 3 * CONV_WIDTH + SSM_WIDTH + XATTN_WIDTH
ALPHA = 2.0 ** 0.25
LN_EPS = 1e-5
ADAM_LR = 0.001
ADAM_B1 = 0.9
ADAM_B2 = 0.999
ADAM_EPS = 1e-08
ADAM_WD = 0.01
ADAM_STEP = 10

N_SEG = 8
SCAN_ROWS = 256
LANE_STRIP = 512
TM = 512
VMEM_LIMIT = 48 * 1024 * 1024
MESH = pl.DeviceIdType.MESH

NT_DIMS = (((1,), (1,)), ((), ()))
TN_DIMS = (((0,), (0,)), ((), ()))


def _params():
    return pltpu.CompilerParams(vmem_limit_bytes=VMEM_LIMIT)


def _full(shape):
    n = len(shape)
    return pl.BlockSpec(shape, lambda *_: (0,) * n)


def _rows(tm, w, cb=0):
    return pl.BlockSpec((tm, w), lambda i: (i, cb))


def _sig(x):
    return 1.0 / (1.0 + jnp.exp(-x))


HBM_SPEC = pl.BlockSpec(memory_space=pl.ANY)
RIDE_PEERS = {"gather": 3, "scatter": 3, "all": 7}


def _xy_peers(x, y):
    return [(1 - x, y), (x, 1 - y), (1 - x, 1 - y)]


def _ride_copies(kind, src, dst, send_sems, recv_sems, local_sem):
    x, y, c = lax.axis_index("x"), lax.axis_index("y"), lax.axis_index("c")
    me = 2 * x + y
    if kind == "all":
        flips = [(fx, fy, fc) for fx in (0, 1) for fy in (0, 1) for fc in (0, 1)][1:]
        peers = [(x ^ fx, y ^ fy, c ^ fc) for fx, fy, fc in flips]
        slot = lambda p: 4 * p[0] + 2 * p[1] + p[2]
        mine = slot((x, y, c))
    else:
        peers = [(px, py, c) for px, py in _xy_peers(x, y)]
        slot = lambda p: 2 * p[0] + p[1]
        mine = me

    def remote(k, s, d):
        return pltpu.make_async_remote_copy(src_ref=s, dst_ref=d, send_sem=send_sems.at[k], recv_sem=recv_sems.at[k],
                                            device_id=peers[k], device_id_type=MESH)

    if kind == "scatter":
        local = pltpu.make_async_copy(src.at[me], dst.at[me], local_sem)
        sends = [remote(k, src.at[slot(p)], dst.at[me]) for k, p in enumerate(peers)]
        lands = [remote(k, src.at[me], dst.at[slot(p)]) for k, p in enumerate(peers)]
    else:
        local = pltpu.make_async_copy(src, dst.at[mine], local_sem)
        sends = [remote(k, src, dst.at[mine]) for k, p in enumerate(peers)]
        lands = [remote(k, src, dst.at[slot(p)]) for k, p in enumerate(peers)]
    return local, sends, lands


def _ride_shape(kind, src):
    lead = {"gather": (4,), "scatter": (), "all": (8,)}[kind]
    return jax.ShapeDtypeStruct(lead + src.shape, src.dtype)


def _pcall(body, args, *, name, grid, in_specs, out_specs, out_shape, scratch_shapes=(), rides=(), aliases=None):
    n_in, n_out, n_scr, nr = len(in_specs), len(out_specs), len(scratch_shapes), len(rides)
    kinds = [k for k, _ in rides]

    def wrapped(*refs):
        ins, rsrc = refs[:n_in], refs[n_in:n_in + nr]
        outs = refs[n_in + nr:n_in + nr + n_out]
        rdst = refs[n_in + nr + n_out:n_in + 2 * nr + n_out]
        scr = refs[n_in + 2 * nr + n_out:n_in + 2 * nr + n_out + n_scr]
        sems = refs[n_in + 2 * nr + n_out + n_scr:]
        def start():
            for r, kind in enumerate(kinds):
                local, sends, _ = _ride_copies(kind, rsrc[r], rdst[r], *sems[3 * r:3 * r + 3])
                local.start()
                for cp in sends:
                    cp.start()

        def finish():
            for r, kind in enumerate(kinds):
                local, sends, lands = _ride_copies(kind, rsrc[r], rdst[r], *sems[3 * r:3 * r + 3])
                for cp in lands:
                    cp.wait_recv()
                for cp in sends:
                    cp.wait_send()
                local.wait()

        if nr and grid:
            ids = [pl.program_id(a) for a in range(len(grid))]
            first = functools.reduce(jnp.logical_and, [i == 0 for i in ids])
            last = functools.reduce(jnp.logical_and, [i == g - 1 for i, g in zip(ids, grid)])
            pl.when(first)(start)
        elif nr:
            start()
        body(*ins, *outs, *scr)
        if nr and grid:
            pl.when(last)(finish)
        elif nr:
            finish()

    sems = []
    for kind in kinds:
        n = RIDE_PEERS[kind]
        sems += [pltpu.SemaphoreType.DMA((n,)), pltpu.SemaphoreType.DMA((n,)), pltpu.SemaphoreType.DMA(())]
    res = pl.pallas_call(
        wrapped, name=name, grid=grid, in_specs=list(in_specs) + [HBM_SPEC] * nr,
        out_specs=list(out_specs) + [HBM_SPEC] * nr,
        out_shape=list(out_shape) + [_ride_shape(k, s) for k, s in rides],
        scratch_shapes=list(scratch_shapes) + sems, input_output_aliases=aliases or {},
        compiler_params=_params())(*args, *[s for _, s in rides])
    return list(res[:n_out]), list(res[n_out:])


def _bf(v):
    return v if v.dtype == bf16 else v.astype(bf16)


def mm_fwd(a, w3, name, *, tm=1024, tn=None, bias=None, outs=(f32,), second=None, rides=()):
    M, K = a.shape
    J, _, n = w3.shape
    tm = min(tm, M)
    tn = tn or n
    nl = n // tn
    nb = 0 if bias is None else 1

    def body(*refs):
        a_ref, w_ref = refs[0], refs[1]
        acc = jnp.dot(_bf(a_ref[...]), w_ref[0], preferred_element_type=f32)
        if bias is not None:
            acc = acc + refs[2][...]
        refs[2 + nb][...] = acc.astype(outs[0])
        if len(outs) > 1:
            refs[3 + nb][...] = second(acc).astype(outs[1])

    in_specs = [pl.BlockSpec((tm, K), lambda j, l, i: (i, 0)),
                pl.BlockSpec((1, K, tn), lambda j, l, i: (j, 0, l))]
    args = [a, w3]
    if bias is not None:
        in_specs.append(pl.BlockSpec((1, tn), lambda j, l, i: (0, j * nl + l)))
        args.append(bias)
    res, landed = _pcall(
        body, args, name=name, grid=(J, nl, M // tm), in_specs=in_specs,
        out_specs=[pl.BlockSpec((tm, tn), lambda j, l, i: (i, j * nl + l)) for _ in outs],
        out_shape=[jax.ShapeDtypeStruct((M, J * n), dt) for dt in outs], rides=rides)
    res = res if len(outs) > 1 else res[0]
    return (res, landed) if rides else res


def mm_bwd_x(dy, w3, name, *, tm=TM, epi=None, extras=(), colsum=False, out_dtype=f32, rides=()):
    M = dy.shape[0]
    J, K, n = w3.shape
    tm = min(tm, M)
    nex = len(extras)

    def body(*refs):
        dy_ref, w_hbm = refs[0], refs[1]
        ex = refs[2:2 + nex]
        o_ref = refs[2 + nex]
        w_ref = refs[-1]
        i = pl.program_id(0)

        @pl.when(i == 0)
        def _():
            pltpu.sync_copy(w_hbm, w_ref)

        acc = None
        for j in range(J):
            part = lax.dot_general(_bf(dy_ref[:, j * n:(j + 1) * n]), w_ref[j], NT_DIMS, preferred_element_type=f32)
            acc = part if acc is None else acc + part
        if epi is not None:
            acc = epi(acc, *[e[...] for e in ex])
        o_ref[...] = acc.astype(out_dtype)
        if colsum:
            s_ref = refs[3 + nex]
            cs = jnp.sum(acc, axis=0, keepdims=True)

            @pl.when(i == 0)
            def _():
                s_ref[...] = cs

            @pl.when(i > 0)
            def _():
                s_ref[...] += cs

    in_specs = [pl.BlockSpec((tm, J * n), lambda i: (i, 0)), pl.BlockSpec(memory_space=pl.ANY)]
    in_specs += [pl.BlockSpec((tm, K), lambda i: (i, 0)) for _ in extras]
    out_specs = [pl.BlockSpec((tm, K), lambda i: (i, 0))]
    out_shape = [jax.ShapeDtypeStruct((M, K), out_dtype)]
    if colsum:
        out_specs.append(pl.BlockSpec((1, K), lambda i: (0, 0)))
        out_shape.append(jax.ShapeDtypeStruct((1, K), f32))
    res, landed = _pcall(body, [dy, w3, *extras], name=name, grid=(M // tm,), in_specs=in_specs, out_specs=out_specs,
                         out_shape=out_shape, scratch_shapes=[pltpu.VMEM((J, K, n), bf16)], rides=rides)
    res = res if colsum else res[0]
    return (res, landed) if rides else res


def mm_bwd_w(a, dy, J, name, *, tm=2048, tn=None, tk=None, rides=()):
    M, K = a.shape
    n = dy.shape[1] // J
    tm = min(tm, M)
    tn = tn or n
    tk = tk or K
    nl = n // tn
    nk = K // tk
    ns = M // tm

    def body(a_ref, dy_ref, o_ref, acc_ref):
        s = pl.program_id(3)
        part = lax.dot_general(_bf(a_ref[...]), _bf(dy_ref[...]), TN_DIMS, preferred_element_type=f32)

        @pl.when(s == 0)
        def _():
            acc_ref[...] = part

        @pl.when(s > 0)
        def _():
            acc_ref[...] += part

        @pl.when(s == ns - 1)
        def _():
            o_ref[0] = acc_ref[...].astype(bf16)

    res, landed = _pcall(
        body, [a, dy], name=name, grid=(J, nl, nk, ns),
        in_specs=[pl.BlockSpec((tm, tk), lambda j, l, k, s: (s, k)),
                  pl.BlockSpec((tm, tn), lambda j, l, k, s: (s, j * nl + l))],
        out_specs=[pl.BlockSpec((1, tk, tn), lambda j, l, k, s: (j, k, l))],
        out_shape=[jax.ShapeDtypeStruct((J, K, n), bf16)],
        scratch_shapes=[pltpu.VMEM((tk, tn), f32)], rides=rides)
    return (res[0], landed) if rides else res[0]


def _relu2(v):
    r = jnp.maximum(v, 0.0)
    return r * r


def _shift_down(z, k, halo):
    r = lax.broadcasted_iota(jnp.int32, z.shape, 0)
    y = pltpu.roll(z, k, 0)
    for q in range(k):
        y = jnp.where(r == q, halo[8 - k + q:8 - k + q + 1, :], y)
    return y


def _shift_up(z, k, halo):
    tm = z.shape[0]
    r = lax.broadcasted_iota(jnp.int32, z.shape, 0)
    y = pltpu.roll(z, tm - k, 0)
    for q in range(k):
        y = jnp.where(r == tm - k + q, halo[q:q + 1, :], y)
    return y


def _prev_halo(tm, cb):
    return pl.BlockSpec((8, CONV_WIDTH), lambda i: (jnp.maximum(i * (tm // 8) - 1, 0), cb))


def _next_halo(tm, cb, nrow8):
    return pl.BlockSpec((8, CONV_WIDTH), lambda i: (jnp.minimum((i + 1) * (tm // 8), nrow8 - 1), cb))


def conv_fwd(proj, conv_w, tm=TM):
    S = proj.shape[0]

    def body(cb_ref, cc_ref, ch_ref, cch_ref, chh_ref, w_ref, a_ref):
        i = pl.program_id(0)
        z = cc_ref[...] * ch_ref[...]
        zh = jnp.where(i == 0, 0.0, cch_ref[...] * chh_ref[...])
        w = w_ref[...]
        dwz = w[0:1, :] * _shift_down(z, 2, zh) + w[1:2, :] * _shift_down(z, 1, zh) + w[2:3, :] * z
        a_ref[...] = (cb_ref[...] * dwz).astype(bf16)

    return pl.pallas_call(
        body, name="conv_fwd", grid=(S // tm,),
        in_specs=[_rows(tm, CONV_WIDTH, 6), _rows(tm, CONV_WIDTH, 7), _rows(tm, CONV_WIDTH, 8),
                  _prev_halo(tm, 7), _prev_halo(tm, 8), _full((3, CONV_WIDTH))],
        out_specs=_rows(tm, CONV_WIDTH),
        out_shape=jax.ShapeDtypeStruct((S, CONV_WIDTH), bf16), compiler_params=_params())(
            proj, proj, proj, proj, proj, conv_w)


def conv_bwd(da, proj, conv_w, dproj, tm=TM):
    S = proj.shape[0]
    nt = S // tm

    def body(da_ref, cb_ref, cc_ref, ch_ref, cch_ref, chh_ref, dan_ref, cbn_ref, w_ref, _, o_ref, dw_ref):
        i = pl.program_id(0)
        cc, ch, cb, dav = cc_ref[...], ch_ref[...], cb_ref[...], da_ref[...]
        z = cc * ch
        zh = jnp.where(i == 0, 0.0, cch_ref[...] * chh_ref[...])
        w = w_ref[...]
        z1 = _shift_down(z, 1, zh)
        z2 = _shift_down(z, 2, zh)
        dwz = w[0:1, :] * z2 + w[1:2, :] * z1 + w[2:3, :] * z
        g = dav * cb
        gn = jnp.where(i == nt - 1, 0.0, dan_ref[...] * cbn_ref[...])
        dz = w[2:3, :] * g + w[1:2, :] * _shift_up(g, 1, gn) + w[0:1, :] * _shift_up(g, 2, gn)
        o_ref[:, 0:CONV_WIDTH] = (dav * dwz).astype(bf16)
        o_ref[:, CONV_WIDTH:2 * CONV_WIDTH] = (dz * ch).astype(bf16)
        o_ref[:, 2 * CONV_WIDTH:3 * CONV_WIDTH] = (dz * cc).astype(bf16)

        @pl.when(i == 0)
        def _():
            dw_ref[...] = jnp.zeros_like(dw_ref)

        dw_ref[0:1, :] += jnp.sum(g * z2, axis=0, keepdims=True)
        dw_ref[1:2, :] += jnp.sum(g * z1, axis=0, keepdims=True)
        dw_ref[2:3, :] += jnp.sum(g * z, axis=0, keepdims=True)

    return pl.pallas_call(
        body, name="conv_bwd", grid=(nt,),
        in_specs=[_rows(tm, CONV_WIDTH), _rows(tm, CONV_WIDTH, 6), _rows(tm, CONV_WIDTH, 7), _rows(tm, CONV_WIDTH, 8),
                  _prev_halo(tm, 7), _prev_halo(tm, 8),
                  pl.BlockSpec((8, CONV_WIDTH), lambda i: (jnp.minimum((i + 1) * (tm // 8), S // 8 - 1), 0)),
                  _next_halo(tm, 6, S // 8), _full((3, CONV_WIDTH)), pl.BlockSpec(memory_space=pl.ANY)],
        out_specs=[_rows(tm, 3 * CONV_WIDTH, GATE_COLS // (3 * CONV_WIDTH)), _full((3, CONV_WIDTH))],
        out_shape=[jax.ShapeDtypeStruct(dproj.shape, bf16), jax.ShapeDtypeStruct((3, CONV_WIDTH), f32)],
        input_output_aliases={9: 0},
        compiler_params=_params())(da, proj, proj, proj, proj, proj, da, proj, conv_w, dproj)


def _cmul(ar, ai, br, bi):
    return ar * br - ai * bi, ar * bi + ai * br


def ssm_prep(lr, li, ldt, bt_re, bt_im):
    def body(lr_ref, li_ref, ldt_ref, br_ref, bi_ref, ar_ref, ai_ref, kr_ref, ki_ref, bbr_ref, bbi_ref):
        lrv, liv = lr_ref[...], li_ref[...]
        dt = jnp.exp(ldt_ref[...])
        mag = jnp.exp(lrv * dt)
        ar = mag * jnp.cos(liv * dt)
        ai = mag * jnp.sin(liv * dt)
        den = lrv * lrv + liv * liv
        nr = ar - 1.0
        kr = (nr * lrv + ai * liv) / den
        ki = (ai * lrv - nr * liv) / den
        ar_ref[...], ai_ref[...], kr_ref[...], ki_ref[...] = ar, ai, kr, ki
        bbr_ref[...] = kr * br_ref[...] - ki * bi_ref[...]
        bbi_ref[...] = kr * bi_ref[...] + ki * br_ref[...]

    v = jax.ShapeDtypeStruct((1, N_STATES), f32)
    m = jax.ShapeDtypeStruct((SSM_GROUP, N_STATES), f32)
    return pl.pallas_call(body, name="ssm_prep", out_shape=[v, v, v, v, m, m])(lr, li, ldt, bt_re, bt_im)


def _pow_segment(ar, ai, seg_len):
    pr, pi = ar, ai
    for _ in range(int(math.log2(seg_len))):
        pr, pi = _cmul(pr, pi, pr, pi)
    return pr, pi


GELU_K = math.sqrt(2.0 / math.pi)
GELU_C = 0.044715


def _gelu(v):
    return 0.5 * v * (1.0 + jnp.tanh(GELU_K * (v + GELU_C * v * v * v)))


def _gelu_grad(v):
    t = jnp.tanh(GELU_K * (v + GELU_C * v * v * v))
    return 0.5 * (1.0 + t) + 0.5 * v * (1.0 - t * t) * GELU_K * (1.0 + 3.0 * GELU_C * v * v)


def ssm_scan_fwd(u_perm, bd_r, bd_i, ar, ai, *, final=None, rides=()):
    S = u_perm.shape[0]
    R = SCAN_ROWS
    nblk = S // R
    seg_len = S // N_SEG
    nstrip = N_STATES // LANE_STRIP
    store = final is not None

    def body(*refs):
        if store:
            (u_ref, bdr_ref, bdi_ref, ar_ref, ai_ref, cdr_ref, cdi_ref, d_ref, ir_ref, ii_ref,
             y_ref, ys_ref, sr_ref, si_ref, bur, bui, car, cai) = refs
        else:
            u_ref, bdr_ref, bdi_ref, ar_ref, ai_ref, or_ref, oi_ref, bur, bui, car, cai = refs
        i = pl.program_id(0)
        u = u_ref[...]
        ub = u.astype(bf16)
        for cb in range(4):
            us = ub[:, cb * 128:(cb + 1) * 128]
            bur[:, cb * 512:(cb + 1) * 512] = jnp.dot(us, bdr_ref[cb], preferred_element_type=f32)
            bui[:, cb * 512:(cb + 1) * 512] = jnp.dot(us, bdi_ref[cb], preferred_element_type=f32)

        @pl.when(i == 0)
        def _():
            if store:
                car[...] = ir_ref[...]
                cai[...] = ii_ref[...]
            else:
                car[...] = jnp.zeros_like(car)
                cai[...] = jnp.zeros_like(cai)

        for ls in range(nstrip):
            lanes = pl.ds(ls * LANE_STRIP, LANE_STRIP)
            a_r = jnp.broadcast_to(ar_ref[:, lanes], (N_SEG, LANE_STRIP))
            a_i = jnp.broadcast_to(ai_ref[:, lanes], (N_SEG, LANE_STRIP))

            def step(t, carry, lanes=lanes, a_r=a_r, a_i=a_i):
                s_r, s_i = carry
                row = pl.multiple_of(t * 8, 8)
                n_r = a_r * s_r - a_i * s_i + bur[pl.ds(row, 8), lanes]
                n_i = a_r * s_i + a_i * s_r + bui[pl.ds(row, 8), lanes]
                if store:
                    sr_ref[pl.ds(row, 8), lanes] = n_r
                    si_ref[pl.ds(row, 8), lanes] = n_i
                return n_r, n_i

            e_r, e_i = lax.fori_loop(0, R // 8, step, (car[:, lanes], cai[:, lanes]), unroll=4)
            car[:, lanes] = e_r
            cai[:, lanes] = e_i

        if store:
            for cb in range(4):
                st_r = sr_ref[:, cb * 512:(cb + 1) * 512].astype(bf16)
                st_i = si_ref[:, cb * 512:(cb + 1) * 512].astype(bf16)
                y = (jnp.dot(st_r, cdr_ref[cb], preferred_element_type=f32)
                     - jnp.dot(st_i, cdi_ref[cb], preferred_element_type=f32))
                cols = slice(cb * 128, (cb + 1) * 128)
                y = y + d_ref[:, cols] * u[:, cols]
                y_ref[:, cols] = y
                ys_ref[:, cols] = _gelu(y).astype(bf16)
        else:
            @pl.when(i == nblk - 1)
            def _():
                p_r, p_i = _pow_segment(ar_ref[...], ai_ref[...], seg_len)
                t_r, t_i = car[0:1, :], cai[0:1, :]
                or_ref[0:1, :] = jnp.zeros((1, N_STATES), f32)
                oi_ref[0:1, :] = jnp.zeros((1, N_STATES), f32)
                for j in range(1, N_SEG):
                    or_ref[j:j + 1, :] = t_r
                    oi_ref[j:j + 1, :] = t_i
                    m_r, m_i = _cmul(p_r, p_i, t_r, t_i)
                    t_r, t_i = car[j:j + 1, :] + m_r, cai[j:j + 1, :] + m_i

    blk = lambda w: pl.BlockSpec((R, w), lambda i: (i, 0))
    in_specs = [blk(SSM_WIDTH), _full((4, 128, 512)), _full((4, 128, 512)), _full((1, N_STATES)), _full((1, N_STATES))]
    args = [u_perm, bd_r, bd_i, ar, ai]
    scratch = [pltpu.VMEM((R, N_STATES), f32), pltpu.VMEM((R, N_STATES), f32),
               pltpu.VMEM((N_SEG, N_STATES), f32), pltpu.VMEM((N_SEG, N_STATES), f32)]
    if store:
        in_specs += [_full((4, 512, 128)), _full((4, 512, 128)), _full((1, SSM_WIDTH)),
                     _full((N_SEG, N_STATES)), _full((N_SEG, N_STATES))]
        args += list(final)
        out_specs = [blk(SSM_WIDTH), blk(SSM_WIDTH), blk(N_STATES), blk(N_STATES)]
        out_shape = [jax.ShapeDtypeStruct((S, SSM_WIDTH), f32), jax.ShapeDtypeStruct((S, SSM_WIDTH), bf16),
                     jax.ShapeDtypeStruct((S, N_STATES), f32), jax.ShapeDtypeStruct((S, N_STATES), f32)]
        name = "ssm_scan_fwd"
    else:
        out_specs = [_full((N_SEG, N_STATES)), _full((N_SEG, N_STATES))]
        out_shape = [jax.ShapeDtypeStruct((N_SEG, N_STATES), f32)] * 2
        name = "ssm_scan_fwd_carry"
    res, landed = _pcall(body, args, name=name, grid=(nblk,), in_specs=in_specs, out_specs=out_specs,
                         out_shape=out_shape, scratch_shapes=scratch, rides=rides)
    return (res, landed) if rides else res


def ssm_scan_bwd(dys_perm, y_perm, cdt_r, cdt_i, ar, ai, *, final=None):
    S = dys_perm.shape[0]
    R = SCAN_ROWS
    nblk = S // R
    seg_len = S // N_SEG
    nstrip = N_STATES // LANE_STRIP
    store = final is not None

    def body(*refs):
        if store:
            (dys_ref, y_ref, cdr_ref, cdi_ref, ar_ref, ai_ref, u_ref, sr_ref, si_ref, bdr_ref, bdi_ref, d_ref, ir_ref,
             ii_ref, du_ref, dbr_ref, dbi_ref, dcr_ref, dci_ref, dar_ref, dai_ref, dd_ref, dsr, dsi, lmr, lmi, car,
             cai) = refs
        else:
            dys_ref, y_ref, cdr_ref, cdi_ref, ar_ref, ai_ref, or_ref, oi_ref, dsr, dsi, car, cai = refs
        i = pl.program_id(0)
        dy = dys_ref[...] * _gelu_grad(y_ref[...])
        dyb = dy.astype(bf16)
        for cb in range(4):
            ds_ = dyb[:, cb * 128:(cb + 1) * 128]
            dsr[:, cb * 512:(cb + 1) * 512] = jnp.dot(ds_, cdr_ref[cb], preferred_element_type=f32)
            dsi[:, cb * 512:(cb + 1) * 512] = -jnp.dot(ds_, cdi_ref[cb], preferred_element_type=f32)

        @pl.when(i == 0)
        def _():
            if store:
                car[...] = ir_ref[...]
                cai[...] = ii_ref[...]
                dar_ref[...] = jnp.zeros_like(dar_ref)
                dai_ref[...] = jnp.zeros_like(dai_ref)
                dbr_ref[...] = jnp.zeros_like(dbr_ref)
                dbi_ref[...] = jnp.zeros_like(dbi_ref)
                dcr_ref[...] = jnp.zeros_like(dcr_ref)
                dci_ref[...] = jnp.zeros_like(dci_ref)
                dd_ref[...] = jnp.zeros_like(dd_ref)
            else:
                car[...] = jnp.zeros_like(car)
                cai[...] = jnp.zeros_like(cai)

        for ls in range(nstrip):
            lanes = pl.ds(ls * LANE_STRIP, LANE_STRIP)
            a_r = jnp.broadcast_to(ar_ref[:, lanes], (N_SEG, LANE_STRIP))
            a_i = jnp.broadcast_to(ai_ref[:, lanes], (N_SEG, LANE_STRIP))
            zero = jnp.zeros((N_SEG, LANE_STRIP), f32)

            def step(k, carry, lanes=lanes, a_r=a_r, a_i=a_i):
                l_r, l_i, g_r, g_i = carry
                row = pl.multiple_of((R // 8 - 1 - k) * 8, 8)
                if store:
                    s_r = sr_ref[pl.ds(row, 8), lanes]
                    s_i = si_ref[pl.ds(row, 8), lanes]
                    g_r = g_r + l_r * s_r + l_i * s_i
                    g_i = g_i + l_i * s_r - l_r * s_i
                n_r = dsr[pl.ds(row, 8), lanes] + a_r * l_r + a_i * l_i
                n_i = dsi[pl.ds(row, 8), lanes] + a_r * l_i - a_i * l_r
                if store:
                    lmr[pl.ds(row, 8), lanes] = n_r
                    lmi[pl.ds(row, 8), lanes] = n_i
                return n_r, n_i, g_r, g_i

            e_r, e_i, g_r, g_i = lax.fori_loop(0, R // 8, step, (car[:, lanes], cai[:, lanes], zero, zero), unroll=4)
            car[:, lanes] = e_r
            cai[:, lanes] = e_i
            if store:
                dar_ref[:, lanes] += g_r
                dai_ref[:, lanes] += g_i

        if store:
            u = u_ref[...]
            ub = u.astype(bf16)
            for cb in range(4):
                cols = slice(cb * 128, (cb + 1) * 128)
                st = slice(cb * 512, (cb + 1) * 512)
                l_r = lmr[:, st].astype(bf16)
                l_i = lmi[:, st].astype(bf16)
                du = (jnp.dot(l_r, bdr_ref[cb], preferred_element_type=f32)
                      + jnp.dot(l_i, bdi_ref[cb], preferred_element_type=f32))
                du_ref[:, cols] = (du + d_ref[:, cols] * dy[:, cols]).astype(bf16)
                dbr_ref[cb] += lax.dot_general(l_r, ub[:, cols], TN_DIMS, preferred_element_type=f32)
                dbi_ref[cb] += lax.dot_general(l_i, ub[:, cols], TN_DIMS, preferred_element_type=f32)
                dcr_ref[cb] += lax.dot_general(dyb[:, cols], sr_ref[:, st].astype(bf16), TN_DIMS,
                                               preferred_element_type=f32)
                dci_ref[cb] += lax.dot_general(dyb[:, cols], si_ref[:, st].astype(bf16), TN_DIMS,
                                               preferred_element_type=f32)
            dd_ref[...] += jnp.sum(dy * u, axis=0, keepdims=True)
        else:
            @pl.when(i == nblk - 1)
            def _():
                p_r, p_i = _pow_segment(ar_ref[...], ai_ref[...], seg_len)
                p_i = -p_i
                t_r, t_i = car[N_SEG - 1:N_SEG, :], cai[N_SEG - 1:N_SEG, :]
                or_ref[N_SEG - 1:N_SEG, :] = jnp.zeros((1, N_STATES), f32)
                oi_ref[N_SEG - 1:N_SEG, :] = jnp.zeros((1, N_STATES), f32)
                for j in range(N_SEG - 2, -1, -1):
                    or_ref[j:j + 1, :] = t_r
                    oi_ref[j:j + 1, :] = t_i
                    m_r, m_i = _cmul(p_r, p_i, t_r, t_i)
                    t_r, t_i = car[j:j + 1, :] + m_r, cai[j:j + 1, :] + m_i

    blk = lambda w: pl.BlockSpec((R, w), lambda i: (nblk - 1 - i, 0))
    in_specs = [blk(SSM_WIDTH), blk(SSM_WIDTH), _full((4, 128, 512)), _full((4, 128, 512)), _full((1, N_STATES)),
                _full((1, N_STATES))]
    args = [dys_perm, y_perm, cdt_r, cdt_i, ar, ai]
    seg = jax.ShapeDtypeStruct((N_SEG, N_STATES), f32)
    if store:
        in_specs += [blk(SSM_WIDTH), blk(N_STATES), blk(N_STATES), _full((4, 512, 128)), _full((4, 512, 128)),
                     _full((1, SSM_WIDTH)), _full((N_SEG, N_STATES)), _full((N_SEG, N_STATES))]
        args += list(final)
        out_specs = [blk(SSM_WIDTH), _full((4, 512, 128)), _full((4, 512, 128)), _full((4, 128, 512)),
                     _full((4, 128, 512)), _full((N_SEG, N_STATES)), _full((N_SEG, N_STATES)), _full((1, SSM_WIDTH))]
        b_acc = jax.ShapeDtypeStruct((4, 512, 128), f32)
        c_acc = jax.ShapeDtypeStruct((4, 128, 512), f32)
        out_shape = [jax.ShapeDtypeStruct((S, SSM_WIDTH), bf16), b_acc, b_acc, c_acc, c_acc, seg, seg,
                     jax.ShapeDtypeStruct((1, SSM_WIDTH), f32)]
        scratch = [pltpu.VMEM((R, N_STATES), f32)] * 4 + [pltpu.VMEM((N_SEG, N_STATES), f32)] * 2
        name = "ssm_scan_bwd"
    else:
        out_specs = [_full((N_SEG, N_STATES)), _full((N_SEG, N_STATES))]
        out_shape = [seg, seg]
        scratch = [pltpu.VMEM((R, N_STATES), f32)] * 2 + [pltpu.VMEM((N_SEG, N_STATES), f32)] * 2
        name = "ssm_scan_bwd_carry"
    return pl.pallas_call(body, name=name, grid=(nblk,), in_specs=in_specs, out_specs=out_specs,
                          out_shape=out_shape, scratch_shapes=scratch, compiler_params=_params())(*args)


def ssm_param_bwd(dbb_r, dbb_i, bt_re, bt_im, kr, ki, ar, ai, lr, li, ldt, da_r, da_i):
    def body(dbr_ref, dbi_ref, br_ref, bi_ref, kr_ref, ki_ref, ar_ref, ai_ref, lr_ref, li_ref, ldt_ref, dar_ref,
             dai_ref, obr_ref, obi_ref, olr_ref, oli_ref, odt_ref):
        dbr, dbi, b_r, b_i = dbr_ref[...], dbi_ref[...], br_ref[...], bi_ref[...]
        k_r, k_i, a_r, a_i = kr_ref[...], ki_ref[...], ar_ref[...], ai_ref[...]
        l_r, l_i = lr_ref[...], li_ref[...]
        dt = jnp.exp(ldt_ref[...])
        obr_ref[...] = k_r * dbr + k_i * dbi
        obi_ref[...] = k_r * dbi - k_i * dbr
        gk_r = jnp.sum(dbr * b_r + dbi * b_i, axis=0, keepdims=True)
        gk_i = jnp.sum(dbi * b_r - dbr * b_i, axis=0, keepdims=True)
        ga_r = jnp.sum(dar_ref[...], axis=0, keepdims=True)
        ga_i = jnp.sum(dai_ref[...], axis=0, keepdims=True)
        den = l_r * l_r + l_i * l_i
        c_r, c_i = l_r / den, l_i / den
        m_r, m_i = _cmul(c_r, c_i, gk_r, gk_i)
        g_r, g_i = ga_r + m_r, ga_i + m_i
        t1_r, t1_i = _cmul(dt * a_r, -dt * a_i, g_r, g_i)
        q_r, q_i = _cmul(k_r, k_i, c_r, -c_i)
        t2_r, t2_i = _cmul(-q_r, q_i, gk_r, gk_i)
        olr_ref[...] = t1_r + t2_r
        oli_ref[...] = t1_i + t2_i
        w_r, w_i = _cmul(l_r, l_i, a_r, a_i)
        odt_ref[...] = dt * (w_r * g_r + w_i * g_i)

    v = jax.ShapeDtypeStruct((1, N_STATES), f32)
    m = jax.ShapeDtypeStruct((SSM_GROUP, N_STATES), f32)
    return pl.pallas_call(body, name="ssm_param_bwd", out_shape=[m, m, v, v, v])(
        dbb_r, dbb_i, bt_re, bt_im, kr, ki, ar, ai, lr, li, ldt, da_r, da_i)


def group_sum(v):
    def body(v_ref, o_ref):
        o_ref[...] = jnp.sum(v_ref[...], axis=-1, keepdims=True)
    return pl.pallas_call(body, name="group_sum", out_shape=jax.ShapeDtypeStruct((v.shape[0], 1), f32))(v)


ATTN_SCALE = XATTN_HEAD_DIM ** -0.5


def _attn_probs(q_h, k_h):
    s = lax.dot_general(q_h, k_h, NT_DIMS, preferred_element_type=f32) * ATTN_SCALE
    e = jnp.exp(s - jnp.max(s, axis=-1, keepdims=True))
    return e / jnp.sum(e, axis=-1, keepdims=True)


def attn_fwd(proj, kv, tm=TM):
    S = proj.shape[0]
    M = kv.shape[0]

    def body(q_ref, kv_ref, o_ref):
        for h in range(XATTN_HEADS):
            cols = slice(h * XATTN_HEAD_DIM, (h + 1) * XATTN_HEAD_DIM)
            q_h = q_ref[:, cols].astype(bf16)
            k_h = kv_ref[:, cols]
            v_h = kv_ref[:, XATTN_WIDTH + h * XATTN_HEAD_DIM:XATTN_WIDTH + (h + 1) * XATTN_HEAD_DIM]
            p = _attn_probs(q_h, k_h)
            o_ref[:, cols] = jnp.dot(p.astype(bf16), v_h, preferred_element_type=f32).astype(bf16)

    return pl.pallas_call(
        body, name="attn_fwd", grid=(S // tm,), in_specs=[_rows(tm, XATTN_WIDTH, 10), _full((M, 2 * XATTN_WIDTH))],
        out_specs=_rows(tm, XATTN_WIDTH), out_shape=jax.ShapeDtypeStruct((S, XATTN_WIDTH), bf16),
        compiler_params=_params())(proj, kv)


def attn_bwd(do, proj, kv, dproj, tm=TM):
    S = proj.shape[0]
    M = kv.shape[0]

    def body(do_ref, q_ref, kv_ref, _, dq_ref, dkv_ref):
        i = pl.program_id(0)

        @pl.when(i == 0)
        def _():
            dkv_ref[...] = jnp.zeros_like(dkv_ref)

        for h in range(XATTN_HEADS):
            cols = slice(h * XATTN_HEAD_DIM, (h + 1) * XATTN_HEAD_DIM)
            vcols = slice(XATTN_WIDTH + h * XATTN_HEAD_DIM, XATTN_WIDTH + (h + 1) * XATTN_HEAD_DIM)
            q_h = q_ref[:, cols].astype(bf16)
            k_h = kv_ref[:, cols]
            v_h = kv_ref[:, vcols]
            do_h = do_ref[:, cols]
            p = _attn_probs(q_h, k_h)
            dp = lax.dot_general(do_h, v_h, NT_DIMS, preferred_element_type=f32)
            ds = (p * (dp - jnp.sum(dp * p, axis=-1, keepdims=True)) * ATTN_SCALE).astype(bf16)
            dq_ref[:, cols] = jnp.dot(ds, k_h, preferred_element_type=f32).astype(bf16)
            dkv_ref[:, cols] += lax.dot_general(ds, q_h, TN_DIMS, preferred_element_type=f32)
            dkv_ref[:, vcols] += lax.dot_general(p.astype(bf16), do_h, TN_DIMS, preferred_element_type=f32)

    return pl.pallas_call(
        body, name="attn_bwd", grid=(S // tm,),
        in_specs=[_rows(tm, XATTN_WIDTH), _rows(tm, XATTN_WIDTH, 10), _full((M, 2 * XATTN_WIDTH)),
                  pl.BlockSpec(memory_space=pl.ANY)],
        out_specs=[_rows(tm, XATTN_WIDTH, 10), _full((M, 2 * XATTN_WIDTH))],
        out_shape=[jax.ShapeDtypeStruct(dproj.shape, bf16), jax.ShapeDtypeStruct((M, 2 * XATTN_WIDTH), f32)],
        input_output_aliases={3: 0},
        compiler_params=_params())(do, proj, kv, dproj)


def merge_fwd(proj, b_gate, y_a, glu, y_c, tm=TM, rides=()):
    S = proj.shape[0]

    def body(g0_ref, g1_ref, g2_ref, b_ref, ya_ref, ga_ref, gb_ref, yc_ref, o_ref):
        b = b_ref[...]
        g0 = _sig(g0_ref[...] + b[:, 0:D_MODEL])
        g1 = _sig(g1_ref[...] + b[:, D_MODEL:2 * D_MODEL])
        g2 = _sig(g2_ref[...] + b[:, 2 * D_MODEL:3 * D_MODEL])
        y_b = ga_ref[...] * _sig(gb_ref[...])
        o_ref[...] = (g0 * ya_ref[...] + g1 * y_b + g2 * yc_ref[...]).astype(bf16)

    res, landed = _pcall(
        body, [proj, proj, proj, b_gate, y_a, glu, glu, y_c], name="merge_fwd", grid=(S // tm,),
        in_specs=[_rows(tm, D_MODEL, 0), _rows(tm, D_MODEL, 1), _rows(tm, D_MODEL, 2), _full((1, GATE_COLS)),
                  _rows(tm, D_MODEL), _rows(tm, D_MODEL, 0), _rows(tm, D_MODEL, 1), _rows(tm, D_MODEL)],
        out_specs=[_rows(tm, D_MODEL)], out_shape=[jax.ShapeDtypeStruct((S, D_MODEL), bf16)], rides=rides)
    return (res[0], landed) if rides else res[0]


def merge_bwd(dm, proj, b_gate, y_a, glu, y_c, tm=256):
    S = proj.shape[0]

    def body(dm_ref, g0_ref, g1_ref, g2_ref, b_ref, ya_ref, ga_ref, gb_ref, yc_ref,
             dg_ref, dya_ref, dgl_ref, dyc_ref, dbg_ref):
        i = pl.program_id(0)
        b = b_ref[...]
        dmv = dm_ref[...]
        ga, sb = ga_ref[...], _sig(gb_ref[...])
        ys = (ya_ref[...], ga * sb, yc_ref[...])
        gs = tuple(_sig(r[...] + b[:, k * D_MODEL:(k + 1) * D_MODEL]) for k, r in enumerate((g0_ref, g1_ref, g2_ref)))

        @pl.when(i == 0)
        def _():
            dbg_ref[...] = jnp.zeros_like(dbg_ref)

        for k in range(3):
            dpre = dmv * ys[k] * gs[k] * (1.0 - gs[k])
            dg_ref[:, k * D_MODEL:(k + 1) * D_MODEL] = dpre.astype(bf16)
            dbg_ref[:, k * D_MODEL:(k + 1) * D_MODEL] += jnp.sum(dpre, axis=0, keepdims=True)
        dya_ref[...] = (dmv * gs[0]).astype(bf16)
        dyc_ref[...] = (dmv * gs[2]).astype(bf16)
        dyb = dmv * gs[1]
        dgl_ref[:, 0:D_MODEL] = (dyb * sb).astype(bf16)
        dgl_ref[:, D_MODEL:2 * D_MODEL] = (dyb * ga * sb * (1.0 - sb)).astype(bf16)

    return pl.pallas_call(
        body, name="merge_bwd", grid=(S // tm,),
        in_specs=[_rows(tm, D_MODEL), _rows(tm, D_MODEL, 0), _rows(tm, D_MODEL, 1), _rows(tm, D_MODEL, 2),
                  _full((1, GATE_COLS)), _rows(tm, D_MODEL), _rows(tm, D_MODEL, 0), _rows(tm, D_MODEL, 1),
                  _rows(tm, D_MODEL)],
        out_specs=[_rows(tm, GATE_COLS), _rows(tm, D_MODEL), _rows(tm, 2 * D_MODEL), _rows(tm, D_MODEL),
                   _full((1, GATE_COLS))],
        out_shape=[jax.ShapeDtypeStruct((S, IN_COLS), bf16), jax.ShapeDtypeStruct((S, D_MODEL), bf16),
                   jax.ShapeDtypeStruct((S, 2 * D_MODEL), bf16), jax.ShapeDtypeStruct((S, D_MODEL), bf16),
                   jax.ShapeDtypeStruct((1, GATE_COLS), f32)],
        compiler_params=_params())(dm, proj, proj, proj, b_gate, y_a, glu, glu, y_c)


def _ln_stats(r):
    mu = jnp.mean(r, axis=-1, keepdims=True)
    xc = r - mu
    var = jnp.mean(xc * xc, axis=-1, keepdims=True)
    rstd = lax.rsqrt(var + LN_EPS)
    return xc * rstd, rstd


def _ln_bwd(dy, xhat, rstd, g):
    dxh = dy * g
    return rstd * (dxh - jnp.mean(dxh, axis=-1, keepdims=True) - xhat * jnp.mean(dxh * xhat, axis=-1, keepdims=True))


def ln1_fwd(x, t1, g, b, tm=TM):
    S = x.shape[0]

    def body(x_ref, t_ref, g_ref, b_ref, o_ref, ob_ref):
        xhat, _ = _ln_stats(ALPHA * x_ref[...] + t_ref[...])
        h = xhat * g_ref[...] + b_ref[...]
        o_ref[...] = h
        ob_ref[...] = h.astype(bf16)

    return pl.pallas_call(
        body, name="ln1_fwd", grid=(S // tm,),
        in_specs=[_rows(tm, D_MODEL), _rows(tm, D_MODEL), _full((1, D_MODEL)), _full((1, D_MODEL))],
        out_specs=[_rows(tm, D_MODEL), _rows(tm, D_MODEL)],
        out_shape=[jax.ShapeDtypeStruct((S, D_MODEL), f32), jax.ShapeDtypeStruct((S, D_MODEL), bf16)],
        compiler_params=_params())(x, t1, g, b)


def ln2_loss_bwd(h1, t2, b_down, g, b, target, tm=TM):
    S = h1.shape[0]

    def body(h_ref, t_ref, bd_ref, g_ref, b_ref, y_ref, dr_ref, drb_ref, l_ref, dg_ref, db_ref, dbd_ref):
        i = pl.program_id(0)
        xhat, rstd = _ln_stats(ALPHA * h_ref[...] + t_ref[...] + bd_ref[...])
        gv = g_ref[...]
        err = xhat * gv + b_ref[...] - y_ref[...]
        dout = err * (1.0 / D_MODEL)
        dr = _ln_bwd(dout, xhat, rstd, gv)
        dr_ref[...] = dr
        drb_ref[...] = dr.astype(bf16)
        part = 0.5 * jnp.sum(jnp.sum(err * err, axis=-1, keepdims=True) * (1.0 / D_MODEL), axis=0, keepdims=True)

        @pl.when(i == 0)
        def _():
            l_ref[...] = jnp.zeros_like(l_ref)
            dg_ref[...] = jnp.zeros_like(dg_ref)
            db_ref[...] = jnp.zeros_like(db_ref)
            dbd_ref[...] = jnp.zeros_like(dbd_ref)

        l_ref[...] += jnp.broadcast_to(part, l_ref.shape)
        dg_ref[...] += jnp.sum(dout * xhat, axis=0, keepdims=True)
        db_ref[...] += jnp.sum(dout, axis=0, keepdims=True)
        dbd_ref[...] += jnp.sum(dr, axis=0, keepdims=True)

    vec = jax.ShapeDtypeStruct((1, D_MODEL), f32)
    return pl.pallas_call(
        body, name="ln2_loss_bwd", grid=(S // tm,),
        in_specs=[_rows(tm, D_MODEL), _rows(tm, D_MODEL), _full((1, D_MODEL)), _full((1, D_MODEL)),
                  _full((1, D_MODEL)), _rows(tm, D_MODEL)],
        out_specs=[_rows(tm, D_MODEL), _rows(tm, D_MODEL), _full((1, 128)), _full((1, D_MODEL)), _full((1, D_MODEL)),
                   _full((1, D_MODEL))],
        out_shape=[jax.ShapeDtypeStruct((S, D_MODEL), f32), jax.ShapeDtypeStruct((S, D_MODEL), bf16),
                   jax.ShapeDtypeStruct((1, 128), f32), vec, vec, vec],
        compiler_params=_params())(h1, t2, b_down, g, b, target)


def ln1_bwd(x, t1, dr2, t3, g, tm=TM):
    S = x.shape[0]

    def body(x_ref, t_ref, dr2_ref, t3_ref, g_ref, dr_ref, drb_ref, dg_ref, db_ref):
        i = pl.program_id(0)
        xhat, rstd = _ln_stats(ALPHA * x_ref[...] + t_ref[...])
        dh = ALPHA * dr2_ref[...] + t3_ref[...]
        dr = _ln_bwd(dh, xhat, rstd, g_ref[...])
        dr_ref[...] = dr
        drb_ref[...] = dr.astype(bf16)

        @pl.when(i == 0)
        def _():
            dg_ref[...] = jnp.zeros_like(dg_ref)
            db_ref[...] = jnp.zeros_like(db_ref)

        dg_ref[...] += jnp.sum(dh * xhat, axis=0, keepdims=True)
        db_ref[...] += jnp.sum(dh, axis=0, keepdims=True)

    vec = jax.ShapeDtypeStruct((1, D_MODEL), f32)
    return pl.pallas_call(
        body, name="ln1_bwd", grid=(S // tm,),
        in_specs=[_rows(tm, D_MODEL)] * 4 + [_full((1, D_MODEL))],
        out_specs=[_rows(tm, D_MODEL), _rows(tm, D_MODEL), _full((1, D_MODEL)), _full((1, D_MODEL))],
        out_shape=[jax.ShapeDtypeStruct((S, D_MODEL), f32), jax.ShapeDtypeStruct((S, D_MODEL), bf16), vec, vec],
        compiler_params=_params())(x, t1, dr2, t3, g)


def gather_weights(shard, name):
    return _pcall(lambda: None, [], name=name, grid=(), in_specs=[], out_specs=[], out_shape=[],
                  rides=(("gather", shard),))[1][0]


def swap_sibling(p):
    def body(p_ref, q_ref, send_sem, recv_sem):
        x, y, c = lax.axis_index("x"), lax.axis_index("y"), lax.axis_index("c")
        cp = pltpu.make_async_remote_copy(src_ref=p_ref, dst_ref=q_ref, send_sem=send_sem, recv_sem=recv_sem,
                                          device_id=(x, y, 1 - c), device_id_type=MESH)
        cp.start()
        cp.wait()

    return pl.pallas_call(
        body, name="swap_sibling", in_specs=[HBM_SPEC], out_specs=HBM_SPEC,
        out_shape=jax.ShapeDtypeStruct(p.shape, p.dtype),
        scratch_shapes=[pltpu.SemaphoreType.DMA(()), pltpu.SemaphoreType.DMA(())],
    )(p)


def sum_slots(recv, name, tm):
    n, R, C = recv.shape

    def body(r_ref, o_ref):
        acc = r_ref[0].astype(f32)
        for k in range(1, n):
            acc = acc + r_ref[k].astype(f32)
        o_ref[...] = acc

    return pl.pallas_call(
        body, name=name, grid=(R // tm,), in_specs=[pl.BlockSpec((n, tm, C), lambda i: (0, i, 0))],
        out_specs=_rows(tm, C), out_shape=jax.ShapeDtypeStruct((R, C), f32), compiler_params=_params())(recv)


def _adamw(w, g, m, v):
    m = ADAM_B1 * m + (1.0 - ADAM_B1) * g
    v = ADAM_B2 * v + (1.0 - ADAM_B2) * (g * g)
    m_hat = m / (1.0 - ADAM_B1 ** ADAM_STEP)
    v_hat = v / (1.0 - ADAM_B2 ** ADAM_STEP)
    delta = -ADAM_LR * (m_hat / (jnp.sqrt(v_hat) + ADAM_EPS) + ADAM_WD * w)
    return delta, m, v


def adam_pair(p, q, w, m, v, tm=280):
    R, C = w.shape

    def body(p_ref, q_ref, w_ref, m_ref, v_ref, g_ref, d_ref, nm_ref, nv_ref):
        g = p_ref[...] + q_ref[...]
        g_ref[...] = g
        d_ref[...], nm_ref[...], nv_ref[...] = _adamw(w_ref[...], g, m_ref[...], v_ref[...])

    o = jax.ShapeDtypeStruct((R, C), f32)
    return pl.pallas_call(body, name="adam_pair", grid=(R // tm,), in_specs=[_rows(tm, C)] * 5,
                          out_specs=[_rows(tm, C)] * 4, out_shape=[o] * 4, compiler_params=_params())(p, q, w, m, v)


def adam_slots(recv, w, m, v, name):
    n = recv.shape[0]

    def body(r_ref, w_ref, m_ref, v_ref, g_ref, d_ref, nm_ref, nv_ref):
        g = r_ref[0]
        for k in range(1, n):
            g = g + r_ref[k]
        g_ref[...] = g
        d_ref[...], nm_ref[...], nv_ref[...] = _adamw(w_ref[...], g, m_ref[...], v_ref[...])

    o = jax.ShapeDtypeStruct(w.shape, f32)
    return pl.pallas_call(body, name=name, out_shape=[o] * 4, compiler_params=_params())(recv, w, m, v)


BIG = ("w_in", "w_conv_out", "w_glu", "w_xattn_out", "w_kv", "w_out", "w_up", "w_down")
MID = ("w_conv_out", "w_glu", "w_xattn_out", "w_kv", "w_out")
SMALL = ("b_gate", "ssm_lam_re", "ssm_lam_im", "ssm_log_dt", "ssm_b_re", "ssm_b_im", "ssm_c_re", "ssm_c_im", "ssm_d",
         "ln1_g", "ln1_b", "b_up", "b_down", "ln2_g", "ln2_b")
PACK_COLS = 1024


def _pack_rows(arrs):
    return jnp.concatenate([a.reshape(-1, PACK_COLS) for a in arrs], axis=0)


def _pad_flat(a, mult=1024):
    a = a.reshape(-1)
    return jnp.pad(a, (0, (-a.shape[0]) % mult))


def _perm(a):
    S, W = a.shape
    return a.reshape(N_SEG, S // N_SEG, W).transpose(1, 0, 2).reshape(S, W)


def _unperm(a):
    S, W = a.shape
    return a.reshape(S // N_SEG, N_SEG, W).transpose(1, 0, 2).reshape(S, W)


def _state_rows(a):
    return a.transpose(2, 0, 1).reshape(SSM_GROUP, N_STATES)


def _block_diag_b(bt):
    b4 = bt.reshape(SSM_GROUP, 4, 8, SSM_STATE)
    eye = jnp.eye(8, dtype=bt.dtype)
    return jnp.einsum("hcgp,gk->cghkp", b4, eye).reshape(4, 128, 512)


def _block_diag_c(c):
    c4 = c.reshape(4, 8, SSM_GROUP, SSM_STATE)
    eye = jnp.eye(8, dtype=c.dtype)
    return jnp.einsum("cghp,gk->cgpkh", c4, eye).reshape(4, 512, 128)


def _diag_b(acc):
    a = acc.reshape(4, 8, SSM_STATE, 8, SSM_GROUP)
    eye = jnp.eye(8, dtype=acc.dtype)
    return jnp.einsum("cgpkh,gk->hcgp", a, eye).reshape(SSM_GROUP, N_STATES)


def _diag_c(acc):
    a = acc.reshape(4, 8, SSM_GROUP, 8, SSM_STATE)
    eye = jnp.eye(8, dtype=acc.dtype)
    return jnp.einsum("cghkp,gk->cghp", a, eye).reshape(SSM_GROUPS, SSM_GROUP, SSM_STATE)


def kernel(x, mem, w_in, b_gate, conv_w, w_conv_out, ssm_lam_re, ssm_lam_im, ssm_log_dt, ssm_b_re, ssm_b_im, ssm_c_re, ssm_c_im, ssm_d, w_glu, w_kv, w_xattn_out, w_out, ln1_g, ln1_b, w_up, b_up, w_down, b_down, ln2_g, ln2_b, loss_target, m_w_in, m_b_gate, m_conv_w, m_w_conv_out, m_ssm_lam_re, m_ssm_lam_im, m_ssm_log_dt, m_ssm_b_re, m_ssm_b_im, m_ssm_c_re, m_ssm_c_im, m_ssm_d, m_w_glu, m_w_kv, m_w_xattn_out, m_w_out, m_ln1_g, m_ln1_b, m_w_up, m_b_up, m_w_down, m_b_down, m_ln2_g, m_ln2_b, v_w_in, v_b_gate, v_conv_w, v_w_conv_out, v_ssm_lam_re, v_ssm_lam_im, v_ssm_log_dt, v_ssm_b_re, v_ssm_b_im, v_ssm_c_re, v_ssm_c_im, v_ssm_d, v_w_glu, v_w_kv, v_w_xattn_out, v_w_out, v_ln1_g, v_ln1_b, v_w_up, v_b_up, v_w_down, v_b_down, v_ln2_g, v_ln2_b):
    W = dict(w_in=w_in, b_gate=b_gate, conv_w=conv_w, w_conv_out=w_conv_out, ssm_lam_re=ssm_lam_re,
             ssm_lam_im=ssm_lam_im, ssm_log_dt=ssm_log_dt, ssm_b_re=ssm_b_re, ssm_b_im=ssm_b_im, ssm_c_re=ssm_c_re,
             ssm_c_im=ssm_c_im, ssm_d=ssm_d, w_glu=w_glu, w_kv=w_kv, w_xattn_out=w_xattn_out, w_out=w_out,
             ln1_g=ln1_g, ln1_b=ln1_b, w_up=w_up, b_up=b_up, w_down=w_down, b_down=b_down, ln2_g=ln2_g, ln2_b=ln2_b)
    MOM = dict(w_in=m_w_in, b_gate=m_b_gate, conv_w=m_conv_w, w_conv_out=m_w_conv_out, ssm_lam_re=m_ssm_lam_re,
               ssm_lam_im=m_ssm_lam_im, ssm_log_dt=m_ssm_log_dt, ssm_b_re=m_ssm_b_re, ssm_b_im=m_ssm_b_im,
               ssm_c_re=m_ssm_c_re, ssm_c_im=m_ssm_c_im, ssm_d=m_ssm_d, w_glu=m_w_glu, w_kv=m_w_kv,
               w_xattn_out=m_w_xattn_out, w_out=m_w_out, ln1_g=m_ln1_g, ln1_b=m_ln1_b, w_up=m_w_up, b_up=m_b_up,
               w_down=m_w_down, b_down=m_b_down, ln2_g=m_ln2_g, ln2_b=m_ln2_b)
    VEL = dict(w_in=v_w_in, b_gate=v_b_gate, conv_w=v_conv_w, w_conv_out=v_w_conv_out, ssm_lam_re=v_ssm_lam_re,
               ssm_lam_im=v_ssm_lam_im, ssm_log_dt=v_ssm_log_dt, ssm_b_re=v_ssm_b_re, ssm_b_im=v_ssm_b_im,
               ssm_c_re=v_ssm_c_re, ssm_c_im=v_ssm_c_im, ssm_d=v_ssm_d, w_glu=v_w_glu, w_kv=v_w_kv,
               w_xattn_out=v_w_xattn_out, w_out=v_w_out, ln1_g=v_ln1_g, ln1_b=v_ln1_b, w_up=v_w_up, b_up=v_b_up,
               w_down=v_w_down, b_down=v_b_down, ln2_g=v_ln2_g, ln2_b=v_ln2_b)
    names = list(W)
    xy = 2 * lax.axis_index("x") + lax.axis_index("y")

    xs = x[0]
    S = xs.shape[0]
    mems = mem[0]
    tgt = loss_target[0]

    shard_shapes = {n: W[n].shape[1:] for n in BIG}
    shard_bf = {n: W[n][0].astype(bf16) for n in BIG}
    win3 = gather_weights(shard_bf["w_in"].reshape(-1, PACK_COLS), "gather_w_in").reshape(4, D_MODEL, -1)
    mid_pack = _pack_rows([shard_bf[n] for n in MID])

    lr = ssm_lam_re.reshape(1, N_STATES)
    li = ssm_lam_im.reshape(1, N_STATES)
    ldt = jnp.repeat(ssm_log_dt.reshape(SSM_GROUPS), SSM_STATE).reshape(1, N_STATES)
    bt_re, bt_im = _state_rows(ssm_b_re[0]), _state_rows(ssm_b_im[0])
    ar, ai, kr, ki, bbt_r, bbt_i = ssm_prep(lr, li, ldt, bt_re, bt_im)
    bd_r, bd_i = _block_diag_b(bbt_r).astype(bf16), _block_diag_b(bbt_i).astype(bf16)
    cd_r, cd_i = _block_diag_c(ssm_c_re[0]).astype(bf16), _block_diag_c(ssm_c_im[0]).astype(bf16)
    bdt_r, bdt_i = bd_r.transpose(0, 2, 1), bd_i.transpose(0, 2, 1)
    cdt_r, cdt_i = cd_r.transpose(0, 2, 1), cd_i.transpose(0, 2, 1)
    d_skip = ssm_d.reshape(1, SSM_WIDTH)

    x_bf = xs.astype(bf16)
    mem_bf = mems.astype(bf16)
    u_col = GATE_COLS + 3 * CONV_WIDTH
    proj, (mid, convw4) = mm_fwd(x_bf, win3, "proj", rides=(("gather", mid_pack), ("gather", conv_w[0])))
    G3, row = {}, 0
    for n in MID:
        k, c = shard_shapes[n]
        nr = k * c // PACK_COLS
        G3[n] = mid[:, row:row + nr, :].reshape(4, k, c)
        row += nr
    wco3, wglu3, wxo3 = G3["w_conv_out"], G3["w_glu"], G3["w_xattn_out"]
    wkv3 = G3["w_kv"].reshape(1, D_MODEL, 2 * XATTN_WIDTH)
    wout3 = G3["w_out"].reshape(1, D_MODEL, D_MODEL)
    convw_full = convw4.transpose(1, 0, 2).reshape(3, CONV_WIDTH)
    a_conv = conv_fwd(proj, convw_full)
    y_a = mm_fwd(a_conv, wco3, "conv_out")
    u_perm = _perm(proj[:, u_col:u_col + SSM_WIDTH])
    init_r, init_i = ssm_scan_fwd(u_perm, bd_r, bd_i, ar, ai)
    (ysm_perm, ys_perm, st_r, st_i), (wup3,) = ssm_scan_fwd(
        u_perm, bd_r, bd_i, ar, ai, final=(cd_r, cd_i, d_skip, init_r, init_i), rides=(("gather", shard_bf["w_up"]),))
    y_s = _unperm(ys_perm)
    glu = mm_fwd(y_s, wglu3, "glu")
    kv = mm_fwd(mem_bf, wkv3, "kv", outs=(bf16,))
    o_att = attn_fwd(proj, kv)
    y_c = mm_fwd(o_att, wxo3, "xattn_out")
    merged, (wdn4,) = merge_fwd(proj, b_gate, y_a, glu, y_c, rides=(("gather", shard_bf["w_down"]),))
    wdn3 = wdn4.reshape(1, D_FF, D_MODEL)
    t1 = mm_fwd(merged, wout3, "w_out")
    h1, h1_bf = ln1_fwd(xs, t1, ln1_g, ln1_b)
    up, hdn = mm_fwd(h1_bf, wup3, "w_up", bias=b_up, outs=(f32, bf16), second=_relu2)
    t2 = mm_fwd(hdn, wdn3, "w_down", tn=512)

    dr2, dr2_bf, loss_part, d_ln2_g, d_ln2_b, d_b_down = ln2_loss_bwd(h1, t2, b_down, ln2_g, ln2_b, tgt)
    g_w_down = mm_bwd_w(hdn, dr2_bf, 1, "dw_down", tk=1024)
    (dup, d_b_up), (recv_dn,) = mm_bwd_x(
        dr2_bf, wdn3, "dup", tm=256, extras=(up,), colsum=True, out_dtype=bf16,
        epi=lambda acc, upv: acc * (2.0 * jnp.maximum(upv, 0.0)),
        rides=(("scatter", g_w_down.reshape(4, -1, PACK_COLS)),))
    g_w_up = mm_bwd_w(h1_bf, dup, 4, "dw_up")
    t3, (recv_up,) = mm_bwd_x(dup, wup3, "dh1", rides=(("scatter", g_w_up),))
    dr1, dr1_bf, d_ln1_g, d_ln1_b = ln1_bwd(xs, t1, dr2, t3, ln1_g)
    g_w_out = mm_bwd_w(merged, dr1_bf, 1, "dw_out")
    dmerged, (recv_out,) = mm_bwd_x(dr1_bf, wout3, "dmerged", tm=1024,
                                    rides=(("scatter", g_w_out.reshape(4, -1, PACK_COLS)),))
    dproj, dy_a, dglu, dy_c, d_b_gate = merge_bwd(dmerged, proj, b_gate, y_a, glu, y_c)

    g_w_co = mm_bwd_w(a_conv, dy_a, 4, "dw_conv_out")
    da_conv = mm_bwd_x(dy_a, wco3, "da_conv", tm=1024)
    dproj, d_conv_w = conv_bwd(da_conv, proj, convw_full, dproj)

    g_w_glu = mm_bwd_w(y_s, dglu, 4, "dw_glu")
    dys_perm = _perm(mm_bwd_x(dglu, wglu3, "dy_s", tm=1024))
    linit_r, linit_i = ssm_scan_bwd(dys_perm, ysm_perm, cdt_r, cdt_i, ar, ai)
    du_perm, dbacc_r, dbacc_i, dcacc_r, dcacc_i, da_r, da_i, d_ssm_d = ssm_scan_bwd(
        dys_perm, ysm_perm, cdt_r, cdt_i, ar, ai,
        final=(u_perm, st_r, st_i, bdt_r, bdt_i, d_skip, linit_r, linit_i))
    dbt_re, dbt_im, d_lr, d_li, d_ldt_state = ssm_param_bwd(
        _diag_b(dbacc_r), _diag_b(dbacc_i), bt_re, bt_im, kr, ki, ar, ai, lr, li, ldt, da_r, da_i)
    d_log_dt = group_sum(d_ldt_state.reshape(SSM_GROUPS, SSM_STATE))
    d_b_re = dbt_re.reshape(SSM_GROUP, SSM_GROUPS, SSM_STATE).transpose(1, 2, 0)
    d_b_im = dbt_im.reshape(SSM_GROUP, SSM_GROUPS, SSM_STATE).transpose(1, 2, 0)
    d_c_re = _diag_c(dcacc_r)
    d_c_im = -_diag_c(dcacc_i)

    g_w_xo = mm_bwd_w(o_att, dy_c, 4, "dw_xattn_out")
    do_att = mm_bwd_x(dy_c, wxo3, "do_att", tm=1024, out_dtype=bf16)
    dproj, dkv = attn_bwd(do_att, proj, kv, dproj)
    g_w_kv = mm_bwd_w(mem_bf, dkv, 1, "dw_kv")

    dproj = lax.dynamic_update_slice(dproj, _unperm(du_perm), (0, u_col))
    g_small = {"b_gate": d_b_gate, "ssm_lam_re": d_lr, "ssm_lam_im": d_li, "ssm_log_dt": d_log_dt, "ssm_b_re": d_b_re,
               "ssm_b_im": d_b_im, "ssm_c_re": d_c_re, "ssm_c_im": d_c_im, "ssm_d": d_ssm_d, "ln1_g": d_ln1_g,
               "ln1_b": d_ln1_b, "b_up": d_b_up, "b_down": d_b_down, "ln2_g": d_ln2_g, "ln2_b": d_ln2_b}
    small_names = SMALL + ("conv_w",)
    g_small["conv_w"] = d_conv_w
    sizes = {n: (g_small[n].size + 1023) // 1024 * 1024 for n in small_names}
    pack = lambda d: jnp.concatenate([_pad_flat(d[n]) for n in small_names]).reshape(-1, 128)
    conv_zero = jnp.zeros((3, CONV_WIDTH), f32)
    g_mid = jnp.concatenate([g.reshape(4, -1, PACK_COLS) for g in (g_w_co, g_w_glu, g_w_xo, g_w_kv)], axis=1)
    g_w_in, (recv_mid,) = mm_bwd_w(x_bf, dproj, 4, "dw_in", rides=(("scatter", g_mid),))
    dx, (recv_in, srecv) = mm_bwd_x(dproj, win3, "dx", extras=(dr1,), epi=lambda acc, d: acc + ALPHA * d,
                                    rides=(("scatter", g_w_in.reshape(4, -1, PACK_COLS)), ("all", pack(g_small))))
    part = jnp.concatenate([sum_slots(recv_in, "sum_w_in", 128), sum_slots(recv_mid, "sum_mid", 256),
                            sum_slots(recv_out, "sum_w_out", 256), sum_slots(recv_up, "sum_w_up", 256),
                            sum_slots(recv_dn, "sum_w_down", 256)], axis=0)
    other = swap_sibling(part)
    big_w = _pack_rows([W[n][0] for n in BIG])
    big_m = _pack_rows([MOM[n][0] for n in BIG])
    big_v = _pack_rows([VEL[n][0] for n in BIG])
    gb, db, mb, vb = adam_pair(part, other, big_w, big_m, big_v)
    gs, ds_, ms, vs = adam_slots(srecv, pack({**{n: W[n] for n in SMALL}, "conv_w": conv_zero}),
                                 pack({**{n: MOM[n] for n in SMALL}, "conv_w": conv_zero}),
                                 pack({**{n: VEL[n] for n in SMALL}, "conv_w": conv_zero}), "adam_small")

    def unpack_big(buf):
        out, r = {}, 0
        for n in BIG:
            k, c = shard_shapes[n]
            nr = k * c // PACK_COLS
            out[n] = buf[r:r + nr].reshape(1, k, c)
            r += nr
        return out

    def unpack_small(buf):
        flat = buf.reshape(-1)
        out, r = {}, 0
        for n in small_names:
            ref = g_small[n] if n == "conv_w" else W[n]
            out[n] = flat[r:r + ref.size].reshape(ref.shape)
            r += sizes[n]
        return out

    res = [unpack_big(b) for b in (gb, db, mb, vb)]
    res_s = [unpack_small(b) for b in (gs, ds_, ms, vs)]
    g_conv = lax.dynamic_slice(res_s[0]["conv_w"], (0, xy * 128), (3, 128))
    conv_slots = g_conv.reshape(1, 3, 128)
    cg, cd, cm, cv = adam_slots(conv_slots, conv_w[0], m_conv_w[0], v_conv_w[0], "adam_conv")
    conv_res = [cg, cd, cm, cv]

    loss = lax.psum(loss_part[0, 0], ("x", "y", "c"))
    outs = [loss, dx.reshape(x.shape)]
    for k in range(4):
        for n in names:
            if n == "conv_w":
                outs.append(conv_res[k].reshape(conv_w.shape))
            elif n in BIG:
                outs.append(res[k][n])
            else:
                outs.append(res_s[k][n])
    return tuple(outs)
```

```python
import functools
import math

import jax
import jax.numpy as jnp
from jax import lax
from jax.experimental import pallas as pl
from jax.experimental.pallas import tpu as pltpu

f32 = jnp.float32
bf16 = jnp.bfloat16

D_MODEL = 1024
CONV_WIDTH = 512
SSM_WIDTH = 512
SSM_GROUP = 16
SSM_GROUPS = 32
SSM_STATE = 64
N_STATES = SSM_GROUPS * SSM_STATE
XATTN_HEADS = 4
XATTN_HEAD_DIM = 128
XATTN_WIDTH = 512
D_FF = 4096
GATE_COLS = 3 * D_MODEL
IN_COLS = GATE_COLS + 3 * CONV_WIDTH + SSM_WIDTH + XATTN_WIDTH
ALPHA = 2.0 ** 0.25
LN_EPS = 1e-5
ADAM_LR = 0.001
ADAM_B1 = 0.9
ADAM_B2 = 0.999
ADAM_EPS = 1e-08
ADAM_WD = 0.01
ADAM_STEP = 10

N_SEG = 8
SCAN_ROWS = 256
LANE_STRIP = 512
TM = 512
VMEM_LIMIT = 48 * 1024 * 1024
MESH = pl.DeviceIdType.MESH

NT_DIMS = (((1,), (1,)), ((), ()))
TN_DIMS = (((0,), (0,)), ((), ()))


def _params():
    return pltpu.CompilerParams(vmem_limit_bytes=VMEM_LIMIT)


def _full(shape):
    n = len(shape)
    return pl.BlockSpec(shape, lambda *_: (0,) * n)


def _rows(tm, w, cb=0):
    return pl.BlockSpec((tm, w), lambda i: (i, cb))


def _sig(x):
    return 1.0 / (1.0 + jnp.exp(-x))


HBM_SPEC = pl.BlockSpec(memory_space=pl.ANY)
RIDE_PEERS = {"gather": 3, "scatter": 3, "all": 7, "pair": 1}


def _xy_peers(x, y):
    return [(1 - x, y), (x, 1 - y), (1 - x, 1 - y)]


def _ride_copies(kind, src, dst, send_sems, recv_sems, local_sem):
    x, y, c = lax.axis_index("x"), lax.axis_index("y"), lax.axis_index("c")
    me = 2 * x + y
    if kind == "all":
        flips = [(fx, fy, fc) for fx in (0, 1) for fy in (0, 1) for fc in (0, 1)][1:]
        peers = [(x ^ fx, y ^ fy, c ^ fc) for fx, fy, fc in flips]
        slot = lambda p: 4 * p[0] + 2 * p[1] + p[2]
        mine = slot((x, y, c))
    else:
        peers = [(px, py, c) for px, py in _xy_peers(x, y)]
        slot = lambda p: 2 * p[0] + p[1]
        mine = me

    def remote(k, s, d):
        return pltpu.make_async_remote_copy(src_ref=s, dst_ref=d, send_sem=send_sems.at[k], recv_sem=recv_sems.at[k],
                                            device_id=peers[k], device_id_type=MESH)

    if kind == "pair":
        peers = [(x, y, 1 - c)]
        return None, [remote(0, src, dst)], [remote(0, src, dst)]
    if kind == "scatter":
        local = pltpu.make_async_copy(src.at[me], dst.at[me], local_sem)
        sends = [remote(k, src.at[slot(p)], dst.at[me]) for k, p in enumerate(peers)]
        lands = [remote(k, src.at[me], dst.at[slot(p)]) for k, p in enumerate(peers)]
    else:
        local = pltpu.make_async_copy(src, dst.at[mine], local_sem)
        sends = [remote(k, src, dst.at[mine]) for k, p in enumerate(peers)]
        lands = [remote(k, src, dst.at[slot(p)]) for k, p in enumerate(peers)]
    return local, sends, lands


def _ride_shape(kind, src):
    lead = {"gather": (4,), "scatter": (), "all": (8,), "pair": ()}[kind]
    return jax.ShapeDtypeStruct(lead + src.shape, src.dtype)


def _pcall(body, args, *, name, grid, in_specs, out_specs, out_shape, scratch_shapes=(), rides=(), aliases=None):
    n_in, n_out, n_scr, nr = len(in_specs), len(out_specs), len(scratch_shapes), len(rides)
    kinds = [k for k, _ in rides]

    def wrapped(*refs):
        ins, rsrc = refs[:n_in], refs[n_in:n_in + nr]
        outs = refs[n_in + nr:n_in + nr + n_out]
        rdst = refs[n_in + nr + n_out:n_in + 2 * nr + n_out]
        scr = refs[n_in + 2 * nr + n_out:n_in + 2 * nr + n_out + n_scr]
        sems = refs[n_in + 2 * nr + n_out + n_scr:]
        def start():
            for r, kind in enumerate(kinds):
                local, sends, _ = _ride_copies(kind, rsrc[r], rdst[r], *sems[3 * r:3 * r + 3])
                if local is not None:
                    local.start()
                for cp in sends:
                    cp.start()

        def finish():
            for r, kind in enumerate(kinds):
                local, sends, lands = _ride_copies(kind, rsrc[r], rdst[r], *sems[3 * r:3 * r + 3])
                for cp in lands:
                    cp.wait_recv()
                for cp in sends:
                    cp.wait_send()
                if local is not None:
                    local.wait()

        if nr and grid:
            ids = [pl.program_id(a) for a in range(len(grid))]
            first = functools.reduce(jnp.logical_and, [i == 0 for i in ids])
            last = functools.reduce(jnp.logical_and, [i == g - 1 for i, g in zip(ids, grid)])
            pl.when(first)(start)
        elif nr:
            start()
        body(*ins, *outs, *scr)
        if nr and grid:
            pl.when(last)(finish)
        elif nr:
            finish()

    sems = []
    for kind in kinds:
        n = RIDE_PEERS[kind]
        sems += [pltpu.SemaphoreType.DMA((n,)), pltpu.SemaphoreType.DMA((n,)), pltpu.SemaphoreType.DMA(())]
    res = pl.pallas_call(
        wrapped, name=name, grid=grid, in_specs=list(in_specs) + [HBM_SPEC] * nr,
        out_specs=list(out_specs) + [HBM_SPEC] * nr,
        out_shape=list(out_shape) + [_ride_shape(k, s) for k, s in rides],
        scratch_shapes=list(scratch_shapes) + sems, input_output_aliases=aliases or {},
        compiler_params=_params())(*args, *[s for _, s in rides])
    return list(res[:n_out]), list(res[n_out:])


def _bf(v):
    return v if v.dtype == bf16 else v.astype(bf16)


def mm_fwd(a, w3, name, *, tm=1024, tn=None, bias=None, outs=(f32,), second=None, rides=()):
    M, K = a.shape
    J, _, n = w3.shape
    tm = min(tm, M)
    tn = tn or n
    nl = n // tn
    nb = 0 if bias is None else 1

    def body(*refs):
        a_ref, w_ref = refs[0], refs[1]
        acc = jnp.dot(_bf(a_ref[...]), w_ref[0], preferred_element_type=f32)
        if bias is not None:
            acc = acc + refs[2][...]
        refs[2 + nb][...] = acc.astype(outs[0])
        if len(outs) > 1:
            refs[3 + nb][...] = second(acc).astype(outs[1])

    in_specs = [pl.BlockSpec((tm, K), lambda j, l, i: (i, 0)),
                pl.BlockSpec((1, K, tn), lambda j, l, i: (j, 0, l))]
    args = [a, w3]
    if bias is not None:
        in_specs.append(pl.BlockSpec((1, tn), lambda j, l, i: (0, j * nl + l)))
        args.append(bias)
    res, landed = _pcall(
        body, args, name=name, grid=(J, nl, M // tm), in_specs=in_specs,
        out_specs=[pl.BlockSpec((tm, tn), lambda j, l, i: (i, j * nl + l)) for _ in outs],
        out_shape=[jax.ShapeDtypeStruct((M, J * n), dt) for dt in outs], rides=rides)
    res = res if len(outs) > 1 else res[0]
    return (res, landed) if rides else res


def mm_bwd_x(dy, w3, name, *, tm=TM, epi=None, extras=(), colsum=False, out_dtype=f32, rides=()):
    M = dy.shape[0]
    J, K, n = w3.shape
    tm = min(tm, M)
    nex = len(extras)

    def body(*refs):
        dy_ref, w_hbm = refs[0], refs[1]
        ex = refs[2:2 + nex]
        o_ref = refs[2 + nex]
        w_ref = refs[-1]
        i = pl.program_id(0)

        @pl.when(i == 0)
        def _():
            pltpu.sync_copy(w_hbm, w_ref)

        acc = None
        for j in range(J):
            part = lax.dot_general(_bf(dy_ref[:, j * n:(j + 1) * n]), w_ref[j], NT_DIMS, preferred_element_type=f32)
            acc = part if acc is None else acc + part
        if epi is not None:
            acc = epi(acc, *[e[...] for e in ex])
        o_ref[...] = acc.astype(out_dtype)
        if colsum:
            s_ref = refs[3 + nex]
            cs = jnp.sum(acc, axis=0, keepdims=True)

            @pl.when(i == 0)
            def _():
                s_ref[...] = cs

            @pl.when(i > 0)
            def _():
                s_ref[...] += cs

    in_specs = [pl.BlockSpec((tm, J * n), lambda i: (i, 0)), pl.BlockSpec(memory_space=pl.ANY)]
    in_specs += [pl.BlockSpec((tm, K), lambda i: (i, 0)) for _ in extras]
    out_specs = [pl.BlockSpec((tm, K), lambda i: (i, 0))]
    out_shape = [jax.ShapeDtypeStruct((M, K), out_dtype)]
    if colsum:
        out_specs.append(pl.BlockSpec((1, K), lambda i: (0, 0)))
        out_shape.append(jax.ShapeDtypeStruct((1, K), f32))
    res, landed = _pcall(body, [dy, w3, *extras], name=name, grid=(M // tm,), in_specs=in_specs, out_specs=out_specs,
                         out_shape=out_shape, scratch_shapes=[pltpu.VMEM((J, K, n), bf16)], rides=rides)
    res = res if colsum else res[0]
    return (res, landed) if rides else res


def mm_bwd_w(a, dy, J, name, *, tm=2048, tn=None, tk=None, rides=()):
    M, K = a.shape
    n = dy.shape[1] // J
    tm = min(tm, M)
    tn = tn or n
    tk = tk or K
    nl = n // tn
    nk = K // tk
    ns = M // tm

    def body(a_ref, dy_ref, o_ref, acc_ref):
        s = pl.program_id(3)
        part = lax.dot_general(_bf(a_ref[...]), _bf(dy_ref[...]), TN_DIMS, preferred_element_type=f32)

        @pl.when(s == 0)
        def _():
            acc_ref[...] = part

        @pl.when(s > 0)
        def _():
            acc_ref[...] += part

        @pl.when(s == ns - 1)
        def _():
            o_ref[0] = acc_ref[...].astype(bf16)

    res, landed = _pcall(
        body, [a, dy], name=name, grid=(J, nl, nk, ns),
        in_specs=[pl.BlockSpec((tm, tk), lambda j, l, k, s: (s, k)),
                  pl.BlockSpec((tm, tn), lambda j, l, k, s: (s, j * nl + l))],
        out_specs=[pl.BlockSpec((1, tk, tn), lambda j, l, k, s: (j, k, l))],
        out_shape=[jax.ShapeDtypeStruct((J, K, n), bf16)],
        scratch_shapes=[pltpu.VMEM((tk, tn), f32)], rides=rides)
    return (res[0], landed) if rides else res[0]


def _relu2(v):
    r = jnp.maximum(v, 0.0)
    return r * r


def _shift_down(z, k, halo):
    r = lax.broadcasted_iota(jnp.int32, z.shape, 0)
    y = pltpu.roll(z, k, 0)
    for q in range(k):
        y = jnp.where(r == q, halo[8 - k + q:8 - k + q + 1, :], y)
    return y


def _shift_up(z, k, halo):
    tm = z.shape[0]
    r = lax.broadcasted_iota(jnp.int32, z.shape, 0)
    y = pltpu.roll(z, tm - k, 0)
    for q in range(k):
        y = jnp.where(r == tm - k + q, halo[q:q + 1, :], y)
    return y


def _prev_halo(tm, cb):
    return pl.BlockSpec((8, CONV_WIDTH), lambda i: (jnp.maximum(i * (tm // 8) - 1, 0), cb))


def _next_halo(tm, cb, nrow8):
    return pl.BlockSpec((8, CONV_WIDTH), lambda i: (jnp.minimum((i + 1) * (tm // 8), nrow8 - 1), cb))


def conv_fwd(proj, conv_w, tm=TM):
    S = proj.shape[0]

    def body(cb_ref, cc_ref, ch_ref, cch_ref, chh_ref, w_ref, a_ref):
        i = pl.program_id(0)
        z = cc_ref[...] * ch_ref[...]
        zh = jnp.where(i == 0, 0.0, cch_ref[...] * chh_ref[...])
        w = w_ref[...]
        dwz = w[0:1, :] * _shift_down(z, 2, zh) + w[1:2, :] * _shift_down(z, 1, zh) + w[2:3, :] * z
        a_ref[...] = (cb_ref[...] * dwz).astype(bf16)

    return pl.pallas_call(
        body, name="conv_fwd", grid=(S // tm,),
        in_specs=[_rows(tm, CONV_WIDTH, 6), _rows(tm, CONV_WIDTH, 7), _rows(tm, CONV_WIDTH, 8),
                  _prev_halo(tm, 7), _prev_halo(tm, 8), _full((3, CONV_WIDTH))],
        out_specs=_rows(tm, CONV_WIDTH),
        out_shape=jax.ShapeDtypeStruct((S, CONV_WIDTH), bf16), compiler_params=_params())(
            proj, proj, proj, proj, proj, conv_w)


def conv_bwd(da, proj, conv_w, dproj, tm=TM):
    S = proj.shape[0]
    nt = S // tm

    def body(da_ref, cb_ref, cc_ref, ch_ref, cch_ref, chh_ref, dan_ref, cbn_ref, w_ref, _, o_ref, dw_ref):
        i = pl.program_id(0)
        cc, ch, cb, dav = cc_ref[...], ch_ref[...], cb_ref[...], da_ref[...]
        z = cc * ch
        zh = jnp.where(i == 0, 0.0, cch_ref[...] * chh_ref[...])
        w = w_ref[...]
        z1 = _shift_down(z, 1, zh)
        z2 = _shift_down(z, 2, zh)
        dwz = w[0:1, :] * z2 + w[1:2, :] * z1 + w[2:3, :] * z
        g = dav * cb
        gn = jnp.where(i == nt - 1, 0.0, dan_ref[...] * cbn_ref[...])
        dz = w[2:3, :] * g + w[1:2, :] * _shift_up(g, 1, gn) + w[0:1, :] * _shift_up(g, 2, gn)
        o_ref[:, 0:CONV_WIDTH] = (dav * dwz).astype(bf16)
        o_ref[:, CONV_WIDTH:2 * CONV_WIDTH] = (dz * ch).astype(bf16)
        o_ref[:, 2 * CONV_WIDTH:3 * CONV_WIDTH] = (dz * cc).astype(bf16)

        @pl.when(i == 0)
        def _():
            dw_ref[...] = jnp.zeros_like(dw_ref)

        dw_ref[0:1, :] += jnp.sum(g * z2, axis=0, keepdims=True)
        dw_ref[1:2, :] += jnp.sum(g * z1, axis=0, keepdims=True)
        dw_ref[2:3, :] += jnp.sum(g * z, axis=0, keepdims=True)

    return pl.pallas_call(
        body, name="conv_bwd", grid=(nt,),
        in_specs=[_rows(tm, CONV_WIDTH), _rows(tm, CONV_WIDTH, 6), _rows(tm, CONV_WIDTH, 7), _rows(tm, CONV_WIDTH, 8),
                  _prev_halo(tm, 7), _prev_halo(tm, 8),
                  pl.BlockSpec((8, CONV_WIDTH), lambda i: (jnp.minimum((i + 1) * (tm // 8), S // 8 - 1), 0)),
                  _next_halo(tm, 6, S // 8), _full((3, CONV_WIDTH)), pl.BlockSpec(memory_space=pl.ANY)],
        out_specs=[_rows(tm, 3 * CONV_WIDTH, GATE_COLS // (3 * CONV_WIDTH)), _full((3, CONV_WIDTH))],
        out_shape=[jax.ShapeDtypeStruct(dproj.shape, bf16), jax.ShapeDtypeStruct((3, CONV_WIDTH), f32)],
        input_output_aliases={9: 0},
        compiler_params=_params())(da, proj, proj, proj, proj, proj, da, proj, conv_w, dproj)


def _cmul(ar, ai, br, bi):
    return ar * br - ai * bi, ar * bi + ai * br


def ssm_prep(lr, li, ldt, bt_re, bt_im):
    def body(lr_ref, li_ref, ldt_ref, br_ref, bi_ref, ar_ref, ai_ref, kr_ref, ki_ref, bbr_ref, bbi_ref):
        lrv, liv = lr_ref[...], li_ref[...]
        dt = jnp.exp(ldt_ref[...])
        mag = jnp.exp(lrv * dt)
        ar = mag * jnp.cos(liv * dt)
        ai = mag * jnp.sin(liv * dt)
        den = lrv * lrv + liv * liv
        nr = ar - 1.0
        kr = (nr * lrv + ai * liv) / den
        ki = (ai * lrv - nr * liv) / den
        ar_ref[...], ai_ref[...], kr_ref[...], ki_ref[...] = ar, ai, kr, ki
        bbr_ref[...] = kr * br_ref[...] - ki * bi_ref[...]
        bbi_ref[...] = kr * bi_ref[...] + ki * br_ref[...]

    v = jax.ShapeDtypeStruct((1, N_STATES), f32)
    m = jax.ShapeDtypeStruct((SSM_GROUP, N_STATES), f32)
    return pl.pallas_call(body, name="ssm_prep", out_shape=[v, v, v, v, m, m])(lr, li, ldt, bt_re, bt_im)


def _pow_segment(ar, ai, seg_len):
    pr, pi = ar, ai
    for _ in range(int(math.log2(seg_len))):
        pr, pi = _cmul(pr, pi, pr, pi)
    return pr, pi


GELU_K = math.sqrt(2.0 / math.pi)
GELU_C = 0.044715


def _gelu(v):
    return 0.5 * v * (1.0 + jnp.tanh(GELU_K * (v + GELU_C * v * v * v)))


def _gelu_grad(v):
    t = jnp.tanh(GELU_K * (v + GELU_C * v * v * v))
    return 0.5 * (1.0 + t) + 0.5 * v * (1.0 - t * t) * GELU_K * (1.0 + 3.0 * GELU_C * v * v)


def ssm_scan_fwd(u_perm, bd_r, bd_i, ar, ai, *, final=None, rides=()):
    S = u_perm.shape[0]
    R = SCAN_ROWS
    nblk = S // R
    seg_len = S // N_SEG
    nstrip = N_STATES // LANE_STRIP
    store = final is not None

    def body(*refs):
        if store:
            (u_ref, bdr_ref, bdi_ref, ar_ref, ai_ref, cdr_ref, cdi_ref, d_ref, ir_ref, ii_ref,
             y_ref, ys_ref, sr_ref, si_ref, bur, bui, car, cai) = refs
        else:
            u_ref, bdr_ref, bdi_ref, ar_ref, ai_ref, or_ref, oi_ref, bur, bui, car, cai = refs
        i = pl.program_id(0)
        u = u_ref[...]
        ub = u.astype(bf16)
        for cb in range(4):
            us = ub[:, cb * 128:(cb + 1) * 128]
            bur[:, cb * 512:(cb + 1) * 512] = jnp.dot(us, bdr_ref[cb], preferred_element_type=f32)
            bui[:, cb * 512:(cb + 1) * 512] = jnp.dot(us, bdi_ref[cb], preferred_element_type=f32)

        @pl.when(i == 0)
        def _():
            if store:
                car[...] = ir_ref[...]
                cai[...] = ii_ref[...]
            else:
                car[...] = jnp.zeros_like(car)
                cai[...] = jnp.zeros_like(cai)

        for ls in range(nstrip):
            lanes = pl.ds(ls * LANE_STRIP, LANE_STRIP)
            a_r = jnp.broadcast_to(ar_ref[:, lanes], (N_SEG, LANE_STRIP))
            a_i = jnp.broadcast_to(ai_ref[:, lanes], (N_SEG, LANE_STRIP))

            def step(t, carry, lanes=lanes, a_r=a_r, a_i=a_i):
                s_r, s_i = carry
                row = pl.multiple_of(t * 8, 8)
                n_r = a_r * s_r - a_i * s_i + bur[pl.ds(row, 8), lanes]
                n_i = a_r * s_i + a_i * s_r + bui[pl.ds(row, 8), lanes]
                if store:
                    sr_ref[pl.ds(row, 8), lanes] = n_r
                    si_ref[pl.ds(row, 8), lanes] = n_i
                return n_r, n_i

            e_r, e_i = lax.fori_loop(0, R // 8, step, (car[:, lanes], cai[:, lanes]), unroll=4)
            car[:, lanes] = e_r
            cai[:, lanes] = e_i

        if store:
            for cb in range(4):
                st_r = sr_ref[:, cb * 512:(cb + 1) * 512].astype(bf16)
                st_i = si_ref[:, cb * 512:(cb + 1) * 512].astype(bf16)
                y = (jnp.dot(st_r, cdr_ref[cb], preferred_element_type=f32)
                     - jnp.dot(st_i, cdi_ref[cb], preferred_element_type=f32))
                cols = slice(cb * 128, (cb + 1) * 128)
                y = y + d_ref[:, cols] * u[:, cols]
                y_ref[:, cols] = y
                ys_ref[:, cols] = _gelu(y).astype(bf16)
        else:
            @pl.when(i == nblk - 1)
            def _():
                p_r, p_i = _pow_segment(ar_ref[...], ai_ref[...], seg_len)
                t_r, t_i = car[0:1, :], cai[0:1, :]
                or_ref[0:1, :] = jnp.zeros((1, N_STATES), f32)
                oi_ref[0:1, :] = jnp.zeros((1, N_STATES), f32)
                for j in range(1, N_SEG):
                    or_ref[j:j + 1, :] = t_r
                    oi_ref[j:j + 1, :] = t_i
                    m_r, m_i = _cmul(p_r, p_i, t_r, t_i)
                    t_r, t_i = car[j:j + 1, :] + m_r, cai[j:j + 1, :] + m_i

    blk = lambda w: pl.BlockSpec((R, w), lambda i: (i, 0))
    in_specs = [blk(SSM_WIDTH), _full((4, 128, 512)), _full((4, 128, 512)), _full((1, N_STATES)), _full((1, N_STATES))]
    args = [u_perm, bd_r, bd_i, ar, ai]
    scratch = [pltpu.VMEM((R, N_STATES), f32), pltpu.VMEM((R, N_STATES), f32),
               pltpu.VMEM((N_SEG, N_STATES), f32), pltpu.VMEM((N_SEG, N_STATES), f32)]
    if store:
        in_specs += [_full((4, 512, 128)), _full((4, 512, 128)), _full((1, SSM_WIDTH)),
                     _full((N_SEG, N_STATES)), _full((N_SEG, N_STATES))]
        args += list(final)
        out_specs = [blk(SSM_WIDTH), blk(SSM_WIDTH), blk(N_STATES), blk(N_STATES)]
        out_shape = [jax.ShapeDtypeStruct((S, SSM_WIDTH), f32), jax.ShapeDtypeStruct((S, SSM_WIDTH), bf16),
                     jax.ShapeDtypeStruct((S, N_STATES), f32), jax.ShapeDtypeStruct((S, N_STATES), f32)]
        name = "ssm_scan_fwd"
    else:
        out_specs = [_full((N_SEG, N_STATES)), _full((N_SEG, N_STATES))]
        out_shape = [jax.ShapeDtypeStruct((N_SEG, N_STATES), f32)] * 2
        name = "ssm_scan_fwd_carry"
    res, landed = _pcall(body, args, name=name, grid=(nblk,), in_specs=in_specs, out_specs=out_specs,
                         out_shape=out_shape, scratch_shapes=scratch, rides=rides)
    return (res, landed) if rides else res


def ssm_scan_bwd(dys_perm, y_perm, cdt_r, cdt_i, ar, ai, *, final=None):
    S = dys_perm.shape[0]
    R = SCAN_ROWS
    nblk = S // R
    seg_len = S // N_SEG
    nstrip = N_STATES // LANE_STRIP
    store = final is not None

    def body(*refs):
        if store:
            (dys_ref, y_ref, cdr_ref, cdi_ref, ar_ref, ai_ref, u_ref, sr_ref, si_ref, bdr_ref, bdi_ref, d_ref, ir_ref,
             ii_ref, du_ref, dbr_ref, dbi_ref, dcr_ref, dci_ref, dar_ref, dai_ref, dd_ref, dsr, dsi, lmr, lmi, car,
             cai) = refs
        else:
            dys_ref, y_ref, cdr_ref, cdi_ref, ar_ref, ai_ref, or_ref, oi_ref, dsr, dsi, car, cai = refs
        i = pl.program_id(0)
        dy = dys_ref[...] * _gelu_grad(y_ref[...])
        dyb = dy.astype(bf16)
        for cb in range(4):
            ds_ = dyb[:, cb * 128:(cb + 1) * 128]
            dsr[:, cb * 512:(cb + 1) * 512] = jnp.dot(ds_, cdr_ref[cb], preferred_element_type=f32)
            dsi[:, cb * 512:(cb + 1) * 512] = -jnp.dot(ds_, cdi_ref[cb], preferred_element_type=f32)

        @pl.when(i == 0)
        def _():
            if store:
                car[...] = ir_ref[...]
                cai[...] = ii_ref[...]
                dar_ref[...] = jnp.zeros_like(dar_ref)
                dai_ref[...] = jnp.zeros_like(dai_ref)
                dbr_ref[...] = jnp.zeros_like(dbr_ref)
                dbi_ref[...] = jnp.zeros_like(dbi_ref)
                dcr_ref[...] = jnp.zeros_like(dcr_ref)
                dci_ref[...] = jnp.zeros_like(dci_ref)
                dd_ref[...] = jnp.zeros_like(dd_ref)
            else:
                car[...] = jnp.zeros_like(car)
                cai[...] = jnp.zeros_like(cai)

        for ls in range(nstrip):
            lanes = pl.ds(ls * LANE_STRIP, LANE_STRIP)
            a_r = jnp.broadcast_to(ar_ref[:, lanes], (N_SEG, LANE_STRIP))
            a_i = jnp.broadcast_to(ai_ref[:, lanes], (N_SEG, LANE_STRIP))
            zero = jnp.zeros((N_SEG, LANE_STRIP), f32)

            def step(k, carry, lanes=lanes, a_r=a_r, a_i=a_i):
                l_r, l_i, g_r, g_i = carry
                row = pl.multiple_of((R // 8 - 1 - k) * 8, 8)
                if store:
                    s_r = sr_ref[pl.ds(row, 8), lanes]
                    s_i = si_ref[pl.ds(row, 8), lanes]
                    g_r = g_r + l_r * s_r + l_i * s_i
                    g_i = g_i + l_i * s_r - l_r * s_i
                n_r = dsr[pl.ds(row, 8), lanes] + a_r * l_r + a_i * l_i
                n_i = dsi[pl.ds(row, 8), lanes] + a_r * l_i - a_i * l_r
                if store:
                    lmr[pl.ds(row, 8), lanes] = n_r
                    lmi[pl.ds(row, 8), lanes] = n_i
                return n_r, n_i, g_r, g_i

            e_r, e_i, g_r, g_i = lax.fori_loop(0, R // 8, step, (car[:, lanes], cai[:, lanes], zero, zero), unroll=4)
            car[:, lanes] = e_r
            cai[:, lanes] = e_i
            if store:
                dar_ref[:, lanes] += g_r
                dai_ref[:, lanes] += g_i

        if store:
            u = u_ref[...]
            ub = u.astype(bf16)
            for cb in range(4):
                cols = slice(cb * 128, (cb + 1) * 128)
                st = slice(cb * 512, (cb + 1) * 512)
                l_r = lmr[:, st].astype(bf16)
                l_i = lmi[:, st].astype(bf16)
                du = (jnp.dot(l_r, bdr_ref[cb], preferred_element_type=f32)
                      + jnp.dot(l_i, bdi_ref[cb], preferred_element_type=f32))
                du_ref[:, cols] = (du + d_ref[:, cols] * dy[:, cols]).astype(bf16)
                dbr_ref[cb] += lax.dot_general(l_r, ub[:, cols], TN_DIMS, preferred_element_type=f32)
                dbi_ref[cb] += lax.dot_general(l_i, ub[:, cols], TN_DIMS, preferred_element_type=f32)
                dcr_ref[cb] += lax.dot_general(dyb[:, cols], sr_ref[:, st].astype(bf16), TN_DIMS,
                                               preferred_element_type=f32)
                dci_ref[cb] += lax.dot_general(dyb[:, cols], si_ref[:, st].astype(bf16), TN_DIMS,
                                               preferred_element_type=f32)
            dd_ref[...] += jnp.sum(dy * u, axis=0, keepdims=True)
        else:
            @pl.when(i == nblk - 1)
            def _():
                p_r, p_i = _pow_segment(ar_ref[...], ai_ref[...], seg_len)
                p_i = -p_i
                t_r, t_i = car[N_SEG - 1:N_SEG, :], cai[N_SEG - 1:N_SEG, :]
                or_ref[N_SEG - 1:N_SEG, :] = jnp.zeros((1, N_STATES), f32)
                oi_ref[N_SEG - 1:N_SEG, :] = jnp.zeros((1, N_STATES), f32)
                for j in range(N_SEG - 2, -1, -1):
                    or_ref[j:j + 1, :] = t_r
                    oi_ref[j:j + 1, :] = t_i
                    m_r, m_i = _cmul(p_r, p_i, t_r, t_i)
                    t_r, t_i = car[j:j + 1, :] + m_r, cai[j:j + 1, :] + m_i

    blk = lambda w: pl.BlockSpec((R, w), lambda i: (nblk - 1 - i, 0))
    in_specs = [blk(SSM_WIDTH), blk(SSM_WIDTH), _full((4, 128, 512)), _full((4, 128, 512)), _full((1, N_STATES)),
                _full((1, N_STATES))]
    args = [dys_perm, y_perm, cdt_r, cdt_i, ar, ai]
    seg = jax.ShapeDtypeStruct((N_SEG, N_STATES), f32)
    if store:
        in_specs += [blk(SSM_WIDTH), blk(N_STATES), blk(N_STATES), _full((4, 512, 128)), _full((4, 512, 128)),
                     _full((1, SSM_WIDTH)), _full((N_SEG, N_STATES)), _full((N_SEG, N_STATES))]
        args += list(final)
        out_specs = [blk(SSM_WIDTH), _full((4, 512, 128)), _full((4, 512, 128)), _full((4, 128, 512)),
                     _full((4, 128, 512)), _full((N_SEG, N_STATES)), _full((N_SEG, N_STATES)), _full((1, SSM_WIDTH))]
        b_acc = jax.ShapeDtypeStruct((4, 512, 128), f32)
        c_acc = jax.ShapeDtypeStruct((4, 128, 512), f32)
        out_shape = [jax.ShapeDtypeStruct((S, SSM_WIDTH), bf16), b_acc, b_acc, c_acc, c_acc, seg, seg,
                     jax.ShapeDtypeStruct((1, SSM_WIDTH), f32)]
        scratch = [pltpu.VMEM((R, N_STATES), f32)] * 4 + [pltpu.VMEM((N_SEG, N_STATES), f32)] * 2
        name = "ssm_scan_bwd"
    else:
        out_specs = [_full((N_SEG, N_STATES)), _full((N_SEG, N_STATES))]
        out_shape = [seg, seg]
        scratch = [pltpu.VMEM((R, N_STATES), f32)] * 2 + [pltpu.VMEM((N_SEG, N_STATES), f32)] * 2
        name = "ssm_scan_bwd_carry"
    return pl.pallas_call(body, name=name, grid=(nblk,), in_specs=in_specs, out_specs=out_specs,
                          out_shape=out_shape, scratch_shapes=scratch, compiler_params=_params())(*args)


def ssm_param_bwd(dbb_r, dbb_i, bt_re, bt_im, kr, ki, ar, ai, lr, li, ldt, da_r, da_i):
    def body(dbr_ref, dbi_ref, br_ref, bi_ref, kr_ref, ki_ref, ar_ref, ai_ref, lr_ref, li_ref, ldt_ref, dar_ref,
             dai_ref, obr_ref, obi_ref, olr_ref, oli_ref, odt_ref):
        dbr, dbi, b_r, b_i = dbr_ref[...], dbi_ref[...], br_ref[...], bi_ref[...]
        k_r, k_i, a_r, a_i = kr_ref[...], ki_ref[...], ar_ref[...], ai_ref[...]
        l_r, l_i = lr_ref[...], li_ref[...]
        dt = jnp.exp(ldt_ref[...])
        obr_ref[...] = k_r * dbr + k_i * dbi
        obi_ref[...] = k_r * dbi - k_i * dbr
        gk_r = jnp.sum(dbr * b_r + dbi * b_i, axis=0, keepdims=True)
        gk_i = jnp.sum(dbi * b_r - dbr * b_i, axis=0, keepdims=True)
        ga_r = jnp.sum(dar_ref[...], axis=0, keepdims=True)
        ga_i = jnp.sum(dai_ref[...], axis=0, keepdims=True)
        den = l_r * l_r + l_i * l_i
        c_r, c_i = l_r / den, l_i / den
        m_r, m_i = _cmul(c_r, c_i, gk_r, gk_i)
        g_r, g_i = ga_r + m_r, ga_i + m_i
        t1_r, t1_i = _cmul(dt * a_r, -dt * a_i, g_r, g_i)
        q_r, q_i = _cmul(k_r, k_i, c_r, -c_i)
        t2_r, t2_i = _cmul(-q_r, q_i, gk_r, gk_i)
        olr_ref[...] = t1_r + t2_r
        oli_ref[...] = t1_i + t2_i
        w_r, w_i = _cmul(l_r, l_i, a_r, a_i)
        odt_ref[...] = dt * (w_r * g_r + w_i * g_i)

    v = jax.ShapeDtypeStruct((1, N_STATES), f32)
    m = jax.ShapeDtypeStruct((SSM_GROUP, N_STATES), f32)
    return pl.pallas_call(body, name="ssm_param_bwd", out_shape=[m, m, v, v, v])(
        dbb_r, dbb_i, bt_re, bt_im, kr, ki, ar, ai, lr, li, ldt, da_r, da_i)


def group_sum(v):
    def body(v_ref, o_ref):
        o_ref[...] = jnp.sum(v_ref[...], axis=-1, keepdims=True)
    return pl.pallas_call(body, name="group_sum", out_shape=jax.ShapeDtypeStruct((v.shape[0], 1), f32))(v)


ATTN_SCALE = XATTN_HEAD_DIM ** -0.5


def _attn_probs(q_h, k_h):
    s = lax.dot_general(q_h, k_h, NT_DIMS, preferred_element_type=f32) * ATTN_SCALE
    e = jnp.exp(s - jnp.max(s, axis=-1, keepdims=True))
    return e / jnp.sum(e, axis=-1, keepdims=True)


def attn_fwd(proj, kv, tm=TM):
    S = proj.shape[0]
    M = kv.shape[0]

    def body(q_ref, kv_ref, o_ref):
        for h in range(XATTN_HEADS):
            cols = slice(h * XATTN_HEAD_DIM, (h + 1) * XATTN_HEAD_DIM)
            q_h = q_ref[:, cols].astype(bf16)
            k_h = kv_ref[:, cols]
            v_h = kv_ref[:, XATTN_WIDTH + h * XATTN_HEAD_DIM:XATTN_WIDTH + (h + 1) * XATTN_HEAD_DIM]
            p = _attn_probs(q_h, k_h)
            o_ref[:, cols] = jnp.dot(p.astype(bf16), v_h, preferred_element_type=f32).astype(bf16)

    return pl.pallas_call(
        body, name="attn_fwd", grid=(S // tm,), in_specs=[_rows(tm, XATTN_WIDTH, 10), _full((M, 2 * XATTN_WIDTH))],
        out_specs=_rows(tm, XATTN_WIDTH), out_shape=jax.ShapeDtypeStruct((S, XATTN_WIDTH), bf16),
        compiler_params=_params())(proj, kv)


def attn_bwd(do, proj, kv, dproj, tm=TM):
    S = proj.shape[0]
    M = kv.shape[0]

    def body(do_ref, q_ref, kv_ref, _, dq_ref, dkv_ref):
        i = pl.program_id(0)

        @pl.when(i == 0)
        def _():
            dkv_ref[...] = jnp.zeros_like(dkv_ref)

        for h in range(XATTN_HEADS):
            cols = slice(h * XATTN_HEAD_DIM, (h + 1) * XATTN_HEAD_DIM)
            vcols = slice(XATTN_WIDTH + h * XATTN_HEAD_DIM, XATTN_WIDTH + (h + 1) * XATTN_HEAD_DIM)
            q_h = q_ref[:, cols].astype(bf16)
            k_h = kv_ref[:, cols]
            v_h = kv_ref[:, vcols]
            do_h = do_ref[:, cols]
            p = _attn_probs(q_h, k_h)
            dp = lax.dot_general(do_h, v_h, NT_DIMS, preferred_element_type=f32)
            ds = (p * (dp - jnp.sum(dp * p, axis=-1, keepdims=True)) * ATTN_SCALE).astype(bf16)
            dq_ref[:, cols] = jnp.dot(ds, k_h, preferred_element_type=f32).astype(bf16)
            dkv_ref[:, cols] += lax.dot_general(ds, q_h, TN_DIMS, preferred_element_type=f32)
            dkv_ref[:, vcols] += lax.dot_general(p.astype(bf16), do_h, TN_DIMS, preferred_element_type=f32)

    return pl.pallas_call(
        body, name="attn_bwd", grid=(S // tm,),
        in_specs=[_rows(tm, XATTN_WIDTH), _rows(tm, XATTN_WIDTH, 10), _full((M, 2 * XATTN_WIDTH)),
                  pl.BlockSpec(memory_space=pl.ANY)],
        out_specs=[_rows(tm, XATTN_WIDTH, 10), _full((M, 2 * XATTN_WIDTH))],
        out_shape=[jax.ShapeDtypeStruct(dproj.shape, bf16), jax.ShapeDtypeStruct((M, 2 * XATTN_WIDTH), f32)],
        input_output_aliases={3: 0},
        compiler_params=_params())(do, proj, kv, dproj)


def merge_fwd(proj, b_gate, y_a, glu, y_c, tm=TM, rides=()):
    S = proj.shape[0]

    def body(g0_ref, g1_ref, g2_ref, b_ref, ya_ref, ga_ref, gb_ref, yc_ref, o_ref):
        b = b_ref[...]
        g0 = _sig(g0_ref[...] + b[:, 0:D_MODEL])
        g1 = _sig(g1_ref[...] + b[:, D_MODEL:2 * D_MODEL])
        g2 = _sig(g2_ref[...] + b[:, 2 * D_MODEL:3 * D_MODEL])
        y_b = ga_ref[...] * _sig(gb_ref[...])
        o_ref[...] = (g0 * ya_ref[...] + g1 * y_b + g2 * yc_ref[...]).astype(bf16)

    res, landed = _pcall(
        body, [proj, proj, proj, b_gate, y_a, glu, glu, y_c], name="merge_fwd", grid=(S // tm,),
        in_specs=[_rows(tm, D_MODEL, 0), _rows(tm, D_MODEL, 1), _rows(tm, D_MODEL, 2), _full((1, GATE_COLS)),
                  _rows(tm, D_MODEL), _rows(tm, D_MODEL, 0), _rows(tm, D_MODEL, 1), _rows(tm, D_MODEL)],
        out_specs=[_rows(tm, D_MODEL)], out_shape=[jax.ShapeDtypeStruct((S, D_MODEL), bf16)], rides=rides)
    return (res[0], landed) if rides else res[0]


def merge_bwd(dm, proj, b_gate, y_a, glu, y_c, tm=256):
    S = proj.shape[0]

    def body(dm_ref, g0_ref, g1_ref, g2_ref, b_ref, ya_ref, ga_ref, gb_ref, yc_ref,
             dg_ref, dya_ref, dgl_ref, dyc_ref, dbg_ref):
        i = pl.program_id(0)
        b = b_ref[...]
        dmv = dm_ref[...]
        ga, sb = ga_ref[...], _sig(gb_ref[...])
        ys = (ya_ref[...], ga * sb, yc_ref[...])
        gs = tuple(_sig(r[...] + b[:, k * D_MODEL:(k + 1) * D_MODEL]) for k, r in enumerate((g0_ref, g1_ref, g2_ref)))

        @pl.when(i == 0)
        def _():
            dbg_ref[...] = jnp.zeros_like(dbg_ref)

        for k in range(3):
            dpre = dmv * ys[k] * gs[k] * (1.0 - gs[k])
            dg_ref[:, k * D_MODEL:(k + 1) * D_MODEL] = dpre.astype(bf16)
            dbg_ref[:, k * D_MODEL:(k + 1) * D_MODEL] += jnp.sum(dpre, axis=0, keepdims=True)
        dya_ref[...] = (dmv * gs[0]).astype(bf16)
        dyc_ref[...] = (dmv * gs[2]).astype(bf16)
        dyb = dmv * gs[1]
        dgl_ref[:, 0:D_MODEL] = (dyb * sb).astype(bf16)
        dgl_ref[:, D_MODEL:2 * D_MODEL] = (dyb * ga * sb * (1.0 - sb)).astype(bf16)

    return pl.pallas_call(
        body, name="merge_bwd", grid=(S // tm,),
        in_specs=[_rows(tm, D_MODEL), _rows(tm, D_MODEL, 0), _rows(tm, D_MODEL, 1), _rows(tm, D_MODEL, 2),
                  _full((1, GATE_COLS)), _rows(tm, D_MODEL), _rows(tm, D_MODEL, 0), _rows(tm, D_MODEL, 1),
                  _rows(tm, D_MODEL)],
        out_specs=[_rows(tm, GATE_COLS), _rows(tm, D_MODEL), _rows(tm, 2 * D_MODEL), _rows(tm, D_MODEL),
                   _full((1, GATE_COLS))],
        out_shape=[jax.ShapeDtypeStruct((S, IN_COLS), bf16), jax.ShapeDtypeStruct((S, D_MODEL), bf16),
                   jax.ShapeDtypeStruct((S, 2 * D_MODEL), bf16), jax.ShapeDtypeStruct((S, D_MODEL), bf16),
                   jax.ShapeDtypeStruct((1, GATE_COLS), f32)],
        compiler_params=_params())(dm, proj, proj, proj, b_gate, y_a, glu, glu, y_c)


def _ln_stats(r):
    mu = jnp.mean(r, axis=-1, keepdims=True)
    xc = r - mu
    var = jnp.mean(xc * xc, axis=-1, keepdims=True)
    rstd = lax.rsqrt(var + LN_EPS)
    return xc * rstd, rstd


def _ln_bwd(dy, xhat, rstd, g):
    dxh = dy * g
    return rstd * (dxh - jnp.mean(dxh, axis=-1, keepdims=True) - xhat * jnp.mean(dxh * xhat, axis=-1, keepdims=True))


def ln1_fwd(x, t1, g, b, tm=TM):
    S = x.shape[0]

    def body(x_ref, t_ref, g_ref, b_ref, o_ref, ob_ref):
        xhat, _ = _ln_stats(ALPHA * x_ref[...] + t_ref[...])
        h = xhat * g_ref[...] + b_ref[...]
        o_ref[...] = h
        ob_ref[...] = h.astype(bf16)

    return pl.pallas_call(
        body, name="ln1_fwd", grid=(S // tm,),
        in_specs=[_rows(tm, D_MODEL), _rows(tm, D_MODEL), _full((1, D_MODEL)), _full((1, D_MODEL))],
        out_specs=[_rows(tm, D_MODEL), _rows(tm, D_MODEL)],
        out_shape=[jax.ShapeDtypeStruct((S, D_MODEL), f32), jax.ShapeDtypeStruct((S, D_MODEL), bf16)],
        compiler_params=_params())(x, t1, g, b)


def ln2_loss_bwd(h1, t2, b_down, g, b, target, tm=TM):
    S = h1.shape[0]

    def body(h_ref, t_ref, bd_ref, g_ref, b_ref, y_ref, dr_ref, drb_ref, l_ref, dg_ref, db_ref, dbd_ref):
        i = pl.program_id(0)
        xhat, rstd = _ln_stats(ALPHA * h_ref[...] + t_ref[...] + bd_ref[...])
        gv = g_ref[...]
        err = xhat * gv + b_ref[...] - y_ref[...]
        dout = err * (1.0 / D_MODEL)
        dr = _ln_bwd(dout, xhat, rstd, gv)
        dr_ref[...] = dr
        drb_ref[...] = dr.astype(bf16)
        part = 0.5 * jnp.sum(jnp.sum(err * err, axis=-1, keepdims=True) * (1.0 / D_MODEL), axis=0, keepdims=True)

        @pl.when(i == 0)
        def _():
            l_ref[...] = jnp.zeros_like(l_ref)
            dg_ref[...] = jnp.zeros_like(dg_ref)
            db_ref[...] = jnp.zeros_like(db_ref)
            dbd_ref[...] = jnp.zeros_like(dbd_ref)

        l_ref[...] += jnp.broadcast_to(part, l_ref.shape)
        dg_ref[...] += jnp.sum(dout * xhat, axis=0, keepdims=True)
        db_ref[...] += jnp.sum(dout, axis=0, keepdims=True)
        dbd_ref[...] += jnp.sum(dr, axis=0, keepdims=True)

    vec = jax.ShapeDtypeStruct((1, D_MODEL), f32)
    return pl.pallas_call(
        body, name="ln2_loss_bwd", grid=(S // tm,),
        in_specs=[_rows(tm, D_MODEL), _rows(tm, D_MODEL), _full((1, D_MODEL)), _full((1, D_MODEL)),
                  _full((1, D_MODEL)), _rows(tm, D_MODEL)],
        out_specs=[_rows(tm, D_MODEL), _rows(tm, D_MODEL), _full((1, 128)), _full((1, D_MODEL)), _full((1, D_MODEL)),
                   _full((1, D_MODEL))],
        out_shape=[jax.ShapeDtypeStruct((S, D_MODEL), f32), jax.ShapeDtypeStruct((S, D_MODEL), bf16),
                   jax.ShapeDtypeStruct((1, 128), f32), vec, vec, vec],
        compiler_params=_params())(h1, t2, b_down, g, b, target)


def ln1_bwd(x, t1, dr2, t3, g, tm=TM):
    S = x.shape[0]

    def body(x_ref, t_ref, dr2_ref, t3_ref, g_ref, dr_ref, drb_ref, dg_ref, db_ref):
        i = pl.program_id(0)
        xhat, rstd = _ln_stats(ALPHA * x_ref[...] + t_ref[...])
        dh = ALPHA * dr2_ref[...] + t3_ref[...]
        dr = _ln_bwd(dh, xhat, rstd, g_ref[...])
        dr_ref[...] = dr
        drb_ref[...] = dr.astype(bf16)

        @pl.when(i == 0)
        def _():
            dg_ref[...] = jnp.zeros_like(dg_ref)
            db_ref[...] = jnp.zeros_like(db_ref)

        dg_ref[...] += jnp.sum(dh * xhat, axis=0, keepdims=True)
        db_ref[...] += jnp.sum(dh, axis=0, keepdims=True)

    vec = jax.ShapeDtypeStruct((1, D_MODEL), f32)
    return pl.pallas_call(
        body, name="ln1_bwd", grid=(S // tm,),
        in_specs=[_rows(tm, D_MODEL)] * 4 + [_full((1, D_MODEL))],
        out_specs=[_rows(tm, D_MODEL), _rows(tm, D_MODEL), _full((1, D_MODEL)), _full((1, D_MODEL))],
        out_shape=[jax.ShapeDtypeStruct((S, D_MODEL), f32), jax.ShapeDtypeStruct((S, D_MODEL), bf16), vec, vec],
        compiler_params=_params())(x, t1, dr2, t3, g)


def exchange(*rides, name):
    return _pcall(lambda: None, [], name=name, grid=(), in_specs=[], out_specs=[], out_shape=[], rides=rides)[1]


ROW_TILE = 256


def sum_slots(recv, name):
    n, R, C = recv.shape
    tm = min(R, ROW_TILE)

    def body(r_ref, o_ref):
        acc = r_ref[0].astype(f32)
        for k in range(1, n):
            acc = acc + r_ref[k].astype(f32)
        o_ref[...] = acc

    return pl.pallas_call(
        body, name=name, grid=(R // tm,), in_specs=[pl.BlockSpec((n, tm, C), lambda i: (0, i, 0))],
        out_specs=_rows(tm, C), out_shape=jax.ShapeDtypeStruct((R, C), f32), compiler_params=_params())(recv)


def _adamw(w, g, m, v):
    m = ADAM_B1 * m + (1.0 - ADAM_B1) * g
    v = ADAM_B2 * v + (1.0 - ADAM_B2) * (g * g)
    m_hat = m / (1.0 - ADAM_B1 ** ADAM_STEP)
    v_hat = v / (1.0 - ADAM_B2 ** ADAM_STEP)
    delta = -ADAM_LR * (m_hat / (jnp.sqrt(v_hat) + ADAM_EPS) + ADAM_WD * w)
    return delta, m, v


def adam_pair(p, q, w, m, v, name):
    R, C = w.shape
    tm = min(R, ROW_TILE)

    def body(p_ref, q_ref, w_ref, m_ref, v_ref, g_ref, d_ref, nm_ref, nv_ref):
        g = p_ref[...] + q_ref[...]
        g_ref[...] = g
        d_ref[...], nm_ref[...], nv_ref[...] = _adamw(w_ref[...], g, m_ref[...], v_ref[...])

    o = jax.ShapeDtypeStruct((R, C), f32)
    return pl.pallas_call(body, name=name, grid=(R // tm,), in_specs=[_rows(tm, C)] * 5,
                          out_specs=[_rows(tm, C)] * 4, out_shape=[o] * 4, compiler_params=_params())(p, q, w, m, v)


def adam_slots(recv, w, m, v, name):
    n = recv.shape[0]

    def body(r_ref, w_ref, m_ref, v_ref, g_ref, d_ref, nm_ref, nv_ref):
        g = r_ref[0]
        for k in range(1, n):
            g = g + r_ref[k]
        g_ref[...] = g
        d_ref[...], nm_ref[...], nv_ref[...] = _adamw(w_ref[...], g, m_ref[...], v_ref[...])

    o = jax.ShapeDtypeStruct(w.shape, f32)
    return pl.pallas_call(body, name=name, out_shape=[o] * 4, compiler_params=_params())(recv, w, m, v)


BIG = ("w_in", "w_conv_out", "w_glu", "w_xattn_out", "w_kv", "w_out", "w_up", "w_down")
MID = ("w_conv_out", "w_glu", "w_xattn_out", "w_kv", "w_out")
SMALL = ("b_gate", "ssm_lam_re", "ssm_lam_im", "ssm_log_dt", "ssm_b_re", "ssm_b_im", "ssm_c_re", "ssm_c_im", "ssm_d",
         "ln1_g", "ln1_b", "b_up", "b_down", "ln2_g", "ln2_b")
def _pad_flat(a, mult=1024):
    a = a.reshape(-1)
    return jnp.pad(a, (0, (-a.shape[0]) % mult))


def _perm(a):
    S, W = a.shape
    return a.reshape(N_SEG, S // N_SEG, W).transpose(1, 0, 2).reshape(S, W)


def _unperm(a):
    S, W = a.shape
    return a.reshape(S // N_SEG, N_SEG, W).transpose(1, 0, 2).reshape(S, W)


def _state_rows(a):
    return a.transpose(2, 0, 1).reshape(SSM_GROUP, N_STATES)


def _block_diag_b(bt):
    b4 = bt.reshape(SSM_GROUP, 4, 8, SSM_STATE)
    eye = jnp.eye(8, dtype=bt.dtype)
    return jnp.einsum("hcgp,gk->cghkp", b4, eye).reshape(4, 128, 512)


def _block_diag_c(c):
    c4 = c.reshape(4, 8, SSM_GROUP, SSM_STATE)
    eye = jnp.eye(8, dtype=c.dtype)
    return jnp.einsum("cghp,gk->cgpkh", c4, eye).reshape(4, 512, 128)


def _diag_b(acc):
    a = acc.reshape(4, 8, SSM_STATE, 8, SSM_GROUP)
    eye = jnp.eye(8, dtype=acc.dtype)
    return jnp.einsum("cgpkh,gk->hcgp", a, eye).reshape(SSM_GROUP, N_STATES)


def _diag_c(acc):
    a = acc.reshape(4, 8, SSM_GROUP, 8, SSM_STATE)
    eye = jnp.eye(8, dtype=acc.dtype)
    return jnp.einsum("cghkp,gk->cghp", a, eye).reshape(SSM_GROUPS, SSM_GROUP, SSM_STATE)


def kernel(x, mem, w_in, b_gate, conv_w, w_conv_out, ssm_lam_re, ssm_lam_im, ssm_log_dt, ssm_b_re, ssm_b_im, ssm_c_re, ssm_c_im, ssm_d, w_glu, w_kv, w_xattn_out, w_out, ln1_g, ln1_b, w_up, b_up, w_down, b_down, ln2_g, ln2_b, loss_target, m_w_in, m_b_gate, m_conv_w, m_w_conv_out, m_ssm_lam_re, m_ssm_lam_im, m_ssm_log_dt, m_ssm_b_re, m_ssm_b_im, m_ssm_c_re, m_ssm_c_im, m_ssm_d, m_w_glu, m_w_kv, m_w_xattn_out, m_w_out, m_ln1_g, m_ln1_b, m_w_up, m_b_up, m_w_down, m_b_down, m_ln2_g, m_ln2_b, v_w_in, v_b_gate, v_conv_w, v_w_conv_out, v_ssm_lam_re, v_ssm_lam_im, v_ssm_log_dt, v_ssm_b_re, v_ssm_b_im, v_ssm_c_re, v_ssm_c_im, v_ssm_d, v_w_glu, v_w_kv, v_w_xattn_out, v_w_out, v_ln1_g, v_ln1_b, v_w_up, v_b_up, v_w_down, v_b_down, v_ln2_g, v_ln2_b):
    W = dict(w_in=w_in, b_gate=b_gate, conv_w=conv_w, w_conv_out=w_conv_out, ssm_lam_re=ssm_lam_re,
             ssm_lam_im=ssm_lam_im, ssm_log_dt=ssm_log_dt, ssm_b_re=ssm_b_re, ssm_b_im=ssm_b_im, ssm_c_re=ssm_c_re,
             ssm_c_im=ssm_c_im, ssm_d=ssm_d, w_glu=w_glu, w_kv=w_kv, w_xattn_out=w_xattn_out, w_out=w_out,
             ln1_g=ln1_g, ln1_b=ln1_b, w_up=w_up, b_up=b_up, w_down=w_down, b_down=b_down, ln2_g=ln2_g, ln2_b=ln2_b)
    MOM = dict(w_in=m_w_in, b_gate=m_b_gate, conv_w=m_conv_w, w_conv_out=m_w_conv_out, ssm_lam_re=m_ssm_lam_re,
               ssm_lam_im=m_ssm_lam_im, ssm_log_dt=m_ssm_log_dt, ssm_b_re=m_ssm_b_re, ssm_b_im=m_ssm_b_im,
               ssm_c_re=m_ssm_c_re, ssm_c_im=m_ssm_c_im, ssm_d=m_ssm_d, w_glu=m_w_glu, w_kv=m_w_kv,
               w_xattn_out=m_w_xattn_out, w_out=m_w_out, ln1_g=m_ln1_g, ln1_b=m_ln1_b, w_up=m_w_up, b_up=m_b_up,
               w_down=m_w_down, b_down=m_b_down, ln2_g=m_ln2_g, ln2_b=m_ln2_b)
    VEL = dict(w_in=v_w_in, b_gate=v_b_gate, conv_w=v_conv_w, w_conv_out=v_w_conv_out, ssm_lam_re=v_ssm_lam_re,
               ssm_lam_im=v_ssm_lam_im, ssm_log_dt=v_ssm_log_dt, ssm_b_re=v_ssm_b_re, ssm_b_im=v_ssm_b_im,
               ssm_c_re=v_ssm_c_re, ssm_c_im=v_ssm_c_im, ssm_d=v_ssm_d, w_glu=v_w_glu, w_kv=v_w_kv,
               w_xattn_out=v_w_xattn_out, w_out=v_w_out, ln1_g=v_ln1_g, ln1_b=v_ln1_b, w_up=v_w_up, b_up=v_b_up,
               w_down=v_w_down, b_down=v_b_down, ln2_g=v_ln2_g, ln2_b=v_ln2_b)
    names = list(W)
    xy = 2 * lax.axis_index("x") + lax.axis_index("y")

    xs = x[0]
    S = xs.shape[0]
    mems = mem[0]
    tgt = loss_target[0]

    shard_bf = {n: W[n][0].astype(bf16) for n in BIG}
    win3, = exchange(("gather", shard_bf["w_in"]), name="gather_w_in")

    lr = ssm_lam_re.reshape(1, N_STATES)
    li = ssm_lam_im.reshape(1, N_STATES)
    ldt = jnp.repeat(ssm_log_dt.reshape(SSM_GROUPS), SSM_STATE).reshape(1, N_STATES)
    bt_re, bt_im = _state_rows(ssm_b_re[0]), _state_rows(ssm_b_im[0])
    ar, ai, kr, ki, bbt_r, bbt_i = ssm_prep(lr, li, ldt, bt_re, bt_im)
    bd_r, bd_i = _block_diag_b(bbt_r).astype(bf16), _block_diag_b(bbt_i).astype(bf16)
    cd_r, cd_i = _block_diag_c(ssm_c_re[0]).astype(bf16), _block_diag_c(ssm_c_im[0]).astype(bf16)
    bdt_r, bdt_i = bd_r.transpose(0, 2, 1), bd_i.transpose(0, 2, 1)
    cdt_r, cdt_i = cd_r.transpose(0, 2, 1), cd_i.transpose(0, 2, 1)
    d_skip = ssm_d.reshape(1, SSM_WIDTH)

    x_bf = xs.astype(bf16)
    mem_bf = mems.astype(bf16)
    u_col = GATE_COLS + 3 * CONV_WIDTH
    proj, (wco3, wglu3, wxo3, wkv4, wout4, convw4) = mm_fwd(
        x_bf, win3, "proj", rides=tuple(("gather", shard_bf[n]) for n in MID) + (("gather", conv_w[0]),))
    wkv3 = wkv4.reshape(1, D_MODEL, 2 * XATTN_WIDTH)
    wout3 = wout4.reshape(1, D_MODEL, D_MODEL)
    convw_full = convw4.transpose(1, 0, 2).reshape(3, CONV_WIDTH)
    a_conv = conv_fwd(proj, convw_full)
    y_a = mm_fwd(a_conv, wco3, "conv_out")
    u_perm = _perm(proj[:, u_col:u_col + SSM_WIDTH])
    init_r, init_i = ssm_scan_fwd(u_perm, bd_r, bd_i, ar, ai)
    (ysm_perm, ys_perm, st_r, st_i), (wup3,) = ssm_scan_fwd(
        u_perm, bd_r, bd_i, ar, ai, final=(cd_r, cd_i, d_skip, init_r, init_i), rides=(("gather", shard_bf["w_up"]),))
    y_s = _unperm(ys_perm)
    glu = mm_fwd(y_s, wglu3, "glu")
    kv = mm_fwd(mem_bf, wkv3, "kv", outs=(bf16,))
    o_att = attn_fwd(proj, kv)
    y_c = mm_fwd(o_att, wxo3, "xattn_out")
    merged, (wdn4,) = merge_fwd(proj, b_gate, y_a, glu, y_c, rides=(("gather", shard_bf["w_down"]),))
    wdn3 = wdn4.reshape(1, D_FF, D_MODEL)
    t1 = mm_fwd(merged, wout3, "w_out")
    h1, h1_bf = ln1_fwd(xs, t1, ln1_g, ln1_b)
    up, hdn = mm_fwd(h1_bf, wup3, "w_up", bias=b_up, outs=(f32, bf16), second=_relu2)
    t2 = mm_fwd(hdn, wdn3, "w_down", tn=512)

    dr2, dr2_bf, loss_part, d_ln2_g, d_ln2_b, d_b_down = ln2_loss_bwd(h1, t2, b_down, ln2_g, ln2_b, tgt)
    dup, d_b_up = mm_bwd_x(dr2_bf, wdn3, "dup", tm=256, extras=(up,), colsum=True, out_dtype=bf16,
                           epi=lambda acc, upv: acc * (2.0 * jnp.maximum(upv, 0.0)))
    t3 = mm_bwd_x(dup, wup3, "dh1")
    dr1, dr1_bf, d_ln1_g, d_ln1_b = ln1_bwd(xs, t1, dr2, t3, ln1_g)
    g_w_out = mm_bwd_w(merged, dr1_bf, 1, "dw_out").reshape(4, -1, D_MODEL)
    dmerged, (recv_out,) = mm_bwd_x(dr1_bf, wout3, "dmerged", tm=1024, rides=(("scatter", g_w_out),))
    dproj, dy_a, dglu, dy_c, d_b_gate = merge_bwd(dmerged, proj, b_gate, y_a, glu, y_c)

    g_w_co = mm_bwd_w(a_conv, dy_a, 4, "dw_conv_out")
    da_conv = mm_bwd_x(dy_a, wco3, "da_conv", tm=1024)
    dproj, d_conv_w = conv_bwd(da_conv, proj, convw_full, dproj)

    g_w_glu = mm_bwd_w(y_s, dglu, 4, "dw_glu")
    dy_s, (recv_co,) = mm_bwd_x(dglu, wglu3, "dy_s", tm=1024, rides=(("scatter", g_w_co),))
    dys_perm = _perm(dy_s)
    linit_r, linit_i = ssm_scan_bwd(dys_perm, ysm_perm, cdt_r, cdt_i, ar, ai)
    du_perm, dbacc_r, dbacc_i, dcacc_r, dcacc_i, da_r, da_i, d_ssm_d = ssm_scan_bwd(
        dys_perm, ysm_perm, cdt_r, cdt_i, ar, ai,
        final=(u_perm, st_r, st_i, bdt_r, bdt_i, d_skip, linit_r, linit_i))
    dbt_re, dbt_im, d_lr, d_li, d_ldt_state = ssm_param_bwd(
        _diag_b(dbacc_r), _diag_b(dbacc_i), bt_re, bt_im, kr, ki, ar, ai, lr, li, ldt, da_r, da_i)
    d_log_dt = group_sum(d_ldt_state.reshape(SSM_GROUPS, SSM_STATE))
    d_b_re = dbt_re.reshape(SSM_GROUP, SSM_GROUPS, SSM_STATE).transpose(1, 2, 0)
    d_b_im = dbt_im.reshape(SSM_GROUP, SSM_GROUPS, SSM_STATE).transpose(1, 2, 0)
    d_c_re = _diag_c(dcacc_r)
    d_c_im = -_diag_c(dcacc_i)

    g_w_xo = mm_bwd_w(o_att, dy_c, 4, "dw_xattn_out")
    do_att = mm_bwd_x(dy_c, wxo3, "do_att", tm=1024, out_dtype=bf16)
    dproj, dkv = attn_bwd(do_att, proj, kv, dproj)
    g_w_kv = mm_bwd_w(mem_bf, dkv, 1, "dw_kv").reshape(4, -1, D_MODEL)

    dproj = lax.dynamic_update_slice(dproj, _unperm(du_perm), (0, u_col))
    g_small = {"b_gate": d_b_gate, "ssm_lam_re": d_lr, "ssm_lam_im": d_li, "ssm_log_dt": d_log_dt, "ssm_b_re": d_b_re,
               "ssm_b_im": d_b_im, "ssm_c_re": d_c_re, "ssm_c_im": d_c_im, "ssm_d": d_ssm_d, "ln1_g": d_ln1_g,
               "ln1_b": d_ln1_b, "b_up": d_b_up, "b_down": d_b_down, "ln2_g": d_ln2_g, "ln2_b": d_ln2_b}
    small_names = SMALL + ("conv_w",)
    g_small["conv_w"] = d_conv_w
    sizes = {n: (g_small[n].size + 1023) // 1024 * 1024 for n in small_names}
    pack = lambda d: jnp.concatenate([_pad_flat(d[n]) for n in small_names]).reshape(-1, 128)
    conv_zero = jnp.zeros((3, CONV_WIDTH), f32)
    part, other = {}, {}
    part["w_out"] = sum_slots(recv_out, "sum_w_out")
    part["w_conv_out"] = sum_slots(recv_co, "sum_w_conv_out")
    g_w_up, (recv_glu, recv_xo, recv_kv) = mm_bwd_w(
        h1_bf, dup, 4, "dw_up", rides=(("scatter", g_w_glu), ("scatter", g_w_xo), ("scatter", g_w_kv)))
    for n, r in (("w_glu", recv_glu), ("w_xattn_out", recv_xo), ("w_kv", recv_kv)):
        part[n] = sum_slots(r, "sum_" + n)
    g_w_down, (recv_up,) = mm_bwd_w(hdn, dr2_bf, 1, "dw_down", tk=1024, rides=(("scatter", g_w_up),))
    part["w_up"] = sum_slots(recv_up, "sum_w_up")
    early = ("w_conv_out", "w_glu", "w_xattn_out", "w_kv", "w_out", "w_up")
    g_w_in, landed = mm_bwd_w(
        x_bf, dproj, 4, "dw_in", rides=(("scatter", g_w_down.reshape(4, -1, D_MODEL)), ("all", pack(g_small)))
        + tuple(("pair", part[n]) for n in early))
    recv_dn, srecv = landed[0], landed[1]
    other.update(zip(early, landed[2:]))
    part["w_down"] = sum_slots(recv_dn, "sum_w_down")
    dx, (recv_in, other["w_down"]) = mm_bwd_x(
        dproj, win3, "dx", extras=(dr1,), epi=lambda acc, d: acc + ALPHA * d,
        rides=(("scatter", g_w_in), ("pair", part["w_down"])))
    part["w_in"] = sum_slots(recv_in, "sum_w_in")
    other["w_in"], = exchange(("pair", part["w_in"]), name="swap_w_in")
    res = [{}, {}, {}, {}]
    for n in BIG:
        for k, r in enumerate(adam_pair(part[n], other[n], W[n][0], MOM[n][0], VEL[n][0], "adam_" + n)):
            res[k][n] = r[None]
    gs, ds_, ms, vs = adam_slots(srecv, pack({**{n: W[n] for n in SMALL}, "conv_w": conv_zero}),
                                 pack({**{n: MOM[n] for n in SMALL}, "conv_w": conv_zero}),
                                 pack({**{n: VEL[n] for n in SMALL}, "conv_w": conv_zero}), "adam_small")

    def unpack_small(buf):
        flat = buf.reshape(-1)
        out, r = {}, 0
        for n in small_names:
            ref = g_small[n] if n == "conv_w" else W[n]
            out[n] = flat[r:r + ref.size].reshape(ref.shape)
            r += sizes[n]
        return out

    res_s = [unpack_small(b) for b in (gs, ds_, ms, vs)]
    g_conv = lax.dynamic_slice(res_s[0]["conv_w"], (0, xy * 128), (3, 128))
    conv_slots = g_conv.reshape(1, 3, 128)
    cg, cd, cm, cv = adam_slots(conv_slots, conv_w[0], m_conv_w[0], v_conv_w[0], "adam_conv")
    conv_res = [cg, cd, cm, cv]

    loss = lax.psum(loss_part[0, 0], ("x", "y", "c"))
    outs = [loss, dx.reshape(x.shape)]
    for k in range(4):
        for n in names:
            if n == "conv_w":
                outs.append(conv_res[k].reshape(conv_w.shape))
            elif n in BIG:
                outs.append(res[k][n])
            else:
                outs.append(res_s[k][n])
    return tuple(outs)
```

```python
import functools
import math

import jax
import jax.numpy as jnp
from jax import lax
from jax.experimental import pallas as pl
from jax.experimental.pallas import tpu as pltpu

f32 = jnp.float32
bf16 = jnp.bfloat16

D_MODEL = 1024
CONV_WIDTH = 512
SSM_WIDTH = 512
SSM_GROUP = 16
SSM_GROUPS = 32
SSM_STATE = 64
N_STATES = SSM_GROUPS * SSM_STATE
XATTN_HEADS = 4
XATTN_HEAD_DIM = 128
XATTN_WIDTH = 512
D_FF = 4096
GATE_COLS = 3 * D_MODEL
IN_COLS = GATE_COLS + 3 * CONV_WIDTH + SSM_WIDTH + XATTN_WIDTH
ALPHA = 2.0 ** 0.25
LN_EPS = 1e-5
ADAM_LR = 0.001
ADAM_B1 = 0.9
ADAM_B2 = 0.999
ADAM_EPS = 1e-08
ADAM_WD = 0.01
ADAM_STEP = 10

N_SEG = 8
SCAN_ROWS = 256
LANE_STRIP = 512
TM = 512
VMEM_LIMIT = 48 * 1024 * 1024
MESH = pl.DeviceIdType.MESH

NT_DIMS = (((1,), (1,)), ((), ()))
TN_DIMS = (((0,), (0,)), ((), ()))


def _params():
    return pltpu.CompilerParams(vmem_limit_bytes=VMEM_LIMIT)


def _full(shape):
    n = len(shape)
    return pl.BlockSpec(shape, lambda *_: (0,) * n)


def _rows(tm, w, cb=0):
    return pl.BlockSpec((tm, w), lambda i: (i, cb))


def _sig(x):
    return 1.0 / (1.0 + jnp.exp(-x))


HBM_SPEC = pl.BlockSpec(memory_space=pl.ANY)
RIDE_PEERS = {"gather": 3, "scatter": 3, "all": 7, "pair": 1}


def _xy_peers(x, y):
    return [(1 - x, y), (x, 1 - y), (1 - x, 1 - y)]


def _ride_copies(kind, src, dst, send_sems, recv_sems, local_sem):
    x, y, c = lax.axis_index("x"), lax.axis_index("y"), lax.axis_index("c")
    me = 2 * x + y
    if kind == "all":
        flips = [(fx, fy, fc) for fx in (0, 1) for fy in (0, 1) for fc in (0, 1)][1:]
        peers = [(x ^ fx, y ^ fy, c ^ fc) for fx, fy, fc in flips]
        slot = lambda p: 4 * p[0] + 2 * p[1] + p[2]
        mine = slot((x, y, c))
    else:
        peers = [(px, py, c) for px, py in _xy_peers(x, y)]
        slot = lambda p: 2 * p[0] + p[1]
        mine = me

    def remote(k, s, d):
        return pltpu.make_async_remote_copy(src_ref=s, dst_ref=d, send_sem=send_sems.at[k], recv_sem=recv_sems.at[k],
                                            device_id=peers[k], device_id_type=MESH)

    if kind == "pair":
        peers = [(x, y, 1 - c)]
        return None, [remote(0, src, dst)], [remote(0, src, dst)]
    if kind == "scatter":
        local = pltpu.make_async_copy(src.at[me], dst.at[me], local_sem)
        sends = [remote(k, src.at[slot(p)], dst.at[me]) for k, p in enumerate(peers)]
        lands = [remote(k, src.at[me], dst.at[slot(p)]) for k, p in enumerate(peers)]
    else:
        local = pltpu.make_async_copy(src, dst.at[mine], local_sem)
        sends = [remote(k, src, dst.at[mine]) for k, p in enumerate(peers)]
        lands = [remote(k, src, dst.at[slot(p)]) for k, p in enumerate(peers)]
    return local, sends, lands


def _ride_shape(kind, src):
    lead = {"gather": (4,), "scatter": (), "all": (8,), "pair": ()}[kind]
    return jax.ShapeDtypeStruct(lead + src.shape, src.dtype)


def _pcall(body, args, *, name, grid, in_specs, out_specs, out_shape, scratch_shapes=(), rides=(), aliases=None):
    n_in, n_out, n_scr, nr = len(in_specs), len(out_specs), len(scratch_shapes), len(rides)
    kinds = [k for k, _ in rides]

    def wrapped(*refs):
        ins, rsrc = refs[:n_in], refs[n_in:n_in + nr]
        outs = refs[n_in + nr:n_in + nr + n_out]
        rdst = refs[n_in + nr + n_out:n_in + 2 * nr + n_out]
        scr = refs[n_in + 2 * nr + n_out:n_in + 2 * nr + n_out + n_scr]
        sems = refs[n_in + 2 * nr + n_out + n_scr:]
        def start():
            for r, kind in enumerate(kinds):
                local, sends, _ = _ride_copies(kind, rsrc[r], rdst[r], *sems[3 * r:3 * r + 3])
                if local is not None:
                    local.start()
                for cp in sends:
                    cp.start()

        def finish():
            for r, kind in enumerate(kinds):
                local, sends, lands = _ride_copies(kind, rsrc[r], rdst[r], *sems[3 * r:3 * r + 3])
                for cp in lands:
                    cp.wait_recv()
                for cp in sends:
                    cp.wait_send()
                if local is not None:
                    local.wait()

        if nr and grid:
            ids = [pl.program_id(a) for a in range(len(grid))]
            first = functools.reduce(jnp.logical_and, [i == 0 for i in ids])
            last = functools.reduce(jnp.logical_and, [i == g - 1 for i, g in zip(ids, grid)])
            pl.when(first)(start)
        elif nr:
            start()
        body(*ins, *outs, *scr)
        if nr and grid:
            pl.when(last)(finish)
        elif nr:
            finish()

    sems = []
    for kind in kinds:
        n = RIDE_PEERS[kind]
        sems += [pltpu.SemaphoreType.DMA((n,)), pltpu.SemaphoreType.DMA((n,)), pltpu.SemaphoreType.DMA(())]
    res = pl.pallas_call(
        wrapped, name=name, grid=grid, in_specs=list(in_specs) + [HBM_SPEC] * nr,
        out_specs=list(out_specs) + [HBM_SPEC] * nr,
        out_shape=list(out_shape) + [_ride_shape(k, s) for k, s in rides],
        scratch_shapes=list(scratch_shapes) + sems, input_output_aliases=aliases or {},
        compiler_params=_params())(*args, *[s for _, s in rides])
    return list(res[:n_out]), list(res[n_out:])


def _bf(v):
    return v if v.dtype == bf16 else v.astype(bf16)


def mm_fwd(a, w3, name, *, tm=1024, tn=None, bias=None, outs=(f32,), second=None, rides=()):
    M, K = a.shape
    J, _, n = w3.shape
    tm = min(tm, M)
    tn = tn or n
    nl = n // tn
    nb = 0 if bias is None else 1

    def body(*refs):
        a_ref, w_ref = refs[0], refs[1]
        acc = jnp.dot(_bf(a_ref[...]), w_ref[0], preferred_element_type=f32)
        if bias is not None:
            acc = acc + refs[2][...]
        refs[2 + nb][...] = acc.astype(outs[0])
        if len(outs) > 1:
            refs[3 + nb][...] = second(acc).astype(outs[1])

    in_specs = [pl.BlockSpec((tm, K), lambda j, l, i: (i, 0)),
                pl.BlockSpec((1, K, tn), lambda j, l, i: (j, 0, l))]
    args = [a, w3]
    if bias is not None:
        in_specs.append(pl.BlockSpec((1, tn), lambda j, l, i: (0, j * nl + l)))
        args.append(bias)
    res, landed = _pcall(
        body, args, name=name, grid=(J, nl, M // tm), in_specs=in_specs,
        out_specs=[pl.BlockSpec((tm, tn), lambda j, l, i: (i, j * nl + l)) for _ in outs],
        out_shape=[jax.ShapeDtypeStruct((M, J * n), dt) for dt in outs], rides=rides)
    res = res if len(outs) > 1 else res[0]
    return (res, landed) if rides else res


def mm_bwd_x(dy, w3, name, *, tm=TM, epi=None, extras=(), colsum=False, out_dtype=f32, rides=()):
    M = dy.shape[0]
    J, K, n = w3.shape
    tm = min(tm, M)
    nex = len(extras)

    def body(*refs):
        dy_ref, w_hbm = refs[0], refs[1]
        ex = refs[2:2 + nex]
        o_ref = refs[2 + nex]
        w_ref = refs[-1]
        i = pl.program_id(0)

        @pl.when(i == 0)
        def _():
            pltpu.sync_copy(w_hbm, w_ref)

        acc = None
        for j in range(J):
            part = lax.dot_general(_bf(dy_ref[:, j * n:(j + 1) * n]), w_ref[j], NT_DIMS, preferred_element_type=f32)
            acc = part if acc is None else acc + part
        if epi is not None:
            acc = epi(acc, *[e[...] for e in ex])
        o_ref[...] = acc.astype(out_dtype)
        if colsum:
            s_ref = refs[3 + nex]
            cs = jnp.sum(acc, axis=0, keepdims=True)

            @pl.when(i == 0)
            def _():
                s_ref[...] = cs

            @pl.when(i > 0)
            def _():
                s_ref[...] += cs

    in_specs = [pl.BlockSpec((tm, J * n), lambda i: (i, 0)), pl.BlockSpec(memory_space=pl.ANY)]
    in_specs += [pl.BlockSpec((tm, K), lambda i: (i, 0)) for _ in extras]
    out_specs = [pl.BlockSpec((tm, K), lambda i: (i, 0))]
    out_shape = [jax.ShapeDtypeStruct((M, K), out_dtype)]
    if colsum:
        out_specs.append(pl.BlockSpec((1, K), lambda i: (0, 0)))
        out_shape.append(jax.ShapeDtypeStruct((1, K), f32))
    res, landed = _pcall(body, [dy, w3, *extras], name=name, grid=(M // tm,), in_specs=in_specs, out_specs=out_specs,
                         out_shape=out_shape, scratch_shapes=[pltpu.VMEM((J, K, n), bf16)], rides=rides)
    res = res if colsum else res[0]
    return (res, landed) if rides else res


def mm_bwd_w(a, dy, J, name, *, tm=2048, tn=None, tk=None, rides=()):
    M, K = a.shape
    n = dy.shape[1] // J
    tm = min(tm, M)
    tn = tn or n
    tk = tk or K
    nl = n // tn
    nk = K // tk
    ns = M // tm

    def body(a_ref, dy_ref, o_ref, acc_ref):
        s = pl.program_id(3)
        part = lax.dot_general(_bf(a_ref[...]), _bf(dy_ref[...]), TN_DIMS, preferred_element_type=f32)

        @pl.when(s == 0)
        def _():
            acc_ref[...] = part

        @pl.when(s > 0)
        def _():
            acc_ref[...] += part

        @pl.when(s == ns - 1)
        def _():
            o_ref[0] = acc_ref[...].astype(bf16)

    res, landed = _pcall(
        body, [a, dy], name=name, grid=(J, nl, nk, ns),
        in_specs=[pl.BlockSpec((tm, tk), lambda j, l, k, s: (s, k)),
                  pl.BlockSpec((tm, tn), lambda j, l, k, s: (s, j * nl + l))],
        out_specs=[pl.BlockSpec((1, tk, tn), lambda j, l, k, s: (j, k, l))],
        out_shape=[jax.ShapeDtypeStruct((J, K, n), bf16)],
        scratch_shapes=[pltpu.VMEM((tk, tn), f32)], rides=rides)
    return (res[0], landed) if rides else res[0]


def _relu2(v):
    r = jnp.maximum(v, 0.0)
    return r * r


def _shift_down(z, k, halo):
    r = lax.broadcasted_iota(jnp.int32, z.shape, 0)
    y = pltpu.roll(z, k, 0)
    for q in range(k):
        y = jnp.where(r == q, halo[8 - k + q:8 - k + q + 1, :], y)
    return y


def _shift_up(z, k, halo):
    tm = z.shape[0]
    r = lax.broadcasted_iota(jnp.int32, z.shape, 0)
    y = pltpu.roll(z, tm - k, 0)
    for q in range(k):
        y = jnp.where(r == tm - k + q, halo[q:q + 1, :], y)
    return y


def _prev_halo(tm, cb):
    return pl.BlockSpec((8, CONV_WIDTH), lambda i: (jnp.maximum(i * (tm // 8) - 1, 0), cb))


def _next_halo(tm, cb, nrow8):
    return pl.BlockSpec((8, CONV_WIDTH), lambda i: (jnp.minimum((i + 1) * (tm // 8), nrow8 - 1), cb))


def conv_fwd(proj, conv_w, tm=TM):
    S = proj.shape[0]

    def body(cb_ref, cc_ref, ch_ref, cch_ref, chh_ref, w_ref, a_ref):
        i = pl.program_id(0)
        z = cc_ref[...] * ch_ref[...]
        zh = jnp.where(i == 0, 0.0, cch_ref[...] * chh_ref[...])
        w = w_ref[...]
        dwz = w[0:1, :] * _shift_down(z, 2, zh) + w[1:2, :] * _shift_down(z, 1, zh) + w[2:3, :] * z
        a_ref[...] = (cb_ref[...] * dwz).astype(bf16)

    return pl.pallas_call(
        body, name="conv_fwd", grid=(S // tm,),
        in_specs=[_rows(tm, CONV_WIDTH, 6), _rows(tm, CONV_WIDTH, 7), _rows(tm, CONV_WIDTH, 8),
                  _prev_halo(tm, 7), _prev_halo(tm, 8), _full((3, CONV_WIDTH))],
        out_specs=_rows(tm, CONV_WIDTH),
        out_shape=jax.ShapeDtypeStruct((S, CONV_WIDTH), bf16), compiler_params=_params())(
            proj, proj, proj, proj, proj, conv_w)


def conv_bwd(da, proj, conv_w, dproj, tm=TM):
    S = proj.shape[0]
    nt = S // tm

    def body(da_ref, cb_ref, cc_ref, ch_ref, cch_ref, chh_ref, dan_ref, cbn_ref, w_ref, _, o_ref, dw_ref):
        i = pl.program_id(0)
        cc, ch, cb, dav = cc_ref[...], ch_ref[...], cb_ref[...], da_ref[...]
        z = cc * ch
        zh = jnp.where(i == 0, 0.0, cch_ref[...] * chh_ref[...])
        w = w_ref[...]
        z1 = _shift_down(z, 1, zh)
        z2 = _shift_down(z, 2, zh)
        dwz = w[0:1, :] * z2 + w[1:2, :] * z1 + w[2:3, :] * z
        g = dav * cb
        gn = jnp.where(i == nt - 1, 0.0, dan_ref[...] * cbn_ref[...])
        dz = w[2:3, :] * g + w[1:2, :] * _shift_up(g, 1, gn) + w[0:1, :] * _shift_up(g, 2, gn)
        o_ref[:, 0:CONV_WIDTH] = (dav * dwz).astype(bf16)
        o_ref[:, CONV_WIDTH:2 * CONV_WIDTH] = (dz * ch).astype(bf16)
        o_ref[:, 2 * CONV_WIDTH:3 * CONV_WIDTH] = (dz * cc).astype(bf16)

        @pl.when(i == 0)
        def _():
            dw_ref[...] = jnp.zeros_like(dw_ref)

        dw_ref[0:1, :] += jnp.sum(g * z2, axis=0, keepdims=True)
        dw_ref[1:2, :] += jnp.sum(g * z1, axis=0, keepdims=True)
        dw_ref[2:3, :] += jnp.sum(g * z, axis=0, keepdims=True)

    return pl.pallas_call(
        body, name="conv_bwd", grid=(nt,),
        in_specs=[_rows(tm, CONV_WIDTH), _rows(tm, CONV_WIDTH, 6), _rows(tm, CONV_WIDTH, 7), _rows(tm, CONV_WIDTH, 8),
                  _prev_halo(tm, 7), _prev_halo(tm, 8),
                  pl.BlockSpec((8, CONV_WIDTH), lambda i: (jnp.minimum((i + 1) * (tm // 8), S // 8 - 1), 0)),
                  _next_halo(tm, 6, S // 8), _full((3, CONV_WIDTH)), pl.BlockSpec(memory_space=pl.ANY)],
        out_specs=[_rows(tm, 3 * CONV_WIDTH, GATE_COLS // (3 * CONV_WIDTH)), _full((3, CONV_WIDTH))],
        out_shape=[jax.ShapeDtypeStruct(dproj.shape, bf16), jax.ShapeDtypeStruct((3, CONV_WIDTH), f32)],
        input_output_aliases={9: 0},
        compiler_params=_params())(da, proj, proj, proj, proj, proj, da, proj, conv_w, dproj)


def _cmul(ar, ai, br, bi):
    return ar * br - ai * bi, ar * bi + ai * br


def ssm_prep(lr, li, ldt, bt_re, bt_im):
    def body(lr_ref, li_ref, ldt_ref, br_ref, bi_ref, ar_ref, ai_ref, kr_ref, ki_ref, bbr_ref, bbi_ref):
        lrv, liv = lr_ref[...], li_ref[...]
        dt = jnp.exp(ldt_ref[...])
        mag = jnp.exp(lrv * dt)
        ar = mag * jnp.cos(liv * dt)
        ai = mag * jnp.sin(liv * dt)
        den = lrv * lrv + liv * liv
        nr = ar - 1.0
        kr = (nr * lrv + ai * liv) / den
        ki = (ai * lrv - nr * liv) / den
        ar_ref[...], ai_ref[...], kr_ref[...], ki_ref[...] = ar, ai, kr, ki
        bbr_ref[...] = kr * br_ref[...] - ki * bi_ref[...]
        bbi_ref[...] = kr * bi_ref[...] + ki * br_ref[...]

    v = jax.ShapeDtypeStruct((1, N_STATES), f32)
    m = jax.ShapeDtypeStruct((SSM_GROUP, N_STATES), f32)
    return pl.pallas_call(body, name="ssm_prep", out_shape=[v, v, v, v, m, m])(lr, li, ldt, bt_re, bt_im)


def _pow_segment(ar, ai, seg_len):
    pr, pi = ar, ai
    for _ in range(int(math.log2(seg_len))):
        pr, pi = _cmul(pr, pi, pr, pi)
    return pr, pi


GELU_K = math.sqrt(2.0 / math.pi)
GELU_C = 0.044715


def _gelu(v):
    return 0.5 * v * (1.0 + jnp.tanh(GELU_K * (v + GELU_C * v * v * v)))


def _gelu_grad(v):
    t = jnp.tanh(GELU_K * (v + GELU_C * v * v * v))
    return 0.5 * (1.0 + t) + 0.5 * v * (1.0 - t * t) * GELU_K * (1.0 + 3.0 * GELU_C * v * v)


def ssm_scan_fwd(u_perm, bd_r, bd_i, ar, ai, *, final=None, rides=()):
    S = u_perm.shape[0]
    R = SCAN_ROWS
    nblk = S // R
    seg_len = S // N_SEG
    nstrip = N_STATES // LANE_STRIP
    store = final is not None

    def body(*refs):
        if store:
            (u_ref, bdr_ref, bdi_ref, ar_ref, ai_ref, cdr_ref, cdi_ref, d_ref, ir_ref, ii_ref,
             y_ref, ys_ref, sr_ref, si_ref, bur, bui, car, cai) = refs
        else:
            u_ref, bdr_ref, bdi_ref, ar_ref, ai_ref, or_ref, oi_ref, bur, bui, car, cai = refs
        i = pl.program_id(0)
        u = u_ref[...]
        ub = u.astype(bf16)
        for cb in range(4):
            us = ub[:, cb * 128:(cb + 1) * 128]
            bur[:, cb * 512:(cb + 1) * 512] = jnp.dot(us, bdr_ref[cb], preferred_element_type=f32)
            bui[:, cb * 512:(cb + 1) * 512] = jnp.dot(us, bdi_ref[cb], preferred_element_type=f32)

        @pl.when(i == 0)
        def _():
            if store:
                car[...] = ir_ref[...]
                cai[...] = ii_ref[...]
            else:
                car[...] = jnp.zeros_like(car)
                cai[...] = jnp.zeros_like(cai)

        for ls in range(nstrip):
            lanes = pl.ds(ls * LANE_STRIP, LANE_STRIP)
            a_r = jnp.broadcast_to(ar_ref[:, lanes], (N_SEG, LANE_STRIP))
            a_i = jnp.broadcast_to(ai_ref[:, lanes], (N_SEG, LANE_STRIP))

            def step(t, carry, lanes=lanes, a_r=a_r, a_i=a_i):
                s_r, s_i = carry
                row = pl.multiple_of(t * 8, 8)
                n_r = a_r * s_r - a_i * s_i + bur[pl.ds(row, 8), lanes]
                n_i = a_r * s_i + a_i * s_r + bui[pl.ds(row, 8), lanes]
                if store:
                    sr_ref[pl.ds(row, 8), lanes] = n_r
                    si_ref[pl.ds(row, 8), lanes] = n_i
                return n_r, n_i

            e_r, e_i = lax.fori_loop(0, R // 8, step, (car[:, lanes], cai[:, lanes]), unroll=4)
            car[:, lanes] = e_r
            cai[:, lanes] = e_i

        if store:
            for cb in range(4):
                st_r = sr_ref[:, cb * 512:(cb + 1) * 512].astype(bf16)
                st_i = si_ref[:, cb * 512:(cb + 1) * 512].astype(bf16)
                y = (jnp.dot(st_r, cdr_ref[cb], preferred_element_type=f32)
                     - jnp.dot(st_i, cdi_ref[cb], preferred_element_type=f32))
                cols = slice(cb * 128, (cb + 1) * 128)
                y = y + d_ref[:, cols] * u[:, cols]
                y_ref[:, cols] = y
                ys_ref[:, cols] = _gelu(y).astype(bf16)
        else:
            @pl.when(i == nblk - 1)
            def _():
                p_r, p_i = _pow_segment(ar_ref[...], ai_ref[...], seg_len)
                t_r, t_i = car[0:1, :], cai[0:1, :]
                or_ref[0:1, :] = jnp.zeros((1, N_STATES), f32)
                oi_ref[0:1, :] = jnp.zeros((1, N_STATES), f32)
                for j in range(1, N_SEG):
                    or_ref[j:j + 1, :] = t_r
                    oi_ref[j:j + 1, :] = t_i
                    m_r, m_i = _cmul(p_r, p_i, t_r, t_i)
                    t_r, t_i = car[j:j + 1, :] + m_r, cai[j:j + 1, :] + m_i

    blk = lambda w: pl.BlockSpec((R, w), lambda i: (i, 0))
    in_specs = [blk(SSM_WIDTH), _full((4, 128, 512)), _full((4, 128, 512)), _full((1, N_STATES)), _full((1, N_STATES))]
    args = [u_perm, bd_r, bd_i, ar, ai]
    scratch = [pltpu.VMEM((R, N_STATES), f32), pltpu.VMEM((R, N_STATES), f32),
               pltpu.VMEM((N_SEG, N_STATES), f32), pltpu.VMEM((N_SEG, N_STATES), f32)]
    if store:
        in_specs += [_full((4, 512, 128)), _full((4, 512, 128)), _full((1, SSM_WIDTH)),
                     _full((N_SEG, N_STATES)), _full((N_SEG, N_STATES))]
        args += list(final)
        out_specs = [blk(SSM_WIDTH), blk(SSM_WIDTH), blk(N_STATES), blk(N_STATES)]
        out_shape = [jax.ShapeDtypeStruct((S, SSM_WIDTH), f32), jax.ShapeDtypeStruct((S, SSM_WIDTH), bf16),
                     jax.ShapeDtypeStruct((S, N_STATES), f32), jax.ShapeDtypeStruct((S, N_STATES), f32)]
        name = "ssm_scan_fwd"
    else:
        out_specs = [_full((N_SEG, N_STATES)), _full((N_SEG, N_STATES))]
        out_shape = [jax.ShapeDtypeStruct((N_SEG, N_STATES), f32)] * 2
        name = "ssm_scan_fwd_carry"
    res, landed = _pcall(body, args, name=name, grid=(nblk,), in_specs=in_specs, out_specs=out_specs,
                         out_shape=out_shape, scratch_shapes=scratch, rides=rides)
    return (res, landed) if rides else res


def ssm_scan_bwd(dys_perm, y_perm, cdt_r, cdt_i, ar, ai, *, final=None, rides=()):
    S = dys_perm.shape[0]
    R = SCAN_ROWS
    nblk = S // R
    seg_len = S // N_SEG
    nstrip = N_STATES // LANE_STRIP
    store = final is not None

    def body(*refs):
        if store:
            (dys_ref, y_ref, cdr_ref, cdi_ref, ar_ref, ai_ref, u_ref, sr_ref, si_ref, bdr_ref, bdi_ref, d_ref, ir_ref,
             ii_ref, du_ref, dbr_ref, dbi_ref, dcr_ref, dci_ref, dar_ref, dai_ref, dd_ref, dsr, dsi, lmr, lmi, car,
             cai) = refs
        else:
            dys_ref, y_ref, cdr_ref, cdi_ref, ar_ref, ai_ref, or_ref, oi_ref, dsr, dsi, car, cai = refs
        i = pl.program_id(0)
        dy = dys_ref[...] * _gelu_grad(y_ref[...])
        dyb = dy.astype(bf16)
        for cb in range(4):
            ds_ = dyb[:, cb * 128:(cb + 1) * 128]
            dsr[:, cb * 512:(cb + 1) * 512] = jnp.dot(ds_, cdr_ref[cb], preferred_element_type=f32)
            dsi[:, cb * 512:(cb + 1) * 512] = -jnp.dot(ds_, cdi_ref[cb], preferred_element_type=f32)

        @pl.when(i == 0)
        def _():
            if store:
                car[...] = ir_ref[...]
                cai[...] = ii_ref[...]
                dar_ref[...] = jnp.zeros_like(dar_ref)
                dai_ref[...] = jnp.zeros_like(dai_ref)
                dbr_ref[...] = jnp.zeros_like(dbr_ref)
                dbi_ref[...] = jnp.zeros_like(dbi_ref)
                dcr_ref[...] = jnp.zeros_like(dcr_ref)
                dci_ref[...] = jnp.zeros_like(dci_ref)
                dd_ref[...] = jnp.zeros_like(dd_ref)
            else:
                car[...] = jnp.zeros_like(car)
                cai[...] = jnp.zeros_like(cai)

        for ls in range(nstrip):
            lanes = pl.ds(ls * LANE_STRIP, LANE_STRIP)
            a_r = jnp.broadcast_to(ar_ref[:, lanes], (N_SEG, LANE_STRIP))
            a_i = jnp.broadcast_to(ai_ref[:, lanes], (N_SEG, LANE_STRIP))
            zero = jnp.zeros((N_SEG, LANE_STRIP), f32)

            def step(k, carry, lanes=lanes, a_r=a_r, a_i=a_i):
                l_r, l_i, g_r, g_i = carry
                row = pl.multiple_of((R // 8 - 1 - k) * 8, 8)
                if store:
                    s_r = sr_ref[pl.ds(row, 8), lanes]
                    s_i = si_ref[pl.ds(row, 8), lanes]
                    g_r = g_r + l_r * s_r + l_i * s_i
                    g_i = g_i + l_i * s_r - l_r * s_i
                n_r = dsr[pl.ds(row, 8), lanes] + a_r * l_r + a_i * l_i
                n_i = dsi[pl.ds(row, 8), lanes] + a_r * l_i - a_i * l_r
                if store:
                    lmr[pl.ds(row, 8), lanes] = n_r
                    lmi[pl.ds(row, 8), lanes] = n_i
                return n_r, n_i, g_r, g_i

            e_r, e_i, g_r, g_i = lax.fori_loop(0, R // 8, step, (car[:, lanes], cai[:, lanes], zero, zero), unroll=4)
            car[:, lanes] = e_r
            cai[:, lanes] = e_i
            if store:
                dar_ref[:, lanes] += g_r
                dai_ref[:, lanes] += g_i

        if store:
            u = u_ref[...]
            ub = u.astype(bf16)
            for cb in range(4):
                cols = slice(cb * 128, (cb + 1) * 128)
                st = slice(cb * 512, (cb + 1) * 512)
                l_r = lmr[:, st].astype(bf16)
                l_i = lmi[:, st].astype(bf16)
                du = (jnp.dot(l_r, bdr_ref[cb], preferred_element_type=f32)
                      + jnp.dot(l_i, bdi_ref[cb], preferred_element_type=f32))
                du_ref[:, cols] = (du + d_ref[:, cols] * dy[:, cols]).astype(bf16)
                dbr_ref[cb] += lax.dot_general(l_r, ub[:, cols], TN_DIMS, preferred_element_type=f32)
                dbi_ref[cb] += lax.dot_general(l_i, ub[:, cols], TN_DIMS, preferred_element_type=f32)
                dcr_ref[cb] += lax.dot_general(dyb[:, cols], sr_ref[:, st].astype(bf16), TN_DIMS,
                                               preferred_element_type=f32)
                dci_ref[cb] += lax.dot_general(dyb[:, cols], si_ref[:, st].astype(bf16), TN_DIMS,
                                               preferred_element_type=f32)
            dd_ref[...] += jnp.sum(dy * u, axis=0, keepdims=True)
        else:
            @pl.when(i == nblk - 1)
            def _():
                p_r, p_i = _pow_segment(ar_ref[...], ai_ref[...], seg_len)
                p_i = -p_i
                t_r, t_i = car[N_SEG - 1:N_SEG, :], cai[N_SEG - 1:N_SEG, :]
                or_ref[N_SEG - 1:N_SEG, :] = jnp.zeros((1, N_STATES), f32)
                oi_ref[N_SEG - 1:N_SEG, :] = jnp.zeros((1, N_STATES), f32)
                for j in range(N_SEG - 2, -1, -1):
                    or_ref[j:j + 1, :] = t_r
                    oi_ref[j:j + 1, :] = t_i
                    m_r, m_i = _cmul(p_r, p_i, t_r, t_i)
                    t_r, t_i = car[j:j + 1, :] + m_r, cai[j:j + 1, :] + m_i

    blk = lambda w: pl.BlockSpec((R, w), lambda i: (nblk - 1 - i, 0))
    in_specs = [blk(SSM_WIDTH), blk(SSM_WIDTH), _full((4, 128, 512)), _full((4, 128, 512)), _full((1, N_STATES)),
                _full((1, N_STATES))]
    args = [dys_perm, y_perm, cdt_r, cdt_i, ar, ai]
    seg = jax.ShapeDtypeStruct((N_SEG, N_STATES), f32)
    if store:
        in_specs += [blk(SSM_WIDTH), blk(N_STATES), blk(N_STATES), _full((4, 512, 128)), _full((4, 512, 128)),
                     _full((1, SSM_WIDTH)), _full((N_SEG, N_STATES)), _full((N_SEG, N_STATES))]
        args += list(final)
        out_specs = [blk(SSM_WIDTH), _full((4, 512, 128)), _full((4, 512, 128)), _full((4, 128, 512)),
                     _full((4, 128, 512)), _full((N_SEG, N_STATES)), _full((N_SEG, N_STATES)), _full((1, SSM_WIDTH))]
        b_acc = jax.ShapeDtypeStruct((4, 512, 128), f32)
        c_acc = jax.ShapeDtypeStruct((4, 128, 512), f32)
        out_shape = [jax.ShapeDtypeStruct((S, SSM_WIDTH), bf16), b_acc, b_acc, c_acc, c_acc, seg, seg,
                     jax.ShapeDtypeStruct((1, SSM_WIDTH), f32)]
        scratch = [pltpu.VMEM((R, N_STATES), f32)] * 4 + [pltpu.VMEM((N_SEG, N_STATES), f32)] * 2
        name = "ssm_scan_bwd"
    else:
        out_specs = [_full((N_SEG, N_STATES)), _full((N_SEG, N_STATES))]
        out_shape = [seg, seg]
        scratch = [pltpu.VMEM((R, N_STATES), f32)] * 2 + [pltpu.VMEM((N_SEG, N_STATES), f32)] * 2
        name = "ssm_scan_bwd_carry"
    res, landed = _pcall(body, args, name=name, grid=(nblk,), in_specs=in_specs, out_specs=out_specs,
                         out_shape=out_shape, scratch_shapes=scratch, rides=rides)
    return (res, landed) if rides else res


def ssm_param_bwd(dbb_r, dbb_i, bt_re, bt_im, kr, ki, ar, ai, lr, li, ldt, da_r, da_i):
    def body(dbr_ref, dbi_ref, br_ref, bi_ref, kr_ref, ki_ref, ar_ref, ai_ref, lr_ref, li_ref, ldt_ref, dar_ref,
             dai_ref, obr_ref, obi_ref, olr_ref, oli_ref, odt_ref):
        dbr, dbi, b_r, b_i = dbr_ref[...], dbi_ref[...], br_ref[...], bi_ref[...]
        k_r, k_i, a_r, a_i = kr_ref[...], ki_ref[...], ar_ref[...], ai_ref[...]
        l_r, l_i = lr_ref[...], li_ref[...]
        dt = jnp.exp(ldt_ref[...])
        obr_ref[...] = k_r * dbr + k_i * dbi
        obi_ref[...] = k_r * dbi - k_i * dbr
        gk_r = jnp.sum(dbr * b_r + dbi * b_i, axis=0, keepdims=True)
        gk_i = jnp.sum(dbi * b_r - dbr * b_i, axis=0, keepdims=True)
        ga_r = jnp.sum(dar_ref[...], axis=0, keepdims=True)
        ga_i = jnp.sum(dai_ref[...], axis=0, keepdims=True)
        den = l_r * l_r + l_i * l_i
        c_r, c_i = l_r / den, l_i / den
        m_r, m_i = _cmul(c_r, c_i, gk_r, gk_i)
        g_r, g_i = ga_r + m_r, ga_i + m_i
        t1_r, t1_i = _cmul(dt * a_r, -dt * a_i, g_r, g_i)
        q_r, q_i = _cmul(k_r, k_i, c_r, -c_i)
        t2_r, t2_i = _cmul(-q_r, q_i, gk_r, gk_i)
        olr_ref[...] = t1_r + t2_r
        oli_ref[...] = t1_i + t2_i
        w_r, w_i = _cmul(l_r, l_i, a_r, a_i)
        odt_ref[...] = dt * (w_r * g_r + w_i * g_i)

    v = jax.ShapeDtypeStruct((1, N_STATES), f32)
    m = jax.ShapeDtypeStruct((SSM_GROUP, N_STATES), f32)
    return pl.pallas_call(body, name="ssm_param_bwd", out_shape=[m, m, v, v, v])(
        dbb_r, dbb_i, bt_re, bt_im, kr, ki, ar, ai, lr, li, ldt, da_r, da_i)


def group_sum(v):
    def body(v_ref, o_ref):
        o_ref[...] = jnp.sum(v_ref[...], axis=-1, keepdims=True)
    return pl.pallas_call(body, name="group_sum", out_shape=jax.ShapeDtypeStruct((v.shape[0], 1), f32))(v)


ATTN_SCALE = XATTN_HEAD_DIM ** -0.5


def _attn_probs(q_h, k_h):
    s = lax.dot_general(q_h, k_h, NT_DIMS, preferred_element_type=f32) * ATTN_SCALE
    e = jnp.exp(s - jnp.max(s, axis=-1, keepdims=True))
    return e / jnp.sum(e, axis=-1, keepdims=True)


def attn_fwd(proj, kv, tm=TM):
    S = proj.shape[0]
    M = kv.shape[0]

    def body(q_ref, kv_ref, o_ref):
        for h in range(XATTN_HEADS):
            cols = slice(h * XATTN_HEAD_DIM, (h + 1) * XATTN_HEAD_DIM)
            q_h = q_ref[:, cols].astype(bf16)
            k_h = kv_ref[:, cols]
            v_h = kv_ref[:, XATTN_WIDTH + h * XATTN_HEAD_DIM:XATTN_WIDTH + (h + 1) * XATTN_HEAD_DIM]
            p = _attn_probs(q_h, k_h)
            o_ref[:, cols] = jnp.dot(p.astype(bf16), v_h, preferred_element_type=f32).astype(bf16)

    return pl.pallas_call(
        body, name="attn_fwd", grid=(S // tm,), in_specs=[_rows(tm, XATTN_WIDTH, 10), _full((M, 2 * XATTN_WIDTH))],
        out_specs=_rows(tm, XATTN_WIDTH), out_shape=jax.ShapeDtypeStruct((S, XATTN_WIDTH), bf16),
        compiler_params=_params())(proj, kv)


def attn_bwd(do, proj, kv, dproj, tm=TM):
    S = proj.shape[0]
    M = kv.shape[0]

    def body(do_ref, q_ref, kv_ref, _, dq_ref, dkv_ref):
        i = pl.program_id(0)

        @pl.when(i == 0)
        def _():
            dkv_ref[...] = jnp.zeros_like(dkv_ref)

        for h in range(XATTN_HEADS):
            cols = slice(h * XATTN_HEAD_DIM, (h + 1) * XATTN_HEAD_DIM)
            vcols = slice(XATTN_WIDTH + h * XATTN_HEAD_DIM, XATTN_WIDTH + (h + 1) * XATTN_HEAD_DIM)
            q_h = q_ref[:, cols].astype(bf16)
            k_h = kv_ref[:, cols]
            v_h = kv_ref[:, vcols]
            do_h = do_ref[:, cols]
            p = _attn_probs(q_h, k_h)
            dp = lax.dot_general(do_h, v_h, NT_DIMS, preferred_element_type=f32)
            ds = (p * (dp - jnp.sum(dp * p, axis=-1, keepdims=True)) * ATTN_SCALE).astype(bf16)
            dq_ref[:, cols] = jnp.dot(ds, k_h, preferred_element_type=f32).astype(bf16)
            dkv_ref[:, cols] += lax.dot_general(ds, q_h, TN_DIMS, preferred_element_type=f32)
            dkv_ref[:, vcols] += lax.dot_general(p.astype(bf16), do_h, TN_DIMS, preferred_element_type=f32)

    return pl.pallas_call(
        body, name="attn_bwd", grid=(S // tm,),
        in_specs=[_rows(tm, XATTN_WIDTH), _rows(tm, XATTN_WIDTH, 10), _full((M, 2 * XATTN_WIDTH)),
                  pl.BlockSpec(memory_space=pl.ANY)],
        out_specs=[_rows(tm, XATTN_WIDTH, 10), _full((M, 2 * XATTN_WIDTH))],
        out_shape=[jax.ShapeDtypeStruct(dproj.shape, bf16), jax.ShapeDtypeStruct((M, 2 * XATTN_WIDTH), f32)],
        input_output_aliases={3: 0},
        compiler_params=_params())(do, proj, kv, dproj)


def merge_fwd(proj, b_gate, y_a, glu, y_c, tm=TM, rides=()):
    S = proj.shape[0]

    def body(g0_ref, g1_ref, g2_ref, b_ref, ya_ref, ga_ref, gb_ref, yc_ref, o_ref):
        b = b_ref[...]
        g0 = _sig(g0_ref[...] + b[:, 0:D_MODEL])
        g1 = _sig(g1_ref[...] + b[:, D_MODEL:2 * D_MODEL])
        g2 = _sig(g2_ref[...] + b[:, 2 * D_MODEL:3 * D_MODEL])
        y_b = ga_ref[...] * _sig(gb_ref[...])
        o_ref[...] = (g0 * ya_ref[...] + g1 * y_b + g2 * yc_ref[...]).astype(bf16)

    res, landed = _pcall(
        body, [proj, proj, proj, b_gate, y_a, glu, glu, y_c], name="merge_fwd", grid=(S // tm,),
        in_specs=[_rows(tm, D_MODEL, 0), _rows(tm, D_MODEL, 1), _rows(tm, D_MODEL, 2), _full((1, GATE_COLS)),
                  _rows(tm, D_MODEL), _rows(tm, D_MODEL, 0), _rows(tm, D_MODEL, 1), _rows(tm, D_MODEL)],
        out_specs=[_rows(tm, D_MODEL)], out_shape=[jax.ShapeDtypeStruct((S, D_MODEL), bf16)], rides=rides)
    return (res[0], landed) if rides else res[0]


def merge_bwd(dm, proj, b_gate, y_a, glu, y_c, tm=256, rides=()):
    S = proj.shape[0]

    def body(dm_ref, g0_ref, g1_ref, g2_ref, b_ref, ya_ref, ga_ref, gb_ref, yc_ref,
             dg_ref, dya_ref, dgl_ref, dyc_ref, dbg_ref):
        i = pl.program_id(0)
        b = b_ref[...]
        dmv = dm_ref[...]
        ga, sb = ga_ref[...], _sig(gb_ref[...])
        ys = (ya_ref[...], ga * sb, yc_ref[...])
        gs = tuple(_sig(r[...] + b[:, k * D_MODEL:(k + 1) * D_MODEL]) for k, r in enumerate((g0_ref, g1_ref, g2_ref)))

        @pl.when(i == 0)
        def _():
            dbg_ref[...] = jnp.zeros_like(dbg_ref)

        for k in range(3):
            dpre = dmv * ys[k] * gs[k] * (1.0 - gs[k])
            dg_ref[:, k * D_MODEL:(k + 1) * D_MODEL] = dpre.astype(bf16)
            dbg_ref[:, k * D_MODEL:(k + 1) * D_MODEL] += jnp.sum(dpre, axis=0, keepdims=True)
        dya_ref[...] = (dmv * gs[0]).astype(bf16)
        dyc_ref[...] = (dmv * gs[2]).astype(bf16)
        dyb = dmv * gs[1]
        dgl_ref[:, 0:D_MODEL] = (dyb * sb).astype(bf16)
        dgl_ref[:, D_MODEL:2 * D_MODEL] = (dyb * ga * sb * (1.0 - sb)).astype(bf16)

    res, landed = _pcall(
        body, [dm, proj, proj, proj, b_gate, y_a, glu, glu, y_c], name="merge_bwd", grid=(S // tm,),
        in_specs=[_rows(tm, D_MODEL), _rows(tm, D_MODEL, 0), _rows(tm, D_MODEL, 1), _rows(tm, D_MODEL, 2),
                  _full((1, GATE_COLS)), _rows(tm, D_MODEL), _rows(tm, D_MODEL, 0), _rows(tm, D_MODEL, 1),
                  _rows(tm, D_MODEL)],
        out_specs=[_rows(tm, GATE_COLS), _rows(tm, D_MODEL), _rows(tm, 2 * D_MODEL), _rows(tm, D_MODEL),
                   _full((1, GATE_COLS))],
        out_shape=[jax.ShapeDtypeStruct((S, IN_COLS), bf16), jax.ShapeDtypeStruct((S, D_MODEL), bf16),
                   jax.ShapeDtypeStruct((S, 2 * D_MODEL), bf16), jax.ShapeDtypeStruct((S, D_MODEL), bf16),
                   jax.ShapeDtypeStruct((1, GATE_COLS), f32)], rides=rides)
    return (res, landed) if rides else res


def _ln_stats(r):
    mu = jnp.mean(r, axis=-1, keepdims=True)
    xc = r - mu
    var = jnp.mean(xc * xc, axis=-1, keepdims=True)
    rstd = lax.rsqrt(var + LN_EPS)
    return xc * rstd, rstd


def _ln_bwd(dy, xhat, rstd, g):
    dxh = dy * g
    return rstd * (dxh - jnp.mean(dxh, axis=-1, keepdims=True) - xhat * jnp.mean(dxh * xhat, axis=-1, keepdims=True))


def ln1_fwd(x, t1, g, b, tm=TM):
    S = x.shape[0]

    def body(x_ref, t_ref, g_ref, b_ref, o_ref, ob_ref):
        xhat, _ = _ln_stats(ALPHA * x_ref[...] + t_ref[...])
        h = xhat * g_ref[...] + b_ref[...]
        o_ref[...] = h
        ob_ref[...] = h.astype(bf16)

    return pl.pallas_call(
        body, name="ln1_fwd", grid=(S // tm,),
        in_specs=[_rows(tm, D_MODEL), _rows(tm, D_MODEL), _full((1, D_MODEL)), _full((1, D_MODEL))],
        out_specs=[_rows(tm, D_MODEL), _rows(tm, D_MODEL)],
        out_shape=[jax.ShapeDtypeStruct((S, D_MODEL), f32), jax.ShapeDtypeStruct((S, D_MODEL), bf16)],
        compiler_params=_params())(x, t1, g, b)


def ln2_loss_bwd(h1, t2, b_down, g, b, target, tm=TM):
    S = h1.shape[0]

    def body(h_ref, t_ref, bd_ref, g_ref, b_ref, y_ref, dr_ref, drb_ref, l_ref, dg_ref, db_ref, dbd_ref):
        i = pl.program_id(0)
        xhat, rstd = _ln_stats(ALPHA * h_ref[...] + t_ref[...] + bd_ref[...])
        gv = g_ref[...]
        err = xhat * gv + b_ref[...] - y_ref[...]
        dout = err * (1.0 / D_MODEL)
        dr = _ln_bwd(dout, xhat, rstd, gv)
        dr_ref[...] = dr
        drb_ref[...] = dr.astype(bf16)
        part = 0.5 * jnp.sum(jnp.sum(err * err, axis=-1, keepdims=True) * (1.0 / D_MODEL), axis=0, keepdims=True)

        @pl.when(i == 0)
        def _():
            l_ref[...] = jnp.zeros_like(l_ref)
            dg_ref[...] = jnp.zeros_like(dg_ref)
            db_ref[...] = jnp.zeros_like(db_ref)
            dbd_ref[...] = jnp.zeros_like(dbd_ref)

        l_ref[...] += jnp.broadcast_to(part, l_ref.shape)
        dg_ref[...] += jnp.sum(dout * xhat, axis=0, keepdims=True)
        db_ref[...] += jnp.sum(dout, axis=0, keepdims=True)
        dbd_ref[...] += jnp.sum(dr, axis=0, keepdims=True)

    vec = jax.ShapeDtypeStruct((1, D_MODEL), f32)
    return pl.pallas_call(
        body, name="ln2_loss_bwd", grid=(S // tm,),
        in_specs=[_rows(tm, D_MODEL), _rows(tm, D_MODEL), _full((1, D_MODEL)), _full((1, D_MODEL)),
                  _full((1, D_MODEL)), _rows(tm, D_MODEL)],
        out_specs=[_rows(tm, D_MODEL), _rows(tm, D_MODEL), _full((1, 128)), _full((1, D_MODEL)), _full((1, D_MODEL)),
                   _full((1, D_MODEL))],
        out_shape=[jax.ShapeDtypeStruct((S, D_MODEL), f32), jax.ShapeDtypeStruct((S, D_MODEL), bf16),
                   jax.ShapeDtypeStruct((1, 128), f32), vec, vec, vec],
        compiler_params=_params())(h1, t2, b_down, g, b, target)


def ln1_bwd(x, t1, dr2, t3, g, tm=TM):
    S = x.shape[0]

    def body(x_ref, t_ref, dr2_ref, t3_ref, g_ref, dr_ref, drb_ref, dg_ref, db_ref):
        i = pl.program_id(0)
        xhat, rstd = _ln_stats(ALPHA * x_ref[...] + t_ref[...])
        dh = ALPHA * dr2_ref[...] + t3_ref[...]
        dr = _ln_bwd(dh, xhat, rstd, g_ref[...])
        dr_ref[...] = dr
        drb_ref[...] = dr.astype(bf16)

        @pl.when(i == 0)
        def _():
            dg_ref[...] = jnp.zeros_like(dg_ref)
            db_ref[...] = jnp.zeros_like(db_ref)

        dg_ref[...] += jnp.sum(dh * xhat, axis=0, keepdims=True)
        db_ref[...] += jnp.sum(dh, axis=0, keepdims=True)

    vec = jax.ShapeDtypeStruct((1, D_MODEL), f32)
    return pl.pallas_call(
        body, name="ln1_bwd", grid=(S // tm,),
        in_specs=[_rows(tm, D_MODEL)] * 4 + [_full((1, D_MODEL))],
        out_specs=[_rows(tm, D_MODEL), _rows(tm, D_MODEL), _full((1, D_MODEL)), _full((1, D_MODEL))],
        out_shape=[jax.ShapeDtypeStruct((S, D_MODEL), f32), jax.ShapeDtypeStruct((S, D_MODEL), bf16), vec, vec],
        compiler_params=_params())(x, t1, dr2, t3, g)


def exchange(*rides, name):
    return _pcall(lambda: None, [], name=name, grid=(), in_specs=[], out_specs=[], out_shape=[], rides=rides)[1]


def gather_two_level(shard, name):
    K, n = shard.shape
    h = K // 2

    def body(in_ref, out_ref, ici_send, ici_recv, d2d_send, d2d_recv, local_sem):
        x, y, c = lax.axis_index("x"), lax.axis_index("y"), lax.axis_index("c")
        me = 2 * x + y
        mine = pl.ds(pl.multiple_of(c * h, 16), h)
        theirs = pl.ds(pl.multiple_of((1 - c) * h, 16), h)
        peers = _xy_peers(x, y)

        def over_ici(k, dst):
            return pltpu.make_async_remote_copy(src_ref=in_ref.at[mine], dst_ref=dst, send_sem=ici_send.at[k],
                                                recv_sem=ici_recv.at[k], device_id=(*peers[k], c), device_id_type=MESH)

        def over_d2d(k, rows):
            blk = out_ref.at[2 * peers[k][0] + peers[k][1], rows]
            return pltpu.make_async_remote_copy(src_ref=blk, dst_ref=blk, send_sem=d2d_send.at[k],
                                                recv_sem=d2d_recv.at[k], device_id=(x, y, 1 - c), device_id_type=MESH)

        local = pltpu.make_async_copy(in_ref, out_ref.at[me], local_sem)
        local.start()
        started = [over_ici(k, out_ref.at[me, mine]) for k in range(3)]
        for cp in started:
            cp.start()
        for k in range(3):
            over_ici(k, out_ref.at[2 * peers[k][0] + peers[k][1], mine]).wait_recv()
            passed = over_d2d(k, mine)
            passed.start()
            started.append(passed)
        for k in range(3):
            over_d2d(k, theirs).wait_recv()
        for cp in started:
            cp.wait_send()
        local.wait()

    return pl.pallas_call(
        body, name=name, in_specs=[HBM_SPEC], out_specs=HBM_SPEC,
        out_shape=jax.ShapeDtypeStruct((4, K, n), shard.dtype),
        scratch_shapes=[pltpu.SemaphoreType.DMA((3,))] * 4 + [pltpu.SemaphoreType.DMA(())])(shard)


ROW_TILE = 256


def sum_slots(recv, name):
    n, R, C = recv.shape
    tm = min(R, ROW_TILE)

    def body(r_ref, o_ref):
        acc = r_ref[0].astype(f32)
        for k in range(1, n):
            acc = acc + r_ref[k].astype(f32)
        o_ref[...] = acc

    return pl.pallas_call(
        body, name=name, grid=(R // tm,), in_specs=[pl.BlockSpec((n, tm, C), lambda i: (0, i, 0))],
        out_specs=_rows(tm, C), out_shape=jax.ShapeDtypeStruct((R, C), f32), compiler_params=_params())(recv)


def _adamw(w, g, m, v):
    m = ADAM_B1 * m + (1.0 - ADAM_B1) * g
    v = ADAM_B2 * v + (1.0 - ADAM_B2) * (g * g)
    m_hat = m / (1.0 - ADAM_B1 ** ADAM_STEP)
    v_hat = v / (1.0 - ADAM_B2 ** ADAM_STEP)
    delta = -ADAM_LR * (m_hat / (jnp.sqrt(v_hat) + ADAM_EPS) + ADAM_WD * w)
    return delta, m, v


def adam_pair(p, q, w, m, v, name):
    R, C = w.shape
    tm = min(R, ROW_TILE)

    def body(p_ref, q_ref, w_ref, m_ref, v_ref, g_ref, d_ref, nm_ref, nv_ref):
        g = p_ref[...] + q_ref[...]
        g_ref[...] = g
        d_ref[...], nm_ref[...], nv_ref[...] = _adamw(w_ref[...], g, m_ref[...], v_ref[...])

    o = jax.ShapeDtypeStruct((R, C), f32)
    return pl.pallas_call(body, name=name, grid=(R // tm,), in_specs=[_rows(tm, C)] * 5,
                          out_specs=[_rows(tm, C)] * 4, out_shape=[o] * 4, compiler_params=_params())(p, q, w, m, v)


def adam_slots(recv, w, m, v, name):
    n = recv.shape[0]

    def body(r_ref, w_ref, m_ref, v_ref, g_ref, d_ref, nm_ref, nv_ref):
        g = r_ref[0]
        for k in range(1, n):
            g = g + r_ref[k]
        g_ref[...] = g
        d_ref[...], nm_ref[...], nv_ref[...] = _adamw(w_ref[...], g, m_ref[...], v_ref[...])

    o = jax.ShapeDtypeStruct(w.shape, f32)
    return pl.pallas_call(body, name=name, out_shape=[o] * 4, compiler_params=_params())(recv, w, m, v)


BIG = ("w_in", "w_conv_out", "w_glu", "w_xattn_out", "w_kv", "w_out", "w_up", "w_down")
MID = ("w_conv_out", "w_glu", "w_xattn_out", "w_kv", "w_out")
SMALL = ("b_gate", "ssm_lam_re", "ssm_lam_im", "ssm_log_dt", "ssm_b_re", "ssm_b_im", "ssm_c_re", "ssm_c_im", "ssm_d",
         "ln1_g", "ln1_b", "b_up", "b_down", "ln2_g", "ln2_b")
def _pad_flat(a, mult=1024):
    a = a.reshape(-1)
    return jnp.pad(a, (0, (-a.shape[0]) % mult))


def _perm(a):
    S, W = a.shape
    return a.reshape(N_SEG, S // N_SEG, W).transpose(1, 0, 2).reshape(S, W)


def _unperm(a):
    S, W = a.shape
    return a.reshape(S // N_SEG, N_SEG, W).transpose(1, 0, 2).reshape(S, W)


def _state_rows(a):
    return a.transpose(2, 0, 1).reshape(SSM_GROUP, N_STATES)


def _block_diag_b(bt):
    b4 = bt.reshape(SSM_GROUP, 4, 8, SSM_STATE)
    eye = jnp.eye(8, dtype=bt.dtype)
    return jnp.einsum("hcgp,gk->cghkp", b4, eye).reshape(4, 128, 512)


def _block_diag_c(c):
    c4 = c.reshape(4, 8, SSM_GROUP, SSM_STATE)
    eye = jnp.eye(8, dtype=c.dtype)
    return jnp.einsum("cghp,gk->cgpkh", c4, eye).reshape(4, 512, 128)


def _diag_b(acc):
    a = acc.reshape(4, 8, SSM_STATE, 8, SSM_GROUP)
    eye = jnp.eye(8, dtype=acc.dtype)
    return jnp.einsum("cgpkh,gk->hcgp", a, eye).reshape(SSM_GROUP, N_STATES)


def _diag_c(acc):
    a = acc.reshape(4, 8, SSM_GROUP, 8, SSM_STATE)
    eye = jnp.eye(8, dtype=acc.dtype)
    return jnp.einsum("cghkp,gk->cghp", a, eye).reshape(SSM_GROUPS, SSM_GROUP, SSM_STATE)


def kernel(x, mem, w_in, b_gate, conv_w, w_conv_out, ssm_lam_re, ssm_lam_im, ssm_log_dt, ssm_b_re, ssm_b_im, ssm_c_re, ssm_c_im, ssm_d, w_glu, w_kv, w_xattn_out, w_out, ln1_g, ln1_b, w_up, b_up, w_down, b_down, ln2_g, ln2_b, loss_target, m_w_in, m_b_gate, m_conv_w, m_w_conv_out, m_ssm_lam_re, m_ssm_lam_im, m_ssm_log_dt, m_ssm_b_re, m_ssm_b_im, m_ssm_c_re, m_ssm_c_im, m_ssm_d, m_w_glu, m_w_kv, m_w_xattn_out, m_w_out, m_ln1_g, m_ln1_b, m_w_up, m_b_up, m_w_down, m_b_down, m_ln2_g, m_ln2_b, v_w_in, v_b_gate, v_conv_w, v_w_conv_out, v_ssm_lam_re, v_ssm_lam_im, v_ssm_log_dt, v_ssm_b_re, v_ssm_b_im, v_ssm_c_re, v_ssm_c_im, v_ssm_d, v_w_glu, v_w_kv, v_w_xattn_out, v_w_out, v_ln1_g, v_ln1_b, v_w_up, v_b_up, v_w_down, v_b_down, v_ln2_g, v_ln2_b):
    W = dict(w_in=w_in, b_gate=b_gate, conv_w=conv_w, w_conv_out=w_conv_out, ssm_lam_re=ssm_lam_re,
             ssm_lam_im=ssm_lam_im, ssm_log_dt=ssm_log_dt, ssm_b_re=ssm_b_re, ssm_b_im=ssm_b_im, ssm_c_re=ssm_c_re,
             ssm_c_im=ssm_c_im, ssm_d=ssm_d, w_glu=w_glu, w_kv=w_kv, w_xattn_out=w_xattn_out, w_out=w_out,
             ln1_g=ln1_g, ln1_b=ln1_b, w_up=w_up, b_up=b_up, w_down=w_down, b_down=b_down, ln2_g=ln2_g, ln2_b=ln2_b)
    MOM = dict(w_in=m_w_in, b_gate=m_b_gate, conv_w=m_conv_w, w_conv_out=m_w_conv_out, ssm_lam_re=m_ssm_lam_re,
               ssm_lam_im=m_ssm_lam_im, ssm_log_dt=m_ssm_log_dt, ssm_b_re=m_ssm_b_re, ssm_b_im=m_ssm_b_im,
               ssm_c_re=m_ssm_c_re, ssm_c_im=m_ssm_c_im, ssm_d=m_ssm_d, w_glu=m_w_glu, w_kv=m_w_kv,
               w_xattn_out=m_w_xattn_out, w_out=m_w_out, ln1_g=m_ln1_g, ln1_b=m_ln1_b, w_up=m_w_up, b_up=m_b_up,
               w_down=m_w_down, b_down=m_b_down, ln2_g=m_ln2_g, ln2_b=m_ln2_b)
    VEL = dict(w_in=v_w_in, b_gate=v_b_gate, conv_w=v_conv_w, w_conv_out=v_w_conv_out, ssm_lam_re=v_ssm_lam_re,
               ssm_lam_im=v_ssm_lam_im, ssm_log_dt=v_ssm_log_dt, ssm_b_re=v_ssm_b_re, ssm_b_im=v_ssm_b_im,
               ssm_c_re=v_ssm_c_re, ssm_c_im=v_ssm_c_im, ssm_d=v_ssm_d, w_glu=v_w_glu, w_kv=v_w_kv,
               w_xattn_out=v_w_xattn_out, w_out=v_w_out, ln1_g=v_ln1_g, ln1_b=v_ln1_b, w_up=v_w_up, b_up=v_b_up,
               w_down=v_w_down, b_down=v_b_down, ln2_g=v_ln2_g, ln2_b=v_ln2_b)
    names = list(W)
    xy = 2 * lax.axis_index("x") + lax.axis_index("y")

    xs = x[0]
    S = xs.shape[0]
    mems = mem[0]
    tgt = loss_target[0]

    shard_bf = {n: W[n][0].astype(bf16) for n in BIG}
    win3 = gather_two_level(shard_bf["w_in"], "gather_w_in")

    lr = ssm_lam_re.reshape(1, N_STATES)
    li = ssm_lam_im.reshape(1, N_STATES)
    ldt = jnp.repeat(ssm_log_dt.reshape(SSM_GROUPS), SSM_STATE).reshape(1, N_STATES)
    bt_re, bt_im = _state_rows(ssm_b_re[0]), _state_rows(ssm_b_im[0])
    ar, ai, kr, ki, bbt_r, bbt_i = ssm_prep(lr, li, ldt, bt_re, bt_im)
    bd_r, bd_i = _block_diag_b(bbt_r).astype(bf16), _block_diag_b(bbt_i).astype(bf16)
    cd_r, cd_i = _block_diag_c(ssm_c_re[0]).astype(bf16), _block_diag_c(ssm_c_im[0]).astype(bf16)
    bdt_r, bdt_i = bd_r.transpose(0, 2, 1), bd_i.transpose(0, 2, 1)
    cdt_r, cdt_i = cd_r.transpose(0, 2, 1), cd_i.transpose(0, 2, 1)
    d_skip = ssm_d.reshape(1, SSM_WIDTH)

    x_bf = xs.astype(bf16)
    mem_bf = mems.astype(bf16)
    u_col = GATE_COLS + 3 * CONV_WIDTH
    proj, (wco3, wglu3, wxo3, wkv4, wout4, convw4) = mm_fwd(
        x_bf, win3, "proj", rides=tuple(("gather", shard_bf[n]) for n in MID) + (("gather", conv_w[0]),))
    wkv3 = wkv4.reshape(1, D_MODEL, 2 * XATTN_WIDTH)
    wout3 = wout4.reshape(1, D_MODEL, D_MODEL)
    convw_full = convw4.transpose(1, 0, 2).reshape(3, CONV_WIDTH)
    a_conv = conv_fwd(proj, convw_full)
    y_a = mm_fwd(a_conv, wco3, "conv_out")
    u_perm = _perm(proj[:, u_col:u_col + SSM_WIDTH])
    half = D_MODEL // 2
    (init_r, init_i), (wup_a,) = ssm_scan_fwd(u_perm, bd_r, bd_i, ar, ai,
                                              rides=(("gather", shard_bf["w_up"][:half]),))
    (ysm_perm, ys_perm, st_r, st_i), (wup_b,) = ssm_scan_fwd(
        u_perm, bd_r, bd_i, ar, ai, final=(cd_r, cd_i, d_skip, init_r, init_i),
        rides=(("gather", shard_bf["w_up"][half:]),))
    wup3 = jnp.concatenate([wup_a, wup_b], axis=1)
    y_s = _unperm(ys_perm)
    glu = mm_fwd(y_s, wglu3, "glu")
    kv = mm_fwd(mem_bf, wkv3, "kv", outs=(bf16,))
    o_att = attn_fwd(proj, kv)
    y_c = mm_fwd(o_att, wxo3, "xattn_out")
    merged = merge_fwd(proj, b_gate, y_a, glu, y_c)
    t1 = mm_fwd(merged, wout3, "w_out")
    h1, h1_bf = ln1_fwd(xs, t1, ln1_g, ln1_b)
    (up, hdn), (wdn4,) = mm_fwd(h1_bf, wup3, "w_up", bias=b_up, outs=(f32, bf16), second=_relu2,
                                rides=(("gather", shard_bf["w_down"]),))
    wdn3 = wdn4.reshape(1, D_FF, D_MODEL)
    t2 = mm_fwd(hdn, wdn3, "w_down", tn=512)

    dr2, dr2_bf, loss_part, d_ln2_g, d_ln2_b, d_b_down = ln2_loss_bwd(h1, t2, b_down, ln2_g, ln2_b, tgt)
    part, other = {}, {}
    g_w_down = mm_bwd_w(hdn, dr2_bf, 1, "dw_down", tk=1024).reshape(4, -1, D_MODEL)
    (dup, d_b_up), (recv_dn,) = mm_bwd_x(dr2_bf, wdn3, "dup", tm=256, extras=(up,), colsum=True, out_dtype=bf16,
                                         epi=lambda acc, upv: acc * (2.0 * jnp.maximum(upv, 0.0)),
                                         rides=(("scatter", g_w_down),))
    part["w_down"] = sum_slots(recv_dn, "sum_w_down")
    g_w_up = mm_bwd_w(h1_bf, dup, 4, "dw_up")
    t3 = mm_bwd_x(dup, wup3, "dh1")
    dr1, dr1_bf, d_ln1_g, d_ln1_b = ln1_bwd(xs, t1, dr2, t3, ln1_g)
    g_w_out = mm_bwd_w(merged, dr1_bf, 1, "dw_out").reshape(4, -1, D_MODEL)
    dmerged, (recv_out,) = mm_bwd_x(dr1_bf, wout3, "dmerged", tm=1024, rides=(("scatter", g_w_out),))
    part["w_out"] = sum_slots(recv_out, "sum_w_out")
    (dproj, dy_a, dglu, dy_c, d_b_gate), (recv_up,) = merge_bwd(dmerged, proj, b_gate, y_a, glu, y_c,
                                                                rides=(("scatter", g_w_up),))
    part["w_up"] = sum_slots(recv_up, "sum_w_up")

    g_w_co = mm_bwd_w(a_conv, dy_a, 4, "dw_conv_out")
    da_conv = mm_bwd_x(dy_a, wco3, "da_conv", tm=1024)
    dproj, d_conv_w = conv_bwd(da_conv, proj, convw_full, dproj)

    g_w_glu = mm_bwd_w(y_s, dglu, 4, "dw_glu")
    dys_perm = _perm(mm_bwd_x(dglu, wglu3, "dy_s", tm=1024))
    linit_r, linit_i = ssm_scan_bwd(dys_perm, ysm_perm, cdt_r, cdt_i, ar, ai)
    (du_perm, dbacc_r, dbacc_i, dcacc_r, dcacc_i, da_r, da_i, d_ssm_d), (recv_co, recv_glu) = ssm_scan_bwd(
        dys_perm, ysm_perm, cdt_r, cdt_i, ar, ai,
        final=(u_perm, st_r, st_i, bdt_r, bdt_i, d_skip, linit_r, linit_i),
        rides=(("scatter", g_w_co), ("scatter", g_w_glu)))
    part["w_conv_out"] = sum_slots(recv_co, "sum_w_conv_out")
    part["w_glu"] = sum_slots(recv_glu, "sum_w_glu")
    dbt_re, dbt_im, d_lr, d_li, d_ldt_state = ssm_param_bwd(
        _diag_b(dbacc_r), _diag_b(dbacc_i), bt_re, bt_im, kr, ki, ar, ai, lr, li, ldt, da_r, da_i)
    d_log_dt = group_sum(d_ldt_state.reshape(SSM_GROUPS, SSM_STATE))
    d_b_re = dbt_re.reshape(SSM_GROUP, SSM_GROUPS, SSM_STATE).transpose(1, 2, 0)
    d_b_im = dbt_im.reshape(SSM_GROUP, SSM_GROUPS, SSM_STATE).transpose(1, 2, 0)
    d_c_re = _diag_c(dcacc_r)
    d_c_im = -_diag_c(dcacc_i)

    g_w_xo = mm_bwd_w(o_att, dy_c, 4, "dw_xattn_out")
    do_att = mm_bwd_x(dy_c, wxo3, "do_att", tm=1024, out_dtype=bf16)
    dproj, dkv = attn_bwd(do_att, proj, kv, dproj)
    g_w_kv = mm_bwd_w(mem_bf, dkv, 1, "dw_kv").reshape(4, -1, D_MODEL)

    dproj = lax.dynamic_update_slice(dproj, _unperm(du_perm), (0, u_col))
    g_small = {"b_gate": d_b_gate, "ssm_lam_re": d_lr, "ssm_lam_im": d_li, "ssm_log_dt": d_log_dt, "ssm_b_re": d_b_re,
               "ssm_b_im": d_b_im, "ssm_c_re": d_c_re, "ssm_c_im": d_c_im, "ssm_d": d_ssm_d, "ln1_g": d_ln1_g,
               "ln1_b": d_ln1_b, "b_up": d_b_up, "b_down": d_b_down, "ln2_g": d_ln2_g, "ln2_b": d_ln2_b}
    small_names = SMALL + ("conv_w",)
    g_small["conv_w"] = d_conv_w
    sizes = {n: (g_small[n].size + 1023) // 1024 * 1024 for n in small_names}
    pack = lambda d: jnp.concatenate([_pad_flat(d[n]) for n in small_names]).reshape(-1, 128)
    conv_zero = jnp.zeros((3, CONV_WIDTH), f32)
    early = ("w_down", "w_up", "w_out", "w_conv_out", "w_glu")
    g_w_in, landed = mm_bwd_w(
        x_bf, dproj, 4, "dw_in", rides=(("scatter", g_w_xo), ("scatter", g_w_kv), ("all", pack(g_small)))
        + tuple(("pair", part[n]) for n in early))
    recv_xo, recv_kv, srecv = landed[:3]
    other.update(zip(early, landed[3:]))
    part["w_xattn_out"] = sum_slots(recv_xo, "sum_w_xattn_out")
    part["w_kv"] = sum_slots(recv_kv, "sum_w_kv")
    dx, (recv_in, other["w_xattn_out"], other["w_kv"]) = mm_bwd_x(
        dproj, win3, "dx", extras=(dr1,), epi=lambda acc, d: acc + ALPHA * d,
        rides=(("scatter", g_w_in), ("pair", part["w_xattn_out"]), ("pair", part["w_kv"])))
    part["w_in"] = sum_slots(recv_in, "sum_w_in")
    other["w_in"], = exchange(("pair", part["w_in"]), name="swap_w_in")
    res = [{}, {}, {}, {}]
    for n in BIG:
        for k, r in enumerate(adam_pair(part[n], other[n], W[n][0], MOM[n][0], VEL[n][0], "adam_" + n)):
            res[k][n] = r[None]
    gs, ds_, ms, vs = adam_slots(srecv, pack({**{n: W[n] for n in SMALL}, "conv_w": conv_zero}),
                                 pack({**{n: MOM[n] for n in SMALL}, "conv_w": conv_zero}),
                                 pack({**{n: VEL[n] for n in SMALL}, "conv_w": conv_zero}), "adam_small")

    def unpack_small(buf):
        flat = buf.reshape(-1)
        out, r = {}, 0
        for n in small_names:
            ref = g_small[n] if n == "conv_w" else W[n]
            out[n] = flat[r:r + ref.size].reshape(ref.shape)
            r += sizes[n]
        return out

    res_s = [unpack_small(b) for b in (gs, ds_, ms, vs)]
    g_conv = lax.dynamic_slice(res_s[0]["conv_w"], (0, xy * 128), (3, 128))
    conv_slots = g_conv.reshape(1, 3, 128)
    cg, cd, cm, cv = adam_slots(conv_slots, conv_w[0], m_conv_w[0], v_conv_w[0], "adam_conv")
    conv_res = [cg, cd, cm, cv]

    loss = lax.psum(loss_part[0, 0], ("x", "y", "c"))
    outs = [loss, dx.reshape(x.shape)]
    for k in range(4):
        for n in names:
            if n == "conv_w":
                outs.append(conv_res[k].reshape(conv_w.shape))
            elif n in BIG:
                outs.append(res[k][n])
            else:
                outs.append(res_s[k][n])
    return tuple(outs)
```

```python
import functools
import math

import jax
import jax.numpy as jnp
from jax import lax
from jax.experimental import pallas as pl
from jax.experimental.pallas import tpu as pltpu

f32 = jnp.float32
bf16 = jnp.bfloat16

D_MODEL = 1024
CONV_WIDTH = 512
SSM_WIDTH = 512
SSM_GROUP = 16
SSM_GROUPS = 32
SSM_STATE = 64
N_STATES = SSM_GROUPS * SSM_STATE
XATTN_HEADS = 4
XATTN_HEAD_DIM = 128
XATTN_WIDTH = 512
D_FF = 4096
GATE_COLS = 3 * D_MODEL
IN_COLS = GATE_COLS + 3 * CONV_WIDTH + SSM_WIDTH + XATTN_WIDTH
ALPHA = 2.0 ** 0.25
LN_EPS = 1e-5
ADAM_LR = 0.001
ADAM_B1 = 0.9
ADAM_B2 = 0.999
ADAM_EPS = 1e-08
ADAM_WD = 0.01
ADAM_STEP = 10

N_SEG = 8
SCAN_ROWS = 256
LANE_STRIP = 512
TM = 512
VMEM_LIMIT = 48 * 1024 * 1024
MESH = pl.DeviceIdType.MESH

NT_DIMS = (((1,), (1,)), ((), ()))
TN_DIMS = (((0,), (0,)), ((), ()))


def _params():
    return pltpu.CompilerParams(vmem_limit_bytes=VMEM_LIMIT)


def _full(shape):
    n = len(shape)
    return pl.BlockSpec(shape, lambda *_: (0,) * n)


def _rows(tm, w, cb=0):
    return pl.BlockSpec((tm, w), lambda i: (i, cb))


def _sig(x):
    return 1.0 / (1.0 + jnp.exp(-x))


HBM_SPEC = pl.BlockSpec(memory_space=pl.ANY)
RIDE_PEERS = {"gather": 3, "scatter": 3, "all": 7, "pair": 1}


def _xy_peers(x, y):
    return [(1 - x, y), (x, 1 - y), (1 - x, 1 - y)]


def _ride_copies(kind, src, dst, send_sems, recv_sems, local_sem):
    x, y, c = lax.axis_index("x"), lax.axis_index("y"), lax.axis_index("c")
    me = 2 * x + y
    if kind == "all":
        flips = [(fx, fy, fc) for fx in (0, 1) for fy in (0, 1) for fc in (0, 1)][1:]
        peers = [(x ^ fx, y ^ fy, c ^ fc) for fx, fy, fc in flips]
        slot = lambda p: 4 * p[0] + 2 * p[1] + p[2]
        mine = slot((x, y, c))
    else:
        peers = [(px, py, c) for px, py in _xy_peers(x, y)]
        slot = lambda p: 2 * p[0] + p[1]
        mine = me

    def remote(k, s, d):
        return pltpu.make_async_remote_copy(src_ref=s, dst_ref=d, send_sem=send_sems.at[k], recv_sem=recv_sems.at[k],
                                            device_id=peers[k], device_id_type=MESH)

    if kind == "pair":
        peers = [(x, y, 1 - c)]
        return None, [remote(0, src, dst)], [remote(0, src, dst)]
    if kind == "scatter":
        local = pltpu.make_async_copy(src.at[me], dst.at[me], local_sem)
        sends = [remote(k, src.at[slot(p)], dst.at[me]) for k, p in enumerate(peers)]
        lands = [remote(k, src.at[me], dst.at[slot(p)]) for k, p in enumerate(peers)]
    else:
        local = pltpu.make_async_copy(src, dst.at[mine], local_sem)
        sends = [remote(k, src, dst.at[mine]) for k, p in enumerate(peers)]
        lands = [remote(k, src, dst.at[slot(p)]) for k, p in enumerate(peers)]
    return local, sends, lands


def _ride_shape(kind, src):
    lead = {"gather": (4,), "scatter": (), "all": (8,), "pair": ()}[kind]
    return jax.ShapeDtypeStruct(lead + src.shape, src.dtype)


def _pcall(body, args, *, name, grid, in_specs, out_specs, out_shape, scratch_shapes=(), rides=(), aliases=None):
    n_in, n_out, n_scr, nr = len(in_specs), len(out_specs), len(scratch_shapes), len(rides)
    kinds = [k for k, _ in rides]

    def wrapped(*refs):
        ins, rsrc = refs[:n_in], refs[n_in:n_in + nr]
        outs = refs[n_in + nr:n_in + nr + n_out]
        rdst = refs[n_in + nr + n_out:n_in + 2 * nr + n_out]
        scr = refs[n_in + 2 * nr + n_out:n_in + 2 * nr + n_out + n_scr]
        sems = refs[n_in + 2 * nr + n_out + n_scr:]
        def start():
            for r, kind in enumerate(kinds):
                local, sends, _ = _ride_copies(kind, rsrc[r], rdst[r], *sems[3 * r:3 * r + 3])
                if local is not None:
                    local.start()
                for cp in sends:
                    cp.start()

        def finish():
            for r, kind in enumerate(kinds):
                local, sends, lands = _ride_copies(kind, rsrc[r], rdst[r], *sems[3 * r:3 * r + 3])
                for cp in lands:
                    cp.wait_recv()
                for cp in sends:
                    cp.wait_send()
                if local is not None:
                    local.wait()

        if nr and grid:
            ids = [pl.program_id(a) for a in range(len(grid))]
            first = functools.reduce(jnp.logical_and, [i == 0 for i in ids])
            last = functools.reduce(jnp.logical_and, [i == g - 1 for i, g in zip(ids, grid)])
            pl.when(first)(start)
        elif nr:
            start()
        body(*ins, *outs, *scr)
        if nr and grid:
            pl.when(last)(finish)
        elif nr:
            finish()

    sems = []
    for kind in kinds:
        n = RIDE_PEERS[kind]
        sems += [pltpu.SemaphoreType.DMA((n,)), pltpu.SemaphoreType.DMA((n,)), pltpu.SemaphoreType.DMA(())]
    res = pl.pallas_call(
        wrapped, name=name, grid=grid, in_specs=list(in_specs) + [HBM_SPEC] * nr,
        out_specs=list(out_specs) + [HBM_SPEC] * nr,
        out_shape=list(out_shape) + [_ride_shape(k, s) for k, s in rides],
        scratch_shapes=list(scratch_shapes) + sems, input_output_aliases=aliases or {},
        compiler_params=_params())(*args, *[s for _, s in rides])
    return list(res[:n_out]), list(res[n_out:])


def _bf(v):
    return v if v.dtype == bf16 else v.astype(bf16)


def mm_fwd(a, w3, name, *, tm=1024, tn=None, bias=None, outs=(f32,), second=None, rides=()):
    M, K = a.shape
    J, _, n = w3.shape
    tm = min(tm, M)
    tn = tn or n
    nl = n // tn
    nb = 0 if bias is None else 1

    def body(*refs):
        a_ref, w_ref = refs[0], refs[1]
        acc = jnp.dot(_bf(a_ref[...]), w_ref[0], preferred_element_type=f32)
        if bias is not None:
            acc = acc + refs[2][...]
        refs[2 + nb][...] = acc.astype(outs[0])
        if len(outs) > 1:
            refs[3 + nb][...] = second(acc).astype(outs[1])

    in_specs = [pl.BlockSpec((tm, K), lambda j, l, i: (i, 0)),
                pl.BlockSpec((1, K, tn), lambda j, l, i: (j, 0, l))]
    args = [a, w3]
    if bias is not None:
        in_specs.append(pl.BlockSpec((1, tn), lambda j, l, i: (0, j * nl + l)))
        args.append(bias)
    res, landed = _pcall(
        body, args, name=name, grid=(J, nl, M // tm), in_specs=in_specs,
        out_specs=[pl.BlockSpec((tm, tn), lambda j, l, i: (i, j * nl + l)) for _ in outs],
        out_shape=[jax.ShapeDtypeStruct((M, J * n), dt) for dt in outs], rides=rides)
    res = res if len(outs) > 1 else res[0]
    return (res, landed) if rides else res


def mm_bwd_x(dy, w3, name, *, tm=TM, epi=None, extras=(), colsum=False, out_dtype=f32, rides=()):
    M = dy.shape[0]
    J, K, n = w3.shape
    tm = min(tm, M)
    nex = len(extras)

    def body(*refs):
        dy_ref, w_hbm = refs[0], refs[1]
        ex = refs[2:2 + nex]
        o_ref = refs[2 + nex]
        w_ref = refs[-1]
        i = pl.program_id(0)

        @pl.when(i == 0)
        def _():
            pltpu.sync_copy(w_hbm, w_ref)

        acc = None
        for j in range(J):
            part = lax.dot_general(_bf(dy_ref[:, j * n:(j + 1) * n]), w_ref[j], NT_DIMS, preferred_element_type=f32)
            acc = part if acc is None else acc + part
        if epi is not None:
            acc = epi(acc, *[e[...] for e in ex])
        o_ref[...] = acc.astype(out_dtype)
        if colsum:
            s_ref = refs[3 + nex]
            cs = jnp.sum(acc, axis=0, keepdims=True)

            @pl.when(i == 0)
            def _():
                s_ref[...] = cs

            @pl.when(i > 0)
            def _():
                s_ref[...] += cs

    in_specs = [pl.BlockSpec((tm, J * n), lambda i: (i, 0)), pl.BlockSpec(memory_space=pl.ANY)]
    in_specs += [pl.BlockSpec((tm, K), lambda i: (i, 0)) for _ in extras]
    out_specs = [pl.BlockSpec((tm, K), lambda i: (i, 0))]
    out_shape = [jax.ShapeDtypeStruct((M, K), out_dtype)]
    if colsum:
        out_specs.append(pl.BlockSpec((1, K), lambda i: (0, 0)))
        out_shape.append(jax.ShapeDtypeStruct((1, K), f32))
    res, landed = _pcall(body, [dy, w3, *extras], name=name, grid=(M // tm,), in_specs=in_specs, out_specs=out_specs,
                         out_shape=out_shape, scratch_shapes=[pltpu.VMEM((J, K, n), bf16)], rides=rides)
    res = res if colsum else res[0]
    return (res, landed) if rides else res


def mm_bwd_w(a, dy, J, name, *, tm=2048, tn=None, tk=None, rides=()):
    M, K = a.shape
    n = dy.shape[1] // J
    tm = min(tm, M)
    tn = tn or n
    tk = tk or K
    nl = n // tn
    nk = K // tk
    ns = M // tm

    def body(a_ref, dy_ref, o_ref, acc_ref):
        s = pl.program_id(3)
        part = lax.dot_general(_bf(a_ref[...]), _bf(dy_ref[...]), TN_DIMS, preferred_element_type=f32)

        @pl.when(s == 0)
        def _():
            acc_ref[...] = part

        @pl.when(s > 0)
        def _():
            acc_ref[...] += part

        @pl.when(s == ns - 1)
        def _():
            o_ref[0] = acc_ref[...].astype(bf16)

    res, landed = _pcall(
        body, [a, dy], name=name, grid=(J, nl, nk, ns),
        in_specs=[pl.BlockSpec((tm, tk), lambda j, l, k, s: (s, k)),
                  pl.BlockSpec((tm, tn), lambda j, l, k, s: (s, j * nl + l))],
        out_specs=[pl.BlockSpec((1, tk, tn), lambda j, l, k, s: (j, k, l))],
        out_shape=[jax.ShapeDtypeStruct((J, K, n), bf16)],
        scratch_shapes=[pltpu.VMEM((tk, tn), f32)], rides=rides)
    return (res[0], landed) if rides else res[0]


def _relu2(v):
    r = jnp.maximum(v, 0.0)
    return r * r


def _shift_down(z, k, halo):
    r = lax.broadcasted_iota(jnp.int32, z.shape, 0)
    y = pltpu.roll(z, k, 0)
    for q in range(k):
        y = jnp.where(r == q, halo[8 - k + q:8 - k + q + 1, :], y)
    return y


def _shift_up(z, k, halo):
    tm = z.shape[0]
    r = lax.broadcasted_iota(jnp.int32, z.shape, 0)
    y = pltpu.roll(z, tm - k, 0)
    for q in range(k):
        y = jnp.where(r == tm - k + q, halo[q:q + 1, :], y)
    return y


def _prev_halo(tm, cb):
    return pl.BlockSpec((8, CONV_WIDTH), lambda i: (jnp.maximum(i * (tm // 8) - 1, 0), cb))


def _next_halo(tm, cb, nrow8):
    return pl.BlockSpec((8, CONV_WIDTH), lambda i: (jnp.minimum((i + 1) * (tm // 8), nrow8 - 1), cb))


def conv_fwd(proj, conv_w, tm=TM):
    S = proj.shape[0]

    def body(cb_ref, cc_ref, ch_ref, cch_ref, chh_ref, w_ref, a_ref):
        i = pl.program_id(0)
        z = cc_ref[...] * ch_ref[...]
        zh = jnp.where(i == 0, 0.0, cch_ref[...] * chh_ref[...])
        w = w_ref[...]
        dwz = w[0:1, :] * _shift_down(z, 2, zh) + w[1:2, :] * _shift_down(z, 1, zh) + w[2:3, :] * z
        a_ref[...] = (cb_ref[...] * dwz).astype(bf16)

    return pl.pallas_call(
        body, name="conv_fwd", grid=(S // tm,),
        in_specs=[_rows(tm, CONV_WIDTH, 6), _rows(tm, CONV_WIDTH, 7), _rows(tm, CONV_WIDTH, 8),
                  _prev_halo(tm, 7), _prev_halo(tm, 8), _full((3, CONV_WIDTH))],
        out_specs=_rows(tm, CONV_WIDTH),
        out_shape=jax.ShapeDtypeStruct((S, CONV_WIDTH), bf16), compiler_params=_params())(
            proj, proj, proj, proj, proj, conv_w)


def conv_bwd(da, proj, conv_w, dproj, tm=TM):
    S = proj.shape[0]
    nt = S // tm

    def body(da_ref, cb_ref, cc_ref, ch_ref, cch_ref, chh_ref, dan_ref, cbn_ref, w_ref, _, o_ref, dw_ref):
        i = pl.program_id(0)
        cc, ch, cb, dav = cc_ref[...], ch_ref[...], cb_ref[...], da_ref[...]
        z = cc * ch
        zh = jnp.where(i == 0, 0.0, cch_ref[...] * chh_ref[...])
        w = w_ref[...]
        z1 = _shift_down(z, 1, zh)
        z2 = _shift_down(z, 2, zh)
        dwz = w[0:1, :] * z2 + w[1:2, :] * z1 + w[2:3, :] * z
        g = dav * cb
        gn = jnp.where(i == nt - 1, 0.0, dan_ref[...] * cbn_ref[...])
        dz = w[2:3, :] * g + w[1:2, :] * _shift_up(g, 1, gn) + w[0:1, :] * _shift_up(g, 2, gn)
        o_ref[:, 0:CONV_WIDTH] = (dav * dwz).astype(bf16)
        o_ref[:, CONV_WIDTH:2 * CONV_WIDTH] = (dz * ch).astype(bf16)
        o_ref[:, 2 * CONV_WIDTH:3 * CONV_WIDTH] = (dz * cc).astype(bf16)

        @pl.when(i == 0)
        def _():
            dw_ref[...] = jnp.zeros_like(dw_ref)

        dw_ref[0:1, :] += jnp.sum(g * z2, axis=0, keepdims=True)
        dw_ref[1:2, :] += jnp.sum(g * z1, axis=0, keepdims=True)
        dw_ref[2:3, :] += jnp.sum(g * z, axis=0, keepdims=True)

    return pl.pallas_call(
        body, name="conv_bwd", grid=(nt,),
        in_specs=[_rows(tm, CONV_WIDTH), _rows(tm, CONV_WIDTH, 6), _rows(tm, CONV_WIDTH, 7), _rows(tm, CONV_WIDTH, 8),
                  _prev_halo(tm, 7), _prev_halo(tm, 8),
                  pl.BlockSpec((8, CONV_WIDTH), lambda i: (jnp.minimum((i + 1) * (tm // 8), S // 8 - 1), 0)),
                  _next_halo(tm, 6, S // 8), _full((3, CONV_WIDTH)), pl.BlockSpec(memory_space=pl.ANY)],
        out_specs=[_rows(tm, 3 * CONV_WIDTH, GATE_COLS // (3 * CONV_WIDTH)), _full((3, CONV_WIDTH))],
        out_shape=[jax.ShapeDtypeStruct(dproj.shape, bf16), jax.ShapeDtypeStruct((3, CONV_WIDTH), f32)],
        input_output_aliases={9: 0},
        compiler_params=_params())(da, proj, proj, proj, proj, proj, da, proj, conv_w, dproj)


def _cmul(ar, ai, br, bi):
    return ar * br - ai * bi, ar * bi + ai * br


def ssm_prep(lr, li, ldt, bt_re, bt_im):
    def body(lr_ref, li_ref, ldt_ref, br_ref, bi_ref, ar_ref, ai_ref, kr_ref, ki_ref, bbr_ref, bbi_ref):
        lrv, liv = lr_ref[...], li_ref[...]
        dt = jnp.exp(ldt_ref[...])
        mag = jnp.exp(lrv * dt)
        ar = mag * jnp.cos(liv * dt)
        ai = mag * jnp.sin(liv * dt)
        den = lrv * lrv + liv * liv
        nr = ar - 1.0
        kr = (nr * lrv + ai * liv) / den
        ki = (ai * lrv - nr * liv) / den
        ar_ref[...], ai_ref[...], kr_ref[...], ki_ref[...] = ar, ai, kr, ki
        bbr_ref[...] = kr * br_ref[...] - ki * bi_ref[...]
        bbi_ref[...] = kr * bi_ref[...] + ki * br_ref[...]

    v = jax.ShapeDtypeStruct((1, N_STATES), f32)
    m = jax.ShapeDtypeStruct((SSM_GROUP, N_STATES), f32)
    return pl.pallas_call(body, name="ssm_prep", out_shape=[v, v, v, v, m, m])(lr, li, ldt, bt_re, bt_im)


def _pow_segment(ar, ai, seg_len):
    pr, pi = ar, ai
    for _ in range(int(math.log2(seg_len))):
        pr, pi = _cmul(pr, pi, pr, pi)
    return pr, pi


GELU_K = math.sqrt(2.0 / math.pi)
GELU_C = 0.044715


def _gelu(v):
    return 0.5 * v * (1.0 + jnp.tanh(GELU_K * (v + GELU_C * v * v * v)))


def _gelu_grad(v):
    t = jnp.tanh(GELU_K * (v + GELU_C * v * v * v))
    return 0.5 * (1.0 + t) + 0.5 * v * (1.0 - t * t) * GELU_K * (1.0 + 3.0 * GELU_C * v * v)


def ssm_scan_fwd(u_perm, bd_r, bd_i, ar, ai, *, final=None, rides=()):
    S = u_perm.shape[0]
    R = SCAN_ROWS
    nblk = S // R
    seg_len = S // N_SEG
    nstrip = N_STATES // LANE_STRIP
    store = final is not None

    def body(*refs):
        if store:
            (u_ref, bdr_ref, bdi_ref, ar_ref, ai_ref, cdr_ref, cdi_ref, d_ref, ir_ref, ii_ref,
             y_ref, ys_ref, sr_ref, si_ref, bur, bui, car, cai) = refs
        else:
            u_ref, bdr_ref, bdi_ref, ar_ref, ai_ref, or_ref, oi_ref, bur, bui, car, cai = refs
        i = pl.program_id(0)
        u = u_ref[...]
        ub = u.astype(bf16)
        for cb in range(4):
            us = ub[:, cb * 128:(cb + 1) * 128]
            bur[:, cb * 512:(cb + 1) * 512] = jnp.dot(us, bdr_ref[cb], preferred_element_type=f32)
            bui[:, cb * 512:(cb + 1) * 512] = jnp.dot(us, bdi_ref[cb], preferred_element_type=f32)

        @pl.when(i == 0)
        def _():
            if store:
                car[...] = ir_ref[...]
                cai[...] = ii_ref[...]
            else:
                car[...] = jnp.zeros_like(car)
                cai[...] = jnp.zeros_like(cai)

        for ls in range(nstrip):
            lanes = pl.ds(ls * LANE_STRIP, LANE_STRIP)
            a_r = jnp.broadcast_to(ar_ref[:, lanes], (N_SEG, LANE_STRIP))
            a_i = jnp.broadcast_to(ai_ref[:, lanes], (N_SEG, LANE_STRIP))

            def step(t, carry, lanes=lanes, a_r=a_r, a_i=a_i):
                s_r, s_i = carry
                row = pl.multiple_of(t * 8, 8)
                n_r = a_r * s_r - a_i * s_i + bur[pl.ds(row, 8), lanes]
                n_i = a_r * s_i + a_i * s_r + bui[pl.ds(row, 8), lanes]
                if store:
                    sr_ref[pl.ds(row, 8), lanes] = n_r
                    si_ref[pl.ds(row, 8), lanes] = n_i
                return n_r, n_i

            e_r, e_i = lax.fori_loop(0, R // 8, step, (car[:, lanes], cai[:, lanes]), unroll=4)
            car[:, lanes] = e_r
            cai[:, lanes] = e_i

        if store:
            for cb in range(4):
                st_r = sr_ref[:, cb * 512:(cb + 1) * 512].astype(bf16)
                st_i = si_ref[:, cb * 512:(cb + 1) * 512].astype(bf16)
                y = (jnp.dot(st_r, cdr_ref[cb], preferred_element_type=f32)
                     - jnp.dot(st_i, cdi_ref[cb], preferred_element_type=f32))
                cols = slice(cb * 128, (cb + 1) * 128)
                y = y + d_ref[:, cols] * u[:, cols]
                y_ref[:, cols] = y
                ys_ref[:, cols] = _gelu(y).astype(bf16)
        else:
            @pl.when(i == nblk - 1)
            def _():
                p_r, p_i = _pow_segment(ar_ref[...], ai_ref[...], seg_len)
                t_r, t_i = car[0:1, :], cai[0:1, :]
                or_ref[0:1, :] = jnp.zeros((1, N_STATES), f32)
                oi_ref[0:1, :] = jnp.zeros((1, N_STATES), f32)
                for j in range(1, N_SEG):
                    or_ref[j:j + 1, :] = t_r
                    oi_ref[j:j + 1, :] = t_i
                    m_r, m_i = _cmul(p_r, p_i, t_r, t_i)
                    t_r, t_i = car[j:j + 1, :] + m_r, cai[j:j + 1, :] + m_i

    blk = lambda w: pl.BlockSpec((R, w), lambda i: (i, 0))
    in_specs = [blk(SSM_WIDTH), _full((4, 128, 512)), _full((4, 128, 512)), _full((1, N_STATES)), _full((1, N_STATES))]
    args = [u_perm, bd_r, bd_i, ar, ai]
    scratch = [pltpu.VMEM((R, N_STATES), f32), pltpu.VMEM((R, N_STATES), f32),
               pltpu.VMEM((N_SEG, N_STATES), f32), pltpu.VMEM((N_SEG, N_STATES), f32)]
    if store:
        in_specs += [_full((4, 512, 128)), _full((4, 512, 128)), _full((1, SSM_WIDTH)),
                     _full((N_SEG, N_STATES)), _full((N_SEG, N_STATES))]
        args += list(final)
        out_specs = [blk(SSM_WIDTH), blk(SSM_WIDTH), blk(N_STATES), blk(N_STATES)]
        out_shape = [jax.ShapeDtypeStruct((S, SSM_WIDTH), f32), jax.ShapeDtypeStruct((S, SSM_WIDTH), bf16),
                     jax.ShapeDtypeStruct((S, N_STATES), f32), jax.ShapeDtypeStruct((S, N_STATES), f32)]
        name = "ssm_scan_fwd"
    else:
        out_specs = [_full((N_SEG, N_STATES)), _full((N_SEG, N_STATES))]
        out_shape = [jax.ShapeDtypeStruct((N_SEG, N_STATES), f32)] * 2
        name = "ssm_scan_fwd_carry"
    res, landed = _pcall(body, args, name=name, grid=(nblk,), in_specs=in_specs, out_specs=out_specs,
                         out_shape=out_shape, scratch_shapes=scratch, rides=rides)
    return (res, landed) if rides else res


def ssm_scan_bwd(dys_perm, y_perm, cdt_r, cdt_i, ar, ai, *, final=None, rides=()):
    S = dys_perm.shape[0]
    R = SCAN_ROWS
    nblk = S // R
    seg_len = S // N_SEG
    nstrip = N_STATES // LANE_STRIP
    store = final is not None

    def body(*refs):
        if store:
            (dys_ref, y_ref, cdr_ref, cdi_ref, ar_ref, ai_ref, u_ref, sr_ref, si_ref, bdr_ref, bdi_ref, d_ref, ir_ref,
             ii_ref, du_ref, dbr_ref, dbi_ref, dcr_ref, dci_ref, dar_ref, dai_ref, dd_ref, dsr, dsi, lmr, lmi, car,
             cai) = refs
        else:
            dys_ref, y_ref, cdr_ref, cdi_ref, ar_ref, ai_ref, or_ref, oi_ref, dsr, dsi, car, cai = refs
        i = pl.program_id(0)
        dy = dys_ref[...] * _gelu_grad(y_ref[...])
        dyb = dy.astype(bf16)
        for cb in range(4):
            ds_ = dyb[:, cb * 128:(cb + 1) * 128]
            dsr[:, cb * 512:(cb + 1) * 512] = jnp.dot(ds_, cdr_ref[cb], preferred_element_type=f32)
            dsi[:, cb * 512:(cb + 1) * 512] = -jnp.dot(ds_, cdi_ref[cb], preferred_element_type=f32)

        @pl.when(i == 0)
        def _():
            if store:
                car[...] = ir_ref[...]
                cai[...] = ii_ref[...]
                dar_ref[...] = jnp.zeros_like(dar_ref)
                dai_ref[...] = jnp.zeros_like(dai_ref)
                dbr_ref[...] = jnp.zeros_like(dbr_ref)
                dbi_ref[...] = jnp.zeros_like(dbi_ref)
                dcr_ref[...] = jnp.zeros_like(dcr_ref)
                dci_ref[...] = jnp.zeros_like(dci_ref)
                dd_ref[...] = jnp.zeros_like(dd_ref)
            else:
                car[...] = jnp.zeros_like(car)
                cai[...] = jnp.zeros_like(cai)

        for ls in range(nstrip):
            lanes = pl.ds(ls * LANE_STRIP, LANE_STRIP)
            a_r = jnp.broadcast_to(ar_ref[:, lanes], (N_SEG, LANE_STRIP))
            a_i = jnp.broadcast_to(ai_ref[:, lanes], (N_SEG, LANE_STRIP))
            zero = jnp.zeros((N_SEG, LANE_STRIP), f32)

            def step(k, carry, lanes=lanes, a_r=a_r, a_i=a_i):
                l_r, l_i, g_r, g_i = carry
                row = pl.multiple_of((R // 8 - 1 - k) * 8, 8)
                if store:
                    s_r = sr_ref[pl.ds(row, 8), lanes]
                    s_i = si_ref[pl.ds(row, 8), lanes]
                    g_r = g_r + l_r * s_r + l_i * s_i
                    g_i = g_i + l_i * s_r - l_r * s_i
                n_r = dsr[pl.ds(row, 8), lanes] + a_r * l_r + a_i * l_i
                n_i = dsi[pl.ds(row, 8), lanes] + a_r * l_i - a_i * l_r
                if store:
                    lmr[pl.ds(row, 8), lanes] = n_r
                    lmi[pl.ds(row, 8), lanes] = n_i
                return n_r, n_i, g_r, g_i

            e_r, e_i, g_r, g_i = lax.fori_loop(0, R // 8, step, (car[:, lanes], cai[:, lanes], zero, zero), unroll=4)
            car[:, lanes] = e_r
            cai[:, lanes] = e_i
            if store:
                dar_ref[:, lanes] += g_r
                dai_ref[:, lanes] += g_i

        if store:
            u = u_ref[...]
            ub = u.astype(bf16)
            for cb in range(4):
                cols = slice(cb * 128, (cb + 1) * 128)
                st = slice(cb * 512, (cb + 1) * 512)
                l_r = lmr[:, st].astype(bf16)
                l_i = lmi[:, st].astype(bf16)
                du = (jnp.dot(l_r, bdr_ref[cb], preferred_element_type=f32)
                      + jnp.dot(l_i, bdi_ref[cb], preferred_element_type=f32))
                du_ref[:, cols] = (du + d_ref[:, cols] * dy[:, cols]).astype(bf16)
                dbr_ref[cb] += lax.dot_general(l_r, ub[:, cols], TN_DIMS, preferred_element_type=f32)
                dbi_ref[cb] += lax.dot_general(l_i, ub[:, cols], TN_DIMS, preferred_element_type=f32)
                dcr_ref[cb] += lax.dot_general(dyb[:, cols], sr_ref[:, st].astype(bf16), TN_DIMS,
                                               preferred_element_type=f32)
                dci_ref[cb] += lax.dot_general(dyb[:, cols], si_ref[:, st].astype(bf16), TN_DIMS,
                                               preferred_element_type=f32)
            dd_ref[...] += jnp.sum(dy * u, axis=0, keepdims=True)
        else:
            @pl.when(i == nblk - 1)
            def _():
                p_r, p_i = _pow_segment(ar_ref[...], ai_ref[...], seg_len)
                p_i = -p_i
                t_r, t_i = car[N_SEG - 1:N_SEG, :], cai[N_SEG - 1:N_SEG, :]
                or_ref[N_SEG - 1:N_SEG, :] = jnp.zeros((1, N_STATES), f32)
                oi_ref[N_SEG - 1:N_SEG, :] = jnp.zeros((1, N_STATES), f32)
                for j in range(N_SEG - 2, -1, -1):
                    or_ref[j:j + 1, :] = t_r
                    oi_ref[j:j + 1, :] = t_i
                    m_r, m_i = _cmul(p_r, p_i, t_r, t_i)
                    t_r, t_i = car[j:j + 1, :] + m_r, cai[j:j + 1, :] + m_i

    blk = lambda w: pl.BlockSpec((R, w), lambda i: (nblk - 1 - i, 0))
    in_specs = [blk(SSM_WIDTH), blk(SSM_WIDTH), _full((4, 128, 512)), _full((4, 128, 512)), _full((1, N_STATES)),
                _full((1, N_STATES))]
    args = [dys_perm, y_perm, cdt_r, cdt_i, ar, ai]
    seg = jax.ShapeDtypeStruct((N_SEG, N_STATES), f32)
    if store:
        in_specs += [blk(SSM_WIDTH), blk(N_STATES), blk(N_STATES), _full((4, 512, 128)), _full((4, 512, 128)),
                     _full((1, SSM_WIDTH)), _full((N_SEG, N_STATES)), _full((N_SEG, N_STATES))]
        args += list(final)
        out_specs = [blk(SSM_WIDTH), _full((4, 512, 128)), _full((4, 512, 128)), _full((4, 128, 512)),
                     _full((4, 128, 512)), _full((N_SEG, N_STATES)), _full((N_SEG, N_STATES)), _full((1, SSM_WIDTH))]
        b_acc = jax.ShapeDtypeStruct((4, 512, 128), f32)
        c_acc = jax.ShapeDtypeStruct((4, 128, 512), f32)
        out_shape = [jax.ShapeDtypeStruct((S, SSM_WIDTH), bf16), b_acc, b_acc, c_acc, c_acc, seg, seg,
                     jax.ShapeDtypeStruct((1, SSM_WIDTH), f32)]
        scratch = [pltpu.VMEM((R, N_STATES), f32)] * 4 + [pltpu.VMEM((N_SEG, N_STATES), f32)] * 2
        name = "ssm_scan_bwd"
    else:
        out_specs = [_full((N_SEG, N_STATES)), _full((N_SEG, N_STATES))]
        out_shape = [seg, seg]
        scratch = [pltpu.VMEM((R, N_STATES), f32)] * 2 + [pltpu.VMEM((N_SEG, N_STATES), f32)] * 2
        name = "ssm_scan_bwd_carry"
    res, landed = _pcall(body, args, name=name, grid=(nblk,), in_specs=in_specs, out_specs=out_specs,
                         out_shape=out_shape, scratch_shapes=scratch, rides=rides)
    return (res, landed) if rides else res


def ssm_param_bwd(dbb_r, dbb_i, bt_re, bt_im, kr, ki, ar, ai, lr, li, ldt, da_r, da_i):
    def body(dbr_ref, dbi_ref, br_ref, bi_ref, kr_ref, ki_ref, ar_ref, ai_ref, lr_ref, li_ref, ldt_ref, dar_ref,
             dai_ref, obr_ref, obi_ref, olr_ref, oli_ref, odt_ref):
        dbr, dbi, b_r, b_i = dbr_ref[...], dbi_ref[...], br_ref[...], bi_ref[...]
        k_r, k_i, a_r, a_i = kr_ref[...], ki_ref[...], ar_ref[...], ai_ref[...]
        l_r, l_i = lr_ref[...], li_ref[...]
        dt = jnp.exp(ldt_ref[...])
        obr_ref[...] = k_r * dbr + k_i * dbi
        obi_ref[...] = k_r * dbi - k_i * dbr
        gk_r = jnp.sum(dbr * b_r + dbi * b_i, axis=0, keepdims=True)
        gk_i = jnp.sum(dbi * b_r - dbr * b_i, axis=0, keepdims=True)
        ga_r = jnp.sum(dar_ref[...], axis=0, keepdims=True)
        ga_i = jnp.sum(dai_ref[...], axis=0, keepdims=True)
        den = l_r * l_r + l_i * l_i
        c_r, c_i = l_r / den, l_i / den
        m_r, m_i = _cmul(c_r, c_i, gk_r, gk_i)
        g_r, g_i = ga_r + m_r, ga_i + m_i
        t1_r, t1_i = _cmul(dt * a_r, -dt * a_i, g_r, g_i)
        q_r, q_i = _cmul(k_r, k_i, c_r, -c_i)
        t2_r, t2_i = _cmul(-q_r, q_i, gk_r, gk_i)
        olr_ref[...] = t1_r + t2_r
        oli_ref[...] = t1_i + t2_i
        w_r, w_i = _cmul(l_r, l_i, a_r, a_i)
        odt_ref[...] = dt * (w_r * g_r + w_i * g_i)

    v = jax.ShapeDtypeStruct((1, N_STATES), f32)
    m = jax.ShapeDtypeStruct((SSM_GROUP, N_STATES), f32)
    return pl.pallas_call(body, name="ssm_param_bwd", out_shape=[m, m, v, v, v])(
        dbb_r, dbb_i, bt_re, bt_im, kr, ki, ar, ai, lr, li, ldt, da_r, da_i)


def group_sum(v):
    def body(v_ref, o_ref):
        o_ref[...] = jnp.sum(v_ref[...], axis=-1, keepdims=True)
    return pl.pallas_call(body, name="group_sum", out_shape=jax.ShapeDtypeStruct((v.shape[0], 1), f32))(v)


ATTN_SCALE = XATTN_HEAD_DIM ** -0.5


def _attn_probs(q_h, k_h):
    s = lax.dot_general(q_h, k_h, NT_DIMS, preferred_element_type=f32) * ATTN_SCALE
    e = jnp.exp(s - jnp.max(s, axis=-1, keepdims=True))
    return e / jnp.sum(e, axis=-1, keepdims=True)


def attn_fwd(proj, kv, tm=TM):
    S = proj.shape[0]
    M = kv.shape[0]

    def body(q_ref, kv_ref, o_ref):
        for h in range(XATTN_HEADS):
            cols = slice(h * XATTN_HEAD_DIM, (h + 1) * XATTN_HEAD_DIM)
            q_h = q_ref[:, cols].astype(bf16)
            k_h = kv_ref[:, cols]
            v_h = kv_ref[:, XATTN_WIDTH + h * XATTN_HEAD_DIM:XATTN_WIDTH + (h + 1) * XATTN_HEAD_DIM]
            p = _attn_probs(q_h, k_h)
            o_ref[:, cols] = jnp.dot(p.astype(bf16), v_h, preferred_element_type=f32).astype(bf16)

    return pl.pallas_call(
        body, name="attn_fwd", grid=(S // tm,), in_specs=[_rows(tm, XATTN_WIDTH, 10), _full((M, 2 * XATTN_WIDTH))],
        out_specs=_rows(tm, XATTN_WIDTH), out_shape=jax.ShapeDtypeStruct((S, XATTN_WIDTH), bf16),
        compiler_params=_params())(proj, kv)


def attn_bwd(do, proj, kv, dproj, tm=TM):
    S = proj.shape[0]
    M = kv.shape[0]

    def body(do_ref, q_ref, kv_ref, _, dq_ref, dkv_ref):
        i = pl.program_id(0)

        @pl.when(i == 0)
        def _():
            dkv_ref[...] = jnp.zeros_like(dkv_ref)

        for h in range(XATTN_HEADS):
            cols = slice(h * XATTN_HEAD_DIM, (h + 1) * XATTN_HEAD_DIM)
            vcols = slice(XATTN_WIDTH + h * XATTN_HEAD_DIM, XATTN_WIDTH + (h + 1) * XATTN_HEAD_DIM)
            q_h = q_ref[:, cols].astype(bf16)
            k_h = kv_ref[:, cols]
            v_h = kv_ref[:, vcols]
            do_h = do_ref[:, cols]
            p = _attn_probs(q_h, k_h)
            dp = lax.dot_general(do_h, v_h, NT_DIMS, preferred_element_type=f32)
            ds = (p * (dp - jnp.sum(dp * p, axis=-1, keepdims=True)) * ATTN_SCALE).astype(bf16)
            dq_ref[:, cols] = jnp.dot(ds, k_h, preferred_element_type=f32).astype(bf16)
            dkv_ref[:, cols] += lax.dot_general(ds, q_h, TN_DIMS, preferred_element_type=f32)
            dkv_ref[:, vcols] += lax.dot_general(p.astype(bf16), do_h, TN_DIMS, preferred_element_type=f32)

    return pl.pallas_call(
        body, name="attn_bwd", grid=(S // tm,),
        in_specs=[_rows(tm, XATTN_WIDTH), _rows(tm, XATTN_WIDTH, 10), _full((M, 2 * XATTN_WIDTH)),
                  pl.BlockSpec(memory_space=pl.ANY)],
        out_specs=[_rows(tm, XATTN_WIDTH, 10), _full((M, 2 * XATTN_WIDTH))],
        out_shape=[jax.ShapeDtypeStruct(dproj.shape, bf16), jax.ShapeDtypeStruct((M, 2 * XATTN_WIDTH), f32)],
        input_output_aliases={3: 0},
        compiler_params=_params())(do, proj, kv, dproj)


def merge_fwd(proj, b_gate, y_a, glu, y_c, tm=TM, rides=()):
    S = proj.shape[0]

    def body(g0_ref, g1_ref, g2_ref, b_ref, ya_ref, ga_ref, gb_ref, yc_ref, o_ref):
        b = b_ref[...]
        g0 = _sig(g0_ref[...] + b[:, 0:D_MODEL])
        g1 = _sig(g1_ref[...] + b[:, D_MODEL:2 * D_MODEL])
        g2 = _sig(g2_ref[...] + b[:, 2 * D_MODEL:3 * D_MODEL])
        y_b = ga_ref[...] * _sig(gb_ref[...])
        o_ref[...] = (g0 * ya_ref[...] + g1 * y_b + g2 * yc_ref[...]).astype(bf16)

    res, landed = _pcall(
        body, [proj, proj, proj, b_gate, y_a, glu, glu, y_c], name="merge_fwd", grid=(S // tm,),
        in_specs=[_rows(tm, D_MODEL, 0), _rows(tm, D_MODEL, 1), _rows(tm, D_MODEL, 2), _full((1, GATE_COLS)),
                  _rows(tm, D_MODEL), _rows(tm, D_MODEL, 0), _rows(tm, D_MODEL, 1), _rows(tm, D_MODEL)],
        out_specs=[_rows(tm, D_MODEL)], out_shape=[jax.ShapeDtypeStruct((S, D_MODEL), bf16)], rides=rides)
    return (res[0], landed) if rides else res[0]


def merge_bwd(dm, proj, b_gate, y_a, glu, y_c, tm=256, rides=()):
    S = proj.shape[0]

    def body(dm_ref, g0_ref, g1_ref, g2_ref, b_ref, ya_ref, ga_ref, gb_ref, yc_ref,
             dg_ref, dya_ref, dgl_ref, dyc_ref, dbg_ref):
        i = pl.program_id(0)
        b = b_ref[...]
        dmv = dm_ref[...]
        ga, sb = ga_ref[...], _sig(gb_ref[...])
        ys = (ya_ref[...], ga * sb, yc_ref[...])
        gs = tuple(_sig(r[...] + b[:, k * D_MODEL:(k + 1) * D_MODEL]) for k, r in enumerate((g0_ref, g1_ref, g2_ref)))

        @pl.when(i == 0)
        def _():
            dbg_ref[...] = jnp.zeros_like(dbg_ref)

        for k in range(3):
            dpre = dmv * ys[k] * gs[k] * (1.0 - gs[k])
            dg_ref[:, k * D_MODEL:(k + 1) * D_MODEL] = dpre.astype(bf16)
            dbg_ref[:, k * D_MODEL:(k + 1) * D_MODEL] += jnp.sum(dpre, axis=0, keepdims=True)
        dya_ref[...] = (dmv * gs[0]).astype(bf16)
        dyc_ref[...] = (dmv * gs[2]).astype(bf16)
        dyb = dmv * gs[1]
        dgl_ref[:, 0:D_MODEL] = (dyb * sb).astype(bf16)
        dgl_ref[:, D_MODEL:2 * D_MODEL] = (dyb * ga * sb * (1.0 - sb)).astype(bf16)

    res, landed = _pcall(
        body, [dm, proj, proj, proj, b_gate, y_a, glu, glu, y_c], name="merge_bwd", grid=(S // tm,),
        in_specs=[_rows(tm, D_MODEL), _rows(tm, D_MODEL, 0), _rows(tm, D_MODEL, 1), _rows(tm, D_MODEL, 2),
                  _full((1, GATE_COLS)), _rows(tm, D_MODEL), _rows(tm, D_MODEL, 0), _rows(tm, D_MODEL, 1),
                  _rows(tm, D_MODEL)],
        out_specs=[_rows(tm, GATE_COLS), _rows(tm, D_MODEL), _rows(tm, 2 * D_MODEL), _rows(tm, D_MODEL),
                   _full((1, GATE_COLS))],
        out_shape=[jax.ShapeDtypeStruct((S, IN_COLS), bf16), jax.ShapeDtypeStruct((S, D_MODEL), bf16),
                   jax.ShapeDtypeStruct((S, 2 * D_MODEL), bf16), jax.ShapeDtypeStruct((S, D_MODEL), bf16),
                   jax.ShapeDtypeStruct((1, GATE_COLS), f32)], rides=rides)
    return (res, landed) if rides else res


def _ln_stats(r):
    mu = jnp.mean(r, axis=-1, keepdims=True)
    xc = r - mu
    var = jnp.mean(xc * xc, axis=-1, keepdims=True)
    rstd = lax.rsqrt(var + LN_EPS)
    return xc * rstd, rstd


def _ln_bwd(dy, xhat, rstd, g):
    dxh = dy * g
    return rstd * (dxh - jnp.mean(dxh, axis=-1, keepdims=True) - xhat * jnp.mean(dxh * xhat, axis=-1, keepdims=True))


def ln1_fwd(x, t1, g, b, tm=TM):
    S = x.shape[0]

    def body(x_ref, t_ref, g_ref, b_ref, o_ref, ob_ref):
        xhat, _ = _ln_stats(ALPHA * x_ref[...] + t_ref[...])
        h = xhat * g_ref[...] + b_ref[...]
        o_ref[...] = h
        ob_ref[...] = h.astype(bf16)

    return pl.pallas_call(
        body, name="ln1_fwd", grid=(S // tm,),
        in_specs=[_rows(tm, D_MODEL), _rows(tm, D_MODEL), _full((1, D_MODEL)), _full((1, D_MODEL))],
        out_specs=[_rows(tm, D_MODEL), _rows(tm, D_MODEL)],
        out_shape=[jax.ShapeDtypeStruct((S, D_MODEL), f32), jax.ShapeDtypeStruct((S, D_MODEL), bf16)],
        compiler_params=_params())(x, t1, g, b)


def ln2_loss_bwd(h1, t2, b_down, g, b, target, tm=TM):
    S = h1.shape[0]

    def body(h_ref, t_ref, bd_ref, g_ref, b_ref, y_ref, dr_ref, drb_ref, l_ref, dg_ref, db_ref, dbd_ref):
        i = pl.program_id(0)
        xhat, rstd = _ln_stats(ALPHA * h_ref[...] + t_ref[...] + bd_ref[...])
        gv = g_ref[...]
        err = xhat * gv + b_ref[...] - y_ref[...]
        dout = err * (1.0 / D_MODEL)
        dr = _ln_bwd(dout, xhat, rstd, gv)
        dr_ref[...] = dr
        drb_ref[...] = dr.astype(bf16)
        part = 0.5 * jnp.sum(jnp.sum(err * err, axis=-1, keepdims=True) * (1.0 / D_MODEL), axis=0, keepdims=True)

        @pl.when(i == 0)
        def _():
            l_ref[...] = jnp.zeros_like(l_ref)
            dg_ref[...] = jnp.zeros_like(dg_ref)
            db_ref[...] = jnp.zeros_like(db_ref)
            dbd_ref[...] = jnp.zeros_like(dbd_ref)

        l_ref[...] += jnp.broadcast_to(part, l_ref.shape)
        dg_ref[...] += jnp.sum(dout * xhat, axis=0, keepdims=True)
        db_ref[...] += jnp.sum(dout, axis=0, keepdims=True)
        dbd_ref[...] += jnp.sum(dr, axis=0, keepdims=True)

    vec = jax.ShapeDtypeStruct((1, D_MODEL), f32)
    return pl.pallas_call(
        body, name="ln2_loss_bwd", grid=(S // tm,),
        in_specs=[_rows(tm, D_MODEL), _rows(tm, D_MODEL), _full((1, D_MODEL)), _full((1, D_MODEL)),
                  _full((1, D_MODEL)), _rows(tm, D_MODEL)],
        out_specs=[_rows(tm, D_MODEL), _rows(tm, D_MODEL), _full((1, 128)), _full((1, D_MODEL)), _full((1, D_MODEL)),
                   _full((1, D_MODEL))],
        out_shape=[jax.ShapeDtypeStruct((S, D_MODEL), f32), jax.ShapeDtypeStruct((S, D_MODEL), bf16),
                   jax.ShapeDtypeStruct((1, 128), f32), vec, vec, vec],
        compiler_params=_params())(h1, t2, b_down, g, b, target)


def ln1_bwd(x, t1, dr2, t3, g, tm=TM):
    S = x.shape[0]

    def body(x_ref, t_ref, dr2_ref, t3_ref, g_ref, dr_ref, drb_ref, dg_ref, db_ref):
        i = pl.program_id(0)
        xhat, rstd = _ln_stats(ALPHA * x_ref[...] + t_ref[...])
        dh = ALPHA * dr2_ref[...] + t3_ref[...]
        dr = _ln_bwd(dh, xhat, rstd, g_ref[...])
        dr_ref[...] = dr
        drb_ref[...] = dr.astype(bf16)

        @pl.when(i == 0)
        def _():
            dg_ref[...] = jnp.zeros_like(dg_ref)
            db_ref[...] = jnp.zeros_like(db_ref)

        dg_ref[...] += jnp.sum(dh * xhat, axis=0, keepdims=True)
        db_ref[...] += jnp.sum(dh, axis=0, keepdims=True)

    vec = jax.ShapeDtypeStruct((1, D_MODEL), f32)
    return pl.pallas_call(
        body, name="ln1_bwd", grid=(S // tm,),
        in_specs=[_rows(tm, D_MODEL)] * 4 + [_full((1, D_MODEL))],
        out_specs=[_rows(tm, D_MODEL), _rows(tm, D_MODEL), _full((1, D_MODEL)), _full((1, D_MODEL))],
        out_shape=[jax.ShapeDtypeStruct((S, D_MODEL), f32), jax.ShapeDtypeStruct((S, D_MODEL), bf16), vec, vec],
        compiler_params=_params())(x, t1, dr2, t3, g)


def exchange(*rides, name):
    return _pcall(lambda: None, [], name=name, grid=(), in_specs=[], out_specs=[], out_shape=[], rides=rides)[1]


def gather_two_level(shard, name):
    K, n = shard.shape
    h = K // 2

    def body(in_ref, out_ref, ici_send, ici_recv, d2d_send, d2d_recv, local_sem):
        x, y, c = lax.axis_index("x"), lax.axis_index("y"), lax.axis_index("c")
        me = 2 * x + y
        mine = pl.ds(pl.multiple_of(c * h, 16), h)
        theirs = pl.ds(pl.multiple_of((1 - c) * h, 16), h)
        peers = _xy_peers(x, y)

        def over_ici(k, dst):
            return pltpu.make_async_remote_copy(src_ref=in_ref.at[mine], dst_ref=dst, send_sem=ici_send.at[k],
                                                recv_sem=ici_recv.at[k], device_id=(*peers[k], c), device_id_type=MESH)

        def over_d2d(k, rows):
            blk = out_ref.at[2 * peers[k][0] + peers[k][1], rows]
            return pltpu.make_async_remote_copy(src_ref=blk, dst_ref=blk, send_sem=d2d_send.at[k],
                                                recv_sem=d2d_recv.at[k], device_id=(x, y, 1 - c), device_id_type=MESH)

        local = pltpu.make_async_copy(in_ref, out_ref.at[me], local_sem)
        local.start()
        started = [over_ici(k, out_ref.at[me, mine]) for k in range(3)]
        for cp in started:
            cp.start()
        for k in range(3):
            over_ici(k, out_ref.at[2 * peers[k][0] + peers[k][1], mine]).wait_recv()
            passed = over_d2d(k, mine)
            passed.start()
            started.append(passed)
        for k in range(3):
            over_d2d(k, theirs).wait_recv()
        for cp in started:
            cp.wait_send()
        local.wait()

    return pl.pallas_call(
        body, name=name, in_specs=[HBM_SPEC], out_specs=HBM_SPEC,
        out_shape=jax.ShapeDtypeStruct((4, K, n), shard.dtype),
        scratch_shapes=[pltpu.SemaphoreType.DMA((3,))] * 4 + [pltpu.SemaphoreType.DMA(())])(shard)


ROW_TILE = 256


def sum_slots(recv, name):
    n, R, C = recv.shape
    tm = min(R, ROW_TILE)

    def body(r_ref, o_ref):
        acc = r_ref[0].astype(f32)
        for k in range(1, n):
            acc = acc + r_ref[k].astype(f32)
        o_ref[...] = acc

    return pl.pallas_call(
        body, name=name, grid=(R // tm,), in_specs=[pl.BlockSpec((n, tm, C), lambda i: (0, i, 0))],
        out_specs=_rows(tm, C), out_shape=jax.ShapeDtypeStruct((R, C), f32), compiler_params=_params())(recv)


def _adamw(w, g, m, v):
    m = ADAM_B1 * m + (1.0 - ADAM_B1) * g
    v = ADAM_B2 * v + (1.0 - ADAM_B2) * (g * g)
    m_hat = m / (1.0 - ADAM_B1 ** ADAM_STEP)
    v_hat = v / (1.0 - ADAM_B2 ** ADAM_STEP)
    delta = -ADAM_LR * (m_hat / (jnp.sqrt(v_hat) + ADAM_EPS) + ADAM_WD * w)
    return delta, m, v


def adam_pair(p, q, w, m, v, name):
    R, C = w.shape
    tm = min(R, ROW_TILE)

    def body(p_ref, q_ref, w_ref, m_ref, v_ref, g_ref, d_ref, nm_ref, nv_ref):
        g = p_ref[...] + q_ref[...]
        g_ref[...] = g
        d_ref[...], nm_ref[...], nv_ref[...] = _adamw(w_ref[...], g, m_ref[...], v_ref[...])

    o = jax.ShapeDtypeStruct((R, C), f32)
    return pl.pallas_call(body, name=name, grid=(R // tm,), in_specs=[_rows(tm, C)] * 5,
                          out_specs=[_rows(tm, C)] * 4, out_shape=[o] * 4, compiler_params=_params())(p, q, w, m, v)


def sum_small(recv):
    n = recv.shape[0]

    def body(r_ref, o_ref):
        g = r_ref[0]
        for k in range(1, n):
            g = g + r_ref[k]
        o_ref[...] = g

    return pl.pallas_call(body, name="sum_small", out_shape=jax.ShapeDtypeStruct(recv.shape[1:], f32),
                          compiler_params=_params())(recv)


def adam_plain(w, g, m, v, name):
    def body(w_ref, g_ref, m_ref, v_ref, d_ref, nm_ref, nv_ref):
        d_ref[...], nm_ref[...], nv_ref[...] = _adamw(w_ref[...], g_ref[...], m_ref[...], v_ref[...])

    o = jax.ShapeDtypeStruct(w.shape, f32)
    return pl.pallas_call(body, name=name, out_shape=[o] * 3, compiler_params=_params())(w, g, m, v)


def pass_through(a):
    def body(a_ref, o_ref):
        del a_ref, o_ref

    return pl.pallas_call(body, name="pass_through", in_specs=[HBM_SPEC], out_specs=HBM_SPEC,
                          out_shape=jax.ShapeDtypeStruct(a.shape, a.dtype), input_output_aliases={0: 0})(a)


BIG = ("w_in", "w_conv_out", "w_glu", "w_xattn_out", "w_kv", "w_out", "w_up", "w_down")
MID = ("w_conv_out", "w_glu", "w_xattn_out", "w_kv", "w_out")
SMALL = ("b_gate", "ssm_lam_re", "ssm_lam_im", "ssm_log_dt", "ssm_b_re", "ssm_b_im", "ssm_c_re", "ssm_c_im", "ssm_d",
         "ln1_g", "ln1_b", "b_up", "b_down", "ln2_g", "ln2_b")
def _pad_flat(a, mult=1024):
    a = a.reshape(-1)
    return jnp.pad(a, (0, (-a.shape[0]) % mult))


def _perm(a):
    S, W = a.shape
    return a.reshape(N_SEG, S // N_SEG, W).transpose(1, 0, 2).reshape(S, W)


def _unperm(a):
    S, W = a.shape
    return a.reshape(S // N_SEG, N_SEG, W).transpose(1, 0, 2).reshape(S, W)


def _state_rows(a):
    return a.transpose(2, 0, 1).reshape(SSM_GROUP, N_STATES)


def _block_diag_b(bt):
    b4 = bt.reshape(SSM_GROUP, 4, 8, SSM_STATE)
    eye = jnp.eye(8, dtype=bt.dtype)
    return jnp.einsum("hcgp,gk->cghkp", b4, eye).reshape(4, 128, 512)


def _block_diag_c(c):
    c4 = c.reshape(4, 8, SSM_GROUP, SSM_STATE)
    eye = jnp.eye(8, dtype=c.dtype)
    return jnp.einsum("cghp,gk->cgpkh", c4, eye).reshape(4, 512, 128)


def _diag_b(acc):
    a = acc.reshape(4, 8, SSM_STATE, 8, SSM_GROUP)
    eye = jnp.eye(8, dtype=acc.dtype)
    return jnp.einsum("cgpkh,gk->hcgp", a, eye).reshape(SSM_GROUP, N_STATES)


def _diag_c(acc):
    a = acc.reshape(4, 8, SSM_GROUP, 8, SSM_STATE)
    eye = jnp.eye(8, dtype=acc.dtype)
    return jnp.einsum("cghkp,gk->cghp", a, eye).reshape(SSM_GROUPS, SSM_GROUP, SSM_STATE)


def kernel(x, mem, w_in, b_gate, conv_w, w_conv_out, ssm_lam_re, ssm_lam_im, ssm_log_dt, ssm_b_re, ssm_b_im, ssm_c_re, ssm_c_im, ssm_d, w_glu, w_kv, w_xattn_out, w_out, ln1_g, ln1_b, w_up, b_up, w_down, b_down, ln2_g, ln2_b, loss_target, m_w_in, m_b_gate, m_conv_w, m_w_conv_out, m_ssm_lam_re, m_ssm_lam_im, m_ssm_log_dt, m_ssm_b_re, m_ssm_b_im, m_ssm_c_re, m_ssm_c_im, m_ssm_d, m_w_glu, m_w_kv, m_w_xattn_out, m_w_out, m_ln1_g, m_ln1_b, m_w_up, m_b_up, m_w_down, m_b_down, m_ln2_g, m_ln2_b, v_w_in, v_b_gate, v_conv_w, v_w_conv_out, v_ssm_lam_re, v_ssm_lam_im, v_ssm_log_dt, v_ssm_b_re, v_ssm_b_im, v_ssm_c_re, v_ssm_c_im, v_ssm_d, v_w_glu, v_w_kv, v_w_xattn_out, v_w_out, v_ln1_g, v_ln1_b, v_w_up, v_b_up, v_w_down, v_b_down, v_ln2_g, v_ln2_b):
    W = dict(w_in=w_in, b_gate=b_gate, conv_w=conv_w, w_conv_out=w_conv_out, ssm_lam_re=ssm_lam_re,
             ssm_lam_im=ssm_lam_im, ssm_log_dt=ssm_log_dt, ssm_b_re=ssm_b_re, ssm_b_im=ssm_b_im, ssm_c_re=ssm_c_re,
             ssm_c_im=ssm_c_im, ssm_d=ssm_d, w_glu=w_glu, w_kv=w_kv, w_xattn_out=w_xattn_out, w_out=w_out,
             ln1_g=ln1_g, ln1_b=ln1_b, w_up=w_up, b_up=b_up, w_down=w_down, b_down=b_down, ln2_g=ln2_g, ln2_b=ln2_b)
    MOM = dict(w_in=m_w_in, b_gate=m_b_gate, conv_w=m_conv_w, w_conv_out=m_w_conv_out, ssm_lam_re=m_ssm_lam_re,
               ssm_lam_im=m_ssm_lam_im, ssm_log_dt=m_ssm_log_dt, ssm_b_re=m_ssm_b_re, ssm_b_im=m_ssm_b_im,
               ssm_c_re=m_ssm_c_re, ssm_c_im=m_ssm_c_im, ssm_d=m_ssm_d, w_glu=m_w_glu, w_kv=m_w_kv,
               w_xattn_out=m_w_xattn_out, w_out=m_w_out, ln1_g=m_ln1_g, ln1_b=m_ln1_b, w_up=m_w_up, b_up=m_b_up,
               w_down=m_w_down, b_down=m_b_down, ln2_g=m_ln2_g, ln2_b=m_ln2_b)
    VEL = dict(w_in=v_w_in, b_gate=v_b_gate, conv_w=v_conv_w, w_conv_out=v_w_conv_out, ssm_lam_re=v_ssm_lam_re,
               ssm_lam_im=v_ssm_lam_im, ssm_log_dt=v_ssm_log_dt, ssm_b_re=v_ssm_b_re, ssm_b_im=v_ssm_b_im,
               ssm_c_re=v_ssm_c_re, ssm_c_im=v_ssm_c_im, ssm_d=v_ssm_d, w_glu=v_w_glu, w_kv=v_w_kv,
               w_xattn_out=v_w_xattn_out, w_out=v_w_out, ln1_g=v_ln1_g, ln1_b=v_ln1_b, w_up=v_w_up, b_up=v_b_up,
               w_down=v_w_down, b_down=v_b_down, ln2_g=v_ln2_g, ln2_b=v_ln2_b)
    names = list(W)
    xy = 2 * lax.axis_index("x") + lax.axis_index("y")

    xs = x[0]
    S = xs.shape[0]
    mems = mem[0]
    tgt = loss_target[0]

    shard_bf = {n: W[n][0].astype(bf16) for n in BIG}
    win3 = gather_two_level(shard_bf["w_in"], "gather_w_in")

    lr = ssm_lam_re.reshape(1, N_STATES)
    li = ssm_lam_im.reshape(1, N_STATES)
    ldt = jnp.repeat(ssm_log_dt.reshape(SSM_GROUPS), SSM_STATE).reshape(1, N_STATES)
    bt_re, bt_im = _state_rows(ssm_b_re[0]), _state_rows(ssm_b_im[0])
    ar, ai, kr, ki, bbt_r, bbt_i = ssm_prep(lr, li, ldt, bt_re, bt_im)
    bd_r, bd_i = _block_diag_b(bbt_r).astype(bf16), _block_diag_b(bbt_i).astype(bf16)
    cd_r, cd_i = _block_diag_c(ssm_c_re[0]).astype(bf16), _block_diag_c(ssm_c_im[0]).astype(bf16)
    bdt_r, bdt_i = bd_r.transpose(0, 2, 1), bd_i.transpose(0, 2, 1)
    cdt_r, cdt_i = cd_r.transpose(0, 2, 1), cd_i.transpose(0, 2, 1)
    d_skip = ssm_d.reshape(1, SSM_WIDTH)

    x_bf = xs.astype(bf16)
    mem_bf = mems.astype(bf16)
    u_col = GATE_COLS + 3 * CONV_WIDTH
    proj, (wco3, wglu3, wxo3, wkv4, wout4, convw4) = mm_fwd(
        x_bf, win3, "proj", rides=tuple(("gather", shard_bf[n]) for n in MID) + (("gather", conv_w[0]),))
    wkv3 = wkv4.reshape(1, D_MODEL, 2 * XATTN_WIDTH)
    wout3 = wout4.reshape(1, D_MODEL, D_MODEL)
    convw_full = convw4.transpose(1, 0, 2).reshape(3, CONV_WIDTH)
    a_conv = conv_fwd(proj, convw_full)
    y_a = mm_fwd(a_conv, wco3, "conv_out")
    u_perm = _perm(proj[:, u_col:u_col + SSM_WIDTH])
    half = D_MODEL // 2
    (init_r, init_i), (wup_a,) = ssm_scan_fwd(u_perm, bd_r, bd_i, ar, ai,
                                              rides=(("gather", shard_bf["w_up"][:half]),))
    (ysm_perm, ys_perm, st_r, st_i), (wup_b,) = ssm_scan_fwd(
        u_perm, bd_r, bd_i, ar, ai, final=(cd_r, cd_i, d_skip, init_r, init_i),
        rides=(("gather", shard_bf["w_up"][half:]),))
    wup3 = jnp.concatenate([wup_a, wup_b], axis=1)
    y_s = _unperm(ys_perm)
    glu = mm_fwd(y_s, wglu3, "glu")
    kv = mm_fwd(mem_bf, wkv3, "kv", outs=(bf16,))
    o_att = attn_fwd(proj, kv)
    y_c = mm_fwd(o_att, wxo3, "xattn_out")
    merged = merge_fwd(proj, b_gate, y_a, glu, y_c)
    t1 = mm_fwd(merged, wout3, "w_out")
    h1, h1_bf = ln1_fwd(xs, t1, ln1_g, ln1_b)
    (up, hdn), (wdn4,) = mm_fwd(h1_bf, wup3, "w_up", bias=b_up, outs=(f32, bf16), second=_relu2,
                                rides=(("gather", shard_bf["w_down"]),))
    wdn3 = wdn4.reshape(1, D_FF, D_MODEL)
    t2 = mm_fwd(hdn, wdn3, "w_down", tn=512)

    dr2, dr2_bf, loss_part, d_ln2_g, d_ln2_b, d_b_down = ln2_loss_bwd(h1, t2, b_down, ln2_g, ln2_b, tgt)
    part, other = {}, {}
    g_w_down = mm_bwd_w(hdn, dr2_bf, 1, "dw_down", tk=1024).reshape(4, -1, D_MODEL)
    (dup, d_b_up), (recv_dn,) = mm_bwd_x(dr2_bf, wdn3, "dup", tm=256, extras=(up,), colsum=True, out_dtype=bf16,
                                         epi=lambda acc, upv: acc * (2.0 * jnp.maximum(upv, 0.0)),
                                         rides=(("scatter", g_w_down),))
    part["w_down"] = sum_slots(recv_dn, "sum_w_down")
    g_w_up = mm_bwd_w(h1_bf, dup, 4, "dw_up")
    t3 = mm_bwd_x(dup, wup3, "dh1")
    dr1, dr1_bf, d_ln1_g, d_ln1_b = ln1_bwd(xs, t1, dr2, t3, ln1_g)
    g_w_out = mm_bwd_w(merged, dr1_bf, 1, "dw_out").reshape(4, -1, D_MODEL)
    dmerged, (recv_out,) = mm_bwd_x(dr1_bf, wout3, "dmerged", tm=1024, rides=(("scatter", g_w_out),))
    part["w_out"] = sum_slots(recv_out, "sum_w_out")
    (dproj, dy_a, dglu, dy_c, d_b_gate), (recv_up,) = merge_bwd(dmerged, proj, b_gate, y_a, glu, y_c,
                                                                rides=(("scatter", g_w_up),))
    part["w_up"] = sum_slots(recv_up, "sum_w_up")

    g_w_co = mm_bwd_w(a_conv, dy_a, 4, "dw_conv_out")
    da_conv = mm_bwd_x(dy_a, wco3, "da_conv", tm=1024)
    dproj, d_conv_w = conv_bwd(da_conv, proj, convw_full, dproj)

    g_w_glu = mm_bwd_w(y_s, dglu, 4, "dw_glu")
    dys_perm = _perm(mm_bwd_x(dglu, wglu3, "dy_s", tm=1024))
    linit_r, linit_i = ssm_scan_bwd(dys_perm, ysm_perm, cdt_r, cdt_i, ar, ai)
    (du_perm, dbacc_r, dbacc_i, dcacc_r, dcacc_i, da_r, da_i, d_ssm_d), (recv_co, recv_glu) = ssm_scan_bwd(
        dys_perm, ysm_perm, cdt_r, cdt_i, ar, ai,
        final=(u_perm, st_r, st_i, bdt_r, bdt_i, d_skip, linit_r, linit_i),
        rides=(("scatter", g_w_co), ("scatter", g_w_glu)))
    part["w_conv_out"] = sum_slots(recv_co, "sum_w_conv_out")
    part["w_glu"] = sum_slots(recv_glu, "sum_w_glu")
    dbt_re, dbt_im, d_lr, d_li, d_ldt_state = ssm_param_bwd(
        _diag_b(dbacc_r), _diag_b(dbacc_i), bt_re, bt_im, kr, ki, ar, ai, lr, li, ldt, da_r, da_i)
    d_log_dt = group_sum(d_ldt_state.reshape(SSM_GROUPS, SSM_STATE))
    d_b_re = dbt_re.reshape(SSM_GROUP, SSM_GROUPS, SSM_STATE).transpose(1, 2, 0)
    d_b_im = dbt_im.reshape(SSM_GROUP, SSM_GROUPS, SSM_STATE).transpose(1, 2, 0)
    d_c_re = _diag_c(dcacc_r)
    d_c_im = -_diag_c(dcacc_i)

    g_w_xo = mm_bwd_w(o_att, dy_c, 4, "dw_xattn_out")
    do_att = mm_bwd_x(dy_c, wxo3, "do_att", tm=1024, out_dtype=bf16)
    dproj, dkv = attn_bwd(do_att, proj, kv, dproj)
    g_w_kv = mm_bwd_w(mem_bf, dkv, 1, "dw_kv").reshape(4, -1, D_MODEL)

    dproj = lax.dynamic_update_slice(dproj, _unperm(du_perm), (0, u_col))
    g_small = {"b_gate": d_b_gate, "ssm_lam_re": d_lr, "ssm_lam_im": d_li, "ssm_log_dt": d_log_dt, "ssm_b_re": d_b_re,
               "ssm_b_im": d_b_im, "ssm_c_re": d_c_re, "ssm_c_im": d_c_im, "ssm_d": d_ssm_d, "ln1_g": d_ln1_g,
               "ln1_b": d_ln1_b, "b_up": d_b_up, "b_down": d_b_down, "ln2_g": d_ln2_g, "ln2_b": d_ln2_b}
    small_names = SMALL + ("conv_w",)
    g_small["conv_w"] = d_conv_w
    sizes = {n: (g_small[n].size + 1023) // 1024 * 1024 for n in small_names}
    pack = lambda d: jnp.concatenate([_pad_flat(d[n]) for n in small_names]).reshape(-1, 128)
    early = ("w_down", "w_up", "w_out", "w_conv_out", "w_glu")
    g_w_in, landed = mm_bwd_w(
        x_bf, dproj, 4, "dw_in", rides=(("scatter", g_w_xo), ("scatter", g_w_kv), ("all", pack(g_small)))
        + tuple(("pair", part[n]) for n in early))
    recv_xo, recv_kv, srecv = landed[:3]
    other.update(zip(early, landed[3:]))
    part["w_xattn_out"] = sum_slots(recv_xo, "sum_w_xattn_out")
    part["w_kv"] = sum_slots(recv_kv, "sum_w_kv")
    dx, (recv_in, other["w_xattn_out"], other["w_kv"]) = mm_bwd_x(
        dproj, win3, "dx", extras=(dr1,), epi=lambda acc, d: acc + ALPHA * d,
        rides=(("scatter", g_w_in), ("pair", part["w_xattn_out"]), ("pair", part["w_kv"])))
    part["w_in"] = sum_slots(recv_in, "sum_w_in")
    other["w_in"], = exchange(("pair", part["w_in"]), name="swap_w_in")
    res = [{}, {}, {}, {}]
    for n in BIG:
        for k, r in enumerate(adam_pair(part[n], other[n], W[n][0], MOM[n][0], VEL[n][0], "adam_" + n)):
            res[k][n] = r[None]
    g_flat = sum_small(srecv).reshape(-1)
    off = 0
    for n in small_names:
        g_n = g_flat[off:off + g_small[n].size]
        off += sizes[n]
        if n == "conv_w":
            g_n = lax.dynamic_slice(g_n.reshape(3, CONV_WIDTH), (0, xy * 128), (3, 128))
        shape2 = (-1, W[n].shape[-1])
        rs = adam_plain(W[n].reshape(shape2), g_n.reshape(shape2), MOM[n].reshape(shape2), VEL[n].reshape(shape2),
                        "adam_" + n)
        for k, r in enumerate((g_n,) + tuple(rs)):
            res[k][n] = r.reshape(W[n].shape)

    loss = lax.psum(loss_part[0, 0], ("x", "y", "c"))
    outs = [loss, pass_through(dx).reshape(x.shape)]
    for k in range(4):
        for n in names:
            outs.append(res[k][n])
    return tuple(outs)
```

```python
import functools
import math

import jax
import jax.numpy as jnp
from jax import lax
from jax.experimental import pallas as pl
from jax.experimental.pallas import tpu as pltpu

f32 = jnp.float32
bf16 = jnp.bfloat16

D_MODEL = 1024
CONV_WIDTH = 512
SSM_WIDTH = 512
SSM_GROUP = 16
SSM_GROUPS = 32
SSM_STATE = 64
N_STATES = SSM_GROUPS * SSM_STATE
XATTN_HEADS = 4
XATTN_HEAD_DIM = 128
XATTN_WIDTH = 512
D_FF = 4096
GATE_COLS = 3 * D_MODEL
IN_COLS = GATE_COLS + 3 * CONV_WIDTH + SSM_WIDTH + XATTN_WIDTH
ALPHA = 2.0 ** 0.25
LN_EPS = 1e-5
ADAM_LR = 0.001
ADAM_B1 = 0.9
ADAM_B2 = 0.999
ADAM_EPS = 1e-08
ADAM_WD = 0.01
ADAM_STEP = 10

N_SEG = 8
SCAN_ROWS = 256
LANE_STRIP = 512
TM = 512
VMEM_LIMIT = 48 * 1024 * 1024
MESH = pl.DeviceIdType.MESH

NT_DIMS = (((1,), (1,)), ((), ()))
TN_DIMS = (((0,), (0,)), ((), ()))


def _params():
    return pltpu.CompilerParams(vmem_limit_bytes=VMEM_LIMIT)


def _full(shape):
    n = len(shape)
    return pl.BlockSpec(shape, lambda *_: (0,) * n)


def _rows(tm, w, cb=0):
    return pl.BlockSpec((tm, w), lambda i: (i, cb))


def _sig(x):
    return 1.0 / (1.0 + jnp.exp(-x))


HBM_SPEC = pl.BlockSpec(memory_space=pl.ANY)
RIDE_PEERS = {"gather": 3, "scatter": 3, "all": 7, "pair": 1}


def _xy_peers(x, y):
    return [(1 - x, y), (x, 1 - y), (1 - x, 1 - y)]


def _ride_copies(kind, src, dst, send_sems, recv_sems, local_sem):
    x, y, c = lax.axis_index("x"), lax.axis_index("y"), lax.axis_index("c")
    me = 2 * x + y
    if kind == "all":
        flips = [(fx, fy, fc) for fx in (0, 1) for fy in (0, 1) for fc in (0, 1)][1:]
        peers = [(x ^ fx, y ^ fy, c ^ fc) for fx, fy, fc in flips]
        slot = lambda p: 4 * p[0] + 2 * p[1] + p[2]
        mine = slot((x, y, c))
    else:
        peers = [(px, py, c) for px, py in _xy_peers(x, y)]
        slot = lambda p: 2 * p[0] + p[1]
        mine = me

    def remote(k, s, d):
        return pltpu.make_async_remote_copy(src_ref=s, dst_ref=d, send_sem=send_sems.at[k], recv_sem=recv_sems.at[k],
                                            device_id=peers[k], device_id_type=MESH)

    if kind == "pair":
        peers = [(x, y, 1 - c)]
        return None, [remote(0, src, dst)], [remote(0, src, dst)]
    if kind == "scatter":
        local = pltpu.make_async_copy(src.at[me], dst.at[me], local_sem)
        sends = [remote(k, src.at[slot(p)], dst.at[me]) for k, p in enumerate(peers)]
        lands = [remote(k, src.at[me], dst.at[slot(p)]) for k, p in enumerate(peers)]
    else:
        local = pltpu.make_async_copy(src, dst.at[mine], local_sem)
        sends = [remote(k, src, dst.at[mine]) for k, p in enumerate(peers)]
        lands = [remote(k, src, dst.at[slot(p)]) for k, p in enumerate(peers)]
    return local, sends, lands


def _ride_shape(kind, src):
    lead = {"gather": (4,), "scatter": (), "all": (8,), "pair": ()}[kind]
    return jax.ShapeDtypeStruct(lead + src.shape, src.dtype)


def _pcall(body, args, *, name, grid, in_specs, out_specs, out_shape, scratch_shapes=(), rides=(), aliases=None):
    n_in, n_out, n_scr, nr = len(in_specs), len(out_specs), len(scratch_shapes), len(rides)
    kinds = [k for k, _ in rides]

    def wrapped(*refs):
        ins, rsrc = refs[:n_in], refs[n_in:n_in + nr]
        outs = refs[n_in + nr:n_in + nr + n_out]
        rdst = refs[n_in + nr + n_out:n_in + 2 * nr + n_out]
        scr = refs[n_in + 2 * nr + n_out:n_in + 2 * nr + n_out + n_scr]
        sems = refs[n_in + 2 * nr + n_out + n_scr:]
        def start():
            for r, kind in enumerate(kinds):
                local, sends, _ = _ride_copies(kind, rsrc[r], rdst[r], *sems[3 * r:3 * r + 3])
                if local is not None:
                    local.start()
                for cp in sends:
                    cp.start()

        def finish():
            for r, kind in enumerate(kinds):
                local, sends, lands = _ride_copies(kind, rsrc[r], rdst[r], *sems[3 * r:3 * r + 3])
                for cp in lands:
                    cp.wait_recv()
                for cp in sends:
                    cp.wait_send()
                if local is not None:
                    local.wait()

        if nr and grid:
            ids = [pl.program_id(a) for a in range(len(grid))]
            first = functools.reduce(jnp.logical_and, [i == 0 for i in ids])
            last = functools.reduce(jnp.logical_and, [i == g - 1 for i, g in zip(ids, grid)])
            pl.when(first)(start)
        elif nr:
            start()
        body(*ins, *outs, *scr)
        if nr and grid:
            pl.when(last)(finish)
        elif nr:
            finish()

    sems = []
    for kind in kinds:
        n = RIDE_PEERS[kind]
        sems += [pltpu.SemaphoreType.DMA((n,)), pltpu.SemaphoreType.DMA((n,)), pltpu.SemaphoreType.DMA(())]
    res = pl.pallas_call(
        wrapped, name=name, grid=grid, in_specs=list(in_specs) + [HBM_SPEC] * nr,
        out_specs=list(out_specs) + [HBM_SPEC] * nr,
        out_shape=list(out_shape) + [_ride_shape(k, s) for k, s in rides],
        scratch_shapes=list(scratch_shapes) + sems, input_output_aliases=aliases or {},
        compiler_params=_params())(*args, *[s for _, s in rides])
    return list(res[:n_out]), list(res[n_out:])


def _bf(v):
    return v if v.dtype == bf16 else v.astype(bf16)


def mm_fwd_small(a, w3, name, *, tm=1024, out_dtype=f32):
    M, K = a.shape
    J, _, n = w3.shape
    tm = min(tm, M)

    def body(a_ref, w_ref, o_ref):
        av = _bf(a_ref[...])
        for j in range(J):
            o_ref[:, j * n:(j + 1) * n] = jnp.dot(av, w_ref[j], preferred_element_type=f32).astype(out_dtype)

    return pl.pallas_call(
        body, name=name, grid=(M // tm,), in_specs=[_rows(tm, K), _full((J, K, n))], out_specs=_rows(tm, J * n),
        out_shape=jax.ShapeDtypeStruct((M, J * n), out_dtype), compiler_params=_params())(a, w3)


def mm_bwd_w_small(a, dy, J, name, *, tm=2048):
    M, K = a.shape
    n = dy.shape[1] // J
    tm = min(tm, M)
    ns = M // tm

    def body(a_ref, dy_ref, o_ref, acc_ref):
        s = pl.program_id(0)
        part = lax.dot_general(_bf(a_ref[...]), _bf(dy_ref[...]), TN_DIMS, preferred_element_type=f32)

        @pl.when(s == 0)
        def _():
            acc_ref[...] = part

        @pl.when(s > 0)
        def _():
            acc_ref[...] += part

        @pl.when(s == ns - 1)
        def _():
            for j in range(J):
                o_ref[j] = acc_ref[:, j * n:(j + 1) * n].astype(bf16)

    return pl.pallas_call(
        body, name=name, grid=(ns,), in_specs=[_rows(tm, K), _rows(tm, J * n)], out_specs=_full((J, K, n)),
        out_shape=jax.ShapeDtypeStruct((J, K, n), bf16), scratch_shapes=[pltpu.VMEM((K, J * n), f32)],
        compiler_params=_params())(a, dy)


def mm_fwd(a, w3, name, *, tm=1024, tn=None, bias=None, outs=(f32,), first=None, second=None, rides=()):
    M, K = a.shape
    J, _, n = w3.shape
    tm = min(tm, M)
    tn = tn or n
    nl = n // tn
    nb = 0 if bias is None else 1

    def body(*refs):
        a_ref, w_ref = refs[0], refs[1]
        acc = jnp.dot(_bf(a_ref[...]), w_ref[0], preferred_element_type=f32)
        if bias is not None:
            acc = acc + refs[2][...]
        refs[2 + nb][...] = (acc if first is None else first(acc)).astype(outs[0])
        if len(outs) > 1:
            refs[3 + nb][...] = second(acc).astype(outs[1])

    in_specs = [pl.BlockSpec((tm, K), lambda j, l, i: (i, 0)),
                pl.BlockSpec((1, K, tn), lambda j, l, i: (j, 0, l))]
    args = [a, w3]
    if bias is not None:
        in_specs.append(pl.BlockSpec((1, tn), lambda j, l, i: (0, j * nl + l)))
        args.append(bias)
    res, landed = _pcall(
        body, args, name=name, grid=(J, nl, M // tm), in_specs=in_specs,
        out_specs=[pl.BlockSpec((tm, tn), lambda j, l, i: (i, j * nl + l)) for _ in outs],
        out_shape=[jax.ShapeDtypeStruct((M, J * n), dt) for dt in outs], rides=rides)
    res = res if len(outs) > 1 else res[0]
    return (res, landed) if rides else res


def mm_bwd_x(dy, w3, name, *, tm=TM, epi=None, extras=(), colsum=False, out_dtype=f32, rides=()):
    M = dy.shape[0]
    J, K, n = w3.shape
    tm = min(tm, M)
    nex = len(extras)

    def body(*refs):
        dy_ref, w_hbm = refs[0], refs[1]
        ex = refs[2:2 + nex]
        o_ref = refs[2 + nex]
        w_ref = refs[-1]
        i = pl.program_id(0)

        @pl.when(i == 0)
        def _():
            pltpu.sync_copy(w_hbm, w_ref)

        acc = None
        for j in range(J):
            part = lax.dot_general(_bf(dy_ref[:, j * n:(j + 1) * n]), w_ref[j], NT_DIMS, preferred_element_type=f32)
            acc = part if acc is None else acc + part
        if epi is not None:
            acc = epi(acc, *[e[...] for e in ex])
        o_ref[...] = acc.astype(out_dtype)
        if colsum:
            s_ref = refs[3 + nex]
            cs = jnp.sum(acc, axis=0, keepdims=True)

            @pl.when(i == 0)
            def _():
                s_ref[...] = cs

            @pl.when(i > 0)
            def _():
                s_ref[...] += cs

    in_specs = [pl.BlockSpec((tm, J * n), lambda i: (i, 0)), pl.BlockSpec(memory_space=pl.ANY)]
    in_specs += [pl.BlockSpec((tm, K), lambda i: (i, 0)) for _ in extras]
    out_specs = [pl.BlockSpec((tm, K), lambda i: (i, 0))]
    out_shape = [jax.ShapeDtypeStruct((M, K), out_dtype)]
    if colsum:
        out_specs.append(pl.BlockSpec((1, K), lambda i: (0, 0)))
        out_shape.append(jax.ShapeDtypeStruct((1, K), f32))
    res, landed = _pcall(body, [dy, w3, *extras], name=name, grid=(M // tm,), in_specs=in_specs, out_specs=out_specs,
                         out_shape=out_shape, scratch_shapes=[pltpu.VMEM((J, K, n), bf16)], rides=rides)
    res = res if colsum else res[0]
    return (res, landed) if rides else res


def mm_bwd_w(a, dy, J, name, *, tm=2048, tn=None, tk=None, rides=()):
    M, K = a.shape
    n = dy.shape[1] // J
    tm = min(tm, M)
    tn = tn or n
    tk = tk or K
    nl = n // tn
    nk = K // tk
    ns = M // tm

    def body(a_ref, dy_ref, o_ref, acc_ref):
        s = pl.program_id(3)
        part = lax.dot_general(_bf(a_ref[...]), _bf(dy_ref[...]), TN_DIMS, preferred_element_type=f32)

        @pl.when(s == 0)
        def _():
            acc_ref[...] = part

        @pl.when(s > 0)
        def _():
            acc_ref[...] += part

        @pl.when(s == ns - 1)
        def _():
            o_ref[0] = acc_ref[...].astype(bf16)

    res, landed = _pcall(
        body, [a, dy], name=name, grid=(J, nl, nk, ns),
        in_specs=[pl.BlockSpec((tm, tk), lambda j, l, k, s: (s, k)),
                  pl.BlockSpec((tm, tn), lambda j, l, k, s: (s, j * nl + l))],
        out_specs=[pl.BlockSpec((1, tk, tn), lambda j, l, k, s: (j, k, l))],
        out_shape=[jax.ShapeDtypeStruct((J, K, n), bf16)],
        scratch_shapes=[pltpu.VMEM((tk, tn), f32)], rides=rides)
    return (res[0], landed) if rides else res[0]


def _relu2(v):
    r = jnp.maximum(v, 0.0)
    return r * r


def _shift_down(z, k, halo):
    r = lax.broadcasted_iota(jnp.int32, z.shape, 0)
    y = pltpu.roll(z, k, 0)
    for q in range(k):
        y = jnp.where(r == q, halo[8 - k + q:8 - k + q + 1, :], y)
    return y


def _shift_up(z, k, halo):
    tm = z.shape[0]
    r = lax.broadcasted_iota(jnp.int32, z.shape, 0)
    y = pltpu.roll(z, tm - k, 0)
    for q in range(k):
        y = jnp.where(r == tm - k + q, halo[q:q + 1, :], y)
    return y


def _prev_halo(tm, cb):
    return pl.BlockSpec((8, CONV_WIDTH), lambda i: (jnp.maximum(i * (tm // 8) - 1, 0), cb))


def _next_halo(tm, cb, nrow8):
    return pl.BlockSpec((8, CONV_WIDTH), lambda i: (jnp.minimum((i + 1) * (tm // 8), nrow8 - 1), cb))


def conv_fwd(proj, conv_w, tm=TM):
    S = proj.shape[0]

    def body(cb_ref, cc_ref, ch_ref, cch_ref, chh_ref, w_ref, a_ref):
        i = pl.program_id(0)
        z = cc_ref[...] * ch_ref[...]
        zh = jnp.where(i == 0, 0.0, cch_ref[...] * chh_ref[...])
        w = w_ref[...]
        dwz = w[0:1, :] * _shift_down(z, 2, zh) + w[1:2, :] * _shift_down(z, 1, zh) + w[2:3, :] * z
        a_ref[...] = (cb_ref[...] * dwz).astype(bf16)

    return pl.pallas_call(
        body, name="conv_fwd", grid=(S // tm,),
        in_specs=[_rows(tm, CONV_WIDTH, 6), _rows(tm, CONV_WIDTH, 7), _rows(tm, CONV_WIDTH, 8),
                  _prev_halo(tm, 7), _prev_halo(tm, 8), _full((3, CONV_WIDTH))],
        out_specs=_rows(tm, CONV_WIDTH),
        out_shape=jax.ShapeDtypeStruct((S, CONV_WIDTH), bf16), compiler_params=_params())(
            proj, proj, proj, proj, proj, conv_w)


def conv_bwd(da, proj, conv_w, dproj, tm=TM):
    S = proj.shape[0]
    nt = S // tm

    def body(da_ref, cb_ref, cc_ref, ch_ref, cch_ref, chh_ref, dan_ref, cbn_ref, w_ref, _, o_ref, dw_ref):
        i = pl.program_id(0)
        cc, ch, cb, dav = cc_ref[...], ch_ref[...], cb_ref[...], da_ref[...]
        z = cc * ch
        zh = jnp.where(i == 0, 0.0, cch_ref[...] * chh_ref[...])
        w = w_ref[...]
        z1 = _shift_down(z, 1, zh)
        z2 = _shift_down(z, 2, zh)
        dwz = w[0:1, :] * z2 + w[1:2, :] * z1 + w[2:3, :] * z
        g = dav * cb
        gn = jnp.where(i == nt - 1, 0.0, dan_ref[...] * cbn_ref[...])
        dz = w[2:3, :] * g + w[1:2, :] * _shift_up(g, 1, gn) + w[0:1, :] * _shift_up(g, 2, gn)
        o_ref[:, 0:CONV_WIDTH] = (dav * dwz).astype(bf16)
        o_ref[:, CONV_WIDTH:2 * CONV_WIDTH] = (dz * ch).astype(bf16)
        o_ref[:, 2 * CONV_WIDTH:3 * CONV_WIDTH] = (dz * cc).astype(bf16)

        @pl.when(i == 0)
        def _():
            dw_ref[...] = jnp.zeros_like(dw_ref)

        dw_ref[0:1, :] += jnp.sum(g * z2, axis=0, keepdims=True)
        dw_ref[1:2, :] += jnp.sum(g * z1, axis=0, keepdims=True)
        dw_ref[2:3, :] += jnp.sum(g * z, axis=0, keepdims=True)

    return pl.pallas_call(
        body, name="conv_bwd", grid=(nt,),
        in_specs=[_rows(tm, CONV_WIDTH), _rows(tm, CONV_WIDTH, 6), _rows(tm, CONV_WIDTH, 7), _rows(tm, CONV_WIDTH, 8),
                  _prev_halo(tm, 7), _prev_halo(tm, 8),
                  pl.BlockSpec((8, CONV_WIDTH), lambda i: (jnp.minimum((i + 1) * (tm // 8), S // 8 - 1), 0)),
                  _next_halo(tm, 6, S // 8), _full((3, CONV_WIDTH)), pl.BlockSpec(memory_space=pl.ANY)],
        out_specs=[_rows(tm, 3 * CONV_WIDTH, GATE_COLS // (3 * CONV_WIDTH)), _full((3, CONV_WIDTH))],
        out_shape=[jax.ShapeDtypeStruct(dproj.shape, bf16), jax.ShapeDtypeStruct((3, CONV_WIDTH), f32)],
        input_output_aliases={9: 0},
        compiler_params=_params())(da, proj, proj, proj, proj, proj, da, proj, conv_w, dproj)


def _cmul(ar, ai, br, bi):
    return ar * br - ai * bi, ar * bi + ai * br


def ssm_prep(lr, li, ldt, bt_re, bt_im):
    def body(lr_ref, li_ref, ldt_ref, br_ref, bi_ref, ar_ref, ai_ref, kr_ref, ki_ref, bbr_ref, bbi_ref):
        lrv, liv = lr_ref[...], li_ref[...]
        dt = jnp.exp(ldt_ref[...])
        mag = jnp.exp(lrv * dt)
        ar = mag * jnp.cos(liv * dt)
        ai = mag * jnp.sin(liv * dt)
        den = lrv * lrv + liv * liv
        nr = ar - 1.0
        kr = (nr * lrv + ai * liv) / den
        ki = (ai * lrv - nr * liv) / den
        ar_ref[...], ai_ref[...], kr_ref[...], ki_ref[...] = ar, ai, kr, ki
        bbr_ref[...] = kr * br_ref[...] - ki * bi_ref[...]
        bbi_ref[...] = kr * bi_ref[...] + ki * br_ref[...]

    v = jax.ShapeDtypeStruct((1, N_STATES), f32)
    m = jax.ShapeDtypeStruct((SSM_GROUP, N_STATES), f32)
    return pl.pallas_call(body, name="ssm_prep", out_shape=[v, v, v, v, m, m])(lr, li, ldt, bt_re, bt_im)


def _pow_segment(ar, ai, seg_len):
    pr, pi = ar, ai
    for _ in range(int(math.log2(seg_len))):
        pr, pi = _cmul(pr, pi, pr, pi)
    return pr, pi


GELU_K = math.sqrt(2.0 / math.pi)
GELU_C = 0.044715


def _gelu(v):
    return 0.5 * v * (1.0 + jnp.tanh(GELU_K * (v + GELU_C * v * v * v)))


def _gelu_grad(v):
    t = jnp.tanh(GELU_K * (v + GELU_C * v * v * v))
    return 0.5 * (1.0 + t) + 0.5 * v * (1.0 - t * t) * GELU_K * (1.0 + 3.0 * GELU_C * v * v)


def ssm_scan_fwd(u_perm, bd_r, bd_i, ar, ai, *, final=None, rides=()):
    S = u_perm.shape[0]
    R = SCAN_ROWS
    nblk = S // R
    seg_len = S // N_SEG
    nstrip = N_STATES // LANE_STRIP
    store = final is not None

    def body(*refs):
        if store:
            (u_ref, bdr_ref, bdi_ref, ar_ref, ai_ref, cdr_ref, cdi_ref, d_ref, ir_ref, ii_ref,
             y_ref, ys_ref, sr_ref, si_ref, bur, bui, car, cai) = refs
        else:
            u_ref, bdr_ref, bdi_ref, ar_ref, ai_ref, or_ref, oi_ref, bur, bui, car, cai = refs
        i = pl.program_id(0)
        u = u_ref[...]
        ub = u.astype(bf16)
        for cb in range(4):
            us = ub[:, cb * 128:(cb + 1) * 128]
            bur[:, cb * 512:(cb + 1) * 512] = jnp.dot(us, bdr_ref[cb], preferred_element_type=f32)
            bui[:, cb * 512:(cb + 1) * 512] = jnp.dot(us, bdi_ref[cb], preferred_element_type=f32)

        @pl.when(i == 0)
        def _():
            if store:
                car[...] = ir_ref[...]
                cai[...] = ii_ref[...]
            else:
                car[...] = jnp.zeros_like(car)
                cai[...] = jnp.zeros_like(cai)

        for ls in range(nstrip):
            lanes = pl.ds(ls * LANE_STRIP, LANE_STRIP)
            a_r = jnp.broadcast_to(ar_ref[:, lanes], (N_SEG, LANE_STRIP))
            a_i = jnp.broadcast_to(ai_ref[:, lanes], (N_SEG, LANE_STRIP))

            def step(t, carry, lanes=lanes, a_r=a_r, a_i=a_i):
                s_r, s_i = carry
                row = pl.multiple_of(t * 8, 8)
                n_r = a_r * s_r - a_i * s_i + bur[pl.ds(row, 8), lanes]
                n_i = a_r * s_i + a_i * s_r + bui[pl.ds(row, 8), lanes]
                if store:
                    sr_ref[pl.ds(row, 8), lanes] = n_r
                    si_ref[pl.ds(row, 8), lanes] = n_i
                return n_r, n_i

            e_r, e_i = lax.fori_loop(0, R // 8, step, (car[:, lanes], cai[:, lanes]), unroll=4)
            car[:, lanes] = e_r
            cai[:, lanes] = e_i

        if store:
            for cb in range(4):
                st_r = sr_ref[:, cb * 512:(cb + 1) * 512].astype(bf16)
                st_i = si_ref[:, cb * 512:(cb + 1) * 512].astype(bf16)
                y = (jnp.dot(st_r, cdr_ref[cb], preferred_element_type=f32)
                     - jnp.dot(st_i, cdi_ref[cb], preferred_element_type=f32))
                cols = slice(cb * 128, (cb + 1) * 128)
                y = y + d_ref[:, cols] * u[:, cols]
                y_ref[:, cols] = y
                ys_ref[:, cols] = _gelu(y).astype(bf16)
        else:
            @pl.when(i == nblk - 1)
            def _():
                p_r, p_i = _pow_segment(ar_ref[...], ai_ref[...], seg_len)
                t_r, t_i = car[0:1, :], cai[0:1, :]
                or_ref[0:1, :] = jnp.zeros((1, N_STATES), f32)
                oi_ref[0:1, :] = jnp.zeros((1, N_STATES), f32)
                for j in range(1, N_SEG):
                    or_ref[j:j + 1, :] = t_r
                    oi_ref[j:j + 1, :] = t_i
                    m_r, m_i = _cmul(p_r, p_i, t_r, t_i)
                    t_r, t_i = car[j:j + 1, :] + m_r, cai[j:j + 1, :] + m_i

    blk = lambda w: pl.BlockSpec((R, w), lambda i: (i, 0))
    in_specs = [blk(SSM_WIDTH), _full((4, 128, 512)), _full((4, 128, 512)), _full((1, N_STATES)), _full((1, N_STATES))]
    args = [u_perm, bd_r, bd_i, ar, ai]
    scratch = [pltpu.VMEM((R, N_STATES), f32), pltpu.VMEM((R, N_STATES), f32),
               pltpu.VMEM((N_SEG, N_STATES), f32), pltpu.VMEM((N_SEG, N_STATES), f32)]
    if store:
        in_specs += [_full((4, 512, 128)), _full((4, 512, 128)), _full((1, SSM_WIDTH)),
                     _full((N_SEG, N_STATES)), _full((N_SEG, N_STATES))]
        args += list(final)
        out_specs = [blk(SSM_WIDTH), blk(SSM_WIDTH), blk(N_STATES), blk(N_STATES)]
        out_shape = [jax.ShapeDtypeStruct((S, SSM_WIDTH), f32), jax.ShapeDtypeStruct((S, SSM_WIDTH), bf16),
                     jax.ShapeDtypeStruct((S, N_STATES), f32), jax.ShapeDtypeStruct((S, N_STATES), f32)]
        name = "ssm_scan_fwd"
    else:
        out_specs = [_full((N_SEG, N_STATES)), _full((N_SEG, N_STATES))]
        out_shape = [jax.ShapeDtypeStruct((N_SEG, N_STATES), f32)] * 2
        name = "ssm_scan_fwd_carry"
    res, landed = _pcall(body, args, name=name, grid=(nblk,), in_specs=in_specs, out_specs=out_specs,
                         out_shape=out_shape, scratch_shapes=scratch, rides=rides)
    return (res, landed) if rides else res


def ssm_scan_bwd(dys_perm, y_perm, cdt_r, cdt_i, ar, ai, *, final=None, rides=()):
    S = dys_perm.shape[0]
    R = SCAN_ROWS
    nblk = S // R
    seg_len = S // N_SEG
    nstrip = N_STATES // LANE_STRIP
    store = final is not None

    def body(*refs):
        if store:
            (dys_ref, y_ref, cdr_ref, cdi_ref, ar_ref, ai_ref, u_ref, sr_ref, si_ref, bdr_ref, bdi_ref, d_ref, ir_ref,
             ii_ref, du_ref, dbr_ref, dbi_ref, dcr_ref, dci_ref, dar_ref, dai_ref, dd_ref, dsr, dsi, lmr, lmi, car,
             cai) = refs
        else:
            dys_ref, y_ref, cdr_ref, cdi_ref, ar_ref, ai_ref, or_ref, oi_ref, dsr, dsi, car, cai = refs
        i = pl.program_id(0)
        dy = dys_ref[...] * _gelu_grad(y_ref[...])
        dyb = dy.astype(bf16)
        for cb in range(4):
            ds_ = dyb[:, cb * 128:(cb + 1) * 128]
            dsr[:, cb * 512:(cb + 1) * 512] = jnp.dot(ds_, cdr_ref[cb], preferred_element_type=f32)
            dsi[:, cb * 512:(cb + 1) * 512] = -jnp.dot(ds_, cdi_ref[cb], preferred_element_type=f32)

        @pl.when(i == 0)
        def _():
            if store:
                car[...] = ir_ref[...]
                cai[...] = ii_ref[...]
                dar_ref[...] = jnp.zeros_like(dar_ref)
                dai_ref[...] = jnp.zeros_like(dai_ref)
                dbr_ref[...] = jnp.zeros_like(dbr_ref)
                dbi_ref[...] = jnp.zeros_like(dbi_ref)
                dcr_ref[...] = jnp.zeros_like(dcr_ref)
                dci_ref[...] = jnp.zeros_like(dci_ref)
                dd_ref[...] = jnp.zeros_like(dd_ref)
            else:
                car[...] = jnp.zeros_like(car)
                cai[...] = jnp.zeros_like(cai)

        for ls in range(nstrip):
            lanes = pl.ds(ls * LANE_STRIP, LANE_STRIP)
            a_r = jnp.broadcast_to(ar_ref[:, lanes], (N_SEG, LANE_STRIP))
            a_i = jnp.broadcast_to(ai_ref[:, lanes], (N_SEG, LANE_STRIP))
            zero = jnp.zeros((N_SEG, LANE_STRIP), f32)

            def step(k, carry, lanes=lanes, a_r=a_r, a_i=a_i):
                l_r, l_i, g_r, g_i = carry
                row = pl.multiple_of((R // 8 - 1 - k) * 8, 8)
                if store:
                    s_r = sr_ref[pl.ds(row, 8), lanes]
                    s_i = si_ref[pl.ds(row, 8), lanes]
                    g_r = g_r + l_r * s_r + l_i * s_i
                    g_i = g_i + l_i * s_r - l_r * s_i
                n_r = dsr[pl.ds(row, 8), lanes] + a_r * l_r + a_i * l_i
                n_i = dsi[pl.ds(row, 8), lanes] + a_r * l_i - a_i * l_r
                if store:
                    lmr[pl.ds(row, 8), lanes] = n_r
                    lmi[pl.ds(row, 8), lanes] = n_i
                return n_r, n_i, g_r, g_i

            e_r, e_i, g_r, g_i = lax.fori_loop(0, R // 8, step, (car[:, lanes], cai[:, lanes], zero, zero), unroll=4)
            car[:, lanes] = e_r
            cai[:, lanes] = e_i
            if store:
                dar_ref[:, lanes] += g_r
                dai_ref[:, lanes] += g_i

        if store:
            u = u_ref[...]
            ub = u.astype(bf16)
            for cb in range(4):
                cols = slice(cb * 128, (cb + 1) * 128)
                st = slice(cb * 512, (cb + 1) * 512)
                l_r = lmr[:, st].astype(bf16)
                l_i = lmi[:, st].astype(bf16)
                du = (jnp.dot(l_r, bdr_ref[cb], preferred_element_type=f32)
                      + jnp.dot(l_i, bdi_ref[cb], preferred_element_type=f32))
                du_ref[:, cols] = (du + d_ref[:, cols] * dy[:, cols]).astype(bf16)
                dbr_ref[cb] += lax.dot_general(l_r, ub[:, cols], TN_DIMS, preferred_element_type=f32)
                dbi_ref[cb] += lax.dot_general(l_i, ub[:, cols], TN_DIMS, preferred_element_type=f32)
                dcr_ref[cb] += lax.dot_general(dyb[:, cols], sr_ref[:, st].astype(bf16), TN_DIMS,
                                               preferred_element_type=f32)
                dci_ref[cb] += lax.dot_general(dyb[:, cols], si_ref[:, st].astype(bf16), TN_DIMS,
                                               preferred_element_type=f32)
            dd_ref[...] += jnp.sum(dy * u, axis=0, keepdims=True)
        else:
            @pl.when(i == nblk - 1)
            def _():
                p_r, p_i = _pow_segment(ar_ref[...], ai_ref[...], seg_len)
                p_i = -p_i
                t_r, t_i = car[N_SEG - 1:N_SEG, :], cai[N_SEG - 1:N_SEG, :]
                or_ref[N_SEG - 1:N_SEG, :] = jnp.zeros((1, N_STATES), f32)
                oi_ref[N_SEG - 1:N_SEG, :] = jnp.zeros((1, N_STATES), f32)
                for j in range(N_SEG - 2, -1, -1):
                    or_ref[j:j + 1, :] = t_r
                    oi_ref[j:j + 1, :] = t_i
                    m_r, m_i = _cmul(p_r, p_i, t_r, t_i)
                    t_r, t_i = car[j:j + 1, :] + m_r, cai[j:j + 1, :] + m_i

    blk = lambda w: pl.BlockSpec((R, w), lambda i: (nblk - 1 - i, 0))
    in_specs = [blk(SSM_WIDTH), blk(SSM_WIDTH), _full((4, 128, 512)), _full((4, 128, 512)), _full((1, N_STATES)),
                _full((1, N_STATES))]
    args = [dys_perm, y_perm, cdt_r, cdt_i, ar, ai]
    seg = jax.ShapeDtypeStruct((N_SEG, N_STATES), f32)
    if store:
        in_specs += [blk(SSM_WIDTH), blk(N_STATES), blk(N_STATES), _full((4, 512, 128)), _full((4, 512, 128)),
                     _full((1, SSM_WIDTH)), _full((N_SEG, N_STATES)), _full((N_SEG, N_STATES))]
        args += list(final)
        out_specs = [blk(SSM_WIDTH), _full((4, 512, 128)), _full((4, 512, 128)), _full((4, 128, 512)),
                     _full((4, 128, 512)), _full((N_SEG, N_STATES)), _full((N_SEG, N_STATES)), _full((1, SSM_WIDTH))]
        b_acc = jax.ShapeDtypeStruct((4, 512, 128), f32)
        c_acc = jax.ShapeDtypeStruct((4, 128, 512), f32)
        out_shape = [jax.ShapeDtypeStruct((S, SSM_WIDTH), bf16), b_acc, b_acc, c_acc, c_acc, seg, seg,
                     jax.ShapeDtypeStruct((1, SSM_WIDTH), f32)]
        scratch = [pltpu.VMEM((R, N_STATES), f32)] * 4 + [pltpu.VMEM((N_SEG, N_STATES), f32)] * 2
        name = "ssm_scan_bwd"
    else:
        out_specs = [_full((N_SEG, N_STATES)), _full((N_SEG, N_STATES))]
        out_shape = [seg, seg]
        scratch = [pltpu.VMEM((R, N_STATES), f32)] * 2 + [pltpu.VMEM((N_SEG, N_STATES), f32)] * 2
        name = "ssm_scan_bwd_carry"
    res, landed = _pcall(body, args, name=name, grid=(nblk,), in_specs=in_specs, out_specs=out_specs,
                         out_shape=out_shape, scratch_shapes=scratch, rides=rides)
    return (res, landed) if rides else res


def ssm_param_bwd(dbb_r, dbb_i, bt_re, bt_im, kr, ki, ar, ai, lr, li, ldt, da_r, da_i):
    def body(dbr_ref, dbi_ref, br_ref, bi_ref, kr_ref, ki_ref, ar_ref, ai_ref, lr_ref, li_ref, ldt_ref, dar_ref,
             dai_ref, obr_ref, obi_ref, olr_ref, oli_ref, odt_ref):
        dbr, dbi, b_r, b_i = dbr_ref[...], dbi_ref[...], br_ref[...], bi_ref[...]
        k_r, k_i, a_r, a_i = kr_ref[...], ki_ref[...], ar_ref[...], ai_ref[...]
        l_r, l_i = lr_ref[...], li_ref[...]
        dt = jnp.exp(ldt_ref[...])
        obr_ref[...] = k_r * dbr + k_i * dbi
        obi_ref[...] = k_r * dbi - k_i * dbr
        gk_r = jnp.sum(dbr * b_r + dbi * b_i, axis=0, keepdims=True)
        gk_i = jnp.sum(dbi * b_r - dbr * b_i, axis=0, keepdims=True)
        ga_r = jnp.sum(dar_ref[...], axis=0, keepdims=True)
        ga_i = jnp.sum(dai_ref[...], axis=0, keepdims=True)
        den = l_r * l_r + l_i * l_i
        c_r, c_i = l_r / den, l_i / den
        m_r, m_i = _cmul(c_r, c_i, gk_r, gk_i)
        g_r, g_i = ga_r + m_r, ga_i + m_i
        t1_r, t1_i = _cmul(dt * a_r, -dt * a_i, g_r, g_i)
        q_r, q_i = _cmul(k_r, k_i, c_r, -c_i)
        t2_r, t2_i = _cmul(-q_r, q_i, gk_r, gk_i)
        olr_ref[...] = t1_r + t2_r
        oli_ref[...] = t1_i + t2_i
        w_r, w_i = _cmul(l_r, l_i, a_r, a_i)
        odt_ref[...] = dt * (w_r * g_r + w_i * g_i)

    v = jax.ShapeDtypeStruct((1, N_STATES), f32)
    m = jax.ShapeDtypeStruct((SSM_GROUP, N_STATES), f32)
    return pl.pallas_call(body, name="ssm_param_bwd", out_shape=[m, m, v, v, v])(
        dbb_r, dbb_i, bt_re, bt_im, kr, ki, ar, ai, lr, li, ldt, da_r, da_i)


def group_sum(v):
    def body(v_ref, o_ref):
        o_ref[...] = jnp.sum(v_ref[...], axis=-1, keepdims=True)
    return pl.pallas_call(body, name="group_sum", out_shape=jax.ShapeDtypeStruct((v.shape[0], 1), f32))(v)


ATTN_SCALE = XATTN_HEAD_DIM ** -0.5


def _attn_probs(q_h, k_h):
    s = lax.dot_general(q_h, k_h, NT_DIMS, preferred_element_type=f32) * ATTN_SCALE
    e = jnp.exp(s - jnp.max(s, axis=-1, keepdims=True))
    return e / jnp.sum(e, axis=-1, keepdims=True)


def attn_fwd(proj, kv, tm=TM):
    S = proj.shape[0]
    M = kv.shape[0]

    def body(q_ref, kv_ref, o_ref):
        for h in range(XATTN_HEADS):
            cols = slice(h * XATTN_HEAD_DIM, (h + 1) * XATTN_HEAD_DIM)
            q_h = q_ref[:, cols].astype(bf16)
            k_h = kv_ref[:, cols]
            v_h = kv_ref[:, XATTN_WIDTH + h * XATTN_HEAD_DIM:XATTN_WIDTH + (h + 1) * XATTN_HEAD_DIM]
            p = _attn_probs(q_h, k_h)
            o_ref[:, cols] = jnp.dot(p.astype(bf16), v_h, preferred_element_type=f32).astype(bf16)

    return pl.pallas_call(
        body, name="attn_fwd", grid=(S // tm,), in_specs=[_rows(tm, XATTN_WIDTH, 10), _full((M, 2 * XATTN_WIDTH))],
        out_specs=_rows(tm, XATTN_WIDTH), out_shape=jax.ShapeDtypeStruct((S, XATTN_WIDTH), bf16),
        compiler_params=_params())(proj, kv)


def attn_bwd(do, proj, kv, dproj, tm=TM):
    S = proj.shape[0]
    M = kv.shape[0]

    def body(do_ref, q_ref, kv_ref, _, dq_ref, dkv_ref):
        i = pl.program_id(0)

        @pl.when(i == 0)
        def _():
            dkv_ref[...] = jnp.zeros_like(dkv_ref)

        for h in range(XATTN_HEADS):
            cols = slice(h * XATTN_HEAD_DIM, (h + 1) * XATTN_HEAD_DIM)
            vcols = slice(XATTN_WIDTH + h * XATTN_HEAD_DIM, XATTN_WIDTH + (h + 1) * XATTN_HEAD_DIM)
            q_h = q_ref[:, cols].astype(bf16)
            k_h = kv_ref[:, cols]
            v_h = kv_ref[:, vcols]
            do_h = do_ref[:, cols]
            p = _attn_probs(q_h, k_h)
            dp = lax.dot_general(do_h, v_h, NT_DIMS, preferred_element_type=f32)
            ds = (p * (dp - jnp.sum(dp * p, axis=-1, keepdims=True)) * ATTN_SCALE).astype(bf16)
            dq_ref[:, cols] = jnp.dot(ds, k_h, preferred_element_type=f32).astype(bf16)
            dkv_ref[:, cols] += lax.dot_general(ds, q_h, TN_DIMS, preferred_element_type=f32)
            dkv_ref[:, vcols] += lax.dot_general(p.astype(bf16), do_h, TN_DIMS, preferred_element_type=f32)

    return pl.pallas_call(
        body, name="attn_bwd", grid=(S // tm,),
        in_specs=[_rows(tm, XATTN_WIDTH), _rows(tm, XATTN_WIDTH, 10), _full((M, 2 * XATTN_WIDTH)),
                  pl.BlockSpec(memory_space=pl.ANY)],
        out_specs=[_rows(tm, XATTN_WIDTH, 10), _full((M, 2 * XATTN_WIDTH))],
        out_shape=[jax.ShapeDtypeStruct(dproj.shape, bf16), jax.ShapeDtypeStruct((M, 2 * XATTN_WIDTH), f32)],
        input_output_aliases={3: 0},
        compiler_params=_params())(do, proj, kv, dproj)


def merge_fwd(proj, b_gate, y_a, glu, y_c, tm=TM, rides=()):
    S = proj.shape[0]

    def body(g0_ref, g1_ref, g2_ref, b_ref, ya_ref, ga_ref, gb_ref, yc_ref, o_ref):
        b = b_ref[...]
        g0 = _sig(g0_ref[...] + b[:, 0:D_MODEL])
        g1 = _sig(g1_ref[...] + b[:, D_MODEL:2 * D_MODEL])
        g2 = _sig(g2_ref[...] + b[:, 2 * D_MODEL:3 * D_MODEL])
        y_b = ga_ref[...] * _sig(gb_ref[...])
        o_ref[...] = (g0 * ya_ref[...] + g1 * y_b + g2 * yc_ref[...]).astype(bf16)

    res, landed = _pcall(
        body, [proj, proj, proj, b_gate, y_a, glu, glu, y_c], name="merge_fwd", grid=(S // tm,),
        in_specs=[_rows(tm, D_MODEL, 0), _rows(tm, D_MODEL, 1), _rows(tm, D_MODEL, 2), _full((1, GATE_COLS)),
                  _rows(tm, D_MODEL), _rows(tm, D_MODEL, 0), _rows(tm, D_MODEL, 1), _rows(tm, D_MODEL)],
        out_specs=[_rows(tm, D_MODEL)], out_shape=[jax.ShapeDtypeStruct((S, D_MODEL), bf16)], rides=rides)
    return (res[0], landed) if rides else res[0]


def merge_bwd(dm, proj, b_gate, y_a, glu, y_c, tm=256, rides=()):
    S = proj.shape[0]

    def body(dm_ref, g0_ref, g1_ref, g2_ref, b_ref, ya_ref, ga_ref, gb_ref, yc_ref,
             dg_ref, dya_ref, dgl_ref, dyc_ref, dbg_ref):
        i = pl.program_id(0)
        b = b_ref[...]
        dmv = dm_ref[...]
        ga, sb = ga_ref[...], _sig(gb_ref[...])
        ys = (ya_ref[...], ga * sb, yc_ref[...])
        gs = tuple(_sig(r[...] + b[:, k * D_MODEL:(k + 1) * D_MODEL]) for k, r in enumerate((g0_ref, g1_ref, g2_ref)))

        @pl.when(i == 0)
        def _():
            dbg_ref[...] = jnp.zeros_like(dbg_ref)

        for k in range(3):
            dpre = dmv * ys[k] * gs[k] * (1.0 - gs[k])
            dg_ref[:, k * D_MODEL:(k + 1) * D_MODEL] = dpre.astype(bf16)
            dbg_ref[:, k * D_MODEL:(k + 1) * D_MODEL] += jnp.sum(dpre, axis=0, keepdims=True)
        dya_ref[...] = (dmv * gs[0]).astype(bf16)
        dyc_ref[...] = (dmv * gs[2]).astype(bf16)
        dyb = dmv * gs[1]
        dgl_ref[:, 0:D_MODEL] = (dyb * sb).astype(bf16)
        dgl_ref[:, D_MODEL:2 * D_MODEL] = (dyb * ga * sb * (1.0 - sb)).astype(bf16)

    res, landed = _pcall(
        body, [dm, proj, proj, proj, b_gate, y_a, glu, glu, y_c], name="merge_bwd", grid=(S // tm,),
        in_specs=[_rows(tm, D_MODEL), _rows(tm, D_MODEL, 0), _rows(tm, D_MODEL, 1), _rows(tm, D_MODEL, 2),
                  _full((1, GATE_COLS)), _rows(tm, D_MODEL), _rows(tm, D_MODEL, 0), _rows(tm, D_MODEL, 1),
                  _rows(tm, D_MODEL)],
        out_specs=[_rows(tm, GATE_COLS), _rows(tm, D_MODEL), _rows(tm, 2 * D_MODEL), _rows(tm, D_MODEL),
                   _full((1, GATE_COLS))],
        out_shape=[jax.ShapeDtypeStruct((S, IN_COLS), bf16), jax.ShapeDtypeStruct((S, D_MODEL), bf16),
                   jax.ShapeDtypeStruct((S, 2 * D_MODEL), bf16), jax.ShapeDtypeStruct((S, D_MODEL), bf16),
                   jax.ShapeDtypeStruct((1, GATE_COLS), f32)], rides=rides)
    return (res, landed) if rides else res


def _ln_stats(r):
    mu = jnp.mean(r, axis=-1, keepdims=True)
    xc = r - mu
    var = jnp.mean(xc * xc, axis=-1, keepdims=True)
    rstd = lax.rsqrt(var + LN_EPS)
    return xc * rstd, rstd


def _ln_bwd(dy, xhat, rstd, g):
    dxh = dy * g
    return rstd * (dxh - jnp.mean(dxh, axis=-1, keepdims=True) - xhat * jnp.mean(dxh * xhat, axis=-1, keepdims=True))


def ln1_fwd(x, t1, g, b, tm=TM):
    S = x.shape[0]

    def body(x_ref, t_ref, g_ref, b_ref, o_ref, ob_ref):
        xhat, _ = _ln_stats(ALPHA * x_ref[...] + t_ref[...])
        h = xhat * g_ref[...] + b_ref[...]
        o_ref[...] = h
        ob_ref[...] = h.astype(bf16)

    return pl.pallas_call(
        body, name="ln1_fwd", grid=(S // tm,),
        in_specs=[_rows(tm, D_MODEL), _rows(tm, D_MODEL), _full((1, D_MODEL)), _full((1, D_MODEL))],
        out_specs=[_rows(tm, D_MODEL), _rows(tm, D_MODEL)],
        out_shape=[jax.ShapeDtypeStruct((S, D_MODEL), f32), jax.ShapeDtypeStruct((S, D_MODEL), bf16)],
        compiler_params=_params())(x, t1, g, b)


def ln2_loss_bwd(h1, t2, b_down, g, b, target, tm=TM):
    S = h1.shape[0]

    def body(h_ref, t_ref, bd_ref, g_ref, b_ref, y_ref, dr_ref, drb_ref, l_ref, dg_ref, db_ref, dbd_ref):
        i = pl.program_id(0)
        xhat, rstd = _ln_stats(ALPHA * h_ref[...] + t_ref[...] + bd_ref[...])
        gv = g_ref[...]
        err = xhat * gv + b_ref[...] - y_ref[...]
        dout = err * (1.0 / D_MODEL)
        dr = _ln_bwd(dout, xhat, rstd, gv)
        dr_ref[...] = dr
        drb_ref[...] = dr.astype(bf16)
        part = 0.5 * jnp.sum(jnp.sum(err * err, axis=-1, keepdims=True) * (1.0 / D_MODEL), axis=0, keepdims=True)

        @pl.when(i == 0)
        def _():
            l_ref[...] = jnp.zeros_like(l_ref)
            dg_ref[...] = jnp.zeros_like(dg_ref)
            db_ref[...] = jnp.zeros_like(db_ref)
            dbd_ref[...] = jnp.zeros_like(dbd_ref)

        l_ref[...] += jnp.broadcast_to(part, l_ref.shape)
        dg_ref[...] += jnp.sum(dout * xhat, axis=0, keepdims=True)
        db_ref[...] += jnp.sum(dout, axis=0, keepdims=True)
        dbd_ref[...] += jnp.sum(dr, axis=0, keepdims=True)

    vec = jax.ShapeDtypeStruct((1, D_MODEL), f32)
    return pl.pallas_call(
        body, name="ln2_loss_bwd", grid=(S // tm,),
        in_specs=[_rows(tm, D_MODEL), _rows(tm, D_MODEL), _full((1, D_MODEL)), _full((1, D_MODEL)),
                  _full((1, D_MODEL)), _rows(tm, D_MODEL)],
        out_specs=[_rows(tm, D_MODEL), _rows(tm, D_MODEL), _full((1, 128)), _full((1, D_MODEL)), _full((1, D_MODEL)),
                   _full((1, D_MODEL))],
        out_shape=[jax.ShapeDtypeStruct((S, D_MODEL), f32), jax.ShapeDtypeStruct((S, D_MODEL), bf16),
                   jax.ShapeDtypeStruct((1, 128), f32), vec, vec, vec],
        compiler_params=_params())(h1, t2, b_down, g, b, target)


def ln1_bwd(x, t1, dr2, t3, g, tm=TM):
    S = x.shape[0]

    def body(x_ref, t_ref, dr2_ref, t3_ref, g_ref, dr_ref, drb_ref, dg_ref, db_ref):
        i = pl.program_id(0)
        xhat, rstd = _ln_stats(ALPHA * x_ref[...] + t_ref[...])
        dh = ALPHA * dr2_ref[...] + t3_ref[...]
        dr = _ln_bwd(dh, xhat, rstd, g_ref[...])
        dr_ref[...] = dr
        drb_ref[...] = dr.astype(bf16)

        @pl.when(i == 0)
        def _():
            dg_ref[...] = jnp.zeros_like(dg_ref)
            db_ref[...] = jnp.zeros_like(db_ref)

        dg_ref[...] += jnp.sum(dh * xhat, axis=0, keepdims=True)
        db_ref[...] += jnp.sum(dh, axis=0, keepdims=True)

    vec = jax.ShapeDtypeStruct((1, D_MODEL), f32)
    return pl.pallas_call(
        body, name="ln1_bwd", grid=(S // tm,),
        in_specs=[_rows(tm, D_MODEL)] * 4 + [_full((1, D_MODEL))],
        out_specs=[_rows(tm, D_MODEL), _rows(tm, D_MODEL), _full((1, D_MODEL)), _full((1, D_MODEL))],
        out_shape=[jax.ShapeDtypeStruct((S, D_MODEL), f32), jax.ShapeDtypeStruct((S, D_MODEL), bf16), vec, vec],
        compiler_params=_params())(x, t1, dr2, t3, g)


def exchange(*rides, name):
    return _pcall(lambda: None, [], name=name, grid=(), in_specs=[], out_specs=[], out_shape=[], rides=rides)[1]


def gather_two_level(shard, name):
    K, n = shard.shape
    h = K // 2

    def body(in_ref, out_ref, ici_send, ici_recv, d2d_send, d2d_recv, local_sem):
        x, y, c = lax.axis_index("x"), lax.axis_index("y"), lax.axis_index("c")
        me = 2 * x + y
        mine = pl.ds(pl.multiple_of(c * h, 16), h)
        theirs = pl.ds(pl.multiple_of((1 - c) * h, 16), h)
        peers = _xy_peers(x, y)

        def over_ici(k, dst):
            return pltpu.make_async_remote_copy(src_ref=in_ref.at[mine], dst_ref=dst, send_sem=ici_send.at[k],
                                                recv_sem=ici_recv.at[k], device_id=(*peers[k], c), device_id_type=MESH)

        def over_d2d(k, rows):
            blk = out_ref.at[2 * peers[k][0] + peers[k][1], rows]
            return pltpu.make_async_remote_copy(src_ref=blk, dst_ref=blk, send_sem=d2d_send.at[k],
                                                recv_sem=d2d_recv.at[k], device_id=(x, y, 1 - c), device_id_type=MESH)

        local = pltpu.make_async_copy(in_ref, out_ref.at[me], local_sem)
        local.start()
        started = [over_ici(k, out_ref.at[me, mine]) for k in range(3)]
        for cp in started:
            cp.start()
        for k in range(3):
            over_ici(k, out_ref.at[2 * peers[k][0] + peers[k][1], mine]).wait_recv()
            passed = over_d2d(k, mine)
            passed.start()
            started.append(passed)
        for k in range(3):
            over_d2d(k, theirs).wait_recv()
        for cp in started:
            cp.wait_send()
        local.wait()

    return pl.pallas_call(
        body, name=name, in_specs=[HBM_SPEC], out_specs=HBM_SPEC,
        out_shape=jax.ShapeDtypeStruct((4, K, n), shard.dtype),
        scratch_shapes=[pltpu.SemaphoreType.DMA((3,))] * 4 + [pltpu.SemaphoreType.DMA(())])(shard)


ROW_TILE = 256


def sum_slots(recv, name):
    n, R, C = recv.shape
    tm = min(R, ROW_TILE)

    def body(r_ref, o_ref):
        acc = r_ref[0].astype(f32)
        for k in range(1, n):
            acc = acc + r_ref[k].astype(f32)
        o_ref[...] = acc

    return pl.pallas_call(
        body, name=name, grid=(R // tm,), in_specs=[pl.BlockSpec((n, tm, C), lambda i: (0, i, 0))],
        out_specs=_rows(tm, C), out_shape=jax.ShapeDtypeStruct((R, C), f32), compiler_params=_params())(recv)


def _adamw(w, g, m, v):
    m = ADAM_B1 * m + (1.0 - ADAM_B1) * g
    v = ADAM_B2 * v + (1.0 - ADAM_B2) * (g * g)
    m_hat = m / (1.0 - ADAM_B1 ** ADAM_STEP)
    v_hat = v / (1.0 - ADAM_B2 ** ADAM_STEP)
    delta = -ADAM_LR * (m_hat / (jnp.sqrt(v_hat) + ADAM_EPS) + ADAM_WD * w)
    return delta, m, v


def adam_pair(p, q, w, m, v, name):
    R, C = w.shape
    tm = min(R, ROW_TILE)

    def body(p_ref, q_ref, w_ref, m_ref, v_ref, g_ref, d_ref, nm_ref, nv_ref):
        g = p_ref[...] + q_ref[...]
        g_ref[...] = g
        d_ref[...], nm_ref[...], nv_ref[...] = _adamw(w_ref[...], g, m_ref[...], v_ref[...])

    o = jax.ShapeDtypeStruct((R, C), f32)
    return pl.pallas_call(body, name=name, grid=(R // tm,), in_specs=[_rows(tm, C)] * 5,
                          out_specs=[_rows(tm, C)] * 4, out_shape=[o] * 4, compiler_params=_params())(p, q, w, m, v)


def adam_slots(recv, w, m, v, name):
    n = recv.shape[0]

    def body(r_ref, w_ref, m_ref, v_ref, g_ref, d_ref, nm_ref, nv_ref):
        g = r_ref[0]
        for k in range(1, n):
            g = g + r_ref[k]
        g_ref[...] = g
        d_ref[...], nm_ref[...], nv_ref[...] = _adamw(w_ref[...], g, m_ref[...], v_ref[...])

    o = jax.ShapeDtypeStruct(w.shape, f32)
    return pl.pallas_call(body, name=name, out_shape=[o] * 4, compiler_params=_params())(recv, w, m, v)


BIG = ("w_in", "w_conv_out", "w_glu", "w_xattn_out", "w_kv", "w_out", "w_up", "w_down")
MID = ("w_conv_out", "w_glu", "w_xattn_out", "w_kv", "w_out")
SMALL = ("b_gate", "ssm_lam_re", "ssm_lam_im", "ssm_log_dt", "ssm_b_re", "ssm_b_im", "ssm_c_re", "ssm_c_im", "ssm_d",
         "ln1_g", "ln1_b", "b_up", "b_down", "ln2_g", "ln2_b")
def _pad_flat(a, mult=1024):
    a = a.reshape(-1)
    return jnp.pad(a, (0, (-a.shape[0]) % mult))


def _perm(a):
    S, W = a.shape
    return a.reshape(N_SEG, S // N_SEG, W).transpose(1, 0, 2).reshape(S, W)


def _unperm(a):
    S, W = a.shape
    return a.reshape(S // N_SEG, N_SEG, W).transpose(1, 0, 2).reshape(S, W)


def _state_rows(a):
    return a.transpose(2, 0, 1).reshape(SSM_GROUP, N_STATES)


def _block_diag_b(bt):
    b4 = bt.reshape(SSM_GROUP, 4, 8, SSM_STATE)
    eye = jnp.eye(8, dtype=bt.dtype)
    return jnp.einsum("hcgp,gk->cghkp", b4, eye).reshape(4, 128, 512)


def _block_diag_c(c):
    c4 = c.reshape(4, 8, SSM_GROUP, SSM_STATE)
    eye = jnp.eye(8, dtype=c.dtype)
    return jnp.einsum("cghp,gk->cgpkh", c4, eye).reshape(4, 512, 128)


def _diag_b(acc):
    a = acc.reshape(4, 8, SSM_STATE, 8, SSM_GROUP)
    eye = jnp.eye(8, dtype=acc.dtype)
    return jnp.einsum("cgpkh,gk->hcgp", a, eye).reshape(SSM_GROUP, N_STATES)


def _diag_c(acc):
    a = acc.reshape(4, 8, SSM_GROUP, 8, SSM_STATE)
    eye = jnp.eye(8, dtype=acc.dtype)
    return jnp.einsum("cghkp,gk->cghp", a, eye).reshape(SSM_GROUPS, SSM_GROUP, SSM_STATE)


def kernel(x, mem, w_in, b_gate, conv_w, w_conv_out, ssm_lam_re, ssm_lam_im, ssm_log_dt, ssm_b_re, ssm_b_im, ssm_c_re, ssm_c_im, ssm_d, w_glu, w_kv, w_xattn_out, w_out, ln1_g, ln1_b, w_up, b_up, w_down, b_down, ln2_g, ln2_b, loss_target, m_w_in, m_b_gate, m_conv_w, m_w_conv_out, m_ssm_lam_re, m_ssm_lam_im, m_ssm_log_dt, m_ssm_b_re, m_ssm_b_im, m_ssm_c_re, m_ssm_c_im, m_ssm_d, m_w_glu, m_w_kv, m_w_xattn_out, m_w_out, m_ln1_g, m_ln1_b, m_w_up, m_b_up, m_w_down, m_b_down, m_ln2_g, m_ln2_b, v_w_in, v_b_gate, v_conv_w, v_w_conv_out, v_ssm_lam_re, v_ssm_lam_im, v_ssm_log_dt, v_ssm_b_re, v_ssm_b_im, v_ssm_c_re, v_ssm_c_im, v_ssm_d, v_w_glu, v_w_kv, v_w_xattn_out, v_w_out, v_ln1_g, v_ln1_b, v_w_up, v_b_up, v_w_down, v_b_down, v_ln2_g, v_ln2_b):
    W = dict(w_in=w_in, b_gate=b_gate, conv_w=conv_w, w_conv_out=w_conv_out, ssm_lam_re=ssm_lam_re,
             ssm_lam_im=ssm_lam_im, ssm_log_dt=ssm_log_dt, ssm_b_re=ssm_b_re, ssm_b_im=ssm_b_im, ssm_c_re=ssm_c_re,
             ssm_c_im=ssm_c_im, ssm_d=ssm_d, w_glu=w_glu, w_kv=w_kv, w_xattn_out=w_xattn_out, w_out=w_out,
             ln1_g=ln1_g, ln1_b=ln1_b, w_up=w_up, b_up=b_up, w_down=w_down, b_down=b_down, ln2_g=ln2_g, ln2_b=ln2_b)
    MOM = dict(w_in=m_w_in, b_gate=m_b_gate, conv_w=m_conv_w, w_conv_out=m_w_conv_out, ssm_lam_re=m_ssm_lam_re,
               ssm_lam_im=m_ssm_lam_im, ssm_log_dt=m_ssm_log_dt, ssm_b_re=m_ssm_b_re, ssm_b_im=m_ssm_b_im,
               ssm_c_re=m_ssm_c_re, ssm_c_im=m_ssm_c_im, ssm_d=m_ssm_d, w_glu=m_w_glu, w_kv=m_w_kv,
               w_xattn_out=m_w_xattn_out, w_out=m_w_out, ln1_g=m_ln1_g, ln1_b=m_ln1_b, w_up=m_w_up, b_up=m_b_up,
               w_down=m_w_down, b_down=m_b_down, ln2_g=m_ln2_g, ln2_b=m_ln2_b)
    VEL = dict(w_in=v_w_in, b_gate=v_b_gate, conv_w=v_conv_w, w_conv_out=v_w_conv_out, ssm_lam_re=v_ssm_lam_re,
               ssm_lam_im=v_ssm_lam_im, ssm_log_dt=v_ssm_log_dt, ssm_b_re=v_ssm_b_re, ssm_b_im=v_ssm_b_im,
               ssm_c_re=v_ssm_c_re, ssm_c_im=v_ssm_c_im, ssm_d=v_ssm_d, w_glu=v_w_glu, w_kv=v_w_kv,
               w_xattn_out=v_w_xattn_out, w_out=v_w_out, ln1_g=v_ln1_g, ln1_b=v_ln1_b, w_up=v_w_up, b_up=v_b_up,
               w_down=v_w_down, b_down=v_b_down, ln2_g=v_ln2_g, ln2_b=v_ln2_b)
    names = list(W)
    xy = 2 * lax.axis_index("x") + lax.axis_index("y")

    xs = x[0]
    S = xs.shape[0]
    mems = mem[0]
    tgt = loss_target[0]

    shard_bf = {n: W[n][0].astype(bf16) for n in BIG}
    win3 = gather_two_level(shard_bf["w_in"], "gather_w_in")

    lr = ssm_lam_re.reshape(1, N_STATES)
    li = ssm_lam_im.reshape(1, N_STATES)
    ldt = jnp.repeat(ssm_log_dt.reshape(SSM_GROUPS), SSM_STATE).reshape(1, N_STATES)
    bt_re, bt_im = _state_rows(ssm_b_re[0]), _state_rows(ssm_b_im[0])
    ar, ai, kr, ki, bbt_r, bbt_i = ssm_prep(lr, li, ldt, bt_re, bt_im)
    bd_r, bd_i = _block_diag_b(bbt_r).astype(bf16), _block_diag_b(bbt_i).astype(bf16)
    cd_r, cd_i = _block_diag_c(ssm_c_re[0]).astype(bf16), _block_diag_c(ssm_c_im[0]).astype(bf16)
    bdt_r, bdt_i = bd_r.transpose(0, 2, 1), bd_i.transpose(0, 2, 1)
    cdt_r, cdt_i = cd_r.transpose(0, 2, 1), cd_i.transpose(0, 2, 1)
    d_skip = ssm_d.reshape(1, SSM_WIDTH)

    x_bf = xs.astype(bf16)
    mem_bf = mems.astype(bf16)
    u_col = GATE_COLS + 3 * CONV_WIDTH
    proj, (wco3, wglu3, wxo3, wkv4, wout4, convw4) = mm_fwd(
        x_bf, win3, "proj", rides=tuple(("gather", shard_bf[n]) for n in MID) + (("gather", conv_w[0]),))
    wkv3 = wkv4.reshape(1, D_MODEL, 2 * XATTN_WIDTH)
    wout3 = wout4.reshape(1, D_MODEL, D_MODEL)
    convw_full = convw4.transpose(1, 0, 2).reshape(3, CONV_WIDTH)
    a_conv = conv_fwd(proj, convw_full)
    y_a = mm_fwd_small(a_conv, wco3, "conv_out")
    u_perm = _perm(proj[:, u_col:u_col + SSM_WIDTH])
    half = D_MODEL // 2
    (init_r, init_i), (wup_a,) = ssm_scan_fwd(u_perm, bd_r, bd_i, ar, ai,
                                              rides=(("gather", shard_bf["w_up"][:half]),))
    (ysm_perm, ys_perm, st_r, st_i), (wup_b,) = ssm_scan_fwd(
        u_perm, bd_r, bd_i, ar, ai, final=(cd_r, cd_i, d_skip, init_r, init_i),
        rides=(("gather", shard_bf["w_up"][half:]),))
    wup3 = jnp.concatenate([wup_a, wup_b], axis=1)
    y_s = _unperm(ys_perm)
    glu = mm_fwd_small(y_s, wglu3, "glu")
    kv = mm_fwd_small(mem_bf, wkv3, "kv", out_dtype=bf16)
    o_att = attn_fwd(proj, kv)
    y_c = mm_fwd_small(o_att, wxo3, "xattn_out")
    merged = merge_fwd(proj, b_gate, y_a, glu, y_c)
    t1 = mm_fwd_small(merged, wout3, "w_out")
    h1, h1_bf = ln1_fwd(xs, t1, ln1_g, ln1_b)
    (r_up, hdn), (wdn4,) = mm_fwd(h1_bf, wup3, "w_up", bias=b_up, outs=(bf16, bf16), second=_relu2,
                                  first=lambda v: jnp.maximum(v, 0.0), rides=(("gather", shard_bf["w_down"]),))
    wdn3 = wdn4.reshape(1, D_FF, D_MODEL)
    t2 = mm_fwd(hdn, wdn3, "w_down", tn=512)

    dr2, dr2_bf, loss_part, d_ln2_g, d_ln2_b, d_b_down = ln2_loss_bwd(h1, t2, b_down, ln2_g, ln2_b, tgt)
    part, other = {}, {}
    g_w_down = mm_bwd_w(hdn, dr2_bf, 1, "dw_down", tk=1024).reshape(4, -1, D_MODEL)
    (dup, d_b_up), (recv_dn,) = mm_bwd_x(dr2_bf, wdn3, "dup", tm=512, extras=(r_up,), colsum=True, out_dtype=bf16,
                                         epi=lambda acc, r: acc * (2.0 * r.astype(f32)),
                                         rides=(("scatter", g_w_down),))
    part["w_down"] = sum_slots(recv_dn, "sum_w_down")
    g_w_up = mm_bwd_w(h1_bf, dup, 4, "dw_up")
    t3 = mm_bwd_x(dup, wup3, "dh1")
    dr1, dr1_bf, d_ln1_g, d_ln1_b = ln1_bwd(xs, t1, dr2, t3, ln1_g)
    g_w_out = mm_bwd_w(merged, dr1_bf, 1, "dw_out").reshape(4, -1, D_MODEL)
    dmerged, (recv_out,) = mm_bwd_x(dr1_bf, wout3, "dmerged", tm=1024, rides=(("scatter", g_w_out),))
    part["w_out"] = sum_slots(recv_out, "sum_w_out")
    (dproj, dy_a, dglu, dy_c, d_b_gate), (recv_up,) = merge_bwd(dmerged, proj, b_gate, y_a, glu, y_c,
                                                                rides=(("scatter", g_w_up),))
    part["w_up"] = sum_slots(recv_up, "sum_w_up")

    g_w_co = mm_bwd_w_small(a_conv, dy_a, 4, "dw_conv_out")
    da_conv = mm_bwd_x(dy_a, wco3, "da_conv", tm=1024)
    dproj, d_conv_w = conv_bwd(da_conv, proj, convw_full, dproj)

    g_w_glu = mm_bwd_w_small(y_s, dglu, 4, "dw_glu")
    dys_perm = _perm(mm_bwd_x(dglu, wglu3, "dy_s", tm=1024))
    linit_r, linit_i = ssm_scan_bwd(dys_perm, ysm_perm, cdt_r, cdt_i, ar, ai)
    (du_perm, dbacc_r, dbacc_i, dcacc_r, dcacc_i, da_r, da_i, d_ssm_d), (recv_co, recv_glu) = ssm_scan_bwd(
        dys_perm, ysm_perm, cdt_r, cdt_i, ar, ai,
        final=(u_perm, st_r, st_i, bdt_r, bdt_i, d_skip, linit_r, linit_i),
        rides=(("scatter", g_w_co), ("scatter", g_w_glu)))
    part["w_conv_out"] = sum_slots(recv_co, "sum_w_conv_out")
    part["w_glu"] = sum_slots(recv_glu, "sum_w_glu")
    dbt_re, dbt_im, d_lr, d_li, d_ldt_state = ssm_param_bwd(
        _diag_b(dbacc_r), _diag_b(dbacc_i), bt_re, bt_im, kr, ki, ar, ai, lr, li, ldt, da_r, da_i)
    d_log_dt = group_sum(d_ldt_state.reshape(SSM_GROUPS, SSM_STATE))
    d_b_re = dbt_re.reshape(SSM_GROUP, SSM_GROUPS, SSM_STATE).transpose(1, 2, 0)
    d_b_im = dbt_im.reshape(SSM_GROUP, SSM_GROUPS, SSM_STATE).transpose(1, 2, 0)
    d_c_re = _diag_c(dcacc_r)
    d_c_im = -_diag_c(dcacc_i)

    g_w_xo = mm_bwd_w_small(o_att, dy_c, 4, "dw_xattn_out")
    do_att = mm_bwd_x(dy_c, wxo3, "do_att", tm=1024, out_dtype=bf16)
    dproj, dkv = attn_bwd(do_att, proj, kv, dproj)
    g_w_kv = mm_bwd_w(mem_bf, dkv, 1, "dw_kv").reshape(4, -1, D_MODEL)

    dproj = lax.dynamic_update_slice(dproj, _unperm(du_perm), (0, u_col))
    g_small = {"b_gate": d_b_gate, "ssm_lam_re": d_lr, "ssm_lam_im": d_li, "ssm_log_dt": d_log_dt, "ssm_b_re": d_b_re,
               "ssm_b_im": d_b_im, "ssm_c_re": d_c_re, "ssm_c_im": d_c_im, "ssm_d": d_ssm_d, "ln1_g": d_ln1_g,
               "ln1_b": d_ln1_b, "b_up": d_b_up, "b_down": d_b_down, "ln2_g": d_ln2_g, "ln2_b": d_ln2_b}
    small_names = SMALL + ("conv_w",)
    g_small["conv_w"] = d_conv_w
    sizes = {n: (g_small[n].size + 1023) // 1024 * 1024 for n in small_names}
    pack = lambda d: jnp.concatenate([_pad_flat(d[n]) for n in small_names]).reshape(-1, 128)
    early = ("w_down", "w_up", "w_out", "w_conv_out", "w_glu")
    g_w_in, landed = mm_bwd_w(
        x_bf, dproj, 4, "dw_in", rides=(("scatter", g_w_xo), ("scatter", g_w_kv), ("all", pack(g_small)))
        + tuple(("pair", part[n]) for n in early))
    recv_xo, recv_kv, srecv = landed[:3]
    other.update(zip(early, landed[3:]))
    part["w_xattn_out"] = sum_slots(recv_xo, "sum_w_xattn_out")
    part["w_kv"] = sum_slots(recv_kv, "sum_w_kv")
    dx, (recv_in, other["w_xattn_out"], other["w_kv"]) = mm_bwd_x(
        dproj, win3, "dx", extras=(dr1,), epi=lambda acc, d: acc + ALPHA * d,
        rides=(("scatter", g_w_in), ("pair", part["w_xattn_out"]), ("pair", part["w_kv"])))
    part["w_in"] = sum_slots(recv_in, "sum_w_in")
    other["w_in"], = exchange(("pair", part["w_in"]), name="swap_w_in")
    res = [{}, {}, {}, {}]
    for n in BIG:
        for k, r in enumerate(adam_pair(part[n], other[n], W[n][0], MOM[n][0], VEL[n][0], "adam_" + n)):
            res[k][n] = r[None]
    conv_zero = jnp.zeros((3, CONV_WIDTH), f32)
    gs, ds_, ms, vs = adam_slots(srecv, pack({**{n: W[n] for n in SMALL}, "conv_w": conv_zero}),
                                 pack({**{n: MOM[n] for n in SMALL}, "conv_w": conv_zero}),
                                 pack({**{n: VEL[n] for n in SMALL}, "conv_w": conv_zero}), "adam_small")

    def unpack_small(buf):
        flat = buf.reshape(-1)
        out, r = {}, 0
        for n in small_names:
            ref = g_small[n] if n == "conv_w" else W[n]
            out[n] = flat[r:r + ref.size].reshape(ref.shape)
            r += sizes[n]
        return out

    res_s = [unpack_small(b) for b in (gs, ds_, ms, vs)]
    g_conv = lax.dynamic_slice(res_s[0]["conv_w"], (0, xy * 128), (3, 128))
    conv_slots = g_conv.reshape(1, 3, 128)
    cg, cd, cm, cv = adam_slots(conv_slots, conv_w[0], m_conv_w[0], v_conv_w[0], "adam_conv")
    conv_res = [cg, cd, cm, cv]

    loss = lax.psum(loss_part[0, 0], ("x", "y", "c"))
    outs = [loss, dx.reshape(x.shape)]
    for k in range(4):
        for n in names:
            if n == "conv_w":
                outs.append(conv_res[k].reshape(conv_w.shape))
            elif n in BIG:
                outs.append(res[k][n])
            else:
                outs.append(res_s[k][n])
    return tuple(outs)
```

```python
import functools
import math

import jax
import jax.numpy as jnp
from jax import lax
from jax.experimental import pallas as pl
from jax.experimental.pallas import tpu as pltpu

f32 = jnp.float32
bf16 = jnp.bfloat16

D_MODEL = 1024
CONV_WIDTH = 512
SSM_WIDTH = 512
SSM_GROUP = 16
SSM_GROUPS = 32
SSM_STATE = 64
N_STATES = SSM_GROUPS * SSM_STATE
XATTN_HEADS = 4
XATTN_HEAD_DIM = 128
XATTN_WIDTH = 512
D_FF = 4096
GATE_COLS = 3 * D_MODEL
IN_COLS = GATE_COLS + 3 * CONV_WIDTH + SSM_WIDTH + XATTN_WIDTH
ALPHA = 2.0 ** 0.25
LN_EPS = 1e-5
ADAM_LR = 0.001
ADAM_B1 = 0.9
ADAM_B2 = 0.999
ADAM_EPS = 1e-08
ADAM_WD = 0.01
ADAM_STEP = 10

N_SEG = 8
SCAN_ROWS = 256
LANE_STRIP = 512
TM = 512
VMEM_LIMIT = 48 * 1024 * 1024
MESH = pl.DeviceIdType.MESH

NT_DIMS = (((1,), (1,)), ((), ()))
TN_DIMS = (((0,), (0,)), ((), ()))


def _params():
    return pltpu.CompilerParams(vmem_limit_bytes=VMEM_LIMIT)


def _full(shape):
    n = len(shape)
    return pl.BlockSpec(shape, lambda *_: (0,) * n)


def _rows(tm, w, cb=0):
    return pl.BlockSpec((tm, w), lambda i: (i, cb))


def _sig(x):
    return 1.0 / (1.0 + jnp.exp(-x))


HBM_SPEC = pl.BlockSpec(memory_space=pl.ANY)
RIDE_PEERS = {"gather": 3, "scatter": 3, "all": 7, "pair": 1}


def _xy_peers(x, y):
    return [(1 - x, y), (x, 1 - y), (1 - x, 1 - y)]


def _ride_copies(kind, src, dst, send_sems, recv_sems, local_sem):
    x, y, c = lax.axis_index("x"), lax.axis_index("y"), lax.axis_index("c")
    me = 2 * x + y
    if kind == "all":
        flips = [(fx, fy, fc) for fx in (0, 1) for fy in (0, 1) for fc in (0, 1)][1:]
        peers = [(x ^ fx, y ^ fy, c ^ fc) for fx, fy, fc in flips]
        slot = lambda p: 4 * p[0] + 2 * p[1] + p[2]
        mine = slot((x, y, c))
    else:
        peers = [(px, py, c) for px, py in _xy_peers(x, y)]
        slot = lambda p: 2 * p[0] + p[1]
        mine = me

    def remote(k, s, d):
        return pltpu.make_async_remote_copy(src_ref=s, dst_ref=d, send_sem=send_sems.at[k], recv_sem=recv_sems.at[k],
                                            device_id=peers[k], device_id_type=MESH)

    if kind == "pair":
        peers = [(x, y, 1 - c)]
        return None, [remote(0, src, dst)], [remote(0, src, dst)]
    if kind == "scatter":
        local = pltpu.make_async_copy(src.at[me], dst.at[me], local_sem)
        sends = [remote(k, src.at[slot(p)], dst.at[me]) for k, p in enumerate(peers)]
        lands = [remote(k, src.at[me], dst.at[slot(p)]) for k, p in enumerate(peers)]
    else:
        local = pltpu.make_async_copy(src, dst.at[mine], local_sem)
        sends = [remote(k, src, dst.at[mine]) for k, p in enumerate(peers)]
        lands = [remote(k, src, dst.at[slot(p)]) for k, p in enumerate(peers)]
    return local, sends, lands


def _ride_shape(kind, src):
    lead = {"gather": (4,), "scatter": (), "all": (8,), "pair": ()}[kind]
    return jax.ShapeDtypeStruct(lead + src.shape, src.dtype)


def _pcall(body, args, *, name, grid, in_specs, out_specs, out_shape, scratch_shapes=(), rides=(), aliases=None):
    n_in, n_out, n_scr, nr = len(in_specs), len(out_specs), len(scratch_shapes), len(rides)
    kinds = [k for k, _ in rides]

    def wrapped(*refs):
        ins, rsrc = refs[:n_in], refs[n_in:n_in + nr]
        outs = refs[n_in + nr:n_in + nr + n_out]
        rdst = refs[n_in + nr + n_out:n_in + 2 * nr + n_out]
        scr = refs[n_in + 2 * nr + n_out:n_in + 2 * nr + n_out + n_scr]
        sems = refs[n_in + 2 * nr + n_out + n_scr:]
        def start():
            for r, kind in enumerate(kinds):
                local, sends, _ = _ride_copies(kind, rsrc[r], rdst[r], *sems[3 * r:3 * r + 3])
                if local is not None:
                    local.start()
                for cp in sends:
                    cp.start()

        def finish():
            for r, kind in enumerate(kinds):
                local, sends, lands = _ride_copies(kind, rsrc[r], rdst[r], *sems[3 * r:3 * r + 3])
                for cp in lands:
                    cp.wait_recv()
                for cp in sends:
                    cp.wait_send()
                if local is not None:
                    local.wait()

        if nr and grid:
            ids = [pl.program_id(a) for a in range(len(grid))]
            first = functools.reduce(jnp.logical_and, [i == 0 for i in ids])
            last = functools.reduce(jnp.logical_and, [i == g - 1 for i, g in zip(ids, grid)])
            pl.when(first)(start)
        elif nr:
            start()
        body(*ins, *outs, *scr)
        if nr and grid:
            pl.when(last)(finish)
        elif nr:
            finish()

    sems = []
    for kind in kinds:
        n = RIDE_PEERS[kind]
        sems += [pltpu.SemaphoreType.DMA((n,)), pltpu.SemaphoreType.DMA((n,)), pltpu.SemaphoreType.DMA(())]
    res = pl.pallas_call(
        wrapped, name=name, grid=grid, in_specs=list(in_specs) + [HBM_SPEC] * nr,
        out_specs=list(out_specs) + [HBM_SPEC] * nr,
        out_shape=list(out_shape) + [_ride_shape(k, s) for k, s in rides],
        scratch_shapes=list(scratch_shapes) + sems, input_output_aliases=aliases or {},
        compiler_params=_params())(*args, *[s for _, s in rides])
    return list(res[:n_out]), list(res[n_out:])


def _bf(v):
    return v if v.dtype == bf16 else v.astype(bf16)


def mm_fwd_small(a, w3, name, *, tm=1024, out_dtype=f32):
    M, K = a.shape
    J, _, n = w3.shape
    tm = min(tm, M)

    def body(a_ref, w_ref, o_ref):
        av = _bf(a_ref[...])
        for j in range(J):
            o_ref[:, j * n:(j + 1) * n] = jnp.dot(av, w_ref[j], preferred_element_type=f32).astype(out_dtype)

    return pl.pallas_call(
        body, name=name, grid=(M // tm,), in_specs=[_rows(tm, K), _full((J, K, n))], out_specs=_rows(tm, J * n),
        out_shape=jax.ShapeDtypeStruct((M, J * n), out_dtype), compiler_params=_params())(a, w3)


def mm_bwd_w_small(a, dy, J, name, *, tm=2048):
    M, K = a.shape
    n = dy.shape[1] // J
    tm = min(tm, M)
    ns = M // tm

    def body(a_ref, dy_ref, o_ref, acc_ref):
        s = pl.program_id(0)
        part = lax.dot_general(_bf(a_ref[...]), _bf(dy_ref[...]), TN_DIMS, preferred_element_type=f32)

        @pl.when(s == 0)
        def _():
            acc_ref[...] = part

        @pl.when(s > 0)
        def _():
            acc_ref[...] += part

        @pl.when(s == ns - 1)
        def _():
            for j in range(J):
                o_ref[j] = acc_ref[:, j * n:(j + 1) * n].astype(bf16)

    return pl.pallas_call(
        body, name=name, grid=(ns,), in_specs=[_rows(tm, K), _rows(tm, J * n)], out_specs=_full((J, K, n)),
        out_shape=jax.ShapeDtypeStruct((J, K, n), bf16), scratch_shapes=[pltpu.VMEM((K, J * n), f32)],
        compiler_params=_params())(a, dy)


def mm_fwd(a, w3, name, *, tm=1024, tn=None, bias=None, outs=(f32,), first=None, second=None, rides=()):
    M, K = a.shape
    J, _, n = w3.shape
    tm = min(tm, M)
    tn = tn or n
    nl = n // tn
    nb = 0 if bias is None else 1

    def body(*refs):
        a_ref, w_ref = refs[0], refs[1]
        acc = jnp.dot(_bf(a_ref[...]), w_ref[0], preferred_element_type=f32)
        if bias is not None:
            acc = acc + refs[2][...]
        refs[2 + nb][...] = (acc if first is None else first(acc)).astype(outs[0])
        if len(outs) > 1:
            refs[3 + nb][...] = second(acc).astype(outs[1])

    in_specs = [pl.BlockSpec((tm, K), lambda j, l, i: (i, 0)),
                pl.BlockSpec((1, K, tn), lambda j, l, i: (j, 0, l))]
    args = [a, w3]
    if bias is not None:
        in_specs.append(pl.BlockSpec((1, tn), lambda j, l, i: (0, j * nl + l)))
        args.append(bias)
    res, landed = _pcall(
        body, args, name=name, grid=(J, nl, M // tm), in_specs=in_specs,
        out_specs=[pl.BlockSpec((tm, tn), lambda j, l, i: (i, j * nl + l)) for _ in outs],
        out_shape=[jax.ShapeDtypeStruct((M, J * n), dt) for dt in outs], rides=rides)
    res = res if len(outs) > 1 else res[0]
    return (res, landed) if rides else res


def mm_bwd_x(dy, w3, name, *, tm=TM, epi=None, extras=(), colsum=False, out_dtype=f32, rides=()):
    M = dy.shape[0]
    J, K, n = w3.shape
    tm = min(tm, M)
    nex = len(extras)

    def body(*refs):
        dy_ref, w_hbm = refs[0], refs[1]
        ex = refs[2:2 + nex]
        o_ref = refs[2 + nex]
        w_ref = refs[-1]
        i = pl.program_id(0)

        @pl.when(i == 0)
        def _():
            pltpu.sync_copy(w_hbm, w_ref)

        acc = None
        for j in range(J):
            part = lax.dot_general(_bf(dy_ref[:, j * n:(j + 1) * n]), w_ref[j], NT_DIMS, preferred_element_type=f32)
            acc = part if acc is None else acc + part
        if epi is not None:
            acc = epi(acc, *[e[...] for e in ex])
        o_ref[...] = acc.astype(out_dtype)
        if colsum:
            s_ref = refs[3 + nex]
            cs = jnp.sum(acc, axis=0, keepdims=True)

            @pl.when(i == 0)
            def _():
                s_ref[...] = cs

            @pl.when(i > 0)
            def _():
                s_ref[...] += cs

    in_specs = [pl.BlockSpec((tm, J * n), lambda i: (i, 0)), pl.BlockSpec(memory_space=pl.ANY)]
    in_specs += [pl.BlockSpec((tm, K), lambda i: (i, 0)) for _ in extras]
    out_specs = [pl.BlockSpec((tm, K), lambda i: (i, 0))]
    out_shape = [jax.ShapeDtypeStruct((M, K), out_dtype)]
    if colsum:
        out_specs.append(pl.BlockSpec((1, K), lambda i: (0, 0)))
        out_shape.append(jax.ShapeDtypeStruct((1, K), f32))
    res, landed = _pcall(body, [dy, w3, *extras], name=name, grid=(M // tm,), in_specs=in_specs, out_specs=out_specs,
                         out_shape=out_shape, scratch_shapes=[pltpu.VMEM((J, K, n), bf16)], rides=rides)
    res = res if colsum else res[0]
    return (res, landed) if rides else res


def mm_bwd_w(a, dy, J, name, *, tm=2048, tn=None, tk=None, rides=()):
    M, K = a.shape
    n = dy.shape[1] // J
    tm = min(tm, M)
    tn = tn or n
    tk = tk or K
    nl = n // tn
    nk = K // tk
    ns = M // tm

    def body(a_ref, dy_ref, o_ref, acc_ref):
        s = pl.program_id(3)
        part = lax.dot_general(_bf(a_ref[...]), _bf(dy_ref[...]), TN_DIMS, preferred_element_type=f32)

        @pl.when(s == 0)
        def _():
            acc_ref[...] = part

        @pl.when(s > 0)
        def _():
            acc_ref[...] += part

        @pl.when(s == ns - 1)
        def _():
            o_ref[0] = acc_ref[...].astype(bf16)

    res, landed = _pcall(
        body, [a, dy], name=name, grid=(J, nl, nk, ns),
        in_specs=[pl.BlockSpec((tm, tk), lambda j, l, k, s: (s, k)),
                  pl.BlockSpec((tm, tn), lambda j, l, k, s: (s, j * nl + l))],
        out_specs=[pl.BlockSpec((1, tk, tn), lambda j, l, k, s: (j, k, l))],
        out_shape=[jax.ShapeDtypeStruct((J, K, n), bf16)],
        scratch_shapes=[pltpu.VMEM((tk, tn), f32)], rides=rides)
    return (res[0], landed) if rides else res[0]


def _relu2(v):
    r = jnp.maximum(v, 0.0)
    return r * r


def _shift_down(z, k, halo):
    r = lax.broadcasted_iota(jnp.int32, z.shape, 0)
    y = pltpu.roll(z, k, 0)
    for q in range(k):
        y = jnp.where(r == q, halo[8 - k + q:8 - k + q + 1, :], y)
    return y


def _shift_up(z, k, halo):
    tm = z.shape[0]
    r = lax.broadcasted_iota(jnp.int32, z.shape, 0)
    y = pltpu.roll(z, tm - k, 0)
    for q in range(k):
        y = jnp.where(r == tm - k + q, halo[q:q + 1, :], y)
    return y


def _prev_halo(tm, cb):
    return pl.BlockSpec((8, CONV_WIDTH), lambda i: (jnp.maximum(i * (tm // 8) - 1, 0), cb))


def _next_halo(tm, cb, nrow8):
    return pl.BlockSpec((8, CONV_WIDTH), lambda i: (jnp.minimum((i + 1) * (tm // 8), nrow8 - 1), cb))


def conv_fwd(proj, conv_w, tm=TM):
    S = proj.shape[0]

    def body(cb_ref, cc_ref, ch_ref, cch_ref, chh_ref, w_ref, a_ref):
        i = pl.program_id(0)
        z = cc_ref[...] * ch_ref[...]
        zh = jnp.where(i == 0, 0.0, cch_ref[...] * chh_ref[...])
        w = w_ref[...]
        dwz = w[0:1, :] * _shift_down(z, 2, zh) + w[1:2, :] * _shift_down(z, 1, zh) + w[2:3, :] * z
        a_ref[...] = (cb_ref[...] * dwz).astype(bf16)

    return pl.pallas_call(
        body, name="conv_fwd", grid=(S // tm,),
        in_specs=[_rows(tm, CONV_WIDTH, 6), _rows(tm, CONV_WIDTH, 7), _rows(tm, CONV_WIDTH, 8),
                  _prev_halo(tm, 7), _prev_halo(tm, 8), _full((3, CONV_WIDTH))],
        out_specs=_rows(tm, CONV_WIDTH),
        out_shape=jax.ShapeDtypeStruct((S, CONV_WIDTH), bf16), compiler_params=_params())(
            proj, proj, proj, proj, proj, conv_w)


def conv_bwd(da, proj, conv_w, dproj, tm=TM):
    S = proj.shape[0]
    nt = S // tm

    def body(da_ref, cb_ref, cc_ref, ch_ref, cch_ref, chh_ref, dan_ref, cbn_ref, w_ref, _, o_ref, dw_ref):
        i = pl.program_id(0)
        cc, ch, cb, dav = cc_ref[...], ch_ref[...], cb_ref[...], da_ref[...]
        z = cc * ch
        zh = jnp.where(i == 0, 0.0, cch_ref[...] * chh_ref[...])
        w = w_ref[...]
        z1 = _shift_down(z, 1, zh)
        z2 = _shift_down(z, 2, zh)
        dwz = w[0:1, :] * z2 + w[1:2, :] * z1 + w[2:3, :] * z
        g = dav * cb
        gn = jnp.where(i == nt - 1, 0.0, dan_ref[...] * cbn_ref[...])
        dz = w[2:3, :] * g + w[1:2, :] * _shift_up(g, 1, gn) + w[0:1, :] * _shift_up(g, 2, gn)
        o_ref[:, 0:CONV_WIDTH] = (dav * dwz).astype(bf16)
        o_ref[:, CONV_WIDTH:2 * CONV_WIDTH] = (dz * ch).astype(bf16)
        o_ref[:, 2 * CONV_WIDTH:3 * CONV_WIDTH] = (dz * cc).astype(bf16)

        @pl.when(i == 0)
        def _():
            dw_ref[...] = jnp.zeros_like(dw_ref)

        dw_ref[0:1, :] += jnp.sum(g * z2, axis=0, keepdims=True)
        dw_ref[1:2, :] += jnp.sum(g * z1, axis=0, keepdims=True)
        dw_ref[2:3, :] += jnp.sum(g * z, axis=0, keepdims=True)

    return pl.pallas_call(
        body, name="conv_bwd", grid=(nt,),
        in_specs=[_rows(tm, CONV_WIDTH), _rows(tm, CONV_WIDTH, 6), _rows(tm, CONV_WIDTH, 7), _rows(tm, CONV_WIDTH, 8),
                  _prev_halo(tm, 7), _prev_halo(tm, 8),
                  pl.BlockSpec((8, CONV_WIDTH), lambda i: (jnp.minimum((i + 1) * (tm // 8), S // 8 - 1), 0)),
                  _next_halo(tm, 6, S // 8), _full((3, CONV_WIDTH)), pl.BlockSpec(memory_space=pl.ANY)],
        out_specs=[_rows(tm, 3 * CONV_WIDTH, GATE_COLS // (3 * CONV_WIDTH)), _full((3, CONV_WIDTH))],
        out_shape=[jax.ShapeDtypeStruct(dproj.shape, bf16), jax.ShapeDtypeStruct((3, CONV_WIDTH), f32)],
        input_output_aliases={9: 0},
        compiler_params=_params())(da, proj, proj, proj, proj, proj, da, proj, conv_w, dproj)


def _cmul(ar, ai, br, bi):
    return ar * br - ai * bi, ar * bi + ai * br


def ssm_prep(lr, li, ldt, bt_re, bt_im):
    def body(lr_ref, li_ref, ldt_ref, br_ref, bi_ref, ar_ref, ai_ref, kr_ref, ki_ref, bbr_ref, bbi_ref):
        lrv, liv = lr_ref[...], li_ref[...]
        dt = jnp.exp(ldt_ref[...])
        mag = jnp.exp(lrv * dt)
        ar = mag * jnp.cos(liv * dt)
        ai = mag * jnp.sin(liv * dt)
        den = lrv * lrv + liv * liv
        nr = ar - 1.0
        kr = (nr * lrv + ai * liv) / den
        ki = (ai * lrv - nr * liv) / den
        ar_ref[...], ai_ref[...], kr_ref[...], ki_ref[...] = ar, ai, kr, ki
        bbr_ref[...] = kr * br_ref[...] - ki * bi_ref[...]
        bbi_ref[...] = kr * bi_ref[...] + ki * br_ref[...]

    v = jax.ShapeDtypeStruct((1, N_STATES), f32)
    m = jax.ShapeDtypeStruct((SSM_GROUP, N_STATES), f32)
    return pl.pallas_call(body, name="ssm_prep", out_shape=[v, v, v, v, m, m])(lr, li, ldt, bt_re, bt_im)


def _pow_segment(ar, ai, seg_len):
    pr, pi = ar, ai
    for _ in range(int(math.log2(seg_len))):
        pr, pi = _cmul(pr, pi, pr, pi)
    return pr, pi


GELU_K = math.sqrt(2.0 / math.pi)
GELU_C = 0.044715


def _gelu(v):
    return 0.5 * v * (1.0 + jnp.tanh(GELU_K * (v + GELU_C * v * v * v)))


def _gelu_grad(v):
    t = jnp.tanh(GELU_K * (v + GELU_C * v * v * v))
    return 0.5 * (1.0 + t) + 0.5 * v * (1.0 - t * t) * GELU_K * (1.0 + 3.0 * GELU_C * v * v)


def ssm_scan_fwd(u_perm, bd_r, bd_i, ar, ai, *, final=None, rides=()):
    S = u_perm.shape[0]
    R = SCAN_ROWS
    nblk = S // R
    seg_len = S // N_SEG
    nstrip = N_STATES // LANE_STRIP
    store = final is not None

    def body(*refs):
        if store:
            (u_ref, bdr_ref, bdi_ref, ar_ref, ai_ref, cdr_ref, cdi_ref, d_ref, ir_ref, ii_ref,
             y_ref, ys_ref, sr_ref, si_ref, bur, bui, car, cai) = refs
        else:
            u_ref, bdr_ref, bdi_ref, ar_ref, ai_ref, or_ref, oi_ref, bur, bui, car, cai = refs
        i = pl.program_id(0)
        u = u_ref[...]
        ub = u.astype(bf16)
        for cb in range(4):
            us = ub[:, cb * 128:(cb + 1) * 128]
            bur[:, cb * 512:(cb + 1) * 512] = jnp.dot(us, bdr_ref[cb], preferred_element_type=f32)
            bui[:, cb * 512:(cb + 1) * 512] = jnp.dot(us, bdi_ref[cb], preferred_element_type=f32)

        @pl.when(i == 0)
        def _():
            if store:
                car[...] = ir_ref[...]
                cai[...] = ii_ref[...]
            else:
                car[...] = jnp.zeros_like(car)
                cai[...] = jnp.zeros_like(cai)

        for ls in range(nstrip):
            lanes = pl.ds(ls * LANE_STRIP, LANE_STRIP)
            a_r = jnp.broadcast_to(ar_ref[:, lanes], (N_SEG, LANE_STRIP))
            a_i = jnp.broadcast_to(ai_ref[:, lanes], (N_SEG, LANE_STRIP))

            def step(t, carry, lanes=lanes, a_r=a_r, a_i=a_i):
                s_r, s_i = carry
                row = pl.multiple_of(t * 8, 8)
                n_r = a_r * s_r - a_i * s_i + bur[pl.ds(row, 8), lanes]
                n_i = a_r * s_i + a_i * s_r + bui[pl.ds(row, 8), lanes]
                if store:
                    sr_ref[pl.ds(row, 8), lanes] = n_r
                    si_ref[pl.ds(row, 8), lanes] = n_i
                return n_r, n_i

            e_r, e_i = lax.fori_loop(0, R // 8, step, (car[:, lanes], cai[:, lanes]), unroll=True)
            car[:, lanes] = e_r
            cai[:, lanes] = e_i

        if store:
            for cb in range(4):
                st_r = sr_ref[:, cb * 512:(cb + 1) * 512].astype(bf16)
                st_i = si_ref[:, cb * 512:(cb + 1) * 512].astype(bf16)
                y = (jnp.dot(st_r, cdr_ref[cb], preferred_element_type=f32)
                     - jnp.dot(st_i, cdi_ref[cb], preferred_element_type=f32))
                cols = slice(cb * 128, (cb + 1) * 128)
                y = y + d_ref[:, cols] * u[:, cols]
                y_ref[:, cols] = y
                ys_ref[:, cols] = _gelu(y).astype(bf16)
        else:
            @pl.when(i == nblk - 1)
            def _():
                p_r, p_i = _pow_segment(ar_ref[...], ai_ref[...], seg_len)
                t_r, t_i = car[0:1, :], cai[0:1, :]
                or_ref[0:1, :] = jnp.zeros((1, N_STATES), f32)
                oi_ref[0:1, :] = jnp.zeros((1, N_STATES), f32)
                for j in range(1, N_SEG):
                    or_ref[j:j + 1, :] = t_r
                    oi_ref[j:j + 1, :] = t_i
                    m_r, m_i = _cmul(p_r, p_i, t_r, t_i)
                    t_r, t_i = car[j:j + 1, :] + m_r, cai[j:j + 1, :] + m_i

    blk = lambda w: pl.BlockSpec((R, w), lambda i: (i, 0))
    in_specs = [blk(SSM_WIDTH), _full((4, 128, 512)), _full((4, 128, 512)), _full((1, N_STATES)), _full((1, N_STATES))]
    args = [u_perm, bd_r, bd_i, ar, ai]
    scratch = [pltpu.VMEM((R, N_STATES), f32), pltpu.VMEM((R, N_STATES), f32),
               pltpu.VMEM((N_SEG, N_STATES), f32), pltpu.VMEM((N_SEG, N_STATES), f32)]
    if store:
        in_specs += [_full((4, 512, 128)), _full((4, 512, 128)), _full((1, SSM_WIDTH)),
                     _full((N_SEG, N_STATES)), _full((N_SEG, N_STATES))]
        args += list(final)
        out_specs = [blk(SSM_WIDTH), blk(SSM_WIDTH), blk(N_STATES), blk(N_STATES)]
        out_shape = [jax.ShapeDtypeStruct((S, SSM_WIDTH), f32), jax.ShapeDtypeStruct((S, SSM_WIDTH), bf16),
                     jax.ShapeDtypeStruct((S, N_STATES), f32), jax.ShapeDtypeStruct((S, N_STATES), f32)]
        name = "ssm_scan_fwd"
    else:
        out_specs = [_full((N_SEG, N_STATES)), _full((N_SEG, N_STATES))]
        out_shape = [jax.ShapeDtypeStruct((N_SEG, N_STATES), f32)] * 2
        name = "ssm_scan_fwd_carry"
    res, landed = _pcall(body, args, name=name, grid=(nblk,), in_specs=in_specs, out_specs=out_specs,
                         out_shape=out_shape, scratch_shapes=scratch, rides=rides)
    return (res, landed) if rides else res


def ssm_scan_bwd(dys_perm, y_perm, cdt_r, cdt_i, ar, ai, *, final=None, rides=()):
    S = dys_perm.shape[0]
    R = SCAN_ROWS
    nblk = S // R
    seg_len = S // N_SEG
    nstrip = N_STATES // LANE_STRIP
    store = final is not None

    def body(*refs):
        if store:
            (dys_ref, y_ref, cdr_ref, cdi_ref, ar_ref, ai_ref, u_ref, sr_ref, si_ref, bdr_ref, bdi_ref, d_ref, ir_ref,
             ii_ref, du_ref, dbr_ref, dbi_ref, dcr_ref, dci_ref, dar_ref, dai_ref, dd_ref, dsr, dsi, lmr, lmi, car,
             cai) = refs
        else:
            dys_ref, y_ref, cdr_ref, cdi_ref, ar_ref, ai_ref, or_ref, oi_ref, dsr, dsi, car, cai = refs
        i = pl.program_id(0)
        dy = dys_ref[...] * _gelu_grad(y_ref[...])
        dyb = dy.astype(bf16)
        for cb in range(4):
            ds_ = dyb[:, cb * 128:(cb + 1) * 128]
            dsr[:, cb * 512:(cb + 1) * 512] = jnp.dot(ds_, cdr_ref[cb], preferred_element_type=f32)
            dsi[:, cb * 512:(cb + 1) * 512] = -jnp.dot(ds_, cdi_ref[cb], preferred_element_type=f32)

        @pl.when(i == 0)
        def _():
            if store:
                car[...] = ir_ref[...]
                cai[...] = ii_ref[...]
                dar_ref[...] = jnp.zeros_like(dar_ref)
                dai_ref[...] = jnp.zeros_like(dai_ref)
                dbr_ref[...] = jnp.zeros_like(dbr_ref)
                dbi_ref[...] = jnp.zeros_like(dbi_ref)
                dcr_ref[...] = jnp.zeros_like(dcr_ref)
                dci_ref[...] = jnp.zeros_like(dci_ref)
                dd_ref[...] = jnp.zeros_like(dd_ref)
            else:
                car[...] = jnp.zeros_like(car)
                cai[...] = jnp.zeros_like(cai)

        for ls in range(nstrip):
            lanes = pl.ds(ls * LANE_STRIP, LANE_STRIP)
            a_r = jnp.broadcast_to(ar_ref[:, lanes], (N_SEG, LANE_STRIP))
            a_i = jnp.broadcast_to(ai_ref[:, lanes], (N_SEG, LANE_STRIP))
            zero = jnp.zeros((N_SEG, LANE_STRIP), f32)

            def step(k, carry, lanes=lanes, a_r=a_r, a_i=a_i):
                l_r, l_i, g_r, g_i = carry
                row = pl.multiple_of((R // 8 - 1 - k) * 8, 8)
                if store:
                    s_r = sr_ref[pl.ds(row, 8), lanes]
                    s_i = si_ref[pl.ds(row, 8), lanes]
                    g_r = g_r + l_r * s_r + l_i * s_i
                    g_i = g_i + l_i * s_r - l_r * s_i
                n_r = dsr[pl.ds(row, 8), lanes] + a_r * l_r + a_i * l_i
                n_i = dsi[pl.ds(row, 8), lanes] + a_r * l_i - a_i * l_r
                if store:
                    lmr[pl.ds(row, 8), lanes] = n_r
                    lmi[pl.ds(row, 8), lanes] = n_i
                return n_r, n_i, g_r, g_i

            e_r, e_i, g_r, g_i = lax.fori_loop(0, R // 8, step, (car[:, lanes], cai[:, lanes], zero, zero),
                                               unroll=True)
            car[:, lanes] = e_r
            cai[:, lanes] = e_i
            if store:
                dar_ref[:, lanes] += g_r
                dai_ref[:, lanes] += g_i

        if store:
            u = u_ref[...]
            ub = u.astype(bf16)
            for cb in range(4):
                cols = slice(cb * 128, (cb + 1) * 128)
                st = slice(cb * 512, (cb + 1) * 512)
                l_r = lmr[:, st].astype(bf16)
                l_i = lmi[:, st].astype(bf16)
                du = (jnp.dot(l_r, bdr_ref[cb], preferred_element_type=f32)
                      + jnp.dot(l_i, bdi_ref[cb], preferred_element_type=f32))
                du_ref[:, cols] = (du + d_ref[:, cols] * dy[:, cols]).astype(bf16)
                dbr_ref[cb] += lax.dot_general(l_r, ub[:, cols], TN_DIMS, preferred_element_type=f32)
                dbi_ref[cb] += lax.dot_general(l_i, ub[:, cols], TN_DIMS, preferred_element_type=f32)
                dcr_ref[cb] += lax.dot_general(dyb[:, cols], sr_ref[:, st].astype(bf16), TN_DIMS,
                                               preferred_element_type=f32)
                dci_ref[cb] += lax.dot_general(dyb[:, cols], si_ref[:, st].astype(bf16), TN_DIMS,
                                               preferred_element_type=f32)
            dd_ref[...] += jnp.sum(dy * u, axis=0, keepdims=True)
        else:
            @pl.when(i == nblk - 1)
            def _():
                p_r, p_i = _pow_segment(ar_ref[...], ai_ref[...], seg_len)
                p_i = -p_i
                t_r, t_i = car[N_SEG - 1:N_SEG, :], cai[N_SEG - 1:N_SEG, :]
                or_ref[N_SEG - 1:N_SEG, :] = jnp.zeros((1, N_STATES), f32)
                oi_ref[N_SEG - 1:N_SEG, :] = jnp.zeros((1, N_STATES), f32)
                for j in range(N_SEG - 2, -1, -1):
                    or_ref[j:j + 1, :] = t_r
                    oi_ref[j:j + 1, :] = t_i
                    m_r, m_i = _cmul(p_r, p_i, t_r, t_i)
                    t_r, t_i = car[j:j + 1, :] + m_r, cai[j:j + 1, :] + m_i

    blk = lambda w: pl.BlockSpec((R, w), lambda i: (nblk - 1 - i, 0))
    in_specs = [blk(SSM_WIDTH), blk(SSM_WIDTH), _full((4, 128, 512)), _full((4, 128, 512)), _full((1, N_STATES)),
                _full((1, N_STATES))]
    args = [dys_perm, y_perm, cdt_r, cdt_i, ar, ai]
    seg = jax.ShapeDtypeStruct((N_SEG, N_STATES), f32)
    if store:
        in_specs += [blk(SSM_WIDTH), blk(N_STATES), blk(N_STATES), _full((4, 512, 128)), _full((4, 512, 128)),
                     _full((1, SSM_WIDTH)), _full((N_SEG, N_STATES)), _full((N_SEG, N_STATES))]
        args += list(final)
        out_specs = [blk(SSM_WIDTH), _full((4, 512, 128)), _full((4, 512, 128)), _full((4, 128, 512)),
                     _full((4, 128, 512)), _full((N_SEG, N_STATES)), _full((N_SEG, N_STATES)), _full((1, SSM_WIDTH))]
        b_acc = jax.ShapeDtypeStruct((4, 512, 128), f32)
        c_acc = jax.ShapeDtypeStruct((4, 128, 512), f32)
        out_shape = [jax.ShapeDtypeStruct((S, SSM_WIDTH), bf16), b_acc, b_acc, c_acc, c_acc, seg, seg,
                     jax.ShapeDtypeStruct((1, SSM_WIDTH), f32)]
        scratch = [pltpu.VMEM((R, N_STATES), f32)] * 4 + [pltpu.VMEM((N_SEG, N_STATES), f32)] * 2
        name = "ssm_scan_bwd"
    else:
        out_specs = [_full((N_SEG, N_STATES)), _full((N_SEG, N_STATES))]
        out_shape = [seg, seg]
        scratch = [pltpu.VMEM((R, N_STATES), f32)] * 2 + [pltpu.VMEM((N_SEG, N_STATES), f32)] * 2
        name = "ssm_scan_bwd_carry"
    res, landed = _pcall(body, args, name=name, grid=(nblk,), in_specs=in_specs, out_specs=out_specs,
                         out_shape=out_shape, scratch_shapes=scratch, rides=rides)
    return (res, landed) if rides else res


def ssm_param_bwd(dbb_r, dbb_i, bt_re, bt_im, kr, ki, ar, ai, lr, li, ldt, da_r, da_i):
    def body(dbr_ref, dbi_ref, br_ref, bi_ref, kr_ref, ki_ref, ar_ref, ai_ref, lr_ref, li_ref, ldt_ref, dar_ref,
             dai_ref, obr_ref, obi_ref, olr_ref, oli_ref, odt_ref):
        dbr, dbi, b_r, b_i = dbr_ref[...], dbi_ref[...], br_ref[...], bi_ref[...]
        k_r, k_i, a_r, a_i = kr_ref[...], ki_ref[...], ar_ref[...], ai_ref[...]
        l_r, l_i = lr_ref[...], li_ref[...]
        dt = jnp.exp(ldt_ref[...])
        obr_ref[...] = k_r * dbr + k_i * dbi
        obi_ref[...] = k_r * dbi - k_i * dbr
        gk_r = jnp.sum(dbr * b_r + dbi * b_i, axis=0, keepdims=True)
        gk_i = jnp.sum(dbi * b_r - dbr * b_i, axis=0, keepdims=True)
        ga_r = jnp.sum(dar_ref[...], axis=0, keepdims=True)
        ga_i = jnp.sum(dai_ref[...], axis=0, keepdims=True)
        den = l_r * l_r + l_i * l_i
        c_r, c_i = l_r / den, l_i / den
        m_r, m_i = _cmul(c_r, c_i, gk_r, gk_i)
        g_r, g_i = ga_r + m_r, ga_i + m_i
        t1_r, t1_i = _cmul(dt * a_r, -dt * a_i, g_r, g_i)
        q_r, q_i = _cmul(k_r, k_i, c_r, -c_i)
        t2_r, t2_i = _cmul(-q_r, q_i, gk_r, gk_i)
        olr_ref[...] = t1_r + t2_r
        oli_ref[...] = t1_i + t2_i
        w_r, w_i = _cmul(l_r, l_i, a_r, a_i)
        odt_ref[...] = dt * (w_r * g_r + w_i * g_i)

    v = jax.ShapeDtypeStruct((1, N_STATES), f32)
    m = jax.ShapeDtypeStruct((SSM_GROUP, N_STATES), f32)
    return pl.pallas_call(body, name="ssm_param_bwd", out_shape=[m, m, v, v, v])(
        dbb_r, dbb_i, bt_re, bt_im, kr, ki, ar, ai, lr, li, ldt, da_r, da_i)


def group_sum(v):
    def body(v_ref, o_ref):
        o_ref[...] = jnp.sum(v_ref[...], axis=-1, keepdims=True)
    return pl.pallas_call(body, name="group_sum", out_shape=jax.ShapeDtypeStruct((v.shape[0], 1), f32))(v)


ATTN_SCALE = XATTN_HEAD_DIM ** -0.5


def _attn_probs(q_h, k_h):
    s = lax.dot_general(q_h, k_h, NT_DIMS, preferred_element_type=f32) * ATTN_SCALE
    e = jnp.exp(s - jnp.max(s, axis=-1, keepdims=True))
    return e / jnp.sum(e, axis=-1, keepdims=True)


def attn_fwd(proj, kv, tm=TM):
    S = proj.shape[0]
    M = kv.shape[0]

    def body(q_ref, kv_ref, o_ref):
        for h in range(XATTN_HEADS):
            cols = slice(h * XATTN_HEAD_DIM, (h + 1) * XATTN_HEAD_DIM)
            q_h = q_ref[:, cols].astype(bf16)
            k_h = kv_ref[:, cols]
            v_h = kv_ref[:, XATTN_WIDTH + h * XATTN_HEAD_DIM:XATTN_WIDTH + (h + 1) * XATTN_HEAD_DIM]
            p = _attn_probs(q_h, k_h)
            o_ref[:, cols] = jnp.dot(p.astype(bf16), v_h, preferred_element_type=f32).astype(bf16)

    return pl.pallas_call(
        body, name="attn_fwd", grid=(S // tm,), in_specs=[_rows(tm, XATTN_WIDTH, 10), _full((M, 2 * XATTN_WIDTH))],
        out_specs=_rows(tm, XATTN_WIDTH), out_shape=jax.ShapeDtypeStruct((S, XATTN_WIDTH), bf16),
        compiler_params=_params())(proj, kv)


def attn_bwd(do, proj, kv, dproj, tm=TM):
    S = proj.shape[0]
    M = kv.shape[0]

    def body(do_ref, q_ref, kv_ref, _, dq_ref, dkv_ref):
        i = pl.program_id(0)

        @pl.when(i == 0)
        def _():
            dkv_ref[...] = jnp.zeros_like(dkv_ref)

        for h in range(XATTN_HEADS):
            cols = slice(h * XATTN_HEAD_DIM, (h + 1) * XATTN_HEAD_DIM)
            vcols = slice(XATTN_WIDTH + h * XATTN_HEAD_DIM, XATTN_WIDTH + (h + 1) * XATTN_HEAD_DIM)
            q_h = q_ref[:, cols].astype(bf16)
            k_h = kv_ref[:, cols]
            v_h = kv_ref[:, vcols]
            do_h = do_ref[:, cols]
            p = _attn_probs(q_h, k_h)
            dp = lax.dot_general(do_h, v_h, NT_DIMS, preferred_element_type=f32)
            ds = (p * (dp - jnp.sum(dp * p, axis=-1, keepdims=True)) * ATTN_SCALE).astype(bf16)
            dq_ref[:, cols] = jnp.dot(ds, k_h, preferred_element_type=f32).astype(bf16)
            dkv_ref[:, cols] += lax.dot_general(ds, q_h, TN_DIMS, preferred_element_type=f32)
            dkv_ref[:, vcols] += lax.dot_general(p.astype(bf16), do_h, TN_DIMS, preferred_element_type=f32)

    return pl.pallas_call(
        body, name="attn_bwd", grid=(S // tm,),
        in_specs=[_rows(tm, XATTN_WIDTH), _rows(tm, XATTN_WIDTH, 10), _full((M, 2 * XATTN_WIDTH)),
                  pl.BlockSpec(memory_space=pl.ANY)],
        out_specs=[_rows(tm, XATTN_WIDTH, 10), _full((M, 2 * XATTN_WIDTH))],
        out_shape=[jax.ShapeDtypeStruct(dproj.shape, bf16), jax.ShapeDtypeStruct((M, 2 * XATTN_WIDTH), f32)],
        input_output_aliases={3: 0},
        compiler_params=_params())(do, proj, kv, dproj)


def merge_fwd(proj, b_gate, y_a, glu, y_c, tm=TM, rides=()):
    S = proj.shape[0]

    def body(g0_ref, g1_ref, g2_ref, b_ref, ya_ref, ga_ref, gb_ref, yc_ref, o_ref):
        b = b_ref[...]
        g0 = _sig(g0_ref[...] + b[:, 0:D_MODEL])
        g1 = _sig(g1_ref[...] + b[:, D_MODEL:2 * D_MODEL])
        g2 = _sig(g2_ref[...] + b[:, 2 * D_MODEL:3 * D_MODEL])
        y_b = ga_ref[...].astype(f32) * _sig(gb_ref[...].astype(f32))
        o_ref[...] = (g0 * ya_ref[...].astype(f32) + g1 * y_b + g2 * yc_ref[...].astype(f32)).astype(bf16)

    res, landed = _pcall(
        body, [proj, proj, proj, b_gate, y_a, glu, glu, y_c], name="merge_fwd", grid=(S // tm,),
        in_specs=[_rows(tm, D_MODEL, 0), _rows(tm, D_MODEL, 1), _rows(tm, D_MODEL, 2), _full((1, GATE_COLS)),
                  _rows(tm, D_MODEL), _rows(tm, D_MODEL, 0), _rows(tm, D_MODEL, 1), _rows(tm, D_MODEL)],
        out_specs=[_rows(tm, D_MODEL)], out_shape=[jax.ShapeDtypeStruct((S, D_MODEL), bf16)], rides=rides)
    return (res[0], landed) if rides else res[0]


def merge_bwd(dm, proj, b_gate, y_a, glu, y_c, tm=256, rides=()):
    S = proj.shape[0]

    def body(dm_ref, g0_ref, g1_ref, g2_ref, b_ref, ya_ref, ga_ref, gb_ref, yc_ref,
             dg_ref, dya_ref, dgl_ref, dyc_ref, dbg_ref):
        i = pl.program_id(0)
        b = b_ref[...]
        dmv = dm_ref[...].astype(f32)
        ga, sb = ga_ref[...].astype(f32), _sig(gb_ref[...].astype(f32))
        ys = (ya_ref[...].astype(f32), ga * sb, yc_ref[...].astype(f32))
        gs = tuple(_sig(r[...] + b[:, k * D_MODEL:(k + 1) * D_MODEL]) for k, r in enumerate((g0_ref, g1_ref, g2_ref)))

        @pl.when(i == 0)
        def _():
            dbg_ref[...] = jnp.zeros_like(dbg_ref)

        for k in range(3):
            dpre = dmv * ys[k] * gs[k] * (1.0 - gs[k])
            dg_ref[:, k * D_MODEL:(k + 1) * D_MODEL] = dpre.astype(bf16)
            dbg_ref[:, k * D_MODEL:(k + 1) * D_MODEL] += jnp.sum(dpre, axis=0, keepdims=True)
        dya_ref[...] = (dmv * gs[0]).astype(bf16)
        dyc_ref[...] = (dmv * gs[2]).astype(bf16)
        dyb = dmv * gs[1]
        dgl_ref[:, 0:D_MODEL] = (dyb * sb).astype(bf16)
        dgl_ref[:, D_MODEL:2 * D_MODEL] = (dyb * ga * sb * (1.0 - sb)).astype(bf16)

    res, landed = _pcall(
        body, [dm, proj, proj, proj, b_gate, y_a, glu, glu, y_c], name="merge_bwd", grid=(S // tm,),
        in_specs=[_rows(tm, D_MODEL), _rows(tm, D_MODEL, 0), _rows(tm, D_MODEL, 1), _rows(tm, D_MODEL, 2),
                  _full((1, GATE_COLS)), _rows(tm, D_MODEL), _rows(tm, D_MODEL, 0), _rows(tm, D_MODEL, 1),
                  _rows(tm, D_MODEL)],
        out_specs=[_rows(tm, GATE_COLS), _rows(tm, D_MODEL), _rows(tm, 2 * D_MODEL), _rows(tm, D_MODEL),
                   _full((1, GATE_COLS))],
        out_shape=[jax.ShapeDtypeStruct((S, IN_COLS), bf16), jax.ShapeDtypeStruct((S, D_MODEL), bf16),
                   jax.ShapeDtypeStruct((S, 2 * D_MODEL), bf16), jax.ShapeDtypeStruct((S, D_MODEL), bf16),
                   jax.ShapeDtypeStruct((1, GATE_COLS), f32)], rides=rides)
    return (res, landed) if rides else res


def _ln_stats(r):
    mu = jnp.mean(r, axis=-1, keepdims=True)
    xc = r - mu
    var = jnp.mean(xc * xc, axis=-1, keepdims=True)
    rstd = lax.rsqrt(var + LN_EPS)
    return xc * rstd, rstd


def _ln_bwd(dy, xhat, rstd, g):
    dxh = dy * g
    return rstd * (dxh - jnp.mean(dxh, axis=-1, keepdims=True) - xhat * jnp.mean(dxh * xhat, axis=-1, keepdims=True))


def ln1_fwd(x, t1, g, b, tm=TM):
    S = x.shape[0]

    def body(x_ref, t_ref, g_ref, b_ref, o_ref, ob_ref):
        xhat, _ = _ln_stats(ALPHA * x_ref[...] + t_ref[...])
        h = xhat * g_ref[...] + b_ref[...]
        o_ref[...] = h
        ob_ref[...] = h.astype(bf16)

    return pl.pallas_call(
        body, name="ln1_fwd", grid=(S // tm,),
        in_specs=[_rows(tm, D_MODEL), _rows(tm, D_MODEL), _full((1, D_MODEL)), _full((1, D_MODEL))],
        out_specs=[_rows(tm, D_MODEL), _rows(tm, D_MODEL)],
        out_shape=[jax.ShapeDtypeStruct((S, D_MODEL), f32), jax.ShapeDtypeStruct((S, D_MODEL), bf16)],
        compiler_params=_params())(x, t1, g, b)


def ln2_loss_bwd(h1, t2, b_down, g, b, target, tm=TM):
    S = h1.shape[0]

    def body(h_ref, t_ref, bd_ref, g_ref, b_ref, y_ref, dr_ref, drb_ref, l_ref, dg_ref, db_ref, dbd_ref):
        i = pl.program_id(0)
        xhat, rstd = _ln_stats(ALPHA * h_ref[...] + t_ref[...] + bd_ref[...])
        gv = g_ref[...]
        err = xhat * gv + b_ref[...] - y_ref[...]
        dout = err * (1.0 / D_MODEL)
        dr = _ln_bwd(dout, xhat, rstd, gv)
        dr_ref[...] = dr
        drb_ref[...] = dr.astype(bf16)
        part = 0.5 * jnp.sum(jnp.sum(err * err, axis=-1, keepdims=True) * (1.0 / D_MODEL), axis=0, keepdims=True)

        @pl.when(i == 0)
        def _():
            l_ref[...] = jnp.zeros_like(l_ref)
            dg_ref[...] = jnp.zeros_like(dg_ref)
            db_ref[...] = jnp.zeros_like(db_ref)
            dbd_ref[...] = jnp.zeros_like(dbd_ref)

        l_ref[...] += jnp.broadcast_to(part, l_ref.shape)
        dg_ref[...] += jnp.sum(dout * xhat, axis=0, keepdims=True)
        db_ref[...] += jnp.sum(dout, axis=0, keepdims=True)
        dbd_ref[...] += jnp.sum(dr, axis=0, keepdims=True)

    vec = jax.ShapeDtypeStruct((1, D_MODEL), f32)
    return pl.pallas_call(
        body, name="ln2_loss_bwd", grid=(S // tm,),
        in_specs=[_rows(tm, D_MODEL), _rows(tm, D_MODEL), _full((1, D_MODEL)), _full((1, D_MODEL)),
                  _full((1, D_MODEL)), _rows(tm, D_MODEL)],
        out_specs=[_rows(tm, D_MODEL), _rows(tm, D_MODEL), _full((1, 128)), _full((1, D_MODEL)), _full((1, D_MODEL)),
                   _full((1, D_MODEL))],
        out_shape=[jax.ShapeDtypeStruct((S, D_MODEL), f32), jax.ShapeDtypeStruct((S, D_MODEL), bf16),
                   jax.ShapeDtypeStruct((1, 128), f32), vec, vec, vec],
        compiler_params=_params())(h1, t2, b_down, g, b, target)


def ln1_bwd(x, t1, dr2, t3, g, tm=TM):
    S = x.shape[0]

    def body(x_ref, t_ref, dr2_ref, t3_ref, g_ref, dr_ref, drb_ref, dg_ref, db_ref):
        i = pl.program_id(0)
        xhat, rstd = _ln_stats(ALPHA * x_ref[...] + t_ref[...])
        dh = ALPHA * dr2_ref[...] + t3_ref[...]
        dr = _ln_bwd(dh, xhat, rstd, g_ref[...])
        dr_ref[...] = dr
        drb_ref[...] = dr.astype(bf16)

        @pl.when(i == 0)
        def _():
            dg_ref[...] = jnp.zeros_like(dg_ref)
            db_ref[...] = jnp.zeros_like(db_ref)

        dg_ref[...] += jnp.sum(dh * xhat, axis=0, keepdims=True)
        db_ref[...] += jnp.sum(dh, axis=0, keepdims=True)

    vec = jax.ShapeDtypeStruct((1, D_MODEL), f32)
    return pl.pallas_call(
        body, name="ln1_bwd", grid=(S // tm,),
        in_specs=[_rows(tm, D_MODEL)] * 4 + [_full((1, D_MODEL))],
        out_specs=[_rows(tm, D_MODEL), _rows(tm, D_MODEL), _full((1, D_MODEL)), _full((1, D_MODEL))],
        out_shape=[jax.ShapeDtypeStruct((S, D_MODEL), f32), jax.ShapeDtypeStruct((S, D_MODEL), bf16), vec, vec],
        compiler_params=_params())(x, t1, dr2, t3, g)


def exchange(*rides, name):
    return _pcall(lambda: None, [], name=name, grid=(), in_specs=[], out_specs=[], out_shape=[], rides=rides)[1]


def gather_two_level(shard, name):
    K, n = shard.shape
    h = K // 2

    def body(in_ref, out_ref, ici_send, ici_recv, d2d_send, d2d_recv, local_sem):
        x, y, c = lax.axis_index("x"), lax.axis_index("y"), lax.axis_index("c")
        me = 2 * x + y
        mine = pl.ds(pl.multiple_of(c * h, 16), h)
        theirs = pl.ds(pl.multiple_of((1 - c) * h, 16), h)
        peers = _xy_peers(x, y)

        def over_ici(k, dst):
            return pltpu.make_async_remote_copy(src_ref=in_ref.at[mine], dst_ref=dst, send_sem=ici_send.at[k],
                                                recv_sem=ici_recv.at[k], device_id=(*peers[k], c), device_id_type=MESH)

        def over_d2d(k, rows):
            blk = out_ref.at[2 * peers[k][0] + peers[k][1], rows]
            return pltpu.make_async_remote_copy(src_ref=blk, dst_ref=blk, send_sem=d2d_send.at[k],
                                                recv_sem=d2d_recv.at[k], device_id=(x, y, 1 - c), device_id_type=MESH)

        local = pltpu.make_async_copy(in_ref, out_ref.at[me], local_sem)
        local.start()
        started = [over_ici(k, out_ref.at[me, mine]) for k in range(3)]
        for cp in started:
            cp.start()
        for k in range(3):
            over_ici(k, out_ref.at[2 * peers[k][0] + peers[k][1], mine]).wait_recv()
            passed = over_d2d(k, mine)
            passed.start()
            started.append(passed)
        for k in range(3):
            over_d2d(k, theirs).wait_recv()
        for cp in started:
            cp.wait_send()
        local.wait()

    return pl.pallas_call(
        body, name=name, in_specs=[HBM_SPEC], out_specs=HBM_SPEC,
        out_shape=jax.ShapeDtypeStruct((4, K, n), shard.dtype),
        scratch_shapes=[pltpu.SemaphoreType.DMA((3,))] * 4 + [pltpu.SemaphoreType.DMA(())])(shard)


ROW_TILE = 256


def sum_slots(recv, name):
    n, R, C = recv.shape
    tm = min(R, ROW_TILE)

    def body(r_ref, o_ref):
        acc = r_ref[0].astype(f32)
        for k in range(1, n):
            acc = acc + r_ref[k].astype(f32)
        o_ref[...] = acc

    return pl.pallas_call(
        body, name=name, grid=(R // tm,), in_specs=[pl.BlockSpec((n, tm, C), lambda i: (0, i, 0))],
        out_specs=_rows(tm, C), out_shape=jax.ShapeDtypeStruct((R, C), f32), compiler_params=_params())(recv)


def _adamw(w, g, m, v):
    m = ADAM_B1 * m + (1.0 - ADAM_B1) * g
    v = ADAM_B2 * v + (1.0 - ADAM_B2) * (g * g)
    m_hat = m / (1.0 - ADAM_B1 ** ADAM_STEP)
    v_hat = v / (1.0 - ADAM_B2 ** ADAM_STEP)
    delta = -ADAM_LR * (m_hat / (jnp.sqrt(v_hat) + ADAM_EPS) + ADAM_WD * w)
    return delta, m, v


def adam_pair(p, q, w, m, v, name):
    R, C = w.shape
    tm = min(R, ROW_TILE)

    def body(p_ref, q_ref, w_ref, m_ref, v_ref, g_ref, d_ref, nm_ref, nv_ref):
        g = p_ref[...] + q_ref[...]
        g_ref[...] = g
        d_ref[...], nm_ref[...], nv_ref[...] = _adamw(w_ref[...], g, m_ref[...], v_ref[...])

    o = jax.ShapeDtypeStruct((R, C), f32)
    return pl.pallas_call(body, name=name, grid=(R // tm,), in_specs=[_rows(tm, C)] * 5,
                          out_specs=[_rows(tm, C)] * 4, out_shape=[o] * 4, compiler_params=_params())(p, q, w, m, v)


def adam_slots(recv, w, m, v, name):
    n = recv.shape[0]

    def body(r_ref, w_ref, m_ref, v_ref, g_ref, d_ref, nm_ref, nv_ref):
        g = r_ref[0]
        for k in range(1, n):
            g = g + r_ref[k]
        g_ref[...] = g
        d_ref[...], nm_ref[...], nv_ref[...] = _adamw(w_ref[...], g, m_ref[...], v_ref[...])

    o = jax.ShapeDtypeStruct(w.shape, f32)
    return pl.pallas_call(body, name=name, out_shape=[o] * 4, compiler_params=_params())(recv, w, m, v)


BIG = ("w_in", "w_conv_out", "w_glu", "w_xattn_out", "w_kv", "w_out", "w_up", "w_down")
MID = ("w_conv_out", "w_glu", "w_xattn_out", "w_kv", "w_out")
SMALL = ("b_gate", "ssm_lam_re", "ssm_lam_im", "ssm_log_dt", "ssm_b_re", "ssm_b_im", "ssm_c_re", "ssm_c_im", "ssm_d",
         "ln1_g", "ln1_b", "b_up", "b_down", "ln2_g", "ln2_b")
def _pad_flat(a, mult=1024):
    a = a.reshape(-1)
    return jnp.pad(a, (0, (-a.shape[0]) % mult))


def _perm(a):
    S, W = a.shape
    return a.reshape(N_SEG, S // N_SEG, W).transpose(1, 0, 2).reshape(S, W)


def _unperm(a):
    S, W = a.shape
    return a.reshape(S // N_SEG, N_SEG, W).transpose(1, 0, 2).reshape(S, W)


def _state_rows(a):
    return a.transpose(2, 0, 1).reshape(SSM_GROUP, N_STATES)


def _block_diag_b(bt):
    b4 = bt.reshape(SSM_GROUP, 4, 8, SSM_STATE)
    eye = jnp.eye(8, dtype=bt.dtype)
    return jnp.einsum("hcgp,gk->cghkp", b4, eye).reshape(4, 128, 512)


def _block_diag_c(c):
    c4 = c.reshape(4, 8, SSM_GROUP, SSM_STATE)
    eye = jnp.eye(8, dtype=c.dtype)
    return jnp.einsum("cghp,gk->cgpkh", c4, eye).reshape(4, 512, 128)


def _diag_b(acc):
    a = acc.reshape(4, 8, SSM_STATE, 8, SSM_GROUP)
    eye = jnp.eye(8, dtype=acc.dtype)
    return jnp.einsum("cgpkh,gk->hcgp", a, eye).reshape(SSM_GROUP, N_STATES)


def _diag_c(acc):
    a = acc.reshape(4, 8, SSM_GROUP, 8, SSM_STATE)
    eye = jnp.eye(8, dtype=acc.dtype)
    return jnp.einsum("cghkp,gk->cghp", a, eye).reshape(SSM_GROUPS, SSM_GROUP, SSM_STATE)


def kernel(x, mem, w_in, b_gate, conv_w, w_conv_out, ssm_lam_re, ssm_lam_im, ssm_log_dt, ssm_b_re, ssm_b_im, ssm_c_re, ssm_c_im, ssm_d, w_glu, w_kv, w_xattn_out, w_out, ln1_g, ln1_b, w_up, b_up, w_down, b_down, ln2_g, ln2_b, loss_target, m_w_in, m_b_gate, m_conv_w, m_w_conv_out, m_ssm_lam_re, m_ssm_lam_im, m_ssm_log_dt, m_ssm_b_re, m_ssm_b_im, m_ssm_c_re, m_ssm_c_im, m_ssm_d, m_w_glu, m_w_kv, m_w_xattn_out, m_w_out, m_ln1_g, m_ln1_b, m_w_up, m_b_up, m_w_down, m_b_down, m_ln2_g, m_ln2_b, v_w_in, v_b_gate, v_conv_w, v_w_conv_out, v_ssm_lam_re, v_ssm_lam_im, v_ssm_log_dt, v_ssm_b_re, v_ssm_b_im, v_ssm_c_re, v_ssm_c_im, v_ssm_d, v_w_glu, v_w_kv, v_w_xattn_out, v_w_out, v_ln1_g, v_ln1_b, v_w_up, v_b_up, v_w_down, v_b_down, v_ln2_g, v_ln2_b):
    W = dict(w_in=w_in, b_gate=b_gate, conv_w=conv_w, w_conv_out=w_conv_out, ssm_lam_re=ssm_lam_re,
             ssm_lam_im=ssm_lam_im, ssm_log_dt=ssm_log_dt, ssm_b_re=ssm_b_re, ssm_b_im=ssm_b_im, ssm_c_re=ssm_c_re,
             ssm_c_im=ssm_c_im, ssm_d=ssm_d, w_glu=w_glu, w_kv=w_kv, w_xattn_out=w_xattn_out, w_out=w_out,
             ln1_g=ln1_g, ln1_b=ln1_b, w_up=w_up, b_up=b_up, w_down=w_down, b_down=b_down, ln2_g=ln2_g, ln2_b=ln2_b)
    MOM = dict(w_in=m_w_in, b_gate=m_b_gate, conv_w=m_conv_w, w_conv_out=m_w_conv_out, ssm_lam_re=m_ssm_lam_re,
               ssm_lam_im=m_ssm_lam_im, ssm_log_dt=m_ssm_log_dt, ssm_b_re=m_ssm_b_re, ssm_b_im=m_ssm_b_im,
               ssm_c_re=m_ssm_c_re, ssm_c_im=m_ssm_c_im, ssm_d=m_ssm_d, w_glu=m_w_glu, w_kv=m_w_kv,
               w_xattn_out=m_w_xattn_out, w_out=m_w_out, ln1_g=m_ln1_g, ln1_b=m_ln1_b, w_up=m_w_up, b_up=m_b_up,
               w_down=m_w_down, b_down=m_b_down, ln2_g=m_ln2_g, ln2_b=m_ln2_b)
    VEL = dict(w_in=v_w_in, b_gate=v_b_gate, conv_w=v_conv_w, w_conv_out=v_w_conv_out, ssm_lam_re=v_ssm_lam_re,
               ssm_lam_im=v_ssm_lam_im, ssm_log_dt=v_ssm_log_dt, ssm_b_re=v_ssm_b_re, ssm_b_im=v_ssm_b_im,
               ssm_c_re=v_ssm_c_re, ssm_c_im=v_ssm_c_im, ssm_d=v_ssm_d, w_glu=v_w_glu, w_kv=v_w_kv,
               w_xattn_out=v_w_xattn_out, w_out=v_w_out, ln1_g=v_ln1_g, ln1_b=v_ln1_b, w_up=v_w_up, b_up=v_b_up,
               w_down=v_w_down, b_down=v_b_down, ln2_g=v_ln2_g, ln2_b=v_ln2_b)
    names = list(W)
    xy = 2 * lax.axis_index("x") + lax.axis_index("y")

    xs = x[0]
    S = xs.shape[0]
    mems = mem[0]
    tgt = loss_target[0]

    shard_bf = {n: W[n][0].astype(bf16) for n in BIG}
    win3 = gather_two_level(shard_bf["w_in"], "gather_w_in")

    lr = ssm_lam_re.reshape(1, N_STATES)
    li = ssm_lam_im.reshape(1, N_STATES)
    ldt = jnp.repeat(ssm_log_dt.reshape(SSM_GROUPS), SSM_STATE).reshape(1, N_STATES)
    bt_re, bt_im = _state_rows(ssm_b_re[0]), _state_rows(ssm_b_im[0])
    ar, ai, kr, ki, bbt_r, bbt_i = ssm_prep(lr, li, ldt, bt_re, bt_im)
    bd_r, bd_i = _block_diag_b(bbt_r).astype(bf16), _block_diag_b(bbt_i).astype(bf16)
    cd_r, cd_i = _block_diag_c(ssm_c_re[0]).astype(bf16), _block_diag_c(ssm_c_im[0]).astype(bf16)
    bdt_r, bdt_i = bd_r.transpose(0, 2, 1), bd_i.transpose(0, 2, 1)
    cdt_r, cdt_i = cd_r.transpose(0, 2, 1), cd_i.transpose(0, 2, 1)
    d_skip = ssm_d.reshape(1, SSM_WIDTH)

    x_bf = xs.astype(bf16)
    mem_bf = mems.astype(bf16)
    u_col = GATE_COLS + 3 * CONV_WIDTH
    proj, (wco3, wglu3, wxo3, wkv4, wout4, convw4) = mm_fwd(
        x_bf, win3, "proj", rides=tuple(("gather", shard_bf[n]) for n in MID) + (("gather", conv_w[0]),))
    wkv3 = wkv4.reshape(1, D_MODEL, 2 * XATTN_WIDTH)
    wout3 = wout4.reshape(1, D_MODEL, D_MODEL)
    convw_full = convw4.transpose(1, 0, 2).reshape(3, CONV_WIDTH)
    a_conv = conv_fwd(proj, convw_full)
    y_a = mm_fwd_small(a_conv, wco3, "conv_out", out_dtype=bf16)
    u_perm = _perm(proj[:, u_col:u_col + SSM_WIDTH])
    half = D_MODEL // 2
    (init_r, init_i), (wup_a,) = ssm_scan_fwd(u_perm, bd_r, bd_i, ar, ai,
                                              rides=(("gather", shard_bf["w_up"][:half]),))
    (ysm_perm, ys_perm, st_r, st_i), (wup_b,) = ssm_scan_fwd(
        u_perm, bd_r, bd_i, ar, ai, final=(cd_r, cd_i, d_skip, init_r, init_i),
        rides=(("gather", shard_bf["w_up"][half:]),))
    wup3 = jnp.concatenate([wup_a, wup_b], axis=1)
    y_s = _unperm(ys_perm)
    glu = mm_fwd_small(y_s, wglu3, "glu", out_dtype=bf16)
    kv = mm_fwd_small(mem_bf, wkv3, "kv", out_dtype=bf16)
    o_att = attn_fwd(proj, kv)
    y_c = mm_fwd_small(o_att, wxo3, "xattn_out", out_dtype=bf16)
    merged = merge_fwd(proj, b_gate, y_a, glu, y_c)
    t1 = mm_fwd_small(merged, wout3, "w_out")
    h1, h1_bf = ln1_fwd(xs, t1, ln1_g, ln1_b)
    (r_up, hdn), (wdn4,) = mm_fwd(h1_bf, wup3, "w_up", bias=b_up, outs=(bf16, bf16), second=_relu2,
                                  first=lambda v: jnp.maximum(v, 0.0), rides=(("gather", shard_bf["w_down"]),))
    wdn3 = wdn4.reshape(1, D_FF, D_MODEL)
    t2 = mm_fwd(hdn, wdn3, "w_down", tn=512)

    dr2, dr2_bf, loss_part, d_ln2_g, d_ln2_b, d_b_down = ln2_loss_bwd(h1, t2, b_down, ln2_g, ln2_b, tgt)
    part, other = {}, {}
    g_w_down = mm_bwd_w(hdn, dr2_bf, 1, "dw_down", tk=1024).reshape(4, -1, D_MODEL)
    (dup, d_b_up), (recv_dn,) = mm_bwd_x(dr2_bf, wdn3, "dup", tm=512, extras=(r_up,), colsum=True, out_dtype=bf16,
                                         epi=lambda acc, r: acc * (2.0 * r.astype(f32)),
                                         rides=(("scatter", g_w_down),))
    part["w_down"] = sum_slots(recv_dn, "sum_w_down")
    g_w_up = mm_bwd_w(h1_bf, dup, 4, "dw_up")
    t3 = mm_bwd_x(dup, wup3, "dh1")
    dr1, dr1_bf, d_ln1_g, d_ln1_b = ln1_bwd(xs, t1, dr2, t3, ln1_g)
    g_w_out = mm_bwd_w(merged, dr1_bf, 1, "dw_out").reshape(4, -1, D_MODEL)
    dmerged, (recv_out,) = mm_bwd_x(dr1_bf, wout3, "dmerged", tm=1024, out_dtype=bf16,
                                    rides=(("scatter", g_w_out),))
    part["w_out"] = sum_slots(recv_out, "sum_w_out")
    (dproj, dy_a, dglu, dy_c, d_b_gate), (recv_up,) = merge_bwd(dmerged, proj, b_gate, y_a, glu, y_c,
                                                                rides=(("scatter", g_w_up),))
    part["w_up"] = sum_slots(recv_up, "sum_w_up")

    g_w_co = mm_bwd_w_small(a_conv, dy_a, 4, "dw_conv_out")
    da_conv = mm_bwd_x(dy_a, wco3, "da_conv", tm=1024)
    dproj, d_conv_w = conv_bwd(da_conv, proj, convw_full, dproj)

    g_w_glu = mm_bwd_w_small(y_s, dglu, 4, "dw_glu")
    dys_perm = _perm(mm_bwd_x(dglu, wglu3, "dy_s", tm=1024))
    linit_r, linit_i = ssm_scan_bwd(dys_perm, ysm_perm, cdt_r, cdt_i, ar, ai)
    (du_perm, dbacc_r, dbacc_i, dcacc_r, dcacc_i, da_r, da_i, d_ssm_d), (recv_co, recv_glu) = ssm_scan_bwd(
        dys_perm, ysm_perm, cdt_r, cdt_i, ar, ai,
        final=(u_perm, st_r, st_i, bdt_r, bdt_i, d_skip, linit_r, linit_i),
        rides=(("scatter", g_w_co), ("scatter", g_w_glu)))
    part["w_conv_out"] = sum_slots(recv_co, "sum_w_conv_out")
    part["w_glu"] = sum_slots(recv_glu, "sum_w_glu")
    dbt_re, dbt_im, d_lr, d_li, d_ldt_state = ssm_param_bwd(
        _diag_b(dbacc_r), _diag_b(dbacc_i), bt_re, bt_im, kr, ki, ar, ai, lr, li, ldt, da_r, da_i)
    d_log_dt = group_sum(d_ldt_state.reshape(SSM_GROUPS, SSM_STATE))
    d_b_re = dbt_re.reshape(SSM_GROUP, SSM_GROUPS, SSM_STATE).transpose(1, 2, 0)
    d_b_im = dbt_im.reshape(SSM_GROUP, SSM_GROUPS, SSM_STATE).transpose(1, 2, 0)
    d_c_re = _diag_c(dcacc_r)
    d_c_im = -_diag_c(dcacc_i)

    g_w_xo = mm_bwd_w_small(o_att, dy_c, 4, "dw_xattn_out")
    do_att = mm_bwd_x(dy_c, wxo3, "do_att", tm=1024, out_dtype=bf16)
    dproj, dkv = attn_bwd(do_att, proj, kv, dproj)
    g_w_kv = mm_bwd_w(mem_bf, dkv, 1, "dw_kv").reshape(4, -1, D_MODEL)

    dproj = lax.dynamic_update_slice(dproj, _unperm(du_perm), (0, u_col))
    g_small = {"b_gate": d_b_gate, "ssm_lam_re": d_lr, "ssm_lam_im": d_li, "ssm_log_dt": d_log_dt, "ssm_b_re": d_b_re,
               "ssm_b_im": d_b_im, "ssm_c_re": d_c_re, "ssm_c_im": d_c_im, "ssm_d": d_ssm_d, "ln1_g": d_ln1_g,
               "ln1_b": d_ln1_b, "b_up": d_b_up, "b_down": d_b_down, "ln2_g": d_ln2_g, "ln2_b": d_ln2_b}
    small_names = SMALL + ("conv_w",)
    g_small["conv_w"] = d_conv_w
    sizes = {n: (g_small[n].size + 1023) // 1024 * 1024 for n in small_names}
    pack = lambda d: jnp.concatenate([_pad_flat(d[n]) for n in small_names]).reshape(-1, 128)
    early = ("w_down", "w_up", "w_out", "w_conv_out", "w_glu")
    g_w_in, landed = mm_bwd_w(
        x_bf, dproj, 4, "dw_in", rides=(("scatter", g_w_xo), ("scatter", g_w_kv), ("all", pack(g_small)))
        + tuple(("pair", part[n]) for n in early))
    recv_xo, recv_kv, srecv = landed[:3]
    other.update(zip(early, landed[3:]))
    part["w_xattn_out"] = sum_slots(recv_xo, "sum_w_xattn_out")
    part["w_kv"] = sum_slots(recv_kv, "sum_w_kv")
    dx, (recv_in, other["w_xattn_out"], other["w_kv"]) = mm_bwd_x(
        dproj, win3, "dx", extras=(dr1,), epi=lambda acc, d: acc + ALPHA * d,
        rides=(("scatter", g_w_in), ("pair", part["w_xattn_out"]), ("pair", part["w_kv"])))
    part["w_in"] = sum_slots(recv_in, "sum_w_in")
    other["w_in"], = exchange(("pair", part["w_in"]), name="swap_w_in")
    res = [{}, {}, {}, {}]
    for n in BIG:
        for k, r in enumerate(adam_pair(part[n], other[n], W[n][0], MOM[n][0], VEL[n][0], "adam_" + n)):
            res[k][n] = r[None]
    conv_zero = jnp.zeros((3, CONV_WIDTH), f32)
    gs, ds_, ms, vs = adam_slots(srecv, pack({**{n: W[n] for n in SMALL}, "conv_w": conv_zero}),
                                 pack({**{n: MOM[n] for n in SMALL}, "conv_w": conv_zero}),
                                 pack({**{n: VEL[n] for n in SMALL}, "conv_w": conv_zero}), "adam_small")

    def unpack_small(buf):
        flat = buf.reshape(-1)
        out, r = {}, 0
        for n in small_names:
            ref = g_small[n] if n == "conv_w" else W[n]
            out[n] = flat[r:r + ref.size].reshape(ref.shape)
            r += sizes[n]
        return out

    res_s = [unpack_small(b) for b in (gs, ds_, ms, vs)]
    g_conv = lax.dynamic_slice(res_s[0]["conv_w"], (0, xy * 128), (3, 128))
    conv_slots = g_conv.reshape(1, 3, 128)
    cg, cd, cm, cv = adam_slots(conv_slots, conv_w[0], m_conv_w[0], v_conv_w[0], "adam_conv")
    conv_res = [cg, cd, cm, cv]

    loss = lax.psum(loss_part[0, 0], ("x", "y", "c"))
    outs = [loss, dx.reshape(x.shape)]
    for k in range(4):
        for n in names:
            if n == "conv_w":
                outs.append(conv_res[k].reshape(conv_w.shape))
            elif n in BIG:
                outs.append(res[k][n])
            else:
                outs.append(res_s[k][n])
    return tuple(outs)
```

```python
import functools
import math

import jax
import jax.numpy as jnp
from jax import lax
from jax.experimental import pallas as pl
from jax.experimental.pallas import tpu as pltpu

f32 = jnp.float32
bf16 = jnp.bfloat16

D_MODEL = 1024
CONV_WIDTH = 512
SSM_WIDTH = 512
SSM_GROUP = 16
SSM_GROUPS = 32
SSM_STATE = 64
N_STATES = SSM_GROUPS * SSM_STATE
XATTN_HEADS = 4
XATTN_HEAD_DIM = 128
XATTN_WIDTH = 512
D_FF = 4096
GATE_COLS = 3 * D_MODEL
IN_COLS = GATE_COLS + 3 * CONV_WIDTH + SSM_WIDTH + XATTN_WIDTH
ALPHA = 2.0 ** 0.25
LN_EPS = 1e-5
ADAM_LR = 0.001
ADAM_B1 = 0.9
ADAM_B2 = 0.999
ADAM_EPS = 1e-08
ADAM_WD = 0.01
ADAM_STEP = 10

N_SEG = 8
SCAN_ROWS = 256
LANE_STRIP = 512
TM = 512
VMEM_LIMIT = 48 * 1024 * 1024
MESH = pl.DeviceIdType.MESH

NT_DIMS = (((1,), (1,)), ((), ()))
TN_DIMS = (((0,), (0,)), ((), ()))


def _params():
    return pltpu.CompilerParams(vmem_limit_bytes=VMEM_LIMIT)


def _full(shape):
    n = len(shape)
    return pl.BlockSpec(shape, lambda *_: (0,) * n)


def _rows(tm, w, cb=0):
    return pl.BlockSpec((tm, w), lambda i: (i, cb))


def _sig(x):
    return 1.0 / (1.0 + jnp.exp(-x))


HBM_SPEC = pl.BlockSpec(memory_space=pl.ANY)
RIDE_PEERS = {"gather": 3, "scatter": 3, "all": 7, "pair": 1}


def _xy_peers(x, y):
    return [(1 - x, y), (x, 1 - y), (1 - x, 1 - y)]


def _ride_copies(kind, src, dst, send_sems, recv_sems, local_sem):
    x, y, c = lax.axis_index("x"), lax.axis_index("y"), lax.axis_index("c")
    me = 2 * x + y
    if kind == "all":
        flips = [(fx, fy, fc) for fx in (0, 1) for fy in (0, 1) for fc in (0, 1)][1:]
        peers = [(x ^ fx, y ^ fy, c ^ fc) for fx, fy, fc in flips]
        slot = lambda p: 4 * p[0] + 2 * p[1] + p[2]
        mine = slot((x, y, c))
    else:
        peers = [(px, py, c) for px, py in _xy_peers(x, y)]
        slot = lambda p: 2 * p[0] + p[1]
        mine = me

    def remote(k, s, d):
        return pltpu.make_async_remote_copy(src_ref=s, dst_ref=d, send_sem=send_sems.at[k], recv_sem=recv_sems.at[k],
                                            device_id=peers[k], device_id_type=MESH)

    if kind == "pair":
        peers = [(x, y, 1 - c)]
        return None, [remote(0, src, dst)], [remote(0, src, dst)]
    if kind == "scatter":
        local = pltpu.make_async_copy(src.at[me], dst.at[me], local_sem)
        sends = [remote(k, src.at[slot(p)], dst.at[me]) for k, p in enumerate(peers)]
        lands = [remote(k, src.at[me], dst.at[slot(p)]) for k, p in enumerate(peers)]
    else:
        local = pltpu.make_async_copy(src, dst.at[mine], local_sem)
        sends = [remote(k, src, dst.at[mine]) for k, p in enumerate(peers)]
        lands = [remote(k, src, dst.at[slot(p)]) for k, p in enumerate(peers)]
    return local, sends, lands


def _ride_shape(kind, src):
    lead = {"gather": (4,), "scatter": (), "all": (8,), "pair": ()}[kind]
    return jax.ShapeDtypeStruct(lead + src.shape, src.dtype)


def _pcall(body, args, *, name, grid, in_specs, out_specs, out_shape, scratch_shapes=(), rides=(), aliases=None):
    n_in, n_out, n_scr, nr = len(in_specs), len(out_specs), len(scratch_shapes), len(rides)
    kinds = [k for k, _ in rides]

    def wrapped(*refs):
        ins, rsrc = refs[:n_in], refs[n_in:n_in + nr]
        outs = refs[n_in + nr:n_in + nr + n_out]
        rdst = refs[n_in + nr + n_out:n_in + 2 * nr + n_out]
        scr = refs[n_in + 2 * nr + n_out:n_in + 2 * nr + n_out + n_scr]
        sems = refs[n_in + 2 * nr + n_out + n_scr:]
        def start():
            for r, kind in enumerate(kinds):
                local, sends, _ = _ride_copies(kind, rsrc[r], rdst[r], *sems[3 * r:3 * r + 3])
                if local is not None:
                    local.start()
                for cp in sends:
                    cp.start()

        def finish():
            for r, kind in enumerate(kinds):
                local, sends, lands = _ride_copies(kind, rsrc[r], rdst[r], *sems[3 * r:3 * r + 3])
                for cp in lands:
                    cp.wait_recv()
                for cp in sends:
                    cp.wait_send()
                if local is not None:
                    local.wait()

        if nr and grid:
            ids = [pl.program_id(a) for a in range(len(grid))]
            first = functools.reduce(jnp.logical_and, [i == 0 for i in ids])
            last = functools.reduce(jnp.logical_and, [i == g - 1 for i, g in zip(ids, grid)])
            pl.when(first)(start)
        elif nr:
            start()
        body(*ins, *outs, *scr)
        if nr and grid:
            pl.when(last)(finish)
        elif nr:
            finish()

    sems = []
    for kind in kinds:
        n = RIDE_PEERS[kind]
        sems += [pltpu.SemaphoreType.DMA((n,)), pltpu.SemaphoreType.DMA((n,)), pltpu.SemaphoreType.DMA(())]
    res = pl.pallas_call(
        wrapped, name=name, grid=grid, in_specs=list(in_specs) + [HBM_SPEC] * nr,
        out_specs=list(out_specs) + [HBM_SPEC] * nr,
        out_shape=list(out_shape) + [_ride_shape(k, s) for k, s in rides],
        scratch_shapes=list(scratch_shapes) + sems, input_output_aliases=aliases or {},
        compiler_params=_params())(*args, *[s for _, s in rides])
    return list(res[:n_out]), list(res[n_out:])


def _bf(v):
    return v if v.dtype == bf16 else v.astype(bf16)


def mm_fwd_small(a, w3, name, *, tm=1024, out_dtype=f32):
    M, K = a.shape
    J, _, n = w3.shape
    tm = min(tm, M)

    def body(a_ref, w_ref, o_ref):
        av = _bf(a_ref[...])
        for j in range(J):
            o_ref[:, j * n:(j + 1) * n] = jnp.dot(av, w_ref[j], preferred_element_type=f32).astype(out_dtype)

    return pl.pallas_call(
        body, name=name, grid=(M // tm,), in_specs=[_rows(tm, K), _full((J, K, n))], out_specs=_rows(tm, J * n),
        out_shape=jax.ShapeDtypeStruct((M, J * n), out_dtype), compiler_params=_params())(a, w3)


def mm_bwd_w_small(a, dy, J, name, *, tm=2048):
    M, K = a.shape
    n = dy.shape[1] // J
    tm = min(tm, M)
    ns = M // tm

    def body(a_ref, dy_ref, o_ref, acc_ref):
        s = pl.program_id(0)
        part = lax.dot_general(_bf(a_ref[...]), _bf(dy_ref[...]), TN_DIMS, preferred_element_type=f32)

        @pl.when(s == 0)
        def _():
            acc_ref[...] = part

        @pl.when(s > 0)
        def _():
            acc_ref[...] += part

        @pl.when(s == ns - 1)
        def _():
            for j in range(J):
                o_ref[j] = acc_ref[:, j * n:(j + 1) * n].astype(bf16)

    return pl.pallas_call(
        body, name=name, grid=(ns,), in_specs=[_rows(tm, K), _rows(tm, J * n)], out_specs=_full((J, K, n)),
        out_shape=jax.ShapeDtypeStruct((J, K, n), bf16), scratch_shapes=[pltpu.VMEM((K, J * n), f32)],
        compiler_params=_params())(a, dy)


def mm_fwd(a, w3, name, *, tm=1024, tn=None, bias=None, outs=(f32,), first=None, second=None, rides=()):
    M, K = a.shape
    J, _, n = w3.shape
    tm = min(tm, M)
    tn = tn or n
    nl = n // tn
    nb = 0 if bias is None else 1

    def body(*refs):
        a_ref, w_ref = refs[0], refs[1]
        acc = jnp.dot(_bf(a_ref[...]), w_ref[0], preferred_element_type=f32)
        if bias is not None:
            acc = acc + refs[2][...]
        refs[2 + nb][...] = (acc if first is None else first(acc)).astype(outs[0])
        if len(outs) > 1:
            refs[3 + nb][...] = second(acc).astype(outs[1])

    in_specs = [pl.BlockSpec((tm, K), lambda j, l, i: (i, 0)),
                pl.BlockSpec((1, K, tn), lambda j, l, i: (j, 0, l))]
    args = [a, w3]
    if bias is not None:
        in_specs.append(pl.BlockSpec((1, tn), lambda j, l, i: (0, j * nl + l)))
        args.append(bias)
    res, landed = _pcall(
        body, args, name=name, grid=(J, nl, M // tm), in_specs=in_specs,
        out_specs=[pl.BlockSpec((tm, tn), lambda j, l, i: (i, j * nl + l)) for _ in outs],
        out_shape=[jax.ShapeDtypeStruct((M, J * n), dt) for dt in outs], rides=rides)
    res = res if len(outs) > 1 else res[0]
    return (res, landed) if rides else res


def mm_bwd_x(dy, w3, name, *, tm=TM, epi=None, extras=(), colsum=False, out_dtype=f32, rides=()):
    M = dy.shape[0]
    J, K, n = w3.shape
    tm = min(tm, M)
    nex = len(extras)

    def body(*refs):
        dy_ref, w_hbm = refs[0], refs[1]
        ex = refs[2:2 + nex]
        o_ref = refs[2 + nex]
        w_ref = refs[-1]
        i = pl.program_id(0)

        @pl.when(i == 0)
        def _():
            pltpu.sync_copy(w_hbm, w_ref)

        acc = None
        for j in range(J):
            part = lax.dot_general(_bf(dy_ref[:, j * n:(j + 1) * n]), w_ref[j], NT_DIMS, preferred_element_type=f32)
            acc = part if acc is None else acc + part
        if epi is not None:
            acc = epi(acc, *[e[...] for e in ex])
        o_ref[...] = acc.astype(out_dtype)
        if colsum:
            s_ref = refs[3 + nex]
            cs = jnp.sum(acc, axis=0, keepdims=True)

            @pl.when(i == 0)
            def _():
                s_ref[...] = cs

            @pl.when(i > 0)
            def _():
                s_ref[...] += cs

    in_specs = [pl.BlockSpec((tm, J * n), lambda i: (i, 0)), pl.BlockSpec(memory_space=pl.ANY)]
    in_specs += [pl.BlockSpec((tm, K), lambda i: (i, 0)) for _ in extras]
    out_specs = [pl.BlockSpec((tm, K), lambda i: (i, 0))]
    out_shape = [jax.ShapeDtypeStruct((M, K), out_dtype)]
    if colsum:
        out_specs.append(pl.BlockSpec((1, K), lambda i: (0, 0)))
        out_shape.append(jax.ShapeDtypeStruct((1, K), f32))
    res, landed = _pcall(body, [dy, w3, *extras], name=name, grid=(M // tm,), in_specs=in_specs, out_specs=out_specs,
                         out_shape=out_shape, scratch_shapes=[pltpu.VMEM((J, K, n), bf16)], rides=rides)
    res = res if colsum else res[0]
    return (res, landed) if rides else res


def mm_bwd_w(a, dy, J, name, *, tm=2048, tn=None, tk=None, rides=()):
    M, K = a.shape
    n = dy.shape[1] // J
    tm = min(tm, M)
    tn = tn or n
    tk = tk or K
    nl = n // tn
    nk = K // tk
    ns = M // tm

    def body(a_ref, dy_ref, o_ref, acc_ref):
        s = pl.program_id(3)
        part = lax.dot_general(_bf(a_ref[...]), _bf(dy_ref[...]), TN_DIMS, preferred_element_type=f32)

        @pl.when(s == 0)
        def _():
            acc_ref[...] = part

        @pl.when(s > 0)
        def _():
            acc_ref[...] += part

        @pl.when(s == ns - 1)
        def _():
            o_ref[0] = acc_ref[...].astype(bf16)

    res, landed = _pcall(
        body, [a, dy], name=name, grid=(J, nl, nk, ns),
        in_specs=[pl.BlockSpec((tm, tk), lambda j, l, k, s: (s, k)),
                  pl.BlockSpec((tm, tn), lambda j, l, k, s: (s, j * nl + l))],
        out_specs=[pl.BlockSpec((1, tk, tn), lambda j, l, k, s: (j, k, l))],
        out_shape=[jax.ShapeDtypeStruct((J, K, n), bf16)],
        scratch_shapes=[pltpu.VMEM((tk, tn), f32)], rides=rides)
    return (res[0], landed) if rides else res[0]


def _relu2(v):
    r = jnp.maximum(v, 0.0)
    return r * r


HALO = 16


def _shift_down(z, k, halo):
    r = lax.broadcasted_iota(jnp.int32, z.shape, 0)
    y = pltpu.roll(z, k, 0)
    for q in range(k):
        y = jnp.where(r == q, halo[HALO - k + q:HALO - k + q + 1, :], y)
    return y


def _shift_up(z, k, halo):
    tm = z.shape[0]
    r = lax.broadcasted_iota(jnp.int32, z.shape, 0)
    y = pltpu.roll(z, tm - k, 0)
    for q in range(k):
        y = jnp.where(r == tm - k + q, halo[q:q + 1, :], y)
    return y


def _prev_halo(tm, cb):
    return pl.BlockSpec((HALO, CONV_WIDTH), lambda i: (jnp.maximum(i * (tm // HALO) - 1, 0), cb))


def _next_halo(tm, cb, nblk):
    return pl.BlockSpec((HALO, CONV_WIDTH), lambda i: (jnp.minimum((i + 1) * (tm // HALO), nblk - 1), cb))


def _f32(ref):
    return ref[...].astype(f32)


def conv_fwd(proj, conv_w, tm=TM):
    S = proj.shape[0]

    def body(cb_ref, cc_ref, ch_ref, cch_ref, chh_ref, w_ref, a_ref):
        i = pl.program_id(0)
        z = _f32(cc_ref) * _f32(ch_ref)
        zh = jnp.where(i == 0, 0.0, _f32(cch_ref) * _f32(chh_ref))
        w = w_ref[...]
        dwz = w[0:1, :] * _shift_down(z, 2, zh) + w[1:2, :] * _shift_down(z, 1, zh) + w[2:3, :] * z
        a_ref[...] = (_f32(cb_ref) * dwz).astype(bf16)

    return pl.pallas_call(
        body, name="conv_fwd", grid=(S // tm,),
        in_specs=[_rows(tm, CONV_WIDTH, 6), _rows(tm, CONV_WIDTH, 7), _rows(tm, CONV_WIDTH, 8),
                  _prev_halo(tm, 7), _prev_halo(tm, 8), _full((3, CONV_WIDTH))],
        out_specs=_rows(tm, CONV_WIDTH),
        out_shape=jax.ShapeDtypeStruct((S, CONV_WIDTH), bf16), compiler_params=_params())(
            proj, proj, proj, proj, proj, conv_w)


def conv_bwd(da, proj, conv_w, dproj, tm=TM):
    S = proj.shape[0]
    nt = S // tm

    def body(da_ref, cb_ref, cc_ref, ch_ref, cch_ref, chh_ref, dan_ref, cbn_ref, w_ref, _, o_ref, dw_ref):
        i = pl.program_id(0)
        cc, ch, cb, dav = _f32(cc_ref), _f32(ch_ref), _f32(cb_ref), da_ref[...]
        z = cc * ch
        zh = jnp.where(i == 0, 0.0, _f32(cch_ref) * _f32(chh_ref))
        w = w_ref[...]
        z1 = _shift_down(z, 1, zh)
        z2 = _shift_down(z, 2, zh)
        dwz = w[0:1, :] * z2 + w[1:2, :] * z1 + w[2:3, :] * z
        g = dav * cb
        gn = jnp.where(i == nt - 1, 0.0, dan_ref[...] * _f32(cbn_ref))
        dz = w[2:3, :] * g + w[1:2, :] * _shift_up(g, 1, gn) + w[0:1, :] * _shift_up(g, 2, gn)
        o_ref[:, 0:CONV_WIDTH] = (dav * dwz).astype(bf16)
        o_ref[:, CONV_WIDTH:2 * CONV_WIDTH] = (dz * ch).astype(bf16)
        o_ref[:, 2 * CONV_WIDTH:3 * CONV_WIDTH] = (dz * cc).astype(bf16)

        @pl.when(i == 0)
        def _():
            dw_ref[...] = jnp.zeros_like(dw_ref)

        dw_ref[0:1, :] += jnp.sum(g * z2, axis=0, keepdims=True)
        dw_ref[1:2, :] += jnp.sum(g * z1, axis=0, keepdims=True)
        dw_ref[2:3, :] += jnp.sum(g * z, axis=0, keepdims=True)

    return pl.pallas_call(
        body, name="conv_bwd", grid=(nt,),
        in_specs=[_rows(tm, CONV_WIDTH), _rows(tm, CONV_WIDTH, 6), _rows(tm, CONV_WIDTH, 7), _rows(tm, CONV_WIDTH, 8),
                  _prev_halo(tm, 7), _prev_halo(tm, 8),
                  _next_halo(tm, 0, S // HALO), _next_halo(tm, 6, S // HALO), _full((3, CONV_WIDTH)),
                  pl.BlockSpec(memory_space=pl.ANY)],
        out_specs=[_rows(tm, 3 * CONV_WIDTH, GATE_COLS // (3 * CONV_WIDTH)), _full((3, CONV_WIDTH))],
        out_shape=[jax.ShapeDtypeStruct(dproj.shape, bf16), jax.ShapeDtypeStruct((3, CONV_WIDTH), f32)],
        input_output_aliases={9: 0},
        compiler_params=_params())(da, proj, proj, proj, proj, proj, da, proj, conv_w, dproj)


def _cmul(ar, ai, br, bi):
    return ar * br - ai * bi, ar * bi + ai * br


def ssm_prep(lr, li, ldt, bt_re, bt_im):
    def body(lr_ref, li_ref, ldt_ref, br_ref, bi_ref, ar_ref, ai_ref, kr_ref, ki_ref, bbr_ref, bbi_ref):
        lrv, liv = lr_ref[...], li_ref[...]
        dt = jnp.exp(ldt_ref[...])
        mag = jnp.exp(lrv * dt)
        ar = mag * jnp.cos(liv * dt)
        ai = mag * jnp.sin(liv * dt)
        den = lrv * lrv + liv * liv
        nr = ar - 1.0
        kr = (nr * lrv + ai * liv) / den
        ki = (ai * lrv - nr * liv) / den
        ar_ref[...], ai_ref[...], kr_ref[...], ki_ref[...] = ar, ai, kr, ki
        bbr_ref[...] = kr * br_ref[...] - ki * bi_ref[...]
        bbi_ref[...] = kr * bi_ref[...] + ki * br_ref[...]

    v = jax.ShapeDtypeStruct((1, N_STATES), f32)
    m = jax.ShapeDtypeStruct((SSM_GROUP, N_STATES), f32)
    return pl.pallas_call(body, name="ssm_prep", out_shape=[v, v, v, v, m, m])(lr, li, ldt, bt_re, bt_im)


def _pow_segment(ar, ai, seg_len):
    pr, pi = ar, ai
    for _ in range(int(math.log2(seg_len))):
        pr, pi = _cmul(pr, pi, pr, pi)
    return pr, pi


GELU_K = math.sqrt(2.0 / math.pi)
GELU_C = 0.044715


def _gelu(v):
    return 0.5 * v * (1.0 + jnp.tanh(GELU_K * (v + GELU_C * v * v * v)))


def _gelu_grad(v):
    t = jnp.tanh(GELU_K * (v + GELU_C * v * v * v))
    return 0.5 * (1.0 + t) + 0.5 * v * (1.0 - t * t) * GELU_K * (1.0 + 3.0 * GELU_C * v * v)


def ssm_scan_fwd(u_perm, bd_r, bd_i, ar, ai, *, final=None, rides=()):
    S = u_perm.shape[0]
    R = SCAN_ROWS
    nblk = S // R
    seg_len = S // N_SEG
    nstrip = N_STATES // LANE_STRIP
    store = final is not None

    def body(*refs):
        if store:
            (u_ref, bdr_ref, bdi_ref, ar_ref, ai_ref, cdr_ref, cdi_ref, d_ref, ir_ref, ii_ref,
             y_ref, ys_ref, sr_ref, si_ref, bur, bui, car, cai) = refs
        else:
            u_ref, bdr_ref, bdi_ref, ar_ref, ai_ref, or_ref, oi_ref, bur, bui, car, cai = refs
        i = pl.program_id(0)
        ub = _bf(u_ref[...])
        u = ub.astype(f32)
        for cb in range(4):
            us = ub[:, cb * 128:(cb + 1) * 128]
            bur[:, cb * 512:(cb + 1) * 512] = jnp.dot(us, bdr_ref[cb], preferred_element_type=f32)
            bui[:, cb * 512:(cb + 1) * 512] = jnp.dot(us, bdi_ref[cb], preferred_element_type=f32)

        @pl.when(i == 0)
        def _():
            if store:
                car[...] = ir_ref[...]
                cai[...] = ii_ref[...]
            else:
                car[...] = jnp.zeros_like(car)
                cai[...] = jnp.zeros_like(cai)

        for ls in range(nstrip):
            lanes = pl.ds(ls * LANE_STRIP, LANE_STRIP)
            a_r = jnp.broadcast_to(ar_ref[:, lanes], (N_SEG, LANE_STRIP))
            a_i = jnp.broadcast_to(ai_ref[:, lanes], (N_SEG, LANE_STRIP))

            def step(t, carry, lanes=lanes, a_r=a_r, a_i=a_i):
                s_r, s_i = carry
                row = pl.multiple_of(t * 8, 8)
                n_r = a_r * s_r - a_i * s_i + bur[pl.ds(row, 8), lanes]
                n_i = a_r * s_i + a_i * s_r + bui[pl.ds(row, 8), lanes]
                if store:
                    sr_ref[pl.ds(row, 8), lanes] = n_r
                    si_ref[pl.ds(row, 8), lanes] = n_i
                return n_r, n_i

            e_r, e_i = lax.fori_loop(0, R // 8, step, (car[:, lanes], cai[:, lanes]), unroll=True)
            car[:, lanes] = e_r
            cai[:, lanes] = e_i

        if store:
            for cb in range(4):
                st_r = sr_ref[:, cb * 512:(cb + 1) * 512].astype(bf16)
                st_i = si_ref[:, cb * 512:(cb + 1) * 512].astype(bf16)
                y = (jnp.dot(st_r, cdr_ref[cb], preferred_element_type=f32)
                     - jnp.dot(st_i, cdi_ref[cb], preferred_element_type=f32))
                cols = slice(cb * 128, (cb + 1) * 128)
                y = y + d_ref[:, cols] * u[:, cols]
                y_ref[:, cols] = y
                ys_ref[:, cols] = _gelu(y).astype(bf16)
        else:
            @pl.when(i == nblk - 1)
            def _():
                p_r, p_i = _pow_segment(ar_ref[...], ai_ref[...], seg_len)
                t_r, t_i = car[0:1, :], cai[0:1, :]
                or_ref[0:1, :] = jnp.zeros((1, N_STATES), f32)
                oi_ref[0:1, :] = jnp.zeros((1, N_STATES), f32)
                for j in range(1, N_SEG):
                    or_ref[j:j + 1, :] = t_r
                    oi_ref[j:j + 1, :] = t_i
                    m_r, m_i = _cmul(p_r, p_i, t_r, t_i)
                    t_r, t_i = car[j:j + 1, :] + m_r, cai[j:j + 1, :] + m_i

    blk = lambda w: pl.BlockSpec((R, w), lambda i: (i, 0))
    in_specs = [blk(SSM_WIDTH), _full((4, 128, 512)), _full((4, 128, 512)), _full((1, N_STATES)), _full((1, N_STATES))]
    args = [u_perm, bd_r, bd_i, ar, ai]
    scratch = [pltpu.VMEM((R, N_STATES), f32), pltpu.VMEM((R, N_STATES), f32),
               pltpu.VMEM((N_SEG, N_STATES), f32), pltpu.VMEM((N_SEG, N_STATES), f32)]
    if store:
        in_specs += [_full((4, 512, 128)), _full((4, 512, 128)), _full((1, SSM_WIDTH)),
                     _full((N_SEG, N_STATES)), _full((N_SEG, N_STATES))]
        args += list(final)
        out_specs = [blk(SSM_WIDTH), blk(SSM_WIDTH), blk(N_STATES), blk(N_STATES)]
        out_shape = [jax.ShapeDtypeStruct((S, SSM_WIDTH), f32), jax.ShapeDtypeStruct((S, SSM_WIDTH), bf16),
                     jax.ShapeDtypeStruct((S, N_STATES), f32), jax.ShapeDtypeStruct((S, N_STATES), f32)]
        name = "ssm_scan_fwd"
    else:
        out_specs = [_full((N_SEG, N_STATES)), _full((N_SEG, N_STATES))]
        out_shape = [jax.ShapeDtypeStruct((N_SEG, N_STATES), f32)] * 2
        name = "ssm_scan_fwd_carry"
    res, landed = _pcall(body, args, name=name, grid=(nblk,), in_specs=in_specs, out_specs=out_specs,
                         out_shape=out_shape, scratch_shapes=scratch, rides=rides)
    return (res, landed) if rides else res


def ssm_scan_bwd(dys_perm, y_perm, cdt_r, cdt_i, ar, ai, *, final=None, rides=()):
    S = dys_perm.shape[0]
    R = SCAN_ROWS
    nblk = S // R
    seg_len = S // N_SEG
    nstrip = N_STATES // LANE_STRIP
    store = final is not None

    def body(*refs):
        if store:
            (dys_ref, y_ref, cdr_ref, cdi_ref, ar_ref, ai_ref, u_ref, sr_ref, si_ref, bdr_ref, bdi_ref, d_ref, ir_ref,
             ii_ref, du_ref, dbr_ref, dbi_ref, dcr_ref, dci_ref, dar_ref, dai_ref, dd_ref, dsr, dsi, lmr, lmi, car,
             cai) = refs
        else:
            dys_ref, y_ref, cdr_ref, cdi_ref, ar_ref, ai_ref, or_ref, oi_ref, dsr, dsi, car, cai = refs
        i = pl.program_id(0)
        dy = dys_ref[...] * _gelu_grad(y_ref[...])
        dyb = dy.astype(bf16)
        for cb in range(4):
            ds_ = dyb[:, cb * 128:(cb + 1) * 128]
            dsr[:, cb * 512:(cb + 1) * 512] = jnp.dot(ds_, cdr_ref[cb], preferred_element_type=f32)
            dsi[:, cb * 512:(cb + 1) * 512] = -jnp.dot(ds_, cdi_ref[cb], preferred_element_type=f32)

        @pl.when(i == 0)
        def _():
            if store:
                car[...] = ir_ref[...]
                cai[...] = ii_ref[...]
                dar_ref[...] = jnp.zeros_like(dar_ref)
                dai_ref[...] = jnp.zeros_like(dai_ref)
                dbr_ref[...] = jnp.zeros_like(dbr_ref)
                dbi_ref[...] = jnp.zeros_like(dbi_ref)
                dcr_ref[...] = jnp.zeros_like(dcr_ref)
                dci_ref[...] = jnp.zeros_like(dci_ref)
                dd_ref[...] = jnp.zeros_like(dd_ref)
            else:
                car[...] = jnp.zeros_like(car)
                cai[...] = jnp.zeros_like(cai)

        for ls in range(nstrip):
            lanes = pl.ds(ls * LANE_STRIP, LANE_STRIP)
            a_r = jnp.broadcast_to(ar_ref[:, lanes], (N_SEG, LANE_STRIP))
            a_i = jnp.broadcast_to(ai_ref[:, lanes], (N_SEG, LANE_STRIP))
            zero = jnp.zeros((N_SEG, LANE_STRIP), f32)

            def step(k, carry, lanes=lanes, a_r=a_r, a_i=a_i):
                l_r, l_i, g_r, g_i = carry
                row = pl.multiple_of((R // 8 - 1 - k) * 8, 8)
                if store:
                    s_r = sr_ref[pl.ds(row, 8), lanes]
                    s_i = si_ref[pl.ds(row, 8), lanes]
                    g_r = g_r + l_r * s_r + l_i * s_i
                    g_i = g_i + l_i * s_r - l_r * s_i
                n_r = dsr[pl.ds(row, 8), lanes] + a_r * l_r + a_i * l_i
                n_i = dsi[pl.ds(row, 8), lanes] + a_r * l_i - a_i * l_r
                if store:
                    lmr[pl.ds(row, 8), lanes] = n_r
                    lmi[pl.ds(row, 8), lanes] = n_i
                return n_r, n_i, g_r, g_i

            e_r, e_i, g_r, g_i = lax.fori_loop(0, R // 8, step, (car[:, lanes], cai[:, lanes], zero, zero),
                                               unroll=True)
            car[:, lanes] = e_r
            cai[:, lanes] = e_i
            if store:
                dar_ref[:, lanes] += g_r
                dai_ref[:, lanes] += g_i

        if store:
            ub = _bf(u_ref[...])
            u = ub.astype(f32)
            for cb in range(4):
                cols = slice(cb * 128, (cb + 1) * 128)
                st = slice(cb * 512, (cb + 1) * 512)
                l_r = lmr[:, st].astype(bf16)
                l_i = lmi[:, st].astype(bf16)
                du = (jnp.dot(l_r, bdr_ref[cb], preferred_element_type=f32)
                      + jnp.dot(l_i, bdi_ref[cb], preferred_element_type=f32))
                du_ref[:, cols] = (du + d_ref[:, cols] * dy[:, cols]).astype(bf16)
                dbr_ref[cb] += lax.dot_general(l_r, ub[:, cols], TN_DIMS, preferred_element_type=f32)
                dbi_ref[cb] += lax.dot_general(l_i, ub[:, cols], TN_DIMS, preferred_element_type=f32)
                dcr_ref[cb] += lax.dot_general(dyb[:, cols], sr_ref[:, st].astype(bf16), TN_DIMS,
                                               preferred_element_type=f32)
                dci_ref[cb] += lax.dot_general(dyb[:, cols], si_ref[:, st].astype(bf16), TN_DIMS,
                                               preferred_element_type=f32)
            dd_ref[...] += jnp.sum(dy * u, axis=0, keepdims=True)
        else:
            @pl.when(i == nblk - 1)
            def _():
                p_r, p_i = _pow_segment(ar_ref[...], ai_ref[...], seg_len)
                p_i = -p_i
                t_r, t_i = car[N_SEG - 1:N_SEG, :], cai[N_SEG - 1:N_SEG, :]
                or_ref[N_SEG - 1:N_SEG, :] = jnp.zeros((1, N_STATES), f32)
                oi_ref[N_SEG - 1:N_SEG, :] = jnp.zeros((1, N_STATES), f32)
                for j in range(N_SEG - 2, -1, -1):
                    or_ref[j:j + 1, :] = t_r
                    oi_ref[j:j + 1, :] = t_i
                    m_r, m_i = _cmul(p_r, p_i, t_r, t_i)
                    t_r, t_i = car[j:j + 1, :] + m_r, cai[j:j + 1, :] + m_i

    blk = lambda w: pl.BlockSpec((R, w), lambda i: (nblk - 1 - i, 0))
    in_specs = [blk(SSM_WIDTH), blk(SSM_WIDTH), _full((4, 128, 512)), _full((4, 128, 512)), _full((1, N_STATES)),
                _full((1, N_STATES))]
    args = [dys_perm, y_perm, cdt_r, cdt_i, ar, ai]
    seg = jax.ShapeDtypeStruct((N_SEG, N_STATES), f32)
    if store:
        in_specs += [blk(SSM_WIDTH), blk(N_STATES), blk(N_STATES), _full((4, 512, 128)), _full((4, 512, 128)),
                     _full((1, SSM_WIDTH)), _full((N_SEG, N_STATES)), _full((N_SEG, N_STATES))]
        args += list(final)
        out_specs = [blk(SSM_WIDTH), _full((4, 512, 128)), _full((4, 512, 128)), _full((4, 128, 512)),
                     _full((4, 128, 512)), _full((N_SEG, N_STATES)), _full((N_SEG, N_STATES)), _full((1, SSM_WIDTH))]
        b_acc = jax.ShapeDtypeStruct((4, 512, 128), f32)
        c_acc = jax.ShapeDtypeStruct((4, 128, 512), f32)
        out_shape = [jax.ShapeDtypeStruct((S, SSM_WIDTH), bf16), b_acc, b_acc, c_acc, c_acc, seg, seg,
                     jax.ShapeDtypeStruct((1, SSM_WIDTH), f32)]
        scratch = [pltpu.VMEM((R, N_STATES), f32)] * 4 + [pltpu.VMEM((N_SEG, N_STATES), f32)] * 2
        name = "ssm_scan_bwd"
    else:
        out_specs = [_full((N_SEG, N_STATES)), _full((N_SEG, N_STATES))]
        out_shape = [seg, seg]
        scratch = [pltpu.VMEM((R, N_STATES), f32)] * 2 + [pltpu.VMEM((N_SEG, N_STATES), f32)] * 2
        name = "ssm_scan_bwd_carry"
    res, landed = _pcall(body, args, name=name, grid=(nblk,), in_specs=in_specs, out_specs=out_specs,
                         out_shape=out_shape, scratch_shapes=scratch, rides=rides)
    return (res, landed) if rides else res


def ssm_param_bwd(dbb_r, dbb_i, bt_re, bt_im, kr, ki, ar, ai, lr, li, ldt, da_r, da_i):
    def body(dbr_ref, dbi_ref, br_ref, bi_ref, kr_ref, ki_ref, ar_ref, ai_ref, lr_ref, li_ref, ldt_ref, dar_ref,
             dai_ref, obr_ref, obi_ref, olr_ref, oli_ref, odt_ref):
        dbr, dbi, b_r, b_i = dbr_ref[...], dbi_ref[...], br_ref[...], bi_ref[...]
        k_r, k_i, a_r, a_i = kr_ref[...], ki_ref[...], ar_ref[...], ai_ref[...]
        l_r, l_i = lr_ref[...], li_ref[...]
        dt = jnp.exp(ldt_ref[...])
        obr_ref[...] = k_r * dbr + k_i * dbi
        obi_ref[...] = k_r * dbi - k_i * dbr
        gk_r = jnp.sum(dbr * b_r + dbi * b_i, axis=0, keepdims=True)
        gk_i = jnp.sum(dbi * b_r - dbr * b_i, axis=0, keepdims=True)
        ga_r = jnp.sum(dar_ref[...], axis=0, keepdims=True)
        ga_i = jnp.sum(dai_ref[...], axis=0, keepdims=True)
        den = l_r * l_r + l_i * l_i
        c_r, c_i = l_r / den, l_i / den
        m_r, m_i = _cmul(c_r, c_i, gk_r, gk_i)
        g_r, g_i = ga_r + m_r, ga_i + m_i
        t1_r, t1_i = _cmul(dt * a_r, -dt * a_i, g_r, g_i)
        q_r, q_i = _cmul(k_r, k_i, c_r, -c_i)
        t2_r, t2_i = _cmul(-q_r, q_i, gk_r, gk_i)
        olr_ref[...] = t1_r + t2_r
        oli_ref[...] = t1_i + t2_i
        w_r, w_i = _cmul(l_r, l_i, a_r, a_i)
        odt_ref[...] = dt * (w_r * g_r + w_i * g_i)

    v = jax.ShapeDtypeStruct((1, N_STATES), f32)
    m = jax.ShapeDtypeStruct((SSM_GROUP, N_STATES), f32)
    return pl.pallas_call(body, name="ssm_param_bwd", out_shape=[m, m, v, v, v])(
        dbb_r, dbb_i, bt_re, bt_im, kr, ki, ar, ai, lr, li, ldt, da_r, da_i)


def group_sum(v):
    def body(v_ref, o_ref):
        o_ref[...] = jnp.sum(v_ref[...], axis=-1, keepdims=True)
    return pl.pallas_call(body, name="group_sum", out_shape=jax.ShapeDtypeStruct((v.shape[0], 1), f32))(v)


ATTN_SCALE = XATTN_HEAD_DIM ** -0.5


def _attn_probs(q_h, k_h):
    s = lax.dot_general(q_h, k_h, NT_DIMS, preferred_element_type=f32) * ATTN_SCALE
    e = jnp.exp(s - jnp.max(s, axis=-1, keepdims=True))
    return e / jnp.sum(e, axis=-1, keepdims=True)


def attn_fwd(proj, kv, tm=TM):
    S = proj.shape[0]
    M = kv.shape[0]

    def body(q_ref, kv_ref, o_ref):
        for h in range(XATTN_HEADS):
            cols = slice(h * XATTN_HEAD_DIM, (h + 1) * XATTN_HEAD_DIM)
            q_h = q_ref[:, cols].astype(bf16)
            k_h = kv_ref[:, cols]
            v_h = kv_ref[:, XATTN_WIDTH + h * XATTN_HEAD_DIM:XATTN_WIDTH + (h + 1) * XATTN_HEAD_DIM]
            p = _attn_probs(q_h, k_h)
            o_ref[:, cols] = jnp.dot(p.astype(bf16), v_h, preferred_element_type=f32).astype(bf16)

    return pl.pallas_call(
        body, name="attn_fwd", grid=(S // tm,), in_specs=[_rows(tm, XATTN_WIDTH, 10), _full((M, 2 * XATTN_WIDTH))],
        out_specs=_rows(tm, XATTN_WIDTH), out_shape=jax.ShapeDtypeStruct((S, XATTN_WIDTH), bf16),
        compiler_params=_params())(proj, kv)


def attn_bwd(do, proj, kv, dproj, tm=TM):
    S = proj.shape[0]
    M = kv.shape[0]

    def body(do_ref, q_ref, kv_ref, _, dq_ref, dkv_ref):
        i = pl.program_id(0)

        @pl.when(i == 0)
        def _():
            dkv_ref[...] = jnp.zeros_like(dkv_ref)

        for h in range(XATTN_HEADS):
            cols = slice(h * XATTN_HEAD_DIM, (h + 1) * XATTN_HEAD_DIM)
            vcols = slice(XATTN_WIDTH + h * XATTN_HEAD_DIM, XATTN_WIDTH + (h + 1) * XATTN_HEAD_DIM)
            q_h = q_ref[:, cols].astype(bf16)
            k_h = kv_ref[:, cols]
            v_h = kv_ref[:, vcols]
            do_h = do_ref[:, cols]
            p = _attn_probs(q_h, k_h)
            dp = lax.dot_general(do_h, v_h, NT_DIMS, preferred_element_type=f32)
            ds = (p * (dp - jnp.sum(dp * p, axis=-1, keepdims=True)) * ATTN_SCALE).astype(bf16)
            dq_ref[:, cols] = jnp.dot(ds, k_h, preferred_element_type=f32).astype(bf16)
            dkv_ref[:, cols] += lax.dot_general(ds, q_h, TN_DIMS, preferred_element_type=f32)
            dkv_ref[:, vcols] += lax.dot_general(p.astype(bf16), do_h, TN_DIMS, preferred_element_type=f32)

    return pl.pallas_call(
        body, name="attn_bwd", grid=(S // tm,),
        in_specs=[_rows(tm, XATTN_WIDTH), _rows(tm, XATTN_WIDTH, 10), _full((M, 2 * XATTN_WIDTH)),
                  pl.BlockSpec(memory_space=pl.ANY)],
        out_specs=[_rows(tm, XATTN_WIDTH, 10), _full((M, 2 * XATTN_WIDTH))],
        out_shape=[jax.ShapeDtypeStruct(dproj.shape, bf16), jax.ShapeDtypeStruct((M, 2 * XATTN_WIDTH), f32)],
        input_output_aliases={3: 0},
        compiler_params=_params())(do, proj, kv, dproj)


def merge_fwd(proj, b_gate, y_a, glu, y_c, tm=TM, rides=()):
    S = proj.shape[0]

    def body(g0_ref, g1_ref, g2_ref, b_ref, ya_ref, ga_ref, gb_ref, yc_ref, o_ref):
        b = b_ref[...]
        g0 = _sig(_f32(g0_ref) + b[:, 0:D_MODEL])
        g1 = _sig(_f32(g1_ref) + b[:, D_MODEL:2 * D_MODEL])
        g2 = _sig(_f32(g2_ref) + b[:, 2 * D_MODEL:3 * D_MODEL])
        y_b = ga_ref[...].astype(f32) * _sig(gb_ref[...].astype(f32))
        o_ref[...] = (g0 * ya_ref[...].astype(f32) + g1 * y_b + g2 * yc_ref[...].astype(f32)).astype(bf16)

    res, landed = _pcall(
        body, [proj, proj, proj, b_gate, y_a, glu, glu, y_c], name="merge_fwd", grid=(S // tm,),
        in_specs=[_rows(tm, D_MODEL, 0), _rows(tm, D_MODEL, 1), _rows(tm, D_MODEL, 2), _full((1, GATE_COLS)),
                  _rows(tm, D_MODEL), _rows(tm, D_MODEL, 0), _rows(tm, D_MODEL, 1), _rows(tm, D_MODEL)],
        out_specs=[_rows(tm, D_MODEL)], out_shape=[jax.ShapeDtypeStruct((S, D_MODEL), bf16)], rides=rides)
    return (res[0], landed) if rides else res[0]


def merge_bwd(dm, proj, b_gate, y_a, glu, y_c, tm=256, rides=()):
    S = proj.shape[0]

    def body(dm_ref, g0_ref, g1_ref, g2_ref, b_ref, ya_ref, ga_ref, gb_ref, yc_ref,
             dg_ref, dya_ref, dgl_ref, dyc_ref, dbg_ref):
        i = pl.program_id(0)
        b = b_ref[...]
        dmv = dm_ref[...].astype(f32)
        ga, sb = ga_ref[...].astype(f32), _sig(gb_ref[...].astype(f32))
        ys = (ya_ref[...].astype(f32), ga * sb, yc_ref[...].astype(f32))
        gs = tuple(_sig(_f32(r) + b[:, k * D_MODEL:(k + 1) * D_MODEL]) for k, r in enumerate((g0_ref, g1_ref, g2_ref)))

        @pl.when(i == 0)
        def _():
            dbg_ref[...] = jnp.zeros_like(dbg_ref)

        for k in range(3):
            dpre = dmv * ys[k] * gs[k] * (1.0 - gs[k])
            dg_ref[:, k * D_MODEL:(k + 1) * D_MODEL] = dpre.astype(bf16)
            dbg_ref[:, k * D_MODEL:(k + 1) * D_MODEL] += jnp.sum(dpre, axis=0, keepdims=True)
        dya_ref[...] = (dmv * gs[0]).astype(bf16)
        dyc_ref[...] = (dmv * gs[2]).astype(bf16)
        dyb = dmv * gs[1]
        dgl_ref[:, 0:D_MODEL] = (dyb * sb).astype(bf16)
        dgl_ref[:, D_MODEL:2 * D_MODEL] = (dyb * ga * sb * (1.0 - sb)).astype(bf16)

    res, landed = _pcall(
        body, [dm, proj, proj, proj, b_gate, y_a, glu, glu, y_c], name="merge_bwd", grid=(S // tm,),
        in_specs=[_rows(tm, D_MODEL), _rows(tm, D_MODEL, 0), _rows(tm, D_MODEL, 1), _rows(tm, D_MODEL, 2),
                  _full((1, GATE_COLS)), _rows(tm, D_MODEL), _rows(tm, D_MODEL, 0), _rows(tm, D_MODEL, 1),
                  _rows(tm, D_MODEL)],
        out_specs=[_rows(tm, GATE_COLS), _rows(tm, D_MODEL), _rows(tm, 2 * D_MODEL), _rows(tm, D_MODEL),
                   _full((1, GATE_COLS))],
        out_shape=[jax.ShapeDtypeStruct((S, IN_COLS), bf16), jax.ShapeDtypeStruct((S, D_MODEL), bf16),
                   jax.ShapeDtypeStruct((S, 2 * D_MODEL), bf16), jax.ShapeDtypeStruct((S, D_MODEL), bf16),
                   jax.ShapeDtypeStruct((1, GATE_COLS), f32)], rides=rides)
    return (res, landed) if rides else res


def _ln_stats(r):
    mu = jnp.mean(r, axis=-1, keepdims=True)
    xc = r - mu
    var = jnp.mean(xc * xc, axis=-1, keepdims=True)
    rstd = lax.rsqrt(var + LN_EPS)
    return xc * rstd, rstd


def _ln_bwd(dy, xhat, rstd, g):
    dxh = dy * g
    return rstd * (dxh - jnp.mean(dxh, axis=-1, keepdims=True) - xhat * jnp.mean(dxh * xhat, axis=-1, keepdims=True))


def ln1_fwd(x, t1, g, b, tm=TM):
    S = x.shape[0]

    def body(x_ref, t_ref, g_ref, b_ref, o_ref, ob_ref):
        xhat, _ = _ln_stats(ALPHA * x_ref[...] + t_ref[...])
        h = xhat * g_ref[...] + b_ref[...]
        o_ref[...] = h
        ob_ref[...] = h.astype(bf16)

    return pl.pallas_call(
        body, name="ln1_fwd", grid=(S // tm,),
        in_specs=[_rows(tm, D_MODEL), _rows(tm, D_MODEL), _full((1, D_MODEL)), _full((1, D_MODEL))],
        out_specs=[_rows(tm, D_MODEL), _rows(tm, D_MODEL)],
        out_shape=[jax.ShapeDtypeStruct((S, D_MODEL), f32), jax.ShapeDtypeStruct((S, D_MODEL), bf16)],
        compiler_params=_params())(x, t1, g, b)


def ln2_loss_bwd(h1, t2, b_down, g, b, target, tm=TM):
    S = h1.shape[0]

    def body(h_ref, t_ref, bd_ref, g_ref, b_ref, y_ref, dr_ref, drb_ref, l_ref, dg_ref, db_ref, dbd_ref):
        i = pl.program_id(0)
        xhat, rstd = _ln_stats(ALPHA * h_ref[...] + t_ref[...] + bd_ref[...])
        gv = g_ref[...]
        err = xhat * gv + b_ref[...] - y_ref[...]
        dout = err * (1.0 / D_MODEL)
        dr = _ln_bwd(dout, xhat, rstd, gv)
        dr_ref[...] = dr
        drb_ref[...] = dr.astype(bf16)
        part = 0.5 * jnp.sum(jnp.sum(err * err, axis=-1, keepdims=True) * (1.0 / D_MODEL), axis=0, keepdims=True)

        @pl.when(i == 0)
        def _():
            l_ref[...] = jnp.zeros_like(l_ref)
            dg_ref[...] = jnp.zeros_like(dg_ref)
            db_ref[...] = jnp.zeros_like(db_ref)
            dbd_ref[...] = jnp.zeros_like(dbd_ref)

        l_ref[...] += jnp.broadcast_to(part, l_ref.shape)
        dg_ref[...] += jnp.sum(dout * xhat, axis=0, keepdims=True)
        db_ref[...] += jnp.sum(dout, axis=0, keepdims=True)
        dbd_ref[...] += jnp.sum(dr, axis=0, keepdims=True)

    vec = jax.ShapeDtypeStruct((1, D_MODEL), f32)
    return pl.pallas_call(
        body, name="ln2_loss_bwd", grid=(S // tm,),
        in_specs=[_rows(tm, D_MODEL), _rows(tm, D_MODEL), _full((1, D_MODEL)), _full((1, D_MODEL)),
                  _full((1, D_MODEL)), _rows(tm, D_MODEL)],
        out_specs=[_rows(tm, D_MODEL), _rows(tm, D_MODEL), _full((1, 128)), _full((1, D_MODEL)), _full((1, D_MODEL)),
                   _full((1, D_MODEL))],
        out_shape=[jax.ShapeDtypeStruct((S, D_MODEL), f32), jax.ShapeDtypeStruct((S, D_MODEL), bf16),
                   jax.ShapeDtypeStruct((1, 128), f32), vec, vec, vec],
        compiler_params=_params())(h1, t2, b_down, g, b, target)


def ln1_bwd(x, t1, dr2, t3, g, tm=TM):
    S = x.shape[0]

    def body(x_ref, t_ref, dr2_ref, t3_ref, g_ref, dr_ref, drb_ref, dg_ref, db_ref):
        i = pl.program_id(0)
        xhat, rstd = _ln_stats(ALPHA * x_ref[...] + t_ref[...])
        dh = ALPHA * dr2_ref[...] + t3_ref[...]
        dr = _ln_bwd(dh, xhat, rstd, g_ref[...])
        dr_ref[...] = dr
        drb_ref[...] = dr.astype(bf16)

        @pl.when(i == 0)
        def _():
            dg_ref[...] = jnp.zeros_like(dg_ref)
            db_ref[...] = jnp.zeros_like(db_ref)

        dg_ref[...] += jnp.sum(dh * xhat, axis=0, keepdims=True)
        db_ref[...] += jnp.sum(dh, axis=0, keepdims=True)

    vec = jax.ShapeDtypeStruct((1, D_MODEL), f32)
    return pl.pallas_call(
        body, name="ln1_bwd", grid=(S // tm,),
        in_specs=[_rows(tm, D_MODEL)] * 4 + [_full((1, D_MODEL))],
        out_specs=[_rows(tm, D_MODEL), _rows(tm, D_MODEL), _full((1, D_MODEL)), _full((1, D_MODEL))],
        out_shape=[jax.ShapeDtypeStruct((S, D_MODEL), f32), jax.ShapeDtypeStruct((S, D_MODEL), bf16), vec, vec],
        compiler_params=_params())(x, t1, dr2, t3, g)


def exchange(*rides, name):
    return _pcall(lambda: None, [], name=name, grid=(), in_specs=[], out_specs=[], out_shape=[], rides=rides)[1]


def gather_two_level(shard, name):
    K, n = shard.shape
    h = K // 2

    def body(in_ref, out_ref, ici_send, ici_recv, d2d_send, d2d_recv, local_sem):
        x, y, c = lax.axis_index("x"), lax.axis_index("y"), lax.axis_index("c")
        me = 2 * x + y
        mine = pl.ds(pl.multiple_of(c * h, 16), h)
        theirs = pl.ds(pl.multiple_of((1 - c) * h, 16), h)
        peers = _xy_peers(x, y)

        def over_ici(k, dst):
            return pltpu.make_async_remote_copy(src_ref=in_ref.at[mine], dst_ref=dst, send_sem=ici_send.at[k],
                                                recv_sem=ici_recv.at[k], device_id=(*peers[k], c), device_id_type=MESH)

        def over_d2d(k, rows):
            blk = out_ref.at[2 * peers[k][0] + peers[k][1], rows]
            return pltpu.make_async_remote_copy(src_ref=blk, dst_ref=blk, send_sem=d2d_send.at[k],
                                                recv_sem=d2d_recv.at[k], device_id=(x, y, 1 - c), device_id_type=MESH)

        local = pltpu.make_async_copy(in_ref, out_ref.at[me], local_sem)
        local.start()
        started = [over_ici(k, out_ref.at[me, mine]) for k in range(3)]
        for cp in started:
            cp.start()
        for k in range(3):
            over_ici(k, out_ref.at[2 * peers[k][0] + peers[k][1], mine]).wait_recv()
            passed = over_d2d(k, mine)
            passed.start()
            started.append(passed)
        for k in range(3):
            over_d2d(k, theirs).wait_recv()
        for cp in started:
            cp.wait_send()
        local.wait()

    return pl.pallas_call(
        body, name=name, in_specs=[HBM_SPEC], out_specs=HBM_SPEC,
        out_shape=jax.ShapeDtypeStruct((4, K, n), shard.dtype),
        scratch_shapes=[pltpu.SemaphoreType.DMA((3,))] * 4 + [pltpu.SemaphoreType.DMA(())])(shard)


ROW_TILE = 256


def sum_slots(recv, name):
    n, R, C = recv.shape
    tm = min(R, ROW_TILE)

    def body(r_ref, o_ref):
        acc = r_ref[0].astype(f32)
        for k in range(1, n):
            acc = acc + r_ref[k].astype(f32)
        o_ref[...] = acc

    return pl.pallas_call(
        body, name=name, grid=(R // tm,), in_specs=[pl.BlockSpec((n, tm, C), lambda i: (0, i, 0))],
        out_specs=_rows(tm, C), out_shape=jax.ShapeDtypeStruct((R, C), f32), compiler_params=_params())(recv)


def _adamw(w, g, m, v):
    m = ADAM_B1 * m + (1.0 - ADAM_B1) * g
    v = ADAM_B2 * v + (1.0 - ADAM_B2) * (g * g)
    m_hat = m / (1.0 - ADAM_B1 ** ADAM_STEP)
    v_hat = v / (1.0 - ADAM_B2 ** ADAM_STEP)
    delta = -ADAM_LR * (m_hat / (jnp.sqrt(v_hat) + ADAM_EPS) + ADAM_WD * w)
    return delta, m, v


def adam_pair(p, q, w, m, v, name):
    R, C = w.shape
    tm = min(R, ROW_TILE)

    def body(p_ref, q_ref, w_ref, m_ref, v_ref, g_ref, d_ref, nm_ref, nv_ref):
        g = p_ref[...] + q_ref[...]
        g_ref[...] = g
        d_ref[...], nm_ref[...], nv_ref[...] = _adamw(w_ref[...], g, m_ref[...], v_ref[...])

    o = jax.ShapeDtypeStruct((R, C), f32)
    return pl.pallas_call(body, name=name, grid=(R // tm,), in_specs=[_rows(tm, C)] * 5,
                          out_specs=[_rows(tm, C)] * 4, out_shape=[o] * 4, compiler_params=_params())(p, q, w, m, v)


def adam_slots(recv, w, m, v, name):
    n = recv.shape[0]

    def body(r_ref, w_ref, m_ref, v_ref, g_ref, d_ref, nm_ref, nv_ref):
        g = r_ref[0]
        for k in range(1, n):
            g = g + r_ref[k]
        g_ref[...] = g
        d_ref[...], nm_ref[...], nv_ref[...] = _adamw(w_ref[...], g, m_ref[...], v_ref[...])

    o = jax.ShapeDtypeStruct(w.shape, f32)
    return pl.pallas_call(body, name=name, out_shape=[o] * 4, compiler_params=_params())(recv, w, m, v)


BIG = ("w_in", "w_conv_out", "w_glu", "w_xattn_out", "w_kv", "w_out", "w_up", "w_down")
MID = ("w_conv_out", "w_glu", "w_xattn_out", "w_kv", "w_out")
SMALL = ("b_gate", "ssm_lam_re", "ssm_lam_im", "ssm_log_dt", "ssm_b_re", "ssm_b_im", "ssm_c_re", "ssm_c_im", "ssm_d",
         "ln1_g", "ln1_b", "b_up", "b_down", "ln2_g", "ln2_b")
def _pad_flat(a, mult=1024):
    a = a.reshape(-1)
    return jnp.pad(a, (0, (-a.shape[0]) % mult))


def _perm(a):
    S, W = a.shape
    return a.reshape(N_SEG, S // N_SEG, W).transpose(1, 0, 2).reshape(S, W)


def _unperm(a):
    S, W = a.shape
    return a.reshape(S // N_SEG, N_SEG, W).transpose(1, 0, 2).reshape(S, W)


def _state_rows(a):
    return a.transpose(2, 0, 1).reshape(SSM_GROUP, N_STATES)


def _block_diag_b(bt):
    b4 = bt.reshape(SSM_GROUP, 4, 8, SSM_STATE)
    eye = jnp.eye(8, dtype=bt.dtype)
    return jnp.einsum("hcgp,gk->cghkp", b4, eye).reshape(4, 128, 512)


def _block_diag_c(c):
    c4 = c.reshape(4, 8, SSM_GROUP, SSM_STATE)
    eye = jnp.eye(8, dtype=c.dtype)
    return jnp.einsum("cghp,gk->cgpkh", c4, eye).reshape(4, 512, 128)


def _diag_b(acc):
    a = acc.reshape(4, 8, SSM_STATE, 8, SSM_GROUP)
    eye = jnp.eye(8, dtype=acc.dtype)
    return jnp.einsum("cgpkh,gk->hcgp", a, eye).reshape(SSM_GROUP, N_STATES)


def _diag_c(acc):
    a = acc.reshape(4, 8, SSM_GROUP, 8, SSM_STATE)
    eye = jnp.eye(8, dtype=acc.dtype)
    return jnp.einsum("cghkp,gk->cghp", a, eye).reshape(SSM_GROUPS, SSM_GROUP, SSM_STATE)


def kernel(x, mem, w_in, b_gate, conv_w, w_conv_out, ssm_lam_re, ssm_lam_im, ssm_log_dt, ssm_b_re, ssm_b_im, ssm_c_re, ssm_c_im, ssm_d, w_glu, w_kv, w_xattn_out, w_out, ln1_g, ln1_b, w_up, b_up, w_down, b_down, ln2_g, ln2_b, loss_target, m_w_in, m_b_gate, m_conv_w, m_w_conv_out, m_ssm_lam_re, m_ssm_lam_im, m_ssm_log_dt, m_ssm_b_re, m_ssm_b_im, m_ssm_c_re, m_ssm_c_im, m_ssm_d, m_w_glu, m_w_kv, m_w_xattn_out, m_w_out, m_ln1_g, m_ln1_b, m_w_up, m_b_up, m_w_down, m_b_down, m_ln2_g, m_ln2_b, v_w_in, v_b_gate, v_conv_w, v_w_conv_out, v_ssm_lam_re, v_ssm_lam_im, v_ssm_log_dt, v_ssm_b_re, v_ssm_b_im, v_ssm_c_re, v_ssm_c_im, v_ssm_d, v_w_glu, v_w_kv, v_w_xattn_out, v_w_out, v_ln1_g, v_ln1_b, v_w_up, v_b_up, v_w_down, v_b_down, v_ln2_g, v_ln2_b):
    W = dict(w_in=w_in, b_gate=b_gate, conv_w=conv_w, w_conv_out=w_conv_out, ssm_lam_re=ssm_lam_re,
             ssm_lam_im=ssm_lam_im, ssm_log_dt=ssm_log_dt, ssm_b_re=ssm_b_re, ssm_b_im=ssm_b_im, ssm_c_re=ssm_c_re,
             ssm_c_im=ssm_c_im, ssm_d=ssm_d, w_glu=w_glu, w_kv=w_kv, w_xattn_out=w_xattn_out, w_out=w_out,
             ln1_g=ln1_g, ln1_b=ln1_b, w_up=w_up, b_up=b_up, w_down=w_down, b_down=b_down, ln2_g=ln2_g, ln2_b=ln2_b)
    MOM = dict(w_in=m_w_in, b_gate=m_b_gate, conv_w=m_conv_w, w_conv_out=m_w_conv_out, ssm_lam_re=m_ssm_lam_re,
               ssm_lam_im=m_ssm_lam_im, ssm_log_dt=m_ssm_log_dt, ssm_b_re=m_ssm_b_re, ssm_b_im=m_ssm_b_im,
               ssm_c_re=m_ssm_c_re, ssm_c_im=m_ssm_c_im, ssm_d=m_ssm_d, w_glu=m_w_glu, w_kv=m_w_kv,
               w_xattn_out=m_w_xattn_out, w_out=m_w_out, ln1_g=m_ln1_g, ln1_b=m_ln1_b, w_up=m_w_up, b_up=m_b_up,
               w_down=m_w_down, b_down=m_b_down, ln2_g=m_ln2_g, ln2_b=m_ln2_b)
    VEL = dict(w_in=v_w_in, b_gate=v_b_gate, conv_w=v_conv_w, w_conv_out=v_w_conv_out, ssm_lam_re=v_ssm_lam_re,
               ssm_lam_im=v_ssm_lam_im, ssm_log_dt=v_ssm_log_dt, ssm_b_re=v_ssm_b_re, ssm_b_im=v_ssm_b_im,
               ssm_c_re=v_ssm_c_re, ssm_c_im=v_ssm_c_im, ssm_d=v_ssm_d, w_glu=v_w_glu, w_kv=v_w_kv,
               w_xattn_out=v_w_xattn_out, w_out=v_w_out, ln1_g=v_ln1_g, ln1_b=v_ln1_b, w_up=v_w_up, b_up=v_b_up,
               w_down=v_w_down, b_down=v_b_down, ln2_g=v_ln2_g, ln2_b=v_ln2_b)
    names = list(W)
    xy = 2 * lax.axis_index("x") + lax.axis_index("y")

    xs = x[0]
    S = xs.shape[0]
    mems = mem[0]
    tgt = loss_target[0]

    shard_bf = {n: W[n][0].astype(bf16) for n in BIG}
    win3 = gather_two_level(shard_bf["w_in"], "gather_w_in")

    lr = ssm_lam_re.reshape(1, N_STATES)
    li = ssm_lam_im.reshape(1, N_STATES)
    ldt = jnp.repeat(ssm_log_dt.reshape(SSM_GROUPS), SSM_STATE).reshape(1, N_STATES)
    bt_re, bt_im = _state_rows(ssm_b_re[0]), _state_rows(ssm_b_im[0])
    ar, ai, kr, ki, bbt_r, bbt_i = ssm_prep(lr, li, ldt, bt_re, bt_im)
    bd_r, bd_i = _block_diag_b(bbt_r).astype(bf16), _block_diag_b(bbt_i).astype(bf16)
    cd_r, cd_i = _block_diag_c(ssm_c_re[0]).astype(bf16), _block_diag_c(ssm_c_im[0]).astype(bf16)
    bdt_r, bdt_i = bd_r.transpose(0, 2, 1), bd_i.transpose(0, 2, 1)
    cdt_r, cdt_i = cd_r.transpose(0, 2, 1), cd_i.transpose(0, 2, 1)
    d_skip = ssm_d.reshape(1, SSM_WIDTH)

    x_bf = xs.astype(bf16)
    mem_bf = mems.astype(bf16)
    u_col = GATE_COLS + 3 * CONV_WIDTH
    proj, (wco3, wglu3, wxo3, wkv4, wout4, convw4) = mm_fwd(
        x_bf, win3, "proj", outs=(bf16,),
        rides=tuple(("gather", shard_bf[n]) for n in MID) + (("gather", conv_w[0]),))
    wkv3 = wkv4.reshape(1, D_MODEL, 2 * XATTN_WIDTH)
    wout3 = wout4.reshape(1, D_MODEL, D_MODEL)
    convw_full = convw4.transpose(1, 0, 2).reshape(3, CONV_WIDTH)
    a_conv = conv_fwd(proj, convw_full)
    y_a = mm_fwd_small(a_conv, wco3, "conv_out", out_dtype=bf16)
    u_perm = _perm(proj[:, u_col:u_col + SSM_WIDTH])
    half = D_MODEL // 2
    (init_r, init_i), (wup_a,) = ssm_scan_fwd(u_perm, bd_r, bd_i, ar, ai,
                                              rides=(("gather", shard_bf["w_up"][:half]),))
    (ysm_perm, ys_perm, st_r, st_i), (wup_b,) = ssm_scan_fwd(
        u_perm, bd_r, bd_i, ar, ai, final=(cd_r, cd_i, d_skip, init_r, init_i),
        rides=(("gather", shard_bf["w_up"][half:]),))
    wup3 = jnp.concatenate([wup_a, wup_b], axis=1)
    y_s = _unperm(ys_perm)
    glu = mm_fwd_small(y_s, wglu3, "glu", out_dtype=bf16)
    kv = mm_fwd_small(mem_bf, wkv3, "kv", out_dtype=bf16)
    o_att = attn_fwd(proj, kv)
    y_c = mm_fwd_small(o_att, wxo3, "xattn_out", out_dtype=bf16)
    merged = merge_fwd(proj, b_gate, y_a, glu, y_c)
    t1 = mm_fwd_small(merged, wout3, "w_out")
    h1, h1_bf = ln1_fwd(xs, t1, ln1_g, ln1_b)
    (r_up, hdn), (wdn4,) = mm_fwd(h1_bf, wup3, "w_up", bias=b_up, outs=(bf16, bf16), second=_relu2,
                                  first=lambda v: jnp.maximum(v, 0.0), rides=(("gather", shard_bf["w_down"]),))
    wdn3 = wdn4.reshape(1, D_FF, D_MODEL)
    t2 = mm_fwd(hdn, wdn3, "w_down", tn=512)

    dr2, dr2_bf, loss_part, d_ln2_g, d_ln2_b, d_b_down = ln2_loss_bwd(h1, t2, b_down, ln2_g, ln2_b, tgt)
    part, other = {}, {}
    g_w_down = mm_bwd_w(hdn, dr2_bf, 1, "dw_down", tk=1024).reshape(4, -1, D_MODEL)
    (dup, d_b_up), (recv_dn,) = mm_bwd_x(dr2_bf, wdn3, "dup", tm=512, extras=(r_up,), colsum=True, out_dtype=bf16,
                                         epi=lambda acc, r: acc * (2.0 * r.astype(f32)),
                                         rides=(("scatter", g_w_down),))
    part["w_down"] = sum_slots(recv_dn, "sum_w_down")
    g_w_up = mm_bwd_w(h1_bf, dup, 4, "dw_up")
    t3 = mm_bwd_x(dup, wup3, "dh1")
    dr1, dr1_bf, d_ln1_g, d_ln1_b = ln1_bwd(xs, t1, dr2, t3, ln1_g)
    g_w_out = mm_bwd_w(merged, dr1_bf, 1, "dw_out").reshape(4, -1, D_MODEL)
    dmerged, (recv_out,) = mm_bwd_x(dr1_bf, wout3, "dmerged", tm=1024, out_dtype=bf16,
                                    rides=(("scatter", g_w_out),))
    part["w_out"] = sum_slots(recv_out, "sum_w_out")
    (dproj, dy_a, dglu, dy_c, d_b_gate), (recv_up,) = merge_bwd(dmerged, proj, b_gate, y_a, glu, y_c,
                                                                rides=(("scatter", g_w_up),))
    part["w_up"] = sum_slots(recv_up, "sum_w_up")

    g_w_co = mm_bwd_w_small(a_conv, dy_a, 4, "dw_conv_out")
    da_conv = mm_bwd_x(dy_a, wco3, "da_conv", tm=1024)
    dproj, d_conv_w = conv_bwd(da_conv, proj, convw_full, dproj)

    g_w_glu = mm_bwd_w_small(y_s, dglu, 4, "dw_glu")
    dys_perm = _perm(mm_bwd_x(dglu, wglu3, "dy_s", tm=1024))
    linit_r, linit_i = ssm_scan_bwd(dys_perm, ysm_perm, cdt_r, cdt_i, ar, ai)
    (du_perm, dbacc_r, dbacc_i, dcacc_r, dcacc_i, da_r, da_i, d_ssm_d), (recv_co, recv_glu) = ssm_scan_bwd(
        dys_perm, ysm_perm, cdt_r, cdt_i, ar, ai,
        final=(u_perm, st_r, st_i, bdt_r, bdt_i, d_skip, linit_r, linit_i),
        rides=(("scatter", g_w_co), ("scatter", g_w_glu)))
    part["w_conv_out"] = sum_slots(recv_co, "sum_w_conv_out")
    part["w_glu"] = sum_slots(recv_glu, "sum_w_glu")
    dbt_re, dbt_im, d_lr, d_li, d_ldt_state = ssm_param_bwd(
        _diag_b(dbacc_r), _diag_b(dbacc_i), bt_re, bt_im, kr, ki, ar, ai, lr, li, ldt, da_r, da_i)
    d_log_dt = group_sum(d_ldt_state.reshape(SSM_GROUPS, SSM_STATE))
    d_b_re = dbt_re.reshape(SSM_GROUP, SSM_GROUPS, SSM_STATE).transpose(1, 2, 0)
    d_b_im = dbt_im.reshape(SSM_GROUP, SSM_GROUPS, SSM_STATE).transpose(1, 2, 0)
    d_c_re = _diag_c(dcacc_r)
    d_c_im = -_diag_c(dcacc_i)

    g_w_xo = mm_bwd_w_small(o_att, dy_c, 4, "dw_xattn_out")
    do_att = mm_bwd_x(dy_c, wxo3, "do_att", tm=1024, out_dtype=bf16)
    dproj, dkv = attn_bwd(do_att, proj, kv, dproj)
    g_w_kv = mm_bwd_w(mem_bf, dkv, 1, "dw_kv").reshape(4, -1, D_MODEL)

    dproj = lax.dynamic_update_slice(dproj, _unperm(du_perm), (0, u_col))
    g_small = {"b_gate": d_b_gate, "ssm_lam_re": d_lr, "ssm_lam_im": d_li, "ssm_log_dt": d_log_dt, "ssm_b_re": d_b_re,
               "ssm_b_im": d_b_im, "ssm_c_re": d_c_re, "ssm_c_im": d_c_im, "ssm_d": d_ssm_d, "ln1_g": d_ln1_g,
               "ln1_b": d_ln1_b, "b_up": d_b_up, "b_down": d_b_down, "ln2_g": d_ln2_g, "ln2_b": d_ln2_b}
    small_names = SMALL + ("conv_w",)
    g_small["conv_w"] = d_conv_w
    sizes = {n: (g_small[n].size + 1023) // 1024 * 1024 for n in small_names}
    pack = lambda d: jnp.concatenate([_pad_flat(d[n]) for n in small_names]).reshape(-1, 128)
    early = ("w_down", "w_up", "w_out", "w_conv_out", "w_glu")
    g_w_in, landed = mm_bwd_w(
        x_bf, dproj, 4, "dw_in", rides=(("scatter", g_w_xo), ("scatter", g_w_kv), ("all", pack(g_small)))
        + tuple(("pair", part[n]) for n in early))
    recv_xo, recv_kv, srecv = landed[:3]
    other.update(zip(early, landed[3:]))
    part["w_xattn_out"] = sum_slots(recv_xo, "sum_w_xattn_out")
    part["w_kv"] = sum_slots(recv_kv, "sum_w_kv")
    dx, (recv_in, other["w_xattn_out"], other["w_kv"]) = mm_bwd_x(
        dproj, win3, "dx", extras=(dr1,), epi=lambda acc, d: acc + ALPHA * d,
        rides=(("scatter", g_w_in), ("pair", part["w_xattn_out"]), ("pair", part["w_kv"])))
    part["w_in"] = sum_slots(recv_in, "sum_w_in")
    other["w_in"], = exchange(("pair", part["w_in"]), name="swap_w_in")
    res = [{}, {}, {}, {}]
    for n in BIG:
        for k, r in enumerate(adam_pair(part[n], other[n], W[n][0], MOM[n][0], VEL[n][0], "adam_" + n)):
            res[k][n] = r[None]
    conv_zero = jnp.zeros((3, CONV_WIDTH), f32)
    gs, ds_, ms, vs = adam_slots(srecv, pack({**{n: W[n] for n in SMALL}, "conv_w": conv_zero}),
                                 pack({**{n: MOM[n] for n in SMALL}, "conv_w": conv_zero}),
                                 pack({**{n: VEL[n] for n in SMALL}, "conv_w": conv_zero}), "adam_small")

    def unpack_small(buf):
        flat = buf.reshape(-1)
        out, r = {}, 0
        for n in small_names:
            ref = g_small[n] if n == "conv_w" else W[n]
            out[n] = flat[r:r + ref.size].reshape(ref.shape)
            r += sizes[n]
        return out

    res_s = [unpack_small(b) for b in (gs, ds_, ms, vs)]
    g_conv = lax.dynamic_slice(res_s[0]["conv_w"], (0, xy * 128), (3, 128))
    conv_slots = g_conv.reshape(1, 3, 128)
    cg, cd, cm, cv = adam_slots(conv_slots, conv_w[0], m_conv_w[0], v_conv_w[0], "adam_conv")
    conv_res = [cg, cd, cm, cv]

    loss = lax.psum(loss_part[0, 0], ("x", "y", "c"))
    outs = [loss, dx.reshape(x.shape)]
    for k in range(4):
        for n in names:
            if n == "conv_w":
                outs.append(conv_res[k].reshape(conv_w.shape))
            elif n in BIG:
                outs.append(res[k][n])
            else:
                outs.append(res_s[k][n])
    return tuple(outs)
```

```python
import functools
import math

import jax
import jax.numpy as jnp
from jax import lax
from jax.experimental import pallas as pl
from jax.experimental.pallas import tpu as pltpu

f32 = jnp.float32
bf16 = jnp.bfloat16

D_MODEL = 1024
CONV_WIDTH = 512
SSM_WIDTH = 512
SSM_GROUP = 16
SSM_GROUPS = 32
SSM_STATE = 64
N_STATES = SSM_GROUPS * SSM_STATE
XATTN_HEADS = 4
XATTN_HEAD_DIM = 128
XATTN_WIDTH = 512
D_FF = 4096
GATE_COLS = 3 * D_MODEL
IN_COLS = GATE_COLS + 3 * CONV_WIDTH + SSM_WIDTH + XATTN_WIDTH
ALPHA = 2.0 ** 0.25
LN_EPS = 1e-5
ADAM_LR = 0.001
ADAM_B1 = 0.9
ADAM_B2 = 0.999
ADAM_EPS = 1e-08
ADAM_WD = 0.01
ADAM_STEP = 10

N_SEG = 8
SCAN_ROWS = 256
LANE_STRIP = 512
TM = 512
VMEM_LIMIT = 48 * 1024 * 1024
MESH = pl.DeviceIdType.MESH

NT_DIMS = (((1,), (1,)), ((), ()))
TN_DIMS = (((0,), (0,)), ((), ()))


def _params():
    return pltpu.CompilerParams(vmem_limit_bytes=VMEM_LIMIT)


def _full(shape):
    n = len(shape)
    return pl.BlockSpec(shape, lambda *_: (0,) * n)


def _rows(tm, w, cb=0):
    return pl.BlockSpec((tm, w), lambda i: (i, cb))


def _sig(x):
    return 1.0 / (1.0 + jnp.exp(-x))


HBM_SPEC = pl.BlockSpec(memory_space=pl.ANY)
RIDE_PEERS = {"gather": 3, "scatter": 3, "all": 7, "pair": 1}


def _xy_peers(x, y):
    return [(1 - x, y), (x, 1 - y), (1 - x, 1 - y)]


def _ride_copies(kind, src, dst, send_sems, recv_sems, local_sem):
    x, y, c = lax.axis_index("x"), lax.axis_index("y"), lax.axis_index("c")
    me = 2 * x + y
    if kind == "all":
        flips = [(fx, fy, fc) for fx in (0, 1) for fy in (0, 1) for fc in (0, 1)][1:]
        peers = [(x ^ fx, y ^ fy, c ^ fc) for fx, fy, fc in flips]
        slot = lambda p: 4 * p[0] + 2 * p[1] + p[2]
        mine = slot((x, y, c))
    else:
        peers = [(px, py, c) for px, py in _xy_peers(x, y)]
        slot = lambda p: 2 * p[0] + p[1]
        mine = me

    def remote(k, s, d):
        return pltpu.make_async_remote_copy(src_ref=s, dst_ref=d, send_sem=send_sems.at[k], recv_sem=recv_sems.at[k],
                                            device_id=peers[k], device_id_type=MESH)

    if kind == "pair":
        peers = [(x, y, 1 - c)]
        return None, [remote(0, src, dst)], [remote(0, src, dst)]
    if kind == "scatter":
        local = pltpu.make_async_copy(src.at[me], dst.at[me], local_sem)
        sends = [remote(k, src.at[slot(p)], dst.at[me]) for k, p in enumerate(peers)]
        lands = [remote(k, src.at[me], dst.at[slot(p)]) for k, p in enumerate(peers)]
    else:
        local = pltpu.make_async_copy(src, dst.at[mine], local_sem)
        sends = [remote(k, src, dst.at[mine]) for k, p in enumerate(peers)]
        lands = [remote(k, src, dst.at[slot(p)]) for k, p in enumerate(peers)]
    return local, sends, lands


def _ride_shape(kind, src):
    lead = {"gather": (4,), "scatter": (), "all": (8,), "pair": ()}[kind]
    return jax.ShapeDtypeStruct(lead + src.shape, src.dtype)


def _pcall(body, args, *, name, grid, in_specs, out_specs, out_shape, scratch_shapes=(), rides=(), aliases=None):
    n_in, n_out, n_scr, nr = len(in_specs), len(out_specs), len(scratch_shapes), len(rides)
    kinds = [k for k, _ in rides]

    def wrapped(*refs):
        ins, rsrc = refs[:n_in], refs[n_in:n_in + nr]
        outs = refs[n_in + nr:n_in + nr + n_out]
        rdst = refs[n_in + nr + n_out:n_in + 2 * nr + n_out]
        scr = refs[n_in + 2 * nr + n_out:n_in + 2 * nr + n_out + n_scr]
        sems = refs[n_in + 2 * nr + n_out + n_scr:]
        def start():
            for r, kind in enumerate(kinds):
                local, sends, _ = _ride_copies(kind, rsrc[r], rdst[r], *sems[3 * r:3 * r + 3])
                if local is not None:
                    local.start()
                for cp in sends:
                    cp.start()

        def finish():
            for r, kind in enumerate(kinds):
                local, sends, lands = _ride_copies(kind, rsrc[r], rdst[r], *sems[3 * r:3 * r + 3])
                for cp in lands:
                    cp.wait_recv()
                for cp in sends:
                    cp.wait_send()
                if local is not None:
                    local.wait()

        if nr and grid:
            ids = [pl.program_id(a) for a in range(len(grid))]
            first = functools.reduce(jnp.logical_and, [i == 0 for i in ids])
            last = functools.reduce(jnp.logical_and, [i == g - 1 for i, g in zip(ids, grid)])
            pl.when(first)(start)
        elif nr:
            start()
        body(*ins, *outs, *scr)
        if nr and grid:
            pl.when(last)(finish)
        elif nr:
            finish()

    sems = []
    for kind in kinds:
        n = RIDE_PEERS[kind]
        sems += [pltpu.SemaphoreType.DMA((n,)), pltpu.SemaphoreType.DMA((n,)), pltpu.SemaphoreType.DMA(())]
    res = pl.pallas_call(
        wrapped, name=name, grid=grid, in_specs=list(in_specs) + [HBM_SPEC] * nr,
        out_specs=list(out_specs) + [HBM_SPEC] * nr,
        out_shape=list(out_shape) + [_ride_shape(k, s) for k, s in rides],
        scratch_shapes=list(scratch_shapes) + sems, input_output_aliases=aliases or {},
        compiler_params=_params())(*args, *[s for _, s in rides])
    return list(res[:n_out]), list(res[n_out:])


def _bf(v):
    return v if v.dtype == bf16 else v.astype(bf16)


def mm_fwd_small(a, w3, name, *, tm=1024, out_dtype=f32):
    M, K = a.shape
    J, _, n = w3.shape
    tm = min(tm, M)

    def body(a_ref, w_ref, o_ref):
        av = _bf(a_ref[...])
        for j in range(J):
            o_ref[:, j * n:(j + 1) * n] = jnp.dot(av, w_ref[j], preferred_element_type=f32).astype(out_dtype)

    return pl.pallas_call(
        body, name=name, grid=(M // tm,), in_specs=[_rows(tm, K), _full((J, K, n))], out_specs=_rows(tm, J * n),
        out_shape=jax.ShapeDtypeStruct((M, J * n), out_dtype), compiler_params=_params())(a, w3)


def mm_bwd_w_small(a, dy, J, name, *, tm=2048):
    M, K = a.shape
    n = dy.shape[1] // J
    tm = min(tm, M)
    ns = M // tm

    def body(a_ref, dy_ref, o_ref, acc_ref):
        s = pl.program_id(0)
        part = lax.dot_general(_bf(a_ref[...]), _bf(dy_ref[...]), TN_DIMS, preferred_element_type=f32)

        @pl.when(s == 0)
        def _():
            acc_ref[...] = part

        @pl.when(s > 0)
        def _():
            acc_ref[...] += part

        @pl.when(s == ns - 1)
        def _():
            for j in range(J):
                o_ref[j] = acc_ref[:, j * n:(j + 1) * n].astype(bf16)

    return pl.pallas_call(
        body, name=name, grid=(ns,), in_specs=[_rows(tm, K), _rows(tm, J * n)], out_specs=_full((J, K, n)),
        out_shape=jax.ShapeDtypeStruct((J, K, n), bf16), scratch_shapes=[pltpu.VMEM((K, J * n), f32)],
        compiler_params=_params())(a, dy)


def mm_fwd(a, w3, name, *, tm=1024, tn=None, bias=None, outs=(f32,), first=None, second=None, rides=()):
    M, K = a.shape
    J, _, n = w3.shape
    tm = min(tm, M)
    tn = tn or n
    nl = n // tn
    nb = 0 if bias is None else 1

    def body(*refs):
        a_ref, w_ref = refs[0], refs[1]
        acc = jnp.dot(_bf(a_ref[...]), w_ref[0], preferred_element_type=f32)
        if bias is not None:
            acc = acc + refs[2][...]
        refs[2 + nb][...] = (acc if first is None else first(acc)).astype(outs[0])
        if len(outs) > 1:
            refs[3 + nb][...] = second(acc).astype(outs[1])

    in_specs = [pl.BlockSpec((tm, K), lambda j, l, i: (i, 0)),
                pl.BlockSpec((1, K, tn), lambda j, l, i: (j, 0, l))]
    args = [a, w3]
    if bias is not None:
        in_specs.append(pl.BlockSpec((1, tn), lambda j, l, i: (0, j * nl + l)))
        args.append(bias)
    res, landed = _pcall(
        body, args, name=name, grid=(J, nl, M // tm), in_specs=in_specs,
        out_specs=[pl.BlockSpec((tm, tn), lambda j, l, i: (i, j * nl + l)) for _ in outs],
        out_shape=[jax.ShapeDtypeStruct((M, J * n), dt) for dt in outs], rides=rides)
    res = res if len(outs) > 1 else res[0]
    return (res, landed) if rides else res


def mm_bwd_x(dy, w3, name, *, tm=TM, epi=None, extras=(), colsum=False, out_dtype=f32, rides=()):
    M = dy.shape[0]
    J, K, n = w3.shape
    tm = min(tm, M)
    nex = len(extras)

    def body(*refs):
        dy_ref, w_hbm = refs[0], refs[1]
        ex = refs[2:2 + nex]
        o_ref = refs[2 + nex]
        w_ref = refs[-1]
        i = pl.program_id(0)

        @pl.when(i == 0)
        def _():
            pltpu.sync_copy(w_hbm, w_ref)

        acc = None
        for j in range(J):
            part = lax.dot_general(_bf(dy_ref[:, j * n:(j + 1) * n]), w_ref[j], NT_DIMS, preferred_element_type=f32)
            acc = part if acc is None else acc + part
        if epi is not None:
            acc = epi(acc, *[e[...] for e in ex])
        o_ref[...] = acc.astype(out_dtype)
        if colsum:
            s_ref = refs[3 + nex]
            cs = jnp.sum(acc, axis=0, keepdims=True)

            @pl.when(i == 0)
            def _():
                s_ref[...] = cs

            @pl.when(i > 0)
            def _():
                s_ref[...] += cs

    in_specs = [pl.BlockSpec((tm, J * n), lambda i: (i, 0)), pl.BlockSpec(memory_space=pl.ANY)]
    in_specs += [pl.BlockSpec((tm, K), lambda i: (i, 0)) for _ in extras]
    out_specs = [pl.BlockSpec((tm, K), lambda i: (i, 0))]
    out_shape = [jax.ShapeDtypeStruct((M, K), out_dtype)]
    if colsum:
        out_specs.append(pl.BlockSpec((1, K), lambda i: (0, 0)))
        out_shape.append(jax.ShapeDtypeStruct((1, K), f32))
    res, landed = _pcall(body, [dy, w3, *extras], name=name, grid=(M // tm,), in_specs=in_specs, out_specs=out_specs,
                         out_shape=out_shape, scratch_shapes=[pltpu.VMEM((J, K, n), bf16)], rides=rides)
    res = res if colsum else res[0]
    return (res, landed) if rides else res


def mm_rows_fused(a, w3, name, *, transposed, tm, epi, extras=(), consts=(), outs=(f32,), nsums=0, rides=()):
    M = a.shape[0]
    J, K, n = w3.shape
    W = K if transposed else n
    tm = min(tm, M)
    nex, nco, nout = len(extras), len(consts), len(outs)

    def body(*refs):
        a_ref, w_hbm = refs[0], refs[1]
        ex = refs[2:2 + nex]
        co = refs[2 + nex:2 + nex + nco]
        o_refs = refs[2 + nex + nco:2 + nex + nco + nout]
        s_refs = refs[2 + nex + nco + nout:2 + nex + nco + nout + nsums]
        w_ref = refs[-1]
        i = pl.program_id(0)

        @pl.when(i == 0)
        def _():
            pltpu.sync_copy(w_hbm, w_ref)

        if transposed:
            acc = None
            for j in range(J):
                part = lax.dot_general(_bf(a_ref[:, j * n:(j + 1) * n]), w_ref[j], NT_DIMS,
                                       preferred_element_type=f32)
                acc = part if acc is None else acc + part
        else:
            acc = jnp.dot(_bf(a_ref[...]), w_ref[0], preferred_element_type=f32)
        rows, sums = epi(acc, *[e[...] for e in ex], *[c[...] for c in co])
        for o_ref, v, dt in zip(o_refs, rows, outs):
            o_ref[...] = v.astype(dt)
        for s_ref, v in zip(s_refs, sums):
            @pl.when(i == 0)
            def _(s_ref=s_ref, v=v):
                s_ref[...] = v

            @pl.when(i > 0)
            def _(s_ref=s_ref, v=v):
                s_ref[...] += v

    in_specs = [_rows(tm, a.shape[1]), pl.BlockSpec(memory_space=pl.ANY)]
    in_specs += [_rows(tm, W) for _ in extras] + [_full((1, W)) for _ in consts]
    out_specs = [_rows(tm, W) for _ in outs] + [_full((1, W)) for _ in range(nsums)]
    out_shape = [jax.ShapeDtypeStruct((M, W), dt) for dt in outs] + [jax.ShapeDtypeStruct((1, W), f32)] * nsums
    res, landed = _pcall(body, [a, w3, *extras, *consts], name=name, grid=(M // tm,), in_specs=in_specs,
                         out_specs=out_specs, out_shape=out_shape, scratch_shapes=[pltpu.VMEM((J, K, n), bf16)],
                         rides=rides)
    return (res, landed) if rides else res


def mm_bwd_w(a, dy, J, name, *, tm=2048, tn=None, tk=None, rides=()):
    M, K = a.shape
    n = dy.shape[1] // J
    tm = min(tm, M)
    tn = tn or n
    tk = tk or K
    nl = n // tn
    nk = K // tk
    ns = M // tm

    def body(a_ref, dy_ref, o_ref, acc_ref):
        s = pl.program_id(3)
        part = lax.dot_general(_bf(a_ref[...]), _bf(dy_ref[...]), TN_DIMS, preferred_element_type=f32)

        @pl.when(s == 0)
        def _():
            acc_ref[...] = part

        @pl.when(s > 0)
        def _():
            acc_ref[...] += part

        @pl.when(s == ns - 1)
        def _():
            o_ref[0] = acc_ref[...].astype(bf16)

    res, landed = _pcall(
        body, [a, dy], name=name, grid=(J, nl, nk, ns),
        in_specs=[pl.BlockSpec((tm, tk), lambda j, l, k, s: (s, k)),
                  pl.BlockSpec((tm, tn), lambda j, l, k, s: (s, j * nl + l))],
        out_specs=[pl.BlockSpec((1, tk, tn), lambda j, l, k, s: (j, k, l))],
        out_shape=[jax.ShapeDtypeStruct((J, K, n), bf16)],
        scratch_shapes=[pltpu.VMEM((tk, tn), f32)], rides=rides)
    return (res[0], landed) if rides else res[0]


def _relu2(v):
    r = jnp.maximum(v, 0.0)
    return r * r


HALO = 16


def _shift_down(z, k, halo):
    r = lax.broadcasted_iota(jnp.int32, z.shape, 0)
    y = pltpu.roll(z, k, 0)
    for q in range(k):
        y = jnp.where(r == q, halo[HALO - k + q:HALO - k + q + 1, :], y)
    return y


def _shift_up(z, k, halo):
    tm = z.shape[0]
    r = lax.broadcasted_iota(jnp.int32, z.shape, 0)
    y = pltpu.roll(z, tm - k, 0)
    for q in range(k):
        y = jnp.where(r == tm - k + q, halo[q:q + 1, :], y)
    return y


def _prev_halo(tm, cb):
    return pl.BlockSpec((HALO, CONV_WIDTH), lambda i: (jnp.maximum(i * (tm // HALO) - 1, 0), cb))


def _next_halo(tm, cb, nblk):
    return pl.BlockSpec((HALO, CONV_WIDTH), lambda i: (jnp.minimum((i + 1) * (tm // HALO), nblk - 1), cb))


def _f32(ref):
    return ref[...].astype(f32)


def conv_fwd(proj, conv_w, tm=TM):
    S = proj.shape[0]

    def body(cb_ref, cc_ref, ch_ref, cch_ref, chh_ref, w_ref, a_ref):
        i = pl.program_id(0)
        z = _f32(cc_ref) * _f32(ch_ref)
        zh = jnp.where(i == 0, 0.0, _f32(cch_ref) * _f32(chh_ref))
        w = w_ref[...]
        dwz = w[0:1, :] * _shift_down(z, 2, zh) + w[1:2, :] * _shift_down(z, 1, zh) + w[2:3, :] * z
        a_ref[...] = (_f32(cb_ref) * dwz).astype(bf16)

    return pl.pallas_call(
        body, name="conv_fwd", grid=(S // tm,),
        in_specs=[_rows(tm, CONV_WIDTH, 6), _rows(tm, CONV_WIDTH, 7), _rows(tm, CONV_WIDTH, 8),
                  _prev_halo(tm, 7), _prev_halo(tm, 8), _full((3, CONV_WIDTH))],
        out_specs=_rows(tm, CONV_WIDTH),
        out_shape=jax.ShapeDtypeStruct((S, CONV_WIDTH), bf16), compiler_params=_params())(
            proj, proj, proj, proj, proj, conv_w)


def conv_bwd(da, proj, conv_w, dproj, tm=TM):
    S = proj.shape[0]
    nt = S // tm

    def body(da_ref, cb_ref, cc_ref, ch_ref, cch_ref, chh_ref, dan_ref, cbn_ref, w_ref, _, o_ref, dw_ref):
        i = pl.program_id(0)
        cc, ch, cb, dav = _f32(cc_ref), _f32(ch_ref), _f32(cb_ref), da_ref[...]
        z = cc * ch
        zh = jnp.where(i == 0, 0.0, _f32(cch_ref) * _f32(chh_ref))
        w = w_ref[...]
        z1 = _shift_down(z, 1, zh)
        z2 = _shift_down(z, 2, zh)
        dwz = w[0:1, :] * z2 + w[1:2, :] * z1 + w[2:3, :] * z
        g = dav * cb
        gn = jnp.where(i == nt - 1, 0.0, dan_ref[...] * _f32(cbn_ref))
        dz = w[2:3, :] * g + w[1:2, :] * _shift_up(g, 1, gn) + w[0:1, :] * _shift_up(g, 2, gn)
        o_ref[:, 0:CONV_WIDTH] = (dav * dwz).astype(bf16)
        o_ref[:, CONV_WIDTH:2 * CONV_WIDTH] = (dz * ch).astype(bf16)
        o_ref[:, 2 * CONV_WIDTH:3 * CONV_WIDTH] = (dz * cc).astype(bf16)

        @pl.when(i == 0)
        def _():
            dw_ref[...] = jnp.zeros_like(dw_ref)

        dw_ref[0:1, :] += jnp.sum(g * z2, axis=0, keepdims=True)
        dw_ref[1:2, :] += jnp.sum(g * z1, axis=0, keepdims=True)
        dw_ref[2:3, :] += jnp.sum(g * z, axis=0, keepdims=True)

    return pl.pallas_call(
        body, name="conv_bwd", grid=(nt,),
        in_specs=[_rows(tm, CONV_WIDTH), _rows(tm, CONV_WIDTH, 6), _rows(tm, CONV_WIDTH, 7), _rows(tm, CONV_WIDTH, 8),
                  _prev_halo(tm, 7), _prev_halo(tm, 8),
                  _next_halo(tm, 0, S // HALO), _next_halo(tm, 6, S // HALO), _full((3, CONV_WIDTH)),
                  pl.BlockSpec(memory_space=pl.ANY)],
        out_specs=[_rows(tm, 3 * CONV_WIDTH, GATE_COLS // (3 * CONV_WIDTH)), _full((3, CONV_WIDTH))],
        out_shape=[jax.ShapeDtypeStruct(dproj.shape, bf16), jax.ShapeDtypeStruct((3, CONV_WIDTH), f32)],
        input_output_aliases={9: 0},
        compiler_params=_params())(da, proj, proj, proj, proj, proj, da, proj, conv_w, dproj)


def _cmul(ar, ai, br, bi):
    return ar * br - ai * bi, ar * bi + ai * br


def ssm_prep(lr, li, ldt, bt_re, bt_im):
    def body(lr_ref, li_ref, ldt_ref, br_ref, bi_ref, ar_ref, ai_ref, kr_ref, ki_ref, bbr_ref, bbi_ref):
        lrv, liv = lr_ref[...], li_ref[...]
        dt = jnp.exp(ldt_ref[...])
        mag = jnp.exp(lrv * dt)
        ar = mag * jnp.cos(liv * dt)
        ai = mag * jnp.sin(liv * dt)
        den = lrv * lrv + liv * liv
        nr = ar - 1.0
        kr = (nr * lrv + ai * liv) / den
        ki = (ai * lrv - nr * liv) / den
        ar_ref[...], ai_ref[...], kr_ref[...], ki_ref[...] = ar, ai, kr, ki
        bbr_ref[...] = kr * br_ref[...] - ki * bi_ref[...]
        bbi_ref[...] = kr * bi_ref[...] + ki * br_ref[...]

    v = jax.ShapeDtypeStruct((1, N_STATES), f32)
    m = jax.ShapeDtypeStruct((SSM_GROUP, N_STATES), f32)
    return pl.pallas_call(body, name="ssm_prep", out_shape=[v, v, v, v, m, m])(lr, li, ldt, bt_re, bt_im)


def _pow_segment(ar, ai, seg_len):
    pr, pi = ar, ai
    for _ in range(int(math.log2(seg_len))):
        pr, pi = _cmul(pr, pi, pr, pi)
    return pr, pi


GELU_K = math.sqrt(2.0 / math.pi)
GELU_C = 0.044715


def _gelu(v):
    return 0.5 * v * (1.0 + jnp.tanh(GELU_K * (v + GELU_C * v * v * v)))


def _gelu_grad(v):
    t = jnp.tanh(GELU_K * (v + GELU_C * v * v * v))
    return 0.5 * (1.0 + t) + 0.5 * v * (1.0 - t * t) * GELU_K * (1.0 + 3.0 * GELU_C * v * v)


def ssm_scan_fwd(u_perm, bd_r, bd_i, ar, ai, *, final=None, rides=()):
    S = u_perm.shape[0]
    R = SCAN_ROWS
    nblk = S // R
    seg_len = S // N_SEG
    nstrip = N_STATES // LANE_STRIP
    store = final is not None

    def body(*refs):
        if store:
            (u_ref, bdr_ref, bdi_ref, ar_ref, ai_ref, cdr_ref, cdi_ref, d_ref, ir_ref, ii_ref,
             y_ref, ys_ref, sr_ref, si_ref, bur, bui, car, cai) = refs
        else:
            u_ref, bdr_ref, bdi_ref, ar_ref, ai_ref, or_ref, oi_ref, bur, bui, car, cai = refs
        i = pl.program_id(0)
        ub = _bf(u_ref[...])
        u = ub.astype(f32)
        for cb in range(4):
            us = ub[:, cb * 128:(cb + 1) * 128]
            bur[:, cb * 512:(cb + 1) * 512] = jnp.dot(us, bdr_ref[cb], preferred_element_type=f32)
            bui[:, cb * 512:(cb + 1) * 512] = jnp.dot(us, bdi_ref[cb], preferred_element_type=f32)

        @pl.when(i == 0)
        def _():
            if store:
                car[...] = ir_ref[...]
                cai[...] = ii_ref[...]
            else:
                car[...] = jnp.zeros_like(car)
                cai[...] = jnp.zeros_like(cai)

        for ls in range(nstrip):
            lanes = pl.ds(ls * LANE_STRIP, LANE_STRIP)
            a_r = jnp.broadcast_to(ar_ref[:, lanes], (N_SEG, LANE_STRIP))
            a_i = jnp.broadcast_to(ai_ref[:, lanes], (N_SEG, LANE_STRIP))

            def step(t, carry, lanes=lanes, a_r=a_r, a_i=a_i):
                s_r, s_i = carry
                row = pl.multiple_of(t * 8, 8)
                n_r = a_r * s_r - a_i * s_i + bur[pl.ds(row, 8), lanes]
                n_i = a_r * s_i + a_i * s_r + bui[pl.ds(row, 8), lanes]
                if store:
                    sr_ref[pl.ds(row, 8), lanes] = n_r
                    si_ref[pl.ds(row, 8), lanes] = n_i
                return n_r, n_i

            e_r, e_i = lax.fori_loop(0, R // 8, step, (car[:, lanes], cai[:, lanes]), unroll=True)
            car[:, lanes] = e_r
            cai[:, lanes] = e_i

        if store:
            for cb in range(4):
                st_r = sr_ref[:, cb * 512:(cb + 1) * 512].astype(bf16)
                st_i = si_ref[:, cb * 512:(cb + 1) * 512].astype(bf16)
                y = (jnp.dot(st_r, cdr_ref[cb], preferred_element_type=f32)
                     - jnp.dot(st_i, cdi_ref[cb], preferred_element_type=f32))
                cols = slice(cb * 128, (cb + 1) * 128)
                y = y + d_ref[:, cols] * u[:, cols]
                y_ref[:, cols] = y
                ys_ref[:, cols] = _gelu(y).astype(bf16)
        else:
            @pl.when(i == nblk - 1)
            def _():
                p_r, p_i = _pow_segment(ar_ref[...], ai_ref[...], seg_len)
                t_r, t_i = car[0:1, :], cai[0:1, :]
                or_ref[0:1, :] = jnp.zeros((1, N_STATES), f32)
                oi_ref[0:1, :] = jnp.zeros((1, N_STATES), f32)
                for j in range(1, N_SEG):
                    or_ref[j:j + 1, :] = t_r
                    oi_ref[j:j + 1, :] = t_i
                    m_r, m_i = _cmul(p_r, p_i, t_r, t_i)
                    t_r, t_i = car[j:j + 1, :] + m_r, cai[j:j + 1, :] + m_i

    blk = lambda w: pl.BlockSpec((R, w), lambda i: (i, 0))
    in_specs = [blk(SSM_WIDTH), _full((4, 128, 512)), _full((4, 128, 512)), _full((1, N_STATES)), _full((1, N_STATES))]
    args = [u_perm, bd_r, bd_i, ar, ai]
    scratch = [pltpu.VMEM((R, N_STATES), f32), pltpu.VMEM((R, N_STATES), f32),
               pltpu.VMEM((N_SEG, N_STATES), f32), pltpu.VMEM((N_SEG, N_STATES), f32)]
    if store:
        in_specs += [_full((4, 512, 128)), _full((4, 512, 128)), _full((1, SSM_WIDTH)),
                     _full((N_SEG, N_STATES)), _full((N_SEG, N_STATES))]
        args += list(final)
        out_specs = [blk(SSM_WIDTH), blk(SSM_WIDTH), blk(N_STATES), blk(N_STATES)]
        out_shape = [jax.ShapeDtypeStruct((S, SSM_WIDTH), f32), jax.ShapeDtypeStruct((S, SSM_WIDTH), bf16),
                     jax.ShapeDtypeStruct((S, N_STATES), f32), jax.ShapeDtypeStruct((S, N_STATES), f32)]
        name = "ssm_scan_fwd"
    else:
        out_specs = [_full((N_SEG, N_STATES)), _full((N_SEG, N_STATES))]
        out_shape = [jax.ShapeDtypeStruct((N_SEG, N_STATES), f32)] * 2
        name = "ssm_scan_fwd_carry"
    res, landed = _pcall(body, args, name=name, grid=(nblk,), in_specs=in_specs, out_specs=out_specs,
                         out_shape=out_shape, scratch_shapes=scratch, rides=rides)
    return (res, landed) if rides else res


def ssm_scan_bwd(dys_perm, y_perm, cdt_r, cdt_i, ar, ai, *, final=None, rides=()):
    S = dys_perm.shape[0]
    R = SCAN_ROWS
    nblk = S // R
    seg_len = S // N_SEG
    nstrip = N_STATES // LANE_STRIP
    store = final is not None

    def body(*refs):
        if store:
            (dys_ref, y_ref, cdr_ref, cdi_ref, ar_ref, ai_ref, u_ref, sr_ref, si_ref, bdr_ref, bdi_ref, d_ref, ir_ref,
             ii_ref, du_ref, dbr_ref, dbi_ref, dcr_ref, dci_ref, dar_ref, dai_ref, dd_ref, dsr, dsi, lmr, lmi, car,
             cai) = refs
        else:
            dys_ref, y_ref, cdr_ref, cdi_ref, ar_ref, ai_ref, or_ref, oi_ref, dsr, dsi, car, cai = refs
        i = pl.program_id(0)
        dy = dys_ref[...] * _gelu_grad(y_ref[...])
        dyb = dy.astype(bf16)
        for cb in range(4):
            ds_ = dyb[:, cb * 128:(cb + 1) * 128]
            dsr[:, cb * 512:(cb + 1) * 512] = jnp.dot(ds_, cdr_ref[cb], preferred_element_type=f32)
            dsi[:, cb * 512:(cb + 1) * 512] = -jnp.dot(ds_, cdi_ref[cb], preferred_element_type=f32)

        @pl.when(i == 0)
        def _():
            if store:
                car[...] = ir_ref[...]
                cai[...] = ii_ref[...]
                dar_ref[...] = jnp.zeros_like(dar_ref)
                dai_ref[...] = jnp.zeros_like(dai_ref)
                dbr_ref[...] = jnp.zeros_like(dbr_ref)
                dbi_ref[...] = jnp.zeros_like(dbi_ref)
                dcr_ref[...] = jnp.zeros_like(dcr_ref)
                dci_ref[...] = jnp.zeros_like(dci_ref)
                dd_ref[...] = jnp.zeros_like(dd_ref)
            else:
                car[...] = jnp.zeros_like(car)
                cai[...] = jnp.zeros_like(cai)

        for ls in range(nstrip):
            lanes = pl.ds(ls * LANE_STRIP, LANE_STRIP)
            a_r = jnp.broadcast_to(ar_ref[:, lanes], (N_SEG, LANE_STRIP))
            a_i = jnp.broadcast_to(ai_ref[:, lanes], (N_SEG, LANE_STRIP))
            zero = jnp.zeros((N_SEG, LANE_STRIP), f32)

            def step(k, carry, lanes=lanes, a_r=a_r, a_i=a_i):
                l_r, l_i, g_r, g_i = carry
                row = pl.multiple_of((R // 8 - 1 - k) * 8, 8)
                if store:
                    s_r = sr_ref[pl.ds(row, 8), lanes]
                    s_i = si_ref[pl.ds(row, 8), lanes]
                    g_r = g_r + l_r * s_r + l_i * s_i
                    g_i = g_i + l_i * s_r - l_r * s_i
                n_r = dsr[pl.ds(row, 8), lanes] + a_r * l_r + a_i * l_i
                n_i = dsi[pl.ds(row, 8), lanes] + a_r * l_i - a_i * l_r
                if store:
                    lmr[pl.ds(row, 8), lanes] = n_r
                    lmi[pl.ds(row, 8), lanes] = n_i
                return n_r, n_i, g_r, g_i

            e_r, e_i, g_r, g_i = lax.fori_loop(0, R // 8, step, (car[:, lanes], cai[:, lanes], zero, zero),
                                               unroll=True)
            car[:, lanes] = e_r
            cai[:, lanes] = e_i
            if store:
                dar_ref[:, lanes] += g_r
                dai_ref[:, lanes] += g_i

        if store:
            ub = _bf(u_ref[...])
            u = ub.astype(f32)
            for cb in range(4):
                cols = slice(cb * 128, (cb + 1) * 128)
                st = slice(cb * 512, (cb + 1) * 512)
                l_r = lmr[:, st].astype(bf16)
                l_i = lmi[:, st].astype(bf16)
                du = (jnp.dot(l_r, bdr_ref[cb], preferred_element_type=f32)
                      + jnp.dot(l_i, bdi_ref[cb], preferred_element_type=f32))
                du_ref[:, cols] = (du + d_ref[:, cols] * dy[:, cols]).astype(bf16)
                dbr_ref[cb] += lax.dot_general(l_r, ub[:, cols], TN_DIMS, preferred_element_type=f32)
                dbi_ref[cb] += lax.dot_general(l_i, ub[:, cols], TN_DIMS, preferred_element_type=f32)
                dcr_ref[cb] += lax.dot_general(dyb[:, cols], sr_ref[:, st].astype(bf16), TN_DIMS,
                                               preferred_element_type=f32)
                dci_ref[cb] += lax.dot_general(dyb[:, cols], si_ref[:, st].astype(bf16), TN_DIMS,
                                               preferred_element_type=f32)
            dd_ref[...] += jnp.sum(dy * u, axis=0, keepdims=True)
        else:
            @pl.when(i == nblk - 1)
            def _():
                p_r, p_i = _pow_segment(ar_ref[...], ai_ref[...], seg_len)
                p_i = -p_i
                t_r, t_i = car[N_SEG - 1:N_SEG, :], cai[N_SEG - 1:N_SEG, :]
                or_ref[N_SEG - 1:N_SEG, :] = jnp.zeros((1, N_STATES), f32)
                oi_ref[N_SEG - 1:N_SEG, :] = jnp.zeros((1, N_STATES), f32)
                for j in range(N_SEG - 2, -1, -1):
                    or_ref[j:j + 1, :] = t_r
                    oi_ref[j:j + 1, :] = t_i
                    m_r, m_i = _cmul(p_r, p_i, t_r, t_i)
                    t_r, t_i = car[j:j + 1, :] + m_r, cai[j:j + 1, :] + m_i

    blk = lambda w: pl.BlockSpec((R, w), lambda i: (nblk - 1 - i, 0))
    in_specs = [blk(SSM_WIDTH), blk(SSM_WIDTH), _full((4, 128, 512)), _full((4, 128, 512)), _full((1, N_STATES)),
                _full((1, N_STATES))]
    args = [dys_perm, y_perm, cdt_r, cdt_i, ar, ai]
    seg = jax.ShapeDtypeStruct((N_SEG, N_STATES), f32)
    if store:
        in_specs += [blk(SSM_WIDTH), blk(N_STATES), blk(N_STATES), _full((4, 512, 128)), _full((4, 512, 128)),
                     _full((1, SSM_WIDTH)), _full((N_SEG, N_STATES)), _full((N_SEG, N_STATES))]
        args += list(final)
        out_specs = [blk(SSM_WIDTH), _full((4, 512, 128)), _full((4, 512, 128)), _full((4, 128, 512)),
                     _full((4, 128, 512)), _full((N_SEG, N_STATES)), _full((N_SEG, N_STATES)), _full((1, SSM_WIDTH))]
        b_acc = jax.ShapeDtypeStruct((4, 512, 128), f32)
        c_acc = jax.ShapeDtypeStruct((4, 128, 512), f32)
        out_shape = [jax.ShapeDtypeStruct((S, SSM_WIDTH), bf16), b_acc, b_acc, c_acc, c_acc, seg, seg,
                     jax.ShapeDtypeStruct((1, SSM_WIDTH), f32)]
        scratch = [pltpu.VMEM((R, N_STATES), f32)] * 4 + [pltpu.VMEM((N_SEG, N_STATES), f32)] * 2
        name = "ssm_scan_bwd"
    else:
        out_specs = [_full((N_SEG, N_STATES)), _full((N_SEG, N_STATES))]
        out_shape = [seg, seg]
        scratch = [pltpu.VMEM((R, N_STATES), f32)] * 2 + [pltpu.VMEM((N_SEG, N_STATES), f32)] * 2
        name = "ssm_scan_bwd_carry"
    res, landed = _pcall(body, args, name=name, grid=(nblk,), in_specs=in_specs, out_specs=out_specs,
                         out_shape=out_shape, scratch_shapes=scratch, rides=rides)
    return (res, landed) if rides else res


def ssm_param_bwd(dbb_r, dbb_i, bt_re, bt_im, kr, ki, ar, ai, lr, li, ldt, da_r, da_i):
    def body(dbr_ref, dbi_ref, br_ref, bi_ref, kr_ref, ki_ref, ar_ref, ai_ref, lr_ref, li_ref, ldt_ref, dar_ref,
             dai_ref, obr_ref, obi_ref, olr_ref, oli_ref, odt_ref):
        dbr, dbi, b_r, b_i = dbr_ref[...], dbi_ref[...], br_ref[...], bi_ref[...]
        k_r, k_i, a_r, a_i = kr_ref[...], ki_ref[...], ar_ref[...], ai_ref[...]
        l_r, l_i = lr_ref[...], li_ref[...]
        dt = jnp.exp(ldt_ref[...])
        obr_ref[...] = k_r * dbr + k_i * dbi
        obi_ref[...] = k_r * dbi - k_i * dbr
        gk_r = jnp.sum(dbr * b_r + dbi * b_i, axis=0, keepdims=True)
        gk_i = jnp.sum(dbi * b_r - dbr * b_i, axis=0, keepdims=True)
        ga_r = jnp.sum(dar_ref[...], axis=0, keepdims=True)
        ga_i = jnp.sum(dai_ref[...], axis=0, keepdims=True)
        den = l_r * l_r + l_i * l_i
        c_r, c_i = l_r / den, l_i / den
        m_r, m_i = _cmul(c_r, c_i, gk_r, gk_i)
        g_r, g_i = ga_r + m_r, ga_i + m_i
        t1_r, t1_i = _cmul(dt * a_r, -dt * a_i, g_r, g_i)
        q_r, q_i = _cmul(k_r, k_i, c_r, -c_i)
        t2_r, t2_i = _cmul(-q_r, q_i, gk_r, gk_i)
        olr_ref[...] = t1_r + t2_r
        oli_ref[...] = t1_i + t2_i
        w_r, w_i = _cmul(l_r, l_i, a_r, a_i)
        odt_ref[...] = dt * (w_r * g_r + w_i * g_i)

    v = jax.ShapeDtypeStruct((1, N_STATES), f32)
    m = jax.ShapeDtypeStruct((SSM_GROUP, N_STATES), f32)
    return pl.pallas_call(body, name="ssm_param_bwd", out_shape=[m, m, v, v, v])(
        dbb_r, dbb_i, bt_re, bt_im, kr, ki, ar, ai, lr, li, ldt, da_r, da_i)


def group_sum(v):
    def body(v_ref, o_ref):
        o_ref[...] = jnp.sum(v_ref[...], axis=-1, keepdims=True)
    return pl.pallas_call(body, name="group_sum", out_shape=jax.ShapeDtypeStruct((v.shape[0], 1), f32))(v)


ATTN_SCALE = XATTN_HEAD_DIM ** -0.5


def _attn_probs(q_h, k_h):
    s = lax.dot_general(q_h, k_h, NT_DIMS, preferred_element_type=f32) * ATTN_SCALE
    e = jnp.exp(s - jnp.max(s, axis=-1, keepdims=True))
    return e / jnp.sum(e, axis=-1, keepdims=True)


def attn_fwd(proj, kv, tm=TM):
    S = proj.shape[0]
    M = kv.shape[0]

    def body(q_ref, kv_ref, o_ref):
        for h in range(XATTN_HEADS):
            cols = slice(h * XATTN_HEAD_DIM, (h + 1) * XATTN_HEAD_DIM)
            q_h = q_ref[:, cols].astype(bf16)
            k_h = kv_ref[:, cols]
            v_h = kv_ref[:, XATTN_WIDTH + h * XATTN_HEAD_DIM:XATTN_WIDTH + (h + 1) * XATTN_HEAD_DIM]
            p = _attn_probs(q_h, k_h)
            o_ref[:, cols] = jnp.dot(p.astype(bf16), v_h, preferred_element_type=f32).astype(bf16)

    return pl.pallas_call(
        body, name="attn_fwd", grid=(S // tm,), in_specs=[_rows(tm, XATTN_WIDTH, 10), _full((M, 2 * XATTN_WIDTH))],
        out_specs=_rows(tm, XATTN_WIDTH), out_shape=jax.ShapeDtypeStruct((S, XATTN_WIDTH), bf16),
        compiler_params=_params())(proj, kv)


def attn_bwd(do, proj, kv, dproj, tm=TM):
    S = proj.shape[0]
    M = kv.shape[0]

    def body(do_ref, q_ref, kv_ref, _, dq_ref, dkv_ref):
        i = pl.program_id(0)

        @pl.when(i == 0)
        def _():
            dkv_ref[...] = jnp.zeros_like(dkv_ref)

        for h in range(XATTN_HEADS):
            cols = slice(h * XATTN_HEAD_DIM, (h + 1) * XATTN_HEAD_DIM)
            vcols = slice(XATTN_WIDTH + h * XATTN_HEAD_DIM, XATTN_WIDTH + (h + 1) * XATTN_HEAD_DIM)
            q_h = q_ref[:, cols].astype(bf16)
            k_h = kv_ref[:, cols]
            v_h = kv_ref[:, vcols]
            do_h = do_ref[:, cols]
            p = _attn_probs(q_h, k_h)
            dp = lax.dot_general(do_h, v_h, NT_DIMS, preferred_element_type=f32)
            ds = (p * (dp - jnp.sum(dp * p, axis=-1, keepdims=True)) * ATTN_SCALE).astype(bf16)
            dq_ref[:, cols] = jnp.dot(ds, k_h, preferred_element_type=f32).astype(bf16)
            dkv_ref[:, cols] += lax.dot_general(ds, q_h, TN_DIMS, preferred_element_type=f32)
            dkv_ref[:, vcols] += lax.dot_general(p.astype(bf16), do_h, TN_DIMS, preferred_element_type=f32)

    return pl.pallas_call(
        body, name="attn_bwd", grid=(S // tm,),
        in_specs=[_rows(tm, XATTN_WIDTH), _rows(tm, XATTN_WIDTH, 10), _full((M, 2 * XATTN_WIDTH)),
                  pl.BlockSpec(memory_space=pl.ANY)],
        out_specs=[_rows(tm, XATTN_WIDTH, 10), _full((M, 2 * XATTN_WIDTH))],
        out_shape=[jax.ShapeDtypeStruct(dproj.shape, bf16), jax.ShapeDtypeStruct((M, 2 * XATTN_WIDTH), f32)],
        input_output_aliases={3: 0},
        compiler_params=_params())(do, proj, kv, dproj)


def merge_fwd(proj, b_gate, y_a, glu, y_c, tm=TM, rides=()):
    S = proj.shape[0]

    def body(g0_ref, g1_ref, g2_ref, b_ref, ya_ref, ga_ref, gb_ref, yc_ref, o_ref):
        b = b_ref[...]
        g0 = _sig(_f32(g0_ref) + b[:, 0:D_MODEL])
        g1 = _sig(_f32(g1_ref) + b[:, D_MODEL:2 * D_MODEL])
        g2 = _sig(_f32(g2_ref) + b[:, 2 * D_MODEL:3 * D_MODEL])
        y_b = ga_ref[...].astype(f32) * _sig(gb_ref[...].astype(f32))
        o_ref[...] = (g0 * ya_ref[...].astype(f32) + g1 * y_b + g2 * yc_ref[...].astype(f32)).astype(bf16)

    res, landed = _pcall(
        body, [proj, proj, proj, b_gate, y_a, glu, glu, y_c], name="merge_fwd", grid=(S // tm,),
        in_specs=[_rows(tm, D_MODEL, 0), _rows(tm, D_MODEL, 1), _rows(tm, D_MODEL, 2), _full((1, GATE_COLS)),
                  _rows(tm, D_MODEL), _rows(tm, D_MODEL, 0), _rows(tm, D_MODEL, 1), _rows(tm, D_MODEL)],
        out_specs=[_rows(tm, D_MODEL)], out_shape=[jax.ShapeDtypeStruct((S, D_MODEL), bf16)], rides=rides)
    return (res[0], landed) if rides else res[0]


def merge_bwd(dm, proj, b_gate, y_a, glu, y_c, tm=256, rides=()):
    S = proj.shape[0]

    def body(dm_ref, g0_ref, g1_ref, g2_ref, b_ref, ya_ref, ga_ref, gb_ref, yc_ref,
             dg_ref, dya_ref, dgl_ref, dyc_ref, dbg_ref):
        i = pl.program_id(0)
        b = b_ref[...]
        dmv = dm_ref[...].astype(f32)
        ga, sb = ga_ref[...].astype(f32), _sig(gb_ref[...].astype(f32))
        ys = (ya_ref[...].astype(f32), ga * sb, yc_ref[...].astype(f32))
        gs = tuple(_sig(_f32(r) + b[:, k * D_MODEL:(k + 1) * D_MODEL]) for k, r in enumerate((g0_ref, g1_ref, g2_ref)))

        @pl.when(i == 0)
        def _():
            dbg_ref[...] = jnp.zeros_like(dbg_ref)

        for k in range(3):
            dpre = dmv * ys[k] * gs[k] * (1.0 - gs[k])
            dg_ref[:, k * D_MODEL:(k + 1) * D_MODEL] = dpre.astype(bf16)
            dbg_ref[:, k * D_MODEL:(k + 1) * D_MODEL] += jnp.sum(dpre, axis=0, keepdims=True)
        dya_ref[...] = (dmv * gs[0]).astype(bf16)
        dyc_ref[...] = (dmv * gs[2]).astype(bf16)
        dyb = dmv * gs[1]
        dgl_ref[:, 0:D_MODEL] = (dyb * sb).astype(bf16)
        dgl_ref[:, D_MODEL:2 * D_MODEL] = (dyb * ga * sb * (1.0 - sb)).astype(bf16)

    res, landed = _pcall(
        body, [dm, proj, proj, proj, b_gate, y_a, glu, glu, y_c], name="merge_bwd", grid=(S // tm,),
        in_specs=[_rows(tm, D_MODEL), _rows(tm, D_MODEL, 0), _rows(tm, D_MODEL, 1), _rows(tm, D_MODEL, 2),
                  _full((1, GATE_COLS)), _rows(tm, D_MODEL), _rows(tm, D_MODEL, 0), _rows(tm, D_MODEL, 1),
                  _rows(tm, D_MODEL)],
        out_specs=[_rows(tm, GATE_COLS), _rows(tm, D_MODEL), _rows(tm, 2 * D_MODEL), _rows(tm, D_MODEL),
                   _full((1, GATE_COLS))],
        out_shape=[jax.ShapeDtypeStruct((S, IN_COLS), bf16), jax.ShapeDtypeStruct((S, D_MODEL), bf16),
                   jax.ShapeDtypeStruct((S, 2 * D_MODEL), bf16), jax.ShapeDtypeStruct((S, D_MODEL), bf16),
                   jax.ShapeDtypeStruct((1, GATE_COLS), f32)], rides=rides)
    return (res, landed) if rides else res


def _ln_stats(r):
    mu = jnp.mean(r, axis=-1, keepdims=True)
    xc = r - mu
    var = jnp.mean(xc * xc, axis=-1, keepdims=True)
    rstd = lax.rsqrt(var + LN_EPS)
    return xc * rstd, rstd


def _ln_bwd(dy, xhat, rstd, g):
    dxh = dy * g
    return rstd * (dxh - jnp.mean(dxh, axis=-1, keepdims=True) - xhat * jnp.mean(dxh * xhat, axis=-1, keepdims=True))


def _colsum(v):
    return jnp.sum(v, axis=0, keepdims=True)


def _ln2_loss_epi(t2, h1, target, b_down, g, b):
    xhat, rstd = _ln_stats(ALPHA * h1 + t2 + b_down)
    err = xhat * g + b - target
    dout = err * (1.0 / D_MODEL)
    dr = _ln_bwd(dout, xhat, rstd, g)
    return (dr, dr), (0.5 * _colsum(err * err) * (1.0 / D_MODEL), _colsum(dout * xhat), _colsum(dout), _colsum(dr))


def _ln1_fwd_epi(t1, x, g, b):
    xhat, _ = _ln_stats(ALPHA * x + t1)
    h = xhat * g + b
    return (t1, h, h), ()


def _ln1_bwd_epi(t3, x, t1, dr2, g):
    xhat, rstd = _ln_stats(ALPHA * x + t1)
    dh = ALPHA * dr2 + t3
    dr = _ln_bwd(dh, xhat, rstd, g)
    return (dr, dr), (_colsum(dh * xhat), _colsum(dh))


def ln1_fwd(x, t1, g, b, tm=TM):
    S = x.shape[0]

    def body(x_ref, t_ref, g_ref, b_ref, o_ref, ob_ref):
        xhat, _ = _ln_stats(ALPHA * x_ref[...] + t_ref[...])
        h = xhat * g_ref[...] + b_ref[...]
        o_ref[...] = h
        ob_ref[...] = h.astype(bf16)

    return pl.pallas_call(
        body, name="ln1_fwd", grid=(S // tm,),
        in_specs=[_rows(tm, D_MODEL), _rows(tm, D_MODEL), _full((1, D_MODEL)), _full((1, D_MODEL))],
        out_specs=[_rows(tm, D_MODEL), _rows(tm, D_MODEL)],
        out_shape=[jax.ShapeDtypeStruct((S, D_MODEL), f32), jax.ShapeDtypeStruct((S, D_MODEL), bf16)],
        compiler_params=_params())(x, t1, g, b)


def ln2_loss_bwd(h1, t2, b_down, g, b, target, tm=TM):
    S = h1.shape[0]

    def body(h_ref, t_ref, bd_ref, g_ref, b_ref, y_ref, dr_ref, drb_ref, l_ref, dg_ref, db_ref, dbd_ref):
        i = pl.program_id(0)
        xhat, rstd = _ln_stats(ALPHA * h_ref[...] + t_ref[...] + bd_ref[...])
        gv = g_ref[...]
        err = xhat * gv + b_ref[...] - y_ref[...]
        dout = err * (1.0 / D_MODEL)
        dr = _ln_bwd(dout, xhat, rstd, gv)
        dr_ref[...] = dr
        drb_ref[...] = dr.astype(bf16)
        part = 0.5 * jnp.sum(jnp.sum(err * err, axis=-1, keepdims=True) * (1.0 / D_MODEL), axis=0, keepdims=True)

        @pl.when(i == 0)
        def _():
            l_ref[...] = jnp.zeros_like(l_ref)
            dg_ref[...] = jnp.zeros_like(dg_ref)
            db_ref[...] = jnp.zeros_like(db_ref)
            dbd_ref[...] = jnp.zeros_like(dbd_ref)

        l_ref[...] += jnp.broadcast_to(part, l_ref.shape)
        dg_ref[...] += jnp.sum(dout * xhat, axis=0, keepdims=True)
        db_ref[...] += jnp.sum(dout, axis=0, keepdims=True)
        dbd_ref[...] += jnp.sum(dr, axis=0, keepdims=True)

    vec = jax.ShapeDtypeStruct((1, D_MODEL), f32)
    return pl.pallas_call(
        body, name="ln2_loss_bwd", grid=(S // tm,),
        in_specs=[_rows(tm, D_MODEL), _rows(tm, D_MODEL), _full((1, D_MODEL)), _full((1, D_MODEL)),
                  _full((1, D_MODEL)), _rows(tm, D_MODEL)],
        out_specs=[_rows(tm, D_MODEL), _rows(tm, D_MODEL), _full((1, 128)), _full((1, D_MODEL)), _full((1, D_MODEL)),
                   _full((1, D_MODEL))],
        out_shape=[jax.ShapeDtypeStruct((S, D_MODEL), f32), jax.ShapeDtypeStruct((S, D_MODEL), bf16),
                   jax.ShapeDtypeStruct((1, 128), f32), vec, vec, vec],
        compiler_params=_params())(h1, t2, b_down, g, b, target)


def ln1_bwd(x, t1, dr2, t3, g, tm=TM):
    S = x.shape[0]

    def body(x_ref, t_ref, dr2_ref, t3_ref, g_ref, dr_ref, drb_ref, dg_ref, db_ref):
        i = pl.program_id(0)
        xhat, rstd = _ln_stats(ALPHA * x_ref[...] + t_ref[...])
        dh = ALPHA * dr2_ref[...] + t3_ref[...]
        dr = _ln_bwd(dh, xhat, rstd, g_ref[...])
        dr_ref[...] = dr
        drb_ref[...] = dr.astype(bf16)

        @pl.when(i == 0)
        def _():
            dg_ref[...] = jnp.zeros_like(dg_ref)
            db_ref[...] = jnp.zeros_like(db_ref)

        dg_ref[...] += jnp.sum(dh * xhat, axis=0, keepdims=True)
        db_ref[...] += jnp.sum(dh, axis=0, keepdims=True)

    vec = jax.ShapeDtypeStruct((1, D_MODEL), f32)
    return pl.pallas_call(
        body, name="ln1_bwd", grid=(S // tm,),
        in_specs=[_rows(tm, D_MODEL)] * 4 + [_full((1, D_MODEL))],
        out_specs=[_rows(tm, D_MODEL), _rows(tm, D_MODEL), _full((1, D_MODEL)), _full((1, D_MODEL))],
        out_shape=[jax.ShapeDtypeStruct((S, D_MODEL), f32), jax.ShapeDtypeStruct((S, D_MODEL), bf16), vec, vec],
        compiler_params=_params())(x, t1, dr2, t3, g)


def exchange(*rides, name):
    return _pcall(lambda: None, [], name=name, grid=(), in_specs=[], out_specs=[], out_shape=[], rides=rides)[1]


def gather_two_level(shard, name):
    K, n = shard.shape
    h = K // 2

    def body(in_ref, out_ref, ici_send, ici_recv, d2d_send, d2d_recv, local_sem):
        x, y, c = lax.axis_index("x"), lax.axis_index("y"), lax.axis_index("c")
        me = 2 * x + y
        mine = pl.ds(pl.multiple_of(c * h, 16), h)
        theirs = pl.ds(pl.multiple_of((1 - c) * h, 16), h)
        peers = _xy_peers(x, y)

        def over_ici(k, dst):
            return pltpu.make_async_remote_copy(src_ref=in_ref.at[mine], dst_ref=dst, send_sem=ici_send.at[k],
                                                recv_sem=ici_recv.at[k], device_id=(*peers[k], c), device_id_type=MESH)

        def over_d2d(k, rows):
            blk = out_ref.at[2 * peers[k][0] + peers[k][1], rows]
            return pltpu.make_async_remote_copy(src_ref=blk, dst_ref=blk, send_sem=d2d_send.at[k],
                                                recv_sem=d2d_recv.at[k], device_id=(x, y, 1 - c), device_id_type=MESH)

        local = pltpu.make_async_copy(in_ref, out_ref.at[me], local_sem)
        local.start()
        started = [over_ici(k, out_ref.at[me, mine]) for k in range(3)]
        for cp in started:
            cp.start()
        for k in range(3):
            over_ici(k, out_ref.at[2 * peers[k][0] + peers[k][1], mine]).wait_recv()
            passed = over_d2d(k, mine)
            passed.start()
            started.append(passed)
        for k in range(3):
            over_d2d(k, theirs).wait_recv()
        for cp in started:
            cp.wait_send()
        local.wait()

    return pl.pallas_call(
        body, name=name, in_specs=[HBM_SPEC], out_specs=HBM_SPEC,
        out_shape=jax.ShapeDtypeStruct((4, K, n), shard.dtype),
        scratch_shapes=[pltpu.SemaphoreType.DMA((3,))] * 4 + [pltpu.SemaphoreType.DMA(())])(shard)


ROW_TILE = 256


def sum_slots(recv, name):
    n, R, C = recv.shape
    tm = min(R, ROW_TILE)

    def body(r_ref, o_ref):
        acc = r_ref[0].astype(f32)
        for k in range(1, n):
            acc = acc + r_ref[k].astype(f32)
        o_ref[...] = acc

    return pl.pallas_call(
        body, name=name, grid=(R // tm,), in_specs=[pl.BlockSpec((n, tm, C), lambda i: (0, i, 0))],
        out_specs=_rows(tm, C), out_shape=jax.ShapeDtypeStruct((R, C), f32), compiler_params=_params())(recv)


def _adamw(w, g, m, v):
    m = ADAM_B1 * m + (1.0 - ADAM_B1) * g
    v = ADAM_B2 * v + (1.0 - ADAM_B2) * (g * g)
    m_hat = m / (1.0 - ADAM_B1 ** ADAM_STEP)
    v_hat = v / (1.0 - ADAM_B2 ** ADAM_STEP)
    delta = -ADAM_LR * (m_hat / (jnp.sqrt(v_hat) + ADAM_EPS) + ADAM_WD * w)
    return delta, m, v


def adam_pair(p, q, w, m, v, name):
    R, C = w.shape
    tm = min(R, ROW_TILE)

    def body(p_ref, q_ref, w_ref, m_ref, v_ref, g_ref, d_ref, nm_ref, nv_ref):
        g = p_ref[...] + q_ref[...]
        g_ref[...] = g
        d_ref[...], nm_ref[...], nv_ref[...] = _adamw(w_ref[...], g, m_ref[...], v_ref[...])

    o = jax.ShapeDtypeStruct((R, C), f32)
    return pl.pallas_call(body, name=name, grid=(R // tm,), in_specs=[_rows(tm, C)] * 5,
                          out_specs=[_rows(tm, C)] * 4, out_shape=[o] * 4, compiler_params=_params())(p, q, w, m, v)


def adam_slots(recv, w, m, v, name):
    n = recv.shape[0]

    def body(r_ref, w_ref, m_ref, v_ref, g_ref, d_ref, nm_ref, nv_ref):
        g = r_ref[0]
        for k in range(1, n):
            g = g + r_ref[k]
        g_ref[...] = g
        d_ref[...], nm_ref[...], nv_ref[...] = _adamw(w_ref[...], g, m_ref[...], v_ref[...])

    o = jax.ShapeDtypeStruct(w.shape, f32)
    return pl.pallas_call(body, name=name, out_shape=[o] * 4, compiler_params=_params())(recv, w, m, v)


BIG = ("w_in", "w_conv_out", "w_glu", "w_xattn_out", "w_kv", "w_out", "w_up", "w_down")
MID = ("w_conv_out", "w_glu", "w_xattn_out", "w_kv", "w_out")
SMALL = ("b_gate", "ssm_lam_re", "ssm_lam_im", "ssm_log_dt", "ssm_b_re", "ssm_b_im", "ssm_c_re", "ssm_c_im", "ssm_d",
         "ln1_g", "ln1_b", "b_up", "b_down", "ln2_g", "ln2_b")
def _pad_flat(a, mult=1024):
    a = a.reshape(-1)
    return jnp.pad(a, (0, (-a.shape[0]) % mult))


def _perm(a):
    S, W = a.shape
    return a.reshape(N_SEG, S // N_SEG, W).transpose(1, 0, 2).reshape(S, W)


def _unperm(a):
    S, W = a.shape
    return a.reshape(S // N_SEG, N_SEG, W).transpose(1, 0, 2).reshape(S, W)


def _state_rows(a):
    return a.transpose(2, 0, 1).reshape(SSM_GROUP, N_STATES)


def _block_diag_b(bt):
    b4 = bt.reshape(SSM_GROUP, 4, 8, SSM_STATE)
    eye = jnp.eye(8, dtype=bt.dtype)
    return jnp.einsum("hcgp,gk->cghkp", b4, eye).reshape(4, 128, 512)


def _block_diag_c(c):
    c4 = c.reshape(4, 8, SSM_GROUP, SSM_STATE)
    eye = jnp.eye(8, dtype=c.dtype)
    return jnp.einsum("cghp,gk->cgpkh", c4, eye).reshape(4, 512, 128)


def _diag_b(acc):
    a = acc.reshape(4, 8, SSM_STATE, 8, SSM_GROUP)
    eye = jnp.eye(8, dtype=acc.dtype)
    return jnp.einsum("cgpkh,gk->hcgp", a, eye).reshape(SSM_GROUP, N_STATES)


def _diag_c(acc):
    a = acc.reshape(4, 8, SSM_GROUP, 8, SSM_STATE)
    eye = jnp.eye(8, dtype=acc.dtype)
    return jnp.einsum("cghkp,gk->cghp", a, eye).reshape(SSM_GROUPS, SSM_GROUP, SSM_STATE)


def kernel(x, mem, w_in, b_gate, conv_w, w_conv_out, ssm_lam_re, ssm_lam_im, ssm_log_dt, ssm_b_re, ssm_b_im, ssm_c_re, ssm_c_im, ssm_d, w_glu, w_kv, w_xattn_out, w_out, ln1_g, ln1_b, w_up, b_up, w_down, b_down, ln2_g, ln2_b, loss_target, m_w_in, m_b_gate, m_conv_w, m_w_conv_out, m_ssm_lam_re, m_ssm_lam_im, m_ssm_log_dt, m_ssm_b_re, m_ssm_b_im, m_ssm_c_re, m_ssm_c_im, m_ssm_d, m_w_glu, m_w_kv, m_w_xattn_out, m_w_out, m_ln1_g, m_ln1_b, m_w_up, m_b_up, m_w_down, m_b_down, m_ln2_g, m_ln2_b, v_w_in, v_b_gate, v_conv_w, v_w_conv_out, v_ssm_lam_re, v_ssm_lam_im, v_ssm_log_dt, v_ssm_b_re, v_ssm_b_im, v_ssm_c_re, v_ssm_c_im, v_ssm_d, v_w_glu, v_w_kv, v_w_xattn_out, v_w_out, v_ln1_g, v_ln1_b, v_w_up, v_b_up, v_w_down, v_b_down, v_ln2_g, v_ln2_b):
    W = dict(w_in=w_in, b_gate=b_gate, conv_w=conv_w, w_conv_out=w_conv_out, ssm_lam_re=ssm_lam_re,
             ssm_lam_im=ssm_lam_im, ssm_log_dt=ssm_log_dt, ssm_b_re=ssm_b_re, ssm_b_im=ssm_b_im, ssm_c_re=ssm_c_re,
             ssm_c_im=ssm_c_im, ssm_d=ssm_d, w_glu=w_glu, w_kv=w_kv, w_xattn_out=w_xattn_out, w_out=w_out,
             ln1_g=ln1_g, ln1_b=ln1_b, w_up=w_up, b_up=b_up, w_down=w_down, b_down=b_down, ln2_g=ln2_g, ln2_b=ln2_b)
    MOM = dict(w_in=m_w_in, b_gate=m_b_gate, conv_w=m_conv_w, w_conv_out=m_w_conv_out, ssm_lam_re=m_ssm_lam_re,
               ssm_lam_im=m_ssm_lam_im, ssm_log_dt=m_ssm_log_dt, ssm_b_re=m_ssm_b_re, ssm_b_im=m_ssm_b_im,
               ssm_c_re=m_ssm_c_re, ssm_c_im=m_ssm_c_im, ssm_d=m_ssm_d, w_glu=m_w_glu, w_kv=m_w_kv,
               w_xattn_out=m_w_xattn_out, w_out=m_w_out, ln1_g=m_ln1_g, ln1_b=m_ln1_b, w_up=m_w_up, b_up=m_b_up,
               w_down=m_w_down, b_down=m_b_down, ln2_g=m_ln2_g, ln2_b=m_ln2_b)
    VEL = dict(w_in=v_w_in, b_gate=v_b_gate, conv_w=v_conv_w, w_conv_out=v_w_conv_out, ssm_lam_re=v_ssm_lam_re,
               ssm_lam_im=v_ssm_lam_im, ssm_log_dt=v_ssm_log_dt, ssm_b_re=v_ssm_b_re, ssm_b_im=v_ssm_b_im,
               ssm_c_re=v_ssm_c_re, ssm_c_im=v_ssm_c_im, ssm_d=v_ssm_d, w_glu=v_w_glu, w_kv=v_w_kv,
               w_xattn_out=v_w_xattn_out, w_out=v_w_out, ln1_g=v_ln1_g, ln1_b=v_ln1_b, w_up=v_w_up, b_up=v_b_up,
               w_down=v_w_down, b_down=v_b_down, ln2_g=v_ln2_g, ln2_b=v_ln2_b)
    names = list(W)
    xy = 2 * lax.axis_index("x") + lax.axis_index("y")

    xs = x[0]
    S = xs.shape[0]
    mems = mem[0]
    tgt = loss_target[0]

    shard_bf = {n: W[n][0].astype(bf16) for n in BIG}
    win3 = gather_two_level(shard_bf["w_in"], "gather_w_in")

    lr = ssm_lam_re.reshape(1, N_STATES)
    li = ssm_lam_im.reshape(1, N_STATES)
    ldt = jnp.repeat(ssm_log_dt.reshape(SSM_GROUPS), SSM_STATE).reshape(1, N_STATES)
    bt_re, bt_im = _state_rows(ssm_b_re[0]), _state_rows(ssm_b_im[0])
    ar, ai, kr, ki, bbt_r, bbt_i = ssm_prep(lr, li, ldt, bt_re, bt_im)
    bd_r, bd_i = _block_diag_b(bbt_r).astype(bf16), _block_diag_b(bbt_i).astype(bf16)
    cd_r, cd_i = _block_diag_c(ssm_c_re[0]).astype(bf16), _block_diag_c(ssm_c_im[0]).astype(bf16)
    bdt_r, bdt_i = bd_r.transpose(0, 2, 1), bd_i.transpose(0, 2, 1)
    cdt_r, cdt_i = cd_r.transpose(0, 2, 1), cd_i.transpose(0, 2, 1)
    d_skip = ssm_d.reshape(1, SSM_WIDTH)

    x_bf = xs.astype(bf16)
    mem_bf = mems.astype(bf16)
    u_col = GATE_COLS + 3 * CONV_WIDTH
    proj, (wco3, wglu3, wxo3, wkv4, wout4, convw4) = mm_fwd(
        x_bf, win3, "proj", outs=(bf16,),
        rides=tuple(("gather", shard_bf[n]) for n in MID) + (("gather", conv_w[0]),))
    wkv3 = wkv4.reshape(1, D_MODEL, 2 * XATTN_WIDTH)
    wout3 = wout4.reshape(1, D_MODEL, D_MODEL)
    convw_full = convw4.transpose(1, 0, 2).reshape(3, CONV_WIDTH)
    a_conv = conv_fwd(proj, convw_full)
    y_a = mm_fwd_small(a_conv, wco3, "conv_out", out_dtype=bf16)
    u_perm = _perm(proj[:, u_col:u_col + SSM_WIDTH])
    half = D_MODEL // 2
    (init_r, init_i), (wup_a,) = ssm_scan_fwd(u_perm, bd_r, bd_i, ar, ai,
                                              rides=(("gather", shard_bf["w_up"][:half]),))
    (ysm_perm, ys_perm, st_r, st_i), (wup_b,) = ssm_scan_fwd(
        u_perm, bd_r, bd_i, ar, ai, final=(cd_r, cd_i, d_skip, init_r, init_i),
        rides=(("gather", shard_bf["w_up"][half:]),))
    wup3 = jnp.concatenate([wup_a, wup_b], axis=1)
    y_s = _unperm(ys_perm)
    glu = mm_fwd_small(y_s, wglu3, "glu", out_dtype=bf16)
    kv = mm_fwd_small(mem_bf, wkv3, "kv", out_dtype=bf16)
    o_att = attn_fwd(proj, kv)
    y_c = mm_fwd_small(o_att, wxo3, "xattn_out", out_dtype=bf16)
    merged = merge_fwd(proj, b_gate, y_a, glu, y_c)
    t1, h1, h1_bf = mm_rows_fused(merged, wout3, "w_out_ln1", transposed=False, tm=512, epi=_ln1_fwd_epi,
                                  extras=(xs,), consts=(ln1_g, ln1_b), outs=(f32, f32, bf16))
    (r_up, hdn), (wdn4,) = mm_fwd(h1_bf, wup3, "w_up", bias=b_up, outs=(bf16, bf16), second=_relu2,
                                  first=lambda v: jnp.maximum(v, 0.0), rides=(("gather", shard_bf["w_down"]),))
    wdn3 = wdn4.reshape(1, D_FF, D_MODEL)

    dr2, dr2_bf, loss_cols, d_ln2_g, d_ln2_b, d_b_down = mm_rows_fused(
        hdn, wdn3, "w_down_ln2_loss", transposed=False, tm=512, epi=_ln2_loss_epi, extras=(h1, tgt),
        consts=(b_down, ln2_g, ln2_b), outs=(f32, bf16), nsums=4)
    part, other = {}, {}
    g_w_down = mm_bwd_w(hdn, dr2_bf, 1, "dw_down", tk=1024).reshape(4, -1, D_MODEL)
    (dup, d_b_up), (recv_dn,) = mm_bwd_x(dr2_bf, wdn3, "dup", tm=512, extras=(r_up,), colsum=True, out_dtype=bf16,
                                         epi=lambda acc, r: acc * (2.0 * r.astype(f32)),
                                         rides=(("scatter", g_w_down),))
    part["w_down"] = sum_slots(recv_dn, "sum_w_down")
    g_w_up = mm_bwd_w(h1_bf, dup, 4, "dw_up")
    dr1, dr1_bf, d_ln1_g, d_ln1_b = mm_rows_fused(
        dup, wup3, "dh1_ln1_bwd", transposed=True, tm=256, epi=_ln1_bwd_epi, extras=(xs, t1, dr2), consts=(ln1_g,),
        outs=(f32, bf16), nsums=2)
    g_w_out = mm_bwd_w(merged, dr1_bf, 1, "dw_out").reshape(4, -1, D_MODEL)
    dmerged, (recv_out,) = mm_bwd_x(dr1_bf, wout3, "dmerged", tm=1024, out_dtype=bf16,
                                    rides=(("scatter", g_w_out),))
    part["w_out"] = sum_slots(recv_out, "sum_w_out")
    (dproj, dy_a, dglu, dy_c, d_b_gate), (recv_up,) = merge_bwd(dmerged, proj, b_gate, y_a, glu, y_c,
                                                                rides=(("scatter", g_w_up),))
    part["w_up"] = sum_slots(recv_up, "sum_w_up")

    g_w_co = mm_bwd_w_small(a_conv, dy_a, 4, "dw_conv_out")
    da_conv = mm_bwd_x(dy_a, wco3, "da_conv", tm=1024)
    dproj, d_conv_w = conv_bwd(da_conv, proj, convw_full, dproj)

    g_w_glu = mm_bwd_w_small(y_s, dglu, 4, "dw_glu")
    dys_perm = _perm(mm_bwd_x(dglu, wglu3, "dy_s", tm=1024))
    linit_r, linit_i = ssm_scan_bwd(dys_perm, ysm_perm, cdt_r, cdt_i, ar, ai)
    (du_perm, dbacc_r, dbacc_i, dcacc_r, dcacc_i, da_r, da_i, d_ssm_d), (recv_co, recv_glu) = ssm_scan_bwd(
        dys_perm, ysm_perm, cdt_r, cdt_i, ar, ai,
        final=(u_perm, st_r, st_i, bdt_r, bdt_i, d_skip, linit_r, linit_i),
        rides=(("scatter", g_w_co), ("scatter", g_w_glu)))
    part["w_conv_out"] = sum_slots(recv_co, "sum_w_conv_out")
    part["w_glu"] = sum_slots(recv_glu, "sum_w_glu")
    dbt_re, dbt_im, d_lr, d_li, d_ldt_state = ssm_param_bwd(
        _diag_b(dbacc_r), _diag_b(dbacc_i), bt_re, bt_im, kr, ki, ar, ai, lr, li, ldt, da_r, da_i)
    d_log_dt = group_sum(d_ldt_state.reshape(SSM_GROUPS, SSM_STATE))
    d_b_re = dbt_re.reshape(SSM_GROUP, SSM_GROUPS, SSM_STATE).transpose(1, 2, 0)
    d_b_im = dbt_im.reshape(SSM_GROUP, SSM_GROUPS, SSM_STATE).transpose(1, 2, 0)
    d_c_re = _diag_c(dcacc_r)
    d_c_im = -_diag_c(dcacc_i)

    g_w_xo = mm_bwd_w_small(o_att, dy_c, 4, "dw_xattn_out")
    do_att = mm_bwd_x(dy_c, wxo3, "do_att", tm=1024, out_dtype=bf16)
    dproj, dkv = attn_bwd(do_att, proj, kv, dproj)
    g_w_kv = mm_bwd_w(mem_bf, dkv, 1, "dw_kv").reshape(4, -1, D_MODEL)

    dproj = lax.dynamic_update_slice(dproj, _unperm(du_perm), (0, u_col))
    g_small = {"b_gate": d_b_gate, "ssm_lam_re": d_lr, "ssm_lam_im": d_li, "ssm_log_dt": d_log_dt, "ssm_b_re": d_b_re,
               "ssm_b_im": d_b_im, "ssm_c_re": d_c_re, "ssm_c_im": d_c_im, "ssm_d": d_ssm_d, "ln1_g": d_ln1_g,
               "ln1_b": d_ln1_b, "b_up": d_b_up, "b_down": d_b_down, "ln2_g": d_ln2_g, "ln2_b": d_ln2_b}
    small_names = SMALL + ("conv_w",)
    g_small["conv_w"] = d_conv_w
    sizes = {n: (g_small[n].size + 1023) // 1024 * 1024 for n in small_names}
    pack = lambda d: jnp.concatenate([_pad_flat(d[n]) for n in small_names]).reshape(-1, 128)
    early = ("w_down", "w_up", "w_out", "w_conv_out", "w_glu")
    g_w_in, landed = mm_bwd_w(
        x_bf, dproj, 4, "dw_in", rides=(("scatter", g_w_xo), ("scatter", g_w_kv), ("all", pack(g_small)))
        + tuple(("pair", part[n]) for n in early))
    recv_xo, recv_kv, srecv = landed[:3]
    other.update(zip(early, landed[3:]))
    part["w_xattn_out"] = sum_slots(recv_xo, "sum_w_xattn_out")
    part["w_kv"] = sum_slots(recv_kv, "sum_w_kv")
    dx, (recv_in, other["w_xattn_out"], other["w_kv"]) = mm_bwd_x(
        dproj, win3, "dx", extras=(dr1,), epi=lambda acc, d: acc + ALPHA * d,
        rides=(("scatter", g_w_in), ("pair", part["w_xattn_out"]), ("pair", part["w_kv"])))
    part["w_in"] = sum_slots(recv_in, "sum_w_in")
    other["w_in"], = exchange(("pair", part["w_in"]), name="swap_w_in")
    res = [{}, {}, {}, {}]
    for n in BIG:
        for k, r in enumerate(adam_pair(part[n], other[n], W[n][0], MOM[n][0], VEL[n][0], "adam_" + n)):
            res[k][n] = r[None]
    conv_zero = jnp.zeros((3, CONV_WIDTH), f32)
    gs, ds_, ms, vs = adam_slots(srecv, pack({**{n: W[n] for n in SMALL}, "conv_w": conv_zero}),
                                 pack({**{n: MOM[n] for n in SMALL}, "conv_w": conv_zero}),
                                 pack({**{n: VEL[n] for n in SMALL}, "conv_w": conv_zero}), "adam_small")

    def unpack_small(buf):
        flat = buf.reshape(-1)
        out, r = {}, 0
        for n in small_names:
            ref = g_small[n] if n == "conv_w" else W[n]
            out[n] = flat[r:r + ref.size].reshape(ref.shape)
            r += sizes[n]
        return out

    res_s = [unpack_small(b) for b in (gs, ds_, ms, vs)]
    g_conv = lax.dynamic_slice(res_s[0]["conv_w"], (0, xy * 128), (3, 128))
    conv_slots = g_conv.reshape(1, 3, 128)
    cg, cd, cm, cv = adam_slots(conv_slots, conv_w[0], m_conv_w[0], v_conv_w[0], "adam_conv")
    conv_res = [cg, cd, cm, cv]

    loss = lax.psum(jnp.sum(loss_cols), ("x", "y", "c"))
    outs = [loss, dx.reshape(x.shape)]
    for k in range(4):
        for n in names:
            if n == "conv_w":
                outs.append(conv_res[k].reshape(conv_w.shape))
            elif n in BIG:
                outs.append(res[k][n])
            else:
                outs.append(res_s[k][n])
    return tuple(outs)
```

```python
import functools
import math

import jax
import jax.numpy as jnp
from jax import lax
from jax.experimental import pallas as pl
from jax.experimental.pallas import tpu as pltpu

f32 = jnp.float32
bf16 = jnp.bfloat16

D_MODEL = 1024
CONV_WIDTH = 512
SSM_WIDTH = 512
SSM_GROUP = 16
SSM_GROUPS = 32
SSM_STATE = 64
N_STATES = SSM_GROUPS * SSM_STATE
XATTN_HEADS = 4
XATTN_HEAD_DIM = 128
XATTN_WIDTH = 512
D_FF = 4096
GATE_COLS = 3 * D_MODEL
IN_COLS = GATE_COLS + 3 * CONV_WIDTH + SSM_WIDTH + XATTN_WIDTH
ALPHA = 2.0 ** 0.25
LN_EPS = 1e-5
ADAM_LR = 0.001
ADAM_B1 = 0.9
ADAM_B2 = 0.999
ADAM_EPS = 1e-08
ADAM_WD = 0.01
ADAM_STEP = 10

N_SEG = 8
SCAN_ROWS = 256
LANE_STRIP = 512
TM = 512
VMEM_LIMIT = 48 * 1024 * 1024
MESH = pl.DeviceIdType.MESH

NT_DIMS = (((1,), (1,)), ((), ()))
TN_DIMS = (((0,), (0,)), ((), ()))


def _params():
    return pltpu.CompilerParams(vmem_limit_bytes=VMEM_LIMIT)


def _full(shape):
    n = len(shape)
    return pl.BlockSpec(shape, lambda *_: (0,) * n)


def _rows(tm, w, cb=0):
    return pl.BlockSpec((tm, w), lambda i: (i, cb))


def _sig(x):
    return 1.0 / (1.0 + jnp.exp(-x))


HBM_SPEC = pl.BlockSpec(memory_space=pl.ANY)
RIDE_PEERS = {"gather": 3, "scatter": 3, "all": 7, "pair": 1}


def _xy_peers(x, y):
    return [(1 - x, y), (x, 1 - y), (1 - x, 1 - y)]


def _ride_copies(kind, src, dst, send_sems, recv_sems, local_sem):
    x, y, c = lax.axis_index("x"), lax.axis_index("y"), lax.axis_index("c")
    me = 2 * x + y
    if kind == "all":
        flips = [(fx, fy, fc) for fx in (0, 1) for fy in (0, 1) for fc in (0, 1)][1:]
        peers = [(x ^ fx, y ^ fy, c ^ fc) for fx, fy, fc in flips]
        slot = lambda p: 4 * p[0] + 2 * p[1] + p[2]
        mine = slot((x, y, c))
    else:
        peers = [(px, py, c) for px, py in _xy_peers(x, y)]
        slot = lambda p: 2 * p[0] + p[1]
        mine = me

    def remote(k, s, d):
        return pltpu.make_async_remote_copy(src_ref=s, dst_ref=d, send_sem=send_sems.at[k], recv_sem=recv_sems.at[k],
                                            device_id=peers[k], device_id_type=MESH)

    if kind == "pair":
        peers = [(x, y, 1 - c)]
        return None, [remote(0, src, dst)], [remote(0, src, dst)]
    if kind == "scatter":
        local = pltpu.make_async_copy(src.at[me], dst.at[me], local_sem)
        sends = [remote(k, src.at[slot(p)], dst.at[me]) for k, p in enumerate(peers)]
        lands = [remote(k, src.at[me], dst.at[slot(p)]) for k, p in enumerate(peers)]
    else:
        local = pltpu.make_async_copy(src, dst.at[mine], local_sem)
        sends = [remote(k, src, dst.at[mine]) for k, p in enumerate(peers)]
        lands = [remote(k, src, dst.at[slot(p)]) for k, p in enumerate(peers)]
    return local, sends, lands


def _ride_shape(kind, src):
    lead = {"gather": (4,), "scatter": (), "all": (8,), "pair": ()}[kind]
    return jax.ShapeDtypeStruct(lead + src.shape, src.dtype)


def _pcall(body, args, *, name, grid, in_specs, out_specs, out_shape, scratch_shapes=(), rides=(), aliases=None):
    n_in, n_out, n_scr, nr = len(in_specs), len(out_specs), len(scratch_shapes), len(rides)
    kinds = [k for k, _ in rides]

    def wrapped(*refs):
        ins, rsrc = refs[:n_in], refs[n_in:n_in + nr]
        outs = refs[n_in + nr:n_in + nr + n_out]
        rdst = refs[n_in + nr + n_out:n_in + 2 * nr + n_out]
        scr = refs[n_in + 2 * nr + n_out:n_in + 2 * nr + n_out + n_scr]
        sems = refs[n_in + 2 * nr + n_out + n_scr:]
        def start():
            for r, kind in enumerate(kinds):
                local, sends, _ = _ride_copies(kind, rsrc[r], rdst[r], *sems[3 * r:3 * r + 3])
                if local is not None:
                    local.start()
                for cp in sends:
                    cp.start()

        def finish():
            for r, kind in enumerate(kinds):
                local, sends, lands = _ride_copies(kind, rsrc[r], rdst[r], *sems[3 * r:3 * r + 3])
                for cp in lands:
                    cp.wait_recv()
                for cp in sends:
                    cp.wait_send()
                if local is not None:
                    local.wait()

        if nr and grid:
            ids = [pl.program_id(a) for a in range(len(grid))]
            first = functools.reduce(jnp.logical_and, [i == 0 for i in ids])
            last = functools.reduce(jnp.logical_and, [i == g - 1 for i, g in zip(ids, grid)])
            pl.when(first)(start)
        elif nr:
            start()
        body(*ins, *outs, *scr)
        if nr and grid:
            pl.when(last)(finish)
        elif nr:
            finish()

    sems = []
    for kind in kinds:
        n = RIDE_PEERS[kind]
        sems += [pltpu.SemaphoreType.DMA((n,)), pltpu.SemaphoreType.DMA((n,)), pltpu.SemaphoreType.DMA(())]
    res = pl.pallas_call(
        wrapped, name=name, grid=grid, in_specs=list(in_specs) + [HBM_SPEC] * nr,
        out_specs=list(out_specs) + [HBM_SPEC] * nr,
        out_shape=list(out_shape) + [_ride_shape(k, s) for k, s in rides],
        scratch_shapes=list(scratch_shapes) + sems, input_output_aliases=aliases or {},
        compiler_params=_params())(*args, *[s for _, s in rides])
    return list(res[:n_out]), list(res[n_out:])


def _bf(v):
    return v if v.dtype == bf16 else v.astype(bf16)


def mm_fwd_small(a, w3, name, *, tm=1024, out_dtype=f32):
    M, K = a.shape
    J, _, n = w3.shape
    tm = min(tm, M)

    def body(a_ref, w_ref, o_ref):
        av = _bf(a_ref[...])
        for j in range(J):
            o_ref[:, j * n:(j + 1) * n] = jnp.dot(av, w_ref[j], preferred_element_type=f32).astype(out_dtype)

    return pl.pallas_call(
        body, name=name, grid=(M // tm,), in_specs=[_rows(tm, K), _full((J, K, n))], out_specs=_rows(tm, J * n),
        out_shape=jax.ShapeDtypeStruct((M, J * n), out_dtype), compiler_params=_params())(a, w3)


def mm_bwd_w_small(a, dy, J, name, *, tm=2048):
    M, K = a.shape
    n = dy.shape[1] // J
    tm = min(tm, M)
    ns = M // tm

    def body(a_ref, dy_ref, o_ref, acc_ref):
        s = pl.program_id(0)
        part = lax.dot_general(_bf(a_ref[...]), _bf(dy_ref[...]), TN_DIMS, preferred_element_type=f32)

        @pl.when(s == 0)
        def _():
            acc_ref[...] = part

        @pl.when(s > 0)
        def _():
            acc_ref[...] += part

        @pl.when(s == ns - 1)
        def _():
            for j in range(J):
                o_ref[j] = acc_ref[:, j * n:(j + 1) * n].astype(bf16)

    return pl.pallas_call(
        body, name=name, grid=(ns,), in_specs=[_rows(tm, K), _rows(tm, J * n)], out_specs=_full((J, K, n)),
        out_shape=jax.ShapeDtypeStruct((J, K, n), bf16), scratch_shapes=[pltpu.VMEM((K, J * n), f32)],
        compiler_params=_params())(a, dy)


def mm_fwd(a, w3, name, *, tm=1024, tn=None, bias=None, outs=(f32,), first=None, second=None, emit_a=False,
           rides=()):
    M, K = a.shape
    J, _, n = w3.shape
    tm = min(tm, M)
    tn = tn or n
    nl = n // tn
    nb = 0 if bias is None else 1

    def body(*refs):
        a_ref, w_ref = refs[0], refs[1]
        av = _bf(a_ref[...])
        acc = jnp.dot(av, w_ref[0], preferred_element_type=f32)
        if bias is not None:
            acc = acc + refs[2][...]
        refs[2 + nb][...] = (acc if first is None else first(acc)).astype(outs[0])
        if len(outs) > 1:
            refs[3 + nb][...] = second(acc).astype(outs[1])
        if emit_a:
            @pl.when(jnp.logical_and(pl.program_id(0) == 0, pl.program_id(1) == 0))
            def _():
                refs[2 + nb + len(outs)][...] = av

    in_specs = [pl.BlockSpec((tm, K), lambda j, l, i: (i, 0)),
                pl.BlockSpec((1, K, tn), lambda j, l, i: (j, 0, l))]
    args = [a, w3]
    if bias is not None:
        in_specs.append(pl.BlockSpec((1, tn), lambda j, l, i: (0, j * nl + l)))
        args.append(bias)
    out_specs = [pl.BlockSpec((tm, tn), lambda j, l, i: (i, j * nl + l)) for _ in outs]
    out_shape = [jax.ShapeDtypeStruct((M, J * n), dt) for dt in outs]
    if emit_a:
        last = M // tm - 1
        out_specs.append(pl.BlockSpec((tm, K), lambda j, l, i: (jnp.where((j == 0) & (l == 0), i, last), 0)))
        out_shape.append(jax.ShapeDtypeStruct((M, K), bf16))
    res, landed = _pcall(body, args, name=name, grid=(J, nl, M // tm), in_specs=in_specs, out_specs=out_specs,
                         out_shape=out_shape, rides=rides)
    res = res if len(res) > 1 else res[0]
    return (res, landed) if rides else res


def mm_bwd_x(dy, w3, name, *, tm=TM, epi=None, extras=(), colsum=False, out_dtype=f32, rides=()):
    M = dy.shape[0]
    J, K, n = w3.shape
    tm = min(tm, M)
    nex = len(extras)

    def body(*refs):
        dy_ref, w_hbm = refs[0], refs[1]
        ex = refs[2:2 + nex]
        o_ref = refs[2 + nex]
        w_ref = refs[-1]
        i = pl.program_id(0)

        @pl.when(i == 0)
        def _():
            pltpu.sync_copy(w_hbm, w_ref)

        acc = None
        for j in range(J):
            part = lax.dot_general(_bf(dy_ref[:, j * n:(j + 1) * n]), w_ref[j], NT_DIMS, preferred_element_type=f32)
            acc = part if acc is None else acc + part
        if epi is not None:
            acc = epi(acc, *[e[...] for e in ex])
        o_ref[...] = acc.astype(out_dtype)
        if colsum:
            s_ref = refs[3 + nex]
            cs = jnp.sum(acc, axis=0, keepdims=True)

            @pl.when(i == 0)
            def _():
                s_ref[...] = cs

            @pl.when(i > 0)
            def _():
                s_ref[...] += cs

    in_specs = [pl.BlockSpec((tm, J * n), lambda i: (i, 0)), pl.BlockSpec(memory_space=pl.ANY)]
    in_specs += [pl.BlockSpec((tm, K), lambda i: (i, 0)) for _ in extras]
    out_specs = [pl.BlockSpec((tm, K), lambda i: (i, 0))]
    out_shape = [jax.ShapeDtypeStruct((M, K), out_dtype)]
    if colsum:
        out_specs.append(pl.BlockSpec((1, K), lambda i: (0, 0)))
        out_shape.append(jax.ShapeDtypeStruct((1, K), f32))
    res, landed = _pcall(body, [dy, w3, *extras], name=name, grid=(M // tm,), in_specs=in_specs, out_specs=out_specs,
                         out_shape=out_shape, scratch_shapes=[pltpu.VMEM((J, K, n), bf16)], rides=rides)
    res = res if colsum else res[0]
    return (res, landed) if rides else res


SUB_TILES = 2


def mm_rows_fused(a, w3, name, *, transposed, tm, epi, extras=(), consts=(), outs=(f32,), nsums=0, rides=()):
    M = a.shape[0]
    J, K, n = w3.shape
    W = K if transposed else n
    tm = min(tm, M)
    nex, nco, nout = len(extras), len(consts), len(outs)

    def body(*refs):
        a_ref, w_hbm = refs[0], refs[1]
        ex = refs[2:2 + nex]
        co = refs[2 + nex:2 + nex + nco]
        o_refs = refs[2 + nex + nco:2 + nex + nco + nout]
        s_refs = refs[2 + nex + nco + nout:2 + nex + nco + nout + nsums]
        w_ref = refs[-1]
        i = pl.program_id(0)

        @pl.when(i == 0)
        def _():
            pltpu.sync_copy(w_hbm, w_ref)

        sums = None
        for h in range(SUB_TILES):
            rs = slice(h * (tm // SUB_TILES), (h + 1) * (tm // SUB_TILES))
            if transposed:
                acc = None
                for j in range(J):
                    part = lax.dot_general(_bf(a_ref[rs, j * n:(j + 1) * n]), w_ref[j], NT_DIMS,
                                           preferred_element_type=f32)
                    acc = part if acc is None else acc + part
            else:
                acc = jnp.dot(_bf(a_ref[rs, :]), w_ref[0], preferred_element_type=f32)
            rows, part_sums = epi(acc, *[e[rs, :] for e in ex], *[c[...] for c in co])
            for o_ref, v, dt in zip(o_refs, rows, outs):
                o_ref[rs, :] = v.astype(dt)
            sums = part_sums if sums is None else tuple(a_ + b_ for a_, b_ in zip(sums, part_sums))
        for s_ref, v in zip(s_refs, sums):
            @pl.when(i == 0)
            def _(s_ref=s_ref, v=v):
                s_ref[...] = v

            @pl.when(i > 0)
            def _(s_ref=s_ref, v=v):
                s_ref[...] += v

    in_specs = [_rows(tm, a.shape[1]), pl.BlockSpec(memory_space=pl.ANY)]
    in_specs += [_rows(tm, W) for _ in extras] + [_full((1, W)) for _ in consts]
    out_specs = [_rows(tm, W) for _ in outs] + [_full((1, W)) for _ in range(nsums)]
    out_shape = [jax.ShapeDtypeStruct((M, W), dt) for dt in outs] + [jax.ShapeDtypeStruct((1, W), f32)] * nsums
    res, landed = _pcall(body, [a, w3, *extras, *consts], name=name, grid=(M // tm,), in_specs=in_specs,
                         out_specs=out_specs, out_shape=out_shape, scratch_shapes=[pltpu.VMEM((J, K, n), bf16)],
                         rides=rides)
    return (res, landed) if rides else res


def mm_bwd_w(a, dy, J, name, *, tm=2048, tn=None, tk=None, rides=()):
    M, K = a.shape
    n = dy.shape[1] // J
    tm = min(tm, M)
    tn = tn or n
    tk = tk or K
    nl = n // tn
    nk = K // tk
    ns = M // tm

    def body(a_ref, dy_ref, o_ref, acc_ref):
        s = pl.program_id(3)
        part = lax.dot_general(_bf(a_ref[...]), _bf(dy_ref[...]), TN_DIMS, preferred_element_type=f32)

        @pl.when(s == 0)
        def _():
            acc_ref[...] = part

        @pl.when(s > 0)
        def _():
            acc_ref[...] += part

        @pl.when(s == ns - 1)
        def _():
            o_ref[0] = acc_ref[...].astype(bf16)

    res, landed = _pcall(
        body, [a, dy], name=name, grid=(J, nl, nk, ns),
        in_specs=[pl.BlockSpec((tm, tk), lambda j, l, k, s: (s, k)),
                  pl.BlockSpec((tm, tn), lambda j, l, k, s: (s, j * nl + l))],
        out_specs=[pl.BlockSpec((1, tk, tn), lambda j, l, k, s: (j, k, l))],
        out_shape=[jax.ShapeDtypeStruct((J, K, n), bf16)],
        scratch_shapes=[pltpu.VMEM((tk, tn), f32)], rides=rides)
    return (res[0], landed) if rides else res[0]


def _relu2(v):
    r = jnp.maximum(v, 0.0)
    return r * r


HALO = 16


def _shift_down(z, k, halo):
    r = lax.broadcasted_iota(jnp.int32, z.shape, 0)
    y = pltpu.roll(z, k, 0)
    for q in range(k):
        y = jnp.where(r == q, halo[HALO - k + q:HALO - k + q + 1, :], y)
    return y


def _shift_up(z, k, halo):
    tm = z.shape[0]
    r = lax.broadcasted_iota(jnp.int32, z.shape, 0)
    y = pltpu.roll(z, tm - k, 0)
    for q in range(k):
        y = jnp.where(r == tm - k + q, halo[q:q + 1, :], y)
    return y


def _prev_halo(tm, cb):
    return pl.BlockSpec((HALO, CONV_WIDTH), lambda i: (jnp.maximum(i * (tm // HALO) - 1, 0), cb))


def _next_halo(tm, cb, nblk):
    return pl.BlockSpec((HALO, CONV_WIDTH), lambda i: (jnp.minimum((i + 1) * (tm // HALO), nblk - 1), cb))


def _f32(ref):
    return ref[...].astype(f32)


def conv_fwd(proj, conv_w, tm=TM):
    S = proj.shape[0]

    def body(cb_ref, cc_ref, ch_ref, cch_ref, chh_ref, w_ref, a_ref):
        i = pl.program_id(0)
        z = _f32(cc_ref) * _f32(ch_ref)
        zh = jnp.where(i == 0, 0.0, _f32(cch_ref) * _f32(chh_ref))
        w = w_ref[...]
        dwz = w[0:1, :] * _shift_down(z, 2, zh) + w[1:2, :] * _shift_down(z, 1, zh) + w[2:3, :] * z
        a_ref[...] = (_f32(cb_ref) * dwz).astype(bf16)

    return pl.pallas_call(
        body, name="conv_fwd", grid=(S // tm,),
        in_specs=[_rows(tm, CONV_WIDTH, 6), _rows(tm, CONV_WIDTH, 7), _rows(tm, CONV_WIDTH, 8),
                  _prev_halo(tm, 7), _prev_halo(tm, 8), _full((3, CONV_WIDTH))],
        out_specs=_rows(tm, CONV_WIDTH),
        out_shape=jax.ShapeDtypeStruct((S, CONV_WIDTH), bf16), compiler_params=_params())(
            proj, proj, proj, proj, proj, conv_w)


def conv_bwd(da, proj, conv_w, dproj, tm=TM):
    S = proj.shape[0]
    nt = S // tm

    def body(da_ref, cb_ref, cc_ref, ch_ref, cch_ref, chh_ref, dan_ref, cbn_ref, w_ref, _, o_ref, dw_ref):
        i = pl.program_id(0)
        cc, ch, cb, dav = _f32(cc_ref), _f32(ch_ref), _f32(cb_ref), da_ref[...]
        z = cc * ch
        zh = jnp.where(i == 0, 0.0, _f32(cch_ref) * _f32(chh_ref))
        w = w_ref[...]
        z1 = _shift_down(z, 1, zh)
        z2 = _shift_down(z, 2, zh)
        dwz = w[0:1, :] * z2 + w[1:2, :] * z1 + w[2:3, :] * z
        g = dav * cb
        gn = jnp.where(i == nt - 1, 0.0, dan_ref[...] * _f32(cbn_ref))
        dz = w[2:3, :] * g + w[1:2, :] * _shift_up(g, 1, gn) + w[0:1, :] * _shift_up(g, 2, gn)
        o_ref[:, 0:CONV_WIDTH] = (dav * dwz).astype(bf16)
        o_ref[:, CONV_WIDTH:2 * CONV_WIDTH] = (dz * ch).astype(bf16)
        o_ref[:, 2 * CONV_WIDTH:3 * CONV_WIDTH] = (dz * cc).astype(bf16)

        @pl.when(i == 0)
        def _():
            dw_ref[...] = jnp.zeros_like(dw_ref)

        dw_ref[0:1, :] += jnp.sum(g * z2, axis=0, keepdims=True)
        dw_ref[1:2, :] += jnp.sum(g * z1, axis=0, keepdims=True)
        dw_ref[2:3, :] += jnp.sum(g * z, axis=0, keepdims=True)

    return pl.pallas_call(
        body, name="conv_bwd", grid=(nt,),
        in_specs=[_rows(tm, CONV_WIDTH), _rows(tm, CONV_WIDTH, 6), _rows(tm, CONV_WIDTH, 7), _rows(tm, CONV_WIDTH, 8),
                  _prev_halo(tm, 7), _prev_halo(tm, 8),
                  _next_halo(tm, 0, S // HALO), _next_halo(tm, 6, S // HALO), _full((3, CONV_WIDTH)),
                  pl.BlockSpec(memory_space=pl.ANY)],
        out_specs=[_rows(tm, 3 * CONV_WIDTH, GATE_COLS // (3 * CONV_WIDTH)), _full((3, CONV_WIDTH))],
        out_shape=[jax.ShapeDtypeStruct(dproj.shape, bf16), jax.ShapeDtypeStruct((3, CONV_WIDTH), f32)],
        input_output_aliases={9: 0},
        compiler_params=_params())(da, proj, proj, proj, proj, proj, da, proj, conv_w, dproj)


def _cmul(ar, ai, br, bi):
    return ar * br - ai * bi, ar * bi + ai * br


def ssm_prep(lr, li, ldt, bt_re, bt_im):
    def body(lr_ref, li_ref, ldt_ref, br_ref, bi_ref, ar_ref, ai_ref, kr_ref, ki_ref, bbr_ref, bbi_ref):
        lrv, liv = lr_ref[...], li_ref[...]
        dt = jnp.exp(ldt_ref[...])
        mag = jnp.exp(lrv * dt)
        ar = mag * jnp.cos(liv * dt)
        ai = mag * jnp.sin(liv * dt)
        den = lrv * lrv + liv * liv
        nr = ar - 1.0
        kr = (nr * lrv + ai * liv) / den
        ki = (ai * lrv - nr * liv) / den
        ar_ref[...], ai_ref[...], kr_ref[...], ki_ref[...] = ar, ai, kr, ki
        bbr_ref[...] = kr * br_ref[...] - ki * bi_ref[...]
        bbi_ref[...] = kr * bi_ref[...] + ki * br_ref[...]

    v = jax.ShapeDtypeStruct((1, N_STATES), f32)
    m = jax.ShapeDtypeStruct((SSM_GROUP, N_STATES), f32)
    return pl.pallas_call(body, name="ssm_prep", out_shape=[v, v, v, v, m, m])(lr, li, ldt, bt_re, bt_im)


def _pow_segment(ar, ai, seg_len):
    pr, pi = ar, ai
    for _ in range(int(math.log2(seg_len))):
        pr, pi = _cmul(pr, pi, pr, pi)
    return pr, pi


GELU_K = math.sqrt(2.0 / math.pi)
GELU_C = 0.044715


def _gelu(v):
    return 0.5 * v * (1.0 + jnp.tanh(GELU_K * (v + GELU_C * v * v * v)))


def _gelu_grad(v):
    t = jnp.tanh(GELU_K * (v + GELU_C * v * v * v))
    return 0.5 * (1.0 + t) + 0.5 * v * (1.0 - t * t) * GELU_K * (1.0 + 3.0 * GELU_C * v * v)


def ssm_scan_fwd(u_perm, bd_r, bd_i, ar, ai, *, final=None, rides=()):
    S = u_perm.shape[0]
    R = SCAN_ROWS
    nblk = S // R
    seg_len = S // N_SEG
    nstrip = N_STATES // LANE_STRIP
    store = final is not None

    def body(*refs):
        if store:
            (u_ref, bdr_ref, bdi_ref, ar_ref, ai_ref, cdr_ref, cdi_ref, d_ref, ir_ref, ii_ref,
             y_ref, ys_ref, sr_ref, si_ref, bur, bui, car, cai) = refs
        else:
            u_ref, bdr_ref, bdi_ref, ar_ref, ai_ref, or_ref, oi_ref, bur, bui, car, cai = refs
        i = pl.program_id(0)
        ub = _bf(u_ref[...])
        u = ub.astype(f32)
        for cb in range(4):
            us = ub[:, cb * 128:(cb + 1) * 128]
            bur[:, cb * 512:(cb + 1) * 512] = jnp.dot(us, bdr_ref[cb], preferred_element_type=f32)
            bui[:, cb * 512:(cb + 1) * 512] = jnp.dot(us, bdi_ref[cb], preferred_element_type=f32)

        @pl.when(i == 0)
        def _():
            if store:
                car[...] = ir_ref[...]
                cai[...] = ii_ref[...]
            else:
                car[...] = jnp.zeros_like(car)
                cai[...] = jnp.zeros_like(cai)

        for ls in range(nstrip):
            lanes = pl.ds(ls * LANE_STRIP, LANE_STRIP)
            a_r = jnp.broadcast_to(ar_ref[:, lanes], (N_SEG, LANE_STRIP))
            a_i = jnp.broadcast_to(ai_ref[:, lanes], (N_SEG, LANE_STRIP))

            def step(t, carry, lanes=lanes, a_r=a_r, a_i=a_i):
                s_r, s_i = carry
                row = pl.multiple_of(t * 8, 8)
                n_r = a_r * s_r - a_i * s_i + bur[pl.ds(row, 8), lanes]
                n_i = a_r * s_i + a_i * s_r + bui[pl.ds(row, 8), lanes]
                if store:
                    sr_ref[pl.ds(row, 8), lanes] = n_r
                    si_ref[pl.ds(row, 8), lanes] = n_i
                return n_r, n_i

            e_r, e_i = lax.fori_loop(0, R // 8, step, (car[:, lanes], cai[:, lanes]), unroll=True)
            car[:, lanes] = e_r
            cai[:, lanes] = e_i

        if store:
            for cb in range(4):
                st_r = sr_ref[:, cb * 512:(cb + 1) * 512].astype(bf16)
                st_i = si_ref[:, cb * 512:(cb + 1) * 512].astype(bf16)
                y = (jnp.dot(st_r, cdr_ref[cb], preferred_element_type=f32)
                     - jnp.dot(st_i, cdi_ref[cb], preferred_element_type=f32))
                cols = slice(cb * 128, (cb + 1) * 128)
                y = y + d_ref[:, cols] * u[:, cols]
                y_ref[:, cols] = y
                ys_ref[:, cols] = _gelu(y).astype(bf16)
        else:
            @pl.when(i == nblk - 1)
            def _():
                p_r, p_i = _pow_segment(ar_ref[...], ai_ref[...], seg_len)
                t_r, t_i = car[0:1, :], cai[0:1, :]
                or_ref[0:1, :] = jnp.zeros((1, N_STATES), f32)
                oi_ref[0:1, :] = jnp.zeros((1, N_STATES), f32)
                for j in range(1, N_SEG):
                    or_ref[j:j + 1, :] = t_r
                    oi_ref[j:j + 1, :] = t_i
                    m_r, m_i = _cmul(p_r, p_i, t_r, t_i)
                    t_r, t_i = car[j:j + 1, :] + m_r, cai[j:j + 1, :] + m_i

    blk = lambda w: pl.BlockSpec((R, w), lambda i: (i, 0))
    in_specs = [blk(SSM_WIDTH), _full((4, 128, 512)), _full((4, 128, 512)), _full((1, N_STATES)), _full((1, N_STATES))]
    args = [u_perm, bd_r, bd_i, ar, ai]
    scratch = [pltpu.VMEM((R, N_STATES), f32), pltpu.VMEM((R, N_STATES), f32),
               pltpu.VMEM((N_SEG, N_STATES), f32), pltpu.VMEM((N_SEG, N_STATES), f32)]
    if store:
        in_specs += [_full((4, 512, 128)), _full((4, 512, 128)), _full((1, SSM_WIDTH)),
                     _full((N_SEG, N_STATES)), _full((N_SEG, N_STATES))]
        args += list(final)
        out_specs = [blk(SSM_WIDTH), blk(SSM_WIDTH), blk(N_STATES), blk(N_STATES)]
        out_shape = [jax.ShapeDtypeStruct((S, SSM_WIDTH), f32), jax.ShapeDtypeStruct((S, SSM_WIDTH), bf16),
                     jax.ShapeDtypeStruct((S, N_STATES), f32), jax.ShapeDtypeStruct((S, N_STATES), f32)]
        name = "ssm_scan_fwd"
    else:
        out_specs = [_full((N_SEG, N_STATES)), _full((N_SEG, N_STATES))]
        out_shape = [jax.ShapeDtypeStruct((N_SEG, N_STATES), f32)] * 2
        name = "ssm_scan_fwd_carry"
    res, landed = _pcall(body, args, name=name, grid=(nblk,), in_specs=in_specs, out_specs=out_specs,
                         out_shape=out_shape, scratch_shapes=scratch, rides=rides)
    return (res, landed) if rides else res


def ssm_scan_bwd(dys_perm, y_perm, cdt_r, cdt_i, ar, ai, *, final=None, rides=()):
    S = dys_perm.shape[0]
    R = SCAN_ROWS
    nblk = S // R
    seg_len = S // N_SEG
    nstrip = N_STATES // LANE_STRIP
    store = final is not None

    def body(*refs):
        if store:
            (dys_ref, y_ref, cdr_ref, cdi_ref, ar_ref, ai_ref, u_ref, sr_ref, si_ref, bdr_ref, bdi_ref, d_ref, ir_ref,
             ii_ref, du_ref, dbr_ref, dbi_ref, dcr_ref, dci_ref, dar_ref, dai_ref, dd_ref, dsr, dsi, lmr, lmi, car,
             cai) = refs
        else:
            dys_ref, y_ref, cdr_ref, cdi_ref, ar_ref, ai_ref, or_ref, oi_ref, dsr, dsi, car, cai = refs
        i = pl.program_id(0)
        dy = dys_ref[...] * _gelu_grad(y_ref[...])
        dyb = dy.astype(bf16)
        for cb in range(4):
            ds_ = dyb[:, cb * 128:(cb + 1) * 128]
            dsr[:, cb * 512:(cb + 1) * 512] = jnp.dot(ds_, cdr_ref[cb], preferred_element_type=f32)
            dsi[:, cb * 512:(cb + 1) * 512] = -jnp.dot(ds_, cdi_ref[cb], preferred_element_type=f32)

        @pl.when(i == 0)
        def _():
            if store:
                car[...] = ir_ref[...]
                cai[...] = ii_ref[...]
                dar_ref[...] = jnp.zeros_like(dar_ref)
                dai_ref[...] = jnp.zeros_like(dai_ref)
                dbr_ref[...] = jnp.zeros_like(dbr_ref)
                dbi_ref[...] = jnp.zeros_like(dbi_ref)
                dcr_ref[...] = jnp.zeros_like(dcr_ref)
                dci_ref[...] = jnp.zeros_like(dci_ref)
                dd_ref[...] = jnp.zeros_like(dd_ref)
            else:
                car[...] = jnp.zeros_like(car)
                cai[...] = jnp.zeros_like(cai)

        for ls in range(nstrip):
            lanes = pl.ds(ls * LANE_STRIP, LANE_STRIP)
            a_r = jnp.broadcast_to(ar_ref[:, lanes], (N_SEG, LANE_STRIP))
            a_i = jnp.broadcast_to(ai_ref[:, lanes], (N_SEG, LANE_STRIP))
            zero = jnp.zeros((N_SEG, LANE_STRIP), f32)

            def step(k, carry, lanes=lanes, a_r=a_r, a_i=a_i):
                l_r, l_i, g_r, g_i = carry
                row = pl.multiple_of((R // 8 - 1 - k) * 8, 8)
                if store:
                    s_r = sr_ref[pl.ds(row, 8), lanes]
                    s_i = si_ref[pl.ds(row, 8), lanes]
                    g_r = g_r + l_r * s_r + l_i * s_i
                    g_i = g_i + l_i * s_r - l_r * s_i
                n_r = dsr[pl.ds(row, 8), lanes] + a_r * l_r + a_i * l_i
                n_i = dsi[pl.ds(row, 8), lanes] + a_r * l_i - a_i * l_r
                if store:
                    lmr[pl.ds(row, 8), lanes] = n_r
                    lmi[pl.ds(row, 8), lanes] = n_i
                return n_r, n_i, g_r, g_i

            e_r, e_i, g_r, g_i = lax.fori_loop(0, R // 8, step, (car[:, lanes], cai[:, lanes], zero, zero),
                                               unroll=True)
            car[:, lanes] = e_r
            cai[:, lanes] = e_i
            if store:
                dar_ref[:, lanes] += g_r
                dai_ref[:, lanes] += g_i

        if store:
            ub = _bf(u_ref[...])
            u = ub.astype(f32)
            for cb in range(4):
                cols = slice(cb * 128, (cb + 1) * 128)
                st = slice(cb * 512, (cb + 1) * 512)
                l_r = lmr[:, st].astype(bf16)
                l_i = lmi[:, st].astype(bf16)
                du = (jnp.dot(l_r, bdr_ref[cb], preferred_element_type=f32)
                      + jnp.dot(l_i, bdi_ref[cb], preferred_element_type=f32))
                du_ref[:, cols] = (du + d_ref[:, cols] * dy[:, cols]).astype(bf16)
                dbr_ref[cb] += lax.dot_general(l_r, ub[:, cols], TN_DIMS, preferred_element_type=f32)
                dbi_ref[cb] += lax.dot_general(l_i, ub[:, cols], TN_DIMS, preferred_element_type=f32)
                dcr_ref[cb] += lax.dot_general(dyb[:, cols], sr_ref[:, st].astype(bf16), TN_DIMS,
                                               preferred_element_type=f32)
                dci_ref[cb] += lax.dot_general(dyb[:, cols], si_ref[:, st].astype(bf16), TN_DIMS,
                                               preferred_element_type=f32)
            dd_ref[...] += jnp.sum(dy * u, axis=0, keepdims=True)
        else:
            @pl.when(i == nblk - 1)
            def _():
                p_r, p_i = _pow_segment(ar_ref[...], ai_ref[...], seg_len)
                p_i = -p_i
                t_r, t_i = car[N_SEG - 1:N_SEG, :], cai[N_SEG - 1:N_SEG, :]
                or_ref[N_SEG - 1:N_SEG, :] = jnp.zeros((1, N_STATES), f32)
                oi_ref[N_SEG - 1:N_SEG, :] = jnp.zeros((1, N_STATES), f32)
                for j in range(N_SEG - 2, -1, -1):
                    or_ref[j:j + 1, :] = t_r
                    oi_ref[j:j + 1, :] = t_i
                    m_r, m_i = _cmul(p_r, p_i, t_r, t_i)
                    t_r, t_i = car[j:j + 1, :] + m_r, cai[j:j + 1, :] + m_i

    blk = lambda w: pl.BlockSpec((R, w), lambda i: (nblk - 1 - i, 0))
    in_specs = [blk(SSM_WIDTH), blk(SSM_WIDTH), _full((4, 128, 512)), _full((4, 128, 512)), _full((1, N_STATES)),
                _full((1, N_STATES))]
    args = [dys_perm, y_perm, cdt_r, cdt_i, ar, ai]
    seg = jax.ShapeDtypeStruct((N_SEG, N_STATES), f32)
    if store:
        in_specs += [blk(SSM_WIDTH), blk(N_STATES), blk(N_STATES), _full((4, 512, 128)), _full((4, 512, 128)),
                     _full((1, SSM_WIDTH)), _full((N_SEG, N_STATES)), _full((N_SEG, N_STATES))]
        args += list(final)
        out_specs = [blk(SSM_WIDTH), _full((4, 512, 128)), _full((4, 512, 128)), _full((4, 128, 512)),
                     _full((4, 128, 512)), _full((N_SEG, N_STATES)), _full((N_SEG, N_STATES)), _full((1, SSM_WIDTH))]
        b_acc = jax.ShapeDtypeStruct((4, 512, 128), f32)
        c_acc = jax.ShapeDtypeStruct((4, 128, 512), f32)
        out_shape = [jax.ShapeDtypeStruct((S, SSM_WIDTH), bf16), b_acc, b_acc, c_acc, c_acc, seg, seg,
                     jax.ShapeDtypeStruct((1, SSM_WIDTH), f32)]
        scratch = [pltpu.VMEM((R, N_STATES), f32)] * 4 + [pltpu.VMEM((N_SEG, N_STATES), f32)] * 2
        name = "ssm_scan_bwd"
    else:
        out_specs = [_full((N_SEG, N_STATES)), _full((N_SEG, N_STATES))]
        out_shape = [seg, seg]
        scratch = [pltpu.VMEM((R, N_STATES), f32)] * 2 + [pltpu.VMEM((N_SEG, N_STATES), f32)] * 2
        name = "ssm_scan_bwd_carry"
    res, landed = _pcall(body, args, name=name, grid=(nblk,), in_specs=in_specs, out_specs=out_specs,
                         out_shape=out_shape, scratch_shapes=scratch, rides=rides)
    return (res, landed) if rides else res


def ssm_param_bwd(dbb_r, dbb_i, bt_re, bt_im, kr, ki, ar, ai, lr, li, ldt, da_r, da_i):
    def body(dbr_ref, dbi_ref, br_ref, bi_ref, kr_ref, ki_ref, ar_ref, ai_ref, lr_ref, li_ref, ldt_ref, dar_ref,
             dai_ref, obr_ref, obi_ref, olr_ref, oli_ref, odt_ref):
        dbr, dbi, b_r, b_i = dbr_ref[...], dbi_ref[...], br_ref[...], bi_ref[...]
        k_r, k_i, a_r, a_i = kr_ref[...], ki_ref[...], ar_ref[...], ai_ref[...]
        l_r, l_i = lr_ref[...], li_ref[...]
        dt = jnp.exp(ldt_ref[...])
        obr_ref[...] = k_r * dbr + k_i * dbi
        obi_ref[...] = k_r * dbi - k_i * dbr
        gk_r = jnp.sum(dbr * b_r + dbi * b_i, axis=0, keepdims=True)
        gk_i = jnp.sum(dbi * b_r - dbr * b_i, axis=0, keepdims=True)
        ga_r = jnp.sum(dar_ref[...], axis=0, keepdims=True)
        ga_i = jnp.sum(dai_ref[...], axis=0, keepdims=True)
        den = l_r * l_r + l_i * l_i
        c_r, c_i = l_r / den, l_i / den
        m_r, m_i = _cmul(c_r, c_i, gk_r, gk_i)
        g_r, g_i = ga_r + m_r, ga_i + m_i
        t1_r, t1_i = _cmul(dt * a_r, -dt * a_i, g_r, g_i)
        q_r, q_i = _cmul(k_r, k_i, c_r, -c_i)
        t2_r, t2_i = _cmul(-q_r, q_i, gk_r, gk_i)
        olr_ref[...] = t1_r + t2_r
        oli_ref[...] = t1_i + t2_i
        w_r, w_i = _cmul(l_r, l_i, a_r, a_i)
        odt_ref[...] = dt * (w_r * g_r + w_i * g_i)

    v = jax.ShapeDtypeStruct((1, N_STATES), f32)
    m = jax.ShapeDtypeStruct((SSM_GROUP, N_STATES), f32)
    return pl.pallas_call(body, name="ssm_param_bwd", out_shape=[m, m, v, v, v])(
        dbb_r, dbb_i, bt_re, bt_im, kr, ki, ar, ai, lr, li, ldt, da_r, da_i)


def group_sum(v):
    def body(v_ref, o_ref):
        o_ref[...] = jnp.sum(v_ref[...], axis=-1, keepdims=True)
    return pl.pallas_call(body, name="group_sum", out_shape=jax.ShapeDtypeStruct((v.shape[0], 1), f32))(v)


ATTN_SCALE = XATTN_HEAD_DIM ** -0.5


def _attn_probs(q_h, k_h):
    s = lax.dot_general(q_h, k_h, NT_DIMS, preferred_element_type=f32) * ATTN_SCALE
    e = jnp.exp(s - jnp.max(s, axis=-1, keepdims=True))
    return e / jnp.sum(e, axis=-1, keepdims=True)


def attn_fwd(proj, kv, tm=TM):
    S = proj.shape[0]
    M = kv.shape[0]

    def body(q_ref, kv_ref, o_ref):
        for h in range(XATTN_HEADS):
            cols = slice(h * XATTN_HEAD_DIM, (h + 1) * XATTN_HEAD_DIM)
            q_h = q_ref[:, cols].astype(bf16)
            k_h = kv_ref[:, cols]
            v_h = kv_ref[:, XATTN_WIDTH + h * XATTN_HEAD_DIM:XATTN_WIDTH + (h + 1) * XATTN_HEAD_DIM]
            p = _attn_probs(q_h, k_h)
            o_ref[:, cols] = jnp.dot(p.astype(bf16), v_h, preferred_element_type=f32).astype(bf16)

    return pl.pallas_call(
        body, name="attn_fwd", grid=(S // tm,), in_specs=[_rows(tm, XATTN_WIDTH, 10), _full((M, 2 * XATTN_WIDTH))],
        out_specs=_rows(tm, XATTN_WIDTH), out_shape=jax.ShapeDtypeStruct((S, XATTN_WIDTH), bf16),
        compiler_params=_params())(proj, kv)


def attn_bwd(do, proj, kv, dproj, tm=TM):
    S = proj.shape[0]
    M = kv.shape[0]

    def body(do_ref, q_ref, kv_ref, _, dq_ref, dkv_ref):
        i = pl.program_id(0)

        @pl.when(i == 0)
        def _():
            dkv_ref[...] = jnp.zeros_like(dkv_ref)

        for h in range(XATTN_HEADS):
            cols = slice(h * XATTN_HEAD_DIM, (h + 1) * XATTN_HEAD_DIM)
            vcols = slice(XATTN_WIDTH + h * XATTN_HEAD_DIM, XATTN_WIDTH + (h + 1) * XATTN_HEAD_DIM)
            q_h = q_ref[:, cols].astype(bf16)
            k_h = kv_ref[:, cols]
            v_h = kv_ref[:, vcols]
            do_h = do_ref[:, cols]
            p = _attn_probs(q_h, k_h)
            dp = lax.dot_general(do_h, v_h, NT_DIMS, preferred_element_type=f32)
            ds = (p * (dp - jnp.sum(dp * p, axis=-1, keepdims=True)) * ATTN_SCALE).astype(bf16)
            dq_ref[:, cols] = jnp.dot(ds, k_h, preferred_element_type=f32).astype(bf16)
            dkv_ref[:, cols] += lax.dot_general(ds, q_h, TN_DIMS, preferred_element_type=f32)
            dkv_ref[:, vcols] += lax.dot_general(p.astype(bf16), do_h, TN_DIMS, preferred_element_type=f32)

    return pl.pallas_call(
        body, name="attn_bwd", grid=(S // tm,),
        in_specs=[_rows(tm, XATTN_WIDTH), _rows(tm, XATTN_WIDTH, 10), _full((M, 2 * XATTN_WIDTH)),
                  pl.BlockSpec(memory_space=pl.ANY)],
        out_specs=[_rows(tm, XATTN_WIDTH, 10), _full((M, 2 * XATTN_WIDTH))],
        out_shape=[jax.ShapeDtypeStruct(dproj.shape, bf16), jax.ShapeDtypeStruct((M, 2 * XATTN_WIDTH), f32)],
        input_output_aliases={3: 0},
        compiler_params=_params())(do, proj, kv, dproj)


def merge_fwd(proj, b_gate, y_a, glu, y_c, tm=TM, rides=()):
    S = proj.shape[0]

    def body(g0_ref, g1_ref, g2_ref, b_ref, ya_ref, ga_ref, gb_ref, yc_ref, o_ref):
        b = b_ref[...]
        g0 = _sig(_f32(g0_ref) + b[:, 0:D_MODEL])
        g1 = _sig(_f32(g1_ref) + b[:, D_MODEL:2 * D_MODEL])
        g2 = _sig(_f32(g2_ref) + b[:, 2 * D_MODEL:3 * D_MODEL])
        y_b = ga_ref[...].astype(f32) * _sig(gb_ref[...].astype(f32))
        o_ref[...] = (g0 * ya_ref[...].astype(f32) + g1 * y_b + g2 * yc_ref[...].astype(f32)).astype(bf16)

    res, landed = _pcall(
        body, [proj, proj, proj, b_gate, y_a, glu, glu, y_c], name="merge_fwd", grid=(S // tm,),
        in_specs=[_rows(tm, D_MODEL, 0), _rows(tm, D_MODEL, 1), _rows(tm, D_MODEL, 2), _full((1, GATE_COLS)),
                  _rows(tm, D_MODEL), _rows(tm, D_MODEL, 0), _rows(tm, D_MODEL, 1), _rows(tm, D_MODEL)],
        out_specs=[_rows(tm, D_MODEL)], out_shape=[jax.ShapeDtypeStruct((S, D_MODEL), bf16)], rides=rides)
    return (res[0], landed) if rides else res[0]


def merge_bwd(dm, proj, b_gate, y_a, glu, y_c, tm=256, rides=()):
    S = proj.shape[0]

    def body(dm_ref, g0_ref, g1_ref, g2_ref, b_ref, ya_ref, ga_ref, gb_ref, yc_ref,
             dg_ref, dya_ref, dgl_ref, dyc_ref, dbg_ref):
        i = pl.program_id(0)
        b = b_ref[...]
        dmv = dm_ref[...].astype(f32)
        ga, sb = ga_ref[...].astype(f32), _sig(gb_ref[...].astype(f32))
        ys = (ya_ref[...].astype(f32), ga * sb, yc_ref[...].astype(f32))
        gs = tuple(_sig(_f32(r) + b[:, k * D_MODEL:(k + 1) * D_MODEL]) for k, r in enumerate((g0_ref, g1_ref, g2_ref)))

        @pl.when(i == 0)
        def _():
            dbg_ref[...] = jnp.zeros_like(dbg_ref)

        for k in range(3):
            dpre = dmv * ys[k] * gs[k] * (1.0 - gs[k])
            dg_ref[:, k * D_MODEL:(k + 1) * D_MODEL] = dpre.astype(bf16)
            dbg_ref[:, k * D_MODEL:(k + 1) * D_MODEL] += jnp.sum(dpre, axis=0, keepdims=True)
        dya_ref[...] = (dmv * gs[0]).astype(bf16)
        dyc_ref[...] = (dmv * gs[2]).astype(bf16)
        dyb = dmv * gs[1]
        dgl_ref[:, 0:D_MODEL] = (dyb * sb).astype(bf16)
        dgl_ref[:, D_MODEL:2 * D_MODEL] = (dyb * ga * sb * (1.0 - sb)).astype(bf16)

    res, landed = _pcall(
        body, [dm, proj, proj, proj, b_gate, y_a, glu, glu, y_c], name="merge_bwd", grid=(S // tm,),
        in_specs=[_rows(tm, D_MODEL), _rows(tm, D_MODEL, 0), _rows(tm, D_MODEL, 1), _rows(tm, D_MODEL, 2),
                  _full((1, GATE_COLS)), _rows(tm, D_MODEL), _rows(tm, D_MODEL, 0), _rows(tm, D_MODEL, 1),
                  _rows(tm, D_MODEL)],
        out_specs=[_rows(tm, GATE_COLS), _rows(tm, D_MODEL), _rows(tm, 2 * D_MODEL), _rows(tm, D_MODEL),
                   _full((1, GATE_COLS))],
        out_shape=[jax.ShapeDtypeStruct((S, IN_COLS), bf16), jax.ShapeDtypeStruct((S, D_MODEL), bf16),
                   jax.ShapeDtypeStruct((S, 2 * D_MODEL), bf16), jax.ShapeDtypeStruct((S, D_MODEL), bf16),
                   jax.ShapeDtypeStruct((1, GATE_COLS), f32)], rides=rides)
    return (res, landed) if rides else res


def _ln_stats(r):
    mu = jnp.mean(r, axis=-1, keepdims=True)
    xc = r - mu
    var = jnp.mean(xc * xc, axis=-1, keepdims=True)
    rstd = lax.rsqrt(var + LN_EPS)
    return xc * rstd, rstd


def _ln_bwd(dy, xhat, rstd, g):
    dxh = dy * g
    return rstd * (dxh - jnp.mean(dxh, axis=-1, keepdims=True) - xhat * jnp.mean(dxh * xhat, axis=-1, keepdims=True))


def _colsum(v):
    return jnp.sum(v, axis=0, keepdims=True)


def _ln2_loss_epi(t2, h1, target, b_down, g, b):
    xhat, rstd = _ln_stats(ALPHA * h1 + t2 + b_down)
    err = xhat * g + b - target
    dout = err * (1.0 / D_MODEL)
    dr = _ln_bwd(dout, xhat, rstd, g)
    return (dr, dr), (0.5 * _colsum(err * err) * (1.0 / D_MODEL), _colsum(dout * xhat), _colsum(dout), _colsum(dr))


def _ln1_fwd_epi(t1, x, g, b):
    xhat, _ = _ln_stats(ALPHA * x + t1)
    h = xhat * g + b
    return (t1, h, h), ()


def _ln1_bwd_epi(t3, x, t1, dr2, g):
    xhat, rstd = _ln_stats(ALPHA * x + t1)
    dh = ALPHA * dr2 + t3
    dr = _ln_bwd(dh, xhat, rstd, g)
    return (dr, dr), (_colsum(dh * xhat), _colsum(dh))


def exchange(*rides, name):
    return _pcall(lambda: None, [], name=name, grid=(), in_specs=[], out_specs=[], out_shape=[], rides=rides)[1]


def gather_two_level(shard, name):
    K, n = shard.shape
    h = K // 2

    def body(in_ref, out_ref, ici_send, ici_recv, d2d_send, d2d_recv, local_sem):
        x, y, c = lax.axis_index("x"), lax.axis_index("y"), lax.axis_index("c")
        me = 2 * x + y
        mine = pl.ds(pl.multiple_of(c * h, 16), h)
        theirs = pl.ds(pl.multiple_of((1 - c) * h, 16), h)
        peers = _xy_peers(x, y)

        def over_ici(k, dst):
            return pltpu.make_async_remote_copy(src_ref=in_ref.at[mine], dst_ref=dst, send_sem=ici_send.at[k],
                                                recv_sem=ici_recv.at[k], device_id=(*peers[k], c), device_id_type=MESH)

        def over_d2d(k, rows):
            blk = out_ref.at[2 * peers[k][0] + peers[k][1], rows]
            return pltpu.make_async_remote_copy(src_ref=blk, dst_ref=blk, send_sem=d2d_send.at[k],
                                                recv_sem=d2d_recv.at[k], device_id=(x, y, 1 - c), device_id_type=MESH)

        local = pltpu.make_async_copy(in_ref, out_ref.at[me], local_sem)
        local.start()
        started = [over_ici(k, out_ref.at[me, mine]) for k in range(3)]
        for cp in started:
            cp.start()
        for k in range(3):
            over_ici(k, out_ref.at[2 * peers[k][0] + peers[k][1], mine]).wait_recv()
            passed = over_d2d(k, mine)
            passed.start()
            started.append(passed)
        for k in range(3):
            over_d2d(k, theirs).wait_recv()
        for cp in started:
            cp.wait_send()
        local.wait()

    return pl.pallas_call(
        body, name=name, in_specs=[HBM_SPEC], out_specs=HBM_SPEC,
        out_shape=jax.ShapeDtypeStruct((4, K, n), shard.dtype),
        scratch_shapes=[pltpu.SemaphoreType.DMA((3,))] * 4 + [pltpu.SemaphoreType.DMA(())])(shard)


ROW_TILE = 256


def sum_slots(recv, name):
    n, R, C = recv.shape
    tm = min(R, ROW_TILE)

    def body(r_ref, o_ref):
        acc = r_ref[0].astype(f32)
        for k in range(1, n):
            acc = acc + r_ref[k].astype(f32)
        o_ref[...] = acc

    return pl.pallas_call(
        body, name=name, grid=(R // tm,), in_specs=[pl.BlockSpec((n, tm, C), lambda i: (0, i, 0))],
        out_specs=_rows(tm, C), out_shape=jax.ShapeDtypeStruct((R, C), f32), compiler_params=_params())(recv)


def _adamw(w, g, m, v):
    m = ADAM_B1 * m + (1.0 - ADAM_B1) * g
    v = ADAM_B2 * v + (1.0 - ADAM_B2) * (g * g)
    m_hat = m / (1.0 - ADAM_B1 ** ADAM_STEP)
    v_hat = v / (1.0 - ADAM_B2 ** ADAM_STEP)
    delta = -ADAM_LR * (m_hat / (jnp.sqrt(v_hat) + ADAM_EPS) + ADAM_WD * w)
    return delta, m, v


def adam_pair(p, q, w, m, v, name):
    R, C = w.shape
    tm = min(R, ROW_TILE)

    def body(p_ref, q_ref, w_ref, m_ref, v_ref, g_ref, d_ref, nm_ref, nv_ref):
        g = p_ref[...] + q_ref[...]
        g_ref[...] = g
        d_ref[...], nm_ref[...], nv_ref[...] = _adamw(w_ref[...], g, m_ref[...], v_ref[...])

    o = jax.ShapeDtypeStruct((R, C), f32)
    return pl.pallas_call(body, name=name, grid=(R // tm,), in_specs=[_rows(tm, C)] * 5,
                          out_specs=[_rows(tm, C)] * 4, out_shape=[o] * 4, compiler_params=_params())(p, q, w, m, v)


def adam_slots(recv, w, m, v, name):
    n = recv.shape[0]

    def body(r_ref, w_ref, m_ref, v_ref, g_ref, d_ref, nm_ref, nv_ref):
        g = r_ref[0]
        for k in range(1, n):
            g = g + r_ref[k]
        g_ref[...] = g
        d_ref[...], nm_ref[...], nv_ref[...] = _adamw(w_ref[...], g, m_ref[...], v_ref[...])

    o = jax.ShapeDtypeStruct(w.shape, f32)
    return pl.pallas_call(body, name=name, out_shape=[o] * 4, compiler_params=_params())(recv, w, m, v)


BIG = ("w_in", "w_conv_out", "w_glu", "w_xattn_out", "w_kv", "w_out", "w_up", "w_down")
MID = ("w_conv_out", "w_glu", "w_xattn_out", "w_kv", "w_out")
SMALL = ("b_gate", "ssm_lam_re", "ssm_lam_im", "ssm_log_dt", "ssm_b_re", "ssm_b_im", "ssm_c_re", "ssm_c_im", "ssm_d",
         "ln1_g", "ln1_b", "b_up", "b_down", "ln2_g", "ln2_b")
def _pad_flat(a, mult=1024):
    a = a.reshape(-1)
    return jnp.pad(a, (0, (-a.shape[0]) % mult))


def _perm(a):
    S, W = a.shape
    return a.reshape(N_SEG, S // N_SEG, W).transpose(1, 0, 2).reshape(S, W)


def _unperm(a):
    S, W = a.shape
    return a.reshape(S // N_SEG, N_SEG, W).transpose(1, 0, 2).reshape(S, W)


def _state_rows(a):
    return a.transpose(2, 0, 1).reshape(SSM_GROUP, N_STATES)


def _block_diag_b(bt):
    b4 = bt.reshape(SSM_GROUP, 4, 8, SSM_STATE)
    eye = jnp.eye(8, dtype=bt.dtype)
    return jnp.einsum("hcgp,gk->cghkp", b4, eye).reshape(4, 128, 512)


def _block_diag_c(c):
    c4 = c.reshape(4, 8, SSM_GROUP, SSM_STATE)
    eye = jnp.eye(8, dtype=c.dtype)
    return jnp.einsum("cghp,gk->cgpkh", c4, eye).reshape(4, 512, 128)


def _diag_b(acc):
    a = acc.reshape(4, 8, SSM_STATE, 8, SSM_GROUP)
    eye = jnp.eye(8, dtype=acc.dtype)
    return jnp.einsum("cgpkh,gk->hcgp", a, eye).reshape(SSM_GROUP, N_STATES)


def _diag_c(acc):
    a = acc.reshape(4, 8, SSM_GROUP, 8, SSM_STATE)
    eye = jnp.eye(8, dtype=acc.dtype)
    return jnp.einsum("cghkp,gk->cghp", a, eye).reshape(SSM_GROUPS, SSM_GROUP, SSM_STATE)


def kernel(x, mem, w_in, b_gate, conv_w, w_conv_out, ssm_lam_re, ssm_lam_im, ssm_log_dt, ssm_b_re, ssm_b_im, ssm_c_re, ssm_c_im, ssm_d, w_glu, w_kv, w_xattn_out, w_out, ln1_g, ln1_b, w_up, b_up, w_down, b_down, ln2_g, ln2_b, loss_target, m_w_in, m_b_gate, m_conv_w, m_w_conv_out, m_ssm_lam_re, m_ssm_lam_im, m_ssm_log_dt, m_ssm_b_re, m_ssm_b_im, m_ssm_c_re, m_ssm_c_im, m_ssm_d, m_w_glu, m_w_kv, m_w_xattn_out, m_w_out, m_ln1_g, m_ln1_b, m_w_up, m_b_up, m_w_down, m_b_down, m_ln2_g, m_ln2_b, v_w_in, v_b_gate, v_conv_w, v_w_conv_out, v_ssm_lam_re, v_ssm_lam_im, v_ssm_log_dt, v_ssm_b_re, v_ssm_b_im, v_ssm_c_re, v_ssm_c_im, v_ssm_d, v_w_glu, v_w_kv, v_w_xattn_out, v_w_out, v_ln1_g, v_ln1_b, v_w_up, v_b_up, v_w_down, v_b_down, v_ln2_g, v_ln2_b):
    W = dict(w_in=w_in, b_gate=b_gate, conv_w=conv_w, w_conv_out=w_conv_out, ssm_lam_re=ssm_lam_re,
             ssm_lam_im=ssm_lam_im, ssm_log_dt=ssm_log_dt, ssm_b_re=ssm_b_re, ssm_b_im=ssm_b_im, ssm_c_re=ssm_c_re,
             ssm_c_im=ssm_c_im, ssm_d=ssm_d, w_glu=w_glu, w_kv=w_kv, w_xattn_out=w_xattn_out, w_out=w_out,
             ln1_g=ln1_g, ln1_b=ln1_b, w_up=w_up, b_up=b_up, w_down=w_down, b_down=b_down, ln2_g=ln2_g, ln2_b=ln2_b)
    MOM = dict(w_in=m_w_in, b_gate=m_b_gate, conv_w=m_conv_w, w_conv_out=m_w_conv_out, ssm_lam_re=m_ssm_lam_re,
               ssm_lam_im=m_ssm_lam_im, ssm_log_dt=m_ssm_log_dt, ssm_b_re=m_ssm_b_re, ssm_b_im=m_ssm_b_im,
               ssm_c_re=m_ssm_c_re, ssm_c_im=m_ssm_c_im, ssm_d=m_ssm_d, w_glu=m_w_glu, w_kv=m_w_kv,
               w_xattn_out=m_w_xattn_out, w_out=m_w_out, ln1_g=m_ln1_g, ln1_b=m_ln1_b, w_up=m_w_up, b_up=m_b_up,
               w_down=m_w_down, b_down=m_b_down, ln2_g=m_ln2_g, ln2_b=m_ln2_b)
    VEL = dict(w_in=v_w_in, b_gate=v_b_gate, conv_w=v_conv_w, w_conv_out=v_w_conv_out, ssm_lam_re=v_ssm_lam_re,
               ssm_lam_im=v_ssm_lam_im, ssm_log_dt=v_ssm_log_dt, ssm_b_re=v_ssm_b_re, ssm_b_im=v_ssm_b_im,
               ssm_c_re=v_ssm_c_re, ssm_c_im=v_ssm_c_im, ssm_d=v_ssm_d, w_glu=v_w_glu, w_kv=v_w_kv,
               w_xattn_out=v_w_xattn_out, w_out=v_w_out, ln1_g=v_ln1_g, ln1_b=v_ln1_b, w_up=v_w_up, b_up=v_b_up,
               w_down=v_w_down, b_down=v_b_down, ln2_g=v_ln2_g, ln2_b=v_ln2_b)
    names = list(W)
    xy = 2 * lax.axis_index("x") + lax.axis_index("y")

    xs = x[0]
    S = xs.shape[0]
    mems = mem[0]
    tgt = loss_target[0]

    shard_bf = {n: W[n][0].astype(bf16) for n in BIG}
    win3 = gather_two_level(shard_bf["w_in"], "gather_w_in")

    lr = ssm_lam_re.reshape(1, N_STATES)
    li = ssm_lam_im.reshape(1, N_STATES)
    ldt = jnp.repeat(ssm_log_dt.reshape(SSM_GROUPS), SSM_STATE).reshape(1, N_STATES)
    bt_re, bt_im = _state_rows(ssm_b_re[0]), _state_rows(ssm_b_im[0])
    ar, ai, kr, ki, bbt_r, bbt_i = ssm_prep(lr, li, ldt, bt_re, bt_im)
    bd_r, bd_i = _block_diag_b(bbt_r).astype(bf16), _block_diag_b(bbt_i).astype(bf16)
    cd_r, cd_i = _block_diag_c(ssm_c_re[0]).astype(bf16), _block_diag_c(ssm_c_im[0]).astype(bf16)
    bdt_r, bdt_i = bd_r.transpose(0, 2, 1), bd_i.transpose(0, 2, 1)
    cdt_r, cdt_i = cd_r.transpose(0, 2, 1), cd_i.transpose(0, 2, 1)
    d_skip = ssm_d.reshape(1, SSM_WIDTH)

    mem_bf = mems.astype(bf16)
    u_col = GATE_COLS + 3 * CONV_WIDTH
    (proj, x_bf), (wco3, wglu3, wxo3, wkv4, wout4, convw4) = mm_fwd(
        xs, win3, "proj", outs=(bf16,), emit_a=True,
        rides=tuple(("gather", shard_bf[n]) for n in MID) + (("gather", conv_w[0]),))
    wkv3 = wkv4.reshape(1, D_MODEL, 2 * XATTN_WIDTH)
    wout3 = wout4.reshape(1, D_MODEL, D_MODEL)
    convw_full = convw4.transpose(1, 0, 2).reshape(3, CONV_WIDTH)
    a_conv = conv_fwd(proj, convw_full)
    y_a = mm_fwd_small(a_conv, wco3, "conv_out", out_dtype=bf16)
    u_perm = _perm(proj[:, u_col:u_col + SSM_WIDTH])
    half = D_MODEL // 2
    (init_r, init_i), (wup_a,) = ssm_scan_fwd(u_perm, bd_r, bd_i, ar, ai,
                                              rides=(("gather", shard_bf["w_up"][:half]),))
    (ysm_perm, ys_perm, st_r, st_i), (wup_b,) = ssm_scan_fwd(
        u_perm, bd_r, bd_i, ar, ai, final=(cd_r, cd_i, d_skip, init_r, init_i),
        rides=(("gather", shard_bf["w_up"][half:]),))
    wup3 = jnp.concatenate([wup_a, wup_b], axis=1)
    y_s = _unperm(ys_perm)
    glu = mm_fwd_small(y_s, wglu3, "glu", out_dtype=bf16)
    kv = mm_fwd_small(mem_bf, wkv3, "kv", out_dtype=bf16)
    o_att = attn_fwd(proj, kv)
    y_c = mm_fwd_small(o_att, wxo3, "xattn_out", out_dtype=bf16)
    merged = merge_fwd(proj, b_gate, y_a, glu, y_c)
    t1, h1, h1_bf = mm_rows_fused(merged, wout3, "w_out_ln1", transposed=False, tm=512, epi=_ln1_fwd_epi,
                                  extras=(xs,), consts=(ln1_g, ln1_b), outs=(f32, f32, bf16))
    (r_up, hdn), (wdn4,) = mm_fwd(h1_bf, wup3, "w_up", bias=b_up, outs=(bf16, bf16), second=_relu2,
                                  first=lambda v: jnp.maximum(v, 0.0), rides=(("gather", shard_bf["w_down"]),))
    wdn3 = wdn4.reshape(1, D_FF, D_MODEL)

    dr2, dr2_bf, loss_cols, d_ln2_g, d_ln2_b, d_b_down = mm_rows_fused(
        hdn, wdn3, "w_down_ln2_loss", transposed=False, tm=512, epi=_ln2_loss_epi, extras=(h1, tgt),
        consts=(b_down, ln2_g, ln2_b), outs=(f32, bf16), nsums=4)
    part, other = {}, {}
    g_w_down = mm_bwd_w(hdn, dr2_bf, 1, "dw_down", tk=1024).reshape(4, -1, D_MODEL)
    (dup, d_b_up), (recv_dn,) = mm_bwd_x(dr2_bf, wdn3, "dup", tm=512, extras=(r_up,), colsum=True, out_dtype=bf16,
                                         epi=lambda acc, r: acc * (2.0 * r.astype(f32)),
                                         rides=(("scatter", g_w_down),))
    part["w_down"] = sum_slots(recv_dn, "sum_w_down")
    g_w_up = mm_bwd_w(h1_bf, dup, 4, "dw_up")
    dr1, dr1_bf, d_ln1_g, d_ln1_b = mm_rows_fused(
        dup, wup3, "dh1_ln1_bwd", transposed=True, tm=256, epi=_ln1_bwd_epi, extras=(xs, t1, dr2), consts=(ln1_g,),
        outs=(f32, bf16), nsums=2)
    g_w_out = mm_bwd_w(merged, dr1_bf, 1, "dw_out").reshape(4, -1, D_MODEL)
    dmerged, (recv_out,) = mm_bwd_x(dr1_bf, wout3, "dmerged", tm=1024, out_dtype=bf16,
                                    rides=(("scatter", g_w_out),))
    part["w_out"] = sum_slots(recv_out, "sum_w_out")
    (dproj, dy_a, dglu, dy_c, d_b_gate), (recv_up,) = merge_bwd(dmerged, proj, b_gate, y_a, glu, y_c,
                                                                rides=(("scatter", g_w_up),))
    part["w_up"] = sum_slots(recv_up, "sum_w_up")

    g_w_co = mm_bwd_w_small(a_conv, dy_a, 4, "dw_conv_out")
    da_conv = mm_bwd_x(dy_a, wco3, "da_conv", tm=1024)
    dproj, d_conv_w = conv_bwd(da_conv, proj, convw_full, dproj)

    g_w_glu = mm_bwd_w_small(y_s, dglu, 4, "dw_glu")
    dys_perm = _perm(mm_bwd_x(dglu, wglu3, "dy_s", tm=1024))
    linit_r, linit_i = ssm_scan_bwd(dys_perm, ysm_perm, cdt_r, cdt_i, ar, ai)
    (du_perm, dbacc_r, dbacc_i, dcacc_r, dcacc_i, da_r, da_i, d_ssm_d), (recv_co, recv_glu) = ssm_scan_bwd(
        dys_perm, ysm_perm, cdt_r, cdt_i, ar, ai,
        final=(u_perm, st_r, st_i, bdt_r, bdt_i, d_skip, linit_r, linit_i),
        rides=(("scatter", g_w_co), ("scatter", g_w_glu)))
    part["w_conv_out"] = sum_slots(recv_co, "sum_w_conv_out")
    part["w_glu"] = sum_slots(recv_glu, "sum_w_glu")
    dbt_re, dbt_im, d_lr, d_li, d_ldt_state = ssm_param_bwd(
        _diag_b(dbacc_r), _diag_b(dbacc_i), bt_re, bt_im, kr, ki, ar, ai, lr, li, ldt, da_r, da_i)
    d_log_dt = group_sum(d_ldt_state.reshape(SSM_GROUPS, SSM_STATE))
    d_b_re = dbt_re.reshape(SSM_GROUP, SSM_GROUPS, SSM_STATE).transpose(1, 2, 0)
    d_b_im = dbt_im.reshape(SSM_GROUP, SSM_GROUPS, SSM_STATE).transpose(1, 2, 0)
    d_c_re = _diag_c(dcacc_r)
    d_c_im = -_diag_c(dcacc_i)

    g_w_xo = mm_bwd_w_small(o_att, dy_c, 4, "dw_xattn_out")
    do_att = mm_bwd_x(dy_c, wxo3, "do_att", tm=1024, out_dtype=bf16)
    dproj, dkv = attn_bwd(do_att, proj, kv, dproj)
    g_w_kv = mm_bwd_w(mem_bf, dkv, 1, "dw_kv").reshape(4, -1, D_MODEL)

    dproj = lax.dynamic_update_slice(dproj, _unperm(du_perm), (0, u_col))
    g_small = {"b_gate": d_b_gate, "ssm_lam_re": d_lr, "ssm_lam_im": d_li, "ssm_log_dt": d_log_dt, "ssm_b_re": d_b_re,
               "ssm_b_im": d_b_im, "ssm_c_re": d_c_re, "ssm_c_im": d_c_im, "ssm_d": d_ssm_d, "ln1_g": d_ln1_g,
               "ln1_b": d_ln1_b, "b_up": d_b_up, "b_down": d_b_down, "ln2_g": d_ln2_g, "ln2_b": d_ln2_b}
    small_names = SMALL + ("conv_w",)
    g_small["conv_w"] = d_conv_w
    sizes = {n: (g_small[n].size + 1023) // 1024 * 1024 for n in small_names}
    pack = lambda d: jnp.concatenate([_pad_flat(d[n]) for n in small_names]).reshape(-1, 128)
    early = ("w_down", "w_up", "w_out", "w_conv_out", "w_glu")
    g_w_in, landed = mm_bwd_w(
        x_bf, dproj, 4, "dw_in", rides=(("scatter", g_w_xo), ("scatter", g_w_kv), ("all", pack(g_small)))
        + tuple(("pair", part[n]) for n in early))
    recv_xo, recv_kv, srecv = landed[:3]
    other.update(zip(early, landed[3:]))
    part["w_xattn_out"] = sum_slots(recv_xo, "sum_w_xattn_out")
    part["w_kv"] = sum_slots(recv_kv, "sum_w_kv")
    dx, (recv_in, other["w_xattn_out"], other["w_kv"]) = mm_bwd_x(
        dproj, win3, "dx", extras=(dr1,), epi=lambda acc, d: acc + ALPHA * d,
        rides=(("scatter", g_w_in), ("pair", part["w_xattn_out"]), ("pair", part["w_kv"])))
    part["w_in"] = sum_slots(recv_in, "sum_w_in")
    other["w_in"], = exchange(("pair", part["w_in"]), name="swap_w_in")
    res = [{}, {}, {}, {}]
    for n in BIG:
        for k, r in enumerate(adam_pair(part[n], other[n], W[n][0], MOM[n][0], VEL[n][0], "adam_" + n)):
            res[k][n] = r[None]
    conv_zero = jnp.zeros((3, CONV_WIDTH), f32)
    gs, ds_, ms, vs = adam_slots(srecv, pack({**{n: W[n] for n in SMALL}, "conv_w": conv_zero}),
                                 pack({**{n: MOM[n] for n in SMALL}, "conv_w": conv_zero}),
                                 pack({**{n: VEL[n] for n in SMALL}, "conv_w": conv_zero}), "adam_small")

    def unpack_small(buf):
        flat = buf.reshape(-1)
        out, r = {}, 0
        for n in small_names:
            ref = g_small[n] if n == "conv_w" else W[n]
            out[n] = flat[r:r + ref.size].reshape(ref.shape)
            r += sizes[n]
        return out

    res_s = [unpack_small(b) for b in (gs, ds_, ms, vs)]
    g_conv = lax.dynamic_slice(res_s[0]["conv_w"], (0, xy * 128), (3, 128))
    conv_slots = g_conv.reshape(1, 3, 128)
    cg, cd, cm, cv = adam_slots(conv_slots, conv_w[0], m_conv_w[0], v_conv_w[0], "adam_conv")
    conv_res = [cg, cd, cm, cv]

    loss = lax.psum(jnp.sum(loss_cols), ("x", "y", "c"))
    outs = [loss, dx.reshape(x.shape)]
    for k in range(4):
        for n in names:
            if n == "conv_w":
                outs.append(conv_res[k].reshape(conv_w.shape))
            elif n in BIG:
                outs.append(res[k][n])
            else:
                outs.append(res_s[k][n])
    return tuple(outs)
```

```python
import functools
import math

import jax
import jax.numpy as jnp
from jax import lax
from jax.experimental import pallas as pl
from jax.experimental.pallas import tpu as pltpu

f32 = jnp.float32
bf16 = jnp.bfloat16

D_MODEL = 1024
CONV_WIDTH = 512
SSM_WIDTH = 512
SSM_GROUP = 16
SSM_GROUPS = 32
SSM_STATE = 64
N_STATES = SSM_GROUPS * SSM_STATE
XATTN_HEADS = 4
XATTN_HEAD_DIM = 128
XATTN_WIDTH = 512
D_FF = 4096
GATE_COLS = 3 * D_MODEL
IN_COLS = GATE_COLS + 3 * CONV_WIDTH + SSM_WIDTH + XATTN_WIDTH
ALPHA = 2.0 ** 0.25
LN_EPS = 1e-5
ADAM_LR = 0.001
ADAM_B1 = 0.9
ADAM_B2 = 0.999
ADAM_EPS = 1e-08
ADAM_WD = 0.01
ADAM_STEP = 10

N_SEG = 8
SCAN_ROWS = 512
LANE_STRIP = 512
TM = 512
VMEM_LIMIT = 48 * 1024 * 1024
MESH = pl.DeviceIdType.MESH

NT_DIMS = (((1,), (1,)), ((), ()))
TN_DIMS = (((0,), (0,)), ((), ()))


def _params():
    return pltpu.CompilerParams(vmem_limit_bytes=VMEM_LIMIT)


def _full(shape):
    n = len(shape)
    return pl.BlockSpec(shape, lambda *_: (0,) * n)


def _rows(tm, w, cb=0):
    return pl.BlockSpec((tm, w), lambda i: (i, cb))


def _sig(x):
    return 1.0 / (1.0 + jnp.exp(-x))


HBM_SPEC = pl.BlockSpec(memory_space=pl.ANY)
RIDE_PEERS = {"gather": 3, "scatter": 3, "all": 7, "pair": 1}


def _xy_peers(x, y):
    return [(1 - x, y), (x, 1 - y), (1 - x, 1 - y)]


def _ride_copies(kind, src, dst, send_sems, recv_sems, local_sem):
    x, y, c = lax.axis_index("x"), lax.axis_index("y"), lax.axis_index("c")
    me = 2 * x + y
    if kind == "all":
        flips = [(fx, fy, fc) for fx in (0, 1) for fy in (0, 1) for fc in (0, 1)][1:]
        peers = [(x ^ fx, y ^ fy, c ^ fc) for fx, fy, fc in flips]
        slot = lambda p: 4 * p[0] + 2 * p[1] + p[2]
        mine = slot((x, y, c))
    else:
        peers = [(px, py, c) for px, py in _xy_peers(x, y)]
        slot = lambda p: 2 * p[0] + p[1]
        mine = me

    def remote(k, s, d):
        return pltpu.make_async_remote_copy(src_ref=s, dst_ref=d, send_sem=send_sems.at[k], recv_sem=recv_sems.at[k],
                                            device_id=peers[k], device_id_type=MESH)

    if kind == "pair":
        peers = [(x, y, 1 - c)]
        return None, [remote(0, src, dst)], [remote(0, src, dst)]
    if kind == "scatter":
        local = pltpu.make_async_copy(src.at[me], dst.at[me], local_sem)
        sends = [remote(k, src.at[slot(p)], dst.at[me]) for k, p in enumerate(peers)]
        lands = [remote(k, src.at[me], dst.at[slot(p)]) for k, p in enumerate(peers)]
    else:
        local = pltpu.make_async_copy(src, dst.at[mine], local_sem)
        sends = [remote(k, src, dst.at[mine]) for k, p in enumerate(peers)]
        lands = [remote(k, src, dst.at[slot(p)]) for k, p in enumerate(peers)]
    return local, sends, lands


def _ride_shape(kind, src):
    lead = {"gather": (4,), "scatter": (), "all": (8,), "pair": ()}[kind]
    return jax.ShapeDtypeStruct(lead + src.shape, src.dtype)


def _pcall(body, args, *, name, grid, in_specs, out_specs, out_shape, scratch_shapes=(), rides=(), aliases=None):
    n_in, n_out, n_scr, nr = len(in_specs), len(out_specs), len(scratch_shapes), len(rides)
    kinds = [k for k, _ in rides]

    def wrapped(*refs):
        ins, rsrc = refs[:n_in], refs[n_in:n_in + nr]
        outs = refs[n_in + nr:n_in + nr + n_out]
        rdst = refs[n_in + nr + n_out:n_in + 2 * nr + n_out]
        scr = refs[n_in + 2 * nr + n_out:n_in + 2 * nr + n_out + n_scr]
        sems = refs[n_in + 2 * nr + n_out + n_scr:]
        def start():
            for r, kind in enumerate(kinds):
                local, sends, _ = _ride_copies(kind, rsrc[r], rdst[r], *sems[3 * r:3 * r + 3])
                if local is not None:
                    local.start()
                for cp in sends:
                    cp.start()

        def finish():
            for r, kind in enumerate(kinds):
                local, sends, lands = _ride_copies(kind, rsrc[r], rdst[r], *sems[3 * r:3 * r + 3])
                for cp in lands:
                    cp.wait_recv()
                for cp in sends:
                    cp.wait_send()
                if local is not None:
                    local.wait()

        if nr and grid:
            ids = [pl.program_id(a) for a in range(len(grid))]
            first = functools.reduce(jnp.logical_and, [i == 0 for i in ids])
            last = functools.reduce(jnp.logical_and, [i == g - 1 for i, g in zip(ids, grid)])
            pl.when(first)(start)
        elif nr:
            start()
        body(*ins, *outs, *scr)
        if nr and grid:
            pl.when(last)(finish)
        elif nr:
            finish()

    sems = []
    for kind in kinds:
        n = RIDE_PEERS[kind]
        sems += [pltpu.SemaphoreType.DMA((n,)), pltpu.SemaphoreType.DMA((n,)), pltpu.SemaphoreType.DMA(())]
    res = pl.pallas_call(
        wrapped, name=name, grid=grid, in_specs=list(in_specs) + [HBM_SPEC] * nr,
        out_specs=list(out_specs) + [HBM_SPEC] * nr,
        out_shape=list(out_shape) + [_ride_shape(k, s) for k, s in rides],
        scratch_shapes=list(scratch_shapes) + sems, input_output_aliases=aliases or {},
        compiler_params=_params())(*args, *[s for _, s in rides])
    return list(res[:n_out]), list(res[n_out:])


def _bf(v):
    return v if v.dtype == bf16 else v.astype(bf16)


def mm_fwd_small(a, w3, name, *, tm=1024, out_dtype=f32):
    M, K = a.shape
    J, _, n = w3.shape
    tm = min(tm, M)

    def body(a_ref, w_ref, o_ref):
        av = _bf(a_ref[...])
        for j in range(J):
            o_ref[:, j * n:(j + 1) * n] = jnp.dot(av, w_ref[j], preferred_element_type=f32).astype(out_dtype)

    return pl.pallas_call(
        body, name=name, grid=(M // tm,), in_specs=[_rows(tm, K), _full((J, K, n))], out_specs=_rows(tm, J * n),
        out_shape=jax.ShapeDtypeStruct((M, J * n), out_dtype), compiler_params=_params())(a, w3)


def mm_bwd_w_small(a, dy, J, name, *, tm=2048):
    M, K = a.shape
    n = dy.shape[1] // J
    tm = min(tm, M)
    ns = M // tm

    def body(a_ref, dy_ref, o_ref, acc_ref):
        s = pl.program_id(0)
        part = lax.dot_general(_bf(a_ref[...]), _bf(dy_ref[...]), TN_DIMS, preferred_element_type=f32)

        @pl.when(s == 0)
        def _():
            acc_ref[...] = part

        @pl.when(s > 0)
        def _():
            acc_ref[...] += part

        @pl.when(s == ns - 1)
        def _():
            for j in range(J):
                o_ref[j] = acc_ref[:, j * n:(j + 1) * n].astype(bf16)

    return pl.pallas_call(
        body, name=name, grid=(ns,), in_specs=[_rows(tm, K), _rows(tm, J * n)], out_specs=_full((J, K, n)),
        out_shape=jax.ShapeDtypeStruct((J, K, n), bf16), scratch_shapes=[pltpu.VMEM((K, J * n), f32)],
        compiler_params=_params())(a, dy)


def mm_fwd(a, w3, name, *, tm=1024, tn=None, bias=None, outs=(f32,), first=None, second=None, emit_a=False,
           rides=()):
    M, K = a.shape
    J, _, n = w3.shape
    tm = min(tm, M)
    tn = tn or n
    nl = n // tn
    nb = 0 if bias is None else 1

    def body(*refs):
        a_ref, w_ref = refs[0], refs[1]
        av = _bf(a_ref[...])
        acc = jnp.dot(av, w_ref[0], preferred_element_type=f32)
        if bias is not None:
            acc = acc + refs[2][...]
        refs[2 + nb][...] = (acc if first is None else first(acc)).astype(outs[0])
        if len(outs) > 1:
            refs[3 + nb][...] = second(acc).astype(outs[1])
        if emit_a:
            @pl.when(jnp.logical_and(pl.program_id(0) == 0, pl.program_id(1) == 0))
            def _():
                refs[2 + nb + len(outs)][...] = av

    in_specs = [pl.BlockSpec((tm, K), lambda j, l, i: (i, 0)),
                pl.BlockSpec((1, K, tn), lambda j, l, i: (j, 0, l))]
    args = [a, w3]
    if bias is not None:
        in_specs.append(pl.BlockSpec((1, tn), lambda j, l, i: (0, j * nl + l)))
        args.append(bias)
    out_specs = [pl.BlockSpec((tm, tn), lambda j, l, i: (i, j * nl + l)) for _ in outs]
    out_shape = [jax.ShapeDtypeStruct((M, J * n), dt) for dt in outs]
    if emit_a:
        last = M // tm - 1
        out_specs.append(pl.BlockSpec((tm, K), lambda j, l, i: (jnp.where((j == 0) & (l == 0), i, last), 0)))
        out_shape.append(jax.ShapeDtypeStruct((M, K), bf16))
    res, landed = _pcall(body, args, name=name, grid=(J, nl, M // tm), in_specs=in_specs, out_specs=out_specs,
                         out_shape=out_shape, rides=rides)
    res = res if len(res) > 1 else res[0]
    return (res, landed) if rides else res


def mm_bwd_x(dy, w3, name, *, tm=TM, epi=None, extras=(), colsum=False, out_dtype=f32, rides=()):
    M = dy.shape[0]
    J, K, n = w3.shape
    tm = min(tm, M)
    nex = len(extras)

    def body(*refs):
        dy_ref, w_hbm = refs[0], refs[1]
        ex = refs[2:2 + nex]
        o_ref = refs[2 + nex]
        w_ref = refs[-1]
        i = pl.program_id(0)

        @pl.when(i == 0)
        def _():
            pltpu.sync_copy(w_hbm, w_ref)

        acc = None
        for j in range(J):
            part = lax.dot_general(_bf(dy_ref[:, j * n:(j + 1) * n]), w_ref[j], NT_DIMS, preferred_element_type=f32)
            acc = part if acc is None else acc + part
        if epi is not None:
            acc = epi(acc, *[e[...] for e in ex])
        o_ref[...] = acc.astype(out_dtype)
        if colsum:
            s_ref = refs[3 + nex]
            cs = jnp.sum(acc, axis=0, keepdims=True)

            @pl.when(i == 0)
            def _():
                s_ref[...] = cs

            @pl.when(i > 0)
            def _():
                s_ref[...] += cs

    in_specs = [pl.BlockSpec((tm, J * n), lambda i: (i, 0)), pl.BlockSpec(memory_space=pl.ANY)]
    in_specs += [pl.BlockSpec((tm, K), lambda i: (i, 0)) for _ in extras]
    out_specs = [pl.BlockSpec((tm, K), lambda i: (i, 0))]
    out_shape = [jax.ShapeDtypeStruct((M, K), out_dtype)]
    if colsum:
        out_specs.append(pl.BlockSpec((1, K), lambda i: (0, 0)))
        out_shape.append(jax.ShapeDtypeStruct((1, K), f32))
    res, landed = _pcall(body, [dy, w3, *extras], name=name, grid=(M // tm,), in_specs=in_specs, out_specs=out_specs,
                         out_shape=out_shape, scratch_shapes=[pltpu.VMEM((J, K, n), bf16)], rides=rides)
    res = res if colsum else res[0]
    return (res, landed) if rides else res


def mm_rows_fused(a, w3, name, *, transposed, tm, epi, extras=(), consts=(), outs=(f32,), nsums=0, rides=()):
    M = a.shape[0]
    J, K, n = w3.shape
    W = K if transposed else n
    tm = min(tm, M)
    nex, nco, nout = len(extras), len(consts), len(outs)

    def body(*refs):
        a_ref, w_hbm = refs[0], refs[1]
        ex = refs[2:2 + nex]
        co = refs[2 + nex:2 + nex + nco]
        o_refs = refs[2 + nex + nco:2 + nex + nco + nout]
        s_refs = refs[2 + nex + nco + nout:2 + nex + nco + nout + nsums]
        w_ref = refs[-1]
        i = pl.program_id(0)

        @pl.when(i == 0)
        def _():
            pltpu.sync_copy(w_hbm, w_ref)

        if transposed:
            acc = None
            for j in range(J):
                part = lax.dot_general(_bf(a_ref[:, j * n:(j + 1) * n]), w_ref[j], NT_DIMS,
                                       preferred_element_type=f32)
                acc = part if acc is None else acc + part
        else:
            acc = jnp.dot(_bf(a_ref[...]), w_ref[0], preferred_element_type=f32)
        rows, sums = epi(acc, *[e[...] for e in ex], *[c[...] for c in co])
        for o_ref, v, dt in zip(o_refs, rows, outs):
            o_ref[...] = v.astype(dt)
        for s_ref, v in zip(s_refs, sums):
            @pl.when(i == 0)
            def _(s_ref=s_ref, v=v):
                s_ref[...] = v

            @pl.when(i > 0)
            def _(s_ref=s_ref, v=v):
                s_ref[...] += v

    in_specs = [_rows(tm, a.shape[1]), pl.BlockSpec(memory_space=pl.ANY)]
    in_specs += [_rows(tm, W) for _ in extras] + [_full((1, W)) for _ in consts]
    out_specs = [_rows(tm, W) for _ in outs] + [_full((1, W)) for _ in range(nsums)]
    out_shape = [jax.ShapeDtypeStruct((M, W), dt) for dt in outs] + [jax.ShapeDtypeStruct((1, W), f32)] * nsums
    res, landed = _pcall(body, [a, w3, *extras, *consts], name=name, grid=(M // tm,), in_specs=in_specs,
                         out_specs=out_specs, out_shape=out_shape, scratch_shapes=[pltpu.VMEM((J, K, n), bf16)],
                         rides=rides)
    return (res, landed) if rides else res


def mm_bwd_w(a, dy, J, name, *, tm=2048, tn=None, tk=None, rides=()):
    M, K = a.shape
    n = dy.shape[1] // J
    tm = min(tm, M)
    tn = tn or n
    tk = tk or K
    nl = n // tn
    nk = K // tk
    ns = M // tm

    def body(a_ref, dy_ref, o_ref, acc_ref):
        s = pl.program_id(3)
        part = lax.dot_general(_bf(a_ref[...]), _bf(dy_ref[...]), TN_DIMS, preferred_element_type=f32)

        @pl.when(s == 0)
        def _():
            acc_ref[...] = part

        @pl.when(s > 0)
        def _():
            acc_ref[...] += part

        @pl.when(s == ns - 1)
        def _():
            o_ref[0] = acc_ref[...].astype(bf16)

    res, landed = _pcall(
        body, [a, dy], name=name, grid=(J, nl, nk, ns),
        in_specs=[pl.BlockSpec((tm, tk), lambda j, l, k, s: (s, k)),
                  pl.BlockSpec((tm, tn), lambda j, l, k, s: (s, j * nl + l))],
        out_specs=[pl.BlockSpec((1, tk, tn), lambda j, l, k, s: (j, k, l))],
        out_shape=[jax.ShapeDtypeStruct((J, K, n), bf16)],
        scratch_shapes=[pltpu.VMEM((tk, tn), f32)], rides=rides)
    return (res[0], landed) if rides else res[0]


def _relu2(v):
    r = jnp.maximum(v, 0.0)
    return r * r


HALO = 16


def _shift_down(z, k, halo):
    r = lax.broadcasted_iota(jnp.int32, z.shape, 0)
    y = pltpu.roll(z, k, 0)
    for q in range(k):
        y = jnp.where(r == q, halo[HALO - k + q:HALO - k + q + 1, :], y)
    return y


def _shift_up(z, k, halo):
    tm = z.shape[0]
    r = lax.broadcasted_iota(jnp.int32, z.shape, 0)
    y = pltpu.roll(z, tm - k, 0)
    for q in range(k):
        y = jnp.where(r == tm - k + q, halo[q:q + 1, :], y)
    return y


def _prev_halo(tm, cb):
    return pl.BlockSpec((HALO, CONV_WIDTH), lambda i: (jnp.maximum(i * (tm // HALO) - 1, 0), cb))


def _next_halo(tm, cb, nblk):
    return pl.BlockSpec((HALO, CONV_WIDTH), lambda i: (jnp.minimum((i + 1) * (tm // HALO), nblk - 1), cb))


def _f32(ref):
    return ref[...].astype(f32)


def conv_fwd(proj, conv_w, tm=TM):
    S = proj.shape[0]

    def body(cb_ref, cc_ref, ch_ref, cch_ref, chh_ref, w_ref, a_ref):
        i = pl.program_id(0)
        z = _f32(cc_ref) * _f32(ch_ref)
        zh = jnp.where(i == 0, 0.0, _f32(cch_ref) * _f32(chh_ref))
        w = w_ref[...]
        dwz = w[0:1, :] * _shift_down(z, 2, zh) + w[1:2, :] * _shift_down(z, 1, zh) + w[2:3, :] * z
        a_ref[...] = (_f32(cb_ref) * dwz).astype(bf16)

    return pl.pallas_call(
        body, name="conv_fwd", grid=(S // tm,),
        in_specs=[_rows(tm, CONV_WIDTH, 6), _rows(tm, CONV_WIDTH, 7), _rows(tm, CONV_WIDTH, 8),
                  _prev_halo(tm, 7), _prev_halo(tm, 8), _full((3, CONV_WIDTH))],
        out_specs=_rows(tm, CONV_WIDTH),
        out_shape=jax.ShapeDtypeStruct((S, CONV_WIDTH), bf16), compiler_params=_params())(
            proj, proj, proj, proj, proj, conv_w)


def conv_bwd(da, proj, conv_w, dproj, tm=TM):
    S = proj.shape[0]
    nt = S // tm

    def body(da_ref, cb_ref, cc_ref, ch_ref, cch_ref, chh_ref, dan_ref, cbn_ref, w_ref, _, o_ref, dw_ref):
        i = pl.program_id(0)
        cc, ch, cb, dav = _f32(cc_ref), _f32(ch_ref), _f32(cb_ref), da_ref[...]
        z = cc * ch
        zh = jnp.where(i == 0, 0.0, _f32(cch_ref) * _f32(chh_ref))
        w = w_ref[...]
        z1 = _shift_down(z, 1, zh)
        z2 = _shift_down(z, 2, zh)
        dwz = w[0:1, :] * z2 + w[1:2, :] * z1 + w[2:3, :] * z
        g = dav * cb
        gn = jnp.where(i == nt - 1, 0.0, dan_ref[...] * _f32(cbn_ref))
        dz = w[2:3, :] * g + w[1:2, :] * _shift_up(g, 1, gn) + w[0:1, :] * _shift_up(g, 2, gn)
        o_ref[:, 0:CONV_WIDTH] = (dav * dwz).astype(bf16)
        o_ref[:, CONV_WIDTH:2 * CONV_WIDTH] = (dz * ch).astype(bf16)
        o_ref[:, 2 * CONV_WIDTH:3 * CONV_WIDTH] = (dz * cc).astype(bf16)

        @pl.when(i == 0)
        def _():
            dw_ref[...] = jnp.zeros_like(dw_ref)

        dw_ref[0:1, :] += jnp.sum(g * z2, axis=0, keepdims=True)
        dw_ref[1:2, :] += jnp.sum(g * z1, axis=0, keepdims=True)
        dw_ref[2:3, :] += jnp.sum(g * z, axis=0, keepdims=True)

    return pl.pallas_call(
        body, name="conv_bwd", grid=(nt,),
        in_specs=[_rows(tm, CONV_WIDTH), _rows(tm, CONV_WIDTH, 6), _rows(tm, CONV_WIDTH, 7), _rows(tm, CONV_WIDTH, 8),
                  _prev_halo(tm, 7), _prev_halo(tm, 8),
                  _next_halo(tm, 0, S // HALO), _next_halo(tm, 6, S // HALO), _full((3, CONV_WIDTH)),
                  pl.BlockSpec(memory_space=pl.ANY)],
        out_specs=[_rows(tm, 3 * CONV_WIDTH, GATE_COLS // (3 * CONV_WIDTH)), _full((3, CONV_WIDTH))],
        out_shape=[jax.ShapeDtypeStruct(dproj.shape, bf16), jax.ShapeDtypeStruct((3, CONV_WIDTH), f32)],
        input_output_aliases={9: 0},
        compiler_params=_params())(da, proj, proj, proj, proj, proj, da, proj, conv_w, dproj)


def _cmul(ar, ai, br, bi):
    return ar * br - ai * bi, ar * bi + ai * br


def ssm_prep(lr, li, ldt, bt_re, bt_im):
    def body(lr_ref, li_ref, ldt_ref, br_ref, bi_ref, ar_ref, ai_ref, kr_ref, ki_ref, bbr_ref, bbi_ref):
        lrv, liv = lr_ref[...], li_ref[...]
        dt = jnp.exp(ldt_ref[...])
        mag = jnp.exp(lrv * dt)
        ar = mag * jnp.cos(liv * dt)
        ai = mag * jnp.sin(liv * dt)
        den = lrv * lrv + liv * liv
        nr = ar - 1.0
        kr = (nr * lrv + ai * liv) / den
        ki = (ai * lrv - nr * liv) / den
        ar_ref[...], ai_ref[...], kr_ref[...], ki_ref[...] = ar, ai, kr, ki
        bbr_ref[...] = kr * br_ref[...] - ki * bi_ref[...]
        bbi_ref[...] = kr * bi_ref[...] + ki * br_ref[...]

    v = jax.ShapeDtypeStruct((1, N_STATES), f32)
    m = jax.ShapeDtypeStruct((SSM_GROUP, N_STATES), f32)
    return pl.pallas_call(body, name="ssm_prep", out_shape=[v, v, v, v, m, m])(lr, li, ldt, bt_re, bt_im)


def _pow_segment(ar, ai, seg_len):
    pr, pi = ar, ai
    for _ in range(int(math.log2(seg_len))):
        pr, pi = _cmul(pr, pi, pr, pi)
    return pr, pi


GELU_K = math.sqrt(2.0 / math.pi)
GELU_C = 0.044715


def _gelu(v):
    return 0.5 * v * (1.0 + jnp.tanh(GELU_K * (v + GELU_C * v * v * v)))


def _gelu_grad(v):
    t = jnp.tanh(GELU_K * (v + GELU_C * v * v * v))
    return 0.5 * (1.0 + t) + 0.5 * v * (1.0 - t * t) * GELU_K * (1.0 + 3.0 * GELU_C * v * v)


def ssm_scan_fwd(u_perm, bd_r, bd_i, ar, ai, *, final=None, rides=()):
    S = u_perm.shape[0]
    R = SCAN_ROWS
    nblk = S // R
    seg_len = S // N_SEG
    nstrip = N_STATES // LANE_STRIP
    store = final is not None

    def body(*refs):
        if store:
            (u_ref, bdr_ref, bdi_ref, ar_ref, ai_ref, cdr_ref, cdi_ref, d_ref, ir_ref, ii_ref,
             y_ref, ys_ref, sr_ref, si_ref, bur, bui, car, cai) = refs
        else:
            u_ref, bdr_ref, bdi_ref, ar_ref, ai_ref, or_ref, oi_ref, bur, bui, car, cai = refs
        i = pl.program_id(0)
        ub = _bf(u_ref[...])
        u = ub.astype(f32)
        for cb in range(4):
            us = ub[:, cb * 128:(cb + 1) * 128]
            bur[:, cb * 512:(cb + 1) * 512] = jnp.dot(us, bdr_ref[cb], preferred_element_type=f32)
            bui[:, cb * 512:(cb + 1) * 512] = jnp.dot(us, bdi_ref[cb], preferred_element_type=f32)

        @pl.when(i == 0)
        def _():
            if store:
                car[...] = ir_ref[...]
                cai[...] = ii_ref[...]
            else:
                car[...] = jnp.zeros_like(car)
                cai[...] = jnp.zeros_like(cai)

        for ls in range(nstrip):
            lanes = pl.ds(ls * LANE_STRIP, LANE_STRIP)
            a_r = jnp.broadcast_to(ar_ref[:, lanes], (N_SEG, LANE_STRIP))
            a_i = jnp.broadcast_to(ai_ref[:, lanes], (N_SEG, LANE_STRIP))

            def step(t, carry, lanes=lanes, a_r=a_r, a_i=a_i):
                s_r, s_i = carry
                row = pl.multiple_of(t * 8, 8)
                n_r = a_r * s_r - a_i * s_i + bur[pl.ds(row, 8), lanes]
                n_i = a_r * s_i + a_i * s_r + bui[pl.ds(row, 8), lanes]
                if store:
                    sr_ref[pl.ds(row, 8), lanes] = n_r
                    si_ref[pl.ds(row, 8), lanes] = n_i
                return n_r, n_i

            e_r, e_i = lax.fori_loop(0, R // 8, step, (car[:, lanes], cai[:, lanes]), unroll=True)
            car[:, lanes] = e_r
            cai[:, lanes] = e_i

        if store:
            for cb in range(4):
                st_r = sr_ref[:, cb * 512:(cb + 1) * 512].astype(bf16)
                st_i = si_ref[:, cb * 512:(cb + 1) * 512].astype(bf16)
                y = (jnp.dot(st_r, cdr_ref[cb], preferred_element_type=f32)
                     - jnp.dot(st_i, cdi_ref[cb], preferred_element_type=f32))
                cols = slice(cb * 128, (cb + 1) * 128)
                y = y + d_ref[:, cols] * u[:, cols]
                y_ref[:, cols] = y
                ys_ref[:, cols] = _gelu(y).astype(bf16)
        else:
            @pl.when(i == nblk - 1)
            def _():
                p_r, p_i = _pow_segment(ar_ref[...], ai_ref[...], seg_len)
                t_r, t_i = car[0:1, :], cai[0:1, :]
                or_ref[0:1, :] = jnp.zeros((1, N_STATES), f32)
                oi_ref[0:1, :] = jnp.zeros((1, N_STATES), f32)
                for j in range(1, N_SEG):
                    or_ref[j:j + 1, :] = t_r
                    oi_ref[j:j + 1, :] = t_i
                    m_r, m_i = _cmul(p_r, p_i, t_r, t_i)
                    t_r, t_i = car[j:j + 1, :] + m_r, cai[j:j + 1, :] + m_i

    blk = lambda w: pl.BlockSpec((R, w), lambda i: (i, 0))
    in_specs = [blk(SSM_WIDTH), _full((4, 128, 512)), _full((4, 128, 512)), _full((1, N_STATES)), _full((1, N_STATES))]
    args = [u_perm, bd_r, bd_i, ar, ai]
    scratch = [pltpu.VMEM((R, N_STATES), f32), pltpu.VMEM((R, N_STATES), f32),
               pltpu.VMEM((N_SEG, N_STATES), f32), pltpu.VMEM((N_SEG, N_STATES), f32)]
    if store:
        in_specs += [_full((4, 512, 128)), _full((4, 512, 128)), _full((1, SSM_WIDTH)),
                     _full((N_SEG, N_STATES)), _full((N_SEG, N_STATES))]
        args += list(final)
        out_specs = [blk(SSM_WIDTH), blk(SSM_WIDTH), blk(N_STATES), blk(N_STATES)]
        out_shape = [jax.ShapeDtypeStruct((S, SSM_WIDTH), f32), jax.ShapeDtypeStruct((S, SSM_WIDTH), bf16),
                     jax.ShapeDtypeStruct((S, N_STATES), f32), jax.ShapeDtypeStruct((S, N_STATES), f32)]
        name = "ssm_scan_fwd"
    else:
        out_specs = [_full((N_SEG, N_STATES)), _full((N_SEG, N_STATES))]
        out_shape = [jax.ShapeDtypeStruct((N_SEG, N_STATES), f32)] * 2
        name = "ssm_scan_fwd_carry"
    res, landed = _pcall(body, args, name=name, grid=(nblk,), in_specs=in_specs, out_specs=out_specs,
                         out_shape=out_shape, scratch_shapes=scratch, rides=rides)
    return (res, landed) if rides else res


def ssm_scan_bwd(dys_perm, y_perm, cdt_r, cdt_i, ar, ai, *, final=None, rides=()):
    S = dys_perm.shape[0]
    R = SCAN_ROWS
    nblk = S // R
    seg_len = S // N_SEG
    nstrip = N_STATES // LANE_STRIP
    store = final is not None

    def body(*refs):
        if store:
            (dys_ref, y_ref, cdr_ref, cdi_ref, ar_ref, ai_ref, u_ref, sr_ref, si_ref, bdr_ref, bdi_ref, d_ref, ir_ref,
             ii_ref, du_ref, dbr_ref, dbi_ref, dcr_ref, dci_ref, dar_ref, dai_ref, dd_ref, dsr, dsi, lmr, lmi, car,
             cai) = refs
        else:
            dys_ref, y_ref, cdr_ref, cdi_ref, ar_ref, ai_ref, or_ref, oi_ref, dsr, dsi, car, cai = refs
        i = pl.program_id(0)
        dy = dys_ref[...] * _gelu_grad(y_ref[...])
        dyb = dy.astype(bf16)
        for cb in range(4):
            ds_ = dyb[:, cb * 128:(cb + 1) * 128]
            dsr[:, cb * 512:(cb + 1) * 512] = jnp.dot(ds_, cdr_ref[cb], preferred_element_type=f32)
            dsi[:, cb * 512:(cb + 1) * 512] = -jnp.dot(ds_, cdi_ref[cb], preferred_element_type=f32)

        @pl.when(i == 0)
        def _():
            if store:
                car[...] = ir_ref[...]
                cai[...] = ii_ref[...]
                dar_ref[...] = jnp.zeros_like(dar_ref)
                dai_ref[...] = jnp.zeros_like(dai_ref)
                dbr_ref[...] = jnp.zeros_like(dbr_ref)
                dbi_ref[...] = jnp.zeros_like(dbi_ref)
                dcr_ref[...] = jnp.zeros_like(dcr_ref)
                dci_ref[...] = jnp.zeros_like(dci_ref)
                dd_ref[...] = jnp.zeros_like(dd_ref)
            else:
                car[...] = jnp.zeros_like(car)
                cai[...] = jnp.zeros_like(cai)

        for ls in range(nstrip):
            lanes = pl.ds(ls * LANE_STRIP, LANE_STRIP)
            a_r = jnp.broadcast_to(ar_ref[:, lanes], (N_SEG, LANE_STRIP))
            a_i = jnp.broadcast_to(ai_ref[:, lanes], (N_SEG, LANE_STRIP))
            zero = jnp.zeros((N_SEG, LANE_STRIP), f32)

            def step(k, carry, lanes=lanes, a_r=a_r, a_i=a_i):
                l_r, l_i, g_r, g_i = carry
                row = pl.multiple_of((R // 8 - 1 - k) * 8, 8)
                if store:
                    s_r = sr_ref[pl.ds(row, 8), lanes]
                    s_i = si_ref[pl.ds(row, 8), lanes]
                    g_r = g_r + l_r * s_r + l_i * s_i
                    g_i = g_i + l_i * s_r - l_r * s_i
                n_r = dsr[pl.ds(row, 8), lanes] + a_r * l_r + a_i * l_i
                n_i = dsi[pl.ds(row, 8), lanes] + a_r * l_i - a_i * l_r
                if store:
                    lmr[pl.ds(row, 8), lanes] = n_r
                    lmi[pl.ds(row, 8), lanes] = n_i
                return n_r, n_i, g_r, g_i

            e_r, e_i, g_r, g_i = lax.fori_loop(0, R // 8, step, (car[:, lanes], cai[:, lanes], zero, zero),
                                               unroll=True)
            car[:, lanes] = e_r
            cai[:, lanes] = e_i
            if store:
                dar_ref[:, lanes] += g_r
                dai_ref[:, lanes] += g_i

        if store:
            ub = _bf(u_ref[...])
            u = ub.astype(f32)
            for cb in range(4):
                cols = slice(cb * 128, (cb + 1) * 128)
                st = slice(cb * 512, (cb + 1) * 512)
                l_r = lmr[:, st].astype(bf16)
                l_i = lmi[:, st].astype(bf16)
                du = (jnp.dot(l_r, bdr_ref[cb], preferred_element_type=f32)
                      + jnp.dot(l_i, bdi_ref[cb], preferred_element_type=f32))
                du_ref[:, cols] = (du + d_ref[:, cols] * dy[:, cols]).astype(bf16)
                dbr_ref[cb] += lax.dot_general(l_r, ub[:, cols], TN_DIMS, preferred_element_type=f32)
                dbi_ref[cb] += lax.dot_general(l_i, ub[:, cols], TN_DIMS, preferred_element_type=f32)
                dcr_ref[cb] += lax.dot_general(dyb[:, cols], sr_ref[:, st].astype(bf16), TN_DIMS,
                                               preferred_element_type=f32)
                dci_ref[cb] += lax.dot_general(dyb[:, cols], si_ref[:, st].astype(bf16), TN_DIMS,
                                               preferred_element_type=f32)
            dd_ref[...] += jnp.sum(dy * u, axis=0, keepdims=True)
        else:
            @pl.when(i == nblk - 1)
            def _():
                p_r, p_i = _pow_segment(ar_ref[...], ai_ref[...], seg_len)
                p_i = -p_i
                t_r, t_i = car[N_SEG - 1:N_SEG, :], cai[N_SEG - 1:N_SEG, :]
                or_ref[N_SEG - 1:N_SEG, :] = jnp.zeros((1, N_STATES), f32)
                oi_ref[N_SEG - 1:N_SEG, :] = jnp.zeros((1, N_STATES), f32)
                for j in range(N_SEG - 2, -1, -1):
                    or_ref[j:j + 1, :] = t_r
                    oi_ref[j:j + 1, :] = t_i
                    m_r, m_i = _cmul(p_r, p_i, t_r, t_i)
                    t_r, t_i = car[j:j + 1, :] + m_r, cai[j:j + 1, :] + m_i

    blk = lambda w: pl.BlockSpec((R, w), lambda i: (nblk - 1 - i, 0))
    in_specs = [blk(SSM_WIDTH), blk(SSM_WIDTH), _full((4, 128, 512)), _full((4, 128, 512)), _full((1, N_STATES)),
                _full((1, N_STATES))]
    args = [dys_perm, y_perm, cdt_r, cdt_i, ar, ai]
    seg = jax.ShapeDtypeStruct((N_SEG, N_STATES), f32)
    if store:
        in_specs += [blk(SSM_WIDTH), blk(N_STATES), blk(N_STATES), _full((4, 512, 128)), _full((4, 512, 128)),
                     _full((1, SSM_WIDTH)), _full((N_SEG, N_STATES)), _full((N_SEG, N_STATES))]
        args += list(final)
        out_specs = [blk(SSM_WIDTH), _full((4, 512, 128)), _full((4, 512, 128)), _full((4, 128, 512)),
                     _full((4, 128, 512)), _full((N_SEG, N_STATES)), _full((N_SEG, N_STATES)), _full((1, SSM_WIDTH))]
        b_acc = jax.ShapeDtypeStruct((4, 512, 128), f32)
        c_acc = jax.ShapeDtypeStruct((4, 128, 512), f32)
        out_shape = [jax.ShapeDtypeStruct((S, SSM_WIDTH), bf16), b_acc, b_acc, c_acc, c_acc, seg, seg,
                     jax.ShapeDtypeStruct((1, SSM_WIDTH), f32)]
        scratch = [pltpu.VMEM((R, N_STATES), f32)] * 4 + [pltpu.VMEM((N_SEG, N_STATES), f32)] * 2
        name = "ssm_scan_bwd"
    else:
        out_specs = [_full((N_SEG, N_STATES)), _full((N_SEG, N_STATES))]
        out_shape = [seg, seg]
        scratch = [pltpu.VMEM((R, N_STATES), f32)] * 2 + [pltpu.VMEM((N_SEG, N_STATES), f32)] * 2
        name = "ssm_scan_bwd_carry"
    res, landed = _pcall(body, args, name=name, grid=(nblk,), in_specs=in_specs, out_specs=out_specs,
                         out_shape=out_shape, scratch_shapes=scratch, rides=rides)
    return (res, landed) if rides else res


def ssm_param_bwd(dbb_r, dbb_i, bt_re, bt_im, kr, ki, ar, ai, lr, li, ldt, da_r, da_i):
    def body(dbr_ref, dbi_ref, br_ref, bi_ref, kr_ref, ki_ref, ar_ref, ai_ref, lr_ref, li_ref, ldt_ref, dar_ref,
             dai_ref, obr_ref, obi_ref, olr_ref, oli_ref, odt_ref):
        dbr, dbi, b_r, b_i = dbr_ref[...], dbi_ref[...], br_ref[...], bi_ref[...]
        k_r, k_i, a_r, a_i = kr_ref[...], ki_ref[...], ar_ref[...], ai_ref[...]
        l_r, l_i = lr_ref[...], li_ref[...]
        dt = jnp.exp(ldt_ref[...])
        obr_ref[...] = k_r * dbr + k_i * dbi
        obi_ref[...] = k_r * dbi - k_i * dbr
        gk_r = jnp.sum(dbr * b_r + dbi * b_i, axis=0, keepdims=True)
        gk_i = jnp.sum(dbi * b_r - dbr * b_i, axis=0, keepdims=True)
        ga_r = jnp.sum(dar_ref[...], axis=0, keepdims=True)
        ga_i = jnp.sum(dai_ref[...], axis=0, keepdims=True)
        den = l_r * l_r + l_i * l_i
        c_r, c_i = l_r / den, l_i / den
        m_r, m_i = _cmul(c_r, c_i, gk_r, gk_i)
        g_r, g_i = ga_r + m_r, ga_i + m_i
        t1_r, t1_i = _cmul(dt * a_r, -dt * a_i, g_r, g_i)
        q_r, q_i = _cmul(k_r, k_i, c_r, -c_i)
        t2_r, t2_i = _cmul(-q_r, q_i, gk_r, gk_i)
        olr_ref[...] = t1_r + t2_r
        oli_ref[...] = t1_i + t2_i
        w_r, w_i = _cmul(l_r, l_i, a_r, a_i)
        odt_ref[...] = dt * (w_r * g_r + w_i * g_i)

    v = jax.ShapeDtypeStruct((1, N_STATES), f32)
    m = jax.ShapeDtypeStruct((SSM_GROUP, N_STATES), f32)
    return pl.pallas_call(body, name="ssm_param_bwd", out_shape=[m, m, v, v, v])(
        dbb_r, dbb_i, bt_re, bt_im, kr, ki, ar, ai, lr, li, ldt, da_r, da_i)


def group_sum(v):
    def body(v_ref, o_ref):
        o_ref[...] = jnp.sum(v_ref[...], axis=-1, keepdims=True)
    return pl.pallas_call(body, name="group_sum", out_shape=jax.ShapeDtypeStruct((v.shape[0], 1), f32))(v)


ATTN_SCALE = XATTN_HEAD_DIM ** -0.5


def _attn_probs(q_h, k_h):
    s = lax.dot_general(q_h, k_h, NT_DIMS, preferred_element_type=f32) * ATTN_SCALE
    e = jnp.exp(s - jnp.max(s, axis=-1, keepdims=True))
    return e / jnp.sum(e, axis=-1, keepdims=True)


def attn_fwd(proj, kv, tm=TM):
    S = proj.shape[0]
    M = kv.shape[0]

    def body(q_ref, kv_ref, o_ref):
        for h in range(XATTN_HEADS):
            cols = slice(h * XATTN_HEAD_DIM, (h + 1) * XATTN_HEAD_DIM)
            q_h = q_ref[:, cols].astype(bf16)
            k_h = kv_ref[:, cols]
            v_h = kv_ref[:, XATTN_WIDTH + h * XATTN_HEAD_DIM:XATTN_WIDTH + (h + 1) * XATTN_HEAD_DIM]
            p = _attn_probs(q_h, k_h)
            o_ref[:, cols] = jnp.dot(p.astype(bf16), v_h, preferred_element_type=f32).astype(bf16)

    return pl.pallas_call(
        body, name="attn_fwd", grid=(S // tm,), in_specs=[_rows(tm, XATTN_WIDTH, 10), _full((M, 2 * XATTN_WIDTH))],
        out_specs=_rows(tm, XATTN_WIDTH), out_shape=jax.ShapeDtypeStruct((S, XATTN_WIDTH), bf16),
        compiler_params=_params())(proj, kv)


def attn_bwd(do, proj, kv, dproj, tm=TM):
    S = proj.shape[0]
    M = kv.shape[0]

    def body(do_ref, q_ref, kv_ref, _, dq_ref, dkv_ref):
        i = pl.program_id(0)

        @pl.when(i == 0)
        def _():
            dkv_ref[...] = jnp.zeros_like(dkv_ref)

        for h in range(XATTN_HEADS):
            cols = slice(h * XATTN_HEAD_DIM, (h + 1) * XATTN_HEAD_DIM)
            vcols = slice(XATTN_WIDTH + h * XATTN_HEAD_DIM, XATTN_WIDTH + (h + 1) * XATTN_HEAD_DIM)
            q_h = q_ref[:, cols].astype(bf16)
            k_h = kv_ref[:, cols]
            v_h = kv_ref[:, vcols]
            do_h = do_ref[:, cols]
            p = _attn_probs(q_h, k_h)
            dp = lax.dot_general(do_h, v_h, NT_DIMS, preferred_element_type=f32)
            ds = (p * (dp - jnp.sum(dp * p, axis=-1, keepdims=True)) * ATTN_SCALE).astype(bf16)
            dq_ref[:, cols] = jnp.dot(ds, k_h, preferred_element_type=f32).astype(bf16)
            dkv_ref[:, cols] += lax.dot_general(ds, q_h, TN_DIMS, preferred_element_type=f32)
            dkv_ref[:, vcols] += lax.dot_general(p.astype(bf16), do_h, TN_DIMS, preferred_element_type=f32)

    return pl.pallas_call(
        body, name="attn_bwd", grid=(S // tm,),
        in_specs=[_rows(tm, XATTN_WIDTH), _rows(tm, XATTN_WIDTH, 10), _full((M, 2 * XATTN_WIDTH)),
                  pl.BlockSpec(memory_space=pl.ANY)],
        out_specs=[_rows(tm, XATTN_WIDTH, 10), _full((M, 2 * XATTN_WIDTH))],
        out_shape=[jax.ShapeDtypeStruct(dproj.shape, bf16), jax.ShapeDtypeStruct((M, 2 * XATTN_WIDTH), f32)],
        input_output_aliases={3: 0},
        compiler_params=_params())(do, proj, kv, dproj)


def merge_fwd(proj, b_gate, y_a, glu, y_c, tm=TM, rides=()):
    S = proj.shape[0]

    def body(g0_ref, g1_ref, g2_ref, b_ref, ya_ref, ga_ref, gb_ref, yc_ref, o_ref):
        b = b_ref[...]
        g0 = _sig(_f32(g0_ref) + b[:, 0:D_MODEL])
        g1 = _sig(_f32(g1_ref) + b[:, D_MODEL:2 * D_MODEL])
        g2 = _sig(_f32(g2_ref) + b[:, 2 * D_MODEL:3 * D_MODEL])
        y_b = ga_ref[...].astype(f32) * _sig(gb_ref[...].astype(f32))
        o_ref[...] = (g0 * ya_ref[...].astype(f32) + g1 * y_b + g2 * yc_ref[...].astype(f32)).astype(bf16)

    res, landed = _pcall(
        body, [proj, proj, proj, b_gate, y_a, glu, glu, y_c], name="merge_fwd", grid=(S // tm,),
        in_specs=[_rows(tm, D_MODEL, 0), _rows(tm, D_MODEL, 1), _rows(tm, D_MODEL, 2), _full((1, GATE_COLS)),
                  _rows(tm, D_MODEL), _rows(tm, D_MODEL, 0), _rows(tm, D_MODEL, 1), _rows(tm, D_MODEL)],
        out_specs=[_rows(tm, D_MODEL)], out_shape=[jax.ShapeDtypeStruct((S, D_MODEL), bf16)], rides=rides)
    return (res[0], landed) if rides else res[0]


def merge_bwd(dm, proj, b_gate, y_a, glu, y_c, tm=256, rides=()):
    S = proj.shape[0]

    def body(dm_ref, g0_ref, g1_ref, g2_ref, b_ref, ya_ref, ga_ref, gb_ref, yc_ref,
             dg_ref, dya_ref, dgl_ref, dyc_ref, dbg_ref):
        i = pl.program_id(0)
        b = b_ref[...]
        dmv = dm_ref[...].astype(f32)
        ga, sb = ga_ref[...].astype(f32), _sig(gb_ref[...].astype(f32))
        ys = (ya_ref[...].astype(f32), ga * sb, yc_ref[...].astype(f32))
        gs = tuple(_sig(_f32(r) + b[:, k * D_MODEL:(k + 1) * D_MODEL]) for k, r in enumerate((g0_ref, g1_ref, g2_ref)))

        @pl.when(i == 0)
        def _():
            dbg_ref[...] = jnp.zeros_like(dbg_ref)

        for k in range(3):
            dpre = dmv * ys[k] * gs[k] * (1.0 - gs[k])
            dg_ref[:, k * D_MODEL:(k + 1) * D_MODEL] = dpre.astype(bf16)
            dbg_ref[:, k * D_MODEL:(k + 1) * D_MODEL] += jnp.sum(dpre, axis=0, keepdims=True)
        dya_ref[...] = (dmv * gs[0]).astype(bf16)
        dyc_ref[...] = (dmv * gs[2]).astype(bf16)
        dyb = dmv * gs[1]
        dgl_ref[:, 0:D_MODEL] = (dyb * sb).astype(bf16)
        dgl_ref[:, D_MODEL:2 * D_MODEL] = (dyb * ga * sb * (1.0 - sb)).astype(bf16)

    res, landed = _pcall(
        body, [dm, proj, proj, proj, b_gate, y_a, glu, glu, y_c], name="merge_bwd", grid=(S // tm,),
        in_specs=[_rows(tm, D_MODEL), _rows(tm, D_MODEL, 0), _rows(tm, D_MODEL, 1), _rows(tm, D_MODEL, 2),
                  _full((1, GATE_COLS)), _rows(tm, D_MODEL), _rows(tm, D_MODEL, 0), _rows(tm, D_MODEL, 1),
                  _rows(tm, D_MODEL)],
        out_specs=[_rows(tm, GATE_COLS), _rows(tm, D_MODEL), _rows(tm, 2 * D_MODEL), _rows(tm, D_MODEL),
                   _full((1, GATE_COLS))],
        out_shape=[jax.ShapeDtypeStruct((S, IN_COLS), bf16), jax.ShapeDtypeStruct((S, D_MODEL), bf16),
                   jax.ShapeDtypeStruct((S, 2 * D_MODEL), bf16), jax.ShapeDtypeStruct((S, D_MODEL), bf16),
                   jax.ShapeDtypeStruct((1, GATE_COLS), f32)], rides=rides)
    return (res, landed) if rides else res


def _ln_stats(r):
    mu = jnp.mean(r, axis=-1, keepdims=True)
    xc = r - mu
    var = jnp.mean(xc * xc, axis=-1, keepdims=True)
    rstd = lax.rsqrt(var + LN_EPS)
    return xc * rstd, rstd


def _ln_bwd(dy, xhat, rstd, g):
    dxh = dy * g
    return rstd * (dxh - jnp.mean(dxh, axis=-1, keepdims=True) - xhat * jnp.mean(dxh * xhat, axis=-1, keepdims=True))


def _colsum(v):
    return jnp.sum(v, axis=0, keepdims=True)


def _ln2_loss_epi(t2, h1, target, b_down, g, b):
    xhat, rstd = _ln_stats(ALPHA * h1 + t2 + b_down)
    err = xhat * g + b - target
    dout = err * (1.0 / D_MODEL)
    dr = _ln_bwd(dout, xhat, rstd, g)
    return (dr, dr), (0.5 * _colsum(err * err) * (1.0 / D_MODEL), _colsum(dout * xhat), _colsum(dout), _colsum(dr))


def _ln1_fwd_epi(t1, x, g, b):
    xhat, _ = _ln_stats(ALPHA * x + t1)
    h = xhat * g + b
    return (t1, h, h), ()


def _ln1_bwd_epi(t3, x, t1, dr2, g):
    xhat, rstd = _ln_stats(ALPHA * x + t1)
    dh = ALPHA * dr2 + t3
    dr = _ln_bwd(dh, xhat, rstd, g)
    return (dr, dr), (_colsum(dh * xhat), _colsum(dh))


def exchange(*rides, name):
    return _pcall(lambda: None, [], name=name, grid=(), in_specs=[], out_specs=[], out_shape=[], rides=rides)[1]


def gather_two_level(shard, name):
    K, n = shard.shape
    h = K // 2

    def body(in_ref, out_ref, ici_send, ici_recv, d2d_send, d2d_recv, local_sem):
        x, y, c = lax.axis_index("x"), lax.axis_index("y"), lax.axis_index("c")
        me = 2 * x + y
        mine = pl.ds(pl.multiple_of(c * h, 16), h)
        theirs = pl.ds(pl.multiple_of((1 - c) * h, 16), h)
        peers = _xy_peers(x, y)

        def over_ici(k, dst):
            return pltpu.make_async_remote_copy(src_ref=in_ref.at[mine], dst_ref=dst, send_sem=ici_send.at[k],
                                                recv_sem=ici_recv.at[k], device_id=(*peers[k], c), device_id_type=MESH)

        def over_d2d(k, rows):
            blk = out_ref.at[2 * peers[k][0] + peers[k][1], rows]
            return pltpu.make_async_remote_copy(src_ref=blk, dst_ref=blk, send_sem=d2d_send.at[k],
                                                recv_sem=d2d_recv.at[k], device_id=(x, y, 1 - c), device_id_type=MESH)

        local = pltpu.make_async_copy(in_ref, out_ref.at[me], local_sem)
        local.start()
        started = [over_ici(k, out_ref.at[me, mine]) for k in range(3)]
        for cp in started:
            cp.start()
        for k in range(3):
            over_ici(k, out_ref.at[2 * peers[k][0] + peers[k][1], mine]).wait_recv()
            passed = over_d2d(k, mine)
            passed.start()
            started.append(passed)
        for k in range(3):
            over_d2d(k, theirs).wait_recv()
        for cp in started:
            cp.wait_send()
        local.wait()

    return pl.pallas_call(
        body, name=name, in_specs=[HBM_SPEC], out_specs=HBM_SPEC,
        out_shape=jax.ShapeDtypeStruct((4, K, n), shard.dtype),
        scratch_shapes=[pltpu.SemaphoreType.DMA((3,))] * 4 + [pltpu.SemaphoreType.DMA(())])(shard)


ROW_TILE = 256


def sum_slots(recv, name):
    n, R, C = recv.shape
    tm = min(R, ROW_TILE)

    def body(r_ref, o_ref):
        acc = r_ref[0].astype(f32)
        for k in range(1, n):
            acc = acc + r_ref[k].astype(f32)
        o_ref[...] = acc

    return pl.pallas_call(
        body, name=name, grid=(R // tm,), in_specs=[pl.BlockSpec((n, tm, C), lambda i: (0, i, 0))],
        out_specs=_rows(tm, C), out_shape=jax.ShapeDtypeStruct((R, C), f32), compiler_params=_params())(recv)


def _adamw(w, g, m, v):
    m = ADAM_B1 * m + (1.0 - ADAM_B1) * g
    v = ADAM_B2 * v + (1.0 - ADAM_B2) * (g * g)
    m_hat = m / (1.0 - ADAM_B1 ** ADAM_STEP)
    v_hat = v / (1.0 - ADAM_B2 ** ADAM_STEP)
    delta = -ADAM_LR * (m_hat / (jnp.sqrt(v_hat) + ADAM_EPS) + ADAM_WD * w)
    return delta, m, v


def adam_pair(p, q, w, m, v, name):
    R, C = w.shape
    tm = min(R, ROW_TILE)

    def body(p_ref, q_ref, w_ref, m_ref, v_ref, g_ref, d_ref, nm_ref, nv_ref):
        g = p_ref[...] + q_ref[...]
        g_ref[...] = g
        d_ref[...], nm_ref[...], nv_ref[...] = _adamw(w_ref[...], g, m_ref[...], v_ref[...])

    o = jax.ShapeDtypeStruct((R, C), f32)
    return pl.pallas_call(body, name=name, grid=(R // tm,), in_specs=[_rows(tm, C)] * 5,
                          out_specs=[_rows(tm, C)] * 4, out_shape=[o] * 4, compiler_params=_params())(p, q, w, m, v)


def adam_slots(recv, w, m, v, name):
    n = recv.shape[0]

    def body(r_ref, w_ref, m_ref, v_ref, g_ref, d_ref, nm_ref, nv_ref):
        g = r_ref[0]
        for k in range(1, n):
            g = g + r_ref[k]
        g_ref[...] = g
        d_ref[...], nm_ref[...], nv_ref[...] = _adamw(w_ref[...], g, m_ref[...], v_ref[...])

    o = jax.ShapeDtypeStruct(w.shape, f32)
    return pl.pallas_call(body, name=name, out_shape=[o] * 4, compiler_params=_params())(recv, w, m, v)


BIG = ("w_in", "w_conv_out", "w_glu", "w_xattn_out", "w_kv", "w_out", "w_up", "w_down")
MID = ("w_conv_out", "w_glu", "w_xattn_out", "w_kv", "w_out")
SMALL = ("b_gate", "ssm_lam_re", "ssm_lam_im", "ssm_log_dt", "ssm_b_re", "ssm_b_im", "ssm_c_re", "ssm_c_im", "ssm_d",
         "ln1_g", "ln1_b", "b_up", "b_down", "ln2_g", "ln2_b")
def _pad_flat(a, mult=1024):
    a = a.reshape(-1)
    return jnp.pad(a, (0, (-a.shape[0]) % mult))


def _perm(a):
    S, W = a.shape
    return a.reshape(N_SEG, S // N_SEG, W).transpose(1, 0, 2).reshape(S, W)


def _unperm(a):
    S, W = a.shape
    return a.reshape(S // N_SEG, N_SEG, W).transpose(1, 0, 2).reshape(S, W)


def _state_rows(a):
    return a.transpose(2, 0, 1).reshape(SSM_GROUP, N_STATES)


def _block_diag_b(bt):
    b4 = bt.reshape(SSM_GROUP, 4, 8, SSM_STATE)
    eye = jnp.eye(8, dtype=bt.dtype)
    return jnp.einsum("hcgp,gk->cghkp", b4, eye).reshape(4, 128, 512)


def _block_diag_c(c):
    c4 = c.reshape(4, 8, SSM_GROUP, SSM_STATE)
    eye = jnp.eye(8, dtype=c.dtype)
    return jnp.einsum("cghp,gk->cgpkh", c4, eye).reshape(4, 512, 128)


def _diag_b(acc):
    a = acc.reshape(4, 8, SSM_STATE, 8, SSM_GROUP)
    eye = jnp.eye(8, dtype=acc.dtype)
    return jnp.einsum("cgpkh,gk->hcgp", a, eye).reshape(SSM_GROUP, N_STATES)


def _diag_c(acc):
    a = acc.reshape(4, 8, SSM_GROUP, 8, SSM_STATE)
    eye = jnp.eye(8, dtype=acc.dtype)
    return jnp.einsum("cghkp,gk->cghp", a, eye).reshape(SSM_GROUPS, SSM_GROUP, SSM_STATE)


def kernel(x, mem, w_in, b_gate, conv_w, w_conv_out, ssm_lam_re, ssm_lam_im, ssm_log_dt, ssm_b_re, ssm_b_im, ssm_c_re, ssm_c_im, ssm_d, w_glu, w_kv, w_xattn_out, w_out, ln1_g, ln1_b, w_up, b_up, w_down, b_down, ln2_g, ln2_b, loss_target, m_w_in, m_b_gate, m_conv_w, m_w_conv_out, m_ssm_lam_re, m_ssm_lam_im, m_ssm_log_dt, m_ssm_b_re, m_ssm_b_im, m_ssm_c_re, m_ssm_c_im, m_ssm_d, m_w_glu, m_w_kv, m_w_xattn_out, m_w_out, m_ln1_g, m_ln1_b, m_w_up, m_b_up, m_w_down, m_b_down, m_ln2_g, m_ln2_b, v_w_in, v_b_gate, v_conv_w, v_w_conv_out, v_ssm_lam_re, v_ssm_lam_im, v_ssm_log_dt, v_ssm_b_re, v_ssm_b_im, v_ssm_c_re, v_ssm_c_im, v_ssm_d, v_w_glu, v_w_kv, v_w_xattn_out, v_w_out, v_ln1_g, v_ln1_b, v_w_up, v_b_up, v_w_down, v_b_down, v_ln2_g, v_ln2_b):
    W = dict(w_in=w_in, b_gate=b_gate, conv_w=conv_w, w_conv_out=w_conv_out, ssm_lam_re=ssm_lam_re,
             ssm_lam_im=ssm_lam_im, ssm_log_dt=ssm_log_dt, ssm_b_re=ssm_b_re, ssm_b_im=ssm_b_im, ssm_c_re=ssm_c_re,
             ssm_c_im=ssm_c_im, ssm_d=ssm_d, w_glu=w_glu, w_kv=w_kv, w_xattn_out=w_xattn_out, w_out=w_out,
             ln1_g=ln1_g, ln1_b=ln1_b, w_up=w_up, b_up=b_up, w_down=w_down, b_down=b_down, ln2_g=ln2_g, ln2_b=ln2_b)
    MOM = dict(w_in=m_w_in, b_gate=m_b_gate, conv_w=m_conv_w, w_conv_out=m_w_conv_out, ssm_lam_re=m_ssm_lam_re,
               ssm_lam_im=m_ssm_lam_im, ssm_log_dt=m_ssm_log_dt, ssm_b_re=m_ssm_b_re, ssm_b_im=m_ssm_b_im,
               ssm_c_re=m_ssm_c_re, ssm_c_im=m_ssm_c_im, ssm_d=m_ssm_d, w_glu=m_w_glu, w_kv=m_w_kv,
               w_xattn_out=m_w_xattn_out, w_out=m_w_out, ln1_g=m_ln1_g, ln1_b=m_ln1_b, w_up=m_w_up, b_up=m_b_up,
               w_down=m_w_down, b_down=m_b_down, ln2_g=m_ln2_g, ln2_b=m_ln2_b)
    VEL = dict(w_in=v_w_in, b_gate=v_b_gate, conv_w=v_conv_w, w_conv_out=v_w_conv_out, ssm_lam_re=v_ssm_lam_re,
               ssm_lam_im=v_ssm_lam_im, ssm_log_dt=v_ssm_log_dt, ssm_b_re=v_ssm_b_re, ssm_b_im=v_ssm_b_im,
               ssm_c_re=v_ssm_c_re, ssm_c_im=v_ssm_c_im, ssm_d=v_ssm_d, w_glu=v_w_glu, w_kv=v_w_kv,
               w_xattn_out=v_w_xattn_out, w_out=v_w_out, ln1_g=v_ln1_g, ln1_b=v_ln1_b, w_up=v_w_up, b_up=v_b_up,
               w_down=v_w_down, b_down=v_b_down, ln2_g=v_ln2_g, ln2_b=v_ln2_b)
    names = list(W)
    xy = 2 * lax.axis_index("x") + lax.axis_index("y")

    xs = x[0]
    S = xs.shape[0]
    mems = mem[0]
    tgt = loss_target[0]

    shard_bf = {n: W[n][0].astype(bf16) for n in BIG}
    win3 = gather_two_level(shard_bf["w_in"], "gather_w_in")

    lr = ssm_lam_re.reshape(1, N_STATES)
    li = ssm_lam_im.reshape(1, N_STATES)
    ldt = jnp.repeat(ssm_log_dt.reshape(SSM_GROUPS), SSM_STATE).reshape(1, N_STATES)
    bt_re, bt_im = _state_rows(ssm_b_re[0]), _state_rows(ssm_b_im[0])
    ar, ai, kr, ki, bbt_r, bbt_i = ssm_prep(lr, li, ldt, bt_re, bt_im)
    bd_r, bd_i = _block_diag_b(bbt_r).astype(bf16), _block_diag_b(bbt_i).astype(bf16)
    cd_r, cd_i = _block_diag_c(ssm_c_re[0]).astype(bf16), _block_diag_c(ssm_c_im[0]).astype(bf16)
    bdt_r, bdt_i = bd_r.transpose(0, 2, 1), bd_i.transpose(0, 2, 1)
    cdt_r, cdt_i = cd_r.transpose(0, 2, 1), cd_i.transpose(0, 2, 1)
    d_skip = ssm_d.reshape(1, SSM_WIDTH)

    mem_bf = mems.astype(bf16)
    u_col = GATE_COLS + 3 * CONV_WIDTH
    (proj, x_bf), (wco3, wglu3, wxo3, wkv4, wout4, convw4) = mm_fwd(
        xs, win3, "proj", outs=(bf16,), emit_a=True,
        rides=tuple(("gather", shard_bf[n]) for n in MID) + (("gather", conv_w[0]),))
    wkv3 = wkv4.reshape(1, D_MODEL, 2 * XATTN_WIDTH)
    wout3 = wout4.reshape(1, D_MODEL, D_MODEL)
    convw_full = convw4.transpose(1, 0, 2).reshape(3, CONV_WIDTH)
    a_conv = conv_fwd(proj, convw_full)
    y_a = mm_fwd_small(a_conv, wco3, "conv_out", out_dtype=bf16)
    u_perm = _perm(proj[:, u_col:u_col + SSM_WIDTH])
    half = D_MODEL // 2
    (init_r, init_i), (wup_a,) = ssm_scan_fwd(u_perm, bd_r, bd_i, ar, ai,
                                              rides=(("gather", shard_bf["w_up"][:half]),))
    (ysm_perm, ys_perm, st_r, st_i), (wup_b,) = ssm_scan_fwd(
        u_perm, bd_r, bd_i, ar, ai, final=(cd_r, cd_i, d_skip, init_r, init_i),
        rides=(("gather", shard_bf["w_up"][half:]),))
    wup3 = jnp.concatenate([wup_a, wup_b], axis=1)
    y_s = _unperm(ys_perm)
    glu = mm_fwd_small(y_s, wglu3, "glu", out_dtype=bf16)
    kv = mm_fwd_small(mem_bf, wkv3, "kv", out_dtype=bf16)
    o_att = attn_fwd(proj, kv)
    y_c = mm_fwd_small(o_att, wxo3, "xattn_out", out_dtype=bf16)
    merged = merge_fwd(proj, b_gate, y_a, glu, y_c)
    t1, h1, h1_bf = mm_rows_fused(merged, wout3, "w_out_ln1", transposed=False, tm=512, epi=_ln1_fwd_epi,
                                  extras=(xs,), consts=(ln1_g, ln1_b), outs=(f32, f32, bf16))
    (r_up, hdn), (wdn4,) = mm_fwd(h1_bf, wup3, "w_up", bias=b_up, outs=(bf16, bf16), second=_relu2,
                                  first=lambda v: jnp.maximum(v, 0.0), rides=(("gather", shard_bf["w_down"]),))
    wdn3 = wdn4.reshape(1, D_FF, D_MODEL)

    dr2, dr2_bf, loss_cols, d_ln2_g, d_ln2_b, d_b_down = mm_rows_fused(
        hdn, wdn3, "w_down_ln2_loss", transposed=False, tm=512, epi=_ln2_loss_epi, extras=(h1, tgt),
        consts=(b_down, ln2_g, ln2_b), outs=(f32, bf16), nsums=4)
    part, other = {}, {}
    g_w_down = mm_bwd_w(hdn, dr2_bf, 1, "dw_down", tk=1024).reshape(4, -1, D_MODEL)
    (dup, d_b_up), (recv_dn,) = mm_bwd_x(dr2_bf, wdn3, "dup", tm=512, extras=(r_up,), colsum=True, out_dtype=bf16,
                                         epi=lambda acc, r: acc * (2.0 * r.astype(f32)),
                                         rides=(("scatter", g_w_down),))
    part["w_down"] = sum_slots(recv_dn, "sum_w_down")
    g_w_up = mm_bwd_w(h1_bf, dup, 4, "dw_up")
    dr1, dr1_bf, d_ln1_g, d_ln1_b = mm_rows_fused(
        dup, wup3, "dh1_ln1_bwd", transposed=True, tm=256, epi=_ln1_bwd_epi, extras=(xs, t1, dr2), consts=(ln1_g,),
        outs=(f32, bf16), nsums=2)
    g_w_out = mm_bwd_w(merged, dr1_bf, 1, "dw_out").reshape(4, -1, D_MODEL)
    dmerged, (recv_out,) = mm_bwd_x(dr1_bf, wout3, "dmerged", tm=1024, out_dtype=bf16,
                                    rides=(("scatter", g_w_out),))
    part["w_out"] = sum_slots(recv_out, "sum_w_out")
    (dproj, dy_a, dglu, dy_c, d_b_gate), (recv_up,) = merge_bwd(dmerged, proj, b_gate, y_a, glu, y_c,
                                                                rides=(("scatter", g_w_up),))
    part["w_up"] = sum_slots(recv_up, "sum_w_up")

    g_w_co = mm_bwd_w_small(a_conv, dy_a, 4, "dw_conv_out")
    da_conv = mm_bwd_x(dy_a, wco3, "da_conv", tm=1024)
    dproj, d_conv_w = conv_bwd(da_conv, proj, convw_full, dproj)

    g_w_glu = mm_bwd_w_small(y_s, dglu, 4, "dw_glu")
    dys_perm = _perm(mm_bwd_x(dglu, wglu3, "dy_s", tm=1024))
    linit_r, linit_i = ssm_scan_bwd(dys_perm, ysm_perm, cdt_r, cdt_i, ar, ai)
    (du_perm, dbacc_r, dbacc_i, dcacc_r, dcacc_i, da_r, da_i, d_ssm_d), (recv_co, recv_glu) = ssm_scan_bwd(
        dys_perm, ysm_perm, cdt_r, cdt_i, ar, ai,
        final=(u_perm, st_r, st_i, bdt_r, bdt_i, d_skip, linit_r, linit_i),
        rides=(("scatter", g_w_co), ("scatter", g_w_glu)))
    part["w_conv_out"] = sum_slots(recv_co, "sum_w_conv_out")
    part["w_glu"] = sum_slots(recv_glu, "sum_w_glu")
    dbt_re, dbt_im, d_lr, d_li, d_ldt_state = ssm_param_bwd(
        _diag_b(dbacc_r), _diag_b(dbacc_i), bt_re, bt_im, kr, ki, ar, ai, lr, li, ldt, da_r, da_i)
    d_log_dt = group_sum(d_ldt_state.reshape(SSM_GROUPS, SSM_STATE))
    d_b_re = dbt_re.reshape(SSM_GROUP, SSM_GROUPS, SSM_STATE).transpose(1, 2, 0)
    d_b_im = dbt_im.reshape(SSM_GROUP, SSM_GROUPS, SSM_STATE).transpose(1, 2, 0)
    d_c_re = _diag_c(dcacc_r)
    d_c_im = -_diag_c(dcacc_i)

    g_w_xo = mm_bwd_w_small(o_att, dy_c, 4, "dw_xattn_out")
    do_att = mm_bwd_x(dy_c, wxo3, "do_att", tm=1024, out_dtype=bf16)
    dproj, dkv = attn_bwd(do_att, proj, kv, dproj)
    g_w_kv = mm_bwd_w(mem_bf, dkv, 1, "dw_kv").reshape(4, -1, D_MODEL)

    dproj = lax.dynamic_update_slice(dproj, _unperm(du_perm), (0, u_col))
    g_small = {"b_gate": d_b_gate, "ssm_lam_re": d_lr, "ssm_lam_im": d_li, "ssm_log_dt": d_log_dt, "ssm_b_re": d_b_re,
               "ssm_b_im": d_b_im, "ssm_c_re": d_c_re, "ssm_c_im": d_c_im, "ssm_d": d_ssm_d, "ln1_g": d_ln1_g,
               "ln1_b": d_ln1_b, "b_up": d_b_up, "b_down": d_b_down, "ln2_g": d_ln2_g, "ln2_b": d_ln2_b}
    small_names = SMALL + ("conv_w",)
    g_small["conv_w"] = d_conv_w
    sizes = {n: (g_small[n].size + 1023) // 1024 * 1024 for n in small_names}
    pack = lambda d: jnp.concatenate([_pad_flat(d[n]) for n in small_names]).reshape(-1, 128)
    early = ("w_down", "w_up", "w_out", "w_conv_out", "w_glu")
    g_w_in, landed = mm_bwd_w(
        x_bf, dproj, 4, "dw_in", rides=(("scatter", g_w_xo), ("scatter", g_w_kv), ("all", pack(g_small)))
        + tuple(("pair", part[n]) for n in early))
    recv_xo, recv_kv, srecv = landed[:3]
    other.update(zip(early, landed[3:]))
    part["w_xattn_out"] = sum_slots(recv_xo, "sum_w_xattn_out")
    part["w_kv"] = sum_slots(recv_kv, "sum_w_kv")
    dx, (recv_in, other["w_xattn_out"], other["w_kv"]) = mm_bwd_x(
        dproj, win3, "dx", extras=(dr1,), epi=lambda acc, d: acc + ALPHA * d,
        rides=(("scatter", g_w_in), ("pair", part["w_xattn_out"]), ("pair", part["w_kv"])))
    part["w_in"] = sum_slots(recv_in, "sum_w_in")
    other["w_in"], = exchange(("pair", part["w_in"]), name="swap_w_in")
    res = [{}, {}, {}, {}]
    for n in BIG:
        for k, r in enumerate(adam_pair(part[n], other[n], W[n][0], MOM[n][0], VEL[n][0], "adam_" + n)):
            res[k][n] = r[None]
    conv_zero = jnp.zeros((3, CONV_WIDTH), f32)
    gs, ds_, ms, vs = adam_slots(srecv, pack({**{n: W[n] for n in SMALL}, "conv_w": conv_zero}),
                                 pack({**{n: MOM[n] for n in SMALL}, "conv_w": conv_zero}),
                                 pack({**{n: VEL[n] for n in SMALL}, "conv_w": conv_zero}), "adam_small")

    def unpack_small(buf):
        flat = buf.reshape(-1)
        out, r = {}, 0
        for n in small_names:
            ref = g_small[n] if n == "conv_w" else W[n]
            out[n] = flat[r:r + ref.size].reshape(ref.shape)
            r += sizes[n]
        return out

    res_s = [unpack_small(b) for b in (gs, ds_, ms, vs)]
    g_conv = lax.dynamic_slice(res_s[0]["conv_w"], (0, xy * 128), (3, 128))
    conv_slots = g_conv.reshape(1, 3, 128)
    cg, cd, cm, cv = adam_slots(conv_slots, conv_w[0], m_conv_w[0], v_conv_w[0], "adam_conv")
    conv_res = [cg, cd, cm, cv]

    loss = lax.psum(jnp.sum(loss_cols), ("x", "y", "c"))
    outs = [loss, dx.reshape(x.shape)]
    for k in range(4):
        for n in names:
            if n == "conv_w":
                outs.append(conv_res[k].reshape(conv_w.shape))
            elif n in BIG:
                outs.append(res[k][n])
            else:
                outs.append(res_s[k][n])
    return tuple(outs)
```

```python
import functools
import math

import jax
import jax.numpy as jnp
from jax import lax
from jax.experimental import pallas as pl
from jax.experimental.pallas import tpu as pltpu

f32 = jnp.float32
bf16 = jnp.bfloat16

D_MODEL = 1024
CONV_WIDTH = 512
SSM_WIDTH = 512
SSM_GROUP = 16
SSM_GROUPS = 32
SSM_STATE = 64
N_STATES = SSM_GROUPS * SSM_STATE
XATTN_HEADS = 4
XATTN_HEAD_DIM = 128
XATTN_WIDTH = 512
D_FF = 4096
GATE_COLS = 3 * D_MODEL
IN_COLS = GATE_COLS + 3 * CONV_WIDTH + SSM_WIDTH + XATTN_WIDTH
ALPHA = 2.0 ** 0.25
LN_EPS = 1e-5
ADAM_LR = 0.001
ADAM_B1 = 0.9
ADAM_B2 = 0.999
ADAM_EPS = 1e-08
ADAM_WD = 0.01
ADAM_STEP = 10

N_SEG = 8
SCAN_ROWS = 512
LANE_STRIP = 512
TM = 512
VMEM_LIMIT = 48 * 1024 * 1024
MESH = pl.DeviceIdType.MESH

NT_DIMS = (((1,), (1,)), ((), ()))
TN_DIMS = (((0,), (0,)), ((), ()))


def _params():
    return pltpu.CompilerParams(vmem_limit_bytes=VMEM_LIMIT)


def _full(shape):
    n = len(shape)
    return pl.BlockSpec(shape, lambda *_: (0,) * n)


def _rows(tm, w, cb=0):
    return pl.BlockSpec((tm, w), lambda i: (i, cb))


def _sig(x):
    return 1.0 / (1.0 + jnp.exp(-x))


HBM_SPEC = pl.BlockSpec(memory_space=pl.ANY)
RIDE_PEERS = {"gather": 3, "gather2": 6, "scatter": 3, "all": 7, "pair": 1}


def _xy_peers(x, y):
    return [(1 - x, y), (x, 1 - y), (1 - x, 1 - y)]


def _ride_copies(kind, src, dst, send_sems, recv_sems, local_sem):
    x, y, c = lax.axis_index("x"), lax.axis_index("y"), lax.axis_index("c")
    me = 2 * x + y
    if kind == "all":
        flips = [(fx, fy, fc) for fx in (0, 1) for fy in (0, 1) for fc in (0, 1)][1:]
        peers = [(x ^ fx, y ^ fy, c ^ fc) for fx, fy, fc in flips]
        slot = lambda p: 4 * p[0] + 2 * p[1] + p[2]
        mine = slot((x, y, c))
    else:
        peers = [(px, py, c) for px, py in _xy_peers(x, y)]
        slot = lambda p: 2 * p[0] + p[1]
        mine = me

    def remote(k, s, d):
        return pltpu.make_async_remote_copy(src_ref=s, dst_ref=d, send_sem=send_sems.at[k], recv_sem=recv_sems.at[k],
                                            device_id=peers[k], device_id_type=MESH)

    if kind == "pair":
        peers = [(x, y, 1 - c)]
        return None, [remote(0, src, dst)], [remote(0, src, dst)]
    if kind == "scatter":
        local = pltpu.make_async_copy(src.at[me], dst.at[me], local_sem)
        sends = [remote(k, src.at[slot(p)], dst.at[me]) for k, p in enumerate(peers)]
        lands = [remote(k, src.at[me], dst.at[slot(p)]) for k, p in enumerate(peers)]
    else:
        local = pltpu.make_async_copy(src, dst.at[mine], local_sem)
        sends = [remote(k, src, dst.at[mine]) for k, p in enumerate(peers)]
        lands = [remote(k, src, dst.at[slot(p)]) for k, p in enumerate(peers)]
    return local, sends, lands


def _ride_shape(kind, src):
    lead = {"gather": (4,), "gather2": (4,), "scatter": (), "all": (8,), "pair": ()}[kind]
    return jax.ShapeDtypeStruct(lead + src.shape, src.dtype)


def _two_level(src, dst, send_sems, recv_sems, local_sem):
    x, y, c = lax.axis_index("x"), lax.axis_index("y"), lax.axis_index("c")
    me = 2 * x + y
    h = src.shape[0] // 2
    mine = pl.ds(pl.multiple_of(c * h, 16), h)
    theirs = pl.ds(pl.multiple_of((1 - c) * h, 16), h)
    peers = _xy_peers(x, y)
    slots = [2 * px + py for px, py in peers]

    def over_ici(k, d):
        return pltpu.make_async_remote_copy(src_ref=src.at[mine], dst_ref=d, send_sem=send_sems.at[k],
                                            recv_sem=recv_sems.at[k], device_id=(*peers[k], c), device_id_type=MESH)

    def over_d2d(k, rows):
        blk = dst.at[slots[k], rows]
        return pltpu.make_async_remote_copy(src_ref=blk, dst_ref=blk, send_sem=send_sems.at[3 + k],
                                            recv_sem=recv_sems.at[3 + k], device_id=(x, y, 1 - c), device_id_type=MESH)

    return (pltpu.make_async_copy(src, dst.at[me], local_sem),
            [over_ici(k, dst.at[me, mine]) for k in range(3)], [over_ici(k, dst.at[slots[k], mine]) for k in range(3)],
            [over_d2d(k, mine) for k in range(3)], [over_d2d(k, theirs) for k in range(3)])


def _pcall(body, args, *, name, grid, in_specs, out_specs, out_shape, scratch_shapes=(), rides=(), aliases=None,
           prefetch=()):
    n_in, n_out, n_scr, nr, npf = len(in_specs), len(out_specs), len(scratch_shapes), len(rides), len(prefetch)
    kinds = [k for k, _ in rides]

    def wrapped(*refs):
        pre, refs = refs[:npf], refs[npf:]
        ins, rsrc = refs[:n_in], refs[n_in:n_in + nr]
        outs = refs[n_in + nr:n_in + nr + n_out]
        rdst = refs[n_in + nr + n_out:n_in + 2 * nr + n_out]
        scr = refs[n_in + 2 * nr + n_out:n_in + 2 * nr + n_out + n_scr]
        sems = refs[n_in + 2 * nr + n_out + n_scr:]

        def start():
            for r, kind in enumerate(kinds):
                if kind == "gather2":
                    local, sends, _, _, _ = _two_level(rsrc[r], rdst[r], *sems[3 * r:3 * r + 3])
                else:
                    local, sends, _ = _ride_copies(kind, rsrc[r], rdst[r], *sems[3 * r:3 * r + 3])
                if local is not None:
                    local.start()
                for cp in sends:
                    cp.start()

        def finish():
            for r, kind in enumerate(kinds):
                if kind == "gather2":
                    local, sends, ici_lands, forwards, lands = _two_level(rsrc[r], rdst[r], *sems[3 * r:3 * r + 3])
                    for k in range(3):
                        ici_lands[k].wait_recv()
                        forwards[k].start()
                    sends = sends + forwards
                else:
                    local, sends, lands = _ride_copies(kind, rsrc[r], rdst[r], *sems[3 * r:3 * r + 3])
                for cp in lands:
                    cp.wait_recv()
                for cp in sends:
                    cp.wait_send()
                if local is not None:
                    local.wait()

        if nr and grid:
            ids = [pl.program_id(a) for a in range(len(grid))]
            first = functools.reduce(jnp.logical_and, [i == 0 for i in ids])
            last = functools.reduce(jnp.logical_and, [i == g - 1 for i, g in zip(ids, grid)])
            pl.when(first)(start)
        elif nr:
            start()
        body(*pre, *ins, *outs, *scr)
        if nr and grid:
            pl.when(last)(finish)
        elif nr:
            finish()

    sems = []
    for kind in kinds:
        n = RIDE_PEERS[kind]
        sems += [pltpu.SemaphoreType.DMA((n,)), pltpu.SemaphoreType.DMA((n,)), pltpu.SemaphoreType.DMA(())]
    all_in = list(in_specs) + [HBM_SPEC] * nr
    all_out = list(out_specs) + [HBM_SPEC] * nr
    all_scratch = list(scratch_shapes) + sems
    if npf:
        how = dict(grid_spec=pltpu.PrefetchScalarGridSpec(num_scalar_prefetch=npf, grid=grid, in_specs=all_in,
                                                          out_specs=all_out, scratch_shapes=all_scratch))
    else:
        how = dict(grid=grid, in_specs=all_in, out_specs=all_out, scratch_shapes=all_scratch)
    res = pl.pallas_call(
        wrapped, name=name, out_shape=list(out_shape) + [_ride_shape(k, s) for k, s in rides],
        input_output_aliases=aliases or {}, compiler_params=_params(), **how)(
            *prefetch, *args, *[s for _, s in rides])
    return list(res[:n_out]), list(res[n_out:])


def _bf(v):
    return v if v.dtype == bf16 else v.astype(bf16)


def mm_fwd_small(a, w3, name, *, tm=1024, out_dtype=f32):
    M, K = a.shape
    J, _, n = w3.shape
    tm = min(tm, M)

    def body(a_ref, w_ref, o_ref):
        av = _bf(a_ref[...])
        for j in range(J):
            o_ref[:, j * n:(j + 1) * n] = jnp.dot(av, w_ref[j], preferred_element_type=f32).astype(out_dtype)

    return pl.pallas_call(
        body, name=name, grid=(M // tm,), in_specs=[_rows(tm, K), _full((J, K, n))], out_specs=_rows(tm, J * n),
        out_shape=jax.ShapeDtypeStruct((M, J * n), out_dtype), compiler_params=_params())(a, w3)


def mm_bwd_w_small(a, dy, J, name, *, tm=2048):
    M, K = a.shape
    n = dy.shape[1] // J
    tm = min(tm, M)
    ns = M // tm

    def body(a_ref, dy_ref, o_ref, acc_ref):
        s = pl.program_id(0)
        part = lax.dot_general(_bf(a_ref[...]), _bf(dy_ref[...]), TN_DIMS, preferred_element_type=f32)

        @pl.when(s == 0)
        def _():
            acc_ref[...] = part

        @pl.when(s > 0)
        def _():
            acc_ref[...] += part

        @pl.when(s == ns - 1)
        def _():
            for j in range(J):
                o_ref[j] = acc_ref[:, j * n:(j + 1) * n].astype(bf16)

    return pl.pallas_call(
        body, name=name, grid=(ns,), in_specs=[_rows(tm, K), _rows(tm, J * n)], out_specs=_full((J, K, n)),
        out_shape=jax.ShapeDtypeStruct((J, K, n), bf16), scratch_shapes=[pltpu.VMEM((K, J * n), f32)],
        compiler_params=_params())(a, dy)


def mm_fwd(a, w3, name, *, tm=1024, tn=None, bias=None, outs=(f32,), first=None, second=None, emit_a=False,
           rides=()):
    M, K = a.shape
    J, _, n = w3.shape
    tm = min(tm, M)
    tn = tn or n
    nl = n // tn
    nb = 0 if bias is None else 1

    def body(*refs):
        a_ref, w_ref = refs[0], refs[1]
        av = _bf(a_ref[...])
        acc = jnp.dot(av, w_ref[0], preferred_element_type=f32)
        if bias is not None:
            acc = acc + refs[2][...]
        refs[2 + nb][...] = (acc if first is None else first(acc)).astype(outs[0])
        if len(outs) > 1:
            refs[3 + nb][...] = second(acc).astype(outs[1])
        if emit_a:
            @pl.when(jnp.logical_and(pl.program_id(0) == 0, pl.program_id(1) == 0))
            def _():
                refs[2 + nb + len(outs)][...] = av

    in_specs = [pl.BlockSpec((tm, K), lambda j, l, i: (i, 0)),
                pl.BlockSpec((1, K, tn), lambda j, l, i: (j, 0, l))]
    args = [a, w3]
    if bias is not None:
        in_specs.append(pl.BlockSpec((1, tn), lambda j, l, i: (0, j * nl + l)))
        args.append(bias)
    out_specs = [pl.BlockSpec((tm, tn), lambda j, l, i: (i, j * nl + l)) for _ in outs]
    out_shape = [jax.ShapeDtypeStruct((M, J * n), dt) for dt in outs]
    if emit_a:
        last = M // tm - 1
        out_specs.append(pl.BlockSpec((tm, K), lambda j, l, i: (jnp.where((j == 0) & (l == 0), i, last), 0)))
        out_shape.append(jax.ShapeDtypeStruct((M, K), bf16))
    res, landed = _pcall(body, args, name=name, grid=(J, nl, M // tm), in_specs=in_specs, out_specs=out_specs,
                         out_shape=out_shape, rides=rides)
    res = res if len(res) > 1 else res[0]
    return (res, landed) if rides else res


def proj_own(x, w_own, me, n_blocks, *, tm=1024, rides=()):
    M, K = x.shape
    n = w_own.shape[1]

    def body(me_ref, x_ref, w_ref, o_ref, xb_ref):
        av = _bf(x_ref[...])
        xb_ref[...] = av
        o_ref[...] = jnp.dot(av, w_ref[...], preferred_element_type=f32).astype(bf16)

    (proj, x_bf), landed = _pcall(
        body, [x, w_own], name="proj_own", grid=(M // tm,), prefetch=(me,),
        in_specs=[pl.BlockSpec((tm, K), lambda i, me_ref: (i, 0)), pl.BlockSpec((K, n), lambda i, me_ref: (0, 0))],
        out_specs=[pl.BlockSpec((tm, n), lambda i, me_ref: (i, me_ref[0])),
                   pl.BlockSpec((tm, K), lambda i, me_ref: (i, 0))],
        out_shape=[jax.ShapeDtypeStruct((M, n_blocks * n), bf16), jax.ShapeDtypeStruct((M, K), bf16)], rides=rides)
    return proj, x_bf, landed


def proj_rest(x_bf, w3, me, proj, *, tm=1024, rides=()):
    M, K = x_bf.shape
    J, _, n = w3.shape

    def body(me_ref, x_ref, w_ref, _, o_ref):
        o_ref[...] = jnp.dot(x_ref[...], w_ref[0], preferred_element_type=f32).astype(bf16)

    (proj,), landed = _pcall(
        body, [x_bf, w3, proj], name="proj_rest", grid=(J - 1, M // tm), prefetch=(me,),
        in_specs=[pl.BlockSpec((tm, K), lambda k, i, me_ref: (i, 0)),
                  pl.BlockSpec((1, K, n), lambda k, i, me_ref: (me_ref[0] ^ (k + 1), 0, 0)), HBM_SPEC],
        out_specs=[pl.BlockSpec((tm, n), lambda k, i, me_ref: (i, me_ref[0] ^ (k + 1)))],
        out_shape=[jax.ShapeDtypeStruct(proj.shape, bf16)], aliases={3: 0}, rides=rides)
    return proj, landed


def mm_bwd_x(dy, w3, name, *, tm=TM, epi=None, extras=(), colsum=False, out_dtype=f32, rides=()):
    M = dy.shape[0]
    J, K, n = w3.shape
    tm = min(tm, M)
    nex = len(extras)

    def body(*refs):
        dy_ref, w_hbm = refs[0], refs[1]
        ex = refs[2:2 + nex]
        o_ref = refs[2 + nex]
        w_ref = refs[-1]
        i = pl.program_id(0)

        @pl.when(i == 0)
        def _():
            pltpu.sync_copy(w_hbm, w_ref)

        acc = None
        for j in range(J):
            part = lax.dot_general(_bf(dy_ref[:, j * n:(j + 1) * n]), w_ref[j], NT_DIMS, preferred_element_type=f32)
            acc = part if acc is None else acc + part
        if epi is not None:
            acc = epi(acc, *[e[...] for e in ex])
        o_ref[...] = acc.astype(out_dtype)
        if colsum:
            s_ref = refs[3 + nex]
            cs = jnp.sum(acc, axis=0, keepdims=True)

            @pl.when(i == 0)
            def _():
                s_ref[...] = cs

            @pl.when(i > 0)
            def _():
                s_ref[...] += cs

    in_specs = [pl.BlockSpec((tm, J * n), lambda i: (i, 0)), pl.BlockSpec(memory_space=pl.ANY)]
    in_specs += [pl.BlockSpec((tm, K), lambda i: (i, 0)) for _ in extras]
    out_specs = [pl.BlockSpec((tm, K), lambda i: (i, 0))]
    out_shape = [jax.ShapeDtypeStruct((M, K), out_dtype)]
    if colsum:
        out_specs.append(pl.BlockSpec((1, K), lambda i: (0, 0)))
        out_shape.append(jax.ShapeDtypeStruct((1, K), f32))
    res, landed = _pcall(body, [dy, w3, *extras], name=name, grid=(M // tm,), in_specs=in_specs, out_specs=out_specs,
                         out_shape=out_shape, scratch_shapes=[pltpu.VMEM((J, K, n), bf16)], rides=rides)
    res = res if colsum else res[0]
    return (res, landed) if rides else res


def mm_rows_fused(a, w3, name, *, transposed, tm, epi, extras=(), consts=(), outs=(f32,), nsums=0, rides=()):
    M = a.shape[0]
    J, K, n = w3.shape
    W = K if transposed else n
    tm = min(tm, M)
    nex, nco, nout = len(extras), len(consts), len(outs)

    def body(*refs):
        a_ref, w_hbm = refs[0], refs[1]
        ex = refs[2:2 + nex]
        co = refs[2 + nex:2 + nex + nco]
        o_refs = refs[2 + nex + nco:2 + nex + nco + nout]
        s_refs = refs[2 + nex + nco + nout:2 + nex + nco + nout + nsums]
        w_ref = refs[-1]
        i = pl.program_id(0)

        @pl.when(i == 0)
        def _():
            pltpu.sync_copy(w_hbm, w_ref)

        if transposed:
            acc = None
            for j in range(J):
                part = lax.dot_general(_bf(a_ref[:, j * n:(j + 1) * n]), w_ref[j], NT_DIMS,
                                       preferred_element_type=f32)
                acc = part if acc is None else acc + part
        else:
            acc = jnp.dot(_bf(a_ref[...]), w_ref[0], preferred_element_type=f32)
        rows, sums = epi(acc, *[e[...] for e in ex], *[c[...] for c in co])
        for o_ref, v, dt in zip(o_refs, rows, outs):
            o_ref[...] = v.astype(dt)
        for s_ref, v in zip(s_refs, sums):
            @pl.when(i == 0)
            def _(s_ref=s_ref, v=v):
                s_ref[...] = v

            @pl.when(i > 0)
            def _(s_ref=s_ref, v=v):
                s_ref[...] += v

    in_specs = [_rows(tm, a.shape[1]), pl.BlockSpec(memory_space=pl.ANY)]
    in_specs += [_rows(tm, W) for _ in extras] + [_full((1, W)) for _ in consts]
    out_specs = [_rows(tm, W) for _ in outs] + [_full((1, W)) for _ in range(nsums)]
    out_shape = [jax.ShapeDtypeStruct((M, W), dt) for dt in outs] + [jax.ShapeDtypeStruct((1, W), f32)] * nsums
    res, landed = _pcall(body, [a, w3, *extras, *consts], name=name, grid=(M // tm,), in_specs=in_specs,
                         out_specs=out_specs, out_shape=out_shape, scratch_shapes=[pltpu.VMEM((J, K, n), bf16)],
                         rides=rides)
    return (res, landed) if rides else res


def mm_bwd_w(a, dy, J, name, *, tm=2048, tn=None, tk=None, rides=()):
    M, K = a.shape
    n = dy.shape[1] // J
    tm = min(tm, M)
    tn = tn or n
    tk = tk or K
    nl = n // tn
    nk = K // tk
    ns = M // tm

    def body(a_ref, dy_ref, o_ref, acc_ref):
        s = pl.program_id(3)
        part = lax.dot_general(_bf(a_ref[...]), _bf(dy_ref[...]), TN_DIMS, preferred_element_type=f32)

        @pl.when(s == 0)
        def _():
            acc_ref[...] = part

        @pl.when(s > 0)
        def _():
            acc_ref[...] += part

        @pl.when(s == ns - 1)
        def _():
            o_ref[0] = acc_ref[...].astype(bf16)

    res, landed = _pcall(
        body, [a, dy], name=name, grid=(J, nl, nk, ns),
        in_specs=[pl.BlockSpec((tm, tk), lambda j, l, k, s: (s, k)),
                  pl.BlockSpec((tm, tn), lambda j, l, k, s: (s, j * nl + l))],
        out_specs=[pl.BlockSpec((1, tk, tn), lambda j, l, k, s: (j, k, l))],
        out_shape=[jax.ShapeDtypeStruct((J, K, n), bf16)],
        scratch_shapes=[pltpu.VMEM((tk, tn), f32)], rides=rides)
    return (res[0], landed) if rides else res[0]


def _relu2(v):
    r = jnp.maximum(v, 0.0)
    return r * r


HALO = 16


def _shift_down(z, k, halo):
    r = lax.broadcasted_iota(jnp.int32, z.shape, 0)
    y = pltpu.roll(z, k, 0)
    for q in range(k):
        y = jnp.where(r == q, halo[HALO - k + q:HALO - k + q + 1, :], y)
    return y


def _shift_up(z, k, halo):
    tm = z.shape[0]
    r = lax.broadcasted_iota(jnp.int32, z.shape, 0)
    y = pltpu.roll(z, tm - k, 0)
    for q in range(k):
        y = jnp.where(r == tm - k + q, halo[q:q + 1, :], y)
    return y


def _prev_halo(tm, cb):
    return pl.BlockSpec((HALO, CONV_WIDTH), lambda i: (jnp.maximum(i * (tm // HALO) - 1, 0), cb))


def _next_halo(tm, cb, nblk):
    return pl.BlockSpec((HALO, CONV_WIDTH), lambda i: (jnp.minimum((i + 1) * (tm // HALO), nblk - 1), cb))


def _f32(ref):
    return ref[...].astype(f32)


def conv_fwd(proj, conv_w, tm=TM):
    S = proj.shape[0]

    def body(cb_ref, cc_ref, ch_ref, cch_ref, chh_ref, w_ref, a_ref):
        i = pl.program_id(0)
        z = _f32(cc_ref) * _f32(ch_ref)
        zh = jnp.where(i == 0, 0.0, _f32(cch_ref) * _f32(chh_ref))
        w = w_ref[...]
        dwz = w[0:1, :] * _shift_down(z, 2, zh) + w[1:2, :] * _shift_down(z, 1, zh) + w[2:3, :] * z
        a_ref[...] = (_f32(cb_ref) * dwz).astype(bf16)

    return pl.pallas_call(
        body, name="conv_fwd", grid=(S // tm,),
        in_specs=[_rows(tm, CONV_WIDTH, 6), _rows(tm, CONV_WIDTH, 7), _rows(tm, CONV_WIDTH, 8),
                  _prev_halo(tm, 7), _prev_halo(tm, 8), _full((3, CONV_WIDTH))],
        out_specs=_rows(tm, CONV_WIDTH),
        out_shape=jax.ShapeDtypeStruct((S, CONV_WIDTH), bf16), compiler_params=_params())(
            proj, proj, proj, proj, proj, conv_w)


def conv_bwd(da, proj, conv_w, dproj, tm=TM):
    S = proj.shape[0]
    nt = S // tm

    def body(da_ref, cb_ref, cc_ref, ch_ref, cch_ref, chh_ref, dan_ref, cbn_ref, w_ref, _, o_ref, dw_ref):
        i = pl.program_id(0)
        cc, ch, cb, dav = _f32(cc_ref), _f32(ch_ref), _f32(cb_ref), da_ref[...]
        z = cc * ch
        zh = jnp.where(i == 0, 0.0, _f32(cch_ref) * _f32(chh_ref))
        w = w_ref[...]
        z1 = _shift_down(z, 1, zh)
        z2 = _shift_down(z, 2, zh)
        dwz = w[0:1, :] * z2 + w[1:2, :] * z1 + w[2:3, :] * z
        g = dav * cb
        gn = jnp.where(i == nt - 1, 0.0, dan_ref[...] * _f32(cbn_ref))
        dz = w[2:3, :] * g + w[1:2, :] * _shift_up(g, 1, gn) + w[0:1, :] * _shift_up(g, 2, gn)
        o_ref[:, 0:CONV_WIDTH] = (dav * dwz).astype(bf16)
        o_ref[:, CONV_WIDTH:2 * CONV_WIDTH] = (dz * ch).astype(bf16)
        o_ref[:, 2 * CONV_WIDTH:3 * CONV_WIDTH] = (dz * cc).astype(bf16)

        @pl.when(i == 0)
        def _():
            dw_ref[...] = jnp.zeros_like(dw_ref)

        dw_ref[0:1, :] += jnp.sum(g * z2, axis=0, keepdims=True)
        dw_ref[1:2, :] += jnp.sum(g * z1, axis=0, keepdims=True)
        dw_ref[2:3, :] += jnp.sum(g * z, axis=0, keepdims=True)

    return pl.pallas_call(
        body, name="conv_bwd", grid=(nt,),
        in_specs=[_rows(tm, CONV_WIDTH), _rows(tm, CONV_WIDTH, 6), _rows(tm, CONV_WIDTH, 7), _rows(tm, CONV_WIDTH, 8),
                  _prev_halo(tm, 7), _prev_halo(tm, 8),
                  _next_halo(tm, 0, S // HALO), _next_halo(tm, 6, S // HALO), _full((3, CONV_WIDTH)),
                  pl.BlockSpec(memory_space=pl.ANY)],
        out_specs=[_rows(tm, 3 * CONV_WIDTH, GATE_COLS // (3 * CONV_WIDTH)), _full((3, CONV_WIDTH))],
        out_shape=[jax.ShapeDtypeStruct(dproj.shape, bf16), jax.ShapeDtypeStruct((3, CONV_WIDTH), f32)],
        input_output_aliases={9: 0},
        compiler_params=_params())(da, proj, proj, proj, proj, proj, da, proj, conv_w, dproj)


def _cmul(ar, ai, br, bi):
    return ar * br - ai * bi, ar * bi + ai * br


def ssm_prep(lr, li, ldt, bt_re, bt_im):
    def body(lr_ref, li_ref, ldt_ref, br_ref, bi_ref, ar_ref, ai_ref, kr_ref, ki_ref, bbr_ref, bbi_ref):
        lrv, liv = lr_ref[...], li_ref[...]
        dt = jnp.exp(ldt_ref[...])
        mag = jnp.exp(lrv * dt)
        ar = mag * jnp.cos(liv * dt)
        ai = mag * jnp.sin(liv * dt)
        den = lrv * lrv + liv * liv
        nr = ar - 1.0
        kr = (nr * lrv + ai * liv) / den
        ki = (ai * lrv - nr * liv) / den
        ar_ref[...], ai_ref[...], kr_ref[...], ki_ref[...] = ar, ai, kr, ki
        bbr_ref[...] = kr * br_ref[...] - ki * bi_ref[...]
        bbi_ref[...] = kr * bi_ref[...] + ki * br_ref[...]

    v = jax.ShapeDtypeStruct((1, N_STATES), f32)
    m = jax.ShapeDtypeStruct((SSM_GROUP, N_STATES), f32)
    return pl.pallas_call(body, name="ssm_prep", out_shape=[v, v, v, v, m, m])(lr, li, ldt, bt_re, bt_im)


def _pow_segment(ar, ai, seg_len):
    pr, pi = ar, ai
    for _ in range(int(math.log2(seg_len))):
        pr, pi = _cmul(pr, pi, pr, pi)
    return pr, pi


GELU_K = math.sqrt(2.0 / math.pi)
GELU_C = 0.044715


def _gelu(v):
    return 0.5 * v * (1.0 + jnp.tanh(GELU_K * (v + GELU_C * v * v * v)))


def _gelu_grad(v):
    t = jnp.tanh(GELU_K * (v + GELU_C * v * v * v))
    return 0.5 * (1.0 + t) + 0.5 * v * (1.0 - t * t) * GELU_K * (1.0 + 3.0 * GELU_C * v * v)


def ssm_scan_fwd(u_perm, bd_r, bd_i, ar, ai, *, final=None, rides=()):
    S = u_perm.shape[0]
    R = SCAN_ROWS
    nblk = S // R
    seg_len = S // N_SEG
    nstrip = N_STATES // LANE_STRIP
    store = final is not None

    def body(*refs):
        if store:
            (u_ref, bdr_ref, bdi_ref, ar_ref, ai_ref, cdr_ref, cdi_ref, d_ref, ir_ref, ii_ref,
             y_ref, ys_ref, sr_ref, si_ref, bur, bui, car, cai) = refs
        else:
            u_ref, bdr_ref, bdi_ref, ar_ref, ai_ref, or_ref, oi_ref, bur, bui, car, cai = refs
        i = pl.program_id(0)
        ub = _bf(u_ref[...])
        u = ub.astype(f32)
        for cb in range(4):
            us = ub[:, cb * 128:(cb + 1) * 128]
            bur[:, cb * 512:(cb + 1) * 512] = jnp.dot(us, bdr_ref[cb], preferred_element_type=f32)
            bui[:, cb * 512:(cb + 1) * 512] = jnp.dot(us, bdi_ref[cb], preferred_element_type=f32)

        @pl.when(i == 0)
        def _():
            if store:
                car[...] = ir_ref[...]
                cai[...] = ii_ref[...]
            else:
                car[...] = jnp.zeros_like(car)
                cai[...] = jnp.zeros_like(cai)

        for ls in range(nstrip):
            lanes = pl.ds(ls * LANE_STRIP, LANE_STRIP)
            a_r = jnp.broadcast_to(ar_ref[:, lanes], (N_SEG, LANE_STRIP))
            a_i = jnp.broadcast_to(ai_ref[:, lanes], (N_SEG, LANE_STRIP))

            def step(t, carry, lanes=lanes, a_r=a_r, a_i=a_i):
                s_r, s_i = carry
                row = pl.multiple_of(t * 8, 8)
                n_r = a_r * s_r - a_i * s_i + bur[pl.ds(row, 8), lanes]
                n_i = a_r * s_i + a_i * s_r + bui[pl.ds(row, 8), lanes]
                if store:
                    sr_ref[pl.ds(row, 8), lanes] = n_r
                    si_ref[pl.ds(row, 8), lanes] = n_i
                return n_r, n_i

            e_r, e_i = lax.fori_loop(0, R // 8, step, (car[:, lanes], cai[:, lanes]), unroll=True)
            car[:, lanes] = e_r
            cai[:, lanes] = e_i

        if store:
            for cb in range(4):
                st_r = sr_ref[:, cb * 512:(cb + 1) * 512].astype(bf16)
                st_i = si_ref[:, cb * 512:(cb + 1) * 512].astype(bf16)
                y = (jnp.dot(st_r, cdr_ref[cb], preferred_element_type=f32)
                     - jnp.dot(st_i, cdi_ref[cb], preferred_element_type=f32))
                cols = slice(cb * 128, (cb + 1) * 128)
                y = y + d_ref[:, cols] * u[:, cols]
                y_ref[:, cols] = y
                ys_ref[:, cols] = _gelu(y).astype(bf16)
        else:
            @pl.when(i == nblk - 1)
            def _():
                p_r, p_i = _pow_segment(ar_ref[...], ai_ref[...], seg_len)
                t_r, t_i = car[0:1, :], cai[0:1, :]
                or_ref[0:1, :] = jnp.zeros((1, N_STATES), f32)
                oi_ref[0:1, :] = jnp.zeros((1, N_STATES), f32)
                for j in range(1, N_SEG):
                    or_ref[j:j + 1, :] = t_r
                    oi_ref[j:j + 1, :] = t_i
                    m_r, m_i = _cmul(p_r, p_i, t_r, t_i)
                    t_r, t_i = car[j:j + 1, :] + m_r, cai[j:j + 1, :] + m_i

    blk = lambda w: pl.BlockSpec((R, w), lambda i: (i, 0))
    in_specs = [blk(SSM_WIDTH), _full((4, 128, 512)), _full((4, 128, 512)), _full((1, N_STATES)), _full((1, N_STATES))]
    args = [u_perm, bd_r, bd_i, ar, ai]
    scratch = [pltpu.VMEM((R, N_STATES), f32), pltpu.VMEM((R, N_STATES), f32),
               pltpu.VMEM((N_SEG, N_STATES), f32), pltpu.VMEM((N_SEG, N_STATES), f32)]
    if store:
        in_specs += [_full((4, 512, 128)), _full((4, 512, 128)), _full((1, SSM_WIDTH)),
                     _full((N_SEG, N_STATES)), _full((N_SEG, N_STATES))]
        args += list(final)
        out_specs = [blk(SSM_WIDTH), blk(SSM_WIDTH), blk(N_STATES), blk(N_STATES)]
        out_shape = [jax.ShapeDtypeStruct((S, SSM_WIDTH), f32), jax.ShapeDtypeStruct((S, SSM_WIDTH), bf16),
                     jax.ShapeDtypeStruct((S, N_STATES), f32), jax.ShapeDtypeStruct((S, N_STATES), f32)]
        name = "ssm_scan_fwd"
    else:
        out_specs = [_full((N_SEG, N_STATES)), _full((N_SEG, N_STATES))]
        out_shape = [jax.ShapeDtypeStruct((N_SEG, N_STATES), f32)] * 2
        name = "ssm_scan_fwd_carry"
    res, landed = _pcall(body, args, name=name, grid=(nblk,), in_specs=in_specs, out_specs=out_specs,
                         out_shape=out_shape, scratch_shapes=scratch, rides=rides)
    return (res, landed) if rides else res


def ssm_scan_bwd(dys_perm, y_perm, cdt_r, cdt_i, ar, ai, *, final=None, rides=()):
    S = dys_perm.shape[0]
    R = SCAN_ROWS
    nblk = S // R
    seg_len = S // N_SEG
    nstrip = N_STATES // LANE_STRIP
    store = final is not None

    def body(*refs):
        if store:
            (dys_ref, y_ref, cdr_ref, cdi_ref, ar_ref, ai_ref, u_ref, sr_ref, si_ref, bdr_ref, bdi_ref, d_ref, ir_ref,
             ii_ref, du_ref, dbr_ref, dbi_ref, dcr_ref, dci_ref, dar_ref, dai_ref, dd_ref, dsr, dsi, lmr, lmi, car,
             cai) = refs
        else:
            dys_ref, y_ref, cdr_ref, cdi_ref, ar_ref, ai_ref, or_ref, oi_ref, dsr, dsi, car, cai = refs
        i = pl.program_id(0)
        dy = dys_ref[...] * _gelu_grad(y_ref[...])
        dyb = dy.astype(bf16)
        for cb in range(4):
            ds_ = dyb[:, cb * 128:(cb + 1) * 128]
            dsr[:, cb * 512:(cb + 1) * 512] = jnp.dot(ds_, cdr_ref[cb], preferred_element_type=f32)
            dsi[:, cb * 512:(cb + 1) * 512] = -jnp.dot(ds_, cdi_ref[cb], preferred_element_type=f32)

        @pl.when(i == 0)
        def _():
            if store:
                car[...] = ir_ref[...]
                cai[...] = ii_ref[...]
                dar_ref[...] = jnp.zeros_like(dar_ref)
                dai_ref[...] = jnp.zeros_like(dai_ref)
                dbr_ref[...] = jnp.zeros_like(dbr_ref)
                dbi_ref[...] = jnp.zeros_like(dbi_ref)
                dcr_ref[...] = jnp.zeros_like(dcr_ref)
                dci_ref[...] = jnp.zeros_like(dci_ref)
                dd_ref[...] = jnp.zeros_like(dd_ref)
            else:
                car[...] = jnp.zeros_like(car)
                cai[...] = jnp.zeros_like(cai)

        for ls in range(nstrip):
            lanes = pl.ds(ls * LANE_STRIP, LANE_STRIP)
            a_r = jnp.broadcast_to(ar_ref[:, lanes], (N_SEG, LANE_STRIP))
            a_i = jnp.broadcast_to(ai_ref[:, lanes], (N_SEG, LANE_STRIP))
            zero = jnp.zeros((N_SEG, LANE_STRIP), f32)

            def step(k, carry, lanes=lanes, a_r=a_r, a_i=a_i):
                l_r, l_i, g_r, g_i = carry
                row = pl.multiple_of((R // 8 - 1 - k) * 8, 8)
                if store:
                    s_r = sr_ref[pl.ds(row, 8), lanes]
                    s_i = si_ref[pl.ds(row, 8), lanes]
                    g_r = g_r + l_r * s_r + l_i * s_i
                    g_i = g_i + l_i * s_r - l_r * s_i
                n_r = dsr[pl.ds(row, 8), lanes] + a_r * l_r + a_i * l_i
                n_i = dsi[pl.ds(row, 8), lanes] + a_r * l_i - a_i * l_r
                if store:
                    lmr[pl.ds(row, 8), lanes] = n_r
                    lmi[pl.ds(row, 8), lanes] = n_i
                return n_r, n_i, g_r, g_i

            e_r, e_i, g_r, g_i = lax.fori_loop(0, R // 8, step, (car[:, lanes], cai[:, lanes], zero, zero),
                                               unroll=True)
            car[:, lanes] = e_r
            cai[:, lanes] = e_i
            if store:
                dar_ref[:, lanes] += g_r
                dai_ref[:, lanes] += g_i

        if store:
            ub = _bf(u_ref[...])
            u = ub.astype(f32)
            for cb in range(4):
                cols = slice(cb * 128, (cb + 1) * 128)
                st = slice(cb * 512, (cb + 1) * 512)
                l_r = lmr[:, st].astype(bf16)
                l_i = lmi[:, st].astype(bf16)
                du = (jnp.dot(l_r, bdr_ref[cb], preferred_element_type=f32)
                      + jnp.dot(l_i, bdi_ref[cb], preferred_element_type=f32))
                du_ref[:, cols] = (du + d_ref[:, cols] * dy[:, cols]).astype(bf16)
                dbr_ref[cb] += lax.dot_general(l_r, ub[:, cols], TN_DIMS, preferred_element_type=f32)
                dbi_ref[cb] += lax.dot_general(l_i, ub[:, cols], TN_DIMS, preferred_element_type=f32)
                dcr_ref[cb] += lax.dot_general(dyb[:, cols], sr_ref[:, st].astype(bf16), TN_DIMS,
                                               preferred_element_type=f32)
                dci_ref[cb] += lax.dot_general(dyb[:, cols], si_ref[:, st].astype(bf16), TN_DIMS,
                                               preferred_element_type=f32)
            dd_ref[...] += jnp.sum(dy * u, axis=0, keepdims=True)
        else:
            @pl.when(i == nblk - 1)
            def _():
                p_r, p_i = _pow_segment(ar_ref[...], ai_ref[...], seg_len)
                p_i = -p_i
                t_r, t_i = car[N_SEG - 1:N_SEG, :], cai[N_SEG - 1:N_SEG, :]
                or_ref[N_SEG - 1:N_SEG, :] = jnp.zeros((1, N_STATES), f32)
                oi_ref[N_SEG - 1:N_SEG, :] = jnp.zeros((1, N_STATES), f32)
                for j in range(N_SEG - 2, -1, -1):
                    or_ref[j:j + 1, :] = t_r
                    oi_ref[j:j + 1, :] = t_i
                    m_r, m_i = _cmul(p_r, p_i, t_r, t_i)
                    t_r, t_i = car[j:j + 1, :] + m_r, cai[j:j + 1, :] + m_i

    blk = lambda w: pl.BlockSpec((R, w), lambda i: (nblk - 1 - i, 0))
    in_specs = [blk(SSM_WIDTH), blk(SSM_WIDTH), _full((4, 128, 512)), _full((4, 128, 512)), _full((1, N_STATES)),
                _full((1, N_STATES))]
    args = [dys_perm, y_perm, cdt_r, cdt_i, ar, ai]
    seg = jax.ShapeDtypeStruct((N_SEG, N_STATES), f32)
    if store:
        in_specs += [blk(SSM_WIDTH), blk(N_STATES), blk(N_STATES), _full((4, 512, 128)), _full((4, 512, 128)),
                     _full((1, SSM_WIDTH)), _full((N_SEG, N_STATES)), _full((N_SEG, N_STATES))]
        args += list(final)
        out_specs = [blk(SSM_WIDTH), _full((4, 512, 128)), _full((4, 512, 128)), _full((4, 128, 512)),
                     _full((4, 128, 512)), _full((N_SEG, N_STATES)), _full((N_SEG, N_STATES)), _full((1, SSM_WIDTH))]
        b_acc = jax.ShapeDtypeStruct((4, 512, 128), f32)
        c_acc = jax.ShapeDtypeStruct((4, 128, 512), f32)
        out_shape = [jax.ShapeDtypeStruct((S, SSM_WIDTH), bf16), b_acc, b_acc, c_acc, c_acc, seg, seg,
                     jax.ShapeDtypeStruct((1, SSM_WIDTH), f32)]
        scratch = [pltpu.VMEM((R, N_STATES), f32)] * 4 + [pltpu.VMEM((N_SEG, N_STATES), f32)] * 2
        name = "ssm_scan_bwd"
    else:
        out_specs = [_full((N_SEG, N_STATES)), _full((N_SEG, N_STATES))]
        out_shape = [seg, seg]
        scratch = [pltpu.VMEM((R, N_STATES), f32)] * 2 + [pltpu.VMEM((N_SEG, N_STATES), f32)] * 2
        name = "ssm_scan_bwd_carry"
    res, landed = _pcall(body, args, name=name, grid=(nblk,), in_specs=in_specs, out_specs=out_specs,
                         out_shape=out_shape, scratch_shapes=scratch, rides=rides)
    return (res, landed) if rides else res


def ssm_param_bwd(dbb_r, dbb_i, bt_re, bt_im, kr, ki, ar, ai, lr, li, ldt, da_r, da_i):
    def body(dbr_ref, dbi_ref, br_ref, bi_ref, kr_ref, ki_ref, ar_ref, ai_ref, lr_ref, li_ref, ldt_ref, dar_ref,
             dai_ref, obr_ref, obi_ref, olr_ref, oli_ref, odt_ref):
        dbr, dbi, b_r, b_i = dbr_ref[...], dbi_ref[...], br_ref[...], bi_ref[...]
        k_r, k_i, a_r, a_i = kr_ref[...], ki_ref[...], ar_ref[...], ai_ref[...]
        l_r, l_i = lr_ref[...], li_ref[...]
        dt = jnp.exp(ldt_ref[...])
        obr_ref[...] = k_r * dbr + k_i * dbi
        obi_ref[...] = k_r * dbi - k_i * dbr
        gk_r = jnp.sum(dbr * b_r + dbi * b_i, axis=0, keepdims=True)
        gk_i = jnp.sum(dbi * b_r - dbr * b_i, axis=0, keepdims=True)
        ga_r = jnp.sum(dar_ref[...], axis=0, keepdims=True)
        ga_i = jnp.sum(dai_ref[...], axis=0, keepdims=True)
        den = l_r * l_r + l_i * l_i
        c_r, c_i = l_r / den, l_i / den
        m_r, m_i = _cmul(c_r, c_i, gk_r, gk_i)
        g_r, g_i = ga_r + m_r, ga_i + m_i
        t1_r, t1_i = _cmul(dt * a_r, -dt * a_i, g_r, g_i)
        q_r, q_i = _cmul(k_r, k_i, c_r, -c_i)
        t2_r, t2_i = _cmul(-q_r, q_i, gk_r, gk_i)
        olr_ref[...] = t1_r + t2_r
        oli_ref[...] = t1_i + t2_i
        w_r, w_i = _cmul(l_r, l_i, a_r, a_i)
        odt_ref[...] = dt * (w_r * g_r + w_i * g_i)

    v = jax.ShapeDtypeStruct((1, N_STATES), f32)
    m = jax.ShapeDtypeStruct((SSM_GROUP, N_STATES), f32)
    return pl.pallas_call(body, name="ssm_param_bwd", out_shape=[m, m, v, v, v])(
        dbb_r, dbb_i, bt_re, bt_im, kr, ki, ar, ai, lr, li, ldt, da_r, da_i)


def group_sum(v):
    def body(v_ref, o_ref):
        o_ref[...] = jnp.sum(v_ref[...], axis=-1, keepdims=True)
    return pl.pallas_call(body, name="group_sum", out_shape=jax.ShapeDtypeStruct((v.shape[0], 1), f32))(v)


ATTN_SCALE = XATTN_HEAD_DIM ** -0.5


def _attn_probs(q_h, k_h):
    s = lax.dot_general(q_h, k_h, NT_DIMS, preferred_element_type=f32) * ATTN_SCALE
    e = jnp.exp(s - jnp.max(s, axis=-1, keepdims=True))
    return e / jnp.sum(e, axis=-1, keepdims=True)


def attn_fwd(proj, kv, tm=TM):
    S = proj.shape[0]
    M = kv.shape[0]

    def body(q_ref, kv_ref, o_ref):
        for h in range(XATTN_HEADS):
            cols = slice(h * XATTN_HEAD_DIM, (h + 1) * XATTN_HEAD_DIM)
            q_h = q_ref[:, cols].astype(bf16)
            k_h = kv_ref[:, cols]
            v_h = kv_ref[:, XATTN_WIDTH + h * XATTN_HEAD_DIM:XATTN_WIDTH + (h + 1) * XATTN_HEAD_DIM]
            p = _attn_probs(q_h, k_h)
            o_ref[:, cols] = jnp.dot(p.astype(bf16), v_h, preferred_element_type=f32).astype(bf16)

    return pl.pallas_call(
        body, name="attn_fwd", grid=(S // tm,), in_specs=[_rows(tm, XATTN_WIDTH, 10), _full((M, 2 * XATTN_WIDTH))],
        out_specs=_rows(tm, XATTN_WIDTH), out_shape=jax.ShapeDtypeStruct((S, XATTN_WIDTH), bf16),
        compiler_params=_params())(proj, kv)


def attn_bwd(do, proj, kv, dproj, tm=TM):
    S = proj.shape[0]
    M = kv.shape[0]

    def body(do_ref, q_ref, kv_ref, _, dq_ref, dkv_ref):
        i = pl.program_id(0)

        @pl.when(i == 0)
        def _():
            dkv_ref[...] = jnp.zeros_like(dkv_ref)

        for h in range(XATTN_HEADS):
            cols = slice(h * XATTN_HEAD_DIM, (h + 1) * XATTN_HEAD_DIM)
            vcols = slice(XATTN_WIDTH + h * XATTN_HEAD_DIM, XATTN_WIDTH + (h + 1) * XATTN_HEAD_DIM)
            q_h = q_ref[:, cols].astype(bf16)
            k_h = kv_ref[:, cols]
            v_h = kv_ref[:, vcols]
            do_h = do_ref[:, cols]
            p = _attn_probs(q_h, k_h)
            dp = lax.dot_general(do_h, v_h, NT_DIMS, preferred_element_type=f32)
            ds = (p * (dp - jnp.sum(dp * p, axis=-1, keepdims=True)) * ATTN_SCALE).astype(bf16)
            dq_ref[:, cols] = jnp.dot(ds, k_h, preferred_element_type=f32).astype(bf16)
            dkv_ref[:, cols] += lax.dot_general(ds, q_h, TN_DIMS, preferred_element_type=f32)
            dkv_ref[:, vcols] += lax.dot_general(p.astype(bf16), do_h, TN_DIMS, preferred_element_type=f32)

    return pl.pallas_call(
        body, name="attn_bwd", grid=(S // tm,),
        in_specs=[_rows(tm, XATTN_WIDTH), _rows(tm, XATTN_WIDTH, 10), _full((M, 2 * XATTN_WIDTH)),
                  pl.BlockSpec(memory_space=pl.ANY)],
        out_specs=[_rows(tm, XATTN_WIDTH, 10), _full((M, 2 * XATTN_WIDTH))],
        out_shape=[jax.ShapeDtypeStruct(dproj.shape, bf16), jax.ShapeDtypeStruct((M, 2 * XATTN_WIDTH), f32)],
        input_output_aliases={3: 0},
        compiler_params=_params())(do, proj, kv, dproj)


def merge_fwd(proj, b_gate, y_a, glu, y_c, tm=TM, rides=()):
    S = proj.shape[0]

    def body(g0_ref, g1_ref, g2_ref, b_ref, ya_ref, ga_ref, gb_ref, yc_ref, o_ref):
        b = b_ref[...]
        g0 = _sig(_f32(g0_ref) + b[:, 0:D_MODEL])
        g1 = _sig(_f32(g1_ref) + b[:, D_MODEL:2 * D_MODEL])
        g2 = _sig(_f32(g2_ref) + b[:, 2 * D_MODEL:3 * D_MODEL])
        y_b = ga_ref[...].astype(f32) * _sig(gb_ref[...].astype(f32))
        o_ref[...] = (g0 * ya_ref[...].astype(f32) + g1 * y_b + g2 * yc_ref[...].astype(f32)).astype(bf16)

    res, landed = _pcall(
        body, [proj, proj, proj, b_gate, y_a, glu, glu, y_c], name="merge_fwd", grid=(S // tm,),
        in_specs=[_rows(tm, D_MODEL, 0), _rows(tm, D_MODEL, 1), _rows(tm, D_MODEL, 2), _full((1, GATE_COLS)),
                  _rows(tm, D_MODEL), _rows(tm, D_MODEL, 0), _rows(tm, D_MODEL, 1), _rows(tm, D_MODEL)],
        out_specs=[_rows(tm, D_MODEL)], out_shape=[jax.ShapeDtypeStruct((S, D_MODEL), bf16)], rides=rides)
    return (res[0], landed) if rides else res[0]


def merge_bwd(dm, proj, b_gate, y_a, glu, y_c, tm=256, rides=()):
    S = proj.shape[0]

    def body(dm_ref, g0_ref, g1_ref, g2_ref, b_ref, ya_ref, ga_ref, gb_ref, yc_ref,
             dg_ref, dya_ref, dgl_ref, dyc_ref, dbg_ref):
        i = pl.program_id(0)
        b = b_ref[...]
        dmv = dm_ref[...].astype(f32)
        ga, sb = ga_ref[...].astype(f32), _sig(gb_ref[...].astype(f32))
        ys = (ya_ref[...].astype(f32), ga * sb, yc_ref[...].astype(f32))
        gs = tuple(_sig(_f32(r) + b[:, k * D_MODEL:(k + 1) * D_MODEL]) for k, r in enumerate((g0_ref, g1_ref, g2_ref)))

        @pl.when(i == 0)
        def _():
            dbg_ref[...] = jnp.zeros_like(dbg_ref)

        for k in range(3):
            dpre = dmv * ys[k] * gs[k] * (1.0 - gs[k])
            dg_ref[:, k * D_MODEL:(k + 1) * D_MODEL] = dpre.astype(bf16)
            dbg_ref[:, k * D_MODEL:(k + 1) * D_MODEL] += jnp.sum(dpre, axis=0, keepdims=True)
        dya_ref[...] = (dmv * gs[0]).astype(bf16)
        dyc_ref[...] = (dmv * gs[2]).astype(bf16)
        dyb = dmv * gs[1]
        dgl_ref[:, 0:D_MODEL] = (dyb * sb).astype(bf16)
        dgl_ref[:, D_MODEL:2 * D_MODEL] = (dyb * ga * sb * (1.0 - sb)).astype(bf16)

    res, landed = _pcall(
        body, [dm, proj, proj, proj, b_gate, y_a, glu, glu, y_c], name="merge_bwd", grid=(S // tm,),
        in_specs=[_rows(tm, D_MODEL), _rows(tm, D_MODEL, 0), _rows(tm, D_MODEL, 1), _rows(tm, D_MODEL, 2),
                  _full((1, GATE_COLS)), _rows(tm, D_MODEL), _rows(tm, D_MODEL, 0), _rows(tm, D_MODEL, 1),
                  _rows(tm, D_MODEL)],
        out_specs=[_rows(tm, GATE_COLS), _rows(tm, D_MODEL), _rows(tm, 2 * D_MODEL), _rows(tm, D_MODEL),
                   _full((1, GATE_COLS))],
        out_shape=[jax.ShapeDtypeStruct((S, IN_COLS), bf16), jax.ShapeDtypeStruct((S, D_MODEL), bf16),
                   jax.ShapeDtypeStruct((S, 2 * D_MODEL), bf16), jax.ShapeDtypeStruct((S, D_MODEL), bf16),
                   jax.ShapeDtypeStruct((1, GATE_COLS), f32)], rides=rides)
    return (res, landed) if rides else res


def _ln_stats(r):
    mu = jnp.mean(r, axis=-1, keepdims=True)
    xc = r - mu
    var = jnp.mean(xc * xc, axis=-1, keepdims=True)
    rstd = lax.rsqrt(var + LN_EPS)
    return xc * rstd, rstd


def _ln_bwd(dy, xhat, rstd, g):
    dxh = dy * g
    return rstd * (dxh - jnp.mean(dxh, axis=-1, keepdims=True) - xhat * jnp.mean(dxh * xhat, axis=-1, keepdims=True))


def _colsum(v):
    return jnp.sum(v, axis=0, keepdims=True)


def _ln2_loss_epi(t2, h1, target, b_down, g, b):
    xhat, rstd = _ln_stats(ALPHA * h1 + t2 + b_down)
    err = xhat * g + b - target
    dout = err * (1.0 / D_MODEL)
    dr = _ln_bwd(dout, xhat, rstd, g)
    return (dr, dr), (0.5 * _colsum(err * err) * (1.0 / D_MODEL), _colsum(dout * xhat), _colsum(dout), _colsum(dr))


def _ln1_fwd_epi(t1, x, g, b):
    xhat, _ = _ln_stats(ALPHA * x + t1)
    h = xhat * g + b
    return (t1, h, h), ()


def _ln1_bwd_epi(t3, x, t1, dr2, g):
    xhat, rstd = _ln_stats(ALPHA * x + t1)
    dh = ALPHA * dr2 + t3
    dr = _ln_bwd(dh, xhat, rstd, g)
    return (dr, dr), (_colsum(dh * xhat), _colsum(dh))


def exchange(*rides, name):
    return _pcall(lambda: None, [], name=name, grid=(), in_specs=[], out_specs=[], out_shape=[], rides=rides)[1]


ROW_TILE = 256


def sum_slots(recv, name):
    n, R, C = recv.shape
    tm = min(R, ROW_TILE)

    def body(r_ref, o_ref):
        acc = r_ref[0].astype(f32)
        for k in range(1, n):
            acc = acc + r_ref[k].astype(f32)
        o_ref[...] = acc

    return pl.pallas_call(
        body, name=name, grid=(R // tm,), in_specs=[pl.BlockSpec((n, tm, C), lambda i: (0, i, 0))],
        out_specs=_rows(tm, C), out_shape=jax.ShapeDtypeStruct((R, C), f32), compiler_params=_params())(recv)


def _adamw(w, g, m, v):
    m = ADAM_B1 * m + (1.0 - ADAM_B1) * g
    v = ADAM_B2 * v + (1.0 - ADAM_B2) * (g * g)
    m_hat = m / (1.0 - ADAM_B1 ** ADAM_STEP)
    v_hat = v / (1.0 - ADAM_B2 ** ADAM_STEP)
    delta = -ADAM_LR * (m_hat / (jnp.sqrt(v_hat) + ADAM_EPS) + ADAM_WD * w)
    return delta, m, v


def adam_pair(p, q, w, m, v, name):
    R, C = w.shape
    tm = min(R, ROW_TILE)

    def body(p_ref, q_ref, w_ref, m_ref, v_ref, g_ref, d_ref, nm_ref, nv_ref):
        g = p_ref[...] + q_ref[...]
        g_ref[...] = g
        d_ref[...], nm_ref[...], nv_ref[...] = _adamw(w_ref[...], g, m_ref[...], v_ref[...])

    o = jax.ShapeDtypeStruct((R, C), f32)
    return pl.pallas_call(body, name=name, grid=(R // tm,), in_specs=[_rows(tm, C)] * 5,
                          out_specs=[_rows(tm, C)] * 4, out_shape=[o] * 4, compiler_params=_params())(p, q, w, m, v)


def adam_slots(recv, w, m, v, name):
    n = recv.shape[0]

    def body(r_ref, w_ref, m_ref, v_ref, g_ref, d_ref, nm_ref, nv_ref):
        g = r_ref[0]
        for k in range(1, n):
            g = g + r_ref[k]
        g_ref[...] = g
        d_ref[...], nm_ref[...], nv_ref[...] = _adamw(w_ref[...], g, m_ref[...], v_ref[...])

    o = jax.ShapeDtypeStruct(w.shape, f32)
    return pl.pallas_call(body, name=name, out_shape=[o] * 4, compiler_params=_params())(recv, w, m, v)


BIG = ("w_in", "w_conv_out", "w_glu", "w_xattn_out", "w_kv", "w_out", "w_up", "w_down")
MID = ("w_conv_out", "w_glu", "w_xattn_out", "w_kv", "w_out")
SMALL = ("b_gate", "ssm_lam_re", "ssm_lam_im", "ssm_log_dt", "ssm_b_re", "ssm_b_im", "ssm_c_re", "ssm_c_im", "ssm_d",
         "ln1_g", "ln1_b", "b_up", "b_down", "ln2_g", "ln2_b")
def _pad_flat(a, mult=1024):
    a = a.reshape(-1)
    return jnp.pad(a, (0, (-a.shape[0]) % mult))


def _perm(a):
    S, W = a.shape
    return a.reshape(N_SEG, S // N_SEG, W).transpose(1, 0, 2).reshape(S, W)


def _unperm(a):
    S, W = a.shape
    return a.reshape(S // N_SEG, N_SEG, W).transpose(1, 0, 2).reshape(S, W)


def _state_rows(a):
    return a.transpose(2, 0, 1).reshape(SSM_GROUP, N_STATES)


def _block_diag_b(bt):
    b4 = bt.reshape(SSM_GROUP, 4, 8, SSM_STATE)
    eye = jnp.eye(8, dtype=bt.dtype)
    return jnp.einsum("hcgp,gk->cghkp", b4, eye).reshape(4, 128, 512)


def _block_diag_c(c):
    c4 = c.reshape(4, 8, SSM_GROUP, SSM_STATE)
    eye = jnp.eye(8, dtype=c.dtype)
    return jnp.einsum("cghp,gk->cgpkh", c4, eye).reshape(4, 512, 128)


def _diag_b(acc):
    a = acc.reshape(4, 8, SSM_STATE, 8, SSM_GROUP)
    eye = jnp.eye(8, dtype=acc.dtype)
    return jnp.einsum("cgpkh,gk->hcgp", a, eye).reshape(SSM_GROUP, N_STATES)


def _diag_c(acc):
    a = acc.reshape(4, 8, SSM_GROUP, 8, SSM_STATE)
    eye = jnp.eye(8, dtype=acc.dtype)
    return jnp.einsum("cghkp,gk->cghp", a, eye).reshape(SSM_GROUPS, SSM_GROUP, SSM_STATE)


def kernel(x, mem, w_in, b_gate, conv_w, w_conv_out, ssm_lam_re, ssm_lam_im, ssm_log_dt, ssm_b_re, ssm_b_im, ssm_c_re, ssm_c_im, ssm_d, w_glu, w_kv, w_xattn_out, w_out, ln1_g, ln1_b, w_up, b_up, w_down, b_down, ln2_g, ln2_b, loss_target, m_w_in, m_b_gate, m_conv_w, m_w_conv_out, m_ssm_lam_re, m_ssm_lam_im, m_ssm_log_dt, m_ssm_b_re, m_ssm_b_im, m_ssm_c_re, m_ssm_c_im, m_ssm_d, m_w_glu, m_w_kv, m_w_xattn_out, m_w_out, m_ln1_g, m_ln1_b, m_w_up, m_b_up, m_w_down, m_b_down, m_ln2_g, m_ln2_b, v_w_in, v_b_gate, v_conv_w, v_w_conv_out, v_ssm_lam_re, v_ssm_lam_im, v_ssm_log_dt, v_ssm_b_re, v_ssm_b_im, v_ssm_c_re, v_ssm_c_im, v_ssm_d, v_w_glu, v_w_kv, v_w_xattn_out, v_w_out, v_ln1_g, v_ln1_b, v_w_up, v_b_up, v_w_down, v_b_down, v_ln2_g, v_ln2_b):
    W = dict(w_in=w_in, b_gate=b_gate, conv_w=conv_w, w_conv_out=w_conv_out, ssm_lam_re=ssm_lam_re,
             ssm_lam_im=ssm_lam_im, ssm_log_dt=ssm_log_dt, ssm_b_re=ssm_b_re, ssm_b_im=ssm_b_im, ssm_c_re=ssm_c_re,
             ssm_c_im=ssm_c_im, ssm_d=ssm_d, w_glu=w_glu, w_kv=w_kv, w_xattn_out=w_xattn_out, w_out=w_out,
             ln1_g=ln1_g, ln1_b=ln1_b, w_up=w_up, b_up=b_up, w_down=w_down, b_down=b_down, ln2_g=ln2_g, ln2_b=ln2_b)
    MOM = dict(w_in=m_w_in, b_gate=m_b_gate, conv_w=m_conv_w, w_conv_out=m_w_conv_out, ssm_lam_re=m_ssm_lam_re,
               ssm_lam_im=m_ssm_lam_im, ssm_log_dt=m_ssm_log_dt, ssm_b_re=m_ssm_b_re, ssm_b_im=m_ssm_b_im,
               ssm_c_re=m_ssm_c_re, ssm_c_im=m_ssm_c_im, ssm_d=m_ssm_d, w_glu=m_w_glu, w_kv=m_w_kv,
               w_xattn_out=m_w_xattn_out, w_out=m_w_out, ln1_g=m_ln1_g, ln1_b=m_ln1_b, w_up=m_w_up, b_up=m_b_up,
               w_down=m_w_down, b_down=m_b_down, ln2_g=m_ln2_g, ln2_b=m_ln2_b)
    VEL = dict(w_in=v_w_in, b_gate=v_b_gate, conv_w=v_conv_w, w_conv_out=v_w_conv_out, ssm_lam_re=v_ssm_lam_re,
               ssm_lam_im=v_ssm_lam_im, ssm_log_dt=v_ssm_log_dt, ssm_b_re=v_ssm_b_re, ssm_b_im=v_ssm_b_im,
               ssm_c_re=v_ssm_c_re, ssm_c_im=v_ssm_c_im, ssm_d=v_ssm_d, w_glu=v_w_glu, w_kv=v_w_kv,
               w_xattn_out=v_w_xattn_out, w_out=v_w_out, ln1_g=v_ln1_g, ln1_b=v_ln1_b, w_up=v_w_up, b_up=v_b_up,
               w_down=v_w_down, b_down=v_b_down, ln2_g=v_ln2_g, ln2_b=v_ln2_b)
    names = list(W)
    xy = 2 * lax.axis_index("x") + lax.axis_index("y")

    xs = x[0]
    S = xs.shape[0]
    mems = mem[0]
    tgt = loss_target[0]

    shard_bf = {n: W[n][0].astype(bf16) for n in BIG}

    lr = ssm_lam_re.reshape(1, N_STATES)
    li = ssm_lam_im.reshape(1, N_STATES)
    ldt = jnp.repeat(ssm_log_dt.reshape(SSM_GROUPS), SSM_STATE).reshape(1, N_STATES)
    bt_re, bt_im = _state_rows(ssm_b_re[0]), _state_rows(ssm_b_im[0])
    ar, ai, kr, ki, bbt_r, bbt_i = ssm_prep(lr, li, ldt, bt_re, bt_im)
    bd_r, bd_i = _block_diag_b(bbt_r).astype(bf16), _block_diag_b(bbt_i).astype(bf16)
    cd_r, cd_i = _block_diag_c(ssm_c_re[0]).astype(bf16), _block_diag_c(ssm_c_im[0]).astype(bf16)
    bdt_r, bdt_i = bd_r.transpose(0, 2, 1), bd_i.transpose(0, 2, 1)
    cdt_r, cdt_i = cd_r.transpose(0, 2, 1), cd_i.transpose(0, 2, 1)
    d_skip = ssm_d.reshape(1, SSM_WIDTH)

    mem_bf = mems.astype(bf16)
    u_col = GATE_COLS + 3 * CONV_WIDTH
    me_xy = jnp.reshape(xy, (1,)).astype(jnp.int32)
    proj, x_bf, (win3,) = proj_own(xs, shard_bf["w_in"], me_xy, 4, rides=(("gather2", shard_bf["w_in"]),))
    proj, (wco3, wglu3, wxo3, wkv4, wout4, convw4) = proj_rest(
        x_bf, win3, me_xy, proj, rides=tuple(("gather", shard_bf[n]) for n in MID) + (("gather", conv_w[0]),))
    wkv3 = wkv4.reshape(1, D_MODEL, 2 * XATTN_WIDTH)
    wout3 = wout4.reshape(1, D_MODEL, D_MODEL)
    convw_full = convw4.transpose(1, 0, 2).reshape(3, CONV_WIDTH)
    a_conv = conv_fwd(proj, convw_full)
    y_a = mm_fwd_small(a_conv, wco3, "conv_out", out_dtype=bf16)
    u_perm = _perm(proj[:, u_col:u_col + SSM_WIDTH])
    half = D_MODEL // 2
    (init_r, init_i), (wup_a,) = ssm_scan_fwd(u_perm, bd_r, bd_i, ar, ai,
                                              rides=(("gather", shard_bf["w_up"][:half]),))
    (ysm_perm, ys_perm, st_r, st_i), (wup_b,) = ssm_scan_fwd(
        u_perm, bd_r, bd_i, ar, ai, final=(cd_r, cd_i, d_skip, init_r, init_i),
        rides=(("gather", shard_bf["w_up"][half:]),))
    wup3 = jnp.concatenate([wup_a, wup_b], axis=1)
    y_s = _unperm(ys_perm)
    glu = mm_fwd_small(y_s, wglu3, "glu", out_dtype=bf16)
    kv = mm_fwd_small(mem_bf, wkv3, "kv", out_dtype=bf16)
    o_att = attn_fwd(proj, kv)
    y_c = mm_fwd_small(o_att, wxo3, "xattn_out", out_dtype=bf16)
    merged = merge_fwd(proj, b_gate, y_a, glu, y_c)
    t1, h1, h1_bf = mm_rows_fused(merged, wout3, "w_out_ln1", transposed=False, tm=512, epi=_ln1_fwd_epi,
                                  extras=(xs,), consts=(ln1_g, ln1_b), outs=(f32, f32, bf16))
    (r_up, hdn), (wdn4,) = mm_fwd(h1_bf, wup3, "w_up", bias=b_up, outs=(bf16, bf16), second=_relu2,
                                  first=lambda v: jnp.maximum(v, 0.0), rides=(("gather", shard_bf["w_down"]),))
    wdn3 = wdn4.reshape(1, D_FF, D_MODEL)

    dr2, dr2_bf, loss_cols, d_ln2_g, d_ln2_b, d_b_down = mm_rows_fused(
        hdn, wdn3, "w_down_ln2_loss", transposed=False, tm=512, epi=_ln2_loss_epi, extras=(h1, tgt),
        consts=(b_down, ln2_g, ln2_b), outs=(f32, bf16), nsums=4)
    part, other = {}, {}
    g_w_down = mm_bwd_w(hdn, dr2_bf, 1, "dw_down", tk=1024).reshape(4, -1, D_MODEL)
    (dup, d_b_up), (recv_dn,) = mm_bwd_x(dr2_bf, wdn3, "dup", tm=512, extras=(r_up,), colsum=True, out_dtype=bf16,
                                         epi=lambda acc, r: acc * (2.0 * r.astype(f32)),
                                         rides=(("scatter", g_w_down),))
    part["w_down"] = sum_slots(recv_dn, "sum_w_down")
    g_w_up = mm_bwd_w(h1_bf, dup, 4, "dw_up")
    dr1, dr1_bf, d_ln1_g, d_ln1_b = mm_rows_fused(
        dup, wup3, "dh1_ln1_bwd", transposed=True, tm=256, epi=_ln1_bwd_epi, extras=(xs, t1, dr2), consts=(ln1_g,),
        outs=(f32, bf16), nsums=2)
    g_w_out = mm_bwd_w(merged, dr1_bf, 1, "dw_out").reshape(4, -1, D_MODEL)
    dmerged, (recv_out,) = mm_bwd_x(dr1_bf, wout3, "dmerged", tm=1024, out_dtype=bf16,
                                    rides=(("scatter", g_w_out),))
    part["w_out"] = sum_slots(recv_out, "sum_w_out")
    (dproj, dy_a, dglu, dy_c, d_b_gate), (recv_up,) = merge_bwd(dmerged, proj, b_gate, y_a, glu, y_c,
                                                                rides=(("scatter", g_w_up),))
    part["w_up"] = sum_slots(recv_up, "sum_w_up")

    g_w_co = mm_bwd_w_small(a_conv, dy_a, 4, "dw_conv_out")
    da_conv = mm_bwd_x(dy_a, wco3, "da_conv", tm=1024)
    dproj, d_conv_w = conv_bwd(da_conv, proj, convw_full, dproj)

    g_w_glu = mm_bwd_w_small(y_s, dglu, 4, "dw_glu")
    dys_perm = _perm(mm_bwd_x(dglu, wglu3, "dy_s", tm=1024))
    linit_r, linit_i = ssm_scan_bwd(dys_perm, ysm_perm, cdt_r, cdt_i, ar, ai)
    (du_perm, dbacc_r, dbacc_i, dcacc_r, dcacc_i, da_r, da_i, d_ssm_d), (recv_co, recv_glu) = ssm_scan_bwd(
        dys_perm, ysm_perm, cdt_r, cdt_i, ar, ai,
        final=(u_perm, st_r, st_i, bdt_r, bdt_i, d_skip, linit_r, linit_i),
        rides=(("scatter", g_w_co), ("scatter", g_w_glu)))
    part["w_conv_out"] = sum_slots(recv_co, "sum_w_conv_out")
    part["w_glu"] = sum_slots(recv_glu, "sum_w_glu")
    dbt_re, dbt_im, d_lr, d_li, d_ldt_state = ssm_param_bwd(
        _diag_b(dbacc_r), _diag_b(dbacc_i), bt_re, bt_im, kr, ki, ar, ai, lr, li, ldt, da_r, da_i)
    d_log_dt = group_sum(d_ldt_state.reshape(SSM_GROUPS, SSM_STATE))
    d_b_re = dbt_re.reshape(SSM_GROUP, SSM_GROUPS, SSM_STATE).transpose(1, 2, 0)
    d_b_im = dbt_im.reshape(SSM_GROUP, SSM_GROUPS, SSM_STATE).transpose(1, 2, 0)
    d_c_re = _diag_c(dcacc_r)
    d_c_im = -_diag_c(dcacc_i)

    g_w_xo = mm_bwd_w_small(o_att, dy_c, 4, "dw_xattn_out")
    do_att = mm_bwd_x(dy_c, wxo3, "do_att", tm=1024, out_dtype=bf16)
    dproj, dkv = attn_bwd(do_att, proj, kv, dproj)
    g_w_kv = mm_bwd_w(mem_bf, dkv, 1, "dw_kv").reshape(4, -1, D_MODEL)

    dproj = lax.dynamic_update_slice(dproj, _unperm(du_perm), (0, u_col))
    g_small = {"b_gate": d_b_gate, "ssm_lam_re": d_lr, "ssm_lam_im": d_li, "ssm_log_dt": d_log_dt, "ssm_b_re": d_b_re,
               "ssm_b_im": d_b_im, "ssm_c_re": d_c_re, "ssm_c_im": d_c_im, "ssm_d": d_ssm_d, "ln1_g": d_ln1_g,
               "ln1_b": d_ln1_b, "b_up": d_b_up, "b_down": d_b_down, "ln2_g": d_ln2_g, "ln2_b": d_ln2_b}
    small_names = SMALL + ("conv_w",)
    g_small["conv_w"] = d_conv_w
    sizes = {n: (g_small[n].size + 1023) // 1024 * 1024 for n in small_names}
    pack = lambda d: jnp.concatenate([_pad_flat(d[n]) for n in small_names]).reshape(-1, 128)
    early = ("w_down", "w_up", "w_out", "w_conv_out", "w_glu")
    g_w_in, landed = mm_bwd_w(
        x_bf, dproj, 4, "dw_in", rides=(("scatter", g_w_xo), ("scatter", g_w_kv), ("all", pack(g_small)))
        + tuple(("pair", part[n]) for n in early))
    recv_xo, recv_kv, srecv = landed[:3]
    other.update(zip(early, landed[3:]))
    part["w_xattn_out"] = sum_slots(recv_xo, "sum_w_xattn_out")
    part["w_kv"] = sum_slots(recv_kv, "sum_w_kv")
    dx, (recv_in, other["w_xattn_out"], other["w_kv"]) = mm_bwd_x(
        dproj, win3, "dx", extras=(dr1,), epi=lambda acc, d: acc + ALPHA * d,
        rides=(("scatter", g_w_in), ("pair", part["w_xattn_out"]), ("pair", part["w_kv"])))
    part["w_in"] = sum_slots(recv_in, "sum_w_in")
    other["w_in"], = exchange(("pair", part["w_in"]), name="swap_w_in")
    res = [{}, {}, {}, {}]
    for n in BIG:
        for k, r in enumerate(adam_pair(part[n], other[n], W[n][0], MOM[n][0], VEL[n][0], "adam_" + n)):
            res[k][n] = r[None]
    conv_zero = jnp.zeros((3, CONV_WIDTH), f32)
    gs, ds_, ms, vs = adam_slots(srecv, pack({**{n: W[n] for n in SMALL}, "conv_w": conv_zero}),
                                 pack({**{n: MOM[n] for n in SMALL}, "conv_w": conv_zero}),
                                 pack({**{n: VEL[n] for n in SMALL}, "conv_w": conv_zero}), "adam_small")

    def unpack_small(buf):
        flat = buf.reshape(-1)
        out, r = {}, 0
        for n in small_names:
            ref = g_small[n] if n == "conv_w" else W[n]
            out[n] = flat[r:r + ref.size].reshape(ref.shape)
            r += sizes[n]
        return out

    res_s = [unpack_small(b) for b in (gs, ds_, ms, vs)]
    g_conv = lax.dynamic_slice(res_s[0]["conv_w"], (0, xy * 128), (3, 128))
    conv_slots = g_conv.reshape(1, 3, 128)
    cg, cd, cm, cv = adam_slots(conv_slots, conv_w[0], m_conv_w[0], v_conv_w[0], "adam_conv")
    conv_res = [cg, cd, cm, cv]

    loss = lax.psum(jnp.sum(loss_cols), ("x", "y", "c"))
    outs = [loss, dx.reshape(x.shape)]
    for k in range(4):
        for n in names:
            if n == "conv_w":
                outs.append(conv_res[k].reshape(conv_w.shape))
            elif n in BIG:
                outs.append(res[k][n])
            else:
                outs.append(res_s[k][n])
    return tuple(outs)
```

```python
import functools
import math

import jax
import jax.numpy as jnp
from jax import lax
from jax.experimental import pallas as pl
from jax.experimental.pallas import tpu as pltpu

f32 = jnp.float32
bf16 = jnp.bfloat16

D_MODEL = 1024
CONV_WIDTH = 512
SSM_WIDTH = 512
SSM_GROUP = 16
SSM_GROUPS = 32
SSM_STATE = 64
N_STATES = SSM_GROUPS * SSM_STATE
XATTN_HEADS = 4
XATTN_HEAD_DIM = 128
XATTN_WIDTH = 512
D_FF = 4096
GATE_COLS = 3 * D_MODEL
IN_COLS = GATE_COLS + 3 * CONV_WIDTH + SSM_WIDTH + XATTN_WIDTH
ALPHA = 2.0 ** 0.25
LN_EPS = 1e-5
ADAM_LR = 0.001
ADAM_B1 = 0.9
ADAM_B2 = 0.999
ADAM_EPS = 1e-08
ADAM_WD = 0.01
ADAM_STEP = 10

N_SEG = 8
SCAN_ROWS = 512
LANE_STRIP = 512
TM = 512
VMEM_LIMIT = 48 * 1024 * 1024
MESH = pl.DeviceIdType.MESH

NT_DIMS = (((1,), (1,)), ((), ()))
TN_DIMS = (((0,), (0,)), ((), ()))


def _params():
    return pltpu.CompilerParams(vmem_limit_bytes=VMEM_LIMIT)


def _full(shape):
    n = len(shape)
    return pl.BlockSpec(shape, lambda *_: (0,) * n)


def _rows(tm, w, cb=0):
    return pl.BlockSpec((tm, w), lambda i: (i, cb))


def _sig(x):
    return 1.0 / (1.0 + jnp.exp(-x))


HBM_SPEC = pl.BlockSpec(memory_space=pl.ANY)
RIDE_PEERS = {"gather": 3, "gather2": 6, "scatter": 3, "all": 7, "pair": 1}


def _xy_peers(x, y):
    return [(1 - x, y), (x, 1 - y), (1 - x, 1 - y)]


def _ride_copies(kind, src, dst, send_sems, recv_sems, local_sem):
    x, y, c = lax.axis_index("x"), lax.axis_index("y"), lax.axis_index("c")
    me = 2 * x + y
    if kind == "all":
        flips = [(fx, fy, fc) for fx in (0, 1) for fy in (0, 1) for fc in (0, 1)][1:]
        peers = [(x ^ fx, y ^ fy, c ^ fc) for fx, fy, fc in flips]
        slot = lambda p: 4 * p[0] + 2 * p[1] + p[2]
        mine = slot((x, y, c))
    else:
        peers = [(px, py, c) for px, py in _xy_peers(x, y)]
        slot = lambda p: 2 * p[0] + p[1]
        mine = me

    def remote(k, s, d):
        return pltpu.make_async_remote_copy(src_ref=s, dst_ref=d, send_sem=send_sems.at[k], recv_sem=recv_sems.at[k],
                                            device_id=peers[k], device_id_type=MESH)

    if kind == "pair":
        peers = [(x, y, 1 - c)]
        return None, [remote(0, src, dst)], [remote(0, src, dst)]
    if kind == "scatter":
        local = pltpu.make_async_copy(src.at[me], dst.at[me], local_sem)
        sends = [remote(k, src.at[slot(p)], dst.at[me]) for k, p in enumerate(peers)]
        lands = [remote(k, src.at[me], dst.at[slot(p)]) for k, p in enumerate(peers)]
    else:
        local = pltpu.make_async_copy(src, dst.at[mine], local_sem)
        sends = [remote(k, src, dst.at[mine]) for k, p in enumerate(peers)]
        lands = [remote(k, src, dst.at[slot(p)]) for k, p in enumerate(peers)]
    return local, sends, lands


def _ride_shape(kind, src):
    lead = {"gather": (4,), "gather2": (4,), "scatter": (), "all": (8,), "pair": ()}[kind]
    return jax.ShapeDtypeStruct(lead + src.shape, src.dtype)


def _two_level(src, dst, send_sems, recv_sems, local_sem):
    x, y, c = lax.axis_index("x"), lax.axis_index("y"), lax.axis_index("c")
    me = 2 * x + y
    h = src.shape[0] // 2
    mine = pl.ds(pl.multiple_of(c * h, 16), h)
    theirs = pl.ds(pl.multiple_of((1 - c) * h, 16), h)
    peers = _xy_peers(x, y)
    slots = [2 * px + py for px, py in peers]

    def over_ici(k, d):
        return pltpu.make_async_remote_copy(src_ref=src.at[mine], dst_ref=d, send_sem=send_sems.at[k],
                                            recv_sem=recv_sems.at[k], device_id=(*peers[k], c), device_id_type=MESH)

    def over_d2d(k, rows):
        blk = dst.at[slots[k], rows]
        return pltpu.make_async_remote_copy(src_ref=blk, dst_ref=blk, send_sem=send_sems.at[3 + k],
                                            recv_sem=recv_sems.at[3 + k], device_id=(x, y, 1 - c), device_id_type=MESH)

    return (pltpu.make_async_copy(src, dst.at[me], local_sem),
            [over_ici(k, dst.at[me, mine]) for k in range(3)], [over_ici(k, dst.at[slots[k], mine]) for k in range(3)],
            [over_d2d(k, mine) for k in range(3)], [over_d2d(k, theirs) for k in range(3)])


def _pcall(body, args, *, name, grid, in_specs, out_specs, out_shape, scratch_shapes=(), rides=(), aliases=None,
           prefetch=()):
    n_in, n_out, n_scr, nr, npf = len(in_specs), len(out_specs), len(scratch_shapes), len(rides), len(prefetch)
    kinds = [k for k, _ in rides]

    def wrapped(*refs):
        pre, refs = refs[:npf], refs[npf:]
        ins, rsrc = refs[:n_in], refs[n_in:n_in + nr]
        outs = refs[n_in + nr:n_in + nr + n_out]
        rdst = refs[n_in + nr + n_out:n_in + 2 * nr + n_out]
        scr = refs[n_in + 2 * nr + n_out:n_in + 2 * nr + n_out + n_scr]
        sems = refs[n_in + 2 * nr + n_out + n_scr:]

        def start():
            for r, kind in enumerate(kinds):
                if kind == "gather2":
                    local, sends, _, _, _ = _two_level(rsrc[r], rdst[r], *sems[3 * r:3 * r + 3])
                else:
                    local, sends, _ = _ride_copies(kind, rsrc[r], rdst[r], *sems[3 * r:3 * r + 3])
                if local is not None:
                    local.start()
                for cp in sends:
                    cp.start()

        def finish():
            for r, kind in enumerate(kinds):
                if kind == "gather2":
                    local, sends, ici_lands, forwards, lands = _two_level(rsrc[r], rdst[r], *sems[3 * r:3 * r + 3])
                    for k in range(3):
                        ici_lands[k].wait_recv()
                        forwards[k].start()
                    sends = sends + forwards
                else:
                    local, sends, lands = _ride_copies(kind, rsrc[r], rdst[r], *sems[3 * r:3 * r + 3])
                for cp in lands:
                    cp.wait_recv()
                for cp in sends:
                    cp.wait_send()
                if local is not None:
                    local.wait()

        if nr and grid:
            ids = [pl.program_id(a) for a in range(len(grid))]
            first = functools.reduce(jnp.logical_and, [i == 0 for i in ids])
            last = functools.reduce(jnp.logical_and, [i == g - 1 for i, g in zip(ids, grid)])
            pl.when(first)(start)
        elif nr:
            start()
        body(*pre, *ins, *outs, *scr)
        if nr and grid:
            pl.when(last)(finish)
        elif nr:
            finish()

    sems = []
    for kind in kinds:
        n = RIDE_PEERS[kind]
        sems += [pltpu.SemaphoreType.DMA((n,)), pltpu.SemaphoreType.DMA((n,)), pltpu.SemaphoreType.DMA(())]
    all_in = list(in_specs) + [HBM_SPEC] * nr
    all_out = list(out_specs) + [HBM_SPEC] * nr
    all_scratch = list(scratch_shapes) + sems
    if npf:
        how = dict(grid_spec=pltpu.PrefetchScalarGridSpec(num_scalar_prefetch=npf, grid=grid, in_specs=all_in,
                                                          out_specs=all_out, scratch_shapes=all_scratch))
    else:
        how = dict(grid=grid, in_specs=all_in, out_specs=all_out, scratch_shapes=all_scratch)
    res = pl.pallas_call(
        wrapped, name=name, out_shape=list(out_shape) + [_ride_shape(k, s) for k, s in rides],
        input_output_aliases=aliases or {}, compiler_params=_params(), **how)(
            *prefetch, *args, *[s for _, s in rides])
    return list(res[:n_out]), list(res[n_out:])


def _bf(v):
    return v if v.dtype == bf16 else v.astype(bf16)


def mm_fwd_small(a, w3, name, *, tm=1024, out_dtype=f32):
    M, K = a.shape
    J, _, n = w3.shape
    tm = min(tm, M)

    def body(a_ref, w_ref, o_ref):
        av = _bf(a_ref[...])
        for j in range(J):
            o_ref[:, j * n:(j + 1) * n] = jnp.dot(av, w_ref[j], preferred_element_type=f32).astype(out_dtype)

    return pl.pallas_call(
        body, name=name, grid=(M // tm,), in_specs=[_rows(tm, K), _full((J, K, n))], out_specs=_rows(tm, J * n),
        out_shape=jax.ShapeDtypeStruct((M, J * n), out_dtype), compiler_params=_params())(a, w3)


def mm_bwd_w_small(a, dy, J, name, *, tm=2048):
    M, K = a.shape
    n = dy.shape[1] // J
    tm = min(tm, M)
    ns = M // tm

    def body(a_ref, dy_ref, o_ref, acc_ref):
        s = pl.program_id(0)
        part = lax.dot_general(_bf(a_ref[...]), _bf(dy_ref[...]), TN_DIMS, preferred_element_type=f32)

        @pl.when(s == 0)
        def _():
            acc_ref[...] = part

        @pl.when(s > 0)
        def _():
            acc_ref[...] += part

        @pl.when(s == ns - 1)
        def _():
            for j in range(J):
                o_ref[j] = acc_ref[:, j * n:(j + 1) * n].astype(bf16)

    return pl.pallas_call(
        body, name=name, grid=(ns,), in_specs=[_rows(tm, K), _rows(tm, J * n)], out_specs=_full((J, K, n)),
        out_shape=jax.ShapeDtypeStruct((J, K, n), bf16), scratch_shapes=[pltpu.VMEM((K, J * n), f32)],
        compiler_params=_params())(a, dy)


def mm_fwd(a, w3, name, *, tm=1024, tn=None, bias=None, outs=(f32,), first=None, second=None, emit_a=False,
           rides=()):
    M, K = a.shape
    J, _, n = w3.shape
    tm = min(tm, M)
    tn = tn or n
    nl = n // tn
    nb = 0 if bias is None else 1

    def body(*refs):
        a_ref, w_ref = refs[0], refs[1]
        av = _bf(a_ref[...])
        acc = jnp.dot(av, w_ref[0], preferred_element_type=f32)
        if bias is not None:
            acc = acc + refs[2][...]
        refs[2 + nb][...] = (acc if first is None else first(acc)).astype(outs[0])
        if len(outs) > 1:
            refs[3 + nb][...] = second(acc).astype(outs[1])
        if emit_a:
            @pl.when(jnp.logical_and(pl.program_id(0) == 0, pl.program_id(1) == 0))
            def _():
                refs[2 + nb + len(outs)][...] = av

    in_specs = [pl.BlockSpec((tm, K), lambda j, l, i: (i, 0)),
                pl.BlockSpec((1, K, tn), lambda j, l, i: (j, 0, l))]
    args = [a, w3]
    if bias is not None:
        in_specs.append(pl.BlockSpec((1, tn), lambda j, l, i: (0, j * nl + l)))
        args.append(bias)
    out_specs = [pl.BlockSpec((tm, tn), lambda j, l, i: (i, j * nl + l)) for _ in outs]
    out_shape = [jax.ShapeDtypeStruct((M, J * n), dt) for dt in outs]
    if emit_a:
        last = M // tm - 1
        out_specs.append(pl.BlockSpec((tm, K), lambda j, l, i: (jnp.where((j == 0) & (l == 0), i, last), 0)))
        out_shape.append(jax.ShapeDtypeStruct((M, K), bf16))
    res, landed = _pcall(body, args, name=name, grid=(J, nl, M // tm), in_specs=in_specs, out_specs=out_specs,
                         out_shape=out_shape, rides=rides)
    res = res if len(res) > 1 else res[0]
    return (res, landed) if rides else res


def proj_own(x, w_own, me, n_blocks, *, tm=1024, rides=()):
    M, K = x.shape
    n = w_own.shape[1]

    def body(me_ref, x_ref, w_ref, o_ref, xb_ref):
        av = _bf(x_ref[...])
        xb_ref[...] = av
        o_ref[...] = jnp.dot(av, w_ref[...], preferred_element_type=f32).astype(bf16)

    (proj, x_bf), landed = _pcall(
        body, [x, w_own], name="proj_own", grid=(M // tm,), prefetch=(me,),
        in_specs=[pl.BlockSpec((tm, K), lambda i, me_ref: (i, 0)), pl.BlockSpec((K, n), lambda i, me_ref: (0, 0))],
        out_specs=[pl.BlockSpec((tm, n), lambda i, me_ref: (i, me_ref[0])),
                   pl.BlockSpec((tm, K), lambda i, me_ref: (i, 0))],
        out_shape=[jax.ShapeDtypeStruct((M, n_blocks * n), bf16), jax.ShapeDtypeStruct((M, K), bf16)], rides=rides)
    return proj, x_bf, landed


def proj_rest(x_bf, w3, me, proj, *, tm=1024, rides=()):
    M, K = x_bf.shape
    J, _, n = w3.shape

    def body(me_ref, x_ref, w_ref, _, o_ref):
        o_ref[...] = jnp.dot(x_ref[...], w_ref[0], preferred_element_type=f32).astype(bf16)

    (proj,), landed = _pcall(
        body, [x_bf, w3, proj], name="proj_rest", grid=(J - 1, M // tm), prefetch=(me,),
        in_specs=[pl.BlockSpec((tm, K), lambda k, i, me_ref: (i, 0)),
                  pl.BlockSpec((1, K, n), lambda k, i, me_ref: (me_ref[0] ^ (k + 1), 0, 0)), HBM_SPEC],
        out_specs=[pl.BlockSpec((tm, n), lambda k, i, me_ref: (i, me_ref[0] ^ (k + 1)))],
        out_shape=[jax.ShapeDtypeStruct(proj.shape, bf16)], aliases={3: 0}, rides=rides)
    return proj, landed


def mm_bwd_x(dy, w3, name, *, tm=TM, epi=None, extras=(), colsum=False, out_dtype=f32, rides=()):
    M = dy.shape[0]
    J, K, n = w3.shape
    tm = min(tm, M)
    nex = len(extras)

    def body(*refs):
        dy_ref, w_hbm = refs[0], refs[1]
        ex = refs[2:2 + nex]
        o_ref = refs[2 + nex]
        w_ref = refs[-1]
        i = pl.program_id(0)

        @pl.when(i == 0)
        def _():
            pltpu.sync_copy(w_hbm, w_ref)

        acc = None
        for j in range(J):
            part = lax.dot_general(_bf(dy_ref[:, j * n:(j + 1) * n]), w_ref[j], NT_DIMS, preferred_element_type=f32)
            acc = part if acc is None else acc + part
        if epi is not None:
            acc = epi(acc, *[e[...] for e in ex])
        o_ref[...] = acc.astype(out_dtype)
        if colsum:
            s_ref = refs[3 + nex]
            cs = jnp.sum(acc, axis=0, keepdims=True)

            @pl.when(i == 0)
            def _():
                s_ref[...] = cs

            @pl.when(i > 0)
            def _():
                s_ref[...] += cs

    in_specs = [pl.BlockSpec((tm, J * n), lambda i: (i, 0)), pl.BlockSpec(memory_space=pl.ANY)]
    in_specs += [pl.BlockSpec((tm, K), lambda i: (i, 0)) for _ in extras]
    out_specs = [pl.BlockSpec((tm, K), lambda i: (i, 0))]
    out_shape = [jax.ShapeDtypeStruct((M, K), out_dtype)]
    if colsum:
        out_specs.append(pl.BlockSpec((1, K), lambda i: (0, 0)))
        out_shape.append(jax.ShapeDtypeStruct((1, K), f32))
    res, landed = _pcall(body, [dy, w3, *extras], name=name, grid=(M // tm,), in_specs=in_specs, out_specs=out_specs,
                         out_shape=out_shape, scratch_shapes=[pltpu.VMEM((J, K, n), bf16)], rides=rides)
    res = res if colsum else res[0]
    return (res, landed) if rides else res


def mm_rows_fused(a, w3, name, *, transposed, tm, epi, extras=(), consts=(), outs=(f32,), nsums=0, rides=()):
    M = a.shape[0]
    J, K, n = w3.shape
    W = K if transposed else n
    tm = min(tm, M)
    nex, nco, nout = len(extras), len(consts), len(outs)

    def body(*refs):
        a_ref, w_hbm = refs[0], refs[1]
        ex = refs[2:2 + nex]
        co = refs[2 + nex:2 + nex + nco]
        o_refs = refs[2 + nex + nco:2 + nex + nco + nout]
        s_refs = refs[2 + nex + nco + nout:2 + nex + nco + nout + nsums]
        w_ref = refs[-1]
        i = pl.program_id(0)

        @pl.when(i == 0)
        def _():
            pltpu.sync_copy(w_hbm, w_ref)

        if transposed:
            acc = None
            for j in range(J):
                part = lax.dot_general(_bf(a_ref[:, j * n:(j + 1) * n]), w_ref[j], NT_DIMS,
                                       preferred_element_type=f32)
                acc = part if acc is None else acc + part
        else:
            acc = jnp.dot(_bf(a_ref[...]), w_ref[0], preferred_element_type=f32)
        rows, sums = epi(acc, *[e[...] for e in ex], *[c[...] for c in co])
        for o_ref, v, dt in zip(o_refs, rows, outs):
            o_ref[...] = v.astype(dt)
        for s_ref, v in zip(s_refs, sums):
            @pl.when(i == 0)
            def _(s_ref=s_ref, v=v):
                s_ref[...] = v

            @pl.when(i > 0)
            def _(s_ref=s_ref, v=v):
                s_ref[...] += v

    in_specs = [_rows(tm, a.shape[1]), pl.BlockSpec(memory_space=pl.ANY)]
    in_specs += [_rows(tm, W) for _ in extras] + [_full((1, W)) for _ in consts]
    out_specs = [_rows(tm, W) for _ in outs] + [_full((1, W)) for _ in range(nsums)]
    out_shape = [jax.ShapeDtypeStruct((M, W), dt) for dt in outs] + [jax.ShapeDtypeStruct((1, W), f32)] * nsums
    res, landed = _pcall(body, [a, w3, *extras, *consts], name=name, grid=(M // tm,), in_specs=in_specs,
                         out_specs=out_specs, out_shape=out_shape, scratch_shapes=[pltpu.VMEM((J, K, n), bf16)],
                         rides=rides)
    return (res, landed) if rides else res


def mm_bwd_w(a, dy, J, name, *, tm=2048, tn=None, tk=None, rides=()):
    M, K = a.shape
    n = dy.shape[1] // J
    tm = min(tm, M)
    tn = tn or n
    tk = tk or K
    nl = n // tn
    nk = K // tk
    ns = M // tm

    def body(a_ref, dy_ref, o_ref, acc_ref):
        s = pl.program_id(3)
        part = lax.dot_general(_bf(a_ref[...]), _bf(dy_ref[...]), TN_DIMS, preferred_element_type=f32)

        @pl.when(s == 0)
        def _():
            acc_ref[...] = part

        @pl.when(s > 0)
        def _():
            acc_ref[...] += part

        @pl.when(s == ns - 1)
        def _():
            o_ref[0] = acc_ref[...].astype(bf16)

    res, landed = _pcall(
        body, [a, dy], name=name, grid=(J, nl, nk, ns),
        in_specs=[pl.BlockSpec((tm, tk), lambda j, l, k, s: (s, k)),
                  pl.BlockSpec((tm, tn), lambda j, l, k, s: (s, j * nl + l))],
        out_specs=[pl.BlockSpec((1, tk, tn), lambda j, l, k, s: (j, k, l))],
        out_shape=[jax.ShapeDtypeStruct((J, K, n), bf16)],
        scratch_shapes=[pltpu.VMEM((tk, tn), f32)], rides=rides)
    return (res[0], landed) if rides else res[0]


def _relu2(v):
    r = jnp.maximum(v, 0.0)
    return r * r


HALO = 16


def _shift_down(z, k, halo):
    r = lax.broadcasted_iota(jnp.int32, z.shape, 0)
    y = pltpu.roll(z, k, 0)
    for q in range(k):
        y = jnp.where(r == q, halo[HALO - k + q:HALO - k + q + 1, :], y)
    return y


def _shift_up(z, k, halo):
    tm = z.shape[0]
    r = lax.broadcasted_iota(jnp.int32, z.shape, 0)
    y = pltpu.roll(z, tm - k, 0)
    for q in range(k):
        y = jnp.where(r == tm - k + q, halo[q:q + 1, :], y)
    return y


def _prev_halo(tm, cb):
    return pl.BlockSpec((HALO, CONV_WIDTH), lambda i: (jnp.maximum(i * (tm // HALO) - 1, 0), cb))


def _next_halo(tm, cb, nblk):
    return pl.BlockSpec((HALO, CONV_WIDTH), lambda i: (jnp.minimum((i + 1) * (tm // HALO), nblk - 1), cb))


def _f32(ref):
    return ref[...].astype(f32)


def conv_fwd(proj, conv_w, tm=TM):
    S = proj.shape[0]

    def body(cb_ref, cc_ref, ch_ref, cch_ref, chh_ref, w_ref, a_ref):
        i = pl.program_id(0)
        z = _f32(cc_ref) * _f32(ch_ref)
        zh = jnp.where(i == 0, 0.0, _f32(cch_ref) * _f32(chh_ref))
        w = w_ref[...]
        dwz = w[0:1, :] * _shift_down(z, 2, zh) + w[1:2, :] * _shift_down(z, 1, zh) + w[2:3, :] * z
        a_ref[...] = (_f32(cb_ref) * dwz).astype(bf16)

    return pl.pallas_call(
        body, name="conv_fwd", grid=(S // tm,),
        in_specs=[_rows(tm, CONV_WIDTH, 6), _rows(tm, CONV_WIDTH, 7), _rows(tm, CONV_WIDTH, 8),
                  _prev_halo(tm, 7), _prev_halo(tm, 8), _full((3, CONV_WIDTH))],
        out_specs=_rows(tm, CONV_WIDTH),
        out_shape=jax.ShapeDtypeStruct((S, CONV_WIDTH), bf16), compiler_params=_params())(
            proj, proj, proj, proj, proj, conv_w)


def conv_bwd(da, proj, conv_w, dproj, tm=TM):
    S = proj.shape[0]
    nt = S // tm

    def body(da_ref, cb_ref, cc_ref, ch_ref, cch_ref, chh_ref, dan_ref, cbn_ref, w_ref, _, o_ref, dw_ref):
        i = pl.program_id(0)
        cc, ch, cb, dav = _f32(cc_ref), _f32(ch_ref), _f32(cb_ref), da_ref[...]
        z = cc * ch
        zh = jnp.where(i == 0, 0.0, _f32(cch_ref) * _f32(chh_ref))
        w = w_ref[...]
        z1 = _shift_down(z, 1, zh)
        z2 = _shift_down(z, 2, zh)
        dwz = w[0:1, :] * z2 + w[1:2, :] * z1 + w[2:3, :] * z
        g = dav * cb
        gn = jnp.where(i == nt - 1, 0.0, dan_ref[...] * _f32(cbn_ref))
        dz = w[2:3, :] * g + w[1:2, :] * _shift_up(g, 1, gn) + w[0:1, :] * _shift_up(g, 2, gn)
        o_ref[:, 0:CONV_WIDTH] = (dav * dwz).astype(bf16)
        o_ref[:, CONV_WIDTH:2 * CONV_WIDTH] = (dz * ch).astype(bf16)
        o_ref[:, 2 * CONV_WIDTH:3 * CONV_WIDTH] = (dz * cc).astype(bf16)

        @pl.when(i == 0)
        def _():
            dw_ref[...] = jnp.zeros_like(dw_ref)

        dw_ref[0:1, :] += jnp.sum(g * z2, axis=0, keepdims=True)
        dw_ref[1:2, :] += jnp.sum(g * z1, axis=0, keepdims=True)
        dw_ref[2:3, :] += jnp.sum(g * z, axis=0, keepdims=True)

    return pl.pallas_call(
        body, name="conv_bwd", grid=(nt,),
        in_specs=[_rows(tm, CONV_WIDTH), _rows(tm, CONV_WIDTH, 6), _rows(tm, CONV_WIDTH, 7), _rows(tm, CONV_WIDTH, 8),
                  _prev_halo(tm, 7), _prev_halo(tm, 8),
                  _next_halo(tm, 0, S // HALO), _next_halo(tm, 6, S // HALO), _full((3, CONV_WIDTH)),
                  pl.BlockSpec(memory_space=pl.ANY)],
        out_specs=[_rows(tm, 3 * CONV_WIDTH, GATE_COLS // (3 * CONV_WIDTH)), _full((3, CONV_WIDTH))],
        out_shape=[jax.ShapeDtypeStruct(dproj.shape, bf16), jax.ShapeDtypeStruct((3, CONV_WIDTH), f32)],
        input_output_aliases={9: 0},
        compiler_params=_params())(da, proj, proj, proj, proj, proj, da, proj, conv_w, dproj)


def _cmul(ar, ai, br, bi):
    return ar * br - ai * bi, ar * bi + ai * br


def ssm_prep(lr, li, ldt, bt_re, bt_im):
    def body(lr_ref, li_ref, ldt_ref, br_ref, bi_ref, ar_ref, ai_ref, kr_ref, ki_ref, bbr_ref, bbi_ref):
        lrv, liv = lr_ref[...], li_ref[...]
        dt = jnp.exp(ldt_ref[...])
        mag = jnp.exp(lrv * dt)
        ar = mag * jnp.cos(liv * dt)
        ai = mag * jnp.sin(liv * dt)
        den = lrv * lrv + liv * liv
        nr = ar - 1.0
        kr = (nr * lrv + ai * liv) / den
        ki = (ai * lrv - nr * liv) / den
        ar_ref[...], ai_ref[...], kr_ref[...], ki_ref[...] = ar, ai, kr, ki
        bbr_ref[...] = kr * br_ref[...] - ki * bi_ref[...]
        bbi_ref[...] = kr * bi_ref[...] + ki * br_ref[...]

    v = jax.ShapeDtypeStruct((1, N_STATES), f32)
    m = jax.ShapeDtypeStruct((SSM_GROUP, N_STATES), f32)
    return pl.pallas_call(body, name="ssm_prep", out_shape=[v, v, v, v, m, m])(lr, li, ldt, bt_re, bt_im)


def _pow_segment(ar, ai, seg_len):
    pr, pi = ar, ai
    for _ in range(int(math.log2(seg_len))):
        pr, pi = _cmul(pr, pi, pr, pi)
    return pr, pi


GELU_K = math.sqrt(2.0 / math.pi)
GELU_C = 0.044715


def _gelu(v):
    return 0.5 * v * (1.0 + jnp.tanh(GELU_K * (v + GELU_C * v * v * v)))


def _gelu_grad(v):
    t = jnp.tanh(GELU_K * (v + GELU_C * v * v * v))
    return 0.5 * (1.0 + t) + 0.5 * v * (1.0 - t * t) * GELU_K * (1.0 + 3.0 * GELU_C * v * v)


def ssm_scan_fwd(u_perm, bd_r, bd_i, ar, ai, *, final=None, rides=()):
    S = u_perm.shape[0]
    R = SCAN_ROWS
    nblk = S // R
    seg_len = S // N_SEG
    nstrip = N_STATES // LANE_STRIP
    store = final is not None

    def body(*refs):
        if store:
            (u_ref, bdr_ref, bdi_ref, ar_ref, ai_ref, cdr_ref, cdi_ref, d_ref, ir_ref, ii_ref,
             y_ref, ys_ref, sr_ref, si_ref, bur, bui, car, cai) = refs
        else:
            u_ref, bdr_ref, bdi_ref, ar_ref, ai_ref, or_ref, oi_ref, bur, bui, car, cai = refs
        i = pl.program_id(0)
        ub = _bf(u_ref[...])
        u = ub.astype(f32)
        for cb in range(4):
            us = ub[:, cb * 128:(cb + 1) * 128]
            bur[:, cb * 512:(cb + 1) * 512] = jnp.dot(us, bdr_ref[cb], preferred_element_type=f32)
            bui[:, cb * 512:(cb + 1) * 512] = jnp.dot(us, bdi_ref[cb], preferred_element_type=f32)

        @pl.when(i == 0)
        def _():
            if store:
                car[...] = ir_ref[...]
                cai[...] = ii_ref[...]
            else:
                car[...] = jnp.zeros_like(car)
                cai[...] = jnp.zeros_like(cai)

        for ls in range(nstrip):
            lanes = pl.ds(ls * LANE_STRIP, LANE_STRIP)
            a_r = jnp.broadcast_to(ar_ref[:, lanes], (N_SEG, LANE_STRIP))
            a_i = jnp.broadcast_to(ai_ref[:, lanes], (N_SEG, LANE_STRIP))

            def step(t, carry, lanes=lanes, a_r=a_r, a_i=a_i):
                s_r, s_i = carry
                row = pl.multiple_of(t * 8, 8)
                n_r = a_r * s_r - a_i * s_i + bur[pl.ds(row, 8), lanes]
                n_i = a_r * s_i + a_i * s_r + bui[pl.ds(row, 8), lanes]
                if store:
                    sr_ref[pl.ds(row, 8), lanes] = n_r
                    si_ref[pl.ds(row, 8), lanes] = n_i
                return n_r, n_i

            e_r, e_i = lax.fori_loop(0, R // 8, step, (car[:, lanes], cai[:, lanes]), unroll=True)
            car[:, lanes] = e_r
            cai[:, lanes] = e_i

        if store:
            for cb in range(4):
                st_r = sr_ref[:, cb * 512:(cb + 1) * 512].astype(bf16)
                st_i = si_ref[:, cb * 512:(cb + 1) * 512].astype(bf16)
                y = (jnp.dot(st_r, cdr_ref[cb], preferred_element_type=f32)
                     - jnp.dot(st_i, cdi_ref[cb], preferred_element_type=f32))
                cols = slice(cb * 128, (cb + 1) * 128)
                y = y + d_ref[:, cols] * u[:, cols]
                y_ref[:, cols] = y
                ys_ref[:, cols] = _gelu(y).astype(bf16)
        else:
            @pl.when(i == nblk - 1)
            def _():
                p_r, p_i = _pow_segment(ar_ref[...], ai_ref[...], seg_len)
                t_r, t_i = car[0:1, :], cai[0:1, :]
                or_ref[0:1, :] = jnp.zeros((1, N_STATES), f32)
                oi_ref[0:1, :] = jnp.zeros((1, N_STATES), f32)
                for j in range(1, N_SEG):
                    or_ref[j:j + 1, :] = t_r
                    oi_ref[j:j + 1, :] = t_i
                    m_r, m_i = _cmul(p_r, p_i, t_r, t_i)
                    t_r, t_i = car[j:j + 1, :] + m_r, cai[j:j + 1, :] + m_i

    blk = lambda w: pl.BlockSpec((R, w), lambda i: (i, 0))
    in_specs = [blk(SSM_WIDTH), _full((4, 128, 512)), _full((4, 128, 512)), _full((1, N_STATES)), _full((1, N_STATES))]
    args = [u_perm, bd_r, bd_i, ar, ai]
    scratch = [pltpu.VMEM((R, N_STATES), f32), pltpu.VMEM((R, N_STATES), f32),
               pltpu.VMEM((N_SEG, N_STATES), f32), pltpu.VMEM((N_SEG, N_STATES), f32)]
    if store:
        in_specs += [_full((4, 512, 128)), _full((4, 512, 128)), _full((1, SSM_WIDTH)),
                     _full((N_SEG, N_STATES)), _full((N_SEG, N_STATES))]
        args += list(final)
        out_specs = [blk(SSM_WIDTH), blk(SSM_WIDTH), blk(N_STATES), blk(N_STATES)]
        out_shape = [jax.ShapeDtypeStruct((S, SSM_WIDTH), f32), jax.ShapeDtypeStruct((S, SSM_WIDTH), bf16),
                     jax.ShapeDtypeStruct((S, N_STATES), f32), jax.ShapeDtypeStruct((S, N_STATES), f32)]
        name = "ssm_scan_fwd"
    else:
        out_specs = [_full((N_SEG, N_STATES)), _full((N_SEG, N_STATES))]
        out_shape = [jax.ShapeDtypeStruct((N_SEG, N_STATES), f32)] * 2
        name = "ssm_scan_fwd_carry"
    res, landed = _pcall(body, args, name=name, grid=(nblk,), in_specs=in_specs, out_specs=out_specs,
                         out_shape=out_shape, scratch_shapes=scratch, rides=rides)
    return (res, landed) if rides else res


def ssm_scan_bwd(dys_perm, y_perm, cdt_r, cdt_i, ar, ai, *, final=None, rides=()):
    S = dys_perm.shape[0]
    R = SCAN_ROWS
    nblk = S // R
    seg_len = S // N_SEG
    store = final is not None
    strip = LANE_STRIP // 2 if store else LANE_STRIP
    nstrip = N_STATES // strip

    def body(*refs):
        if store:
            (dys_ref, y_ref, cdr_ref, cdi_ref, ar_ref, ai_ref, u_ref, sr_ref, si_ref, bdr_ref, bdi_ref, d_ref, ir_ref,
             ii_ref, du_ref, dbr_ref, dbi_ref, dcr_ref, dci_ref, dar_ref, dai_ref, dd_ref, dsr, dsi, lmr, lmi, car,
             cai) = refs
        else:
            dys_ref, y_ref, cdr_ref, cdi_ref, ar_ref, ai_ref, or_ref, oi_ref, dsr, dsi, car, cai = refs
        i = pl.program_id(0)
        dy = dys_ref[...] * _gelu_grad(y_ref[...])
        dyb = dy.astype(bf16)
        for cb in range(4):
            ds_ = dyb[:, cb * 128:(cb + 1) * 128]
            dsr[:, cb * 512:(cb + 1) * 512] = jnp.dot(ds_, cdr_ref[cb], preferred_element_type=f32)
            dsi[:, cb * 512:(cb + 1) * 512] = -jnp.dot(ds_, cdi_ref[cb], preferred_element_type=f32)

        @pl.when(i == 0)
        def _():
            if store:
                car[...] = ir_ref[...]
                cai[...] = ii_ref[...]
                dar_ref[...] = jnp.zeros_like(dar_ref)
                dai_ref[...] = jnp.zeros_like(dai_ref)
                dbr_ref[...] = jnp.zeros_like(dbr_ref)
                dbi_ref[...] = jnp.zeros_like(dbi_ref)
                dcr_ref[...] = jnp.zeros_like(dcr_ref)
                dci_ref[...] = jnp.zeros_like(dci_ref)
                dd_ref[...] = jnp.zeros_like(dd_ref)
            else:
                car[...] = jnp.zeros_like(car)
                cai[...] = jnp.zeros_like(cai)

        for ls in range(nstrip):
            lanes = pl.ds(ls * strip, strip)
            a_r = jnp.broadcast_to(ar_ref[:, lanes], (N_SEG, strip))
            a_i = jnp.broadcast_to(ai_ref[:, lanes], (N_SEG, strip))
            zero = jnp.zeros((N_SEG, strip), f32)

            def step(k, carry, lanes=lanes, a_r=a_r, a_i=a_i):
                l_r, l_i, g_r, g_i = carry
                row = pl.multiple_of((R // 8 - 1 - k) * 8, 8)
                if store:
                    s_r = sr_ref[pl.ds(row, 8), lanes]
                    s_i = si_ref[pl.ds(row, 8), lanes]
                    g_r = g_r + l_r * s_r + l_i * s_i
                    g_i = g_i + l_i * s_r - l_r * s_i
                n_r = dsr[pl.ds(row, 8), lanes] + a_r * l_r + a_i * l_i
                n_i = dsi[pl.ds(row, 8), lanes] + a_r * l_i - a_i * l_r
                if store:
                    lmr[pl.ds(row, 8), lanes] = n_r
                    lmi[pl.ds(row, 8), lanes] = n_i
                return n_r, n_i, g_r, g_i

            e_r, e_i, g_r, g_i = lax.fori_loop(0, R // 8, step, (car[:, lanes], cai[:, lanes], zero, zero),
                                               unroll=True)
            car[:, lanes] = e_r
            cai[:, lanes] = e_i
            if store:
                dar_ref[:, lanes] += g_r
                dai_ref[:, lanes] += g_i

        if store:
            ub = _bf(u_ref[...])
            u = ub.astype(f32)
            for cb in range(4):
                cols = slice(cb * 128, (cb + 1) * 128)
                st = slice(cb * 512, (cb + 1) * 512)
                l_r = lmr[:, st].astype(bf16)
                l_i = lmi[:, st].astype(bf16)
                du = (jnp.dot(l_r, bdr_ref[cb], preferred_element_type=f32)
                      + jnp.dot(l_i, bdi_ref[cb], preferred_element_type=f32))
                du_ref[:, cols] = (du + d_ref[:, cols] * dy[:, cols]).astype(bf16)
                dbr_ref[cb] += lax.dot_general(l_r, ub[:, cols], TN_DIMS, preferred_element_type=f32)
                dbi_ref[cb] += lax.dot_general(l_i, ub[:, cols], TN_DIMS, preferred_element_type=f32)
                dcr_ref[cb] += lax.dot_general(dyb[:, cols], sr_ref[:, st].astype(bf16), TN_DIMS,
                                               preferred_element_type=f32)
                dci_ref[cb] += lax.dot_general(dyb[:, cols], si_ref[:, st].astype(bf16), TN_DIMS,
                                               preferred_element_type=f32)
            dd_ref[...] += jnp.sum(dy * u, axis=0, keepdims=True)
        else:
            @pl.when(i == nblk - 1)
            def _():
                p_r, p_i = _pow_segment(ar_ref[...], ai_ref[...], seg_len)
                p_i = -p_i
                t_r, t_i = car[N_SEG - 1:N_SEG, :], cai[N_SEG - 1:N_SEG, :]
                or_ref[N_SEG - 1:N_SEG, :] = jnp.zeros((1, N_STATES), f32)
                oi_ref[N_SEG - 1:N_SEG, :] = jnp.zeros((1, N_STATES), f32)
                for j in range(N_SEG - 2, -1, -1):
                    or_ref[j:j + 1, :] = t_r
                    oi_ref[j:j + 1, :] = t_i
                    m_r, m_i = _cmul(p_r, p_i, t_r, t_i)
                    t_r, t_i = car[j:j + 1, :] + m_r, cai[j:j + 1, :] + m_i

    blk = lambda w: pl.BlockSpec((R, w), lambda i: (nblk - 1 - i, 0))
    in_specs = [blk(SSM_WIDTH), blk(SSM_WIDTH), _full((4, 128, 512)), _full((4, 128, 512)), _full((1, N_STATES)),
                _full((1, N_STATES))]
    args = [dys_perm, y_perm, cdt_r, cdt_i, ar, ai]
    seg = jax.ShapeDtypeStruct((N_SEG, N_STATES), f32)
    if store:
        in_specs += [blk(SSM_WIDTH), blk(N_STATES), blk(N_STATES), _full((4, 512, 128)), _full((4, 512, 128)),
                     _full((1, SSM_WIDTH)), _full((N_SEG, N_STATES)), _full((N_SEG, N_STATES))]
        args += list(final)
        out_specs = [blk(SSM_WIDTH), _full((4, 512, 128)), _full((4, 512, 128)), _full((4, 128, 512)),
                     _full((4, 128, 512)), _full((N_SEG, N_STATES)), _full((N_SEG, N_STATES)), _full((1, SSM_WIDTH))]
        b_acc = jax.ShapeDtypeStruct((4, 512, 128), f32)
        c_acc = jax.ShapeDtypeStruct((4, 128, 512), f32)
        out_shape = [jax.ShapeDtypeStruct((S, SSM_WIDTH), bf16), b_acc, b_acc, c_acc, c_acc, seg, seg,
                     jax.ShapeDtypeStruct((1, SSM_WIDTH), f32)]
        scratch = [pltpu.VMEM((R, N_STATES), f32)] * 4 + [pltpu.VMEM((N_SEG, N_STATES), f32)] * 2
        name = "ssm_scan_bwd"
    else:
        out_specs = [_full((N_SEG, N_STATES)), _full((N_SEG, N_STATES))]
        out_shape = [seg, seg]
        scratch = [pltpu.VMEM((R, N_STATES), f32)] * 2 + [pltpu.VMEM((N_SEG, N_STATES), f32)] * 2
        name = "ssm_scan_bwd_carry"
    res, landed = _pcall(body, args, name=name, grid=(nblk,), in_specs=in_specs, out_specs=out_specs,
                         out_shape=out_shape, scratch_shapes=scratch, rides=rides)
    return (res, landed) if rides else res


def ssm_param_bwd(dbb_r, dbb_i, bt_re, bt_im, kr, ki, ar, ai, lr, li, ldt, da_r, da_i):
    def body(dbr_ref, dbi_ref, br_ref, bi_ref, kr_ref, ki_ref, ar_ref, ai_ref, lr_ref, li_ref, ldt_ref, dar_ref,
             dai_ref, obr_ref, obi_ref, olr_ref, oli_ref, odt_ref):
        dbr, dbi, b_r, b_i = dbr_ref[...], dbi_ref[...], br_ref[...], bi_ref[...]
        k_r, k_i, a_r, a_i = kr_ref[...], ki_ref[...], ar_ref[...], ai_ref[...]
        l_r, l_i = lr_ref[...], li_ref[...]
        dt = jnp.exp(ldt_ref[...])
        obr_ref[...] = k_r * dbr + k_i * dbi
        obi_ref[...] = k_r * dbi - k_i * dbr
        gk_r = jnp.sum(dbr * b_r + dbi * b_i, axis=0, keepdims=True)
        gk_i = jnp.sum(dbi * b_r - dbr * b_i, axis=0, keepdims=True)
        ga_r = jnp.sum(dar_ref[...], axis=0, keepdims=True)
        ga_i = jnp.sum(dai_ref[...], axis=0, keepdims=True)
        den = l_r * l_r + l_i * l_i
        c_r, c_i = l_r / den, l_i / den
        m_r, m_i = _cmul(c_r, c_i, gk_r, gk_i)
        g_r, g_i = ga_r + m_r, ga_i + m_i
        t1_r, t1_i = _cmul(dt * a_r, -dt * a_i, g_r, g_i)
        q_r, q_i = _cmul(k_r, k_i, c_r, -c_i)
        t2_r, t2_i = _cmul(-q_r, q_i, gk_r, gk_i)
        olr_ref[...] = t1_r + t2_r
        oli_ref[...] = t1_i + t2_i
        w_r, w_i = _cmul(l_r, l_i, a_r, a_i)
        odt_ref[...] = dt * (w_r * g_r + w_i * g_i)

    v = jax.ShapeDtypeStruct((1, N_STATES), f32)
    m = jax.ShapeDtypeStruct((SSM_GROUP, N_STATES), f32)
    return pl.pallas_call(body, name="ssm_param_bwd", out_shape=[m, m, v, v, v])(
        dbb_r, dbb_i, bt_re, bt_im, kr, ki, ar, ai, lr, li, ldt, da_r, da_i)


def group_sum(v):
    def body(v_ref, o_ref):
        o_ref[...] = jnp.sum(v_ref[...], axis=-1, keepdims=True)
    return pl.pallas_call(body, name="group_sum", out_shape=jax.ShapeDtypeStruct((v.shape[0], 1), f32))(v)


ATTN_SCALE = XATTN_HEAD_DIM ** -0.5


def _attn_probs(q_h, k_h):
    s = lax.dot_general(q_h, k_h, NT_DIMS, preferred_element_type=f32) * ATTN_SCALE
    e = jnp.exp(s - jnp.max(s, axis=-1, keepdims=True))
    return e / jnp.sum(e, axis=-1, keepdims=True)


def attn_fwd(proj, kv, tm=TM):
    S = proj.shape[0]
    M = kv.shape[0]

    def body(q_ref, kv_ref, o_ref):
        for h in range(XATTN_HEADS):
            cols = slice(h * XATTN_HEAD_DIM, (h + 1) * XATTN_HEAD_DIM)
            q_h = q_ref[:, cols].astype(bf16)
            k_h = kv_ref[:, cols]
            v_h = kv_ref[:, XATTN_WIDTH + h * XATTN_HEAD_DIM:XATTN_WIDTH + (h + 1) * XATTN_HEAD_DIM]
            p = _attn_probs(q_h, k_h)
            o_ref[:, cols] = jnp.dot(p.astype(bf16), v_h, preferred_element_type=f32).astype(bf16)

    return pl.pallas_call(
        body, name="attn_fwd", grid=(S // tm,), in_specs=[_rows(tm, XATTN_WIDTH, 10), _full((M, 2 * XATTN_WIDTH))],
        out_specs=_rows(tm, XATTN_WIDTH), out_shape=jax.ShapeDtypeStruct((S, XATTN_WIDTH), bf16),
        compiler_params=_params())(proj, kv)


def attn_bwd(do, proj, kv, dproj, tm=TM):
    S = proj.shape[0]
    M = kv.shape[0]

    def body(do_ref, q_ref, kv_ref, _, dq_ref, dkv_ref):
        i = pl.program_id(0)

        @pl.when(i == 0)
        def _():
            dkv_ref[...] = jnp.zeros_like(dkv_ref)

        for h in range(XATTN_HEADS):
            cols = slice(h * XATTN_HEAD_DIM, (h + 1) * XATTN_HEAD_DIM)
            vcols = slice(XATTN_WIDTH + h * XATTN_HEAD_DIM, XATTN_WIDTH + (h + 1) * XATTN_HEAD_DIM)
            q_h = q_ref[:, cols].astype(bf16)
            k_h = kv_ref[:, cols]
            v_h = kv_ref[:, vcols]
            do_h = do_ref[:, cols]
            p = _attn_probs(q_h, k_h)
            dp = lax.dot_general(do_h, v_h, NT_DIMS, preferred_element_type=f32)
            ds = (p * (dp - jnp.sum(dp * p, axis=-1, keepdims=True)) * ATTN_SCALE).astype(bf16)
            dq_ref[:, cols] = jnp.dot(ds, k_h, preferred_element_type=f32).astype(bf16)
            dkv_ref[:, cols] += lax.dot_general(ds, q_h, TN_DIMS, preferred_element_type=f32)
            dkv_ref[:, vcols] += lax.dot_general(p.astype(bf16), do_h, TN_DIMS, preferred_element_type=f32)

    return pl.pallas_call(
        body, name="attn_bwd", grid=(S // tm,),
        in_specs=[_rows(tm, XATTN_WIDTH), _rows(tm, XATTN_WIDTH, 10), _full((M, 2 * XATTN_WIDTH)),
                  pl.BlockSpec(memory_space=pl.ANY)],
        out_specs=[_rows(tm, XATTN_WIDTH, 10), _full((M, 2 * XATTN_WIDTH))],
        out_shape=[jax.ShapeDtypeStruct(dproj.shape, bf16), jax.ShapeDtypeStruct((M, 2 * XATTN_WIDTH), f32)],
        input_output_aliases={3: 0},
        compiler_params=_params())(do, proj, kv, dproj)


def merge_fwd(proj, b_gate, y_a, glu, y_c, tm=TM, rides=()):
    S = proj.shape[0]

    def body(g0_ref, g1_ref, g2_ref, b_ref, ya_ref, ga_ref, gb_ref, yc_ref, o_ref):
        b = b_ref[...]
        g0 = _sig(_f32(g0_ref) + b[:, 0:D_MODEL])
        g1 = _sig(_f32(g1_ref) + b[:, D_MODEL:2 * D_MODEL])
        g2 = _sig(_f32(g2_ref) + b[:, 2 * D_MODEL:3 * D_MODEL])
        y_b = ga_ref[...].astype(f32) * _sig(gb_ref[...].astype(f32))
        o_ref[...] = (g0 * ya_ref[...].astype(f32) + g1 * y_b + g2 * yc_ref[...].astype(f32)).astype(bf16)

    res, landed = _pcall(
        body, [proj, proj, proj, b_gate, y_a, glu, glu, y_c], name="merge_fwd", grid=(S // tm,),
        in_specs=[_rows(tm, D_MODEL, 0), _rows(tm, D_MODEL, 1), _rows(tm, D_MODEL, 2), _full((1, GATE_COLS)),
                  _rows(tm, D_MODEL), _rows(tm, D_MODEL, 0), _rows(tm, D_MODEL, 1), _rows(tm, D_MODEL)],
        out_specs=[_rows(tm, D_MODEL)], out_shape=[jax.ShapeDtypeStruct((S, D_MODEL), bf16)], rides=rides)
    return (res[0], landed) if rides else res[0]


def merge_bwd(dm, proj, b_gate, y_a, glu, y_c, tm=256, rides=()):
    S = proj.shape[0]

    def body(dm_ref, g0_ref, g1_ref, g2_ref, b_ref, ya_ref, ga_ref, gb_ref, yc_ref,
             dg_ref, dya_ref, dgl_ref, dyc_ref, dbg_ref):
        i = pl.program_id(0)
        b = b_ref[...]
        dmv = dm_ref[...].astype(f32)
        ga, sb = ga_ref[...].astype(f32), _sig(gb_ref[...].astype(f32))
        ys = (ya_ref[...].astype(f32), ga * sb, yc_ref[...].astype(f32))
        gs = tuple(_sig(_f32(r) + b[:, k * D_MODEL:(k + 1) * D_MODEL]) for k, r in enumerate((g0_ref, g1_ref, g2_ref)))

        @pl.when(i == 0)
        def _():
            dbg_ref[...] = jnp.zeros_like(dbg_ref)

        for k in range(3):
            dpre = dmv * ys[k] * gs[k] * (1.0 - gs[k])
            dg_ref[:, k * D_MODEL:(k + 1) * D_MODEL] = dpre.astype(bf16)
            dbg_ref[:, k * D_MODEL:(k + 1) * D_MODEL] += jnp.sum(dpre, axis=0, keepdims=True)
        dya_ref[...] = (dmv * gs[0]).astype(bf16)
        dyc_ref[...] = (dmv * gs[2]).astype(bf16)
        dyb = dmv * gs[1]
        dgl_ref[:, 0:D_MODEL] = (dyb * sb).astype(bf16)
        dgl_ref[:, D_MODEL:2 * D_MODEL] = (dyb * ga * sb * (1.0 - sb)).astype(bf16)

    res, landed = _pcall(
        body, [dm, proj, proj, proj, b_gate, y_a, glu, glu, y_c], name="merge_bwd", grid=(S // tm,),
        in_specs=[_rows(tm, D_MODEL), _rows(tm, D_MODEL, 0), _rows(tm, D_MODEL, 1), _rows(tm, D_MODEL, 2),
                  _full((1, GATE_COLS)), _rows(tm, D_MODEL), _rows(tm, D_MODEL, 0), _rows(tm, D_MODEL, 1),
                  _rows(tm, D_MODEL)],
        out_specs=[_rows(tm, GATE_COLS), _rows(tm, D_MODEL), _rows(tm, 2 * D_MODEL), _rows(tm, D_MODEL),
                   _full((1, GATE_COLS))],
        out_shape=[jax.ShapeDtypeStruct((S, IN_COLS), bf16), jax.ShapeDtypeStruct((S, D_MODEL), bf16),
                   jax.ShapeDtypeStruct((S, 2 * D_MODEL), bf16), jax.ShapeDtypeStruct((S, D_MODEL), bf16),
                   jax.ShapeDtypeStruct((1, GATE_COLS), f32)], rides=rides)
    return (res, landed) if rides else res


def _ln_stats(r):
    mu = jnp.mean(r, axis=-1, keepdims=True)
    xc = r - mu
    var = jnp.mean(xc * xc, axis=-1, keepdims=True)
    rstd = lax.rsqrt(var + LN_EPS)
    return xc * rstd, rstd


def _ln_bwd(dy, xhat, rstd, g):
    dxh = dy * g
    return rstd * (dxh - jnp.mean(dxh, axis=-1, keepdims=True) - xhat * jnp.mean(dxh * xhat, axis=-1, keepdims=True))


def _colsum(v):
    return jnp.sum(v, axis=0, keepdims=True)


def _ln2_loss_epi(t2, h1, target, b_down, g, b):
    xhat, rstd = _ln_stats(ALPHA * h1 + t2 + b_down)
    err = xhat * g + b - target
    dout = err * (1.0 / D_MODEL)
    dr = _ln_bwd(dout, xhat, rstd, g)
    return (dr, dr), (0.5 * _colsum(err * err) * (1.0 / D_MODEL), _colsum(dout * xhat), _colsum(dout), _colsum(dr))


def _ln1_fwd_epi(t1, x, g, b):
    xhat, _ = _ln_stats(ALPHA * x + t1)
    h = xhat * g + b
    return (t1, h, h), ()


def _ln1_bwd_epi(t3, x, t1, dr2, g):
    xhat, rstd = _ln_stats(ALPHA * x + t1)
    dh = ALPHA * dr2 + t3
    dr = _ln_bwd(dh, xhat, rstd, g)
    return (dr, dr), (_colsum(dh * xhat), _colsum(dh))


def exchange(*rides, name):
    return _pcall(lambda: None, [], name=name, grid=(), in_specs=[], out_specs=[], out_shape=[], rides=rides)[1]


ROW_TILE = 256


def sum_slots(recv, name):
    n, R, C = recv.shape
    tm = min(R, ROW_TILE)

    def body(r_ref, o_ref):
        acc = r_ref[0].astype(f32)
        for k in range(1, n):
            acc = acc + r_ref[k].astype(f32)
        o_ref[...] = acc

    return pl.pallas_call(
        body, name=name, grid=(R // tm,), in_specs=[pl.BlockSpec((n, tm, C), lambda i: (0, i, 0))],
        out_specs=_rows(tm, C), out_shape=jax.ShapeDtypeStruct((R, C), f32), compiler_params=_params())(recv)


def _adamw(w, g, m, v):
    m = ADAM_B1 * m + (1.0 - ADAM_B1) * g
    v = ADAM_B2 * v + (1.0 - ADAM_B2) * (g * g)
    m_hat = m / (1.0 - ADAM_B1 ** ADAM_STEP)
    v_hat = v / (1.0 - ADAM_B2 ** ADAM_STEP)
    delta = -ADAM_LR * (m_hat / (jnp.sqrt(v_hat) + ADAM_EPS) + ADAM_WD * w)
    return delta, m, v


def adam_pair(p, q, w, m, v, name):
    R, C = w.shape
    tm = min(R, ROW_TILE)

    def body(p_ref, q_ref, w_ref, m_ref, v_ref, g_ref, d_ref, nm_ref, nv_ref):
        g = p_ref[...] + q_ref[...]
        g_ref[...] = g
        d_ref[...], nm_ref[...], nv_ref[...] = _adamw(w_ref[...], g, m_ref[...], v_ref[...])

    o = jax.ShapeDtypeStruct((R, C), f32)
    return pl.pallas_call(body, name=name, grid=(R // tm,), in_specs=[_rows(tm, C)] * 5,
                          out_specs=[_rows(tm, C)] * 4, out_shape=[o] * 4, compiler_params=_params())(p, q, w, m, v)


def adam_slots(recv, w, m, v, name):
    n = recv.shape[0]

    def body(r_ref, w_ref, m_ref, v_ref, g_ref, d_ref, nm_ref, nv_ref):
        g = r_ref[0]
        for k in range(1, n):
            g = g + r_ref[k]
        g_ref[...] = g
        d_ref[...], nm_ref[...], nv_ref[...] = _adamw(w_ref[...], g, m_ref[...], v_ref[...])

    o = jax.ShapeDtypeStruct(w.shape, f32)
    return pl.pallas_call(body, name=name, out_shape=[o] * 4, compiler_params=_params())(recv, w, m, v)


BIG = ("w_in", "w_conv_out", "w_glu", "w_xattn_out", "w_kv", "w_out", "w_up", "w_down")
MID = ("w_conv_out", "w_glu", "w_xattn_out", "w_kv", "w_out")
SMALL = ("b_gate", "ssm_lam_re", "ssm_lam_im", "ssm_log_dt", "ssm_b_re", "ssm_b_im", "ssm_c_re", "ssm_c_im", "ssm_d",
         "ln1_g", "ln1_b", "b_up", "b_down", "ln2_g", "ln2_b")
def _pad_flat(a, mult=1024):
    a = a.reshape(-1)
    return jnp.pad(a, (0, (-a.shape[0]) % mult))


def _perm(a):
    S, W = a.shape
    return a.reshape(N_SEG, S // N_SEG, W).transpose(1, 0, 2).reshape(S, W)


def _unperm(a):
    S, W = a.shape
    return a.reshape(S // N_SEG, N_SEG, W).transpose(1, 0, 2).reshape(S, W)


def _state_rows(a):
    return a.transpose(2, 0, 1).reshape(SSM_GROUP, N_STATES)


def _block_diag_b(bt):
    b4 = bt.reshape(SSM_GROUP, 4, 8, SSM_STATE)
    eye = jnp.eye(8, dtype=bt.dtype)
    return jnp.einsum("hcgp,gk->cghkp", b4, eye).reshape(4, 128, 512)


def _block_diag_c(c):
    c4 = c.reshape(4, 8, SSM_GROUP, SSM_STATE)
    eye = jnp.eye(8, dtype=c.dtype)
    return jnp.einsum("cghp,gk->cgpkh", c4, eye).reshape(4, 512, 128)


def _diag_b(acc):
    a = acc.reshape(4, 8, SSM_STATE, 8, SSM_GROUP)
    eye = jnp.eye(8, dtype=acc.dtype)
    return jnp.einsum("cgpkh,gk->hcgp", a, eye).reshape(SSM_GROUP, N_STATES)


def _diag_c(acc):
    a = acc.reshape(4, 8, SSM_GROUP, 8, SSM_STATE)
    eye = jnp.eye(8, dtype=acc.dtype)
    return jnp.einsum("cghkp,gk->cghp", a, eye).reshape(SSM_GROUPS, SSM_GROUP, SSM_STATE)


def kernel(x, mem, w_in, b_gate, conv_w, w_conv_out, ssm_lam_re, ssm_lam_im, ssm_log_dt, ssm_b_re, ssm_b_im, ssm_c_re, ssm_c_im, ssm_d, w_glu, w_kv, w_xattn_out, w_out, ln1_g, ln1_b, w_up, b_up, w_down, b_down, ln2_g, ln2_b, loss_target, m_w_in, m_b_gate, m_conv_w, m_w_conv_out, m_ssm_lam_re, m_ssm_lam_im, m_ssm_log_dt, m_ssm_b_re, m_ssm_b_im, m_ssm_c_re, m_ssm_c_im, m_ssm_d, m_w_glu, m_w_kv, m_w_xattn_out, m_w_out, m_ln1_g, m_ln1_b, m_w_up, m_b_up, m_w_down, m_b_down, m_ln2_g, m_ln2_b, v_w_in, v_b_gate, v_conv_w, v_w_conv_out, v_ssm_lam_re, v_ssm_lam_im, v_ssm_log_dt, v_ssm_b_re, v_ssm_b_im, v_ssm_c_re, v_ssm_c_im, v_ssm_d, v_w_glu, v_w_kv, v_w_xattn_out, v_w_out, v_ln1_g, v_ln1_b, v_w_up, v_b_up, v_w_down, v_b_down, v_ln2_g, v_ln2_b):
    W = dict(w_in=w_in, b_gate=b_gate, conv_w=conv_w, w_conv_out=w_conv_out, ssm_lam_re=ssm_lam_re,
             ssm_lam_im=ssm_lam_im, ssm_log_dt=ssm_log_dt, ssm_b_re=ssm_b_re, ssm_b_im=ssm_b_im, ssm_c_re=ssm_c_re,
             ssm_c_im=ssm_c_im, ssm_d=ssm_d, w_glu=w_glu, w_kv=w_kv, w_xattn_out=w_xattn_out, w_out=w_out,
             ln1_g=ln1_g, ln1_b=ln1_b, w_up=w_up, b_up=b_up, w_down=w_down, b_down=b_down, ln2_g=ln2_g, ln2_b=ln2_b)
    MOM = dict(w_in=m_w_in, b_gate=m_b_gate, conv_w=m_conv_w, w_conv_out=m_w_conv_out, ssm_lam_re=m_ssm_lam_re,
               ssm_lam_im=m_ssm_lam_im, ssm_log_dt=m_ssm_log_dt, ssm_b_re=m_ssm_b_re, ssm_b_im=m_ssm_b_im,
               ssm_c_re=m_ssm_c_re, ssm_c_im=m_ssm_c_im, ssm_d=m_ssm_d, w_glu=m_w_glu, w_kv=m_w_kv,
               w_xattn_out=m_w_xattn_out, w_out=m_w_out, ln1_g=m_ln1_g, ln1_b=m_ln1_b, w_up=m_w_up, b_up=m_b_up,
               w_down=m_w_down, b_down=m_b_down, ln2_g=m_ln2_g, ln2_b=m_ln2_b)
    VEL = dict(w_in=v_w_in, b_gate=v_b_gate, conv_w=v_conv_w, w_conv_out=v_w_conv_out, ssm_lam_re=v_ssm_lam_re,
               ssm_lam_im=v_ssm_lam_im, ssm_log_dt=v_ssm_log_dt, ssm_b_re=v_ssm_b_re, ssm_b_im=v_ssm_b_im,
               ssm_c_re=v_ssm_c_re, ssm_c_im=v_ssm_c_im, ssm_d=v_ssm_d, w_glu=v_w_glu, w_kv=v_w_kv,
               w_xattn_out=v_w_xattn_out, w_out=v_w_out, ln1_g=v_ln1_g, ln1_b=v_ln1_b, w_up=v_w_up, b_up=v_b_up,
               w_down=v_w_down, b_down=v_b_down, ln2_g=v_ln2_g, ln2_b=v_ln2_b)
    names = list(W)
    xy = 2 * lax.axis_index("x") + lax.axis_index("y")

    xs = x[0]
    S = xs.shape[0]
    mems = mem[0]
    tgt = loss_target[0]

    shard_bf = {n: W[n][0].astype(bf16) for n in BIG}

    lr = ssm_lam_re.reshape(1, N_STATES)
    li = ssm_lam_im.reshape(1, N_STATES)
    ldt = jnp.repeat(ssm_log_dt.reshape(SSM_GROUPS), SSM_STATE).reshape(1, N_STATES)
    bt_re, bt_im = _state_rows(ssm_b_re[0]), _state_rows(ssm_b_im[0])
    ar, ai, kr, ki, bbt_r, bbt_i = ssm_prep(lr, li, ldt, bt_re, bt_im)
    bd_r, bd_i = _block_diag_b(bbt_r).astype(bf16), _block_diag_b(bbt_i).astype(bf16)
    cd_r, cd_i = _block_diag_c(ssm_c_re[0]).astype(bf16), _block_diag_c(ssm_c_im[0]).astype(bf16)
    bdt_r, bdt_i = bd_r.transpose(0, 2, 1), bd_i.transpose(0, 2, 1)
    cdt_r, cdt_i = cd_r.transpose(0, 2, 1), cd_i.transpose(0, 2, 1)
    d_skip = ssm_d.reshape(1, SSM_WIDTH)

    mem_bf = mems.astype(bf16)
    u_col = GATE_COLS + 3 * CONV_WIDTH
    me_xy = jnp.reshape(xy, (1,)).astype(jnp.int32)
    proj, x_bf, (win3,) = proj_own(xs, shard_bf["w_in"], me_xy, 4, rides=(("gather2", shard_bf["w_in"]),))
    proj, (wco3, wglu3, wxo3, wkv4, wout4, convw4) = proj_rest(
        x_bf, win3, me_xy, proj, rides=tuple(("gather", shard_bf[n]) for n in MID) + (("gather", conv_w[0]),))
    wkv3 = wkv4.reshape(1, D_MODEL, 2 * XATTN_WIDTH)
    wout3 = wout4.reshape(1, D_MODEL, D_MODEL)
    convw_full = convw4.transpose(1, 0, 2).reshape(3, CONV_WIDTH)
    a_conv = conv_fwd(proj, convw_full)
    y_a = mm_fwd_small(a_conv, wco3, "conv_out", out_dtype=bf16)
    u_perm = _perm(proj[:, u_col:u_col + SSM_WIDTH])
    half = D_MODEL // 2
    (init_r, init_i), (wup_a,) = ssm_scan_fwd(u_perm, bd_r, bd_i, ar, ai,
                                              rides=(("gather", shard_bf["w_up"][:half]),))
    (ysm_perm, ys_perm, st_r, st_i), (wup_b,) = ssm_scan_fwd(
        u_perm, bd_r, bd_i, ar, ai, final=(cd_r, cd_i, d_skip, init_r, init_i),
        rides=(("gather", shard_bf["w_up"][half:]),))
    wup3 = jnp.concatenate([wup_a, wup_b], axis=1)
    y_s = _unperm(ys_perm)
    glu = mm_fwd_small(y_s, wglu3, "glu", out_dtype=bf16)
    kv = mm_fwd_small(mem_bf, wkv3, "kv", out_dtype=bf16)
    o_att = attn_fwd(proj, kv)
    y_c = mm_fwd_small(o_att, wxo3, "xattn_out", out_dtype=bf16)
    merged = merge_fwd(proj, b_gate, y_a, glu, y_c)
    t1, h1, h1_bf = mm_rows_fused(merged, wout3, "w_out_ln1", transposed=False, tm=512, epi=_ln1_fwd_epi,
                                  extras=(xs,), consts=(ln1_g, ln1_b), outs=(f32, f32, bf16))
    (r_up, hdn), (wdn4,) = mm_fwd(h1_bf, wup3, "w_up", bias=b_up, outs=(bf16, bf16), second=_relu2,
                                  first=lambda v: jnp.maximum(v, 0.0), rides=(("gather", shard_bf["w_down"]),))
    wdn3 = wdn4.reshape(1, D_FF, D_MODEL)

    dr2, dr2_bf, loss_cols, d_ln2_g, d_ln2_b, d_b_down = mm_rows_fused(
        hdn, wdn3, "w_down_ln2_loss", transposed=False, tm=512, epi=_ln2_loss_epi, extras=(h1, tgt),
        consts=(b_down, ln2_g, ln2_b), outs=(f32, bf16), nsums=4)
    part, other = {}, {}
    g_w_down = mm_bwd_w(hdn, dr2_bf, 1, "dw_down", tk=1024).reshape(4, -1, D_MODEL)
    (dup, d_b_up), (recv_dn,) = mm_bwd_x(dr2_bf, wdn3, "dup", tm=512, extras=(r_up,), colsum=True, out_dtype=bf16,
                                         epi=lambda acc, r: acc * (2.0 * r.astype(f32)),
                                         rides=(("scatter", g_w_down),))
    part["w_down"] = sum_slots(recv_dn, "sum_w_down")
    g_w_up = mm_bwd_w(h1_bf, dup, 4, "dw_up")
    dr1, dr1_bf, d_ln1_g, d_ln1_b = mm_rows_fused(
        dup, wup3, "dh1_ln1_bwd", transposed=True, tm=256, epi=_ln1_bwd_epi, extras=(xs, t1, dr2), consts=(ln1_g,),
        outs=(f32, bf16), nsums=2)
    g_w_out = mm_bwd_w(merged, dr1_bf, 1, "dw_out").reshape(4, -1, D_MODEL)
    dmerged = mm_bwd_x(dr1_bf, wout3, "dmerged", tm=1024, out_dtype=bf16)
    (dproj, dy_a, dglu, dy_c, d_b_gate), (recv_up,) = merge_bwd(dmerged, proj, b_gate, y_a, glu, y_c,
                                                                rides=(("scatter", g_w_up),))
    part["w_up"] = sum_slots(recv_up, "sum_w_up")

    g_w_co = mm_bwd_w_small(a_conv, dy_a, 4, "dw_conv_out")
    da_conv = mm_bwd_x(dy_a, wco3, "da_conv", tm=1024)
    dproj, d_conv_w = conv_bwd(da_conv, proj, convw_full, dproj)

    g_w_glu = mm_bwd_w_small(y_s, dglu, 4, "dw_glu")
    dys_perm = _perm(mm_bwd_x(dglu, wglu3, "dy_s", tm=1024))
    linit_r, linit_i = ssm_scan_bwd(dys_perm, ysm_perm, cdt_r, cdt_i, ar, ai)
    (du_perm, dbacc_r, dbacc_i, dcacc_r, dcacc_i, da_r, da_i, d_ssm_d), (recv_co, recv_glu, recv_out) = ssm_scan_bwd(
        dys_perm, ysm_perm, cdt_r, cdt_i, ar, ai,
        final=(u_perm, st_r, st_i, bdt_r, bdt_i, d_skip, linit_r, linit_i),
        rides=(("scatter", g_w_co), ("scatter", g_w_glu), ("scatter", g_w_out)))
    part["w_out"] = sum_slots(recv_out, "sum_w_out")
    part["w_conv_out"] = sum_slots(recv_co, "sum_w_conv_out")
    part["w_glu"] = sum_slots(recv_glu, "sum_w_glu")
    dbt_re, dbt_im, d_lr, d_li, d_ldt_state = ssm_param_bwd(
        _diag_b(dbacc_r), _diag_b(dbacc_i), bt_re, bt_im, kr, ki, ar, ai, lr, li, ldt, da_r, da_i)
    d_log_dt = group_sum(d_ldt_state.reshape(SSM_GROUPS, SSM_STATE))
    d_b_re = dbt_re.reshape(SSM_GROUP, SSM_GROUPS, SSM_STATE).transpose(1, 2, 0)
    d_b_im = dbt_im.reshape(SSM_GROUP, SSM_GROUPS, SSM_STATE).transpose(1, 2, 0)
    d_c_re = _diag_c(dcacc_r)
    d_c_im = -_diag_c(dcacc_i)

    g_w_xo = mm_bwd_w_small(o_att, dy_c, 4, "dw_xattn_out")
    do_att = mm_bwd_x(dy_c, wxo3, "do_att", tm=1024, out_dtype=bf16)
    dproj, dkv = attn_bwd(do_att, proj, kv, dproj)
    g_w_kv = mm_bwd_w(mem_bf, dkv, 1, "dw_kv").reshape(4, -1, D_MODEL)

    dproj = lax.dynamic_update_slice(dproj, _unperm(du_perm), (0, u_col))
    g_small = {"b_gate": d_b_gate, "ssm_lam_re": d_lr, "ssm_lam_im": d_li, "ssm_log_dt": d_log_dt, "ssm_b_re": d_b_re,
               "ssm_b_im": d_b_im, "ssm_c_re": d_c_re, "ssm_c_im": d_c_im, "ssm_d": d_ssm_d, "ln1_g": d_ln1_g,
               "ln1_b": d_ln1_b, "b_up": d_b_up, "b_down": d_b_down, "ln2_g": d_ln2_g, "ln2_b": d_ln2_b}
    small_names = SMALL + ("conv_w",)
    g_small["conv_w"] = d_conv_w
    sizes = {n: (g_small[n].size + 1023) // 1024 * 1024 for n in small_names}
    pack = lambda d: jnp.concatenate([_pad_flat(d[n]) for n in small_names]).reshape(-1, 128)
    early = ("w_down", "w_up", "w_out", "w_conv_out", "w_glu")
    g_w_in, landed = mm_bwd_w(
        x_bf, dproj, 4, "dw_in", rides=(("scatter", g_w_xo), ("scatter", g_w_kv), ("all", pack(g_small)))
        + tuple(("pair", part[n]) for n in early))
    recv_xo, recv_kv, srecv = landed[:3]
    other.update(zip(early, landed[3:]))
    part["w_xattn_out"] = sum_slots(recv_xo, "sum_w_xattn_out")
    part["w_kv"] = sum_slots(recv_kv, "sum_w_kv")
    dx, (recv_in, other["w_xattn_out"], other["w_kv"]) = mm_bwd_x(
        dproj, win3, "dx", extras=(dr1,), epi=lambda acc, d: acc + ALPHA * d,
        rides=(("scatter", g_w_in), ("pair", part["w_xattn_out"]), ("pair", part["w_kv"])))
    part["w_in"] = sum_slots(recv_in, "sum_w_in")
    other["w_in"], = exchange(("pair", part["w_in"]), name="swap_w_in")
    res = [{}, {}, {}, {}]
    for n in BIG:
        for k, r in enumerate(adam_pair(part[n], other[n], W[n][0], MOM[n][0], VEL[n][0], "adam_" + n)):
            res[k][n] = r[None]
    conv_zero = jnp.zeros((3, CONV_WIDTH), f32)
    gs, ds_, ms, vs = adam_slots(srecv, pack({**{n: W[n] for n in SMALL}, "conv_w": conv_zero}),
                                 pack({**{n: MOM[n] for n in SMALL}, "conv_w": conv_zero}),
                                 pack({**{n: VEL[n] for n in SMALL}, "conv_w": conv_zero}), "adam_small")

    def unpack_small(buf):
        flat = buf.reshape(-1)
        out, r = {}, 0
        for n in small_names:
            ref = g_small[n] if n == "conv_w" else W[n]
            out[n] = flat[r:r + ref.size].reshape(ref.shape)
            r += sizes[n]
        return out

    res_s = [unpack_small(b) for b in (gs, ds_, ms, vs)]
    g_conv = lax.dynamic_slice(res_s[0]["conv_w"], (0, xy * 128), (3, 128))
    conv_slots = g_conv.reshape(1, 3, 128)
    cg, cd, cm, cv = adam_slots(conv_slots, conv_w[0], m_conv_w[0], v_conv_w[0], "adam_conv")
    conv_res = [cg, cd, cm, cv]

    loss = lax.psum(jnp.sum(loss_cols), ("x", "y", "c"))
    outs = [loss, dx.reshape(x.shape)]
    for k in range(4):
        for n in names:
            if n == "conv_w":
                outs.append(conv_res[k].reshape(conv_w.shape))
            elif n in BIG:
                outs.append(res[k][n])
            else:
                outs.append(res_s[k][n])
    return tuple(outs)
```

```python
import functools
import math

import jax
import jax.numpy as jnp
from jax import lax
from jax.experimental import pallas as pl
from jax.experimental.pallas import tpu as pltpu

f32 = jnp.float32
bf16 = jnp.bfloat16

D_MODEL = 1024
CONV_WIDTH = 512
SSM_WIDTH = 512
SSM_GROUP = 16
SSM_GROUPS = 32
SSM_STATE = 64
N_STATES = SSM_GROUPS * SSM_STATE
XATTN_HEADS = 4
XATTN_HEAD_DIM = 128
XATTN_WIDTH = 512
D_FF = 4096
GATE_COLS = 3 * D_MODEL
IN_COLS = GATE_COLS + 3 * CONV_WIDTH + SSM_WIDTH + XATTN_WIDTH
ALPHA = 2.0 ** 0.25
LN_EPS = 1e-5
ADAM_LR = 0.001
ADAM_B1 = 0.9
ADAM_B2 = 0.999
ADAM_EPS = 1e-08
ADAM_WD = 0.01
ADAM_STEP = 10

N_SEG = 8
SCAN_ROWS = 512
LANE_STRIP = 512
TM = 512
VMEM_LIMIT = 48 * 1024 * 1024
MESH = pl.DeviceIdType.MESH

NT_DIMS = (((1,), (1,)), ((), ()))
TN_DIMS = (((0,), (0,)), ((), ()))


def _params():
    return pltpu.CompilerParams(vmem_limit_bytes=VMEM_LIMIT)


def _full(shape):
    n = len(shape)
    return pl.BlockSpec(shape, lambda *_: (0,) * n)


def _rows(tm, w, cb=0):
    return pl.BlockSpec((tm, w), lambda i: (i, cb))


def _sig(x):
    return 1.0 / (1.0 + jnp.exp(-x))


HBM_SPEC = pl.BlockSpec(memory_space=pl.ANY)
RIDE_PEERS = {"gather": 3, "gather2": 6, "scatter": 3, "all": 7, "pair": 1}


def _xy_peers(x, y):
    return [(1 - x, y), (x, 1 - y), (1 - x, 1 - y)]


def _ride_copies(kind, src, dst, send_sems, recv_sems, local_sem):
    x, y, c = lax.axis_index("x"), lax.axis_index("y"), lax.axis_index("c")
    me = 2 * x + y
    if kind == "all":
        flips = [(fx, fy, fc) for fx in (0, 1) for fy in (0, 1) for fc in (0, 1)][1:]
        peers = [(x ^ fx, y ^ fy, c ^ fc) for fx, fy, fc in flips]
        slot = lambda p: 4 * p[0] + 2 * p[1] + p[2]
        mine = slot((x, y, c))
    else:
        peers = [(px, py, c) for px, py in _xy_peers(x, y)]
        slot = lambda p: 2 * p[0] + p[1]
        mine = me

    def remote(k, s, d):
        return pltpu.make_async_remote_copy(src_ref=s, dst_ref=d, send_sem=send_sems.at[k], recv_sem=recv_sems.at[k],
                                            device_id=peers[k], device_id_type=MESH)

    if kind == "pair":
        peers = [(x, y, 1 - c)]
        return None, [remote(0, src, dst)], [remote(0, src, dst)]
    if kind == "scatter":
        local = pltpu.make_async_copy(src.at[me], dst.at[me], local_sem)
        sends = [remote(k, src.at[slot(p)], dst.at[me]) for k, p in enumerate(peers)]
        lands = [remote(k, src.at[me], dst.at[slot(p)]) for k, p in enumerate(peers)]
    else:
        local = pltpu.make_async_copy(src, dst.at[mine], local_sem)
        sends = [remote(k, src, dst.at[mine]) for k, p in enumerate(peers)]
        lands = [remote(k, src, dst.at[slot(p)]) for k, p in enumerate(peers)]
    return local, sends, lands


def _ride_shape(kind, src):
    lead = {"gather": (4,), "gather2": (4,), "scatter": (), "all": (8,), "pair": ()}[kind]
    return jax.ShapeDtypeStruct(lead + src.shape, src.dtype)


def _two_level(src, dst, send_sems, recv_sems, local_sem):
    x, y, c = lax.axis_index("x"), lax.axis_index("y"), lax.axis_index("c")
    me = 2 * x + y
    h = src.shape[0] // 2
    mine = pl.ds(pl.multiple_of(c * h, 16), h)
    theirs = pl.ds(pl.multiple_of((1 - c) * h, 16), h)
    peers = _xy_peers(x, y)
    slots = [2 * px + py for px, py in peers]

    def over_ici(k, d):
        return pltpu.make_async_remote_copy(src_ref=src.at[mine], dst_ref=d, send_sem=send_sems.at[k],
                                            recv_sem=recv_sems.at[k], device_id=(*peers[k], c), device_id_type=MESH)

    def over_d2d(k, rows):
        blk = dst.at[slots[k], rows]
        return pltpu.make_async_remote_copy(src_ref=blk, dst_ref=blk, send_sem=send_sems.at[3 + k],
                                            recv_sem=recv_sems.at[3 + k], device_id=(x, y, 1 - c), device_id_type=MESH)

    return (pltpu.make_async_copy(src, dst.at[me], local_sem),
            [over_ici(k, dst.at[me, mine]) for k in range(3)], [over_ici(k, dst.at[slots[k], mine]) for k in range(3)],
            [over_d2d(k, mine) for k in range(3)], [over_d2d(k, theirs) for k in range(3)])


def _pcall(body, args, *, name, grid, in_specs, out_specs, out_shape, scratch_shapes=(), rides=(), aliases=None,
           prefetch=()):
    n_in, n_out, n_scr, nr, npf = len(in_specs), len(out_specs), len(scratch_shapes), len(rides), len(prefetch)
    kinds = [k for k, _ in rides]

    def wrapped(*refs):
        pre, refs = refs[:npf], refs[npf:]
        ins, rsrc = refs[:n_in], refs[n_in:n_in + nr]
        outs = refs[n_in + nr:n_in + nr + n_out]
        rdst = refs[n_in + nr + n_out:n_in + 2 * nr + n_out]
        scr = refs[n_in + 2 * nr + n_out:n_in + 2 * nr + n_out + n_scr]
        sems = refs[n_in + 2 * nr + n_out + n_scr:]

        def start():
            for r, kind in enumerate(kinds):
                if kind == "gather2":
                    local, sends, _, _, _ = _two_level(rsrc[r], rdst[r], *sems[3 * r:3 * r + 3])
                else:
                    local, sends, _ = _ride_copies(kind, rsrc[r], rdst[r], *sems[3 * r:3 * r + 3])
                if local is not None:
                    local.start()
                for cp in sends:
                    cp.start()

        def finish():
            for r, kind in enumerate(kinds):
                if kind == "gather2":
                    local, sends, ici_lands, forwards, lands = _two_level(rsrc[r], rdst[r], *sems[3 * r:3 * r + 3])
                    for k in range(3):
                        ici_lands[k].wait_recv()
                        forwards[k].start()
                    sends = sends + forwards
                else:
                    local, sends, lands = _ride_copies(kind, rsrc[r], rdst[r], *sems[3 * r:3 * r + 3])
                for cp in lands:
                    cp.wait_recv()
                for cp in sends:
                    cp.wait_send()
                if local is not None:
                    local.wait()

        if nr and grid:
            ids = [pl.program_id(a) for a in range(len(grid))]
            first = functools.reduce(jnp.logical_and, [i == 0 for i in ids])
            last = functools.reduce(jnp.logical_and, [i == g - 1 for i, g in zip(ids, grid)])
            pl.when(first)(start)
        elif nr:
            start()
        body(*pre, *ins, *outs, *scr)
        if nr and grid:
            pl.when(last)(finish)
        elif nr:
            finish()

    sems = []
    for kind in kinds:
        n = RIDE_PEERS[kind]
        sems += [pltpu.SemaphoreType.DMA((n,)), pltpu.SemaphoreType.DMA((n,)), pltpu.SemaphoreType.DMA(())]
    all_in = list(in_specs) + [HBM_SPEC] * nr
    all_out = list(out_specs) + [HBM_SPEC] * nr
    all_scratch = list(scratch_shapes) + sems
    if npf:
        how = dict(grid_spec=pltpu.PrefetchScalarGridSpec(num_scalar_prefetch=npf, grid=grid, in_specs=all_in,
                                                          out_specs=all_out, scratch_shapes=all_scratch))
    else:
        how = dict(grid=grid, in_specs=all_in, out_specs=all_out, scratch_shapes=all_scratch)
    res = pl.pallas_call(
        wrapped, name=name, out_shape=list(out_shape) + [_ride_shape(k, s) for k, s in rides],
        input_output_aliases=aliases or {}, compiler_params=_params(), **how)(
            *prefetch, *args, *[s for _, s in rides])
    return list(res[:n_out]), list(res[n_out:])


def _bf(v):
    return v if v.dtype == bf16 else v.astype(bf16)


def mm_fwd_small(a, w3, name, *, tm=1024, out_dtype=f32):
    M, K = a.shape
    J, _, n = w3.shape
    tm = min(tm, M)

    def body(a_ref, w_ref, o_ref):
        av = _bf(a_ref[...])
        for j in range(J):
            o_ref[:, j * n:(j + 1) * n] = jnp.dot(av, w_ref[j], preferred_element_type=f32).astype(out_dtype)

    return pl.pallas_call(
        body, name=name, grid=(M // tm,), in_specs=[_rows(tm, K), _full((J, K, n))], out_specs=_rows(tm, J * n),
        out_shape=jax.ShapeDtypeStruct((M, J * n), out_dtype), compiler_params=_params())(a, w3)


def mm_bwd_w_small(a, dy, J, name, *, tm=2048):
    M, K = a.shape
    n = dy.shape[1] // J
    tm = min(tm, M)
    ns = M // tm

    def body(a_ref, dy_ref, o_ref, acc_ref):
        s = pl.program_id(0)
        part = lax.dot_general(_bf(a_ref[...]), _bf(dy_ref[...]), TN_DIMS, preferred_element_type=f32)

        @pl.when(s == 0)
        def _():
            acc_ref[...] = part

        @pl.when(s > 0)
        def _():
            acc_ref[...] += part

        @pl.when(s == ns - 1)
        def _():
            for j in range(J):
                o_ref[j] = acc_ref[:, j * n:(j + 1) * n].astype(bf16)

    return pl.pallas_call(
        body, name=name, grid=(ns,), in_specs=[_rows(tm, K), _rows(tm, J * n)], out_specs=_full((J, K, n)),
        out_shape=jax.ShapeDtypeStruct((J, K, n), bf16), scratch_shapes=[pltpu.VMEM((K, J * n), f32)],
        compiler_params=_params())(a, dy)


def mm_fwd(a, w3, name, *, tm=1024, tn=None, bias=None, outs=(f32,), first=None, second=None, emit_a=False,
           rides=()):
    M, K = a.shape
    J, _, n = w3.shape
    tm = min(tm, M)
    tn = tn or n
    nl = n // tn
    nb = 0 if bias is None else 1

    def body(*refs):
        a_ref, w_ref = refs[0], refs[1]
        av = _bf(a_ref[...])
        acc = jnp.dot(av, w_ref[0], preferred_element_type=f32)
        if bias is not None:
            acc = acc + refs[2][...]
        refs[2 + nb][...] = (acc if first is None else first(acc)).astype(outs[0])
        if len(outs) > 1:
            refs[3 + nb][...] = second(acc).astype(outs[1])
        if emit_a:
            @pl.when(jnp.logical_and(pl.program_id(0) == 0, pl.program_id(1) == 0))
            def _():
                refs[2 + nb + len(outs)][...] = av

    in_specs = [pl.BlockSpec((tm, K), lambda j, l, i: (i, 0)),
                pl.BlockSpec((1, K, tn), lambda j, l, i: (j, 0, l))]
    args = [a, w3]
    if bias is not None:
        in_specs.append(pl.BlockSpec((1, tn), lambda j, l, i: (0, j * nl + l)))
        args.append(bias)
    out_specs = [pl.BlockSpec((tm, tn), lambda j, l, i: (i, j * nl + l)) for _ in outs]
    out_shape = [jax.ShapeDtypeStruct((M, J * n), dt) for dt in outs]
    if emit_a:
        last = M // tm - 1
        out_specs.append(pl.BlockSpec((tm, K), lambda j, l, i: (jnp.where((j == 0) & (l == 0), i, last), 0)))
        out_shape.append(jax.ShapeDtypeStruct((M, K), bf16))
    res, landed = _pcall(body, args, name=name, grid=(J, nl, M // tm), in_specs=in_specs, out_specs=out_specs,
                         out_shape=out_shape, rides=rides)
    res = res if len(res) > 1 else res[0]
    return (res, landed) if rides else res


def proj_own(x, w_own, me, n_blocks, *, tm=1024, rides=()):
    M, K = x.shape
    n = w_own.shape[1]

    def body(me_ref, x_ref, w_ref, o_ref, xb_ref):
        av = _bf(x_ref[...])
        xb_ref[...] = av
        o_ref[...] = jnp.dot(av, w_ref[...], preferred_element_type=f32).astype(bf16)

    (proj, x_bf), landed = _pcall(
        body, [x, w_own], name="proj_own", grid=(M // tm,), prefetch=(me,),
        in_specs=[pl.BlockSpec((tm, K), lambda i, me_ref: (i, 0)), pl.BlockSpec((K, n), lambda i, me_ref: (0, 0))],
        out_specs=[pl.BlockSpec((tm, n), lambda i, me_ref: (i, me_ref[0])),
                   pl.BlockSpec((tm, K), lambda i, me_ref: (i, 0))],
        out_shape=[jax.ShapeDtypeStruct((M, n_blocks * n), bf16), jax.ShapeDtypeStruct((M, K), bf16)], rides=rides)
    return proj, x_bf, landed


def proj_rest(x_bf, w3, me, proj, *, tm=1024, rides=()):
    M, K = x_bf.shape
    J, _, n = w3.shape

    def body(me_ref, x_ref, w_ref, _, o_ref):
        o_ref[...] = jnp.dot(x_ref[...], w_ref[0], preferred_element_type=f32).astype(bf16)

    (proj,), landed = _pcall(
        body, [x_bf, w3, proj], name="proj_rest", grid=(J - 1, M // tm), prefetch=(me,),
        in_specs=[pl.BlockSpec((tm, K), lambda k, i, me_ref: (i, 0)),
                  pl.BlockSpec((1, K, n), lambda k, i, me_ref: (me_ref[0] ^ (k + 1), 0, 0)), HBM_SPEC],
        out_specs=[pl.BlockSpec((tm, n), lambda k, i, me_ref: (i, me_ref[0] ^ (k + 1)))],
        out_shape=[jax.ShapeDtypeStruct(proj.shape, bf16)], aliases={3: 0}, rides=rides)
    return proj, landed


def mm_bwd_x(dy, w3, name, *, tm=TM, epi=None, extras=(), colsum=False, out_dtype=f32, rides=()):
    M = dy.shape[0]
    J, K, n = w3.shape
    tm = min(tm, M)
    nex = len(extras)

    def body(*refs):
        dy_ref, w_hbm = refs[0], refs[1]
        ex = refs[2:2 + nex]
        o_ref = refs[2 + nex]
        w_ref = refs[-1]
        i = pl.program_id(0)

        @pl.when(i == 0)
        def _():
            pltpu.sync_copy(w_hbm, w_ref)

        acc = None
        for j in range(J):
            part = lax.dot_general(_bf(dy_ref[:, j * n:(j + 1) * n]), w_ref[j], NT_DIMS, preferred_element_type=f32)
            acc = part if acc is None else acc + part
        if epi is not None:
            acc = epi(acc, *[e[...] for e in ex])
        o_ref[...] = acc.astype(out_dtype)
        if colsum:
            s_ref = refs[3 + nex]
            cs = jnp.sum(acc, axis=0, keepdims=True)

            @pl.when(i == 0)
            def _():
                s_ref[...] = cs

            @pl.when(i > 0)
            def _():
                s_ref[...] += cs

    in_specs = [pl.BlockSpec((tm, J * n), lambda i: (i, 0)), pl.BlockSpec(memory_space=pl.ANY)]
    in_specs += [pl.BlockSpec((tm, K), lambda i: (i, 0)) for _ in extras]
    out_specs = [pl.BlockSpec((tm, K), lambda i: (i, 0))]
    out_shape = [jax.ShapeDtypeStruct((M, K), out_dtype)]
    if colsum:
        out_specs.append(pl.BlockSpec((1, K), lambda i: (0, 0)))
        out_shape.append(jax.ShapeDtypeStruct((1, K), f32))
    res, landed = _pcall(body, [dy, w3, *extras], name=name, grid=(M // tm,), in_specs=in_specs, out_specs=out_specs,
                         out_shape=out_shape, scratch_shapes=[pltpu.VMEM((J, K, n), bf16)], rides=rides)
    res = res if colsum else res[0]
    return (res, landed) if rides else res


def mm_rows_fused(a, w3, name, *, transposed, tm, epi, extras=(), consts=(), outs=(f32,), nsums=0, sum_widths=None,
                  rides=()):
    M = a.shape[0]
    J, K, n = w3.shape
    W = K if transposed else n
    tm = min(tm, M)
    nex, nco, nout = len(extras), len(consts), len(outs)
    nsums = len(sum_widths) if sum_widths is not None else nsums
    out_dtypes = [o[0] if isinstance(o, tuple) else o for o in outs]

    def body(*refs):
        a_ref, w_hbm = refs[0], refs[1]
        ex = refs[2:2 + nex]
        co = refs[2 + nex:2 + nex + nco]
        o_refs = refs[2 + nex + nco:2 + nex + nco + nout]
        s_refs = refs[2 + nex + nco + nout:2 + nex + nco + nout + nsums]
        w_ref = refs[-1]
        i = pl.program_id(0)

        @pl.when(i == 0)
        def _():
            pltpu.sync_copy(w_hbm, w_ref)

        if transposed:
            acc = None
            for j in range(J):
                part = lax.dot_general(_bf(a_ref[:, j * n:(j + 1) * n]), w_ref[j], NT_DIMS,
                                       preferred_element_type=f32)
                acc = part if acc is None else acc + part
        else:
            acc = jnp.dot(_bf(a_ref[...]), w_ref[0], preferred_element_type=f32)
        rows, sums = epi(acc, *[e[...] for e in ex], *[c[...] for c in co])
        for o_ref, v, dt in zip(o_refs, rows, out_dtypes):
            o_ref[...] = v.astype(dt)
        for s_ref, v in zip(s_refs, sums):
            @pl.when(i == 0)
            def _(s_ref=s_ref, v=v):
                s_ref[...] = v

            @pl.when(i > 0)
            def _(s_ref=s_ref, v=v):
                s_ref[...] += v

    extras = [e if isinstance(e, tuple) else (e, W, 0) for e in extras]
    outs_full = [o if isinstance(o, tuple) else (o, W, W) for o in outs]
    widths = list(sum_widths) if sum_widths is not None else [W] * nsums
    in_specs = [_rows(tm, a.shape[1]), pl.BlockSpec(memory_space=pl.ANY)]
    in_specs += [_rows(tm, w, cb) for _, w, cb in extras] + [_full(c.shape) for c in consts]
    out_specs = [_rows(tm, w) for _, w, _ in outs_full] + [_full((1, w)) for w in widths]
    out_shape = ([jax.ShapeDtypeStruct((M, cols), dt) for dt, _, cols in outs_full]
                 + [jax.ShapeDtypeStruct((1, w), f32) for w in widths])
    res, landed = _pcall(body, [a, w3, *[e for e, _, _ in extras], *consts], name=name, grid=(M // tm,),
                         in_specs=in_specs, out_specs=out_specs, out_shape=out_shape,
                         scratch_shapes=[pltpu.VMEM((J, K, n), bf16)], rides=rides)
    return (res, landed) if rides else res


def mm_bwd_w(a, dy, J, name, *, tm=2048, tn=None, tk=None, rides=()):
    M, K = a.shape
    n = dy.shape[1] // J
    tm = min(tm, M)
    tn = tn or n
    tk = tk or K
    nl = n // tn
    nk = K // tk
    ns = M // tm

    def body(a_ref, dy_ref, o_ref, acc_ref):
        s = pl.program_id(3)
        part = lax.dot_general(_bf(a_ref[...]), _bf(dy_ref[...]), TN_DIMS, preferred_element_type=f32)

        @pl.when(s == 0)
        def _():
            acc_ref[...] = part

        @pl.when(s > 0)
        def _():
            acc_ref[...] += part

        @pl.when(s == ns - 1)
        def _():
            o_ref[0] = acc_ref[...].astype(bf16)

    res, landed = _pcall(
        body, [a, dy], name=name, grid=(J, nl, nk, ns),
        in_specs=[pl.BlockSpec((tm, tk), lambda j, l, k, s: (s, k)),
                  pl.BlockSpec((tm, tn), lambda j, l, k, s: (s, j * nl + l))],
        out_specs=[pl.BlockSpec((1, tk, tn), lambda j, l, k, s: (j, k, l))],
        out_shape=[jax.ShapeDtypeStruct((J, K, n), bf16)],
        scratch_shapes=[pltpu.VMEM((tk, tn), f32)], rides=rides)
    return (res[0], landed) if rides else res[0]


def _relu2(v):
    r = jnp.maximum(v, 0.0)
    return r * r


HALO = 16


def _shift_down(z, k, halo):
    r = lax.broadcasted_iota(jnp.int32, z.shape, 0)
    y = pltpu.roll(z, k, 0)
    for q in range(k):
        y = jnp.where(r == q, halo[HALO - k + q:HALO - k + q + 1, :], y)
    return y


def _shift_up(z, k, halo):
    tm = z.shape[0]
    r = lax.broadcasted_iota(jnp.int32, z.shape, 0)
    y = pltpu.roll(z, tm - k, 0)
    for q in range(k):
        y = jnp.where(r == tm - k + q, halo[q:q + 1, :], y)
    return y


def _prev_halo(tm, cb):
    return pl.BlockSpec((HALO, CONV_WIDTH), lambda i: (jnp.maximum(i * (tm // HALO) - 1, 0), cb))


def _next_halo(tm, cb, nblk):
    return pl.BlockSpec((HALO, CONV_WIDTH), lambda i: (jnp.minimum((i + 1) * (tm // HALO), nblk - 1), cb))


def _f32(ref):
    return ref[...].astype(f32)


def conv_fwd(proj, conv_w, tm=TM):
    S = proj.shape[0]

    def body(cb_ref, cc_ref, ch_ref, cch_ref, chh_ref, w_ref, a_ref):
        i = pl.program_id(0)
        z = _f32(cc_ref) * _f32(ch_ref)
        zh = jnp.where(i == 0, 0.0, _f32(cch_ref) * _f32(chh_ref))
        w = w_ref[...]
        dwz = w[0:1, :] * _shift_down(z, 2, zh) + w[1:2, :] * _shift_down(z, 1, zh) + w[2:3, :] * z
        a_ref[...] = (_f32(cb_ref) * dwz).astype(bf16)

    return pl.pallas_call(
        body, name="conv_fwd", grid=(S // tm,),
        in_specs=[_rows(tm, CONV_WIDTH, 6), _rows(tm, CONV_WIDTH, 7), _rows(tm, CONV_WIDTH, 8),
                  _prev_halo(tm, 7), _prev_halo(tm, 8), _full((3, CONV_WIDTH))],
        out_specs=_rows(tm, CONV_WIDTH),
        out_shape=jax.ShapeDtypeStruct((S, CONV_WIDTH), bf16), compiler_params=_params())(
            proj, proj, proj, proj, proj, conv_w)


def conv_bwd(da, proj, conv_w, dproj, tm=TM):
    S = proj.shape[0]
    nt = S // tm

    def body(da_ref, cb_ref, cc_ref, ch_ref, cch_ref, chh_ref, dan_ref, cbn_ref, w_ref, _, o_ref, dw_ref):
        i = pl.program_id(0)
        cc, ch, cb, dav = _f32(cc_ref), _f32(ch_ref), _f32(cb_ref), da_ref[...]
        z = cc * ch
        zh = jnp.where(i == 0, 0.0, _f32(cch_ref) * _f32(chh_ref))
        w = w_ref[...]
        z1 = _shift_down(z, 1, zh)
        z2 = _shift_down(z, 2, zh)
        dwz = w[0:1, :] * z2 + w[1:2, :] * z1 + w[2:3, :] * z
        g = dav * cb
        gn = jnp.where(i == nt - 1, 0.0, dan_ref[...] * _f32(cbn_ref))
        dz = w[2:3, :] * g + w[1:2, :] * _shift_up(g, 1, gn) + w[0:1, :] * _shift_up(g, 2, gn)
        o_ref[:, 0:CONV_WIDTH] = (dav * dwz).astype(bf16)
        o_ref[:, CONV_WIDTH:2 * CONV_WIDTH] = (dz * ch).astype(bf16)
        o_ref[:, 2 * CONV_WIDTH:3 * CONV_WIDTH] = (dz * cc).astype(bf16)

        @pl.when(i == 0)
        def _():
            dw_ref[...] = jnp.zeros_like(dw_ref)

        dw_ref[0:1, :] += jnp.sum(g * z2, axis=0, keepdims=True)
        dw_ref[1:2, :] += jnp.sum(g * z1, axis=0, keepdims=True)
        dw_ref[2:3, :] += jnp.sum(g * z, axis=0, keepdims=True)

    return pl.pallas_call(
        body, name="conv_bwd", grid=(nt,),
        in_specs=[_rows(tm, CONV_WIDTH), _rows(tm, CONV_WIDTH, 6), _rows(tm, CONV_WIDTH, 7), _rows(tm, CONV_WIDTH, 8),
                  _prev_halo(tm, 7), _prev_halo(tm, 8),
                  _next_halo(tm, 0, S // HALO), _next_halo(tm, 6, S // HALO), _full((3, CONV_WIDTH)),
                  pl.BlockSpec(memory_space=pl.ANY)],
        out_specs=[_rows(tm, 3 * CONV_WIDTH, GATE_COLS // (3 * CONV_WIDTH)), _full((3, CONV_WIDTH))],
        out_shape=[jax.ShapeDtypeStruct(dproj.shape, bf16), jax.ShapeDtypeStruct((3, CONV_WIDTH), f32)],
        input_output_aliases={9: 0},
        compiler_params=_params())(da, proj, proj, proj, proj, proj, da, proj, conv_w, dproj)


def _cmul(ar, ai, br, bi):
    return ar * br - ai * bi, ar * bi + ai * br


def ssm_prep(lr, li, ldt, bt_re, bt_im):
    def body(lr_ref, li_ref, ldt_ref, br_ref, bi_ref, ar_ref, ai_ref, kr_ref, ki_ref, bbr_ref, bbi_ref):
        lrv, liv = lr_ref[...], li_ref[...]
        dt = jnp.exp(ldt_ref[...])
        mag = jnp.exp(lrv * dt)
        ar = mag * jnp.cos(liv * dt)
        ai = mag * jnp.sin(liv * dt)
        den = lrv * lrv + liv * liv
        nr = ar - 1.0
        kr = (nr * lrv + ai * liv) / den
        ki = (ai * lrv - nr * liv) / den
        ar_ref[...], ai_ref[...], kr_ref[...], ki_ref[...] = ar, ai, kr, ki
        bbr_ref[...] = kr * br_ref[...] - ki * bi_ref[...]
        bbi_ref[...] = kr * bi_ref[...] + ki * br_ref[...]

    v = jax.ShapeDtypeStruct((1, N_STATES), f32)
    m = jax.ShapeDtypeStruct((SSM_GROUP, N_STATES), f32)
    return pl.pallas_call(body, name="ssm_prep", out_shape=[v, v, v, v, m, m])(lr, li, ldt, bt_re, bt_im)


def _pow_segment(ar, ai, seg_len):
    pr, pi = ar, ai
    for _ in range(int(math.log2(seg_len))):
        pr, pi = _cmul(pr, pi, pr, pi)
    return pr, pi


GELU_K = math.sqrt(2.0 / math.pi)
GELU_C = 0.044715


def _gelu(v):
    return 0.5 * v * (1.0 + jnp.tanh(GELU_K * (v + GELU_C * v * v * v)))


def _gelu_grad(v):
    t = jnp.tanh(GELU_K * (v + GELU_C * v * v * v))
    return 0.5 * (1.0 + t) + 0.5 * v * (1.0 - t * t) * GELU_K * (1.0 + 3.0 * GELU_C * v * v)


def ssm_scan_fwd(u_perm, bd_r, bd_i, ar, ai, *, final=None, rides=()):
    S = u_perm.shape[0]
    R = SCAN_ROWS
    nblk = S // R
    seg_len = S // N_SEG
    nstrip = N_STATES // LANE_STRIP
    store = final is not None

    def body(*refs):
        if store:
            (u_ref, bdr_ref, bdi_ref, ar_ref, ai_ref, cdr_ref, cdi_ref, d_ref, ir_ref, ii_ref,
             y_ref, ys_ref, sr_ref, si_ref, bur, bui, car, cai) = refs
        else:
            u_ref, bdr_ref, bdi_ref, ar_ref, ai_ref, or_ref, oi_ref, bur, bui, car, cai = refs
        i = pl.program_id(0)
        ub = _bf(u_ref[...])
        u = ub.astype(f32)
        for cb in range(4):
            us = ub[:, cb * 128:(cb + 1) * 128]
            bur[:, cb * 512:(cb + 1) * 512] = jnp.dot(us, bdr_ref[cb], preferred_element_type=f32)
            bui[:, cb * 512:(cb + 1) * 512] = jnp.dot(us, bdi_ref[cb], preferred_element_type=f32)

        @pl.when(i == 0)
        def _():
            if store:
                car[...] = ir_ref[...]
                cai[...] = ii_ref[...]
            else:
                car[...] = jnp.zeros_like(car)
                cai[...] = jnp.zeros_like(cai)

        for ls in range(nstrip):
            lanes = pl.ds(ls * LANE_STRIP, LANE_STRIP)
            a_r = jnp.broadcast_to(ar_ref[:, lanes], (N_SEG, LANE_STRIP))
            a_i = jnp.broadcast_to(ai_ref[:, lanes], (N_SEG, LANE_STRIP))

            def step(t, carry, lanes=lanes, a_r=a_r, a_i=a_i):
                s_r, s_i = carry
                row = pl.multiple_of(t * 8, 8)
                n_r = a_r * s_r - a_i * s_i + bur[pl.ds(row, 8), lanes]
                n_i = a_r * s_i + a_i * s_r + bui[pl.ds(row, 8), lanes]
                if store:
                    sr_ref[pl.ds(row, 8), lanes] = n_r
                    si_ref[pl.ds(row, 8), lanes] = n_i
                return n_r, n_i

            e_r, e_i = lax.fori_loop(0, R // 8, step, (car[:, lanes], cai[:, lanes]), unroll=True)
            car[:, lanes] = e_r
            cai[:, lanes] = e_i

        if store:
            for cb in range(4):
                st_r = sr_ref[:, cb * 512:(cb + 1) * 512].astype(bf16)
                st_i = si_ref[:, cb * 512:(cb + 1) * 512].astype(bf16)
                y = (jnp.dot(st_r, cdr_ref[cb], preferred_element_type=f32)
                     - jnp.dot(st_i, cdi_ref[cb], preferred_element_type=f32))
                cols = slice(cb * 128, (cb + 1) * 128)
                y = y + d_ref[:, cols] * u[:, cols]
                y_ref[:, cols] = y
                ys_ref[:, cols] = _gelu(y).astype(bf16)
        else:
            @pl.when(i == nblk - 1)
            def _():
                p_r, p_i = _pow_segment(ar_ref[...], ai_ref[...], seg_len)
                t_r, t_i = car[0:1, :], cai[0:1, :]
                or_ref[0:1, :] = jnp.zeros((1, N_STATES), f32)
                oi_ref[0:1, :] = jnp.zeros((1, N_STATES), f32)
                for j in range(1, N_SEG):
                    or_ref[j:j + 1, :] = t_r
                    oi_ref[j:j + 1, :] = t_i
                    m_r, m_i = _cmul(p_r, p_i, t_r, t_i)
                    t_r, t_i = car[j:j + 1, :] + m_r, cai[j:j + 1, :] + m_i

    blk = lambda w: pl.BlockSpec((R, w), lambda i: (i, 0))
    in_specs = [blk(SSM_WIDTH), _full((4, 128, 512)), _full((4, 128, 512)), _full((1, N_STATES)), _full((1, N_STATES))]
    args = [u_perm, bd_r, bd_i, ar, ai]
    scratch = [pltpu.VMEM((R, N_STATES), f32), pltpu.VMEM((R, N_STATES), f32),
               pltpu.VMEM((N_SEG, N_STATES), f32), pltpu.VMEM((N_SEG, N_STATES), f32)]
    if store:
        in_specs += [_full((4, 512, 128)), _full((4, 512, 128)), _full((1, SSM_WIDTH)),
                     _full((N_SEG, N_STATES)), _full((N_SEG, N_STATES))]
        args += list(final)
        out_specs = [blk(SSM_WIDTH), blk(SSM_WIDTH), blk(N_STATES), blk(N_STATES)]
        out_shape = [jax.ShapeDtypeStruct((S, SSM_WIDTH), f32), jax.ShapeDtypeStruct((S, SSM_WIDTH), bf16),
                     jax.ShapeDtypeStruct((S, N_STATES), f32), jax.ShapeDtypeStruct((S, N_STATES), f32)]
        name = "ssm_scan_fwd"
    else:
        out_specs = [_full((N_SEG, N_STATES)), _full((N_SEG, N_STATES))]
        out_shape = [jax.ShapeDtypeStruct((N_SEG, N_STATES), f32)] * 2
        name = "ssm_scan_fwd_carry"
    res, landed = _pcall(body, args, name=name, grid=(nblk,), in_specs=in_specs, out_specs=out_specs,
                         out_shape=out_shape, scratch_shapes=scratch, rides=rides)
    return (res, landed) if rides else res


def ssm_scan_bwd(dys_perm, y_perm, cdt_r, cdt_i, ar, ai, *, final=None, rides=()):
    S = dys_perm.shape[0]
    R = SCAN_ROWS
    nblk = S // R
    seg_len = S // N_SEG
    store = final is not None
    strip = LANE_STRIP // 2 if store else LANE_STRIP
    nstrip = N_STATES // strip

    def body(*refs):
        if store:
            (dys_ref, y_ref, cdr_ref, cdi_ref, ar_ref, ai_ref, u_ref, sr_ref, si_ref, bdr_ref, bdi_ref, d_ref, ir_ref,
             ii_ref, du_ref, dbr_ref, dbi_ref, dcr_ref, dci_ref, dar_ref, dai_ref, dd_ref, dsr, dsi, lmr, lmi, car,
             cai) = refs
        else:
            dys_ref, y_ref, cdr_ref, cdi_ref, ar_ref, ai_ref, or_ref, oi_ref, dsr, dsi, car, cai = refs
        i = pl.program_id(0)
        dy = dys_ref[...] * _gelu_grad(y_ref[...])
        dyb = dy.astype(bf16)
        for cb in range(4):
            ds_ = dyb[:, cb * 128:(cb + 1) * 128]
            dsr[:, cb * 512:(cb + 1) * 512] = jnp.dot(ds_, cdr_ref[cb], preferred_element_type=f32)
            dsi[:, cb * 512:(cb + 1) * 512] = -jnp.dot(ds_, cdi_ref[cb], preferred_element_type=f32)

        @pl.when(i == 0)
        def _():
            if store:
                car[...] = ir_ref[...]
                cai[...] = ii_ref[...]
                dar_ref[...] = jnp.zeros_like(dar_ref)
                dai_ref[...] = jnp.zeros_like(dai_ref)
                dbr_ref[...] = jnp.zeros_like(dbr_ref)
                dbi_ref[...] = jnp.zeros_like(dbi_ref)
                dcr_ref[...] = jnp.zeros_like(dcr_ref)
                dci_ref[...] = jnp.zeros_like(dci_ref)
                dd_ref[...] = jnp.zeros_like(dd_ref)
            else:
                car[...] = jnp.zeros_like(car)
                cai[...] = jnp.zeros_like(cai)

        for ls in range(nstrip):
            lanes = pl.ds(ls * strip, strip)
            a_r = jnp.broadcast_to(ar_ref[:, lanes], (N_SEG, strip))
            a_i = jnp.broadcast_to(ai_ref[:, lanes], (N_SEG, strip))
            zero = jnp.zeros((N_SEG, strip), f32)

            def step(k, carry, lanes=lanes, a_r=a_r, a_i=a_i):
                l_r, l_i, g_r, g_i = carry
                row = pl.multiple_of((R // 8 - 1 - k) * 8, 8)
                if store:
                    s_r = sr_ref[pl.ds(row, 8), lanes]
                    s_i = si_ref[pl.ds(row, 8), lanes]
                    g_r = g_r + l_r * s_r + l_i * s_i
                    g_i = g_i + l_i * s_r - l_r * s_i
                n_r = dsr[pl.ds(row, 8), lanes] + a_r * l_r + a_i * l_i
                n_i = dsi[pl.ds(row, 8), lanes] + a_r * l_i - a_i * l_r
                if store:
                    lmr[pl.ds(row, 8), lanes] = n_r
                    lmi[pl.ds(row, 8), lanes] = n_i
                return n_r, n_i, g_r, g_i

            e_r, e_i, g_r, g_i = lax.fori_loop(0, R // 8, step, (car[:, lanes], cai[:, lanes], zero, zero),
                                               unroll=True)
            car[:, lanes] = e_r
            cai[:, lanes] = e_i
            if store:
                dar_ref[:, lanes] += g_r
                dai_ref[:, lanes] += g_i

        if store:
            ub = _bf(u_ref[...])
            u = ub.astype(f32)
            for cb in range(4):
                cols = slice(cb * 128, (cb + 1) * 128)
                st = slice(cb * 512, (cb + 1) * 512)
                l_r = lmr[:, st].astype(bf16)
                l_i = lmi[:, st].astype(bf16)
                du = (jnp.dot(l_r, bdr_ref[cb], preferred_element_type=f32)
                      + jnp.dot(l_i, bdi_ref[cb], preferred_element_type=f32))
                du_ref[:, cols] = (du + d_ref[:, cols] * dy[:, cols]).astype(bf16)
                dbr_ref[cb] += lax.dot_general(l_r, ub[:, cols], TN_DIMS, preferred_element_type=f32)
                dbi_ref[cb] += lax.dot_general(l_i, ub[:, cols], TN_DIMS, preferred_element_type=f32)
                dcr_ref[cb] += lax.dot_general(dyb[:, cols], sr_ref[:, st].astype(bf16), TN_DIMS,
                                               preferred_element_type=f32)
                dci_ref[cb] += lax.dot_general(dyb[:, cols], si_ref[:, st].astype(bf16), TN_DIMS,
                                               preferred_element_type=f32)
            dd_ref[...] += jnp.sum(dy * u, axis=0, keepdims=True)
        else:
            @pl.when(i == nblk - 1)
            def _():
                p_r, p_i = _pow_segment(ar_ref[...], ai_ref[...], seg_len)
                p_i = -p_i
                t_r, t_i = car[N_SEG - 1:N_SEG, :], cai[N_SEG - 1:N_SEG, :]
                or_ref[N_SEG - 1:N_SEG, :] = jnp.zeros((1, N_STATES), f32)
                oi_ref[N_SEG - 1:N_SEG, :] = jnp.zeros((1, N_STATES), f32)
                for j in range(N_SEG - 2, -1, -1):
                    or_ref[j:j + 1, :] = t_r
                    oi_ref[j:j + 1, :] = t_i
                    m_r, m_i = _cmul(p_r, p_i, t_r, t_i)
                    t_r, t_i = car[j:j + 1, :] + m_r, cai[j:j + 1, :] + m_i

    blk = lambda w: pl.BlockSpec((R, w), lambda i: (nblk - 1 - i, 0))
    in_specs = [blk(SSM_WIDTH), blk(SSM_WIDTH), _full((4, 128, 512)), _full((4, 128, 512)), _full((1, N_STATES)),
                _full((1, N_STATES))]
    args = [dys_perm, y_perm, cdt_r, cdt_i, ar, ai]
    seg = jax.ShapeDtypeStruct((N_SEG, N_STATES), f32)
    if store:
        in_specs += [blk(SSM_WIDTH), blk(N_STATES), blk(N_STATES), _full((4, 512, 128)), _full((4, 512, 128)),
                     _full((1, SSM_WIDTH)), _full((N_SEG, N_STATES)), _full((N_SEG, N_STATES))]
        args += list(final)
        out_specs = [blk(SSM_WIDTH), _full((4, 512, 128)), _full((4, 512, 128)), _full((4, 128, 512)),
                     _full((4, 128, 512)), _full((N_SEG, N_STATES)), _full((N_SEG, N_STATES)), _full((1, SSM_WIDTH))]
        b_acc = jax.ShapeDtypeStruct((4, 512, 128), f32)
        c_acc = jax.ShapeDtypeStruct((4, 128, 512), f32)
        out_shape = [jax.ShapeDtypeStruct((S, SSM_WIDTH), bf16), b_acc, b_acc, c_acc, c_acc, seg, seg,
                     jax.ShapeDtypeStruct((1, SSM_WIDTH), f32)]
        scratch = [pltpu.VMEM((R, N_STATES), f32)] * 4 + [pltpu.VMEM((N_SEG, N_STATES), f32)] * 2
        name = "ssm_scan_bwd"
    else:
        out_specs = [_full((N_SEG, N_STATES)), _full((N_SEG, N_STATES))]
        out_shape = [seg, seg]
        scratch = [pltpu.VMEM((R, N_STATES), f32)] * 2 + [pltpu.VMEM((N_SEG, N_STATES), f32)] * 2
        name = "ssm_scan_bwd_carry"
    res, landed = _pcall(body, args, name=name, grid=(nblk,), in_specs=in_specs, out_specs=out_specs,
                         out_shape=out_shape, scratch_shapes=scratch, rides=rides)
    return (res, landed) if rides else res


def ssm_param_bwd(dbb_r, dbb_i, bt_re, bt_im, kr, ki, ar, ai, lr, li, ldt, da_r, da_i):
    def body(dbr_ref, dbi_ref, br_ref, bi_ref, kr_ref, ki_ref, ar_ref, ai_ref, lr_ref, li_ref, ldt_ref, dar_ref,
             dai_ref, obr_ref, obi_ref, olr_ref, oli_ref, odt_ref):
        dbr, dbi, b_r, b_i = dbr_ref[...], dbi_ref[...], br_ref[...], bi_ref[...]
        k_r, k_i, a_r, a_i = kr_ref[...], ki_ref[...], ar_ref[...], ai_ref[...]
        l_r, l_i = lr_ref[...], li_ref[...]
        dt = jnp.exp(ldt_ref[...])
        obr_ref[...] = k_r * dbr + k_i * dbi
        obi_ref[...] = k_r * dbi - k_i * dbr
        gk_r = jnp.sum(dbr * b_r + dbi * b_i, axis=0, keepdims=True)
        gk_i = jnp.sum(dbi * b_r - dbr * b_i, axis=0, keepdims=True)
        ga_r = jnp.sum(dar_ref[...], axis=0, keepdims=True)
        ga_i = jnp.sum(dai_ref[...], axis=0, keepdims=True)
        den = l_r * l_r + l_i * l_i
        c_r, c_i = l_r / den, l_i / den
        m_r, m_i = _cmul(c_r, c_i, gk_r, gk_i)
        g_r, g_i = ga_r + m_r, ga_i + m_i
        t1_r, t1_i = _cmul(dt * a_r, -dt * a_i, g_r, g_i)
        q_r, q_i = _cmul(k_r, k_i, c_r, -c_i)
        t2_r, t2_i = _cmul(-q_r, q_i, gk_r, gk_i)
        olr_ref[...] = t1_r + t2_r
        oli_ref[...] = t1_i + t2_i
        w_r, w_i = _cmul(l_r, l_i, a_r, a_i)
        odt_ref[...] = dt * (w_r * g_r + w_i * g_i)

    v = jax.ShapeDtypeStruct((1, N_STATES), f32)
    m = jax.ShapeDtypeStruct((SSM_GROUP, N_STATES), f32)
    return pl.pallas_call(body, name="ssm_param_bwd", out_shape=[m, m, v, v, v])(
        dbb_r, dbb_i, bt_re, bt_im, kr, ki, ar, ai, lr, li, ldt, da_r, da_i)


def group_sum(v):
    def body(v_ref, o_ref):
        o_ref[...] = jnp.sum(v_ref[...], axis=-1, keepdims=True)
    return pl.pallas_call(body, name="group_sum", out_shape=jax.ShapeDtypeStruct((v.shape[0], 1), f32))(v)


ATTN_SCALE = XATTN_HEAD_DIM ** -0.5


def _attn_probs(q_h, k_h):
    s = lax.dot_general(q_h, k_h, NT_DIMS, preferred_element_type=f32) * ATTN_SCALE
    e = jnp.exp(s - jnp.max(s, axis=-1, keepdims=True))
    return e / jnp.sum(e, axis=-1, keepdims=True)


def attn_fwd(proj, kv, tm=TM):
    S = proj.shape[0]
    M = kv.shape[0]

    def body(q_ref, kv_ref, o_ref):
        for h in range(XATTN_HEADS):
            cols = slice(h * XATTN_HEAD_DIM, (h + 1) * XATTN_HEAD_DIM)
            q_h = q_ref[:, cols].astype(bf16)
            k_h = kv_ref[:, cols]
            v_h = kv_ref[:, XATTN_WIDTH + h * XATTN_HEAD_DIM:XATTN_WIDTH + (h + 1) * XATTN_HEAD_DIM]
            p = _attn_probs(q_h, k_h)
            o_ref[:, cols] = jnp.dot(p.astype(bf16), v_h, preferred_element_type=f32).astype(bf16)

    return pl.pallas_call(
        body, name="attn_fwd", grid=(S // tm,), in_specs=[_rows(tm, XATTN_WIDTH, 10), _full((M, 2 * XATTN_WIDTH))],
        out_specs=_rows(tm, XATTN_WIDTH), out_shape=jax.ShapeDtypeStruct((S, XATTN_WIDTH), bf16),
        compiler_params=_params())(proj, kv)


def attn_bwd(do, proj, kv, dproj, tm=TM):
    S = proj.shape[0]
    M = kv.shape[0]

    def body(do_ref, q_ref, kv_ref, _, dq_ref, dkv_ref):
        i = pl.program_id(0)

        @pl.when(i == 0)
        def _():
            dkv_ref[...] = jnp.zeros_like(dkv_ref)

        for h in range(XATTN_HEADS):
            cols = slice(h * XATTN_HEAD_DIM, (h + 1) * XATTN_HEAD_DIM)
            vcols = slice(XATTN_WIDTH + h * XATTN_HEAD_DIM, XATTN_WIDTH + (h + 1) * XATTN_HEAD_DIM)
            q_h = q_ref[:, cols].astype(bf16)
            k_h = kv_ref[:, cols]
            v_h = kv_ref[:, vcols]
            do_h = do_ref[:, cols]
            p = _attn_probs(q_h, k_h)
            dp = lax.dot_general(do_h, v_h, NT_DIMS, preferred_element_type=f32)
            ds = (p * (dp - jnp.sum(dp * p, axis=-1, keepdims=True)) * ATTN_SCALE).astype(bf16)
            dq_ref[:, cols] = jnp.dot(ds, k_h, preferred_element_type=f32).astype(bf16)
            dkv_ref[:, cols] += lax.dot_general(ds, q_h, TN_DIMS, preferred_element_type=f32)
            dkv_ref[:, vcols] += lax.dot_general(p.astype(bf16), do_h, TN_DIMS, preferred_element_type=f32)

    return pl.pallas_call(
        body, name="attn_bwd", grid=(S // tm,),
        in_specs=[_rows(tm, XATTN_WIDTH), _rows(tm, XATTN_WIDTH, 10), _full((M, 2 * XATTN_WIDTH)),
                  pl.BlockSpec(memory_space=pl.ANY)],
        out_specs=[_rows(tm, XATTN_WIDTH, 10), _full((M, 2 * XATTN_WIDTH))],
        out_shape=[jax.ShapeDtypeStruct(dproj.shape, bf16), jax.ShapeDtypeStruct((M, 2 * XATTN_WIDTH), f32)],
        input_output_aliases={3: 0},
        compiler_params=_params())(do, proj, kv, dproj)


def merge_fwd(proj, b_gate, y_a, glu, y_c, tm=TM, rides=()):
    S = proj.shape[0]

    def body(g0_ref, g1_ref, g2_ref, b_ref, ya_ref, ga_ref, gb_ref, yc_ref, o_ref):
        b = b_ref[...]
        g0 = _sig(_f32(g0_ref) + b[:, 0:D_MODEL])
        g1 = _sig(_f32(g1_ref) + b[:, D_MODEL:2 * D_MODEL])
        g2 = _sig(_f32(g2_ref) + b[:, 2 * D_MODEL:3 * D_MODEL])
        y_b = ga_ref[...].astype(f32) * _sig(gb_ref[...].astype(f32))
        o_ref[...] = (g0 * ya_ref[...].astype(f32) + g1 * y_b + g2 * yc_ref[...].astype(f32)).astype(bf16)

    res, landed = _pcall(
        body, [proj, proj, proj, b_gate, y_a, glu, glu, y_c], name="merge_fwd", grid=(S // tm,),
        in_specs=[_rows(tm, D_MODEL, 0), _rows(tm, D_MODEL, 1), _rows(tm, D_MODEL, 2), _full((1, GATE_COLS)),
                  _rows(tm, D_MODEL), _rows(tm, D_MODEL, 0), _rows(tm, D_MODEL, 1), _rows(tm, D_MODEL)],
        out_specs=[_rows(tm, D_MODEL)], out_shape=[jax.ShapeDtypeStruct((S, D_MODEL), bf16)], rides=rides)
    return (res[0], landed) if rides else res[0]


def merge_bwd(dm, proj, b_gate, y_a, glu, y_c, tm=256, rides=()):
    S = proj.shape[0]

    def body(dm_ref, g0_ref, g1_ref, g2_ref, b_ref, ya_ref, ga_ref, gb_ref, yc_ref,
             dg_ref, dya_ref, dgl_ref, dyc_ref, dbg_ref):
        i = pl.program_id(0)
        b = b_ref[...]
        dmv = dm_ref[...].astype(f32)
        ga, sb = ga_ref[...].astype(f32), _sig(gb_ref[...].astype(f32))
        ys = (ya_ref[...].astype(f32), ga * sb, yc_ref[...].astype(f32))
        gs = tuple(_sig(_f32(r) + b[:, k * D_MODEL:(k + 1) * D_MODEL]) for k, r in enumerate((g0_ref, g1_ref, g2_ref)))

        @pl.when(i == 0)
        def _():
            dbg_ref[...] = jnp.zeros_like(dbg_ref)

        for k in range(3):
            dpre = dmv * ys[k] * gs[k] * (1.0 - gs[k])
            dg_ref[:, k * D_MODEL:(k + 1) * D_MODEL] = dpre.astype(bf16)
            dbg_ref[:, k * D_MODEL:(k + 1) * D_MODEL] += jnp.sum(dpre, axis=0, keepdims=True)
        dya_ref[...] = (dmv * gs[0]).astype(bf16)
        dyc_ref[...] = (dmv * gs[2]).astype(bf16)
        dyb = dmv * gs[1]
        dgl_ref[:, 0:D_MODEL] = (dyb * sb).astype(bf16)
        dgl_ref[:, D_MODEL:2 * D_MODEL] = (dyb * ga * sb * (1.0 - sb)).astype(bf16)

    res, landed = _pcall(
        body, [dm, proj, proj, proj, b_gate, y_a, glu, glu, y_c], name="merge_bwd", grid=(S // tm,),
        in_specs=[_rows(tm, D_MODEL), _rows(tm, D_MODEL, 0), _rows(tm, D_MODEL, 1), _rows(tm, D_MODEL, 2),
                  _full((1, GATE_COLS)), _rows(tm, D_MODEL), _rows(tm, D_MODEL, 0), _rows(tm, D_MODEL, 1),
                  _rows(tm, D_MODEL)],
        out_specs=[_rows(tm, GATE_COLS), _rows(tm, D_MODEL), _rows(tm, 2 * D_MODEL), _rows(tm, D_MODEL),
                   _full((1, GATE_COLS))],
        out_shape=[jax.ShapeDtypeStruct((S, IN_COLS), bf16), jax.ShapeDtypeStruct((S, D_MODEL), bf16),
                   jax.ShapeDtypeStruct((S, 2 * D_MODEL), bf16), jax.ShapeDtypeStruct((S, D_MODEL), bf16),
                   jax.ShapeDtypeStruct((1, GATE_COLS), f32)], rides=rides)
    return (res, landed) if rides else res


def _ln_stats(r):
    mu = jnp.mean(r, axis=-1, keepdims=True)
    xc = r - mu
    var = jnp.mean(xc * xc, axis=-1, keepdims=True)
    rstd = lax.rsqrt(var + LN_EPS)
    return xc * rstd, rstd


def _ln_bwd(dy, xhat, rstd, g):
    dxh = dy * g
    return rstd * (dxh - jnp.mean(dxh, axis=-1, keepdims=True) - xhat * jnp.mean(dxh * xhat, axis=-1, keepdims=True))


def _colsum(v):
    return jnp.sum(v, axis=0, keepdims=True)


def _ln2_loss_epi(t2, h1, target, b_down, g, b):
    xhat, rstd = _ln_stats(ALPHA * h1 + t2 + b_down)
    err = xhat * g + b - target
    dout = err * (1.0 / D_MODEL)
    dr = _ln_bwd(dout, xhat, rstd, g)
    return (dr, dr), (0.5 * _colsum(err * err) * (1.0 / D_MODEL), _colsum(dout * xhat), _colsum(dout), _colsum(dr))


def _merge_bwd_epi(dm, g0, g1, g2, y_a, glu_a, glu_b, y_c, b):
    ga, sb = glu_a.astype(f32), _sig(glu_b.astype(f32))
    ys = (y_a.astype(f32), ga * sb, y_c.astype(f32))
    gs = [_sig(g.astype(f32) + b[:, k * D_MODEL:(k + 1) * D_MODEL]) for k, g in enumerate((g0, g1, g2))]
    dpre = [dm * ys[k] * gs[k] * (1.0 - gs[k]) for k in range(3)]
    dyb = dm * gs[1]
    dglu = jnp.concatenate([dyb * sb, dyb * ga * sb * (1.0 - sb)], axis=1)
    return ((jnp.concatenate(dpre, axis=1), dm * gs[0], dglu, dm * gs[2]),
            (jnp.concatenate([_colsum(d) for d in dpre], axis=1),))


def _ln1_fwd_epi(t1, x, g, b):
    xhat, _ = _ln_stats(ALPHA * x + t1)
    h = xhat * g + b
    return (t1, h, h), ()


def _ln1_bwd_epi(t3, x, t1, dr2, g):
    xhat, rstd = _ln_stats(ALPHA * x + t1)
    dh = ALPHA * dr2 + t3
    dr = _ln_bwd(dh, xhat, rstd, g)
    return (dr, dr), (_colsum(dh * xhat), _colsum(dh))


def exchange(*rides, name):
    return _pcall(lambda: None, [], name=name, grid=(), in_specs=[], out_specs=[], out_shape=[], rides=rides)[1]


ROW_TILE = 256


def sum_slots(recv, name):
    n, R, C = recv.shape
    tm = min(R, ROW_TILE)

    def body(r_ref, o_ref):
        acc = r_ref[0].astype(f32)
        for k in range(1, n):
            acc = acc + r_ref[k].astype(f32)
        o_ref[...] = acc

    return pl.pallas_call(
        body, name=name, grid=(R // tm,), in_specs=[pl.BlockSpec((n, tm, C), lambda i: (0, i, 0))],
        out_specs=_rows(tm, C), out_shape=jax.ShapeDtypeStruct((R, C), f32), compiler_params=_params())(recv)


def _adamw(w, g, m, v):
    m = ADAM_B1 * m + (1.0 - ADAM_B1) * g
    v = ADAM_B2 * v + (1.0 - ADAM_B2) * (g * g)
    m_hat = m / (1.0 - ADAM_B1 ** ADAM_STEP)
    v_hat = v / (1.0 - ADAM_B2 ** ADAM_STEP)
    delta = -ADAM_LR * (m_hat / (jnp.sqrt(v_hat) + ADAM_EPS) + ADAM_WD * w)
    return delta, m, v


def adam_pair(p, q, w, m, v, name):
    R, C = w.shape
    tm = min(R, ROW_TILE)

    def body(p_ref, q_ref, w_ref, m_ref, v_ref, g_ref, d_ref, nm_ref, nv_ref):
        g = p_ref[...] + q_ref[...]
        g_ref[...] = g
        d_ref[...], nm_ref[...], nv_ref[...] = _adamw(w_ref[...], g, m_ref[...], v_ref[...])

    o = jax.ShapeDtypeStruct((R, C), f32)
    return pl.pallas_call(body, name=name, grid=(R // tm,), in_specs=[_rows(tm, C)] * 5,
                          out_specs=[_rows(tm, C)] * 4, out_shape=[o] * 4, compiler_params=_params())(p, q, w, m, v)


def adam_slots(recv, w, m, v, name):
    n = recv.shape[0]

    def body(r_ref, w_ref, m_ref, v_ref, g_ref, d_ref, nm_ref, nv_ref):
        g = r_ref[0]
        for k in range(1, n):
            g = g + r_ref[k]
        g_ref[...] = g
        d_ref[...], nm_ref[...], nv_ref[...] = _adamw(w_ref[...], g, m_ref[...], v_ref[...])

    o = jax.ShapeDtypeStruct(w.shape, f32)
    return pl.pallas_call(body, name=name, out_shape=[o] * 4, compiler_params=_params())(recv, w, m, v)


BIG = ("w_in", "w_conv_out", "w_glu", "w_xattn_out", "w_kv", "w_out", "w_up", "w_down")
MID = ("w_conv_out", "w_glu", "w_xattn_out", "w_kv", "w_out")
SMALL = ("b_gate", "ssm_lam_re", "ssm_lam_im", "ssm_log_dt", "ssm_b_re", "ssm_b_im", "ssm_c_re", "ssm_c_im", "ssm_d",
         "ln1_g", "ln1_b", "b_up", "b_down", "ln2_g", "ln2_b")
def _pad_flat(a, mult=1024):
    a = a.reshape(-1)
    return jnp.pad(a, (0, (-a.shape[0]) % mult))


def _perm(a):
    S, W = a.shape
    return a.reshape(N_SEG, S // N_SEG, W).transpose(1, 0, 2).reshape(S, W)


def _unperm(a):
    S, W = a.shape
    return a.reshape(S // N_SEG, N_SEG, W).transpose(1, 0, 2).reshape(S, W)


def _state_rows(a):
    return a.transpose(2, 0, 1).reshape(SSM_GROUP, N_STATES)


def _block_diag_b(bt):
    b4 = bt.reshape(SSM_GROUP, 4, 8, SSM_STATE)
    eye = jnp.eye(8, dtype=bt.dtype)
    return jnp.einsum("hcgp,gk->cghkp", b4, eye).reshape(4, 128, 512)


def _block_diag_c(c):
    c4 = c.reshape(4, 8, SSM_GROUP, SSM_STATE)
    eye = jnp.eye(8, dtype=c.dtype)
    return jnp.einsum("cghp,gk->cgpkh", c4, eye).reshape(4, 512, 128)


def _diag_b(acc):
    a = acc.reshape(4, 8, SSM_STATE, 8, SSM_GROUP)
    eye = jnp.eye(8, dtype=acc.dtype)
    return jnp.einsum("cgpkh,gk->hcgp", a, eye).reshape(SSM_GROUP, N_STATES)


def _diag_c(acc):
    a = acc.reshape(4, 8, SSM_GROUP, 8, SSM_STATE)
    eye = jnp.eye(8, dtype=acc.dtype)
    return jnp.einsum("cghkp,gk->cghp", a, eye).reshape(SSM_GROUPS, SSM_GROUP, SSM_STATE)


def kernel(x, mem, w_in, b_gate, conv_w, w_conv_out, ssm_lam_re, ssm_lam_im, ssm_log_dt, ssm_b_re, ssm_b_im, ssm_c_re, ssm_c_im, ssm_d, w_glu, w_kv, w_xattn_out, w_out, ln1_g, ln1_b, w_up, b_up, w_down, b_down, ln2_g, ln2_b, loss_target, m_w_in, m_b_gate, m_conv_w, m_w_conv_out, m_ssm_lam_re, m_ssm_lam_im, m_ssm_log_dt, m_ssm_b_re, m_ssm_b_im, m_ssm_c_re, m_ssm_c_im, m_ssm_d, m_w_glu, m_w_kv, m_w_xattn_out, m_w_out, m_ln1_g, m_ln1_b, m_w_up, m_b_up, m_w_down, m_b_down, m_ln2_g, m_ln2_b, v_w_in, v_b_gate, v_conv_w, v_w_conv_out, v_ssm_lam_re, v_ssm_lam_im, v_ssm_log_dt, v_ssm_b_re, v_ssm_b_im, v_ssm_c_re, v_ssm_c_im, v_ssm_d, v_w_glu, v_w_kv, v_w_xattn_out, v_w_out, v_ln1_g, v_ln1_b, v_w_up, v_b_up, v_w_down, v_b_down, v_ln2_g, v_ln2_b):
    W = dict(w_in=w_in, b_gate=b_gate, conv_w=conv_w, w_conv_out=w_conv_out, ssm_lam_re=ssm_lam_re,
             ssm_lam_im=ssm_lam_im, ssm_log_dt=ssm_log_dt, ssm_b_re=ssm_b_re, ssm_b_im=ssm_b_im, ssm_c_re=ssm_c_re,
             ssm_c_im=ssm_c_im, ssm_d=ssm_d, w_glu=w_glu, w_kv=w_kv, w_xattn_out=w_xattn_out, w_out=w_out,
             ln1_g=ln1_g, ln1_b=ln1_b, w_up=w_up, b_up=b_up, w_down=w_down, b_down=b_down, ln2_g=ln2_g, ln2_b=ln2_b)
    MOM = dict(w_in=m_w_in, b_gate=m_b_gate, conv_w=m_conv_w, w_conv_out=m_w_conv_out, ssm_lam_re=m_ssm_lam_re,
               ssm_lam_im=m_ssm_lam_im, ssm_log_dt=m_ssm_log_dt, ssm_b_re=m_ssm_b_re, ssm_b_im=m_ssm_b_im,
               ssm_c_re=m_ssm_c_re, ssm_c_im=m_ssm_c_im, ssm_d=m_ssm_d, w_glu=m_w_glu, w_kv=m_w_kv,
               w_xattn_out=m_w_xattn_out, w_out=m_w_out, ln1_g=m_ln1_g, ln1_b=m_ln1_b, w_up=m_w_up, b_up=m_b_up,
               w_down=m_w_down, b_down=m_b_down, ln2_g=m_ln2_g, ln2_b=m_ln2_b)
    VEL = dict(w_in=v_w_in, b_gate=v_b_gate, conv_w=v_conv_w, w_conv_out=v_w_conv_out, ssm_lam_re=v_ssm_lam_re,
               ssm_lam_im=v_ssm_lam_im, ssm_log_dt=v_ssm_log_dt, ssm_b_re=v_ssm_b_re, ssm_b_im=v_ssm_b_im,
               ssm_c_re=v_ssm_c_re, ssm_c_im=v_ssm_c_im, ssm_d=v_ssm_d, w_glu=v_w_glu, w_kv=v_w_kv,
               w_xattn_out=v_w_xattn_out, w_out=v_w_out, ln1_g=v_ln1_g, ln1_b=v_ln1_b, w_up=v_w_up, b_up=v_b_up,
               w_down=v_w_down, b_down=v_b_down, ln2_g=v_ln2_g, ln2_b=v_ln2_b)
    names = list(W)
    xy = 2 * lax.axis_index("x") + lax.axis_index("y")

    xs = x[0]
    S = xs.shape[0]
    mems = mem[0]
    tgt = loss_target[0]

    shard_bf = {n: W[n][0].astype(bf16) for n in BIG}

    lr = ssm_lam_re.reshape(1, N_STATES)
    li = ssm_lam_im.reshape(1, N_STATES)
    ldt = jnp.repeat(ssm_log_dt.reshape(SSM_GROUPS), SSM_STATE).reshape(1, N_STATES)
    bt_re, bt_im = _state_rows(ssm_b_re[0]), _state_rows(ssm_b_im[0])
    ar, ai, kr, ki, bbt_r, bbt_i = ssm_prep(lr, li, ldt, bt_re, bt_im)
    bd_r, bd_i = _block_diag_b(bbt_r).astype(bf16), _block_diag_b(bbt_i).astype(bf16)
    cd_r, cd_i = _block_diag_c(ssm_c_re[0]).astype(bf16), _block_diag_c(ssm_c_im[0]).astype(bf16)
    bdt_r, bdt_i = bd_r.transpose(0, 2, 1), bd_i.transpose(0, 2, 1)
    cdt_r, cdt_i = cd_r.transpose(0, 2, 1), cd_i.transpose(0, 2, 1)
    d_skip = ssm_d.reshape(1, SSM_WIDTH)

    mem_bf = mems.astype(bf16)
    u_col = GATE_COLS + 3 * CONV_WIDTH
    me_xy = jnp.reshape(xy, (1,)).astype(jnp.int32)
    proj, x_bf, (win3,) = proj_own(xs, shard_bf["w_in"], me_xy, 4, rides=(("gather2", shard_bf["w_in"]),))
    proj, (wco3, wglu3, wxo3, wkv4, wout4, convw4) = proj_rest(
        x_bf, win3, me_xy, proj, rides=tuple(("gather", shard_bf[n]) for n in MID) + (("gather", conv_w[0]),))
    wkv3 = wkv4.reshape(1, D_MODEL, 2 * XATTN_WIDTH)
    wout3 = wout4.reshape(1, D_MODEL, D_MODEL)
    convw_full = convw4.transpose(1, 0, 2).reshape(3, CONV_WIDTH)
    a_conv = conv_fwd(proj, convw_full)
    y_a = mm_fwd_small(a_conv, wco3, "conv_out", out_dtype=bf16)
    u_perm = _perm(proj[:, u_col:u_col + SSM_WIDTH])
    half = D_MODEL // 2
    (init_r, init_i), (wup_a,) = ssm_scan_fwd(u_perm, bd_r, bd_i, ar, ai,
                                              rides=(("gather", shard_bf["w_up"][:half]),))
    (ysm_perm, ys_perm, st_r, st_i), (wup_b,) = ssm_scan_fwd(
        u_perm, bd_r, bd_i, ar, ai, final=(cd_r, cd_i, d_skip, init_r, init_i),
        rides=(("gather", shard_bf["w_up"][half:]),))
    wup3 = jnp.concatenate([wup_a, wup_b], axis=1)
    y_s = _unperm(ys_perm)
    glu = mm_fwd_small(y_s, wglu3, "glu", out_dtype=bf16)
    kv = mm_fwd_small(mem_bf, wkv3, "kv", out_dtype=bf16)
    o_att = attn_fwd(proj, kv)
    y_c = mm_fwd_small(o_att, wxo3, "xattn_out", out_dtype=bf16)
    merged = merge_fwd(proj, b_gate, y_a, glu, y_c)
    t1, h1, h1_bf = mm_rows_fused(merged, wout3, "w_out_ln1", transposed=False, tm=512, epi=_ln1_fwd_epi,
                                  extras=(xs,), consts=(ln1_g, ln1_b), outs=(f32, f32, bf16))
    (r_up, hdn), (wdn4,) = mm_fwd(h1_bf, wup3, "w_up", bias=b_up, outs=(bf16, bf16), second=_relu2,
                                  first=lambda v: jnp.maximum(v, 0.0), rides=(("gather", shard_bf["w_down"]),))
    wdn3 = wdn4.reshape(1, D_FF, D_MODEL)

    dr2, dr2_bf, loss_cols, d_ln2_g, d_ln2_b, d_b_down = mm_rows_fused(
        hdn, wdn3, "w_down_ln2_loss", transposed=False, tm=512, epi=_ln2_loss_epi, extras=(h1, tgt),
        consts=(b_down, ln2_g, ln2_b), outs=(f32, bf16), nsums=4)
    part, other = {}, {}
    g_w_down = mm_bwd_w(hdn, dr2_bf, 1, "dw_down", tk=1024).reshape(4, -1, D_MODEL)
    (dup, d_b_up), (recv_dn,) = mm_bwd_x(dr2_bf, wdn3, "dup", tm=512, extras=(r_up,), colsum=True, out_dtype=bf16,
                                         epi=lambda acc, r: acc * (2.0 * r.astype(f32)),
                                         rides=(("scatter", g_w_down),))
    part["w_down"] = sum_slots(recv_dn, "sum_w_down")
    g_w_up = mm_bwd_w(h1_bf, dup, 4, "dw_up")
    dr1, dr1_bf, d_ln1_g, d_ln1_b = mm_rows_fused(
        dup, wup3, "dh1_ln1_bwd", transposed=True, tm=256, epi=_ln1_bwd_epi, extras=(xs, t1, dr2), consts=(ln1_g,),
        outs=(f32, bf16), nsums=2)
    g_w_out = mm_bwd_w(merged, dr1_bf, 1, "dw_out").reshape(4, -1, D_MODEL)
    (dproj, dy_a, dglu, dy_c, d_b_gate), (recv_up,) = mm_rows_fused(
        dr1_bf, wout3, "dmerged_merge_bwd", transposed=True, tm=256, epi=_merge_bwd_epi,
        extras=((proj, D_MODEL, 0), (proj, D_MODEL, 1), (proj, D_MODEL, 2), y_a, (glu, D_MODEL, 0),
                (glu, D_MODEL, 1), y_c),
        consts=(b_gate,), outs=((bf16, GATE_COLS, IN_COLS), bf16, (bf16, 2 * D_MODEL, 2 * D_MODEL), bf16),
        sum_widths=(GATE_COLS,), rides=(("scatter", g_w_up),))
    part["w_up"] = sum_slots(recv_up, "sum_w_up")

    g_w_co = mm_bwd_w_small(a_conv, dy_a, 4, "dw_conv_out")
    da_conv = mm_bwd_x(dy_a, wco3, "da_conv", tm=1024)
    dproj, d_conv_w = conv_bwd(da_conv, proj, convw_full, dproj)

    g_w_glu = mm_bwd_w_small(y_s, dglu, 4, "dw_glu")
    dys_perm = _perm(mm_bwd_x(dglu, wglu3, "dy_s", tm=1024))
    linit_r, linit_i = ssm_scan_bwd(dys_perm, ysm_perm, cdt_r, cdt_i, ar, ai)
    (du_perm, dbacc_r, dbacc_i, dcacc_r, dcacc_i, da_r, da_i, d_ssm_d), (recv_co, recv_glu, recv_out) = ssm_scan_bwd(
        dys_perm, ysm_perm, cdt_r, cdt_i, ar, ai,
        final=(u_perm, st_r, st_i, bdt_r, bdt_i, d_skip, linit_r, linit_i),
        rides=(("scatter", g_w_co), ("scatter", g_w_glu), ("scatter", g_w_out)))
    part["w_out"] = sum_slots(recv_out, "sum_w_out")
    part["w_conv_out"] = sum_slots(recv_co, "sum_w_conv_out")
    part["w_glu"] = sum_slots(recv_glu, "sum_w_glu")
    dbt_re, dbt_im, d_lr, d_li, d_ldt_state = ssm_param_bwd(
        _diag_b(dbacc_r), _diag_b(dbacc_i), bt_re, bt_im, kr, ki, ar, ai, lr, li, ldt, da_r, da_i)
    d_log_dt = group_sum(d_ldt_state.reshape(SSM_GROUPS, SSM_STATE))
    d_b_re = dbt_re.reshape(SSM_GROUP, SSM_GROUPS, SSM_STATE).transpose(1, 2, 0)
    d_b_im = dbt_im.reshape(SSM_GROUP, SSM_GROUPS, SSM_STATE).transpose(1, 2, 0)
    d_c_re = _diag_c(dcacc_r)
    d_c_im = -_diag_c(dcacc_i)

    g_w_xo = mm_bwd_w_small(o_att, dy_c, 4, "dw_xattn_out")
    do_att = mm_bwd_x(dy_c, wxo3, "do_att", tm=1024, out_dtype=bf16)
    dproj, dkv = attn_bwd(do_att, proj, kv, dproj)
    g_w_kv = mm_bwd_w(mem_bf, dkv, 1, "dw_kv").reshape(4, -1, D_MODEL)

    dproj = lax.dynamic_update_slice(dproj, _unperm(du_perm), (0, u_col))
    g_small = {"b_gate": d_b_gate, "ssm_lam_re": d_lr, "ssm_lam_im": d_li, "ssm_log_dt": d_log_dt, "ssm_b_re": d_b_re,
               "ssm_b_im": d_b_im, "ssm_c_re": d_c_re, "ssm_c_im": d_c_im, "ssm_d": d_ssm_d, "ln1_g": d_ln1_g,
               "ln1_b": d_ln1_b, "b_up": d_b_up, "b_down": d_b_down, "ln2_g": d_ln2_g, "ln2_b": d_ln2_b}
    small_names = SMALL + ("conv_w",)
    g_small["conv_w"] = d_conv_w
    sizes = {n: (g_small[n].size + 1023) // 1024 * 1024 for n in small_names}
    pack = lambda d: jnp.concatenate([_pad_flat(d[n]) for n in small_names]).reshape(-1, 128)
    early = ("w_down", "w_up", "w_out", "w_conv_out", "w_glu")
    g_w_in, landed = mm_bwd_w(
        x_bf, dproj, 4, "dw_in", rides=(("scatter", g_w_xo), ("scatter", g_w_kv), ("all", pack(g_small)))
        + tuple(("pair", part[n]) for n in early))
    recv_xo, recv_kv, srecv = landed[:3]
    other.update(zip(early, landed[3:]))
    part["w_xattn_out"] = sum_slots(recv_xo, "sum_w_xattn_out")
    part["w_kv"] = sum_slots(recv_kv, "sum_w_kv")
    dx, (recv_in, other["w_xattn_out"], other["w_kv"]) = mm_bwd_x(
        dproj, win3, "dx", extras=(dr1,), epi=lambda acc, d: acc + ALPHA * d,
        rides=(("scatter", g_w_in), ("pair", part["w_xattn_out"]), ("pair", part["w_kv"])))
    part["w_in"] = sum_slots(recv_in, "sum_w_in")
    other["w_in"], = exchange(("pair", part["w_in"]), name="swap_w_in")
    res = [{}, {}, {}, {}]
    for n in BIG:
        for k, r in enumerate(adam_pair(part[n], other[n], W[n][0], MOM[n][0], VEL[n][0], "adam_" + n)):
            res[k][n] = r[None]
    conv_zero = jnp.zeros((3, CONV_WIDTH), f32)
    gs, ds_, ms, vs = adam_slots(srecv, pack({**{n: W[n] for n in SMALL}, "conv_w": conv_zero}),
                                 pack({**{n: MOM[n] for n in SMALL}, "conv_w": conv_zero}),
                                 pack({**{n: VEL[n] for n in SMALL}, "conv_w": conv_zero}), "adam_small")

    def unpack_small(buf):
        flat = buf.reshape(-1)
        out, r = {}, 0
        for n in small_names:
            ref = g_small[n] if n == "conv_w" else W[n]
            out[n] = flat[r:r + ref.size].reshape(ref.shape)
            r += sizes[n]
        return out

    res_s = [unpack_small(b) for b in (gs, ds_, ms, vs)]
    g_conv = lax.dynamic_slice(res_s[0]["conv_w"], (0, xy * 128), (3, 128))
    conv_slots = g_conv.reshape(1, 3, 128)
    cg, cd, cm, cv = adam_slots(conv_slots, conv_w[0], m_conv_w[0], v_conv_w[0], "adam_conv")
    conv_res = [cg, cd, cm, cv]

    loss = lax.psum(jnp.sum(loss_cols), ("x", "y", "c"))
    outs = [loss, dx.reshape(x.shape)]
    for k in range(4):
        for n in names:
            if n == "conv_w":
                outs.append(conv_res[k].reshape(conv_w.shape))
            elif n in BIG:
                outs.append(res[k][n])
            else:
                outs.append(res_s[k][n])
    return tuple(outs)
```

```python
import functools
import math

import jax
import jax.numpy as jnp
from jax import lax
from jax.experimental import pallas as pl
from jax.experimental.pallas import tpu as pltpu

f32 = jnp.float32
bf16 = jnp.bfloat16

D_MODEL = 1024
CONV_WIDTH = 512
SSM_WIDTH = 512
SSM_GROUP = 16
SSM_GROUPS = 32
SSM_STATE = 64
N_STATES = SSM_GROUPS * SSM_STATE
XATTN_HEADS = 4
XATTN_HEAD_DIM = 128
XATTN_WIDTH = 512
D_FF = 4096
GATE_COLS = 3 * D_MODEL
IN_COLS = GATE_COLS + 3 * CONV_WIDTH + SSM_WIDTH + XATTN_WIDTH
ALPHA = 2.0 ** 0.25
LN_EPS = 1e-5
ADAM_LR = 0.001
ADAM_B1 = 0.9
ADAM_B2 = 0.999
ADAM_EPS = 1e-08
ADAM_WD = 0.01
ADAM_STEP = 10

N_SEG = 8
SCAN_ROWS = 512
LANE_STRIP = 512
TM = 512
VMEM_LIMIT = 48 * 1024 * 1024
MESH = pl.DeviceIdType.MESH

NT_DIMS = (((1,), (1,)), ((), ()))
TN_DIMS = (((0,), (0,)), ((), ()))


def _params():
    return pltpu.CompilerParams(vmem_limit_bytes=VMEM_LIMIT)


def _full(shape):
    n = len(shape)
    return pl.BlockSpec(shape, lambda *_: (0,) * n)


def _rows(tm, w, cb=0):
    return pl.BlockSpec((tm, w), lambda i: (i, cb))


def _sig(x):
    return 1.0 / (1.0 + jnp.exp(-x))


HBM_SPEC = pl.BlockSpec(memory_space=pl.ANY)
RIDE_PEERS = {"gather": 3, "gather2": 6, "scatter": 3, "all": 7, "pair": 1}


def _xy_peers(x, y):
    return [(1 - x, y), (x, 1 - y), (1 - x, 1 - y)]


def _ride_copies(kind, src, dst, send_sems, recv_sems, local_sem):
    x, y, c = lax.axis_index("x"), lax.axis_index("y"), lax.axis_index("c")
    me = 2 * x + y
    if kind == "all":
        flips = [(fx, fy, fc) for fx in (0, 1) for fy in (0, 1) for fc in (0, 1)][1:]
        peers = [(x ^ fx, y ^ fy, c ^ fc) for fx, fy, fc in flips]
        slot = lambda p: 4 * p[0] + 2 * p[1] + p[2]
        mine = slot((x, y, c))
    else:
        peers = [(px, py, c) for px, py in _xy_peers(x, y)]
        slot = lambda p: 2 * p[0] + p[1]
        mine = me

    def remote(k, s, d):
        return pltpu.make_async_remote_copy(src_ref=s, dst_ref=d, send_sem=send_sems.at[k], recv_sem=recv_sems.at[k],
                                            device_id=peers[k], device_id_type=MESH)

    if kind == "pair":
        peers = [(x, y, 1 - c)]
        return None, [remote(0, src, dst)], [remote(0, src, dst)]
    if kind == "scatter":
        local = pltpu.make_async_copy(src.at[me], dst.at[me], local_sem)
        sends = [remote(k, src.at[slot(p)], dst.at[me]) for k, p in enumerate(peers)]
        lands = [remote(k, src.at[me], dst.at[slot(p)]) for k, p in enumerate(peers)]
    else:
        local = pltpu.make_async_copy(src, dst.at[mine], local_sem)
        sends = [remote(k, src, dst.at[mine]) for k, p in enumerate(peers)]
        lands = [remote(k, src, dst.at[slot(p)]) for k, p in enumerate(peers)]
    return local, sends, lands


def _ride_shape(kind, src):
    lead = {"gather": (4,), "gather2": (4,), "scatter": (), "all": (8,), "pair": ()}[kind]
    return jax.ShapeDtypeStruct(lead + src.shape, src.dtype)


def _two_level(src, dst, send_sems, recv_sems, local_sem):
    x, y, c = lax.axis_index("x"), lax.axis_index("y"), lax.axis_index("c")
    me = 2 * x + y
    h = src.shape[0] // 2
    mine = pl.ds(pl.multiple_of(c * h, 16), h)
    theirs = pl.ds(pl.multiple_of((1 - c) * h, 16), h)
    peers = _xy_peers(x, y)
    slots = [2 * px + py for px, py in peers]

    def over_ici(k, d):
        return pltpu.make_async_remote_copy(src_ref=src.at[mine], dst_ref=d, send_sem=send_sems.at[k],
                                            recv_sem=recv_sems.at[k], device_id=(*peers[k], c), device_id_type=MESH)

    def over_d2d(k, rows):
        blk = dst.at[slots[k], rows]
        return pltpu.make_async_remote_copy(src_ref=blk, dst_ref=blk, send_sem=send_sems.at[3 + k],
                                            recv_sem=recv_sems.at[3 + k], device_id=(x, y, 1 - c), device_id_type=MESH)

    return (pltpu.make_async_copy(src, dst.at[me], local_sem),
            [over_ici(k, dst.at[me, mine]) for k in range(3)], [over_ici(k, dst.at[slots[k], mine]) for k in range(3)],
            [over_d2d(k, mine) for k in range(3)], [over_d2d(k, theirs) for k in range(3)])


def _pcall(body, args, *, name, grid, in_specs, out_specs, out_shape, scratch_shapes=(), rides=(), aliases=None,
           prefetch=()):
    n_in, n_out, n_scr, nr, npf = len(in_specs), len(out_specs), len(scratch_shapes), len(rides), len(prefetch)
    kinds = [k for k, _ in rides]

    def wrapped(*refs):
        pre, refs = refs[:npf], refs[npf:]
        ins, rsrc = refs[:n_in], refs[n_in:n_in + nr]
        outs = refs[n_in + nr:n_in + nr + n_out]
        rdst = refs[n_in + nr + n_out:n_in + 2 * nr + n_out]
        scr = refs[n_in + 2 * nr + n_out:n_in + 2 * nr + n_out + n_scr]
        sems = refs[n_in + 2 * nr + n_out + n_scr:]

        def start():
            for r, kind in enumerate(kinds):
                if kind == "gather2":
                    local, sends, _, _, _ = _two_level(rsrc[r], rdst[r], *sems[3 * r:3 * r + 3])
                else:
                    local, sends, _ = _ride_copies(kind, rsrc[r], rdst[r], *sems[3 * r:3 * r + 3])
                if local is not None:
                    local.start()
                for cp in sends:
                    cp.start()

        def finish():
            for r, kind in enumerate(kinds):
                if kind == "gather2":
                    local, sends, ici_lands, forwards, lands = _two_level(rsrc[r], rdst[r], *sems[3 * r:3 * r + 3])
                    for k in range(3):
                        ici_lands[k].wait_recv()
                        forwards[k].start()
                    sends = sends + forwards
                else:
                    local, sends, lands = _ride_copies(kind, rsrc[r], rdst[r], *sems[3 * r:3 * r + 3])
                for cp in lands:
                    cp.wait_recv()
                for cp in sends:
                    cp.wait_send()
                if local is not None:
                    local.wait()

        if nr and grid:
            ids = [pl.program_id(a) for a in range(len(grid))]
            first = functools.reduce(jnp.logical_and, [i == 0 for i in ids])
            last = functools.reduce(jnp.logical_and, [i == g - 1 for i, g in zip(ids, grid)])
            pl.when(first)(start)
        elif nr:
            start()
        body(*pre, *ins, *outs, *scr)
        if nr and grid:
            pl.when(last)(finish)
        elif nr:
            finish()

    sems = []
    for kind in kinds:
        n = RIDE_PEERS[kind]
        sems += [pltpu.SemaphoreType.DMA((n,)), pltpu.SemaphoreType.DMA((n,)), pltpu.SemaphoreType.DMA(())]
    all_in = list(in_specs) + [HBM_SPEC] * nr
    all_out = list(out_specs) + [HBM_SPEC] * nr
    all_scratch = list(scratch_shapes) + sems
    if npf:
        how = dict(grid_spec=pltpu.PrefetchScalarGridSpec(num_scalar_prefetch=npf, grid=grid, in_specs=all_in,
                                                          out_specs=all_out, scratch_shapes=all_scratch))
    else:
        how = dict(grid=grid, in_specs=all_in, out_specs=all_out, scratch_shapes=all_scratch)
    res = pl.pallas_call(
        wrapped, name=name, out_shape=list(out_shape) + [_ride_shape(k, s) for k, s in rides],
        input_output_aliases=aliases or {}, compiler_params=_params(), **how)(
            *prefetch, *args, *[s for _, s in rides])
    return list(res[:n_out]), list(res[n_out:])


def _bf(v):
    return v if v.dtype == bf16 else v.astype(bf16)


def mm_fwd_small(a, w3, name, *, tm=1024, out_dtype=f32):
    M, K = a.shape
    J, _, n = w3.shape
    tm = min(tm, M)

    def body(a_ref, w_ref, o_ref):
        av = _bf(a_ref[...])
        for j in range(J):
            o_ref[:, j * n:(j + 1) * n] = jnp.dot(av, w_ref[j], preferred_element_type=f32).astype(out_dtype)

    return pl.pallas_call(
        body, name=name, grid=(M // tm,), in_specs=[_rows(tm, K), _full((J, K, n))], out_specs=_rows(tm, J * n),
        out_shape=jax.ShapeDtypeStruct((M, J * n), out_dtype), compiler_params=_params())(a, w3)


def mm_bwd_w_small(a, dy, J, name, *, tm=2048):
    M, K = a.shape
    n = dy.shape[1] // J
    tm = min(tm, M)
    ns = M // tm

    def body(a_ref, dy_ref, o_ref, acc_ref):
        s = pl.program_id(0)
        part = lax.dot_general(_bf(a_ref[...]), _bf(dy_ref[...]), TN_DIMS, preferred_element_type=f32)

        @pl.when(s == 0)
        def _():
            acc_ref[...] = part

        @pl.when(s > 0)
        def _():
            acc_ref[...] += part

        @pl.when(s == ns - 1)
        def _():
            for j in range(J):
                o_ref[j] = acc_ref[:, j * n:(j + 1) * n].astype(bf16)

    return pl.pallas_call(
        body, name=name, grid=(ns,), in_specs=[_rows(tm, K), _rows(tm, J * n)], out_specs=_full((J, K, n)),
        out_shape=jax.ShapeDtypeStruct((J, K, n), bf16), scratch_shapes=[pltpu.VMEM((K, J * n), f32)],
        compiler_params=_params())(a, dy)


def mm_fwd(a, w3, name, *, tm=1024, tn=None, bias=None, outs=(f32,), first=None, second=None, emit_a=False,
           rides=()):
    M, K = a.shape
    J, _, n = w3.shape
    tm = min(tm, M)
    tn = tn or n
    nl = n // tn
    nb = 0 if bias is None else 1

    def body(*refs):
        a_ref, w_ref = refs[0], refs[1]
        av = _bf(a_ref[...])
        acc = jnp.dot(av, w_ref[0], preferred_element_type=f32)
        if bias is not None:
            acc = acc + refs[2][...]
        refs[2 + nb][...] = (acc if first is None else first(acc)).astype(outs[0])
        if len(outs) > 1:
            refs[3 + nb][...] = second(acc).astype(outs[1])
        if emit_a:
            @pl.when(jnp.logical_and(pl.program_id(0) == 0, pl.program_id(1) == 0))
            def _():
                refs[2 + nb + len(outs)][...] = av

    in_specs = [pl.BlockSpec((tm, K), lambda j, l, i: (i, 0)),
                pl.BlockSpec((1, K, tn), lambda j, l, i: (j, 0, l))]
    args = [a, w3]
    if bias is not None:
        in_specs.append(pl.BlockSpec((1, tn), lambda j, l, i: (0, j * nl + l)))
        args.append(bias)
    out_specs = [pl.BlockSpec((tm, tn), lambda j, l, i: (i, j * nl + l)) for _ in outs]
    out_shape = [jax.ShapeDtypeStruct((M, J * n), dt) for dt in outs]
    if emit_a:
        last = M // tm - 1
        out_specs.append(pl.BlockSpec((tm, K), lambda j, l, i: (jnp.where((j == 0) & (l == 0), i, last), 0)))
        out_shape.append(jax.ShapeDtypeStruct((M, K), bf16))
    res, landed = _pcall(body, args, name=name, grid=(J, nl, M // tm), in_specs=in_specs, out_specs=out_specs,
                         out_shape=out_shape, rides=rides)
    res = res if len(res) > 1 else res[0]
    return (res, landed) if rides else res


def proj_own(x, w_own, me, n_blocks, *, tm=1024, rides=()):
    M, K = x.shape
    n = w_own.shape[1]

    def body(me_ref, x_ref, w_ref, o_ref, xb_ref):
        av = _bf(x_ref[...])
        xb_ref[...] = av
        o_ref[...] = jnp.dot(av, w_ref[...], preferred_element_type=f32).astype(bf16)

    (proj, x_bf), landed = _pcall(
        body, [x, w_own], name="proj_own", grid=(M // tm,), prefetch=(me,),
        in_specs=[pl.BlockSpec((tm, K), lambda i, me_ref: (i, 0)), pl.BlockSpec((K, n), lambda i, me_ref: (0, 0))],
        out_specs=[pl.BlockSpec((tm, n), lambda i, me_ref: (i, me_ref[0])),
                   pl.BlockSpec((tm, K), lambda i, me_ref: (i, 0))],
        out_shape=[jax.ShapeDtypeStruct((M, n_blocks * n), bf16), jax.ShapeDtypeStruct((M, K), bf16)], rides=rides)
    return proj, x_bf, landed


def proj_rest(x_bf, w3, me, proj, *, tm=1024, rides=()):
    M, K = x_bf.shape
    J, _, n = w3.shape

    def body(me_ref, x_ref, w_ref, _, o_ref):
        o_ref[...] = jnp.dot(x_ref[...], w_ref[0], preferred_element_type=f32).astype(bf16)

    (proj,), landed = _pcall(
        body, [x_bf, w3, proj], name="proj_rest", grid=(J - 1, M // tm), prefetch=(me,),
        in_specs=[pl.BlockSpec((tm, K), lambda k, i, me_ref: (i, 0)),
                  pl.BlockSpec((1, K, n), lambda k, i, me_ref: (me_ref[0] ^ (k + 1), 0, 0)), HBM_SPEC],
        out_specs=[pl.BlockSpec((tm, n), lambda k, i, me_ref: (i, me_ref[0] ^ (k + 1)))],
        out_shape=[jax.ShapeDtypeStruct(proj.shape, bf16)], aliases={3: 0}, rides=rides)
    return proj, landed


def mm_bwd_x(dy, w3, name, *, tm=TM, epi=None, extras=(), colsum=False, out_dtype=f32, rides=()):
    M = dy.shape[0]
    J, K, n = w3.shape
    tm = min(tm, M)
    nex = len(extras)

    def body(*refs):
        dy_ref, w_hbm = refs[0], refs[1]
        ex = refs[2:2 + nex]
        o_ref = refs[2 + nex]
        w_ref = refs[-1]
        i = pl.program_id(0)

        @pl.when(i == 0)
        def _():
            pltpu.sync_copy(w_hbm, w_ref)

        acc = None
        for j in range(J):
            part = lax.dot_general(_bf(dy_ref[:, j * n:(j + 1) * n]), w_ref[j], NT_DIMS, preferred_element_type=f32)
            acc = part if acc is None else acc + part
        if epi is not None:
            acc = epi(acc, *[e[...] for e in ex])
        o_ref[...] = acc.astype(out_dtype)
        if colsum:
            s_ref = refs[3 + nex]
            cs = jnp.sum(acc, axis=0, keepdims=True)

            @pl.when(i == 0)
            def _():
                s_ref[...] = cs

            @pl.when(i > 0)
            def _():
                s_ref[...] += cs

    in_specs = [pl.BlockSpec((tm, J * n), lambda i: (i, 0)), pl.BlockSpec(memory_space=pl.ANY)]
    in_specs += [pl.BlockSpec((tm, K), lambda i: (i, 0)) for _ in extras]
    out_specs = [pl.BlockSpec((tm, K), lambda i: (i, 0))]
    out_shape = [jax.ShapeDtypeStruct((M, K), out_dtype)]
    if colsum:
        out_specs.append(pl.BlockSpec((1, K), lambda i: (0, 0)))
        out_shape.append(jax.ShapeDtypeStruct((1, K), f32))
    res, landed = _pcall(body, [dy, w3, *extras], name=name, grid=(M // tm,), in_specs=in_specs, out_specs=out_specs,
                         out_shape=out_shape, scratch_shapes=[pltpu.VMEM((J, K, n), bf16)], rides=rides)
    res = res if colsum else res[0]
    return (res, landed) if rides else res


def mm_rows_fused(a, w3, name, *, transposed, tm, epi, extras=(), consts=(), outs=(f32,), nsums=0, sum_widths=None,
                  rides=()):
    pro = a[0] if isinstance(a, tuple) else None
    J, K, n = w3.shape
    W = K if transposed else n
    ka = J * n if transposed else K
    if pro is not None:
        a_rows = [e if isinstance(e, tuple) else (e, ka, 0) for e in a[1]]
        a_consts = list(a[2])
    else:
        a_rows, a_consts = [(a, ka, 0)], []
    M = a_rows[0][0].shape[0]
    tm = min(tm, M)
    na = len(a_rows) + len(a_consts)
    nex, nco = len(extras), len(consts)
    if pro is not None:
        outs = (bf16 if ka == W else (bf16, ka, ka),) + tuple(outs)
    nout = len(outs)
    nsums = len(sum_widths) if sum_widths is not None else nsums
    out_dtypes = [o[0] if isinstance(o, tuple) else o for o in outs]

    def body(*refs):
        a_refs, w_hbm = refs[:na], refs[na]
        ex = refs[na + 1:na + 1 + nex]
        co = refs[na + 1 + nex:na + 1 + nex + nco]
        o_refs = refs[na + 1 + nex + nco:na + 1 + nex + nco + nout]
        s_refs = refs[na + 1 + nex + nco + nout:na + 1 + nex + nco + nout + nsums]
        w_ref = refs[-1]
        i = pl.program_id(0)

        @pl.when(i == 0)
        def _():
            pltpu.sync_copy(w_hbm, w_ref)

        av = _bf(a_refs[0][...] if pro is None else pro(*[r[...] for r in a_refs]))
        if transposed:
            acc = None
            for j in range(J):
                part = lax.dot_general(av[:, j * n:(j + 1) * n], w_ref[j], NT_DIMS, preferred_element_type=f32)
                acc = part if acc is None else acc + part
        else:
            acc = jnp.dot(av, w_ref[0], preferred_element_type=f32)
        rows, sums = epi(acc, *[e[...] for e in ex], *[c[...] for c in co])
        if pro is not None:
            rows = (av,) + tuple(rows)
        for o_ref, v, dt in zip(o_refs, rows, out_dtypes):
            o_ref[...] = v.astype(dt)
        for s_ref, v in zip(s_refs, sums):
            @pl.when(i == 0)
            def _(s_ref=s_ref, v=v):
                s_ref[...] = v

            @pl.when(i > 0)
            def _(s_ref=s_ref, v=v):
                s_ref[...] += v

    extras = [e if isinstance(e, tuple) else (e, W, 0) for e in extras]
    outs_full = [o if isinstance(o, tuple) else (o, W, W) for o in outs]
    widths = list(sum_widths) if sum_widths is not None else [W] * nsums
    in_specs = [_rows(tm, w, cb) for _, w, cb in a_rows] + [_full(c.shape) for c in a_consts]
    in_specs += [pl.BlockSpec(memory_space=pl.ANY)]
    in_specs += [_rows(tm, w, cb) for _, w, cb in extras] + [_full(c.shape) for c in consts]
    out_specs = [_rows(tm, w) for _, w, _ in outs_full] + [_full((1, w)) for w in widths]
    out_shape = ([jax.ShapeDtypeStruct((M, cols), dt) for dt, _, cols in outs_full]
                 + [jax.ShapeDtypeStruct((1, w), f32) for w in widths])
    args = [e for e, _, _ in a_rows] + a_consts + [w3] + [e for e, _, _ in extras] + list(consts)
    res, landed = _pcall(body, args, name=name, grid=(M // tm,), in_specs=in_specs, out_specs=out_specs,
                         out_shape=out_shape, scratch_shapes=[pltpu.VMEM((J, K, n), bf16)], rides=rides)
    return (res, landed) if rides else res


def mm_bwd_w(a, dy, J, name, *, tm=2048, tn=None, tk=None, rides=()):
    M, K = a.shape
    n = dy.shape[1] // J
    tm = min(tm, M)
    tn = tn or n
    tk = tk or K
    nl = n // tn
    nk = K // tk
    ns = M // tm

    def body(a_ref, dy_ref, o_ref, acc_ref):
        s = pl.program_id(3)
        part = lax.dot_general(_bf(a_ref[...]), _bf(dy_ref[...]), TN_DIMS, preferred_element_type=f32)

        @pl.when(s == 0)
        def _():
            acc_ref[...] = part

        @pl.when(s > 0)
        def _():
            acc_ref[...] += part

        @pl.when(s == ns - 1)
        def _():
            o_ref[0] = acc_ref[...].astype(bf16)

    res, landed = _pcall(
        body, [a, dy], name=name, grid=(J, nl, nk, ns),
        in_specs=[pl.BlockSpec((tm, tk), lambda j, l, k, s: (s, k)),
                  pl.BlockSpec((tm, tn), lambda j, l, k, s: (s, j * nl + l))],
        out_specs=[pl.BlockSpec((1, tk, tn), lambda j, l, k, s: (j, k, l))],
        out_shape=[jax.ShapeDtypeStruct((J, K, n), bf16)],
        scratch_shapes=[pltpu.VMEM((tk, tn), f32)], rides=rides)
    return (res[0], landed) if rides else res[0]


def _relu2(v):
    r = jnp.maximum(v, 0.0)
    return r * r


HALO = 16


def _shift_down(z, k, halo):
    r = lax.broadcasted_iota(jnp.int32, z.shape, 0)
    y = pltpu.roll(z, k, 0)
    for q in range(k):
        y = jnp.where(r == q, halo[HALO - k + q:HALO - k + q + 1, :], y)
    return y


def _shift_up(z, k, halo):
    tm = z.shape[0]
    r = lax.broadcasted_iota(jnp.int32, z.shape, 0)
    y = pltpu.roll(z, tm - k, 0)
    for q in range(k):
        y = jnp.where(r == tm - k + q, halo[q:q + 1, :], y)
    return y


def _prev_halo(tm, cb):
    return pl.BlockSpec((HALO, CONV_WIDTH), lambda i: (jnp.maximum(i * (tm // HALO) - 1, 0), cb))


def _next_halo(tm, cb, nblk):
    return pl.BlockSpec((HALO, CONV_WIDTH), lambda i: (jnp.minimum((i + 1) * (tm // HALO), nblk - 1), cb))


def _f32(ref):
    return ref[...].astype(f32)


def conv_fwd(proj, conv_w, tm=TM):
    S = proj.shape[0]

    def body(cb_ref, cc_ref, ch_ref, cch_ref, chh_ref, w_ref, a_ref):
        i = pl.program_id(0)
        z = _f32(cc_ref) * _f32(ch_ref)
        zh = jnp.where(i == 0, 0.0, _f32(cch_ref) * _f32(chh_ref))
        w = w_ref[...]
        dwz = w[0:1, :] * _shift_down(z, 2, zh) + w[1:2, :] * _shift_down(z, 1, zh) + w[2:3, :] * z
        a_ref[...] = (_f32(cb_ref) * dwz).astype(bf16)

    return pl.pallas_call(
        body, name="conv_fwd", grid=(S // tm,),
        in_specs=[_rows(tm, CONV_WIDTH, 6), _rows(tm, CONV_WIDTH, 7), _rows(tm, CONV_WIDTH, 8),
                  _prev_halo(tm, 7), _prev_halo(tm, 8), _full((3, CONV_WIDTH))],
        out_specs=_rows(tm, CONV_WIDTH),
        out_shape=jax.ShapeDtypeStruct((S, CONV_WIDTH), bf16), compiler_params=_params())(
            proj, proj, proj, proj, proj, conv_w)


def conv_bwd(da, proj, conv_w, dproj, tm=TM):
    S = proj.shape[0]
    nt = S // tm

    def body(da_ref, cb_ref, cc_ref, ch_ref, cch_ref, chh_ref, dan_ref, cbn_ref, w_ref, _, o_ref, dw_ref):
        i = pl.program_id(0)
        cc, ch, cb, dav = _f32(cc_ref), _f32(ch_ref), _f32(cb_ref), da_ref[...]
        z = cc * ch
        zh = jnp.where(i == 0, 0.0, _f32(cch_ref) * _f32(chh_ref))
        w = w_ref[...]
        z1 = _shift_down(z, 1, zh)
        z2 = _shift_down(z, 2, zh)
        dwz = w[0:1, :] * z2 + w[1:2, :] * z1 + w[2:3, :] * z
        g = dav * cb
        gn = jnp.where(i == nt - 1, 0.0, dan_ref[...] * _f32(cbn_ref))
        dz = w[2:3, :] * g + w[1:2, :] * _shift_up(g, 1, gn) + w[0:1, :] * _shift_up(g, 2, gn)
        o_ref[:, 0:CONV_WIDTH] = (dav * dwz).astype(bf16)
        o_ref[:, CONV_WIDTH:2 * CONV_WIDTH] = (dz * ch).astype(bf16)
        o_ref[:, 2 * CONV_WIDTH:3 * CONV_WIDTH] = (dz * cc).astype(bf16)

        @pl.when(i == 0)
        def _():
            dw_ref[...] = jnp.zeros_like(dw_ref)

        dw_ref[0:1, :] += jnp.sum(g * z2, axis=0, keepdims=True)
        dw_ref[1:2, :] += jnp.sum(g * z1, axis=0, keepdims=True)
        dw_ref[2:3, :] += jnp.sum(g * z, axis=0, keepdims=True)

    return pl.pallas_call(
        body, name="conv_bwd", grid=(nt,),
        in_specs=[_rows(tm, CONV_WIDTH), _rows(tm, CONV_WIDTH, 6), _rows(tm, CONV_WIDTH, 7), _rows(tm, CONV_WIDTH, 8),
                  _prev_halo(tm, 7), _prev_halo(tm, 8),
                  _next_halo(tm, 0, S // HALO), _next_halo(tm, 6, S // HALO), _full((3, CONV_WIDTH)),
                  pl.BlockSpec(memory_space=pl.ANY)],
        out_specs=[_rows(tm, 3 * CONV_WIDTH, GATE_COLS // (3 * CONV_WIDTH)), _full((3, CONV_WIDTH))],
        out_shape=[jax.ShapeDtypeStruct(dproj.shape, bf16), jax.ShapeDtypeStruct((3, CONV_WIDTH), f32)],
        input_output_aliases={9: 0},
        compiler_params=_params())(da, proj, proj, proj, proj, proj, da, proj, conv_w, dproj)


def _cmul(ar, ai, br, bi):
    return ar * br - ai * bi, ar * bi + ai * br


def ssm_prep(lr, li, ldt, bt_re, bt_im):
    def body(lr_ref, li_ref, ldt_ref, br_ref, bi_ref, ar_ref, ai_ref, kr_ref, ki_ref, bbr_ref, bbi_ref):
        lrv, liv = lr_ref[...], li_ref[...]
        dt = jnp.exp(ldt_ref[...])
        mag = jnp.exp(lrv * dt)
        ar = mag * jnp.cos(liv * dt)
        ai = mag * jnp.sin(liv * dt)
        den = lrv * lrv + liv * liv
        nr = ar - 1.0
        kr = (nr * lrv + ai * liv) / den
        ki = (ai * lrv - nr * liv) / den
        ar_ref[...], ai_ref[...], kr_ref[...], ki_ref[...] = ar, ai, kr, ki
        bbr_ref[...] = kr * br_ref[...] - ki * bi_ref[...]
        bbi_ref[...] = kr * bi_ref[...] + ki * br_ref[...]

    v = jax.ShapeDtypeStruct((1, N_STATES), f32)
    m = jax.ShapeDtypeStruct((SSM_GROUP, N_STATES), f32)
    return pl.pallas_call(body, name="ssm_prep", out_shape=[v, v, v, v, m, m])(lr, li, ldt, bt_re, bt_im)


def _pow_segment(ar, ai, seg_len):
    pr, pi = ar, ai
    for _ in range(int(math.log2(seg_len))):
        pr, pi = _cmul(pr, pi, pr, pi)
    return pr, pi


GELU_K = math.sqrt(2.0 / math.pi)
GELU_C = 0.044715


def _gelu(v):
    return 0.5 * v * (1.0 + jnp.tanh(GELU_K * (v + GELU_C * v * v * v)))


def _gelu_grad(v):
    t = jnp.tanh(GELU_K * (v + GELU_C * v * v * v))
    return 0.5 * (1.0 + t) + 0.5 * v * (1.0 - t * t) * GELU_K * (1.0 + 3.0 * GELU_C * v * v)


def ssm_scan_fwd(u_perm, bd_r, bd_i, ar, ai, *, final=None, rides=()):
    S = u_perm.shape[0]
    R = SCAN_ROWS
    nblk = S // R
    seg_len = S // N_SEG
    nstrip = N_STATES // LANE_STRIP
    store = final is not None

    def body(*refs):
        if store:
            (u_ref, bdr_ref, bdi_ref, ar_ref, ai_ref, cdr_ref, cdi_ref, d_ref, ir_ref, ii_ref,
             y_ref, ys_ref, sr_ref, si_ref, bur, bui, car, cai) = refs
        else:
            u_ref, bdr_ref, bdi_ref, ar_ref, ai_ref, or_ref, oi_ref, bur, bui, car, cai = refs
        i = pl.program_id(0)
        ub = _bf(u_ref[...])
        u = ub.astype(f32)
        for cb in range(4):
            us = ub[:, cb * 128:(cb + 1) * 128]
            bur[:, cb * 512:(cb + 1) * 512] = jnp.dot(us, bdr_ref[cb], preferred_element_type=f32)
            bui[:, cb * 512:(cb + 1) * 512] = jnp.dot(us, bdi_ref[cb], preferred_element_type=f32)

        @pl.when(i == 0)
        def _():
            if store:
                car[...] = ir_ref[...]
                cai[...] = ii_ref[...]
            else:
                car[...] = jnp.zeros_like(car)
                cai[...] = jnp.zeros_like(cai)

        for ls in range(nstrip):
            lanes = pl.ds(ls * LANE_STRIP, LANE_STRIP)
            a_r = jnp.broadcast_to(ar_ref[:, lanes], (N_SEG, LANE_STRIP))
            a_i = jnp.broadcast_to(ai_ref[:, lanes], (N_SEG, LANE_STRIP))

            def step(t, carry, lanes=lanes, a_r=a_r, a_i=a_i):
                s_r, s_i = carry
                row = pl.multiple_of(t * 8, 8)
                n_r = a_r * s_r - a_i * s_i + bur[pl.ds(row, 8), lanes]
                n_i = a_r * s_i + a_i * s_r + bui[pl.ds(row, 8), lanes]
                if store:
                    sr_ref[pl.ds(row, 8), lanes] = n_r
                    si_ref[pl.ds(row, 8), lanes] = n_i
                return n_r, n_i

            e_r, e_i = lax.fori_loop(0, R // 8, step, (car[:, lanes], cai[:, lanes]), unroll=True)
            car[:, lanes] = e_r
            cai[:, lanes] = e_i

        if store:
            for cb in range(4):
                st_r = sr_ref[:, cb * 512:(cb + 1) * 512].astype(bf16)
                st_i = si_ref[:, cb * 512:(cb + 1) * 512].astype(bf16)
                y = (jnp.dot(st_r, cdr_ref[cb], preferred_element_type=f32)
                     - jnp.dot(st_i, cdi_ref[cb], preferred_element_type=f32))
                cols = slice(cb * 128, (cb + 1) * 128)
                y = y + d_ref[:, cols] * u[:, cols]
                y_ref[:, cols] = y
                ys_ref[:, cols] = _gelu(y).astype(bf16)
        else:
            @pl.when(i == nblk - 1)
            def _():
                p_r, p_i = _pow_segment(ar_ref[...], ai_ref[...], seg_len)
                t_r, t_i = car[0:1, :], cai[0:1, :]
                or_ref[0:1, :] = jnp.zeros((1, N_STATES), f32)
                oi_ref[0:1, :] = jnp.zeros((1, N_STATES), f32)
                for j in range(1, N_SEG):
                    or_ref[j:j + 1, :] = t_r
                    oi_ref[j:j + 1, :] = t_i
                    m_r, m_i = _cmul(p_r, p_i, t_r, t_i)
                    t_r, t_i = car[j:j + 1, :] + m_r, cai[j:j + 1, :] + m_i

    blk = lambda w: pl.BlockSpec((R, w), lambda i: (i, 0))
    in_specs = [blk(SSM_WIDTH), _full((4, 128, 512)), _full((4, 128, 512)), _full((1, N_STATES)), _full((1, N_STATES))]
    args = [u_perm, bd_r, bd_i, ar, ai]
    scratch = [pltpu.VMEM((R, N_STATES), f32), pltpu.VMEM((R, N_STATES), f32),
               pltpu.VMEM((N_SEG, N_STATES), f32), pltpu.VMEM((N_SEG, N_STATES), f32)]
    if store:
        in_specs += [_full((4, 512, 128)), _full((4, 512, 128)), _full((1, SSM_WIDTH)),
                     _full((N_SEG, N_STATES)), _full((N_SEG, N_STATES))]
        args += list(final)
        out_specs = [blk(SSM_WIDTH), blk(SSM_WIDTH), blk(N_STATES), blk(N_STATES)]
        out_shape = [jax.ShapeDtypeStruct((S, SSM_WIDTH), f32), jax.ShapeDtypeStruct((S, SSM_WIDTH), bf16),
                     jax.ShapeDtypeStruct((S, N_STATES), f32), jax.ShapeDtypeStruct((S, N_STATES), f32)]
        name = "ssm_scan_fwd"
    else:
        out_specs = [_full((N_SEG, N_STATES)), _full((N_SEG, N_STATES))]
        out_shape = [jax.ShapeDtypeStruct((N_SEG, N_STATES), f32)] * 2
        name = "ssm_scan_fwd_carry"
    res, landed = _pcall(body, args, name=name, grid=(nblk,), in_specs=in_specs, out_specs=out_specs,
                         out_shape=out_shape, scratch_shapes=scratch, rides=rides)
    return (res, landed) if rides else res


def ssm_scan_bwd(dys_perm, y_perm, cdt_r, cdt_i, ar, ai, *, final=None, rides=()):
    S = dys_perm.shape[0]
    R = SCAN_ROWS
    nblk = S // R
    seg_len = S // N_SEG
    store = final is not None
    strip = LANE_STRIP // 2 if store else LANE_STRIP
    nstrip = N_STATES // strip

    def body(*refs):
        if store:
            (dys_ref, y_ref, cdr_ref, cdi_ref, ar_ref, ai_ref, u_ref, sr_ref, si_ref, bdr_ref, bdi_ref, d_ref, ir_ref,
             ii_ref, du_ref, dbr_ref, dbi_ref, dcr_ref, dci_ref, dar_ref, dai_ref, dd_ref, dsr, dsi, lmr, lmi, car,
             cai) = refs
        else:
            dys_ref, y_ref, cdr_ref, cdi_ref, ar_ref, ai_ref, or_ref, oi_ref, dsr, dsi, car, cai = refs
        i = pl.program_id(0)
        dy = dys_ref[...] * _gelu_grad(y_ref[...])
        dyb = dy.astype(bf16)
        for cb in range(4):
            ds_ = dyb[:, cb * 128:(cb + 1) * 128]
            dsr[:, cb * 512:(cb + 1) * 512] = jnp.dot(ds_, cdr_ref[cb], preferred_element_type=f32)
            dsi[:, cb * 512:(cb + 1) * 512] = -jnp.dot(ds_, cdi_ref[cb], preferred_element_type=f32)

        @pl.when(i == 0)
        def _():
            if store:
                car[...] = ir_ref[...]
                cai[...] = ii_ref[...]
                dar_ref[...] = jnp.zeros_like(dar_ref)
                dai_ref[...] = jnp.zeros_like(dai_ref)
                dbr_ref[...] = jnp.zeros_like(dbr_ref)
                dbi_ref[...] = jnp.zeros_like(dbi_ref)
                dcr_ref[...] = jnp.zeros_like(dcr_ref)
                dci_ref[...] = jnp.zeros_like(dci_ref)
                dd_ref[...] = jnp.zeros_like(dd_ref)
            else:
                car[...] = jnp.zeros_like(car)
                cai[...] = jnp.zeros_like(cai)

        for ls in range(nstrip):
            lanes = pl.ds(ls * strip, strip)
            a_r = jnp.broadcast_to(ar_ref[:, lanes], (N_SEG, strip))
            a_i = jnp.broadcast_to(ai_ref[:, lanes], (N_SEG, strip))
            zero = jnp.zeros((N_SEG, strip), f32)

            def step(k, carry, lanes=lanes, a_r=a_r, a_i=a_i):
                l_r, l_i, g_r, g_i = carry
                row = pl.multiple_of((R // 8 - 1 - k) * 8, 8)
                if store:
                    s_r = sr_ref[pl.ds(row, 8), lanes]
                    s_i = si_ref[pl.ds(row, 8), lanes]
                    g_r = g_r + l_r * s_r + l_i * s_i
                    g_i = g_i + l_i * s_r - l_r * s_i
                n_r = dsr[pl.ds(row, 8), lanes] + a_r * l_r + a_i * l_i
                n_i = dsi[pl.ds(row, 8), lanes] + a_r * l_i - a_i * l_r
                if store:
                    lmr[pl.ds(row, 8), lanes] = n_r
                    lmi[pl.ds(row, 8), lanes] = n_i
                return n_r, n_i, g_r, g_i

            e_r, e_i, g_r, g_i = lax.fori_loop(0, R // 8, step, (car[:, lanes], cai[:, lanes], zero, zero),
                                               unroll=True)
            car[:, lanes] = e_r
            cai[:, lanes] = e_i
            if store:
                dar_ref[:, lanes] += g_r
                dai_ref[:, lanes] += g_i

        if store:
            ub = _bf(u_ref[...])
            u = ub.astype(f32)
            for cb in range(4):
                cols = slice(cb * 128, (cb + 1) * 128)
                st = slice(cb * 512, (cb + 1) * 512)
                l_r = lmr[:, st].astype(bf16)
                l_i = lmi[:, st].astype(bf16)
                du = (jnp.dot(l_r, bdr_ref[cb], preferred_element_type=f32)
                      + jnp.dot(l_i, bdi_ref[cb], preferred_element_type=f32))
                du_ref[:, cols] = (du + d_ref[:, cols] * dy[:, cols]).astype(bf16)
                dbr_ref[cb] += lax.dot_general(l_r, ub[:, cols], TN_DIMS, preferred_element_type=f32)
                dbi_ref[cb] += lax.dot_general(l_i, ub[:, cols], TN_DIMS, preferred_element_type=f32)
                dcr_ref[cb] += lax.dot_general(dyb[:, cols], sr_ref[:, st].astype(bf16), TN_DIMS,
                                               preferred_element_type=f32)
                dci_ref[cb] += lax.dot_general(dyb[:, cols], si_ref[:, st].astype(bf16), TN_DIMS,
                                               preferred_element_type=f32)
            dd_ref[...] += jnp.sum(dy * u, axis=0, keepdims=True)
        else:
            @pl.when(i == nblk - 1)
            def _():
                p_r, p_i = _pow_segment(ar_ref[...], ai_ref[...], seg_len)
                p_i = -p_i
                t_r, t_i = car[N_SEG - 1:N_SEG, :], cai[N_SEG - 1:N_SEG, :]
                or_ref[N_SEG - 1:N_SEG, :] = jnp.zeros((1, N_STATES), f32)
                oi_ref[N_SEG - 1:N_SEG, :] = jnp.zeros((1, N_STATES), f32)
                for j in range(N_SEG - 2, -1, -1):
                    or_ref[j:j + 1, :] = t_r
                    oi_ref[j:j + 1, :] = t_i
                    m_r, m_i = _cmul(p_r, p_i, t_r, t_i)
                    t_r, t_i = car[j:j + 1, :] + m_r, cai[j:j + 1, :] + m_i

    blk = lambda w: pl.BlockSpec((R, w), lambda i: (nblk - 1 - i, 0))
    in_specs = [blk(SSM_WIDTH), blk(SSM_WIDTH), _full((4, 128, 512)), _full((4, 128, 512)), _full((1, N_STATES)),
                _full((1, N_STATES))]
    args = [dys_perm, y_perm, cdt_r, cdt_i, ar, ai]
    seg = jax.ShapeDtypeStruct((N_SEG, N_STATES), f32)
    if store:
        in_specs += [blk(SSM_WIDTH), blk(N_STATES), blk(N_STATES), _full((4, 512, 128)), _full((4, 512, 128)),
                     _full((1, SSM_WIDTH)), _full((N_SEG, N_STATES)), _full((N_SEG, N_STATES))]
        args += list(final)
        out_specs = [blk(SSM_WIDTH), _full((4, 512, 128)), _full((4, 512, 128)), _full((4, 128, 512)),
                     _full((4, 128, 512)), _full((N_SEG, N_STATES)), _full((N_SEG, N_STATES)), _full((1, SSM_WIDTH))]
        b_acc = jax.ShapeDtypeStruct((4, 512, 128), f32)
        c_acc = jax.ShapeDtypeStruct((4, 128, 512), f32)
        out_shape = [jax.ShapeDtypeStruct((S, SSM_WIDTH), bf16), b_acc, b_acc, c_acc, c_acc, seg, seg,
                     jax.ShapeDtypeStruct((1, SSM_WIDTH), f32)]
        scratch = [pltpu.VMEM((R, N_STATES), f32)] * 4 + [pltpu.VMEM((N_SEG, N_STATES), f32)] * 2
        name = "ssm_scan_bwd"
    else:
        out_specs = [_full((N_SEG, N_STATES)), _full((N_SEG, N_STATES))]
        out_shape = [seg, seg]
        scratch = [pltpu.VMEM((R, N_STATES), f32)] * 2 + [pltpu.VMEM((N_SEG, N_STATES), f32)] * 2
        name = "ssm_scan_bwd_carry"
    res, landed = _pcall(body, args, name=name, grid=(nblk,), in_specs=in_specs, out_specs=out_specs,
                         out_shape=out_shape, scratch_shapes=scratch, rides=rides)
    return (res, landed) if rides else res


def ssm_param_bwd(dbb_r, dbb_i, bt_re, bt_im, kr, ki, ar, ai, lr, li, ldt, da_r, da_i):
    def body(dbr_ref, dbi_ref, br_ref, bi_ref, kr_ref, ki_ref, ar_ref, ai_ref, lr_ref, li_ref, ldt_ref, dar_ref,
             dai_ref, obr_ref, obi_ref, olr_ref, oli_ref, odt_ref):
        dbr, dbi, b_r, b_i = dbr_ref[...], dbi_ref[...], br_ref[...], bi_ref[...]
        k_r, k_i, a_r, a_i = kr_ref[...], ki_ref[...], ar_ref[...], ai_ref[...]
        l_r, l_i = lr_ref[...], li_ref[...]
        dt = jnp.exp(ldt_ref[...])
        obr_ref[...] = k_r * dbr + k_i * dbi
        obi_ref[...] = k_r * dbi - k_i * dbr
        gk_r = jnp.sum(dbr * b_r + dbi * b_i, axis=0, keepdims=True)
        gk_i = jnp.sum(dbi * b_r - dbr * b_i, axis=0, keepdims=True)
        ga_r = jnp.sum(dar_ref[...], axis=0, keepdims=True)
        ga_i = jnp.sum(dai_ref[...], axis=0, keepdims=True)
        den = l_r * l_r + l_i * l_i
        c_r, c_i = l_r / den, l_i / den
        m_r, m_i = _cmul(c_r, c_i, gk_r, gk_i)
        g_r, g_i = ga_r + m_r, ga_i + m_i
        t1_r, t1_i = _cmul(dt * a_r, -dt * a_i, g_r, g_i)
        q_r, q_i = _cmul(k_r, k_i, c_r, -c_i)
        t2_r, t2_i = _cmul(-q_r, q_i, gk_r, gk_i)
        olr_ref[...] = t1_r + t2_r
        oli_ref[...] = t1_i + t2_i
        w_r, w_i = _cmul(l_r, l_i, a_r, a_i)
        odt_ref[...] = dt * (w_r * g_r + w_i * g_i)

    v = jax.ShapeDtypeStruct((1, N_STATES), f32)
    m = jax.ShapeDtypeStruct((SSM_GROUP, N_STATES), f32)
    return pl.pallas_call(body, name="ssm_param_bwd", out_shape=[m, m, v, v, v])(
        dbb_r, dbb_i, bt_re, bt_im, kr, ki, ar, ai, lr, li, ldt, da_r, da_i)


def group_sum(v):
    def body(v_ref, o_ref):
        o_ref[...] = jnp.sum(v_ref[...], axis=-1, keepdims=True)
    return pl.pallas_call(body, name="group_sum", out_shape=jax.ShapeDtypeStruct((v.shape[0], 1), f32))(v)


ATTN_SCALE = XATTN_HEAD_DIM ** -0.5


def _attn_probs(q_h, k_h):
    s = lax.dot_general(q_h, k_h, NT_DIMS, preferred_element_type=f32) * ATTN_SCALE
    e = jnp.exp(s - jnp.max(s, axis=-1, keepdims=True))
    return e / jnp.sum(e, axis=-1, keepdims=True)


def attn_fwd(proj, kv, tm=TM):
    S = proj.shape[0]
    M = kv.shape[0]

    def body(q_ref, kv_ref, o_ref):
        for h in range(XATTN_HEADS):
            cols = slice(h * XATTN_HEAD_DIM, (h + 1) * XATTN_HEAD_DIM)
            q_h = q_ref[:, cols].astype(bf16)
            k_h = kv_ref[:, cols]
            v_h = kv_ref[:, XATTN_WIDTH + h * XATTN_HEAD_DIM:XATTN_WIDTH + (h + 1) * XATTN_HEAD_DIM]
            p = _attn_probs(q_h, k_h)
            o_ref[:, cols] = jnp.dot(p.astype(bf16), v_h, preferred_element_type=f32).astype(bf16)

    return pl.pallas_call(
        body, name="attn_fwd", grid=(S // tm,), in_specs=[_rows(tm, XATTN_WIDTH, 10), _full((M, 2 * XATTN_WIDTH))],
        out_specs=_rows(tm, XATTN_WIDTH), out_shape=jax.ShapeDtypeStruct((S, XATTN_WIDTH), bf16),
        compiler_params=_params())(proj, kv)


def attn_bwd(do, proj, kv, dproj, tm=TM):
    S = proj.shape[0]
    M = kv.shape[0]

    def body(do_ref, q_ref, kv_ref, _, dq_ref, dkv_ref):
        i = pl.program_id(0)

        @pl.when(i == 0)
        def _():
            dkv_ref[...] = jnp.zeros_like(dkv_ref)

        for h in range(XATTN_HEADS):
            cols = slice(h * XATTN_HEAD_DIM, (h + 1) * XATTN_HEAD_DIM)
            vcols = slice(XATTN_WIDTH + h * XATTN_HEAD_DIM, XATTN_WIDTH + (h + 1) * XATTN_HEAD_DIM)
            q_h = q_ref[:, cols].astype(bf16)
            k_h = kv_ref[:, cols]
            v_h = kv_ref[:, vcols]
            do_h = do_ref[:, cols]
            p = _attn_probs(q_h, k_h)
            dp = lax.dot_general(do_h, v_h, NT_DIMS, preferred_element_type=f32)
            ds = (p * (dp - jnp.sum(dp * p, axis=-1, keepdims=True)) * ATTN_SCALE).astype(bf16)
            dq_ref[:, cols] = jnp.dot(ds, k_h, preferred_element_type=f32).astype(bf16)
            dkv_ref[:, cols] += lax.dot_general(ds, q_h, TN_DIMS, preferred_element_type=f32)
            dkv_ref[:, vcols] += lax.dot_general(p.astype(bf16), do_h, TN_DIMS, preferred_element_type=f32)

    return pl.pallas_call(
        body, name="attn_bwd", grid=(S // tm,),
        in_specs=[_rows(tm, XATTN_WIDTH), _rows(tm, XATTN_WIDTH, 10), _full((M, 2 * XATTN_WIDTH)),
                  pl.BlockSpec(memory_space=pl.ANY)],
        out_specs=[_rows(tm, XATTN_WIDTH, 10), _full((M, 2 * XATTN_WIDTH))],
        out_shape=[jax.ShapeDtypeStruct(dproj.shape, bf16), jax.ShapeDtypeStruct((M, 2 * XATTN_WIDTH), f32)],
        input_output_aliases={3: 0},
        compiler_params=_params())(do, proj, kv, dproj)


def _ln_stats(r):
    mu = jnp.mean(r, axis=-1, keepdims=True)
    xc = r - mu
    var = jnp.mean(xc * xc, axis=-1, keepdims=True)
    rstd = lax.rsqrt(var + LN_EPS)
    return xc * rstd, rstd


def _ln_bwd(dy, xhat, rstd, g):
    dxh = dy * g
    return rstd * (dxh - jnp.mean(dxh, axis=-1, keepdims=True) - xhat * jnp.mean(dxh * xhat, axis=-1, keepdims=True))


def _colsum(v):
    return jnp.sum(v, axis=0, keepdims=True)


def _ln2_loss_epi(t2, h1, target, b_down, g, b):
    xhat, rstd = _ln_stats(ALPHA * h1 + t2 + b_down)
    err = xhat * g + b - target
    dout = err * (1.0 / D_MODEL)
    dr = _ln_bwd(dout, xhat, rstd, g)
    return (dr, dr), (0.5 * _colsum(err * err) * (1.0 / D_MODEL), _colsum(dout * xhat), _colsum(dout), _colsum(dr))


def _merge_fwd_pro(g0, g1, g2, y_a, glu_a, glu_b, y_c, b):
    gs = [_sig(g.astype(f32) + b[:, k * D_MODEL:(k + 1) * D_MODEL]) for k, g in enumerate((g0, g1, g2))]
    y_b = glu_a.astype(f32) * _sig(glu_b.astype(f32))
    return gs[0] * y_a.astype(f32) + gs[1] * y_b + gs[2] * y_c.astype(f32)


def _merge_bwd_epi(dm, g0, g1, g2, y_a, glu_a, glu_b, y_c, b):
    ga, sb = glu_a.astype(f32), _sig(glu_b.astype(f32))
    ys = (y_a.astype(f32), ga * sb, y_c.astype(f32))
    gs = [_sig(g.astype(f32) + b[:, k * D_MODEL:(k + 1) * D_MODEL]) for k, g in enumerate((g0, g1, g2))]
    dpre = [dm * ys[k] * gs[k] * (1.0 - gs[k]) for k in range(3)]
    dyb = dm * gs[1]
    dglu = jnp.concatenate([dyb * sb, dyb * ga * sb * (1.0 - sb)], axis=1)
    return ((jnp.concatenate(dpre, axis=1), dm * gs[0], dglu, dm * gs[2]),
            (jnp.concatenate([_colsum(d) for d in dpre], axis=1),))


def _ln1_fwd_epi(t1, x, g, b):
    xhat, _ = _ln_stats(ALPHA * x + t1)
    h = xhat * g + b
    return (t1, h, h), ()


def _ln1_bwd_epi(t3, x, t1, dr2, g):
    xhat, rstd = _ln_stats(ALPHA * x + t1)
    dh = ALPHA * dr2 + t3
    dr = _ln_bwd(dh, xhat, rstd, g)
    return (dr, dr), (_colsum(dh * xhat), _colsum(dh))


def exchange(*rides, name):
    return _pcall(lambda: None, [], name=name, grid=(), in_specs=[], out_specs=[], out_shape=[], rides=rides)[1]


ROW_TILE = 256


def sum_slots(recv, name):
    n, R, C = recv.shape
    tm = min(R, ROW_TILE)

    def body(r_ref, o_ref):
        acc = r_ref[0].astype(f32)
        for k in range(1, n):
            acc = acc + r_ref[k].astype(f32)
        o_ref[...] = acc

    return pl.pallas_call(
        body, name=name, grid=(R // tm,), in_specs=[pl.BlockSpec((n, tm, C), lambda i: (0, i, 0))],
        out_specs=_rows(tm, C), out_shape=jax.ShapeDtypeStruct((R, C), f32), compiler_params=_params())(recv)


def _adamw(w, g, m, v):
    m = ADAM_B1 * m + (1.0 - ADAM_B1) * g
    v = ADAM_B2 * v + (1.0 - ADAM_B2) * (g * g)
    m_hat = m / (1.0 - ADAM_B1 ** ADAM_STEP)
    v_hat = v / (1.0 - ADAM_B2 ** ADAM_STEP)
    delta = -ADAM_LR * (m_hat / (jnp.sqrt(v_hat) + ADAM_EPS) + ADAM_WD * w)
    return delta, m, v


def adam_pair(p, q, w, m, v, name):
    R, C = w.shape
    tm = min(R, ROW_TILE)

    def body(p_ref, q_ref, w_ref, m_ref, v_ref, g_ref, d_ref, nm_ref, nv_ref):
        g = p_ref[...] + q_ref[...]
        g_ref[...] = g
        d_ref[...], nm_ref[...], nv_ref[...] = _adamw(w_ref[...], g, m_ref[...], v_ref[...])

    o = jax.ShapeDtypeStruct((R, C), f32)
    return pl.pallas_call(body, name=name, grid=(R // tm,), in_specs=[_rows(tm, C)] * 5,
                          out_specs=[_rows(tm, C)] * 4, out_shape=[o] * 4, compiler_params=_params())(p, q, w, m, v)


def adam_slots(recv, w, m, v, name):
    n = recv.shape[0]

    def body(r_ref, w_ref, m_ref, v_ref, g_ref, d_ref, nm_ref, nv_ref):
        g = r_ref[0]
        for k in range(1, n):
            g = g + r_ref[k]
        g_ref[...] = g
        d_ref[...], nm_ref[...], nv_ref[...] = _adamw(w_ref[...], g, m_ref[...], v_ref[...])

    o = jax.ShapeDtypeStruct(w.shape, f32)
    return pl.pallas_call(body, name=name, out_shape=[o] * 4, compiler_params=_params())(recv, w, m, v)


BIG = ("w_in", "w_conv_out", "w_glu", "w_xattn_out", "w_kv", "w_out", "w_up", "w_down")
MID = ("w_conv_out", "w_glu", "w_xattn_out", "w_kv", "w_out")
SMALL = ("b_gate", "ssm_lam_re", "ssm_lam_im", "ssm_log_dt", "ssm_b_re", "ssm_b_im", "ssm_c_re", "ssm_c_im", "ssm_d",
         "ln1_g", "ln1_b", "b_up", "b_down", "ln2_g", "ln2_b")
def _pad_flat(a, mult=1024):
    a = a.reshape(-1)
    return jnp.pad(a, (0, (-a.shape[0]) % mult))


def _perm(a):
    S, W = a.shape
    return a.reshape(N_SEG, S // N_SEG, W).transpose(1, 0, 2).reshape(S, W)


def _unperm(a):
    S, W = a.shape
    return a.reshape(S // N_SEG, N_SEG, W).transpose(1, 0, 2).reshape(S, W)


def _state_rows(a):
    return a.transpose(2, 0, 1).reshape(SSM_GROUP, N_STATES)


def _block_diag_b(bt):
    b4 = bt.reshape(SSM_GROUP, 4, 8, SSM_STATE)
    eye = jnp.eye(8, dtype=bt.dtype)
    return jnp.einsum("hcgp,gk->cghkp", b4, eye).reshape(4, 128, 512)


def _block_diag_c(c):
    c4 = c.reshape(4, 8, SSM_GROUP, SSM_STATE)
    eye = jnp.eye(8, dtype=c.dtype)
    return jnp.einsum("cghp,gk->cgpkh", c4, eye).reshape(4, 512, 128)


def _diag_b(acc):
    a = acc.reshape(4, 8, SSM_STATE, 8, SSM_GROUP)
    eye = jnp.eye(8, dtype=acc.dtype)
    return jnp.einsum("cgpkh,gk->hcgp", a, eye).reshape(SSM_GROUP, N_STATES)


def _diag_c(acc):
    a = acc.reshape(4, 8, SSM_GROUP, 8, SSM_STATE)
    eye = jnp.eye(8, dtype=acc.dtype)
    return jnp.einsum("cghkp,gk->cghp", a, eye).reshape(SSM_GROUPS, SSM_GROUP, SSM_STATE)


def kernel(x, mem, w_in, b_gate, conv_w, w_conv_out, ssm_lam_re, ssm_lam_im, ssm_log_dt, ssm_b_re, ssm_b_im, ssm_c_re, ssm_c_im, ssm_d, w_glu, w_kv, w_xattn_out, w_out, ln1_g, ln1_b, w_up, b_up, w_down, b_down, ln2_g, ln2_b, loss_target, m_w_in, m_b_gate, m_conv_w, m_w_conv_out, m_ssm_lam_re, m_ssm_lam_im, m_ssm_log_dt, m_ssm_b_re, m_ssm_b_im, m_ssm_c_re, m_ssm_c_im, m_ssm_d, m_w_glu, m_w_kv, m_w_xattn_out, m_w_out, m_ln1_g, m_ln1_b, m_w_up, m_b_up, m_w_down, m_b_down, m_ln2_g, m_ln2_b, v_w_in, v_b_gate, v_conv_w, v_w_conv_out, v_ssm_lam_re, v_ssm_lam_im, v_ssm_log_dt, v_ssm_b_re, v_ssm_b_im, v_ssm_c_re, v_ssm_c_im, v_ssm_d, v_w_glu, v_w_kv, v_w_xattn_out, v_w_out, v_ln1_g, v_ln1_b, v_w_up, v_b_up, v_w_down, v_b_down, v_ln2_g, v_ln2_b):
    W = dict(w_in=w_in, b_gate=b_gate, conv_w=conv_w, w_conv_out=w_conv_out, ssm_lam_re=ssm_lam_re,
             ssm_lam_im=ssm_lam_im, ssm_log_dt=ssm_log_dt, ssm_b_re=ssm_b_re, ssm_b_im=ssm_b_im, ssm_c_re=ssm_c_re,
             ssm_c_im=ssm_c_im, ssm_d=ssm_d, w_glu=w_glu, w_kv=w_kv, w_xattn_out=w_xattn_out, w_out=w_out,
             ln1_g=ln1_g, ln1_b=ln1_b, w_up=w_up, b_up=b_up, w_down=w_down, b_down=b_down, ln2_g=ln2_g, ln2_b=ln2_b)
    MOM = dict(w_in=m_w_in, b_gate=m_b_gate, conv_w=m_conv_w, w_conv_out=m_w_conv_out, ssm_lam_re=m_ssm_lam_re,
               ssm_lam_im=m_ssm_lam_im, ssm_log_dt=m_ssm_log_dt, ssm_b_re=m_ssm_b_re, ssm_b_im=m_ssm_b_im,
               ssm_c_re=m_ssm_c_re, ssm_c_im=m_ssm_c_im, ssm_d=m_ssm_d, w_glu=m_w_glu, w_kv=m_w_kv,
               w_xattn_out=m_w_xattn_out, w_out=m_w_out, ln1_g=m_ln1_g, ln1_b=m_ln1_b, w_up=m_w_up, b_up=m_b_up,
               w_down=m_w_down, b_down=m_b_down, ln2_g=m_ln2_g, ln2_b=m_ln2_b)
    VEL = dict(w_in=v_w_in, b_gate=v_b_gate, conv_w=v_conv_w, w_conv_out=v_w_conv_out, ssm_lam_re=v_ssm_lam_re,
               ssm_lam_im=v_ssm_lam_im, ssm_log_dt=v_ssm_log_dt, ssm_b_re=v_ssm_b_re, ssm_b_im=v_ssm_b_im,
               ssm_c_re=v_ssm_c_re, ssm_c_im=v_ssm_c_im, ssm_d=v_ssm_d, w_glu=v_w_glu, w_kv=v_w_kv,
               w_xattn_out=v_w_xattn_out, w_out=v_w_out, ln1_g=v_ln1_g, ln1_b=v_ln1_b, w_up=v_w_up, b_up=v_b_up,
               w_down=v_w_down, b_down=v_b_down, ln2_g=v_ln2_g, ln2_b=v_ln2_b)
    names = list(W)
    xy = 2 * lax.axis_index("x") + lax.axis_index("y")

    xs = x[0]
    S = xs.shape[0]
    mems = mem[0]
    tgt = loss_target[0]

    shard_bf = {n: W[n][0].astype(bf16) for n in BIG}

    lr = ssm_lam_re.reshape(1, N_STATES)
    li = ssm_lam_im.reshape(1, N_STATES)
    ldt = jnp.repeat(ssm_log_dt.reshape(SSM_GROUPS), SSM_STATE).reshape(1, N_STATES)
    bt_re, bt_im = _state_rows(ssm_b_re[0]), _state_rows(ssm_b_im[0])
    ar, ai, kr, ki, bbt_r, bbt_i = ssm_prep(lr, li, ldt, bt_re, bt_im)
    bd_r, bd_i = _block_diag_b(bbt_r).astype(bf16), _block_diag_b(bbt_i).astype(bf16)
    cd_r, cd_i = _block_diag_c(ssm_c_re[0]).astype(bf16), _block_diag_c(ssm_c_im[0]).astype(bf16)
    bdt_r, bdt_i = bd_r.transpose(0, 2, 1), bd_i.transpose(0, 2, 1)
    cdt_r, cdt_i = cd_r.transpose(0, 2, 1), cd_i.transpose(0, 2, 1)
    d_skip = ssm_d.reshape(1, SSM_WIDTH)

    mem_bf = mems.astype(bf16)
    u_col = GATE_COLS + 3 * CONV_WIDTH
    me_xy = jnp.reshape(xy, (1,)).astype(jnp.int32)
    proj, x_bf, (win3,) = proj_own(xs, shard_bf["w_in"], me_xy, 4, rides=(("gather2", shard_bf["w_in"]),))
    proj, (wco3, wglu3, wxo3, wkv4, wout4, convw4) = proj_rest(
        x_bf, win3, me_xy, proj, rides=tuple(("gather", shard_bf[n]) for n in MID) + (("gather", conv_w[0]),))
    wkv3 = wkv4.reshape(1, D_MODEL, 2 * XATTN_WIDTH)
    wout3 = wout4.reshape(1, D_MODEL, D_MODEL)
    convw_full = convw4.transpose(1, 0, 2).reshape(3, CONV_WIDTH)
    a_conv = conv_fwd(proj, convw_full)
    y_a = mm_fwd_small(a_conv, wco3, "conv_out", out_dtype=bf16)
    u_perm = _perm(proj[:, u_col:u_col + SSM_WIDTH])
    half = D_MODEL // 2
    (init_r, init_i), (wup_a,) = ssm_scan_fwd(u_perm, bd_r, bd_i, ar, ai,
                                              rides=(("gather", shard_bf["w_up"][:half]),))
    (ysm_perm, ys_perm, st_r, st_i), (wup_b,) = ssm_scan_fwd(
        u_perm, bd_r, bd_i, ar, ai, final=(cd_r, cd_i, d_skip, init_r, init_i),
        rides=(("gather", shard_bf["w_up"][half:]),))
    wup3 = jnp.concatenate([wup_a, wup_b], axis=1)
    y_s = _unperm(ys_perm)
    glu = mm_fwd_small(y_s, wglu3, "glu", out_dtype=bf16)
    kv = mm_fwd_small(mem_bf, wkv3, "kv", out_dtype=bf16)
    o_att = attn_fwd(proj, kv)
    y_c = mm_fwd_small(o_att, wxo3, "xattn_out", out_dtype=bf16)
    merge_rows = ((proj, D_MODEL, 0), (proj, D_MODEL, 1), (proj, D_MODEL, 2), y_a, (glu, D_MODEL, 0),
                  (glu, D_MODEL, 1), y_c)
    merged, t1, h1, h1_bf = mm_rows_fused(
        (_merge_fwd_pro, merge_rows, (b_gate,)), wout3, "merge_w_out_ln1", transposed=False, tm=256,
        epi=_ln1_fwd_epi, extras=(xs,), consts=(ln1_g, ln1_b), outs=(f32, f32, bf16))
    (r_up, hdn), (wdn4,) = mm_fwd(h1_bf, wup3, "w_up", bias=b_up, outs=(bf16, bf16), second=_relu2,
                                  first=lambda v: jnp.maximum(v, 0.0), rides=(("gather", shard_bf["w_down"]),))
    wdn3 = wdn4.reshape(1, D_FF, D_MODEL)

    dr2, dr2_bf, loss_cols, d_ln2_g, d_ln2_b, d_b_down = mm_rows_fused(
        hdn, wdn3, "w_down_ln2_loss", transposed=False, tm=512, epi=_ln2_loss_epi, extras=(h1, tgt),
        consts=(b_down, ln2_g, ln2_b), outs=(f32, bf16), nsums=4)
    part, other = {}, {}
    g_w_down = mm_bwd_w(hdn, dr2_bf, 1, "dw_down", tk=1024).reshape(4, -1, D_MODEL)
    (dup, d_b_up), (recv_dn,) = mm_bwd_x(dr2_bf, wdn3, "dup", tm=512, extras=(r_up,), colsum=True, out_dtype=bf16,
                                         epi=lambda acc, r: acc * (2.0 * r.astype(f32)),
                                         rides=(("scatter", g_w_down),))
    part["w_down"] = sum_slots(recv_dn, "sum_w_down")
    g_w_up = mm_bwd_w(h1_bf, dup, 4, "dw_up")
    dr1, dr1_bf, d_ln1_g, d_ln1_b = mm_rows_fused(
        dup, wup3, "dh1_ln1_bwd", transposed=True, tm=256, epi=_ln1_bwd_epi, extras=(xs, t1, dr2), consts=(ln1_g,),
        outs=(f32, bf16), nsums=2)
    g_w_out = mm_bwd_w(merged, dr1_bf, 1, "dw_out").reshape(4, -1, D_MODEL)
    (dproj, dy_a, dglu, dy_c, d_b_gate), (recv_up,) = mm_rows_fused(
        dr1_bf, wout3, "dmerged_merge_bwd", transposed=True, tm=256, epi=_merge_bwd_epi, extras=merge_rows,
        consts=(b_gate,), outs=((bf16, GATE_COLS, IN_COLS), bf16, (bf16, 2 * D_MODEL, 2 * D_MODEL), bf16),
        sum_widths=(GATE_COLS,), rides=(("scatter", g_w_up),))
    part["w_up"] = sum_slots(recv_up, "sum_w_up")

    g_w_co = mm_bwd_w_small(a_conv, dy_a, 4, "dw_conv_out")
    da_conv = mm_bwd_x(dy_a, wco3, "da_conv", tm=1024)
    dproj, d_conv_w = conv_bwd(da_conv, proj, convw_full, dproj)

    g_w_glu = mm_bwd_w_small(y_s, dglu, 4, "dw_glu")
    dys_perm = _perm(mm_bwd_x(dglu, wglu3, "dy_s", tm=1024))
    linit_r, linit_i = ssm_scan_bwd(dys_perm, ysm_perm, cdt_r, cdt_i, ar, ai)
    (du_perm, dbacc_r, dbacc_i, dcacc_r, dcacc_i, da_r, da_i, d_ssm_d), (recv_co, recv_glu, recv_out) = ssm_scan_bwd(
        dys_perm, ysm_perm, cdt_r, cdt_i, ar, ai,
        final=(u_perm, st_r, st_i, bdt_r, bdt_i, d_skip, linit_r, linit_i),
        rides=(("scatter", g_w_co), ("scatter", g_w_glu), ("scatter", g_w_out)))
    part["w_out"] = sum_slots(recv_out, "sum_w_out")
    part["w_conv_out"] = sum_slots(recv_co, "sum_w_conv_out")
    part["w_glu"] = sum_slots(recv_glu, "sum_w_glu")
    dbt_re, dbt_im, d_lr, d_li, d_ldt_state = ssm_param_bwd(
        _diag_b(dbacc_r), _diag_b(dbacc_i), bt_re, bt_im, kr, ki, ar, ai, lr, li, ldt, da_r, da_i)
    d_log_dt = group_sum(d_ldt_state.reshape(SSM_GROUPS, SSM_STATE))
    d_b_re = dbt_re.reshape(SSM_GROUP, SSM_GROUPS, SSM_STATE).transpose(1, 2, 0)
    d_b_im = dbt_im.reshape(SSM_GROUP, SSM_GROUPS, SSM_STATE).transpose(1, 2, 0)
    d_c_re = _diag_c(dcacc_r)
    d_c_im = -_diag_c(dcacc_i)

    g_w_xo = mm_bwd_w_small(o_att, dy_c, 4, "dw_xattn_out")
    do_att = mm_bwd_x(dy_c, wxo3, "do_att", tm=1024, out_dtype=bf16)
    dproj, dkv = attn_bwd(do_att, proj, kv, dproj)
    g_w_kv = mm_bwd_w(mem_bf, dkv, 1, "dw_kv").reshape(4, -1, D_MODEL)

    dproj = lax.dynamic_update_slice(dproj, _unperm(du_perm), (0, u_col))
    g_small = {"b_gate": d_b_gate, "ssm_lam_re": d_lr, "ssm_lam_im": d_li, "ssm_log_dt": d_log_dt, "ssm_b_re": d_b_re,
               "ssm_b_im": d_b_im, "ssm_c_re": d_c_re, "ssm_c_im": d_c_im, "ssm_d": d_ssm_d, "ln1_g": d_ln1_g,
               "ln1_b": d_ln1_b, "b_up": d_b_up, "b_down": d_b_down, "ln2_g": d_ln2_g, "ln2_b": d_ln2_b}
    small_names = SMALL + ("conv_w",)
    g_small["conv_w"] = d_conv_w
    sizes = {n: (g_small[n].size + 1023) // 1024 * 1024 for n in small_names}
    pack = lambda d: jnp.concatenate([_pad_flat(d[n]) for n in small_names]).reshape(-1, 128)
    early = ("w_down", "w_up", "w_out", "w_conv_out", "w_glu")
    g_w_in, landed = mm_bwd_w(
        x_bf, dproj, 4, "dw_in", rides=(("scatter", g_w_xo), ("scatter", g_w_kv), ("all", pack(g_small)))
        + tuple(("pair", part[n]) for n in early))
    recv_xo, recv_kv, srecv = landed[:3]
    other.update(zip(early, landed[3:]))
    part["w_xattn_out"] = sum_slots(recv_xo, "sum_w_xattn_out")
    part["w_kv"] = sum_slots(recv_kv, "sum_w_kv")
    dx, (recv_in, other["w_xattn_out"], other["w_kv"]) = mm_bwd_x(
        dproj, win3, "dx", extras=(dr1,), epi=lambda acc, d: acc + ALPHA * d,
        rides=(("scatter", g_w_in), ("pair", part["w_xattn_out"]), ("pair", part["w_kv"])))
    part["w_in"] = sum_slots(recv_in, "sum_w_in")
    other["w_in"], = exchange(("pair", part["w_in"]), name="swap_w_in")
    res = [{}, {}, {}, {}]
    for n in BIG:
        for k, r in enumerate(adam_pair(part[n], other[n], W[n][0], MOM[n][0], VEL[n][0], "adam_" + n)):
            res[k][n] = r[None]
    conv_zero = jnp.zeros((3, CONV_WIDTH), f32)
    gs, ds_, ms, vs = adam_slots(srecv, pack({**{n: W[n] for n in SMALL}, "conv_w": conv_zero}),
                                 pack({**{n: MOM[n] for n in SMALL}, "conv_w": conv_zero}),
                                 pack({**{n: VEL[n] for n in SMALL}, "conv_w": conv_zero}), "adam_small")

    def unpack_small(buf):
        flat = buf.reshape(-1)
        out, r = {}, 0
        for n in small_names:
            ref = g_small[n] if n == "conv_w" else W[n]
            out[n] = flat[r:r + ref.size].reshape(ref.shape)
            r += sizes[n]
        return out

    res_s = [unpack_small(b) for b in (gs, ds_, ms, vs)]
    g_conv = lax.dynamic_slice(res_s[0]["conv_w"], (0, xy * 128), (3, 128))
    conv_slots = g_conv.reshape(1, 3, 128)
    cg, cd, cm, cv = adam_slots(conv_slots, conv_w[0], m_conv_w[0], v_conv_w[0], "adam_conv")
    conv_res = [cg, cd, cm, cv]

    loss = lax.psum(jnp.sum(loss_cols), ("x", "y", "c"))
    outs = [loss, dx.reshape(x.shape)]
    for k in range(4):
        for n in names:
            if n == "conv_w":
                outs.append(conv_res[k].reshape(conv_w.shape))
            elif n in BIG:
                outs.append(res[k][n])
            else:
                outs.append(res_s[k][n])
    return tuple(outs)
```

```python
import functools
import math

import jax
import jax.numpy as jnp
from jax import lax
from jax.experimental import pallas as pl
from jax.experimental.pallas import tpu as pltpu

f32 = jnp.float32
bf16 = jnp.bfloat16

D_MODEL = 1024
CONV_WIDTH = 512
SSM_WIDTH = 512
SSM_GROUP = 16
SSM_GROUPS = 32
SSM_STATE = 64
N_STATES = SSM_GROUPS * SSM_STATE
XATTN_HEADS = 4
XATTN_HEAD_DIM = 128
XATTN_WIDTH = 512
D_FF = 4096
GATE_COLS = 3 * D_MODEL
IN_COLS = GATE_COLS + 3 * CONV_WIDTH + SSM_WIDTH + XATTN_WIDTH
ALPHA = 2.0 ** 0.25
LN_EPS = 1e-5
ADAM_LR = 0.001
ADAM_B1 = 0.9
ADAM_B2 = 0.999
ADAM_EPS = 1e-08
ADAM_WD = 0.01
ADAM_STEP = 10

N_SEG = 8
SCAN_ROWS = 512
LANE_STRIP = 512
TM = 512
VMEM_LIMIT = 48 * 1024 * 1024
MESH = pl.DeviceIdType.MESH

NT_DIMS = (((1,), (1,)), ((), ()))
TN_DIMS = (((0,), (0,)), ((), ()))


def _params():
    return pltpu.CompilerParams(vmem_limit_bytes=VMEM_LIMIT)


def _full(shape):
    n = len(shape)
    return pl.BlockSpec(shape, lambda *_: (0,) * n)


def _rows(tm, w, cb=0):
    return pl.BlockSpec((tm, w), lambda i: (i, cb))


def _sig(x):
    return 1.0 / (1.0 + jnp.exp(-x))


HBM_SPEC = pl.BlockSpec(memory_space=pl.ANY)
RIDE_PEERS = {"gather": 3, "gather2": 6, "scatter": 3, "all": 7, "pair": 1}


def _xy_peers(x, y):
    return [(1 - x, y), (x, 1 - y), (1 - x, 1 - y)]


def _ride_copies(kind, src, dst, send_sems, recv_sems, local_sem):
    x, y, c = lax.axis_index("x"), lax.axis_index("y"), lax.axis_index("c")
    me = 2 * x + y
    if kind == "all":
        flips = [(fx, fy, fc) for fx in (0, 1) for fy in (0, 1) for fc in (0, 1)][1:]
        peers = [(x ^ fx, y ^ fy, c ^ fc) for fx, fy, fc in flips]
        slot = lambda p: 4 * p[0] + 2 * p[1] + p[2]
        mine = slot((x, y, c))
    else:
        peers = [(px, py, c) for px, py in _xy_peers(x, y)]
        slot = lambda p: 2 * p[0] + p[1]
        mine = me

    def remote(k, s, d):
        return pltpu.make_async_remote_copy(src_ref=s, dst_ref=d, send_sem=send_sems.at[k], recv_sem=recv_sems.at[k],
                                            device_id=peers[k], device_id_type=MESH)

    if kind == "pair":
        peers = [(x, y, 1 - c)]
        return None, [remote(0, src, dst)], [remote(0, src, dst)]
    if kind == "scatter":
        local = pltpu.make_async_copy(src.at[me], dst.at[me], local_sem)
        sends = [remote(k, src.at[slot(p)], dst.at[me]) for k, p in enumerate(peers)]
        lands = [remote(k, src.at[me], dst.at[slot(p)]) for k, p in enumerate(peers)]
    else:
        local = pltpu.make_async_copy(src, dst.at[mine], local_sem)
        sends = [remote(k, src, dst.at[mine]) for k, p in enumerate(peers)]
        lands = [remote(k, src, dst.at[slot(p)]) for k, p in enumerate(peers)]
    return local, sends, lands


def _ride_shape(kind, src):
    lead = {"gather": (4,), "gather2": (4,), "scatter": (), "all": (8,), "pair": ()}[kind]
    return jax.ShapeDtypeStruct(lead + src.shape, src.dtype)


def _two_level(src, dst, send_sems, recv_sems, local_sem):
    x, y, c = lax.axis_index("x"), lax.axis_index("y"), lax.axis_index("c")
    me = 2 * x + y
    h = src.shape[0] // 2
    mine = pl.ds(pl.multiple_of(c * h, 16), h)
    theirs = pl.ds(pl.multiple_of((1 - c) * h, 16), h)
    peers = _xy_peers(x, y)
    slots = [2 * px + py for px, py in peers]

    def over_ici(k, d):
        return pltpu.make_async_remote_copy(src_ref=src.at[mine], dst_ref=d, send_sem=send_sems.at[k],
                                            recv_sem=recv_sems.at[k], device_id=(*peers[k], c), device_id_type=MESH)

    def over_d2d(k, rows):
        blk = dst.at[slots[k], rows]
        return pltpu.make_async_remote_copy(src_ref=blk, dst_ref=blk, send_sem=send_sems.at[3 + k],
                                            recv_sem=recv_sems.at[3 + k], device_id=(x, y, 1 - c), device_id_type=MESH)

    return (pltpu.make_async_copy(src, dst.at[me], local_sem),
            [over_ici(k, dst.at[me, mine]) for k in range(3)], [over_ici(k, dst.at[slots[k], mine]) for k in range(3)],
            [over_d2d(k, mine) for k in range(3)], [over_d2d(k, theirs) for k in range(3)])


def _pcall(body, args, *, name, grid, in_specs, out_specs, out_shape, scratch_shapes=(), rides=(), aliases=None,
           prefetch=()):
    n_in, n_out, n_scr, nr, npf = len(in_specs), len(out_specs), len(scratch_shapes), len(rides), len(prefetch)
    kinds = [k for k, _ in rides]

    def wrapped(*refs):
        pre, refs = refs[:npf], refs[npf:]
        ins, rsrc = refs[:n_in], refs[n_in:n_in + nr]
        outs = refs[n_in + nr:n_in + nr + n_out]
        rdst = refs[n_in + nr + n_out:n_in + 2 * nr + n_out]
        scr = refs[n_in + 2 * nr + n_out:n_in + 2 * nr + n_out + n_scr]
        sems = refs[n_in + 2 * nr + n_out + n_scr:]

        def start():
            for r, kind in enumerate(kinds):
                if kind == "gather2":
                    local, sends, _, _, _ = _two_level(rsrc[r], rdst[r], *sems[3 * r:3 * r + 3])
                else:
                    local, sends, _ = _ride_copies(kind, rsrc[r], rdst[r], *sems[3 * r:3 * r + 3])
                if local is not None:
                    local.start()
                for cp in sends:
                    cp.start()

        def finish():
            for r, kind in enumerate(kinds):
                if kind == "gather2":
                    local, sends, ici_lands, forwards, lands = _two_level(rsrc[r], rdst[r], *sems[3 * r:3 * r + 3])
                    for k in range(3):
                        ici_lands[k].wait_recv()
                        forwards[k].start()
                    sends = sends + forwards
                else:
                    local, sends, lands = _ride_copies(kind, rsrc[r], rdst[r], *sems[3 * r:3 * r + 3])
                for cp in lands:
                    cp.wait_recv()
                for cp in sends:
                    cp.wait_send()
                if local is not None:
                    local.wait()

        if nr and grid:
            ids = [pl.program_id(a) for a in range(len(grid))]
            first = functools.reduce(jnp.logical_and, [i == 0 for i in ids])
            last = functools.reduce(jnp.logical_and, [i == g - 1 for i, g in zip(ids, grid)])
            pl.when(first)(start)
        elif nr:
            start()
        body(*pre, *ins, *outs, *scr)
        if nr and grid:
            pl.when(last)(finish)
        elif nr:
            finish()

    sems = []
    for kind in kinds:
        n = RIDE_PEERS[kind]
        sems += [pltpu.SemaphoreType.DMA((n,)), pltpu.SemaphoreType.DMA((n,)), pltpu.SemaphoreType.DMA(())]
    all_in = list(in_specs) + [HBM_SPEC] * nr
    all_out = list(out_specs) + [HBM_SPEC] * nr
    all_scratch = list(scratch_shapes) + sems
    if npf:
        how = dict(grid_spec=pltpu.PrefetchScalarGridSpec(num_scalar_prefetch=npf, grid=grid, in_specs=all_in,
                                                          out_specs=all_out, scratch_shapes=all_scratch))
    else:
        how = dict(grid=grid, in_specs=all_in, out_specs=all_out, scratch_shapes=all_scratch)
    res = pl.pallas_call(
        wrapped, name=name, out_shape=list(out_shape) + [_ride_shape(k, s) for k, s in rides],
        input_output_aliases=aliases or {}, compiler_params=_params(), **how)(
            *prefetch, *args, *[s for _, s in rides])
    return list(res[:n_out]), list(res[n_out:])


def _bf(v):
    return v if v.dtype == bf16 else v.astype(bf16)


def mm_fwd_small(a, w3, name, *, tm=1024, out_dtype=f32):
    M, K = a.shape
    J, _, n = w3.shape
    tm = min(tm, M)

    def body(a_ref, w_ref, o_ref):
        av = _bf(a_ref[...])
        for j in range(J):
            o_ref[:, j * n:(j + 1) * n] = jnp.dot(av, w_ref[j], preferred_element_type=f32).astype(out_dtype)

    return pl.pallas_call(
        body, name=name, grid=(M // tm,), in_specs=[_rows(tm, K), _full((J, K, n))], out_specs=_rows(tm, J * n),
        out_shape=jax.ShapeDtypeStruct((M, J * n), out_dtype), compiler_params=_params())(a, w3)


def mm_bwd_w_small(a, dy, J, name, *, tm=2048):
    M, K = a.shape
    n = dy.shape[1] // J
    tm = min(tm, M)
    ns = M // tm

    def body(a_ref, dy_ref, o_ref, acc_ref):
        s = pl.program_id(0)
        part = lax.dot_general(_bf(a_ref[...]), _bf(dy_ref[...]), TN_DIMS, preferred_element_type=f32)

        @pl.when(s == 0)
        def _():
            acc_ref[...] = part

        @pl.when(s > 0)
        def _():
            acc_ref[...] += part

        @pl.when(s == ns - 1)
        def _():
            for j in range(J):
                o_ref[j] = acc_ref[:, j * n:(j + 1) * n].astype(bf16)

    return pl.pallas_call(
        body, name=name, grid=(ns,), in_specs=[_rows(tm, K), _rows(tm, J * n)], out_specs=_full((J, K, n)),
        out_shape=jax.ShapeDtypeStruct((J, K, n), bf16), scratch_shapes=[pltpu.VMEM((K, J * n), f32)],
        compiler_params=_params())(a, dy)


def mm_fwd(a, w3, name, *, tm=1024, tn=None, bias=None, outs=(f32,), first=None, second=None, emit_a=False,
           rides=()):
    M, K = a.shape
    J, _, n = w3.shape
    tm = min(tm, M)
    tn = tn or n
    nl = n // tn
    nb = 0 if bias is None else 1

    def body(*refs):
        a_ref, w_ref = refs[0], refs[1]
        av = _bf(a_ref[...])
        acc = jnp.dot(av, w_ref[0], preferred_element_type=f32)
        if bias is not None:
            acc = acc + refs[2][...]
        refs[2 + nb][...] = (acc if first is None else first(acc)).astype(outs[0])
        if len(outs) > 1:
            refs[3 + nb][...] = second(acc).astype(outs[1])
        if emit_a:
            @pl.when(jnp.logical_and(pl.program_id(0) == 0, pl.program_id(1) == 0))
            def _():
                refs[2 + nb + len(outs)][...] = av

    in_specs = [pl.BlockSpec((tm, K), lambda j, l, i: (i, 0)),
                pl.BlockSpec((1, K, tn), lambda j, l, i: (j, 0, l))]
    args = [a, w3]
    if bias is not None:
        in_specs.append(pl.BlockSpec((1, tn), lambda j, l, i: (0, j * nl + l)))
        args.append(bias)
    out_specs = [pl.BlockSpec((tm, tn), lambda j, l, i: (i, j * nl + l)) for _ in outs]
    out_shape = [jax.ShapeDtypeStruct((M, J * n), dt) for dt in outs]
    if emit_a:
        last = M // tm - 1
        out_specs.append(pl.BlockSpec((tm, K), lambda j, l, i: (jnp.where((j == 0) & (l == 0), i, last), 0)))
        out_shape.append(jax.ShapeDtypeStruct((M, K), bf16))
    res, landed = _pcall(body, args, name=name, grid=(J, nl, M // tm), in_specs=in_specs, out_specs=out_specs,
                         out_shape=out_shape, rides=rides)
    res = res if len(res) > 1 else res[0]
    return (res, landed) if rides else res


def proj_own(x, w_own, me, n_blocks, *, tm=1024, rides=()):
    M, K = x.shape
    n = w_own.shape[1]

    def body(me_ref, x_ref, w_ref, o_ref, xb_ref):
        av = _bf(x_ref[...])
        xb_ref[...] = av
        o_ref[...] = jnp.dot(av, w_ref[...], preferred_element_type=f32).astype(bf16)

    (proj, x_bf), landed = _pcall(
        body, [x, w_own], name="proj_own", grid=(M // tm,), prefetch=(me,),
        in_specs=[pl.BlockSpec((tm, K), lambda i, me_ref: (i, 0)), pl.BlockSpec((K, n), lambda i, me_ref: (0, 0))],
        out_specs=[pl.BlockSpec((tm, n), lambda i, me_ref: (i, me_ref[0])),
                   pl.BlockSpec((tm, K), lambda i, me_ref: (i, 0))],
        out_shape=[jax.ShapeDtypeStruct((M, n_blocks * n), bf16), jax.ShapeDtypeStruct((M, K), bf16)], rides=rides)
    return proj, x_bf, landed


def proj_rest(x_bf, w3, me, proj, *, tm=1024, rides=()):
    M, K = x_bf.shape
    J, _, n = w3.shape

    def body(me_ref, x_ref, w_ref, _, o_ref):
        o_ref[...] = jnp.dot(x_ref[...], w_ref[0], preferred_element_type=f32).astype(bf16)

    (proj,), landed = _pcall(
        body, [x_bf, w3, proj], name="proj_rest", grid=(J - 1, M // tm), prefetch=(me,),
        in_specs=[pl.BlockSpec((tm, K), lambda k, i, me_ref: (i, 0)),
                  pl.BlockSpec((1, K, n), lambda k, i, me_ref: (me_ref[0] ^ (k + 1), 0, 0)), HBM_SPEC],
        out_specs=[pl.BlockSpec((tm, n), lambda k, i, me_ref: (i, me_ref[0] ^ (k + 1)))],
        out_shape=[jax.ShapeDtypeStruct(proj.shape, bf16)], aliases={3: 0}, rides=rides)
    return proj, landed


def mm_bwd_x(dy, w3, name, *, tm=TM, epi=None, extras=(), colsum=False, out_dtype=f32, rides=()):
    M = dy.shape[0]
    J, K, n = w3.shape
    tm = min(tm, M)
    nex = len(extras)

    def body(*refs):
        dy_ref, w_hbm = refs[0], refs[1]
        ex = refs[2:2 + nex]
        o_ref = refs[2 + nex]
        w_ref = refs[-1]
        i = pl.program_id(0)

        @pl.when(i == 0)
        def _():
            pltpu.sync_copy(w_hbm, w_ref)

        acc = None
        for j in range(J):
            part = lax.dot_general(_bf(dy_ref[:, j * n:(j + 1) * n]), w_ref[j], NT_DIMS, preferred_element_type=f32)
            acc = part if acc is None else acc + part
        if epi is not None:
            acc = epi(acc, *[e[...] for e in ex])
        o_ref[...] = acc.astype(out_dtype)
        if colsum:
            s_ref = refs[3 + nex]
            cs = jnp.sum(acc, axis=0, keepdims=True)

            @pl.when(i == 0)
            def _():
                s_ref[...] = cs

            @pl.when(i > 0)
            def _():
                s_ref[...] += cs

    in_specs = [pl.BlockSpec((tm, J * n), lambda i: (i, 0)), pl.BlockSpec(memory_space=pl.ANY)]
    in_specs += [pl.BlockSpec((tm, K), lambda i: (i, 0)) for _ in extras]
    out_specs = [pl.BlockSpec((tm, K), lambda i: (i, 0))]
    out_shape = [jax.ShapeDtypeStruct((M, K), out_dtype)]
    if colsum:
        out_specs.append(pl.BlockSpec((1, K), lambda i: (0, 0)))
        out_shape.append(jax.ShapeDtypeStruct((1, K), f32))
    res, landed = _pcall(body, [dy, w3, *extras], name=name, grid=(M // tm,), in_specs=in_specs, out_specs=out_specs,
                         out_shape=out_shape, scratch_shapes=[pltpu.VMEM((J, K, n), bf16)], rides=rides)
    res = res if colsum else res[0]
    return (res, landed) if rides else res


def mm_rows_fused(a, w3, name, *, transposed, tm, epi, extras=(), consts=(), outs=(f32,), nsums=0, sum_widths=None,
                  rides=()):
    pro = a[0] if isinstance(a, tuple) else None
    J, K, n = w3.shape
    W = K if transposed else n
    ka = J * n if transposed else K
    if pro is not None:
        a_rows = [e if isinstance(e, tuple) else (e, ka, 0) for e in a[1]]
        a_consts = list(a[2])
    else:
        a_rows, a_consts = [(a, ka, 0)], []
    M = a_rows[0][0].shape[0]
    tm = min(tm, M)
    na = len(a_rows) + len(a_consts)
    nex, nco = len(extras), len(consts)
    if pro is not None:
        outs = (bf16 if ka == W else (bf16, ka, ka),) + tuple(outs)
    nout = len(outs)
    nsums = len(sum_widths) if sum_widths is not None else nsums
    out_dtypes = [o[0] if isinstance(o, tuple) else o for o in outs]

    def body(*refs):
        a_refs, w_hbm = refs[:na], refs[na]
        ex = refs[na + 1:na + 1 + nex]
        co = refs[na + 1 + nex:na + 1 + nex + nco]
        o_refs = refs[na + 1 + nex + nco:na + 1 + nex + nco + nout]
        s_refs = refs[na + 1 + nex + nco + nout:na + 1 + nex + nco + nout + nsums]
        w_ref = refs[-1]
        i = pl.program_id(0)

        @pl.when(i == 0)
        def _():
            pltpu.sync_copy(w_hbm, w_ref)

        av = _bf(a_refs[0][...] if pro is None else pro(*[r[...] for r in a_refs]))
        if transposed:
            acc = None
            for j in range(J):
                part = lax.dot_general(av[:, j * n:(j + 1) * n], w_ref[j], NT_DIMS, preferred_element_type=f32)
                acc = part if acc is None else acc + part
        else:
            acc = jnp.dot(av, w_ref[0], preferred_element_type=f32)
        rows, sums = epi(acc, *[e[...] for e in ex], *[c[...] for c in co])
        if pro is not None:
            rows = (av,) + tuple(rows)
        for o_ref, v, dt in zip(o_refs, rows, out_dtypes):
            o_ref[...] = v.astype(dt)
        for s_ref, v in zip(s_refs, sums):
            @pl.when(i == 0)
            def _(s_ref=s_ref, v=v):
                s_ref[...] = v

            @pl.when(i > 0)
            def _(s_ref=s_ref, v=v):
                s_ref[...] += v

    extras = [e if isinstance(e, tuple) else (e, W, 0) for e in extras]
    outs_full = [o if isinstance(o, tuple) else (o, W, W) for o in outs]
    widths = list(sum_widths) if sum_widths is not None else [W] * nsums
    in_specs = [_rows(tm, w, cb) for _, w, cb in a_rows] + [_full(c.shape) for c in a_consts]
    in_specs += [pl.BlockSpec(memory_space=pl.ANY)]
    in_specs += [_rows(tm, w, cb) for _, w, cb in extras] + [_full(c.shape) for c in consts]
    out_specs = [_rows(tm, w) for _, w, _ in outs_full] + [_full((1, w)) for w in widths]
    out_shape = ([jax.ShapeDtypeStruct((M, cols), dt) for dt, _, cols in outs_full]
                 + [jax.ShapeDtypeStruct((1, w), f32) for w in widths])
    args = [e for e, _, _ in a_rows] + a_consts + [w3] + [e for e, _, _ in extras] + list(consts)
    res, landed = _pcall(body, args, name=name, grid=(M // tm,), in_specs=in_specs, out_specs=out_specs,
                         out_shape=out_shape, scratch_shapes=[pltpu.VMEM((J, K, n), bf16)], rides=rides)
    return (res, landed) if rides else res


def mm_bwd_w(a, dy, J, name, *, tm=2048, tn=None, tk=None, rides=()):
    M, K = a.shape
    n = dy.shape[1] // J
    tm = min(tm, M)
    tn = tn or n
    tk = tk or K
    nl = n // tn
    nk = K // tk
    ns = M // tm

    def body(a_ref, dy_ref, o_ref, acc_ref):
        s = pl.program_id(3)
        part = lax.dot_general(_bf(a_ref[...]), _bf(dy_ref[...]), TN_DIMS, preferred_element_type=f32)

        @pl.when(s == 0)
        def _():
            acc_ref[...] = part

        @pl.when(s > 0)
        def _():
            acc_ref[...] += part

        @pl.when(s == ns - 1)
        def _():
            o_ref[0] = acc_ref[...].astype(bf16)

    res, landed = _pcall(
        body, [a, dy], name=name, grid=(J, nl, nk, ns),
        in_specs=[pl.BlockSpec((tm, tk), lambda j, l, k, s: (s, k)),
                  pl.BlockSpec((tm, tn), lambda j, l, k, s: (s, j * nl + l))],
        out_specs=[pl.BlockSpec((1, tk, tn), lambda j, l, k, s: (j, k, l))],
        out_shape=[jax.ShapeDtypeStruct((J, K, n), bf16)],
        scratch_shapes=[pltpu.VMEM((tk, tn), f32)], rides=rides)
    return (res[0], landed) if rides else res[0]


def _relu2(v):
    r = jnp.maximum(v, 0.0)
    return r * r


HALO = 16


def _shift_down(z, k, halo):
    r = lax.broadcasted_iota(jnp.int32, z.shape, 0)
    y = pltpu.roll(z, k, 0)
    for q in range(k):
        y = jnp.where(r == q, halo[HALO - k + q:HALO - k + q + 1, :], y)
    return y


def _shift_up(z, k, halo):
    tm = z.shape[0]
    r = lax.broadcasted_iota(jnp.int32, z.shape, 0)
    y = pltpu.roll(z, tm - k, 0)
    for q in range(k):
        y = jnp.where(r == tm - k + q, halo[q:q + 1, :], y)
    return y


def _prev_halo(tm, cb):
    return pl.BlockSpec((HALO, CONV_WIDTH), lambda i: (jnp.maximum(i * (tm // HALO) - 1, 0), cb))


def _next_halo(tm, cb, nblk):
    return pl.BlockSpec((HALO, CONV_WIDTH), lambda i: (jnp.minimum((i + 1) * (tm // HALO), nblk - 1), cb))


def _f32(ref):
    return ref[...].astype(f32)


def conv_fwd(proj, conv_w, wco3, tm=TM):
    S = proj.shape[0]
    J, _, n = wco3.shape

    def body(cb_ref, cc_ref, ch_ref, cch_ref, chh_ref, w_ref, wo_ref, a_ref, y_ref):
        i = pl.program_id(0)
        z = _f32(cc_ref) * _f32(ch_ref)
        zh = jnp.where(i == 0, 0.0, _f32(cch_ref) * _f32(chh_ref))
        w = w_ref[...]
        dwz = w[0:1, :] * _shift_down(z, 2, zh) + w[1:2, :] * _shift_down(z, 1, zh) + w[2:3, :] * z
        a = (_f32(cb_ref) * dwz).astype(bf16)
        a_ref[...] = a
        for j in range(J):
            y_ref[:, j * n:(j + 1) * n] = jnp.dot(a, wo_ref[j], preferred_element_type=f32).astype(bf16)

    return pl.pallas_call(
        body, name="conv_fwd", grid=(S // tm,),
        in_specs=[_rows(tm, CONV_WIDTH, 6), _rows(tm, CONV_WIDTH, 7), _rows(tm, CONV_WIDTH, 8),
                  _prev_halo(tm, 7), _prev_halo(tm, 8), _full((3, CONV_WIDTH)), _full(wco3.shape)],
        out_specs=[_rows(tm, CONV_WIDTH), _rows(tm, J * n)],
        out_shape=[jax.ShapeDtypeStruct((S, CONV_WIDTH), bf16), jax.ShapeDtypeStruct((S, J * n), bf16)],
        compiler_params=_params())(proj, proj, proj, proj, proj, conv_w, wco3)


def conv_bwd(da, proj, conv_w, dproj, tm=TM):
    S = proj.shape[0]
    nt = S // tm

    def body(da_ref, cb_ref, cc_ref, ch_ref, cch_ref, chh_ref, dan_ref, cbn_ref, w_ref, _, o_ref, dw_ref):
        i = pl.program_id(0)
        cc, ch, cb, dav = _f32(cc_ref), _f32(ch_ref), _f32(cb_ref), da_ref[...]
        z = cc * ch
        zh = jnp.where(i == 0, 0.0, _f32(cch_ref) * _f32(chh_ref))
        w = w_ref[...]
        z1 = _shift_down(z, 1, zh)
        z2 = _shift_down(z, 2, zh)
        dwz = w[0:1, :] * z2 + w[1:2, :] * z1 + w[2:3, :] * z
        g = dav * cb
        gn = jnp.where(i == nt - 1, 0.0, dan_ref[...] * _f32(cbn_ref))
        dz = w[2:3, :] * g + w[1:2, :] * _shift_up(g, 1, gn) + w[0:1, :] * _shift_up(g, 2, gn)
        o_ref[:, 0:CONV_WIDTH] = (dav * dwz).astype(bf16)
        o_ref[:, CONV_WIDTH:2 * CONV_WIDTH] = (dz * ch).astype(bf16)
        o_ref[:, 2 * CONV_WIDTH:3 * CONV_WIDTH] = (dz * cc).astype(bf16)

        @pl.when(i == 0)
        def _():
            dw_ref[...] = jnp.zeros_like(dw_ref)

        dw_ref[0:1, :] += jnp.sum(g * z2, axis=0, keepdims=True)
        dw_ref[1:2, :] += jnp.sum(g * z1, axis=0, keepdims=True)
        dw_ref[2:3, :] += jnp.sum(g * z, axis=0, keepdims=True)

    return pl.pallas_call(
        body, name="conv_bwd", grid=(nt,),
        in_specs=[_rows(tm, CONV_WIDTH), _rows(tm, CONV_WIDTH, 6), _rows(tm, CONV_WIDTH, 7), _rows(tm, CONV_WIDTH, 8),
                  _prev_halo(tm, 7), _prev_halo(tm, 8),
                  _next_halo(tm, 0, S // HALO), _next_halo(tm, 6, S // HALO), _full((3, CONV_WIDTH)),
                  pl.BlockSpec(memory_space=pl.ANY)],
        out_specs=[_rows(tm, 3 * CONV_WIDTH, GATE_COLS // (3 * CONV_WIDTH)), _full((3, CONV_WIDTH))],
        out_shape=[jax.ShapeDtypeStruct(dproj.shape, bf16), jax.ShapeDtypeStruct((3, CONV_WIDTH), f32)],
        input_output_aliases={9: 0},
        compiler_params=_params())(da, proj, proj, proj, proj, proj, da, proj, conv_w, dproj)


def _cmul(ar, ai, br, bi):
    return ar * br - ai * bi, ar * bi + ai * br


def ssm_prep(lr, li, ldt, bt_re, bt_im):
    def body(lr_ref, li_ref, ldt_ref, br_ref, bi_ref, ar_ref, ai_ref, kr_ref, ki_ref, bbr_ref, bbi_ref):
        lrv, liv = lr_ref[...], li_ref[...]
        dt = jnp.exp(ldt_ref[...])
        mag = jnp.exp(lrv * dt)
        ar = mag * jnp.cos(liv * dt)
        ai = mag * jnp.sin(liv * dt)
        den = lrv * lrv + liv * liv
        nr = ar - 1.0
        kr = (nr * lrv + ai * liv) / den
        ki = (ai * lrv - nr * liv) / den
        ar_ref[...], ai_ref[...], kr_ref[...], ki_ref[...] = ar, ai, kr, ki
        bbr_ref[...] = kr * br_ref[...] - ki * bi_ref[...]
        bbi_ref[...] = kr * bi_ref[...] + ki * br_ref[...]

    v = jax.ShapeDtypeStruct((1, N_STATES), f32)
    m = jax.ShapeDtypeStruct((SSM_GROUP, N_STATES), f32)
    return pl.pallas_call(body, name="ssm_prep", out_shape=[v, v, v, v, m, m])(lr, li, ldt, bt_re, bt_im)


def _pow_segment(ar, ai, seg_len):
    pr, pi = ar, ai
    for _ in range(int(math.log2(seg_len))):
        pr, pi = _cmul(pr, pi, pr, pi)
    return pr, pi


GELU_K = math.sqrt(2.0 / math.pi)
GELU_C = 0.044715


def _gelu(v):
    return 0.5 * v * (1.0 + jnp.tanh(GELU_K * (v + GELU_C * v * v * v)))


def _gelu_grad(v):
    t = jnp.tanh(GELU_K * (v + GELU_C * v * v * v))
    return 0.5 * (1.0 + t) + 0.5 * v * (1.0 - t * t) * GELU_K * (1.0 + 3.0 * GELU_C * v * v)


def ssm_scan_fwd(u_perm, bd_r, bd_i, ar, ai, *, final=None, rides=()):
    S = u_perm.shape[0]
    R = SCAN_ROWS
    nblk = S // R
    seg_len = S // N_SEG
    nstrip = N_STATES // LANE_STRIP
    store = final is not None

    def body(*refs):
        if store:
            (u_ref, bdr_ref, bdi_ref, ar_ref, ai_ref, cdr_ref, cdi_ref, d_ref, ir_ref, ii_ref,
             y_ref, ys_ref, sr_ref, si_ref, bur, bui, car, cai) = refs
        else:
            u_ref, bdr_ref, bdi_ref, ar_ref, ai_ref, or_ref, oi_ref, bur, bui, car, cai = refs
        i = pl.program_id(0)
        ub = _bf(u_ref[...])
        u = ub.astype(f32)
        for cb in range(4):
            us = ub[:, cb * 128:(cb + 1) * 128]
            bur[:, cb * 512:(cb + 1) * 512] = jnp.dot(us, bdr_ref[cb], preferred_element_type=f32)
            bui[:, cb * 512:(cb + 1) * 512] = jnp.dot(us, bdi_ref[cb], preferred_element_type=f32)

        @pl.when(i == 0)
        def _():
            if store:
                car[...] = ir_ref[...]
                cai[...] = ii_ref[...]
            else:
                car[...] = jnp.zeros_like(car)
                cai[...] = jnp.zeros_like(cai)

        for ls in range(nstrip):
            lanes = pl.ds(ls * LANE_STRIP, LANE_STRIP)
            a_r = jnp.broadcast_to(ar_ref[:, lanes], (N_SEG, LANE_STRIP))
            a_i = jnp.broadcast_to(ai_ref[:, lanes], (N_SEG, LANE_STRIP))

            def step(t, carry, lanes=lanes, a_r=a_r, a_i=a_i):
                s_r, s_i = carry
                row = pl.multiple_of(t * 8, 8)
                n_r = a_r * s_r - a_i * s_i + bur[pl.ds(row, 8), lanes]
                n_i = a_r * s_i + a_i * s_r + bui[pl.ds(row, 8), lanes]
                if store:
                    sr_ref[pl.ds(row, 8), lanes] = n_r
                    si_ref[pl.ds(row, 8), lanes] = n_i
                return n_r, n_i

            e_r, e_i = lax.fori_loop(0, R // 8, step, (car[:, lanes], cai[:, lanes]), unroll=True)
            car[:, lanes] = e_r
            cai[:, lanes] = e_i

        if store:
            for cb in range(4):
                st_r = sr_ref[:, cb * 512:(cb + 1) * 512].astype(bf16)
                st_i = si_ref[:, cb * 512:(cb + 1) * 512].astype(bf16)
                y = (jnp.dot(st_r, cdr_ref[cb], preferred_element_type=f32)
                     - jnp.dot(st_i, cdi_ref[cb], preferred_element_type=f32))
                cols = slice(cb * 128, (cb + 1) * 128)
                y = y + d_ref[:, cols] * u[:, cols]
                y_ref[:, cols] = y
                ys_ref[:, cols] = _gelu(y).astype(bf16)
        else:
            @pl.when(i == nblk - 1)
            def _():
                p_r, p_i = _pow_segment(ar_ref[...], ai_ref[...], seg_len)
                t_r, t_i = car[0:1, :], cai[0:1, :]
                or_ref[0:1, :] = jnp.zeros((1, N_STATES), f32)
                oi_ref[0:1, :] = jnp.zeros((1, N_STATES), f32)
                for j in range(1, N_SEG):
                    or_ref[j:j + 1, :] = t_r
                    oi_ref[j:j + 1, :] = t_i
                    m_r, m_i = _cmul(p_r, p_i, t_r, t_i)
                    t_r, t_i = car[j:j + 1, :] + m_r, cai[j:j + 1, :] + m_i

    blk = lambda w: pl.BlockSpec((R, w), lambda i: (i, 0))
    in_specs = [blk(SSM_WIDTH), _full((4, 128, 512)), _full((4, 128, 512)), _full((1, N_STATES)), _full((1, N_STATES))]
    args = [u_perm, bd_r, bd_i, ar, ai]
    scratch = [pltpu.VMEM((R, N_STATES), f32), pltpu.VMEM((R, N_STATES), f32),
               pltpu.VMEM((N_SEG, N_STATES), f32), pltpu.VMEM((N_SEG, N_STATES), f32)]
    if store:
        in_specs += [_full((4, 512, 128)), _full((4, 512, 128)), _full((1, SSM_WIDTH)),
                     _full((N_SEG, N_STATES)), _full((N_SEG, N_STATES))]
        args += list(final)
        out_specs = [blk(SSM_WIDTH), blk(SSM_WIDTH), blk(N_STATES), blk(N_STATES)]
        out_shape = [jax.ShapeDtypeStruct((S, SSM_WIDTH), f32), jax.ShapeDtypeStruct((S, SSM_WIDTH), bf16),
                     jax.ShapeDtypeStruct((S, N_STATES), f32), jax.ShapeDtypeStruct((S, N_STATES), f32)]
        name = "ssm_scan_fwd"
    else:
        out_specs = [_full((N_SEG, N_STATES)), _full((N_SEG, N_STATES))]
        out_shape = [jax.ShapeDtypeStruct((N_SEG, N_STATES), f32)] * 2
        name = "ssm_scan_fwd_carry"
    res, landed = _pcall(body, args, name=name, grid=(nblk,), in_specs=in_specs, out_specs=out_specs,
                         out_shape=out_shape, scratch_shapes=scratch, rides=rides)
    return (res, landed) if rides else res


def ssm_scan_bwd(dys_perm, y_perm, cdt_r, cdt_i, ar, ai, *, final=None, rides=()):
    S = dys_perm.shape[0]
    R = SCAN_ROWS
    nblk = S // R
    seg_len = S // N_SEG
    store = final is not None
    strip = LANE_STRIP // 2 if store else LANE_STRIP
    nstrip = N_STATES // strip

    def body(*refs):
        if store:
            (dys_ref, y_ref, cdr_ref, cdi_ref, ar_ref, ai_ref, u_ref, sr_ref, si_ref, bdr_ref, bdi_ref, d_ref, ir_ref,
             ii_ref, du_ref, dbr_ref, dbi_ref, dcr_ref, dci_ref, dar_ref, dai_ref, dd_ref, dsr, dsi, lmr, lmi, car,
             cai) = refs
        else:
            dys_ref, y_ref, cdr_ref, cdi_ref, ar_ref, ai_ref, or_ref, oi_ref, dsr, dsi, car, cai = refs
        i = pl.program_id(0)
        dy = dys_ref[...] * _gelu_grad(y_ref[...])
        dyb = dy.astype(bf16)
        for cb in range(4):
            ds_ = dyb[:, cb * 128:(cb + 1) * 128]
            dsr[:, cb * 512:(cb + 1) * 512] = jnp.dot(ds_, cdr_ref[cb], preferred_element_type=f32)
            dsi[:, cb * 512:(cb + 1) * 512] = -jnp.dot(ds_, cdi_ref[cb], preferred_element_type=f32)

        @pl.when(i == 0)
        def _():
            if store:
                car[...] = ir_ref[...]
                cai[...] = ii_ref[...]
                dar_ref[...] = jnp.zeros_like(dar_ref)
                dai_ref[...] = jnp.zeros_like(dai_ref)
                dbr_ref[...] = jnp.zeros_like(dbr_ref)
                dbi_ref[...] = jnp.zeros_like(dbi_ref)
                dcr_ref[...] = jnp.zeros_like(dcr_ref)
                dci_ref[...] = jnp.zeros_like(dci_ref)
                dd_ref[...] = jnp.zeros_like(dd_ref)
            else:
                car[...] = jnp.zeros_like(car)
                cai[...] = jnp.zeros_like(cai)

        for ls in range(nstrip):
            lanes = pl.ds(ls * strip, strip)
            a_r = jnp.broadcast_to(ar_ref[:, lanes], (N_SEG, strip))
            a_i = jnp.broadcast_to(ai_ref[:, lanes], (N_SEG, strip))
            zero = jnp.zeros((N_SEG, strip), f32)

            def step(k, carry, lanes=lanes, a_r=a_r, a_i=a_i):
                l_r, l_i, g_r, g_i = carry
                row = pl.multiple_of((R // 8 - 1 - k) * 8, 8)
                if store:
                    s_r = sr_ref[pl.ds(row, 8), lanes]
                    s_i = si_ref[pl.ds(row, 8), lanes]
                    g_r = g_r + l_r * s_r + l_i * s_i
                    g_i = g_i + l_i * s_r - l_r * s_i
                n_r = dsr[pl.ds(row, 8), lanes] + a_r * l_r + a_i * l_i
                n_i = dsi[pl.ds(row, 8), lanes] + a_r * l_i - a_i * l_r
                if store:
                    lmr[pl.ds(row, 8), lanes] = n_r
                    lmi[pl.ds(row, 8), lanes] = n_i
                return n_r, n_i, g_r, g_i

            e_r, e_i, g_r, g_i = lax.fori_loop(0, R // 8, step, (car[:, lanes], cai[:, lanes], zero, zero),
                                               unroll=True)
            car[:, lanes] = e_r
            cai[:, lanes] = e_i
            if store:
                dar_ref[:, lanes] += g_r
                dai_ref[:, lanes] += g_i

        if store:
            ub = _bf(u_ref[...])
            u = ub.astype(f32)
            for cb in range(4):
                cols = slice(cb * 128, (cb + 1) * 128)
                st = slice(cb * 512, (cb + 1) * 512)
                l_r = lmr[:, st].astype(bf16)
                l_i = lmi[:, st].astype(bf16)
                du = (jnp.dot(l_r, bdr_ref[cb], preferred_element_type=f32)
                      + jnp.dot(l_i, bdi_ref[cb], preferred_element_type=f32))
                du_ref[:, cols] = (du + d_ref[:, cols] * dy[:, cols]).astype(bf16)
                dbr_ref[cb] += lax.dot_general(l_r, ub[:, cols], TN_DIMS, preferred_element_type=f32)
                dbi_ref[cb] += lax.dot_general(l_i, ub[:, cols], TN_DIMS, preferred_element_type=f32)
                dcr_ref[cb] += lax.dot_general(dyb[:, cols], sr_ref[:, st].astype(bf16), TN_DIMS,
                                               preferred_element_type=f32)
                dci_ref[cb] += lax.dot_general(dyb[:, cols], si_ref[:, st].astype(bf16), TN_DIMS,
                                               preferred_element_type=f32)
            dd_ref[...] += jnp.sum(dy * u, axis=0, keepdims=True)
        else:
            @pl.when(i == nblk - 1)
            def _():
                p_r, p_i = _pow_segment(ar_ref[...], ai_ref[...], seg_len)
                p_i = -p_i
                t_r, t_i = car[N_SEG - 1:N_SEG, :], cai[N_SEG - 1:N_SEG, :]
                or_ref[N_SEG - 1:N_SEG, :] = jnp.zeros((1, N_STATES), f32)
                oi_ref[N_SEG - 1:N_SEG, :] = jnp.zeros((1, N_STATES), f32)
                for j in range(N_SEG - 2, -1, -1):
                    or_ref[j:j + 1, :] = t_r
                    oi_ref[j:j + 1, :] = t_i
                    m_r, m_i = _cmul(p_r, p_i, t_r, t_i)
                    t_r, t_i = car[j:j + 1, :] + m_r, cai[j:j + 1, :] + m_i

    blk = lambda w: pl.BlockSpec((R, w), lambda i: (nblk - 1 - i, 0))
    in_specs = [blk(SSM_WIDTH), blk(SSM_WIDTH), _full((4, 128, 512)), _full((4, 128, 512)), _full((1, N_STATES)),
                _full((1, N_STATES))]
    args = [dys_perm, y_perm, cdt_r, cdt_i, ar, ai]
    seg = jax.ShapeDtypeStruct((N_SEG, N_STATES), f32)
    if store:
        in_specs += [blk(SSM_WIDTH), blk(N_STATES), blk(N_STATES), _full((4, 512, 128)), _full((4, 512, 128)),
                     _full((1, SSM_WIDTH)), _full((N_SEG, N_STATES)), _full((N_SEG, N_STATES))]
        args += list(final)
        out_specs = [blk(SSM_WIDTH), _full((4, 512, 128)), _full((4, 512, 128)), _full((4, 128, 512)),
                     _full((4, 128, 512)), _full((N_SEG, N_STATES)), _full((N_SEG, N_STATES)), _full((1, SSM_WIDTH))]
        b_acc = jax.ShapeDtypeStruct((4, 512, 128), f32)
        c_acc = jax.ShapeDtypeStruct((4, 128, 512), f32)
        out_shape = [jax.ShapeDtypeStruct((S, SSM_WIDTH), bf16), b_acc, b_acc, c_acc, c_acc, seg, seg,
                     jax.ShapeDtypeStruct((1, SSM_WIDTH), f32)]
        scratch = [pltpu.VMEM((R, N_STATES), f32)] * 4 + [pltpu.VMEM((N_SEG, N_STATES), f32)] * 2
        name = "ssm_scan_bwd"
    else:
        out_specs = [_full((N_SEG, N_STATES)), _full((N_SEG, N_STATES))]
        out_shape = [seg, seg]
        scratch = [pltpu.VMEM((R, N_STATES), f32)] * 2 + [pltpu.VMEM((N_SEG, N_STATES), f32)] * 2
        name = "ssm_scan_bwd_carry"
    res, landed = _pcall(body, args, name=name, grid=(nblk,), in_specs=in_specs, out_specs=out_specs,
                         out_shape=out_shape, scratch_shapes=scratch, rides=rides)
    return (res, landed) if rides else res


def ssm_param_bwd(dbb_r, dbb_i, bt_re, bt_im, kr, ki, ar, ai, lr, li, ldt, da_r, da_i):
    def body(dbr_ref, dbi_ref, br_ref, bi_ref, kr_ref, ki_ref, ar_ref, ai_ref, lr_ref, li_ref, ldt_ref, dar_ref,
             dai_ref, obr_ref, obi_ref, olr_ref, oli_ref, odt_ref):
        dbr, dbi, b_r, b_i = dbr_ref[...], dbi_ref[...], br_ref[...], bi_ref[...]
        k_r, k_i, a_r, a_i = kr_ref[...], ki_ref[...], ar_ref[...], ai_ref[...]
        l_r, l_i = lr_ref[...], li_ref[...]
        dt = jnp.exp(ldt_ref[...])
        obr_ref[...] = k_r * dbr + k_i * dbi
        obi_ref[...] = k_r * dbi - k_i * dbr
        gk_r = jnp.sum(dbr * b_r + dbi * b_i, axis=0, keepdims=True)
        gk_i = jnp.sum(dbi * b_r - dbr * b_i, axis=0, keepdims=True)
        ga_r = jnp.sum(dar_ref[...], axis=0, keepdims=True)
        ga_i = jnp.sum(dai_ref[...], axis=0, keepdims=True)
        den = l_r * l_r + l_i * l_i
        c_r, c_i = l_r / den, l_i / den
        m_r, m_i = _cmul(c_r, c_i, gk_r, gk_i)
        g_r, g_i = ga_r + m_r, ga_i + m_i
        t1_r, t1_i = _cmul(dt * a_r, -dt * a_i, g_r, g_i)
        q_r, q_i = _cmul(k_r, k_i, c_r, -c_i)
        t2_r, t2_i = _cmul(-q_r, q_i, gk_r, gk_i)
        olr_ref[...] = t1_r + t2_r
        oli_ref[...] = t1_i + t2_i
        w_r, w_i = _cmul(l_r, l_i, a_r, a_i)
        odt_ref[...] = dt * (w_r * g_r + w_i * g_i)

    v = jax.ShapeDtypeStruct((1, N_STATES), f32)
    m = jax.ShapeDtypeStruct((SSM_GROUP, N_STATES), f32)
    return pl.pallas_call(body, name="ssm_param_bwd", out_shape=[m, m, v, v, v])(
        dbb_r, dbb_i, bt_re, bt_im, kr, ki, ar, ai, lr, li, ldt, da_r, da_i)


def group_sum(v):
    def body(v_ref, o_ref):
        o_ref[...] = jnp.sum(v_ref[...], axis=-1, keepdims=True)
    return pl.pallas_call(body, name="group_sum", out_shape=jax.ShapeDtypeStruct((v.shape[0], 1), f32))(v)


ATTN_SCALE = XATTN_HEAD_DIM ** -0.5


def _attn_probs(q_h, k_h):
    s = lax.dot_general(q_h, k_h, NT_DIMS, preferred_element_type=f32) * ATTN_SCALE
    e = jnp.exp(s - jnp.max(s, axis=-1, keepdims=True))
    return e / jnp.sum(e, axis=-1, keepdims=True)


def attn_fwd(proj, kv, wxo3, tm=TM):
    S = proj.shape[0]
    M = kv.shape[0]
    J, _, n = wxo3.shape

    def body(q_ref, kv_ref, w_ref, o_ref, y_ref):
        for h in range(XATTN_HEADS):
            cols = slice(h * XATTN_HEAD_DIM, (h + 1) * XATTN_HEAD_DIM)
            q_h = q_ref[:, cols].astype(bf16)
            k_h = kv_ref[:, cols]
            v_h = kv_ref[:, XATTN_WIDTH + h * XATTN_HEAD_DIM:XATTN_WIDTH + (h + 1) * XATTN_HEAD_DIM]
            p = _attn_probs(q_h, k_h)
            o_ref[:, cols] = jnp.dot(p.astype(bf16), v_h, preferred_element_type=f32).astype(bf16)
        o = o_ref[...]
        for j in range(J):
            y_ref[:, j * n:(j + 1) * n] = jnp.dot(o, w_ref[j], preferred_element_type=f32).astype(bf16)

    return pl.pallas_call(
        body, name="attn_fwd", grid=(S // tm,),
        in_specs=[_rows(tm, XATTN_WIDTH, 10), _full((M, 2 * XATTN_WIDTH)), _full(wxo3.shape)],
        out_specs=[_rows(tm, XATTN_WIDTH), _rows(tm, J * n)],
        out_shape=[jax.ShapeDtypeStruct((S, XATTN_WIDTH), bf16), jax.ShapeDtypeStruct((S, J * n), bf16)],
        compiler_params=_params())(proj, kv, wxo3)


def attn_bwd(dy_c, wxo3, proj, kv, dproj, tm=TM):
    S = proj.shape[0]
    M = kv.shape[0]
    J, _, n = wxo3.shape

    def body(dy_ref, w_ref, q_ref, kv_ref, _, dq_ref, dkv_ref):
        i = pl.program_id(0)

        @pl.when(i == 0)
        def _():
            dkv_ref[...] = jnp.zeros_like(dkv_ref)

        do = None
        for j in range(J):
            part = lax.dot_general(dy_ref[:, j * n:(j + 1) * n], w_ref[j], NT_DIMS, preferred_element_type=f32)
            do = part if do is None else do + part
        do = do.astype(bf16)
        for h in range(XATTN_HEADS):
            cols = slice(h * XATTN_HEAD_DIM, (h + 1) * XATTN_HEAD_DIM)
            vcols = slice(XATTN_WIDTH + h * XATTN_HEAD_DIM, XATTN_WIDTH + (h + 1) * XATTN_HEAD_DIM)
            q_h = q_ref[:, cols].astype(bf16)
            k_h = kv_ref[:, cols]
            v_h = kv_ref[:, vcols]
            do_h = do[:, cols]
            p = _attn_probs(q_h, k_h)
            dp = lax.dot_general(do_h, v_h, NT_DIMS, preferred_element_type=f32)
            ds = (p * (dp - jnp.sum(dp * p, axis=-1, keepdims=True)) * ATTN_SCALE).astype(bf16)
            dq_ref[:, cols] = jnp.dot(ds, k_h, preferred_element_type=f32).astype(bf16)
            dkv_ref[:, cols] += lax.dot_general(ds, q_h, TN_DIMS, preferred_element_type=f32)
            dkv_ref[:, vcols] += lax.dot_general(p.astype(bf16), do_h, TN_DIMS, preferred_element_type=f32)

    return pl.pallas_call(
        body, name="attn_bwd", grid=(S // tm,),
        in_specs=[_rows(tm, J * n), _full(wxo3.shape), _rows(tm, XATTN_WIDTH, 10), _full((M, 2 * XATTN_WIDTH)),
                  pl.BlockSpec(memory_space=pl.ANY)],
        out_specs=[_rows(tm, XATTN_WIDTH, 10), _full((M, 2 * XATTN_WIDTH))],
        out_shape=[jax.ShapeDtypeStruct(dproj.shape, bf16), jax.ShapeDtypeStruct((M, 2 * XATTN_WIDTH), f32)],
        input_output_aliases={4: 0},
        compiler_params=_params())(dy_c, wxo3, proj, kv, dproj)


def _ln_stats(r):
    mu = jnp.mean(r, axis=-1, keepdims=True)
    xc = r - mu
    var = jnp.mean(xc * xc, axis=-1, keepdims=True)
    rstd = lax.rsqrt(var + LN_EPS)
    return xc * rstd, rstd


def _ln_bwd(dy, xhat, rstd, g):
    dxh = dy * g
    return rstd * (dxh - jnp.mean(dxh, axis=-1, keepdims=True) - xhat * jnp.mean(dxh * xhat, axis=-1, keepdims=True))


def _colsum(v):
    return jnp.sum(v, axis=0, keepdims=True)


def _ln2_loss_epi(t2, h1, target, b_down, g, b):
    xhat, rstd = _ln_stats(ALPHA * h1 + t2 + b_down)
    err = xhat * g + b - target
    dout = err * (1.0 / D_MODEL)
    dr = _ln_bwd(dout, xhat, rstd, g)
    return (dr, dr), (0.5 * _colsum(err * err) * (1.0 / D_MODEL), _colsum(dout * xhat), _colsum(dout), _colsum(dr))


def _merge_fwd_pro(g0, g1, g2, y_a, glu_a, glu_b, y_c, b):
    gs = [_sig(g.astype(f32) + b[:, k * D_MODEL:(k + 1) * D_MODEL]) for k, g in enumerate((g0, g1, g2))]
    y_b = glu_a.astype(f32) * _sig(glu_b.astype(f32))
    return gs[0] * y_a.astype(f32) + gs[1] * y_b + gs[2] * y_c.astype(f32)


def _merge_bwd_epi(dm, g0, g1, g2, y_a, glu_a, glu_b, y_c, b):
    ga, sb = glu_a.astype(f32), _sig(glu_b.astype(f32))
    ys = (y_a.astype(f32), ga * sb, y_c.astype(f32))
    gs = [_sig(g.astype(f32) + b[:, k * D_MODEL:(k + 1) * D_MODEL]) for k, g in enumerate((g0, g1, g2))]
    dpre = [dm * ys[k] * gs[k] * (1.0 - gs[k]) for k in range(3)]
    dyb = dm * gs[1]
    dglu = jnp.concatenate([dyb * sb, dyb * ga * sb * (1.0 - sb)], axis=1)
    return ((jnp.concatenate(dpre, axis=1), dm * gs[0], dglu, dm * gs[2]),
            (jnp.concatenate([_colsum(d) for d in dpre], axis=1),))


def _ln1_fwd_epi(t1, x, g, b):
    xhat, _ = _ln_stats(ALPHA * x + t1)
    h = xhat * g + b
    return (t1, h, h), ()


def _ln1_bwd_epi(t3, x, t1, dr2, g):
    xhat, rstd = _ln_stats(ALPHA * x + t1)
    dh = ALPHA * dr2 + t3
    dr = _ln_bwd(dh, xhat, rstd, g)
    return (dr, dr), (_colsum(dh * xhat), _colsum(dh))


def exchange(*rides, name):
    return _pcall(lambda: None, [], name=name, grid=(), in_specs=[], out_specs=[], out_shape=[], rides=rides)[1]


ROW_TILE = 256


def sum_slots(recv, name):
    n, R, C = recv.shape
    tm = min(R, ROW_TILE)

    def body(r_ref, o_ref):
        acc = r_ref[0].astype(f32)
        for k in range(1, n):
            acc = acc + r_ref[k].astype(f32)
        o_ref[...] = acc

    return pl.pallas_call(
        body, name=name, grid=(R // tm,), in_specs=[pl.BlockSpec((n, tm, C), lambda i: (0, i, 0))],
        out_specs=_rows(tm, C), out_shape=jax.ShapeDtypeStruct((R, C), f32), compiler_params=_params())(recv)


def _adamw(w, g, m, v):
    m = ADAM_B1 * m + (1.0 - ADAM_B1) * g
    v = ADAM_B2 * v + (1.0 - ADAM_B2) * (g * g)
    m_hat = m / (1.0 - ADAM_B1 ** ADAM_STEP)
    v_hat = v / (1.0 - ADAM_B2 ** ADAM_STEP)
    delta = -ADAM_LR * (m_hat / (jnp.sqrt(v_hat) + ADAM_EPS) + ADAM_WD * w)
    return delta, m, v


def adam_pair(p, q, w, m, v, name):
    R, C = w.shape
    tm = min(R, ROW_TILE)

    def body(p_ref, q_ref, w_ref, m_ref, v_ref, g_ref, d_ref, nm_ref, nv_ref):
        g = p_ref[...] + q_ref[...]
        g_ref[...] = g
        d_ref[...], nm_ref[...], nv_ref[...] = _adamw(w_ref[...], g, m_ref[...], v_ref[...])

    o = jax.ShapeDtypeStruct((R, C), f32)
    return pl.pallas_call(body, name=name, grid=(R // tm,), in_specs=[_rows(tm, C)] * 5,
                          out_specs=[_rows(tm, C)] * 4, out_shape=[o] * 4, compiler_params=_params())(p, q, w, m, v)


def adam_slots(recv, w, m, v, name):
    n = recv.shape[0]

    def body(r_ref, w_ref, m_ref, v_ref, g_ref, d_ref, nm_ref, nv_ref):
        g = r_ref[0]
        for k in range(1, n):
            g = g + r_ref[k]
        g_ref[...] = g
        d_ref[...], nm_ref[...], nv_ref[...] = _adamw(w_ref[...], g, m_ref[...], v_ref[...])

    o = jax.ShapeDtypeStruct(w.shape, f32)
    return pl.pallas_call(body, name=name, out_shape=[o] * 4, compiler_params=_params())(recv, w, m, v)


BIG = ("w_in", "w_conv_out", "w_glu", "w_xattn_out", "w_kv", "w_out", "w_up", "w_down")
MID = ("w_conv_out", "w_glu", "w_xattn_out", "w_kv", "w_out")
SMALL = ("b_gate", "ssm_lam_re", "ssm_lam_im", "ssm_log_dt", "ssm_b_re", "ssm_b_im", "ssm_c_re", "ssm_c_im", "ssm_d",
         "ln1_g", "ln1_b", "b_up", "b_down", "ln2_g", "ln2_b")
def _pad_flat(a, mult=1024):
    a = a.reshape(-1)
    return jnp.pad(a, (0, (-a.shape[0]) % mult))


def _perm(a):
    S, W = a.shape
    return a.reshape(N_SEG, S // N_SEG, W).transpose(1, 0, 2).reshape(S, W)


def _unperm(a):
    S, W = a.shape
    return a.reshape(S // N_SEG, N_SEG, W).transpose(1, 0, 2).reshape(S, W)


def _state_rows(a):
    return a.transpose(2, 0, 1).reshape(SSM_GROUP, N_STATES)


def _block_diag_b(bt):
    b4 = bt.reshape(SSM_GROUP, 4, 8, SSM_STATE)
    eye = jnp.eye(8, dtype=bt.dtype)
    return jnp.einsum("hcgp,gk->cghkp", b4, eye).reshape(4, 128, 512)


def _block_diag_c(c):
    c4 = c.reshape(4, 8, SSM_GROUP, SSM_STATE)
    eye = jnp.eye(8, dtype=c.dtype)
    return jnp.einsum("cghp,gk->cgpkh", c4, eye).reshape(4, 512, 128)


def _diag_b(acc):
    a = acc.reshape(4, 8, SSM_STATE, 8, SSM_GROUP)
    eye = jnp.eye(8, dtype=acc.dtype)
    return jnp.einsum("cgpkh,gk->hcgp", a, eye).reshape(SSM_GROUP, N_STATES)


def _diag_c(acc):
    a = acc.reshape(4, 8, SSM_GROUP, 8, SSM_STATE)
    eye = jnp.eye(8, dtype=acc.dtype)
    return jnp.einsum("cghkp,gk->cghp", a, eye).reshape(SSM_GROUPS, SSM_GROUP, SSM_STATE)


def kernel(x, mem, w_in, b_gate, conv_w, w_conv_out, ssm_lam_re, ssm_lam_im, ssm_log_dt, ssm_b_re, ssm_b_im, ssm_c_re, ssm_c_im, ssm_d, w_glu, w_kv, w_xattn_out, w_out, ln1_g, ln1_b, w_up, b_up, w_down, b_down, ln2_g, ln2_b, loss_target, m_w_in, m_b_gate, m_conv_w, m_w_conv_out, m_ssm_lam_re, m_ssm_lam_im, m_ssm_log_dt, m_ssm_b_re, m_ssm_b_im, m_ssm_c_re, m_ssm_c_im, m_ssm_d, m_w_glu, m_w_kv, m_w_xattn_out, m_w_out, m_ln1_g, m_ln1_b, m_w_up, m_b_up, m_w_down, m_b_down, m_ln2_g, m_ln2_b, v_w_in, v_b_gate, v_conv_w, v_w_conv_out, v_ssm_lam_re, v_ssm_lam_im, v_ssm_log_dt, v_ssm_b_re, v_ssm_b_im, v_ssm_c_re, v_ssm_c_im, v_ssm_d, v_w_glu, v_w_kv, v_w_xattn_out, v_w_out, v_ln1_g, v_ln1_b, v_w_up, v_b_up, v_w_down, v_b_down, v_ln2_g, v_ln2_b):
    W = dict(w_in=w_in, b_gate=b_gate, conv_w=conv_w, w_conv_out=w_conv_out, ssm_lam_re=ssm_lam_re,
             ssm_lam_im=ssm_lam_im, ssm_log_dt=ssm_log_dt, ssm_b_re=ssm_b_re, ssm_b_im=ssm_b_im, ssm_c_re=ssm_c_re,
             ssm_c_im=ssm_c_im, ssm_d=ssm_d, w_glu=w_glu, w_kv=w_kv, w_xattn_out=w_xattn_out, w_out=w_out,
             ln1_g=ln1_g, ln1_b=ln1_b, w_up=w_up, b_up=b_up, w_down=w_down, b_down=b_down, ln2_g=ln2_g, ln2_b=ln2_b)
    MOM = dict(w_in=m_w_in, b_gate=m_b_gate, conv_w=m_conv_w, w_conv_out=m_w_conv_out, ssm_lam_re=m_ssm_lam_re,
               ssm_lam_im=m_ssm_lam_im, ssm_log_dt=m_ssm_log_dt, ssm_b_re=m_ssm_b_re, ssm_b_im=m_ssm_b_im,
               ssm_c_re=m_ssm_c_re, ssm_c_im=m_ssm_c_im, ssm_d=m_ssm_d, w_glu=m_w_glu, w_kv=m_w_kv,
               w_xattn_out=m_w_xattn_out, w_out=m_w_out, ln1_g=m_ln1_g, ln1_b=m_ln1_b, w_up=m_w_up, b_up=m_b_up,
               w_down=m_w_down, b_down=m_b_down, ln2_g=m_ln2_g, ln2_b=m_ln2_b)
    VEL = dict(w_in=v_w_in, b_gate=v_b_gate, conv_w=v_conv_w, w_conv_out=v_w_conv_out, ssm_lam_re=v_ssm_lam_re,
               ssm_lam_im=v_ssm_lam_im, ssm_log_dt=v_ssm_log_dt, ssm_b_re=v_ssm_b_re, ssm_b_im=v_ssm_b_im,
               ssm_c_re=v_ssm_c_re, ssm_c_im=v_ssm_c_im, ssm_d=v_ssm_d, w_glu=v_w_glu, w_kv=v_w_kv,
               w_xattn_out=v_w_xattn_out, w_out=v_w_out, ln1_g=v_ln1_g, ln1_b=v_ln1_b, w_up=v_w_up, b_up=v_b_up,
               w_down=v_w_down, b_down=v_b_down, ln2_g=v_ln2_g, ln2_b=v_ln2_b)
    names = list(W)
    xy = 2 * lax.axis_index("x") + lax.axis_index("y")

    xs = x[0]
    S = xs.shape[0]
    mems = mem[0]
    tgt = loss_target[0]

    shard_bf = {n: W[n][0].astype(bf16) for n in BIG}

    lr = ssm_lam_re.reshape(1, N_STATES)
    li = ssm_lam_im.reshape(1, N_STATES)
    ldt = jnp.repeat(ssm_log_dt.reshape(SSM_GROUPS), SSM_STATE).reshape(1, N_STATES)
    bt_re, bt_im = _state_rows(ssm_b_re[0]), _state_rows(ssm_b_im[0])
    ar, ai, kr, ki, bbt_r, bbt_i = ssm_prep(lr, li, ldt, bt_re, bt_im)
    bd_r, bd_i = _block_diag_b(bbt_r).astype(bf16), _block_diag_b(bbt_i).astype(bf16)
    cd_r, cd_i = _block_diag_c(ssm_c_re[0]).astype(bf16), _block_diag_c(ssm_c_im[0]).astype(bf16)
    bdt_r, bdt_i = bd_r.transpose(0, 2, 1), bd_i.transpose(0, 2, 1)
    cdt_r, cdt_i = cd_r.transpose(0, 2, 1), cd_i.transpose(0, 2, 1)
    d_skip = ssm_d.reshape(1, SSM_WIDTH)

    mem_bf = mems.astype(bf16)
    u_col = GATE_COLS + 3 * CONV_WIDTH
    me_xy = jnp.reshape(xy, (1,)).astype(jnp.int32)
    proj, x_bf, (win3,) = proj_own(xs, shard_bf["w_in"], me_xy, 4, rides=(("gather2", shard_bf["w_in"]),))
    proj, (wco3, wglu3, wxo3, wkv4, wout4, convw4) = proj_rest(
        x_bf, win3, me_xy, proj, rides=tuple(("gather", shard_bf[n]) for n in MID) + (("gather", conv_w[0]),))
    wkv3 = wkv4.reshape(1, D_MODEL, 2 * XATTN_WIDTH)
    wout3 = wout4.reshape(1, D_MODEL, D_MODEL)
    convw_full = convw4.transpose(1, 0, 2).reshape(3, CONV_WIDTH)
    a_conv, y_a = conv_fwd(proj, convw_full, wco3)
    u_perm = _perm(proj[:, u_col:u_col + SSM_WIDTH])
    half = D_MODEL // 2
    (init_r, init_i), (wup_a,) = ssm_scan_fwd(u_perm, bd_r, bd_i, ar, ai,
                                              rides=(("gather", shard_bf["w_up"][:half]),))
    (ysm_perm, ys_perm, st_r, st_i), (wup_b,) = ssm_scan_fwd(
        u_perm, bd_r, bd_i, ar, ai, final=(cd_r, cd_i, d_skip, init_r, init_i),
        rides=(("gather", shard_bf["w_up"][half:]),))
    wup3 = jnp.concatenate([wup_a, wup_b], axis=1)
    y_s = _unperm(ys_perm)
    glu = mm_fwd_small(y_s, wglu3, "glu", out_dtype=bf16)
    kv = mm_fwd_small(mem_bf, wkv3, "kv", out_dtype=bf16)
    o_att, y_c = attn_fwd(proj, kv, wxo3)
    merge_rows = ((proj, D_MODEL, 0), (proj, D_MODEL, 1), (proj, D_MODEL, 2), y_a, (glu, D_MODEL, 0),
                  (glu, D_MODEL, 1), y_c)
    merged, t1, h1, h1_bf = mm_rows_fused(
        (_merge_fwd_pro, merge_rows, (b_gate,)), wout3, "merge_w_out_ln1", transposed=False, tm=256,
        epi=_ln1_fwd_epi, extras=(xs,), consts=(ln1_g, ln1_b), outs=(f32, f32, bf16))
    (r_up, hdn), (wdn4,) = mm_fwd(h1_bf, wup3, "w_up", bias=b_up, outs=(bf16, bf16), second=_relu2,
                                  first=lambda v: jnp.maximum(v, 0.0), rides=(("gather", shard_bf["w_down"]),))
    wdn3 = wdn4.reshape(1, D_FF, D_MODEL)

    dr2, dr2_bf, loss_cols, d_ln2_g, d_ln2_b, d_b_down = mm_rows_fused(
        hdn, wdn3, "w_down_ln2_loss", transposed=False, tm=512, epi=_ln2_loss_epi, extras=(h1, tgt),
        consts=(b_down, ln2_g, ln2_b), outs=(f32, bf16), nsums=4)
    part, other = {}, {}
    g_w_down = mm_bwd_w(hdn, dr2_bf, 1, "dw_down", tk=1024).reshape(4, -1, D_MODEL)
    (dup, d_b_up), (recv_dn,) = mm_bwd_x(dr2_bf, wdn3, "dup", tm=512, extras=(r_up,), colsum=True, out_dtype=bf16,
                                         epi=lambda acc, r: acc * (2.0 * r.astype(f32)),
                                         rides=(("scatter", g_w_down),))
    part["w_down"] = sum_slots(recv_dn, "sum_w_down")
    g_w_up = mm_bwd_w(h1_bf, dup, 4, "dw_up")
    dr1, dr1_bf, d_ln1_g, d_ln1_b = mm_rows_fused(
        dup, wup3, "dh1_ln1_bwd", transposed=True, tm=256, epi=_ln1_bwd_epi, extras=(xs, t1, dr2), consts=(ln1_g,),
        outs=(f32, bf16), nsums=2)
    g_w_out = mm_bwd_w(merged, dr1_bf, 1, "dw_out").reshape(4, -1, D_MODEL)
    (dproj, dy_a, dglu, dy_c, d_b_gate), (recv_up,) = mm_rows_fused(
        dr1_bf, wout3, "dmerged_merge_bwd", transposed=True, tm=256, epi=_merge_bwd_epi, extras=merge_rows,
        consts=(b_gate,), outs=((bf16, GATE_COLS, IN_COLS), bf16, (bf16, 2 * D_MODEL, 2 * D_MODEL), bf16),
        sum_widths=(GATE_COLS,), rides=(("scatter", g_w_up),))
    part["w_up"] = sum_slots(recv_up, "sum_w_up")

    g_w_co = mm_bwd_w_small(a_conv, dy_a, 4, "dw_conv_out")
    da_conv = mm_bwd_x(dy_a, wco3, "da_conv", tm=1024)
    dproj, d_conv_w = conv_bwd(da_conv, proj, convw_full, dproj)

    g_w_glu = mm_bwd_w_small(y_s, dglu, 4, "dw_glu")
    dys_perm = _perm(mm_bwd_x(dglu, wglu3, "dy_s", tm=1024))
    linit_r, linit_i = ssm_scan_bwd(dys_perm, ysm_perm, cdt_r, cdt_i, ar, ai)
    (du_perm, dbacc_r, dbacc_i, dcacc_r, dcacc_i, da_r, da_i, d_ssm_d), (recv_co, recv_glu, recv_out) = ssm_scan_bwd(
        dys_perm, ysm_perm, cdt_r, cdt_i, ar, ai,
        final=(u_perm, st_r, st_i, bdt_r, bdt_i, d_skip, linit_r, linit_i),
        rides=(("scatter", g_w_co), ("scatter", g_w_glu), ("scatter", g_w_out)))
    part["w_out"] = sum_slots(recv_out, "sum_w_out")
    part["w_conv_out"] = sum_slots(recv_co, "sum_w_conv_out")
    part["w_glu"] = sum_slots(recv_glu, "sum_w_glu")
    dbt_re, dbt_im, d_lr, d_li, d_ldt_state = ssm_param_bwd(
        _diag_b(dbacc_r), _diag_b(dbacc_i), bt_re, bt_im, kr, ki, ar, ai, lr, li, ldt, da_r, da_i)
    d_log_dt = group_sum(d_ldt_state.reshape(SSM_GROUPS, SSM_STATE))
    d_b_re = dbt_re.reshape(SSM_GROUP, SSM_GROUPS, SSM_STATE).transpose(1, 2, 0)
    d_b_im = dbt_im.reshape(SSM_GROUP, SSM_GROUPS, SSM_STATE).transpose(1, 2, 0)
    d_c_re = _diag_c(dcacc_r)
    d_c_im = -_diag_c(dcacc_i)

    g_w_xo = mm_bwd_w_small(o_att, dy_c, 4, "dw_xattn_out")
    dproj, dkv = attn_bwd(dy_c, wxo3, proj, kv, dproj)
    g_w_kv = mm_bwd_w(mem_bf, dkv, 1, "dw_kv").reshape(4, -1, D_MODEL)

    dproj = lax.dynamic_update_slice(dproj, _unperm(du_perm), (0, u_col))
    g_small = {"b_gate": d_b_gate, "ssm_lam_re": d_lr, "ssm_lam_im": d_li, "ssm_log_dt": d_log_dt, "ssm_b_re": d_b_re,
               "ssm_b_im": d_b_im, "ssm_c_re": d_c_re, "ssm_c_im": d_c_im, "ssm_d": d_ssm_d, "ln1_g": d_ln1_g,
               "ln1_b": d_ln1_b, "b_up": d_b_up, "b_down": d_b_down, "ln2_g": d_ln2_g, "ln2_b": d_ln2_b}
    small_names = SMALL + ("conv_w",)
    g_small["conv_w"] = d_conv_w
    sizes = {n: (g_small[n].size + 1023) // 1024 * 1024 for n in small_names}
    pack = lambda d: jnp.concatenate([_pad_flat(d[n]) for n in small_names]).reshape(-1, 128)
    early = ("w_down", "w_up", "w_out", "w_conv_out", "w_glu")
    g_w_in, landed = mm_bwd_w(
        x_bf, dproj, 4, "dw_in", rides=(("scatter", g_w_xo), ("scatter", g_w_kv), ("all", pack(g_small)))
        + tuple(("pair", part[n]) for n in early))
    recv_xo, recv_kv, srecv = landed[:3]
    other.update(zip(early, landed[3:]))
    part["w_xattn_out"] = sum_slots(recv_xo, "sum_w_xattn_out")
    part["w_kv"] = sum_slots(recv_kv, "sum_w_kv")
    dx, (recv_in, other["w_xattn_out"], other["w_kv"]) = mm_bwd_x(
        dproj, win3, "dx", extras=(dr1,), epi=lambda acc, d: acc + ALPHA * d,
        rides=(("scatter", g_w_in), ("pair", part["w_xattn_out"]), ("pair", part["w_kv"])))
    part["w_in"] = sum_slots(recv_in, "sum_w_in")
    other["w_in"], = exchange(("pair", part["w_in"]), name="swap_w_in")
    res = [{}, {}, {}, {}]
    for n in BIG:
        for k, r in enumerate(adam_pair(part[n], other[n], W[n][0], MOM[n][0], VEL[n][0], "adam_" + n)):
            res[k][n] = r[None]
    conv_zero = jnp.zeros((3, CONV_WIDTH), f32)
    gs, ds_, ms, vs = adam_slots(srecv, pack({**{n: W[n] for n in SMALL}, "conv_w": conv_zero}),
                                 pack({**{n: MOM[n] for n in SMALL}, "conv_w": conv_zero}),
                                 pack({**{n: VEL[n] for n in SMALL}, "conv_w": conv_zero}), "adam_small")

    def unpack_small(buf):
        flat = buf.reshape(-1)
        out, r = {}, 0
        for n in small_names:
            ref = g_small[n] if n == "conv_w" else W[n]
            out[n] = flat[r:r + ref.size].reshape(ref.shape)
            r += sizes[n]
        return out

    res_s = [unpack_small(b) for b in (gs, ds_, ms, vs)]
    g_conv = lax.dynamic_slice(res_s[0]["conv_w"], (0, xy * 128), (3, 128))
    conv_slots = g_conv.reshape(1, 3, 128)
    cg, cd, cm, cv = adam_slots(conv_slots, conv_w[0], m_conv_w[0], v_conv_w[0], "adam_conv")
    conv_res = [cg, cd, cm, cv]

    loss = lax.psum(jnp.sum(loss_cols), ("x", "y", "c"))
    outs = [loss, dx.reshape(x.shape)]
    for k in range(4):
        for n in names:
            if n == "conv_w":
                outs.append(conv_res[k].reshape(conv_w.shape))
            elif n in BIG:
                outs.append(res[k][n])
            else:
                outs.append(res_s[k][n])
    return tuple(outs)
```

```python
import functools
import math

import jax
import jax.numpy as jnp
from jax import lax
from jax.experimental import pallas as pl
from jax.experimental.pallas import tpu as pltpu

f32 = jnp.float32
bf16 = jnp.bfloat16

D_MODEL = 1024
CONV_WIDTH = 512
SSM_WIDTH = 512
SSM_GROUP = 16
SSM_GROUPS = 32
SSM_STATE = 64
N_STATES = SSM_GROUPS * SSM_STATE
XATTN_HEADS = 4
XATTN_HEAD_DIM = 128
XATTN_WIDTH = 512
D_FF = 4096
GATE_COLS = 3 * D_MODEL
IN_COLS = GATE_COLS + 3 * CONV_WIDTH + SSM_WIDTH + XATTN_WIDTH
ALPHA = 2.0 ** 0.25
LN_EPS = 1e-5
ADAM_LR = 0.001
ADAM_B1 = 0.9
ADAM_B2 = 0.999
ADAM_EPS = 1e-08
ADAM_WD = 0.01
ADAM_STEP = 10

N_SEG = 8
SCAN_ROWS = 512
LANE_STRIP = 512
TM = 512
VMEM_LIMIT = 48 * 1024 * 1024
MESH = pl.DeviceIdType.MESH

NT_DIMS = (((1,), (1,)), ((), ()))
TN_DIMS = (((0,), (0,)), ((), ()))


def _params():
    return pltpu.CompilerParams(vmem_limit_bytes=VMEM_LIMIT)


def _full(shape):
    n = len(shape)
    return pl.BlockSpec(shape, lambda *_: (0,) * n)


def _rows(tm, w, cb=0):
    return pl.BlockSpec((tm, w), lambda i: (i, cb))


def _sig(x):
    return 1.0 / (1.0 + jnp.exp(-x))


HBM_SPEC = pl.BlockSpec(memory_space=pl.ANY)
RIDE_PEERS = {"gather": 3, "gather2": 6, "scatter": 3, "all": 7, "pair": 1}


def _xy_peers(x, y):
    return [(1 - x, y), (x, 1 - y), (1 - x, 1 - y)]


def _ride_copies(kind, src, dst, send_sems, recv_sems, local_sem):
    x, y, c = lax.axis_index("x"), lax.axis_index("y"), lax.axis_index("c")
    me = 2 * x + y
    if kind == "all":
        flips = [(fx, fy, fc) for fx in (0, 1) for fy in (0, 1) for fc in (0, 1)][1:]
        peers = [(x ^ fx, y ^ fy, c ^ fc) for fx, fy, fc in flips]
        slot = lambda p: 4 * p[0] + 2 * p[1] + p[2]
        mine = slot((x, y, c))
    else:
        peers = [(px, py, c) for px, py in _xy_peers(x, y)]
        slot = lambda p: 2 * p[0] + p[1]
        mine = me

    def remote(k, s, d):
        return pltpu.make_async_remote_copy(src_ref=s, dst_ref=d, send_sem=send_sems.at[k], recv_sem=recv_sems.at[k],
                                            device_id=peers[k], device_id_type=MESH)

    if kind == "pair":
        peers = [(x, y, 1 - c)]
        return None, [remote(0, src, dst)], [remote(0, src, dst)]
    if kind == "scatter":
        local = pltpu.make_async_copy(src.at[me], dst.at[me], local_sem)
        sends = [remote(k, src.at[slot(p)], dst.at[me]) for k, p in enumerate(peers)]
        lands = [remote(k, src.at[me], dst.at[slot(p)]) for k, p in enumerate(peers)]
    else:
        local = pltpu.make_async_copy(src, dst.at[mine], local_sem)
        sends = [remote(k, src, dst.at[mine]) for k, p in enumerate(peers)]
        lands = [remote(k, src, dst.at[slot(p)]) for k, p in enumerate(peers)]
    return local, sends, lands


def _ride_shape(kind, src):
    lead = {"gather": (4,), "gather2": (4,), "scatter": (), "all": (8,), "pair": ()}[kind]
    return jax.ShapeDtypeStruct(lead + src.shape, src.dtype)


def _two_level(src, dst, send_sems, recv_sems, local_sem):
    x, y, c = lax.axis_index("x"), lax.axis_index("y"), lax.axis_index("c")
    me = 2 * x + y
    h = src.shape[0] // 2
    mine = pl.ds(pl.multiple_of(c * h, 16), h)
    theirs = pl.ds(pl.multiple_of((1 - c) * h, 16), h)
    peers = _xy_peers(x, y)
    slots = [2 * px + py for px, py in peers]

    def over_ici(k, d):
        return pltpu.make_async_remote_copy(src_ref=src.at[mine], dst_ref=d, send_sem=send_sems.at[k],
                                            recv_sem=recv_sems.at[k], device_id=(*peers[k], c), device_id_type=MESH)

    def over_d2d(k, rows):
        blk = dst.at[slots[k], rows]
        return pltpu.make_async_remote_copy(src_ref=blk, dst_ref=blk, send_sem=send_sems.at[3 + k],
                                            recv_sem=recv_sems.at[3 + k], device_id=(x, y, 1 - c), device_id_type=MESH)

    return (pltpu.make_async_copy(src, dst.at[me], local_sem),
            [over_ici(k, dst.at[me, mine]) for k in range(3)], [over_ici(k, dst.at[slots[k], mine]) for k in range(3)],
            [over_d2d(k, mine) for k in range(3)], [over_d2d(k, theirs) for k in range(3)])


def _pcall(body, args, *, name, grid, in_specs, out_specs, out_shape, scratch_shapes=(), rides=(), aliases=None,
           prefetch=()):
    n_in, n_out, n_scr, nr, npf = len(in_specs), len(out_specs), len(scratch_shapes), len(rides), len(prefetch)
    kinds = [k for k, _ in rides]

    def wrapped(*refs):
        pre, refs = refs[:npf], refs[npf:]
        ins, rsrc = refs[:n_in], refs[n_in:n_in + nr]
        outs = refs[n_in + nr:n_in + nr + n_out]
        rdst = refs[n_in + nr + n_out:n_in + 2 * nr + n_out]
        scr = refs[n_in + 2 * nr + n_out:n_in + 2 * nr + n_out + n_scr]
        sems = refs[n_in + 2 * nr + n_out + n_scr:]

        def start():
            for r, kind in enumerate(kinds):
                if kind == "gather2":
                    local, sends, _, _, _ = _two_level(rsrc[r], rdst[r], *sems[3 * r:3 * r + 3])
                else:
                    local, sends, _ = _ride_copies(kind, rsrc[r], rdst[r], *sems[3 * r:3 * r + 3])
                if local is not None:
                    local.start()
                for cp in sends:
                    cp.start()

        def finish():
            for r, kind in enumerate(kinds):
                if kind == "gather2":
                    local, sends, ici_lands, forwards, lands = _two_level(rsrc[r], rdst[r], *sems[3 * r:3 * r + 3])
                    for k in range(3):
                        ici_lands[k].wait_recv()
                        forwards[k].start()
                    sends = sends + forwards
                else:
                    local, sends, lands = _ride_copies(kind, rsrc[r], rdst[r], *sems[3 * r:3 * r + 3])
                for cp in lands:
                    cp.wait_recv()
                for cp in sends:
                    cp.wait_send()
                if local is not None:
                    local.wait()

        if nr and grid:
            ids = [pl.program_id(a) for a in range(len(grid))]
            first = functools.reduce(jnp.logical_and, [i == 0 for i in ids])
            last = functools.reduce(jnp.logical_and, [i == g - 1 for i, g in zip(ids, grid)])
            pl.when(first)(start)
        elif nr:
            start()
        body(*pre, *ins, *outs, *scr)
        if nr and grid:
            pl.when(last)(finish)
        elif nr:
            finish()

    sems = []
    for kind in kinds:
        n = RIDE_PEERS[kind]
        sems += [pltpu.SemaphoreType.DMA((n,)), pltpu.SemaphoreType.DMA((n,)), pltpu.SemaphoreType.DMA(())]
    all_in = list(in_specs) + [HBM_SPEC] * nr
    all_out = list(out_specs) + [HBM_SPEC] * nr
    all_scratch = list(scratch_shapes) + sems
    if npf:
        how = dict(grid_spec=pltpu.PrefetchScalarGridSpec(num_scalar_prefetch=npf, grid=grid, in_specs=all_in,
                                                          out_specs=all_out, scratch_shapes=all_scratch))
    else:
        how = dict(grid=grid, in_specs=all_in, out_specs=all_out, scratch_shapes=all_scratch)
    res = pl.pallas_call(
        wrapped, name=name, out_shape=list(out_shape) + [_ride_shape(k, s) for k, s in rides],
        input_output_aliases=aliases or {}, compiler_params=_params(), **how)(
            *prefetch, *args, *[s for _, s in rides])
    return list(res[:n_out]), list(res[n_out:])


def _bf(v):
    return v if v.dtype == bf16 else v.astype(bf16)


def mm_fwd_small(a, w3, name, *, tm=1024, out_dtype=f32):
    M, K = a.shape
    J, _, n = w3.shape
    tm = min(tm, M)

    def body(a_ref, w_ref, o_ref):
        av = _bf(a_ref[...])
        for j in range(J):
            o_ref[:, j * n:(j + 1) * n] = jnp.dot(av, w_ref[j], preferred_element_type=f32).astype(out_dtype)

    return pl.pallas_call(
        body, name=name, grid=(M // tm,), in_specs=[_rows(tm, K), _full((J, K, n))], out_specs=_rows(tm, J * n),
        out_shape=jax.ShapeDtypeStruct((M, J * n), out_dtype), compiler_params=_params())(a, w3)


def mm_bwd_w_small(a, dy, J, name, *, tm=2048):
    M, K = a.shape
    n = dy.shape[1] // J
    tm = min(tm, M)
    ns = M // tm

    def body(a_ref, dy_ref, o_ref, acc_ref):
        s = pl.program_id(0)
        part = lax.dot_general(_bf(a_ref[...]), _bf(dy_ref[...]), TN_DIMS, preferred_element_type=f32)

        @pl.when(s == 0)
        def _():
            acc_ref[...] = part

        @pl.when(s > 0)
        def _():
            acc_ref[...] += part

        @pl.when(s == ns - 1)
        def _():
            for j in range(J):
                o_ref[j] = acc_ref[:, j * n:(j + 1) * n].astype(bf16)

    return pl.pallas_call(
        body, name=name, grid=(ns,), in_specs=[_rows(tm, K), _rows(tm, J * n)], out_specs=_full((J, K, n)),
        out_shape=jax.ShapeDtypeStruct((J, K, n), bf16), scratch_shapes=[pltpu.VMEM((K, J * n), f32)],
        compiler_params=_params())(a, dy)


def mm_fwd(a, w3, name, *, tm=1024, tn=None, bias=None, outs=(f32,), first=None, second=None, emit_a=False,
           rides=()):
    M, K = a.shape
    J, _, n = w3.shape
    tm = min(tm, M)
    tn = tn or n
    nl = n // tn
    nb = 0 if bias is None else 1

    def body(*refs):
        a_ref, w_ref = refs[0], refs[1]
        av = _bf(a_ref[...])
        acc = jnp.dot(av, w_ref[0], preferred_element_type=f32)
        if bias is not None:
            acc = acc + refs[2][...]
        refs[2 + nb][...] = (acc if first is None else first(acc)).astype(outs[0])
        if len(outs) > 1:
            refs[3 + nb][...] = second(acc).astype(outs[1])
        if emit_a:
            @pl.when(jnp.logical_and(pl.program_id(0) == 0, pl.program_id(1) == 0))
            def _():
                refs[2 + nb + len(outs)][...] = av

    in_specs = [pl.BlockSpec((tm, K), lambda j, l, i: (i, 0)),
                pl.BlockSpec((1, K, tn), lambda j, l, i: (j, 0, l))]
    args = [a, w3]
    if bias is not None:
        in_specs.append(pl.BlockSpec((1, tn), lambda j, l, i: (0, j * nl + l)))
        args.append(bias)
    out_specs = [pl.BlockSpec((tm, tn), lambda j, l, i: (i, j * nl + l)) for _ in outs]
    out_shape = [jax.ShapeDtypeStruct((M, J * n), dt) for dt in outs]
    if emit_a:
        last = M // tm - 1
        out_specs.append(pl.BlockSpec((tm, K), lambda j, l, i: (jnp.where((j == 0) & (l == 0), i, last), 0)))
        out_shape.append(jax.ShapeDtypeStruct((M, K), bf16))
    res, landed = _pcall(body, args, name=name, grid=(J, nl, M // tm), in_specs=in_specs, out_specs=out_specs,
                         out_shape=out_shape, rides=rides)
    res = res if len(res) > 1 else res[0]
    return (res, landed) if rides else res


def proj_own(x, w_own, me, n_blocks, *, tm=1024, rides=()):
    M, K = x.shape
    n = w_own.shape[1]

    def body(me_ref, x_ref, w_ref, o_ref, xb_ref):
        av = _bf(x_ref[...])
        xb_ref[...] = av
        o_ref[...] = jnp.dot(av, w_ref[...], preferred_element_type=f32).astype(bf16)

    (proj, x_bf), landed = _pcall(
        body, [x, w_own], name="proj_own", grid=(M // tm,), prefetch=(me,),
        in_specs=[pl.BlockSpec((tm, K), lambda i, me_ref: (i, 0)), pl.BlockSpec((K, n), lambda i, me_ref: (0, 0))],
        out_specs=[pl.BlockSpec((tm, n), lambda i, me_ref: (i, me_ref[0])),
                   pl.BlockSpec((tm, K), lambda i, me_ref: (i, 0))],
        out_shape=[jax.ShapeDtypeStruct((M, n_blocks * n), bf16), jax.ShapeDtypeStruct((M, K), bf16)], rides=rides)
    return proj, x_bf, landed


def proj_rest(x_bf, w3, me, proj, *, tm=1024, rides=()):
    M, K = x_bf.shape
    J, _, n = w3.shape

    def body(me_ref, x_ref, w_ref, _, o_ref):
        o_ref[...] = jnp.dot(x_ref[...], w_ref[0], preferred_element_type=f32).astype(bf16)

    (proj,), landed = _pcall(
        body, [x_bf, w3, proj], name="proj_rest", grid=(J - 1, M // tm), prefetch=(me,),
        in_specs=[pl.BlockSpec((tm, K), lambda k, i, me_ref: (i, 0)),
                  pl.BlockSpec((1, K, n), lambda k, i, me_ref: (me_ref[0] ^ (k + 1), 0, 0)), HBM_SPEC],
        out_specs=[pl.BlockSpec((tm, n), lambda k, i, me_ref: (i, me_ref[0] ^ (k + 1)))],
        out_shape=[jax.ShapeDtypeStruct(proj.shape, bf16)], aliases={3: 0}, rides=rides)
    return proj, landed


def mm_bwd_x(dy, w3, name, *, tm=TM, epi=None, extras=(), colsum=False, out_dtype=f32, rides=()):
    M = dy.shape[0]
    J, K, n = w3.shape
    tm = min(tm, M)
    nex = len(extras)

    def body(*refs):
        dy_ref, w_hbm = refs[0], refs[1]
        ex = refs[2:2 + nex]
        o_ref = refs[2 + nex]
        w_ref = refs[-1]
        i = pl.program_id(0)

        @pl.when(i == 0)
        def _():
            pltpu.sync_copy(w_hbm, w_ref)

        acc = None
        for j in range(J):
            part = lax.dot_general(_bf(dy_ref[:, j * n:(j + 1) * n]), w_ref[j], NT_DIMS, preferred_element_type=f32)
            acc = part if acc is None else acc + part
        if epi is not None:
            acc = epi(acc, *[e[...] for e in ex])
        o_ref[...] = acc.astype(out_dtype)
        if colsum:
            s_ref = refs[3 + nex]
            cs = jnp.sum(acc, axis=0, keepdims=True)

            @pl.when(i == 0)
            def _():
                s_ref[...] = cs

            @pl.when(i > 0)
            def _():
                s_ref[...] += cs

    in_specs = [pl.BlockSpec((tm, J * n), lambda i: (i, 0)), pl.BlockSpec(memory_space=pl.ANY)]
    in_specs += [pl.BlockSpec((tm, K), lambda i: (i, 0)) for _ in extras]
    out_specs = [pl.BlockSpec((tm, K), lambda i: (i, 0))]
    out_shape = [jax.ShapeDtypeStruct((M, K), out_dtype)]
    if colsum:
        out_specs.append(pl.BlockSpec((1, K), lambda i: (0, 0)))
        out_shape.append(jax.ShapeDtypeStruct((1, K), f32))
    res, landed = _pcall(body, [dy, w3, *extras], name=name, grid=(M // tm,), in_specs=in_specs, out_specs=out_specs,
                         out_shape=out_shape, scratch_shapes=[pltpu.VMEM((J, K, n), bf16)], rides=rides)
    res = res if colsum else res[0]
    return (res, landed) if rides else res


def mm_rows_fused(a, w3, name, *, transposed, tm, epi, extras=(), consts=(), outs=(f32,), nsums=0, sum_widths=None,
                  rides=()):
    pro = a[0] if isinstance(a, tuple) else None
    J, K, n = w3.shape
    W = K if transposed else n
    ka = J * n if transposed else K
    if pro is not None:
        a_rows = [e if isinstance(e, tuple) else (e, ka, 0) for e in a[1]]
        a_consts = list(a[2])
    else:
        a_rows, a_consts = [(a, ka, 0)], []
    M = a_rows[0][0].shape[0]
    tm = min(tm, M)
    na = len(a_rows) + len(a_consts)
    nex, nco = len(extras), len(consts)
    if pro is not None:
        outs = (bf16 if ka == W else (bf16, ka, ka),) + tuple(outs)
    nout = len(outs)
    nsums = len(sum_widths) if sum_widths is not None else nsums
    out_dtypes = [o[0] if isinstance(o, tuple) else o for o in outs]

    def body(*refs):
        a_refs, w_hbm = refs[:na], refs[na]
        ex = refs[na + 1:na + 1 + nex]
        co = refs[na + 1 + nex:na + 1 + nex + nco]
        o_refs = refs[na + 1 + nex + nco:na + 1 + nex + nco + nout]
        s_refs = refs[na + 1 + nex + nco + nout:na + 1 + nex + nco + nout + nsums]
        w_ref = refs[-1]
        i = pl.program_id(0)

        @pl.when(i == 0)
        def _():
            pltpu.sync_copy(w_hbm, w_ref)

        av = _bf(a_refs[0][...] if pro is None else pro(*[r[...] for r in a_refs]))
        if transposed:
            acc = None
            for j in range(J):
                part = lax.dot_general(av[:, j * n:(j + 1) * n], w_ref[j], NT_DIMS, preferred_element_type=f32)
                acc = part if acc is None else acc + part
        else:
            acc = jnp.dot(av, w_ref[0], preferred_element_type=f32)
        rows, sums = epi(acc, *[e[...] for e in ex], *[c[...] for c in co])
        if pro is not None:
            rows = (av,) + tuple(rows)
        for o_ref, v, dt in zip(o_refs, rows, out_dtypes):
            o_ref[...] = v.astype(dt)
        for s_ref, v in zip(s_refs, sums):
            @pl.when(i == 0)
            def _(s_ref=s_ref, v=v):
                s_ref[...] = v

            @pl.when(i > 0)
            def _(s_ref=s_ref, v=v):
                s_ref[...] += v

    extras = [e if isinstance(e, tuple) else (e, W, 0) for e in extras]
    outs_full = [o if isinstance(o, tuple) else (o, W, W) for o in outs]
    widths = list(sum_widths) if sum_widths is not None else [W] * nsums
    in_specs = [_rows(tm, w, cb) for _, w, cb in a_rows] + [_full(c.shape) for c in a_consts]
    in_specs += [pl.BlockSpec(memory_space=pl.ANY)]
    in_specs += [_rows(tm, w, cb) for _, w, cb in extras] + [_full(c.shape) for c in consts]
    out_specs = [_rows(tm, w) for _, w, _ in outs_full] + [_full((1, w)) for w in widths]
    out_shape = ([jax.ShapeDtypeStruct((M, cols), dt) for dt, _, cols in outs_full]
                 + [jax.ShapeDtypeStruct((1, w), f32) for w in widths])
    args = [e for e, _, _ in a_rows] + a_consts + [w3] + [e for e, _, _ in extras] + list(consts)
    res, landed = _pcall(body, args, name=name, grid=(M // tm,), in_specs=in_specs, out_specs=out_specs,
                         out_shape=out_shape, scratch_shapes=[pltpu.VMEM((J, K, n), bf16)], rides=rides)
    return (res, landed) if rides else res


def mm_bwd_w(a, dy, J, name, *, tm=2048, tn=None, tk=None, rides=()):
    M, K = a.shape
    n = dy.shape[1] // J
    tm = min(tm, M)
    tn = tn or n
    tk = tk or K
    nl = n // tn
    nk = K // tk
    ns = M // tm

    def body(a_ref, dy_ref, o_ref, acc_ref):
        s = pl.program_id(3)
        part = lax.dot_general(_bf(a_ref[...]), _bf(dy_ref[...]), TN_DIMS, preferred_element_type=f32)

        @pl.when(s == 0)
        def _():
            acc_ref[...] = part

        @pl.when(s > 0)
        def _():
            acc_ref[...] += part

        @pl.when(s == ns - 1)
        def _():
            o_ref[0] = acc_ref[...].astype(bf16)

    res, landed = _pcall(
        body, [a, dy], name=name, grid=(J, nl, nk, ns),
        in_specs=[pl.BlockSpec((tm, tk), lambda j, l, k, s: (s, k)),
                  pl.BlockSpec((tm, tn), lambda j, l, k, s: (s, j * nl + l))],
        out_specs=[pl.BlockSpec((1, tk, tn), lambda j, l, k, s: (j, k, l))],
        out_shape=[jax.ShapeDtypeStruct((J, K, n), bf16)],
        scratch_shapes=[pltpu.VMEM((tk, tn), f32)], rides=rides)
    return (res[0], landed) if rides else res[0]


def _relu2(v):
    r = jnp.maximum(v, 0.0)
    return r * r


HALO = 16


def _shift_down(z, k, halo):
    r = lax.broadcasted_iota(jnp.int32, z.shape, 0)
    y = pltpu.roll(z, k, 0)
    for q in range(k):
        y = jnp.where(r == q, halo[HALO - k + q:HALO - k + q + 1, :], y)
    return y


def _shift_up(z, k, halo):
    tm = z.shape[0]
    r = lax.broadcasted_iota(jnp.int32, z.shape, 0)
    y = pltpu.roll(z, tm - k, 0)
    for q in range(k):
        y = jnp.where(r == tm - k + q, halo[q:q + 1, :], y)
    return y


def _prev_halo(tm, cb):
    return pl.BlockSpec((HALO, CONV_WIDTH), lambda i: (jnp.maximum(i * (tm // HALO) - 1, 0), cb))


def _next_halo(tm, cb, nblk):
    return pl.BlockSpec((HALO, CONV_WIDTH), lambda i: (jnp.minimum((i + 1) * (tm // HALO), nblk - 1), cb))


def _f32(ref):
    return ref[...].astype(f32)


def conv_fwd(proj, conv_w, wco3, tm=TM):
    S = proj.shape[0]
    J, _, n = wco3.shape

    def body(cb_ref, cc_ref, ch_ref, cch_ref, chh_ref, w_ref, wo_ref, a_ref, y_ref):
        i = pl.program_id(0)
        z = _f32(cc_ref) * _f32(ch_ref)
        zh = jnp.where(i == 0, 0.0, _f32(cch_ref) * _f32(chh_ref))
        w = w_ref[...]
        dwz = w[0:1, :] * _shift_down(z, 2, zh) + w[1:2, :] * _shift_down(z, 1, zh) + w[2:3, :] * z
        a = (_f32(cb_ref) * dwz).astype(bf16)
        a_ref[...] = a
        for j in range(J):
            y_ref[:, j * n:(j + 1) * n] = jnp.dot(a, wo_ref[j], preferred_element_type=f32).astype(bf16)

    return pl.pallas_call(
        body, name="conv_fwd", grid=(S // tm,),
        in_specs=[_rows(tm, CONV_WIDTH, 6), _rows(tm, CONV_WIDTH, 7), _rows(tm, CONV_WIDTH, 8),
                  _prev_halo(tm, 7), _prev_halo(tm, 8), _full((3, CONV_WIDTH)), _full(wco3.shape)],
        out_specs=[_rows(tm, CONV_WIDTH), _rows(tm, J * n)],
        out_shape=[jax.ShapeDtypeStruct((S, CONV_WIDTH), bf16), jax.ShapeDtypeStruct((S, J * n), bf16)],
        compiler_params=_params())(proj, proj, proj, proj, proj, conv_w, wco3)


def conv_bwd(dy_a, a_conv, wco3, proj, conv_w, dproj, tm=TM):
    S = proj.shape[0]
    nt = S // tm
    J, _, n = wco3.shape

    def body(dy_ref, dyn_ref, a_ref, wo_ref, cb_ref, cc_ref, ch_ref, cch_ref, chh_ref, cbn_ref, w_ref, _, o_ref,
             dw_ref, dwo_ref, acc_ref):
        i = pl.program_id(0)

        def times_wo_t(dy):
            out = None
            for j in range(J):
                part = lax.dot_general(dy[:, j * n:(j + 1) * n], wo_ref[j], NT_DIMS, preferred_element_type=f32)
                out = part if out is None else out + part
            return out

        dyv = dy_ref[...]
        dav, dan = times_wo_t(dyv), times_wo_t(dyn_ref[...])
        part = lax.dot_general(a_ref[...], dyv, TN_DIMS, preferred_element_type=f32)

        @pl.when(i == 0)
        def _():
            acc_ref[...] = part

        @pl.when(i > 0)
        def _():
            acc_ref[...] += part

        @pl.when(i == nt - 1)
        def _():
            for j in range(J):
                dwo_ref[j] = acc_ref[:, j * n:(j + 1) * n].astype(bf16)

        cc, ch, cb = _f32(cc_ref), _f32(ch_ref), _f32(cb_ref)
        z = cc * ch
        zh = jnp.where(i == 0, 0.0, _f32(cch_ref) * _f32(chh_ref))
        w = w_ref[...]
        z1 = _shift_down(z, 1, zh)
        z2 = _shift_down(z, 2, zh)
        dwz = w[0:1, :] * z2 + w[1:2, :] * z1 + w[2:3, :] * z
        g = dav * cb
        gn = jnp.where(i == nt - 1, 0.0, dan * _f32(cbn_ref))
        dz = w[2:3, :] * g + w[1:2, :] * _shift_up(g, 1, gn) + w[0:1, :] * _shift_up(g, 2, gn)
        o_ref[:, 0:CONV_WIDTH] = (dav * dwz).astype(bf16)
        o_ref[:, CONV_WIDTH:2 * CONV_WIDTH] = (dz * ch).astype(bf16)
        o_ref[:, 2 * CONV_WIDTH:3 * CONV_WIDTH] = (dz * cc).astype(bf16)

        @pl.when(i == 0)
        def _():
            dw_ref[...] = jnp.zeros_like(dw_ref)

        dw_ref[0:1, :] += jnp.sum(g * z2, axis=0, keepdims=True)
        dw_ref[1:2, :] += jnp.sum(g * z1, axis=0, keepdims=True)
        dw_ref[2:3, :] += jnp.sum(g * z, axis=0, keepdims=True)

    return pl.pallas_call(
        body, name="conv_bwd", grid=(nt,),
        in_specs=[_rows(tm, J * n),
                  pl.BlockSpec((HALO, J * n), lambda i: (jnp.minimum((i + 1) * (tm // HALO), S // HALO - 1), 0)),
                  _rows(tm, CONV_WIDTH), _full(wco3.shape),
                  _rows(tm, CONV_WIDTH, 6), _rows(tm, CONV_WIDTH, 7), _rows(tm, CONV_WIDTH, 8),
                  _prev_halo(tm, 7), _prev_halo(tm, 8), _next_halo(tm, 6, S // HALO), _full((3, CONV_WIDTH)),
                  pl.BlockSpec(memory_space=pl.ANY)],
        out_specs=[_rows(tm, 3 * CONV_WIDTH, GATE_COLS // (3 * CONV_WIDTH)), _full((3, CONV_WIDTH)),
                   _full(wco3.shape)],
        out_shape=[jax.ShapeDtypeStruct(dproj.shape, bf16), jax.ShapeDtypeStruct((3, CONV_WIDTH), f32),
                   jax.ShapeDtypeStruct(wco3.shape, bf16)],
        scratch_shapes=[pltpu.VMEM((CONV_WIDTH, J * n), f32)], input_output_aliases={11: 0},
        compiler_params=_params())(dy_a, dy_a, a_conv, wco3, proj, proj, proj, proj, proj, proj, conv_w, dproj)


def _cmul(ar, ai, br, bi):
    return ar * br - ai * bi, ar * bi + ai * br


def ssm_prep(lr, li, ldt, bt_re, bt_im):
    def body(lr_ref, li_ref, ldt_ref, br_ref, bi_ref, ar_ref, ai_ref, kr_ref, ki_ref, bbr_ref, bbi_ref):
        lrv, liv = lr_ref[...], li_ref[...]
        dt = jnp.exp(ldt_ref[...])
        mag = jnp.exp(lrv * dt)
        ar = mag * jnp.cos(liv * dt)
        ai = mag * jnp.sin(liv * dt)
        den = lrv * lrv + liv * liv
        nr = ar - 1.0
        kr = (nr * lrv + ai * liv) / den
        ki = (ai * lrv - nr * liv) / den
        ar_ref[...], ai_ref[...], kr_ref[...], ki_ref[...] = ar, ai, kr, ki
        bbr_ref[...] = kr * br_ref[...] - ki * bi_ref[...]
        bbi_ref[...] = kr * bi_ref[...] + ki * br_ref[...]

    v = jax.ShapeDtypeStruct((1, N_STATES), f32)
    m = jax.ShapeDtypeStruct((SSM_GROUP, N_STATES), f32)
    return pl.pallas_call(body, name="ssm_prep", out_shape=[v, v, v, v, m, m])(lr, li, ldt, bt_re, bt_im)


def _pow_segment(ar, ai, seg_len):
    pr, pi = ar, ai
    for _ in range(int(math.log2(seg_len))):
        pr, pi = _cmul(pr, pi, pr, pi)
    return pr, pi


GELU_K = math.sqrt(2.0 / math.pi)
GELU_C = 0.044715


def _gelu(v):
    return 0.5 * v * (1.0 + jnp.tanh(GELU_K * (v + GELU_C * v * v * v)))


def _gelu_grad(v):
    t = jnp.tanh(GELU_K * (v + GELU_C * v * v * v))
    return 0.5 * (1.0 + t) + 0.5 * v * (1.0 - t * t) * GELU_K * (1.0 + 3.0 * GELU_C * v * v)


def ssm_scan_fwd(u_perm, bd_r, bd_i, ar, ai, *, final=None, rides=()):
    S = u_perm.shape[0]
    R = SCAN_ROWS
    nblk = S // R
    seg_len = S // N_SEG
    nstrip = N_STATES // LANE_STRIP
    store = final is not None

    def body(*refs):
        if store:
            (u_ref, bdr_ref, bdi_ref, ar_ref, ai_ref, cdr_ref, cdi_ref, d_ref, ir_ref, ii_ref,
             y_ref, ys_ref, sr_ref, si_ref, bur, bui, car, cai) = refs
        else:
            u_ref, bdr_ref, bdi_ref, ar_ref, ai_ref, or_ref, oi_ref, bur, bui, car, cai = refs
        i = pl.program_id(0)
        ub = _bf(u_ref[...])
        u = ub.astype(f32)
        for cb in range(4):
            us = ub[:, cb * 128:(cb + 1) * 128]
            bur[:, cb * 512:(cb + 1) * 512] = jnp.dot(us, bdr_ref[cb], preferred_element_type=f32)
            bui[:, cb * 512:(cb + 1) * 512] = jnp.dot(us, bdi_ref[cb], preferred_element_type=f32)

        @pl.when(i == 0)
        def _():
            if store:
                car[...] = ir_ref[...]
                cai[...] = ii_ref[...]
            else:
                car[...] = jnp.zeros_like(car)
                cai[...] = jnp.zeros_like(cai)

        for ls in range(nstrip):
            lanes = pl.ds(ls * LANE_STRIP, LANE_STRIP)
            a_r = jnp.broadcast_to(ar_ref[:, lanes], (N_SEG, LANE_STRIP))
            a_i = jnp.broadcast_to(ai_ref[:, lanes], (N_SEG, LANE_STRIP))

            def step(t, carry, lanes=lanes, a_r=a_r, a_i=a_i):
                s_r, s_i = carry
                row = pl.multiple_of(t * 8, 8)
                n_r = a_r * s_r - a_i * s_i + bur[pl.ds(row, 8), lanes]
                n_i = a_r * s_i + a_i * s_r + bui[pl.ds(row, 8), lanes]
                if store:
                    sr_ref[pl.ds(row, 8), lanes] = n_r
                    si_ref[pl.ds(row, 8), lanes] = n_i
                return n_r, n_i

            e_r, e_i = lax.fori_loop(0, R // 8, step, (car[:, lanes], cai[:, lanes]), unroll=True)
            car[:, lanes] = e_r
            cai[:, lanes] = e_i

        if store:
            for cb in range(4):
                st_r = sr_ref[:, cb * 512:(cb + 1) * 512].astype(bf16)
                st_i = si_ref[:, cb * 512:(cb + 1) * 512].astype(bf16)
                y = (jnp.dot(st_r, cdr_ref[cb], preferred_element_type=f32)
                     - jnp.dot(st_i, cdi_ref[cb], preferred_element_type=f32))
                cols = slice(cb * 128, (cb + 1) * 128)
                y = y + d_ref[:, cols] * u[:, cols]
                y_ref[:, cols] = y
                ys_ref[:, cols] = _gelu(y).astype(bf16)
        else:
            @pl.when(i == nblk - 1)
            def _():
                p_r, p_i = _pow_segment(ar_ref[...], ai_ref[...], seg_len)
                t_r, t_i = car[0:1, :], cai[0:1, :]
                or_ref[0:1, :] = jnp.zeros((1, N_STATES), f32)
                oi_ref[0:1, :] = jnp.zeros((1, N_STATES), f32)
                for j in range(1, N_SEG):
                    or_ref[j:j + 1, :] = t_r
                    oi_ref[j:j + 1, :] = t_i
                    m_r, m_i = _cmul(p_r, p_i, t_r, t_i)
                    t_r, t_i = car[j:j + 1, :] + m_r, cai[j:j + 1, :] + m_i

    blk = lambda w: pl.BlockSpec((R, w), lambda i: (i, 0))
    in_specs = [blk(SSM_WIDTH), _full((4, 128, 512)), _full((4, 128, 512)), _full((1, N_STATES)), _full((1, N_STATES))]
    args = [u_perm, bd_r, bd_i, ar, ai]
    scratch = [pltpu.VMEM((R, N_STATES), f32), pltpu.VMEM((R, N_STATES), f32),
               pltpu.VMEM((N_SEG, N_STATES), f32), pltpu.VMEM((N_SEG, N_STATES), f32)]
    if store:
        in_specs += [_full((4, 512, 128)), _full((4, 512, 128)), _full((1, SSM_WIDTH)),
                     _full((N_SEG, N_STATES)), _full((N_SEG, N_STATES))]
        args += list(final)
        out_specs = [blk(SSM_WIDTH), blk(SSM_WIDTH), blk(N_STATES), blk(N_STATES)]
        out_shape = [jax.ShapeDtypeStruct((S, SSM_WIDTH), f32), jax.ShapeDtypeStruct((S, SSM_WIDTH), bf16),
                     jax.ShapeDtypeStruct((S, N_STATES), f32), jax.ShapeDtypeStruct((S, N_STATES), f32)]
        name = "ssm_scan_fwd"
    else:
        out_specs = [_full((N_SEG, N_STATES)), _full((N_SEG, N_STATES))]
        out_shape = [jax.ShapeDtypeStruct((N_SEG, N_STATES), f32)] * 2
        name = "ssm_scan_fwd_carry"
    res, landed = _pcall(body, args, name=name, grid=(nblk,), in_specs=in_specs, out_specs=out_specs,
                         out_shape=out_shape, scratch_shapes=scratch, rides=rides)
    return (res, landed) if rides else res


def ssm_scan_bwd(dys_perm, y_perm, cdt_r, cdt_i, ar, ai, *, final=None, rides=()):
    S = dys_perm.shape[0]
    R = SCAN_ROWS
    nblk = S // R
    seg_len = S // N_SEG
    store = final is not None
    strip = LANE_STRIP // 2 if store else LANE_STRIP
    nstrip = N_STATES // strip

    def body(*refs):
        if store:
            (dys_ref, y_ref, cdr_ref, cdi_ref, ar_ref, ai_ref, u_ref, sr_ref, si_ref, bdr_ref, bdi_ref, d_ref, ir_ref,
             ii_ref, du_ref, dbr_ref, dbi_ref, dcr_ref, dci_ref, dar_ref, dai_ref, dd_ref, dsr, dsi, lmr, lmi, car,
             cai) = refs
        else:
            dys_ref, y_ref, cdr_ref, cdi_ref, ar_ref, ai_ref, or_ref, oi_ref, dsr, dsi, car, cai = refs
        i = pl.program_id(0)
        dy = dys_ref[...] * _gelu_grad(y_ref[...])
        dyb = dy.astype(bf16)
        for cb in range(4):
            ds_ = dyb[:, cb * 128:(cb + 1) * 128]
            dsr[:, cb * 512:(cb + 1) * 512] = jnp.dot(ds_, cdr_ref[cb], preferred_element_type=f32)
            dsi[:, cb * 512:(cb + 1) * 512] = -jnp.dot(ds_, cdi_ref[cb], preferred_element_type=f32)

        @pl.when(i == 0)
        def _():
            if store:
                car[...] = ir_ref[...]
                cai[...] = ii_ref[...]
                dar_ref[...] = jnp.zeros_like(dar_ref)
                dai_ref[...] = jnp.zeros_like(dai_ref)
                dbr_ref[...] = jnp.zeros_like(dbr_ref)
                dbi_ref[...] = jnp.zeros_like(dbi_ref)
                dcr_ref[...] = jnp.zeros_like(dcr_ref)
                dci_ref[...] = jnp.zeros_like(dci_ref)
                dd_ref[...] = jnp.zeros_like(dd_ref)
            else:
                car[...] = jnp.zeros_like(car)
                cai[...] = jnp.zeros_like(cai)

        for ls in range(nstrip):
            lanes = pl.ds(ls * strip, strip)
            a_r = jnp.broadcast_to(ar_ref[:, lanes], (N_SEG, strip))
            a_i = jnp.broadcast_to(ai_ref[:, lanes], (N_SEG, strip))
            zero = jnp.zeros((N_SEG, strip), f32)

            def step(k, carry, lanes=lanes, a_r=a_r, a_i=a_i):
                l_r, l_i, g_r, g_i = carry
                row = pl.multiple_of((R // 8 - 1 - k) * 8, 8)
                if store:
                    s_r = sr_ref[pl.ds(row, 8), lanes]
                    s_i = si_ref[pl.ds(row, 8), lanes]
                    g_r = g_r + l_r * s_r + l_i * s_i
                    g_i = g_i + l_i * s_r - l_r * s_i
                n_r = dsr[pl.ds(row, 8), lanes] + a_r * l_r + a_i * l_i
                n_i = dsi[pl.ds(row, 8), lanes] + a_r * l_i - a_i * l_r
                if store:
                    lmr[pl.ds(row, 8), lanes] = n_r
                    lmi[pl.ds(row, 8), lanes] = n_i
                return n_r, n_i, g_r, g_i

            e_r, e_i, g_r, g_i = lax.fori_loop(0, R // 8, step, (car[:, lanes], cai[:, lanes], zero, zero),
                                               unroll=True)
            car[:, lanes] = e_r
            cai[:, lanes] = e_i
            if store:
                dar_ref[:, lanes] += g_r
                dai_ref[:, lanes] += g_i

        if store:
            ub = _bf(u_ref[...])
            u = ub.astype(f32)
            for cb in range(4):
                cols = slice(cb * 128, (cb + 1) * 128)
                st = slice(cb * 512, (cb + 1) * 512)
                l_r = lmr[:, st].astype(bf16)
                l_i = lmi[:, st].astype(bf16)
                du = (jnp.dot(l_r, bdr_ref[cb], preferred_element_type=f32)
                      + jnp.dot(l_i, bdi_ref[cb], preferred_element_type=f32))
                du_ref[:, cols] = (du + d_ref[:, cols] * dy[:, cols]).astype(bf16)
                dbr_ref[cb] += lax.dot_general(l_r, ub[:, cols], TN_DIMS, preferred_element_type=f32)
                dbi_ref[cb] += lax.dot_general(l_i, ub[:, cols], TN_DIMS, preferred_element_type=f32)
                dcr_ref[cb] += lax.dot_general(dyb[:, cols], sr_ref[:, st].astype(bf16), TN_DIMS,
                                               preferred_element_type=f32)
                dci_ref[cb] += lax.dot_general(dyb[:, cols], si_ref[:, st].astype(bf16), TN_DIMS,
                                               preferred_element_type=f32)
            dd_ref[...] += jnp.sum(dy * u, axis=0, keepdims=True)
        else:
            @pl.when(i == nblk - 1)
            def _():
                p_r, p_i = _pow_segment(ar_ref[...], ai_ref[...], seg_len)
                p_i = -p_i
                t_r, t_i = car[N_SEG - 1:N_SEG, :], cai[N_SEG - 1:N_SEG, :]
                or_ref[N_SEG - 1:N_SEG, :] = jnp.zeros((1, N_STATES), f32)
                oi_ref[N_SEG - 1:N_SEG, :] = jnp.zeros((1, N_STATES), f32)
                for j in range(N_SEG - 2, -1, -1):
                    or_ref[j:j + 1, :] = t_r
                    oi_ref[j:j + 1, :] = t_i
                    m_r, m_i = _cmul(p_r, p_i, t_r, t_i)
                    t_r, t_i = car[j:j + 1, :] + m_r, cai[j:j + 1, :] + m_i

    blk = lambda w: pl.BlockSpec((R, w), lambda i: (nblk - 1 - i, 0))
    in_specs = [blk(SSM_WIDTH), blk(SSM_WIDTH), _full((4, 128, 512)), _full((4, 128, 512)), _full((1, N_STATES)),
                _full((1, N_STATES))]
    args = [dys_perm, y_perm, cdt_r, cdt_i, ar, ai]
    seg = jax.ShapeDtypeStruct((N_SEG, N_STATES), f32)
    if store:
        in_specs += [blk(SSM_WIDTH), blk(N_STATES), blk(N_STATES), _full((4, 512, 128)), _full((4, 512, 128)),
                     _full((1, SSM_WIDTH)), _full((N_SEG, N_STATES)), _full((N_SEG, N_STATES))]
        args += list(final)
        out_specs = [blk(SSM_WIDTH), _full((4, 512, 128)), _full((4, 512, 128)), _full((4, 128, 512)),
                     _full((4, 128, 512)), _full((N_SEG, N_STATES)), _full((N_SEG, N_STATES)), _full((1, SSM_WIDTH))]
        b_acc = jax.ShapeDtypeStruct((4, 512, 128), f32)
        c_acc = jax.ShapeDtypeStruct((4, 128, 512), f32)
        out_shape = [jax.ShapeDtypeStruct((S, SSM_WIDTH), bf16), b_acc, b_acc, c_acc, c_acc, seg, seg,
                     jax.ShapeDtypeStruct((1, SSM_WIDTH), f32)]
        scratch = [pltpu.VMEM((R, N_STATES), f32)] * 4 + [pltpu.VMEM((N_SEG, N_STATES), f32)] * 2
        name = "ssm_scan_bwd"
    else:
        out_specs = [_full((N_SEG, N_STATES)), _full((N_SEG, N_STATES))]
        out_shape = [seg, seg]
        scratch = [pltpu.VMEM((R, N_STATES), f32)] * 2 + [pltpu.VMEM((N_SEG, N_STATES), f32)] * 2
        name = "ssm_scan_bwd_carry"
    res, landed = _pcall(body, args, name=name, grid=(nblk,), in_specs=in_specs, out_specs=out_specs,
                         out_shape=out_shape, scratch_shapes=scratch, rides=rides)
    return (res, landed) if rides else res


def ssm_param_bwd(dbb_r, dbb_i, bt_re, bt_im, kr, ki, ar, ai, lr, li, ldt, da_r, da_i):
    def body(dbr_ref, dbi_ref, br_ref, bi_ref, kr_ref, ki_ref, ar_ref, ai_ref, lr_ref, li_ref, ldt_ref, dar_ref,
             dai_ref, obr_ref, obi_ref, olr_ref, oli_ref, odt_ref):
        dbr, dbi, b_r, b_i = dbr_ref[...], dbi_ref[...], br_ref[...], bi_ref[...]
        k_r, k_i, a_r, a_i = kr_ref[...], ki_ref[...], ar_ref[...], ai_ref[...]
        l_r, l_i = lr_ref[...], li_ref[...]
        dt = jnp.exp(ldt_ref[...])
        obr_ref[...] = k_r * dbr + k_i * dbi
        obi_ref[...] = k_r * dbi - k_i * dbr
        gk_r = jnp.sum(dbr * b_r + dbi * b_i, axis=0, keepdims=True)
        gk_i = jnp.sum(dbi * b_r - dbr * b_i, axis=0, keepdims=True)
        ga_r = jnp.sum(dar_ref[...], axis=0, keepdims=True)
        ga_i = jnp.sum(dai_ref[...], axis=0, keepdims=True)
        den = l_r * l_r + l_i * l_i
        c_r, c_i = l_r / den, l_i / den
        m_r, m_i = _cmul(c_r, c_i, gk_r, gk_i)
        g_r, g_i = ga_r + m_r, ga_i + m_i
        t1_r, t1_i = _cmul(dt * a_r, -dt * a_i, g_r, g_i)
        q_r, q_i = _cmul(k_r, k_i, c_r, -c_i)
        t2_r, t2_i = _cmul(-q_r, q_i, gk_r, gk_i)
        olr_ref[...] = t1_r + t2_r
        oli_ref[...] = t1_i + t2_i
        w_r, w_i = _cmul(l_r, l_i, a_r, a_i)
        odt_ref[...] = dt * (w_r * g_r + w_i * g_i)

    v = jax.ShapeDtypeStruct((1, N_STATES), f32)
    m = jax.ShapeDtypeStruct((SSM_GROUP, N_STATES), f32)
    return pl.pallas_call(body, name="ssm_param_bwd", out_shape=[m, m, v, v, v])(
        dbb_r, dbb_i, bt_re, bt_im, kr, ki, ar, ai, lr, li, ldt, da_r, da_i)


def group_sum(v):
    def body(v_ref, o_ref):
        o_ref[...] = jnp.sum(v_ref[...], axis=-1, keepdims=True)
    return pl.pallas_call(body, name="group_sum", out_shape=jax.ShapeDtypeStruct((v.shape[0], 1), f32))(v)


ATTN_SCALE = XATTN_HEAD_DIM ** -0.5


def _attn_probs(q_h, k_h):
    s = lax.dot_general(q_h, k_h, NT_DIMS, preferred_element_type=f32) * ATTN_SCALE
    e = jnp.exp(s - jnp.max(s, axis=-1, keepdims=True))
    return e / jnp.sum(e, axis=-1, keepdims=True)


def attn_fwd(proj, kv, wxo3, tm=TM):
    S = proj.shape[0]
    M = kv.shape[0]
    J, _, n = wxo3.shape

    def body(q_ref, kv_ref, w_ref, o_ref, y_ref):
        for h in range(XATTN_HEADS):
            cols = slice(h * XATTN_HEAD_DIM, (h + 1) * XATTN_HEAD_DIM)
            q_h = q_ref[:, cols].astype(bf16)
            k_h = kv_ref[:, cols]
            v_h = kv_ref[:, XATTN_WIDTH + h * XATTN_HEAD_DIM:XATTN_WIDTH + (h + 1) * XATTN_HEAD_DIM]
            p = _attn_probs(q_h, k_h)
            o_ref[:, cols] = jnp.dot(p.astype(bf16), v_h, preferred_element_type=f32).astype(bf16)
        o = o_ref[...]
        for j in range(J):
            y_ref[:, j * n:(j + 1) * n] = jnp.dot(o, w_ref[j], preferred_element_type=f32).astype(bf16)

    return pl.pallas_call(
        body, name="attn_fwd", grid=(S // tm,),
        in_specs=[_rows(tm, XATTN_WIDTH, 10), _full((M, 2 * XATTN_WIDTH)), _full(wxo3.shape)],
        out_specs=[_rows(tm, XATTN_WIDTH), _rows(tm, J * n)],
        out_shape=[jax.ShapeDtypeStruct((S, XATTN_WIDTH), bf16), jax.ShapeDtypeStruct((S, J * n), bf16)],
        compiler_params=_params())(proj, kv, wxo3)


def attn_bwd(dy_c, o_att, wxo3, proj, kv, dproj, tm=TM):
    S = proj.shape[0]
    M = kv.shape[0]
    J, _, n = wxo3.shape
    nt = S // tm

    def body(dy_ref, o_ref, w_ref, q_ref, kv_ref, _, dq_ref, dkv_ref, dwo_ref, acc_ref):
        i = pl.program_id(0)

        @pl.when(i == 0)
        def _():
            dkv_ref[...] = jnp.zeros_like(dkv_ref)

        part = lax.dot_general(o_ref[...], dy_ref[...], TN_DIMS, preferred_element_type=f32)

        @pl.when(i == 0)
        def _():
            acc_ref[...] = part

        @pl.when(i > 0)
        def _():
            acc_ref[...] += part

        @pl.when(i == nt - 1)
        def _():
            for j in range(J):
                dwo_ref[j] = acc_ref[:, j * n:(j + 1) * n].astype(bf16)

        do = None
        for j in range(J):
            part = lax.dot_general(dy_ref[:, j * n:(j + 1) * n], w_ref[j], NT_DIMS, preferred_element_type=f32)
            do = part if do is None else do + part
        do = do.astype(bf16)
        for h in range(XATTN_HEADS):
            cols = slice(h * XATTN_HEAD_DIM, (h + 1) * XATTN_HEAD_DIM)
            vcols = slice(XATTN_WIDTH + h * XATTN_HEAD_DIM, XATTN_WIDTH + (h + 1) * XATTN_HEAD_DIM)
            q_h = q_ref[:, cols].astype(bf16)
            k_h = kv_ref[:, cols]
            v_h = kv_ref[:, vcols]
            do_h = do[:, cols]
            p = _attn_probs(q_h, k_h)
            dp = lax.dot_general(do_h, v_h, NT_DIMS, preferred_element_type=f32)
            ds = (p * (dp - jnp.sum(dp * p, axis=-1, keepdims=True)) * ATTN_SCALE).astype(bf16)
            dq_ref[:, cols] = jnp.dot(ds, k_h, preferred_element_type=f32).astype(bf16)
            dkv_ref[:, cols] += lax.dot_general(ds, q_h, TN_DIMS, preferred_element_type=f32)
            dkv_ref[:, vcols] += lax.dot_general(p.astype(bf16), do_h, TN_DIMS, preferred_element_type=f32)

    return pl.pallas_call(
        body, name="attn_bwd", grid=(S // tm,),
        in_specs=[_rows(tm, J * n), _rows(tm, XATTN_WIDTH), _full(wxo3.shape), _rows(tm, XATTN_WIDTH, 10),
                  _full((M, 2 * XATTN_WIDTH)), pl.BlockSpec(memory_space=pl.ANY)],
        out_specs=[_rows(tm, XATTN_WIDTH, 10), _full((M, 2 * XATTN_WIDTH)), _full(wxo3.shape)],
        out_shape=[jax.ShapeDtypeStruct(dproj.shape, bf16), jax.ShapeDtypeStruct((M, 2 * XATTN_WIDTH), f32),
                   jax.ShapeDtypeStruct(wxo3.shape, bf16)],
        scratch_shapes=[pltpu.VMEM((XATTN_WIDTH, J * n), f32)], input_output_aliases={5: 0},
        compiler_params=_params())(dy_c, o_att, wxo3, proj, kv, dproj)


def _ln_stats(r):
    mu = jnp.mean(r, axis=-1, keepdims=True)
    xc = r - mu
    var = jnp.mean(xc * xc, axis=-1, keepdims=True)
    rstd = lax.rsqrt(var + LN_EPS)
    return xc * rstd, rstd


def _ln_bwd(dy, xhat, rstd, g):
    dxh = dy * g
    return rstd * (dxh - jnp.mean(dxh, axis=-1, keepdims=True) - xhat * jnp.mean(dxh * xhat, axis=-1, keepdims=True))


def _colsum(v):
    return jnp.sum(v, axis=0, keepdims=True)


def _ln2_loss_epi(t2, h1, target, b_down, g, b):
    xhat, rstd = _ln_stats(ALPHA * h1 + t2 + b_down)
    err = xhat * g + b - target
    dout = err * (1.0 / D_MODEL)
    dr = _ln_bwd(dout, xhat, rstd, g)
    return (dr, dr), (0.5 * _colsum(err * err) * (1.0 / D_MODEL), _colsum(dout * xhat), _colsum(dout), _colsum(dr))


def _merge_fwd_pro(g0, g1, g2, y_a, glu_a, glu_b, y_c, b):
    gs = [_sig(g.astype(f32) + b[:, k * D_MODEL:(k + 1) * D_MODEL]) for k, g in enumerate((g0, g1, g2))]
    y_b = glu_a.astype(f32) * _sig(glu_b.astype(f32))
    return gs[0] * y_a.astype(f32) + gs[1] * y_b + gs[2] * y_c.astype(f32)


def _merge_bwd_epi(dm, g0, g1, g2, y_a, glu_a, glu_b, y_c, b):
    ga, sb = glu_a.astype(f32), _sig(glu_b.astype(f32))
    ys = (y_a.astype(f32), ga * sb, y_c.astype(f32))
    gs = [_sig(g.astype(f32) + b[:, k * D_MODEL:(k + 1) * D_MODEL]) for k, g in enumerate((g0, g1, g2))]
    dpre = [dm * ys[k] * gs[k] * (1.0 - gs[k]) for k in range(3)]
    dyb = dm * gs[1]
    dglu = jnp.concatenate([dyb * sb, dyb * ga * sb * (1.0 - sb)], axis=1)
    return ((jnp.concatenate(dpre, axis=1), dm * gs[0], dglu, dm * gs[2]),
            (jnp.concatenate([_colsum(d) for d in dpre], axis=1),))


def _ln1_fwd_epi(t1, x, g, b):
    xhat, _ = _ln_stats(ALPHA * x + t1)
    h = xhat * g + b
    return (t1, h, h), ()


def _ln1_bwd_epi(t3, x, t1, dr2, g):
    xhat, rstd = _ln_stats(ALPHA * x + t1)
    dh = ALPHA * dr2 + t3
    dr = _ln_bwd(dh, xhat, rstd, g)
    return (dr, dr), (_colsum(dh * xhat), _colsum(dh))


def exchange(*rides, name):
    return _pcall(lambda: None, [], name=name, grid=(), in_specs=[], out_specs=[], out_shape=[], rides=rides)[1]


ROW_TILE = 256


def sum_slots(recv, name):
    n, R, C = recv.shape
    tm = min(R, ROW_TILE)

    def body(r_ref, o_ref):
        acc = r_ref[0].astype(f32)
        for k in range(1, n):
            acc = acc + r_ref[k].astype(f32)
        o_ref[...] = acc

    return pl.pallas_call(
        body, name=name, grid=(R // tm,), in_specs=[pl.BlockSpec((n, tm, C), lambda i: (0, i, 0))],
        out_specs=_rows(tm, C), out_shape=jax.ShapeDtypeStruct((R, C), f32), compiler_params=_params())(recv)


def _adamw(w, g, m, v):
    m = ADAM_B1 * m + (1.0 - ADAM_B1) * g
    v = ADAM_B2 * v + (1.0 - ADAM_B2) * (g * g)
    m_hat = m / (1.0 - ADAM_B1 ** ADAM_STEP)
    v_hat = v / (1.0 - ADAM_B2 ** ADAM_STEP)
    delta = -ADAM_LR * (m_hat / (jnp.sqrt(v_hat) + ADAM_EPS) + ADAM_WD * w)
    return delta, m, v


def adam_pair(p, q, w, m, v, name):
    R, C = w.shape
    tm = min(R, ROW_TILE)

    def body(p_ref, q_ref, w_ref, m_ref, v_ref, g_ref, d_ref, nm_ref, nv_ref):
        g = p_ref[...] + q_ref[...]
        g_ref[...] = g
        d_ref[...], nm_ref[...], nv_ref[...] = _adamw(w_ref[...], g, m_ref[...], v_ref[...])

    o = jax.ShapeDtypeStruct((R, C), f32)
    return pl.pallas_call(body, name=name, grid=(R // tm,), in_specs=[_rows(tm, C)] * 5,
                          out_specs=[_rows(tm, C)] * 4, out_shape=[o] * 4, compiler_params=_params())(p, q, w, m, v)


def adam_slots(recv, w, m, v, name):
    n = recv.shape[0]

    def body(r_ref, w_ref, m_ref, v_ref, g_ref, d_ref, nm_ref, nv_ref):
        g = r_ref[0]
        for k in range(1, n):
            g = g + r_ref[k]
        g_ref[...] = g
        d_ref[...], nm_ref[...], nv_ref[...] = _adamw(w_ref[...], g, m_ref[...], v_ref[...])

    o = jax.ShapeDtypeStruct(w.shape, f32)
    return pl.pallas_call(body, name=name, out_shape=[o] * 4, compiler_params=_params())(recv, w, m, v)


BIG = ("w_in", "w_conv_out", "w_glu", "w_xattn_out", "w_kv", "w_out", "w_up", "w_down")
MID = ("w_conv_out", "w_glu", "w_xattn_out", "w_kv", "w_out")
SMALL = ("b_gate", "ssm_lam_re", "ssm_lam_im", "ssm_log_dt", "ssm_b_re", "ssm_b_im", "ssm_c_re", "ssm_c_im", "ssm_d",
         "ln1_g", "ln1_b", "b_up", "b_down", "ln2_g", "ln2_b")
def _pad_flat(a, mult=1024):
    a = a.reshape(-1)
    return jnp.pad(a, (0, (-a.shape[0]) % mult))


def _perm(a):
    S, W = a.shape
    return a.reshape(N_SEG, S // N_SEG, W).transpose(1, 0, 2).reshape(S, W)


def _unperm(a):
    S, W = a.shape
    return a.reshape(S // N_SEG, N_SEG, W).transpose(1, 0, 2).reshape(S, W)


def _state_rows(a):
    return a.transpose(2, 0, 1).reshape(SSM_GROUP, N_STATES)


def _block_diag_b(bt):
    b4 = bt.reshape(SSM_GROUP, 4, 8, SSM_STATE)
    eye = jnp.eye(8, dtype=bt.dtype)
    return jnp.einsum("hcgp,gk->cghkp", b4, eye).reshape(4, 128, 512)


def _block_diag_c(c):
    c4 = c.reshape(4, 8, SSM_GROUP, SSM_STATE)
    eye = jnp.eye(8, dtype=c.dtype)
    return jnp.einsum("cghp,gk->cgpkh", c4, eye).reshape(4, 512, 128)


def _diag_b(acc):
    a = acc.reshape(4, 8, SSM_STATE, 8, SSM_GROUP)
    eye = jnp.eye(8, dtype=acc.dtype)
    return jnp.einsum("cgpkh,gk->hcgp", a, eye).reshape(SSM_GROUP, N_STATES)


def _diag_c(acc):
    a = acc.reshape(4, 8, SSM_GROUP, 8, SSM_STATE)
    eye = jnp.eye(8, dtype=acc.dtype)
    return jnp.einsum("cghkp,gk->cghp", a, eye).reshape(SSM_GROUPS, SSM_GROUP, SSM_STATE)


def kernel(x, mem, w_in, b_gate, conv_w, w_conv_out, ssm_lam_re, ssm_lam_im, ssm_log_dt, ssm_b_re, ssm_b_im, ssm_c_re, ssm_c_im, ssm_d, w_glu, w_kv, w_xattn_out, w_out, ln1_g, ln1_b, w_up, b_up, w_down, b_down, ln2_g, ln2_b, loss_target, m_w_in, m_b_gate, m_conv_w, m_w_conv_out, m_ssm_lam_re, m_ssm_lam_im, m_ssm_log_dt, m_ssm_b_re, m_ssm_b_im, m_ssm_c_re, m_ssm_c_im, m_ssm_d, m_w_glu, m_w_kv, m_w_xattn_out, m_w_out, m_ln1_g, m_ln1_b, m_w_up, m_b_up, m_w_down, m_b_down, m_ln2_g, m_ln2_b, v_w_in, v_b_gate, v_conv_w, v_w_conv_out, v_ssm_lam_re, v_ssm_lam_im, v_ssm_log_dt, v_ssm_b_re, v_ssm_b_im, v_ssm_c_re, v_ssm_c_im, v_ssm_d, v_w_glu, v_w_kv, v_w_xattn_out, v_w_out, v_ln1_g, v_ln1_b, v_w_up, v_b_up, v_w_down, v_b_down, v_ln2_g, v_ln2_b):
    W = dict(w_in=w_in, b_gate=b_gate, conv_w=conv_w, w_conv_out=w_conv_out, ssm_lam_re=ssm_lam_re,
             ssm_lam_im=ssm_lam_im, ssm_log_dt=ssm_log_dt, ssm_b_re=ssm_b_re, ssm_b_im=ssm_b_im, ssm_c_re=ssm_c_re,
             ssm_c_im=ssm_c_im, ssm_d=ssm_d, w_glu=w_glu, w_kv=w_kv, w_xattn_out=w_xattn_out, w_out=w_out,
             ln1_g=ln1_g, ln1_b=ln1_b, w_up=w_up, b_up=b_up, w_down=w_down, b_down=b_down, ln2_g=ln2_g, ln2_b=ln2_b)
    MOM = dict(w_in=m_w_in, b_gate=m_b_gate, conv_w=m_conv_w, w_conv_out=m_w_conv_out, ssm_lam_re=m_ssm_lam_re,
               ssm_lam_im=m_ssm_lam_im, ssm_log_dt=m_ssm_log_dt, ssm_b_re=m_ssm_b_re, ssm_b_im=m_ssm_b_im,
               ssm_c_re=m_ssm_c_re, ssm_c_im=m_ssm_c_im, ssm_d=m_ssm_d, w_glu=m_w_glu, w_kv=m_w_kv,
               w_xattn_out=m_w_xattn_out, w_out=m_w_out, ln1_g=m_ln1_g, ln1_b=m_ln1_b, w_up=m_w_up, b_up=m_b_up,
               w_down=m_w_down, b_down=m_b_down, ln2_g=m_ln2_g, ln2_b=m_ln2_b)
    VEL = dict(w_in=v_w_in, b_gate=v_b_gate, conv_w=v_conv_w, w_conv_out=v_w_conv_out, ssm_lam_re=v_ssm_lam_re,
               ssm_lam_im=v_ssm_lam_im, ssm_log_dt=v_ssm_log_dt, ssm_b_re=v_ssm_b_re, ssm_b_im=v_ssm_b_im,
               ssm_c_re=v_ssm_c_re, ssm_c_im=v_ssm_c_im, ssm_d=v_ssm_d, w_glu=v_w_glu, w_kv=v_w_kv,
               w_xattn_out=v_w_xattn_out, w_out=v_w_out, ln1_g=v_ln1_g, ln1_b=v_ln1_b, w_up=v_w_up, b_up=v_b_up,
               w_down=v_w_down, b_down=v_b_down, ln2_g=v_ln2_g, ln2_b=v_ln2_b)
    names = list(W)
    xy = 2 * lax.axis_index("x") + lax.axis_index("y")

    xs = x[0]
    S = xs.shape[0]
    mems = mem[0]
    tgt = loss_target[0]

    shard_bf = {n: W[n][0].astype(bf16) for n in BIG}

    lr = ssm_lam_re.reshape(1, N_STATES)
    li = ssm_lam_im.reshape(1, N_STATES)
    ldt = jnp.repeat(ssm_log_dt.reshape(SSM_GROUPS), SSM_STATE).reshape(1, N_STATES)
    bt_re, bt_im = _state_rows(ssm_b_re[0]), _state_rows(ssm_b_im[0])
    ar, ai, kr, ki, bbt_r, bbt_i = ssm_prep(lr, li, ldt, bt_re, bt_im)
    bd_r, bd_i = _block_diag_b(bbt_r).astype(bf16), _block_diag_b(bbt_i).astype(bf16)
    cd_r, cd_i = _block_diag_c(ssm_c_re[0]).astype(bf16), _block_diag_c(ssm_c_im[0]).astype(bf16)
    bdt_r, bdt_i = bd_r.transpose(0, 2, 1), bd_i.transpose(0, 2, 1)
    cdt_r, cdt_i = cd_r.transpose(0, 2, 1), cd_i.transpose(0, 2, 1)
    d_skip = ssm_d.reshape(1, SSM_WIDTH)

    mem_bf = mems.astype(bf16)
    u_col = GATE_COLS + 3 * CONV_WIDTH
    me_xy = jnp.reshape(xy, (1,)).astype(jnp.int32)
    proj, x_bf, (win3,) = proj_own(xs, shard_bf["w_in"], me_xy, 4, rides=(("gather2", shard_bf["w_in"]),))
    proj, (wco3, wglu3, wxo3, wkv4, wout4, convw4) = proj_rest(
        x_bf, win3, me_xy, proj, rides=tuple(("gather", shard_bf[n]) for n in MID) + (("gather", conv_w[0]),))
    wkv3 = wkv4.reshape(1, D_MODEL, 2 * XATTN_WIDTH)
    wout3 = wout4.reshape(1, D_MODEL, D_MODEL)
    convw_full = convw4.transpose(1, 0, 2).reshape(3, CONV_WIDTH)
    a_conv, y_a = conv_fwd(proj, convw_full, wco3)
    u_perm = _perm(proj[:, u_col:u_col + SSM_WIDTH])
    half = D_MODEL // 2
    (init_r, init_i), (wup_a,) = ssm_scan_fwd(u_perm, bd_r, bd_i, ar, ai,
                                              rides=(("gather", shard_bf["w_up"][:half]),))
    (ysm_perm, ys_perm, st_r, st_i), (wup_b,) = ssm_scan_fwd(
        u_perm, bd_r, bd_i, ar, ai, final=(cd_r, cd_i, d_skip, init_r, init_i),
        rides=(("gather", shard_bf["w_up"][half:]),))
    wup3 = jnp.concatenate([wup_a, wup_b], axis=1)
    y_s = _unperm(ys_perm)
    glu = mm_fwd_small(y_s, wglu3, "glu", out_dtype=bf16)
    kv = mm_fwd_small(mem_bf, wkv3, "kv", out_dtype=bf16)
    o_att, y_c = attn_fwd(proj, kv, wxo3)
    merge_rows = ((proj, D_MODEL, 0), (proj, D_MODEL, 1), (proj, D_MODEL, 2), y_a, (glu, D_MODEL, 0),
                  (glu, D_MODEL, 1), y_c)
    merged, t1, h1, h1_bf = mm_rows_fused(
        (_merge_fwd_pro, merge_rows, (b_gate,)), wout3, "merge_w_out_ln1", transposed=False, tm=256,
        epi=_ln1_fwd_epi, extras=(xs,), consts=(ln1_g, ln1_b), outs=(f32, f32, bf16))
    (r_up, hdn), (wdn4,) = mm_fwd(h1_bf, wup3, "w_up", bias=b_up, outs=(bf16, bf16), second=_relu2,
                                  first=lambda v: jnp.maximum(v, 0.0), rides=(("gather", shard_bf["w_down"]),))
    wdn3 = wdn4.reshape(1, D_FF, D_MODEL)

    dr2, dr2_bf, loss_cols, d_ln2_g, d_ln2_b, d_b_down = mm_rows_fused(
        hdn, wdn3, "w_down_ln2_loss", transposed=False, tm=512, epi=_ln2_loss_epi, extras=(h1, tgt),
        consts=(b_down, ln2_g, ln2_b), outs=(f32, bf16), nsums=4)
    part, other = {}, {}
    g_w_down = mm_bwd_w(hdn, dr2_bf, 1, "dw_down", tk=1024).reshape(4, -1, D_MODEL)
    (dup, d_b_up), (recv_dn,) = mm_bwd_x(dr2_bf, wdn3, "dup", tm=512, extras=(r_up,), colsum=True, out_dtype=bf16,
                                         epi=lambda acc, r: acc * (2.0 * r.astype(f32)),
                                         rides=(("scatter", g_w_down),))
    part["w_down"] = sum_slots(recv_dn, "sum_w_down")
    g_w_up = mm_bwd_w(h1_bf, dup, 4, "dw_up")
    dr1, dr1_bf, d_ln1_g, d_ln1_b = mm_rows_fused(
        dup, wup3, "dh1_ln1_bwd", transposed=True, tm=256, epi=_ln1_bwd_epi, extras=(xs, t1, dr2), consts=(ln1_g,),
        outs=(f32, bf16), nsums=2)
    g_w_out = mm_bwd_w(merged, dr1_bf, 1, "dw_out").reshape(4, -1, D_MODEL)
    (dproj, dy_a, dglu, dy_c, d_b_gate), (recv_up,) = mm_rows_fused(
        dr1_bf, wout3, "dmerged_merge_bwd", transposed=True, tm=256, epi=_merge_bwd_epi, extras=merge_rows,
        consts=(b_gate,), outs=((bf16, GATE_COLS, IN_COLS), bf16, (bf16, 2 * D_MODEL, 2 * D_MODEL), bf16),
        sum_widths=(GATE_COLS,), rides=(("scatter", g_w_up),))
    part["w_up"] = sum_slots(recv_up, "sum_w_up")

    dproj, d_conv_w, g_w_co = conv_bwd(dy_a, a_conv, wco3, proj, convw_full, dproj)

    g_w_glu = mm_bwd_w_small(y_s, dglu, 4, "dw_glu")
    dys_perm = _perm(mm_bwd_x(dglu, wglu3, "dy_s", tm=1024))
    linit_r, linit_i = ssm_scan_bwd(dys_perm, ysm_perm, cdt_r, cdt_i, ar, ai)
    (du_perm, dbacc_r, dbacc_i, dcacc_r, dcacc_i, da_r, da_i, d_ssm_d), (recv_co, recv_glu, recv_out) = ssm_scan_bwd(
        dys_perm, ysm_perm, cdt_r, cdt_i, ar, ai,
        final=(u_perm, st_r, st_i, bdt_r, bdt_i, d_skip, linit_r, linit_i),
        rides=(("scatter", g_w_co), ("scatter", g_w_glu), ("scatter", g_w_out)))
    part["w_out"] = sum_slots(recv_out, "sum_w_out")
    part["w_conv_out"] = sum_slots(recv_co, "sum_w_conv_out")
    part["w_glu"] = sum_slots(recv_glu, "sum_w_glu")
    dbt_re, dbt_im, d_lr, d_li, d_ldt_state = ssm_param_bwd(
        _diag_b(dbacc_r), _diag_b(dbacc_i), bt_re, bt_im, kr, ki, ar, ai, lr, li, ldt, da_r, da_i)
    d_log_dt = group_sum(d_ldt_state.reshape(SSM_GROUPS, SSM_STATE))
    d_b_re = dbt_re.reshape(SSM_GROUP, SSM_GROUPS, SSM_STATE).transpose(1, 2, 0)
    d_b_im = dbt_im.reshape(SSM_GROUP, SSM_GROUPS, SSM_STATE).transpose(1, 2, 0)
    d_c_re = _diag_c(dcacc_r)
    d_c_im = -_diag_c(dcacc_i)

    dproj, dkv, g_w_xo = attn_bwd(dy_c, o_att, wxo3, proj, kv, dproj)
    g_w_kv = mm_bwd_w(mem_bf, dkv, 1, "dw_kv").reshape(4, -1, D_MODEL)

    dproj = lax.dynamic_update_slice(dproj, _unperm(du_perm), (0, u_col))
    g_small = {"b_gate": d_b_gate, "ssm_lam_re": d_lr, "ssm_lam_im": d_li, "ssm_log_dt": d_log_dt, "ssm_b_re": d_b_re,
               "ssm_b_im": d_b_im, "ssm_c_re": d_c_re, "ssm_c_im": d_c_im, "ssm_d": d_ssm_d, "ln1_g": d_ln1_g,
               "ln1_b": d_ln1_b, "b_up": d_b_up, "b_down": d_b_down, "ln2_g": d_ln2_g, "ln2_b": d_ln2_b}
    small_names = SMALL + ("conv_w",)
    g_small["conv_w"] = d_conv_w
    sizes = {n: (g_small[n].size + 1023) // 1024 * 1024 for n in small_names}
    pack = lambda d: jnp.concatenate([_pad_flat(d[n]) for n in small_names]).reshape(-1, 128)
    early = ("w_down", "w_up", "w_out", "w_conv_out", "w_glu")
    g_w_in, landed = mm_bwd_w(
        x_bf, dproj, 4, "dw_in", rides=(("scatter", g_w_xo), ("scatter", g_w_kv), ("all", pack(g_small)))
        + tuple(("pair", part[n]) for n in early))
    recv_xo, recv_kv, srecv = landed[:3]
    other.update(zip(early, landed[3:]))
    part["w_xattn_out"] = sum_slots(recv_xo, "sum_w_xattn_out")
    part["w_kv"] = sum_slots(recv_kv, "sum_w_kv")
    dx, (recv_in, other["w_xattn_out"], other["w_kv"]) = mm_bwd_x(
        dproj, win3, "dx", extras=(dr1,), epi=lambda acc, d: acc + ALPHA * d,
        rides=(("scatter", g_w_in), ("pair", part["w_xattn_out"]), ("pair", part["w_kv"])))
    part["w_in"] = sum_slots(recv_in, "sum_w_in")
    other["w_in"], = exchange(("pair", part["w_in"]), name="swap_w_in")
    res = [{}, {}, {}, {}]
    for n in BIG:
        for k, r in enumerate(adam_pair(part[n], other[n], W[n][0], MOM[n][0], VEL[n][0], "adam_" + n)):
            res[k][n] = r[None]
    conv_zero = jnp.zeros((3, CONV_WIDTH), f32)
    gs, ds_, ms, vs = adam_slots(srecv, pack({**{n: W[n] for n in SMALL}, "conv_w": conv_zero}),
                                 pack({**{n: MOM[n] for n in SMALL}, "conv_w": conv_zero}),
                                 pack({**{n: VEL[n] for n in SMALL}, "conv_w": conv_zero}), "adam_small")

    def unpack_small(buf):
        flat = buf.reshape(-1)
        out, r = {}, 0
        for n in small_names:
            ref = g_small[n] if n == "conv_w" else W[n]
            out[n] = flat[r:r + ref.size].reshape(ref.shape)
            r += sizes[n]
        return out

    res_s = [unpack_small(b) for b in (gs, ds_, ms, vs)]
    g_conv = lax.dynamic_slice(res_s[0]["conv_w"], (0, xy * 128), (3, 128))
    conv_slots = g_conv.reshape(1, 3, 128)
    cg, cd, cm, cv = adam_slots(conv_slots, conv_w[0], m_conv_w[0], v_conv_w[0], "adam_conv")
    conv_res = [cg, cd, cm, cv]

    loss = lax.psum(jnp.sum(loss_cols), ("x", "y", "c"))
    outs = [loss, dx.reshape(x.shape)]
    for k in range(4):
        for n in names:
            if n == "conv_w":
                outs.append(conv_res[k].reshape(conv_w.shape))
            elif n in BIG:
                outs.append(res[k][n])
            else:
                outs.append(res_s[k][n])
    return tuple(outs)
```

```python
import functools
import math

import jax
import jax.numpy as jnp
from jax import lax
from jax.experimental import pallas as pl
from jax.experimental.pallas import tpu as pltpu

f32 = jnp.float32
bf16 = jnp.bfloat16

D_MODEL = 1024
CONV_WIDTH = 512
SSM_WIDTH = 512
SSM_GROUP = 16
SSM_GROUPS = 32
SSM_STATE = 64
N_STATES = SSM_GROUPS * SSM_STATE
XATTN_HEADS = 4
XATTN_HEAD_DIM = 128
XATTN_WIDTH = 512
D_FF = 4096
GATE_COLS = 3 * D_MODEL
IN_COLS = GATE_COLS + 3 * CONV_WIDTH + SSM_WIDTH + XATTN_WIDTH
ALPHA = 2.0 ** 0.25
LN_EPS = 1e-5
ADAM_LR = 0.001
ADAM_B1 = 0.9
ADAM_B2 = 0.999
ADAM_EPS = 1e-08
ADAM_WD = 0.01
ADAM_STEP = 10

N_SEG = 8
SCAN_ROWS = 512
LANE_STRIP = 512
TM = 512
VMEM_LIMIT = 48 * 1024 * 1024
MESH = pl.DeviceIdType.MESH

NT_DIMS = (((1,), (1,)), ((), ()))
TN_DIMS = (((0,), (0,)), ((), ()))


def _params():
    return pltpu.CompilerParams(vmem_limit_bytes=VMEM_LIMIT)


def _full(shape):
    n = len(shape)
    return pl.BlockSpec(shape, lambda *_: (0,) * n)


def _rows(tm, w, cb=0):
    return pl.BlockSpec((tm, w), lambda i: (i, cb))


def _sig(x):
    return 1.0 / (1.0 + jnp.exp(-x))


HBM_SPEC = pl.BlockSpec(memory_space=pl.ANY)
RIDE_PEERS = {"gather": 3, "gather2": 6, "scatter": 3, "all": 7, "pair": 1}


def _xy_peers(x, y):
    return [(1 - x, y), (x, 1 - y), (1 - x, 1 - y)]


def _ride_copies(kind, src, dst, send_sems, recv_sems, local_sem):
    x, y, c = lax.axis_index("x"), lax.axis_index("y"), lax.axis_index("c")
    me = 2 * x + y
    if kind == "all":
        flips = [(fx, fy, fc) for fx in (0, 1) for fy in (0, 1) for fc in (0, 1)][1:]
        peers = [(x ^ fx, y ^ fy, c ^ fc) for fx, fy, fc in flips]
        slot = lambda p: 4 * p[0] + 2 * p[1] + p[2]
        mine = slot((x, y, c))
    else:
        peers = [(px, py, c) for px, py in _xy_peers(x, y)]
        slot = lambda p: 2 * p[0] + p[1]
        mine = me

    def remote(k, s, d):
        return pltpu.make_async_remote_copy(src_ref=s, dst_ref=d, send_sem=send_sems.at[k], recv_sem=recv_sems.at[k],
                                            device_id=peers[k], device_id_type=MESH)

    if kind == "pair":
        peers = [(x, y, 1 - c)]
        return None, [remote(0, src, dst)], [remote(0, src, dst)]
    if kind == "scatter":
        local = pltpu.make_async_copy(src.at[me], dst.at[me], local_sem)
        sends = [remote(k, src.at[slot(p)], dst.at[me]) for k, p in enumerate(peers)]
        lands = [remote(k, src.at[me], dst.at[slot(p)]) for k, p in enumerate(peers)]
    else:
        local = pltpu.make_async_copy(src, dst.at[mine], local_sem)
        sends = [remote(k, src, dst.at[mine]) for k, p in enumerate(peers)]
        lands = [remote(k, src, dst.at[slot(p)]) for k, p in enumerate(peers)]
    return local, sends, lands


def _ride_shape(kind, src):
    lead = {"gather": (4,), "gather2": (4,), "scatter": (), "all": (8,), "pair": ()}[kind]
    return jax.ShapeDtypeStruct(lead + src.shape, src.dtype)


def _two_level(src, dst, send_sems, recv_sems, local_sem):
    x, y, c = lax.axis_index("x"), lax.axis_index("y"), lax.axis_index("c")
    me = 2 * x + y
    h = src.shape[0] // 2
    mine = pl.ds(pl.multiple_of(c * h, 16), h)
    theirs = pl.ds(pl.multiple_of((1 - c) * h, 16), h)
    peers = _xy_peers(x, y)
    slots = [2 * px + py for px, py in peers]

    def over_ici(k, d):
        return pltpu.make_async_remote_copy(src_ref=src.at[mine], dst_ref=d, send_sem=send_sems.at[k],
                                            recv_sem=recv_sems.at[k], device_id=(*peers[k], c), device_id_type=MESH)

    def over_d2d(k, rows):
        blk = dst.at[slots[k], rows]
        return pltpu.make_async_remote_copy(src_ref=blk, dst_ref=blk, send_sem=send_sems.at[3 + k],
                                            recv_sem=recv_sems.at[3 + k], device_id=(x, y, 1 - c), device_id_type=MESH)

    return (pltpu.make_async_copy(src, dst.at[me], local_sem),
            [over_ici(k, dst.at[me, mine]) for k in range(3)], [over_ici(k, dst.at[slots[k], mine]) for k in range(3)],
            [over_d2d(k, mine) for k in range(3)], [over_d2d(k, theirs) for k in range(3)])


def _pcall(body, args, *, name, grid, in_specs, out_specs, out_shape, scratch_shapes=(), rides=(), aliases=None,
           prefetch=()):
    n_in, n_out, n_scr, nr, npf = len(in_specs), len(out_specs), len(scratch_shapes), len(rides), len(prefetch)
    kinds = [k for k, _ in rides]

    def wrapped(*refs):
        pre, refs = refs[:npf], refs[npf:]
        ins, rsrc = refs[:n_in], refs[n_in:n_in + nr]
        outs = refs[n_in + nr:n_in + nr + n_out]
        rdst = refs[n_in + nr + n_out:n_in + 2 * nr + n_out]
        scr = refs[n_in + 2 * nr + n_out:n_in + 2 * nr + n_out + n_scr]
        sems = refs[n_in + 2 * nr + n_out + n_scr:]

        def start():
            for r, kind in enumerate(kinds):
                if kind == "gather2":
                    local, sends, _, _, _ = _two_level(rsrc[r], rdst[r], *sems[3 * r:3 * r + 3])
                else:
                    local, sends, _ = _ride_copies(kind, rsrc[r], rdst[r], *sems[3 * r:3 * r + 3])
                if local is not None:
                    local.start()
                for cp in sends:
                    cp.start()

        def finish():
            for r, kind in enumerate(kinds):
                if kind == "gather2":
                    local, sends, ici_lands, forwards, lands = _two_level(rsrc[r], rdst[r], *sems[3 * r:3 * r + 3])
                    for k in range(3):
                        ici_lands[k].wait_recv()
                        forwards[k].start()
                    sends = sends + forwards
                else:
                    local, sends, lands = _ride_copies(kind, rsrc[r], rdst[r], *sems[3 * r:3 * r + 3])
                for cp in lands:
                    cp.wait_recv()
                for cp in sends:
                    cp.wait_send()
                if local is not None:
                    local.wait()

        if nr and grid:
            ids = [pl.program_id(a) for a in range(len(grid))]
            first = functools.reduce(jnp.logical_and, [i == 0 for i in ids])
            last = functools.reduce(jnp.logical_and, [i == g - 1 for i, g in zip(ids, grid)])
            pl.when(first)(start)
        elif nr:
            start()
        body(*pre, *ins, *outs, *scr)
        if nr and grid:
            pl.when(last)(finish)
        elif nr:
            finish()

    sems = []
    for kind in kinds:
        n = RIDE_PEERS[kind]
        sems += [pltpu.SemaphoreType.DMA((n,)), pltpu.SemaphoreType.DMA((n,)), pltpu.SemaphoreType.DMA(())]
    all_in = list(in_specs) + [HBM_SPEC] * nr
    all_out = list(out_specs) + [HBM_SPEC] * nr
    all_scratch = list(scratch_shapes) + sems
    if npf:
        how = dict(grid_spec=pltpu.PrefetchScalarGridSpec(num_scalar_prefetch=npf, grid=grid, in_specs=all_in,
                                                          out_specs=all_out, scratch_shapes=all_scratch))
    else:
        how = dict(grid=grid, in_specs=all_in, out_specs=all_out, scratch_shapes=all_scratch)
    res = pl.pallas_call(
        wrapped, name=name, out_shape=list(out_shape) + [_ride_shape(k, s) for k, s in rides],
        input_output_aliases=aliases or {}, compiler_params=_params(), **how)(
            *prefetch, *args, *[s for _, s in rides])
    return list(res[:n_out]), list(res[n_out:])


def _bf(v):
    return v if v.dtype == bf16 else v.astype(bf16)


def mm_fwd_small(a, w3, name, *, tm=1024, out_dtype=f32):
    M, K = a.shape
    J, _, n = w3.shape
    tm = min(tm, M)

    def body(a_ref, w_ref, o_ref):
        av = _bf(a_ref[...])
        for j in range(J):
            o_ref[:, j * n:(j + 1) * n] = jnp.dot(av, w_ref[j], preferred_element_type=f32).astype(out_dtype)

    return pl.pallas_call(
        body, name=name, grid=(M // tm,), in_specs=[_rows(tm, K), _full((J, K, n))], out_specs=_rows(tm, J * n),
        out_shape=jax.ShapeDtypeStruct((M, J * n), out_dtype), compiler_params=_params())(a, w3)


def mm_bwd_w_small(a, dy, J, name, *, tm=2048):
    M, K = a.shape
    n = dy.shape[1] // J
    tm = min(tm, M)
    ns = M // tm

    def body(a_ref, dy_ref, o_ref, acc_ref):
        s = pl.program_id(0)
        part = lax.dot_general(_bf(a_ref[...]), _bf(dy_ref[...]), TN_DIMS, preferred_element_type=f32)

        @pl.when(s == 0)
        def _():
            acc_ref[...] = part

        @pl.when(s > 0)
        def _():
            acc_ref[...] += part

        @pl.when(s == ns - 1)
        def _():
            for j in range(J):
                o_ref[j] = acc_ref[:, j * n:(j + 1) * n].astype(bf16)

    return pl.pallas_call(
        body, name=name, grid=(ns,), in_specs=[_rows(tm, K), _rows(tm, J * n)], out_specs=_full((J, K, n)),
        out_shape=jax.ShapeDtypeStruct((J, K, n), bf16), scratch_shapes=[pltpu.VMEM((K, J * n), f32)],
        compiler_params=_params())(a, dy)


def mm_fwd(a, w3, name, *, tm=1024, tn=None, bias=None, outs=(f32,), first=None, second=None, emit_a=False,
           rides=()):
    M, K = a.shape
    J, _, n = w3.shape
    tm = min(tm, M)
    tn = tn or n
    nl = n // tn
    nb = 0 if bias is None else 1

    def body(*refs):
        a_ref, w_ref = refs[0], refs[1]
        av = _bf(a_ref[...])
        acc = jnp.dot(av, w_ref[0], preferred_element_type=f32)
        if bias is not None:
            acc = acc + refs[2][...]
        refs[2 + nb][...] = (acc if first is None else first(acc)).astype(outs[0])
        if len(outs) > 1:
            refs[3 + nb][...] = second(acc).astype(outs[1])
        if emit_a:
            @pl.when(jnp.logical_and(pl.program_id(0) == 0, pl.program_id(1) == 0))
            def _():
                refs[2 + nb + len(outs)][...] = av

    in_specs = [pl.BlockSpec((tm, K), lambda j, l, i: (i, 0)),
                pl.BlockSpec((1, K, tn), lambda j, l, i: (j, 0, l))]
    args = [a, w3]
    if bias is not None:
        in_specs.append(pl.BlockSpec((1, tn), lambda j, l, i: (0, j * nl + l)))
        args.append(bias)
    out_specs = [pl.BlockSpec((tm, tn), lambda j, l, i: (i, j * nl + l)) for _ in outs]
    out_shape = [jax.ShapeDtypeStruct((M, J * n), dt) for dt in outs]
    if emit_a:
        last = M // tm - 1
        out_specs.append(pl.BlockSpec((tm, K), lambda j, l, i: (jnp.where((j == 0) & (l == 0), i, last), 0)))
        out_shape.append(jax.ShapeDtypeStruct((M, K), bf16))
    res, landed = _pcall(body, args, name=name, grid=(J, nl, M // tm), in_specs=in_specs, out_specs=out_specs,
                         out_shape=out_shape, rides=rides)
    res = res if len(res) > 1 else res[0]
    return (res, landed) if rides else res


def proj_own(x, w_own, me, n_blocks, *, tm=1024, rides=()):
    M, K = x.shape
    n = w_own.shape[1]

    def body(me_ref, x_ref, w_ref, o_ref, xb_ref):
        av = _bf(x_ref[...])
        xb_ref[...] = av
        o_ref[...] = jnp.dot(av, w_ref[...], preferred_element_type=f32).astype(bf16)

    (proj, x_bf), landed = _pcall(
        body, [x, w_own], name="proj_own", grid=(M // tm,), prefetch=(me,),
        in_specs=[pl.BlockSpec((tm, K), lambda i, me_ref: (i, 0)), pl.BlockSpec((K, n), lambda i, me_ref: (0, 0))],
        out_specs=[pl.BlockSpec((tm, n), lambda i, me_ref: (i, me_ref[0])),
                   pl.BlockSpec((tm, K), lambda i, me_ref: (i, 0))],
        out_shape=[jax.ShapeDtypeStruct((M, n_blocks * n), bf16), jax.ShapeDtypeStruct((M, K), bf16)], rides=rides)
    return proj, x_bf, landed


def proj_rest(x_bf, w3, me, proj, *, tm=1024, rides=()):
    M, K = x_bf.shape
    J, _, n = w3.shape

    def body(me_ref, x_ref, w_ref, _, o_ref):
        o_ref[...] = jnp.dot(x_ref[...], w_ref[0], preferred_element_type=f32).astype(bf16)

    (proj,), landed = _pcall(
        body, [x_bf, w3, proj], name="proj_rest", grid=(J - 1, M // tm), prefetch=(me,),
        in_specs=[pl.BlockSpec((tm, K), lambda k, i, me_ref: (i, 0)),
                  pl.BlockSpec((1, K, n), lambda k, i, me_ref: (me_ref[0] ^ (k + 1), 0, 0)), HBM_SPEC],
        out_specs=[pl.BlockSpec((tm, n), lambda k, i, me_ref: (i, me_ref[0] ^ (k + 1)))],
        out_shape=[jax.ShapeDtypeStruct(proj.shape, bf16)], aliases={3: 0}, rides=rides)
    return proj, landed


def mm_bwd_x(dy, w3, name, *, tm=TM, epi=None, extras=(), colsum=False, out_dtype=f32, rides=()):
    M = dy.shape[0]
    J, K, n = w3.shape
    tm = min(tm, M)
    nex = len(extras)

    def body(*refs):
        dy_ref, w_hbm = refs[0], refs[1]
        ex = refs[2:2 + nex]
        o_ref = refs[2 + nex]
        w_ref = refs[-1]
        i = pl.program_id(0)

        @pl.when(i == 0)
        def _():
            pltpu.sync_copy(w_hbm, w_ref)

        acc = None
        for j in range(J):
            part = lax.dot_general(_bf(dy_ref[:, j * n:(j + 1) * n]), w_ref[j], NT_DIMS, preferred_element_type=f32)
            acc = part if acc is None else acc + part
        if epi is not None:
            acc = epi(acc, *[e[...] for e in ex])
        o_ref[...] = acc.astype(out_dtype)
        if colsum:
            s_ref = refs[3 + nex]
            cs = jnp.sum(acc, axis=0, keepdims=True)

            @pl.when(i == 0)
            def _():
                s_ref[...] = cs

            @pl.when(i > 0)
            def _():
                s_ref[...] += cs

    in_specs = [pl.BlockSpec((tm, J * n), lambda i: (i, 0)), pl.BlockSpec(memory_space=pl.ANY)]
    in_specs += [pl.BlockSpec((tm, K), lambda i: (i, 0)) for _ in extras]
    out_specs = [pl.BlockSpec((tm, K), lambda i: (i, 0))]
    out_shape = [jax.ShapeDtypeStruct((M, K), out_dtype)]
    if colsum:
        out_specs.append(pl.BlockSpec((1, K), lambda i: (0, 0)))
        out_shape.append(jax.ShapeDtypeStruct((1, K), f32))
    res, landed = _pcall(body, [dy, w3, *extras], name=name, grid=(M // tm,), in_specs=in_specs, out_specs=out_specs,
                         out_shape=out_shape, scratch_shapes=[pltpu.VMEM((J, K, n), bf16)], rides=rides)
    res = res if colsum else res[0]
    return (res, landed) if rides else res


def mm_rows_fused(a, w3, name, *, transposed, tm, epi, extras=(), consts=(), outs=(f32,), nsums=0, sum_widths=None,
                  rides=()):
    pro = a[0] if isinstance(a, tuple) else None
    J, K, n = w3.shape
    W = K if transposed else n
    ka = J * n if transposed else K
    if pro is not None:
        a_rows = [e if isinstance(e, tuple) else (e, ka, 0) for e in a[1]]
        a_consts = list(a[2])
    else:
        a_rows, a_consts = [(a, ka, 0)], []
    M = a_rows[0][0].shape[0]
    tm = min(tm, M)
    na = len(a_rows) + len(a_consts)
    nex, nco = len(extras), len(consts)
    if pro is not None:
        outs = (bf16 if ka == W else (bf16, ka, ka),) + tuple(outs)
    nout = len(outs)
    nsums = len(sum_widths) if sum_widths is not None else nsums
    out_dtypes = [o[0] if isinstance(o, tuple) else o for o in outs]

    def body(*refs):
        a_refs, w_hbm = refs[:na], refs[na]
        ex = refs[na + 1:na + 1 + nex]
        co = refs[na + 1 + nex:na + 1 + nex + nco]
        o_refs = refs[na + 1 + nex + nco:na + 1 + nex + nco + nout]
        s_refs = refs[na + 1 + nex + nco + nout:na + 1 + nex + nco + nout + nsums]
        w_ref = refs[-1]
        i = pl.program_id(0)

        @pl.when(i == 0)
        def _():
            pltpu.sync_copy(w_hbm, w_ref)

        av = _bf(a_refs[0][...] if pro is None else pro(*[r[...] for r in a_refs]))
        if transposed:
            acc = None
            for j in range(J):
                part = lax.dot_general(av[:, j * n:(j + 1) * n], w_ref[j], NT_DIMS, preferred_element_type=f32)
                acc = part if acc is None else acc + part
        else:
            acc = jnp.dot(av, w_ref[0], preferred_element_type=f32)
        rows, sums = epi(acc, *[e[...] for e in ex], *[c[...] for c in co])
        if pro is not None:
            rows = (av,) + tuple(rows)
        for o_ref, v, dt in zip(o_refs, rows, out_dtypes):
            o_ref[...] = v.astype(dt)
        for s_ref, v in zip(s_refs, sums):
            @pl.when(i == 0)
            def _(s_ref=s_ref, v=v):
                s_ref[...] = v

            @pl.when(i > 0)
            def _(s_ref=s_ref, v=v):
                s_ref[...] += v

    extras = [e if isinstance(e, tuple) else (e, W, 0) for e in extras]
    outs_full = [o if isinstance(o, tuple) else (o, W, W) for o in outs]
    widths = list(sum_widths) if sum_widths is not None else [W] * nsums
    in_specs = [_rows(tm, w, cb) for _, w, cb in a_rows] + [_full(c.shape) for c in a_consts]
    in_specs += [pl.BlockSpec(memory_space=pl.ANY)]
    in_specs += [_rows(tm, w, cb) for _, w, cb in extras] + [_full(c.shape) for c in consts]
    out_specs = [_rows(tm, w) for _, w, _ in outs_full] + [_full((1, w)) for w in widths]
    out_shape = ([jax.ShapeDtypeStruct((M, cols), dt) for dt, _, cols in outs_full]
                 + [jax.ShapeDtypeStruct((1, w), f32) for w in widths])
    args = [e for e, _, _ in a_rows] + a_consts + [w3] + [e for e, _, _ in extras] + list(consts)
    res, landed = _pcall(body, args, name=name, grid=(M // tm,), in_specs=in_specs, out_specs=out_specs,
                         out_shape=out_shape, scratch_shapes=[pltpu.VMEM((J, K, n), bf16)], rides=rides)
    return (res, landed) if rides else res


def mm_bwd_w(a, dy, J, name, *, tm=2048, tn=None, tk=None, rides=()):
    M, K = a.shape
    n = dy.shape[1] // J
    tm = min(tm, M)
    tn = tn or n
    tk = tk or K
    nl = n // tn
    nk = K // tk
    ns = M // tm

    def body(a_ref, dy_ref, o_ref, acc_ref):
        s = pl.program_id(3)
        part = lax.dot_general(_bf(a_ref[...]), _bf(dy_ref[...]), TN_DIMS, preferred_element_type=f32)

        @pl.when(s == 0)
        def _():
            acc_ref[...] = part

        @pl.when(s > 0)
        def _():
            acc_ref[...] += part

        @pl.when(s == ns - 1)
        def _():
            o_ref[0] = acc_ref[...].astype(bf16)

    res, landed = _pcall(
        body, [a, dy], name=name, grid=(J, nl, nk, ns),
        in_specs=[pl.BlockSpec((tm, tk), lambda j, l, k, s: (s, k)),
                  pl.BlockSpec((tm, tn), lambda j, l, k, s: (s, j * nl + l))],
        out_specs=[pl.BlockSpec((1, tk, tn), lambda j, l, k, s: (j, k, l))],
        out_shape=[jax.ShapeDtypeStruct((J, K, n), bf16)],
        scratch_shapes=[pltpu.VMEM((tk, tn), f32)], rides=rides)
    return (res[0], landed) if rides else res[0]


def dw_scatter(a, dy, me, name, *, tm=2048, rides=()):
    M, K = a.shape
    n = dy.shape[1] // 4
    tm = min(tm, M)
    ns = M // tm

    def body(me_ref, a_ref, dy_ref, recv_ref, acc_ref, slots, send_sems, recv_sems, local_sem):
        k, s = pl.program_id(0), pl.program_id(1)
        x, y, c = lax.axis_index("x"), lax.axis_index("y"), lax.axis_index("c")
        me_xy = 2 * x + y

        def push(kk):
            j = me_xy ^ (kk + 1)
            return pltpu.make_async_remote_copy(src_ref=slots.at[kk % 2], dst_ref=recv_ref.at[me_xy],
                                                send_sem=send_sems.at[kk], recv_sem=recv_sems.at[kk],
                                                device_id=(j // 2, j % 2, c), device_id_type=MESH)

        def landing(kk):
            j = me_xy ^ (kk + 1)
            return pltpu.make_async_remote_copy(src_ref=slots.at[kk % 2], dst_ref=recv_ref.at[j],
                                                send_sem=send_sems.at[kk], recv_sem=recv_sems.at[kk],
                                                device_id=(j // 2, j % 2, c), device_id_type=MESH)

        part = lax.dot_general(_bf(a_ref[...]), _bf(dy_ref[...]), TN_DIMS, preferred_element_type=f32)

        @pl.when(s == 0)
        def _():
            acc_ref[...] = part

        @pl.when(s > 0)
        def _():
            acc_ref[...] += part

        for kk in range(4):
            @pl.when(jnp.logical_and(k == kk, s == ns - 1))
            def _(kk=kk):
                if kk >= 2:
                    push(kk - 2).wait_send()
                slots[kk % 2] = acc_ref[...].astype(bf16)
                if kk < 3:
                    push(kk).start()
                else:
                    own = pltpu.make_async_copy(slots.at[1], recv_ref.at[me_xy], local_sem)
                    own.start()
                    for q in range(3):
                        landing(q).wait_recv()
                    push(2).wait_send()
                    own.wait()

    (recv,), landed = _pcall(
        body, [a, dy], name=name, grid=(4, ns), prefetch=(me,),
        in_specs=[pl.BlockSpec((tm, K), lambda k, s, me_ref: (s, 0)),
                  pl.BlockSpec((tm, n), lambda k, s, me_ref: (s, me_ref[0] ^ ((k + 1) % 4)))],
        out_specs=[HBM_SPEC], out_shape=[jax.ShapeDtypeStruct((4, K, n), bf16)],
        scratch_shapes=[pltpu.VMEM((K, n), f32), pltpu.VMEM((2, K, n), bf16), pltpu.SemaphoreType.DMA((3,)),
                        pltpu.SemaphoreType.DMA((3,)), pltpu.SemaphoreType.DMA(())], rides=rides)
    return recv, landed


def _relu2(v):
    r = jnp.maximum(v, 0.0)
    return r * r


HALO = 16


def _shift_down(z, k, halo):
    r = lax.broadcasted_iota(jnp.int32, z.shape, 0)
    y = pltpu.roll(z, k, 0)
    for q in range(k):
        y = jnp.where(r == q, halo[HALO - k + q:HALO - k + q + 1, :], y)
    return y


def _shift_up(z, k, halo):
    tm = z.shape[0]
    r = lax.broadcasted_iota(jnp.int32, z.shape, 0)
    y = pltpu.roll(z, tm - k, 0)
    for q in range(k):
        y = jnp.where(r == tm - k + q, halo[q:q + 1, :], y)
    return y


def _prev_halo(tm, cb):
    return pl.BlockSpec((HALO, CONV_WIDTH), lambda i: (jnp.maximum(i * (tm // HALO) - 1, 0), cb))


def _next_halo(tm, cb, nblk):
    return pl.BlockSpec((HALO, CONV_WIDTH), lambda i: (jnp.minimum((i + 1) * (tm // HALO), nblk - 1), cb))


def _f32(ref):
    return ref[...].astype(f32)


def conv_fwd(proj, conv_w, wco3, tm=TM):
    S = proj.shape[0]
    J, _, n = wco3.shape

    def body(cb_ref, cc_ref, ch_ref, cch_ref, chh_ref, w_ref, wo_ref, a_ref, y_ref):
        i = pl.program_id(0)
        z = _f32(cc_ref) * _f32(ch_ref)
        zh = jnp.where(i == 0, 0.0, _f32(cch_ref) * _f32(chh_ref))
        w = w_ref[...]
        dwz = w[0:1, :] * _shift_down(z, 2, zh) + w[1:2, :] * _shift_down(z, 1, zh) + w[2:3, :] * z
        a = (_f32(cb_ref) * dwz).astype(bf16)
        a_ref[...] = a
        for j in range(J):
            y_ref[:, j * n:(j + 1) * n] = jnp.dot(a, wo_ref[j], preferred_element_type=f32).astype(bf16)

    return pl.pallas_call(
        body, name="conv_fwd", grid=(S // tm,),
        in_specs=[_rows(tm, CONV_WIDTH, 6), _rows(tm, CONV_WIDTH, 7), _rows(tm, CONV_WIDTH, 8),
                  _prev_halo(tm, 7), _prev_halo(tm, 8), _full((3, CONV_WIDTH)), _full(wco3.shape)],
        out_specs=[_rows(tm, CONV_WIDTH), _rows(tm, J * n)],
        out_shape=[jax.ShapeDtypeStruct((S, CONV_WIDTH), bf16), jax.ShapeDtypeStruct((S, J * n), bf16)],
        compiler_params=_params())(proj, proj, proj, proj, proj, conv_w, wco3)


def conv_bwd(dy_a, a_conv, wco3, proj, conv_w, dproj, tm=TM):
    S = proj.shape[0]
    nt = S // tm
    J, _, n = wco3.shape

    def body(dy_ref, dyn_ref, a_ref, wo_ref, cb_ref, cc_ref, ch_ref, cch_ref, chh_ref, cbn_ref, w_ref, _, o_ref,
             dw_ref, dwo_ref, acc_ref):
        i = pl.program_id(0)

        def times_wo_t(dy):
            out = None
            for j in range(J):
                part = lax.dot_general(dy[:, j * n:(j + 1) * n], wo_ref[j], NT_DIMS, preferred_element_type=f32)
                out = part if out is None else out + part
            return out

        dyv = dy_ref[...]
        dav, dan = times_wo_t(dyv), times_wo_t(dyn_ref[...])
        part = lax.dot_general(a_ref[...], dyv, TN_DIMS, preferred_element_type=f32)

        @pl.when(i == 0)
        def _():
            acc_ref[...] = part

        @pl.when(i > 0)
        def _():
            acc_ref[...] += part

        @pl.when(i == nt - 1)
        def _():
            for j in range(J):
                dwo_ref[j] = acc_ref[:, j * n:(j + 1) * n].astype(bf16)

        cc, ch, cb = _f32(cc_ref), _f32(ch_ref), _f32(cb_ref)
        z = cc * ch
        zh = jnp.where(i == 0, 0.0, _f32(cch_ref) * _f32(chh_ref))
        w = w_ref[...]
        z1 = _shift_down(z, 1, zh)
        z2 = _shift_down(z, 2, zh)
        dwz = w[0:1, :] * z2 + w[1:2, :] * z1 + w[2:3, :] * z
        g = dav * cb
        gn = jnp.where(i == nt - 1, 0.0, dan * _f32(cbn_ref))
        dz = w[2:3, :] * g + w[1:2, :] * _shift_up(g, 1, gn) + w[0:1, :] * _shift_up(g, 2, gn)
        o_ref[:, 0:CONV_WIDTH] = (dav * dwz).astype(bf16)
        o_ref[:, CONV_WIDTH:2 * CONV_WIDTH] = (dz * ch).astype(bf16)
        o_ref[:, 2 * CONV_WIDTH:3 * CONV_WIDTH] = (dz * cc).astype(bf16)

        @pl.when(i == 0)
        def _():
            dw_ref[...] = jnp.zeros_like(dw_ref)

        dw_ref[0:1, :] += jnp.sum(g * z2, axis=0, keepdims=True)
        dw_ref[1:2, :] += jnp.sum(g * z1, axis=0, keepdims=True)
        dw_ref[2:3, :] += jnp.sum(g * z, axis=0, keepdims=True)

    return pl.pallas_call(
        body, name="conv_bwd", grid=(nt,),
        in_specs=[_rows(tm, J * n),
                  pl.BlockSpec((HALO, J * n), lambda i: (jnp.minimum((i + 1) * (tm // HALO), S // HALO - 1), 0)),
                  _rows(tm, CONV_WIDTH), _full(wco3.shape),
                  _rows(tm, CONV_WIDTH, 6), _rows(tm, CONV_WIDTH, 7), _rows(tm, CONV_WIDTH, 8),
                  _prev_halo(tm, 7), _prev_halo(tm, 8), _next_halo(tm, 6, S // HALO), _full((3, CONV_WIDTH)),
                  pl.BlockSpec(memory_space=pl.ANY)],
        out_specs=[_rows(tm, 3 * CONV_WIDTH, GATE_COLS // (3 * CONV_WIDTH)), _full((3, CONV_WIDTH)),
                   _full(wco3.shape)],
        out_shape=[jax.ShapeDtypeStruct(dproj.shape, bf16), jax.ShapeDtypeStruct((3, CONV_WIDTH), f32),
                   jax.ShapeDtypeStruct(wco3.shape, bf16)],
        scratch_shapes=[pltpu.VMEM((CONV_WIDTH, J * n), f32)], input_output_aliases={11: 0},
        compiler_params=_params())(dy_a, dy_a, a_conv, wco3, proj, proj, proj, proj, proj, proj, conv_w, dproj)


def _cmul(ar, ai, br, bi):
    return ar * br - ai * bi, ar * bi + ai * br


def ssm_prep(lr, li, ldt, bt_re, bt_im):
    def body(lr_ref, li_ref, ldt_ref, br_ref, bi_ref, ar_ref, ai_ref, kr_ref, ki_ref, bbr_ref, bbi_ref):
        lrv, liv = lr_ref[...], li_ref[...]
        dt = jnp.exp(ldt_ref[...])
        mag = jnp.exp(lrv * dt)
        ar = mag * jnp.cos(liv * dt)
        ai = mag * jnp.sin(liv * dt)
        den = lrv * lrv + liv * liv
        nr = ar - 1.0
        kr = (nr * lrv + ai * liv) / den
        ki = (ai * lrv - nr * liv) / den
        ar_ref[...], ai_ref[...], kr_ref[...], ki_ref[...] = ar, ai, kr, ki
        bbr_ref[...] = kr * br_ref[...] - ki * bi_ref[...]
        bbi_ref[...] = kr * bi_ref[...] + ki * br_ref[...]

    v = jax.ShapeDtypeStruct((1, N_STATES), f32)
    m = jax.ShapeDtypeStruct((SSM_GROUP, N_STATES), f32)
    return pl.pallas_call(body, name="ssm_prep", out_shape=[v, v, v, v, m, m])(lr, li, ldt, bt_re, bt_im)


def _pow_segment(ar, ai, seg_len):
    pr, pi = ar, ai
    for _ in range(int(math.log2(seg_len))):
        pr, pi = _cmul(pr, pi, pr, pi)
    return pr, pi


GELU_K = math.sqrt(2.0 / math.pi)
GELU_C = 0.044715


def _gelu(v):
    return 0.5 * v * (1.0 + jnp.tanh(GELU_K * (v + GELU_C * v * v * v)))


def _gelu_grad(v):
    t = jnp.tanh(GELU_K * (v + GELU_C * v * v * v))
    return 0.5 * (1.0 + t) + 0.5 * v * (1.0 - t * t) * GELU_K * (1.0 + 3.0 * GELU_C * v * v)


def ssm_scan_fwd(u_perm, bd_r, bd_i, ar, ai, *, final=None, rides=()):
    S = u_perm.shape[0]
    R = SCAN_ROWS
    nblk = S // R
    seg_len = S // N_SEG
    nstrip = N_STATES // LANE_STRIP
    store = final is not None

    def body(*refs):
        if store:
            (u_ref, bdr_ref, bdi_ref, ar_ref, ai_ref, cdr_ref, cdi_ref, d_ref, ir_ref, ii_ref,
             y_ref, ys_ref, sr_ref, si_ref, bur, bui, car, cai) = refs
        else:
            u_ref, bdr_ref, bdi_ref, ar_ref, ai_ref, or_ref, oi_ref, bur, bui, car, cai = refs
        i = pl.program_id(0)
        ub = _bf(u_ref[...])
        u = ub.astype(f32)
        for cb in range(4):
            us = ub[:, cb * 128:(cb + 1) * 128]
            bur[:, cb * 512:(cb + 1) * 512] = jnp.dot(us, bdr_ref[cb], preferred_element_type=f32)
            bui[:, cb * 512:(cb + 1) * 512] = jnp.dot(us, bdi_ref[cb], preferred_element_type=f32)

        @pl.when(i == 0)
        def _():
            if store:
                car[...] = ir_ref[...]
                cai[...] = ii_ref[...]
            else:
                car[...] = jnp.zeros_like(car)
                cai[...] = jnp.zeros_like(cai)

        for ls in range(nstrip):
            lanes = pl.ds(ls * LANE_STRIP, LANE_STRIP)
            a_r = jnp.broadcast_to(ar_ref[:, lanes], (N_SEG, LANE_STRIP))
            a_i = jnp.broadcast_to(ai_ref[:, lanes], (N_SEG, LANE_STRIP))

            def step(t, carry, lanes=lanes, a_r=a_r, a_i=a_i):
                s_r, s_i = carry
                row = pl.multiple_of(t * 8, 8)
                n_r = a_r * s_r - a_i * s_i + bur[pl.ds(row, 8), lanes]
                n_i = a_r * s_i + a_i * s_r + bui[pl.ds(row, 8), lanes]
                if store:
                    sr_ref[pl.ds(row, 8), lanes] = n_r
                    si_ref[pl.ds(row, 8), lanes] = n_i
                return n_r, n_i

            e_r, e_i = lax.fori_loop(0, R // 8, step, (car[:, lanes], cai[:, lanes]), unroll=True)
            car[:, lanes] = e_r
            cai[:, lanes] = e_i

        if store:
            for cb in range(4):
                st_r = sr_ref[:, cb * 512:(cb + 1) * 512].astype(bf16)
                st_i = si_ref[:, cb * 512:(cb + 1) * 512].astype(bf16)
                y = (jnp.dot(st_r, cdr_ref[cb], preferred_element_type=f32)
                     - jnp.dot(st_i, cdi_ref[cb], preferred_element_type=f32))
                cols = slice(cb * 128, (cb + 1) * 128)
                y = y + d_ref[:, cols] * u[:, cols]
                y_ref[:, cols] = y
                ys_ref[:, cols] = _gelu(y).astype(bf16)
        else:
            @pl.when(i == nblk - 1)
            def _():
                p_r, p_i = _pow_segment(ar_ref[...], ai_ref[...], seg_len)
                t_r, t_i = car[0:1, :], cai[0:1, :]
                or_ref[0:1, :] = jnp.zeros((1, N_STATES), f32)
                oi_ref[0:1, :] = jnp.zeros((1, N_STATES), f32)
                for j in range(1, N_SEG):
                    or_ref[j:j + 1, :] = t_r
                    oi_ref[j:j + 1, :] = t_i
                    m_r, m_i = _cmul(p_r, p_i, t_r, t_i)
                    t_r, t_i = car[j:j + 1, :] + m_r, cai[j:j + 1, :] + m_i

    blk = lambda w: pl.BlockSpec((R, w), lambda i: (i, 0))
    in_specs = [blk(SSM_WIDTH), _full((4, 128, 512)), _full((4, 128, 512)), _full((1, N_STATES)), _full((1, N_STATES))]
    args = [u_perm, bd_r, bd_i, ar, ai]
    scratch = [pltpu.VMEM((R, N_STATES), f32), pltpu.VMEM((R, N_STATES), f32),
               pltpu.VMEM((N_SEG, N_STATES), f32), pltpu.VMEM((N_SEG, N_STATES), f32)]
    if store:
        in_specs += [_full((4, 512, 128)), _full((4, 512, 128)), _full((1, SSM_WIDTH)),
                     _full((N_SEG, N_STATES)), _full((N_SEG, N_STATES))]
        args += list(final)
        out_specs = [blk(SSM_WIDTH), blk(SSM_WIDTH), blk(N_STATES), blk(N_STATES)]
        out_shape = [jax.ShapeDtypeStruct((S, SSM_WIDTH), f32), jax.ShapeDtypeStruct((S, SSM_WIDTH), bf16),
                     jax.ShapeDtypeStruct((S, N_STATES), f32), jax.ShapeDtypeStruct((S, N_STATES), f32)]
        name = "ssm_scan_fwd"
    else:
        out_specs = [_full((N_SEG, N_STATES)), _full((N_SEG, N_STATES))]
        out_shape = [jax.ShapeDtypeStruct((N_SEG, N_STATES), f32)] * 2
        name = "ssm_scan_fwd_carry"
    res, landed = _pcall(body, args, name=name, grid=(nblk,), in_specs=in_specs, out_specs=out_specs,
                         out_shape=out_shape, scratch_shapes=scratch, rides=rides)
    return (res, landed) if rides else res


def ssm_scan_bwd(dys_perm, y_perm, cdt_r, cdt_i, ar, ai, *, final=None, rides=()):
    S = dys_perm.shape[0]
    R = SCAN_ROWS
    nblk = S // R
    seg_len = S // N_SEG
    store = final is not None
    strip = LANE_STRIP // 2 if store else LANE_STRIP
    nstrip = N_STATES // strip

    def body(*refs):
        if store:
            (dys_ref, y_ref, cdr_ref, cdi_ref, ar_ref, ai_ref, u_ref, sr_ref, si_ref, bdr_ref, bdi_ref, d_ref, ir_ref,
             ii_ref, du_ref, dbr_ref, dbi_ref, dcr_ref, dci_ref, dar_ref, dai_ref, dd_ref, dsr, dsi, lmr, lmi, car,
             cai) = refs
        else:
            dys_ref, y_ref, cdr_ref, cdi_ref, ar_ref, ai_ref, or_ref, oi_ref, dsr, dsi, car, cai = refs
        i = pl.program_id(0)
        dy = dys_ref[...] * _gelu_grad(y_ref[...])
        dyb = dy.astype(bf16)
        for cb in range(4):
            ds_ = dyb[:, cb * 128:(cb + 1) * 128]
            dsr[:, cb * 512:(cb + 1) * 512] = jnp.dot(ds_, cdr_ref[cb], preferred_element_type=f32)
            dsi[:, cb * 512:(cb + 1) * 512] = -jnp.dot(ds_, cdi_ref[cb], preferred_element_type=f32)

        @pl.when(i == 0)
        def _():
            if store:
                car[...] = ir_ref[...]
                cai[...] = ii_ref[...]
                dar_ref[...] = jnp.zeros_like(dar_ref)
                dai_ref[...] = jnp.zeros_like(dai_ref)
                dbr_ref[...] = jnp.zeros_like(dbr_ref)
                dbi_ref[...] = jnp.zeros_like(dbi_ref)
                dcr_ref[...] = jnp.zeros_like(dcr_ref)
                dci_ref[...] = jnp.zeros_like(dci_ref)
                dd_ref[...] = jnp.zeros_like(dd_ref)
            else:
                car[...] = jnp.zeros_like(car)
                cai[...] = jnp.zeros_like(cai)

        for ls in range(nstrip):
            lanes = pl.ds(ls * strip, strip)
            a_r = jnp.broadcast_to(ar_ref[:, lanes], (N_SEG, strip))
            a_i = jnp.broadcast_to(ai_ref[:, lanes], (N_SEG, strip))
            zero = jnp.zeros((N_SEG, strip), f32)

            def step(k, carry, lanes=lanes, a_r=a_r, a_i=a_i):
                l_r, l_i, g_r, g_i = carry
                row = pl.multiple_of((R // 8 - 1 - k) * 8, 8)
                if store:
                    s_r = sr_ref[pl.ds(row, 8), lanes]
                    s_i = si_ref[pl.ds(row, 8), lanes]
                    g_r = g_r + l_r * s_r + l_i * s_i
                    g_i = g_i + l_i * s_r - l_r * s_i
                n_r = dsr[pl.ds(row, 8), lanes] + a_r * l_r + a_i * l_i
                n_i = dsi[pl.ds(row, 8), lanes] + a_r * l_i - a_i * l_r
                if store:
                    lmr[pl.ds(row, 8), lanes] = n_r
                    lmi[pl.ds(row, 8), lanes] = n_i
                return n_r, n_i, g_r, g_i

            e_r, e_i, g_r, g_i = lax.fori_loop(0, R // 8, step, (car[:, lanes], cai[:, lanes], zero, zero),
                                               unroll=True)
            car[:, lanes] = e_r
            cai[:, lanes] = e_i
            if store:
                dar_ref[:, lanes] += g_r
                dai_ref[:, lanes] += g_i

        if store:
            ub = _bf(u_ref[...])
            u = ub.astype(f32)
            for cb in range(4):
                cols = slice(cb * 128, (cb + 1) * 128)
                st = slice(cb * 512, (cb + 1) * 512)
                l_r = lmr[:, st].astype(bf16)
                l_i = lmi[:, st].astype(bf16)
                du = (jnp.dot(l_r, bdr_ref[cb], preferred_element_type=f32)
                      + jnp.dot(l_i, bdi_ref[cb], preferred_element_type=f32))
                du_ref[:, cols] = (du + d_ref[:, cols] * dy[:, cols]).astype(bf16)
                dbr_ref[cb] += lax.dot_general(l_r, ub[:, cols], TN_DIMS, preferred_element_type=f32)
                dbi_ref[cb] += lax.dot_general(l_i, ub[:, cols], TN_DIMS, preferred_element_type=f32)
                dcr_ref[cb] += lax.dot_general(dyb[:, cols], sr_ref[:, st].astype(bf16), TN_DIMS,
                                               preferred_element_type=f32)
                dci_ref[cb] += lax.dot_general(dyb[:, cols], si_ref[:, st].astype(bf16), TN_DIMS,
                                               preferred_element_type=f32)
            dd_ref[...] += jnp.sum(dy * u, axis=0, keepdims=True)
        else:
            @pl.when(i == nblk - 1)
            def _():
                p_r, p_i = _pow_segment(ar_ref[...], ai_ref[...], seg_len)
                p_i = -p_i
                t_r, t_i = car[N_SEG - 1:N_SEG, :], cai[N_SEG - 1:N_SEG, :]
                or_ref[N_SEG - 1:N_SEG, :] = jnp.zeros((1, N_STATES), f32)
                oi_ref[N_SEG - 1:N_SEG, :] = jnp.zeros((1, N_STATES), f32)
                for j in range(N_SEG - 2, -1, -1):
                    or_ref[j:j + 1, :] = t_r
                    oi_ref[j:j + 1, :] = t_i
                    m_r, m_i = _cmul(p_r, p_i, t_r, t_i)
                    t_r, t_i = car[j:j + 1, :] + m_r, cai[j:j + 1, :] + m_i

    blk = lambda w: pl.BlockSpec((R, w), lambda i: (nblk - 1 - i, 0))
    in_specs = [blk(SSM_WIDTH), blk(SSM_WIDTH), _full((4, 128, 512)), _full((4, 128, 512)), _full((1, N_STATES)),
                _full((1, N_STATES))]
    args = [dys_perm, y_perm, cdt_r, cdt_i, ar, ai]
    seg = jax.ShapeDtypeStruct((N_SEG, N_STATES), f32)
    if store:
        in_specs += [blk(SSM_WIDTH), blk(N_STATES), blk(N_STATES), _full((4, 512, 128)), _full((4, 512, 128)),
                     _full((1, SSM_WIDTH)), _full((N_SEG, N_STATES)), _full((N_SEG, N_STATES))]
        args += list(final)
        out_specs = [blk(SSM_WIDTH), _full((4, 512, 128)), _full((4, 512, 128)), _full((4, 128, 512)),
                     _full((4, 128, 512)), _full((N_SEG, N_STATES)), _full((N_SEG, N_STATES)), _full((1, SSM_WIDTH))]
        b_acc = jax.ShapeDtypeStruct((4, 512, 128), f32)
        c_acc = jax.ShapeDtypeStruct((4, 128, 512), f32)
        out_shape = [jax.ShapeDtypeStruct((S, SSM_WIDTH), bf16), b_acc, b_acc, c_acc, c_acc, seg, seg,
                     jax.ShapeDtypeStruct((1, SSM_WIDTH), f32)]
        scratch = [pltpu.VMEM((R, N_STATES), f32)] * 4 + [pltpu.VMEM((N_SEG, N_STATES), f32)] * 2
        name = "ssm_scan_bwd"
    else:
        out_specs = [_full((N_SEG, N_STATES)), _full((N_SEG, N_STATES))]
        out_shape = [seg, seg]
        scratch = [pltpu.VMEM((R, N_STATES), f32)] * 2 + [pltpu.VMEM((N_SEG, N_STATES), f32)] * 2
        name = "ssm_scan_bwd_carry"
    res, landed = _pcall(body, args, name=name, grid=(nblk,), in_specs=in_specs, out_specs=out_specs,
                         out_shape=out_shape, scratch_shapes=scratch, rides=rides)
    return (res, landed) if rides else res


def ssm_param_bwd(dbb_r, dbb_i, bt_re, bt_im, kr, ki, ar, ai, lr, li, ldt, da_r, da_i):
    def body(dbr_ref, dbi_ref, br_ref, bi_ref, kr_ref, ki_ref, ar_ref, ai_ref, lr_ref, li_ref, ldt_ref, dar_ref,
             dai_ref, obr_ref, obi_ref, olr_ref, oli_ref, odt_ref):
        dbr, dbi, b_r, b_i = dbr_ref[...], dbi_ref[...], br_ref[...], bi_ref[...]
        k_r, k_i, a_r, a_i = kr_ref[...], ki_ref[...], ar_ref[...], ai_ref[...]
        l_r, l_i = lr_ref[...], li_ref[...]
        dt = jnp.exp(ldt_ref[...])
        obr_ref[...] = k_r * dbr + k_i * dbi
        obi_ref[...] = k_r * dbi - k_i * dbr
        gk_r = jnp.sum(dbr * b_r + dbi * b_i, axis=0, keepdims=True)
        gk_i = jnp.sum(dbi * b_r - dbr * b_i, axis=0, keepdims=True)
        ga_r = jnp.sum(dar_ref[...], axis=0, keepdims=True)
        ga_i = jnp.sum(dai_ref[...], axis=0, keepdims=True)
        den = l_r * l_r + l_i * l_i
        c_r, c_i = l_r / den, l_i / den
        m_r, m_i = _cmul(c_r, c_i, gk_r, gk_i)
        g_r, g_i = ga_r + m_r, ga_i + m_i
        t1_r, t1_i = _cmul(dt * a_r, -dt * a_i, g_r, g_i)
        q_r, q_i = _cmul(k_r, k_i, c_r, -c_i)
        t2_r, t2_i = _cmul(-q_r, q_i, gk_r, gk_i)
        olr_ref[...] = t1_r + t2_r
        oli_ref[...] = t1_i + t2_i
        w_r, w_i = _cmul(l_r, l_i, a_r, a_i)
        odt_ref[...] = dt * (w_r * g_r + w_i * g_i)

    v = jax.ShapeDtypeStruct((1, N_STATES), f32)
    m = jax.ShapeDtypeStruct((SSM_GROUP, N_STATES), f32)
    return pl.pallas_call(body, name="ssm_param_bwd", out_shape=[m, m, v, v, v])(
        dbb_r, dbb_i, bt_re, bt_im, kr, ki, ar, ai, lr, li, ldt, da_r, da_i)


def group_sum(v):
    def body(v_ref, o_ref):
        o_ref[...] = jnp.sum(v_ref[...], axis=-1, keepdims=True)
    return pl.pallas_call(body, name="group_sum", out_shape=jax.ShapeDtypeStruct((v.shape[0], 1), f32))(v)


ATTN_SCALE = XATTN_HEAD_DIM ** -0.5


def _attn_probs(q_h, k_h):
    s = lax.dot_general(q_h, k_h, NT_DIMS, preferred_element_type=f32) * ATTN_SCALE
    e = jnp.exp(s - jnp.max(s, axis=-1, keepdims=True))
    return e / jnp.sum(e, axis=-1, keepdims=True)


def attn_fwd(proj, kv, wxo3, tm=TM):
    S = proj.shape[0]
    M = kv.shape[0]
    J, _, n = wxo3.shape

    def body(q_ref, kv_ref, w_ref, o_ref, y_ref):
        for h in range(XATTN_HEADS):
            cols = slice(h * XATTN_HEAD_DIM, (h + 1) * XATTN_HEAD_DIM)
            q_h = q_ref[:, cols].astype(bf16)
            k_h = kv_ref[:, cols]
            v_h = kv_ref[:, XATTN_WIDTH + h * XATTN_HEAD_DIM:XATTN_WIDTH + (h + 1) * XATTN_HEAD_DIM]
            p = _attn_probs(q_h, k_h)
            o_ref[:, cols] = jnp.dot(p.astype(bf16), v_h, preferred_element_type=f32).astype(bf16)
        o = o_ref[...]
        for j in range(J):
            y_ref[:, j * n:(j + 1) * n] = jnp.dot(o, w_ref[j], preferred_element_type=f32).astype(bf16)

    return pl.pallas_call(
        body, name="attn_fwd", grid=(S // tm,),
        in_specs=[_rows(tm, XATTN_WIDTH, 10), _full((M, 2 * XATTN_WIDTH)), _full(wxo3.shape)],
        out_specs=[_rows(tm, XATTN_WIDTH), _rows(tm, J * n)],
        out_shape=[jax.ShapeDtypeStruct((S, XATTN_WIDTH), bf16), jax.ShapeDtypeStruct((S, J * n), bf16)],
        compiler_params=_params())(proj, kv, wxo3)


def attn_bwd(dy_c, o_att, wxo3, proj, kv, dproj, tm=TM):
    S = proj.shape[0]
    M = kv.shape[0]
    J, _, n = wxo3.shape
    nt = S // tm

    def body(dy_ref, o_ref, w_ref, q_ref, kv_ref, _, dq_ref, dkv_ref, dwo_ref, acc_ref):
        i = pl.program_id(0)

        @pl.when(i == 0)
        def _():
            dkv_ref[...] = jnp.zeros_like(dkv_ref)

        part = lax.dot_general(o_ref[...], dy_ref[...], TN_DIMS, preferred_element_type=f32)

        @pl.when(i == 0)
        def _():
            acc_ref[...] = part

        @pl.when(i > 0)
        def _():
            acc_ref[...] += part

        @pl.when(i == nt - 1)
        def _():
            for j in range(J):
                dwo_ref[j] = acc_ref[:, j * n:(j + 1) * n].astype(bf16)

        do = None
        for j in range(J):
            part = lax.dot_general(dy_ref[:, j * n:(j + 1) * n], w_ref[j], NT_DIMS, preferred_element_type=f32)
            do = part if do is None else do + part
        do = do.astype(bf16)
        for h in range(XATTN_HEADS):
            cols = slice(h * XATTN_HEAD_DIM, (h + 1) * XATTN_HEAD_DIM)
            vcols = slice(XATTN_WIDTH + h * XATTN_HEAD_DIM, XATTN_WIDTH + (h + 1) * XATTN_HEAD_DIM)
            q_h = q_ref[:, cols].astype(bf16)
            k_h = kv_ref[:, cols]
            v_h = kv_ref[:, vcols]
            do_h = do[:, cols]
            p = _attn_probs(q_h, k_h)
            dp = lax.dot_general(do_h, v_h, NT_DIMS, preferred_element_type=f32)
            ds = (p * (dp - jnp.sum(dp * p, axis=-1, keepdims=True)) * ATTN_SCALE).astype(bf16)
            dq_ref[:, cols] = jnp.dot(ds, k_h, preferred_element_type=f32).astype(bf16)
            dkv_ref[:, cols] += lax.dot_general(ds, q_h, TN_DIMS, preferred_element_type=f32)
            dkv_ref[:, vcols] += lax.dot_general(p.astype(bf16), do_h, TN_DIMS, preferred_element_type=f32)

    return pl.pallas_call(
        body, name="attn_bwd", grid=(S // tm,),
        in_specs=[_rows(tm, J * n), _rows(tm, XATTN_WIDTH), _full(wxo3.shape), _rows(tm, XATTN_WIDTH, 10),
                  _full((M, 2 * XATTN_WIDTH)), pl.BlockSpec(memory_space=pl.ANY)],
        out_specs=[_rows(tm, XATTN_WIDTH, 10), _full((M, 2 * XATTN_WIDTH)), _full(wxo3.shape)],
        out_shape=[jax.ShapeDtypeStruct(dproj.shape, bf16), jax.ShapeDtypeStruct((M, 2 * XATTN_WIDTH), f32),
                   jax.ShapeDtypeStruct(wxo3.shape, bf16)],
        scratch_shapes=[pltpu.VMEM((XATTN_WIDTH, J * n), f32)], input_output_aliases={5: 0},
        compiler_params=_params())(dy_c, o_att, wxo3, proj, kv, dproj)


def _ln_stats(r):
    mu = jnp.mean(r, axis=-1, keepdims=True)
    xc = r - mu
    var = jnp.mean(xc * xc, axis=-1, keepdims=True)
    rstd = lax.rsqrt(var + LN_EPS)
    return xc * rstd, rstd


def _ln_bwd(dy, xhat, rstd, g):
    dxh = dy * g
    return rstd * (dxh - jnp.mean(dxh, axis=-1, keepdims=True) - xhat * jnp.mean(dxh * xhat, axis=-1, keepdims=True))


def _colsum(v):
    return jnp.sum(v, axis=0, keepdims=True)


def _ln2_loss_epi(t2, h1, target, b_down, g, b):
    xhat, rstd = _ln_stats(ALPHA * h1 + t2 + b_down)
    err = xhat * g + b - target
    dout = err * (1.0 / D_MODEL)
    dr = _ln_bwd(dout, xhat, rstd, g)
    return (dr, dr), (0.5 * _colsum(err * err) * (1.0 / D_MODEL), _colsum(dout * xhat), _colsum(dout), _colsum(dr))


def _merge_fwd_pro(g0, g1, g2, y_a, glu_a, glu_b, y_c, b):
    gs = [_sig(g.astype(f32) + b[:, k * D_MODEL:(k + 1) * D_MODEL]) for k, g in enumerate((g0, g1, g2))]
    y_b = glu_a.astype(f32) * _sig(glu_b.astype(f32))
    return gs[0] * y_a.astype(f32) + gs[1] * y_b + gs[2] * y_c.astype(f32)


def _merge_bwd_epi(dm, g0, g1, g2, y_a, glu_a, glu_b, y_c, b):
    ga, sb = glu_a.astype(f32), _sig(glu_b.astype(f32))
    ys = (y_a.astype(f32), ga * sb, y_c.astype(f32))
    gs = [_sig(g.astype(f32) + b[:, k * D_MODEL:(k + 1) * D_MODEL]) for k, g in enumerate((g0, g1, g2))]
    dpre = [dm * ys[k] * gs[k] * (1.0 - gs[k]) for k in range(3)]
    dyb = dm * gs[1]
    dglu = jnp.concatenate([dyb * sb, dyb * ga * sb * (1.0 - sb)], axis=1)
    return ((jnp.concatenate(dpre, axis=1), dm * gs[0], dglu, dm * gs[2]),
            (jnp.concatenate([_colsum(d) for d in dpre], axis=1),))


def _ln1_fwd_epi(t1, x, g, b):
    xhat, _ = _ln_stats(ALPHA * x + t1)
    h = xhat * g + b
    return (t1, h, h), ()


def _ln1_bwd_epi(t3, x, t1, dr2, g):
    xhat, rstd = _ln_stats(ALPHA * x + t1)
    dh = ALPHA * dr2 + t3
    dr = _ln_bwd(dh, xhat, rstd, g)
    return (dr, dr), (_colsum(dh * xhat), _colsum(dh))


def exchange(*rides, name):
    return _pcall(lambda: None, [], name=name, grid=(), in_specs=[], out_specs=[], out_shape=[], rides=rides)[1]


ROW_TILE = 256


def sum_slots(recv, name):
    n, R, C = recv.shape
    tm = min(R, ROW_TILE)

    def body(r_ref, o_ref):
        acc = r_ref[0].astype(f32)
        for k in range(1, n):
            acc = acc + r_ref[k].astype(f32)
        o_ref[...] = acc

    return pl.pallas_call(
        body, name=name, grid=(R // tm,), in_specs=[pl.BlockSpec((n, tm, C), lambda i: (0, i, 0))],
        out_specs=_rows(tm, C), out_shape=jax.ShapeDtypeStruct((R, C), f32), compiler_params=_params())(recv)


def _adamw(w, g, m, v):
    m = ADAM_B1 * m + (1.0 - ADAM_B1) * g
    v = ADAM_B2 * v + (1.0 - ADAM_B2) * (g * g)
    m_hat = m / (1.0 - ADAM_B1 ** ADAM_STEP)
    v_hat = v / (1.0 - ADAM_B2 ** ADAM_STEP)
    delta = -ADAM_LR * (m_hat / (jnp.sqrt(v_hat) + ADAM_EPS) + ADAM_WD * w)
    return delta, m, v


def adam_pair(p, q, w, m, v, name):
    R, C = w.shape
    tm = min(R, ROW_TILE)

    def body(p_ref, q_ref, w_ref, m_ref, v_ref, g_ref, d_ref, nm_ref, nv_ref):
        g = p_ref[...] + q_ref[...]
        g_ref[...] = g
        d_ref[...], nm_ref[...], nv_ref[...] = _adamw(w_ref[...], g, m_ref[...], v_ref[...])

    o = jax.ShapeDtypeStruct((R, C), f32)
    return pl.pallas_call(body, name=name, grid=(R // tm,), in_specs=[_rows(tm, C)] * 5,
                          out_specs=[_rows(tm, C)] * 4, out_shape=[o] * 4, compiler_params=_params())(p, q, w, m, v)


def adam_slots(recv, w, m, v, name):
    n = recv.shape[0]

    def body(r_ref, w_ref, m_ref, v_ref, g_ref, d_ref, nm_ref, nv_ref):
        g = r_ref[0]
        for k in range(1, n):
            g = g + r_ref[k]
        g_ref[...] = g
        d_ref[...], nm_ref[...], nv_ref[...] = _adamw(w_ref[...], g, m_ref[...], v_ref[...])

    o = jax.ShapeDtypeStruct(w.shape, f32)
    return pl.pallas_call(body, name=name, out_shape=[o] * 4, compiler_params=_params())(recv, w, m, v)


BIG = ("w_in", "w_conv_out", "w_glu", "w_xattn_out", "w_kv", "w_out", "w_up", "w_down")
MID = ("w_conv_out", "w_glu", "w_xattn_out", "w_kv", "w_out")
SMALL = ("b_gate", "ssm_lam_re", "ssm_lam_im", "ssm_log_dt", "ssm_b_re", "ssm_b_im", "ssm_c_re", "ssm_c_im", "ssm_d",
         "ln1_g", "ln1_b", "b_up", "b_down", "ln2_g", "ln2_b")
def _pad_flat(a, mult=1024):
    a = a.reshape(-1)
    return jnp.pad(a, (0, (-a.shape[0]) % mult))


def _perm(a):
    S, W = a.shape
    return a.reshape(N_SEG, S // N_SEG, W).transpose(1, 0, 2).reshape(S, W)


def _unperm(a):
    S, W = a.shape
    return a.reshape(S // N_SEG, N_SEG, W).transpose(1, 0, 2).reshape(S, W)


def _state_rows(a):
    return a.transpose(2, 0, 1).reshape(SSM_GROUP, N_STATES)


def _block_diag_b(bt):
    b4 = bt.reshape(SSM_GROUP, 4, 8, SSM_STATE)
    eye = jnp.eye(8, dtype=bt.dtype)
    return jnp.einsum("hcgp,gk->cghkp", b4, eye).reshape(4, 128, 512)


def _block_diag_c(c):
    c4 = c.reshape(4, 8, SSM_GROUP, SSM_STATE)
    eye = jnp.eye(8, dtype=c.dtype)
    return jnp.einsum("cghp,gk->cgpkh", c4, eye).reshape(4, 512, 128)


def _diag_b(acc):
    a = acc.reshape(4, 8, SSM_STATE, 8, SSM_GROUP)
    eye = jnp.eye(8, dtype=acc.dtype)
    return jnp.einsum("cgpkh,gk->hcgp", a, eye).reshape(SSM_GROUP, N_STATES)


def _diag_c(acc):
    a = acc.reshape(4, 8, SSM_GROUP, 8, SSM_STATE)
    eye = jnp.eye(8, dtype=acc.dtype)
    return jnp.einsum("cghkp,gk->cghp", a, eye).reshape(SSM_GROUPS, SSM_GROUP, SSM_STATE)


def kernel(x, mem, w_in, b_gate, conv_w, w_conv_out, ssm_lam_re, ssm_lam_im, ssm_log_dt, ssm_b_re, ssm_b_im, ssm_c_re, ssm_c_im, ssm_d, w_glu, w_kv, w_xattn_out, w_out, ln1_g, ln1_b, w_up, b_up, w_down, b_down, ln2_g, ln2_b, loss_target, m_w_in, m_b_gate, m_conv_w, m_w_conv_out, m_ssm_lam_re, m_ssm_lam_im, m_ssm_log_dt, m_ssm_b_re, m_ssm_b_im, m_ssm_c_re, m_ssm_c_im, m_ssm_d, m_w_glu, m_w_kv, m_w_xattn_out, m_w_out, m_ln1_g, m_ln1_b, m_w_up, m_b_up, m_w_down, m_b_down, m_ln2_g, m_ln2_b, v_w_in, v_b_gate, v_conv_w, v_w_conv_out, v_ssm_lam_re, v_ssm_lam_im, v_ssm_log_dt, v_ssm_b_re, v_ssm_b_im, v_ssm_c_re, v_ssm_c_im, v_ssm_d, v_w_glu, v_w_kv, v_w_xattn_out, v_w_out, v_ln1_g, v_ln1_b, v_w_up, v_b_up, v_w_down, v_b_down, v_ln2_g, v_ln2_b):
    W = dict(w_in=w_in, b_gate=b_gate, conv_w=conv_w, w_conv_out=w_conv_out, ssm_lam_re=ssm_lam_re,
             ssm_lam_im=ssm_lam_im, ssm_log_dt=ssm_log_dt, ssm_b_re=ssm_b_re, ssm_b_im=ssm_b_im, ssm_c_re=ssm_c_re,
             ssm_c_im=ssm_c_im, ssm_d=ssm_d, w_glu=w_glu, w_kv=w_kv, w_xattn_out=w_xattn_out, w_out=w_out,
             ln1_g=ln1_g, ln1_b=ln1_b, w_up=w_up, b_up=b_up, w_down=w_down, b_down=b_down, ln2_g=ln2_g, ln2_b=ln2_b)
    MOM = dict(w_in=m_w_in, b_gate=m_b_gate, conv_w=m_conv_w, w_conv_out=m_w_conv_out, ssm_lam_re=m_ssm_lam_re,
               ssm_lam_im=m_ssm_lam_im, ssm_log_dt=m_ssm_log_dt, ssm_b_re=m_ssm_b_re, ssm_b_im=m_ssm_b_im,
               ssm_c_re=m_ssm_c_re, ssm_c_im=m_ssm_c_im, ssm_d=m_ssm_d, w_glu=m_w_glu, w_kv=m_w_kv,
               w_xattn_out=m_w_xattn_out, w_out=m_w_out, ln1_g=m_ln1_g, ln1_b=m_ln1_b, w_up=m_w_up, b_up=m_b_up,
               w_down=m_w_down, b_down=m_b_down, ln2_g=m_ln2_g, ln2_b=m_ln2_b)
    VEL = dict(w_in=v_w_in, b_gate=v_b_gate, conv_w=v_conv_w, w_conv_out=v_w_conv_out, ssm_lam_re=v_ssm_lam_re,
               ssm_lam_im=v_ssm_lam_im, ssm_log_dt=v_ssm_log_dt, ssm_b_re=v_ssm_b_re, ssm_b_im=v_ssm_b_im,
               ssm_c_re=v_ssm_c_re, ssm_c_im=v_ssm_c_im, ssm_d=v_ssm_d, w_glu=v_w_glu, w_kv=v_w_kv,
               w_xattn_out=v_w_xattn_out, w_out=v_w_out, ln1_g=v_ln1_g, ln1_b=v_ln1_b, w_up=v_w_up, b_up=v_b_up,
               w_down=v_w_down, b_down=v_b_down, ln2_g=v_ln2_g, ln2_b=v_ln2_b)
    names = list(W)
    xy = 2 * lax.axis_index("x") + lax.axis_index("y")

    xs = x[0]
    S = xs.shape[0]
    mems = mem[0]
    tgt = loss_target[0]

    shard_bf = {n: W[n][0].astype(bf16) for n in BIG}

    lr = ssm_lam_re.reshape(1, N_STATES)
    li = ssm_lam_im.reshape(1, N_STATES)
    ldt = jnp.repeat(ssm_log_dt.reshape(SSM_GROUPS), SSM_STATE).reshape(1, N_STATES)
    bt_re, bt_im = _state_rows(ssm_b_re[0]), _state_rows(ssm_b_im[0])
    ar, ai, kr, ki, bbt_r, bbt_i = ssm_prep(lr, li, ldt, bt_re, bt_im)
    bd_r, bd_i = _block_diag_b(bbt_r).astype(bf16), _block_diag_b(bbt_i).astype(bf16)
    cd_r, cd_i = _block_diag_c(ssm_c_re[0]).astype(bf16), _block_diag_c(ssm_c_im[0]).astype(bf16)
    bdt_r, bdt_i = bd_r.transpose(0, 2, 1), bd_i.transpose(0, 2, 1)
    cdt_r, cdt_i = cd_r.transpose(0, 2, 1), cd_i.transpose(0, 2, 1)
    d_skip = ssm_d.reshape(1, SSM_WIDTH)

    mem_bf = mems.astype(bf16)
    u_col = GATE_COLS + 3 * CONV_WIDTH
    me_xy = jnp.reshape(xy, (1,)).astype(jnp.int32)
    proj, x_bf, (win3,) = proj_own(xs, shard_bf["w_in"], me_xy, 4, rides=(("gather2", shard_bf["w_in"]),))
    proj, (wco3, wglu3, wxo3, wkv4, wout4, convw4) = proj_rest(
        x_bf, win3, me_xy, proj, rides=tuple(("gather", shard_bf[n]) for n in MID) + (("gather", conv_w[0]),))
    wkv3 = wkv4.reshape(1, D_MODEL, 2 * XATTN_WIDTH)
    wout3 = wout4.reshape(1, D_MODEL, D_MODEL)
    convw_full = convw4.transpose(1, 0, 2).reshape(3, CONV_WIDTH)
    a_conv, y_a = conv_fwd(proj, convw_full, wco3)
    u_perm = _perm(proj[:, u_col:u_col + SSM_WIDTH])
    half = D_MODEL // 2
    (init_r, init_i), (wup_a,) = ssm_scan_fwd(u_perm, bd_r, bd_i, ar, ai,
                                              rides=(("gather", shard_bf["w_up"][:half]),))
    (ysm_perm, ys_perm, st_r, st_i), (wup_b,) = ssm_scan_fwd(
        u_perm, bd_r, bd_i, ar, ai, final=(cd_r, cd_i, d_skip, init_r, init_i),
        rides=(("gather", shard_bf["w_up"][half:]),))
    wup3 = jnp.concatenate([wup_a, wup_b], axis=1)
    y_s = _unperm(ys_perm)
    glu = mm_fwd_small(y_s, wglu3, "glu", out_dtype=bf16)
    kv = mm_fwd_small(mem_bf, wkv3, "kv", out_dtype=bf16)
    o_att, y_c = attn_fwd(proj, kv, wxo3)
    merge_rows = ((proj, D_MODEL, 0), (proj, D_MODEL, 1), (proj, D_MODEL, 2), y_a, (glu, D_MODEL, 0),
                  (glu, D_MODEL, 1), y_c)
    merged, t1, h1, h1_bf = mm_rows_fused(
        (_merge_fwd_pro, merge_rows, (b_gate,)), wout3, "merge_w_out_ln1", transposed=False, tm=256,
        epi=_ln1_fwd_epi, extras=(xs,), consts=(ln1_g, ln1_b), outs=(f32, f32, bf16))
    (r_up, hdn), (wdn4,) = mm_fwd(h1_bf, wup3, "w_up", bias=b_up, outs=(bf16, bf16), second=_relu2,
                                  first=lambda v: jnp.maximum(v, 0.0), rides=(("gather", shard_bf["w_down"]),))
    wdn3 = wdn4.reshape(1, D_FF, D_MODEL)

    dr2, dr2_bf, loss_cols, d_ln2_g, d_ln2_b, d_b_down = mm_rows_fused(
        hdn, wdn3, "w_down_ln2_loss", transposed=False, tm=512, epi=_ln2_loss_epi, extras=(h1, tgt),
        consts=(b_down, ln2_g, ln2_b), outs=(f32, bf16), nsums=4)
    part, other = {}, {}
    g_w_down = mm_bwd_w(hdn, dr2_bf, 1, "dw_down", tk=1024).reshape(4, -1, D_MODEL)
    (dup, d_b_up), (recv_dn,) = mm_bwd_x(dr2_bf, wdn3, "dup", tm=512, extras=(r_up,), colsum=True, out_dtype=bf16,
                                         epi=lambda acc, r: acc * (2.0 * r.astype(f32)),
                                         rides=(("scatter", g_w_down),))
    part["w_down"] = sum_slots(recv_dn, "sum_w_down")
    g_w_up = mm_bwd_w(h1_bf, dup, 4, "dw_up")
    dr1, dr1_bf, d_ln1_g, d_ln1_b = mm_rows_fused(
        dup, wup3, "dh1_ln1_bwd", transposed=True, tm=256, epi=_ln1_bwd_epi, extras=(xs, t1, dr2), consts=(ln1_g,),
        outs=(f32, bf16), nsums=2)
    g_w_out = mm_bwd_w(merged, dr1_bf, 1, "dw_out").reshape(4, -1, D_MODEL)
    (dproj, dy_a, dglu, dy_c, d_b_gate), (recv_up,) = mm_rows_fused(
        dr1_bf, wout3, "dmerged_merge_bwd", transposed=True, tm=256, epi=_merge_bwd_epi, extras=merge_rows,
        consts=(b_gate,), outs=((bf16, GATE_COLS, IN_COLS), bf16, (bf16, 2 * D_MODEL, 2 * D_MODEL), bf16),
        sum_widths=(GATE_COLS,), rides=(("scatter", g_w_up),))
    part["w_up"] = sum_slots(recv_up, "sum_w_up")

    dproj, d_conv_w, g_w_co = conv_bwd(dy_a, a_conv, wco3, proj, convw_full, dproj)

    g_w_glu = mm_bwd_w_small(y_s, dglu, 4, "dw_glu")
    dys_perm = _perm(mm_bwd_x(dglu, wglu3, "dy_s", tm=1024))
    linit_r, linit_i = ssm_scan_bwd(dys_perm, ysm_perm, cdt_r, cdt_i, ar, ai)
    (du_perm, dbacc_r, dbacc_i, dcacc_r, dcacc_i, da_r, da_i, d_ssm_d), (recv_co, recv_glu, recv_out) = ssm_scan_bwd(
        dys_perm, ysm_perm, cdt_r, cdt_i, ar, ai,
        final=(u_perm, st_r, st_i, bdt_r, bdt_i, d_skip, linit_r, linit_i),
        rides=(("scatter", g_w_co), ("scatter", g_w_glu), ("scatter", g_w_out)))
    part["w_out"] = sum_slots(recv_out, "sum_w_out")
    part["w_conv_out"] = sum_slots(recv_co, "sum_w_conv_out")
    part["w_glu"] = sum_slots(recv_glu, "sum_w_glu")
    dbt_re, dbt_im, d_lr, d_li, d_ldt_state = ssm_param_bwd(
        _diag_b(dbacc_r), _diag_b(dbacc_i), bt_re, bt_im, kr, ki, ar, ai, lr, li, ldt, da_r, da_i)
    d_log_dt = group_sum(d_ldt_state.reshape(SSM_GROUPS, SSM_STATE))
    d_b_re = dbt_re.reshape(SSM_GROUP, SSM_GROUPS, SSM_STATE).transpose(1, 2, 0)
    d_b_im = dbt_im.reshape(SSM_GROUP, SSM_GROUPS, SSM_STATE).transpose(1, 2, 0)
    d_c_re = _diag_c(dcacc_r)
    d_c_im = -_diag_c(dcacc_i)

    dproj, dkv, g_w_xo = attn_bwd(dy_c, o_att, wxo3, proj, kv, dproj)
    g_w_kv = mm_bwd_w(mem_bf, dkv, 1, "dw_kv").reshape(4, -1, D_MODEL)

    dproj = lax.dynamic_update_slice(dproj, _unperm(du_perm), (0, u_col))
    g_small = {"b_gate": d_b_gate, "ssm_lam_re": d_lr, "ssm_lam_im": d_li, "ssm_log_dt": d_log_dt, "ssm_b_re": d_b_re,
               "ssm_b_im": d_b_im, "ssm_c_re": d_c_re, "ssm_c_im": d_c_im, "ssm_d": d_ssm_d, "ln1_g": d_ln1_g,
               "ln1_b": d_ln1_b, "b_up": d_b_up, "b_down": d_b_down, "ln2_g": d_ln2_g, "ln2_b": d_ln2_b}
    small_names = SMALL + ("conv_w",)
    g_small["conv_w"] = d_conv_w
    sizes = {n: (g_small[n].size + 1023) // 1024 * 1024 for n in small_names}
    pack = lambda d: jnp.concatenate([_pad_flat(d[n]) for n in small_names]).reshape(-1, 128)
    early = ("w_down", "w_up", "w_out", "w_conv_out", "w_glu")
    recv_in, landed = dw_scatter(
        x_bf, dproj, me_xy, "dw_in", rides=(("scatter", g_w_xo), ("scatter", g_w_kv), ("all", pack(g_small)))
        + tuple(("pair", part[n]) for n in early))
    recv_xo, recv_kv, srecv = landed[:3]
    other.update(zip(early, landed[3:]))
    late = ("w_in", "w_xattn_out", "w_kv")
    for n, r in zip(late, (recv_in, recv_xo, recv_kv)):
        part[n] = sum_slots(r, "sum_" + n)
    dx = mm_bwd_x(dproj, win3, "dx", extras=(dr1,), epi=lambda acc, d: acc + ALPHA * d)
    other.update(zip(late, exchange(*[("pair", part[n]) for n in late], name="swap_late")))
    res = [{}, {}, {}, {}]
    for n in BIG:
        for k, r in enumerate(adam_pair(part[n], other[n], W[n][0], MOM[n][0], VEL[n][0], "adam_" + n)):
            res[k][n] = r[None]
    conv_zero = jnp.zeros((3, CONV_WIDTH), f32)
    gs, ds_, ms, vs = adam_slots(srecv, pack({**{n: W[n] for n in SMALL}, "conv_w": conv_zero}),
                                 pack({**{n: MOM[n] for n in SMALL}, "conv_w": conv_zero}),
                                 pack({**{n: VEL[n] for n in SMALL}, "conv_w": conv_zero}), "adam_small")

    def unpack_small(buf):
        flat = buf.reshape(-1)
        out, r = {}, 0
        for n in small_names:
            ref = g_small[n] if n == "conv_w" else W[n]
            out[n] = flat[r:r + ref.size].reshape(ref.shape)
            r += sizes[n]
        return out

    res_s = [unpack_small(b) for b in (gs, ds_, ms, vs)]
    g_conv = lax.dynamic_slice(res_s[0]["conv_w"], (0, xy * 128), (3, 128))
    conv_slots = g_conv.reshape(1, 3, 128)
    cg, cd, cm, cv = adam_slots(conv_slots, conv_w[0], m_conv_w[0], v_conv_w[0], "adam_conv")
    conv_res = [cg, cd, cm, cv]

    loss = lax.psum(jnp.sum(loss_cols), ("x", "y", "c"))
    outs = [loss, dx.reshape(x.shape)]
    for k in range(4):
        for n in names:
            if n == "conv_w":
                outs.append(conv_res[k].reshape(conv_w.shape))
            elif n in BIG:
                outs.append(res[k][n])
            else:
                outs.append(res_s[k][n])
    return tuple(outs)
```

```python
import functools
import math

import jax
import jax.numpy as jnp
from jax import lax
from jax.experimental import pallas as pl
from jax.experimental.pallas import tpu as pltpu

f32 = jnp.float32
bf16 = jnp.bfloat16

D_MODEL = 1024
CONV_WIDTH = 512
SSM_WIDTH = 512
SSM_GROUP = 16
SSM_GROUPS = 32
SSM_STATE = 64
N_STATES = SSM_GROUPS * SSM_STATE
XATTN_HEADS = 4
XATTN_HEAD_DIM = 128
XATTN_WIDTH = 512
D_FF = 4096
GATE_COLS = 3 * D_MODEL
IN_COLS = GATE_COLS + 3 * CONV_WIDTH + SSM_WIDTH + XATTN_WIDTH
ALPHA = 2.0 ** 0.25
LN_EPS = 1e-5
ADAM_LR = 0.001
ADAM_B1 = 0.9
ADAM_B2 = 0.999
ADAM_EPS = 1e-08
ADAM_WD = 0.01
ADAM_STEP = 10

N_SEG = 8
SCAN_ROWS = 512
LANE_STRIP = 512
TM = 512
VMEM_LIMIT = 48 * 1024 * 1024
MESH = pl.DeviceIdType.MESH

NT_DIMS = (((1,), (1,)), ((), ()))
TN_DIMS = (((0,), (0,)), ((), ()))


def _params():
    return pltpu.CompilerParams(vmem_limit_bytes=VMEM_LIMIT)


def _full(shape):
    n = len(shape)
    return pl.BlockSpec(shape, lambda *_: (0,) * n)


def _rows(tm, w, cb=0):
    return pl.BlockSpec((tm, w), lambda i: (i, cb))


def _sig(x):
    return 1.0 / (1.0 + jnp.exp(-x))


HBM_SPEC = pl.BlockSpec(memory_space=pl.ANY)
RIDE_PEERS = {"gather": 3, "gather2": 6, "scatter": 3, "all": 7, "pair": 1}


def _xy_peers(x, y):
    return [(1 - x, y), (x, 1 - y), (1 - x, 1 - y)]


def _ride_copies(kind, src, dst, send_sems, recv_sems, local_sem):
    x, y, c = lax.axis_index("x"), lax.axis_index("y"), lax.axis_index("c")
    me = 2 * x + y
    if kind == "all":
        flips = [(fx, fy, fc) for fx in (0, 1) for fy in (0, 1) for fc in (0, 1)][1:]
        peers = [(x ^ fx, y ^ fy, c ^ fc) for fx, fy, fc in flips]
        slot = lambda p: 4 * p[0] + 2 * p[1] + p[2]
        mine = slot((x, y, c))
    else:
        peers = [(px, py, c) for px, py in _xy_peers(x, y)]
        slot = lambda p: 2 * p[0] + p[1]
        mine = me

    def remote(k, s, d):
        return pltpu.make_async_remote_copy(src_ref=s, dst_ref=d, send_sem=send_sems.at[k], recv_sem=recv_sems.at[k],
                                            device_id=peers[k], device_id_type=MESH)

    if kind == "pair":
        peers = [(x, y, 1 - c)]
        return None, [remote(0, src, dst)], [remote(0, src, dst)]
    if kind == "scatter":
        local = pltpu.make_async_copy(src.at[me], dst.at[me], local_sem)
        sends = [remote(k, src.at[slot(p)], dst.at[me]) for k, p in enumerate(peers)]
        lands = [remote(k, src.at[me], dst.at[slot(p)]) for k, p in enumerate(peers)]
    else:
        local = pltpu.make_async_copy(src, dst.at[mine], local_sem)
        sends = [remote(k, src, dst.at[mine]) for k, p in enumerate(peers)]
        lands = [remote(k, src, dst.at[slot(p)]) for k, p in enumerate(peers)]
    return local, sends, lands


def _ride_shape(kind, src):
    lead = {"gather": (4,), "gather2": (4,), "scatter": (), "all": (8,), "pair": ()}[kind]
    return jax.ShapeDtypeStruct(lead + src.shape, src.dtype)


def _two_level(src, dst, send_sems, recv_sems, local_sem):
    x, y, c = lax.axis_index("x"), lax.axis_index("y"), lax.axis_index("c")
    me = 2 * x + y
    h = src.shape[0] // 2
    mine = pl.ds(pl.multiple_of(c * h, 16), h)
    theirs = pl.ds(pl.multiple_of((1 - c) * h, 16), h)
    peers = _xy_peers(x, y)
    slots = [2 * px + py for px, py in peers]

    def over_ici(k, d):
        return pltpu.make_async_remote_copy(src_ref=src.at[mine], dst_ref=d, send_sem=send_sems.at[k],
                                            recv_sem=recv_sems.at[k], device_id=(*peers[k], c), device_id_type=MESH)

    def over_d2d(k, rows):
        blk = dst.at[slots[k], rows]
        return pltpu.make_async_remote_copy(src_ref=blk, dst_ref=blk, send_sem=send_sems.at[3 + k],
                                            recv_sem=recv_sems.at[3 + k], device_id=(x, y, 1 - c), device_id_type=MESH)

    return (pltpu.make_async_copy(src, dst.at[me], local_sem),
            [over_ici(k, dst.at[me, mine]) for k in range(3)], [over_ici(k, dst.at[slots[k], mine]) for k in range(3)],
            [over_d2d(k, mine) for k in range(3)], [over_d2d(k, theirs) for k in range(3)])


def _pcall(body, args, *, name, grid, in_specs, out_specs, out_shape, scratch_shapes=(), rides=(), aliases=None,
           prefetch=()):
    n_in, n_out, n_scr, nr, npf = len(in_specs), len(out_specs), len(scratch_shapes), len(rides), len(prefetch)
    kinds = [k for k, _ in rides]

    def wrapped(*refs):
        pre, refs = refs[:npf], refs[npf:]
        ins, rsrc = refs[:n_in], refs[n_in:n_in + nr]
        outs = refs[n_in + nr:n_in + nr + n_out]
        rdst = refs[n_in + nr + n_out:n_in + 2 * nr + n_out]
        scr = refs[n_in + 2 * nr + n_out:n_in + 2 * nr + n_out + n_scr]
        sems = refs[n_in + 2 * nr + n_out + n_scr:]

        def start():
            for r, kind in enumerate(kinds):
                if kind == "gather2":
                    local, sends, _, _, _ = _two_level(rsrc[r], rdst[r], *sems[3 * r:3 * r + 3])
                else:
                    local, sends, _ = _ride_copies(kind, rsrc[r], rdst[r], *sems[3 * r:3 * r + 3])
                if local is not None:
                    local.start()
                for cp in sends:
                    cp.start()

        def finish():
            for r, kind in enumerate(kinds):
                if kind == "gather2":
                    local, sends, ici_lands, forwards, lands = _two_level(rsrc[r], rdst[r], *sems[3 * r:3 * r + 3])
                    for k in range(3):
                        ici_lands[k].wait_recv()
                        forwards[k].start()
                    sends = sends + forwards
                else:
                    local, sends, lands = _ride_copies(kind, rsrc[r], rdst[r], *sems[3 * r:3 * r + 3])
                for cp in lands:
                    cp.wait_recv()
                for cp in sends:
                    cp.wait_send()
                if local is not None:
                    local.wait()

        if nr and grid:
            ids = [pl.program_id(a) for a in range(len(grid))]
            first = functools.reduce(jnp.logical_and, [i == 0 for i in ids])
            last = functools.reduce(jnp.logical_and, [i == g - 1 for i, g in zip(ids, grid)])
            pl.when(first)(start)
        elif nr:
            start()
        body(*pre, *ins, *outs, *scr)
        if nr and grid:
            pl.when(last)(finish)
        elif nr:
            finish()

    sems = []
    for kind in kinds:
        n = RIDE_PEERS[kind]
        sems += [pltpu.SemaphoreType.DMA((n,)), pltpu.SemaphoreType.DMA((n,)), pltpu.SemaphoreType.DMA(())]
    all_in = list(in_specs) + [HBM_SPEC] * nr
    all_out = list(out_specs) + [HBM_SPEC] * nr
    all_scratch = list(scratch_shapes) + sems
    if npf:
        how = dict(grid_spec=pltpu.PrefetchScalarGridSpec(num_scalar_prefetch=npf, grid=grid, in_specs=all_in,
                                                          out_specs=all_out, scratch_shapes=all_scratch))
    else:
        how = dict(grid=grid, in_specs=all_in, out_specs=all_out, scratch_shapes=all_scratch)
    res = pl.pallas_call(
        wrapped, name=name, out_shape=list(out_shape) + [_ride_shape(k, s) for k, s in rides],
        input_output_aliases=aliases or {}, compiler_params=_params(), **how)(
            *prefetch, *args, *[s for _, s in rides])
    return list(res[:n_out]), list(res[n_out:])


def _bf(v):
    return v if v.dtype == bf16 else v.astype(bf16)


def mm_fwd_small(a, w3, name, *, tm=1024, out_dtype=f32):
    M, K = a.shape
    J, _, n = w3.shape
    tm = min(tm, M)

    def body(a_ref, w_ref, o_ref):
        av = _bf(a_ref[...])
        for j in range(J):
            o_ref[:, j * n:(j + 1) * n] = jnp.dot(av, w_ref[j], preferred_element_type=f32).astype(out_dtype)

    return pl.pallas_call(
        body, name=name, grid=(M // tm,), in_specs=[_rows(tm, K), _full((J, K, n))], out_specs=_rows(tm, J * n),
        out_shape=jax.ShapeDtypeStruct((M, J * n), out_dtype), compiler_params=_params())(a, w3)


def mm_bwd_w_small(a, dy, J, name, *, tm=2048):
    M, K = a.shape
    n = dy.shape[1] // J
    tm = min(tm, M)
    ns = M // tm

    def body(a_ref, dy_ref, o_ref, acc_ref):
        s = pl.program_id(0)
        part = lax.dot_general(_bf(a_ref[...]), _bf(dy_ref[...]), TN_DIMS, preferred_element_type=f32)

        @pl.when(s == 0)
        def _():
            acc_ref[...] = part

        @pl.when(s > 0)
        def _():
            acc_ref[...] += part

        @pl.when(s == ns - 1)
        def _():
            for j in range(J):
                o_ref[j] = acc_ref[:, j * n:(j + 1) * n].astype(bf16)

    return pl.pallas_call(
        body, name=name, grid=(ns,), in_specs=[_rows(tm, K), _rows(tm, J * n)], out_specs=_full((J, K, n)),
        out_shape=jax.ShapeDtypeStruct((J, K, n), bf16), scratch_shapes=[pltpu.VMEM((K, J * n), f32)],
        compiler_params=_params())(a, dy)


def mm_fwd(a, w3, name, *, tm=1024, tn=None, bias=None, outs=(f32,), first=None, second=None, emit_a=False,
           rides=()):
    M, K = a.shape
    J, _, n = w3.shape
    tm = min(tm, M)
    tn = tn or n
    nl = n // tn
    nb = 0 if bias is None else 1

    def body(*refs):
        a_ref, w_ref = refs[0], refs[1]
        av = _bf(a_ref[...])
        acc = jnp.dot(av, w_ref[0], preferred_element_type=f32)
        if bias is not None:
            acc = acc + refs[2][...]
        refs[2 + nb][...] = (acc if first is None else first(acc)).astype(outs[0])
        if len(outs) > 1:
            refs[3 + nb][...] = second(acc).astype(outs[1])
        if emit_a:
            @pl.when(jnp.logical_and(pl.program_id(0) == 0, pl.program_id(1) == 0))
            def _():
                refs[2 + nb + len(outs)][...] = av

    in_specs = [pl.BlockSpec((tm, K), lambda j, l, i: (i, 0)),
                pl.BlockSpec((1, K, tn), lambda j, l, i: (j, 0, l))]
    args = [a, w3]
    if bias is not None:
        in_specs.append(pl.BlockSpec((1, tn), lambda j, l, i: (0, j * nl + l)))
        args.append(bias)
    out_specs = [pl.BlockSpec((tm, tn), lambda j, l, i: (i, j * nl + l)) for _ in outs]
    out_shape = [jax.ShapeDtypeStruct((M, J * n), dt) for dt in outs]
    if emit_a:
        last = M // tm - 1
        out_specs.append(pl.BlockSpec((tm, K), lambda j, l, i: (jnp.where((j == 0) & (l == 0), i, last), 0)))
        out_shape.append(jax.ShapeDtypeStruct((M, K), bf16))
    res, landed = _pcall(body, args, name=name, grid=(J, nl, M // tm), in_specs=in_specs, out_specs=out_specs,
                         out_shape=out_shape, rides=rides)
    res = res if len(res) > 1 else res[0]
    return (res, landed) if rides else res


def proj_own(x, w_own, me, n_blocks, *, tm=1024, rides=()):
    M, K = x.shape
    n = w_own.shape[1]

    def body(me_ref, x_ref, w_ref, o_ref, xb_ref):
        av = _bf(x_ref[...])
        xb_ref[...] = av
        o_ref[...] = jnp.dot(av, w_ref[...], preferred_element_type=f32).astype(bf16)

    (proj, x_bf), landed = _pcall(
        body, [x, w_own], name="proj_own", grid=(M // tm,), prefetch=(me,),
        in_specs=[pl.BlockSpec((tm, K), lambda i, me_ref: (i, 0)), pl.BlockSpec((K, n), lambda i, me_ref: (0, 0))],
        out_specs=[pl.BlockSpec((tm, n), lambda i, me_ref: (i, me_ref[0])),
                   pl.BlockSpec((tm, K), lambda i, me_ref: (i, 0))],
        out_shape=[jax.ShapeDtypeStruct((M, n_blocks * n), bf16), jax.ShapeDtypeStruct((M, K), bf16)], rides=rides)
    return proj, x_bf, landed


def proj_rest(x_bf, w3, me, proj, *, tm=1024, rides=()):
    M, K = x_bf.shape
    J, _, n = w3.shape

    def body(me_ref, x_ref, w_ref, _, o_ref):
        o_ref[...] = jnp.dot(x_ref[...], w_ref[0], preferred_element_type=f32).astype(bf16)

    (proj,), landed = _pcall(
        body, [x_bf, w3, proj], name="proj_rest", grid=(J - 1, M // tm), prefetch=(me,),
        in_specs=[pl.BlockSpec((tm, K), lambda k, i, me_ref: (i, 0)),
                  pl.BlockSpec((1, K, n), lambda k, i, me_ref: (me_ref[0] ^ (k + 1), 0, 0)), HBM_SPEC],
        out_specs=[pl.BlockSpec((tm, n), lambda k, i, me_ref: (i, me_ref[0] ^ (k + 1)))],
        out_shape=[jax.ShapeDtypeStruct(proj.shape, bf16)], aliases={3: 0}, rides=rides)
    return proj, landed


def mm_bwd_x(dy, w3, name, *, tm=TM, epi=None, extras=(), colsum=False, out_dtype=f32, rides=()):
    M = dy.shape[0]
    J, K, n = w3.shape
    tm = min(tm, M)
    nex = len(extras)

    def body(*refs):
        dy_ref, w_hbm = refs[0], refs[1]
        ex = refs[2:2 + nex]
        o_ref = refs[2 + nex]
        w_ref = refs[-1]
        i = pl.program_id(0)

        @pl.when(i == 0)
        def _():
            pltpu.sync_copy(w_hbm, w_ref)

        acc = None
        for j in range(J):
            part = lax.dot_general(_bf(dy_ref[:, j * n:(j + 1) * n]), w_ref[j], NT_DIMS, preferred_element_type=f32)
            acc = part if acc is None else acc + part
        if epi is not None:
            acc = epi(acc, *[e[...] for e in ex])
        o_ref[...] = acc.astype(out_dtype)
        if colsum:
            s_ref = refs[3 + nex]
            cs = jnp.sum(acc, axis=0, keepdims=True)

            @pl.when(i == 0)
            def _():
                s_ref[...] = cs

            @pl.when(i > 0)
            def _():
                s_ref[...] += cs

    in_specs = [pl.BlockSpec((tm, J * n), lambda i: (i, 0)), pl.BlockSpec(memory_space=pl.ANY)]
    in_specs += [pl.BlockSpec((tm, K), lambda i: (i, 0)) for _ in extras]
    out_specs = [pl.BlockSpec((tm, K), lambda i: (i, 0))]
    out_shape = [jax.ShapeDtypeStruct((M, K), out_dtype)]
    if colsum:
        out_specs.append(pl.BlockSpec((1, K), lambda i: (0, 0)))
        out_shape.append(jax.ShapeDtypeStruct((1, K), f32))
    res, landed = _pcall(body, [dy, w3, *extras], name=name, grid=(M // tm,), in_specs=in_specs, out_specs=out_specs,
                         out_shape=out_shape, scratch_shapes=[pltpu.VMEM((J, K, n), bf16)], rides=rides)
    res = res if colsum else res[0]
    return (res, landed) if rides else res


def mm_rows_fused(a, w3, name, *, transposed, tm, epi, extras=(), consts=(), outs=(f32,), nsums=0, sum_widths=None,
                  rides=()):
    pro = a[0] if isinstance(a, tuple) else None
    J, K, n = w3.shape
    W = K if transposed else n
    ka = J * n if transposed else K
    if pro is not None:
        a_rows = [e if isinstance(e, tuple) else (e, ka, 0) for e in a[1]]
        a_consts = list(a[2])
    else:
        a_rows, a_consts = [(a, ka, 0)], []
    M = a_rows[0][0].shape[0]
    tm = min(tm, M)
    na = len(a_rows) + len(a_consts)
    nex, nco = len(extras), len(consts)
    if pro is not None:
        outs = (bf16 if ka == W else (bf16, ka, ka),) + tuple(outs)
    nout = len(outs)
    nsums = len(sum_widths) if sum_widths is not None else nsums
    out_dtypes = [o[0] if isinstance(o, tuple) else o for o in outs]

    def body(*refs):
        a_refs, w_hbm = refs[:na], refs[na]
        ex = refs[na + 1:na + 1 + nex]
        co = refs[na + 1 + nex:na + 1 + nex + nco]
        o_refs = refs[na + 1 + nex + nco:na + 1 + nex + nco + nout]
        s_refs = refs[na + 1 + nex + nco + nout:na + 1 + nex + nco + nout + nsums]
        w_ref = refs[-1]
        i = pl.program_id(0)

        @pl.when(i == 0)
        def _():
            pltpu.sync_copy(w_hbm, w_ref)

        av = _bf(a_refs[0][...] if pro is None else pro(*[r[...] for r in a_refs]))
        if transposed:
            acc = None
            for j in range(J):
                part = lax.dot_general(av[:, j * n:(j + 1) * n], w_ref[j], NT_DIMS, preferred_element_type=f32)
                acc = part if acc is None else acc + part
        else:
            acc = jnp.dot(av, w_ref[0], preferred_element_type=f32)
        rows, sums = epi(acc, *[e[...] for e in ex], *[c[...] for c in co])
        if pro is not None:
            rows = (av,) + tuple(rows)
        for o_ref, v, dt in zip(o_refs, rows, out_dtypes):
            o_ref[...] = v.astype(dt)
        for s_ref, v in zip(s_refs, sums):
            @pl.when(i == 0)
            def _(s_ref=s_ref, v=v):
                s_ref[...] = v

            @pl.when(i > 0)
            def _(s_ref=s_ref, v=v):
                s_ref[...] += v

    extras = [e if isinstance(e, tuple) else (e, W, 0) for e in extras]
    outs_full = [o if isinstance(o, tuple) else (o, W, W) for o in outs]
    widths = list(sum_widths) if sum_widths is not None else [W] * nsums
    in_specs = [_rows(tm, w, cb) for _, w, cb in a_rows] + [_full(c.shape) for c in a_consts]
    in_specs += [pl.BlockSpec(memory_space=pl.ANY)]
    in_specs += [_rows(tm, w, cb) for _, w, cb in extras] + [_full(c.shape) for c in consts]
    out_specs = [_rows(tm, w) for _, w, _ in outs_full] + [_full((1, w)) for w in widths]
    out_shape = ([jax.ShapeDtypeStruct((M, cols), dt) for dt, _, cols in outs_full]
                 + [jax.ShapeDtypeStruct((1, w), f32) for w in widths])
    args = [e for e, _, _ in a_rows] + a_consts + [w3] + [e for e, _, _ in extras] + list(consts)
    res, landed = _pcall(body, args, name=name, grid=(M // tm,), in_specs=in_specs, out_specs=out_specs,
                         out_shape=out_shape, scratch_shapes=[pltpu.VMEM((J, K, n), bf16)], rides=rides)
    return (res, landed) if rides else res


def mm_bwd_w(a, dy, J, name, *, tm=2048, tn=None, tk=None, rides=()):
    M, K = a.shape
    n = dy.shape[1] // J
    tm = min(tm, M)
    tn = tn or n
    tk = tk or K
    nl = n // tn
    nk = K // tk
    ns = M // tm

    def body(a_ref, dy_ref, o_ref, acc_ref):
        s = pl.program_id(3)
        part = lax.dot_general(_bf(a_ref[...]), _bf(dy_ref[...]), TN_DIMS, preferred_element_type=f32)

        @pl.when(s == 0)
        def _():
            acc_ref[...] = part

        @pl.when(s > 0)
        def _():
            acc_ref[...] += part

        @pl.when(s == ns - 1)
        def _():
            o_ref[0] = acc_ref[...].astype(bf16)

    res, landed = _pcall(
        body, [a, dy], name=name, grid=(J, nl, nk, ns),
        in_specs=[pl.BlockSpec((tm, tk), lambda j, l, k, s: (s, k)),
                  pl.BlockSpec((tm, tn), lambda j, l, k, s: (s, j * nl + l))],
        out_specs=[pl.BlockSpec((1, tk, tn), lambda j, l, k, s: (j, k, l))],
        out_shape=[jax.ShapeDtypeStruct((J, K, n), bf16)],
        scratch_shapes=[pltpu.VMEM((tk, tn), f32)], rides=rides)
    return (res[0], landed) if rides else res[0]


def _relu2(v):
    r = jnp.maximum(v, 0.0)
    return r * r


HALO = 16


def _shift_down(z, k, halo):
    r = lax.broadcasted_iota(jnp.int32, z.shape, 0)
    y = pltpu.roll(z, k, 0)
    for q in range(k):
        y = jnp.where(r == q, halo[HALO - k + q:HALO - k + q + 1, :], y)
    return y


def _shift_up(z, k, halo):
    tm = z.shape[0]
    r = lax.broadcasted_iota(jnp.int32, z.shape, 0)
    y = pltpu.roll(z, tm - k, 0)
    for q in range(k):
        y = jnp.where(r == tm - k + q, halo[q:q + 1, :], y)
    return y


def _prev_halo(tm, cb):
    return pl.BlockSpec((HALO, CONV_WIDTH), lambda i: (jnp.maximum(i * (tm // HALO) - 1, 0), cb))


def _next_halo(tm, cb, nblk):
    return pl.BlockSpec((HALO, CONV_WIDTH), lambda i: (jnp.minimum((i + 1) * (tm // HALO), nblk - 1), cb))


def _f32(ref):
    return ref[...].astype(f32)


def conv_fwd(proj, conv_w, wco3, tm=TM):
    S = proj.shape[0]
    J, _, n = wco3.shape

    def body(cb_ref, cc_ref, ch_ref, cch_ref, chh_ref, w_ref, wo_ref, a_ref, y_ref):
        i = pl.program_id(0)
        z = _f32(cc_ref) * _f32(ch_ref)
        zh = jnp.where(i == 0, 0.0, _f32(cch_ref) * _f32(chh_ref))
        w = w_ref[...]
        dwz = w[0:1, :] * _shift_down(z, 2, zh) + w[1:2, :] * _shift_down(z, 1, zh) + w[2:3, :] * z
        a = (_f32(cb_ref) * dwz).astype(bf16)
        a_ref[...] = a
        for j in range(J):
            y_ref[:, j * n:(j + 1) * n] = jnp.dot(a, wo_ref[j], preferred_element_type=f32).astype(bf16)

    return pl.pallas_call(
        body, name="conv_fwd", grid=(S // tm,),
        in_specs=[_rows(tm, CONV_WIDTH, 6), _rows(tm, CONV_WIDTH, 7), _rows(tm, CONV_WIDTH, 8),
                  _prev_halo(tm, 7), _prev_halo(tm, 8), _full((3, CONV_WIDTH)), _full(wco3.shape)],
        out_specs=[_rows(tm, CONV_WIDTH), _rows(tm, J * n)],
        out_shape=[jax.ShapeDtypeStruct((S, CONV_WIDTH), bf16), jax.ShapeDtypeStruct((S, J * n), bf16)],
        compiler_params=_params())(proj, proj, proj, proj, proj, conv_w, wco3)


def conv_bwd(dy_a, a_conv, wco3, proj, conv_w, dproj, tm=TM):
    S = proj.shape[0]
    nt = S // tm
    J, _, n = wco3.shape

    def body(dy_ref, dyn_ref, a_ref, wo_ref, cb_ref, cc_ref, ch_ref, cch_ref, chh_ref, cbn_ref, w_ref, _, o_ref,
             dw_ref, dwo_ref, acc_ref):
        i = pl.program_id(0)

        def times_wo_t(dy):
            out = None
            for j in range(J):
                part = lax.dot_general(dy[:, j * n:(j + 1) * n], wo_ref[j], NT_DIMS, preferred_element_type=f32)
                out = part if out is None else out + part
            return out

        dyv = dy_ref[...]
        dav, dan = times_wo_t(dyv), times_wo_t(dyn_ref[...])
        part = lax.dot_general(a_ref[...], dyv, TN_DIMS, preferred_element_type=f32)

        @pl.when(i == 0)
        def _():
            acc_ref[...] = part

        @pl.when(i > 0)
        def _():
            acc_ref[...] += part

        @pl.when(i == nt - 1)
        def _():
            for j in range(J):
                dwo_ref[j] = acc_ref[:, j * n:(j + 1) * n].astype(bf16)

        cc, ch, cb = _f32(cc_ref), _f32(ch_ref), _f32(cb_ref)
        z = cc * ch
        zh = jnp.where(i == 0, 0.0, _f32(cch_ref) * _f32(chh_ref))
        w = w_ref[...]
        z1 = _shift_down(z, 1, zh)
        z2 = _shift_down(z, 2, zh)
        dwz = w[0:1, :] * z2 + w[1:2, :] * z1 + w[2:3, :] * z
        g = dav * cb
        gn = jnp.where(i == nt - 1, 0.0, dan * _f32(cbn_ref))
        dz = w[2:3, :] * g + w[1:2, :] * _shift_up(g, 1, gn) + w[0:1, :] * _shift_up(g, 2, gn)
        o_ref[:, 0:CONV_WIDTH] = (dav * dwz).astype(bf16)
        o_ref[:, CONV_WIDTH:2 * CONV_WIDTH] = (dz * ch).astype(bf16)
        o_ref[:, 2 * CONV_WIDTH:3 * CONV_WIDTH] = (dz * cc).astype(bf16)

        @pl.when(i == 0)
        def _():
            dw_ref[...] = jnp.zeros_like(dw_ref)

        dw_ref[0:1, :] += jnp.sum(g * z2, axis=0, keepdims=True)
        dw_ref[1:2, :] += jnp.sum(g * z1, axis=0, keepdims=True)
        dw_ref[2:3, :] += jnp.sum(g * z, axis=0, keepdims=True)

    return pl.pallas_call(
        body, name="conv_bwd", grid=(nt,),
        in_specs=[_rows(tm, J * n),
                  pl.BlockSpec((HALO, J * n), lambda i: (jnp.minimum((i + 1) * (tm // HALO), S // HALO - 1), 0)),
                  _rows(tm, CONV_WIDTH), _full(wco3.shape),
                  _rows(tm, CONV_WIDTH, 6), _rows(tm, CONV_WIDTH, 7), _rows(tm, CONV_WIDTH, 8),
                  _prev_halo(tm, 7), _prev_halo(tm, 8), _next_halo(tm, 6, S // HALO), _full((3, CONV_WIDTH)),
                  pl.BlockSpec(memory_space=pl.ANY)],
        out_specs=[_rows(tm, 3 * CONV_WIDTH, GATE_COLS // (3 * CONV_WIDTH)), _full((3, CONV_WIDTH)),
                   _full(wco3.shape)],
        out_shape=[jax.ShapeDtypeStruct(dproj.shape, bf16), jax.ShapeDtypeStruct((3, CONV_WIDTH), f32),
                   jax.ShapeDtypeStruct(wco3.shape, bf16)],
        scratch_shapes=[pltpu.VMEM((CONV_WIDTH, J * n), f32)], input_output_aliases={11: 0},
        compiler_params=_params())(dy_a, dy_a, a_conv, wco3, proj, proj, proj, proj, proj, proj, conv_w, dproj)


def _cmul(ar, ai, br, bi):
    return ar * br - ai * bi, ar * bi + ai * br


def ssm_prep(lr, li, ldt, bt_re, bt_im):
    def body(lr_ref, li_ref, ldt_ref, br_ref, bi_ref, ar_ref, ai_ref, kr_ref, ki_ref, bbr_ref, bbi_ref):
        lrv, liv = lr_ref[...], li_ref[...]
        dt = jnp.exp(ldt_ref[...])
        mag = jnp.exp(lrv * dt)
        ar = mag * jnp.cos(liv * dt)
        ai = mag * jnp.sin(liv * dt)
        den = lrv * lrv + liv * liv
        nr = ar - 1.0
        kr = (nr * lrv + ai * liv) / den
        ki = (ai * lrv - nr * liv) / den
        ar_ref[...], ai_ref[...], kr_ref[...], ki_ref[...] = ar, ai, kr, ki
        bbr_ref[...] = kr * br_ref[...] - ki * bi_ref[...]
        bbi_ref[...] = kr * bi_ref[...] + ki * br_ref[...]

    v = jax.ShapeDtypeStruct((1, N_STATES), f32)
    m = jax.ShapeDtypeStruct((SSM_GROUP, N_STATES), f32)
    return pl.pallas_call(body, name="ssm_prep", out_shape=[v, v, v, v, m, m])(lr, li, ldt, bt_re, bt_im)


def _pow_segment(ar, ai, seg_len):
    pr, pi = ar, ai
    for _ in range(int(math.log2(seg_len))):
        pr, pi = _cmul(pr, pi, pr, pi)
    return pr, pi


GELU_K = math.sqrt(2.0 / math.pi)
GELU_C = 0.044715


def _gelu(v):
    return 0.5 * v * (1.0 + jnp.tanh(GELU_K * (v + GELU_C * v * v * v)))


def _gelu_grad(v):
    t = jnp.tanh(GELU_K * (v + GELU_C * v * v * v))
    return 0.5 * (1.0 + t) + 0.5 * v * (1.0 - t * t) * GELU_K * (1.0 + 3.0 * GELU_C * v * v)


def ssm_scan_fwd(u_perm, bd_r, bd_i, ar, ai, *, final=None, rides=()):
    S = u_perm.shape[0]
    R = SCAN_ROWS
    nblk = S // R
    seg_len = S // N_SEG
    nstrip = N_STATES // LANE_STRIP
    store = final is not None

    def body(*refs):
        if store:
            (u_ref, bdr_ref, bdi_ref, ar_ref, ai_ref, cdr_ref, cdi_ref, d_ref, ir_ref, ii_ref,
             y_ref, ys_ref, sr_ref, si_ref, bur, bui, car, cai) = refs
        else:
            u_ref, bdr_ref, bdi_ref, ar_ref, ai_ref, or_ref, oi_ref, bur, bui, car, cai = refs
        i = pl.program_id(0)
        ub = _bf(u_ref[...])
        u = ub.astype(f32)
        for cb in range(4):
            us = ub[:, cb * 128:(cb + 1) * 128]
            bur[:, cb * 512:(cb + 1) * 512] = jnp.dot(us, bdr_ref[cb], preferred_element_type=f32)
            bui[:, cb * 512:(cb + 1) * 512] = jnp.dot(us, bdi_ref[cb], preferred_element_type=f32)

        @pl.when(i == 0)
        def _():
            if store:
                car[...] = ir_ref[...]
                cai[...] = ii_ref[...]
            else:
                car[...] = jnp.zeros_like(car)
                cai[...] = jnp.zeros_like(cai)

        for ls in range(nstrip):
            lanes = pl.ds(ls * LANE_STRIP, LANE_STRIP)
            a_r = jnp.broadcast_to(ar_ref[:, lanes], (N_SEG, LANE_STRIP))
            a_i = jnp.broadcast_to(ai_ref[:, lanes], (N_SEG, LANE_STRIP))

            def step(t, carry, lanes=lanes, a_r=a_r, a_i=a_i):
                s_r, s_i = carry
                row = pl.multiple_of(t * 8, 8)
                n_r = a_r * s_r - a_i * s_i + bur[pl.ds(row, 8), lanes]
                n_i = a_r * s_i + a_i * s_r + bui[pl.ds(row, 8), lanes]
                if store:
                    sr_ref[pl.ds(row, 8), lanes] = n_r
                    si_ref[pl.ds(row, 8), lanes] = n_i
                return n_r, n_i

            e_r, e_i = lax.fori_loop(0, R // 8, step, (car[:, lanes], cai[:, lanes]), unroll=True)
            car[:, lanes] = e_r
            cai[:, lanes] = e_i

        if store:
            for cb in range(4):
                st_r = sr_ref[:, cb * 512:(cb + 1) * 512].astype(bf16)
                st_i = si_ref[:, cb * 512:(cb + 1) * 512].astype(bf16)
                y = (jnp.dot(st_r, cdr_ref[cb], preferred_element_type=f32)
                     - jnp.dot(st_i, cdi_ref[cb], preferred_element_type=f32))
                cols = slice(cb * 128, (cb + 1) * 128)
                y = y + d_ref[:, cols] * u[:, cols]
                y_ref[:, cols] = y
                ys_ref[:, cols] = _gelu(y).astype(bf16)
        else:
            @pl.when(i == nblk - 1)
            def _():
                p_r, p_i = _pow_segment(ar_ref[...], ai_ref[...], seg_len)
                t_r, t_i = car[0:1, :], cai[0:1, :]
                or_ref[0:1, :] = jnp.zeros((1, N_STATES), f32)
                oi_ref[0:1, :] = jnp.zeros((1, N_STATES), f32)
                for j in range(1, N_SEG):
                    or_ref[j:j + 1, :] = t_r
                    oi_ref[j:j + 1, :] = t_i
                    m_r, m_i = _cmul(p_r, p_i, t_r, t_i)
                    t_r, t_i = car[j:j + 1, :] + m_r, cai[j:j + 1, :] + m_i

    blk = lambda w: pl.BlockSpec((R, w), lambda i: (i, 0))
    in_specs = [blk(SSM_WIDTH), _full((4, 128, 512)), _full((4, 128, 512)), _full((1, N_STATES)), _full((1, N_STATES))]
    args = [u_perm, bd_r, bd_i, ar, ai]
    scratch = [pltpu.VMEM((R, N_STATES), f32), pltpu.VMEM((R, N_STATES), f32),
               pltpu.VMEM((N_SEG, N_STATES), f32), pltpu.VMEM((N_SEG, N_STATES), f32)]
    if store:
        in_specs += [_full((4, 512, 128)), _full((4, 512, 128)), _full((1, SSM_WIDTH)),
                     _full((N_SEG, N_STATES)), _full((N_SEG, N_STATES))]
        args += list(final)
        out_specs = [blk(SSM_WIDTH), blk(SSM_WIDTH), blk(N_STATES), blk(N_STATES)]
        out_shape = [jax.ShapeDtypeStruct((S, SSM_WIDTH), f32), jax.ShapeDtypeStruct((S, SSM_WIDTH), bf16),
                     jax.ShapeDtypeStruct((S, N_STATES), f32), jax.ShapeDtypeStruct((S, N_STATES), f32)]
        name = "ssm_scan_fwd"
    else:
        out_specs = [_full((N_SEG, N_STATES)), _full((N_SEG, N_STATES))]
        out_shape = [jax.ShapeDtypeStruct((N_SEG, N_STATES), f32)] * 2
        name = "ssm_scan_fwd_carry"
    res, landed = _pcall(body, args, name=name, grid=(nblk,), in_specs=in_specs, out_specs=out_specs,
                         out_shape=out_shape, scratch_shapes=scratch, rides=rides)
    return (res, landed) if rides else res


def ssm_scan_bwd(dys_perm, y_perm, cdt_r, cdt_i, ar, ai, *, final=None, rides=()):
    S = dys_perm.shape[0]
    R = SCAN_ROWS
    nblk = S // R
    seg_len = S // N_SEG
    store = final is not None
    strip = LANE_STRIP // 2 if store else LANE_STRIP
    nstrip = N_STATES // strip

    def body(*refs):
        if store:
            (dys_ref, y_ref, cdr_ref, cdi_ref, ar_ref, ai_ref, u_ref, sr_ref, si_ref, bdr_ref, bdi_ref, d_ref, ir_ref,
             ii_ref, du_ref, dbr_ref, dbi_ref, dcr_ref, dci_ref, dar_ref, dai_ref, dd_ref, dsr, dsi, lmr, lmi, car,
             cai) = refs
        else:
            dys_ref, y_ref, cdr_ref, cdi_ref, ar_ref, ai_ref, or_ref, oi_ref, dsr, dsi, car, cai = refs
        i = pl.program_id(0)
        dy = dys_ref[...] * _gelu_grad(y_ref[...])
        dyb = dy.astype(bf16)
        for cb in range(4):
            ds_ = dyb[:, cb * 128:(cb + 1) * 128]
            dsr[:, cb * 512:(cb + 1) * 512] = jnp.dot(ds_, cdr_ref[cb], preferred_element_type=f32)
            dsi[:, cb * 512:(cb + 1) * 512] = -jnp.dot(ds_, cdi_ref[cb], preferred_element_type=f32)

        @pl.when(i == 0)
        def _():
            if store:
                car[...] = ir_ref[...]
                cai[...] = ii_ref[...]
                dar_ref[...] = jnp.zeros_like(dar_ref)
                dai_ref[...] = jnp.zeros_like(dai_ref)
                dbr_ref[...] = jnp.zeros_like(dbr_ref)
                dbi_ref[...] = jnp.zeros_like(dbi_ref)
                dcr_ref[...] = jnp.zeros_like(dcr_ref)
                dci_ref[...] = jnp.zeros_like(dci_ref)
                dd_ref[...] = jnp.zeros_like(dd_ref)
            else:
                car[...] = jnp.zeros_like(car)
                cai[...] = jnp.zeros_like(cai)

        for ls in range(nstrip):
            lanes = pl.ds(ls * strip, strip)
            a_r = jnp.broadcast_to(ar_ref[:, lanes], (N_SEG, strip))
            a_i = jnp.broadcast_to(ai_ref[:, lanes], (N_SEG, strip))
            zero = jnp.zeros((N_SEG, strip), f32)

            def step(k, carry, lanes=lanes, a_r=a_r, a_i=a_i):
                l_r, l_i, g_r, g_i = carry
                row = pl.multiple_of((R // 8 - 1 - k) * 8, 8)
                if store:
                    s_r = sr_ref[pl.ds(row, 8), lanes]
                    s_i = si_ref[pl.ds(row, 8), lanes]
                    g_r = g_r + l_r * s_r + l_i * s_i
                    g_i = g_i + l_i * s_r - l_r * s_i
                n_r = dsr[pl.ds(row, 8), lanes] + a_r * l_r + a_i * l_i
                n_i = dsi[pl.ds(row, 8), lanes] + a_r * l_i - a_i * l_r
                if store:
                    lmr[pl.ds(row, 8), lanes] = n_r
                    lmi[pl.ds(row, 8), lanes] = n_i
                return n_r, n_i, g_r, g_i

            e_r, e_i, g_r, g_i = lax.fori_loop(0, R // 8, step, (car[:, lanes], cai[:, lanes], zero, zero),
                                               unroll=True)
            car[:, lanes] = e_r
            cai[:, lanes] = e_i
            if store:
                dar_ref[:, lanes] += g_r
                dai_ref[:, lanes] += g_i

        if store:
            ub = _bf(u_ref[...])
            u = ub.astype(f32)
            for cb in range(4):
                cols = slice(cb * 128, (cb + 1) * 128)
                st = slice(cb * 512, (cb + 1) * 512)
                l_r = lmr[:, st].astype(bf16)
                l_i = lmi[:, st].astype(bf16)
                du = (jnp.dot(l_r, bdr_ref[cb], preferred_element_type=f32)
                      + jnp.dot(l_i, bdi_ref[cb], preferred_element_type=f32))
                du_ref[:, cols] = (du + d_ref[:, cols] * dy[:, cols]).astype(bf16)
                dbr_ref[cb] += lax.dot_general(l_r, ub[:, cols], TN_DIMS, preferred_element_type=f32)
                dbi_ref[cb] += lax.dot_general(l_i, ub[:, cols], TN_DIMS, preferred_element_type=f32)
                dcr_ref[cb] += lax.dot_general(dyb[:, cols], sr_ref[:, st].astype(bf16), TN_DIMS,
                                               preferred_element_type=f32)
                dci_ref[cb] += lax.dot_general(dyb[:, cols], si_ref[:, st].astype(bf16), TN_DIMS,
                                               preferred_element_type=f32)
            dd_ref[...] += jnp.sum(dy * u, axis=0, keepdims=True)
        else:
            @pl.when(i == nblk - 1)
            def _():
                p_r, p_i = _pow_segment(ar_ref[...], ai_ref[...], seg_len)
                p_i = -p_i
                t_r, t_i = car[N_SEG - 1:N_SEG, :], cai[N_SEG - 1:N_SEG, :]
                or_ref[N_SEG - 1:N_SEG, :] = jnp.zeros((1, N_STATES), f32)
                oi_ref[N_SEG - 1:N_SEG, :] = jnp.zeros((1, N_STATES), f32)
                for j in range(N_SEG - 2, -1, -1):
                    or_ref[j:j + 1, :] = t_r
                    oi_ref[j:j + 1, :] = t_i
                    m_r, m_i = _cmul(p_r, p_i, t_r, t_i)
                    t_r, t_i = car[j:j + 1, :] + m_r, cai[j:j + 1, :] + m_i

    blk = lambda w: pl.BlockSpec((R, w), lambda i: (nblk - 1 - i, 0))
    in_specs = [blk(SSM_WIDTH), blk(SSM_WIDTH), _full((4, 128, 512)), _full((4, 128, 512)), _full((1, N_STATES)),
                _full((1, N_STATES))]
    args = [dys_perm, y_perm, cdt_r, cdt_i, ar, ai]
    seg = jax.ShapeDtypeStruct((N_SEG, N_STATES), f32)
    if store:
        in_specs += [blk(SSM_WIDTH), blk(N_STATES), blk(N_STATES), _full((4, 512, 128)), _full((4, 512, 128)),
                     _full((1, SSM_WIDTH)), _full((N_SEG, N_STATES)), _full((N_SEG, N_STATES))]
        args += list(final)
        out_specs = [blk(SSM_WIDTH), _full((4, 512, 128)), _full((4, 512, 128)), _full((4, 128, 512)),
                     _full((4, 128, 512)), _full((N_SEG, N_STATES)), _full((N_SEG, N_STATES)), _full((1, SSM_WIDTH))]
        b_acc = jax.ShapeDtypeStruct((4, 512, 128), f32)
        c_acc = jax.ShapeDtypeStruct((4, 128, 512), f32)
        out_shape = [jax.ShapeDtypeStruct((S, SSM_WIDTH), bf16), b_acc, b_acc, c_acc, c_acc, seg, seg,
                     jax.ShapeDtypeStruct((1, SSM_WIDTH), f32)]
        scratch = [pltpu.VMEM((R, N_STATES), f32)] * 4 + [pltpu.VMEM((N_SEG, N_STATES), f32)] * 2
        name = "ssm_scan_bwd"
    else:
        out_specs = [_full((N_SEG, N_STATES)), _full((N_SEG, N_STATES))]
        out_shape = [seg, seg]
        scratch = [pltpu.VMEM((R, N_STATES), f32)] * 2 + [pltpu.VMEM((N_SEG, N_STATES), f32)] * 2
        name = "ssm_scan_bwd_carry"
    res, landed = _pcall(body, args, name=name, grid=(nblk,), in_specs=in_specs, out_specs=out_specs,
                         out_shape=out_shape, scratch_shapes=scratch, rides=rides)
    return (res, landed) if rides else res


def ssm_param_bwd(dbb_r, dbb_i, bt_re, bt_im, kr, ki, ar, ai, lr, li, ldt, da_r, da_i):
    def body(dbr_ref, dbi_ref, br_ref, bi_ref, kr_ref, ki_ref, ar_ref, ai_ref, lr_ref, li_ref, ldt_ref, dar_ref,
             dai_ref, obr_ref, obi_ref, olr_ref, oli_ref, odt_ref):
        dbr, dbi, b_r, b_i = dbr_ref[...], dbi_ref[...], br_ref[...], bi_ref[...]
        k_r, k_i, a_r, a_i = kr_ref[...], ki_ref[...], ar_ref[...], ai_ref[...]
        l_r, l_i = lr_ref[...], li_ref[...]
        dt = jnp.exp(ldt_ref[...])
        obr_ref[...] = k_r * dbr + k_i * dbi
        obi_ref[...] = k_r * dbi - k_i * dbr
        gk_r = jnp.sum(dbr * b_r + dbi * b_i, axis=0, keepdims=True)
        gk_i = jnp.sum(dbi * b_r - dbr * b_i, axis=0, keepdims=True)
        ga_r = jnp.sum(dar_ref[...], axis=0, keepdims=True)
        ga_i = jnp.sum(dai_ref[...], axis=0, keepdims=True)
        den = l_r * l_r + l_i * l_i
        c_r, c_i = l_r / den, l_i / den
        m_r, m_i = _cmul(c_r, c_i, gk_r, gk_i)
        g_r, g_i = ga_r + m_r, ga_i + m_i
        t1_r, t1_i = _cmul(dt * a_r, -dt * a_i, g_r, g_i)
        q_r, q_i = _cmul(k_r, k_i, c_r, -c_i)
        t2_r, t2_i = _cmul(-q_r, q_i, gk_r, gk_i)
        olr_ref[...] = t1_r + t2_r
        oli_ref[...] = t1_i + t2_i
        w_r, w_i = _cmul(l_r, l_i, a_r, a_i)
        odt_ref[...] = dt * (w_r * g_r + w_i * g_i)

    v = jax.ShapeDtypeStruct((1, N_STATES), f32)
    m = jax.ShapeDtypeStruct((SSM_GROUP, N_STATES), f32)
    return pl.pallas_call(body, name="ssm_param_bwd", out_shape=[m, m, v, v, v])(
        dbb_r, dbb_i, bt_re, bt_im, kr, ki, ar, ai, lr, li, ldt, da_r, da_i)


def group_sum(v):
    def body(v_ref, o_ref):
        o_ref[...] = jnp.sum(v_ref[...], axis=-1, keepdims=True)
    return pl.pallas_call(body, name="group_sum", out_shape=jax.ShapeDtypeStruct((v.shape[0], 1), f32))(v)


ATTN_SCALE = XATTN_HEAD_DIM ** -0.5


def _attn_probs(q_h, k_h):
    s = lax.dot_general(q_h, k_h, NT_DIMS, preferred_element_type=f32) * ATTN_SCALE
    e = jnp.exp(s - jnp.max(s, axis=-1, keepdims=True))
    return e / jnp.sum(e, axis=-1, keepdims=True)


def attn_fwd(proj, kv, wxo3, tm=TM):
    S = proj.shape[0]
    M = kv.shape[0]
    J, _, n = wxo3.shape

    def body(q_ref, kv_ref, w_ref, o_ref, y_ref):
        for h in range(XATTN_HEADS):
            cols = slice(h * XATTN_HEAD_DIM, (h + 1) * XATTN_HEAD_DIM)
            q_h = q_ref[:, cols].astype(bf16)
            k_h = kv_ref[:, cols]
            v_h = kv_ref[:, XATTN_WIDTH + h * XATTN_HEAD_DIM:XATTN_WIDTH + (h + 1) * XATTN_HEAD_DIM]
            p = _attn_probs(q_h, k_h)
            o_ref[:, cols] = jnp.dot(p.astype(bf16), v_h, preferred_element_type=f32).astype(bf16)
        o = o_ref[...]
        for j in range(J):
            y_ref[:, j * n:(j + 1) * n] = jnp.dot(o, w_ref[j], preferred_element_type=f32).astype(bf16)

    return pl.pallas_call(
        body, name="attn_fwd", grid=(S // tm,),
        in_specs=[_rows(tm, XATTN_WIDTH, 10), _full((M, 2 * XATTN_WIDTH)), _full(wxo3.shape)],
        out_specs=[_rows(tm, XATTN_WIDTH), _rows(tm, J * n)],
        out_shape=[jax.ShapeDtypeStruct((S, XATTN_WIDTH), bf16), jax.ShapeDtypeStruct((S, J * n), bf16)],
        compiler_params=_params())(proj, kv, wxo3)


def attn_bwd(dy_c, o_att, wxo3, proj, kv, dproj, tm=TM):
    S = proj.shape[0]
    M = kv.shape[0]
    J, _, n = wxo3.shape
    nt = S // tm

    def body(dy_ref, o_ref, w_ref, q_ref, kv_ref, _, dq_ref, dkv_ref, dwo_ref, acc_ref):
        i = pl.program_id(0)

        @pl.when(i == 0)
        def _():
            dkv_ref[...] = jnp.zeros_like(dkv_ref)

        part = lax.dot_general(o_ref[...], dy_ref[...], TN_DIMS, preferred_element_type=f32)

        @pl.when(i == 0)
        def _():
            acc_ref[...] = part

        @pl.when(i > 0)
        def _():
            acc_ref[...] += part

        @pl.when(i == nt - 1)
        def _():
            for j in range(J):
                dwo_ref[j] = acc_ref[:, j * n:(j + 1) * n].astype(bf16)

        do = None
        for j in range(J):
            part = lax.dot_general(dy_ref[:, j * n:(j + 1) * n], w_ref[j], NT_DIMS, preferred_element_type=f32)
            do = part if do is None else do + part
        do = do.astype(bf16)
        for h in range(XATTN_HEADS):
            cols = slice(h * XATTN_HEAD_DIM, (h + 1) * XATTN_HEAD_DIM)
            vcols = slice(XATTN_WIDTH + h * XATTN_HEAD_DIM, XATTN_WIDTH + (h + 1) * XATTN_HEAD_DIM)
            q_h = q_ref[:, cols].astype(bf16)
            k_h = kv_ref[:, cols]
            v_h = kv_ref[:, vcols]
            do_h = do[:, cols]
            p = _attn_probs(q_h, k_h)
            dp = lax.dot_general(do_h, v_h, NT_DIMS, preferred_element_type=f32)
            ds = (p * (dp - jnp.sum(dp * p, axis=-1, keepdims=True)) * ATTN_SCALE).astype(bf16)
            dq_ref[:, cols] = jnp.dot(ds, k_h, preferred_element_type=f32).astype(bf16)
            dkv_ref[:, cols] += lax.dot_general(ds, q_h, TN_DIMS, preferred_element_type=f32)
            dkv_ref[:, vcols] += lax.dot_general(p.astype(bf16), do_h, TN_DIMS, preferred_element_type=f32)

    return pl.pallas_call(
        body, name="attn_bwd", grid=(S // tm,),
        in_specs=[_rows(tm, J * n), _rows(tm, XATTN_WIDTH), _full(wxo3.shape), _rows(tm, XATTN_WIDTH, 10),
                  _full((M, 2 * XATTN_WIDTH)), pl.BlockSpec(memory_space=pl.ANY)],
        out_specs=[_rows(tm, XATTN_WIDTH, 10), _full((M, 2 * XATTN_WIDTH)), _full(wxo3.shape)],
        out_shape=[jax.ShapeDtypeStruct(dproj.shape, bf16), jax.ShapeDtypeStruct((M, 2 * XATTN_WIDTH), f32),
                   jax.ShapeDtypeStruct(wxo3.shape, bf16)],
        scratch_shapes=[pltpu.VMEM((XATTN_WIDTH, J * n), f32)], input_output_aliases={5: 0},
        compiler_params=_params())(dy_c, o_att, wxo3, proj, kv, dproj)


def _ln_stats(r):
    mu = jnp.mean(r, axis=-1, keepdims=True)
    xc = r - mu
    var = jnp.mean(xc * xc, axis=-1, keepdims=True)
    rstd = lax.rsqrt(var + LN_EPS)
    return xc * rstd, rstd


def _ln_bwd(dy, xhat, rstd, g):
    dxh = dy * g
    return rstd * (dxh - jnp.mean(dxh, axis=-1, keepdims=True) - xhat * jnp.mean(dxh * xhat, axis=-1, keepdims=True))


def _colsum(v):
    return jnp.sum(v, axis=0, keepdims=True)


def _ln2_loss_epi(t2, h1, target, b_down, g, b):
    xhat, rstd = _ln_stats(ALPHA * h1 + t2 + b_down)
    err = xhat * g + b - target
    dout = err * (1.0 / D_MODEL)
    dr = _ln_bwd(dout, xhat, rstd, g)
    return (dr, dr), (0.5 * _colsum(err * err) * (1.0 / D_MODEL), _colsum(dout * xhat), _colsum(dout), _colsum(dr))


def _merge_fwd_pro(g0, g1, g2, y_a, glu_a, glu_b, y_c, b):
    gs = [_sig(g.astype(f32) + b[:, k * D_MODEL:(k + 1) * D_MODEL]) for k, g in enumerate((g0, g1, g2))]
    y_b = glu_a.astype(f32) * _sig(glu_b.astype(f32))
    return gs[0] * y_a.astype(f32) + gs[1] * y_b + gs[2] * y_c.astype(f32)


def _merge_bwd_epi(dm, g0, g1, g2, y_a, glu_a, glu_b, y_c, b):
    ga, sb = glu_a.astype(f32), _sig(glu_b.astype(f32))
    ys = (y_a.astype(f32), ga * sb, y_c.astype(f32))
    gs = [_sig(g.astype(f32) + b[:, k * D_MODEL:(k + 1) * D_MODEL]) for k, g in enumerate((g0, g1, g2))]
    dpre = [dm * ys[k] * gs[k] * (1.0 - gs[k]) for k in range(3)]
    dyb = dm * gs[1]
    dglu = jnp.concatenate([dyb * sb, dyb * ga * sb * (1.0 - sb)], axis=1)
    return ((jnp.concatenate(dpre, axis=1), dm * gs[0], dglu, dm * gs[2]),
            (jnp.concatenate([_colsum(d) for d in dpre], axis=1),))


def _ln1_fwd_epi(t1, x, g, b):
    xhat, _ = _ln_stats(ALPHA * x + t1)
    h = xhat * g + b
    return (t1, h, h), ()


def _ln1_bwd_epi(t3, x, t1, dr2, g):
    xhat, rstd = _ln_stats(ALPHA * x + t1)
    dh = ALPHA * dr2 + t3
    dr = _ln_bwd(dh, xhat, rstd, g)
    return (dr, dr), (_colsum(dh * xhat), _colsum(dh))


def exchange(*rides, name):
    return _pcall(lambda: None, [], name=name, grid=(), in_specs=[], out_specs=[], out_shape=[], rides=rides)[1]


ROW_TILE = 256


def sum_slots(recv, name):
    n, R, C = recv.shape
    tm = min(R, ROW_TILE)

    def body(r_ref, o_ref):
        acc = r_ref[0].astype(f32)
        for k in range(1, n):
            acc = acc + r_ref[k].astype(f32)
        o_ref[...] = acc

    return pl.pallas_call(
        body, name=name, grid=(R // tm,), in_specs=[pl.BlockSpec((n, tm, C), lambda i: (0, i, 0))],
        out_specs=_rows(tm, C), out_shape=jax.ShapeDtypeStruct((R, C), f32), compiler_params=_params())(recv)


def _adamw(w, g, m, v):
    m = ADAM_B1 * m + (1.0 - ADAM_B1) * g
    v = ADAM_B2 * v + (1.0 - ADAM_B2) * (g * g)
    m_hat = m / (1.0 - ADAM_B1 ** ADAM_STEP)
    v_hat = v / (1.0 - ADAM_B2 ** ADAM_STEP)
    delta = -ADAM_LR * (m_hat / (jnp.sqrt(v_hat) + ADAM_EPS) + ADAM_WD * w)
    return delta, m, v


def adam_pair(p, q, w, m, v, name):
    R, C = w.shape
    tm = min(R, ROW_TILE)

    def body(p_ref, q_ref, w_ref, m_ref, v_ref, g_ref, d_ref, nm_ref, nv_ref):
        g = p_ref[...] + q_ref[...]
        g_ref[...] = g
        d_ref[...], nm_ref[...], nv_ref[...] = _adamw(w_ref[...], g, m_ref[...], v_ref[...])

    o = jax.ShapeDtypeStruct((R, C), f32)
    return pl.pallas_call(body, name=name, grid=(R // tm,), in_specs=[_rows(tm, C)] * 5,
                          out_specs=[_rows(tm, C)] * 4, out_shape=[o] * 4, compiler_params=_params())(p, q, w, m, v)


def adam_slots(recv, w, m, v, name):
    n = recv.shape[0]

    def body(r_ref, w_ref, m_ref, v_ref, g_ref, d_ref, nm_ref, nv_ref):
        g = r_ref[0]
        for k in range(1, n):
            g = g + r_ref[k]
        g_ref[...] = g
        d_ref[...], nm_ref[...], nv_ref[...] = _adamw(w_ref[...], g, m_ref[...], v_ref[...])

    o = jax.ShapeDtypeStruct(w.shape, f32)
    return pl.pallas_call(body, name=name, out_shape=[o] * 4, compiler_params=_params())(recv, w, m, v)


BIG = ("w_in", "w_conv_out", "w_glu", "w_xattn_out", "w_kv", "w_out", "w_up", "w_down")
MID = ("w_conv_out", "w_glu", "w_xattn_out", "w_kv", "w_out")
SMALL = ("b_gate", "ssm_lam_re", "ssm_lam_im", "ssm_log_dt", "ssm_b_re", "ssm_b_im", "ssm_c_re", "ssm_c_im", "ssm_d",
         "ln1_g", "ln1_b", "b_up", "b_down", "ln2_g", "ln2_b")
def _pad_flat(a, mult=1024):
    a = a.reshape(-1)
    return jnp.pad(a, (0, (-a.shape[0]) % mult))


def _perm(a):
    S, W = a.shape
    return a.reshape(N_SEG, S // N_SEG, W).transpose(1, 0, 2).reshape(S, W)


def _unperm(a):
    S, W = a.shape
    return a.reshape(S // N_SEG, N_SEG, W).transpose(1, 0, 2).reshape(S, W)


def _state_rows(a):
    return a.transpose(2, 0, 1).reshape(SSM_GROUP, N_STATES)


def _block_diag_b(bt):
    b4 = bt.reshape(SSM_GROUP, 4, 8, SSM_STATE)
    eye = jnp.eye(8, dtype=bt.dtype)
    return jnp.einsum("hcgp,gk->cghkp", b4, eye).reshape(4, 128, 512)


def _block_diag_c(c):
    c4 = c.reshape(4, 8, SSM_GROUP, SSM_STATE)
    eye = jnp.eye(8, dtype=c.dtype)
    return jnp.einsum("cghp,gk->cgpkh", c4, eye).reshape(4, 512, 128)


def _diag_b(acc):
    a = acc.reshape(4, 8, SSM_STATE, 8, SSM_GROUP)
    eye = jnp.eye(8, dtype=acc.dtype)
    return jnp.einsum("cgpkh,gk->hcgp", a, eye).reshape(SSM_GROUP, N_STATES)


def _diag_c(acc):
    a = acc.reshape(4, 8, SSM_GROUP, 8, SSM_STATE)
    eye = jnp.eye(8, dtype=acc.dtype)
    return jnp.einsum("cghkp,gk->cghp", a, eye).reshape(SSM_GROUPS, SSM_GROUP, SSM_STATE)


def kernel(x, mem, w_in, b_gate, conv_w, w_conv_out, ssm_lam_re, ssm_lam_im, ssm_log_dt, ssm_b_re, ssm_b_im, ssm_c_re, ssm_c_im, ssm_d, w_glu, w_kv, w_xattn_out, w_out, ln1_g, ln1_b, w_up, b_up, w_down, b_down, ln2_g, ln2_b, loss_target, m_w_in, m_b_gate, m_conv_w, m_w_conv_out, m_ssm_lam_re, m_ssm_lam_im, m_ssm_log_dt, m_ssm_b_re, m_ssm_b_im, m_ssm_c_re, m_ssm_c_im, m_ssm_d, m_w_glu, m_w_kv, m_w_xattn_out, m_w_out, m_ln1_g, m_ln1_b, m_w_up, m_b_up, m_w_down, m_b_down, m_ln2_g, m_ln2_b, v_w_in, v_b_gate, v_conv_w, v_w_conv_out, v_ssm_lam_re, v_ssm_lam_im, v_ssm_log_dt, v_ssm_b_re, v_ssm_b_im, v_ssm_c_re, v_ssm_c_im, v_ssm_d, v_w_glu, v_w_kv, v_w_xattn_out, v_w_out, v_ln1_g, v_ln1_b, v_w_up, v_b_up, v_w_down, v_b_down, v_ln2_g, v_ln2_b):
    W = dict(w_in=w_in, b_gate=b_gate, conv_w=conv_w, w_conv_out=w_conv_out, ssm_lam_re=ssm_lam_re,
             ssm_lam_im=ssm_lam_im, ssm_log_dt=ssm_log_dt, ssm_b_re=ssm_b_re, ssm_b_im=ssm_b_im, ssm_c_re=ssm_c_re,
             ssm_c_im=ssm_c_im, ssm_d=ssm_d, w_glu=w_glu, w_kv=w_kv, w_xattn_out=w_xattn_out, w_out=w_out,
             ln1_g=ln1_g, ln1_b=ln1_b, w_up=w_up, b_up=b_up, w_down=w_down, b_down=b_down, ln2_g=ln2_g, ln2_b=ln2_b)
    MOM = dict(w_in=m_w_in, b_gate=m_b_gate, conv_w=m_conv_w, w_conv_out=m_w_conv_out, ssm_lam_re=m_ssm_lam_re,
               ssm_lam_im=m_ssm_lam_im, ssm_log_dt=m_ssm_log_dt, ssm_b_re=m_ssm_b_re, ssm_b_im=m_ssm_b_im,
               ssm_c_re=m_ssm_c_re, ssm_c_im=m_ssm_c_im, ssm_d=m_ssm_d, w_glu=m_w_glu, w_kv=m_w_kv,
               w_xattn_out=m_w_xattn_out, w_out=m_w_out, ln1_g=m_ln1_g, ln1_b=m_ln1_b, w_up=m_w_up, b_up=m_b_up,
               w_down=m_w_down, b_down=m_b_down, ln2_g=m_ln2_g, ln2_b=m_ln2_b)
    VEL = dict(w_in=v_w_in, b_gate=v_b_gate, conv_w=v_conv_w, w_conv_out=v_w_conv_out, ssm_lam_re=v_ssm_lam_re,
               ssm_lam_im=v_ssm_lam_im, ssm_log_dt=v_ssm_log_dt, ssm_b_re=v_ssm_b_re, ssm_b_im=v_ssm_b_im,
               ssm_c_re=v_ssm_c_re, ssm_c_im=v_ssm_c_im, ssm_d=v_ssm_d, w_glu=v_w_glu, w_kv=v_w_kv,
               w_xattn_out=v_w_xattn_out, w_out=v_w_out, ln1_g=v_ln1_g, ln1_b=v_ln1_b, w_up=v_w_up, b_up=v_b_up,
               w_down=v_w_down, b_down=v_b_down, ln2_g=v_ln2_g, ln2_b=v_ln2_b)
    names = list(W)
    xy = 2 * lax.axis_index("x") + lax.axis_index("y")

    xs = x[0]
    S = xs.shape[0]
    mems = mem[0]
    tgt = loss_target[0]

    shard_bf = {n: W[n][0].astype(bf16) for n in BIG}

    lr = ssm_lam_re.reshape(1, N_STATES)
    li = ssm_lam_im.reshape(1, N_STATES)
    ldt = jnp.repeat(ssm_log_dt.reshape(SSM_GROUPS), SSM_STATE).reshape(1, N_STATES)
    bt_re, bt_im = _state_rows(ssm_b_re[0]), _state_rows(ssm_b_im[0])
    ar, ai, kr, ki, bbt_r, bbt_i = ssm_prep(lr, li, ldt, bt_re, bt_im)
    bd_r, bd_i = _block_diag_b(bbt_r).astype(bf16), _block_diag_b(bbt_i).astype(bf16)
    cd_r, cd_i = _block_diag_c(ssm_c_re[0]).astype(bf16), _block_diag_c(ssm_c_im[0]).astype(bf16)
    bdt_r, bdt_i = bd_r.transpose(0, 2, 1), bd_i.transpose(0, 2, 1)
    cdt_r, cdt_i = cd_r.transpose(0, 2, 1), cd_i.transpose(0, 2, 1)
    d_skip = ssm_d.reshape(1, SSM_WIDTH)

    mem_bf = mems.astype(bf16)
    u_col = GATE_COLS + 3 * CONV_WIDTH
    me_xy = jnp.reshape(xy, (1,)).astype(jnp.int32)
    proj, x_bf, (win3,) = proj_own(xs, shard_bf["w_in"], me_xy, 4, rides=(("gather2", shard_bf["w_in"]),))
    proj, (wco3, wglu3, wxo3, wkv4, wout4, convw4) = proj_rest(
        x_bf, win3, me_xy, proj, rides=tuple(("gather", shard_bf[n]) for n in MID) + (("gather", conv_w[0]),))
    wkv3 = wkv4.reshape(1, D_MODEL, 2 * XATTN_WIDTH)
    wout3 = wout4.reshape(1, D_MODEL, D_MODEL)
    convw_full = convw4.transpose(1, 0, 2).reshape(3, CONV_WIDTH)
    a_conv, y_a = conv_fwd(proj, convw_full, wco3)
    u_perm = _perm(proj[:, u_col:u_col + SSM_WIDTH])
    (init_r, init_i), (wup3,) = ssm_scan_fwd(u_perm, bd_r, bd_i, ar, ai, rides=(("gather2", shard_bf["w_up"]),))
    ysm_perm, ys_perm, st_r, st_i = ssm_scan_fwd(u_perm, bd_r, bd_i, ar, ai,
                                                 final=(cd_r, cd_i, d_skip, init_r, init_i))
    y_s = _unperm(ys_perm)
    glu = mm_fwd_small(y_s, wglu3, "glu", out_dtype=bf16)
    kv = mm_fwd_small(mem_bf, wkv3, "kv", out_dtype=bf16)
    o_att, y_c = attn_fwd(proj, kv, wxo3)
    merge_rows = ((proj, D_MODEL, 0), (proj, D_MODEL, 1), (proj, D_MODEL, 2), y_a, (glu, D_MODEL, 0),
                  (glu, D_MODEL, 1), y_c)
    merged, t1, h1, h1_bf = mm_rows_fused(
        (_merge_fwd_pro, merge_rows, (b_gate,)), wout3, "merge_w_out_ln1", transposed=False, tm=256,
        epi=_ln1_fwd_epi, extras=(xs,), consts=(ln1_g, ln1_b), outs=(f32, f32, bf16))
    (r_up, hdn), (wdn4,) = mm_fwd(h1_bf, wup3, "w_up", bias=b_up, outs=(bf16, bf16), second=_relu2,
                                  first=lambda v: jnp.maximum(v, 0.0), rides=(("gather2", shard_bf["w_down"]),))
    wdn3 = wdn4.reshape(1, D_FF, D_MODEL)

    dr2, dr2_bf, loss_cols, d_ln2_g, d_ln2_b, d_b_down = mm_rows_fused(
        hdn, wdn3, "w_down_ln2_loss", transposed=False, tm=512, epi=_ln2_loss_epi, extras=(h1, tgt),
        consts=(b_down, ln2_g, ln2_b), outs=(f32, bf16), nsums=4)
    part, other = {}, {}
    g_w_down = mm_bwd_w(hdn, dr2_bf, 1, "dw_down", tk=1024).reshape(4, -1, D_MODEL)
    (dup, d_b_up), (recv_dn,) = mm_bwd_x(dr2_bf, wdn3, "dup", tm=512, extras=(r_up,), colsum=True, out_dtype=bf16,
                                         epi=lambda acc, r: acc * (2.0 * r.astype(f32)),
                                         rides=(("scatter", g_w_down),))
    part["w_down"] = sum_slots(recv_dn, "sum_w_down")
    g_w_up = mm_bwd_w(h1_bf, dup, 4, "dw_up")
    dr1, dr1_bf, d_ln1_g, d_ln1_b = mm_rows_fused(
        dup, wup3, "dh1_ln1_bwd", transposed=True, tm=512, epi=_ln1_bwd_epi, extras=(xs, t1, dr2), consts=(ln1_g,),
        outs=(f32, bf16), nsums=2)
    g_w_out = mm_bwd_w(merged, dr1_bf, 1, "dw_out").reshape(4, -1, D_MODEL)
    (dproj, dy_a, dglu, dy_c, d_b_gate), (recv_up,) = mm_rows_fused(
        dr1_bf, wout3, "dmerged_merge_bwd", transposed=True, tm=256, epi=_merge_bwd_epi, extras=merge_rows,
        consts=(b_gate,), outs=((bf16, GATE_COLS, IN_COLS), bf16, (bf16, 2 * D_MODEL, 2 * D_MODEL), bf16),
        sum_widths=(GATE_COLS,), rides=(("scatter", g_w_up),))
    part["w_up"] = sum_slots(recv_up, "sum_w_up")

    dproj, d_conv_w, g_w_co = conv_bwd(dy_a, a_conv, wco3, proj, convw_full, dproj)

    g_w_glu = mm_bwd_w_small(y_s, dglu, 4, "dw_glu")
    dys_perm = _perm(mm_bwd_x(dglu, wglu3, "dy_s", tm=1024))
    linit_r, linit_i = ssm_scan_bwd(dys_perm, ysm_perm, cdt_r, cdt_i, ar, ai)
    (du_perm, dbacc_r, dbacc_i, dcacc_r, dcacc_i, da_r, da_i, d_ssm_d), (recv_co, recv_glu, recv_out) = ssm_scan_bwd(
        dys_perm, ysm_perm, cdt_r, cdt_i, ar, ai,
        final=(u_perm, st_r, st_i, bdt_r, bdt_i, d_skip, linit_r, linit_i),
        rides=(("scatter", g_w_co), ("scatter", g_w_glu), ("scatter", g_w_out)))
    part["w_out"] = sum_slots(recv_out, "sum_w_out")
    part["w_conv_out"] = sum_slots(recv_co, "sum_w_conv_out")
    part["w_glu"] = sum_slots(recv_glu, "sum_w_glu")
    dbt_re, dbt_im, d_lr, d_li, d_ldt_state = ssm_param_bwd(
        _diag_b(dbacc_r), _diag_b(dbacc_i), bt_re, bt_im, kr, ki, ar, ai, lr, li, ldt, da_r, da_i)
    d_log_dt = group_sum(d_ldt_state.reshape(SSM_GROUPS, SSM_STATE))
    d_b_re = dbt_re.reshape(SSM_GROUP, SSM_GROUPS, SSM_STATE).transpose(1, 2, 0)
    d_b_im = dbt_im.reshape(SSM_GROUP, SSM_GROUPS, SSM_STATE).transpose(1, 2, 0)
    d_c_re = _diag_c(dcacc_r)
    d_c_im = -_diag_c(dcacc_i)

    dproj, dkv, g_w_xo = attn_bwd(dy_c, o_att, wxo3, proj, kv, dproj)
    g_w_kv = mm_bwd_w(mem_bf, dkv, 1, "dw_kv").reshape(4, -1, D_MODEL)

    dproj = lax.dynamic_update_slice(dproj, _unperm(du_perm), (0, u_col))
    g_small = {"b_gate": d_b_gate, "ssm_lam_re": d_lr, "ssm_lam_im": d_li, "ssm_log_dt": d_log_dt, "ssm_b_re": d_b_re,
               "ssm_b_im": d_b_im, "ssm_c_re": d_c_re, "ssm_c_im": d_c_im, "ssm_d": d_ssm_d, "ln1_g": d_ln1_g,
               "ln1_b": d_ln1_b, "b_up": d_b_up, "b_down": d_b_down, "ln2_g": d_ln2_g, "ln2_b": d_ln2_b}
    small_names = SMALL + ("conv_w",)
    g_small["conv_w"] = d_conv_w
    sizes = {n: (g_small[n].size + 1023) // 1024 * 1024 for n in small_names}
    pack = lambda d: jnp.concatenate([_pad_flat(d[n]) for n in small_names]).reshape(-1, 128)
    early = ("w_down", "w_up", "w_out", "w_conv_out", "w_glu")
    g_w_in, landed = mm_bwd_w(
        x_bf, dproj, 4, "dw_in", rides=(("scatter", g_w_xo), ("scatter", g_w_kv), ("all", pack(g_small)))
        + tuple(("pair", part[n]) for n in early))
    recv_xo, recv_kv, srecv = landed[:3]
    other.update(zip(early, landed[3:]))
    part["w_xattn_out"] = sum_slots(recv_xo, "sum_w_xattn_out")
    part["w_kv"] = sum_slots(recv_kv, "sum_w_kv")
    dx, (recv_in, other["w_xattn_out"], other["w_kv"]) = mm_bwd_x(
        dproj, win3, "dx", extras=(dr1,), epi=lambda acc, d: acc + ALPHA * d,
        rides=(("scatter", g_w_in), ("pair", part["w_xattn_out"]), ("pair", part["w_kv"])))
    part["w_in"] = sum_slots(recv_in, "sum_w_in")
    other["w_in"], = exchange(("pair", part["w_in"]), name="swap_w_in")
    res = [{}, {}, {}, {}]
    for n in BIG:
        for k, r in enumerate(adam_pair(part[n], other[n], W[n][0], MOM[n][0], VEL[n][0], "adam_" + n)):
            res[k][n] = r[None]
    conv_zero = jnp.zeros((3, CONV_WIDTH), f32)
    gs, ds_, ms, vs = adam_slots(srecv, pack({**{n: W[n] for n in SMALL}, "conv_w": conv_zero}),
                                 pack({**{n: MOM[n] for n in SMALL}, "conv_w": conv_zero}),
                                 pack({**{n: VEL[n] for n in SMALL}, "conv_w": conv_zero}), "adam_small")

    def unpack_small(buf):
        flat = buf.reshape(-1)
        out, r = {}, 0
        for n in small_names:
            ref = g_small[n] if n == "conv_w" else W[n]
            out[n] = flat[r:r + ref.size].reshape(ref.shape)
            r += sizes[n]
        return out

    res_s = [unpack_small(b) for b in (gs, ds_, ms, vs)]
    g_conv = lax.dynamic_slice(res_s[0]["conv_w"], (0, xy * 128), (3, 128))
    conv_slots = g_conv.reshape(1, 3, 128)
    cg, cd, cm, cv = adam_slots(conv_slots, conv_w[0], m_conv_w[0], v_conv_w[0], "adam_conv")
    conv_res = [cg, cd, cm, cv]

    loss = lax.psum(jnp.sum(loss_cols), ("x", "y", "c"))
    outs = [loss, dx.reshape(x.shape)]
    for k in range(4):
        for n in names:
            if n == "conv_w":
                outs.append(conv_res[k].reshape(conv_w.shape))
            elif n in BIG:
                outs.append(res[k][n])
            else:
                outs.append(res_s[k][n])
    return tuple(outs)
```

```python
import functools
import math

import jax
import jax.numpy as jnp
from jax import lax
from jax.experimental import pallas as pl
from jax.experimental.pallas import tpu as pltpu

f32 = jnp.float32
bf16 = jnp.bfloat16

D_MODEL = 1024
CONV_WIDTH = 512
SSM_WIDTH = 512
SSM_GROUP = 16
SSM_GROUPS = 32
SSM_STATE = 64
N_STATES = SSM_GROUPS * SSM_STATE
XATTN_HEADS = 4
XATTN_HEAD_DIM = 128
XATTN_WIDTH = 512
D_FF = 4096
GATE_COLS = 3 * D_MODEL
IN_COLS = GATE_COLS + 3 * CONV_WIDTH + SSM_WIDTH + XATTN_WIDTH
ALPHA = 2.0 ** 0.25
LN_EPS = 1e-5
ADAM_LR = 0.001
ADAM_B1 = 0.9
ADAM_B2 = 0.999
ADAM_EPS = 1e-08
ADAM_WD = 0.01
ADAM_STEP = 10

N_SEG = 8
SCAN_ROWS = 512
LANE_STRIP = 512
TM = 512
VMEM_LIMIT = 48 * 1024 * 1024
MESH = pl.DeviceIdType.MESH

NT_DIMS = (((1,), (1,)), ((), ()))
TN_DIMS = (((0,), (0,)), ((), ()))


def _params():
    return pltpu.CompilerParams(vmem_limit_bytes=VMEM_LIMIT)


def _full(shape):
    n = len(shape)
    return pl.BlockSpec(shape, lambda *_: (0,) * n)


def _rows(tm, w, cb=0):
    return pl.BlockSpec((tm, w), lambda i: (i, cb))


def _sig(x):
    return 1.0 / (1.0 + jnp.exp(-x))


HBM_SPEC = pl.BlockSpec(memory_space=pl.ANY)
RIDE_PEERS = {"gather": 3, "gather2": 6, "scatter": 3, "all": 7, "pair": 1}


def _xy_peers(x, y):
    return [(1 - x, y), (x, 1 - y), (1 - x, 1 - y)]


def _ride_copies(kind, src, dst, send_sems, recv_sems, local_sem):
    x, y, c = lax.axis_index("x"), lax.axis_index("y"), lax.axis_index("c")
    me = 2 * x + y
    if kind == "all":
        flips = [(fx, fy, fc) for fx in (0, 1) for fy in (0, 1) for fc in (0, 1)][1:]
        peers = [(x ^ fx, y ^ fy, c ^ fc) for fx, fy, fc in flips]
        slot = lambda p: 4 * p[0] + 2 * p[1] + p[2]
        mine = slot((x, y, c))
    else:
        peers = [(px, py, c) for px, py in _xy_peers(x, y)]
        slot = lambda p: 2 * p[0] + p[1]
        mine = me

    def remote(k, s, d):
        return pltpu.make_async_remote_copy(src_ref=s, dst_ref=d, send_sem=send_sems.at[k], recv_sem=recv_sems.at[k],
                                            device_id=peers[k], device_id_type=MESH)

    if kind == "pair":
        peers = [(x, y, 1 - c)]
        return None, [remote(0, src, dst)], [remote(0, src, dst)]
    if kind == "scatter":
        local = pltpu.make_async_copy(src.at[me], dst.at[me], local_sem)
        sends = [remote(k, src.at[slot(p)], dst.at[me]) for k, p in enumerate(peers)]
        lands = [remote(k, src.at[me], dst.at[slot(p)]) for k, p in enumerate(peers)]
    else:
        local = pltpu.make_async_copy(src, dst.at[mine], local_sem)
        sends = [remote(k, src, dst.at[mine]) for k, p in enumerate(peers)]
        lands = [remote(k, src, dst.at[slot(p)]) for k, p in enumerate(peers)]
    return local, sends, lands


def _ride_shape(kind, src):
    lead = {"gather": (4,), "gather2": (4,), "scatter": (), "all": (8,), "pair": ()}[kind]
    return jax.ShapeDtypeStruct(lead + src.shape, src.dtype)


def _two_level(src, dst, send_sems, recv_sems, local_sem):
    x, y, c = lax.axis_index("x"), lax.axis_index("y"), lax.axis_index("c")
    me = 2 * x + y
    h = src.shape[0] // 2
    mine = pl.ds(pl.multiple_of(c * h, 16), h)
    theirs = pl.ds(pl.multiple_of((1 - c) * h, 16), h)
    peers = _xy_peers(x, y)
    slots = [2 * px + py for px, py in peers]

    def over_ici(k, d):
        return pltpu.make_async_remote_copy(src_ref=src.at[mine], dst_ref=d, send_sem=send_sems.at[k],
                                            recv_sem=recv_sems.at[k], device_id=(*peers[k], c), device_id_type=MESH)

    def over_d2d(k, rows):
        blk = dst.at[slots[k], rows]
        return pltpu.make_async_remote_copy(src_ref=blk, dst_ref=blk, send_sem=send_sems.at[3 + k],
                                            recv_sem=recv_sems.at[3 + k], device_id=(x, y, 1 - c), device_id_type=MESH)

    return (pltpu.make_async_copy(src, dst.at[me], local_sem),
            [over_ici(k, dst.at[me, mine]) for k in range(3)], [over_ici(k, dst.at[slots[k], mine]) for k in range(3)],
            [over_d2d(k, mine) for k in range(3)], [over_d2d(k, theirs) for k in range(3)])


def _pcall(body, args, *, name, grid, in_specs, out_specs, out_shape, scratch_shapes=(), rides=(), aliases=None,
           prefetch=()):
    n_in, n_out, n_scr, nr, npf = len(in_specs), len(out_specs), len(scratch_shapes), len(rides), len(prefetch)
    kinds = [k for k, _ in rides]

    def wrapped(*refs):
        pre, refs = refs[:npf], refs[npf:]
        ins, rsrc = refs[:n_in], refs[n_in:n_in + nr]
        outs = refs[n_in + nr:n_in + nr + n_out]
        rdst = refs[n_in + nr + n_out:n_in + 2 * nr + n_out]
        scr = refs[n_in + 2 * nr + n_out:n_in + 2 * nr + n_out + n_scr]
        sems = refs[n_in + 2 * nr + n_out + n_scr:]

        def start():
            for r, kind in enumerate(kinds):
                if kind == "gather2":
                    local, sends, _, _, _ = _two_level(rsrc[r], rdst[r], *sems[3 * r:3 * r + 3])
                else:
                    local, sends, _ = _ride_copies(kind, rsrc[r], rdst[r], *sems[3 * r:3 * r + 3])
                if local is not None:
                    local.start()
                for cp in sends:
                    cp.start()

        def finish():
            for r, kind in enumerate(kinds):
                if kind == "gather2":
                    local, sends, ici_lands, forwards, lands = _two_level(rsrc[r], rdst[r], *sems[3 * r:3 * r + 3])
                    for k in range(3):
                        ici_lands[k].wait_recv()
                        forwards[k].start()
                    sends = sends + forwards
                else:
                    local, sends, lands = _ride_copies(kind, rsrc[r], rdst[r], *sems[3 * r:3 * r + 3])
                for cp in lands:
                    cp.wait_recv()
                for cp in sends:
                    cp.wait_send()
                if local is not None:
                    local.wait()

        if nr and grid:
            ids = [pl.program_id(a) for a in range(len(grid))]
            first = functools.reduce(jnp.logical_and, [i == 0 for i in ids])
            last = functools.reduce(jnp.logical_and, [i == g - 1 for i, g in zip(ids, grid)])
            pl.when(first)(start)
        elif nr:
            start()
        body(*pre, *ins, *outs, *scr)
        if nr and grid:
            pl.when(last)(finish)
        elif nr:
            finish()

    sems = []
    for kind in kinds:
        n = RIDE_PEERS[kind]
        sems += [pltpu.SemaphoreType.DMA((n,)), pltpu.SemaphoreType.DMA((n,)), pltpu.SemaphoreType.DMA(())]
    all_in = list(in_specs) + [HBM_SPEC] * nr
    all_out = list(out_specs) + [HBM_SPEC] * nr
    all_scratch = list(scratch_shapes) + sems
    if npf:
        how = dict(grid_spec=pltpu.PrefetchScalarGridSpec(num_scalar_prefetch=npf, grid=grid, in_specs=all_in,
                                                          out_specs=all_out, scratch_shapes=all_scratch))
    else:
        how = dict(grid=grid, in_specs=all_in, out_specs=all_out, scratch_shapes=all_scratch)
    res = pl.pallas_call(
        wrapped, name=name, out_shape=list(out_shape) + [_ride_shape(k, s) for k, s in rides],
        input_output_aliases=aliases or {}, compiler_params=_params(), **how)(
            *prefetch, *args, *[s for _, s in rides])
    return list(res[:n_out]), list(res[n_out:])


def _bf(v):
    return v if v.dtype == bf16 else v.astype(bf16)


def mm_fwd_small(a, w3, name, *, tm=1024, out_dtype=f32):
    M, K = a.shape
    J, _, n = w3.shape
    tm = min(tm, M)

    def body(a_ref, w_ref, o_ref):
        av = _bf(a_ref[...])
        for j in range(J):
            o_ref[:, j * n:(j + 1) * n] = jnp.dot(av, w_ref[j], preferred_element_type=f32).astype(out_dtype)

    return pl.pallas_call(
        body, name=name, grid=(M // tm,), in_specs=[_rows(tm, K), _full((J, K, n))], out_specs=_rows(tm, J * n),
        out_shape=jax.ShapeDtypeStruct((M, J * n), out_dtype), compiler_params=_params())(a, w3)


def mm_bwd_w_small(a, dy, J, name, *, tm=2048):
    M, K = a.shape
    n = dy.shape[1] // J
    tm = min(tm, M)
    ns = M // tm

    def body(a_ref, dy_ref, o_ref, acc_ref):
        s = pl.program_id(0)
        part = lax.dot_general(_bf(a_ref[...]), _bf(dy_ref[...]), TN_DIMS, preferred_element_type=f32)

        @pl.when(s == 0)
        def _():
            acc_ref[...] = part

        @pl.when(s > 0)
        def _():
            acc_ref[...] += part

        @pl.when(s == ns - 1)
        def _():
            for j in range(J):
                o_ref[j] = acc_ref[:, j * n:(j + 1) * n].astype(bf16)

    return pl.pallas_call(
        body, name=name, grid=(ns,), in_specs=[_rows(tm, K), _rows(tm, J * n)], out_specs=_full((J, K, n)),
        out_shape=jax.ShapeDtypeStruct((J, K, n), bf16), scratch_shapes=[pltpu.VMEM((K, J * n), f32)],
        compiler_params=_params())(a, dy)


def mm_fwd(a, w3, name, *, tm=1024, tn=None, bias=None, outs=(f32,), first=None, second=None, emit_a=False,
           rides=()):
    M, K = a.shape
    J, _, n = w3.shape
    tm = min(tm, M)
    tn = tn or n
    nl = n // tn
    nb = 0 if bias is None else 1

    def body(*refs):
        a_ref, w_ref = refs[0], refs[1]
        av = _bf(a_ref[...])
        acc = jnp.dot(av, w_ref[0], preferred_element_type=f32)
        if bias is not None:
            acc = acc + refs[2][...]
        refs[2 + nb][...] = (acc if first is None else first(acc)).astype(outs[0])
        if len(outs) > 1:
            refs[3 + nb][...] = second(acc).astype(outs[1])
        if emit_a:
            @pl.when(jnp.logical_and(pl.program_id(0) == 0, pl.program_id(1) == 0))
            def _():
                refs[2 + nb + len(outs)][...] = av

    in_specs = [pl.BlockSpec((tm, K), lambda j, l, i: (i, 0)),
                pl.BlockSpec((1, K, tn), lambda j, l, i: (j, 0, l))]
    args = [a, w3]
    if bias is not None:
        in_specs.append(pl.BlockSpec((1, tn), lambda j, l, i: (0, j * nl + l)))
        args.append(bias)
    out_specs = [pl.BlockSpec((tm, tn), lambda j, l, i: (i, j * nl + l)) for _ in outs]
    out_shape = [jax.ShapeDtypeStruct((M, J * n), dt) for dt in outs]
    if emit_a:
        last = M // tm - 1
        out_specs.append(pl.BlockSpec((tm, K), lambda j, l, i: (jnp.where((j == 0) & (l == 0), i, last), 0)))
        out_shape.append(jax.ShapeDtypeStruct((M, K), bf16))
    res, landed = _pcall(body, args, name=name, grid=(J, nl, M // tm), in_specs=in_specs, out_specs=out_specs,
                         out_shape=out_shape, rides=rides)
    res = res if len(res) > 1 else res[0]
    return (res, landed) if rides else res


def proj_own(x, w_own, me, n_blocks, *, tm=1024, rides=()):
    M, K = x.shape
    n = w_own.shape[1]

    def body(me_ref, x_ref, w_ref, o_ref, xb_ref):
        av = _bf(x_ref[...])
        xb_ref[...] = av
        o_ref[...] = jnp.dot(av, w_ref[...], preferred_element_type=f32).astype(bf16)

    (proj, x_bf), landed = _pcall(
        body, [x, w_own], name="proj_own", grid=(M // tm,), prefetch=(me,),
        in_specs=[pl.BlockSpec((tm, K), lambda i, me_ref: (i, 0)), pl.BlockSpec((K, n), lambda i, me_ref: (0, 0))],
        out_specs=[pl.BlockSpec((tm, n), lambda i, me_ref: (i, me_ref[0])),
                   pl.BlockSpec((tm, K), lambda i, me_ref: (i, 0))],
        out_shape=[jax.ShapeDtypeStruct((M, n_blocks * n), bf16), jax.ShapeDtypeStruct((M, K), bf16)], rides=rides)
    return proj, x_bf, landed


def proj_rest(x_bf, w3, me, proj, *, tm=1024, rides=()):
    M, K = x_bf.shape
    J, _, n = w3.shape

    def body(me_ref, x_ref, w_ref, _, o_ref):
        o_ref[...] = jnp.dot(x_ref[...], w_ref[0], preferred_element_type=f32).astype(bf16)

    (proj,), landed = _pcall(
        body, [x_bf, w3, proj], name="proj_rest", grid=(J - 1, M // tm), prefetch=(me,),
        in_specs=[pl.BlockSpec((tm, K), lambda k, i, me_ref: (i, 0)),
                  pl.BlockSpec((1, K, n), lambda k, i, me_ref: (me_ref[0] ^ (k + 1), 0, 0)), HBM_SPEC],
        out_specs=[pl.BlockSpec((tm, n), lambda k, i, me_ref: (i, me_ref[0] ^ (k + 1)))],
        out_shape=[jax.ShapeDtypeStruct(proj.shape, bf16)], aliases={3: 0}, rides=rides)
    return proj, landed


def mm_bwd_x(dy, w3, name, *, tm=TM, epi=None, extras=(), colsum=False, out_dtype=f32, rides=()):
    M = dy.shape[0]
    J, K, n = w3.shape
    tm = min(tm, M)
    nex = len(extras)

    def body(*refs):
        dy_ref, w_hbm = refs[0], refs[1]
        ex = refs[2:2 + nex]
        o_ref = refs[2 + nex]
        w_ref = refs[-1]
        i = pl.program_id(0)

        @pl.when(i == 0)
        def _():
            pltpu.sync_copy(w_hbm, w_ref)

        acc = None
        for j in range(J):
            part = lax.dot_general(_bf(dy_ref[:, j * n:(j + 1) * n]), w_ref[j], NT_DIMS, preferred_element_type=f32)
            acc = part if acc is None else acc + part
        if epi is not None:
            acc = epi(acc, *[e[...] for e in ex])
        o_ref[...] = acc.astype(out_dtype)
        if colsum:
            s_ref = refs[3 + nex]
            cs = jnp.sum(acc, axis=0, keepdims=True)

            @pl.when(i == 0)
            def _():
                s_ref[...] = cs

            @pl.when(i > 0)
            def _():
                s_ref[...] += cs

    in_specs = [pl.BlockSpec((tm, J * n), lambda i: (i, 0)), pl.BlockSpec(memory_space=pl.ANY)]
    in_specs += [pl.BlockSpec((tm, K), lambda i: (i, 0)) for _ in extras]
    out_specs = [pl.BlockSpec((tm, K), lambda i: (i, 0))]
    out_shape = [jax.ShapeDtypeStruct((M, K), out_dtype)]
    if colsum:
        out_specs.append(pl.BlockSpec((1, K), lambda i: (0, 0)))
        out_shape.append(jax.ShapeDtypeStruct((1, K), f32))
    res, landed = _pcall(body, [dy, w3, *extras], name=name, grid=(M // tm,), in_specs=in_specs, out_specs=out_specs,
                         out_shape=out_shape, scratch_shapes=[pltpu.VMEM((J, K, n), bf16)], rides=rides)
    res = res if colsum else res[0]
    return (res, landed) if rides else res


def mm_rows_fused(a, w3, name, *, transposed, tm, epi, extras=(), consts=(), outs=(f32,), nsums=0, sum_widths=None,
                  rides=()):
    pro = a[0] if isinstance(a, tuple) else None
    J, K, n = w3.shape
    W = K if transposed else n
    ka = J * n if transposed else K
    if pro is not None:
        a_rows = [e if isinstance(e, tuple) else (e, ka, 0) for e in a[1]]
        a_consts = list(a[2])
    else:
        a_rows, a_consts = [(a, ka, 0)], []
    M = a_rows[0][0].shape[0]
    tm = min(tm, M)
    na = len(a_rows) + len(a_consts)
    nex, nco = len(extras), len(consts)
    if pro is not None:
        outs = (bf16 if ka == W else (bf16, ka, ka),) + tuple(outs)
    nout = len(outs)
    nsums = len(sum_widths) if sum_widths is not None else nsums
    out_dtypes = [o[0] if isinstance(o, tuple) else o for o in outs]

    def body(*refs):
        a_refs, w_hbm = refs[:na], refs[na]
        ex = refs[na + 1:na + 1 + nex]
        co = refs[na + 1 + nex:na + 1 + nex + nco]
        o_refs = refs[na + 1 + nex + nco:na + 1 + nex + nco + nout]
        s_refs = refs[na + 1 + nex + nco + nout:na + 1 + nex + nco + nout + nsums]
        w_ref = refs[-1]
        i = pl.program_id(0)

        @pl.when(i == 0)
        def _():
            pltpu.sync_copy(w_hbm, w_ref)

        av = _bf(a_refs[0][...] if pro is None else pro(*[r[...] for r in a_refs]))
        if transposed:
            acc = None
            for j in range(J):
                part = lax.dot_general(av[:, j * n:(j + 1) * n], w_ref[j], NT_DIMS, preferred_element_type=f32)
                acc = part if acc is None else acc + part
        else:
            acc = jnp.dot(av, w_ref[0], preferred_element_type=f32)
        rows, sums = epi(acc, *[e[...] for e in ex], *[c[...] for c in co])
        if pro is not None:
            rows = (av,) + tuple(rows)
        for o_ref, v, dt in zip(o_refs, rows, out_dtypes):
            o_ref[...] = v.astype(dt)
        for s_ref, v in zip(s_refs, sums):
            @pl.when(i == 0)
            def _(s_ref=s_ref, v=v):
                s_ref[...] = v

            @pl.when(i > 0)
            def _(s_ref=s_ref, v=v):
                s_ref[...] += v

    extras = [e if isinstance(e, tuple) else (e, W, 0) for e in extras]
    outs_full = [o if isinstance(o, tuple) else (o, W, W) for o in outs]
    widths = list(sum_widths) if sum_widths is not None else [W] * nsums
    in_specs = [_rows(tm, w, cb) for _, w, cb in a_rows] + [_full(c.shape) for c in a_consts]
    in_specs += [pl.BlockSpec(memory_space=pl.ANY)]
    in_specs += [_rows(tm, w, cb) for _, w, cb in extras] + [_full(c.shape) for c in consts]
    out_specs = [_rows(tm, w) for _, w, _ in outs_full] + [_full((1, w)) for w in widths]
    out_shape = ([jax.ShapeDtypeStruct((M, cols), dt) for dt, _, cols in outs_full]
                 + [jax.ShapeDtypeStruct((1, w), f32) for w in widths])
    args = [e for e, _, _ in a_rows] + a_consts + [w3] + [e for e, _, _ in extras] + list(consts)
    res, landed = _pcall(body, args, name=name, grid=(M // tm,), in_specs=in_specs, out_specs=out_specs,
                         out_shape=out_shape, scratch_shapes=[pltpu.VMEM((J, K, n), bf16)], rides=rides)
    return (res, landed) if rides else res


def mm_bwd_w(a, dy, J, name, *, tm=2048, tn=None, tk=None, rides=()):
    M, K = a.shape
    n = dy.shape[1] // J
    tm = min(tm, M)
    tn = tn or n
    tk = tk or K
    nl = n // tn
    nk = K // tk
    ns = M // tm

    def body(a_ref, dy_ref, o_ref, acc_ref):
        s = pl.program_id(3)
        part = lax.dot_general(_bf(a_ref[...]), _bf(dy_ref[...]), TN_DIMS, preferred_element_type=f32)

        @pl.when(s == 0)
        def _():
            acc_ref[...] = part

        @pl.when(s > 0)
        def _():
            acc_ref[...] += part

        @pl.when(s == ns - 1)
        def _():
            o_ref[0] = acc_ref[...].astype(bf16)

    res, landed = _pcall(
        body, [a, dy], name=name, grid=(J, nl, nk, ns),
        in_specs=[pl.BlockSpec((tm, tk), lambda j, l, k, s: (s, k)),
                  pl.BlockSpec((tm, tn), lambda j, l, k, s: (s, j * nl + l))],
        out_specs=[pl.BlockSpec((1, tk, tn), lambda j, l, k, s: (j, k, l))],
        out_shape=[jax.ShapeDtypeStruct((J, K, n), bf16)],
        scratch_shapes=[pltpu.VMEM((tk, tn), f32)], rides=rides)
    return (res[0], landed) if rides else res[0]


def _relu2(v):
    r = jnp.maximum(v, 0.0)
    return r * r


HALO = 16


def _shift_down(z, k, halo):
    r = lax.broadcasted_iota(jnp.int32, z.shape, 0)
    y = pltpu.roll(z, k, 0)
    for q in range(k):
        y = jnp.where(r == q, halo[HALO - k + q:HALO - k + q + 1, :], y)
    return y


def _shift_up(z, k, halo):
    tm = z.shape[0]
    r = lax.broadcasted_iota(jnp.int32, z.shape, 0)
    y = pltpu.roll(z, tm - k, 0)
    for q in range(k):
        y = jnp.where(r == tm - k + q, halo[q:q + 1, :], y)
    return y


def _prev_halo(tm, cb):
    return pl.BlockSpec((HALO, CONV_WIDTH), lambda i: (jnp.maximum(i * (tm // HALO) - 1, 0), cb))


def _next_halo(tm, cb, nblk):
    return pl.BlockSpec((HALO, CONV_WIDTH), lambda i: (jnp.minimum((i + 1) * (tm // HALO), nblk - 1), cb))


def _f32(ref):
    return ref[...].astype(f32)


def conv_fwd(proj, conv_w, wco3, tm=TM):
    S = proj.shape[0]
    J, _, n = wco3.shape

    def body(cb_ref, cc_ref, ch_ref, cch_ref, chh_ref, w_ref, wo_ref, a_ref, y_ref):
        i = pl.program_id(0)
        z = _f32(cc_ref) * _f32(ch_ref)
        zh = jnp.where(i == 0, 0.0, _f32(cch_ref) * _f32(chh_ref))
        w = w_ref[...]
        dwz = w[0:1, :] * _shift_down(z, 2, zh) + w[1:2, :] * _shift_down(z, 1, zh) + w[2:3, :] * z
        a = (_f32(cb_ref) * dwz).astype(bf16)
        a_ref[...] = a
        for j in range(J):
            y_ref[:, j * n:(j + 1) * n] = jnp.dot(a, wo_ref[j], preferred_element_type=f32).astype(bf16)

    return pl.pallas_call(
        body, name="conv_fwd", grid=(S // tm,),
        in_specs=[_rows(tm, CONV_WIDTH, 6), _rows(tm, CONV_WIDTH, 7), _rows(tm, CONV_WIDTH, 8),
                  _prev_halo(tm, 7), _prev_halo(tm, 8), _full((3, CONV_WIDTH)), _full(wco3.shape)],
        out_specs=[_rows(tm, CONV_WIDTH), _rows(tm, J * n)],
        out_shape=[jax.ShapeDtypeStruct((S, CONV_WIDTH), bf16), jax.ShapeDtypeStruct((S, J * n), bf16)],
        compiler_params=_params())(proj, proj, proj, proj, proj, conv_w, wco3)


def conv_bwd(dy_a, a_conv, wco3, proj, conv_w, dproj, tm=TM):
    S = proj.shape[0]
    nt = S // tm
    J, _, n = wco3.shape

    def body(dy_ref, dyn_ref, a_ref, wo_ref, cb_ref, cc_ref, ch_ref, cch_ref, chh_ref, cbn_ref, w_ref, _, o_ref,
             dw_ref, dwo_ref, acc_ref):
        i = pl.program_id(0)

        def times_wo_t(dy):
            out = None
            for j in range(J):
                part = lax.dot_general(dy[:, j * n:(j + 1) * n], wo_ref[j], NT_DIMS, preferred_element_type=f32)
                out = part if out is None else out + part
            return out

        dyv = dy_ref[...]
        dav, dan = times_wo_t(dyv), times_wo_t(dyn_ref[...])
        part = lax.dot_general(a_ref[...], dyv, TN_DIMS, preferred_element_type=f32)

        @pl.when(i == 0)
        def _():
            acc_ref[...] = part

        @pl.when(i > 0)
        def _():
            acc_ref[...] += part

        @pl.when(i == nt - 1)
        def _():
            for j in range(J):
                dwo_ref[j] = acc_ref[:, j * n:(j + 1) * n].astype(bf16)

        cc, ch, cb = _f32(cc_ref), _f32(ch_ref), _f32(cb_ref)
        z = cc * ch
        zh = jnp.where(i == 0, 0.0, _f32(cch_ref) * _f32(chh_ref))
        w = w_ref[...]
        z1 = _shift_down(z, 1, zh)
        z2 = _shift_down(z, 2, zh)
        dwz = w[0:1, :] * z2 + w[1:2, :] * z1 + w[2:3, :] * z
        g = dav * cb
        gn = jnp.where(i == nt - 1, 0.0, dan * _f32(cbn_ref))
        dz = w[2:3, :] * g + w[1:2, :] * _shift_up(g, 1, gn) + w[0:1, :] * _shift_up(g, 2, gn)
        o_ref[:, 0:CONV_WIDTH] = (dav * dwz).astype(bf16)
        o_ref[:, CONV_WIDTH:2 * CONV_WIDTH] = (dz * ch).astype(bf16)
        o_ref[:, 2 * CONV_WIDTH:3 * CONV_WIDTH] = (dz * cc).astype(bf16)

        @pl.when(i == 0)
        def _():
            dw_ref[...] = jnp.zeros_like(dw_ref)

        dw_ref[0:1, :] += jnp.sum(g * z2, axis=0, keepdims=True)
        dw_ref[1:2, :] += jnp.sum(g * z1, axis=0, keepdims=True)
        dw_ref[2:3, :] += jnp.sum(g * z, axis=0, keepdims=True)

    return pl.pallas_call(
        body, name="conv_bwd", grid=(nt,),
        in_specs=[_rows(tm, J * n),
                  pl.BlockSpec((HALO, J * n), lambda i: (jnp.minimum((i + 1) * (tm // HALO), S // HALO - 1), 0)),
                  _rows(tm, CONV_WIDTH), _full(wco3.shape),
                  _rows(tm, CONV_WIDTH, 6), _rows(tm, CONV_WIDTH, 7), _rows(tm, CONV_WIDTH, 8),
                  _prev_halo(tm, 7), _prev_halo(tm, 8), _next_halo(tm, 6, S // HALO), _full((3, CONV_WIDTH)),
                  pl.BlockSpec(memory_space=pl.ANY)],
        out_specs=[_rows(tm, 3 * CONV_WIDTH, GATE_COLS // (3 * CONV_WIDTH)), _full((3, CONV_WIDTH)),
                   _full(wco3.shape)],
        out_shape=[jax.ShapeDtypeStruct(dproj.shape, bf16), jax.ShapeDtypeStruct((3, CONV_WIDTH), f32),
                   jax.ShapeDtypeStruct(wco3.shape, bf16)],
        scratch_shapes=[pltpu.VMEM((CONV_WIDTH, J * n), f32)], input_output_aliases={11: 0},
        compiler_params=_params())(dy_a, dy_a, a_conv, wco3, proj, proj, proj, proj, proj, proj, conv_w, dproj)


def _cmul(ar, ai, br, bi):
    return ar * br - ai * bi, ar * bi + ai * br


def ssm_prep(lr, li, ldt, bt_re, bt_im):
    def body(lr_ref, li_ref, ldt_ref, br_ref, bi_ref, ar_ref, ai_ref, kr_ref, ki_ref, bbr_ref, bbi_ref):
        lrv, liv = lr_ref[...], li_ref[...]
        dt = jnp.exp(ldt_ref[...])
        mag = jnp.exp(lrv * dt)
        ar = mag * jnp.cos(liv * dt)
        ai = mag * jnp.sin(liv * dt)
        den = lrv * lrv + liv * liv
        nr = ar - 1.0
        kr = (nr * lrv + ai * liv) / den
        ki = (ai * lrv - nr * liv) / den
        ar_ref[...], ai_ref[...], kr_ref[...], ki_ref[...] = ar, ai, kr, ki
        bbr_ref[...] = kr * br_ref[...] - ki * bi_ref[...]
        bbi_ref[...] = kr * bi_ref[...] + ki * br_ref[...]

    v = jax.ShapeDtypeStruct((1, N_STATES), f32)
    m = jax.ShapeDtypeStruct((SSM_GROUP, N_STATES), f32)
    return pl.pallas_call(body, name="ssm_prep", out_shape=[v, v, v, v, m, m])(lr, li, ldt, bt_re, bt_im)


def _pow_segment(ar, ai, seg_len):
    pr, pi = ar, ai
    for _ in range(int(math.log2(seg_len))):
        pr, pi = _cmul(pr, pi, pr, pi)
    return pr, pi


GELU_K = math.sqrt(2.0 / math.pi)
GELU_C = 0.044715


def _gelu(v):
    return 0.5 * v * (1.0 + jnp.tanh(GELU_K * (v + GELU_C * v * v * v)))


def _gelu_grad(v):
    t = jnp.tanh(GELU_K * (v + GELU_C * v * v * v))
    return 0.5 * (1.0 + t) + 0.5 * v * (1.0 - t * t) * GELU_K * (1.0 + 3.0 * GELU_C * v * v)


def ssm_scan_fwd(u_perm, bd_r, bd_i, ar, ai, *, final=None, rides=()):
    S = u_perm.shape[0]
    store = final is not None
    R = SCAN_ROWS if store else min(2 * SCAN_ROWS, S // 2)
    nblk = S // R
    seg_len = S // N_SEG
    nstrip = N_STATES // LANE_STRIP

    def body(*refs):
        if store:
            (u_ref, bdr_ref, bdi_ref, ar_ref, ai_ref, cdr_ref, cdi_ref, d_ref, ir_ref, ii_ref,
             y_ref, ys_ref, sr_ref, si_ref, bur, bui, car, cai) = refs
        else:
            u_ref, bdr_ref, bdi_ref, ar_ref, ai_ref, or_ref, oi_ref, bur, bui, car, cai = refs
        i = pl.program_id(0)
        ub = _bf(u_ref[...])
        u = ub.astype(f32)
        for cb in range(4):
            us = ub[:, cb * 128:(cb + 1) * 128]
            bur[:, cb * 512:(cb + 1) * 512] = jnp.dot(us, bdr_ref[cb], preferred_element_type=f32)
            bui[:, cb * 512:(cb + 1) * 512] = jnp.dot(us, bdi_ref[cb], preferred_element_type=f32)

        @pl.when(i == 0)
        def _():
            if store:
                car[...] = ir_ref[...]
                cai[...] = ii_ref[...]
            else:
                car[...] = jnp.zeros_like(car)
                cai[...] = jnp.zeros_like(cai)

        for ls in range(nstrip):
            lanes = pl.ds(ls * LANE_STRIP, LANE_STRIP)
            a_r = jnp.broadcast_to(ar_ref[:, lanes], (N_SEG, LANE_STRIP))
            a_i = jnp.broadcast_to(ai_ref[:, lanes], (N_SEG, LANE_STRIP))

            def step(t, carry, lanes=lanes, a_r=a_r, a_i=a_i):
                s_r, s_i = carry
                row = pl.multiple_of(t * 8, 8)
                n_r = a_r * s_r - a_i * s_i + bur[pl.ds(row, 8), lanes]
                n_i = a_r * s_i + a_i * s_r + bui[pl.ds(row, 8), lanes]
                if store:
                    sr_ref[pl.ds(row, 8), lanes] = n_r
                    si_ref[pl.ds(row, 8), lanes] = n_i
                return n_r, n_i

            e_r, e_i = lax.fori_loop(0, R // 8, step, (car[:, lanes], cai[:, lanes]), unroll=True)
            car[:, lanes] = e_r
            cai[:, lanes] = e_i

        if store:
            for cb in range(4):
                st_r = sr_ref[:, cb * 512:(cb + 1) * 512].astype(bf16)
                st_i = si_ref[:, cb * 512:(cb + 1) * 512].astype(bf16)
                y = (jnp.dot(st_r, cdr_ref[cb], preferred_element_type=f32)
                     - jnp.dot(st_i, cdi_ref[cb], preferred_element_type=f32))
                cols = slice(cb * 128, (cb + 1) * 128)
                y = y + d_ref[:, cols] * u[:, cols]
                y_ref[:, cols] = y
                ys_ref[:, cols] = _gelu(y).astype(bf16)
        else:
            @pl.when(i == nblk - 1)
            def _():
                p_r, p_i = _pow_segment(ar_ref[...], ai_ref[...], seg_len)
                t_r, t_i = car[0:1, :], cai[0:1, :]
                or_ref[0:1, :] = jnp.zeros((1, N_STATES), f32)
                oi_ref[0:1, :] = jnp.zeros((1, N_STATES), f32)
                for j in range(1, N_SEG):
                    or_ref[j:j + 1, :] = t_r
                    oi_ref[j:j + 1, :] = t_i
                    m_r, m_i = _cmul(p_r, p_i, t_r, t_i)
                    t_r, t_i = car[j:j + 1, :] + m_r, cai[j:j + 1, :] + m_i

    blk = lambda w: pl.BlockSpec((R, w), lambda i: (i, 0))
    in_specs = [blk(SSM_WIDTH), _full((4, 128, 512)), _full((4, 128, 512)), _full((1, N_STATES)), _full((1, N_STATES))]
    args = [u_perm, bd_r, bd_i, ar, ai]
    scratch = [pltpu.VMEM((R, N_STATES), f32), pltpu.VMEM((R, N_STATES), f32),
               pltpu.VMEM((N_SEG, N_STATES), f32), pltpu.VMEM((N_SEG, N_STATES), f32)]
    if store:
        in_specs += [_full((4, 512, 128)), _full((4, 512, 128)), _full((1, SSM_WIDTH)),
                     _full((N_SEG, N_STATES)), _full((N_SEG, N_STATES))]
        args += list(final)
        out_specs = [blk(SSM_WIDTH), blk(SSM_WIDTH), blk(N_STATES), blk(N_STATES)]
        out_shape = [jax.ShapeDtypeStruct((S, SSM_WIDTH), f32), jax.ShapeDtypeStruct((S, SSM_WIDTH), bf16),
                     jax.ShapeDtypeStruct((S, N_STATES), f32), jax.ShapeDtypeStruct((S, N_STATES), f32)]
        name = "ssm_scan_fwd"
    else:
        out_specs = [_full((N_SEG, N_STATES)), _full((N_SEG, N_STATES))]
        out_shape = [jax.ShapeDtypeStruct((N_SEG, N_STATES), f32)] * 2
        name = "ssm_scan_fwd_carry"
    res, landed = _pcall(body, args, name=name, grid=(nblk,), in_specs=in_specs, out_specs=out_specs,
                         out_shape=out_shape, scratch_shapes=scratch, rides=rides)
    return (res, landed) if rides else res


def ssm_scan_bwd(dys_perm, y_perm, cdt_r, cdt_i, ar, ai, *, final=None, rides=()):
    S = dys_perm.shape[0]
    store = final is not None
    R = SCAN_ROWS if store else min(2 * SCAN_ROWS, S // 2)
    nblk = S // R
    seg_len = S // N_SEG
    strip = LANE_STRIP // 2 if store else LANE_STRIP
    nstrip = N_STATES // strip

    def body(*refs):
        if store:
            (dys_ref, y_ref, cdr_ref, cdi_ref, ar_ref, ai_ref, u_ref, sr_ref, si_ref, bdr_ref, bdi_ref, d_ref, ir_ref,
             ii_ref, du_ref, dbr_ref, dbi_ref, dcr_ref, dci_ref, dar_ref, dai_ref, dd_ref, dsr, dsi, lmr, lmi, car,
             cai) = refs
        else:
            dys_ref, y_ref, cdr_ref, cdi_ref, ar_ref, ai_ref, or_ref, oi_ref, dsr, dsi, car, cai = refs
        i = pl.program_id(0)
        dy = dys_ref[...] * _gelu_grad(y_ref[...])
        dyb = dy.astype(bf16)
        for cb in range(4):
            ds_ = dyb[:, cb * 128:(cb + 1) * 128]
            dsr[:, cb * 512:(cb + 1) * 512] = jnp.dot(ds_, cdr_ref[cb], preferred_element_type=f32)
            dsi[:, cb * 512:(cb + 1) * 512] = -jnp.dot(ds_, cdi_ref[cb], preferred_element_type=f32)

        @pl.when(i == 0)
        def _():
            if store:
                car[...] = ir_ref[...]
                cai[...] = ii_ref[...]
                dar_ref[...] = jnp.zeros_like(dar_ref)
                dai_ref[...] = jnp.zeros_like(dai_ref)
                dbr_ref[...] = jnp.zeros_like(dbr_ref)
                dbi_ref[...] = jnp.zeros_like(dbi_ref)
                dcr_ref[...] = jnp.zeros_like(dcr_ref)
                dci_ref[...] = jnp.zeros_like(dci_ref)
                dd_ref[...] = jnp.zeros_like(dd_ref)
            else:
                car[...] = jnp.zeros_like(car)
                cai[...] = jnp.zeros_like(cai)

        for ls in range(nstrip):
            lanes = pl.ds(ls * strip, strip)
            a_r = jnp.broadcast_to(ar_ref[:, lanes], (N_SEG, strip))
            a_i = jnp.broadcast_to(ai_ref[:, lanes], (N_SEG, strip))
            zero = jnp.zeros((N_SEG, strip), f32)

            def step(k, carry, lanes=lanes, a_r=a_r, a_i=a_i):
                l_r, l_i, g_r, g_i = carry
                row = pl.multiple_of((R // 8 - 1 - k) * 8, 8)
                if store:
                    s_r = sr_ref[pl.ds(row, 8), lanes]
                    s_i = si_ref[pl.ds(row, 8), lanes]
                    g_r = g_r + l_r * s_r + l_i * s_i
                    g_i = g_i + l_i * s_r - l_r * s_i
                n_r = dsr[pl.ds(row, 8), lanes] + a_r * l_r + a_i * l_i
                n_i = dsi[pl.ds(row, 8), lanes] + a_r * l_i - a_i * l_r
                if store:
                    lmr[pl.ds(row, 8), lanes] = n_r
                    lmi[pl.ds(row, 8), lanes] = n_i
                return n_r, n_i, g_r, g_i

            e_r, e_i, g_r, g_i = lax.fori_loop(0, R // 8, step, (car[:, lanes], cai[:, lanes], zero, zero),
                                               unroll=True)
            car[:, lanes] = e_r
            cai[:, lanes] = e_i
            if store:
                dar_ref[:, lanes] += g_r
                dai_ref[:, lanes] += g_i

        if store:
            ub = _bf(u_ref[...])
            u = ub.astype(f32)
            for cb in range(4):
                cols = slice(cb * 128, (cb + 1) * 128)
                st = slice(cb * 512, (cb + 1) * 512)
                l_r = lmr[:, st].astype(bf16)
                l_i = lmi[:, st].astype(bf16)
                du = (jnp.dot(l_r, bdr_ref[cb], preferred_element_type=f32)
                      + jnp.dot(l_i, bdi_ref[cb], preferred_element_type=f32))
                du_ref[:, cols] = (du + d_ref[:, cols] * dy[:, cols]).astype(bf16)
                dbr_ref[cb] += lax.dot_general(l_r, ub[:, cols], TN_DIMS, preferred_element_type=f32)
                dbi_ref[cb] += lax.dot_general(l_i, ub[:, cols], TN_DIMS, preferred_element_type=f32)
                dcr_ref[cb] += lax.dot_general(dyb[:, cols], sr_ref[:, st].astype(bf16), TN_DIMS,
                                               preferred_element_type=f32)
                dci_ref[cb] += lax.dot_general(dyb[:, cols], si_ref[:, st].astype(bf16), TN_DIMS,
                                               preferred_element_type=f32)
            dd_ref[...] += jnp.sum(dy * u, axis=0, keepdims=True)
        else:
            @pl.when(i == nblk - 1)
            def _():
                p_r, p_i = _pow_segment(ar_ref[...], ai_ref[...], seg_len)
                p_i = -p_i
                t_r, t_i = car[N_SEG - 1:N_SEG, :], cai[N_SEG - 1:N_SEG, :]
                or_ref[N_SEG - 1:N_SEG, :] = jnp.zeros((1, N_STATES), f32)
                oi_ref[N_SEG - 1:N_SEG, :] = jnp.zeros((1, N_STATES), f32)
                for j in range(N_SEG - 2, -1, -1):
                    or_ref[j:j + 1, :] = t_r
                    oi_ref[j:j + 1, :] = t_i
                    m_r, m_i = _cmul(p_r, p_i, t_r, t_i)
                    t_r, t_i = car[j:j + 1, :] + m_r, cai[j:j + 1, :] + m_i

    blk = lambda w: pl.BlockSpec((R, w), lambda i: (nblk - 1 - i, 0))
    in_specs = [blk(SSM_WIDTH), blk(SSM_WIDTH), _full((4, 128, 512)), _full((4, 128, 512)), _full((1, N_STATES)),
                _full((1, N_STATES))]
    args = [dys_perm, y_perm, cdt_r, cdt_i, ar, ai]
    seg = jax.ShapeDtypeStruct((N_SEG, N_STATES), f32)
    if store:
        in_specs += [blk(SSM_WIDTH), blk(N_STATES), blk(N_STATES), _full((4, 512, 128)), _full((4, 512, 128)),
                     _full((1, SSM_WIDTH)), _full((N_SEG, N_STATES)), _full((N_SEG, N_STATES))]
        args += list(final)
        out_specs = [blk(SSM_WIDTH), _full((4, 512, 128)), _full((4, 512, 128)), _full((4, 128, 512)),
                     _full((4, 128, 512)), _full((N_SEG, N_STATES)), _full((N_SEG, N_STATES)), _full((1, SSM_WIDTH))]
        b_acc = jax.ShapeDtypeStruct((4, 512, 128), f32)
        c_acc = jax.ShapeDtypeStruct((4, 128, 512), f32)
        out_shape = [jax.ShapeDtypeStruct((S, SSM_WIDTH), bf16), b_acc, b_acc, c_acc, c_acc, seg, seg,
                     jax.ShapeDtypeStruct((1, SSM_WIDTH), f32)]
        scratch = [pltpu.VMEM((R, N_STATES), f32)] * 4 + [pltpu.VMEM((N_SEG, N_STATES), f32)] * 2
        name = "ssm_scan_bwd"
    else:
        out_specs = [_full((N_SEG, N_STATES)), _full((N_SEG, N_STATES))]
        out_shape = [seg, seg]
        scratch = [pltpu.VMEM((R, N_STATES), f32)] * 2 + [pltpu.VMEM((N_SEG, N_STATES), f32)] * 2
        name = "ssm_scan_bwd_carry"
    res, landed = _pcall(body, args, name=name, grid=(nblk,), in_specs=in_specs, out_specs=out_specs,
                         out_shape=out_shape, scratch_shapes=scratch, rides=rides)
    return (res, landed) if rides else res


def ssm_param_bwd(dbb_r, dbb_i, bt_re, bt_im, kr, ki, ar, ai, lr, li, ldt, da_r, da_i):
    def body(dbr_ref, dbi_ref, br_ref, bi_ref, kr_ref, ki_ref, ar_ref, ai_ref, lr_ref, li_ref, ldt_ref, dar_ref,
             dai_ref, obr_ref, obi_ref, olr_ref, oli_ref, odt_ref):
        dbr, dbi, b_r, b_i = dbr_ref[...], dbi_ref[...], br_ref[...], bi_ref[...]
        k_r, k_i, a_r, a_i = kr_ref[...], ki_ref[...], ar_ref[...], ai_ref[...]
        l_r, l_i = lr_ref[...], li_ref[...]
        dt = jnp.exp(ldt_ref[...])
        obr_ref[...] = k_r * dbr + k_i * dbi
        obi_ref[...] = k_r * dbi - k_i * dbr
        gk_r = jnp.sum(dbr * b_r + dbi * b_i, axis=0, keepdims=True)
        gk_i = jnp.sum(dbi * b_r - dbr * b_i, axis=0, keepdims=True)
        ga_r = jnp.sum(dar_ref[...], axis=0, keepdims=True)
        ga_i = jnp.sum(dai_ref[...], axis=0, keepdims=True)
        den = l_r * l_r + l_i * l_i
        c_r, c_i = l_r / den, l_i / den
        m_r, m_i = _cmul(c_r, c_i, gk_r, gk_i)
        g_r, g_i = ga_r + m_r, ga_i + m_i
        t1_r, t1_i = _cmul(dt * a_r, -dt * a_i, g_r, g_i)
        q_r, q_i = _cmul(k_r, k_i, c_r, -c_i)
        t2_r, t2_i = _cmul(-q_r, q_i, gk_r, gk_i)
        olr_ref[...] = t1_r + t2_r
        oli_ref[...] = t1_i + t2_i
        w_r, w_i = _cmul(l_r, l_i, a_r, a_i)
        odt_ref[...] = dt * (w_r * g_r + w_i * g_i)

    v = jax.ShapeDtypeStruct((1, N_STATES), f32)
    m = jax.ShapeDtypeStruct((SSM_GROUP, N_STATES), f32)
    return pl.pallas_call(body, name="ssm_param_bwd", out_shape=[m, m, v, v, v])(
        dbb_r, dbb_i, bt_re, bt_im, kr, ki, ar, ai, lr, li, ldt, da_r, da_i)


def group_sum(v):
    def body(v_ref, o_ref):
        o_ref[...] = jnp.sum(v_ref[...], axis=-1, keepdims=True)
    return pl.pallas_call(body, name="group_sum", out_shape=jax.ShapeDtypeStruct((v.shape[0], 1), f32))(v)


ATTN_SCALE = XATTN_HEAD_DIM ** -0.5


def _attn_probs(q_h, k_h):
    s = lax.dot_general(q_h, k_h, NT_DIMS, preferred_element_type=f32) * ATTN_SCALE
    e = jnp.exp(s - jnp.max(s, axis=-1, keepdims=True))
    return e / jnp.sum(e, axis=-1, keepdims=True)


def attn_fwd(proj, kv, wxo3, tm=TM):
    S = proj.shape[0]
    M = kv.shape[0]
    J, _, n = wxo3.shape

    def body(q_ref, kv_ref, w_ref, o_ref, y_ref):
        for h in range(XATTN_HEADS):
            cols = slice(h * XATTN_HEAD_DIM, (h + 1) * XATTN_HEAD_DIM)
            q_h = q_ref[:, cols].astype(bf16)
            k_h = kv_ref[:, cols]
            v_h = kv_ref[:, XATTN_WIDTH + h * XATTN_HEAD_DIM:XATTN_WIDTH + (h + 1) * XATTN_HEAD_DIM]
            p = _attn_probs(q_h, k_h)
            o_ref[:, cols] = jnp.dot(p.astype(bf16), v_h, preferred_element_type=f32).astype(bf16)
        o = o_ref[...]
        for j in range(J):
            y_ref[:, j * n:(j + 1) * n] = jnp.dot(o, w_ref[j], preferred_element_type=f32).astype(bf16)

    return pl.pallas_call(
        body, name="attn_fwd", grid=(S // tm,),
        in_specs=[_rows(tm, XATTN_WIDTH, 10), _full((M, 2 * XATTN_WIDTH)), _full(wxo3.shape)],
        out_specs=[_rows(tm, XATTN_WIDTH), _rows(tm, J * n)],
        out_shape=[jax.ShapeDtypeStruct((S, XATTN_WIDTH), bf16), jax.ShapeDtypeStruct((S, J * n), bf16)],
        compiler_params=_params())(proj, kv, wxo3)


def attn_bwd(dy_c, o_att, wxo3, proj, kv, dproj, tm=TM):
    S = proj.shape[0]
    M = kv.shape[0]
    J, _, n = wxo3.shape
    nt = S // tm

    def body(dy_ref, o_ref, w_ref, q_ref, kv_ref, _, dq_ref, dkv_ref, dwo_ref, acc_ref):
        i = pl.program_id(0)

        @pl.when(i == 0)
        def _():
            dkv_ref[...] = jnp.zeros_like(dkv_ref)

        part = lax.dot_general(o_ref[...], dy_ref[...], TN_DIMS, preferred_element_type=f32)

        @pl.when(i == 0)
        def _():
            acc_ref[...] = part

        @pl.when(i > 0)
        def _():
            acc_ref[...] += part

        @pl.when(i == nt - 1)
        def _():
            for j in range(J):
                dwo_ref[j] = acc_ref[:, j * n:(j + 1) * n].astype(bf16)

        do = None
        for j in range(J):
            part = lax.dot_general(dy_ref[:, j * n:(j + 1) * n], w_ref[j], NT_DIMS, preferred_element_type=f32)
            do = part if do is None else do + part
        do = do.astype(bf16)
        for h in range(XATTN_HEADS):
            cols = slice(h * XATTN_HEAD_DIM, (h + 1) * XATTN_HEAD_DIM)
            vcols = slice(XATTN_WIDTH + h * XATTN_HEAD_DIM, XATTN_WIDTH + (h + 1) * XATTN_HEAD_DIM)
            q_h = q_ref[:, cols].astype(bf16)
            k_h = kv_ref[:, cols]
            v_h = kv_ref[:, vcols]
            do_h = do[:, cols]
            p = _attn_probs(q_h, k_h)
            dp = lax.dot_general(do_h, v_h, NT_DIMS, preferred_element_type=f32)
            ds = (p * (dp - jnp.sum(dp * p, axis=-1, keepdims=True)) * ATTN_SCALE).astype(bf16)
            dq_ref[:, cols] = jnp.dot(ds, k_h, preferred_element_type=f32).astype(bf16)
            dkv_ref[:, cols] += lax.dot_general(ds, q_h, TN_DIMS, preferred_element_type=f32)
            dkv_ref[:, vcols] += lax.dot_general(p.astype(bf16), do_h, TN_DIMS, preferred_element_type=f32)

    return pl.pallas_call(
        body, name="attn_bwd", grid=(S // tm,),
        in_specs=[_rows(tm, J * n), _rows(tm, XATTN_WIDTH), _full(wxo3.shape), _rows(tm, XATTN_WIDTH, 10),
                  _full((M, 2 * XATTN_WIDTH)), pl.BlockSpec(memory_space=pl.ANY)],
        out_specs=[_rows(tm, XATTN_WIDTH, 10), _full((M, 2 * XATTN_WIDTH)), _full(wxo3.shape)],
        out_shape=[jax.ShapeDtypeStruct(dproj.shape, bf16), jax.ShapeDtypeStruct((M, 2 * XATTN_WIDTH), f32),
                   jax.ShapeDtypeStruct(wxo3.shape, bf16)],
        scratch_shapes=[pltpu.VMEM((XATTN_WIDTH, J * n), f32)], input_output_aliases={5: 0},
        compiler_params=_params())(dy_c, o_att, wxo3, proj, kv, dproj)


def _ln_stats(r):
    mu = jnp.mean(r, axis=-1, keepdims=True)
    xc = r - mu
    var = jnp.mean(xc * xc, axis=-1, keepdims=True)
    rstd = lax.rsqrt(var + LN_EPS)
    return xc * rstd, rstd


def _ln_bwd(dy, xhat, rstd, g):
    dxh = dy * g
    return rstd * (dxh - jnp.mean(dxh, axis=-1, keepdims=True) - xhat * jnp.mean(dxh * xhat, axis=-1, keepdims=True))


def _colsum(v):
    return jnp.sum(v, axis=0, keepdims=True)


def _ln2_loss_epi(t2, h1, target, b_down, g, b):
    xhat, rstd = _ln_stats(ALPHA * h1 + t2 + b_down)
    err = xhat * g + b - target
    dout = err * (1.0 / D_MODEL)
    dr = _ln_bwd(dout, xhat, rstd, g)
    return (dr, dr), (0.5 * _colsum(err * err) * (1.0 / D_MODEL), _colsum(dout * xhat), _colsum(dout), _colsum(dr))


def _merge_fwd_pro(g0, g1, g2, y_a, glu_a, glu_b, y_c, b):
    gs = [_sig(g.astype(f32) + b[:, k * D_MODEL:(k + 1) * D_MODEL]) for k, g in enumerate((g0, g1, g2))]
    y_b = glu_a.astype(f32) * _sig(glu_b.astype(f32))
    return gs[0] * y_a.astype(f32) + gs[1] * y_b + gs[2] * y_c.astype(f32)


def _merge_bwd_epi(dm, g0, g1, g2, y_a, glu_a, glu_b, y_c, b):
    ga, sb = glu_a.astype(f32), _sig(glu_b.astype(f32))
    ys = (y_a.astype(f32), ga * sb, y_c.astype(f32))
    gs = [_sig(g.astype(f32) + b[:, k * D_MODEL:(k + 1) * D_MODEL]) for k, g in enumerate((g0, g1, g2))]
    dpre = [dm * ys[k] * gs[k] * (1.0 - gs[k]) for k in range(3)]
    dyb = dm * gs[1]
    dglu = jnp.concatenate([dyb * sb, dyb * ga * sb * (1.0 - sb)], axis=1)
    return ((jnp.concatenate(dpre, axis=1), dm * gs[0], dglu, dm * gs[2]),
            (jnp.concatenate([_colsum(d) for d in dpre], axis=1),))


def _ln1_fwd_epi(t1, x, g, b):
    xhat, _ = _ln_stats(ALPHA * x + t1)
    h = xhat * g + b
    return (t1, h, h), ()


def _ln1_bwd_epi(t3, x, t1, dr2, g):
    xhat, rstd = _ln_stats(ALPHA * x + t1)
    dh = ALPHA * dr2 + t3
    dr = _ln_bwd(dh, xhat, rstd, g)
    return (dr, dr), (_colsum(dh * xhat), _colsum(dh))


def exchange(*rides, name):
    return _pcall(lambda: None, [], name=name, grid=(), in_specs=[], out_specs=[], out_shape=[], rides=rides)[1]


ROW_TILE = 256


def sum_slots(recv, name):
    n, R, C = recv.shape
    tm = min(R, ROW_TILE)

    def body(r_ref, o_ref):
        acc = r_ref[0].astype(f32)
        for k in range(1, n):
            acc = acc + r_ref[k].astype(f32)
        o_ref[...] = acc

    return pl.pallas_call(
        body, name=name, grid=(R // tm,), in_specs=[pl.BlockSpec((n, tm, C), lambda i: (0, i, 0))],
        out_specs=_rows(tm, C), out_shape=jax.ShapeDtypeStruct((R, C), f32), compiler_params=_params())(recv)


def _adamw(w, g, m, v):
    m = ADAM_B1 * m + (1.0 - ADAM_B1) * g
    v = ADAM_B2 * v + (1.0 - ADAM_B2) * (g * g)
    m_hat = m / (1.0 - ADAM_B1 ** ADAM_STEP)
    v_hat = v / (1.0 - ADAM_B2 ** ADAM_STEP)
    delta = -ADAM_LR * (m_hat / (jnp.sqrt(v_hat) + ADAM_EPS) + ADAM_WD * w)
    return delta, m, v


def adam_pair(p, q, w, m, v, name):
    R, C = w.shape
    tm = min(R, ROW_TILE)

    def body(p_ref, q_ref, w_ref, m_ref, v_ref, g_ref, d_ref, nm_ref, nv_ref):
        g = p_ref[...] + q_ref[...]
        g_ref[...] = g
        d_ref[...], nm_ref[...], nv_ref[...] = _adamw(w_ref[...], g, m_ref[...], v_ref[...])

    o = jax.ShapeDtypeStruct((R, C), f32)
    return pl.pallas_call(body, name=name, grid=(R // tm,), in_specs=[_rows(tm, C)] * 5,
                          out_specs=[_rows(tm, C)] * 4, out_shape=[o] * 4, compiler_params=_params())(p, q, w, m, v)


def adam_slots(recv, w, m, v, name):
    n = recv.shape[0]

    def body(r_ref, w_ref, m_ref, v_ref, g_ref, d_ref, nm_ref, nv_ref):
        g = r_ref[0]
        for k in range(1, n):
            g = g + r_ref[k]
        g_ref[...] = g
        d_ref[...], nm_ref[...], nv_ref[...] = _adamw(w_ref[...], g, m_ref[...], v_ref[...])

    o = jax.ShapeDtypeStruct(w.shape, f32)
    return pl.pallas_call(body, name=name, out_shape=[o] * 4, compiler_params=_params())(recv, w, m, v)


BIG = ("w_in", "w_conv_out", "w_glu", "w_xattn_out", "w_kv", "w_out", "w_up", "w_down")
MID = ("w_conv_out", "w_glu", "w_xattn_out", "w_kv", "w_out")
SMALL = ("b_gate", "ssm_lam_re", "ssm_lam_im", "ssm_log_dt", "ssm_b_re", "ssm_b_im", "ssm_c_re", "ssm_c_im", "ssm_d",
         "ln1_g", "ln1_b", "b_up", "b_down", "ln2_g", "ln2_b")
def _pad_flat(a, mult=1024):
    a = a.reshape(-1)
    return jnp.pad(a, (0, (-a.shape[0]) % mult))


def _perm(a):
    S, W = a.shape
    return a.reshape(N_SEG, S // N_SEG, W).transpose(1, 0, 2).reshape(S, W)


def _unperm(a):
    S, W = a.shape
    return a.reshape(S // N_SEG, N_SEG, W).transpose(1, 0, 2).reshape(S, W)


def _state_rows(a):
    return a.transpose(2, 0, 1).reshape(SSM_GROUP, N_STATES)


def _block_diag_b(bt):
    b4 = bt.reshape(SSM_GROUP, 4, 8, SSM_STATE)
    eye = jnp.eye(8, dtype=bt.dtype)
    return jnp.einsum("hcgp,gk->cghkp", b4, eye).reshape(4, 128, 512)


def _block_diag_c(c):
    c4 = c.reshape(4, 8, SSM_GROUP, SSM_STATE)
    eye = jnp.eye(8, dtype=c.dtype)
    return jnp.einsum("cghp,gk->cgpkh", c4, eye).reshape(4, 512, 128)


def _diag_b(acc):
    a = acc.reshape(4, 8, SSM_STATE, 8, SSM_GROUP)
    eye = jnp.eye(8, dtype=acc.dtype)
    return jnp.einsum("cgpkh,gk->hcgp", a, eye).reshape(SSM_GROUP, N_STATES)


def _diag_c(acc):
    a = acc.reshape(4, 8, SSM_GROUP, 8, SSM_STATE)
    eye = jnp.eye(8, dtype=acc.dtype)
    return jnp.einsum("cghkp,gk->cghp", a, eye).reshape(SSM_GROUPS, SSM_GROUP, SSM_STATE)


def kernel(x, mem, w_in, b_gate, conv_w, w_conv_out, ssm_lam_re, ssm_lam_im, ssm_log_dt, ssm_b_re, ssm_b_im, ssm_c_re, ssm_c_im, ssm_d, w_glu, w_kv, w_xattn_out, w_out, ln1_g, ln1_b, w_up, b_up, w_down, b_down, ln2_g, ln2_b, loss_target, m_w_in, m_b_gate, m_conv_w, m_w_conv_out, m_ssm_lam_re, m_ssm_lam_im, m_ssm_log_dt, m_ssm_b_re, m_ssm_b_im, m_ssm_c_re, m_ssm_c_im, m_ssm_d, m_w_glu, m_w_kv, m_w_xattn_out, m_w_out, m_ln1_g, m_ln1_b, m_w_up, m_b_up, m_w_down, m_b_down, m_ln2_g, m_ln2_b, v_w_in, v_b_gate, v_conv_w, v_w_conv_out, v_ssm_lam_re, v_ssm_lam_im, v_ssm_log_dt, v_ssm_b_re, v_ssm_b_im, v_ssm_c_re, v_ssm_c_im, v_ssm_d, v_w_glu, v_w_kv, v_w_xattn_out, v_w_out, v_ln1_g, v_ln1_b, v_w_up, v_b_up, v_w_down, v_b_down, v_ln2_g, v_ln2_b):
    W = dict(w_in=w_in, b_gate=b_gate, conv_w=conv_w, w_conv_out=w_conv_out, ssm_lam_re=ssm_lam_re,
             ssm_lam_im=ssm_lam_im, ssm_log_dt=ssm_log_dt, ssm_b_re=ssm_b_re, ssm_b_im=ssm_b_im, ssm_c_re=ssm_c_re,
             ssm_c_im=ssm_c_im, ssm_d=ssm_d, w_glu=w_glu, w_kv=w_kv, w_xattn_out=w_xattn_out, w_out=w_out,
             ln1_g=ln1_g, ln1_b=ln1_b, w_up=w_up, b_up=b_up, w_down=w_down, b_down=b_down, ln2_g=ln2_g, ln2_b=ln2_b)
    MOM = dict(w_in=m_w_in, b_gate=m_b_gate, conv_w=m_conv_w, w_conv_out=m_w_conv_out, ssm_lam_re=m_ssm_lam_re,
               ssm_lam_im=m_ssm_lam_im, ssm_log_dt=m_ssm_log_dt, ssm_b_re=m_ssm_b_re, ssm_b_im=m_ssm_b_im,
               ssm_c_re=m_ssm_c_re, ssm_c_im=m_ssm_c_im, ssm_d=m_ssm_d, w_glu=m_w_glu, w_kv=m_w_kv,
               w_xattn_out=m_w_xattn_out, w_out=m_w_out, ln1_g=m_ln1_g, ln1_b=m_ln1_b, w_up=m_w_up, b_up=m_b_up,
               w_down=m_w_down, b_down=m_b_down, ln2_g=m_ln2_g, ln2_b=m_ln2_b)
    VEL = dict(w_in=v_w_in, b_gate=v_b_gate, conv_w=v_conv_w, w_conv_out=v_w_conv_out, ssm_lam_re=v_ssm_lam_re,
               ssm_lam_im=v_ssm_lam_im, ssm_log_dt=v_ssm_log_dt, ssm_b_re=v_ssm_b_re, ssm_b_im=v_ssm_b_im,
               ssm_c_re=v_ssm_c_re, ssm_c_im=v_ssm_c_im, ssm_d=v_ssm_d, w_glu=v_w_glu, w_kv=v_w_kv,
               w_xattn_out=v_w_xattn_out, w_out=v_w_out, ln1_g=v_ln1_g, ln1_b=v_ln1_b, w_up=v_w_up, b_up=v_b_up,
               w_down=v_w_down, b_down=v_b_down, ln2_g=v_ln2_g, ln2_b=v_ln2_b)
    names = list(W)
    xy = 2 * lax.axis_index("x") + lax.axis_index("y")

    xs = x[0]
    S = xs.shape[0]
    mems = mem[0]
    tgt = loss_target[0]

    shard_bf = {n: W[n][0].astype(bf16) for n in BIG}

    lr = ssm_lam_re.reshape(1, N_STATES)
    li = ssm_lam_im.reshape(1, N_STATES)
    ldt = jnp.repeat(ssm_log_dt.reshape(SSM_GROUPS), SSM_STATE).reshape(1, N_STATES)
    bt_re, bt_im = _state_rows(ssm_b_re[0]), _state_rows(ssm_b_im[0])
    ar, ai, kr, ki, bbt_r, bbt_i = ssm_prep(lr, li, ldt, bt_re, bt_im)
    bd_r, bd_i = _block_diag_b(bbt_r).astype(bf16), _block_diag_b(bbt_i).astype(bf16)
    cd_r, cd_i = _block_diag_c(ssm_c_re[0]).astype(bf16), _block_diag_c(ssm_c_im[0]).astype(bf16)
    bdt_r, bdt_i = bd_r.transpose(0, 2, 1), bd_i.transpose(0, 2, 1)
    cdt_r, cdt_i = cd_r.transpose(0, 2, 1), cd_i.transpose(0, 2, 1)
    d_skip = ssm_d.reshape(1, SSM_WIDTH)

    mem_bf = mems.astype(bf16)
    u_col = GATE_COLS + 3 * CONV_WIDTH
    me_xy = jnp.reshape(xy, (1,)).astype(jnp.int32)
    proj, x_bf, (win3,) = proj_own(xs, shard_bf["w_in"], me_xy, 4, rides=(("gather2", shard_bf["w_in"]),))
    proj, (wco3, wglu3, wxo3, wkv4, wout4, convw4) = proj_rest(
        x_bf, win3, me_xy, proj, rides=tuple(("gather2", shard_bf[n]) for n in MID) + (("gather", conv_w[0]),))
    wkv3 = wkv4.reshape(1, D_MODEL, 2 * XATTN_WIDTH)
    wout3 = wout4.reshape(1, D_MODEL, D_MODEL)
    convw_full = convw4.transpose(1, 0, 2).reshape(3, CONV_WIDTH)
    a_conv, y_a = conv_fwd(proj, convw_full, wco3)
    u_perm = _perm(proj[:, u_col:u_col + SSM_WIDTH])
    (init_r, init_i), (wup3,) = ssm_scan_fwd(u_perm, bd_r, bd_i, ar, ai, rides=(("gather2", shard_bf["w_up"]),))
    ysm_perm, ys_perm, st_r, st_i = ssm_scan_fwd(u_perm, bd_r, bd_i, ar, ai,
                                                 final=(cd_r, cd_i, d_skip, init_r, init_i))
    y_s = _unperm(ys_perm)
    glu = mm_fwd_small(y_s, wglu3, "glu", out_dtype=bf16)
    kv = mm_fwd_small(mem_bf, wkv3, "kv", out_dtype=bf16)
    o_att, y_c = attn_fwd(proj, kv, wxo3)
    merge_rows = ((proj, D_MODEL, 0), (proj, D_MODEL, 1), (proj, D_MODEL, 2), y_a, (glu, D_MODEL, 0),
                  (glu, D_MODEL, 1), y_c)
    merged, t1, h1, h1_bf = mm_rows_fused(
        (_merge_fwd_pro, merge_rows, (b_gate,)), wout3, "merge_w_out_ln1", transposed=False, tm=256,
        epi=_ln1_fwd_epi, extras=(xs,), consts=(ln1_g, ln1_b), outs=(f32, f32, bf16))
    (r_up, hdn), (wdn4,) = mm_fwd(h1_bf, wup3, "w_up", bias=b_up, outs=(bf16, bf16), second=_relu2,
                                  first=lambda v: jnp.maximum(v, 0.0), rides=(("gather2", shard_bf["w_down"]),))
    wdn3 = wdn4.reshape(1, D_FF, D_MODEL)

    dr2, dr2_bf, loss_cols, d_ln2_g, d_ln2_b, d_b_down = mm_rows_fused(
        hdn, wdn3, "w_down_ln2_loss", transposed=False, tm=512, epi=_ln2_loss_epi, extras=(h1, tgt),
        consts=(b_down, ln2_g, ln2_b), outs=(f32, bf16), nsums=4)
    part, other = {}, {}
    g_w_down = mm_bwd_w(hdn, dr2_bf, 1, "dw_down", tk=1024).reshape(4, -1, D_MODEL)
    (dup, d_b_up), (recv_dn,) = mm_bwd_x(dr2_bf, wdn3, "dup", tm=512, extras=(r_up,), colsum=True, out_dtype=bf16,
                                         epi=lambda acc, r: acc * (2.0 * r.astype(f32)),
                                         rides=(("scatter", g_w_down),))
    part["w_down"] = sum_slots(recv_dn, "sum_w_down")
    g_w_up = mm_bwd_w(h1_bf, dup, 4, "dw_up")
    dr1, dr1_bf, d_ln1_g, d_ln1_b = mm_rows_fused(
        dup, wup3, "dh1_ln1_bwd", transposed=True, tm=512, epi=_ln1_bwd_epi, extras=(xs, t1, dr2), consts=(ln1_g,),
        outs=(f32, bf16), nsums=2)
    g_w_out = mm_bwd_w(merged, dr1_bf, 1, "dw_out").reshape(4, -1, D_MODEL)
    (dproj, dy_a, dglu, dy_c, d_b_gate), (recv_up,) = mm_rows_fused(
        dr1_bf, wout3, "dmerged_merge_bwd", transposed=True, tm=256, epi=_merge_bwd_epi, extras=merge_rows,
        consts=(b_gate,), outs=((bf16, GATE_COLS, IN_COLS), bf16, (bf16, 2 * D_MODEL, 2 * D_MODEL), bf16),
        sum_widths=(GATE_COLS,), rides=(("scatter", g_w_up),))
    part["w_up"] = sum_slots(recv_up, "sum_w_up")

    dproj, d_conv_w, g_w_co = conv_bwd(dy_a, a_conv, wco3, proj, convw_full, dproj)

    g_w_glu = mm_bwd_w_small(y_s, dglu, 4, "dw_glu")
    dys_perm = _perm(mm_bwd_x(dglu, wglu3, "dy_s", tm=1024))
    linit_r, linit_i = ssm_scan_bwd(dys_perm, ysm_perm, cdt_r, cdt_i, ar, ai)
    (du_perm, dbacc_r, dbacc_i, dcacc_r, dcacc_i, da_r, da_i, d_ssm_d), (recv_co, recv_glu, recv_out) = ssm_scan_bwd(
        dys_perm, ysm_perm, cdt_r, cdt_i, ar, ai,
        final=(u_perm, st_r, st_i, bdt_r, bdt_i, d_skip, linit_r, linit_i),
        rides=(("scatter", g_w_co), ("scatter", g_w_glu), ("scatter", g_w_out)))
    part["w_out"] = sum_slots(recv_out, "sum_w_out")
    part["w_conv_out"] = sum_slots(recv_co, "sum_w_conv_out")
    part["w_glu"] = sum_slots(recv_glu, "sum_w_glu")
    dbt_re, dbt_im, d_lr, d_li, d_ldt_state = ssm_param_bwd(
        _diag_b(dbacc_r), _diag_b(dbacc_i), bt_re, bt_im, kr, ki, ar, ai, lr, li, ldt, da_r, da_i)
    d_log_dt = group_sum(d_ldt_state.reshape(SSM_GROUPS, SSM_STATE))
    d_b_re = dbt_re.reshape(SSM_GROUP, SSM_GROUPS, SSM_STATE).transpose(1, 2, 0)
    d_b_im = dbt_im.reshape(SSM_GROUP, SSM_GROUPS, SSM_STATE).transpose(1, 2, 0)
    d_c_re = _diag_c(dcacc_r)
    d_c_im = -_diag_c(dcacc_i)

    dproj, dkv, g_w_xo = attn_bwd(dy_c, o_att, wxo3, proj, kv, dproj)
    g_w_kv = mm_bwd_w(mem_bf, dkv, 1, "dw_kv").reshape(4, -1, D_MODEL)

    dproj = lax.dynamic_update_slice(dproj, _unperm(du_perm), (0, u_col))
    g_small = {"b_gate": d_b_gate, "ssm_lam_re": d_lr, "ssm_lam_im": d_li, "ssm_log_dt": d_log_dt, "ssm_b_re": d_b_re,
               "ssm_b_im": d_b_im, "ssm_c_re": d_c_re, "ssm_c_im": d_c_im, "ssm_d": d_ssm_d, "ln1_g": d_ln1_g,
               "ln1_b": d_ln1_b, "b_up": d_b_up, "b_down": d_b_down, "ln2_g": d_ln2_g, "ln2_b": d_ln2_b}
    small_names = SMALL + ("conv_w",)
    g_small["conv_w"] = d_conv_w
    sizes = {n: (g_small[n].size + 1023) // 1024 * 1024 for n in small_names}
    pack = lambda d: jnp.concatenate([_pad_flat(d[n]) for n in small_names]).reshape(-1, 128)
    early = ("w_down", "w_up", "w_out", "w_conv_out", "w_glu")
    g_w_in, landed = mm_bwd_w(
        x_bf, dproj, 4, "dw_in", rides=(("scatter", g_w_xo), ("scatter", g_w_kv), ("all", pack(g_small)))
        + tuple(("pair", part[n]) for n in early))
    recv_xo, recv_kv, srecv = landed[:3]
    other.update(zip(early, landed[3:]))
    part["w_xattn_out"] = sum_slots(recv_xo, "sum_w_xattn_out")
    part["w_kv"] = sum_slots(recv_kv, "sum_w_kv")
    dx, (recv_in, other["w_xattn_out"], other["w_kv"]) = mm_bwd_x(
        dproj, win3, "dx", extras=(dr1,), epi=lambda acc, d: acc + ALPHA * d,
        rides=(("scatter", g_w_in), ("pair", part["w_xattn_out"]), ("pair", part["w_kv"])))
    part["w_in"] = sum_slots(recv_in, "sum_w_in")
    other["w_in"], = exchange(("pair", part["w_in"]), name="swap_w_in")
    res = [{}, {}, {}, {}]
    for n in BIG:
        for k, r in enumerate(adam_pair(part[n], other[n], W[n][0], MOM[n][0], VEL[n][0], "adam_" + n)):
            res[k][n] = r[None]
    conv_zero = jnp.zeros((3, CONV_WIDTH), f32)
    gs, ds_, ms, vs = adam_slots(srecv, pack({**{n: W[n] for n in SMALL}, "conv_w": conv_zero}),
                                 pack({**{n: MOM[n] for n in SMALL}, "conv_w": conv_zero}),
                                 pack({**{n: VEL[n] for n in SMALL}, "conv_w": conv_zero}), "adam_small")

    def unpack_small(buf):
        flat = buf.reshape(-1)
        out, r = {}, 0
        for n in small_names:
            ref = g_small[n] if n == "conv_w" else W[n]
            out[n] = flat[r:r + ref.size].reshape(ref.shape)
            r += sizes[n]
        return out

    res_s = [unpack_small(b) for b in (gs, ds_, ms, vs)]
    g_conv = lax.dynamic_slice(res_s[0]["conv_w"], (0, xy * 128), (3, 128))
    conv_slots = g_conv.reshape(1, 3, 128)
    cg, cd, cm, cv = adam_slots(conv_slots, conv_w[0], m_conv_w[0], v_conv_w[0], "adam_conv")
    conv_res = [cg, cd, cm, cv]

    loss = lax.psum(jnp.sum(loss_cols), ("x", "y", "c"))
    outs = [loss, dx.reshape(x.shape)]
    for k in range(4):
        for n in names:
            if n == "conv_w":
                outs.append(conv_res[k].reshape(conv_w.shape))
            elif n in BIG:
                outs.append(res[k][n])
            else:
                outs.append(res_s[k][n])
    return tuple(outs)
```

```python
import functools
import math

import jax
import jax.numpy as jnp
from jax import lax
from jax.experimental import pallas as pl
from jax.experimental.pallas import tpu as pltpu

f32 = jnp.float32
bf16 = jnp.bfloat16

D_MODEL = 1024
CONV_WIDTH = 512
SSM_WIDTH = 512
SSM_GROUP = 16
SSM_GROUPS = 32
SSM_STATE = 64
N_STATES = SSM_GROUPS * SSM_STATE
XATTN_HEADS = 4
XATTN_HEAD_DIM = 128
XATTN_WIDTH = 512
D_FF = 4096
GATE_COLS = 3 * D_MODEL
IN_COLS = GATE_COLS + 3 * CONV_WIDTH + SSM_WIDTH + XATTN_WIDTH
ALPHA = 2.0 ** 0.25
LN_EPS = 1e-5
ADAM_LR = 0.001
ADAM_B1 = 0.9
ADAM_B2 = 0.999
ADAM_EPS = 1e-08
ADAM_WD = 0.01
ADAM_STEP = 10

N_SEG = 8
SCAN_ROWS = 512
LANE_STRIP = 512
TM = 512
VMEM_LIMIT = 48 * 1024 * 1024
MESH = pl.DeviceIdType.MESH

NT_DIMS = (((1,), (1,)), ((), ()))
TN_DIMS = (((0,), (0,)), ((), ()))


def _params():
    return pltpu.CompilerParams(vmem_limit_bytes=VMEM_LIMIT)


def _full(shape):
    n = len(shape)
    return pl.BlockSpec(shape, lambda *_: (0,) * n)


def _rows(tm, w, cb=0):
    return pl.BlockSpec((tm, w), lambda i: (i, cb))


def _sig(x):
    return 1.0 / (1.0 + jnp.exp(-x))


HBM_SPEC = pl.BlockSpec(memory_space=pl.ANY)
RIDE_PEERS = {"gather": 3, "gather2": 6, "scatter": 3, "all": 7, "pair": 1}


def _xy_peers(x, y):
    return [(1 - x, y), (x, 1 - y), (1 - x, 1 - y)]


def _ride_copies(kind, src, dst, send_sems, recv_sems, local_sem):
    x, y, c = lax.axis_index("x"), lax.axis_index("y"), lax.axis_index("c")
    me = 2 * x + y
    if kind == "all":
        flips = [(fx, fy, fc) for fx in (0, 1) for fy in (0, 1) for fc in (0, 1)][1:]
        peers = [(x ^ fx, y ^ fy, c ^ fc) for fx, fy, fc in flips]
        slot = lambda p: 4 * p[0] + 2 * p[1] + p[2]
        mine = slot((x, y, c))
    else:
        peers = [(px, py, c) for px, py in _xy_peers(x, y)]
        slot = lambda p: 2 * p[0] + p[1]
        mine = me

    def remote(k, s, d):
        return pltpu.make_async_remote_copy(src_ref=s, dst_ref=d, send_sem=send_sems.at[k], recv_sem=recv_sems.at[k],
                                            device_id=peers[k], device_id_type=MESH)

    if kind == "pair":
        peers = [(x, y, 1 - c)]
        return None, [remote(0, src, dst)], [remote(0, src, dst)]
    if kind == "scatter":
        local = pltpu.make_async_copy(src.at[me], dst.at[me], local_sem)
        sends = [remote(k, src.at[slot(p)], dst.at[me]) for k, p in enumerate(peers)]
        lands = [remote(k, src.at[me], dst.at[slot(p)]) for k, p in enumerate(peers)]
    else:
        local = pltpu.make_async_copy(src, dst.at[mine], local_sem)
        sends = [remote(k, src, dst.at[mine]) for k, p in enumerate(peers)]
        lands = [remote(k, src, dst.at[slot(p)]) for k, p in enumerate(peers)]
    return local, sends, lands


def _ride_shape(kind, src):
    lead = {"gather": (4,), "gather2": (4,), "scatter": (), "all": (8,), "pair": ()}[kind]
    return jax.ShapeDtypeStruct(lead + src.shape, src.dtype)


def _two_level(src, dst, send_sems, recv_sems, local_sem):
    x, y, c = lax.axis_index("x"), lax.axis_index("y"), lax.axis_index("c")
    me = 2 * x + y
    h = src.shape[0] // 2
    mine = pl.ds(pl.multiple_of(c * h, 16), h)
    theirs = pl.ds(pl.multiple_of((1 - c) * h, 16), h)
    peers = _xy_peers(x, y)
    slots = [2 * px + py for px, py in peers]

    def over_ici(k, d):
        return pltpu.make_async_remote_copy(src_ref=src.at[mine], dst_ref=d, send_sem=send_sems.at[k],
                                            recv_sem=recv_sems.at[k], device_id=(*peers[k], c), device_id_type=MESH)

    def over_d2d(k, rows):
        blk = dst.at[slots[k], rows]
        return pltpu.make_async_remote_copy(src_ref=blk, dst_ref=blk, send_sem=send_sems.at[3 + k],
                                            recv_sem=recv_sems.at[3 + k], device_id=(x, y, 1 - c), device_id_type=MESH)

    return (pltpu.make_async_copy(src, dst.at[me], local_sem),
            [over_ici(k, dst.at[me, mine]) for k in range(3)], [over_ici(k, dst.at[slots[k], mine]) for k in range(3)],
            [over_d2d(k, mine) for k in range(3)], [over_d2d(k, theirs) for k in range(3)])


def _pcall(body, args, *, name, grid, in_specs, out_specs, out_shape, scratch_shapes=(), rides=(), aliases=None,
           prefetch=()):
    n_in, n_out, n_scr, nr, npf = len(in_specs), len(out_specs), len(scratch_shapes), len(rides), len(prefetch)
    kinds = [k for k, _ in rides]

    def wrapped(*refs):
        pre, refs = refs[:npf], refs[npf:]
        ins, rsrc = refs[:n_in], refs[n_in:n_in + nr]
        outs = refs[n_in + nr:n_in + nr + n_out]
        rdst = refs[n_in + nr + n_out:n_in + 2 * nr + n_out]
        scr = refs[n_in + 2 * nr + n_out:n_in + 2 * nr + n_out + n_scr]
        sems = refs[n_in + 2 * nr + n_out + n_scr:]

        def start():
            for r, kind in enumerate(kinds):
                if kind == "gather2":
                    local, sends, _, _, _ = _two_level(rsrc[r], rdst[r], *sems[3 * r:3 * r + 3])
                else:
                    local, sends, _ = _ride_copies(kind, rsrc[r], rdst[r], *sems[3 * r:3 * r + 3])
                if local is not None:
                    local.start()
                for cp in sends:
                    cp.start()

        def finish():
            for r, kind in enumerate(kinds):
                if kind == "gather2":
                    local, sends, ici_lands, forwards, lands = _two_level(rsrc[r], rdst[r], *sems[3 * r:3 * r + 3])
                    for k in range(3):
                        ici_lands[k].wait_recv()
                        forwards[k].start()
                    sends = sends + forwards
                else:
                    local, sends, lands = _ride_copies(kind, rsrc[r], rdst[r], *sems[3 * r:3 * r + 3])
                for cp in lands:
                    cp.wait_recv()
                for cp in sends:
                    cp.wait_send()
                if local is not None:
                    local.wait()

        if nr and grid:
            ids = [pl.program_id(a) for a in range(len(grid))]
            first = functools.reduce(jnp.logical_and, [i == 0 for i in ids])
            last = functools.reduce(jnp.logical_and, [i == g - 1 for i, g in zip(ids, grid)])
            pl.when(first)(start)
        elif nr:
            start()
        body(*pre, *ins, *outs, *scr)
        if nr and grid:
            pl.when(last)(finish)
        elif nr:
            finish()

    sems = []
    for kind in kinds:
        n = RIDE_PEERS[kind]
        sems += [pltpu.SemaphoreType.DMA((n,)), pltpu.SemaphoreType.DMA((n,)), pltpu.SemaphoreType.DMA(())]
    all_in = list(in_specs) + [HBM_SPEC] * nr
    all_out = list(out_specs) + [HBM_SPEC] * nr
    all_scratch = list(scratch_shapes) + sems
    if npf:
        how = dict(grid_spec=pltpu.PrefetchScalarGridSpec(num_scalar_prefetch=npf, grid=grid, in_specs=all_in,
                                                          out_specs=all_out, scratch_shapes=all_scratch))
    else:
        how = dict(grid=grid, in_specs=all_in, out_specs=all_out, scratch_shapes=all_scratch)
    res = pl.pallas_call(
        wrapped, name=name, out_shape=list(out_shape) + [_ride_shape(k, s) for k, s in rides],
        input_output_aliases=aliases or {}, compiler_params=_params(), **how)(
            *prefetch, *args, *[s for _, s in rides])
    return list(res[:n_out]), list(res[n_out:])


def _bf(v):
    return v if v.dtype == bf16 else v.astype(bf16)


def mm_fwd_small(a, w3, name, *, tm=1024, out_dtype=f32):
    M, K = a.shape
    J, _, n = w3.shape
    tm = min(tm, M)

    def body(a_ref, w_ref, o_ref):
        av = _bf(a_ref[...])
        for j in range(J):
            o_ref[:, j * n:(j + 1) * n] = jnp.dot(av, w_ref[j], preferred_element_type=f32).astype(out_dtype)

    return pl.pallas_call(
        body, name=name, grid=(M // tm,), in_specs=[_rows(tm, K), _full((J, K, n))], out_specs=_rows(tm, J * n),
        out_shape=jax.ShapeDtypeStruct((M, J * n), out_dtype), compiler_params=_params())(a, w3)


def mm_bwd_w_small(a, dy, J, name, *, tm=2048):
    M, K = a.shape
    n = dy.shape[1] // J
    tm = min(tm, M)
    ns = M // tm

    def body(a_ref, dy_ref, o_ref, acc_ref):
        s = pl.program_id(0)
        part = lax.dot_general(_bf(a_ref[...]), _bf(dy_ref[...]), TN_DIMS, preferred_element_type=f32)

        @pl.when(s == 0)
        def _():
            acc_ref[...] = part

        @pl.when(s > 0)
        def _():
            acc_ref[...] += part

        @pl.when(s == ns - 1)
        def _():
            for j in range(J):
                o_ref[j] = acc_ref[:, j * n:(j + 1) * n].astype(bf16)

    return pl.pallas_call(
        body, name=name, grid=(ns,), in_specs=[_rows(tm, K), _rows(tm, J * n)], out_specs=_full((J, K, n)),
        out_shape=jax.ShapeDtypeStruct((J, K, n), bf16), scratch_shapes=[pltpu.VMEM((K, J * n), f32)],
        compiler_params=_params())(a, dy)


def mm_fwd(a, w3, name, *, tm=1024, tn=None, bias=None, outs=(f32,), first=None, second=None, emit_a=False,
           rides=()):
    M, K = a.shape
    J, _, n = w3.shape
    tm = min(tm, M)
    tn = tn or n
    nl = n // tn
    nb = 0 if bias is None else 1

    def body(*refs):
        a_ref, w_ref = refs[0], refs[1]
        av = _bf(a_ref[...])
        acc = jnp.dot(av, w_ref[0], preferred_element_type=f32)
        if bias is not None:
            acc = acc + refs[2][...]
        refs[2 + nb][...] = (acc if first is None else first(acc)).astype(outs[0])
        if len(outs) > 1:
            refs[3 + nb][...] = second(acc).astype(outs[1])
        if emit_a:
            @pl.when(jnp.logical_and(pl.program_id(0) == 0, pl.program_id(1) == 0))
            def _():
                refs[2 + nb + len(outs)][...] = av

    in_specs = [pl.BlockSpec((tm, K), lambda j, l, i: (i, 0)),
                pl.BlockSpec((1, K, tn), lambda j, l, i: (j, 0, l))]
    args = [a, w3]
    if bias is not None:
        in_specs.append(pl.BlockSpec((1, tn), lambda j, l, i: (0, j * nl + l)))
        args.append(bias)
    out_specs = [pl.BlockSpec((tm, tn), lambda j, l, i: (i, j * nl + l)) for _ in outs]
    out_shape = [jax.ShapeDtypeStruct((M, J * n), dt) for dt in outs]
    if emit_a:
        last = M // tm - 1
        out_specs.append(pl.BlockSpec((tm, K), lambda j, l, i: (jnp.where((j == 0) & (l == 0), i, last), 0)))
        out_shape.append(jax.ShapeDtypeStruct((M, K), bf16))
    res, landed = _pcall(body, args, name=name, grid=(J, nl, M // tm), in_specs=in_specs, out_specs=out_specs,
                         out_shape=out_shape, rides=rides)
    res = res if len(res) > 1 else res[0]
    return (res, landed) if rides else res


def proj_own(x, w_own, me, n_blocks, *, tm=1024, rides=()):
    M, K = x.shape
    n = w_own.shape[1]

    def body(me_ref, x_ref, w_ref, o_ref, xb_ref):
        av = _bf(x_ref[...])
        xb_ref[...] = av
        o_ref[...] = jnp.dot(av, w_ref[...], preferred_element_type=f32).astype(bf16)

    (proj, x_bf), landed = _pcall(
        body, [x, w_own], name="proj_own", grid=(M // tm,), prefetch=(me,),
        in_specs=[pl.BlockSpec((tm, K), lambda i, me_ref: (i, 0)), pl.BlockSpec((K, n), lambda i, me_ref: (0, 0))],
        out_specs=[pl.BlockSpec((tm, n), lambda i, me_ref: (i, me_ref[0])),
                   pl.BlockSpec((tm, K), lambda i, me_ref: (i, 0))],
        out_shape=[jax.ShapeDtypeStruct((M, n_blocks * n), bf16), jax.ShapeDtypeStruct((M, K), bf16)], rides=rides)
    return proj, x_bf, landed


def proj_rest(x_bf, w3, me, proj, *, tm=1024, rides=()):
    M, K = x_bf.shape
    J, _, n = w3.shape

    def body(me_ref, x_ref, w_ref, _, o_ref):
        o_ref[...] = jnp.dot(x_ref[...], w_ref[0], preferred_element_type=f32).astype(bf16)

    (proj,), landed = _pcall(
        body, [x_bf, w3, proj], name="proj_rest", grid=(J - 1, M // tm), prefetch=(me,),
        in_specs=[pl.BlockSpec((tm, K), lambda k, i, me_ref: (i, 0)),
                  pl.BlockSpec((1, K, n), lambda k, i, me_ref: (me_ref[0] ^ (k + 1), 0, 0)), HBM_SPEC],
        out_specs=[pl.BlockSpec((tm, n), lambda k, i, me_ref: (i, me_ref[0] ^ (k + 1)))],
        out_shape=[jax.ShapeDtypeStruct(proj.shape, bf16)], aliases={3: 0}, rides=rides)
    return proj, landed


def mm_bwd_x(dy, w3, name, *, tm=TM, epi=None, extras=(), colsum=False, out_dtype=f32, rides=()):
    M = dy.shape[0]
    J, K, n = w3.shape
    tm = min(tm, M)
    nex = len(extras)

    def body(*refs):
        dy_ref, w_hbm = refs[0], refs[1]
        ex = refs[2:2 + nex]
        o_ref = refs[2 + nex]
        w_ref = refs[-1]
        i = pl.program_id(0)

        @pl.when(i == 0)
        def _():
            pltpu.sync_copy(w_hbm, w_ref)

        acc = None
        for j in range(J):
            part = lax.dot_general(_bf(dy_ref[:, j * n:(j + 1) * n]), w_ref[j], NT_DIMS, preferred_element_type=f32)
            acc = part if acc is None else acc + part
        if epi is not None:
            acc = epi(acc, *[e[...] for e in ex])
        o_ref[...] = acc.astype(out_dtype)
        if colsum:
            s_ref = refs[3 + nex]
            cs = jnp.sum(acc, axis=0, keepdims=True)

            @pl.when(i == 0)
            def _():
                s_ref[...] = cs

            @pl.when(i > 0)
            def _():
                s_ref[...] += cs

    in_specs = [pl.BlockSpec((tm, J * n), lambda i: (i, 0)), pl.BlockSpec(memory_space=pl.ANY)]
    in_specs += [pl.BlockSpec((tm, K), lambda i: (i, 0)) for _ in extras]
    out_specs = [pl.BlockSpec((tm, K), lambda i: (i, 0))]
    out_shape = [jax.ShapeDtypeStruct((M, K), out_dtype)]
    if colsum:
        out_specs.append(pl.BlockSpec((1, K), lambda i: (0, 0)))
        out_shape.append(jax.ShapeDtypeStruct((1, K), f32))
    res, landed = _pcall(body, [dy, w3, *extras], name=name, grid=(M // tm,), in_specs=in_specs, out_specs=out_specs,
                         out_shape=out_shape, scratch_shapes=[pltpu.VMEM((J, K, n), bf16)], rides=rides)
    res = res if colsum else res[0]
    return (res, landed) if rides else res


def mm_rows_fused(a, w3, name, *, transposed, tm, epi, extras=(), consts=(), outs=(f32,), nsums=0, sum_widths=None,
                  rides=()):
    pro = a[0] if isinstance(a, tuple) else None
    J, K, n = w3.shape
    W = K if transposed else n
    ka = J * n if transposed else K
    if pro is not None:
        a_rows = [e if isinstance(e, tuple) else (e, ka, 0) for e in a[1]]
        a_consts = list(a[2])
    else:
        a_rows, a_consts = [(a, ka, 0)], []
    M = a_rows[0][0].shape[0]
    tm = min(tm, M)
    na = len(a_rows) + len(a_consts)
    nex, nco = len(extras), len(consts)
    if pro is not None:
        outs = (bf16 if ka == W else (bf16, ka, ka),) + tuple(outs)
    nout = len(outs)
    nsums = len(sum_widths) if sum_widths is not None else nsums
    out_dtypes = [o[0] if isinstance(o, tuple) else o for o in outs]

    def body(*refs):
        a_refs, w_hbm = refs[:na], refs[na]
        ex = refs[na + 1:na + 1 + nex]
        co = refs[na + 1 + nex:na + 1 + nex + nco]
        o_refs = refs[na + 1 + nex + nco:na + 1 + nex + nco + nout]
        s_refs = refs[na + 1 + nex + nco + nout:na + 1 + nex + nco + nout + nsums]
        w_ref = refs[-1]
        i = pl.program_id(0)

        @pl.when(i == 0)
        def _():
            pltpu.sync_copy(w_hbm, w_ref)

        av = _bf(a_refs[0][...] if pro is None else pro(*[r[...] for r in a_refs]))
        if transposed:
            acc = None
            for j in range(J):
                part = lax.dot_general(av[:, j * n:(j + 1) * n], w_ref[j], NT_DIMS, preferred_element_type=f32)
                acc = part if acc is None else acc + part
        else:
            acc = jnp.dot(av, w_ref[0], preferred_element_type=f32)
        rows, sums = epi(acc, *[e[...] for e in ex], *[c[...] for c in co])
        if pro is not None:
            rows = (av,) + tuple(rows)
        for o_ref, v, dt in zip(o_refs, rows, out_dtypes):
            o_ref[...] = v.astype(dt)
        for s_ref, v in zip(s_refs, sums):
            @pl.when(i == 0)
            def _(s_ref=s_ref, v=v):
                s_ref[...] = v

            @pl.when(i > 0)
            def _(s_ref=s_ref, v=v):
                s_ref[...] += v

    extras = [e if isinstance(e, tuple) else (e, W, 0) for e in extras]
    outs_full = [o if isinstance(o, tuple) else (o, W, W) for o in outs]
    widths = list(sum_widths) if sum_widths is not None else [W] * nsums
    in_specs = [_rows(tm, w, cb) for _, w, cb in a_rows] + [_full(c.shape) for c in a_consts]
    in_specs += [pl.BlockSpec(memory_space=pl.ANY)]
    in_specs += [_rows(tm, w, cb) for _, w, cb in extras] + [_full(c.shape) for c in consts]
    out_specs = [_rows(tm, w) for _, w, _ in outs_full] + [_full((1, w)) for w in widths]
    out_shape = ([jax.ShapeDtypeStruct((M, cols), dt) for dt, _, cols in outs_full]
                 + [jax.ShapeDtypeStruct((1, w), f32) for w in widths])
    args = [e for e, _, _ in a_rows] + a_consts + [w3] + [e for e, _, _ in extras] + list(consts)
    res, landed = _pcall(body, args, name=name, grid=(M // tm,), in_specs=in_specs, out_specs=out_specs,
                         out_shape=out_shape, scratch_shapes=[pltpu.VMEM((J, K, n), bf16)], rides=rides)
    return (res, landed) if rides else res


def mm_bwd_w(a, dy, J, name, *, tm=2048, tn=None, tk=None, rides=()):
    M, K = a.shape
    n = dy.shape[1] // J
    tm = min(tm, M)
    tn = tn or n
    tk = tk or K
    nl = n // tn
    nk = K // tk
    ns = M // tm

    def body(a_ref, dy_ref, o_ref, acc_ref):
        s = pl.program_id(3)
        part = lax.dot_general(_bf(a_ref[...]), _bf(dy_ref[...]), TN_DIMS, preferred_element_type=f32)

        @pl.when(s == 0)
        def _():
            acc_ref[...] = part

        @pl.when(s > 0)
        def _():
            acc_ref[...] += part

        @pl.when(s == ns - 1)
        def _():
            o_ref[0] = acc_ref[...].astype(bf16)

    res, landed = _pcall(
        body, [a, dy], name=name, grid=(J, nl, nk, ns),
        in_specs=[pl.BlockSpec((tm, tk), lambda j, l, k, s: (s, k)),
                  pl.BlockSpec((tm, tn), lambda j, l, k, s: (s, j * nl + l))],
        out_specs=[pl.BlockSpec((1, tk, tn), lambda j, l, k, s: (j, k, l))],
        out_shape=[jax.ShapeDtypeStruct((J, K, n), bf16)],
        scratch_shapes=[pltpu.VMEM((tk, tn), f32)], rides=rides)
    return (res[0], landed) if rides else res[0]


def _relu2(v):
    r = jnp.maximum(v, 0.0)
    return r * r


HALO = 16


def _shift_down(z, k, halo):
    r = lax.broadcasted_iota(jnp.int32, z.shape, 0)
    y = pltpu.roll(z, k, 0)
    for q in range(k):
        y = jnp.where(r == q, halo[HALO - k + q:HALO - k + q + 1, :], y)
    return y


def _shift_up(z, k, halo):
    tm = z.shape[0]
    r = lax.broadcasted_iota(jnp.int32, z.shape, 0)
    y = pltpu.roll(z, tm - k, 0)
    for q in range(k):
        y = jnp.where(r == tm - k + q, halo[q:q + 1, :], y)
    return y


def _prev_halo(tm, cb):
    return pl.BlockSpec((HALO, CONV_WIDTH), lambda i: (jnp.maximum(i * (tm // HALO) - 1, 0), cb))


def _next_halo(tm, cb, nblk):
    return pl.BlockSpec((HALO, CONV_WIDTH), lambda i: (jnp.minimum((i + 1) * (tm // HALO), nblk - 1), cb))


def _f32(ref):
    return ref[...].astype(f32)


def conv_fwd(proj, conv_w, wco3, tm=TM):
    S = proj.shape[0]
    J, _, n = wco3.shape

    def body(cb_ref, cc_ref, ch_ref, cch_ref, chh_ref, w_ref, wo_ref, a_ref, y_ref):
        i = pl.program_id(0)
        z = _f32(cc_ref) * _f32(ch_ref)
        zh = jnp.where(i == 0, 0.0, _f32(cch_ref) * _f32(chh_ref))
        w = w_ref[...]
        dwz = w[0:1, :] * _shift_down(z, 2, zh) + w[1:2, :] * _shift_down(z, 1, zh) + w[2:3, :] * z
        a = (_f32(cb_ref) * dwz).astype(bf16)
        a_ref[...] = a
        for j in range(J):
            y_ref[:, j * n:(j + 1) * n] = jnp.dot(a, wo_ref[j], preferred_element_type=f32).astype(bf16)

    return pl.pallas_call(
        body, name="conv_fwd", grid=(S // tm,),
        in_specs=[_rows(tm, CONV_WIDTH, 6), _rows(tm, CONV_WIDTH, 7), _rows(tm, CONV_WIDTH, 8),
                  _prev_halo(tm, 7), _prev_halo(tm, 8), _full((3, CONV_WIDTH)), _full(wco3.shape)],
        out_specs=[_rows(tm, CONV_WIDTH), _rows(tm, J * n)],
        out_shape=[jax.ShapeDtypeStruct((S, CONV_WIDTH), bf16), jax.ShapeDtypeStruct((S, J * n), bf16)],
        compiler_params=_params())(proj, proj, proj, proj, proj, conv_w, wco3)


def conv_bwd(dy_a, a_conv, wco3, proj, conv_w, dproj, tm=TM):
    S = proj.shape[0]
    nt = S // tm
    J, _, n = wco3.shape

    def body(dy_ref, dyn_ref, a_ref, wo_ref, cb_ref, cc_ref, ch_ref, cch_ref, chh_ref, cbn_ref, w_ref, _, o_ref,
             dw_ref, dwo_ref, acc_ref):
        i = pl.program_id(0)

        def times_wo_t(dy):
            out = None
            for j in range(J):
                part = lax.dot_general(dy[:, j * n:(j + 1) * n], wo_ref[j], NT_DIMS, preferred_element_type=f32)
                out = part if out is None else out + part
            return out

        dyv = dy_ref[...]
        dav, dan = times_wo_t(dyv), times_wo_t(dyn_ref[...])
        part = lax.dot_general(a_ref[...], dyv, TN_DIMS, preferred_element_type=f32)

        @pl.when(i == 0)
        def _():
            acc_ref[...] = part

        @pl.when(i > 0)
        def _():
            acc_ref[...] += part

        @pl.when(i == nt - 1)
        def _():
            for j in range(J):
                dwo_ref[j] = acc_ref[:, j * n:(j + 1) * n].astype(bf16)

        cc, ch, cb = _f32(cc_ref), _f32(ch_ref), _f32(cb_ref)
        z = cc * ch
        zh = jnp.where(i == 0, 0.0, _f32(cch_ref) * _f32(chh_ref))
        w = w_ref[...]
        z1 = _shift_down(z, 1, zh)
        z2 = _shift_down(z, 2, zh)
        dwz = w[0:1, :] * z2 + w[1:2, :] * z1 + w[2:3, :] * z
        g = dav * cb
        gn = jnp.where(i == nt - 1, 0.0, dan * _f32(cbn_ref))
        dz = w[2:3, :] * g + w[1:2, :] * _shift_up(g, 1, gn) + w[0:1, :] * _shift_up(g, 2, gn)
        o_ref[:, 0:CONV_WIDTH] = (dav * dwz).astype(bf16)
        o_ref[:, CONV_WIDTH:2 * CONV_WIDTH] = (dz * ch).astype(bf16)
        o_ref[:, 2 * CONV_WIDTH:3 * CONV_WIDTH] = (dz * cc).astype(bf16)

        @pl.when(i == 0)
        def _():
            dw_ref[...] = jnp.zeros_like(dw_ref)

        dw_ref[0:1, :] += jnp.sum(g * z2, axis=0, keepdims=True)
        dw_ref[1:2, :] += jnp.sum(g * z1, axis=0, keepdims=True)
        dw_ref[2:3, :] += jnp.sum(g * z, axis=0, keepdims=True)

    return pl.pallas_call(
        body, name="conv_bwd", grid=(nt,),
        in_specs=[_rows(tm, J * n),
                  pl.BlockSpec((HALO, J * n), lambda i: (jnp.minimum((i + 1) * (tm // HALO), S // HALO - 1), 0)),
                  _rows(tm, CONV_WIDTH), _full(wco3.shape),
                  _rows(tm, CONV_WIDTH, 6), _rows(tm, CONV_WIDTH, 7), _rows(tm, CONV_WIDTH, 8),
                  _prev_halo(tm, 7), _prev_halo(tm, 8), _next_halo(tm, 6, S // HALO), _full((3, CONV_WIDTH)),
                  pl.BlockSpec(memory_space=pl.ANY)],
        out_specs=[_rows(tm, 3 * CONV_WIDTH, GATE_COLS // (3 * CONV_WIDTH)), _full((3, CONV_WIDTH)),
                   _full(wco3.shape)],
        out_shape=[jax.ShapeDtypeStruct(dproj.shape, bf16), jax.ShapeDtypeStruct((3, CONV_WIDTH), f32),
                   jax.ShapeDtypeStruct(wco3.shape, bf16)],
        scratch_shapes=[pltpu.VMEM((CONV_WIDTH, J * n), f32)], input_output_aliases={11: 0},
        compiler_params=_params())(dy_a, dy_a, a_conv, wco3, proj, proj, proj, proj, proj, proj, conv_w, dproj)


def _cmul(ar, ai, br, bi):
    return ar * br - ai * bi, ar * bi + ai * br


def ssm_prep(lr, li, ldt, bt_re, bt_im):
    def body(lr_ref, li_ref, ldt_ref, br_ref, bi_ref, ar_ref, ai_ref, kr_ref, ki_ref, bbr_ref, bbi_ref):
        lrv, liv = lr_ref[...], li_ref[...]
        dt = jnp.exp(ldt_ref[...])
        mag = jnp.exp(lrv * dt)
        ar = mag * jnp.cos(liv * dt)
        ai = mag * jnp.sin(liv * dt)
        den = lrv * lrv + liv * liv
        nr = ar - 1.0
        kr = (nr * lrv + ai * liv) / den
        ki = (ai * lrv - nr * liv) / den
        ar_ref[...], ai_ref[...], kr_ref[...], ki_ref[...] = ar, ai, kr, ki
        bbr_ref[...] = kr * br_ref[...] - ki * bi_ref[...]
        bbi_ref[...] = kr * bi_ref[...] + ki * br_ref[...]

    v = jax.ShapeDtypeStruct((1, N_STATES), f32)
    m = jax.ShapeDtypeStruct((SSM_GROUP, N_STATES), f32)
    return pl.pallas_call(body, name="ssm_prep", out_shape=[v, v, v, v, m, m])(lr, li, ldt, bt_re, bt_im)


def _pow_segment(ar, ai, seg_len):
    pr, pi = ar, ai
    for _ in range(int(math.log2(seg_len))):
        pr, pi = _cmul(pr, pi, pr, pi)
    return pr, pi


GELU_K = math.sqrt(2.0 / math.pi)
GELU_C = 0.044715


def _gelu(v):
    return 0.5 * v * (1.0 + jnp.tanh(GELU_K * (v + GELU_C * v * v * v)))


def _gelu_grad(v):
    t = jnp.tanh(GELU_K * (v + GELU_C * v * v * v))
    return 0.5 * (1.0 + t) + 0.5 * v * (1.0 - t * t) * GELU_K * (1.0 + 3.0 * GELU_C * v * v)


def ssm_scan_fwd(u_perm, bd_r, bd_i, ar, ai, *, final=None, rides=()):
    S = u_perm.shape[0]
    R = SCAN_ROWS
    nblk = S // R
    seg_len = S // N_SEG
    nstrip = N_STATES // LANE_STRIP
    store = final is not None

    def body(*refs):
        if store:
            (u_ref, bdr_ref, bdi_ref, ar_ref, ai_ref, cdr_ref, cdi_ref, d_ref, ir_ref, ii_ref,
             y_ref, ys_ref, sr_ref, si_ref, bur, bui, car, cai) = refs
        else:
            u_ref, bdr_ref, bdi_ref, ar_ref, ai_ref, or_ref, oi_ref, bur, bui, car, cai = refs
        i = pl.program_id(0)
        ub = _bf(u_ref[...])
        u = ub.astype(f32)
        for cb in range(4):
            us = ub[:, cb * 128:(cb + 1) * 128]
            bur[:, cb * 512:(cb + 1) * 512] = jnp.dot(us, bdr_ref[cb], preferred_element_type=f32)
            bui[:, cb * 512:(cb + 1) * 512] = jnp.dot(us, bdi_ref[cb], preferred_element_type=f32)

        @pl.when(i == 0)
        def _():
            if store:
                car[...] = ir_ref[...]
                cai[...] = ii_ref[...]
            else:
                car[...] = jnp.zeros_like(car)
                cai[...] = jnp.zeros_like(cai)

        for ls in range(nstrip):
            lanes = pl.ds(ls * LANE_STRIP, LANE_STRIP)
            a_r = jnp.broadcast_to(ar_ref[:, lanes], (N_SEG, LANE_STRIP))
            a_i = jnp.broadcast_to(ai_ref[:, lanes], (N_SEG, LANE_STRIP))

            def step(t, carry, lanes=lanes, a_r=a_r, a_i=a_i):
                s_r, s_i = carry
                row = pl.multiple_of(t * 8, 8)
                n_r = a_r * s_r - a_i * s_i + bur[pl.ds(row, 8), lanes]
                n_i = a_r * s_i + a_i * s_r + bui[pl.ds(row, 8), lanes]
                if store:
                    sr_ref[pl.ds(row, 8), lanes] = n_r
                    si_ref[pl.ds(row, 8), lanes] = n_i
                return n_r, n_i

            e_r, e_i = lax.fori_loop(0, R // 8, step, (car[:, lanes], cai[:, lanes]), unroll=True)
            car[:, lanes] = e_r
            cai[:, lanes] = e_i

        if store:
            for cb in range(4):
                st_r = sr_ref[:, cb * 512:(cb + 1) * 512].astype(bf16)
                st_i = si_ref[:, cb * 512:(cb + 1) * 512].astype(bf16)
                y = (jnp.dot(st_r, cdr_ref[cb], preferred_element_type=f32)
                     - jnp.dot(st_i, cdi_ref[cb], preferred_element_type=f32))
                cols = slice(cb * 128, (cb + 1) * 128)
                y = y + d_ref[:, cols] * u[:, cols]
                y_ref[:, cols] = y
                ys_ref[:, cols] = _gelu(y).astype(bf16)
        else:
            @pl.when(i == nblk - 1)
            def _():
                p_r, p_i = _pow_segment(ar_ref[...], ai_ref[...], seg_len)
                t_r, t_i = car[0:1, :], cai[0:1, :]
                or_ref[0:1, :] = jnp.zeros((1, N_STATES), f32)
                oi_ref[0:1, :] = jnp.zeros((1, N_STATES), f32)
                for j in range(1, N_SEG):
                    or_ref[j:j + 1, :] = t_r
                    oi_ref[j:j + 1, :] = t_i
                    m_r, m_i = _cmul(p_r, p_i, t_r, t_i)
                    t_r, t_i = car[j:j + 1, :] + m_r, cai[j:j + 1, :] + m_i

    blk = lambda w: pl.BlockSpec((R, w), lambda i: (i, 0))
    in_specs = [blk(SSM_WIDTH), _full((4, 128, 512)), _full((4, 128, 512)), _full((1, N_STATES)), _full((1, N_STATES))]
    args = [u_perm, bd_r, bd_i, ar, ai]
    scratch = [pltpu.VMEM((R, N_STATES), f32), pltpu.VMEM((R, N_STATES), f32),
               pltpu.VMEM((N_SEG, N_STATES), f32), pltpu.VMEM((N_SEG, N_STATES), f32)]
    if store:
        in_specs += [_full((4, 512, 128)), _full((4, 512, 128)), _full((1, SSM_WIDTH)),
                     _full((N_SEG, N_STATES)), _full((N_SEG, N_STATES))]
        args += list(final)
        out_specs = [blk(SSM_WIDTH), blk(SSM_WIDTH), blk(N_STATES), blk(N_STATES)]
        out_shape = [jax.ShapeDtypeStruct((S, SSM_WIDTH), f32), jax.ShapeDtypeStruct((S, SSM_WIDTH), bf16),
                     jax.ShapeDtypeStruct((S, N_STATES), f32), jax.ShapeDtypeStruct((S, N_STATES), f32)]
        name = "ssm_scan_fwd"
    else:
        out_specs = [_full((N_SEG, N_STATES)), _full((N_SEG, N_STATES))]
        out_shape = [jax.ShapeDtypeStruct((N_SEG, N_STATES), f32)] * 2
        name = "ssm_scan_fwd_carry"
    res, landed = _pcall(body, args, name=name, grid=(nblk,), in_specs=in_specs, out_specs=out_specs,
                         out_shape=out_shape, scratch_shapes=scratch, rides=rides)
    return (res, landed) if rides else res


def ssm_scan_bwd(dys_perm, y_perm, cdt_r, cdt_i, ar, ai, *, final=None, rides=()):
    S = dys_perm.shape[0]
    R = SCAN_ROWS
    nblk = S // R
    seg_len = S // N_SEG
    store = final is not None
    strip = LANE_STRIP // 2 if store else LANE_STRIP
    nstrip = N_STATES // strip

    def body(*refs):
        if store:
            (dys_ref, y_ref, cdr_ref, cdi_ref, ar_ref, ai_ref, u_ref, sr_ref, si_ref, bdr_ref, bdi_ref, d_ref, ir_ref,
             ii_ref, du_ref, dbr_ref, dbi_ref, dcr_ref, dci_ref, dar_ref, dai_ref, dd_ref, dsr, dsi, lmr, lmi, car,
             cai) = refs
        else:
            dys_ref, y_ref, cdr_ref, cdi_ref, ar_ref, ai_ref, or_ref, oi_ref, dsr, dsi, car, cai = refs
        i = pl.program_id(0)
        dy = dys_ref[...] * _gelu_grad(y_ref[...])
        dyb = dy.astype(bf16)
        for cb in range(4):
            ds_ = dyb[:, cb * 128:(cb + 1) * 128]
            dsr[:, cb * 512:(cb + 1) * 512] = jnp.dot(ds_, cdr_ref[cb], preferred_element_type=f32)
            dsi[:, cb * 512:(cb + 1) * 512] = -jnp.dot(ds_, cdi_ref[cb], preferred_element_type=f32)

        @pl.when(i == 0)
        def _():
            if store:
                car[...] = ir_ref[...]
                cai[...] = ii_ref[...]
                dar_ref[...] = jnp.zeros_like(dar_ref)
                dai_ref[...] = jnp.zeros_like(dai_ref)
                dbr_ref[...] = jnp.zeros_like(dbr_ref)
                dbi_ref[...] = jnp.zeros_like(dbi_ref)
                dcr_ref[...] = jnp.zeros_like(dcr_ref)
                dci_ref[...] = jnp.zeros_like(dci_ref)
                dd_ref[...] = jnp.zeros_like(dd_ref)
            else:
                car[...] = jnp.zeros_like(car)
                cai[...] = jnp.zeros_like(cai)

        for ls in range(nstrip):
            lanes = pl.ds(ls * strip, strip)
            a_r = jnp.broadcast_to(ar_ref[:, lanes], (N_SEG, strip))
            a_i = jnp.broadcast_to(ai_ref[:, lanes], (N_SEG, strip))
            zero = jnp.zeros((N_SEG, strip), f32)

            def step(k, carry, lanes=lanes, a_r=a_r, a_i=a_i):
                l_r, l_i, g_r, g_i = carry
                row = pl.multiple_of((R // 8 - 1 - k) * 8, 8)
                if store:
                    s_r = sr_ref[pl.ds(row, 8), lanes]
                    s_i = si_ref[pl.ds(row, 8), lanes]
                    g_r = g_r + l_r * s_r + l_i * s_i
                    g_i = g_i + l_i * s_r - l_r * s_i
                n_r = dsr[pl.ds(row, 8), lanes] + a_r * l_r + a_i * l_i
                n_i = dsi[pl.ds(row, 8), lanes] + a_r * l_i - a_i * l_r
                if store:
                    lmr[pl.ds(row, 8), lanes] = n_r
                    lmi[pl.ds(row, 8), lanes] = n_i
                return n_r, n_i, g_r, g_i

            e_r, e_i, g_r, g_i = lax.fori_loop(0, R // 8, step, (car[:, lanes], cai[:, lanes], zero, zero),
                                               unroll=True)
            car[:, lanes] = e_r
            cai[:, lanes] = e_i
            if store:
                dar_ref[:, lanes] += g_r
                dai_ref[:, lanes] += g_i

        if store:
            ub = _bf(u_ref[...])
            u = ub.astype(f32)
            for cb in range(4):
                cols = slice(cb * 128, (cb + 1) * 128)
                st = slice(cb * 512, (cb + 1) * 512)
                l_r = lmr[:, st].astype(bf16)
                l_i = lmi[:, st].astype(bf16)
                du = (jnp.dot(l_r, bdr_ref[cb], preferred_element_type=f32)
                      + jnp.dot(l_i, bdi_ref[cb], preferred_element_type=f32))
                du_ref[:, cols] = (du + d_ref[:, cols] * dy[:, cols]).astype(bf16)
                dbr_ref[cb] += lax.dot_general(l_r, ub[:, cols], TN_DIMS, preferred_element_type=f32)
                dbi_ref[cb] += lax.dot_general(l_i, ub[:, cols], TN_DIMS, preferred_element_type=f32)
                dcr_ref[cb] += lax.dot_general(dyb[:, cols], sr_ref[:, st].astype(bf16), TN_DIMS,
                                               preferred_element_type=f32)
                dci_ref[cb] += lax.dot_general(dyb[:, cols], si_ref[:, st].astype(bf16), TN_DIMS,
                                               preferred_element_type=f32)
            dd_ref[...] += jnp.sum(dy * u, axis=0, keepdims=True)
        else:
            @pl.when(i == nblk - 1)
            def _():
                p_r, p_i = _pow_segment(ar_ref[...], ai_ref[...], seg_len)
                p_i = -p_i
                t_r, t_i = car[N_SEG - 1:N_SEG, :], cai[N_SEG - 1:N_SEG, :]
                or_ref[N_SEG - 1:N_SEG, :] = jnp.zeros((1, N_STATES), f32)
                oi_ref[N_SEG - 1:N_SEG, :] = jnp.zeros((1, N_STATES), f32)
                for j in range(N_SEG - 2, -1, -1):
                    or_ref[j:j + 1, :] = t_r
                    oi_ref[j:j + 1, :] = t_i
                    m_r, m_i = _cmul(p_r, p_i, t_r, t_i)
                    t_r, t_i = car[j:j + 1, :] + m_r, cai[j:j + 1, :] + m_i

    blk = lambda w: pl.BlockSpec((R, w), lambda i: (nblk - 1 - i, 0))
    in_specs = [blk(SSM_WIDTH), blk(SSM_WIDTH), _full((4, 128, 512)), _full((4, 128, 512)), _full((1, N_STATES)),
                _full((1, N_STATES))]
    args = [dys_perm, y_perm, cdt_r, cdt_i, ar, ai]
    seg = jax.ShapeDtypeStruct((N_SEG, N_STATES), f32)
    if store:
        in_specs += [blk(SSM_WIDTH), blk(N_STATES), blk(N_STATES), _full((4, 512, 128)), _full((4, 512, 128)),
                     _full((1, SSM_WIDTH)), _full((N_SEG, N_STATES)), _full((N_SEG, N_STATES))]
        args += list(final)
        out_specs = [blk(SSM_WIDTH), _full((4, 512, 128)), _full((4, 512, 128)), _full((4, 128, 512)),
                     _full((4, 128, 512)), _full((N_SEG, N_STATES)), _full((N_SEG, N_STATES)), _full((1, SSM_WIDTH))]
        b_acc = jax.ShapeDtypeStruct((4, 512, 128), f32)
        c_acc = jax.ShapeDtypeStruct((4, 128, 512), f32)
        out_shape = [jax.ShapeDtypeStruct((S, SSM_WIDTH), bf16), b_acc, b_acc, c_acc, c_acc, seg, seg,
                     jax.ShapeDtypeStruct((1, SSM_WIDTH), f32)]
        scratch = [pltpu.VMEM((R, N_STATES), f32)] * 4 + [pltpu.VMEM((N_SEG, N_STATES), f32)] * 2
        name = "ssm_scan_bwd"
    else:
        out_specs = [_full((N_SEG, N_STATES)), _full((N_SEG, N_STATES))]
        out_shape = [seg, seg]
        scratch = [pltpu.VMEM((R, N_STATES), f32)] * 2 + [pltpu.VMEM((N_SEG, N_STATES), f32)] * 2
        name = "ssm_scan_bwd_carry"
    res, landed = _pcall(body, args, name=name, grid=(nblk,), in_specs=in_specs, out_specs=out_specs,
                         out_shape=out_shape, scratch_shapes=scratch, rides=rides)
    return (res, landed) if rides else res


def ssm_param_bwd(dbb_r, dbb_i, bt_re, bt_im, kr, ki, ar, ai, lr, li, ldt, da_r, da_i):
    def body(dbr_ref, dbi_ref, br_ref, bi_ref, kr_ref, ki_ref, ar_ref, ai_ref, lr_ref, li_ref, ldt_ref, dar_ref,
             dai_ref, obr_ref, obi_ref, olr_ref, oli_ref, odt_ref):
        dbr, dbi, b_r, b_i = dbr_ref[...], dbi_ref[...], br_ref[...], bi_ref[...]
        k_r, k_i, a_r, a_i = kr_ref[...], ki_ref[...], ar_ref[...], ai_ref[...]
        l_r, l_i = lr_ref[...], li_ref[...]
        dt = jnp.exp(ldt_ref[...])
        obr_ref[...] = k_r * dbr + k_i * dbi
        obi_ref[...] = k_r * dbi - k_i * dbr
        gk_r = jnp.sum(dbr * b_r + dbi * b_i, axis=0, keepdims=True)
        gk_i = jnp.sum(dbi * b_r - dbr * b_i, axis=0, keepdims=True)
        ga_r = jnp.sum(dar_ref[...], axis=0, keepdims=True)
        ga_i = jnp.sum(dai_ref[...], axis=0, keepdims=True)
        den = l_r * l_r + l_i * l_i
        c_r, c_i = l_r / den, l_i / den
        m_r, m_i = _cmul(c_r, c_i, gk_r, gk_i)
        g_r, g_i = ga_r + m_r, ga_i + m_i
        t1_r, t1_i = _cmul(dt * a_r, -dt * a_i, g_r, g_i)
        q_r, q_i = _cmul(k_r, k_i, c_r, -c_i)
        t2_r, t2_i = _cmul(-q_r, q_i, gk_r, gk_i)
        olr_ref[...] = t1_r + t2_r
        oli_ref[...] = t1_i + t2_i
        w_r, w_i = _cmul(l_r, l_i, a_r, a_i)
        odt_ref[...] = dt * (w_r * g_r + w_i * g_i)

    v = jax.ShapeDtypeStruct((1, N_STATES), f32)
    m = jax.ShapeDtypeStruct((SSM_GROUP, N_STATES), f32)
    return pl.pallas_call(body, name="ssm_param_bwd", out_shape=[m, m, v, v, v])(
        dbb_r, dbb_i, bt_re, bt_im, kr, ki, ar, ai, lr, li, ldt, da_r, da_i)


def group_sum(v):
    def body(v_ref, o_ref):
        o_ref[...] = jnp.sum(v_ref[...], axis=-1, keepdims=True)
    return pl.pallas_call(body, name="group_sum", out_shape=jax.ShapeDtypeStruct((v.shape[0], 1), f32))(v)


ATTN_SCALE = XATTN_HEAD_DIM ** -0.5


def _attn_probs(q_h, k_h):
    s = lax.dot_general(q_h, k_h, NT_DIMS, preferred_element_type=f32) * ATTN_SCALE
    e = jnp.exp(s - jnp.max(s, axis=-1, keepdims=True))
    return e / jnp.sum(e, axis=-1, keepdims=True)


def attn_fwd(proj, kv, wxo3, tm=TM):
    S = proj.shape[0]
    M = kv.shape[0]
    J, _, n = wxo3.shape

    def body(q_ref, kv_ref, w_ref, o_ref, y_ref):
        for h in range(XATTN_HEADS):
            cols = slice(h * XATTN_HEAD_DIM, (h + 1) * XATTN_HEAD_DIM)
            q_h = q_ref[:, cols].astype(bf16)
            k_h = kv_ref[:, cols]
            v_h = kv_ref[:, XATTN_WIDTH + h * XATTN_HEAD_DIM:XATTN_WIDTH + (h + 1) * XATTN_HEAD_DIM]
            p = _attn_probs(q_h, k_h)
            o_ref[:, cols] = jnp.dot(p.astype(bf16), v_h, preferred_element_type=f32).astype(bf16)
        o = o_ref[...]
        for j in range(J):
            y_ref[:, j * n:(j + 1) * n] = jnp.dot(o, w_ref[j], preferred_element_type=f32).astype(bf16)

    return pl.pallas_call(
        body, name="attn_fwd", grid=(S // tm,),
        in_specs=[_rows(tm, XATTN_WIDTH, 10), _full((M, 2 * XATTN_WIDTH)), _full(wxo3.shape)],
        out_specs=[_rows(tm, XATTN_WIDTH), _rows(tm, J * n)],
        out_shape=[jax.ShapeDtypeStruct((S, XATTN_WIDTH), bf16), jax.ShapeDtypeStruct((S, J * n), bf16)],
        compiler_params=_params())(proj, kv, wxo3)


def attn_bwd(dy_c, o_att, wxo3, proj, kv, dproj, tm=TM):
    S = proj.shape[0]
    M = kv.shape[0]
    J, _, n = wxo3.shape
    nt = S // tm

    def body(dy_ref, o_ref, w_ref, q_ref, kv_ref, _, dq_ref, dkv_ref, dwo_ref, acc_ref):
        i = pl.program_id(0)

        @pl.when(i == 0)
        def _():
            dkv_ref[...] = jnp.zeros_like(dkv_ref)

        part = lax.dot_general(o_ref[...], dy_ref[...], TN_DIMS, preferred_element_type=f32)

        @pl.when(i == 0)
        def _():
            acc_ref[...] = part

        @pl.when(i > 0)
        def _():
            acc_ref[...] += part

        @pl.when(i == nt - 1)
        def _():
            for j in range(J):
                dwo_ref[j] = acc_ref[:, j * n:(j + 1) * n].astype(bf16)

        do = None
        for j in range(J):
            part = lax.dot_general(dy_ref[:, j * n:(j + 1) * n], w_ref[j], NT_DIMS, preferred_element_type=f32)
            do = part if do is None else do + part
        do = do.astype(bf16)
        for h in range(XATTN_HEADS):
            cols = slice(h * XATTN_HEAD_DIM, (h + 1) * XATTN_HEAD_DIM)
            vcols = slice(XATTN_WIDTH + h * XATTN_HEAD_DIM, XATTN_WIDTH + (h + 1) * XATTN_HEAD_DIM)
            q_h = q_ref[:, cols].astype(bf16)
            k_h = kv_ref[:, cols]
            v_h = kv_ref[:, vcols]
            do_h = do[:, cols]
            p = _attn_probs(q_h, k_h)
            dp = lax.dot_general(do_h, v_h, NT_DIMS, preferred_element_type=f32)
            ds = (p * (dp - jnp.sum(dp * p, axis=-1, keepdims=True)) * ATTN_SCALE).astype(bf16)
            dq_ref[:, cols] = jnp.dot(ds, k_h, preferred_element_type=f32).astype(bf16)
            dkv_ref[:, cols] += lax.dot_general(ds, q_h, TN_DIMS, preferred_element_type=f32)
            dkv_ref[:, vcols] += lax.dot_general(p.astype(bf16), do_h, TN_DIMS, preferred_element_type=f32)

    return pl.pallas_call(
        body, name="attn_bwd", grid=(S // tm,),
        in_specs=[_rows(tm, J * n), _rows(tm, XATTN_WIDTH), _full(wxo3.shape), _rows(tm, XATTN_WIDTH, 10),
                  _full((M, 2 * XATTN_WIDTH)), pl.BlockSpec(memory_space=pl.ANY)],
        out_specs=[_rows(tm, XATTN_WIDTH, 10), _full((M, 2 * XATTN_WIDTH)), _full(wxo3.shape)],
        out_shape=[jax.ShapeDtypeStruct(dproj.shape, bf16), jax.ShapeDtypeStruct((M, 2 * XATTN_WIDTH), f32),
                   jax.ShapeDtypeStruct(wxo3.shape, bf16)],
        scratch_shapes=[pltpu.VMEM((XATTN_WIDTH, J * n), f32)], input_output_aliases={5: 0},
        compiler_params=_params())(dy_c, o_att, wxo3, proj, kv, dproj)


def _ln_stats(r):
    mu = jnp.mean(r, axis=-1, keepdims=True)
    xc = r - mu
    var = jnp.mean(xc * xc, axis=-1, keepdims=True)
    rstd = lax.rsqrt(var + LN_EPS)
    return xc * rstd, rstd


def _ln_bwd(dy, xhat, rstd, g):
    dxh = dy * g
    return rstd * (dxh - jnp.mean(dxh, axis=-1, keepdims=True) - xhat * jnp.mean(dxh * xhat, axis=-1, keepdims=True))


def _colsum(v):
    return jnp.sum(v, axis=0, keepdims=True)


def _ln2_loss_epi(t2, h1, target, b_down, g, b):
    xhat, rstd = _ln_stats(ALPHA * h1 + t2 + b_down)
    err = xhat * g + b - target
    dout = err * (1.0 / D_MODEL)
    dr = _ln_bwd(dout, xhat, rstd, g)
    return (dr, dr), (0.5 * _colsum(err * err) * (1.0 / D_MODEL), _colsum(dout * xhat), _colsum(dout), _colsum(dr))


def _merge_fwd_pro(g0, g1, g2, y_a, glu_a, glu_b, y_c, b):
    gs = [_sig(g.astype(f32) + b[:, k * D_MODEL:(k + 1) * D_MODEL]) for k, g in enumerate((g0, g1, g2))]
    y_b = glu_a.astype(f32) * _sig(glu_b.astype(f32))
    return gs[0] * y_a.astype(f32) + gs[1] * y_b + gs[2] * y_c.astype(f32)


def _merge_bwd_epi(dm, g0, g1, g2, y_a, glu_a, glu_b, y_c, b):
    ga, sb = glu_a.astype(f32), _sig(glu_b.astype(f32))
    ys = (y_a.astype(f32), ga * sb, y_c.astype(f32))
    gs = [_sig(g.astype(f32) + b[:, k * D_MODEL:(k + 1) * D_MODEL]) for k, g in enumerate((g0, g1, g2))]
    dpre = [dm * ys[k] * gs[k] * (1.0 - gs[k]) for k in range(3)]
    dyb = dm * gs[1]
    dglu = jnp.concatenate([dyb * sb, dyb * ga * sb * (1.0 - sb)], axis=1)
    return ((jnp.concatenate(dpre, axis=1), dm * gs[0], dglu, dm * gs[2]),
            (jnp.concatenate([_colsum(d) for d in dpre], axis=1),))


def _ln1_fwd_epi(t1, x, g, b):
    xhat, _ = _ln_stats(ALPHA * x + t1)
    h = xhat * g + b
    return (t1, h, h), ()


def _ln1_bwd_epi(t3, x, t1, dr2, g):
    xhat, rstd = _ln_stats(ALPHA * x + t1)
    dh = ALPHA * dr2 + t3
    dr = _ln_bwd(dh, xhat, rstd, g)
    return (dr, dr), (_colsum(dh * xhat), _colsum(dh))


def exchange(*rides, name):
    return _pcall(lambda: None, [], name=name, grid=(), in_specs=[], out_specs=[], out_shape=[], rides=rides)[1]


ROW_TILE = 256


def sum_slots(recv, name):
    n, R, C = recv.shape
    tm = min(R, ROW_TILE)

    def body(r_ref, o_ref):
        acc = r_ref[0].astype(f32)
        for k in range(1, n):
            acc = acc + r_ref[k].astype(f32)
        o_ref[...] = acc

    return pl.pallas_call(
        body, name=name, grid=(R // tm,), in_specs=[pl.BlockSpec((n, tm, C), lambda i: (0, i, 0))],
        out_specs=_rows(tm, C), out_shape=jax.ShapeDtypeStruct((R, C), f32), compiler_params=_params())(recv)


def _adamw(w, g, m, v):
    m = ADAM_B1 * m + (1.0 - ADAM_B1) * g
    v = ADAM_B2 * v + (1.0 - ADAM_B2) * (g * g)
    m_hat = m / (1.0 - ADAM_B1 ** ADAM_STEP)
    v_hat = v / (1.0 - ADAM_B2 ** ADAM_STEP)
    delta = -ADAM_LR * (m_hat / (jnp.sqrt(v_hat) + ADAM_EPS) + ADAM_WD * w)
    return delta, m, v


def adam_pair(p, q, w, m, v, name):
    R, C = w.shape
    tm = min(R, ROW_TILE)

    def body(p_ref, q_ref, w_ref, m_ref, v_ref, g_ref, d_ref, nm_ref, nv_ref):
        g = p_ref[...] + q_ref[...]
        g_ref[...] = g
        d_ref[...], nm_ref[...], nv_ref[...] = _adamw(w_ref[...], g, m_ref[...], v_ref[...])

    o = jax.ShapeDtypeStruct((R, C), f32)
    return pl.pallas_call(body, name=name, grid=(R // tm,), in_specs=[_rows(tm, C)] * 5,
                          out_specs=[_rows(tm, C)] * 4, out_shape=[o] * 4, compiler_params=_params())(p, q, w, m, v)


def adam_slots(recv, w, m, v, name):
    n = recv.shape[0]

    def body(r_ref, w_ref, m_ref, v_ref, g_ref, d_ref, nm_ref, nv_ref):
        g = r_ref[0]
        for k in range(1, n):
            g = g + r_ref[k]
        g_ref[...] = g
        d_ref[...], nm_ref[...], nv_ref[...] = _adamw(w_ref[...], g, m_ref[...], v_ref[...])

    o = jax.ShapeDtypeStruct(w.shape, f32)
    return pl.pallas_call(body, name=name, out_shape=[o] * 4, compiler_params=_params())(recv, w, m, v)


BIG = ("w_in", "w_conv_out", "w_glu", "w_xattn_out", "w_kv", "w_out", "w_up", "w_down")
MID = ("w_conv_out", "w_glu", "w_xattn_out", "w_kv", "w_out")
SMALL = ("b_gate", "ssm_lam_re", "ssm_lam_im", "ssm_log_dt", "ssm_b_re", "ssm_b_im", "ssm_c_re", "ssm_c_im", "ssm_d",
         "ln1_g", "ln1_b", "b_up", "b_down", "ln2_g", "ln2_b")
def _pad_flat(a, mult=1024):
    a = a.reshape(-1)
    return jnp.pad(a, (0, (-a.shape[0]) % mult))


def _perm(a):
    S, W = a.shape
    return a.reshape(N_SEG, S // N_SEG, W).transpose(1, 0, 2).reshape(S, W)


def _unperm(a):
    S, W = a.shape
    return a.reshape(S // N_SEG, N_SEG, W).transpose(1, 0, 2).reshape(S, W)


def _state_rows(a):
    return a.transpose(2, 0, 1).reshape(SSM_GROUP, N_STATES)


def _block_diag_b(bt):
    b4 = bt.reshape(SSM_GROUP, 4, 8, SSM_STATE)
    eye = jnp.eye(8, dtype=bt.dtype)
    return jnp.einsum("hcgp,gk->cghkp", b4, eye).reshape(4, 128, 512)


def _block_diag_c(c):
    c4 = c.reshape(4, 8, SSM_GROUP, SSM_STATE)
    eye = jnp.eye(8, dtype=c.dtype)
    return jnp.einsum("cghp,gk->cgpkh", c4, eye).reshape(4, 512, 128)


def _diag_b(acc):
    a = acc.reshape(4, 8, SSM_STATE, 8, SSM_GROUP)
    eye = jnp.eye(8, dtype=acc.dtype)
    return jnp.einsum("cgpkh,gk->hcgp", a, eye).reshape(SSM_GROUP, N_STATES)


def _diag_c(acc):
    a = acc.reshape(4, 8, SSM_GROUP, 8, SSM_STATE)
    eye = jnp.eye(8, dtype=acc.dtype)
    return jnp.einsum("cghkp,gk->cghp", a, eye).reshape(SSM_GROUPS, SSM_GROUP, SSM_STATE)


def kernel(x, mem, w_in, b_gate, conv_w, w_conv_out, ssm_lam_re, ssm_lam_im, ssm_log_dt, ssm_b_re, ssm_b_im, ssm_c_re, ssm_c_im, ssm_d, w_glu, w_kv, w_xattn_out, w_out, ln1_g, ln1_b, w_up, b_up, w_down, b_down, ln2_g, ln2_b, loss_target, m_w_in, m_b_gate, m_conv_w, m_w_conv_out, m_ssm_lam_re, m_ssm_lam_im, m_ssm_log_dt, m_ssm_b_re, m_ssm_b_im, m_ssm_c_re, m_ssm_c_im, m_ssm_d, m_w_glu, m_w_kv, m_w_xattn_out, m_w_out, m_ln1_g, m_ln1_b, m_w_up, m_b_up, m_w_down, m_b_down, m_ln2_g, m_ln2_b, v_w_in, v_b_gate, v_conv_w, v_w_conv_out, v_ssm_lam_re, v_ssm_lam_im, v_ssm_log_dt, v_ssm_b_re, v_ssm_b_im, v_ssm_c_re, v_ssm_c_im, v_ssm_d, v_w_glu, v_w_kv, v_w_xattn_out, v_w_out, v_ln1_g, v_ln1_b, v_w_up, v_b_up, v_w_down, v_b_down, v_ln2_g, v_ln2_b):
    W = dict(w_in=w_in, b_gate=b_gate, conv_w=conv_w, w_conv_out=w_conv_out, ssm_lam_re=ssm_lam_re,
             ssm_lam_im=ssm_lam_im, ssm_log_dt=ssm_log_dt, ssm_b_re=ssm_b_re, ssm_b_im=ssm_b_im, ssm_c_re=ssm_c_re,
             ssm_c_im=ssm_c_im, ssm_d=ssm_d, w_glu=w_glu, w_kv=w_kv, w_xattn_out=w_xattn_out, w_out=w_out,
             ln1_g=ln1_g, ln1_b=ln1_b, w_up=w_up, b_up=b_up, w_down=w_down, b_down=b_down, ln2_g=ln2_g, ln2_b=ln2_b)
    MOM = dict(w_in=m_w_in, b_gate=m_b_gate, conv_w=m_conv_w, w_conv_out=m_w_conv_out, ssm_lam_re=m_ssm_lam_re,
               ssm_lam_im=m_ssm_lam_im, ssm_log_dt=m_ssm_log_dt, ssm_b_re=m_ssm_b_re, ssm_b_im=m_ssm_b_im,
               ssm_c_re=m_ssm_c_re, ssm_c_im=m_ssm_c_im, ssm_d=m_ssm_d, w_glu=m_w_glu, w_kv=m_w_kv,
               w_xattn_out=m_w_xattn_out, w_out=m_w_out, ln1_g=m_ln1_g, ln1_b=m_ln1_b, w_up=m_w_up, b_up=m_b_up,
               w_down=m_w_down, b_down=m_b_down, ln2_g=m_ln2_g, ln2_b=m_ln2_b)
    VEL = dict(w_in=v_w_in, b_gate=v_b_gate, conv_w=v_conv_w, w_conv_out=v_w_conv_out, ssm_lam_re=v_ssm_lam_re,
               ssm_lam_im=v_ssm_lam_im, ssm_log_dt=v_ssm_log_dt, ssm_b_re=v_ssm_b_re, ssm_b_im=v_ssm_b_im,
               ssm_c_re=v_ssm_c_re, ssm_c_im=v_ssm_c_im, ssm_d=v_ssm_d, w_glu=v_w_glu, w_kv=v_w_kv,
               w_xattn_out=v_w_xattn_out, w_out=v_w_out, ln1_g=v_ln1_g, ln1_b=v_ln1_b, w_up=v_w_up, b_up=v_b_up,
               w_down=v_w_down, b_down=v_b_down, ln2_g=v_ln2_g, ln2_b=v_ln2_b)
    names = list(W)
    xy = 2 * lax.axis_index("x") + lax.axis_index("y")

    xs = x[0]
    S = xs.shape[0]
    mems = mem[0]
    tgt = loss_target[0]

    shard_bf = {n: W[n][0].astype(bf16) for n in BIG}

    lr = ssm_lam_re.reshape(1, N_STATES)
    li = ssm_lam_im.reshape(1, N_STATES)
    ldt = jnp.repeat(ssm_log_dt.reshape(SSM_GROUPS), SSM_STATE).reshape(1, N_STATES)
    bt_re, bt_im = _state_rows(ssm_b_re[0]), _state_rows(ssm_b_im[0])
    ar, ai, kr, ki, bbt_r, bbt_i = ssm_prep(lr, li, ldt, bt_re, bt_im)
    bd_r, bd_i = _block_diag_b(bbt_r).astype(bf16), _block_diag_b(bbt_i).astype(bf16)
    cd_r, cd_i = _block_diag_c(ssm_c_re[0]).astype(bf16), _block_diag_c(ssm_c_im[0]).astype(bf16)
    bdt_r, bdt_i = bd_r.transpose(0, 2, 1), bd_i.transpose(0, 2, 1)
    cdt_r, cdt_i = cd_r.transpose(0, 2, 1), cd_i.transpose(0, 2, 1)
    d_skip = ssm_d.reshape(1, SSM_WIDTH)

    mem_bf = mems.astype(bf16)
    u_col = GATE_COLS + 3 * CONV_WIDTH
    me_xy = jnp.reshape(xy, (1,)).astype(jnp.int32)
    proj, x_bf, (win3,) = proj_own(xs, shard_bf["w_in"], me_xy, 4, rides=(("gather2", shard_bf["w_in"]),))
    proj, (wco3, wglu3, wxo3, wkv4, wout4, convw4) = proj_rest(
        x_bf, win3, me_xy, proj, rides=tuple(("gather", shard_bf[n]) for n in MID) + (("gather", conv_w[0]),))
    wkv3 = wkv4.reshape(1, D_MODEL, 2 * XATTN_WIDTH)
    wout3 = wout4.reshape(1, D_MODEL, D_MODEL)
    convw_full = convw4.transpose(1, 0, 2).reshape(3, CONV_WIDTH)
    a_conv, y_a = conv_fwd(proj, convw_full, wco3)
    u_perm = _perm(proj[:, u_col:u_col + SSM_WIDTH])
    (init_r, init_i), (wup3,) = ssm_scan_fwd(u_perm, bd_r, bd_i, ar, ai, rides=(("gather2", shard_bf["w_up"]),))
    ysm_perm, ys_perm, st_r, st_i = ssm_scan_fwd(u_perm, bd_r, bd_i, ar, ai,
                                                 final=(cd_r, cd_i, d_skip, init_r, init_i))
    y_s = _unperm(ys_perm)
    glu = mm_fwd_small(y_s, wglu3, "glu", out_dtype=bf16)
    kv = mm_fwd_small(mem_bf, wkv3, "kv", out_dtype=bf16)
    o_att, y_c = attn_fwd(proj, kv, wxo3)
    merge_rows = ((proj, D_MODEL, 0), (proj, D_MODEL, 1), (proj, D_MODEL, 2), y_a, (glu, D_MODEL, 0),
                  (glu, D_MODEL, 1), y_c)
    merged, t1, h1, h1_bf = mm_rows_fused(
        (_merge_fwd_pro, merge_rows, (b_gate,)), wout3, "merge_w_out_ln1", transposed=False, tm=256,
        epi=_ln1_fwd_epi, extras=(xs,), consts=(ln1_g, ln1_b), outs=(f32, f32, bf16))
    (r_up, hdn), (wdn4,) = mm_fwd(h1_bf, wup3, "w_up", bias=b_up, outs=(bf16, bf16), second=_relu2,
                                  first=lambda v: jnp.maximum(v, 0.0), rides=(("gather2", shard_bf["w_down"]),))
    wdn3 = wdn4.reshape(1, D_FF, D_MODEL)

    dr2, dr2_bf, loss_cols, d_ln2_g, d_ln2_b, d_b_down = mm_rows_fused(
        hdn, wdn3, "w_down_ln2_loss", transposed=False, tm=512, epi=_ln2_loss_epi, extras=(h1, tgt),
        consts=(b_down, ln2_g, ln2_b), outs=(f32, bf16), nsums=4)
    part, other = {}, {}
    g_w_down = mm_bwd_w(hdn, dr2_bf, 1, "dw_down", tk=1024).reshape(4, -1, D_MODEL)
    dup, d_b_up = mm_bwd_x(dr2_bf, wdn3, "dup", tm=512, extras=(r_up,), colsum=True, out_dtype=bf16,
                           epi=lambda acc, r: acc * (2.0 * r.astype(f32)))
    g_w_up = mm_bwd_w(h1_bf, dup, 4, "dw_up")
    (dr1, dr1_bf, d_ln1_g, d_ln1_b), (recv_dn,) = mm_rows_fused(
        dup, wup3, "dh1_ln1_bwd", transposed=True, tm=512, epi=_ln1_bwd_epi, extras=(xs, t1, dr2), consts=(ln1_g,),
        outs=(f32, bf16), nsums=2, rides=(("scatter", g_w_down),))
    part["w_down"] = sum_slots(recv_dn, "sum_w_down")
    g_w_out = mm_bwd_w(merged, dr1_bf, 1, "dw_out").reshape(4, -1, D_MODEL)
    (dproj, dy_a, dglu, dy_c, d_b_gate), (recv_up,) = mm_rows_fused(
        dr1_bf, wout3, "dmerged_merge_bwd", transposed=True, tm=256, epi=_merge_bwd_epi, extras=merge_rows,
        consts=(b_gate,), outs=((bf16, GATE_COLS, IN_COLS), bf16, (bf16, 2 * D_MODEL, 2 * D_MODEL), bf16),
        sum_widths=(GATE_COLS,), rides=(("scatter", g_w_up),))
    part["w_up"] = sum_slots(recv_up, "sum_w_up")

    dproj, d_conv_w, g_w_co = conv_bwd(dy_a, a_conv, wco3, proj, convw_full, dproj)

    g_w_glu = mm_bwd_w_small(y_s, dglu, 4, "dw_glu")
    dys_perm = _perm(mm_bwd_x(dglu, wglu3, "dy_s", tm=1024))
    linit_r, linit_i = ssm_scan_bwd(dys_perm, ysm_perm, cdt_r, cdt_i, ar, ai)
    (du_perm, dbacc_r, dbacc_i, dcacc_r, dcacc_i, da_r, da_i, d_ssm_d), (recv_co, recv_glu, recv_out) = ssm_scan_bwd(
        dys_perm, ysm_perm, cdt_r, cdt_i, ar, ai,
        final=(u_perm, st_r, st_i, bdt_r, bdt_i, d_skip, linit_r, linit_i),
        rides=(("scatter", g_w_co), ("scatter", g_w_glu), ("scatter", g_w_out)))
    part["w_out"] = sum_slots(recv_out, "sum_w_out")
    part["w_conv_out"] = sum_slots(recv_co, "sum_w_conv_out")
    part["w_glu"] = sum_slots(recv_glu, "sum_w_glu")
    dbt_re, dbt_im, d_lr, d_li, d_ldt_state = ssm_param_bwd(
        _diag_b(dbacc_r), _diag_b(dbacc_i), bt_re, bt_im, kr, ki, ar, ai, lr, li, ldt, da_r, da_i)
    d_log_dt = group_sum(d_ldt_state.reshape(SSM_GROUPS, SSM_STATE))
    d_b_re = dbt_re.reshape(SSM_GROUP, SSM_GROUPS, SSM_STATE).transpose(1, 2, 0)
    d_b_im = dbt_im.reshape(SSM_GROUP, SSM_GROUPS, SSM_STATE).transpose(1, 2, 0)
    d_c_re = _diag_c(dcacc_r)
    d_c_im = -_diag_c(dcacc_i)

    dproj, dkv, g_w_xo = attn_bwd(dy_c, o_att, wxo3, proj, kv, dproj)
    g_w_kv = mm_bwd_w(mem_bf, dkv, 1, "dw_kv").reshape(4, -1, D_MODEL)

    dproj = lax.dynamic_update_slice(dproj, _unperm(du_perm), (0, u_col))
    g_small = {"b_gate": d_b_gate, "ssm_lam_re": d_lr, "ssm_lam_im": d_li, "ssm_log_dt": d_log_dt, "ssm_b_re": d_b_re,
               "ssm_b_im": d_b_im, "ssm_c_re": d_c_re, "ssm_c_im": d_c_im, "ssm_d": d_ssm_d, "ln1_g": d_ln1_g,
               "ln1_b": d_ln1_b, "b_up": d_b_up, "b_down": d_b_down, "ln2_g": d_ln2_g, "ln2_b": d_ln2_b}
    small_names = SMALL + ("conv_w",)
    g_small["conv_w"] = d_conv_w
    sizes = {n: (g_small[n].size + 1023) // 1024 * 1024 for n in small_names}
    pack = lambda d: jnp.concatenate([_pad_flat(d[n]) for n in small_names]).reshape(-1, 128)
    early = ("w_down", "w_up", "w_out", "w_conv_out", "w_glu")
    g_w_in, landed = mm_bwd_w(
        x_bf, dproj, 4, "dw_in", rides=(("scatter", g_w_xo), ("scatter", g_w_kv), ("all", pack(g_small)))
        + tuple(("pair", part[n]) for n in early))
    recv_xo, recv_kv, srecv = landed[:3]
    other.update(zip(early, landed[3:]))
    part["w_xattn_out"] = sum_slots(recv_xo, "sum_w_xattn_out")
    part["w_kv"] = sum_slots(recv_kv, "sum_w_kv")
    dx, (recv_in, other["w_xattn_out"], other["w_kv"]) = mm_bwd_x(
        dproj, win3, "dx", extras=(dr1,), epi=lambda acc, d: acc + ALPHA * d,
        rides=(("scatter", g_w_in), ("pair", part["w_xattn_out"]), ("pair", part["w_kv"])))
    part["w_in"] = sum_slots(recv_in, "sum_w_in")
    other["w_in"], = exchange(("pair", part["w_in"]), name="swap_w_in")
    res = [{}, {}, {}, {}]
    for n in BIG:
        for k, r in enumerate(adam_pair(part[n], other[n], W[n][0], MOM[n][0], VEL[n][0], "adam_" + n)):
            res[k][n] = r[None]
    conv_zero = jnp.zeros((3, CONV_WIDTH), f32)
    gs, ds_, ms, vs = adam_slots(srecv, pack({**{n: W[n] for n in SMALL}, "conv_w": conv_zero}),
                                 pack({**{n: MOM[n] for n in SMALL}, "conv_w": conv_zero}),
                                 pack({**{n: VEL[n] for n in SMALL}, "conv_w": conv_zero}), "adam_small")

    def unpack_small(buf):
        flat = buf.reshape(-1)
        out, r = {}, 0
        for n in small_names:
            ref = g_small[n] if n == "conv_w" else W[n]
            out[n] = flat[r:r + ref.size].reshape(ref.shape)
            r += sizes[n]
        return out

    res_s = [unpack_small(b) for b in (gs, ds_, ms, vs)]
    g_conv = lax.dynamic_slice(res_s[0]["conv_w"], (0, xy * 128), (3, 128))
    conv_slots = g_conv.reshape(1, 3, 128)
    cg, cd, cm, cv = adam_slots(conv_slots, conv_w[0], m_conv_w[0], v_conv_w[0], "adam_conv")
    conv_res = [cg, cd, cm, cv]

    loss = lax.psum(jnp.sum(loss_cols), ("x", "y", "c"))
    outs = [loss, dx.reshape(x.shape)]
    for k in range(4):
        for n in names:
            if n == "conv_w":
                outs.append(conv_res[k].reshape(conv_w.shape))
            elif n in BIG:
                outs.append(res[k][n])
            else:
                outs.append(res_s[k][n])
    return tuple(outs)
```

```python
import functools
import math

import jax
import jax.numpy as jnp
from jax import lax
from jax.experimental import pallas as pl
from jax.experimental.pallas import tpu as pltpu

f32 = jnp.float32
bf16 = jnp.bfloat16

D_MODEL = 1024
CONV_WIDTH = 512
SSM_WIDTH = 512
SSM_GROUP = 16
SSM_GROUPS = 32
SSM_STATE = 64
N_STATES = SSM_GROUPS * SSM_STATE
XATTN_HEADS = 4
XATTN_HEAD_DIM = 128
XATTN_WIDTH = 512
D_FF = 4096
GATE_COLS = 3 * D_MODEL
IN_COLS = GATE_COLS + 3 * CONV_WIDTH + SSM_WIDTH + XATTN_WIDTH
ALPHA = 2.0 ** 0.25
LN_EPS = 1e-5
ADAM_LR = 0.001
ADAM_B1 = 0.9
ADAM_B2 = 0.999
ADAM_EPS = 1e-08
ADAM_WD = 0.01
ADAM_STEP = 10

N_SEG = 8
SCAN_ROWS = 512
LANE_STRIP = 512
TM = 512
VMEM_LIMIT = 48 * 1024 * 1024
MESH = pl.DeviceIdType.MESH

NT_DIMS = (((1,), (1,)), ((), ()))
TN_DIMS = (((0,), (0,)), ((), ()))


def _params():
    return pltpu.CompilerParams(vmem_limit_bytes=VMEM_LIMIT)


def _full(shape):
    n = len(shape)
    return pl.BlockSpec(shape, lambda *_: (0,) * n)


def _rows(tm, w, cb=0):
    return pl.BlockSpec((tm, w), lambda i: (i, cb))


def _sig(x):
    return 1.0 / (1.0 + jnp.exp(-x))


HBM_SPEC = pl.BlockSpec(memory_space=pl.ANY)
RIDE_PEERS = {"gather": 3, "gather2": 6, "scatter": 3, "all": 7, "pair": 1}


def _xy_peers(x, y):
    return [(1 - x, y), (x, 1 - y), (1 - x, 1 - y)]


def _ride_copies(kind, src, dst, send_sems, recv_sems, local_sem):
    x, y, c = lax.axis_index("x"), lax.axis_index("y"), lax.axis_index("c")
    me = 2 * x + y
    if kind == "all":
        flips = [(fx, fy, fc) for fx in (0, 1) for fy in (0, 1) for fc in (0, 1)][1:]
        peers = [(x ^ fx, y ^ fy, c ^ fc) for fx, fy, fc in flips]
        slot = lambda p: 4 * p[0] + 2 * p[1] + p[2]
        mine = slot((x, y, c))
    else:
        peers = [(px, py, c) for px, py in _xy_peers(x, y)]
        slot = lambda p: 2 * p[0] + p[1]
        mine = me

    def remote(k, s, d):
        return pltpu.make_async_remote_copy(src_ref=s, dst_ref=d, send_sem=send_sems.at[k], recv_sem=recv_sems.at[k],
                                            device_id=peers[k], device_id_type=MESH)

    if kind == "pair":
        peers = [(x, y, 1 - c)]
        return None, [remote(0, src, dst)], [remote(0, src, dst)]
    if kind == "scatter":
        local = pltpu.make_async_copy(src.at[me], dst.at[me], local_sem)
        sends = [remote(k, src.at[slot(p)], dst.at[me]) for k, p in enumerate(peers)]
        lands = [remote(k, src.at[me], dst.at[slot(p)]) for k, p in enumerate(peers)]
    else:
        local = pltpu.make_async_copy(src, dst.at[mine], local_sem)
        sends = [remote(k, src, dst.at[mine]) for k, p in enumerate(peers)]
        lands = [remote(k, src, dst.at[slot(p)]) for k, p in enumerate(peers)]
    return local, sends, lands


def _ride_shape(kind, src):
    lead = {"gather": (4,), "gather2": (4,), "scatter": (), "all": (8,), "pair": ()}[kind]
    return jax.ShapeDtypeStruct(lead + src.shape, src.dtype)


def _two_level(src, dst, send_sems, recv_sems, local_sem):
    x, y, c = lax.axis_index("x"), lax.axis_index("y"), lax.axis_index("c")
    me = 2 * x + y
    h = src.shape[0] // 2
    mine = pl.ds(pl.multiple_of(c * h, 16), h)
    theirs = pl.ds(pl.multiple_of((1 - c) * h, 16), h)
    peers = _xy_peers(x, y)
    slots = [2 * px + py for px, py in peers]

    def over_ici(k, d):
        return pltpu.make_async_remote_copy(src_ref=src.at[mine], dst_ref=d, send_sem=send_sems.at[k],
                                            recv_sem=recv_sems.at[k], device_id=(*peers[k], c), device_id_type=MESH)

    def over_d2d(k, rows):
        blk = dst.at[slots[k], rows]
        return pltpu.make_async_remote_copy(src_ref=blk, dst_ref=blk, send_sem=send_sems.at[3 + k],
                                            recv_sem=recv_sems.at[3 + k], device_id=(x, y, 1 - c), device_id_type=MESH)

    return (pltpu.make_async_copy(src, dst.at[me], local_sem),
            [over_ici(k, dst.at[me, mine]) for k in range(3)], [over_ici(k, dst.at[slots[k], mine]) for k in range(3)],
            [over_d2d(k, mine) for k in range(3)], [over_d2d(k, theirs) for k in range(3)])


def _pcall(body, args, *, name, grid, in_specs, out_specs, out_shape, scratch_shapes=(), rides=(), aliases=None,
           prefetch=()):
    n_in, n_out, n_scr, nr, npf = len(in_specs), len(out_specs), len(scratch_shapes), len(rides), len(prefetch)
    kinds = [k for k, _ in rides]

    def wrapped(*refs):
        pre, refs = refs[:npf], refs[npf:]
        ins, rsrc = refs[:n_in], refs[n_in:n_in + nr]
        outs = refs[n_in + nr:n_in + nr + n_out]
        rdst = refs[n_in + nr + n_out:n_in + 2 * nr + n_out]
        scr = refs[n_in + 2 * nr + n_out:n_in + 2 * nr + n_out + n_scr]
        sems = refs[n_in + 2 * nr + n_out + n_scr:]

        def start():
            for r, kind in enumerate(kinds):
                if kind == "gather2":
                    local, sends, _, _, _ = _two_level(rsrc[r], rdst[r], *sems[3 * r:3 * r + 3])
                else:
                    local, sends, _ = _ride_copies(kind, rsrc[r], rdst[r], *sems[3 * r:3 * r + 3])
                if local is not None:
                    local.start()
                for cp in sends:
                    cp.start()

        def finish():
            for r, kind in enumerate(kinds):
                if kind == "gather2":
                    local, sends, ici_lands, forwards, lands = _two_level(rsrc[r], rdst[r], *sems[3 * r:3 * r + 3])
                    for k in range(3):
                        ici_lands[k].wait_recv()
                        forwards[k].start()
                    sends = sends + forwards
                else:
                    local, sends, lands = _ride_copies(kind, rsrc[r], rdst[r], *sems[3 * r:3 * r + 3])
                for cp in lands:
                    cp.wait_recv()
                for cp in sends:
                    cp.wait_send()
                if local is not None:
                    local.wait()

        if nr and grid:
            ids = [pl.program_id(a) for a in range(len(grid))]
            first = functools.reduce(jnp.logical_and, [i == 0 for i in ids])
            last = functools.reduce(jnp.logical_and, [i == g - 1 for i, g in zip(ids, grid)])
            pl.when(first)(start)
        elif nr:
            start()
        body(*pre, *ins, *outs, *scr)
        if nr and grid:
            pl.when(last)(finish)
        elif nr:
            finish()

    sems = []
    for kind in kinds:
        n = RIDE_PEERS[kind]
        sems += [pltpu.SemaphoreType.DMA((n,)), pltpu.SemaphoreType.DMA((n,)), pltpu.SemaphoreType.DMA(())]
    all_in = list(in_specs) + [HBM_SPEC] * nr
    all_out = list(out_specs) + [HBM_SPEC] * nr
    all_scratch = list(scratch_shapes) + sems
    if npf:
        how = dict(grid_spec=pltpu.PrefetchScalarGridSpec(num_scalar_prefetch=npf, grid=grid, in_specs=all_in,
                                                          out_specs=all_out, scratch_shapes=all_scratch))
    else:
        how = dict(grid=grid, in_specs=all_in, out_specs=all_out, scratch_shapes=all_scratch)
    res = pl.pallas_call(
        wrapped, name=name, out_shape=list(out_shape) + [_ride_shape(k, s) for k, s in rides],
        input_output_aliases=aliases or {}, compiler_params=_params(), **how)(
            *prefetch, *args, *[s for _, s in rides])
    return list(res[:n_out]), list(res[n_out:])


def _bf(v):
    return v if v.dtype == bf16 else v.astype(bf16)


def mm_fwd_small(a, w3, name, *, tm=1024, out_dtype=f32):
    M, K = a.shape
    J, _, n = w3.shape
    tm = min(tm, M)

    def body(a_ref, w_ref, o_ref):
        av = _bf(a_ref[...])
        for j in range(J):
            o_ref[:, j * n:(j + 1) * n] = jnp.dot(av, w_ref[j], preferred_element_type=f32).astype(out_dtype)

    return pl.pallas_call(
        body, name=name, grid=(M // tm,), in_specs=[_rows(tm, K), _full((J, K, n))], out_specs=_rows(tm, J * n),
        out_shape=jax.ShapeDtypeStruct((M, J * n), out_dtype), compiler_params=_params())(a, w3)


def mm_bwd_w_small(a, dy, J, name, *, tm=2048):
    M, K = a.shape
    n = dy.shape[1] // J
    tm = min(tm, M)
    ns = M // tm

    def body(a_ref, dy_ref, o_ref, acc_ref):
        s = pl.program_id(0)
        part = lax.dot_general(_bf(a_ref[...]), _bf(dy_ref[...]), TN_DIMS, preferred_element_type=f32)

        @pl.when(s == 0)
        def _():
            acc_ref[...] = part

        @pl.when(s > 0)
        def _():
            acc_ref[...] += part

        @pl.when(s == ns - 1)
        def _():
            for j in range(J):
                o_ref[j] = acc_ref[:, j * n:(j + 1) * n].astype(bf16)

    return pl.pallas_call(
        body, name=name, grid=(ns,), in_specs=[_rows(tm, K), _rows(tm, J * n)], out_specs=_full((J, K, n)),
        out_shape=jax.ShapeDtypeStruct((J, K, n), bf16), scratch_shapes=[pltpu.VMEM((K, J * n), f32)],
        compiler_params=_params())(a, dy)


def mm_fwd(a, w3, name, *, tm=1024, tn=None, bias=None, outs=(f32,), first=None, second=None, rides=()):
    M, K = a.shape
    J, _, n = w3.shape
    tm = min(tm, M)
    tn = tn or n
    nl = n // tn
    nb = 0 if bias is None else 1

    def body(*refs):
        a_ref, w_ref = refs[0], refs[1]
        acc = jnp.dot(_bf(a_ref[...]), w_ref[0], preferred_element_type=f32)
        if bias is not None:
            acc = acc + refs[2][...]
        refs[2 + nb][...] = (acc if first is None else first(acc)).astype(outs[0])
        if len(outs) > 1:
            refs[3 + nb][...] = second(acc).astype(outs[1])

    in_specs = [pl.BlockSpec((tm, K), lambda j, l, i: (i, 0)),
                pl.BlockSpec((1, K, tn), lambda j, l, i: (j, 0, l))]
    args = [a, w3]
    if bias is not None:
        in_specs.append(pl.BlockSpec((1, tn), lambda j, l, i: (0, j * nl + l)))
        args.append(bias)
    res, landed = _pcall(
        body, args, name=name, grid=(J, nl, M // tm), in_specs=in_specs,
        out_specs=[pl.BlockSpec((tm, tn), lambda j, l, i: (i, j * nl + l)) for _ in outs],
        out_shape=[jax.ShapeDtypeStruct((M, J * n), dt) for dt in outs], rides=rides)
    res = res if len(outs) > 1 else res[0]
    return (res, landed) if rides else res


def proj_own(x, w_own, me, n_blocks, *, tm=1024, rides=()):
    M, K = x.shape
    n = w_own.shape[1]

    def body(me_ref, x_ref, w_ref, o_ref, xb_ref):
        av = _bf(x_ref[...])
        xb_ref[...] = av
        o_ref[...] = jnp.dot(av, w_ref[...], preferred_element_type=f32).astype(bf16)

    (proj, x_bf), landed = _pcall(
        body, [x, w_own], name="proj_own", grid=(M // tm,), prefetch=(me,),
        in_specs=[pl.BlockSpec((tm, K), lambda i, me_ref: (i, 0)), pl.BlockSpec((K, n), lambda i, me_ref: (0, 0))],
        out_specs=[pl.BlockSpec((tm, n), lambda i, me_ref: (i, me_ref[0])),
                   pl.BlockSpec((tm, K), lambda i, me_ref: (i, 0))],
        out_shape=[jax.ShapeDtypeStruct((M, n_blocks * n), bf16), jax.ShapeDtypeStruct((M, K), bf16)], rides=rides)
    return proj, x_bf, landed


def proj_rest(x_bf, w3, me, proj, *, tm=1024, rides=()):
    M, K = x_bf.shape
    J, _, n = w3.shape

    def body(me_ref, x_ref, w_ref, _, o_ref):
        o_ref[...] = jnp.dot(x_ref[...], w_ref[0], preferred_element_type=f32).astype(bf16)

    (proj,), landed = _pcall(
        body, [x_bf, w3, proj], name="proj_rest", grid=(J - 1, M // tm), prefetch=(me,),
        in_specs=[pl.BlockSpec((tm, K), lambda k, i, me_ref: (i, 0)),
                  pl.BlockSpec((1, K, n), lambda k, i, me_ref: (me_ref[0] ^ (k + 1), 0, 0)), HBM_SPEC],
        out_specs=[pl.BlockSpec((tm, n), lambda k, i, me_ref: (i, me_ref[0] ^ (k + 1)))],
        out_shape=[jax.ShapeDtypeStruct(proj.shape, bf16)], aliases={3: 0}, rides=rides)
    return proj, landed


def mm_bwd_x(dy, w3, name, *, tm=TM, epi=None, extras=(), colsum=False, out_dtype=f32, rides=()):
    M = dy.shape[0]
    J, K, n = w3.shape
    tm = min(tm, M)
    nex = len(extras)

    def body(*refs):
        dy_ref, w_hbm = refs[0], refs[1]
        ex = refs[2:2 + nex]
        o_ref = refs[2 + nex]
        w_ref = refs[-1]
        i = pl.program_id(0)

        @pl.when(i == 0)
        def _():
            pltpu.sync_copy(w_hbm, w_ref)

        acc = None
        for j in range(J):
            part = lax.dot_general(_bf(dy_ref[:, j * n:(j + 1) * n]), w_ref[j], NT_DIMS, preferred_element_type=f32)
            acc = part if acc is None else acc + part
        if epi is not None:
            acc = epi(acc, *[e[...] for e in ex])
        o_ref[...] = acc.astype(out_dtype)
        if colsum:
            s_ref = refs[3 + nex]
            cs = jnp.sum(acc, axis=0, keepdims=True)

            @pl.when(i == 0)
            def _():
                s_ref[...] = cs

            @pl.when(i > 0)
            def _():
                s_ref[...] += cs

    in_specs = [pl.BlockSpec((tm, J * n), lambda i: (i, 0)), pl.BlockSpec(memory_space=pl.ANY)]
    in_specs += [pl.BlockSpec((tm, K), lambda i: (i, 0)) for _ in extras]
    out_specs = [pl.BlockSpec((tm, K), lambda i: (i, 0))]
    out_shape = [jax.ShapeDtypeStruct((M, K), out_dtype)]
    if colsum:
        out_specs.append(pl.BlockSpec((1, K), lambda i: (0, 0)))
        out_shape.append(jax.ShapeDtypeStruct((1, K), f32))
    res, landed = _pcall(body, [dy, w3, *extras], name=name, grid=(M // tm,), in_specs=in_specs, out_specs=out_specs,
                         out_shape=out_shape, scratch_shapes=[pltpu.VMEM((J, K, n), bf16)], rides=rides)
    res = res if colsum else res[0]
    return (res, landed) if rides else res


def mm_rows_fused(a, w3, name, *, transposed, tm, epi, extras=(), consts=(), outs=(f32,), nsums=0, sum_widths=None,
                  rides=()):
    pro = a[0] if isinstance(a, tuple) else None
    J, K, n = w3.shape
    W = K if transposed else n
    ka = J * n if transposed else K
    if pro is not None:
        a_rows = [e if isinstance(e, tuple) else (e, ka, 0) for e in a[1]]
        a_consts = list(a[2])
    else:
        a_rows, a_consts = [(a, ka, 0)], []
    M = a_rows[0][0].shape[0]
    tm = min(tm, M)
    na = len(a_rows) + len(a_consts)
    nex, nco = len(extras), len(consts)
    if pro is not None:
        outs = (bf16 if ka == W else (bf16, ka, ka),) + tuple(outs)
    nout = len(outs)
    nsums = len(sum_widths) if sum_widths is not None else nsums
    out_dtypes = [o[0] if isinstance(o, tuple) else o for o in outs]

    def body(*refs):
        a_refs, w_hbm = refs[:na], refs[na]
        ex = refs[na + 1:na + 1 + nex]
        co = refs[na + 1 + nex:na + 1 + nex + nco]
        o_refs = refs[na + 1 + nex + nco:na + 1 + nex + nco + nout]
        s_refs = refs[na + 1 + nex + nco + nout:na + 1 + nex + nco + nout + nsums]
        w_ref = refs[-1]
        i = pl.program_id(0)

        @pl.when(i == 0)
        def _():
            pltpu.sync_copy(w_hbm, w_ref)

        av = _bf(a_refs[0][...] if pro is None else pro(*[r[...] for r in a_refs]))
        if transposed:
            acc = None
            for j in range(J):
                part = lax.dot_general(av[:, j * n:(j + 1) * n], w_ref[j], NT_DIMS, preferred_element_type=f32)
                acc = part if acc is None else acc + part
        else:
            acc = jnp.dot(av, w_ref[0], preferred_element_type=f32)
        rows, sums = epi(acc, *[e[...] for e in ex], *[c[...] for c in co])
        if pro is not None:
            rows = (av,) + tuple(rows)
        for o_ref, v, dt in zip(o_refs, rows, out_dtypes):
            o_ref[...] = v.astype(dt)
        for s_ref, v in zip(s_refs, sums):
            @pl.when(i == 0)
            def _(s_ref=s_ref, v=v):
                s_ref[...] = v

            @pl.when(i > 0)
            def _(s_ref=s_ref, v=v):
                s_ref[...] += v

    extras = [e if isinstance(e, tuple) else (e, W, 0) for e in extras]
    outs_full = [o if isinstance(o, tuple) else (o, W, W) for o in outs]
    widths = list(sum_widths) if sum_widths is not None else [W] * nsums
    in_specs = [_rows(tm, w, cb) for _, w, cb in a_rows] + [_full(c.shape) for c in a_consts]
    in_specs += [pl.BlockSpec(memory_space=pl.ANY)]
    in_specs += [_rows(tm, w, cb) for _, w, cb in extras] + [_full(c.shape) for c in consts]
    out_specs = [_rows(tm, w) for _, w, _ in outs_full] + [_full((1, w)) for w in widths]
    out_shape = ([jax.ShapeDtypeStruct((M, cols), dt) for dt, _, cols in outs_full]
                 + [jax.ShapeDtypeStruct((1, w), f32) for w in widths])
    args = [e for e, _, _ in a_rows] + a_consts + [w3] + [e for e, _, _ in extras] + list(consts)
    res, landed = _pcall(body, args, name=name, grid=(M // tm,), in_specs=in_specs, out_specs=out_specs,
                         out_shape=out_shape, scratch_shapes=[pltpu.VMEM((J, K, n), bf16)], rides=rides)
    return (res, landed) if rides else res


def mm_bwd_w(a, dy, J, name, *, tm=2048, tn=None, tk=None, rides=()):
    M, K = a.shape
    n = dy.shape[1] // J
    tm = min(tm, M)
    tn = tn or n
    tk = tk or K
    nl = n // tn
    nk = K // tk
    ns = M // tm

    def body(a_ref, dy_ref, o_ref, acc_ref):
        s = pl.program_id(3)
        part = lax.dot_general(_bf(a_ref[...]), _bf(dy_ref[...]), TN_DIMS, preferred_element_type=f32)

        @pl.when(s == 0)
        def _():
            acc_ref[...] = part

        @pl.when(s > 0)
        def _():
            acc_ref[...] += part

        @pl.when(s == ns - 1)
        def _():
            o_ref[0] = acc_ref[...].astype(bf16)

    res, landed = _pcall(
        body, [a, dy], name=name, grid=(J, nl, nk, ns),
        in_specs=[pl.BlockSpec((tm, tk), lambda j, l, k, s: (s, k)),
                  pl.BlockSpec((tm, tn), lambda j, l, k, s: (s, j * nl + l))],
        out_specs=[pl.BlockSpec((1, tk, tn), lambda j, l, k, s: (j, k, l))],
        out_shape=[jax.ShapeDtypeStruct((J, K, n), bf16)],
        scratch_shapes=[pltpu.VMEM((tk, tn), f32)], rides=rides)
    return (res[0], landed) if rides else res[0]


def _relu2(v):
    r = jnp.maximum(v, 0.0)
    return r * r


HALO = 16


def _shift_down(z, k, halo):
    r = lax.broadcasted_iota(jnp.int32, z.shape, 0)
    y = pltpu.roll(z, k, 0)
    for q in range(k):
        y = jnp.where(r == q, halo[HALO - k + q:HALO - k + q + 1, :], y)
    return y


def _shift_up(z, k, halo):
    tm = z.shape[0]
    r = lax.broadcasted_iota(jnp.int32, z.shape, 0)
    y = pltpu.roll(z, tm - k, 0)
    for q in range(k):
        y = jnp.where(r == tm - k + q, halo[q:q + 1, :], y)
    return y


def _prev_halo(tm, cb):
    return pl.BlockSpec((HALO, CONV_WIDTH), lambda i: (jnp.maximum(i * (tm // HALO) - 1, 0), cb))


def _next_halo(tm, cb, nblk):
    return pl.BlockSpec((HALO, CONV_WIDTH), lambda i: (jnp.minimum((i + 1) * (tm // HALO), nblk - 1), cb))


def _f32(ref):
    return ref[...].astype(f32)


def conv_fwd(proj, conv_w, wco3, tm=TM):
    S = proj.shape[0]
    J, _, n = wco3.shape

    def body(cb_ref, cc_ref, ch_ref, cch_ref, chh_ref, w_ref, wo_ref, a_ref, y_ref):
        i = pl.program_id(0)
        z = _f32(cc_ref) * _f32(ch_ref)
        zh = jnp.where(i == 0, 0.0, _f32(cch_ref) * _f32(chh_ref))
        w = w_ref[...]
        dwz = w[0:1, :] * _shift_down(z, 2, zh) + w[1:2, :] * _shift_down(z, 1, zh) + w[2:3, :] * z
        a = (_f32(cb_ref) * dwz).astype(bf16)
        a_ref[...] = a
        for j in range(J):
            y_ref[:, j * n:(j + 1) * n] = jnp.dot(a, wo_ref[j], preferred_element_type=f32).astype(bf16)

    return pl.pallas_call(
        body, name="conv_fwd", grid=(S // tm,),
        in_specs=[_rows(tm, CONV_WIDTH, 6), _rows(tm, CONV_WIDTH, 7), _rows(tm, CONV_WIDTH, 8),
                  _prev_halo(tm, 7), _prev_halo(tm, 8), _full((3, CONV_WIDTH)), _full(wco3.shape)],
        out_specs=[_rows(tm, CONV_WIDTH), _rows(tm, J * n)],
        out_shape=[jax.ShapeDtypeStruct((S, CONV_WIDTH), bf16), jax.ShapeDtypeStruct((S, J * n), bf16)],
        compiler_params=_params())(proj, proj, proj, proj, proj, conv_w, wco3)


def conv_bwd(dy_a, a_conv, wco3, proj, conv_w, dproj, tm=TM):
    S = proj.shape[0]
    nt = S // tm
    J, _, n = wco3.shape

    def body(dy_ref, dyn_ref, a_ref, wo_ref, cb_ref, cc_ref, ch_ref, cch_ref, chh_ref, cbn_ref, w_ref, _, o_ref,
             dw_ref, dwo_ref, acc_ref):
        i = pl.program_id(0)

        def times_wo_t(dy):
            out = None
            for j in range(J):
                part = lax.dot_general(dy[:, j * n:(j + 1) * n], wo_ref[j], NT_DIMS, preferred_element_type=f32)
                out = part if out is None else out + part
            return out

        dyv = dy_ref[...]
        dav, dan = times_wo_t(dyv), times_wo_t(dyn_ref[...])
        part = lax.dot_general(a_ref[...], dyv, TN_DIMS, preferred_element_type=f32)

        @pl.when(i == 0)
        def _():
            acc_ref[...] = part

        @pl.when(i > 0)
        def _():
            acc_ref[...] += part

        @pl.when(i == nt - 1)
        def _():
            for j in range(J):
                dwo_ref[j] = acc_ref[:, j * n:(j + 1) * n].astype(bf16)

        cc, ch, cb = _f32(cc_ref), _f32(ch_ref), _f32(cb_ref)
        z = cc * ch
        zh = jnp.where(i == 0, 0.0, _f32(cch_ref) * _f32(chh_ref))
        w = w_ref[...]
        z1 = _shift_down(z, 1, zh)
        z2 = _shift_down(z, 2, zh)
        dwz = w[0:1, :] * z2 + w[1:2, :] * z1 + w[2:3, :] * z
        g = dav * cb
        gn = jnp.where(i == nt - 1, 0.0, dan * _f32(cbn_ref))
        dz = w[2:3, :] * g + w[1:2, :] * _shift_up(g, 1, gn) + w[0:1, :] * _shift_up(g, 2, gn)
        o_ref[:, 0:CONV_WIDTH] = (dav * dwz).astype(bf16)
        o_ref[:, CONV_WIDTH:2 * CONV_WIDTH] = (dz * ch).astype(bf16)
        o_ref[:, 2 * CONV_WIDTH:3 * CONV_WIDTH] = (dz * cc).astype(bf16)

        @pl.when(i == 0)
        def _():
            dw_ref[...] = jnp.zeros_like(dw_ref)

        dw_ref[0:1, :] += jnp.sum(g * z2, axis=0, keepdims=True)
        dw_ref[1:2, :] += jnp.sum(g * z1, axis=0, keepdims=True)
        dw_ref[2:3, :] += jnp.sum(g * z, axis=0, keepdims=True)

    return pl.pallas_call(
        body, name="conv_bwd", grid=(nt,),
        in_specs=[_rows(tm, J * n),
                  pl.BlockSpec((HALO, J * n), lambda i: (jnp.minimum((i + 1) * (tm // HALO), S // HALO - 1), 0)),
                  _rows(tm, CONV_WIDTH), _full(wco3.shape),
                  _rows(tm, CONV_WIDTH, 6), _rows(tm, CONV_WIDTH, 7), _rows(tm, CONV_WIDTH, 8),
                  _prev_halo(tm, 7), _prev_halo(tm, 8), _next_halo(tm, 6, S // HALO), _full((3, CONV_WIDTH)),
                  pl.BlockSpec(memory_space=pl.ANY)],
        out_specs=[_rows(tm, 3 * CONV_WIDTH, GATE_COLS // (3 * CONV_WIDTH)), _full((3, CONV_WIDTH)),
                   _full(wco3.shape)],
        out_shape=[jax.ShapeDtypeStruct(dproj.shape, bf16), jax.ShapeDtypeStruct((3, CONV_WIDTH), f32),
                   jax.ShapeDtypeStruct(wco3.shape, bf16)],
        scratch_shapes=[pltpu.VMEM((CONV_WIDTH, J * n), f32)], input_output_aliases={11: 0},
        compiler_params=_params())(dy_a, dy_a, a_conv, wco3, proj, proj, proj, proj, proj, proj, conv_w, dproj)


def _cmul(ar, ai, br, bi):
    return ar * br - ai * bi, ar * bi + ai * br


def ssm_prep(lr, li, ldt, bt_re, bt_im):
    def body(lr_ref, li_ref, ldt_ref, br_ref, bi_ref, ar_ref, ai_ref, kr_ref, ki_ref, bbr_ref, bbi_ref):
        lrv, liv = lr_ref[...], li_ref[...]
        dt = jnp.exp(ldt_ref[...])
        mag = jnp.exp(lrv * dt)
        ar = mag * jnp.cos(liv * dt)
        ai = mag * jnp.sin(liv * dt)
        den = lrv * lrv + liv * liv
        nr = ar - 1.0
        kr = (nr * lrv + ai * liv) / den
        ki = (ai * lrv - nr * liv) / den
        ar_ref[...], ai_ref[...], kr_ref[...], ki_ref[...] = ar, ai, kr, ki
        bbr_ref[...] = kr * br_ref[...] - ki * bi_ref[...]
        bbi_ref[...] = kr * bi_ref[...] + ki * br_ref[...]

    v = jax.ShapeDtypeStruct((1, N_STATES), f32)
    m = jax.ShapeDtypeStruct((SSM_GROUP, N_STATES), f32)
    return pl.pallas_call(body, name="ssm_prep", out_shape=[v, v, v, v, m, m])(lr, li, ldt, bt_re, bt_im)


def _pow_segment(ar, ai, seg_len):
    pr, pi = ar, ai
    for _ in range(int(math.log2(seg_len))):
        pr, pi = _cmul(pr, pi, pr, pi)
    return pr, pi


GELU_K = math.sqrt(2.0 / math.pi)
GELU_C = 0.044715


def _gelu(v):
    return 0.5 * v * (1.0 + jnp.tanh(GELU_K * (v + GELU_C * v * v * v)))


def _gelu_grad(v):
    t = jnp.tanh(GELU_K * (v + GELU_C * v * v * v))
    return 0.5 * (1.0 + t) + 0.5 * v * (1.0 - t * t) * GELU_K * (1.0 + 3.0 * GELU_C * v * v)


def ssm_scan_fwd(u_perm, bd_r, bd_i, ar, ai, *, final=None, rides=()):
    S = u_perm.shape[0]
    R = SCAN_ROWS
    nblk = S // R
    seg_len = S // N_SEG
    nstrip = N_STATES // LANE_STRIP
    store = final is not None

    def body(*refs):
        if store:
            (u_ref, bdr_ref, bdi_ref, ar_ref, ai_ref, cdr_ref, cdi_ref, d_ref, ir_ref, ii_ref,
             y_ref, ys_ref, sr_ref, si_ref, bur, bui, car, cai) = refs
        else:
            u_ref, bdr_ref, bdi_ref, ar_ref, ai_ref, or_ref, oi_ref, bur, bui, car, cai = refs
        i = pl.program_id(0)
        ub = _bf(u_ref[...])
        u = ub.astype(f32)
        for cb in range(4):
            us = ub[:, cb * 128:(cb + 1) * 128]
            bur[:, cb * 512:(cb + 1) * 512] = jnp.dot(us, bdr_ref[cb], preferred_element_type=f32)
            bui[:, cb * 512:(cb + 1) * 512] = jnp.dot(us, bdi_ref[cb], preferred_element_type=f32)

        @pl.when(i == 0)
        def _():
            if store:
                car[...] = ir_ref[...]
                cai[...] = ii_ref[...]
            else:
                car[...] = jnp.zeros_like(car)
                cai[...] = jnp.zeros_like(cai)

        for ls in range(nstrip):
            lanes = pl.ds(ls * LANE_STRIP, LANE_STRIP)
            a_r = jnp.broadcast_to(ar_ref[:, lanes], (N_SEG, LANE_STRIP))
            a_i = jnp.broadcast_to(ai_ref[:, lanes], (N_SEG, LANE_STRIP))

            def step(t, carry, lanes=lanes, a_r=a_r, a_i=a_i):
                s_r, s_i = carry
                row = pl.multiple_of(t * 8, 8)
                n_r = a_r * s_r - a_i * s_i + bur[pl.ds(row, 8), lanes]
                n_i = a_r * s_i + a_i * s_r + bui[pl.ds(row, 8), lanes]
                if store:
                    sr_ref[pl.ds(row, 8), lanes] = n_r
                    si_ref[pl.ds(row, 8), lanes] = n_i
                return n_r, n_i

            e_r, e_i = lax.fori_loop(0, R // 8, step, (car[:, lanes], cai[:, lanes]), unroll=True)
            car[:, lanes] = e_r
            cai[:, lanes] = e_i

        if store:
            for cb in range(4):
                st_r = sr_ref[:, cb * 512:(cb + 1) * 512].astype(bf16)
                st_i = si_ref[:, cb * 512:(cb + 1) * 512].astype(bf16)
                y = (jnp.dot(st_r, cdr_ref[cb], preferred_element_type=f32)
                     - jnp.dot(st_i, cdi_ref[cb], preferred_element_type=f32))
                cols = slice(cb * 128, (cb + 1) * 128)
                y = y + d_ref[:, cols] * u[:, cols]
                y_ref[:, cols] = y
                ys_ref[:, cols] = _gelu(y).astype(bf16)
        else:
            @pl.when(i == nblk - 1)
            def _():
                p_r, p_i = _pow_segment(ar_ref[...], ai_ref[...], seg_len)
                t_r, t_i = car[0:1, :], cai[0:1, :]
                or_ref[0:1, :] = jnp.zeros((1, N_STATES), f32)
                oi_ref[0:1, :] = jnp.zeros((1, N_STATES), f32)
                for j in range(1, N_SEG):
                    or_ref[j:j + 1, :] = t_r
                    oi_ref[j:j + 1, :] = t_i
                    m_r, m_i = _cmul(p_r, p_i, t_r, t_i)
                    t_r, t_i = car[j:j + 1, :] + m_r, cai[j:j + 1, :] + m_i

    blk = lambda w: pl.BlockSpec((R, w), lambda i: (i, 0))
    in_specs = [blk(SSM_WIDTH), _full((4, 128, 512)), _full((4, 128, 512)), _full((1, N_STATES)), _full((1, N_STATES))]
    args = [u_perm, bd_r, bd_i, ar, ai]
    scratch = [pltpu.VMEM((R, N_STATES), f32), pltpu.VMEM((R, N_STATES), f32),
               pltpu.VMEM((N_SEG, N_STATES), f32), pltpu.VMEM((N_SEG, N_STATES), f32)]
    if store:
        in_specs += [_full((4, 512, 128)), _full((4, 512, 128)), _full((1, SSM_WIDTH)),
                     _full((N_SEG, N_STATES)), _full((N_SEG, N_STATES))]
        args += list(final)
        out_specs = [blk(SSM_WIDTH), blk(SSM_WIDTH), blk(N_STATES), blk(N_STATES)]
        out_shape = [jax.ShapeDtypeStruct((S, SSM_WIDTH), f32), jax.ShapeDtypeStruct((S, SSM_WIDTH), bf16),
                     jax.ShapeDtypeStruct((S, N_STATES), f32), jax.ShapeDtypeStruct((S, N_STATES), f32)]
        name = "ssm_scan_fwd"
    else:
        out_specs = [_full((N_SEG, N_STATES)), _full((N_SEG, N_STATES))]
        out_shape = [jax.ShapeDtypeStruct((N_SEG, N_STATES), f32)] * 2
        name = "ssm_scan_fwd_carry"
    res, landed = _pcall(body, args, name=name, grid=(nblk,), in_specs=in_specs, out_specs=out_specs,
                         out_shape=out_shape, scratch_shapes=scratch, rides=rides)
    return (res, landed) if rides else res


def ssm_scan_bwd(dys_perm, y_perm, cdt_r, cdt_i, ar, ai, *, final=None, rides=()):
    S = dys_perm.shape[0]
    R = SCAN_ROWS
    nblk = S // R
    seg_len = S // N_SEG
    store = final is not None
    strip = LANE_STRIP // 2 if store else LANE_STRIP
    nstrip = N_STATES // strip

    def body(*refs):
        if store:
            (dys_ref, y_ref, cdr_ref, cdi_ref, ar_ref, ai_ref, u_ref, sr_ref, si_ref, bdr_ref, bdi_ref, d_ref, ir_ref,
             ii_ref, du_ref, dbr_ref, dbi_ref, dcr_ref, dci_ref, dar_ref, dai_ref, dd_ref, dsr, dsi, lmr, lmi, car,
             cai) = refs
        else:
            dys_ref, y_ref, cdr_ref, cdi_ref, ar_ref, ai_ref, or_ref, oi_ref, dsr, dsi, car, cai = refs
        i = pl.program_id(0)
        dy = dys_ref[...] * _gelu_grad(y_ref[...])
        dyb = dy.astype(bf16)
        for cb in range(4):
            ds_ = dyb[:, cb * 128:(cb + 1) * 128]
            dsr[:, cb * 512:(cb + 1) * 512] = jnp.dot(ds_, cdr_ref[cb], preferred_element_type=f32)
            dsi[:, cb * 512:(cb + 1) * 512] = -jnp.dot(ds_, cdi_ref[cb], preferred_element_type=f32)

        @pl.when(i == 0)
        def _():
            if store:
                car[...] = ir_ref[...]
                cai[...] = ii_ref[...]
                dar_ref[...] = jnp.zeros_like(dar_ref)
                dai_ref[...] = jnp.zeros_like(dai_ref)
                dbr_ref[...] = jnp.zeros_like(dbr_ref)
                dbi_ref[...] = jnp.zeros_like(dbi_ref)
                dcr_ref[...] = jnp.zeros_like(dcr_ref)
                dci_ref[...] = jnp.zeros_like(dci_ref)
                dd_ref[...] = jnp.zeros_like(dd_ref)
            else:
                car[...] = jnp.zeros_like(car)
                cai[...] = jnp.zeros_like(cai)

        for ls in range(nstrip):
            lanes = pl.ds(ls * strip, strip)
            a_r = jnp.broadcast_to(ar_ref[:, lanes], (N_SEG, strip))
            a_i = jnp.broadcast_to(ai_ref[:, lanes], (N_SEG, strip))
            zero = jnp.zeros((N_SEG, strip), f32)

            def step(k, carry, lanes=lanes, a_r=a_r, a_i=a_i):
                l_r, l_i, g_r, g_i = carry
                row = pl.multiple_of((R // 8 - 1 - k) * 8, 8)
                if store:
                    s_r = sr_ref[pl.ds(row, 8), lanes]
                    s_i = si_ref[pl.ds(row, 8), lanes]
                    g_r = g_r + l_r * s_r + l_i * s_i
                    g_i = g_i + l_i * s_r - l_r * s_i
                n_r = dsr[pl.ds(row, 8), lanes] + a_r * l_r + a_i * l_i
                n_i = dsi[pl.ds(row, 8), lanes] + a_r * l_i - a_i * l_r
                if store:
                    lmr[pl.ds(row, 8), lanes] = n_r
                    lmi[pl.ds(row, 8), lanes] = n_i
                return n_r, n_i, g_r, g_i

            e_r, e_i, g_r, g_i = lax.fori_loop(0, R // 8, step, (car[:, lanes], cai[:, lanes], zero, zero),
                                               unroll=True)
            car[:, lanes] = e_r
            cai[:, lanes] = e_i
            if store:
                dar_ref[:, lanes] += g_r
                dai_ref[:, lanes] += g_i

        if store:
            ub = _bf(u_ref[...])
            u = ub.astype(f32)
            for cb in range(4):
                cols = slice(cb * 128, (cb + 1) * 128)
                st = slice(cb * 512, (cb + 1) * 512)
                l_r = lmr[:, st].astype(bf16)
                l_i = lmi[:, st].astype(bf16)
                du = (jnp.dot(l_r, bdr_ref[cb], preferred_element_type=f32)
                      + jnp.dot(l_i, bdi_ref[cb], preferred_element_type=f32))
                du_ref[:, cols] = (du + d_ref[:, cols] * dy[:, cols]).astype(bf16)
                dbr_ref[cb] += lax.dot_general(l_r, ub[:, cols], TN_DIMS, preferred_element_type=f32)
                dbi_ref[cb] += lax.dot_general(l_i, ub[:, cols], TN_DIMS, preferred_element_type=f32)
                dcr_ref[cb] += lax.dot_general(dyb[:, cols], sr_ref[:, st].astype(bf16), TN_DIMS,
                                               preferred_element_type=f32)
                dci_ref[cb] += lax.dot_general(dyb[:, cols], si_ref[:, st].astype(bf16), TN_DIMS,
                                               preferred_element_type=f32)
            dd_ref[...] += jnp.sum(dy * u, axis=0, keepdims=True)
        else:
            @pl.when(i == nblk - 1)
            def _():
                p_r, p_i = _pow_segment(ar_ref[...], ai_ref[...], seg_len)
                p_i = -p_i
                t_r, t_i = car[N_SEG - 1:N_SEG, :], cai[N_SEG - 1:N_SEG, :]
                or_ref[N_SEG - 1:N_SEG, :] = jnp.zeros((1, N_STATES), f32)
                oi_ref[N_SEG - 1:N_SEG, :] = jnp.zeros((1, N_STATES), f32)
                for j in range(N_SEG - 2, -1, -1):
                    or_ref[j:j + 1, :] = t_r
                    oi_ref[j:j + 1, :] = t_i
                    m_r, m_i = _cmul(p_r, p_i, t_r, t_i)
                    t_r, t_i = car[j:j + 1, :] + m_r, cai[j:j + 1, :] + m_i

    blk = lambda w: pl.BlockSpec((R, w), lambda i: (nblk - 1 - i, 0))
    in_specs = [blk(SSM_WIDTH), blk(SSM_WIDTH), _full((4, 128, 512)), _full((4, 128, 512)), _full((1, N_STATES)),
                _full((1, N_STATES))]
    args = [dys_perm, y_perm, cdt_r, cdt_i, ar, ai]
    seg = jax.ShapeDtypeStruct((N_SEG, N_STATES), f32)
    if store:
        in_specs += [blk(SSM_WIDTH), blk(N_STATES), blk(N_STATES), _full((4, 512, 128)), _full((4, 512, 128)),
                     _full((1, SSM_WIDTH)), _full((N_SEG, N_STATES)), _full((N_SEG, N_STATES))]
        args += list(final)
        out_specs = [blk(SSM_WIDTH), _full((4, 512, 128)), _full((4, 512, 128)), _full((4, 128, 512)),
                     _full((4, 128, 512)), _full((N_SEG, N_STATES)), _full((N_SEG, N_STATES)), _full((1, SSM_WIDTH))]
        b_acc = jax.ShapeDtypeStruct((4, 512, 128), f32)
        c_acc = jax.ShapeDtypeStruct((4, 128, 512), f32)
        out_shape = [jax.ShapeDtypeStruct((S, SSM_WIDTH), bf16), b_acc, b_acc, c_acc, c_acc, seg, seg,
                     jax.ShapeDtypeStruct((1, SSM_WIDTH), f32)]
        scratch = [pltpu.VMEM((R, N_STATES), f32)] * 4 + [pltpu.VMEM((N_SEG, N_STATES), f32)] * 2
        name = "ssm_scan_bwd"
    else:
        out_specs = [_full((N_SEG, N_STATES)), _full((N_SEG, N_STATES))]
        out_shape = [seg, seg]
        scratch = [pltpu.VMEM((R, N_STATES), f32)] * 2 + [pltpu.VMEM((N_SEG, N_STATES), f32)] * 2
        name = "ssm_scan_bwd_carry"
    res, landed = _pcall(body, args, name=name, grid=(nblk,), in_specs=in_specs, out_specs=out_specs,
                         out_shape=out_shape, scratch_shapes=scratch, rides=rides)
    return (res, landed) if rides else res


def ssm_param_bwd(dbb_r, dbb_i, bt_re, bt_im, kr, ki, ar, ai, lr, li, ldt, da_r, da_i):
    def body(dbr_ref, dbi_ref, br_ref, bi_ref, kr_ref, ki_ref, ar_ref, ai_ref, lr_ref, li_ref, ldt_ref, dar_ref,
             dai_ref, obr_ref, obi_ref, olr_ref, oli_ref, odt_ref):
        dbr, dbi, b_r, b_i = dbr_ref[...], dbi_ref[...], br_ref[...], bi_ref[...]
        k_r, k_i, a_r, a_i = kr_ref[...], ki_ref[...], ar_ref[...], ai_ref[...]
        l_r, l_i = lr_ref[...], li_ref[...]
        dt = jnp.exp(ldt_ref[...])
        obr_ref[...] = k_r * dbr + k_i * dbi
        obi_ref[...] = k_r * dbi - k_i * dbr
        gk_r = jnp.sum(dbr * b_r + dbi * b_i, axis=0, keepdims=True)
        gk_i = jnp.sum(dbi * b_r - dbr * b_i, axis=0, keepdims=True)
        ga_r = jnp.sum(dar_ref[...], axis=0, keepdims=True)
        ga_i = jnp.sum(dai_ref[...], axis=0, keepdims=True)
        den = l_r * l_r + l_i * l_i
        c_r, c_i = l_r / den, l_i / den
        m_r, m_i = _cmul(c_r, c_i, gk_r, gk_i)
        g_r, g_i = ga_r + m_r, ga_i + m_i
        t1_r, t1_i = _cmul(dt * a_r, -dt * a_i, g_r, g_i)
        q_r, q_i = _cmul(k_r, k_i, c_r, -c_i)
        t2_r, t2_i = _cmul(-q_r, q_i, gk_r, gk_i)
        olr_ref[...] = t1_r + t2_r
        oli_ref[...] = t1_i + t2_i
        w_r, w_i = _cmul(l_r, l_i, a_r, a_i)
        odt_ref[...] = dt * (w_r * g_r + w_i * g_i)

    v = jax.ShapeDtypeStruct((1, N_STATES), f32)
    m = jax.ShapeDtypeStruct((SSM_GROUP, N_STATES), f32)
    return pl.pallas_call(body, name="ssm_param_bwd", out_shape=[m, m, v, v, v])(
        dbb_r, dbb_i, bt_re, bt_im, kr, ki, ar, ai, lr, li, ldt, da_r, da_i)


def group_sum(v):
    def body(v_ref, o_ref):
        o_ref[...] = jnp.sum(v_ref[...], axis=-1, keepdims=True)
    return pl.pallas_call(body, name="group_sum", out_shape=jax.ShapeDtypeStruct((v.shape[0], 1), f32))(v)


ATTN_SCALE = XATTN_HEAD_DIM ** -0.5


def _attn_probs(q_h, k_h):
    s = lax.dot_general(q_h, k_h, NT_DIMS, preferred_element_type=f32) * ATTN_SCALE
    e = jnp.exp(s - jnp.max(s, axis=-1, keepdims=True))
    return e / jnp.sum(e, axis=-1, keepdims=True)


def attn_fwd(proj, kv, wxo3, tm=TM):
    S = proj.shape[0]
    M = kv.shape[0]
    J, _, n = wxo3.shape

    def body(q_ref, kv_ref, w_ref, o_ref, y_ref):
        for h in range(XATTN_HEADS):
            cols = slice(h * XATTN_HEAD_DIM, (h + 1) * XATTN_HEAD_DIM)
            q_h = q_ref[:, cols].astype(bf16)
            k_h = kv_ref[:, cols]
            v_h = kv_ref[:, XATTN_WIDTH + h * XATTN_HEAD_DIM:XATTN_WIDTH + (h + 1) * XATTN_HEAD_DIM]
            p = _attn_probs(q_h, k_h)
            o_ref[:, cols] = jnp.dot(p.astype(bf16), v_h, preferred_element_type=f32).astype(bf16)
        o = o_ref[...]
        for j in range(J):
            y_ref[:, j * n:(j + 1) * n] = jnp.dot(o, w_ref[j], preferred_element_type=f32).astype(bf16)

    return pl.pallas_call(
        body, name="attn_fwd", grid=(S // tm,),
        in_specs=[_rows(tm, XATTN_WIDTH, 10), _full((M, 2 * XATTN_WIDTH)), _full(wxo3.shape)],
        out_specs=[_rows(tm, XATTN_WIDTH), _rows(tm, J * n)],
        out_shape=[jax.ShapeDtypeStruct((S, XATTN_WIDTH), bf16), jax.ShapeDtypeStruct((S, J * n), bf16)],
        compiler_params=_params())(proj, kv, wxo3)


def attn_bwd(dy_c, o_att, wxo3, proj, kv, dproj, tm=TM):
    S = proj.shape[0]
    M = kv.shape[0]
    J, _, n = wxo3.shape
    nt = S // tm

    def body(dy_ref, o_ref, w_ref, q_ref, kv_ref, _, dq_ref, dkv_ref, dwo_ref, acc_ref):
        i = pl.program_id(0)

        @pl.when(i == 0)
        def _():
            dkv_ref[...] = jnp.zeros_like(dkv_ref)

        part = lax.dot_general(o_ref[...], dy_ref[...], TN_DIMS, preferred_element_type=f32)

        @pl.when(i == 0)
        def _():
            acc_ref[...] = part

        @pl.when(i > 0)
        def _():
            acc_ref[...] += part

        @pl.when(i == nt - 1)
        def _():
            for j in range(J):
                dwo_ref[j] = acc_ref[:, j * n:(j + 1) * n].astype(bf16)

        do = None
        for j in range(J):
            part = lax.dot_general(dy_ref[:, j * n:(j + 1) * n], w_ref[j], NT_DIMS, preferred_element_type=f32)
            do = part if do is None else do + part
        do = do.astype(bf16)
        for h in range(XATTN_HEADS):
            cols = slice(h * XATTN_HEAD_DIM, (h + 1) * XATTN_HEAD_DIM)
            vcols = slice(XATTN_WIDTH + h * XATTN_HEAD_DIM, XATTN_WIDTH + (h + 1) * XATTN_HEAD_DIM)
            q_h = q_ref[:, cols].astype(bf16)
            k_h = kv_ref[:, cols]
            v_h = kv_ref[:, vcols]
            do_h = do[:, cols]
            p = _attn_probs(q_h, k_h)
            dp = lax.dot_general(do_h, v_h, NT_DIMS, preferred_element_type=f32)
            ds = (p * (dp - jnp.sum(dp * p, axis=-1, keepdims=True)) * ATTN_SCALE).astype(bf16)
            dq_ref[:, cols] = jnp.dot(ds, k_h, preferred_element_type=f32).astype(bf16)
            dkv_ref[:, cols] += lax.dot_general(ds, q_h, TN_DIMS, preferred_element_type=f32)
            dkv_ref[:, vcols] += lax.dot_general(p.astype(bf16), do_h, TN_DIMS, preferred_element_type=f32)

    return pl.pallas_call(
        body, name="attn_bwd", grid=(S // tm,),
        in_specs=[_rows(tm, J * n), _rows(tm, XATTN_WIDTH), _full(wxo3.shape), _rows(tm, XATTN_WIDTH, 10),
                  _full((M, 2 * XATTN_WIDTH)), pl.BlockSpec(memory_space=pl.ANY)],
        out_specs=[_rows(tm, XATTN_WIDTH, 10), _full((M, 2 * XATTN_WIDTH)), _full(wxo3.shape)],
        out_shape=[jax.ShapeDtypeStruct(dproj.shape, bf16), jax.ShapeDtypeStruct((M, 2 * XATTN_WIDTH), f32),
                   jax.ShapeDtypeStruct(wxo3.shape, bf16)],
        scratch_shapes=[pltpu.VMEM((XATTN_WIDTH, J * n), f32)], input_output_aliases={5: 0},
        compiler_params=_params())(dy_c, o_att, wxo3, proj, kv, dproj)


def _ln_stats(r):
    mu = jnp.mean(r, axis=-1, keepdims=True)
    xc = r - mu
    var = jnp.mean(xc * xc, axis=-1, keepdims=True)
    rstd = lax.rsqrt(var + LN_EPS)
    return xc * rstd, rstd


def _ln_bwd(dy, xhat, rstd, g):
    dxh = dy * g
    return rstd * (dxh - jnp.mean(dxh, axis=-1, keepdims=True) - xhat * jnp.mean(dxh * xhat, axis=-1, keepdims=True))


def _colsum(v):
    return jnp.sum(v, axis=0, keepdims=True)


def _ln2_loss_epi(t2, h1, target, b_down, g, b):
    xhat, rstd = _ln_stats(ALPHA * h1 + t2 + b_down)
    err = xhat * g + b - target
    dout = err * (1.0 / D_MODEL)
    dr = _ln_bwd(dout, xhat, rstd, g)
    return (dr, dr), (0.5 * _colsum(err * err) * (1.0 / D_MODEL), _colsum(dout * xhat), _colsum(dout), _colsum(dr))


def _merge_fwd_pro(g0, g1, g2, y_a, glu_a, glu_b, y_c, b):
    gs = [_sig(g.astype(f32) + b[:, k * D_MODEL:(k + 1) * D_MODEL]) for k, g in enumerate((g0, g1, g2))]
    y_b = glu_a.astype(f32) * _sig(glu_b.astype(f32))
    return gs[0] * y_a.astype(f32) + gs[1] * y_b + gs[2] * y_c.astype(f32)


def _merge_bwd_epi(dm, g0, g1, g2, y_a, glu_a, glu_b, y_c, b):
    ga, sb = glu_a.astype(f32), _sig(glu_b.astype(f32))
    ys = (y_a.astype(f32), ga * sb, y_c.astype(f32))
    gs = [_sig(g.astype(f32) + b[:, k * D_MODEL:(k + 1) * D_MODEL]) for k, g in enumerate((g0, g1, g2))]
    dpre = [dm * ys[k] * gs[k] * (1.0 - gs[k]) for k in range(3)]
    dyb = dm * gs[1]
    dglu = jnp.concatenate([dyb * sb, dyb * ga * sb * (1.0 - sb)], axis=1)
    return ((jnp.concatenate(dpre, axis=1), dm * gs[0], dglu, dm * gs[2]),
            (jnp.concatenate([_colsum(d) for d in dpre], axis=1),))


def _ln1_fwd_epi(t1, x, g, b):
    xhat, _ = _ln_stats(ALPHA * x + t1)
    h = xhat * g + b
    return (t1, h, h), ()


def _ln1_bwd_epi(t3, x, t1, dr2, g):
    xhat, rstd = _ln_stats(ALPHA * x + t1)
    dh = ALPHA * dr2 + t3
    dr = _ln_bwd(dh, xhat, rstd, g)
    return (dr, dr), (_colsum(dh * xhat), _colsum(dh))


def exchange(*rides, name):
    return _pcall(lambda: None, [], name=name, grid=(), in_specs=[], out_specs=[], out_shape=[], rides=rides)[1]


HBM_ONLY = pl.BlockSpec(memory_space=pltpu.HBM)
SEM_SPEC = pl.BlockSpec(memory_space=pltpu.SEMAPHORE)
DATAFLOW = pltpu.SideEffectType.DATAFLOW_SIDE_EFFECTING


def scatter_start(g):
    def body(g_ref, land_ref, s0, s1, s2, r0, r1, r2, g_thru, land_thru, token):
        del g_thru, land_thru
        x, y, c = lax.axis_index("x"), lax.axis_index("y"), lax.axis_index("c")
        me = 2 * x + y
        for k, (px, py) in enumerate(_xy_peers(x, y)):
            pltpu.make_async_remote_copy(src_ref=g_ref.at[2 * px + py], dst_ref=land_ref.at[me],
                                         send_sem=(s0, s1, s2)[k], recv_sem=(r0, r1, r2)[k],
                                         device_id=(px, py, c), device_id_type=MESH).start()
        token[...] = jnp.zeros_like(token)

    sem = pltpu.SemaphoreType.DMA(())
    return pl.pallas_call(
        body, name="scatter_start",
        out_shape=(sem,) * 6 + (pltpu.HBM(g.shape, g.dtype), pltpu.HBM(g.shape, g.dtype),
                                jax.ShapeDtypeStruct((1, D_MODEL), f32)),
        in_specs=(HBM_ONLY, HBM_ONLY), out_specs=(SEM_SPEC,) * 6 + (HBM_ONLY, HBM_ONLY, _full((1, D_MODEL))),
        input_output_aliases={0: 6, 1: 7}, compiler_params=pltpu.CompilerParams(has_side_effects=DATAFLOW),
    )(pltpu.with_memory_space_constraint(g, pltpu.HBM),
      pltpu.with_memory_space_constraint(lax.empty(g.shape, g.dtype), pltpu.HBM))


def scatter_wait(sems, g_thru, land_thru, after):
    def body(g_ref, land_ref, s0, s1, s2, r0, r1, r2, after_ref, g_out, land_out):
        del after_ref, g_out, land_out
        x, y, c = lax.axis_index("x"), lax.axis_index("y"), lax.axis_index("c")
        for k, (px, py) in enumerate(_xy_peers(x, y)):
            copy = pltpu.make_async_remote_copy(src_ref=g_ref.at[2 * px + py], dst_ref=land_ref.at[2 * px + py],
                                                send_sem=(s0, s1, s2)[k], recv_sem=(r0, r1, r2)[k],
                                                device_id=(px, py, c), device_id_type=MESH)
            copy.wait_send()
            copy.wait_recv()

    return pl.pallas_call(
        body, name="scatter_wait",
        out_shape=(pltpu.HBM(g_thru.shape, g_thru.dtype), pltpu.HBM(g_thru.shape, g_thru.dtype)),
        in_specs=(HBM_ONLY, HBM_ONLY) + (SEM_SPEC,) * 6 + (pl.BlockSpec(memory_space=pl.ANY),),
        out_specs=(HBM_ONLY, HBM_ONLY), input_output_aliases={0: 0, 1: 1},
        compiler_params=pltpu.CompilerParams(has_side_effects=DATAFLOW),
    )(g_thru, land_thru, *sems, after)


ROW_TILE = 256


def sum_slots(recv, name):
    n, R, C = recv.shape
    tm = min(R, ROW_TILE)

    def body(r_ref, o_ref):
        acc = r_ref[0].astype(f32)
        for k in range(1, n):
            acc = acc + r_ref[k].astype(f32)
        o_ref[...] = acc

    return pl.pallas_call(
        body, name=name, grid=(R // tm,), in_specs=[pl.BlockSpec((n, tm, C), lambda i: (0, i, 0))],
        out_specs=_rows(tm, C), out_shape=jax.ShapeDtypeStruct((R, C), f32), compiler_params=_params())(recv)


def _adamw(w, g, m, v):
    m = ADAM_B1 * m + (1.0 - ADAM_B1) * g
    v = ADAM_B2 * v + (1.0 - ADAM_B2) * (g * g)
    m_hat = m / (1.0 - ADAM_B1 ** ADAM_STEP)
    v_hat = v / (1.0 - ADAM_B2 ** ADAM_STEP)
    delta = -ADAM_LR * (m_hat / (jnp.sqrt(v_hat) + ADAM_EPS) + ADAM_WD * w)
    return delta, m, v


def adam_pair(p, q, w, m, v, name):
    R, C = w.shape
    tm = min(R, ROW_TILE)

    def body(p_ref, q_ref, w_ref, m_ref, v_ref, g_ref, d_ref, nm_ref, nv_ref):
        g = p_ref[...] + q_ref[...]
        g_ref[...] = g
        d_ref[...], nm_ref[...], nv_ref[...] = _adamw(w_ref[...], g, m_ref[...], v_ref[...])

    o = jax.ShapeDtypeStruct((R, C), f32)
    return pl.pallas_call(body, name=name, grid=(R // tm,), in_specs=[_rows(tm, C)] * 5,
                          out_specs=[_rows(tm, C)] * 4, out_shape=[o] * 4, compiler_params=_params())(p, q, w, m, v)


def adam_slots(recv, w, m, v, name):
    n = recv.shape[0]

    def body(r_ref, w_ref, m_ref, v_ref, g_ref, d_ref, nm_ref, nv_ref):
        g = r_ref[0]
        for k in range(1, n):
            g = g + r_ref[k]
        g_ref[...] = g
        d_ref[...], nm_ref[...], nv_ref[...] = _adamw(w_ref[...], g, m_ref[...], v_ref[...])

    o = jax.ShapeDtypeStruct(w.shape, f32)
    return pl.pallas_call(body, name=name, out_shape=[o] * 4, compiler_params=_params())(recv, w, m, v)


BIG = ("w_in", "w_conv_out", "w_glu", "w_xattn_out", "w_kv", "w_out", "w_up", "w_down")
MID = ("w_conv_out", "w_glu", "w_xattn_out", "w_kv", "w_out")
SMALL = ("b_gate", "ssm_lam_re", "ssm_lam_im", "ssm_log_dt", "ssm_b_re", "ssm_b_im", "ssm_c_re", "ssm_c_im", "ssm_d",
         "ln1_g", "ln1_b", "b_up", "b_down", "ln2_g", "ln2_b")
def _pad_flat(a, mult=1024):
    a = a.reshape(-1)
    return jnp.pad(a, (0, (-a.shape[0]) % mult))


def _perm(a):
    S, W = a.shape
    return a.reshape(N_SEG, S // N_SEG, W).transpose(1, 0, 2).reshape(S, W)


def _unperm(a):
    S, W = a.shape
    return a.reshape(S // N_SEG, N_SEG, W).transpose(1, 0, 2).reshape(S, W)


def _state_rows(a):
    return a.transpose(2, 0, 1).reshape(SSM_GROUP, N_STATES)


def _block_diag_b(bt):
    b4 = bt.reshape(SSM_GROUP, 4, 8, SSM_STATE)
    eye = jnp.eye(8, dtype=bt.dtype)
    return jnp.einsum("hcgp,gk->cghkp", b4, eye).reshape(4, 128, 512)


def _block_diag_c(c):
    c4 = c.reshape(4, 8, SSM_GROUP, SSM_STATE)
    eye = jnp.eye(8, dtype=c.dtype)
    return jnp.einsum("cghp,gk->cgpkh", c4, eye).reshape(4, 512, 128)


def _diag_b(acc):
    a = acc.reshape(4, 8, SSM_STATE, 8, SSM_GROUP)
    eye = jnp.eye(8, dtype=acc.dtype)
    return jnp.einsum("cgpkh,gk->hcgp", a, eye).reshape(SSM_GROUP, N_STATES)


def _diag_c(acc):
    a = acc.reshape(4, 8, SSM_GROUP, 8, SSM_STATE)
    eye = jnp.eye(8, dtype=acc.dtype)
    return jnp.einsum("cghkp,gk->cghp", a, eye).reshape(SSM_GROUPS, SSM_GROUP, SSM_STATE)


def kernel(x, mem, w_in, b_gate, conv_w, w_conv_out, ssm_lam_re, ssm_lam_im, ssm_log_dt, ssm_b_re, ssm_b_im, ssm_c_re, ssm_c_im, ssm_d, w_glu, w_kv, w_xattn_out, w_out, ln1_g, ln1_b, w_up, b_up, w_down, b_down, ln2_g, ln2_b, loss_target, m_w_in, m_b_gate, m_conv_w, m_w_conv_out, m_ssm_lam_re, m_ssm_lam_im, m_ssm_log_dt, m_ssm_b_re, m_ssm_b_im, m_ssm_c_re, m_ssm_c_im, m_ssm_d, m_w_glu, m_w_kv, m_w_xattn_out, m_w_out, m_ln1_g, m_ln1_b, m_w_up, m_b_up, m_w_down, m_b_down, m_ln2_g, m_ln2_b, v_w_in, v_b_gate, v_conv_w, v_w_conv_out, v_ssm_lam_re, v_ssm_lam_im, v_ssm_log_dt, v_ssm_b_re, v_ssm_b_im, v_ssm_c_re, v_ssm_c_im, v_ssm_d, v_w_glu, v_w_kv, v_w_xattn_out, v_w_out, v_ln1_g, v_ln1_b, v_w_up, v_b_up, v_w_down, v_b_down, v_ln2_g, v_ln2_b):
    W = dict(w_in=w_in, b_gate=b_gate, conv_w=conv_w, w_conv_out=w_conv_out, ssm_lam_re=ssm_lam_re,
             ssm_lam_im=ssm_lam_im, ssm_log_dt=ssm_log_dt, ssm_b_re=ssm_b_re, ssm_b_im=ssm_b_im, ssm_c_re=ssm_c_re,
             ssm_c_im=ssm_c_im, ssm_d=ssm_d, w_glu=w_glu, w_kv=w_kv, w_xattn_out=w_xattn_out, w_out=w_out,
             ln1_g=ln1_g, ln1_b=ln1_b, w_up=w_up, b_up=b_up, w_down=w_down, b_down=b_down, ln2_g=ln2_g, ln2_b=ln2_b)
    MOM = dict(w_in=m_w_in, b_gate=m_b_gate, conv_w=m_conv_w, w_conv_out=m_w_conv_out, ssm_lam_re=m_ssm_lam_re,
               ssm_lam_im=m_ssm_lam_im, ssm_log_dt=m_ssm_log_dt, ssm_b_re=m_ssm_b_re, ssm_b_im=m_ssm_b_im,
               ssm_c_re=m_ssm_c_re, ssm_c_im=m_ssm_c_im, ssm_d=m_ssm_d, w_glu=m_w_glu, w_kv=m_w_kv,
               w_xattn_out=m_w_xattn_out, w_out=m_w_out, ln1_g=m_ln1_g, ln1_b=m_ln1_b, w_up=m_w_up, b_up=m_b_up,
               w_down=m_w_down, b_down=m_b_down, ln2_g=m_ln2_g, ln2_b=m_ln2_b)
    VEL = dict(w_in=v_w_in, b_gate=v_b_gate, conv_w=v_conv_w, w_conv_out=v_w_conv_out, ssm_lam_re=v_ssm_lam_re,
               ssm_lam_im=v_ssm_lam_im, ssm_log_dt=v_ssm_log_dt, ssm_b_re=v_ssm_b_re, ssm_b_im=v_ssm_b_im,
               ssm_c_re=v_ssm_c_re, ssm_c_im=v_ssm_c_im, ssm_d=v_ssm_d, w_glu=v_w_glu, w_kv=v_w_kv,
               w_xattn_out=v_w_xattn_out, w_out=v_w_out, ln1_g=v_ln1_g, ln1_b=v_ln1_b, w_up=v_w_up, b_up=v_b_up,
               w_down=v_w_down, b_down=v_b_down, ln2_g=v_ln2_g, ln2_b=v_ln2_b)
    names = list(W)
    xy = 2 * lax.axis_index("x") + lax.axis_index("y")

    xs = x[0]
    S = xs.shape[0]
    mems = mem[0]
    tgt = loss_target[0]

    shard_bf = {n: W[n][0].astype(bf16) for n in BIG}

    lr = ssm_lam_re.reshape(1, N_STATES)
    li = ssm_lam_im.reshape(1, N_STATES)
    ldt = jnp.repeat(ssm_log_dt.reshape(SSM_GROUPS), SSM_STATE).reshape(1, N_STATES)
    bt_re, bt_im = _state_rows(ssm_b_re[0]), _state_rows(ssm_b_im[0])
    ar, ai, kr, ki, bbt_r, bbt_i = ssm_prep(lr, li, ldt, bt_re, bt_im)
    bd_r, bd_i = _block_diag_b(bbt_r).astype(bf16), _block_diag_b(bbt_i).astype(bf16)
    cd_r, cd_i = _block_diag_c(ssm_c_re[0]).astype(bf16), _block_diag_c(ssm_c_im[0]).astype(bf16)
    bdt_r, bdt_i = bd_r.transpose(0, 2, 1), bd_i.transpose(0, 2, 1)
    cdt_r, cdt_i = cd_r.transpose(0, 2, 1), cd_i.transpose(0, 2, 1)
    d_skip = ssm_d.reshape(1, SSM_WIDTH)

    mem_bf = mems.astype(bf16)
    u_col = GATE_COLS + 3 * CONV_WIDTH
    me_xy = jnp.reshape(xy, (1,)).astype(jnp.int32)
    proj, x_bf, (win3,) = proj_own(xs, shard_bf["w_in"], me_xy, 4, rides=(("gather2", shard_bf["w_in"]),))
    proj, (wco3, wglu3, wxo3, wkv4, wout4, convw4) = proj_rest(
        x_bf, win3, me_xy, proj, rides=tuple(("gather", shard_bf[n]) for n in MID) + (("gather", conv_w[0]),))
    wkv3 = wkv4.reshape(1, D_MODEL, 2 * XATTN_WIDTH)
    wout3 = wout4.reshape(1, D_MODEL, D_MODEL)
    convw_full = convw4.transpose(1, 0, 2).reshape(3, CONV_WIDTH)
    a_conv, y_a = conv_fwd(proj, convw_full, wco3)
    u_perm = _perm(proj[:, u_col:u_col + SSM_WIDTH])
    (init_r, init_i), (wup3,) = ssm_scan_fwd(u_perm, bd_r, bd_i, ar, ai, rides=(("gather2", shard_bf["w_up"]),))
    ysm_perm, ys_perm, st_r, st_i = ssm_scan_fwd(u_perm, bd_r, bd_i, ar, ai,
                                                 final=(cd_r, cd_i, d_skip, init_r, init_i))
    y_s = _unperm(ys_perm)
    glu = mm_fwd_small(y_s, wglu3, "glu", out_dtype=bf16)
    kv = mm_fwd_small(mem_bf, wkv3, "kv", out_dtype=bf16)
    o_att, y_c = attn_fwd(proj, kv, wxo3)
    merge_rows = ((proj, D_MODEL, 0), (proj, D_MODEL, 1), (proj, D_MODEL, 2), y_a, (glu, D_MODEL, 0),
                  (glu, D_MODEL, 1), y_c)
    merged, t1, h1, h1_bf = mm_rows_fused(
        (_merge_fwd_pro, merge_rows, (b_gate,)), wout3, "merge_w_out_ln1", transposed=False, tm=256,
        epi=_ln1_fwd_epi, extras=(xs,), consts=(ln1_g, ln1_b), outs=(f32, f32, bf16))
    (r_up, hdn), (wdn4,) = mm_fwd(h1_bf, wup3, "w_up", bias=b_up, outs=(bf16, bf16), second=_relu2,
                                  first=lambda v: jnp.maximum(v, 0.0), rides=(("gather2", shard_bf["w_down"]),))
    wdn3 = wdn4.reshape(1, D_FF, D_MODEL)

    dr2, dr2_bf, loss_cols, d_ln2_g, d_ln2_b, d_b_down = mm_rows_fused(
        hdn, wdn3, "w_down_ln2_loss", transposed=False, tm=512, epi=_ln2_loss_epi, extras=(h1, tgt),
        consts=(b_down, ln2_g, ln2_b), outs=(f32, bf16), nsums=4)
    part, other = {}, {}
    g_w_down = mm_bwd_w(hdn, dr2_bf, 1, "dw_down", tk=1024).reshape(4, -1, D_MODEL)
    dup, d_b_up = mm_bwd_x(dr2_bf, wdn3, "dup", tm=512, extras=(r_up,), colsum=True, out_dtype=bf16,
                           epi=lambda acc, r: acc * (2.0 * r.astype(f32)))
    g_w_up = mm_bwd_w(h1_bf, dup, 4, "dw_up")
    (dr1, dr1_bf, d_ln1_g, d_ln1_b), (recv_dn,) = mm_rows_fused(
        dup, wup3, "dh1_ln1_bwd", transposed=True, tm=512, epi=_ln1_bwd_epi, extras=(xs, t1, dr2), consts=(ln1_g,),
        outs=(f32, bf16), nsums=2, rides=(("scatter", g_w_down),))
    part["w_down"] = sum_slots(recv_dn, "sum_w_down")
    g_w_out = mm_bwd_w(merged, dr1_bf, 1, "dw_out").reshape(4, -1, D_MODEL)
    (dproj, dy_a, dglu, dy_c, d_b_gate), (recv_up,) = mm_rows_fused(
        dr1_bf, wout3, "dmerged_merge_bwd", transposed=True, tm=256, epi=_merge_bwd_epi, extras=merge_rows,
        consts=(b_gate,), outs=((bf16, GATE_COLS, IN_COLS), bf16, (bf16, 2 * D_MODEL, 2 * D_MODEL), bf16),
        sum_widths=(GATE_COLS,), rides=(("scatter", g_w_up),))
    part["w_up"] = sum_slots(recv_up, "sum_w_up")

    dproj, d_conv_w, g_w_co = conv_bwd(dy_a, a_conv, wco3, proj, convw_full, dproj)

    g_w_glu = mm_bwd_w_small(y_s, dglu, 4, "dw_glu")
    dys_perm = _perm(mm_bwd_x(dglu, wglu3, "dy_s", tm=1024))
    linit_r, linit_i = ssm_scan_bwd(dys_perm, ysm_perm, cdt_r, cdt_i, ar, ai)
    (du_perm, dbacc_r, dbacc_i, dcacc_r, dcacc_i, da_r, da_i, d_ssm_d), (recv_co, recv_glu, recv_out) = ssm_scan_bwd(
        dys_perm, ysm_perm, cdt_r, cdt_i, ar, ai,
        final=(u_perm, st_r, st_i, bdt_r, bdt_i, d_skip, linit_r, linit_i),
        rides=(("scatter", g_w_co), ("scatter", g_w_glu), ("scatter", g_w_out)))
    part["w_out"] = sum_slots(recv_out, "sum_w_out")
    part["w_conv_out"] = sum_slots(recv_co, "sum_w_conv_out")
    part["w_glu"] = sum_slots(recv_glu, "sum_w_glu")
    dbt_re, dbt_im, d_lr, d_li, d_ldt_state = ssm_param_bwd(
        _diag_b(dbacc_r), _diag_b(dbacc_i), bt_re, bt_im, kr, ki, ar, ai, lr, li, ldt, da_r, da_i)
    d_log_dt = group_sum(d_ldt_state.reshape(SSM_GROUPS, SSM_STATE))
    d_b_re = dbt_re.reshape(SSM_GROUP, SSM_GROUPS, SSM_STATE).transpose(1, 2, 0)
    d_b_im = dbt_im.reshape(SSM_GROUP, SSM_GROUPS, SSM_STATE).transpose(1, 2, 0)
    d_c_re = _diag_c(dcacc_r)
    d_c_im = -_diag_c(dcacc_i)

    dproj, dkv, g_w_xo = attn_bwd(dy_c, o_att, wxo3, proj, kv, dproj)
    g_w_kv = mm_bwd_w(mem_bf, dkv, 1, "dw_kv").reshape(4, -1, D_MODEL)

    dproj = lax.dynamic_update_slice(dproj, _unperm(du_perm), (0, u_col))
    g_small = {"b_gate": d_b_gate, "ssm_lam_re": d_lr, "ssm_lam_im": d_li, "ssm_log_dt": d_log_dt, "ssm_b_re": d_b_re,
               "ssm_b_im": d_b_im, "ssm_c_re": d_c_re, "ssm_c_im": d_c_im, "ssm_d": d_ssm_d, "ln1_g": d_ln1_g,
               "ln1_b": d_ln1_b, "b_up": d_b_up, "b_down": d_b_down, "ln2_g": d_ln2_g, "ln2_b": d_ln2_b}
    small_names = SMALL + ("conv_w",)
    g_small["conv_w"] = d_conv_w
    sizes = {n: (g_small[n].size + 1023) // 1024 * 1024 for n in small_names}
    pack = lambda d: jnp.concatenate([_pad_flat(d[n]) for n in small_names]).reshape(-1, 128)
    early = ("w_down", "w_up", "w_out", "w_conv_out", "w_glu")
    g_w_in, landed = mm_bwd_w(
        x_bf, dproj, 4, "dw_in", rides=(("scatter", g_w_xo), ("scatter", g_w_kv), ("all", pack(g_small)))
        + tuple(("pair", part[n]) for n in early))
    recv_xo, recv_kv, srecv = landed[:3]
    other.update(zip(early, landed[3:]))
    part["w_xattn_out"] = sum_slots(recv_xo, "sum_w_xattn_out")
    part["w_kv"] = sum_slots(recv_kv, "sum_w_kv")
    *sems, g_thru, land_thru, token = scatter_start(g_w_in)
    dx, = mm_rows_fused(dproj, win3, "dx", transposed=True, tm=512, extras=(dr1,), consts=(token,),
                        epi=lambda acc, d, tk: ((acc + ALPHA * d + tk,), ()), outs=(f32,))
    res = [{}, {}, {}, {}]
    for n in early:
        for k, r in enumerate(adam_pair(part[n], other[n], W[n][0], MOM[n][0], VEL[n][0], "adam_" + n)):
            res[k][n] = r[None]
    g_own, landed_in = scatter_wait(sems, g_thru, land_thru, dx)
    recv_in = lax.dynamic_update_slice(landed_in, lax.dynamic_index_in_dim(g_own, xy, 0, keepdims=True), (xy, 0, 0))
    late = ("w_in", "w_xattn_out", "w_kv")
    part["w_in"] = sum_slots(recv_in, "sum_w_in")
    other.update(zip(late, exchange(*[("pair", part[n]) for n in late], name="swap_late")))
    for n in late:
        for k, r in enumerate(adam_pair(part[n], other[n], W[n][0], MOM[n][0], VEL[n][0], "adam_" + n)):
            res[k][n] = r[None]
    conv_zero = jnp.zeros((3, CONV_WIDTH), f32)
    gs, ds_, ms, vs = adam_slots(srecv, pack({**{n: W[n] for n in SMALL}, "conv_w": conv_zero}),
                                 pack({**{n: MOM[n] for n in SMALL}, "conv_w": conv_zero}),
                                 pack({**{n: VEL[n] for n in SMALL}, "conv_w": conv_zero}), "adam_small")

    def unpack_small(buf):
        flat = buf.reshape(-1)
        out, r = {}, 0
        for n in small_names:
            ref = g_small[n] if n == "conv_w" else W[n]
            out[n] = flat[r:r + ref.size].reshape(ref.shape)
            r += sizes[n]
        return out

    res_s = [unpack_small(b) for b in (gs, ds_, ms, vs)]
    g_conv = lax.dynamic_slice(res_s[0]["conv_w"], (0, xy * 128), (3, 128))
    conv_slots = g_conv.reshape(1, 3, 128)
    cg, cd, cm, cv = adam_slots(conv_slots, conv_w[0], m_conv_w[0], v_conv_w[0], "adam_conv")
    conv_res = [cg, cd, cm, cv]

    loss = lax.psum(jnp.sum(loss_cols), ("x", "y", "c"))
    outs = [loss, dx.reshape(x.shape)]
    for k in range(4):
        for n in names:
            if n == "conv_w":
                outs.append(conv_res[k].reshape(conv_w.shape))
            elif n in BIG:
                outs.append(res[k][n])
            else:
                outs.append(res_s[k][n])
    return tuple(outs)
```

```python
import functools
import math

import jax
import jax.numpy as jnp
from jax import lax
from jax.experimental import pallas as pl
from jax.experimental.pallas import tpu as pltpu

f32 = jnp.float32
bf16 = jnp.bfloat16

D_MODEL = 1024
CONV_WIDTH = 512
SSM_WIDTH = 512
SSM_GROUP = 16
SSM_GROUPS = 32
SSM_STATE = 64
N_STATES = SSM_GROUPS * SSM_STATE
XATTN_HEADS = 4
XATTN_HEAD_DIM = 128
XATTN_WIDTH = 512
D_FF = 4096
GATE_COLS = 3 * D_MODEL
IN_COLS = GATE_COLS + 3 * CONV_WIDTH + SSM_WIDTH + XATTN_WIDTH
ALPHA = 2.0 ** 0.25
LN_EPS = 1e-5
ADAM_LR = 0.001
ADAM_B1 = 0.9
ADAM_B2 = 0.999
ADAM_EPS = 1e-08
ADAM_WD = 0.01
ADAM_STEP = 10

N_SEG = 8
SCAN_ROWS = 512
LANE_STRIP = 512
TM = 512
VMEM_LIMIT = 48 * 1024 * 1024
MESH = pl.DeviceIdType.MESH

NT_DIMS = (((1,), (1,)), ((), ()))
TN_DIMS = (((0,), (0,)), ((), ()))


def _params():
    return pltpu.CompilerParams(vmem_limit_bytes=VMEM_LIMIT)


def _full(shape):
    n = len(shape)
    return pl.BlockSpec(shape, lambda *_: (0,) * n)


def _rows(tm, w, cb=0):
    return pl.BlockSpec((tm, w), lambda i: (i, cb))


def _sig(x):
    return 1.0 / (1.0 + jnp.exp(-x))


HBM_SPEC = pl.BlockSpec(memory_space=pl.ANY)
RIDE_PEERS = {"gather": 3, "gather2": 6, "scatter": 3, "all": 7, "pair": 1}


def _xy_peers(x, y):
    return [(1 - x, y), (x, 1 - y), (1 - x, 1 - y)]


def _ride_copies(kind, src, dst, send_sems, recv_sems, local_sem):
    x, y, c = lax.axis_index("x"), lax.axis_index("y"), lax.axis_index("c")
    me = 2 * x + y
    if kind == "all":
        flips = [(fx, fy, fc) for fx in (0, 1) for fy in (0, 1) for fc in (0, 1)][1:]
        peers = [(x ^ fx, y ^ fy, c ^ fc) for fx, fy, fc in flips]
        slot = lambda p: 4 * p[0] + 2 * p[1] + p[2]
        mine = slot((x, y, c))
    else:
        peers = [(px, py, c) for px, py in _xy_peers(x, y)]
        slot = lambda p: 2 * p[0] + p[1]
        mine = me

    def remote(k, s, d):
        return pltpu.make_async_remote_copy(src_ref=s, dst_ref=d, send_sem=send_sems.at[k], recv_sem=recv_sems.at[k],
                                            device_id=peers[k], device_id_type=MESH)

    if kind == "pair":
        peers = [(x, y, 1 - c)]
        return None, [remote(0, src, dst)], [remote(0, src, dst)]
    if kind == "scatter":
        local = pltpu.make_async_copy(src.at[me], dst.at[me], local_sem)
        sends = [remote(k, src.at[slot(p)], dst.at[me]) for k, p in enumerate(peers)]
        lands = [remote(k, src.at[me], dst.at[slot(p)]) for k, p in enumerate(peers)]
    else:
        local = pltpu.make_async_copy(src, dst.at[mine], local_sem)
        sends = [remote(k, src, dst.at[mine]) for k, p in enumerate(peers)]
        lands = [remote(k, src, dst.at[slot(p)]) for k, p in enumerate(peers)]
    return local, sends, lands


def _ride_shape(kind, src):
    lead = {"gather": (4,), "gather2": (4,), "scatter": (), "all": (8,), "pair": ()}[kind]
    return jax.ShapeDtypeStruct(lead + src.shape, src.dtype)


def _two_level(src, dst, send_sems, recv_sems, local_sem):
    x, y, c = lax.axis_index("x"), lax.axis_index("y"), lax.axis_index("c")
    me = 2 * x + y
    h = src.shape[0] // 2
    mine = pl.ds(pl.multiple_of(c * h, 16), h)
    theirs = pl.ds(pl.multiple_of((1 - c) * h, 16), h)
    peers = _xy_peers(x, y)
    slots = [2 * px + py for px, py in peers]

    def over_ici(k, d):
        return pltpu.make_async_remote_copy(src_ref=src.at[mine], dst_ref=d, send_sem=send_sems.at[k],
                                            recv_sem=recv_sems.at[k], device_id=(*peers[k], c), device_id_type=MESH)

    def over_d2d(k, rows):
        blk = dst.at[slots[k], rows]
        return pltpu.make_async_remote_copy(src_ref=blk, dst_ref=blk, send_sem=send_sems.at[3 + k],
                                            recv_sem=recv_sems.at[3 + k], device_id=(x, y, 1 - c), device_id_type=MESH)

    return (pltpu.make_async_copy(src, dst.at[me], local_sem),
            [over_ici(k, dst.at[me, mine]) for k in range(3)], [over_ici(k, dst.at[slots[k], mine]) for k in range(3)],
            [over_d2d(k, mine) for k in range(3)], [over_d2d(k, theirs) for k in range(3)])


def _pcall(body, args, *, name, grid, in_specs, out_specs, out_shape, scratch_shapes=(), rides=(), aliases=None,
           prefetch=()):
    n_in, n_out, n_scr, nr, npf = len(in_specs), len(out_specs), len(scratch_shapes), len(rides), len(prefetch)
    kinds = [k for k, _ in rides]

    def wrapped(*refs):
        pre, refs = refs[:npf], refs[npf:]
        ins, rsrc = refs[:n_in], refs[n_in:n_in + nr]
        outs = refs[n_in + nr:n_in + nr + n_out]
        rdst = refs[n_in + nr + n_out:n_in + 2 * nr + n_out]
        scr = refs[n_in + 2 * nr + n_out:n_in + 2 * nr + n_out + n_scr]
        sems = refs[n_in + 2 * nr + n_out + n_scr:]

        def start():
            for r, kind in enumerate(kinds):
                if kind == "gather2":
                    local, sends, _, _, _ = _two_level(rsrc[r], rdst[r], *sems[3 * r:3 * r + 3])
                else:
                    local, sends, _ = _ride_copies(kind, rsrc[r], rdst[r], *sems[3 * r:3 * r + 3])
                if local is not None:
                    local.start()
                for cp in sends:
                    cp.start()

        def finish():
            for r, kind in enumerate(kinds):
                if kind == "gather2":
                    local, sends, ici_lands, forwards, lands = _two_level(rsrc[r], rdst[r], *sems[3 * r:3 * r + 3])
                    for k in range(3):
                        ici_lands[k].wait_recv()
                        forwards[k].start()
                    sends = sends + forwards
                else:
                    local, sends, lands = _ride_copies(kind, rsrc[r], rdst[r], *sems[3 * r:3 * r + 3])
                for cp in lands:
                    cp.wait_recv()
                for cp in sends:
                    cp.wait_send()
                if local is not None:
                    local.wait()

        if nr and grid:
            ids = [pl.program_id(a) for a in range(len(grid))]
            first = functools.reduce(jnp.logical_and, [i == 0 for i in ids])
            last = functools.reduce(jnp.logical_and, [i == g - 1 for i, g in zip(ids, grid)])
            pl.when(first)(start)
        elif nr:
            start()
        body(*pre, *ins, *outs, *scr)
        if nr and grid:
            pl.when(last)(finish)
        elif nr:
            finish()

    sems = []
    for kind in kinds:
        n = RIDE_PEERS[kind]
        sems += [pltpu.SemaphoreType.DMA((n,)), pltpu.SemaphoreType.DMA((n,)), pltpu.SemaphoreType.DMA(())]
    all_in = list(in_specs) + [HBM_SPEC] * nr
    all_out = list(out_specs) + [HBM_SPEC] * nr
    all_scratch = list(scratch_shapes) + sems
    if npf:
        how = dict(grid_spec=pltpu.PrefetchScalarGridSpec(num_scalar_prefetch=npf, grid=grid, in_specs=all_in,
                                                          out_specs=all_out, scratch_shapes=all_scratch))
    else:
        how = dict(grid=grid, in_specs=all_in, out_specs=all_out, scratch_shapes=all_scratch)
    res = pl.pallas_call(
        wrapped, name=name, out_shape=list(out_shape) + [_ride_shape(k, s) for k, s in rides],
        input_output_aliases=aliases or {}, compiler_params=_params(), **how)(
            *prefetch, *args, *[s for _, s in rides])
    return list(res[:n_out]), list(res[n_out:])


def _bf(v):
    return v if v.dtype == bf16 else v.astype(bf16)


def mm_fwd_small(a, w3, name, *, tm=1024, out_dtype=f32):
    M, K = a.shape
    J, _, n = w3.shape
    tm = min(tm, M)

    def body(a_ref, w_ref, o_ref):
        av = _bf(a_ref[...])
        for j in range(J):
            o_ref[:, j * n:(j + 1) * n] = jnp.dot(av, w_ref[j], preferred_element_type=f32).astype(out_dtype)

    return pl.pallas_call(
        body, name=name, grid=(M // tm,), in_specs=[_rows(tm, K), _full((J, K, n))], out_specs=_rows(tm, J * n),
        out_shape=jax.ShapeDtypeStruct((M, J * n), out_dtype), compiler_params=_params())(a, w3)


def mm_bwd_w_small(a, dy, J, name, *, tm=2048):
    M, K = a.shape
    n = dy.shape[1] // J
    tm = min(tm, M)
    ns = M // tm

    def body(a_ref, dy_ref, o_ref, acc_ref):
        s = pl.program_id(0)
        part = lax.dot_general(_bf(a_ref[...]), _bf(dy_ref[...]), TN_DIMS, preferred_element_type=f32)

        @pl.when(s == 0)
        def _():
            acc_ref[...] = part

        @pl.when(s > 0)
        def _():
            acc_ref[...] += part

        @pl.when(s == ns - 1)
        def _():
            for j in range(J):
                o_ref[j] = acc_ref[:, j * n:(j + 1) * n].astype(bf16)

    return pl.pallas_call(
        body, name=name, grid=(ns,), in_specs=[_rows(tm, K), _rows(tm, J * n)], out_specs=_full((J, K, n)),
        out_shape=jax.ShapeDtypeStruct((J, K, n), bf16), scratch_shapes=[pltpu.VMEM((K, J * n), f32)],
        compiler_params=_params())(a, dy)


def mm_fwd(a, w3, name, *, tm=1024, tn=None, bias=None, outs=(f32,), first=None, second=None, rides=()):
    M, K = a.shape
    J, _, n = w3.shape
    tm = min(tm, M)
    tn = tn or n
    nl = n // tn
    nb = 0 if bias is None else 1

    def body(*refs):
        a_ref, w_ref = refs[0], refs[1]
        acc = jnp.dot(_bf(a_ref[...]), w_ref[0], preferred_element_type=f32)
        if bias is not None:
            acc = acc + refs[2][...]
        refs[2 + nb][...] = (acc if first is None else first(acc)).astype(outs[0])
        if len(outs) > 1:
            refs[3 + nb][...] = second(acc).astype(outs[1])

    in_specs = [pl.BlockSpec((tm, K), lambda j, l, i: (i, 0)),
                pl.BlockSpec((1, K, tn), lambda j, l, i: (j, 0, l))]
    args = [a, w3]
    if bias is not None:
        in_specs.append(pl.BlockSpec((1, tn), lambda j, l, i: (0, j * nl + l)))
        args.append(bias)
    res, landed = _pcall(
        body, args, name=name, grid=(J, nl, M // tm), in_specs=in_specs,
        out_specs=[pl.BlockSpec((tm, tn), lambda j, l, i: (i, j * nl + l)) for _ in outs],
        out_shape=[jax.ShapeDtypeStruct((M, J * n), dt) for dt in outs], rides=rides)
    res = res if len(outs) > 1 else res[0]
    return (res, landed) if rides else res


def proj_own(x, w_own, me, n_blocks, *, tm=1024, rides=()):
    M, K = x.shape
    n = w_own.shape[1]

    def body(me_ref, x_ref, w_ref, o_ref, xb_ref):
        av = _bf(x_ref[...])
        xb_ref[...] = av
        o_ref[...] = jnp.dot(av, w_ref[...], preferred_element_type=f32).astype(bf16)

    (proj, x_bf), landed = _pcall(
        body, [x, w_own], name="proj_own", grid=(M // tm,), prefetch=(me,),
        in_specs=[pl.BlockSpec((tm, K), lambda i, me_ref: (i, 0)), pl.BlockSpec((K, n), lambda i, me_ref: (0, 0))],
        out_specs=[pl.BlockSpec((tm, n), lambda i, me_ref: (i, me_ref[0])),
                   pl.BlockSpec((tm, K), lambda i, me_ref: (i, 0))],
        out_shape=[jax.ShapeDtypeStruct((M, n_blocks * n), bf16), jax.ShapeDtypeStruct((M, K), bf16)], rides=rides)
    return proj, x_bf, landed


def proj_rest(x_bf, w3, me, proj, *, tm=1024, rides=()):
    M, K = x_bf.shape
    J, _, n = w3.shape

    def body(me_ref, x_ref, w_ref, _, o_ref):
        o_ref[...] = jnp.dot(x_ref[...], w_ref[0], preferred_element_type=f32).astype(bf16)

    (proj,), landed = _pcall(
        body, [x_bf, w3, proj], name="proj_rest", grid=(J - 1, M // tm), prefetch=(me,),
        in_specs=[pl.BlockSpec((tm, K), lambda k, i, me_ref: (i, 0)),
                  pl.BlockSpec((1, K, n), lambda k, i, me_ref: (me_ref[0] ^ (k + 1), 0, 0)), HBM_SPEC],
        out_specs=[pl.BlockSpec((tm, n), lambda k, i, me_ref: (i, me_ref[0] ^ (k + 1)))],
        out_shape=[jax.ShapeDtypeStruct(proj.shape, bf16)], aliases={3: 0}, rides=rides)
    return proj, landed


def mm_bwd_x(dy, w3, name, *, tm=TM, epi=None, extras=(), colsum=False, out_dtype=f32, rides=()):
    M = dy.shape[0]
    J, K, n = w3.shape
    tm = min(tm, M)
    nex = len(extras)

    def body(*refs):
        dy_ref, w_hbm = refs[0], refs[1]
        ex = refs[2:2 + nex]
        o_ref = refs[2 + nex]
        w_ref = refs[-1]
        i = pl.program_id(0)

        @pl.when(i == 0)
        def _():
            pltpu.sync_copy(w_hbm, w_ref)

        acc = None
        for j in range(J):
            part = lax.dot_general(_bf(dy_ref[:, j * n:(j + 1) * n]), w_ref[j], NT_DIMS, preferred_element_type=f32)
            acc = part if acc is None else acc + part
        if epi is not None:
            acc = epi(acc, *[e[...] for e in ex])
        o_ref[...] = acc.astype(out_dtype)
        if colsum:
            s_ref = refs[3 + nex]
            cs = jnp.sum(acc, axis=0, keepdims=True)

            @pl.when(i == 0)
            def _():
                s_ref[...] = cs

            @pl.when(i > 0)
            def _():
                s_ref[...] += cs

    in_specs = [pl.BlockSpec((tm, J * n), lambda i: (i, 0)), pl.BlockSpec(memory_space=pl.ANY)]
    in_specs += [pl.BlockSpec((tm, K), lambda i: (i, 0)) for _ in extras]
    out_specs = [pl.BlockSpec((tm, K), lambda i: (i, 0))]
    out_shape = [jax.ShapeDtypeStruct((M, K), out_dtype)]
    if colsum:
        out_specs.append(pl.BlockSpec((1, K), lambda i: (0, 0)))
        out_shape.append(jax.ShapeDtypeStruct((1, K), f32))
    res, landed = _pcall(body, [dy, w3, *extras], name=name, grid=(M // tm,), in_specs=in_specs, out_specs=out_specs,
                         out_shape=out_shape, scratch_shapes=[pltpu.VMEM((J, K, n), bf16)], rides=rides)
    res = res if colsum else res[0]
    return (res, landed) if rides else res


def mm_rows_fused(a, w3, name, *, transposed, tm, epi, extras=(), consts=(), outs=(f32,), nsums=0, sum_widths=None,
                  rides=()):
    pro = a[0] if isinstance(a, tuple) else None
    J, K, n = w3.shape
    W = K if transposed else n
    ka = J * n if transposed else K
    if pro is not None:
        a_rows = [e if isinstance(e, tuple) else (e, ka, 0) for e in a[1]]
        a_consts = list(a[2])
    else:
        a_rows, a_consts = [(a, ka, 0)], []
    M = a_rows[0][0].shape[0]
    tm = min(tm, M)
    na = len(a_rows) + len(a_consts)
    nex, nco = len(extras), len(consts)
    if pro is not None:
        outs = (bf16 if ka == W else (bf16, ka, ka),) + tuple(outs)
    nout = len(outs)
    nsums = len(sum_widths) if sum_widths is not None else nsums
    out_dtypes = [o[0] if isinstance(o, tuple) else o for o in outs]

    def body(*refs):
        a_refs, w_hbm = refs[:na], refs[na]
        ex = refs[na + 1:na + 1 + nex]
        co = refs[na + 1 + nex:na + 1 + nex + nco]
        o_refs = refs[na + 1 + nex + nco:na + 1 + nex + nco + nout]
        s_refs = refs[na + 1 + nex + nco + nout:na + 1 + nex + nco + nout + nsums]
        w_ref = refs[-1]
        i = pl.program_id(0)

        @pl.when(i == 0)
        def _():
            pltpu.sync_copy(w_hbm, w_ref)

        av = _bf(a_refs[0][...] if pro is None else pro(*[r[...] for r in a_refs]))
        if transposed:
            acc = None
            for j in range(J):
                part = lax.dot_general(av[:, j * n:(j + 1) * n], w_ref[j], NT_DIMS, preferred_element_type=f32)
                acc = part if acc is None else acc + part
        else:
            acc = jnp.dot(av, w_ref[0], preferred_element_type=f32)
        rows, sums = epi(acc, *[e[...] for e in ex], *[c[...] for c in co])
        if pro is not None:
            rows = (av,) + tuple(rows)
        for o_ref, v, dt in zip(o_refs, rows, out_dtypes):
            o_ref[...] = v.astype(dt)
        for s_ref, v in zip(s_refs, sums):
            @pl.when(i == 0)
            def _(s_ref=s_ref, v=v):
                s_ref[...] = v

            @pl.when(i > 0)
            def _(s_ref=s_ref, v=v):
                s_ref[...] += v

    extras = [e if isinstance(e, tuple) else (e, W, 0) for e in extras]
    outs_full = [o if isinstance(o, tuple) else (o, W, W) for o in outs]
    widths = list(sum_widths) if sum_widths is not None else [W] * nsums
    in_specs = [_rows(tm, w, cb) for _, w, cb in a_rows] + [_full(c.shape) for c in a_consts]
    in_specs += [pl.BlockSpec(memory_space=pl.ANY)]
    in_specs += [_rows(tm, w, cb) for _, w, cb in extras] + [_full(c.shape) for c in consts]
    out_specs = [_rows(tm, w) for _, w, _ in outs_full] + [_full((1, w)) for w in widths]
    out_shape = ([jax.ShapeDtypeStruct((M, cols), dt) for dt, _, cols in outs_full]
                 + [jax.ShapeDtypeStruct((1, w), f32) for w in widths])
    args = [e for e, _, _ in a_rows] + a_consts + [w3] + [e for e, _, _ in extras] + list(consts)
    res, landed = _pcall(body, args, name=name, grid=(M // tm,), in_specs=in_specs, out_specs=out_specs,
                         out_shape=out_shape, scratch_shapes=[pltpu.VMEM((J, K, n), bf16)], rides=rides)
    return (res, landed) if rides else res


def mm_bwd_w(a, dy, J, name, *, tm=2048, tn=None, tk=None, rides=()):
    M, K = a.shape
    n = dy.shape[1] // J
    tm = min(tm, M)
    tn = tn or n
    tk = tk or K
    nl = n // tn
    nk = K // tk
    ns = M // tm

    def body(a_ref, dy_ref, o_ref, acc_ref):
        s = pl.program_id(3)
        part = lax.dot_general(_bf(a_ref[...]), _bf(dy_ref[...]), TN_DIMS, preferred_element_type=f32)

        @pl.when(s == 0)
        def _():
            acc_ref[...] = part

        @pl.when(s > 0)
        def _():
            acc_ref[...] += part

        @pl.when(s == ns - 1)
        def _():
            o_ref[0] = acc_ref[...].astype(bf16)

    res, landed = _pcall(
        body, [a, dy], name=name, grid=(J, nl, nk, ns),
        in_specs=[pl.BlockSpec((tm, tk), lambda j, l, k, s: (s, k)),
                  pl.BlockSpec((tm, tn), lambda j, l, k, s: (s, j * nl + l))],
        out_specs=[pl.BlockSpec((1, tk, tn), lambda j, l, k, s: (j, k, l))],
        out_shape=[jax.ShapeDtypeStruct((J, K, n), bf16)],
        scratch_shapes=[pltpu.VMEM((tk, tn), f32)], rides=rides)
    return (res[0], landed) if rides else res[0]


def _relu2(v):
    r = jnp.maximum(v, 0.0)
    return r * r


HALO = 16


def _shift_down(z, k, halo):
    r = lax.broadcasted_iota(jnp.int32, z.shape, 0)
    y = pltpu.roll(z, k, 0)
    for q in range(k):
        y = jnp.where(r == q, halo[HALO - k + q:HALO - k + q + 1, :], y)
    return y


def _shift_up(z, k, halo):
    tm = z.shape[0]
    r = lax.broadcasted_iota(jnp.int32, z.shape, 0)
    y = pltpu.roll(z, tm - k, 0)
    for q in range(k):
        y = jnp.where(r == tm - k + q, halo[q:q + 1, :], y)
    return y


def _prev_halo(tm, cb):
    return pl.BlockSpec((HALO, CONV_WIDTH), lambda i: (jnp.maximum(i * (tm // HALO) - 1, 0), cb))


def _next_halo(tm, cb, nblk):
    return pl.BlockSpec((HALO, CONV_WIDTH), lambda i: (jnp.minimum((i + 1) * (tm // HALO), nblk - 1), cb))


def _f32(ref):
    return ref[...].astype(f32)


def conv_fwd(proj, conv_w, wco3, tm=TM):
    S = proj.shape[0]
    J, _, n = wco3.shape

    def body(cb_ref, cc_ref, ch_ref, cch_ref, chh_ref, w_ref, wo_ref, a_ref, y_ref):
        i = pl.program_id(0)
        z = _f32(cc_ref) * _f32(ch_ref)
        zh = jnp.where(i == 0, 0.0, _f32(cch_ref) * _f32(chh_ref))
        w = w_ref[...]
        dwz = w[0:1, :] * _shift_down(z, 2, zh) + w[1:2, :] * _shift_down(z, 1, zh) + w[2:3, :] * z
        a = (_f32(cb_ref) * dwz).astype(bf16)
        a_ref[...] = a
        for j in range(J):
            y_ref[:, j * n:(j + 1) * n] = jnp.dot(a, wo_ref[j], preferred_element_type=f32).astype(bf16)

    return pl.pallas_call(
        body, name="conv_fwd", grid=(S // tm,),
        in_specs=[_rows(tm, CONV_WIDTH, 6), _rows(tm, CONV_WIDTH, 7), _rows(tm, CONV_WIDTH, 8),
                  _prev_halo(tm, 7), _prev_halo(tm, 8), _full((3, CONV_WIDTH)), _full(wco3.shape)],
        out_specs=[_rows(tm, CONV_WIDTH), _rows(tm, J * n)],
        out_shape=[jax.ShapeDtypeStruct((S, CONV_WIDTH), bf16), jax.ShapeDtypeStruct((S, J * n), bf16)],
        compiler_params=_params())(proj, proj, proj, proj, proj, conv_w, wco3)


def conv_bwd(dy_a, a_conv, wco3, proj, conv_w, dproj, tm=TM):
    S = proj.shape[0]
    nt = S // tm
    J, _, n = wco3.shape

    def body(dy_ref, dyn_ref, a_ref, wo_ref, cb_ref, cc_ref, ch_ref, cch_ref, chh_ref, cbn_ref, w_ref, _, o_ref,
             dw_ref, dwo_ref, acc_ref):
        i = pl.program_id(0)

        def times_wo_t(dy):
            out = None
            for j in range(J):
                part = lax.dot_general(dy[:, j * n:(j + 1) * n], wo_ref[j], NT_DIMS, preferred_element_type=f32)
                out = part if out is None else out + part
            return out

        dyv = dy_ref[...]
        dav, dan = times_wo_t(dyv), times_wo_t(dyn_ref[...])
        part = lax.dot_general(a_ref[...], dyv, TN_DIMS, preferred_element_type=f32)

        @pl.when(i == 0)
        def _():
            acc_ref[...] = part

        @pl.when(i > 0)
        def _():
            acc_ref[...] += part

        @pl.when(i == nt - 1)
        def _():
            for j in range(J):
                dwo_ref[j] = acc_ref[:, j * n:(j + 1) * n].astype(bf16)

        cc, ch, cb = _f32(cc_ref), _f32(ch_ref), _f32(cb_ref)
        z = cc * ch
        zh = jnp.where(i == 0, 0.0, _f32(cch_ref) * _f32(chh_ref))
        w = w_ref[...]
        z1 = _shift_down(z, 1, zh)
        z2 = _shift_down(z, 2, zh)
        dwz = w[0:1, :] * z2 + w[1:2, :] * z1 + w[2:3, :] * z
        g = dav * cb
        gn = jnp.where(i == nt - 1, 0.0, dan * _f32(cbn_ref))
        dz = w[2:3, :] * g + w[1:2, :] * _shift_up(g, 1, gn) + w[0:1, :] * _shift_up(g, 2, gn)
        o_ref[:, 0:CONV_WIDTH] = (dav * dwz).astype(bf16)
        o_ref[:, CONV_WIDTH:2 * CONV_WIDTH] = (dz * ch).astype(bf16)
        o_ref[:, 2 * CONV_WIDTH:3 * CONV_WIDTH] = (dz * cc).astype(bf16)

        @pl.when(i == 0)
        def _():
            dw_ref[...] = jnp.zeros_like(dw_ref)

        dw_ref[0:1, :] += jnp.sum(g * z2, axis=0, keepdims=True)
        dw_ref[1:2, :] += jnp.sum(g * z1, axis=0, keepdims=True)
        dw_ref[2:3, :] += jnp.sum(g * z, axis=0, keepdims=True)

    return pl.pallas_call(
        body, name="conv_bwd", grid=(nt,),
        in_specs=[_rows(tm, J * n),
                  pl.BlockSpec((HALO, J * n), lambda i: (jnp.minimum((i + 1) * (tm // HALO), S // HALO - 1), 0)),
                  _rows(tm, CONV_WIDTH), _full(wco3.shape),
                  _rows(tm, CONV_WIDTH, 6), _rows(tm, CONV_WIDTH, 7), _rows(tm, CONV_WIDTH, 8),
                  _prev_halo(tm, 7), _prev_halo(tm, 8), _next_halo(tm, 6, S // HALO), _full((3, CONV_WIDTH)),
                  pl.BlockSpec(memory_space=pl.ANY)],
        out_specs=[_rows(tm, 3 * CONV_WIDTH, GATE_COLS // (3 * CONV_WIDTH)), _full((3, CONV_WIDTH)),
                   _full(wco3.shape)],
        out_shape=[jax.ShapeDtypeStruct(dproj.shape, bf16), jax.ShapeDtypeStruct((3, CONV_WIDTH), f32),
                   jax.ShapeDtypeStruct(wco3.shape, bf16)],
        scratch_shapes=[pltpu.VMEM((CONV_WIDTH, J * n), f32)], input_output_aliases={11: 0},
        compiler_params=_params())(dy_a, dy_a, a_conv, wco3, proj, proj, proj, proj, proj, proj, conv_w, dproj)


def _cmul(ar, ai, br, bi):
    return ar * br - ai * bi, ar * bi + ai * br


def ssm_prep(lr, li, ldt, bt_re, bt_im):
    def body(lr_ref, li_ref, ldt_ref, br_ref, bi_ref, ar_ref, ai_ref, kr_ref, ki_ref, bbr_ref, bbi_ref):
        lrv, liv = lr_ref[...], li_ref[...]
        dt = jnp.exp(ldt_ref[...])
        mag = jnp.exp(lrv * dt)
        ar = mag * jnp.cos(liv * dt)
        ai = mag * jnp.sin(liv * dt)
        den = lrv * lrv + liv * liv
        nr = ar - 1.0
        kr = (nr * lrv + ai * liv) / den
        ki = (ai * lrv - nr * liv) / den
        ar_ref[...], ai_ref[...], kr_ref[...], ki_ref[...] = ar, ai, kr, ki
        bbr_ref[...] = kr * br_ref[...] - ki * bi_ref[...]
        bbi_ref[...] = kr * bi_ref[...] + ki * br_ref[...]

    v = jax.ShapeDtypeStruct((1, N_STATES), f32)
    m = jax.ShapeDtypeStruct((SSM_GROUP, N_STATES), f32)
    return pl.pallas_call(body, name="ssm_prep", out_shape=[v, v, v, v, m, m])(lr, li, ldt, bt_re, bt_im)


def _pow_segment(ar, ai, seg_len):
    pr, pi = ar, ai
    for _ in range(int(math.log2(seg_len))):
        pr, pi = _cmul(pr, pi, pr, pi)
    return pr, pi


GELU_K = math.sqrt(2.0 / math.pi)
GELU_C = 0.044715


def _gelu(v):
    return 0.5 * v * (1.0 + jnp.tanh(GELU_K * (v + GELU_C * v * v * v)))


def _gelu_grad(v):
    t = jnp.tanh(GELU_K * (v + GELU_C * v * v * v))
    return 0.5 * (1.0 + t) + 0.5 * v * (1.0 - t * t) * GELU_K * (1.0 + 3.0 * GELU_C * v * v)


def ssm_scan_fwd(u_perm, bd_r, bd_i, ar, ai, *, final=None, rides=()):
    S = u_perm.shape[0]
    R = SCAN_ROWS
    nblk = S // R
    seg_len = S // N_SEG
    nstrip = N_STATES // LANE_STRIP
    store = final is not None

    def body(*refs):
        if store:
            (u_ref, bdr_ref, bdi_ref, ar_ref, ai_ref, cdr_ref, cdi_ref, d_ref, ir_ref, ii_ref,
             y_ref, ys_ref, sr_ref, si_ref, bur, bui, car, cai) = refs
        else:
            u_ref, bdr_ref, bdi_ref, ar_ref, ai_ref, or_ref, oi_ref, bur, bui, car, cai = refs
        i = pl.program_id(0)
        ub = _bf(u_ref[...])
        u = ub.astype(f32)
        for cb in range(4):
            us = ub[:, cb * 128:(cb + 1) * 128]
            bur[:, cb * 512:(cb + 1) * 512] = jnp.dot(us, bdr_ref[cb], preferred_element_type=f32)
            bui[:, cb * 512:(cb + 1) * 512] = jnp.dot(us, bdi_ref[cb], preferred_element_type=f32)

        @pl.when(i == 0)
        def _():
            if store:
                car[...] = ir_ref[...]
                cai[...] = ii_ref[...]
            else:
                car[...] = jnp.zeros_like(car)
                cai[...] = jnp.zeros_like(cai)

        for ls in range(nstrip):
            lanes = pl.ds(ls * LANE_STRIP, LANE_STRIP)
            a_r = jnp.broadcast_to(ar_ref[:, lanes], (N_SEG, LANE_STRIP))
            a_i = jnp.broadcast_to(ai_ref[:, lanes], (N_SEG, LANE_STRIP))

            def step(t, carry, lanes=lanes, a_r=a_r, a_i=a_i):
                s_r, s_i = carry
                row = pl.multiple_of(t * 8, 8)
                n_r = a_r * s_r - a_i * s_i + bur[pl.ds(row, 8), lanes]
                n_i = a_r * s_i + a_i * s_r + bui[pl.ds(row, 8), lanes]
                if store:
                    sr_ref[pl.ds(row, 8), lanes] = n_r
                    si_ref[pl.ds(row, 8), lanes] = n_i
                return n_r, n_i

            e_r, e_i = lax.fori_loop(0, R // 8, step, (car[:, lanes], cai[:, lanes]), unroll=True)
            car[:, lanes] = e_r
            cai[:, lanes] = e_i

        if store:
            for cb in range(4):
                st_r = sr_ref[:, cb * 512:(cb + 1) * 512].astype(bf16)
                st_i = si_ref[:, cb * 512:(cb + 1) * 512].astype(bf16)
                y = (jnp.dot(st_r, cdr_ref[cb], preferred_element_type=f32)
                     - jnp.dot(st_i, cdi_ref[cb], preferred_element_type=f32))
                cols = slice(cb * 128, (cb + 1) * 128)
                y = y + d_ref[:, cols] * u[:, cols]
                y_ref[:, cols] = y
                ys_ref[:, cols] = _gelu(y).astype(bf16)
        else:
            @pl.when(i == nblk - 1)
            def _():
                p_r, p_i = _pow_segment(ar_ref[...], ai_ref[...], seg_len)
                t_r, t_i = car[0:1, :], cai[0:1, :]
                or_ref[0:1, :] = jnp.zeros((1, N_STATES), f32)
                oi_ref[0:1, :] = jnp.zeros((1, N_STATES), f32)
                for j in range(1, N_SEG):
                    or_ref[j:j + 1, :] = t_r
                    oi_ref[j:j + 1, :] = t_i
                    m_r, m_i = _cmul(p_r, p_i, t_r, t_i)
                    t_r, t_i = car[j:j + 1, :] + m_r, cai[j:j + 1, :] + m_i

    blk = lambda w: pl.BlockSpec((R, w), lambda i: (i, 0))
    in_specs = [blk(SSM_WIDTH), _full((4, 128, 512)), _full((4, 128, 512)), _full((1, N_STATES)), _full((1, N_STATES))]
    args = [u_perm, bd_r, bd_i, ar, ai]
    scratch = [pltpu.VMEM((R, N_STATES), f32), pltpu.VMEM((R, N_STATES), f32),
               pltpu.VMEM((N_SEG, N_STATES), f32), pltpu.VMEM((N_SEG, N_STATES), f32)]
    if store:
        in_specs += [_full((4, 512, 128)), _full((4, 512, 128)), _full((1, SSM_WIDTH)),
                     _full((N_SEG, N_STATES)), _full((N_SEG, N_STATES))]
        args += list(final)
        out_specs = [blk(SSM_WIDTH), blk(SSM_WIDTH), blk(N_STATES), blk(N_STATES)]
        out_shape = [jax.ShapeDtypeStruct((S, SSM_WIDTH), f32), jax.ShapeDtypeStruct((S, SSM_WIDTH), bf16),
                     jax.ShapeDtypeStruct((S, N_STATES), f32), jax.ShapeDtypeStruct((S, N_STATES), f32)]
        name = "ssm_scan_fwd"
    else:
        out_specs = [_full((N_SEG, N_STATES)), _full((N_SEG, N_STATES))]
        out_shape = [jax.ShapeDtypeStruct((N_SEG, N_STATES), f32)] * 2
        name = "ssm_scan_fwd_carry"
    res, landed = _pcall(body, args, name=name, grid=(nblk,), in_specs=in_specs, out_specs=out_specs,
                         out_shape=out_shape, scratch_shapes=scratch, rides=rides)
    return (res, landed) if rides else res


def ssm_scan_bwd(dys_perm, y_perm, cdt_r, cdt_i, ar, ai, *, final=None, rides=()):
    S = dys_perm.shape[0]
    R = SCAN_ROWS
    nblk = S // R
    seg_len = S // N_SEG
    store = final is not None
    strip = LANE_STRIP // 2 if store else LANE_STRIP
    nstrip = N_STATES // strip

    def body(*refs):
        if store:
            (dys_ref, y_ref, cdr_ref, cdi_ref, ar_ref, ai_ref, u_ref, sr_ref, si_ref, bdr_ref, bdi_ref, d_ref, ir_ref,
             ii_ref, du_ref, dbr_ref, dbi_ref, dcr_ref, dci_ref, dar_ref, dai_ref, dd_ref, dsr, dsi, lmr, lmi, car,
             cai) = refs
        else:
            dys_ref, y_ref, cdr_ref, cdi_ref, ar_ref, ai_ref, or_ref, oi_ref, dsr, dsi, car, cai = refs
        i = pl.program_id(0)
        dy = dys_ref[...] * _gelu_grad(y_ref[...])
        dyb = dy.astype(bf16)
        for cb in range(4):
            ds_ = dyb[:, cb * 128:(cb + 1) * 128]
            dsr[:, cb * 512:(cb + 1) * 512] = jnp.dot(ds_, cdr_ref[cb], preferred_element_type=f32)
            dsi[:, cb * 512:(cb + 1) * 512] = -jnp.dot(ds_, cdi_ref[cb], preferred_element_type=f32)

        @pl.when(i == 0)
        def _():
            if store:
                car[...] = ir_ref[...]
                cai[...] = ii_ref[...]
                dar_ref[...] = jnp.zeros_like(dar_ref)
                dai_ref[...] = jnp.zeros_like(dai_ref)
                dbr_ref[...] = jnp.zeros_like(dbr_ref)
                dbi_ref[...] = jnp.zeros_like(dbi_ref)
                dcr_ref[...] = jnp.zeros_like(dcr_ref)
                dci_ref[...] = jnp.zeros_like(dci_ref)
                dd_ref[...] = jnp.zeros_like(dd_ref)
            else:
                car[...] = jnp.zeros_like(car)
                cai[...] = jnp.zeros_like(cai)

        for ls in range(nstrip):
            lanes = pl.ds(ls * strip, strip)
            a_r = jnp.broadcast_to(ar_ref[:, lanes], (N_SEG, strip))
            a_i = jnp.broadcast_to(ai_ref[:, lanes], (N_SEG, strip))
            zero = jnp.zeros((N_SEG, strip), f32)

            def step(k, carry, lanes=lanes, a_r=a_r, a_i=a_i):
                l_r, l_i, g_r, g_i = carry
                row = pl.multiple_of((R // 8 - 1 - k) * 8, 8)
                if store:
                    s_r = sr_ref[pl.ds(row, 8), lanes]
                    s_i = si_ref[pl.ds(row, 8), lanes]
                    g_r = g_r + l_r * s_r + l_i * s_i
                    g_i = g_i + l_i * s_r - l_r * s_i
                n_r = dsr[pl.ds(row, 8), lanes] + a_r * l_r + a_i * l_i
                n_i = dsi[pl.ds(row, 8), lanes] + a_r * l_i - a_i * l_r
                if store:
                    lmr[pl.ds(row, 8), lanes] = n_r
                    lmi[pl.ds(row, 8), lanes] = n_i
                return n_r, n_i, g_r, g_i

            e_r, e_i, g_r, g_i = lax.fori_loop(0, R // 8, step, (car[:, lanes], cai[:, lanes], zero, zero),
                                               unroll=True)
            car[:, lanes] = e_r
            cai[:, lanes] = e_i
            if store:
                dar_ref[:, lanes] += g_r
                dai_ref[:, lanes] += g_i

        if store:
            ub = _bf(u_ref[...])
            u = ub.astype(f32)
            for cb in range(4):
                cols = slice(cb * 128, (cb + 1) * 128)
                st = slice(cb * 512, (cb + 1) * 512)
                l_r = lmr[:, st].astype(bf16)
                l_i = lmi[:, st].astype(bf16)
                du = (jnp.dot(l_r, bdr_ref[cb], preferred_element_type=f32)
                      + jnp.dot(l_i, bdi_ref[cb], preferred_element_type=f32))
                du_ref[:, cols] = (du + d_ref[:, cols] * dy[:, cols]).astype(bf16)
                dbr_ref[cb] += lax.dot_general(l_r, ub[:, cols], TN_DIMS, preferred_element_type=f32)
                dbi_ref[cb] += lax.dot_general(l_i, ub[:, cols], TN_DIMS, preferred_element_type=f32)
                dcr_ref[cb] += lax.dot_general(dyb[:, cols], sr_ref[:, st].astype(bf16), TN_DIMS,
                                               preferred_element_type=f32)
                dci_ref[cb] += lax.dot_general(dyb[:, cols], si_ref[:, st].astype(bf16), TN_DIMS,
                                               preferred_element_type=f32)
            dd_ref[...] += jnp.sum(dy * u, axis=0, keepdims=True)
        else:
            @pl.when(i == nblk - 1)
            def _():
                p_r, p_i = _pow_segment(ar_ref[...], ai_ref[...], seg_len)
                p_i = -p_i
                t_r, t_i = car[N_SEG - 1:N_SEG, :], cai[N_SEG - 1:N_SEG, :]
                or_ref[N_SEG - 1:N_SEG, :] = jnp.zeros((1, N_STATES), f32)
                oi_ref[N_SEG - 1:N_SEG, :] = jnp.zeros((1, N_STATES), f32)
                for j in range(N_SEG - 2, -1, -1):
                    or_ref[j:j + 1, :] = t_r
                    oi_ref[j:j + 1, :] = t_i
                    m_r, m_i = _cmul(p_r, p_i, t_r, t_i)
                    t_r, t_i = car[j:j + 1, :] + m_r, cai[j:j + 1, :] + m_i

    blk = lambda w: pl.BlockSpec((R, w), lambda i: (nblk - 1 - i, 0))
    in_specs = [blk(SSM_WIDTH), blk(SSM_WIDTH), _full((4, 128, 512)), _full((4, 128, 512)), _full((1, N_STATES)),
                _full((1, N_STATES))]
    args = [dys_perm, y_perm, cdt_r, cdt_i, ar, ai]
    seg = jax.ShapeDtypeStruct((N_SEG, N_STATES), f32)
    if store:
        in_specs += [blk(SSM_WIDTH), blk(N_STATES), blk(N_STATES), _full((4, 512, 128)), _full((4, 512, 128)),
                     _full((1, SSM_WIDTH)), _full((N_SEG, N_STATES)), _full((N_SEG, N_STATES))]
        args += list(final)
        out_specs = [blk(SSM_WIDTH), _full((4, 512, 128)), _full((4, 512, 128)), _full((4, 128, 512)),
                     _full((4, 128, 512)), _full((N_SEG, N_STATES)), _full((N_SEG, N_STATES)), _full((1, SSM_WIDTH))]
        b_acc = jax.ShapeDtypeStruct((4, 512, 128), f32)
        c_acc = jax.ShapeDtypeStruct((4, 128, 512), f32)
        out_shape = [jax.ShapeDtypeStruct((S, SSM_WIDTH), bf16), b_acc, b_acc, c_acc, c_acc, seg, seg,
                     jax.ShapeDtypeStruct((1, SSM_WIDTH), f32)]
        scratch = [pltpu.VMEM((R, N_STATES), f32)] * 4 + [pltpu.VMEM((N_SEG, N_STATES), f32)] * 2
        name = "ssm_scan_bwd"
    else:
        out_specs = [_full((N_SEG, N_STATES)), _full((N_SEG, N_STATES))]
        out_shape = [seg, seg]
        scratch = [pltpu.VMEM((R, N_STATES), f32)] * 2 + [pltpu.VMEM((N_SEG, N_STATES), f32)] * 2
        name = "ssm_scan_bwd_carry"
    res, landed = _pcall(body, args, name=name, grid=(nblk,), in_specs=in_specs, out_specs=out_specs,
                         out_shape=out_shape, scratch_shapes=scratch, rides=rides)
    return (res, landed) if rides else res


def ssm_param_bwd(dbb_r, dbb_i, bt_re, bt_im, kr, ki, ar, ai, lr, li, ldt, da_r, da_i):
    def body(dbr_ref, dbi_ref, br_ref, bi_ref, kr_ref, ki_ref, ar_ref, ai_ref, lr_ref, li_ref, ldt_ref, dar_ref,
             dai_ref, obr_ref, obi_ref, olr_ref, oli_ref, odt_ref):
        dbr, dbi, b_r, b_i = dbr_ref[...], dbi_ref[...], br_ref[...], bi_ref[...]
        k_r, k_i, a_r, a_i = kr_ref[...], ki_ref[...], ar_ref[...], ai_ref[...]
        l_r, l_i = lr_ref[...], li_ref[...]
        dt = jnp.exp(ldt_ref[...])
        obr_ref[...] = k_r * dbr + k_i * dbi
        obi_ref[...] = k_r * dbi - k_i * dbr
        gk_r = jnp.sum(dbr * b_r + dbi * b_i, axis=0, keepdims=True)
        gk_i = jnp.sum(dbi * b_r - dbr * b_i, axis=0, keepdims=True)
        ga_r = jnp.sum(dar_ref[...], axis=0, keepdims=True)
        ga_i = jnp.sum(dai_ref[...], axis=0, keepdims=True)
        den = l_r * l_r + l_i * l_i
        c_r, c_i = l_r / den, l_i / den
        m_r, m_i = _cmul(c_r, c_i, gk_r, gk_i)
        g_r, g_i = ga_r + m_r, ga_i + m_i
        t1_r, t1_i = _cmul(dt * a_r, -dt * a_i, g_r, g_i)
        q_r, q_i = _cmul(k_r, k_i, c_r, -c_i)
        t2_r, t2_i = _cmul(-q_r, q_i, gk_r, gk_i)
        olr_ref[...] = t1_r + t2_r
        oli_ref[...] = t1_i + t2_i
        w_r, w_i = _cmul(l_r, l_i, a_r, a_i)
        odt_ref[...] = dt * (w_r * g_r + w_i * g_i)

    v = jax.ShapeDtypeStruct((1, N_STATES), f32)
    m = jax.ShapeDtypeStruct((SSM_GROUP, N_STATES), f32)
    return pl.pallas_call(body, name="ssm_param_bwd", out_shape=[m, m, v, v, v])(
        dbb_r, dbb_i, bt_re, bt_im, kr, ki, ar, ai, lr, li, ldt, da_r, da_i)


def group_sum(v):
    def body(v_ref, o_ref):
        o_ref[...] = jnp.sum(v_ref[...], axis=-1, keepdims=True)
    return pl.pallas_call(body, name="group_sum", out_shape=jax.ShapeDtypeStruct((v.shape[0], 1), f32))(v)


ATTN_SCALE = XATTN_HEAD_DIM ** -0.5


def _attn_probs(q_h, k_h):
    s = lax.dot_general(q_h, k_h, NT_DIMS, preferred_element_type=f32) * ATTN_SCALE
    e = jnp.exp(s - jnp.max(s, axis=-1, keepdims=True))
    return e / jnp.sum(e, axis=-1, keepdims=True)


def attn_fwd(proj, kv, wxo3, tm=TM):
    S = proj.shape[0]
    M = kv.shape[0]
    J, _, n = wxo3.shape

    def body(q_ref, kv_ref, w_ref, o_ref, y_ref):
        for h in range(XATTN_HEADS):
            cols = slice(h * XATTN_HEAD_DIM, (h + 1) * XATTN_HEAD_DIM)
            q_h = q_ref[:, cols].astype(bf16)
            k_h = kv_ref[:, cols]
            v_h = kv_ref[:, XATTN_WIDTH + h * XATTN_HEAD_DIM:XATTN_WIDTH + (h + 1) * XATTN_HEAD_DIM]
            p = _attn_probs(q_h, k_h)
            o_ref[:, cols] = jnp.dot(p.astype(bf16), v_h, preferred_element_type=f32).astype(bf16)
        o = o_ref[...]
        for j in range(J):
            y_ref[:, j * n:(j + 1) * n] = jnp.dot(o, w_ref[j], preferred_element_type=f32).astype(bf16)

    return pl.pallas_call(
        body, name="attn_fwd", grid=(S // tm,),
        in_specs=[_rows(tm, XATTN_WIDTH, 10), _full((M, 2 * XATTN_WIDTH)), _full(wxo3.shape)],
        out_specs=[_rows(tm, XATTN_WIDTH), _rows(tm, J * n)],
        out_shape=[jax.ShapeDtypeStruct((S, XATTN_WIDTH), bf16), jax.ShapeDtypeStruct((S, J * n), bf16)],
        compiler_params=_params())(proj, kv, wxo3)


def attn_bwd(dy_c, o_att, wxo3, proj, kv, dproj, tm=TM):
    S = proj.shape[0]
    M = kv.shape[0]
    J, _, n = wxo3.shape
    nt = S // tm

    def body(dy_ref, o_ref, w_ref, q_ref, kv_ref, _, dq_ref, dkv_ref, dwo_ref, acc_ref):
        i = pl.program_id(0)

        @pl.when(i == 0)
        def _():
            dkv_ref[...] = jnp.zeros_like(dkv_ref)

        part = lax.dot_general(o_ref[...], dy_ref[...], TN_DIMS, preferred_element_type=f32)

        @pl.when(i == 0)
        def _():
            acc_ref[...] = part

        @pl.when(i > 0)
        def _():
            acc_ref[...] += part

        @pl.when(i == nt - 1)
        def _():
            for j in range(J):
                dwo_ref[j] = acc_ref[:, j * n:(j + 1) * n].astype(bf16)

        do = None
        for j in range(J):
            part = lax.dot_general(dy_ref[:, j * n:(j + 1) * n], w_ref[j], NT_DIMS, preferred_element_type=f32)
            do = part if do is None else do + part
        do = do.astype(bf16)
        for h in range(XATTN_HEADS):
            cols = slice(h * XATTN_HEAD_DIM, (h + 1) * XATTN_HEAD_DIM)
            vcols = slice(XATTN_WIDTH + h * XATTN_HEAD_DIM, XATTN_WIDTH + (h + 1) * XATTN_HEAD_DIM)
            q_h = q_ref[:, cols].astype(bf16)
            k_h = kv_ref[:, cols]
            v_h = kv_ref[:, vcols]
            do_h = do[:, cols]
            p = _attn_probs(q_h, k_h)
            dp = lax.dot_general(do_h, v_h, NT_DIMS, preferred_element_type=f32)
            ds = (p * (dp - jnp.sum(dp * p, axis=-1, keepdims=True)) * ATTN_SCALE).astype(bf16)
            dq_ref[:, cols] = jnp.dot(ds, k_h, preferred_element_type=f32).astype(bf16)
            dkv_ref[:, cols] += lax.dot_general(ds, q_h, TN_DIMS, preferred_element_type=f32)
            dkv_ref[:, vcols] += lax.dot_general(p.astype(bf16), do_h, TN_DIMS, preferred_element_type=f32)

    return pl.pallas_call(
        body, name="attn_bwd", grid=(S // tm,),
        in_specs=[_rows(tm, J * n), _rows(tm, XATTN_WIDTH), _full(wxo3.shape), _rows(tm, XATTN_WIDTH, 10),
                  _full((M, 2 * XATTN_WIDTH)), pl.BlockSpec(memory_space=pl.ANY)],
        out_specs=[_rows(tm, XATTN_WIDTH, 10), _full((M, 2 * XATTN_WIDTH)), _full(wxo3.shape)],
        out_shape=[jax.ShapeDtypeStruct(dproj.shape, bf16), jax.ShapeDtypeStruct((M, 2 * XATTN_WIDTH), f32),
                   jax.ShapeDtypeStruct(wxo3.shape, bf16)],
        scratch_shapes=[pltpu.VMEM((XATTN_WIDTH, J * n), f32)], input_output_aliases={5: 0},
        compiler_params=_params())(dy_c, o_att, wxo3, proj, kv, dproj)


def _ln_stats(r):
    mu = jnp.mean(r, axis=-1, keepdims=True)
    xc = r - mu
    var = jnp.mean(xc * xc, axis=-1, keepdims=True)
    rstd = lax.rsqrt(var + LN_EPS)
    return xc * rstd, rstd


def _ln_bwd(dy, xhat, rstd, g):
    dxh = dy * g
    return rstd * (dxh - jnp.mean(dxh, axis=-1, keepdims=True) - xhat * jnp.mean(dxh * xhat, axis=-1, keepdims=True))


def _colsum(v):
    return jnp.sum(v, axis=0, keepdims=True)


def _ln2_loss_epi(t2, h1, target, b_down, g, b):
    xhat, rstd = _ln_stats(ALPHA * h1 + t2 + b_down)
    err = xhat * g + b - target
    dout = err * (1.0 / D_MODEL)
    dr = _ln_bwd(dout, xhat, rstd, g)
    return (dr, dr), (0.5 * _colsum(err * err) * (1.0 / D_MODEL), _colsum(dout * xhat), _colsum(dout), _colsum(dr))


def _merge_fwd_pro(g0, g1, g2, y_a, glu_a, glu_b, y_c, b):
    gs = [_sig(g.astype(f32) + b[:, k * D_MODEL:(k + 1) * D_MODEL]) for k, g in enumerate((g0, g1, g2))]
    y_b = glu_a.astype(f32) * _sig(glu_b.astype(f32))
    return gs[0] * y_a.astype(f32) + gs[1] * y_b + gs[2] * y_c.astype(f32)


def _merge_bwd_epi(dm, g0, g1, g2, y_a, glu_a, glu_b, y_c, b):
    ga, sb = glu_a.astype(f32), _sig(glu_b.astype(f32))
    ys = (y_a.astype(f32), ga * sb, y_c.astype(f32))
    gs = [_sig(g.astype(f32) + b[:, k * D_MODEL:(k + 1) * D_MODEL]) for k, g in enumerate((g0, g1, g2))]
    dpre = [dm * ys[k] * gs[k] * (1.0 - gs[k]) for k in range(3)]
    dyb = dm * gs[1]
    dglu = jnp.concatenate([dyb * sb, dyb * ga * sb * (1.0 - sb)], axis=1)
    return ((jnp.concatenate(dpre, axis=1), dm * gs[0], dglu, dm * gs[2]),
            (jnp.concatenate([_colsum(d) for d in dpre], axis=1),))


def _ln1_fwd_epi(t1, x, g, b):
    xhat, _ = _ln_stats(ALPHA * x + t1)
    h = xhat * g + b
    return (t1, h, h), ()


def _ln1_bwd_epi(t3, x, t1, dr2, g):
    xhat, rstd = _ln_stats(ALPHA * x + t1)
    dh = ALPHA * dr2 + t3
    dr = _ln_bwd(dh, xhat, rstd, g)
    return (dr, dr), (_colsum(dh * xhat), _colsum(dh))


def exchange(*rides, name):
    return _pcall(lambda: None, [], name=name, grid=(), in_specs=[], out_specs=[], out_shape=[], rides=rides)[1]


HBM_ONLY = pl.BlockSpec(memory_space=pltpu.HBM)
SEM_SPEC = pl.BlockSpec(memory_space=pltpu.SEMAPHORE)
DATAFLOW = pltpu.SideEffectType.DATAFLOW_SIDE_EFFECTING


def scatter_start(g):
    def body(g_ref, land_ref, s0, s1, s2, r0, r1, r2, g_thru, land_thru, token):
        del g_thru, land_thru
        x, y, c = lax.axis_index("x"), lax.axis_index("y"), lax.axis_index("c")
        me = 2 * x + y
        for k, (px, py) in enumerate(_xy_peers(x, y)):
            pltpu.make_async_remote_copy(src_ref=g_ref.at[2 * px + py], dst_ref=land_ref.at[me],
                                         send_sem=(s0, s1, s2)[k], recv_sem=(r0, r1, r2)[k],
                                         device_id=(px, py, c), device_id_type=MESH).start()
        token[...] = jnp.zeros_like(token)

    sem = pltpu.SemaphoreType.DMA(())
    return pl.pallas_call(
        body, name="scatter_start",
        out_shape=(sem,) * 6 + (pltpu.HBM(g.shape, g.dtype), pltpu.HBM(g.shape, g.dtype),
                                jax.ShapeDtypeStruct((1, D_MODEL), f32)),
        in_specs=(HBM_ONLY, HBM_ONLY), out_specs=(SEM_SPEC,) * 6 + (HBM_ONLY, HBM_ONLY, _full((1, D_MODEL))),
        input_output_aliases={0: 6, 1: 7}, compiler_params=pltpu.CompilerParams(has_side_effects=DATAFLOW),
    )(pltpu.with_memory_space_constraint(g, pltpu.HBM),
      pltpu.with_memory_space_constraint(lax.empty(g.shape, g.dtype), pltpu.HBM))


def scatter_wait(sems, g_thru, land_thru, after):
    def body(g_ref, land_ref, s0, s1, s2, r0, r1, r2, after_ref, g_out, land_out):
        del after_ref, g_out, land_out
        x, y, c = lax.axis_index("x"), lax.axis_index("y"), lax.axis_index("c")
        for k, (px, py) in enumerate(_xy_peers(x, y)):
            copy = pltpu.make_async_remote_copy(src_ref=g_ref.at[2 * px + py], dst_ref=land_ref.at[2 * px + py],
                                                send_sem=(s0, s1, s2)[k], recv_sem=(r0, r1, r2)[k],
                                                device_id=(px, py, c), device_id_type=MESH)
            copy.wait_send()
            copy.wait_recv()

    return pl.pallas_call(
        body, name="scatter_wait",
        out_shape=(pltpu.HBM(g_thru.shape, g_thru.dtype), pltpu.HBM(g_thru.shape, g_thru.dtype)),
        in_specs=(HBM_ONLY, HBM_ONLY) + (SEM_SPEC,) * 6 + (pl.BlockSpec(memory_space=pl.ANY),),
        out_specs=(HBM_ONLY, HBM_ONLY), input_output_aliases={0: 0, 1: 1},
        compiler_params=pltpu.CompilerParams(has_side_effects=DATAFLOW),
    )(g_thru, land_thru, *sems, after)


ROW_TILE = 256


def sum_slots(recv, name):
    n, R, C = recv.shape
    tm = min(R, ROW_TILE)

    def body(r_ref, o_ref):
        acc = r_ref[0].astype(f32)
        for k in range(1, n):
            acc = acc + r_ref[k].astype(f32)
        o_ref[...] = acc

    return pl.pallas_call(
        body, name=name, grid=(R // tm,), in_specs=[pl.BlockSpec((n, tm, C), lambda i: (0, i, 0))],
        out_specs=_rows(tm, C), out_shape=jax.ShapeDtypeStruct((R, C), f32), compiler_params=_params())(recv)


def _adamw(w, g, m, v):
    m = ADAM_B1 * m + (1.0 - ADAM_B1) * g
    v = ADAM_B2 * v + (1.0 - ADAM_B2) * (g * g)
    m_hat = m / (1.0 - ADAM_B1 ** ADAM_STEP)
    v_hat = v / (1.0 - ADAM_B2 ** ADAM_STEP)
    delta = -ADAM_LR * (m_hat / (jnp.sqrt(v_hat) + ADAM_EPS) + ADAM_WD * w)
    return delta, m, v


def adam_pair(p, q, w, m, v, name):
    R, C = w.shape
    tm = min(R, ROW_TILE)

    def body(p_ref, q_ref, w_ref, m_ref, v_ref, g_ref, d_ref, nm_ref, nv_ref):
        g = p_ref[...] + q_ref[...]
        g_ref[...] = g
        d_ref[...], nm_ref[...], nv_ref[...] = _adamw(w_ref[...], g, m_ref[...], v_ref[...])

    o = jax.ShapeDtypeStruct((R, C), f32)
    return pl.pallas_call(body, name=name, grid=(R // tm,), in_specs=[_rows(tm, C)] * 5,
                          out_specs=[_rows(tm, C)] * 4, out_shape=[o] * 4, compiler_params=_params())(p, q, w, m, v)


def adam_plain(w, g, m, v, name):
    def body(w_ref, g_ref, m_ref, v_ref, d_ref, nm_ref, nv_ref):
        d_ref[...], nm_ref[...], nv_ref[...] = _adamw(w_ref[...], g_ref[...], m_ref[...], v_ref[...])

    o = jax.ShapeDtypeStruct(w.shape, f32)
    return pl.pallas_call(body, name=name, out_shape=[o] * 3, compiler_params=_params())(w, g, m, v)


def adam_slots(recv, w, m, v, name):
    n = recv.shape[0]

    def body(r_ref, w_ref, m_ref, v_ref, g_ref, d_ref, nm_ref, nv_ref):
        g = r_ref[0]
        for k in range(1, n):
            g = g + r_ref[k]
        g_ref[...] = g
        d_ref[...], nm_ref[...], nv_ref[...] = _adamw(w_ref[...], g, m_ref[...], v_ref[...])

    o = jax.ShapeDtypeStruct(w.shape, f32)
    return pl.pallas_call(body, name=name, out_shape=[o] * 4, compiler_params=_params())(recv, w, m, v)


BIG = ("w_in", "w_conv_out", "w_glu", "w_xattn_out", "w_kv", "w_out", "w_up", "w_down")
MID = ("w_conv_out", "w_glu", "w_xattn_out", "w_kv", "w_out")
WIDE_MINOR = ("ssm_b_re", "ssm_b_im", "ssm_c_re", "ssm_c_im")
SMALL = ("b_gate", "ssm_lam_re", "ssm_lam_im", "ssm_log_dt", "ssm_b_re", "ssm_b_im", "ssm_c_re", "ssm_c_im", "ssm_d",
         "ln1_g", "ln1_b", "b_up", "b_down", "ln2_g", "ln2_b")
def _pad_flat(a, mult=1024):
    a = a.reshape(-1)
    return jnp.pad(a, (0, (-a.shape[0]) % mult))


def _perm(a):
    S, W = a.shape
    return a.reshape(N_SEG, S // N_SEG, W).transpose(1, 0, 2).reshape(S, W)


def _unperm(a):
    S, W = a.shape
    return a.reshape(S // N_SEG, N_SEG, W).transpose(1, 0, 2).reshape(S, W)


def _state_rows(a):
    return a.transpose(2, 0, 1).reshape(SSM_GROUP, N_STATES)


def _block_diag_b(bt):
    b4 = bt.reshape(SSM_GROUP, 4, 8, SSM_STATE)
    eye = jnp.eye(8, dtype=bt.dtype)
    return jnp.einsum("hcgp,gk->cghkp", b4, eye).reshape(4, 128, 512)


def _block_diag_c(c):
    c4 = c.reshape(4, 8, SSM_GROUP, SSM_STATE)
    eye = jnp.eye(8, dtype=c.dtype)
    return jnp.einsum("cghp,gk->cgpkh", c4, eye).reshape(4, 512, 128)


def _diag_b(acc):
    a = acc.reshape(4, 8, SSM_STATE, 8, SSM_GROUP)
    eye = jnp.eye(8, dtype=acc.dtype)
    return jnp.einsum("cgpkh,gk->hcgp", a, eye).reshape(SSM_GROUP, N_STATES)


def _diag_c(acc):
    a = acc.reshape(4, 8, SSM_GROUP, 8, SSM_STATE)
    eye = jnp.eye(8, dtype=acc.dtype)
    return jnp.einsum("cghkp,gk->cghp", a, eye).reshape(SSM_GROUPS, SSM_GROUP, SSM_STATE)


def kernel(x, mem, w_in, b_gate, conv_w, w_conv_out, ssm_lam_re, ssm_lam_im, ssm_log_dt, ssm_b_re, ssm_b_im, ssm_c_re, ssm_c_im, ssm_d, w_glu, w_kv, w_xattn_out, w_out, ln1_g, ln1_b, w_up, b_up, w_down, b_down, ln2_g, ln2_b, loss_target, m_w_in, m_b_gate, m_conv_w, m_w_conv_out, m_ssm_lam_re, m_ssm_lam_im, m_ssm_log_dt, m_ssm_b_re, m_ssm_b_im, m_ssm_c_re, m_ssm_c_im, m_ssm_d, m_w_glu, m_w_kv, m_w_xattn_out, m_w_out, m_ln1_g, m_ln1_b, m_w_up, m_b_up, m_w_down, m_b_down, m_ln2_g, m_ln2_b, v_w_in, v_b_gate, v_conv_w, v_w_conv_out, v_ssm_lam_re, v_ssm_lam_im, v_ssm_log_dt, v_ssm_b_re, v_ssm_b_im, v_ssm_c_re, v_ssm_c_im, v_ssm_d, v_w_glu, v_w_kv, v_w_xattn_out, v_w_out, v_ln1_g, v_ln1_b, v_w_up, v_b_up, v_w_down, v_b_down, v_ln2_g, v_ln2_b):
    W = dict(w_in=w_in, b_gate=b_gate, conv_w=conv_w, w_conv_out=w_conv_out, ssm_lam_re=ssm_lam_re,
             ssm_lam_im=ssm_lam_im, ssm_log_dt=ssm_log_dt, ssm_b_re=ssm_b_re, ssm_b_im=ssm_b_im, ssm_c_re=ssm_c_re,
             ssm_c_im=ssm_c_im, ssm_d=ssm_d, w_glu=w_glu, w_kv=w_kv, w_xattn_out=w_xattn_out, w_out=w_out,
             ln1_g=ln1_g, ln1_b=ln1_b, w_up=w_up, b_up=b_up, w_down=w_down, b_down=b_down, ln2_g=ln2_g, ln2_b=ln2_b)
    MOM = dict(w_in=m_w_in, b_gate=m_b_gate, conv_w=m_conv_w, w_conv_out=m_w_conv_out, ssm_lam_re=m_ssm_lam_re,
               ssm_lam_im=m_ssm_lam_im, ssm_log_dt=m_ssm_log_dt, ssm_b_re=m_ssm_b_re, ssm_b_im=m_ssm_b_im,
               ssm_c_re=m_ssm_c_re, ssm_c_im=m_ssm_c_im, ssm_d=m_ssm_d, w_glu=m_w_glu, w_kv=m_w_kv,
               w_xattn_out=m_w_xattn_out, w_out=m_w_out, ln1_g=m_ln1_g, ln1_b=m_ln1_b, w_up=m_w_up, b_up=m_b_up,
               w_down=m_w_down, b_down=m_b_down, ln2_g=m_ln2_g, ln2_b=m_ln2_b)
    VEL = dict(w_in=v_w_in, b_gate=v_b_gate, conv_w=v_conv_w, w_conv_out=v_w_conv_out, ssm_lam_re=v_ssm_lam_re,
               ssm_lam_im=v_ssm_lam_im, ssm_log_dt=v_ssm_log_dt, ssm_b_re=v_ssm_b_re, ssm_b_im=v_ssm_b_im,
               ssm_c_re=v_ssm_c_re, ssm_c_im=v_ssm_c_im, ssm_d=v_ssm_d, w_glu=v_w_glu, w_kv=v_w_kv,
               w_xattn_out=v_w_xattn_out, w_out=v_w_out, ln1_g=v_ln1_g, ln1_b=v_ln1_b, w_up=v_w_up, b_up=v_b_up,
               w_down=v_w_down, b_down=v_b_down, ln2_g=v_ln2_g, ln2_b=v_ln2_b)
    names = list(W)
    xy = 2 * lax.axis_index("x") + lax.axis_index("y")

    xs = x[0]
    S = xs.shape[0]
    mems = mem[0]
    tgt = loss_target[0]

    shard_bf = {n: W[n][0].astype(bf16) for n in BIG}

    lr = ssm_lam_re.reshape(1, N_STATES)
    li = ssm_lam_im.reshape(1, N_STATES)
    ldt = jnp.repeat(ssm_log_dt.reshape(SSM_GROUPS), SSM_STATE).reshape(1, N_STATES)
    bt_re, bt_im = _state_rows(ssm_b_re[0]), _state_rows(ssm_b_im[0])
    ar, ai, kr, ki, bbt_r, bbt_i = ssm_prep(lr, li, ldt, bt_re, bt_im)
    bd_r, bd_i = _block_diag_b(bbt_r).astype(bf16), _block_diag_b(bbt_i).astype(bf16)
    cd_r, cd_i = _block_diag_c(ssm_c_re[0]).astype(bf16), _block_diag_c(ssm_c_im[0]).astype(bf16)
    bdt_r, bdt_i = bd_r.transpose(0, 2, 1), bd_i.transpose(0, 2, 1)
    cdt_r, cdt_i = cd_r.transpose(0, 2, 1), cd_i.transpose(0, 2, 1)
    d_skip = ssm_d.reshape(1, SSM_WIDTH)

    mem_bf = mems.astype(bf16)
    u_col = GATE_COLS + 3 * CONV_WIDTH
    me_xy = jnp.reshape(xy, (1,)).astype(jnp.int32)
    proj, x_bf, (win3,) = proj_own(xs, shard_bf["w_in"], me_xy, 4, rides=(("gather2", shard_bf["w_in"]),))
    proj, (wco3, wglu3, wxo3, wkv4, wout4, convw4) = proj_rest(
        x_bf, win3, me_xy, proj, rides=tuple(("gather", shard_bf[n]) for n in MID) + (("gather", conv_w[0]),))
    wkv3 = wkv4.reshape(1, D_MODEL, 2 * XATTN_WIDTH)
    wout3 = wout4.reshape(1, D_MODEL, D_MODEL)
    convw_full = convw4.transpose(1, 0, 2).reshape(3, CONV_WIDTH)
    a_conv, y_a = conv_fwd(proj, convw_full, wco3)
    u_perm = _perm(proj[:, u_col:u_col + SSM_WIDTH])
    (init_r, init_i), (wup3,) = ssm_scan_fwd(u_perm, bd_r, bd_i, ar, ai, rides=(("gather2", shard_bf["w_up"]),))
    ysm_perm, ys_perm, st_r, st_i = ssm_scan_fwd(u_perm, bd_r, bd_i, ar, ai,
                                                 final=(cd_r, cd_i, d_skip, init_r, init_i))
    y_s = _unperm(ys_perm)
    glu = mm_fwd_small(y_s, wglu3, "glu", out_dtype=bf16)
    kv = mm_fwd_small(mem_bf, wkv3, "kv", out_dtype=bf16)
    o_att, y_c = attn_fwd(proj, kv, wxo3)
    merge_rows = ((proj, D_MODEL, 0), (proj, D_MODEL, 1), (proj, D_MODEL, 2), y_a, (glu, D_MODEL, 0),
                  (glu, D_MODEL, 1), y_c)
    merged, t1, h1, h1_bf = mm_rows_fused(
        (_merge_fwd_pro, merge_rows, (b_gate,)), wout3, "merge_w_out_ln1", transposed=False, tm=256,
        epi=_ln1_fwd_epi, extras=(xs,), consts=(ln1_g, ln1_b), outs=(f32, f32, bf16))
    (r_up, hdn), (wdn4,) = mm_fwd(h1_bf, wup3, "w_up", bias=b_up, outs=(bf16, bf16), second=_relu2,
                                  first=lambda v: jnp.maximum(v, 0.0), rides=(("gather2", shard_bf["w_down"]),))
    wdn3 = wdn4.reshape(1, D_FF, D_MODEL)

    dr2, dr2_bf, loss_cols, d_ln2_g, d_ln2_b, d_b_down = mm_rows_fused(
        hdn, wdn3, "w_down_ln2_loss", transposed=False, tm=512, epi=_ln2_loss_epi, extras=(h1, tgt),
        consts=(b_down, ln2_g, ln2_b), outs=(f32, bf16), nsums=4)
    part, other = {}, {}
    g_w_down = mm_bwd_w(hdn, dr2_bf, 1, "dw_down", tk=1024).reshape(4, -1, D_MODEL)
    dup, d_b_up = mm_bwd_x(dr2_bf, wdn3, "dup", tm=512, extras=(r_up,), colsum=True, out_dtype=bf16,
                           epi=lambda acc, r: acc * (2.0 * r.astype(f32)))
    g_w_up = mm_bwd_w(h1_bf, dup, 4, "dw_up")
    (dr1, dr1_bf, d_ln1_g, d_ln1_b), (recv_dn,) = mm_rows_fused(
        dup, wup3, "dh1_ln1_bwd", transposed=True, tm=512, epi=_ln1_bwd_epi, extras=(xs, t1, dr2), consts=(ln1_g,),
        outs=(f32, bf16), nsums=2, rides=(("scatter", g_w_down),))
    part["w_down"] = sum_slots(recv_dn, "sum_w_down")
    g_w_out = mm_bwd_w(merged, dr1_bf, 1, "dw_out").reshape(4, -1, D_MODEL)
    (dproj, dy_a, dglu, dy_c, d_b_gate), (recv_up,) = mm_rows_fused(
        dr1_bf, wout3, "dmerged_merge_bwd", transposed=True, tm=256, epi=_merge_bwd_epi, extras=merge_rows,
        consts=(b_gate,), outs=((bf16, GATE_COLS, IN_COLS), bf16, (bf16, 2 * D_MODEL, 2 * D_MODEL), bf16),
        sum_widths=(GATE_COLS,), rides=(("scatter", g_w_up),))
    part["w_up"] = sum_slots(recv_up, "sum_w_up")

    dproj, d_conv_w, g_w_co = conv_bwd(dy_a, a_conv, wco3, proj, convw_full, dproj)

    g_w_glu = mm_bwd_w_small(y_s, dglu, 4, "dw_glu")
    dys_perm = _perm(mm_bwd_x(dglu, wglu3, "dy_s", tm=1024))
    linit_r, linit_i = ssm_scan_bwd(dys_perm, ysm_perm, cdt_r, cdt_i, ar, ai)
    (du_perm, dbacc_r, dbacc_i, dcacc_r, dcacc_i, da_r, da_i, d_ssm_d), (recv_co, recv_glu, recv_out) = ssm_scan_bwd(
        dys_perm, ysm_perm, cdt_r, cdt_i, ar, ai,
        final=(u_perm, st_r, st_i, bdt_r, bdt_i, d_skip, linit_r, linit_i),
        rides=(("scatter", g_w_co), ("scatter", g_w_glu), ("scatter", g_w_out)))
    part["w_out"] = sum_slots(recv_out, "sum_w_out")
    part["w_conv_out"] = sum_slots(recv_co, "sum_w_conv_out")
    part["w_glu"] = sum_slots(recv_glu, "sum_w_glu")
    dbt_re, dbt_im, d_lr, d_li, d_ldt_state = ssm_param_bwd(
        _diag_b(dbacc_r), _diag_b(dbacc_i), bt_re, bt_im, kr, ki, ar, ai, lr, li, ldt, da_r, da_i)
    d_log_dt = group_sum(d_ldt_state.reshape(SSM_GROUPS, SSM_STATE))
    d_b_re = dbt_re.reshape(SSM_GROUP, SSM_GROUPS, SSM_STATE).transpose(1, 2, 0)
    d_b_im = dbt_im.reshape(SSM_GROUP, SSM_GROUPS, SSM_STATE).transpose(1, 2, 0)
    d_c_re = _diag_c(dcacc_r)
    d_c_im = -_diag_c(dcacc_i)

    dproj, dkv, g_w_xo = attn_bwd(dy_c, o_att, wxo3, proj, kv, dproj)
    g_w_kv = mm_bwd_w(mem_bf, dkv, 1, "dw_kv").reshape(4, -1, D_MODEL)

    dproj = lax.dynamic_update_slice(dproj, _unperm(du_perm), (0, u_col))
    g_small = {"b_gate": d_b_gate, "ssm_lam_re": d_lr, "ssm_lam_im": d_li, "ssm_log_dt": d_log_dt, "ssm_b_re": d_b_re,
               "ssm_b_im": d_b_im, "ssm_c_re": d_c_re, "ssm_c_im": d_c_im, "ssm_d": d_ssm_d, "ln1_g": d_ln1_g,
               "ln1_b": d_ln1_b, "b_up": d_b_up, "b_down": d_b_down, "ln2_g": d_ln2_g, "ln2_b": d_ln2_b}
    small_names = SMALL + ("conv_w",)
    g_small["conv_w"] = d_conv_w
    sizes = {n: (g_small[n].size + 1023) // 1024 * 1024 for n in small_names}
    pack = lambda d: jnp.concatenate([_pad_flat(d[n]) for n in small_names]).reshape(-1, 128)
    early = ("w_down", "w_up", "w_out", "w_conv_out", "w_glu")
    g_w_in, landed = mm_bwd_w(
        x_bf, dproj, 4, "dw_in", rides=(("scatter", g_w_xo), ("scatter", g_w_kv), ("all", pack(g_small)))
        + tuple(("pair", part[n]) for n in early))
    recv_xo, recv_kv, srecv = landed[:3]
    other.update(zip(early, landed[3:]))
    part["w_xattn_out"] = sum_slots(recv_xo, "sum_w_xattn_out")
    part["w_kv"] = sum_slots(recv_kv, "sum_w_kv")
    *sems, g_thru, land_thru, token = scatter_start(g_w_in)
    dx, = mm_rows_fused(dproj, win3, "dx", transposed=True, tm=512, extras=(dr1,), consts=(token,),
                        epi=lambda acc, d, tk: ((acc + ALPHA * d + tk,), ()), outs=(f32,))
    res = [{}, {}, {}, {}]
    for n in early:
        for k, r in enumerate(adam_pair(part[n], other[n], W[n][0], MOM[n][0], VEL[n][0], "adam_" + n)):
            res[k][n] = r[None]
    g_own, landed_in = scatter_wait(sems, g_thru, land_thru, dx)
    recv_in = lax.dynamic_update_slice(landed_in, lax.dynamic_index_in_dim(g_own, xy, 0, keepdims=True), (xy, 0, 0))
    late = ("w_in", "w_xattn_out", "w_kv")
    part["w_in"] = sum_slots(recv_in, "sum_w_in")
    other.update(zip(late, exchange(*[("pair", part[n]) for n in late], name="swap_late")))
    for n in late:
        for k, r in enumerate(adam_pair(part[n], other[n], W[n][0], MOM[n][0], VEL[n][0], "adam_" + n)):
            res[k][n] = r[None]
    conv_zero = jnp.zeros((3, CONV_WIDTH), f32)
    in_pack = lambda d: pack({**{n: (jnp.zeros(W[n].shape, f32) if n in WIDE_MINOR else d[n]) for n in SMALL},
                              "conv_w": conv_zero})
    gs, ds_, ms, vs = adam_slots(srecv, in_pack(W), in_pack(MOM), in_pack(VEL), "adam_small")

    def unpack_small(buf, skip=()):
        flat = buf.reshape(-1)
        out, r = {}, 0
        for n in small_names:
            ref = g_small[n] if n == "conv_w" else W[n]
            if n not in skip:
                out[n] = flat[r:r + ref.size].reshape(ref.shape)
            r += sizes[n]
        return out

    res_s = [unpack_small(gs)] + [unpack_small(b, WIDE_MINOR) for b in (ds_, ms, vs)]
    for n in WIDE_MINOR:
        two_d = (-1, W[n].shape[-1])
        outs_n = adam_plain(W[n].reshape(two_d), res_s[0][n].reshape(two_d), MOM[n].reshape(two_d),
                            VEL[n].reshape(two_d), "adam_" + n)
        for k, r in enumerate(outs_n):
            res_s[k + 1][n] = r.reshape(W[n].shape)
    g_conv = lax.dynamic_slice(res_s[0]["conv_w"], (0, xy * 128), (3, 128))
    conv_slots = g_conv.reshape(1, 3, 128)
    cg, cd, cm, cv = adam_slots(conv_slots, conv_w[0], m_conv_w[0], v_conv_w[0], "adam_conv")
    conv_res = [cg, cd, cm, cv]

    loss = lax.psum(jnp.sum(loss_cols), ("x", "y", "c"))
    outs = [loss, dx.reshape(x.shape)]
    for k in range(4):
        for n in names:
            if n == "conv_w":
                outs.append(conv_res[k].reshape(conv_w.shape))
            elif n in BIG:
                outs.append(res[k][n])
            else:
                outs.append(res_s[k][n])
    return tuple(outs)
```

```python
import functools
import math

import jax
import jax.numpy as jnp
from jax import lax
from jax.experimental import pallas as pl
from jax.experimental.pallas import tpu as pltpu
from jax.experimental.pallas import tpu_sc as plsc

f32 = jnp.float32
bf16 = jnp.bfloat16

D_MODEL = 1024
CONV_WIDTH = 512
SSM_WIDTH = 512
SSM_GROUP = 16
SSM_GROUPS = 32
SSM_STATE = 64
N_STATES = SSM_GROUPS * SSM_STATE
XATTN_HEADS = 4
XATTN_HEAD_DIM = 128
XATTN_WIDTH = 512
D_FF = 4096
GATE_COLS = 3 * D_MODEL
IN_COLS = GATE_COLS + 3 * CONV_WIDTH + SSM_WIDTH + XATTN_WIDTH
ALPHA = 2.0 ** 0.25
LN_EPS = 1e-5
ADAM_LR = 0.001
ADAM_B1 = 0.9
ADAM_B2 = 0.999
ADAM_EPS = 1e-08
ADAM_WD = 0.01
ADAM_STEP = 10

N_SEG = 8
SCAN_ROWS = 512
LANE_STRIP = 512
TM = 512
VMEM_LIMIT = 48 * 1024 * 1024
MESH = pl.DeviceIdType.MESH

NT_DIMS = (((1,), (1,)), ((), ()))
TN_DIMS = (((0,), (0,)), ((), ()))


def _params():
    return pltpu.CompilerParams(vmem_limit_bytes=VMEM_LIMIT)


def _full(shape):
    n = len(shape)
    return pl.BlockSpec(shape, lambda *_: (0,) * n)


def _rows(tm, w, cb=0):
    return pl.BlockSpec((tm, w), lambda i: (i, cb))


def _sig(x):
    return 1.0 / (1.0 + jnp.exp(-x))


HBM_SPEC = pl.BlockSpec(memory_space=pl.ANY)
RIDE_PEERS = {"gather": 3, "gather2": 6, "scatter": 3, "all": 7, "pair": 1}


def _xy_peers(x, y):
    return [(1 - x, y), (x, 1 - y), (1 - x, 1 - y)]


def _ride_copies(kind, src, dst, send_sems, recv_sems, local_sem):
    x, y, c = lax.axis_index("x"), lax.axis_index("y"), lax.axis_index("c")
    me = 2 * x + y
    if kind == "all":
        flips = [(fx, fy, fc) for fx in (0, 1) for fy in (0, 1) for fc in (0, 1)][1:]
        peers = [(x ^ fx, y ^ fy, c ^ fc) for fx, fy, fc in flips]
        slot = lambda p: 4 * p[0] + 2 * p[1] + p[2]
        mine = slot((x, y, c))
    else:
        peers = [(px, py, c) for px, py in _xy_peers(x, y)]
        slot = lambda p: 2 * p[0] + p[1]
        mine = me

    def remote(k, s, d):
        return pltpu.make_async_remote_copy(src_ref=s, dst_ref=d, send_sem=send_sems.at[k], recv_sem=recv_sems.at[k],
                                            device_id=peers[k], device_id_type=MESH)

    if kind == "pair":
        peers = [(x, y, 1 - c)]
        return None, [remote(0, src, dst)], [remote(0, src, dst)]
    if kind == "scatter":
        local = pltpu.make_async_copy(src.at[me], dst.at[me], local_sem)
        sends = [remote(k, src.at[slot(p)], dst.at[me]) for k, p in enumerate(peers)]
        lands = [remote(k, src.at[me], dst.at[slot(p)]) for k, p in enumerate(peers)]
    else:
        local = pltpu.make_async_copy(src, dst.at[mine], local_sem)
        sends = [remote(k, src, dst.at[mine]) for k, p in enumerate(peers)]
        lands = [remote(k, src, dst.at[slot(p)]) for k, p in enumerate(peers)]
    return local, sends, lands


def _ride_shape(kind, src):
    lead = {"gather": (4,), "gather2": (4,), "scatter": (), "all": (8,), "pair": ()}[kind]
    return jax.ShapeDtypeStruct(lead + src.shape, src.dtype)


def _two_level(src, dst, send_sems, recv_sems, local_sem):
    x, y, c = lax.axis_index("x"), lax.axis_index("y"), lax.axis_index("c")
    me = 2 * x + y
    h = src.shape[0] // 2
    mine = pl.ds(pl.multiple_of(c * h, 16), h)
    theirs = pl.ds(pl.multiple_of((1 - c) * h, 16), h)
    peers = _xy_peers(x, y)
    slots = [2 * px + py for px, py in peers]

    def over_ici(k, d):
        return pltpu.make_async_remote_copy(src_ref=src.at[mine], dst_ref=d, send_sem=send_sems.at[k],
                                            recv_sem=recv_sems.at[k], device_id=(*peers[k], c), device_id_type=MESH)

    def over_d2d(k, rows):
        blk = dst.at[slots[k], rows]
        return pltpu.make_async_remote_copy(src_ref=blk, dst_ref=blk, send_sem=send_sems.at[3 + k],
                                            recv_sem=recv_sems.at[3 + k], device_id=(x, y, 1 - c), device_id_type=MESH)

    return (pltpu.make_async_copy(src, dst.at[me], local_sem),
            [over_ici(k, dst.at[me, mine]) for k in range(3)], [over_ici(k, dst.at[slots[k], mine]) for k in range(3)],
            [over_d2d(k, mine) for k in range(3)], [over_d2d(k, theirs) for k in range(3)])


def _pcall(body, args, *, name, grid, in_specs, out_specs, out_shape, scratch_shapes=(), rides=(), aliases=None,
           prefetch=()):
    n_in, n_out, n_scr, nr, npf = len(in_specs), len(out_specs), len(scratch_shapes), len(rides), len(prefetch)
    kinds = [k for k, _ in rides]

    def wrapped(*refs):
        pre, refs = refs[:npf], refs[npf:]
        ins, rsrc = refs[:n_in], refs[n_in:n_in + nr]
        outs = refs[n_in + nr:n_in + nr + n_out]
        rdst = refs[n_in + nr + n_out:n_in + 2 * nr + n_out]
        scr = refs[n_in + 2 * nr + n_out:n_in + 2 * nr + n_out + n_scr]
        sems = refs[n_in + 2 * nr + n_out + n_scr:]

        def start():
            for r, kind in enumerate(kinds):
                if kind == "gather2":
                    local, sends, _, _, _ = _two_level(rsrc[r], rdst[r], *sems[3 * r:3 * r + 3])
                else:
                    local, sends, _ = _ride_copies(kind, rsrc[r], rdst[r], *sems[3 * r:3 * r + 3])
                if local is not None:
                    local.start()
                for cp in sends:
                    cp.start()

        def finish():
            for r, kind in enumerate(kinds):
                if kind == "gather2":
                    local, sends, ici_lands, forwards, lands = _two_level(rsrc[r], rdst[r], *sems[3 * r:3 * r + 3])
                    for k in range(3):
                        ici_lands[k].wait_recv()
                        forwards[k].start()
                    sends = sends + forwards
                else:
                    local, sends, lands = _ride_copies(kind, rsrc[r], rdst[r], *sems[3 * r:3 * r + 3])
                for cp in lands:
                    cp.wait_recv()
                for cp in sends:
                    cp.wait_send()
                if local is not None:
                    local.wait()

        if nr and grid:
            ids = [pl.program_id(a) for a in range(len(grid))]
            first = functools.reduce(jnp.logical_and, [i == 0 for i in ids])
            last = functools.reduce(jnp.logical_and, [i == g - 1 for i, g in zip(ids, grid)])
            pl.when(first)(start)
        elif nr:
            start()
        body(*pre, *ins, *outs, *scr)
        if nr and grid:
            pl.when(last)(finish)
        elif nr:
            finish()

    sems = []
    for kind in kinds:
        n = RIDE_PEERS[kind]
        sems += [pltpu.SemaphoreType.DMA((n,)), pltpu.SemaphoreType.DMA((n,)), pltpu.SemaphoreType.DMA(())]
    all_in = list(in_specs) + [HBM_SPEC] * nr
    all_out = list(out_specs) + [HBM_SPEC] * nr
    all_scratch = list(scratch_shapes) + sems
    if npf:
        how = dict(grid_spec=pltpu.PrefetchScalarGridSpec(num_scalar_prefetch=npf, grid=grid, in_specs=all_in,
                                                          out_specs=all_out, scratch_shapes=all_scratch))
    else:
        how = dict(grid=grid, in_specs=all_in, out_specs=all_out, scratch_shapes=all_scratch)
    res = pl.pallas_call(
        wrapped, name=name, out_shape=list(out_shape) + [_ride_shape(k, s) for k, s in rides],
        input_output_aliases=aliases or {}, compiler_params=_params(), **how)(
            *prefetch, *args, *[s for _, s in rides])
    return list(res[:n_out]), list(res[n_out:])


def _bf(v):
    return v if v.dtype == bf16 else v.astype(bf16)


def mm_fwd_small(a, w3, name, *, tm=1024, out_dtype=f32):
    M, K = a.shape
    J, _, n = w3.shape
    tm = min(tm, M)

    def body(a_ref, w_ref, o_ref):
        av = _bf(a_ref[...])
        for j in range(J):
            o_ref[:, j * n:(j + 1) * n] = jnp.dot(av, w_ref[j], preferred_element_type=f32).astype(out_dtype)

    return pl.pallas_call(
        body, name=name, grid=(M // tm,), in_specs=[_rows(tm, K), _full((J, K, n))], out_specs=_rows(tm, J * n),
        out_shape=jax.ShapeDtypeStruct((M, J * n), out_dtype), compiler_params=_params())(a, w3)


def mm_bwd_w_small(a, dy, J, name, *, tm=2048):
    M, K = a.shape
    n = dy.shape[1] // J
    tm = min(tm, M)
    ns = M // tm

    def body(a_ref, dy_ref, o_ref, acc_ref):
        s = pl.program_id(0)
        part = lax.dot_general(_bf(a_ref[...]), _bf(dy_ref[...]), TN_DIMS, preferred_element_type=f32)

        @pl.when(s == 0)
        def _():
            acc_ref[...] = part

        @pl.when(s > 0)
        def _():
            acc_ref[...] += part

        @pl.when(s == ns - 1)
        def _():
            for j in range(J):
                o_ref[j] = acc_ref[:, j * n:(j + 1) * n].astype(bf16)

    return pl.pallas_call(
        body, name=name, grid=(ns,), in_specs=[_rows(tm, K), _rows(tm, J * n)], out_specs=_full((J, K, n)),
        out_shape=jax.ShapeDtypeStruct((J, K, n), bf16), scratch_shapes=[pltpu.VMEM((K, J * n), f32)],
        compiler_params=_params())(a, dy)


def mm_fwd(a, w3, name, *, tm=1024, tn=None, bias=None, outs=(f32,), first=None, second=None, rides=()):
    M, K = a.shape
    J, _, n = w3.shape
    tm = min(tm, M)
    tn = tn or n
    nl = n // tn
    nb = 0 if bias is None else 1

    def body(*refs):
        a_ref, w_ref = refs[0], refs[1]
        acc = jnp.dot(_bf(a_ref[...]), w_ref[0], preferred_element_type=f32)
        if bias is not None:
            acc = acc + refs[2][...]
        refs[2 + nb][...] = (acc if first is None else first(acc)).astype(outs[0])
        if len(outs) > 1:
            refs[3 + nb][...] = second(acc).astype(outs[1])

    in_specs = [pl.BlockSpec((tm, K), lambda j, l, i: (i, 0)),
                pl.BlockSpec((1, K, tn), lambda j, l, i: (j, 0, l))]
    args = [a, w3]
    if bias is not None:
        in_specs.append(pl.BlockSpec((1, tn), lambda j, l, i: (0, j * nl + l)))
        args.append(bias)
    res, landed = _pcall(
        body, args, name=name, grid=(J, nl, M // tm), in_specs=in_specs,
        out_specs=[pl.BlockSpec((tm, tn), lambda j, l, i: (i, j * nl + l)) for _ in outs],
        out_shape=[jax.ShapeDtypeStruct((M, J * n), dt) for dt in outs], rides=rides)
    res = res if len(outs) > 1 else res[0]
    return (res, landed) if rides else res


def proj_own(x, w_own, me, n_blocks, *, tm=1024, rides=()):
    M, K = x.shape
    n = w_own.shape[1]

    def body(me_ref, x_ref, w_ref, o_ref, xb_ref):
        av = _bf(x_ref[...])
        xb_ref[...] = av
        o_ref[...] = jnp.dot(av, w_ref[...], preferred_element_type=f32).astype(bf16)

    (proj, x_bf), landed = _pcall(
        body, [x, w_own], name="proj_own", grid=(M // tm,), prefetch=(me,),
        in_specs=[pl.BlockSpec((tm, K), lambda i, me_ref: (i, 0)), pl.BlockSpec((K, n), lambda i, me_ref: (0, 0))],
        out_specs=[pl.BlockSpec((tm, n), lambda i, me_ref: (i, me_ref[0])),
                   pl.BlockSpec((tm, K), lambda i, me_ref: (i, 0))],
        out_shape=[jax.ShapeDtypeStruct((M, n_blocks * n), bf16), jax.ShapeDtypeStruct((M, K), bf16)], rides=rides)
    return proj, x_bf, landed


def proj_rest(x_bf, w3, me, proj, *, tm=1024, rides=()):
    M, K = x_bf.shape
    J, _, n = w3.shape

    def body(me_ref, x_ref, w_ref, _, o_ref):
        o_ref[...] = jnp.dot(x_ref[...], w_ref[0], preferred_element_type=f32).astype(bf16)

    (proj,), landed = _pcall(
        body, [x_bf, w3, proj], name="proj_rest", grid=(J - 1, M // tm), prefetch=(me,),
        in_specs=[pl.BlockSpec((tm, K), lambda k, i, me_ref: (i, 0)),
                  pl.BlockSpec((1, K, n), lambda k, i, me_ref: (me_ref[0] ^ (k + 1), 0, 0)), HBM_SPEC],
        out_specs=[pl.BlockSpec((tm, n), lambda k, i, me_ref: (i, me_ref[0] ^ (k + 1)))],
        out_shape=[jax.ShapeDtypeStruct(proj.shape, bf16)], aliases={3: 0}, rides=rides)
    return proj, landed


def mm_bwd_x(dy, w3, name, *, tm=TM, epi=None, extras=(), colsum=False, out_dtype=f32, rides=()):
    M = dy.shape[0]
    J, K, n = w3.shape
    tm = min(tm, M)
    nex = len(extras)

    def body(*refs):
        dy_ref, w_hbm = refs[0], refs[1]
        ex = refs[2:2 + nex]
        o_ref = refs[2 + nex]
        w_ref = refs[-1]
        i = pl.program_id(0)

        @pl.when(i == 0)
        def _():
            pltpu.sync_copy(w_hbm, w_ref)

        acc = None
        for j in range(J):
            part = lax.dot_general(_bf(dy_ref[:, j * n:(j + 1) * n]), w_ref[j], NT_DIMS, preferred_element_type=f32)
            acc = part if acc is None else acc + part
        if epi is not None:
            acc = epi(acc, *[e[...] for e in ex])
        o_ref[...] = acc.astype(out_dtype)
        if colsum:
            s_ref = refs[3 + nex]
            cs = jnp.sum(acc, axis=0, keepdims=True)

            @pl.when(i == 0)
            def _():
                s_ref[...] = cs

            @pl.when(i > 0)
            def _():
                s_ref[...] += cs

    in_specs = [pl.BlockSpec((tm, J * n), lambda i: (i, 0)), pl.BlockSpec(memory_space=pl.ANY)]
    in_specs += [pl.BlockSpec((tm, K), lambda i: (i, 0)) for _ in extras]
    out_specs = [pl.BlockSpec((tm, K), lambda i: (i, 0))]
    out_shape = [jax.ShapeDtypeStruct((M, K), out_dtype)]
    if colsum:
        out_specs.append(pl.BlockSpec((1, K), lambda i: (0, 0)))
        out_shape.append(jax.ShapeDtypeStruct((1, K), f32))
    res, landed = _pcall(body, [dy, w3, *extras], name=name, grid=(M // tm,), in_specs=in_specs, out_specs=out_specs,
                         out_shape=out_shape, scratch_shapes=[pltpu.VMEM((J, K, n), bf16)], rides=rides)
    res = res if colsum else res[0]
    return (res, landed) if rides else res


def mm_rows_fused(a, w3, name, *, transposed, tm, epi, extras=(), consts=(), outs=(f32,), nsums=0, sum_widths=None,
                  rides=()):
    pro = a[0] if isinstance(a, tuple) else None
    J, K, n = w3.shape
    W = K if transposed else n
    ka = J * n if transposed else K
    if pro is not None:
        a_rows = [e if isinstance(e, tuple) else (e, ka, 0) for e in a[1]]
        a_consts = list(a[2])
    else:
        a_rows, a_consts = [(a, ka, 0)], []
    M = a_rows[0][0].shape[0]
    tm = min(tm, M)
    na = len(a_rows) + len(a_consts)
    nex, nco = len(extras), len(consts)
    if pro is not None:
        outs = (bf16 if ka == W else (bf16, ka, ka),) + tuple(outs)
    nout = len(outs)
    nsums = len(sum_widths) if sum_widths is not None else nsums
    out_dtypes = [o[0] if isinstance(o, tuple) else o for o in outs]

    def body(*refs):
        a_refs, w_hbm = refs[:na], refs[na]
        ex = refs[na + 1:na + 1 + nex]
        co = refs[na + 1 + nex:na + 1 + nex + nco]
        o_refs = refs[na + 1 + nex + nco:na + 1 + nex + nco + nout]
        s_refs = refs[na + 1 + nex + nco + nout:na + 1 + nex + nco + nout + nsums]
        w_ref = refs[-1]
        i = pl.program_id(0)

        @pl.when(i == 0)
        def _():
            pltpu.sync_copy(w_hbm, w_ref)

        av = _bf(a_refs[0][...] if pro is None else pro(*[r[...] for r in a_refs]))
        if transposed:
            acc = None
            for j in range(J):
                part = lax.dot_general(av[:, j * n:(j + 1) * n], w_ref[j], NT_DIMS, preferred_element_type=f32)
                acc = part if acc is None else acc + part
        else:
            acc = jnp.dot(av, w_ref[0], preferred_element_type=f32)
        rows, sums = epi(acc, *[e[...] for e in ex], *[c[...] for c in co])
        if pro is not None:
            rows = (av,) + tuple(rows)
        for o_ref, v, dt in zip(o_refs, rows, out_dtypes):
            o_ref[...] = v.astype(dt)
        for s_ref, v in zip(s_refs, sums):
            @pl.when(i == 0)
            def _(s_ref=s_ref, v=v):
                s_ref[...] = v

            @pl.when(i > 0)
            def _(s_ref=s_ref, v=v):
                s_ref[...] += v

    extras = [e if isinstance(e, tuple) else (e, W, 0) for e in extras]
    outs_full = [o if isinstance(o, tuple) else (o, W, W) for o in outs]
    widths = list(sum_widths) if sum_widths is not None else [W] * nsums
    in_specs = [_rows(tm, w, cb) for _, w, cb in a_rows] + [_full(c.shape) for c in a_consts]
    in_specs += [pl.BlockSpec(memory_space=pl.ANY)]
    in_specs += [_rows(tm, w, cb) for _, w, cb in extras] + [_full(c.shape) for c in consts]
    out_specs = [_rows(tm, w) for _, w, _ in outs_full] + [_full((1, w)) for w in widths]
    out_shape = ([jax.ShapeDtypeStruct((M, cols), dt) for dt, _, cols in outs_full]
                 + [jax.ShapeDtypeStruct((1, w), f32) for w in widths])
    args = [e for e, _, _ in a_rows] + a_consts + [w3] + [e for e, _, _ in extras] + list(consts)
    res, landed = _pcall(body, args, name=name, grid=(M // tm,), in_specs=in_specs, out_specs=out_specs,
                         out_shape=out_shape, scratch_shapes=[pltpu.VMEM((J, K, n), bf16)], rides=rides)
    return (res, landed) if rides else res


def mm_bwd_w(a, dy, J, name, *, tm=2048, tn=None, tk=None, rides=()):
    M, K = a.shape
    n = dy.shape[1] // J
    tm = min(tm, M)
    tn = tn or n
    tk = tk or K
    nl = n // tn
    nk = K // tk
    ns = M // tm

    def body(a_ref, dy_ref, o_ref, acc_ref):
        s = pl.program_id(3)
        part = lax.dot_general(_bf(a_ref[...]), _bf(dy_ref[...]), TN_DIMS, preferred_element_type=f32)

        @pl.when(s == 0)
        def _():
            acc_ref[...] = part

        @pl.when(s > 0)
        def _():
            acc_ref[...] += part

        @pl.when(s == ns - 1)
        def _():
            o_ref[0] = acc_ref[...].astype(bf16)

    res, landed = _pcall(
        body, [a, dy], name=name, grid=(J, nl, nk, ns),
        in_specs=[pl.BlockSpec((tm, tk), lambda j, l, k, s: (s, k)),
                  pl.BlockSpec((tm, tn), lambda j, l, k, s: (s, j * nl + l))],
        out_specs=[pl.BlockSpec((1, tk, tn), lambda j, l, k, s: (j, k, l))],
        out_shape=[jax.ShapeDtypeStruct((J, K, n), bf16)],
        scratch_shapes=[pltpu.VMEM((tk, tn), f32)], rides=rides)
    return (res[0], landed) if rides else res[0]


def _relu2(v):
    r = jnp.maximum(v, 0.0)
    return r * r


HALO = 16


def _shift_down(z, k, halo):
    r = lax.broadcasted_iota(jnp.int32, z.shape, 0)
    y = pltpu.roll(z, k, 0)
    for q in range(k):
        y = jnp.where(r == q, halo[HALO - k + q:HALO - k + q + 1, :], y)
    return y


def _shift_up(z, k, halo):
    tm = z.shape[0]
    r = lax.broadcasted_iota(jnp.int32, z.shape, 0)
    y = pltpu.roll(z, tm - k, 0)
    for q in range(k):
        y = jnp.where(r == tm - k + q, halo[q:q + 1, :], y)
    return y


def _prev_halo(tm, cb):
    return pl.BlockSpec((HALO, CONV_WIDTH), lambda i: (jnp.maximum(i * (tm // HALO) - 1, 0), cb))


def _next_halo(tm, cb, nblk):
    return pl.BlockSpec((HALO, CONV_WIDTH), lambda i: (jnp.minimum((i + 1) * (tm // HALO), nblk - 1), cb))


def _f32(ref):
    return ref[...].astype(f32)


def conv_fwd(proj, conv_w, wco3, tm=TM):
    S = proj.shape[0]
    J, _, n = wco3.shape

    def body(cb_ref, cc_ref, ch_ref, cch_ref, chh_ref, w_ref, wo_ref, a_ref, y_ref):
        i = pl.program_id(0)
        z = _f32(cc_ref) * _f32(ch_ref)
        zh = jnp.where(i == 0, 0.0, _f32(cch_ref) * _f32(chh_ref))
        w = w_ref[...]
        dwz = w[0:1, :] * _shift_down(z, 2, zh) + w[1:2, :] * _shift_down(z, 1, zh) + w[2:3, :] * z
        a = (_f32(cb_ref) * dwz).astype(bf16)
        a_ref[...] = a
        for j in range(J):
            y_ref[:, j * n:(j + 1) * n] = jnp.dot(a, wo_ref[j], preferred_element_type=f32).astype(bf16)

    return pl.pallas_call(
        body, name="conv_fwd", grid=(S // tm,),
        in_specs=[_rows(tm, CONV_WIDTH, 6), _rows(tm, CONV_WIDTH, 7), _rows(tm, CONV_WIDTH, 8),
                  _prev_halo(tm, 7), _prev_halo(tm, 8), _full((3, CONV_WIDTH)), _full(wco3.shape)],
        out_specs=[_rows(tm, CONV_WIDTH), _rows(tm, J * n)],
        out_shape=[jax.ShapeDtypeStruct((S, CONV_WIDTH), bf16), jax.ShapeDtypeStruct((S, J * n), bf16)],
        compiler_params=_params())(proj, proj, proj, proj, proj, conv_w, wco3)


def conv_bwd(dy_a, a_conv, wco3, proj, conv_w, dproj, tm=TM):
    S = proj.shape[0]
    nt = S // tm
    J, _, n = wco3.shape

    def body(dy_ref, dyn_ref, a_ref, wo_ref, cb_ref, cc_ref, ch_ref, cch_ref, chh_ref, cbn_ref, w_ref, _, o_ref,
             dw_ref, dwo_ref, acc_ref):
        i = pl.program_id(0)

        def times_wo_t(dy):
            out = None
            for j in range(J):
                part = lax.dot_general(dy[:, j * n:(j + 1) * n], wo_ref[j], NT_DIMS, preferred_element_type=f32)
                out = part if out is None else out + part
            return out

        dyv = dy_ref[...]
        dav, dan = times_wo_t(dyv), times_wo_t(dyn_ref[...])
        part = lax.dot_general(a_ref[...], dyv, TN_DIMS, preferred_element_type=f32)

        @pl.when(i == 0)
        def _():
            acc_ref[...] = part

        @pl.when(i > 0)
        def _():
            acc_ref[...] += part

        @pl.when(i == nt - 1)
        def _():
            for j in range(J):
                dwo_ref[j] = acc_ref[:, j * n:(j + 1) * n].astype(bf16)

        cc, ch, cb = _f32(cc_ref), _f32(ch_ref), _f32(cb_ref)
        z = cc * ch
        zh = jnp.where(i == 0, 0.0, _f32(cch_ref) * _f32(chh_ref))
        w = w_ref[...]
        z1 = _shift_down(z, 1, zh)
        z2 = _shift_down(z, 2, zh)
        dwz = w[0:1, :] * z2 + w[1:2, :] * z1 + w[2:3, :] * z
        g = dav * cb
        gn = jnp.where(i == nt - 1, 0.0, dan * _f32(cbn_ref))
        dz = w[2:3, :] * g + w[1:2, :] * _shift_up(g, 1, gn) + w[0:1, :] * _shift_up(g, 2, gn)
        o_ref[:, 0:CONV_WIDTH] = (dav * dwz).astype(bf16)
        o_ref[:, CONV_WIDTH:2 * CONV_WIDTH] = (dz * ch).astype(bf16)
        o_ref[:, 2 * CONV_WIDTH:3 * CONV_WIDTH] = (dz * cc).astype(bf16)

        @pl.when(i == 0)
        def _():
            dw_ref[...] = jnp.zeros_like(dw_ref)

        dw_ref[0:1, :] += jnp.sum(g * z2, axis=0, keepdims=True)
        dw_ref[1:2, :] += jnp.sum(g * z1, axis=0, keepdims=True)
        dw_ref[2:3, :] += jnp.sum(g * z, axis=0, keepdims=True)

    return pl.pallas_call(
        body, name="conv_bwd", grid=(nt,),
        in_specs=[_rows(tm, J * n),
                  pl.BlockSpec((HALO, J * n), lambda i: (jnp.minimum((i + 1) * (tm // HALO), S // HALO - 1), 0)),
                  _rows(tm, CONV_WIDTH), _full(wco3.shape),
                  _rows(tm, CONV_WIDTH, 6), _rows(tm, CONV_WIDTH, 7), _rows(tm, CONV_WIDTH, 8),
                  _prev_halo(tm, 7), _prev_halo(tm, 8), _next_halo(tm, 6, S // HALO), _full((3, CONV_WIDTH)),
                  pl.BlockSpec(memory_space=pl.ANY)],
        out_specs=[_rows(tm, 3 * CONV_WIDTH, GATE_COLS // (3 * CONV_WIDTH)), _full((3, CONV_WIDTH)),
                   _full(wco3.shape)],
        out_shape=[jax.ShapeDtypeStruct(dproj.shape, bf16), jax.ShapeDtypeStruct((3, CONV_WIDTH), f32),
                   jax.ShapeDtypeStruct(wco3.shape, bf16)],
        scratch_shapes=[pltpu.VMEM((CONV_WIDTH, J * n), f32)], input_output_aliases={11: 0},
        compiler_params=_params())(dy_a, dy_a, a_conv, wco3, proj, proj, proj, proj, proj, proj, conv_w, dproj)


def _cmul(ar, ai, br, bi):
    return ar * br - ai * bi, ar * bi + ai * br


def ssm_prep(lr, li, ldt, bt_re, bt_im):
    def body(lr_ref, li_ref, ldt_ref, br_ref, bi_ref, ar_ref, ai_ref, kr_ref, ki_ref, bbr_ref, bbi_ref):
        lrv, liv = lr_ref[...], li_ref[...]
        dt = jnp.exp(ldt_ref[...])
        mag = jnp.exp(lrv * dt)
        ar = mag * jnp.cos(liv * dt)
        ai = mag * jnp.sin(liv * dt)
        den = lrv * lrv + liv * liv
        nr = ar - 1.0
        kr = (nr * lrv + ai * liv) / den
        ki = (ai * lrv - nr * liv) / den
        ar_ref[...], ai_ref[...], kr_ref[...], ki_ref[...] = ar, ai, kr, ki
        bbr_ref[...] = kr * br_ref[...] - ki * bi_ref[...]
        bbi_ref[...] = kr * bi_ref[...] + ki * br_ref[...]

    v = jax.ShapeDtypeStruct((1, N_STATES), f32)
    m = jax.ShapeDtypeStruct((SSM_GROUP, N_STATES), f32)
    return pl.pallas_call(body, name="ssm_prep", out_shape=[v, v, v, v, m, m])(lr, li, ldt, bt_re, bt_im)


def _pow_segment(ar, ai, seg_len):
    pr, pi = ar, ai
    for _ in range(int(math.log2(seg_len))):
        pr, pi = _cmul(pr, pi, pr, pi)
    return pr, pi


GELU_K = math.sqrt(2.0 / math.pi)
GELU_C = 0.044715


def _gelu(v):
    return 0.5 * v * (1.0 + jnp.tanh(GELU_K * (v + GELU_C * v * v * v)))


def _gelu_grad(v):
    t = jnp.tanh(GELU_K * (v + GELU_C * v * v * v))
    return 0.5 * (1.0 + t) + 0.5 * v * (1.0 - t * t) * GELU_K * (1.0 + 3.0 * GELU_C * v * v)


def ssm_scan_fwd(u_perm, bd_r, bd_i, ar, ai, *, final=None, rides=()):
    S = u_perm.shape[0]
    R = SCAN_ROWS
    nblk = S // R
    seg_len = S // N_SEG
    nstrip = N_STATES // LANE_STRIP
    store = final is not None

    def body(*refs):
        if store:
            (u_ref, bdr_ref, bdi_ref, ar_ref, ai_ref, cdr_ref, cdi_ref, d_ref, ir_ref, ii_ref,
             y_ref, ys_ref, sr_ref, si_ref, bur, bui, car, cai) = refs
        else:
            u_ref, bdr_ref, bdi_ref, ar_ref, ai_ref, or_ref, oi_ref, bur, bui, car, cai = refs
        i = pl.program_id(0)
        ub = _bf(u_ref[...])
        u = ub.astype(f32)
        for cb in range(4):
            us = ub[:, cb * 128:(cb + 1) * 128]
            bur[:, cb * 512:(cb + 1) * 512] = jnp.dot(us, bdr_ref[cb], preferred_element_type=f32)
            bui[:, cb * 512:(cb + 1) * 512] = jnp.dot(us, bdi_ref[cb], preferred_element_type=f32)

        @pl.when(i == 0)
        def _():
            if store:
                car[...] = ir_ref[...]
                cai[...] = ii_ref[...]
            else:
                car[...] = jnp.zeros_like(car)
                cai[...] = jnp.zeros_like(cai)

        for ls in range(nstrip):
            lanes = pl.ds(ls * LANE_STRIP, LANE_STRIP)
            a_r = jnp.broadcast_to(ar_ref[:, lanes], (N_SEG, LANE_STRIP))
            a_i = jnp.broadcast_to(ai_ref[:, lanes], (N_SEG, LANE_STRIP))

            def step(t, carry, lanes=lanes, a_r=a_r, a_i=a_i):
                s_r, s_i = carry
                row = pl.multiple_of(t * 8, 8)
                n_r = a_r * s_r - a_i * s_i + bur[pl.ds(row, 8), lanes]
                n_i = a_r * s_i + a_i * s_r + bui[pl.ds(row, 8), lanes]
                if store:
                    sr_ref[pl.ds(row, 8), lanes] = n_r
                    si_ref[pl.ds(row, 8), lanes] = n_i
                return n_r, n_i

            e_r, e_i = lax.fori_loop(0, R // 8, step, (car[:, lanes], cai[:, lanes]), unroll=True)
            car[:, lanes] = e_r
            cai[:, lanes] = e_i

        if store:
            for cb in range(4):
                st_r = sr_ref[:, cb * 512:(cb + 1) * 512].astype(bf16)
                st_i = si_ref[:, cb * 512:(cb + 1) * 512].astype(bf16)
                y = (jnp.dot(st_r, cdr_ref[cb], preferred_element_type=f32)
                     - jnp.dot(st_i, cdi_ref[cb], preferred_element_type=f32))
                cols = slice(cb * 128, (cb + 1) * 128)
                y = y + d_ref[:, cols] * u[:, cols]
                y_ref[:, cols] = y
                ys_ref[:, cols] = _gelu(y).astype(bf16)
        else:
            @pl.when(i == nblk - 1)
            def _():
                p_r, p_i = _pow_segment(ar_ref[...], ai_ref[...], seg_len)
                t_r, t_i = car[0:1, :], cai[0:1, :]
                or_ref[0:1, :] = jnp.zeros((1, N_STATES), f32)
                oi_ref[0:1, :] = jnp.zeros((1, N_STATES), f32)
                for j in range(1, N_SEG):
                    or_ref[j:j + 1, :] = t_r
                    oi_ref[j:j + 1, :] = t_i
                    m_r, m_i = _cmul(p_r, p_i, t_r, t_i)
                    t_r, t_i = car[j:j + 1, :] + m_r, cai[j:j + 1, :] + m_i

    blk = lambda w: pl.BlockSpec((R, w), lambda i: (i, 0))
    in_specs = [blk(SSM_WIDTH), _full((4, 128, 512)), _full((4, 128, 512)), _full((1, N_STATES)), _full((1, N_STATES))]
    args = [u_perm, bd_r, bd_i, ar, ai]
    scratch = [pltpu.VMEM((R, N_STATES), f32), pltpu.VMEM((R, N_STATES), f32),
               pltpu.VMEM((N_SEG, N_STATES), f32), pltpu.VMEM((N_SEG, N_STATES), f32)]
    if store:
        in_specs += [_full((4, 512, 128)), _full((4, 512, 128)), _full((1, SSM_WIDTH)),
                     _full((N_SEG, N_STATES)), _full((N_SEG, N_STATES))]
        args += list(final)
        out_specs = [blk(SSM_WIDTH), blk(SSM_WIDTH), blk(N_STATES), blk(N_STATES)]
        out_shape = [jax.ShapeDtypeStruct((S, SSM_WIDTH), f32), jax.ShapeDtypeStruct((S, SSM_WIDTH), bf16),
                     jax.ShapeDtypeStruct((S, N_STATES), f32), jax.ShapeDtypeStruct((S, N_STATES), f32)]
        name = "ssm_scan_fwd"
    else:
        out_specs = [_full((N_SEG, N_STATES)), _full((N_SEG, N_STATES))]
        out_shape = [jax.ShapeDtypeStruct((N_SEG, N_STATES), f32)] * 2
        name = "ssm_scan_fwd_carry"
    res, landed = _pcall(body, args, name=name, grid=(nblk,), in_specs=in_specs, out_specs=out_specs,
                         out_shape=out_shape, scratch_shapes=scratch, rides=rides)
    return (res, landed) if rides else res


def ssm_scan_bwd(dys_perm, y_perm, cdt_r, cdt_i, ar, ai, *, final=None, rides=()):
    S = dys_perm.shape[0]
    R = SCAN_ROWS
    nblk = S // R
    seg_len = S // N_SEG
    store = final is not None
    strip = LANE_STRIP // 2 if store else LANE_STRIP
    nstrip = N_STATES // strip

    def body(*refs):
        if store:
            (dys_ref, y_ref, cdr_ref, cdi_ref, ar_ref, ai_ref, u_ref, sr_ref, si_ref, bdr_ref, bdi_ref, d_ref, ir_ref,
             ii_ref, du_ref, dbr_ref, dbi_ref, dcr_ref, dci_ref, dar_ref, dai_ref, dd_ref, dsr, dsi, lmr, lmi, car,
             cai) = refs
        else:
            dys_ref, y_ref, cdr_ref, cdi_ref, ar_ref, ai_ref, or_ref, oi_ref, dsr, dsi, car, cai = refs
        i = pl.program_id(0)
        dy = dys_ref[...] * _gelu_grad(y_ref[...])
        dyb = dy.astype(bf16)
        for cb in range(4):
            ds_ = dyb[:, cb * 128:(cb + 1) * 128]
            dsr[:, cb * 512:(cb + 1) * 512] = jnp.dot(ds_, cdr_ref[cb], preferred_element_type=f32)
            dsi[:, cb * 512:(cb + 1) * 512] = -jnp.dot(ds_, cdi_ref[cb], preferred_element_type=f32)

        @pl.when(i == 0)
        def _():
            if store:
                car[...] = ir_ref[...]
                cai[...] = ii_ref[...]
                dar_ref[...] = jnp.zeros_like(dar_ref)
                dai_ref[...] = jnp.zeros_like(dai_ref)
                dbr_ref[...] = jnp.zeros_like(dbr_ref)
                dbi_ref[...] = jnp.zeros_like(dbi_ref)
                dcr_ref[...] = jnp.zeros_like(dcr_ref)
                dci_ref[...] = jnp.zeros_like(dci_ref)
                dd_ref[...] = jnp.zeros_like(dd_ref)
            else:
                car[...] = jnp.zeros_like(car)
                cai[...] = jnp.zeros_like(cai)

        for ls in range(nstrip):
            lanes = pl.ds(ls * strip, strip)
            a_r = jnp.broadcast_to(ar_ref[:, lanes], (N_SEG, strip))
            a_i = jnp.broadcast_to(ai_ref[:, lanes], (N_SEG, strip))
            zero = jnp.zeros((N_SEG, strip), f32)

            def step(k, carry, lanes=lanes, a_r=a_r, a_i=a_i):
                l_r, l_i, g_r, g_i = carry
                row = pl.multiple_of((R // 8 - 1 - k) * 8, 8)
                if store:
                    s_r = sr_ref[pl.ds(row, 8), lanes]
                    s_i = si_ref[pl.ds(row, 8), lanes]
                    g_r = g_r + l_r * s_r + l_i * s_i
                    g_i = g_i + l_i * s_r - l_r * s_i
                n_r = dsr[pl.ds(row, 8), lanes] + a_r * l_r + a_i * l_i
                n_i = dsi[pl.ds(row, 8), lanes] + a_r * l_i - a_i * l_r
                if store:
                    lmr[pl.ds(row, 8), lanes] = n_r
                    lmi[pl.ds(row, 8), lanes] = n_i
                return n_r, n_i, g_r, g_i

            e_r, e_i, g_r, g_i = lax.fori_loop(0, R // 8, step, (car[:, lanes], cai[:, lanes], zero, zero),
                                               unroll=True)
            car[:, lanes] = e_r
            cai[:, lanes] = e_i
            if store:
                dar_ref[:, lanes] += g_r
                dai_ref[:, lanes] += g_i

        if store:
            ub = _bf(u_ref[...])
            u = ub.astype(f32)
            for cb in range(4):
                cols = slice(cb * 128, (cb + 1) * 128)
                st = slice(cb * 512, (cb + 1) * 512)
                l_r = lmr[:, st].astype(bf16)
                l_i = lmi[:, st].astype(bf16)
                du = (jnp.dot(l_r, bdr_ref[cb], preferred_element_type=f32)
                      + jnp.dot(l_i, bdi_ref[cb], preferred_element_type=f32))
                du_ref[:, cols] = (du + d_ref[:, cols] * dy[:, cols]).astype(bf16)
                dbr_ref[cb] += lax.dot_general(l_r, ub[:, cols], TN_DIMS, preferred_element_type=f32)
                dbi_ref[cb] += lax.dot_general(l_i, ub[:, cols], TN_DIMS, preferred_element_type=f32)
                dcr_ref[cb] += lax.dot_general(dyb[:, cols], sr_ref[:, st].astype(bf16), TN_DIMS,
                                               preferred_element_type=f32)
                dci_ref[cb] += lax.dot_general(dyb[:, cols], si_ref[:, st].astype(bf16), TN_DIMS,
                                               preferred_element_type=f32)
            dd_ref[...] += jnp.sum(dy * u, axis=0, keepdims=True)
        else:
            @pl.when(i == nblk - 1)
            def _():
                p_r, p_i = _pow_segment(ar_ref[...], ai_ref[...], seg_len)
                p_i = -p_i
                t_r, t_i = car[N_SEG - 1:N_SEG, :], cai[N_SEG - 1:N_SEG, :]
                or_ref[N_SEG - 1:N_SEG, :] = jnp.zeros((1, N_STATES), f32)
                oi_ref[N_SEG - 1:N_SEG, :] = jnp.zeros((1, N_STATES), f32)
                for j in range(N_SEG - 2, -1, -1):
                    or_ref[j:j + 1, :] = t_r
                    oi_ref[j:j + 1, :] = t_i
                    m_r, m_i = _cmul(p_r, p_i, t_r, t_i)
                    t_r, t_i = car[j:j + 1, :] + m_r, cai[j:j + 1, :] + m_i

    blk = lambda w: pl.BlockSpec((R, w), lambda i: (nblk - 1 - i, 0))
    in_specs = [blk(SSM_WIDTH), blk(SSM_WIDTH), _full((4, 128, 512)), _full((4, 128, 512)), _full((1, N_STATES)),
                _full((1, N_STATES))]
    args = [dys_perm, y_perm, cdt_r, cdt_i, ar, ai]
    seg = jax.ShapeDtypeStruct((N_SEG, N_STATES), f32)
    if store:
        in_specs += [blk(SSM_WIDTH), blk(N_STATES), blk(N_STATES), _full((4, 512, 128)), _full((4, 512, 128)),
                     _full((1, SSM_WIDTH)), _full((N_SEG, N_STATES)), _full((N_SEG, N_STATES))]
        args += list(final)
        out_specs = [blk(SSM_WIDTH), _full((4, 512, 128)), _full((4, 512, 128)), _full((4, 128, 512)),
                     _full((4, 128, 512)), _full((N_SEG, N_STATES)), _full((N_SEG, N_STATES)), _full((1, SSM_WIDTH))]
        b_acc = jax.ShapeDtypeStruct((4, 512, 128), f32)
        c_acc = jax.ShapeDtypeStruct((4, 128, 512), f32)
        out_shape = [jax.ShapeDtypeStruct((S, SSM_WIDTH), bf16), b_acc, b_acc, c_acc, c_acc, seg, seg,
                     jax.ShapeDtypeStruct((1, SSM_WIDTH), f32)]
        scratch = [pltpu.VMEM((R, N_STATES), f32)] * 4 + [pltpu.VMEM((N_SEG, N_STATES), f32)] * 2
        name = "ssm_scan_bwd"
    else:
        out_specs = [_full((N_SEG, N_STATES)), _full((N_SEG, N_STATES))]
        out_shape = [seg, seg]
        scratch = [pltpu.VMEM((R, N_STATES), f32)] * 2 + [pltpu.VMEM((N_SEG, N_STATES), f32)] * 2
        name = "ssm_scan_bwd_carry"
    res, landed = _pcall(body, args, name=name, grid=(nblk,), in_specs=in_specs, out_specs=out_specs,
                         out_shape=out_shape, scratch_shapes=scratch, rides=rides)
    return (res, landed) if rides else res


def ssm_param_bwd(dbb_r, dbb_i, bt_re, bt_im, kr, ki, ar, ai, lr, li, ldt, da_r, da_i):
    def body(dbr_ref, dbi_ref, br_ref, bi_ref, kr_ref, ki_ref, ar_ref, ai_ref, lr_ref, li_ref, ldt_ref, dar_ref,
             dai_ref, obr_ref, obi_ref, olr_ref, oli_ref, odt_ref):
        dbr, dbi, b_r, b_i = dbr_ref[...], dbi_ref[...], br_ref[...], bi_ref[...]
        k_r, k_i, a_r, a_i = kr_ref[...], ki_ref[...], ar_ref[...], ai_ref[...]
        l_r, l_i = lr_ref[...], li_ref[...]
        dt = jnp.exp(ldt_ref[...])
        obr_ref[...] = k_r * dbr + k_i * dbi
        obi_ref[...] = k_r * dbi - k_i * dbr
        gk_r = jnp.sum(dbr * b_r + dbi * b_i, axis=0, keepdims=True)
        gk_i = jnp.sum(dbi * b_r - dbr * b_i, axis=0, keepdims=True)
        ga_r = jnp.sum(dar_ref[...], axis=0, keepdims=True)
        ga_i = jnp.sum(dai_ref[...], axis=0, keepdims=True)
        den = l_r * l_r + l_i * l_i
        c_r, c_i = l_r / den, l_i / den
        m_r, m_i = _cmul(c_r, c_i, gk_r, gk_i)
        g_r, g_i = ga_r + m_r, ga_i + m_i
        t1_r, t1_i = _cmul(dt * a_r, -dt * a_i, g_r, g_i)
        q_r, q_i = _cmul(k_r, k_i, c_r, -c_i)
        t2_r, t2_i = _cmul(-q_r, q_i, gk_r, gk_i)
        olr_ref[...] = t1_r + t2_r
        oli_ref[...] = t1_i + t2_i
        w_r, w_i = _cmul(l_r, l_i, a_r, a_i)
        odt_ref[...] = dt * (w_r * g_r + w_i * g_i)

    v = jax.ShapeDtypeStruct((1, N_STATES), f32)
    m = jax.ShapeDtypeStruct((SSM_GROUP, N_STATES), f32)
    return pl.pallas_call(body, name="ssm_param_bwd", out_shape=[m, m, v, v, v])(
        dbb_r, dbb_i, bt_re, bt_im, kr, ki, ar, ai, lr, li, ldt, da_r, da_i)


def group_sum(v):
    def body(v_ref, o_ref):
        o_ref[...] = jnp.sum(v_ref[...], axis=-1, keepdims=True)
    return pl.pallas_call(body, name="group_sum", out_shape=jax.ShapeDtypeStruct((v.shape[0], 1), f32))(v)


ATTN_SCALE = XATTN_HEAD_DIM ** -0.5


def _attn_probs(q_h, k_h):
    s = lax.dot_general(q_h, k_h, NT_DIMS, preferred_element_type=f32) * ATTN_SCALE
    e = jnp.exp(s - jnp.max(s, axis=-1, keepdims=True))
    return e / jnp.sum(e, axis=-1, keepdims=True)


def attn_fwd(proj, kv, wxo3, tm=TM):
    S = proj.shape[0]
    M = kv.shape[0]
    J, _, n = wxo3.shape

    def body(q_ref, kv_ref, w_ref, o_ref, y_ref):
        for h in range(XATTN_HEADS):
            cols = slice(h * XATTN_HEAD_DIM, (h + 1) * XATTN_HEAD_DIM)
            q_h = q_ref[:, cols].astype(bf16)
            k_h = kv_ref[:, cols]
            v_h = kv_ref[:, XATTN_WIDTH + h * XATTN_HEAD_DIM:XATTN_WIDTH + (h + 1) * XATTN_HEAD_DIM]
            p = _attn_probs(q_h, k_h)
            o_ref[:, cols] = jnp.dot(p.astype(bf16), v_h, preferred_element_type=f32).astype(bf16)
        o = o_ref[...]
        for j in range(J):
            y_ref[:, j * n:(j + 1) * n] = jnp.dot(o, w_ref[j], preferred_element_type=f32).astype(bf16)

    return pl.pallas_call(
        body, name="attn_fwd", grid=(S // tm,),
        in_specs=[_rows(tm, XATTN_WIDTH, 10), _full((M, 2 * XATTN_WIDTH)), _full(wxo3.shape)],
        out_specs=[_rows(tm, XATTN_WIDTH), _rows(tm, J * n)],
        out_shape=[jax.ShapeDtypeStruct((S, XATTN_WIDTH), bf16), jax.ShapeDtypeStruct((S, J * n), bf16)],
        compiler_params=_params())(proj, kv, wxo3)


def attn_bwd(dy_c, o_att, wxo3, proj, kv, dproj, tm=TM):
    S = proj.shape[0]
    M = kv.shape[0]
    J, _, n = wxo3.shape
    nt = S // tm

    def body(dy_ref, o_ref, w_ref, q_ref, kv_ref, _, dq_ref, dkv_ref, dwo_ref, acc_ref):
        i = pl.program_id(0)

        @pl.when(i == 0)
        def _():
            dkv_ref[...] = jnp.zeros_like(dkv_ref)

        part = lax.dot_general(o_ref[...], dy_ref[...], TN_DIMS, preferred_element_type=f32)

        @pl.when(i == 0)
        def _():
            acc_ref[...] = part

        @pl.when(i > 0)
        def _():
            acc_ref[...] += part

        @pl.when(i == nt - 1)
        def _():
            for j in range(J):
                dwo_ref[j] = acc_ref[:, j * n:(j + 1) * n].astype(bf16)

        do = None
        for j in range(J):
            part = lax.dot_general(dy_ref[:, j * n:(j + 1) * n], w_ref[j], NT_DIMS, preferred_element_type=f32)
            do = part if do is None else do + part
        do = do.astype(bf16)
        for h in range(XATTN_HEADS):
            cols = slice(h * XATTN_HEAD_DIM, (h + 1) * XATTN_HEAD_DIM)
            vcols = slice(XATTN_WIDTH + h * XATTN_HEAD_DIM, XATTN_WIDTH + (h + 1) * XATTN_HEAD_DIM)
            q_h = q_ref[:, cols].astype(bf16)
            k_h = kv_ref[:, cols]
            v_h = kv_ref[:, vcols]
            do_h = do[:, cols]
            p = _attn_probs(q_h, k_h)
            dp = lax.dot_general(do_h, v_h, NT_DIMS, preferred_element_type=f32)
            ds = (p * (dp - jnp.sum(dp * p, axis=-1, keepdims=True)) * ATTN_SCALE).astype(bf16)
            dq_ref[:, cols] = jnp.dot(ds, k_h, preferred_element_type=f32).astype(bf16)
            dkv_ref[:, cols] += lax.dot_general(ds, q_h, TN_DIMS, preferred_element_type=f32)
            dkv_ref[:, vcols] += lax.dot_general(p.astype(bf16), do_h, TN_DIMS, preferred_element_type=f32)

    return pl.pallas_call(
        body, name="attn_bwd", grid=(S // tm,),
        in_specs=[_rows(tm, J * n), _rows(tm, XATTN_WIDTH), _full(wxo3.shape), _rows(tm, XATTN_WIDTH, 10),
                  _full((M, 2 * XATTN_WIDTH)), pl.BlockSpec(memory_space=pl.ANY)],
        out_specs=[_rows(tm, XATTN_WIDTH, 10), _full((M, 2 * XATTN_WIDTH)), _full(wxo3.shape)],
        out_shape=[jax.ShapeDtypeStruct(dproj.shape, bf16), jax.ShapeDtypeStruct((M, 2 * XATTN_WIDTH), f32),
                   jax.ShapeDtypeStruct(wxo3.shape, bf16)],
        scratch_shapes=[pltpu.VMEM((XATTN_WIDTH, J * n), f32)], input_output_aliases={5: 0},
        compiler_params=_params())(dy_c, o_att, wxo3, proj, kv, dproj)


def _ln_stats(r):
    mu = jnp.mean(r, axis=-1, keepdims=True)
    xc = r - mu
    var = jnp.mean(xc * xc, axis=-1, keepdims=True)
    rstd = lax.rsqrt(var + LN_EPS)
    return xc * rstd, rstd


def _ln_bwd(dy, xhat, rstd, g):
    dxh = dy * g
    return rstd * (dxh - jnp.mean(dxh, axis=-1, keepdims=True) - xhat * jnp.mean(dxh * xhat, axis=-1, keepdims=True))


def _colsum(v):
    return jnp.sum(v, axis=0, keepdims=True)


def _ln2_loss_epi(t2, h1, target, b_down, g, b):
    xhat, rstd = _ln_stats(ALPHA * h1 + t2 + b_down)
    err = xhat * g + b - target
    dout = err * (1.0 / D_MODEL)
    dr = _ln_bwd(dout, xhat, rstd, g)
    return (dr, dr), (0.5 * _colsum(err * err) * (1.0 / D_MODEL), _colsum(dout * xhat), _colsum(dout), _colsum(dr))


def _merge_fwd_pro(g0, g1, g2, y_a, glu_a, glu_b, y_c, b):
    gs = [_sig(g.astype(f32) + b[:, k * D_MODEL:(k + 1) * D_MODEL]) for k, g in enumerate((g0, g1, g2))]
    y_b = glu_a.astype(f32) * _sig(glu_b.astype(f32))
    return gs[0] * y_a.astype(f32) + gs[1] * y_b + gs[2] * y_c.astype(f32)


def _merge_bwd_epi(dm, g0, g1, g2, y_a, glu_a, glu_b, y_c, b):
    ga, sb = glu_a.astype(f32), _sig(glu_b.astype(f32))
    ys = (y_a.astype(f32), ga * sb, y_c.astype(f32))
    gs = [_sig(g.astype(f32) + b[:, k * D_MODEL:(k + 1) * D_MODEL]) for k, g in enumerate((g0, g1, g2))]
    dpre = [dm * ys[k] * gs[k] * (1.0 - gs[k]) for k in range(3)]
    dyb = dm * gs[1]
    dglu = jnp.concatenate([dyb * sb, dyb * ga * sb * (1.0 - sb)], axis=1)
    return ((jnp.concatenate(dpre, axis=1), dm * gs[0], dglu, dm * gs[2]),
            (jnp.concatenate([_colsum(d) for d in dpre], axis=1),))


def _ln1_fwd_epi(t1, x, g, b):
    xhat, _ = _ln_stats(ALPHA * x + t1)
    h = xhat * g + b
    return (t1, h, h), ()


def _ln1_bwd_epi(t3, x, t1, dr2, g):
    xhat, rstd = _ln_stats(ALPHA * x + t1)
    dh = ALPHA * dr2 + t3
    dr = _ln_bwd(dh, xhat, rstd, g)
    return (dr, dr), (_colsum(dh * xhat), _colsum(dh))


def exchange(*rides, name):
    return _pcall(lambda: None, [], name=name, grid=(), in_specs=[], out_specs=[], out_shape=[], rides=rides)[1]


HBM_ONLY = pl.BlockSpec(memory_space=pltpu.HBM)
SEM_SPEC = pl.BlockSpec(memory_space=pltpu.SEMAPHORE)
DATAFLOW = pltpu.SideEffectType.DATAFLOW_SIDE_EFFECTING


def scatter_start(g):
    def body(g_ref, land_ref, s0, s1, s2, r0, r1, r2, g_thru, land_thru, token):
        del g_thru, land_thru
        x, y, c = lax.axis_index("x"), lax.axis_index("y"), lax.axis_index("c")
        me = 2 * x + y
        for k, (px, py) in enumerate(_xy_peers(x, y)):
            pltpu.make_async_remote_copy(src_ref=g_ref.at[2 * px + py], dst_ref=land_ref.at[me],
                                         send_sem=(s0, s1, s2)[k], recv_sem=(r0, r1, r2)[k],
                                         device_id=(px, py, c), device_id_type=MESH).start()
        token[...] = jnp.zeros_like(token)

    sem = pltpu.SemaphoreType.DMA(())
    return pl.pallas_call(
        body, name="scatter_start",
        out_shape=(sem,) * 6 + (pltpu.HBM(g.shape, g.dtype), pltpu.HBM(g.shape, g.dtype),
                                jax.ShapeDtypeStruct((1, D_MODEL), f32)),
        in_specs=(HBM_ONLY, HBM_ONLY), out_specs=(SEM_SPEC,) * 6 + (HBM_ONLY, HBM_ONLY, _full((1, D_MODEL))),
        input_output_aliases={0: 6, 1: 7}, compiler_params=pltpu.CompilerParams(has_side_effects=DATAFLOW),
    )(pltpu.with_memory_space_constraint(g, pltpu.HBM),
      pltpu.with_memory_space_constraint(lax.empty(g.shape, g.dtype), pltpu.HBM))


def scatter_wait(sems, g_thru, land_thru, after):
    def body(g_ref, land_ref, s0, s1, s2, r0, r1, r2, after_ref, g_out, land_out):
        del after_ref, g_out, land_out
        x, y, c = lax.axis_index("x"), lax.axis_index("y"), lax.axis_index("c")
        for k, (px, py) in enumerate(_xy_peers(x, y)):
            copy = pltpu.make_async_remote_copy(src_ref=g_ref.at[2 * px + py], dst_ref=land_ref.at[2 * px + py],
                                                send_sem=(s0, s1, s2)[k], recv_sem=(r0, r1, r2)[k],
                                                device_id=(px, py, c), device_id_type=MESH)
            copy.wait_send()
            copy.wait_recv()

    return pl.pallas_call(
        body, name="scatter_wait",
        out_shape=(pltpu.HBM(g_thru.shape, g_thru.dtype), pltpu.HBM(g_thru.shape, g_thru.dtype)),
        in_specs=(HBM_ONLY, HBM_ONLY) + (SEM_SPEC,) * 6 + (pl.BlockSpec(memory_space=pl.ANY),),
        out_specs=(HBM_ONLY, HBM_ONLY), input_output_aliases={0: 0, 1: 1},
        compiler_params=pltpu.CompilerParams(has_side_effects=DATAFLOW),
    )(g_thru, land_thru, *sems, after)


ROW_TILE = 256


def sum_slots(recv, name):
    n, R, C = recv.shape
    tm = min(R, ROW_TILE)

    def body(r_ref, o_ref):
        acc = r_ref[0].astype(f32)
        for k in range(1, n):
            acc = acc + r_ref[k].astype(f32)
        o_ref[...] = acc

    return pl.pallas_call(
        body, name=name, grid=(R // tm,), in_specs=[pl.BlockSpec((n, tm, C), lambda i: (0, i, 0))],
        out_specs=_rows(tm, C), out_shape=jax.ShapeDtypeStruct((R, C), f32), compiler_params=_params())(recv)


def _adamw(w, g, m, v):
    m = ADAM_B1 * m + (1.0 - ADAM_B1) * g
    v = ADAM_B2 * v + (1.0 - ADAM_B2) * (g * g)
    m_hat = m / (1.0 - ADAM_B1 ** ADAM_STEP)
    v_hat = v / (1.0 - ADAM_B2 ** ADAM_STEP)
    delta = -ADAM_LR * (m_hat / (jnp.sqrt(v_hat) + ADAM_EPS) + ADAM_WD * w)
    return delta, m, v


def adam_pair(p, q, w, m, v, name):
    R, C = w.shape
    tm = min(R, ROW_TILE)

    def body(p_ref, q_ref, w_ref, m_ref, v_ref, g_ref, d_ref, nm_ref, nv_ref):
        g = p_ref[...] + q_ref[...]
        g_ref[...] = g
        d_ref[...], nm_ref[...], nv_ref[...] = _adamw(w_ref[...], g, m_ref[...], v_ref[...])

    o = jax.ShapeDtypeStruct((R, C), f32)
    return pl.pallas_call(body, name=name, grid=(R // tm,), in_specs=[_rows(tm, C)] * 5,
                          out_specs=[_rows(tm, C)] * 4, out_shape=[o] * 4, compiler_params=_params())(p, q, w, m, v)


SC_TILES = 32
SC_ROWS = 8


def adam_pair_sc(p, q, w, m, v, name):
    R, C = w.shape
    per_tile = R // SC_TILES

    def body(p_hbm, q_hbm, w_hbm, m_hbm, v_hbm, g_hbm, d_hbm, nm_hbm, nv_hbm, pb, qb, wb, mb, vb, gb, db, nmb, nvb):
        tile = lax.axis_index("sc_tile") * 2 + lax.axis_index("sc_core")

        @pl.loop(0, per_tile, step=SC_ROWS)
        def _(r0):
            rows = pl.ds(tile * per_tile + r0, SC_ROWS)
            for src, buf in ((p_hbm, pb), (q_hbm, qb), (w_hbm, wb), (m_hbm, mb), (v_hbm, vb)):
                pltpu.sync_copy(src.at[rows], buf)

            @pl.loop(0, SC_ROWS)
            def _(r):
                @pl.loop(0, C, step=16)
                def _(i):
                    at = (r, pl.ds(i, 16))
                    g = pb[at] + qb[at]
                    d, nm, nv = _adamw(wb[at], g, mb[at], vb[at])
                    gb[at], db[at], nmb[at], nvb[at] = g, d, nm, nv

            for buf, dst in ((gb, g_hbm), (db, d_hbm), (nmb, nm_hbm), (nvb, nv_hbm)):
                pltpu.sync_copy(buf, dst.at[rows])

    o = jax.ShapeDtypeStruct((R, C), f32)
    return pl.kernel(
        body, name=name, out_type=[o] * 4,
        mesh=plsc.VectorSubcoreMesh(core_axis_name="sc_core", subcore_axis_name="sc_tile"),
        scratch_types=[pltpu.VMEM((SC_ROWS, C), f32)] * 9)(p, q, w, m, v)


def adam_plain(w, g, m, v, name):
    def body(w_ref, g_ref, m_ref, v_ref, d_ref, nm_ref, nv_ref):
        d_ref[...], nm_ref[...], nv_ref[...] = _adamw(w_ref[...], g_ref[...], m_ref[...], v_ref[...])

    o = jax.ShapeDtypeStruct(w.shape, f32)
    return pl.pallas_call(body, name=name, out_shape=[o] * 3, compiler_params=_params())(w, g, m, v)


def adam_slots(recv, w, m, v, name):
    n = recv.shape[0]

    def body(r_ref, w_ref, m_ref, v_ref, g_ref, d_ref, nm_ref, nv_ref):
        g = r_ref[0]
        for k in range(1, n):
            g = g + r_ref[k]
        g_ref[...] = g
        d_ref[...], nm_ref[...], nv_ref[...] = _adamw(w_ref[...], g, m_ref[...], v_ref[...])

    o = jax.ShapeDtypeStruct(w.shape, f32)
    return pl.pallas_call(body, name=name, out_shape=[o] * 4, compiler_params=_params())(recv, w, m, v)


BIG = ("w_in", "w_conv_out", "w_glu", "w_xattn_out", "w_kv", "w_out", "w_up", "w_down")
MID = ("w_conv_out", "w_glu", "w_xattn_out", "w_kv", "w_out")
WIDE_MINOR = ("ssm_b_re", "ssm_b_im", "ssm_c_re", "ssm_c_im")
SMALL = ("b_gate", "ssm_lam_re", "ssm_lam_im", "ssm_log_dt", "ssm_b_re", "ssm_b_im", "ssm_c_re", "ssm_c_im", "ssm_d",
         "ln1_g", "ln1_b", "b_up", "b_down", "ln2_g", "ln2_b")
def _pad_flat(a, mult=1024):
    a = a.reshape(-1)
    return jnp.pad(a, (0, (-a.shape[0]) % mult))


def _perm(a):
    S, W = a.shape
    return a.reshape(N_SEG, S // N_SEG, W).transpose(1, 0, 2).reshape(S, W)


def _unperm(a):
    S, W = a.shape
    return a.reshape(S // N_SEG, N_SEG, W).transpose(1, 0, 2).reshape(S, W)


def _state_rows(a):
    return a.transpose(2, 0, 1).reshape(SSM_GROUP, N_STATES)


def _block_diag_b(bt):
    b4 = bt.reshape(SSM_GROUP, 4, 8, SSM_STATE)
    eye = jnp.eye(8, dtype=bt.dtype)
    return jnp.einsum("hcgp,gk->cghkp", b4, eye).reshape(4, 128, 512)


def _block_diag_c(c):
    c4 = c.reshape(4, 8, SSM_GROUP, SSM_STATE)
    eye = jnp.eye(8, dtype=c.dtype)
    return jnp.einsum("cghp,gk->cgpkh", c4, eye).reshape(4, 512, 128)


def _diag_b(acc):
    a = acc.reshape(4, 8, SSM_STATE, 8, SSM_GROUP)
    eye = jnp.eye(8, dtype=acc.dtype)
    return jnp.einsum("cgpkh,gk->hcgp", a, eye).reshape(SSM_GROUP, N_STATES)


def _diag_c(acc):
    a = acc.reshape(4, 8, SSM_GROUP, 8, SSM_STATE)
    eye = jnp.eye(8, dtype=acc.dtype)
    return jnp.einsum("cghkp,gk->cghp", a, eye).reshape(SSM_GROUPS, SSM_GROUP, SSM_STATE)


def kernel(x, mem, w_in, b_gate, conv_w, w_conv_out, ssm_lam_re, ssm_lam_im, ssm_log_dt, ssm_b_re, ssm_b_im, ssm_c_re, ssm_c_im, ssm_d, w_glu, w_kv, w_xattn_out, w_out, ln1_g, ln1_b, w_up, b_up, w_down, b_down, ln2_g, ln2_b, loss_target, m_w_in, m_b_gate, m_conv_w, m_w_conv_out, m_ssm_lam_re, m_ssm_lam_im, m_ssm_log_dt, m_ssm_b_re, m_ssm_b_im, m_ssm_c_re, m_ssm_c_im, m_ssm_d, m_w_glu, m_w_kv, m_w_xattn_out, m_w_out, m_ln1_g, m_ln1_b, m_w_up, m_b_up, m_w_down, m_b_down, m_ln2_g, m_ln2_b, v_w_in, v_b_gate, v_conv_w, v_w_conv_out, v_ssm_lam_re, v_ssm_lam_im, v_ssm_log_dt, v_ssm_b_re, v_ssm_b_im, v_ssm_c_re, v_ssm_c_im, v_ssm_d, v_w_glu, v_w_kv, v_w_xattn_out, v_w_out, v_ln1_g, v_ln1_b, v_w_up, v_b_up, v_w_down, v_b_down, v_ln2_g, v_ln2_b):
    W = dict(w_in=w_in, b_gate=b_gate, conv_w=conv_w, w_conv_out=w_conv_out, ssm_lam_re=ssm_lam_re,
             ssm_lam_im=ssm_lam_im, ssm_log_dt=ssm_log_dt, ssm_b_re=ssm_b_re, ssm_b_im=ssm_b_im, ssm_c_re=ssm_c_re,
             ssm_c_im=ssm_c_im, ssm_d=ssm_d, w_glu=w_glu, w_kv=w_kv, w_xattn_out=w_xattn_out, w_out=w_out,
             ln1_g=ln1_g, ln1_b=ln1_b, w_up=w_up, b_up=b_up, w_down=w_down, b_down=b_down, ln2_g=ln2_g, ln2_b=ln2_b)
    MOM = dict(w_in=m_w_in, b_gate=m_b_gate, conv_w=m_conv_w, w_conv_out=m_w_conv_out, ssm_lam_re=m_ssm_lam_re,
               ssm_lam_im=m_ssm_lam_im, ssm_log_dt=m_ssm_log_dt, ssm_b_re=m_ssm_b_re, ssm_b_im=m_ssm_b_im,
               ssm_c_re=m_ssm_c_re, ssm_c_im=m_ssm_c_im, ssm_d=m_ssm_d, w_glu=m_w_glu, w_kv=m_w_kv,
               w_xattn_out=m_w_xattn_out, w_out=m_w_out, ln1_g=m_ln1_g, ln1_b=m_ln1_b, w_up=m_w_up, b_up=m_b_up,
               w_down=m_w_down, b_down=m_b_down, ln2_g=m_ln2_g, ln2_b=m_ln2_b)
    VEL = dict(w_in=v_w_in, b_gate=v_b_gate, conv_w=v_conv_w, w_conv_out=v_w_conv_out, ssm_lam_re=v_ssm_lam_re,
               ssm_lam_im=v_ssm_lam_im, ssm_log_dt=v_ssm_log_dt, ssm_b_re=v_ssm_b_re, ssm_b_im=v_ssm_b_im,
               ssm_c_re=v_ssm_c_re, ssm_c_im=v_ssm_c_im, ssm_d=v_ssm_d, w_glu=v_w_glu, w_kv=v_w_kv,
               w_xattn_out=v_w_xattn_out, w_out=v_w_out, ln1_g=v_ln1_g, ln1_b=v_ln1_b, w_up=v_w_up, b_up=v_b_up,
               w_down=v_w_down, b_down=v_b_down, ln2_g=v_ln2_g, ln2_b=v_ln2_b)
    names = list(W)
    xy = 2 * lax.axis_index("x") + lax.axis_index("y")

    xs = x[0]
    S = xs.shape[0]
    mems = mem[0]
    tgt = loss_target[0]

    shard_bf = {n: W[n][0].astype(bf16) for n in BIG}

    lr = ssm_lam_re.reshape(1, N_STATES)
    li = ssm_lam_im.reshape(1, N_STATES)
    ldt = jnp.repeat(ssm_log_dt.reshape(SSM_GROUPS), SSM_STATE).reshape(1, N_STATES)
    bt_re, bt_im = _state_rows(ssm_b_re[0]), _state_rows(ssm_b_im[0])
    ar, ai, kr, ki, bbt_r, bbt_i = ssm_prep(lr, li, ldt, bt_re, bt_im)
    bd_r, bd_i = _block_diag_b(bbt_r).astype(bf16), _block_diag_b(bbt_i).astype(bf16)
    cd_r, cd_i = _block_diag_c(ssm_c_re[0]).astype(bf16), _block_diag_c(ssm_c_im[0]).astype(bf16)
    bdt_r, bdt_i = bd_r.transpose(0, 2, 1), bd_i.transpose(0, 2, 1)
    cdt_r, cdt_i = cd_r.transpose(0, 2, 1), cd_i.transpose(0, 2, 1)
    d_skip = ssm_d.reshape(1, SSM_WIDTH)

    mem_bf = mems.astype(bf16)
    u_col = GATE_COLS + 3 * CONV_WIDTH
    me_xy = jnp.reshape(xy, (1,)).astype(jnp.int32)
    proj, x_bf, (win3,) = proj_own(xs, shard_bf["w_in"], me_xy, 4, rides=(("gather2", shard_bf["w_in"]),))
    proj, (wco3, wglu3, wxo3, wkv4, wout4, convw4) = proj_rest(
        x_bf, win3, me_xy, proj, rides=tuple(("gather", shard_bf[n]) for n in MID) + (("gather", conv_w[0]),))
    wkv3 = wkv4.reshape(1, D_MODEL, 2 * XATTN_WIDTH)
    wout3 = wout4.reshape(1, D_MODEL, D_MODEL)
    convw_full = convw4.transpose(1, 0, 2).reshape(3, CONV_WIDTH)
    a_conv, y_a = conv_fwd(proj, convw_full, wco3)
    u_perm = _perm(proj[:, u_col:u_col + SSM_WIDTH])
    (init_r, init_i), (wup3,) = ssm_scan_fwd(u_perm, bd_r, bd_i, ar, ai, rides=(("gather2", shard_bf["w_up"]),))
    ysm_perm, ys_perm, st_r, st_i = ssm_scan_fwd(u_perm, bd_r, bd_i, ar, ai,
                                                 final=(cd_r, cd_i, d_skip, init_r, init_i))
    y_s = _unperm(ys_perm)
    glu = mm_fwd_small(y_s, wglu3, "glu", out_dtype=bf16)
    kv = mm_fwd_small(mem_bf, wkv3, "kv", out_dtype=bf16)
    o_att, y_c = attn_fwd(proj, kv, wxo3)
    merge_rows = ((proj, D_MODEL, 0), (proj, D_MODEL, 1), (proj, D_MODEL, 2), y_a, (glu, D_MODEL, 0),
                  (glu, D_MODEL, 1), y_c)
    merged, t1, h1, h1_bf = mm_rows_fused(
        (_merge_fwd_pro, merge_rows, (b_gate,)), wout3, "merge_w_out_ln1", transposed=False, tm=256,
        epi=_ln1_fwd_epi, extras=(xs,), consts=(ln1_g, ln1_b), outs=(f32, f32, bf16))
    (r_up, hdn), (wdn4,) = mm_fwd(h1_bf, wup3, "w_up", bias=b_up, outs=(bf16, bf16), second=_relu2,
                                  first=lambda v: jnp.maximum(v, 0.0), rides=(("gather2", shard_bf["w_down"]),))
    wdn3 = wdn4.reshape(1, D_FF, D_MODEL)

    dr2, dr2_bf, loss_cols, d_ln2_g, d_ln2_b, d_b_down = mm_rows_fused(
        hdn, wdn3, "w_down_ln2_loss", transposed=False, tm=512, epi=_ln2_loss_epi, extras=(h1, tgt),
        consts=(b_down, ln2_g, ln2_b), outs=(f32, bf16), nsums=4)
    part, other = {}, {}
    g_w_down = mm_bwd_w(hdn, dr2_bf, 1, "dw_down", tk=1024).reshape(4, -1, D_MODEL)
    dup, d_b_up = mm_bwd_x(dr2_bf, wdn3, "dup", tm=512, extras=(r_up,), colsum=True, out_dtype=bf16,
                           epi=lambda acc, r: acc * (2.0 * r.astype(f32)))
    g_w_up = mm_bwd_w(h1_bf, dup, 4, "dw_up")
    (dr1, dr1_bf, d_ln1_g, d_ln1_b), (recv_dn,) = mm_rows_fused(
        dup, wup3, "dh1_ln1_bwd", transposed=True, tm=512, epi=_ln1_bwd_epi, extras=(xs, t1, dr2), consts=(ln1_g,),
        outs=(f32, bf16), nsums=2, rides=(("scatter", g_w_down),))
    part["w_down"] = sum_slots(recv_dn, "sum_w_down")
    g_w_out = mm_bwd_w(merged, dr1_bf, 1, "dw_out").reshape(4, -1, D_MODEL)
    (dproj, dy_a, dglu, dy_c, d_b_gate), (recv_up,) = mm_rows_fused(
        dr1_bf, wout3, "dmerged_merge_bwd", transposed=True, tm=256, epi=_merge_bwd_epi, extras=merge_rows,
        consts=(b_gate,), outs=((bf16, GATE_COLS, IN_COLS), bf16, (bf16, 2 * D_MODEL, 2 * D_MODEL), bf16),
        sum_widths=(GATE_COLS,), rides=(("scatter", g_w_up),))
    part["w_up"] = sum_slots(recv_up, "sum_w_up")

    dproj, d_conv_w, g_w_co = conv_bwd(dy_a, a_conv, wco3, proj, convw_full, dproj)

    g_w_glu = mm_bwd_w_small(y_s, dglu, 4, "dw_glu")
    dys_perm = _perm(mm_bwd_x(dglu, wglu3, "dy_s", tm=1024))
    linit_r, linit_i = ssm_scan_bwd(dys_perm, ysm_perm, cdt_r, cdt_i, ar, ai)
    (du_perm, dbacc_r, dbacc_i, dcacc_r, dcacc_i, da_r, da_i, d_ssm_d), (recv_co, recv_glu, recv_out) = ssm_scan_bwd(
        dys_perm, ysm_perm, cdt_r, cdt_i, ar, ai,
        final=(u_perm, st_r, st_i, bdt_r, bdt_i, d_skip, linit_r, linit_i),
        rides=(("scatter", g_w_co), ("scatter", g_w_glu), ("scatter", g_w_out)))
    part["w_out"] = sum_slots(recv_out, "sum_w_out")
    part["w_conv_out"] = sum_slots(recv_co, "sum_w_conv_out")
    part["w_glu"] = sum_slots(recv_glu, "sum_w_glu")
    dbt_re, dbt_im, d_lr, d_li, d_ldt_state = ssm_param_bwd(
        _diag_b(dbacc_r), _diag_b(dbacc_i), bt_re, bt_im, kr, ki, ar, ai, lr, li, ldt, da_r, da_i)
    d_log_dt = group_sum(d_ldt_state.reshape(SSM_GROUPS, SSM_STATE))
    d_b_re = dbt_re.reshape(SSM_GROUP, SSM_GROUPS, SSM_STATE).transpose(1, 2, 0)
    d_b_im = dbt_im.reshape(SSM_GROUP, SSM_GROUPS, SSM_STATE).transpose(1, 2, 0)
    d_c_re = _diag_c(dcacc_r)
    d_c_im = -_diag_c(dcacc_i)

    dproj, dkv, g_w_xo = attn_bwd(dy_c, o_att, wxo3, proj, kv, dproj)
    g_w_kv = mm_bwd_w(mem_bf, dkv, 1, "dw_kv").reshape(4, -1, D_MODEL)

    dproj = lax.dynamic_update_slice(dproj, _unperm(du_perm), (0, u_col))
    g_small = {"b_gate": d_b_gate, "ssm_lam_re": d_lr, "ssm_lam_im": d_li, "ssm_log_dt": d_log_dt, "ssm_b_re": d_b_re,
               "ssm_b_im": d_b_im, "ssm_c_re": d_c_re, "ssm_c_im": d_c_im, "ssm_d": d_ssm_d, "ln1_g": d_ln1_g,
               "ln1_b": d_ln1_b, "b_up": d_b_up, "b_down": d_b_down, "ln2_g": d_ln2_g, "ln2_b": d_ln2_b}
    small_names = SMALL + ("conv_w",)
    g_small["conv_w"] = d_conv_w
    sizes = {n: (g_small[n].size + 1023) // 1024 * 1024 for n in small_names}
    pack = lambda d: jnp.concatenate([_pad_flat(d[n]) for n in small_names]).reshape(-1, 128)
    early = ("w_down", "w_up", "w_out", "w_conv_out", "w_glu")
    g_w_in, landed = mm_bwd_w(
        x_bf, dproj, 4, "dw_in", rides=(("scatter", g_w_xo), ("scatter", g_w_kv), ("all", pack(g_small)))
        + tuple(("pair", part[n]) for n in early))
    recv_xo, recv_kv, srecv = landed[:3]
    other.update(zip(early, landed[3:]))
    part["w_xattn_out"] = sum_slots(recv_xo, "sum_w_xattn_out")
    part["w_kv"] = sum_slots(recv_kv, "sum_w_kv")
    *sems, g_thru, land_thru, token = scatter_start(g_w_in)
    res = [{}, {}, {}, {}]
    for n in ("w_down", "w_up"):
        for k, r in enumerate(adam_pair_sc(part[n], other[n], W[n][0], MOM[n][0], VEL[n][0], "adam_sc_" + n)):
            res[k][n] = r[None]
    dx, = mm_rows_fused(dproj, win3, "dx", transposed=True, tm=512, extras=(dr1,), consts=(token,),
                        epi=lambda acc, d, tk: ((acc + ALPHA * d + tk,), ()), outs=(f32,))
    for n in ("w_out", "w_conv_out", "w_glu"):
        for k, r in enumerate(adam_pair(part[n], other[n], W[n][0], MOM[n][0], VEL[n][0], "adam_" + n)):
            res[k][n] = r[None]
    g_own, landed_in = scatter_wait(sems, g_thru, land_thru, dx)
    recv_in = lax.dynamic_update_slice(landed_in, lax.dynamic_index_in_dim(g_own, xy, 0, keepdims=True), (xy, 0, 0))
    late = ("w_in", "w_xattn_out", "w_kv")
    part["w_in"] = sum_slots(recv_in, "sum_w_in")
    other.update(zip(late, exchange(*[("pair", part[n]) for n in late], name="swap_late")))
    for n in late:
        for k, r in enumerate(adam_pair(part[n], other[n], W[n][0], MOM[n][0], VEL[n][0], "adam_" + n)):
            res[k][n] = r[None]
    conv_zero = jnp.zeros((3, CONV_WIDTH), f32)
    in_pack = lambda d: pack({**{n: (jnp.zeros(W[n].shape, f32) if n in WIDE_MINOR else d[n]) for n in SMALL},
                              "conv_w": conv_zero})
    gs, ds_, ms, vs = adam_slots(srecv, in_pack(W), in_pack(MOM), in_pack(VEL), "adam_small")

    def unpack_small(buf, skip=()):
        flat = buf.reshape(-1)
        out, r = {}, 0
        for n in small_names:
            ref = g_small[n] if n == "conv_w" else W[n]
            if n not in skip:
                out[n] = flat[r:r + ref.size].reshape(ref.shape)
            r += sizes[n]
        return out

    res_s = [unpack_small(gs)] + [unpack_small(b, WIDE_MINOR) for b in (ds_, ms, vs)]
    for n in WIDE_MINOR:
        two_d = (-1, W[n].shape[-1])
        outs_n = adam_plain(W[n].reshape(two_d), res_s[0][n].reshape(two_d), MOM[n].reshape(two_d),
                            VEL[n].reshape(two_d), "adam_" + n)
        for k, r in enumerate(outs_n):
            res_s[k + 1][n] = r.reshape(W[n].shape)
    g_conv = lax.dynamic_slice(res_s[0]["conv_w"], (0, xy * 128), (3, 128))
    conv_slots = g_conv.reshape(1, 3, 128)
    cg, cd, cm, cv = adam_slots(conv_slots, conv_w[0], m_conv_w[0], v_conv_w[0], "adam_conv")
    conv_res = [cg, cd, cm, cv]

    loss = lax.psum(jnp.sum(loss_cols), ("x", "y", "c"))
    outs = [loss, dx.reshape(x.shape)]
    for k in range(4):
        for n in names:
            if n == "conv_w":
                outs.append(conv_res[k].reshape(conv_w.shape))
            elif n in BIG:
                outs.append(res[k][n])
            else:
                outs.append(res_s[k][n])
    return tuple(outs)
```

```python
import functools
import math

import jax
import jax.numpy as jnp
from jax import lax
from jax.experimental import pallas as pl
from jax.experimental.pallas import tpu as pltpu
from jax.experimental.pallas import tpu_sc as plsc

f32 = jnp.float32
bf16 = jnp.bfloat16

D_MODEL = 1024
CONV_WIDTH = 512
SSM_WIDTH = 512
SSM_GROUP = 16
SSM_GROUPS = 32
SSM_STATE = 64
N_STATES = SSM_GROUPS * SSM_STATE
XATTN_HEADS = 4
XATTN_HEAD_DIM = 128
XATTN_WIDTH = 512
D_FF = 4096
GATE_COLS = 3 * D_MODEL
IN_COLS = GATE_COLS + 3 * CONV_WIDTH + SSM_WIDTH + XATTN_WIDTH
ALPHA = 2.0 ** 0.25
LN_EPS = 1e-5
ADAM_LR = 0.001
ADAM_B1 = 0.9
ADAM_B2 = 0.999
ADAM_EPS = 1e-08
ADAM_WD = 0.01
ADAM_STEP = 10

N_SEG = 8
SCAN_ROWS = 512
LANE_STRIP = 512
TM = 512
VMEM_LIMIT = 48 * 1024 * 1024
MESH = pl.DeviceIdType.MESH

NT_DIMS = (((1,), (1,)), ((), ()))
TN_DIMS = (((0,), (0,)), ((), ()))


def _params():
    return pltpu.CompilerParams(vmem_limit_bytes=VMEM_LIMIT)


def _full(shape):
    n = len(shape)
    return pl.BlockSpec(shape, lambda *_: (0,) * n)


def _rows(tm, w, cb=0):
    return pl.BlockSpec((tm, w), lambda i: (i, cb))


def _sig(x):
    return 1.0 / (1.0 + jnp.exp(-x))


HBM_SPEC = pl.BlockSpec(memory_space=pl.ANY)
RIDE_PEERS = {"gather": 3, "gather2": 6, "scatter": 3, "all": 7, "pair": 1}


def _xy_peers(x, y):
    return [(1 - x, y), (x, 1 - y), (1 - x, 1 - y)]


def _ride_copies(kind, src, dst, send_sems, recv_sems, local_sem):
    x, y, c = lax.axis_index("x"), lax.axis_index("y"), lax.axis_index("c")
    me = 2 * x + y
    if kind == "all":
        flips = [(fx, fy, fc) for fx in (0, 1) for fy in (0, 1) for fc in (0, 1)][1:]
        peers = [(x ^ fx, y ^ fy, c ^ fc) for fx, fy, fc in flips]
        slot = lambda p: 4 * p[0] + 2 * p[1] + p[2]
        mine = slot((x, y, c))
    else:
        peers = [(px, py, c) for px, py in _xy_peers(x, y)]
        slot = lambda p: 2 * p[0] + p[1]
        mine = me

    def remote(k, s, d):
        return pltpu.make_async_remote_copy(src_ref=s, dst_ref=d, send_sem=send_sems.at[k], recv_sem=recv_sems.at[k],
                                            device_id=peers[k], device_id_type=MESH)

    if kind == "pair":
        peers = [(x, y, 1 - c)]
        return None, [remote(0, src, dst)], [remote(0, src, dst)]
    if kind == "scatter":
        local = pltpu.make_async_copy(src.at[me], dst.at[me], local_sem)
        sends = [remote(k, src.at[slot(p)], dst.at[me]) for k, p in enumerate(peers)]
        lands = [remote(k, src.at[me], dst.at[slot(p)]) for k, p in enumerate(peers)]
    else:
        local = pltpu.make_async_copy(src, dst.at[mine], local_sem)
        sends = [remote(k, src, dst.at[mine]) for k, p in enumerate(peers)]
        lands = [remote(k, src, dst.at[slot(p)]) for k, p in enumerate(peers)]
    return local, sends, lands


def _ride_shape(kind, src):
    lead = {"gather": (4,), "gather2": (4,), "scatter": (), "all": (8,), "pair": ()}[kind]
    return jax.ShapeDtypeStruct(lead + src.shape, src.dtype)


def _two_level(src, dst, send_sems, recv_sems, local_sem):
    x, y, c = lax.axis_index("x"), lax.axis_index("y"), lax.axis_index("c")
    me = 2 * x + y
    h = src.shape[0] // 2
    mine = pl.ds(pl.multiple_of(c * h, 16), h)
    theirs = pl.ds(pl.multiple_of((1 - c) * h, 16), h)
    peers = _xy_peers(x, y)
    slots = [2 * px + py for px, py in peers]

    def over_ici(k, d):
        return pltpu.make_async_remote_copy(src_ref=src.at[mine], dst_ref=d, send_sem=send_sems.at[k],
                                            recv_sem=recv_sems.at[k], device_id=(*peers[k], c), device_id_type=MESH)

    def over_d2d(k, rows):
        blk = dst.at[slots[k], rows]
        return pltpu.make_async_remote_copy(src_ref=blk, dst_ref=blk, send_sem=send_sems.at[3 + k],
                                            recv_sem=recv_sems.at[3 + k], device_id=(x, y, 1 - c), device_id_type=MESH)

    return (pltpu.make_async_copy(src, dst.at[me], local_sem),
            [over_ici(k, dst.at[me, mine]) for k in range(3)], [over_ici(k, dst.at[slots[k], mine]) for k in range(3)],
            [over_d2d(k, mine) for k in range(3)], [over_d2d(k, theirs) for k in range(3)])


def _pcall(body, args, *, name, grid, in_specs, out_specs, out_shape, scratch_shapes=(), rides=(), aliases=None,
           prefetch=()):
    n_in, n_out, n_scr, nr, npf = len(in_specs), len(out_specs), len(scratch_shapes), len(rides), len(prefetch)
    kinds = [k for k, _ in rides]

    def wrapped(*refs):
        pre, refs = refs[:npf], refs[npf:]
        ins, rsrc = refs[:n_in], refs[n_in:n_in + nr]
        outs = refs[n_in + nr:n_in + nr + n_out]
        rdst = refs[n_in + nr + n_out:n_in + 2 * nr + n_out]
        scr = refs[n_in + 2 * nr + n_out:n_in + 2 * nr + n_out + n_scr]
        sems = refs[n_in + 2 * nr + n_out + n_scr:]

        def start():
            for r, kind in enumerate(kinds):
                if kind == "gather2":
                    local, sends, _, _, _ = _two_level(rsrc[r], rdst[r], *sems[3 * r:3 * r + 3])
                else:
                    local, sends, _ = _ride_copies(kind, rsrc[r], rdst[r], *sems[3 * r:3 * r + 3])
                if local is not None:
                    local.start()
                for cp in sends:
                    cp.start()

        def finish():
            for r, kind in enumerate(kinds):
                if kind == "gather2":
                    local, sends, ici_lands, forwards, lands = _two_level(rsrc[r], rdst[r], *sems[3 * r:3 * r + 3])
                    for k in range(3):
                        ici_lands[k].wait_recv()
                        forwards[k].start()
                    sends = sends + forwards
                else:
                    local, sends, lands = _ride_copies(kind, rsrc[r], rdst[r], *sems[3 * r:3 * r + 3])
                for cp in lands:
                    cp.wait_recv()
                for cp in sends:
                    cp.wait_send()
                if local is not None:
                    local.wait()

        if nr and grid:
            ids = [pl.program_id(a) for a in range(len(grid))]
            first = functools.reduce(jnp.logical_and, [i == 0 for i in ids])
            last = functools.reduce(jnp.logical_and, [i == g - 1 for i, g in zip(ids, grid)])
            pl.when(first)(start)
        elif nr:
            start()
        body(*pre, *ins, *outs, *scr)
        if nr and grid:
            pl.when(last)(finish)
        elif nr:
            finish()

    sems = []
    for kind in kinds:
        n = RIDE_PEERS[kind]
        sems += [pltpu.SemaphoreType.DMA((n,)), pltpu.SemaphoreType.DMA((n,)), pltpu.SemaphoreType.DMA(())]
    all_in = list(in_specs) + [HBM_SPEC] * nr
    all_out = list(out_specs) + [HBM_SPEC] * nr
    all_scratch = list(scratch_shapes) + sems
    if npf:
        how = dict(grid_spec=pltpu.PrefetchScalarGridSpec(num_scalar_prefetch=npf, grid=grid, in_specs=all_in,
                                                          out_specs=all_out, scratch_shapes=all_scratch))
    else:
        how = dict(grid=grid, in_specs=all_in, out_specs=all_out, scratch_shapes=all_scratch)
    res = pl.pallas_call(
        wrapped, name=name, out_shape=list(out_shape) + [_ride_shape(k, s) for k, s in rides],
        input_output_aliases=aliases or {}, compiler_params=_params(), **how)(
            *prefetch, *args, *[s for _, s in rides])
    return list(res[:n_out]), list(res[n_out:])


def _bf(v):
    return v if v.dtype == bf16 else v.astype(bf16)


def mm_fwd_small(a, w3, name, *, tm=1024, out_dtype=f32):
    M, K = a.shape
    J, _, n = w3.shape
    tm = min(tm, M)

    def body(a_ref, w_ref, o_ref):
        av = _bf(a_ref[...])
        for j in range(J):
            o_ref[:, j * n:(j + 1) * n] = jnp.dot(av, w_ref[j], preferred_element_type=f32).astype(out_dtype)

    return pl.pallas_call(
        body, name=name, grid=(M // tm,), in_specs=[_rows(tm, K), _full((J, K, n))], out_specs=_rows(tm, J * n),
        out_shape=jax.ShapeDtypeStruct((M, J * n), out_dtype), compiler_params=_params())(a, w3)


def mm_bwd_w_small(a, dy, J, name, *, tm=2048):
    M, K = a.shape
    n = dy.shape[1] // J
    tm = min(tm, M)
    ns = M // tm

    def body(a_ref, dy_ref, o_ref, acc_ref):
        s = pl.program_id(0)
        part = lax.dot_general(_bf(a_ref[...]), _bf(dy_ref[...]), TN_DIMS, preferred_element_type=f32)

        @pl.when(s == 0)
        def _():
            acc_ref[...] = part

        @pl.when(s > 0)
        def _():
            acc_ref[...] += part

        @pl.when(s == ns - 1)
        def _():
            for j in range(J):
                o_ref[j] = acc_ref[:, j * n:(j + 1) * n].astype(bf16)

    return pl.pallas_call(
        body, name=name, grid=(ns,), in_specs=[_rows(tm, K), _rows(tm, J * n)], out_specs=_full((J, K, n)),
        out_shape=jax.ShapeDtypeStruct((J, K, n), bf16), scratch_shapes=[pltpu.VMEM((K, J * n), f32)],
        compiler_params=_params())(a, dy)


def mm_fwd(a, w3, name, *, tm=1024, tn=None, bias=None, outs=(f32,), first=None, second=None, rides=()):
    M, K = a.shape
    J, _, n = w3.shape
    tm = min(tm, M)
    tn = tn or n
    nl = n // tn
    nb = 0 if bias is None else 1

    def body(*refs):
        a_ref, w_ref = refs[0], refs[1]
        acc = jnp.dot(_bf(a_ref[...]), w_ref[0], preferred_element_type=f32)
        if bias is not None:
            acc = acc + refs[2][...]
        refs[2 + nb][...] = (acc if first is None else first(acc)).astype(outs[0])
        if len(outs) > 1:
            refs[3 + nb][...] = second(acc).astype(outs[1])

    in_specs = [pl.BlockSpec((tm, K), lambda j, l, i: (i, 0)),
                pl.BlockSpec((1, K, tn), lambda j, l, i: (j, 0, l))]
    args = [a, w3]
    if bias is not None:
        in_specs.append(pl.BlockSpec((1, tn), lambda j, l, i: (0, j * nl + l)))
        args.append(bias)
    res, landed = _pcall(
        body, args, name=name, grid=(J, nl, M // tm), in_specs=in_specs,
        out_specs=[pl.BlockSpec((tm, tn), lambda j, l, i: (i, j * nl + l)) for _ in outs],
        out_shape=[jax.ShapeDtypeStruct((M, J * n), dt) for dt in outs], rides=rides)
    res = res if len(outs) > 1 else res[0]
    return (res, landed) if rides else res


def proj_own(x, w_own, me, n_blocks, *, tm=1024, rides=()):
    M, K = x.shape
    n = w_own.shape[1]

    def body(me_ref, x_ref, w_ref, o_ref, xb_ref):
        av = _bf(x_ref[...])
        xb_ref[...] = av
        o_ref[...] = jnp.dot(av, w_ref[...], preferred_element_type=f32).astype(bf16)

    (proj, x_bf), landed = _pcall(
        body, [x, w_own], name="proj_own", grid=(M // tm,), prefetch=(me,),
        in_specs=[pl.BlockSpec((tm, K), lambda i, me_ref: (i, 0)), pl.BlockSpec((K, n), lambda i, me_ref: (0, 0))],
        out_specs=[pl.BlockSpec((tm, n), lambda i, me_ref: (i, me_ref[0])),
                   pl.BlockSpec((tm, K), lambda i, me_ref: (i, 0))],
        out_shape=[jax.ShapeDtypeStruct((M, n_blocks * n), bf16), jax.ShapeDtypeStruct((M, K), bf16)], rides=rides)
    return proj, x_bf, landed


def proj_rest(x_bf, w3, me, proj, *, tm=1024, rides=()):
    M, K = x_bf.shape
    J, _, n = w3.shape

    def body(me_ref, x_ref, w_ref, _, o_ref):
        o_ref[...] = jnp.dot(x_ref[...], w_ref[0], preferred_element_type=f32).astype(bf16)

    (proj,), landed = _pcall(
        body, [x_bf, w3, proj], name="proj_rest", grid=(J - 1, M // tm), prefetch=(me,),
        in_specs=[pl.BlockSpec((tm, K), lambda k, i, me_ref: (i, 0)),
                  pl.BlockSpec((1, K, n), lambda k, i, me_ref: (me_ref[0] ^ (k + 1), 0, 0)), HBM_SPEC],
        out_specs=[pl.BlockSpec((tm, n), lambda k, i, me_ref: (i, me_ref[0] ^ (k + 1)))],
        out_shape=[jax.ShapeDtypeStruct(proj.shape, bf16)], aliases={3: 0}, rides=rides)
    return proj, landed


def mm_bwd_x(dy, w3, name, *, tm=TM, epi=None, extras=(), colsum=False, out_dtype=f32, rides=()):
    M = dy.shape[0]
    J, K, n = w3.shape
    tm = min(tm, M)
    nex = len(extras)

    def body(*refs):
        dy_ref, w_hbm = refs[0], refs[1]
        ex = refs[2:2 + nex]
        o_ref = refs[2 + nex]
        w_ref = refs[-1]
        i = pl.program_id(0)

        @pl.when(i == 0)
        def _():
            pltpu.sync_copy(w_hbm, w_ref)

        acc = None
        for j in range(J):
            part = lax.dot_general(_bf(dy_ref[:, j * n:(j + 1) * n]), w_ref[j], NT_DIMS, preferred_element_type=f32)
            acc = part if acc is None else acc + part
        if epi is not None:
            acc = epi(acc, *[e[...] for e in ex])
        o_ref[...] = acc.astype(out_dtype)
        if colsum:
            s_ref = refs[3 + nex]
            cs = jnp.sum(acc, axis=0, keepdims=True)

            @pl.when(i == 0)
            def _():
                s_ref[...] = cs

            @pl.when(i > 0)
            def _():
                s_ref[...] += cs

    in_specs = [pl.BlockSpec((tm, J * n), lambda i: (i, 0)), pl.BlockSpec(memory_space=pl.ANY)]
    in_specs += [pl.BlockSpec((tm, K), lambda i: (i, 0)) for _ in extras]
    out_specs = [pl.BlockSpec((tm, K), lambda i: (i, 0))]
    out_shape = [jax.ShapeDtypeStruct((M, K), out_dtype)]
    if colsum:
        out_specs.append(pl.BlockSpec((1, K), lambda i: (0, 0)))
        out_shape.append(jax.ShapeDtypeStruct((1, K), f32))
    res, landed = _pcall(body, [dy, w3, *extras], name=name, grid=(M // tm,), in_specs=in_specs, out_specs=out_specs,
                         out_shape=out_shape, scratch_shapes=[pltpu.VMEM((J, K, n), bf16)], rides=rides)
    res = res if colsum else res[0]
    return (res, landed) if rides else res


def mm_rows_fused(a, w3, name, *, transposed, tm, epi, extras=(), consts=(), outs=(f32,), nsums=0, sum_widths=None,
                  rides=()):
    pro = a[0] if isinstance(a, tuple) else None
    J, K, n = w3.shape
    W = K if transposed else n
    ka = J * n if transposed else K
    if pro is not None:
        a_rows = [e if isinstance(e, tuple) else (e, ka, 0) for e in a[1]]
        a_consts = list(a[2])
    else:
        a_rows, a_consts = [(a, ka, 0)], []
    M = a_rows[0][0].shape[0]
    tm = min(tm, M)
    na = len(a_rows) + len(a_consts)
    nex, nco = len(extras), len(consts)
    if pro is not None:
        outs = (bf16 if ka == W else (bf16, ka, ka),) + tuple(outs)
    nout = len(outs)
    nsums = len(sum_widths) if sum_widths is not None else nsums
    out_dtypes = [o[0] if isinstance(o, tuple) else o for o in outs]

    def body(*refs):
        a_refs, w_hbm = refs[:na], refs[na]
        ex = refs[na + 1:na + 1 + nex]
        co = refs[na + 1 + nex:na + 1 + nex + nco]
        o_refs = refs[na + 1 + nex + nco:na + 1 + nex + nco + nout]
        s_refs = refs[na + 1 + nex + nco + nout:na + 1 + nex + nco + nout + nsums]
        w_ref = refs[-1]
        i = pl.program_id(0)

        @pl.when(i == 0)
        def _():
            pltpu.sync_copy(w_hbm, w_ref)

        av = _bf(a_refs[0][...] if pro is None else pro(*[r[...] for r in a_refs]))
        if transposed:
            acc = None
            for j in range(J):
                part = lax.dot_general(av[:, j * n:(j + 1) * n], w_ref[j], NT_DIMS, preferred_element_type=f32)
                acc = part if acc is None else acc + part
        else:
            acc = jnp.dot(av, w_ref[0], preferred_element_type=f32)
        rows, sums = epi(acc, *[e[...] for e in ex], *[c[...] for c in co])
        if pro is not None:
            rows = (av,) + tuple(rows)
        for o_ref, v, dt in zip(o_refs, rows, out_dtypes):
            o_ref[...] = v.astype(dt)
        for s_ref, v in zip(s_refs, sums):
            @pl.when(i == 0)
            def _(s_ref=s_ref, v=v):
                s_ref[...] = v

            @pl.when(i > 0)
            def _(s_ref=s_ref, v=v):
                s_ref[...] += v

    extras = [e if isinstance(e, tuple) else (e, W, 0) for e in extras]
    outs_full = [o if isinstance(o, tuple) else (o, W, W) for o in outs]
    widths = list(sum_widths) if sum_widths is not None else [W] * nsums
    in_specs = [_rows(tm, w, cb) for _, w, cb in a_rows] + [_full(c.shape) for c in a_consts]
    in_specs += [pl.BlockSpec(memory_space=pl.ANY)]
    in_specs += [_rows(tm, w, cb) for _, w, cb in extras] + [_full(c.shape) for c in consts]
    out_specs = [_rows(tm, w) for _, w, _ in outs_full] + [_full((1, w)) for w in widths]
    out_shape = ([jax.ShapeDtypeStruct((M, cols), dt) for dt, _, cols in outs_full]
                 + [jax.ShapeDtypeStruct((1, w), f32) for w in widths])
    args = [e for e, _, _ in a_rows] + a_consts + [w3] + [e for e, _, _ in extras] + list(consts)
    res, landed = _pcall(body, args, name=name, grid=(M // tm,), in_specs=in_specs, out_specs=out_specs,
                         out_shape=out_shape, scratch_shapes=[pltpu.VMEM((J, K, n), bf16)], rides=rides)
    return (res, landed) if rides else res


def mm_bwd_w(a, dy, J, name, *, tm=2048, tn=None, tk=None, rides=()):
    M, K = a.shape
    n = dy.shape[1] // J
    tm = min(tm, M)
    tn = tn or n
    tk = tk or K
    nl = n // tn
    nk = K // tk
    ns = M // tm

    def body(a_ref, dy_ref, o_ref, acc_ref):
        s = pl.program_id(3)
        part = lax.dot_general(_bf(a_ref[...]), _bf(dy_ref[...]), TN_DIMS, preferred_element_type=f32)

        @pl.when(s == 0)
        def _():
            acc_ref[...] = part

        @pl.when(s > 0)
        def _():
            acc_ref[...] += part

        @pl.when(s == ns - 1)
        def _():
            o_ref[0] = acc_ref[...].astype(bf16)

    res, landed = _pcall(
        body, [a, dy], name=name, grid=(J, nl, nk, ns),
        in_specs=[pl.BlockSpec((tm, tk), lambda j, l, k, s: (s, k)),
                  pl.BlockSpec((tm, tn), lambda j, l, k, s: (s, j * nl + l))],
        out_specs=[pl.BlockSpec((1, tk, tn), lambda j, l, k, s: (j, k, l))],
        out_shape=[jax.ShapeDtypeStruct((J, K, n), bf16)],
        scratch_shapes=[pltpu.VMEM((tk, tn), f32)], rides=rides)
    return (res[0], landed) if rides else res[0]


def _relu2(v):
    r = jnp.maximum(v, 0.0)
    return r * r


HALO = 16


def _shift_down(z, k, halo):
    r = lax.broadcasted_iota(jnp.int32, z.shape, 0)
    y = pltpu.roll(z, k, 0)
    for q in range(k):
        y = jnp.where(r == q, halo[HALO - k + q:HALO - k + q + 1, :], y)
    return y


def _shift_up(z, k, halo):
    tm = z.shape[0]
    r = lax.broadcasted_iota(jnp.int32, z.shape, 0)
    y = pltpu.roll(z, tm - k, 0)
    for q in range(k):
        y = jnp.where(r == tm - k + q, halo[q:q + 1, :], y)
    return y


def _prev_halo(tm, cb):
    return pl.BlockSpec((HALO, CONV_WIDTH), lambda i: (jnp.maximum(i * (tm // HALO) - 1, 0), cb))


def _next_halo(tm, cb, nblk):
    return pl.BlockSpec((HALO, CONV_WIDTH), lambda i: (jnp.minimum((i + 1) * (tm // HALO), nblk - 1), cb))


def _f32(ref):
    return ref[...].astype(f32)


def conv_fwd(proj, conv_w, wco3, tm=TM):
    S = proj.shape[0]
    J, _, n = wco3.shape

    def body(cb_ref, cc_ref, ch_ref, cch_ref, chh_ref, w_ref, wo_ref, a_ref, y_ref):
        i = pl.program_id(0)
        z = _f32(cc_ref) * _f32(ch_ref)
        zh = jnp.where(i == 0, 0.0, _f32(cch_ref) * _f32(chh_ref))
        w = w_ref[...]
        dwz = w[0:1, :] * _shift_down(z, 2, zh) + w[1:2, :] * _shift_down(z, 1, zh) + w[2:3, :] * z
        a = (_f32(cb_ref) * dwz).astype(bf16)
        a_ref[...] = a
        for j in range(J):
            y_ref[:, j * n:(j + 1) * n] = jnp.dot(a, wo_ref[j], preferred_element_type=f32).astype(bf16)

    return pl.pallas_call(
        body, name="conv_fwd", grid=(S // tm,),
        in_specs=[_rows(tm, CONV_WIDTH, 6), _rows(tm, CONV_WIDTH, 7), _rows(tm, CONV_WIDTH, 8),
                  _prev_halo(tm, 7), _prev_halo(tm, 8), _full((3, CONV_WIDTH)), _full(wco3.shape)],
        out_specs=[_rows(tm, CONV_WIDTH), _rows(tm, J * n)],
        out_shape=[jax.ShapeDtypeStruct((S, CONV_WIDTH), bf16), jax.ShapeDtypeStruct((S, J * n), bf16)],
        compiler_params=_params())(proj, proj, proj, proj, proj, conv_w, wco3)


def conv_bwd(dy_a, a_conv, wco3, proj, conv_w, dproj, tm=TM):
    S = proj.shape[0]
    nt = S // tm
    J, _, n = wco3.shape

    def body(dy_ref, dyn_ref, a_ref, wo_ref, cb_ref, cc_ref, ch_ref, cch_ref, chh_ref, cbn_ref, w_ref, _, o_ref,
             dw_ref, dwo_ref, acc_ref):
        i = pl.program_id(0)

        def times_wo_t(dy):
            out = None
            for j in range(J):
                part = lax.dot_general(dy[:, j * n:(j + 1) * n], wo_ref[j], NT_DIMS, preferred_element_type=f32)
                out = part if out is None else out + part
            return out

        dyv = dy_ref[...]
        dav, dan = times_wo_t(dyv), times_wo_t(dyn_ref[...])
        part = lax.dot_general(a_ref[...], dyv, TN_DIMS, preferred_element_type=f32)

        @pl.when(i == 0)
        def _():
            acc_ref[...] = part

        @pl.when(i > 0)
        def _():
            acc_ref[...] += part

        @pl.when(i == nt - 1)
        def _():
            for j in range(J):
                dwo_ref[j] = acc_ref[:, j * n:(j + 1) * n].astype(bf16)

        cc, ch, cb = _f32(cc_ref), _f32(ch_ref), _f32(cb_ref)
        z = cc * ch
        zh = jnp.where(i == 0, 0.0, _f32(cch_ref) * _f32(chh_ref))
        w = w_ref[...]
        z1 = _shift_down(z, 1, zh)
        z2 = _shift_down(z, 2, zh)
        dwz = w[0:1, :] * z2 + w[1:2, :] * z1 + w[2:3, :] * z
        g = dav * cb
        gn = jnp.where(i == nt - 1, 0.0, dan * _f32(cbn_ref))
        dz = w[2:3, :] * g + w[1:2, :] * _shift_up(g, 1, gn) + w[0:1, :] * _shift_up(g, 2, gn)
        o_ref[:, 0:CONV_WIDTH] = (dav * dwz).astype(bf16)
        o_ref[:, CONV_WIDTH:2 * CONV_WIDTH] = (dz * ch).astype(bf16)
        o_ref[:, 2 * CONV_WIDTH:3 * CONV_WIDTH] = (dz * cc).astype(bf16)

        @pl.when(i == 0)
        def _():
            dw_ref[...] = jnp.zeros_like(dw_ref)

        dw_ref[0:1, :] += jnp.sum(g * z2, axis=0, keepdims=True)
        dw_ref[1:2, :] += jnp.sum(g * z1, axis=0, keepdims=True)
        dw_ref[2:3, :] += jnp.sum(g * z, axis=0, keepdims=True)

    return pl.pallas_call(
        body, name="conv_bwd", grid=(nt,),
        in_specs=[_rows(tm, J * n),
                  pl.BlockSpec((HALO, J * n), lambda i: (jnp.minimum((i + 1) * (tm // HALO), S // HALO - 1), 0)),
                  _rows(tm, CONV_WIDTH), _full(wco3.shape),
                  _rows(tm, CONV_WIDTH, 6), _rows(tm, CONV_WIDTH, 7), _rows(tm, CONV_WIDTH, 8),
                  _prev_halo(tm, 7), _prev_halo(tm, 8), _next_halo(tm, 6, S // HALO), _full((3, CONV_WIDTH)),
                  pl.BlockSpec(memory_space=pl.ANY)],
        out_specs=[_rows(tm, 3 * CONV_WIDTH, GATE_COLS // (3 * CONV_WIDTH)), _full((3, CONV_WIDTH)),
                   _full(wco3.shape)],
        out_shape=[jax.ShapeDtypeStruct(dproj.shape, bf16), jax.ShapeDtypeStruct((3, CONV_WIDTH), f32),
                   jax.ShapeDtypeStruct(wco3.shape, bf16)],
        scratch_shapes=[pltpu.VMEM((CONV_WIDTH, J * n), f32)], input_output_aliases={11: 0},
        compiler_params=_params())(dy_a, dy_a, a_conv, wco3, proj, proj, proj, proj, proj, proj, conv_w, dproj)


def _cmul(ar, ai, br, bi):
    return ar * br - ai * bi, ar * bi + ai * br


def ssm_prep(lr, li, ldt, bt_re, bt_im):
    def body(lr_ref, li_ref, ldt_ref, br_ref, bi_ref, ar_ref, ai_ref, kr_ref, ki_ref, bbr_ref, bbi_ref):
        lrv, liv = lr_ref[...], li_ref[...]
        dt = jnp.exp(ldt_ref[...])
        mag = jnp.exp(lrv * dt)
        ar = mag * jnp.cos(liv * dt)
        ai = mag * jnp.sin(liv * dt)
        den = lrv * lrv + liv * liv
        nr = ar - 1.0
        kr = (nr * lrv + ai * liv) / den
        ki = (ai * lrv - nr * liv) / den
        ar_ref[...], ai_ref[...], kr_ref[...], ki_ref[...] = ar, ai, kr, ki
        bbr_ref[...] = kr * br_ref[...] - ki * bi_ref[...]
        bbi_ref[...] = kr * bi_ref[...] + ki * br_ref[...]

    v = jax.ShapeDtypeStruct((1, N_STATES), f32)
    m = jax.ShapeDtypeStruct((SSM_GROUP, N_STATES), f32)
    return pl.pallas_call(body, name="ssm_prep", out_shape=[v, v, v, v, m, m])(lr, li, ldt, bt_re, bt_im)


def _pow_segment(ar, ai, seg_len):
    pr, pi = ar, ai
    for _ in range(int(math.log2(seg_len))):
        pr, pi = _cmul(pr, pi, pr, pi)
    return pr, pi


GELU_K = math.sqrt(2.0 / math.pi)
GELU_C = 0.044715


def _gelu(v):
    return 0.5 * v * (1.0 + jnp.tanh(GELU_K * (v + GELU_C * v * v * v)))


def _gelu_grad(v):
    t = jnp.tanh(GELU_K * (v + GELU_C * v * v * v))
    return 0.5 * (1.0 + t) + 0.5 * v * (1.0 - t * t) * GELU_K * (1.0 + 3.0 * GELU_C * v * v)


def ssm_scan_fwd(u_perm, bd_r, bd_i, ar, ai, *, final=None, rides=()):
    S = u_perm.shape[0]
    R = SCAN_ROWS
    nblk = S // R
    seg_len = S // N_SEG
    nstrip = N_STATES // LANE_STRIP
    store = final is not None

    def body(*refs):
        if store:
            (u_ref, bdr_ref, bdi_ref, ar_ref, ai_ref, cdr_ref, cdi_ref, d_ref, ir_ref, ii_ref,
             y_ref, ys_ref, sr_ref, si_ref, bur, bui, car, cai) = refs
        else:
            u_ref, bdr_ref, bdi_ref, ar_ref, ai_ref, or_ref, oi_ref, bur, bui, car, cai = refs
        i = pl.program_id(0)
        ub = _bf(u_ref[...])
        u = ub.astype(f32)
        for cb in range(4):
            us = ub[:, cb * 128:(cb + 1) * 128]
            bur[:, cb * 512:(cb + 1) * 512] = jnp.dot(us, bdr_ref[cb], preferred_element_type=f32)
            bui[:, cb * 512:(cb + 1) * 512] = jnp.dot(us, bdi_ref[cb], preferred_element_type=f32)

        @pl.when(i == 0)
        def _():
            if store:
                car[...] = ir_ref[...]
                cai[...] = ii_ref[...]
            else:
                car[...] = jnp.zeros_like(car)
                cai[...] = jnp.zeros_like(cai)

        for ls in range(nstrip):
            lanes = pl.ds(ls * LANE_STRIP, LANE_STRIP)
            a_r = jnp.broadcast_to(ar_ref[:, lanes], (N_SEG, LANE_STRIP))
            a_i = jnp.broadcast_to(ai_ref[:, lanes], (N_SEG, LANE_STRIP))

            def step(t, carry, lanes=lanes, a_r=a_r, a_i=a_i):
                s_r, s_i = carry
                row = pl.multiple_of(t * 8, 8)
                n_r = a_r * s_r - a_i * s_i + bur[pl.ds(row, 8), lanes]
                n_i = a_r * s_i + a_i * s_r + bui[pl.ds(row, 8), lanes]
                if store:
                    sr_ref[pl.ds(row, 8), lanes] = n_r
                    si_ref[pl.ds(row, 8), lanes] = n_i
                return n_r, n_i

            e_r, e_i = lax.fori_loop(0, R // 8, step, (car[:, lanes], cai[:, lanes]), unroll=True)
            car[:, lanes] = e_r
            cai[:, lanes] = e_i

        if store:
            for cb in range(4):
                st_r = sr_ref[:, cb * 512:(cb + 1) * 512].astype(bf16)
                st_i = si_ref[:, cb * 512:(cb + 1) * 512].astype(bf16)
                y = (jnp.dot(st_r, cdr_ref[cb], preferred_element_type=f32)
                     - jnp.dot(st_i, cdi_ref[cb], preferred_element_type=f32))
                cols = slice(cb * 128, (cb + 1) * 128)
                y = y + d_ref[:, cols] * u[:, cols]
                y_ref[:, cols] = y
                ys_ref[:, cols] = _gelu(y).astype(bf16)
        else:
            @pl.when(i == nblk - 1)
            def _():
                p_r, p_i = _pow_segment(ar_ref[...], ai_ref[...], seg_len)
                t_r, t_i = car[0:1, :], cai[0:1, :]
                or_ref[0:1, :] = jnp.zeros((1, N_STATES), f32)
                oi_ref[0:1, :] = jnp.zeros((1, N_STATES), f32)
                for j in range(1, N_SEG):
                    or_ref[j:j + 1, :] = t_r
                    oi_ref[j:j + 1, :] = t_i
                    m_r, m_i = _cmul(p_r, p_i, t_r, t_i)
                    t_r, t_i = car[j:j + 1, :] + m_r, cai[j:j + 1, :] + m_i

    blk = lambda w: pl.BlockSpec((R, w), lambda i: (i, 0))
    in_specs = [blk(SSM_WIDTH), _full((4, 128, 512)), _full((4, 128, 512)), _full((1, N_STATES)), _full((1, N_STATES))]
    args = [u_perm, bd_r, bd_i, ar, ai]
    scratch = [pltpu.VMEM((R, N_STATES), f32), pltpu.VMEM((R, N_STATES), f32),
               pltpu.VMEM((N_SEG, N_STATES), f32), pltpu.VMEM((N_SEG, N_STATES), f32)]
    if store:
        in_specs += [_full((4, 512, 128)), _full((4, 512, 128)), _full((1, SSM_WIDTH)),
                     _full((N_SEG, N_STATES)), _full((N_SEG, N_STATES))]
        args += list(final)
        out_specs = [blk(SSM_WIDTH), blk(SSM_WIDTH), blk(N_STATES), blk(N_STATES)]
        out_shape = [jax.ShapeDtypeStruct((S, SSM_WIDTH), f32), jax.ShapeDtypeStruct((S, SSM_WIDTH), bf16),
                     jax.ShapeDtypeStruct((S, N_STATES), f32), jax.ShapeDtypeStruct((S, N_STATES), f32)]
        name = "ssm_scan_fwd"
    else:
        out_specs = [_full((N_SEG, N_STATES)), _full((N_SEG, N_STATES))]
        out_shape = [jax.ShapeDtypeStruct((N_SEG, N_STATES), f32)] * 2
        name = "ssm_scan_fwd_carry"
    res, landed = _pcall(body, args, name=name, grid=(nblk,), in_specs=in_specs, out_specs=out_specs,
                         out_shape=out_shape, scratch_shapes=scratch, rides=rides)
    return (res, landed) if rides else res


def ssm_scan_bwd(dys_perm, y_perm, cdt_r, cdt_i, ar, ai, *, final=None, rides=()):
    S = dys_perm.shape[0]
    R = SCAN_ROWS
    nblk = S // R
    seg_len = S // N_SEG
    store = final is not None
    strip = LANE_STRIP // 2 if store else LANE_STRIP
    nstrip = N_STATES // strip

    def body(*refs):
        if store:
            (dys_ref, y_ref, cdr_ref, cdi_ref, ar_ref, ai_ref, u_ref, sr_ref, si_ref, bdr_ref, bdi_ref, d_ref, ir_ref,
             ii_ref, du_ref, dbr_ref, dbi_ref, dcr_ref, dci_ref, dar_ref, dai_ref, dd_ref, dsr, dsi, lmr, lmi, car,
             cai) = refs
        else:
            dys_ref, y_ref, cdr_ref, cdi_ref, ar_ref, ai_ref, or_ref, oi_ref, dsr, dsi, car, cai = refs
        i = pl.program_id(0)
        dy = dys_ref[...] * _gelu_grad(y_ref[...])
        dyb = dy.astype(bf16)
        for cb in range(4):
            ds_ = dyb[:, cb * 128:(cb + 1) * 128]
            dsr[:, cb * 512:(cb + 1) * 512] = jnp.dot(ds_, cdr_ref[cb], preferred_element_type=f32)
            dsi[:, cb * 512:(cb + 1) * 512] = -jnp.dot(ds_, cdi_ref[cb], preferred_element_type=f32)

        @pl.when(i == 0)
        def _():
            if store:
                car[...] = ir_ref[...]
                cai[...] = ii_ref[...]
                dar_ref[...] = jnp.zeros_like(dar_ref)
                dai_ref[...] = jnp.zeros_like(dai_ref)
                dbr_ref[...] = jnp.zeros_like(dbr_ref)
                dbi_ref[...] = jnp.zeros_like(dbi_ref)
                dcr_ref[...] = jnp.zeros_like(dcr_ref)
                dci_ref[...] = jnp.zeros_like(dci_ref)
                dd_ref[...] = jnp.zeros_like(dd_ref)
            else:
                car[...] = jnp.zeros_like(car)
                cai[...] = jnp.zeros_like(cai)

        for ls in range(nstrip):
            lanes = pl.ds(ls * strip, strip)
            a_r = jnp.broadcast_to(ar_ref[:, lanes], (N_SEG, strip))
            a_i = jnp.broadcast_to(ai_ref[:, lanes], (N_SEG, strip))
            zero = jnp.zeros((N_SEG, strip), f32)

            def step(k, carry, lanes=lanes, a_r=a_r, a_i=a_i):
                l_r, l_i, g_r, g_i = carry
                row = pl.multiple_of((R // 8 - 1 - k) * 8, 8)
                if store:
                    s_r = sr_ref[pl.ds(row, 8), lanes]
                    s_i = si_ref[pl.ds(row, 8), lanes]
                    g_r = g_r + l_r * s_r + l_i * s_i
                    g_i = g_i + l_i * s_r - l_r * s_i
                n_r = dsr[pl.ds(row, 8), lanes] + a_r * l_r + a_i * l_i
                n_i = dsi[pl.ds(row, 8), lanes] + a_r * l_i - a_i * l_r
                if store:
                    lmr[pl.ds(row, 8), lanes] = n_r
                    lmi[pl.ds(row, 8), lanes] = n_i
                return n_r, n_i, g_r, g_i

            e_r, e_i, g_r, g_i = lax.fori_loop(0, R // 8, step, (car[:, lanes], cai[:, lanes], zero, zero),
                                               unroll=True)
            car[:, lanes] = e_r
            cai[:, lanes] = e_i
            if store:
                dar_ref[:, lanes] += g_r
                dai_ref[:, lanes] += g_i

        if store:
            ub = _bf(u_ref[...])
            u = ub.astype(f32)
            for cb in range(4):
                cols = slice(cb * 128, (cb + 1) * 128)
                st = slice(cb * 512, (cb + 1) * 512)
                l_r = lmr[:, st].astype(bf16)
                l_i = lmi[:, st].astype(bf16)
                du = (jnp.dot(l_r, bdr_ref[cb], preferred_element_type=f32)
                      + jnp.dot(l_i, bdi_ref[cb], preferred_element_type=f32))
                du_ref[:, cols] = (du + d_ref[:, cols] * dy[:, cols]).astype(bf16)
                dbr_ref[cb] += lax.dot_general(l_r, ub[:, cols], TN_DIMS, preferred_element_type=f32)
                dbi_ref[cb] += lax.dot_general(l_i, ub[:, cols], TN_DIMS, preferred_element_type=f32)
                dcr_ref[cb] += lax.dot_general(dyb[:, cols], sr_ref[:, st].astype(bf16), TN_DIMS,
                                               preferred_element_type=f32)
                dci_ref[cb] += lax.dot_general(dyb[:, cols], si_ref[:, st].astype(bf16), TN_DIMS,
                                               preferred_element_type=f32)
            dd_ref[...] += jnp.sum(dy * u, axis=0, keepdims=True)
        else:
            @pl.when(i == nblk - 1)
            def _():
                p_r, p_i = _pow_segment(ar_ref[...], ai_ref[...], seg_len)
                p_i = -p_i
                t_r, t_i = car[N_SEG - 1:N_SEG, :], cai[N_SEG - 1:N_SEG, :]
                or_ref[N_SEG - 1:N_SEG, :] = jnp.zeros((1, N_STATES), f32)
                oi_ref[N_SEG - 1:N_SEG, :] = jnp.zeros((1, N_STATES), f32)
                for j in range(N_SEG - 2, -1, -1):
                    or_ref[j:j + 1, :] = t_r
                    oi_ref[j:j + 1, :] = t_i
                    m_r, m_i = _cmul(p_r, p_i, t_r, t_i)
                    t_r, t_i = car[j:j + 1, :] + m_r, cai[j:j + 1, :] + m_i

    blk = lambda w: pl.BlockSpec((R, w), lambda i: (nblk - 1 - i, 0))
    in_specs = [blk(SSM_WIDTH), blk(SSM_WIDTH), _full((4, 128, 512)), _full((4, 128, 512)), _full((1, N_STATES)),
                _full((1, N_STATES))]
    args = [dys_perm, y_perm, cdt_r, cdt_i, ar, ai]
    seg = jax.ShapeDtypeStruct((N_SEG, N_STATES), f32)
    if store:
        in_specs += [blk(SSM_WIDTH), blk(N_STATES), blk(N_STATES), _full((4, 512, 128)), _full((4, 512, 128)),
                     _full((1, SSM_WIDTH)), _full((N_SEG, N_STATES)), _full((N_SEG, N_STATES))]
        args += list(final)
        out_specs = [blk(SSM_WIDTH), _full((4, 512, 128)), _full((4, 512, 128)), _full((4, 128, 512)),
                     _full((4, 128, 512)), _full((N_SEG, N_STATES)), _full((N_SEG, N_STATES)), _full((1, SSM_WIDTH))]
        b_acc = jax.ShapeDtypeStruct((4, 512, 128), f32)
        c_acc = jax.ShapeDtypeStruct((4, 128, 512), f32)
        out_shape = [jax.ShapeDtypeStruct((S, SSM_WIDTH), bf16), b_acc, b_acc, c_acc, c_acc, seg, seg,
                     jax.ShapeDtypeStruct((1, SSM_WIDTH), f32)]
        scratch = [pltpu.VMEM((R, N_STATES), f32)] * 4 + [pltpu.VMEM((N_SEG, N_STATES), f32)] * 2
        name = "ssm_scan_bwd"
    else:
        out_specs = [_full((N_SEG, N_STATES)), _full((N_SEG, N_STATES))]
        out_shape = [seg, seg]
        scratch = [pltpu.VMEM((R, N_STATES), f32)] * 2 + [pltpu.VMEM((N_SEG, N_STATES), f32)] * 2
        name = "ssm_scan_bwd_carry"
    res, landed = _pcall(body, args, name=name, grid=(nblk,), in_specs=in_specs, out_specs=out_specs,
                         out_shape=out_shape, scratch_shapes=scratch, rides=rides)
    return (res, landed) if rides else res


def ssm_param_bwd(dbb_r, dbb_i, bt_re, bt_im, kr, ki, ar, ai, lr, li, ldt, da_r, da_i):
    def body(dbr_ref, dbi_ref, br_ref, bi_ref, kr_ref, ki_ref, ar_ref, ai_ref, lr_ref, li_ref, ldt_ref, dar_ref,
             dai_ref, obr_ref, obi_ref, olr_ref, oli_ref, odt_ref):
        dbr, dbi, b_r, b_i = dbr_ref[...], dbi_ref[...], br_ref[...], bi_ref[...]
        k_r, k_i, a_r, a_i = kr_ref[...], ki_ref[...], ar_ref[...], ai_ref[...]
        l_r, l_i = lr_ref[...], li_ref[...]
        dt = jnp.exp(ldt_ref[...])
        obr_ref[...] = k_r * dbr + k_i * dbi
        obi_ref[...] = k_r * dbi - k_i * dbr
        gk_r = jnp.sum(dbr * b_r + dbi * b_i, axis=0, keepdims=True)
        gk_i = jnp.sum(dbi * b_r - dbr * b_i, axis=0, keepdims=True)
        ga_r = jnp.sum(dar_ref[...], axis=0, keepdims=True)
        ga_i = jnp.sum(dai_ref[...], axis=0, keepdims=True)
        den = l_r * l_r + l_i * l_i
        c_r, c_i = l_r / den, l_i / den
        m_r, m_i = _cmul(c_r, c_i, gk_r, gk_i)
        g_r, g_i = ga_r + m_r, ga_i + m_i
        t1_r, t1_i = _cmul(dt * a_r, -dt * a_i, g_r, g_i)
        q_r, q_i = _cmul(k_r, k_i, c_r, -c_i)
        t2_r, t2_i = _cmul(-q_r, q_i, gk_r, gk_i)
        olr_ref[...] = t1_r + t2_r
        oli_ref[...] = t1_i + t2_i
        w_r, w_i = _cmul(l_r, l_i, a_r, a_i)
        odt_ref[...] = dt * (w_r * g_r + w_i * g_i)

    v = jax.ShapeDtypeStruct((1, N_STATES), f32)
    m = jax.ShapeDtypeStruct((SSM_GROUP, N_STATES), f32)
    return pl.pallas_call(body, name="ssm_param_bwd", out_shape=[m, m, v, v, v])(
        dbb_r, dbb_i, bt_re, bt_im, kr, ki, ar, ai, lr, li, ldt, da_r, da_i)


def group_sum(v):
    def body(v_ref, o_ref):
        o_ref[...] = jnp.sum(v_ref[...], axis=-1, keepdims=True)
    return pl.pallas_call(body, name="group_sum", out_shape=jax.ShapeDtypeStruct((v.shape[0], 1), f32))(v)


ATTN_SCALE = XATTN_HEAD_DIM ** -0.5


def _attn_probs(q_h, k_h):
    s = lax.dot_general(q_h, k_h, NT_DIMS, preferred_element_type=f32) * ATTN_SCALE
    e = jnp.exp(s - jnp.max(s, axis=-1, keepdims=True))
    return e / jnp.sum(e, axis=-1, keepdims=True)


def attn_fwd(proj, kv, wxo3, tm=TM):
    S = proj.shape[0]
    M = kv.shape[0]
    J, _, n = wxo3.shape

    def body(q_ref, kv_ref, w_ref, o_ref, y_ref):
        for h in range(XATTN_HEADS):
            cols = slice(h * XATTN_HEAD_DIM, (h + 1) * XATTN_HEAD_DIM)
            q_h = q_ref[:, cols].astype(bf16)
            k_h = kv_ref[:, cols]
            v_h = kv_ref[:, XATTN_WIDTH + h * XATTN_HEAD_DIM:XATTN_WIDTH + (h + 1) * XATTN_HEAD_DIM]
            p = _attn_probs(q_h, k_h)
            o_ref[:, cols] = jnp.dot(p.astype(bf16), v_h, preferred_element_type=f32).astype(bf16)
        o = o_ref[...]
        for j in range(J):
            y_ref[:, j * n:(j + 1) * n] = jnp.dot(o, w_ref[j], preferred_element_type=f32).astype(bf16)

    return pl.pallas_call(
        body, name="attn_fwd", grid=(S // tm,),
        in_specs=[_rows(tm, XATTN_WIDTH, 10), _full((M, 2 * XATTN_WIDTH)), _full(wxo3.shape)],
        out_specs=[_rows(tm, XATTN_WIDTH), _rows(tm, J * n)],
        out_shape=[jax.ShapeDtypeStruct((S, XATTN_WIDTH), bf16), jax.ShapeDtypeStruct((S, J * n), bf16)],
        compiler_params=_params())(proj, kv, wxo3)


def attn_bwd(dy_c, o_att, wxo3, proj, kv, dproj, tm=TM):
    S = proj.shape[0]
    M = kv.shape[0]
    J, _, n = wxo3.shape
    nt = S // tm

    def body(dy_ref, o_ref, w_ref, q_ref, kv_ref, _, dq_ref, dkv_ref, dwo_ref, acc_ref):
        i = pl.program_id(0)

        @pl.when(i == 0)
        def _():
            dkv_ref[...] = jnp.zeros_like(dkv_ref)

        part = lax.dot_general(o_ref[...], dy_ref[...], TN_DIMS, preferred_element_type=f32)

        @pl.when(i == 0)
        def _():
            acc_ref[...] = part

        @pl.when(i > 0)
        def _():
            acc_ref[...] += part

        @pl.when(i == nt - 1)
        def _():
            for j in range(J):
                dwo_ref[j] = acc_ref[:, j * n:(j + 1) * n].astype(bf16)

        do = None
        for j in range(J):
            part = lax.dot_general(dy_ref[:, j * n:(j + 1) * n], w_ref[j], NT_DIMS, preferred_element_type=f32)
            do = part if do is None else do + part
        do = do.astype(bf16)
        for h in range(XATTN_HEADS):
            cols = slice(h * XATTN_HEAD_DIM, (h + 1) * XATTN_HEAD_DIM)
            vcols = slice(XATTN_WIDTH + h * XATTN_HEAD_DIM, XATTN_WIDTH + (h + 1) * XATTN_HEAD_DIM)
            q_h = q_ref[:, cols].astype(bf16)
            k_h = kv_ref[:, cols]
            v_h = kv_ref[:, vcols]
            do_h = do[:, cols]
            p = _attn_probs(q_h, k_h)
            dp = lax.dot_general(do_h, v_h, NT_DIMS, preferred_element_type=f32)
            ds = (p * (dp - jnp.sum(dp * p, axis=-1, keepdims=True)) * ATTN_SCALE).astype(bf16)
            dq_ref[:, cols] = jnp.dot(ds, k_h, preferred_element_type=f32).astype(bf16)
            dkv_ref[:, cols] += lax.dot_general(ds, q_h, TN_DIMS, preferred_element_type=f32)
            dkv_ref[:, vcols] += lax.dot_general(p.astype(bf16), do_h, TN_DIMS, preferred_element_type=f32)

    return pl.pallas_call(
        body, name="attn_bwd", grid=(S // tm,),
        in_specs=[_rows(tm, J * n), _rows(tm, XATTN_WIDTH), _full(wxo3.shape), _rows(tm, XATTN_WIDTH, 10),
                  _full((M, 2 * XATTN_WIDTH)), pl.BlockSpec(memory_space=pl.ANY)],
        out_specs=[_rows(tm, XATTN_WIDTH, 10), _full((M, 2 * XATTN_WIDTH)), _full(wxo3.shape)],
        out_shape=[jax.ShapeDtypeStruct(dproj.shape, bf16), jax.ShapeDtypeStruct((M, 2 * XATTN_WIDTH), f32),
                   jax.ShapeDtypeStruct(wxo3.shape, bf16)],
        scratch_shapes=[pltpu.VMEM((XATTN_WIDTH, J * n), f32)], input_output_aliases={5: 0},
        compiler_params=_params())(dy_c, o_att, wxo3, proj, kv, dproj)


def _ln_stats(r):
    mu = jnp.mean(r, axis=-1, keepdims=True)
    xc = r - mu
    var = jnp.mean(xc * xc, axis=-1, keepdims=True)
    rstd = lax.rsqrt(var + LN_EPS)
    return xc * rstd, rstd


def _ln_bwd(dy, xhat, rstd, g):
    dxh = dy * g
    return rstd * (dxh - jnp.mean(dxh, axis=-1, keepdims=True) - xhat * jnp.mean(dxh * xhat, axis=-1, keepdims=True))


def _colsum(v):
    return jnp.sum(v, axis=0, keepdims=True)


def _ln2_loss_epi(t2, h1, target, b_down, g, b):
    xhat, rstd = _ln_stats(ALPHA * h1 + t2 + b_down)
    err = xhat * g + b - target
    dout = err * (1.0 / D_MODEL)
    dr = _ln_bwd(dout, xhat, rstd, g)
    return (dr, dr), (0.5 * _colsum(err * err) * (1.0 / D_MODEL), _colsum(dout * xhat), _colsum(dout), _colsum(dr))


def _merge_fwd_pro(g0, g1, g2, y_a, glu_a, glu_b, y_c, b):
    gs = [_sig(g.astype(f32) + b[:, k * D_MODEL:(k + 1) * D_MODEL]) for k, g in enumerate((g0, g1, g2))]
    y_b = glu_a.astype(f32) * _sig(glu_b.astype(f32))
    return gs[0] * y_a.astype(f32) + gs[1] * y_b + gs[2] * y_c.astype(f32)


def _merge_bwd_epi(dm, g0, g1, g2, y_a, glu_a, glu_b, y_c, b):
    ga, sb = glu_a.astype(f32), _sig(glu_b.astype(f32))
    ys = (y_a.astype(f32), ga * sb, y_c.astype(f32))
    gs = [_sig(g.astype(f32) + b[:, k * D_MODEL:(k + 1) * D_MODEL]) for k, g in enumerate((g0, g1, g2))]
    dpre = [dm * ys[k] * gs[k] * (1.0 - gs[k]) for k in range(3)]
    dyb = dm * gs[1]
    dglu = jnp.concatenate([dyb * sb, dyb * ga * sb * (1.0 - sb)], axis=1)
    return ((jnp.concatenate(dpre, axis=1), dm * gs[0], dglu, dm * gs[2]),
            (jnp.concatenate([_colsum(d) for d in dpre], axis=1),))


def _ln1_fwd_epi(t1, x, g, b):
    xhat, _ = _ln_stats(ALPHA * x + t1)
    h = xhat * g + b
    return (t1, h, h), ()


def _ln1_bwd_epi(t3, x, t1, dr2, g):
    xhat, rstd = _ln_stats(ALPHA * x + t1)
    dh = ALPHA * dr2 + t3
    dr = _ln_bwd(dh, xhat, rstd, g)
    return (dr, dr), (_colsum(dh * xhat), _colsum(dh))


def exchange(*rides, name):
    return _pcall(lambda: None, [], name=name, grid=(), in_specs=[], out_specs=[], out_shape=[], rides=rides)[1]


HBM_ONLY = pl.BlockSpec(memory_space=pltpu.HBM)
SEM_SPEC = pl.BlockSpec(memory_space=pltpu.SEMAPHORE)
DATAFLOW = pltpu.SideEffectType.DATAFLOW_SIDE_EFFECTING


def scatter_start(g):
    def body(g_ref, land_ref, s0, s1, s2, r0, r1, r2, g_thru, land_thru, token):
        del g_thru, land_thru
        x, y, c = lax.axis_index("x"), lax.axis_index("y"), lax.axis_index("c")
        me = 2 * x + y
        for k, (px, py) in enumerate(_xy_peers(x, y)):
            pltpu.make_async_remote_copy(src_ref=g_ref.at[2 * px + py], dst_ref=land_ref.at[me],
                                         send_sem=(s0, s1, s2)[k], recv_sem=(r0, r1, r2)[k],
                                         device_id=(px, py, c), device_id_type=MESH).start()
        token[...] = jnp.zeros_like(token)

    sem = pltpu.SemaphoreType.DMA(())
    return pl.pallas_call(
        body, name="scatter_start",
        out_shape=(sem,) * 6 + (pltpu.HBM(g.shape, g.dtype), pltpu.HBM(g.shape, g.dtype),
                                jax.ShapeDtypeStruct((1, D_MODEL), f32)),
        in_specs=(HBM_ONLY, HBM_ONLY), out_specs=(SEM_SPEC,) * 6 + (HBM_ONLY, HBM_ONLY, _full((1, D_MODEL))),
        input_output_aliases={0: 6, 1: 7}, compiler_params=pltpu.CompilerParams(has_side_effects=DATAFLOW),
    )(pltpu.with_memory_space_constraint(g, pltpu.HBM),
      pltpu.with_memory_space_constraint(lax.empty(g.shape, g.dtype), pltpu.HBM))


def scatter_wait(sems, g_thru, land_thru, after):
    def body(g_ref, land_ref, s0, s1, s2, r0, r1, r2, after_ref, g_out, land_out):
        del after_ref, g_out, land_out
        x, y, c = lax.axis_index("x"), lax.axis_index("y"), lax.axis_index("c")
        for k, (px, py) in enumerate(_xy_peers(x, y)):
            copy = pltpu.make_async_remote_copy(src_ref=g_ref.at[2 * px + py], dst_ref=land_ref.at[2 * px + py],
                                                send_sem=(s0, s1, s2)[k], recv_sem=(r0, r1, r2)[k],
                                                device_id=(px, py, c), device_id_type=MESH)
            copy.wait_send()
            copy.wait_recv()

    return pl.pallas_call(
        body, name="scatter_wait",
        out_shape=(pltpu.HBM(g_thru.shape, g_thru.dtype), pltpu.HBM(g_thru.shape, g_thru.dtype)),
        in_specs=(HBM_ONLY, HBM_ONLY) + (SEM_SPEC,) * 6 + (pl.BlockSpec(memory_space=pl.ANY),),
        out_specs=(HBM_ONLY, HBM_ONLY), input_output_aliases={0: 0, 1: 1},
        compiler_params=pltpu.CompilerParams(has_side_effects=DATAFLOW),
    )(g_thru, land_thru, *sems, after)


ROW_TILE = 256


def sum_slots(recv, name):
    n, R, C = recv.shape
    tm = min(R, ROW_TILE)

    def body(r_ref, o_ref):
        acc = r_ref[0].astype(f32)
        for k in range(1, n):
            acc = acc + r_ref[k].astype(f32)
        o_ref[...] = acc

    return pl.pallas_call(
        body, name=name, grid=(R // tm,), in_specs=[pl.BlockSpec((n, tm, C), lambda i: (0, i, 0))],
        out_specs=_rows(tm, C), out_shape=jax.ShapeDtypeStruct((R, C), f32), compiler_params=_params())(recv)


def _adamw(w, g, m, v):
    m = ADAM_B1 * m + (1.0 - ADAM_B1) * g
    v = ADAM_B2 * v + (1.0 - ADAM_B2) * (g * g)
    m_hat = m / (1.0 - ADAM_B1 ** ADAM_STEP)
    v_hat = v / (1.0 - ADAM_B2 ** ADAM_STEP)
    delta = -ADAM_LR * (m_hat / (jnp.sqrt(v_hat) + ADAM_EPS) + ADAM_WD * w)
    return delta, m, v


def adam_pair(p, q, w, m, v, name):
    R, C = w.shape
    tm = min(R, ROW_TILE)

    def body(p_ref, q_ref, w_ref, m_ref, v_ref, g_ref, d_ref, nm_ref, nv_ref):
        g = p_ref[...] + q_ref[...]
        g_ref[...] = g
        d_ref[...], nm_ref[...], nv_ref[...] = _adamw(w_ref[...], g, m_ref[...], v_ref[...])

    o = jax.ShapeDtypeStruct((R, C), f32)
    return pl.pallas_call(body, name=name, grid=(R // tm,), in_specs=[_rows(tm, C)] * 5,
                          out_specs=[_rows(tm, C)] * 4, out_shape=[o] * 4, compiler_params=_params())(p, q, w, m, v)


SC_TILES = 32
SC_ROWS = 8


def adam_pair_sc(p, q, w, m, v, name):
    R, C = w.shape
    per_tile = R // SC_TILES

    def body(p_hbm, q_hbm, w_hbm, m_hbm, v_hbm, g_hbm, d_hbm, nm_hbm, nv_hbm, pb, qb, wb, mb, vb, gb, db, nmb, nvb):
        tile = lax.axis_index("sc_tile") * 2 + lax.axis_index("sc_core")

        @pl.loop(0, per_tile, step=SC_ROWS)
        def _(r0):
            rows = pl.ds(tile * per_tile + r0, SC_ROWS)
            for src, buf in ((p_hbm, pb), (q_hbm, qb), (w_hbm, wb), (m_hbm, mb), (v_hbm, vb)):
                pltpu.sync_copy(src.at[rows], buf)

            @pl.loop(0, SC_ROWS)
            def _(r):
                @pl.loop(0, C, step=16)
                def _(i):
                    at = (r, pl.ds(i, 16))
                    g = pb[at] + qb[at]
                    d, nm, nv = _adamw(wb[at], g, mb[at], vb[at])
                    gb[at], db[at], nmb[at], nvb[at] = g, d, nm, nv

            for buf, dst in ((gb, g_hbm), (db, d_hbm), (nmb, nm_hbm), (nvb, nv_hbm)):
                pltpu.sync_copy(buf, dst.at[rows])

    o = jax.ShapeDtypeStruct((R, C), f32)
    return pl.kernel(
        body, name=name, out_type=[o] * 4,
        mesh=plsc.VectorSubcoreMesh(core_axis_name="sc_core", subcore_axis_name="sc_tile"),
        scratch_types=[pltpu.VMEM((SC_ROWS, C), f32)] * 9)(p, q, w, m, v)


def adam_plain(w, g, m, v, name):
    def body(w_ref, g_ref, m_ref, v_ref, d_ref, nm_ref, nv_ref):
        d_ref[...], nm_ref[...], nv_ref[...] = _adamw(w_ref[...], g_ref[...], m_ref[...], v_ref[...])

    o = jax.ShapeDtypeStruct(w.shape, f32)
    return pl.pallas_call(body, name=name, out_shape=[o] * 3, compiler_params=_params())(w, g, m, v)


def adam_slots(recv, w, m, v, name):
    n = recv.shape[0]

    def body(r_ref, w_ref, m_ref, v_ref, g_ref, d_ref, nm_ref, nv_ref):
        g = r_ref[0]
        for k in range(1, n):
            g = g + r_ref[k]
        g_ref[...] = g
        d_ref[...], nm_ref[...], nv_ref[...] = _adamw(w_ref[...], g, m_ref[...], v_ref[...])

    o = jax.ShapeDtypeStruct(w.shape, f32)
    return pl.pallas_call(body, name=name, out_shape=[o] * 4, compiler_params=_params())(recv, w, m, v)


BIG = ("w_in", "w_conv_out", "w_glu", "w_xattn_out", "w_kv", "w_out", "w_up", "w_down")
MID = ("w_conv_out", "w_glu", "w_xattn_out", "w_kv", "w_out")
WIDE_MINOR = ("ssm_b_re", "ssm_b_im", "ssm_c_re", "ssm_c_im")
SMALL = ("b_gate", "ssm_lam_re", "ssm_lam_im", "ssm_log_dt", "ssm_b_re", "ssm_b_im", "ssm_c_re", "ssm_c_im", "ssm_d",
         "ln1_g", "ln1_b", "b_up", "b_down", "ln2_g", "ln2_b")
def _pad_flat(a, mult=1024):
    a = a.reshape(-1)
    return jnp.pad(a, (0, (-a.shape[0]) % mult))


def _perm(a):
    S, W = a.shape
    return a.reshape(N_SEG, S // N_SEG, W).transpose(1, 0, 2).reshape(S, W)


def _unperm(a):
    S, W = a.shape
    return a.reshape(S // N_SEG, N_SEG, W).transpose(1, 0, 2).reshape(S, W)


def _state_rows(a):
    return a.transpose(2, 0, 1).reshape(SSM_GROUP, N_STATES)


def _block_diag_b(bt):
    b4 = bt.reshape(SSM_GROUP, 4, 8, SSM_STATE)
    eye = jnp.eye(8, dtype=bt.dtype)
    return jnp.einsum("hcgp,gk->cghkp", b4, eye).reshape(4, 128, 512)


def _block_diag_c(c):
    c4 = c.reshape(4, 8, SSM_GROUP, SSM_STATE)
    eye = jnp.eye(8, dtype=c.dtype)
    return jnp.einsum("cghp,gk->cgpkh", c4, eye).reshape(4, 512, 128)


def _diag_b(acc):
    a = acc.reshape(4, 8, SSM_STATE, 8, SSM_GROUP)
    eye = jnp.eye(8, dtype=acc.dtype)
    return jnp.einsum("cgpkh,gk->hcgp", a, eye).reshape(SSM_GROUP, N_STATES)


def _diag_c(acc):
    a = acc.reshape(4, 8, SSM_GROUP, 8, SSM_STATE)
    eye = jnp.eye(8, dtype=acc.dtype)
    return jnp.einsum("cghkp,gk->cghp", a, eye).reshape(SSM_GROUPS, SSM_GROUP, SSM_STATE)


def kernel(x, mem, w_in, b_gate, conv_w, w_conv_out, ssm_lam_re, ssm_lam_im, ssm_log_dt, ssm_b_re, ssm_b_im, ssm_c_re, ssm_c_im, ssm_d, w_glu, w_kv, w_xattn_out, w_out, ln1_g, ln1_b, w_up, b_up, w_down, b_down, ln2_g, ln2_b, loss_target, m_w_in, m_b_gate, m_conv_w, m_w_conv_out, m_ssm_lam_re, m_ssm_lam_im, m_ssm_log_dt, m_ssm_b_re, m_ssm_b_im, m_ssm_c_re, m_ssm_c_im, m_ssm_d, m_w_glu, m_w_kv, m_w_xattn_out, m_w_out, m_ln1_g, m_ln1_b, m_w_up, m_b_up, m_w_down, m_b_down, m_ln2_g, m_ln2_b, v_w_in, v_b_gate, v_conv_w, v_w_conv_out, v_ssm_lam_re, v_ssm_lam_im, v_ssm_log_dt, v_ssm_b_re, v_ssm_b_im, v_ssm_c_re, v_ssm_c_im, v_ssm_d, v_w_glu, v_w_kv, v_w_xattn_out, v_w_out, v_ln1_g, v_ln1_b, v_w_up, v_b_up, v_w_down, v_b_down, v_ln2_g, v_ln2_b):
    W = dict(w_in=w_in, b_gate=b_gate, conv_w=conv_w, w_conv_out=w_conv_out, ssm_lam_re=ssm_lam_re,
             ssm_lam_im=ssm_lam_im, ssm_log_dt=ssm_log_dt, ssm_b_re=ssm_b_re, ssm_b_im=ssm_b_im, ssm_c_re=ssm_c_re,
             ssm_c_im=ssm_c_im, ssm_d=ssm_d, w_glu=w_glu, w_kv=w_kv, w_xattn_out=w_xattn_out, w_out=w_out,
             ln1_g=ln1_g, ln1_b=ln1_b, w_up=w_up, b_up=b_up, w_down=w_down, b_down=b_down, ln2_g=ln2_g, ln2_b=ln2_b)
    MOM = dict(w_in=m_w_in, b_gate=m_b_gate, conv_w=m_conv_w, w_conv_out=m_w_conv_out, ssm_lam_re=m_ssm_lam_re,
               ssm_lam_im=m_ssm_lam_im, ssm_log_dt=m_ssm_log_dt, ssm_b_re=m_ssm_b_re, ssm_b_im=m_ssm_b_im,
               ssm_c_re=m_ssm_c_re, ssm_c_im=m_ssm_c_im, ssm_d=m_ssm_d, w_glu=m_w_glu, w_kv=m_w_kv,
               w_xattn_out=m_w_xattn_out, w_out=m_w_out, ln1_g=m_ln1_g, ln1_b=m_ln1_b, w_up=m_w_up, b_up=m_b_up,
               w_down=m_w_down, b_down=m_b_down, ln2_g=m_ln2_g, ln2_b=m_ln2_b)
    VEL = dict(w_in=v_w_in, b_gate=v_b_gate, conv_w=v_conv_w, w_conv_out=v_w_conv_out, ssm_lam_re=v_ssm_lam_re,
               ssm_lam_im=v_ssm_lam_im, ssm_log_dt=v_ssm_log_dt, ssm_b_re=v_ssm_b_re, ssm_b_im=v_ssm_b_im,
               ssm_c_re=v_ssm_c_re, ssm_c_im=v_ssm_c_im, ssm_d=v_ssm_d, w_glu=v_w_glu, w_kv=v_w_kv,
               w_xattn_out=v_w_xattn_out, w_out=v_w_out, ln1_g=v_ln1_g, ln1_b=v_ln1_b, w_up=v_w_up, b_up=v_b_up,
               w_down=v_w_down, b_down=v_b_down, ln2_g=v_ln2_g, ln2_b=v_ln2_b)
    names = list(W)
    xy = 2 * lax.axis_index("x") + lax.axis_index("y")

    xs = x[0]
    S = xs.shape[0]
    mems = mem[0]
    tgt = loss_target[0]

    shard_bf = {n: W[n][0].astype(bf16) for n in BIG}

    lr = ssm_lam_re.reshape(1, N_STATES)
    li = ssm_lam_im.reshape(1, N_STATES)
    ldt = jnp.repeat(ssm_log_dt.reshape(SSM_GROUPS), SSM_STATE).reshape(1, N_STATES)
    bt_re, bt_im = _state_rows(ssm_b_re[0]), _state_rows(ssm_b_im[0])
    ar, ai, kr, ki, bbt_r, bbt_i = ssm_prep(lr, li, ldt, bt_re, bt_im)
    bd_r, bd_i = _block_diag_b(bbt_r).astype(bf16), _block_diag_b(bbt_i).astype(bf16)
    cd_r, cd_i = _block_diag_c(ssm_c_re[0]).astype(bf16), _block_diag_c(ssm_c_im[0]).astype(bf16)
    bdt_r, bdt_i = bd_r.transpose(0, 2, 1), bd_i.transpose(0, 2, 1)
    cdt_r, cdt_i = cd_r.transpose(0, 2, 1), cd_i.transpose(0, 2, 1)
    d_skip = ssm_d.reshape(1, SSM_WIDTH)

    mem_bf = mems.astype(bf16)
    u_col = GATE_COLS + 3 * CONV_WIDTH
    me_xy = jnp.reshape(xy, (1,)).astype(jnp.int32)
    proj, x_bf, (win3,) = proj_own(xs, shard_bf["w_in"], me_xy, 4, rides=(("gather2", shard_bf["w_in"]),))
    proj, (wco3, wglu3, wxo3, wkv4, wout4, convw4) = proj_rest(
        x_bf, win3, me_xy, proj, rides=tuple(("gather", shard_bf[n]) for n in MID) + (("gather", conv_w[0]),))
    wkv3 = wkv4.reshape(1, D_MODEL, 2 * XATTN_WIDTH)
    wout3 = wout4.reshape(1, D_MODEL, D_MODEL)
    convw_full = convw4.transpose(1, 0, 2).reshape(3, CONV_WIDTH)
    a_conv, y_a = conv_fwd(proj, convw_full, wco3)
    u_perm = _perm(proj[:, u_col:u_col + SSM_WIDTH])
    (init_r, init_i), (wup3,) = ssm_scan_fwd(u_perm, bd_r, bd_i, ar, ai, rides=(("gather2", shard_bf["w_up"]),))
    ysm_perm, ys_perm, st_r, st_i = ssm_scan_fwd(u_perm, bd_r, bd_i, ar, ai,
                                                 final=(cd_r, cd_i, d_skip, init_r, init_i))
    y_s = _unperm(ys_perm)
    glu = mm_fwd_small(y_s, wglu3, "glu", out_dtype=bf16)
    kv = mm_fwd_small(mem_bf, wkv3, "kv", out_dtype=bf16)
    o_att, y_c = attn_fwd(proj, kv, wxo3)
    merge_rows = ((proj, D_MODEL, 0), (proj, D_MODEL, 1), (proj, D_MODEL, 2), y_a, (glu, D_MODEL, 0),
                  (glu, D_MODEL, 1), y_c)
    merged, t1, h1, h1_bf = mm_rows_fused(
        (_merge_fwd_pro, merge_rows, (b_gate,)), wout3, "merge_w_out_ln1", transposed=False, tm=256,
        epi=_ln1_fwd_epi, extras=(xs,), consts=(ln1_g, ln1_b), outs=(f32, f32, bf16))
    (r_up, hdn), (wdn4,) = mm_fwd(h1_bf, wup3, "w_up", bias=b_up, outs=(bf16, bf16), second=_relu2,
                                  first=lambda v: jnp.maximum(v, 0.0), rides=(("gather2", shard_bf["w_down"]),))
    wdn3 = wdn4.reshape(1, D_FF, D_MODEL)

    dr2, dr2_bf, loss_cols, d_ln2_g, d_ln2_b, d_b_down = mm_rows_fused(
        hdn, wdn3, "w_down_ln2_loss", transposed=False, tm=512, epi=_ln2_loss_epi, extras=(h1, tgt),
        consts=(b_down, ln2_g, ln2_b), outs=(f32, bf16), nsums=4)
    part, other = {}, {}
    g_w_down = mm_bwd_w(hdn, dr2_bf, 1, "dw_down", tk=1024).reshape(4, -1, D_MODEL)
    dup, d_b_up = mm_bwd_x(dr2_bf, wdn3, "dup", tm=512, extras=(r_up,), colsum=True, out_dtype=bf16,
                           epi=lambda acc, r: acc * (2.0 * r.astype(f32)))
    g_w_up = mm_bwd_w(h1_bf, dup, 4, "dw_up")
    (dr1, dr1_bf, d_ln1_g, d_ln1_b), (recv_dn,) = mm_rows_fused(
        dup, wup3, "dh1_ln1_bwd", transposed=True, tm=512, epi=_ln1_bwd_epi, extras=(xs, t1, dr2), consts=(ln1_g,),
        outs=(f32, bf16), nsums=2, rides=(("scatter", g_w_down),))
    part["w_down"] = sum_slots(recv_dn, "sum_w_down")
    g_w_out = mm_bwd_w(merged, dr1_bf, 1, "dw_out").reshape(4, -1, D_MODEL)
    (dproj, dy_a, dglu, dy_c, d_b_gate), (recv_up,) = mm_rows_fused(
        dr1_bf, wout3, "dmerged_merge_bwd", transposed=True, tm=256, epi=_merge_bwd_epi, extras=merge_rows,
        consts=(b_gate,), outs=((bf16, GATE_COLS, IN_COLS), bf16, (bf16, 2 * D_MODEL, 2 * D_MODEL), bf16),
        sum_widths=(GATE_COLS,), rides=(("scatter", g_w_up),))
    part["w_up"] = sum_slots(recv_up, "sum_w_up")

    dproj, d_conv_w, g_w_co = conv_bwd(dy_a, a_conv, wco3, proj, convw_full, dproj)

    g_w_glu = mm_bwd_w_small(y_s, dglu, 4, "dw_glu")
    dys_perm = _perm(mm_bwd_x(dglu, wglu3, "dy_s", tm=1024))
    linit_r, linit_i = ssm_scan_bwd(dys_perm, ysm_perm, cdt_r, cdt_i, ar, ai)
    (du_perm, dbacc_r, dbacc_i, dcacc_r, dcacc_i, da_r, da_i, d_ssm_d), (recv_co, recv_glu, recv_out) = ssm_scan_bwd(
        dys_perm, ysm_perm, cdt_r, cdt_i, ar, ai,
        final=(u_perm, st_r, st_i, bdt_r, bdt_i, d_skip, linit_r, linit_i),
        rides=(("scatter", g_w_co), ("scatter", g_w_glu), ("scatter", g_w_out)))
    part["w_out"] = sum_slots(recv_out, "sum_w_out")
    part["w_conv_out"] = sum_slots(recv_co, "sum_w_conv_out")
    part["w_glu"] = sum_slots(recv_glu, "sum_w_glu")
    dbt_re, dbt_im, d_lr, d_li, d_ldt_state = ssm_param_bwd(
        _diag_b(dbacc_r), _diag_b(dbacc_i), bt_re, bt_im, kr, ki, ar, ai, lr, li, ldt, da_r, da_i)
    d_log_dt = group_sum(d_ldt_state.reshape(SSM_GROUPS, SSM_STATE))
    d_b_re = dbt_re.reshape(SSM_GROUP, SSM_GROUPS, SSM_STATE).transpose(1, 2, 0)
    d_b_im = dbt_im.reshape(SSM_GROUP, SSM_GROUPS, SSM_STATE).transpose(1, 2, 0)
    d_c_re = _diag_c(dcacc_r)
    d_c_im = -_diag_c(dcacc_i)

    dproj, dkv, g_w_xo = attn_bwd(dy_c, o_att, wxo3, proj, kv, dproj)
    g_w_kv = mm_bwd_w(mem_bf, dkv, 1, "dw_kv").reshape(4, -1, D_MODEL)

    dproj = lax.dynamic_update_slice(dproj, _unperm(du_perm), (0, u_col))
    g_small = {"b_gate": d_b_gate, "ssm_lam_re": d_lr, "ssm_lam_im": d_li, "ssm_log_dt": d_log_dt, "ssm_b_re": d_b_re,
               "ssm_b_im": d_b_im, "ssm_c_re": d_c_re, "ssm_c_im": d_c_im, "ssm_d": d_ssm_d, "ln1_g": d_ln1_g,
               "ln1_b": d_ln1_b, "b_up": d_b_up, "b_down": d_b_down, "ln2_g": d_ln2_g, "ln2_b": d_ln2_b}
    small_names = SMALL + ("conv_w",)
    g_small["conv_w"] = d_conv_w
    sizes = {n: (g_small[n].size + 1023) // 1024 * 1024 for n in small_names}
    pack = lambda d: jnp.concatenate([_pad_flat(d[n]) for n in small_names]).reshape(-1, 128)
    early = ("w_down", "w_up", "w_out", "w_conv_out", "w_glu")
    g_w_in, landed = mm_bwd_w(
        x_bf, dproj, 4, "dw_in", rides=(("scatter", g_w_xo), ("scatter", g_w_kv), ("all", pack(g_small)))
        + tuple(("pair", part[n]) for n in early))
    recv_xo, recv_kv, srecv = landed[:3]
    other.update(zip(early, landed[3:]))
    part["w_xattn_out"] = sum_slots(recv_xo, "sum_w_xattn_out")
    part["w_kv"] = sum_slots(recv_kv, "sum_w_kv")
    *sems, g_thru, land_thru, token = scatter_start(g_w_in)
    res = [{}, {}, {}, {}]
    for n in early:
        for k, r in enumerate(adam_pair_sc(part[n], other[n], W[n][0], MOM[n][0], VEL[n][0], "adam_sc_" + n)):
            res[k][n] = r[None]
    dx, = mm_rows_fused(dproj, win3, "dx", transposed=True, tm=512, extras=(dr1,), consts=(token,),
                        epi=lambda acc, d, tk: ((acc + ALPHA * d + tk,), ()), outs=(f32,))
    g_own, landed_in = scatter_wait(sems, g_thru, land_thru, dx)
    recv_in = lax.dynamic_update_slice(landed_in, lax.dynamic_index_in_dim(g_own, xy, 0, keepdims=True), (xy, 0, 0))
    late = ("w_in", "w_xattn_out", "w_kv")
    part["w_in"] = sum_slots(recv_in, "sum_w_in")
    other.update(zip(late, exchange(*[("pair", part[n]) for n in late], name="swap_late")))
    for n in late:
        for k, r in enumerate(adam_pair(part[n], other[n], W[n][0], MOM[n][0], VEL[n][0], "adam_" + n)):
            res[k][n] = r[None]
    conv_zero = jnp.zeros((3, CONV_WIDTH), f32)
    in_pack = lambda d: pack({**{n: (jnp.zeros(W[n].shape, f32) if n in WIDE_MINOR else d[n]) for n in SMALL},
                              "conv_w": conv_zero})
    gs, ds_, ms, vs = adam_slots(srecv, in_pack(W), in_pack(MOM), in_pack(VEL), "adam_small")

    def unpack_small(buf, skip=()):
        flat = buf.reshape(-1)
        out, r = {}, 0
        for n in small_names:
            ref = g_small[n] if n == "conv_w" else W[n]
            if n not in skip:
                out[n] = flat[r:r + ref.size].reshape(ref.shape)
            r += sizes[n]
        return out

    res_s = [unpack_small(gs)] + [unpack_small(b, WIDE_MINOR) for b in (ds_, ms, vs)]
    for n in WIDE_MINOR:
        two_d = (-1, W[n].shape[-1])
        outs_n = adam_plain(W[n].reshape(two_d), res_s[0][n].reshape(two_d), MOM[n].reshape(two_d),
                            VEL[n].reshape(two_d), "adam_" + n)
        for k, r in enumerate(outs_n):
            res_s[k + 1][n] = r.reshape(W[n].shape)
    g_conv = lax.dynamic_slice(res_s[0]["conv_w"], (0, xy * 128), (3, 128))
    conv_slots = g_conv.reshape(1, 3, 128)
    cg, cd, cm, cv = adam_slots(conv_slots, conv_w[0], m_conv_w[0], v_conv_w[0], "adam_conv")
    conv_res = [cg, cd, cm, cv]

    loss = lax.psum(jnp.sum(loss_cols), ("x", "y", "c"))
    outs = [loss, dx.reshape(x.shape)]
    for k in range(4):
        for n in names:
            if n == "conv_w":
                outs.append(conv_res[k].reshape(conv_w.shape))
            elif n in BIG:
                outs.append(res[k][n])
            else:
                outs.append(res_s[k][n])
    return tuple(outs)
```

```python
import functools
import math

import jax
import jax.numpy as jnp
from jax import lax
from jax.experimental import pallas as pl
from jax.experimental.pallas import tpu as pltpu
from jax.experimental.pallas import tpu_sc as plsc

f32 = jnp.float32
bf16 = jnp.bfloat16

D_MODEL = 1024
CONV_WIDTH = 512
SSM_WIDTH = 512
SSM_GROUP = 16
SSM_GROUPS = 32
SSM_STATE = 64
N_STATES = SSM_GROUPS * SSM_STATE
XATTN_HEADS = 4
XATTN_HEAD_DIM = 128
XATTN_WIDTH = 512
D_FF = 4096
GATE_COLS = 3 * D_MODEL
IN_COLS = GATE_COLS + 3 * CONV_WIDTH + SSM_WIDTH + XATTN_WIDTH
ALPHA = 2.0 ** 0.25
LN_EPS = 1e-5
ADAM_LR = 0.001
ADAM_B1 = 0.9
ADAM_B2 = 0.999
ADAM_EPS = 1e-08
ADAM_WD = 0.01
ADAM_STEP = 10

N_SEG = 8
SCAN_ROWS = 512
LANE_STRIP = 512
TM = 512
VMEM_LIMIT = 48 * 1024 * 1024
MESH = pl.DeviceIdType.MESH

NT_DIMS = (((1,), (1,)), ((), ()))
TN_DIMS = (((0,), (0,)), ((), ()))


def _params():
    return pltpu.CompilerParams(vmem_limit_bytes=VMEM_LIMIT)


def _full(shape):
    n = len(shape)
    return pl.BlockSpec(shape, lambda *_: (0,) * n)


def _rows(tm, w, cb=0):
    return pl.BlockSpec((tm, w), lambda i: (i, cb))


def _sig(x):
    return 1.0 / (1.0 + jnp.exp(-x))


HBM_SPEC = pl.BlockSpec(memory_space=pl.ANY)
RIDE_PEERS = {"gather": 3, "gather2": 6, "scatter": 3, "all": 7, "pair": 1}


def _xy_peers(x, y):
    return [(1 - x, y), (x, 1 - y), (1 - x, 1 - y)]


def _ride_copies(kind, src, dst, send_sems, recv_sems, local_sem):
    x, y, c = lax.axis_index("x"), lax.axis_index("y"), lax.axis_index("c")
    me = 2 * x + y
    if kind == "all":
        flips = [(fx, fy, fc) for fx in (0, 1) for fy in (0, 1) for fc in (0, 1)][1:]
        peers = [(x ^ fx, y ^ fy, c ^ fc) for fx, fy, fc in flips]
        slot = lambda p: 4 * p[0] + 2 * p[1] + p[2]
        mine = slot((x, y, c))
    else:
        peers = [(px, py, c) for px, py in _xy_peers(x, y)]
        slot = lambda p: 2 * p[0] + p[1]
        mine = me

    def remote(k, s, d):
        return pltpu.make_async_remote_copy(src_ref=s, dst_ref=d, send_sem=send_sems.at[k], recv_sem=recv_sems.at[k],
                                            device_id=peers[k], device_id_type=MESH)

    if kind == "pair":
        peers = [(x, y, 1 - c)]
        return None, [remote(0, src, dst)], [remote(0, src, dst)]
    if kind == "scatter":
        local = pltpu.make_async_copy(src.at[me], dst.at[me], local_sem)
        sends = [remote(k, src.at[slot(p)], dst.at[me]) for k, p in enumerate(peers)]
        lands = [remote(k, src.at[me], dst.at[slot(p)]) for k, p in enumerate(peers)]
    else:
        local = pltpu.make_async_copy(src, dst.at[mine], local_sem)
        sends = [remote(k, src, dst.at[mine]) for k, p in enumerate(peers)]
        lands = [remote(k, src, dst.at[slot(p)]) for k, p in enumerate(peers)]
    return local, sends, lands


def _ride_shape(kind, src):
    lead = {"gather": (4,), "gather2": (4,), "scatter": (), "all": (8,), "pair": ()}[kind]
    return jax.ShapeDtypeStruct(lead + src.shape, src.dtype)


def _two_level(src, dst, send_sems, recv_sems, local_sem):
    x, y, c = lax.axis_index("x"), lax.axis_index("y"), lax.axis_index("c")
    me = 2 * x + y
    h = src.shape[0] // 2
    mine = pl.ds(pl.multiple_of(c * h, 16), h)
    theirs = pl.ds(pl.multiple_of((1 - c) * h, 16), h)
    peers = _xy_peers(x, y)
    slots = [2 * px + py for px, py in peers]

    def over_ici(k, d):
        return pltpu.make_async_remote_copy(src_ref=src.at[mine], dst_ref=d, send_sem=send_sems.at[k],
                                            recv_sem=recv_sems.at[k], device_id=(*peers[k], c), device_id_type=MESH)

    def over_d2d(k, rows):
        blk = dst.at[slots[k], rows]
        return pltpu.make_async_remote_copy(src_ref=blk, dst_ref=blk, send_sem=send_sems.at[3 + k],
                                            recv_sem=recv_sems.at[3 + k], device_id=(x, y, 1 - c), device_id_type=MESH)

    return (pltpu.make_async_copy(src, dst.at[me], local_sem),
            [over_ici(k, dst.at[me, mine]) for k in range(3)], [over_ici(k, dst.at[slots[k], mine]) for k in range(3)],
            [over_d2d(k, mine) for k in range(3)], [over_d2d(k, theirs) for k in range(3)])


def _pcall(body, args, *, name, grid, in_specs, out_specs, out_shape, scratch_shapes=(), rides=(), aliases=None,
           prefetch=()):
    n_in, n_out, n_scr, nr, npf = len(in_specs), len(out_specs), len(scratch_shapes), len(rides), len(prefetch)
    kinds = [k for k, _ in rides]

    def wrapped(*refs):
        pre, refs = refs[:npf], refs[npf:]
        ins, rsrc = refs[:n_in], refs[n_in:n_in + nr]
        outs = refs[n_in + nr:n_in + nr + n_out]
        rdst = refs[n_in + nr + n_out:n_in + 2 * nr + n_out]
        scr = refs[n_in + 2 * nr + n_out:n_in + 2 * nr + n_out + n_scr]
        sems = refs[n_in + 2 * nr + n_out + n_scr:]

        def start():
            for r, kind in enumerate(kinds):
                if kind == "gather2":
                    local, sends, _, _, _ = _two_level(rsrc[r], rdst[r], *sems[3 * r:3 * r + 3])
                else:
                    local, sends, _ = _ride_copies(kind, rsrc[r], rdst[r], *sems[3 * r:3 * r + 3])
                if local is not None:
                    local.start()
                for cp in sends:
                    cp.start()

        def finish():
            for r, kind in enumerate(kinds):
                if kind == "gather2":
                    local, sends, ici_lands, forwards, lands = _two_level(rsrc[r], rdst[r], *sems[3 * r:3 * r + 3])
                    for k in range(3):
                        ici_lands[k].wait_recv()
                        forwards[k].start()
                    sends = sends + forwards
                else:
                    local, sends, lands = _ride_copies(kind, rsrc[r], rdst[r], *sems[3 * r:3 * r + 3])
                for cp in lands:
                    cp.wait_recv()
                for cp in sends:
                    cp.wait_send()
                if local is not None:
                    local.wait()

        if nr and grid:
            ids = [pl.program_id(a) for a in range(len(grid))]
            first = functools.reduce(jnp.logical_and, [i == 0 for i in ids])
            last = functools.reduce(jnp.logical_and, [i == g - 1 for i, g in zip(ids, grid)])
            pl.when(first)(start)
        elif nr:
            start()
        body(*pre, *ins, *outs, *scr)
        if nr and grid:
            pl.when(last)(finish)
        elif nr:
            finish()

    sems = []
    for kind in kinds:
        n = RIDE_PEERS[kind]
        sems += [pltpu.SemaphoreType.DMA((n,)), pltpu.SemaphoreType.DMA((n,)), pltpu.SemaphoreType.DMA(())]
    all_in = list(in_specs) + [HBM_SPEC] * nr
    all_out = list(out_specs) + [HBM_SPEC] * nr
    all_scratch = list(scratch_shapes) + sems
    if npf:
        how = dict(grid_spec=pltpu.PrefetchScalarGridSpec(num_scalar_prefetch=npf, grid=grid, in_specs=all_in,
                                                          out_specs=all_out, scratch_shapes=all_scratch))
    else:
        how = dict(grid=grid, in_specs=all_in, out_specs=all_out, scratch_shapes=all_scratch)
    res = pl.pallas_call(
        wrapped, name=name, out_shape=list(out_shape) + [_ride_shape(k, s) for k, s in rides],
        input_output_aliases=aliases or {}, compiler_params=_params(), **how)(
            *prefetch, *args, *[s for _, s in rides])
    return list(res[:n_out]), list(res[n_out:])


def _bf(v):
    return v if v.dtype == bf16 else v.astype(bf16)


def mm_fwd_small(a, w3, name, *, tm=1024, out_dtype=f32):
    M, K = a.shape
    J, _, n = w3.shape
    tm = min(tm, M)

    def body(a_ref, w_ref, o_ref):
        av = _bf(a_ref[...])
        for j in range(J):
            o_ref[:, j * n:(j + 1) * n] = jnp.dot(av, w_ref[j], preferred_element_type=f32).astype(out_dtype)

    return pl.pallas_call(
        body, name=name, grid=(M // tm,), in_specs=[_rows(tm, K), _full((J, K, n))], out_specs=_rows(tm, J * n),
        out_shape=jax.ShapeDtypeStruct((M, J * n), out_dtype), compiler_params=_params())(a, w3)


def mm_bwd_w_small(a, dy, J, name, *, tm=2048):
    M, K = a.shape
    n = dy.shape[1] // J
    tm = min(tm, M)
    ns = M // tm

    def body(a_ref, dy_ref, o_ref, acc_ref):
        s = pl.program_id(0)
        part = lax.dot_general(_bf(a_ref[...]), _bf(dy_ref[...]), TN_DIMS, preferred_element_type=f32)

        @pl.when(s == 0)
        def _():
            acc_ref[...] = part

        @pl.when(s > 0)
        def _():
            acc_ref[...] += part

        @pl.when(s == ns - 1)
        def _():
            for j in range(J):
                o_ref[j] = acc_ref[:, j * n:(j + 1) * n].astype(bf16)

    return pl.pallas_call(
        body, name=name, grid=(ns,), in_specs=[_rows(tm, K), _rows(tm, J * n)], out_specs=_full((J, K, n)),
        out_shape=jax.ShapeDtypeStruct((J, K, n), bf16), scratch_shapes=[pltpu.VMEM((K, J * n), f32)],
        compiler_params=_params())(a, dy)


def mm_fwd(a, w3, name, *, tm=1024, tn=None, bias=None, outs=(f32,), first=None, second=None, rides=()):
    M, K = a.shape
    J, _, n = w3.shape
    tm = min(tm, M)
    tn = tn or n
    nl = n // tn
    nb = 0 if bias is None else 1

    def body(*refs):
        a_ref, w_ref = refs[0], refs[1]
        acc = jnp.dot(_bf(a_ref[...]), w_ref[0], preferred_element_type=f32)
        if bias is not None:
            acc = acc + refs[2][...]
        refs[2 + nb][...] = (acc if first is None else first(acc)).astype(outs[0])
        if len(outs) > 1:
            refs[3 + nb][...] = second(acc).astype(outs[1])

    in_specs = [pl.BlockSpec((tm, K), lambda j, l, i: (i, 0)),
                pl.BlockSpec((1, K, tn), lambda j, l, i: (j, 0, l))]
    args = [a, w3]
    if bias is not None:
        in_specs.append(pl.BlockSpec((1, tn), lambda j, l, i: (0, j * nl + l)))
        args.append(bias)
    res, landed = _pcall(
        body, args, name=name, grid=(J, nl, M // tm), in_specs=in_specs,
        out_specs=[pl.BlockSpec((tm, tn), lambda j, l, i: (i, j * nl + l)) for _ in outs],
        out_shape=[jax.ShapeDtypeStruct((M, J * n), dt) for dt in outs], rides=rides)
    res = res if len(outs) > 1 else res[0]
    return (res, landed) if rides else res


def proj_own(x, w_own, me, n_blocks, *, tm=1024, rides=()):
    M, K = x.shape
    n = w_own.shape[1]

    def body(me_ref, x_ref, w_ref, o_ref, xb_ref):
        av = _bf(x_ref[...])
        xb_ref[...] = av
        o_ref[...] = jnp.dot(av, w_ref[...], preferred_element_type=f32).astype(bf16)

    (proj, x_bf), landed = _pcall(
        body, [x, w_own], name="proj_own", grid=(M // tm,), prefetch=(me,),
        in_specs=[pl.BlockSpec((tm, K), lambda i, me_ref: (i, 0)), pl.BlockSpec((K, n), lambda i, me_ref: (0, 0))],
        out_specs=[pl.BlockSpec((tm, n), lambda i, me_ref: (i, me_ref[0])),
                   pl.BlockSpec((tm, K), lambda i, me_ref: (i, 0))],
        out_shape=[jax.ShapeDtypeStruct((M, n_blocks * n), bf16), jax.ShapeDtypeStruct((M, K), bf16)], rides=rides)
    return proj, x_bf, landed


def proj_rest(x_bf, w3, me, proj, *, tm=1024, rides=()):
    M, K = x_bf.shape
    J, _, n = w3.shape

    def body(me_ref, x_ref, w_ref, _, o_ref):
        o_ref[...] = jnp.dot(x_ref[...], w_ref[0], preferred_element_type=f32).astype(bf16)

    (proj,), landed = _pcall(
        body, [x_bf, w3, proj], name="proj_rest", grid=(J - 1, M // tm), prefetch=(me,),
        in_specs=[pl.BlockSpec((tm, K), lambda k, i, me_ref: (i, 0)),
                  pl.BlockSpec((1, K, n), lambda k, i, me_ref: (me_ref[0] ^ (k + 1), 0, 0)), HBM_SPEC],
        out_specs=[pl.BlockSpec((tm, n), lambda k, i, me_ref: (i, me_ref[0] ^ (k + 1)))],
        out_shape=[jax.ShapeDtypeStruct(proj.shape, bf16)], aliases={3: 0}, rides=rides)
    return proj, landed


def mm_bwd_x(dy, w3, name, *, tm=TM, epi=None, extras=(), colsum=False, out_dtype=f32, rides=()):
    M = dy.shape[0]
    J, K, n = w3.shape
    tm = min(tm, M)
    nex = len(extras)

    def body(*refs):
        dy_ref, w_hbm = refs[0], refs[1]
        ex = refs[2:2 + nex]
        o_ref = refs[2 + nex]
        w_ref = refs[-1]
        i = pl.program_id(0)

        @pl.when(i == 0)
        def _():
            pltpu.sync_copy(w_hbm, w_ref)

        acc = None
        for j in range(J):
            part = lax.dot_general(_bf(dy_ref[:, j * n:(j + 1) * n]), w_ref[j], NT_DIMS, preferred_element_type=f32)
            acc = part if acc is None else acc + part
        if epi is not None:
            acc = epi(acc, *[e[...] for e in ex])
        o_ref[...] = acc.astype(out_dtype)
        if colsum:
            s_ref = refs[3 + nex]
            cs = jnp.sum(acc, axis=0, keepdims=True)

            @pl.when(i == 0)
            def _():
                s_ref[...] = cs

            @pl.when(i > 0)
            def _():
                s_ref[...] += cs

    in_specs = [pl.BlockSpec((tm, J * n), lambda i: (i, 0)), pl.BlockSpec(memory_space=pl.ANY)]
    in_specs += [pl.BlockSpec((tm, K), lambda i: (i, 0)) for _ in extras]
    out_specs = [pl.BlockSpec((tm, K), lambda i: (i, 0))]
    out_shape = [jax.ShapeDtypeStruct((M, K), out_dtype)]
    if colsum:
        out_specs.append(pl.BlockSpec((1, K), lambda i: (0, 0)))
        out_shape.append(jax.ShapeDtypeStruct((1, K), f32))
    res, landed = _pcall(body, [dy, w3, *extras], name=name, grid=(M // tm,), in_specs=in_specs, out_specs=out_specs,
                         out_shape=out_shape, scratch_shapes=[pltpu.VMEM((J, K, n), bf16)], rides=rides)
    res = res if colsum else res[0]
    return (res, landed) if rides else res


def mm_rows_fused(a, w3, name, *, transposed, tm, epi, extras=(), consts=(), outs=(f32,), nsums=0, sum_widths=None,
                  rides=()):
    pro = a[0] if isinstance(a, tuple) else None
    J, K, n = w3.shape
    W = K if transposed else n
    ka = J * n if transposed else K
    if pro is not None:
        a_rows = [e if isinstance(e, tuple) else (e, ka, 0) for e in a[1]]
        a_consts = list(a[2])
    else:
        a_rows, a_consts = [(a, ka, 0)], []
    M = a_rows[0][0].shape[0]
    tm = min(tm, M)
    na = len(a_rows) + len(a_consts)
    nex, nco = len(extras), len(consts)
    if pro is not None:
        outs = (bf16 if ka == W else (bf16, ka, ka),) + tuple(outs)
    nout = len(outs)
    nsums = len(sum_widths) if sum_widths is not None else nsums
    out_dtypes = [o[0] if isinstance(o, tuple) else o for o in outs]

    def body(*refs):
        a_refs, w_hbm = refs[:na], refs[na]
        ex = refs[na + 1:na + 1 + nex]
        co = refs[na + 1 + nex:na + 1 + nex + nco]
        o_refs = refs[na + 1 + nex + nco:na + 1 + nex + nco + nout]
        s_refs = refs[na + 1 + nex + nco + nout:na + 1 + nex + nco + nout + nsums]
        w_ref = refs[-1]
        i = pl.program_id(0)

        @pl.when(i == 0)
        def _():
            pltpu.sync_copy(w_hbm, w_ref)

        av = _bf(a_refs[0][...] if pro is None else pro(*[r[...] for r in a_refs]))
        if transposed:
            acc = None
            for j in range(J):
                part = lax.dot_general(av[:, j * n:(j + 1) * n], w_ref[j], NT_DIMS, preferred_element_type=f32)
                acc = part if acc is None else acc + part
        else:
            acc = jnp.dot(av, w_ref[0], preferred_element_type=f32)
        rows, sums = epi(acc, *[e[...] for e in ex], *[c[...] for c in co])
        if pro is not None:
            rows = (av,) + tuple(rows)
        for o_ref, v, dt in zip(o_refs, rows, out_dtypes):
            o_ref[...] = v.astype(dt)
        for s_ref, v in zip(s_refs, sums):
            @pl.when(i == 0)
            def _(s_ref=s_ref, v=v):
                s_ref[...] = v

            @pl.when(i > 0)
            def _(s_ref=s_ref, v=v):
                s_ref[...] += v

    extras = [e if isinstance(e, tuple) else (e, W, 0) for e in extras]
    outs_full = [o if isinstance(o, tuple) else (o, W, W) for o in outs]
    widths = list(sum_widths) if sum_widths is not None else [W] * nsums
    in_specs = [_rows(tm, w, cb) for _, w, cb in a_rows] + [_full(c.shape) for c in a_consts]
    in_specs += [pl.BlockSpec(memory_space=pl.ANY)]
    in_specs += [_rows(tm, w, cb) for _, w, cb in extras] + [_full(c.shape) for c in consts]
    out_specs = [_rows(tm, w) for _, w, _ in outs_full] + [_full((1, w)) for w in widths]
    out_shape = ([jax.ShapeDtypeStruct((M, cols), dt) for dt, _, cols in outs_full]
                 + [jax.ShapeDtypeStruct((1, w), f32) for w in widths])
    args = [e for e, _, _ in a_rows] + a_consts + [w3] + [e for e, _, _ in extras] + list(consts)
    res, landed = _pcall(body, args, name=name, grid=(M // tm,), in_specs=in_specs, out_specs=out_specs,
                         out_shape=out_shape, scratch_shapes=[pltpu.VMEM((J, K, n), bf16)], rides=rides)
    return (res, landed) if rides else res


def mm_bwd_w(a, dy, J, name, *, tm=2048, tn=None, tk=None, rides=()):
    M, K = a.shape
    n = dy.shape[1] // J
    tm = min(tm, M)
    tn = tn or n
    tk = tk or K
    nl = n // tn
    nk = K // tk
    ns = M // tm

    def body(a_ref, dy_ref, o_ref, acc_ref):
        s = pl.program_id(3)
        part = lax.dot_general(_bf(a_ref[...]), _bf(dy_ref[...]), TN_DIMS, preferred_element_type=f32)

        @pl.when(s == 0)
        def _():
            acc_ref[...] = part

        @pl.when(s > 0)
        def _():
            acc_ref[...] += part

        @pl.when(s == ns - 1)
        def _():
            o_ref[0] = acc_ref[...].astype(bf16)

    res, landed = _pcall(
        body, [a, dy], name=name, grid=(J, nl, nk, ns),
        in_specs=[pl.BlockSpec((tm, tk), lambda j, l, k, s: (s, k)),
                  pl.BlockSpec((tm, tn), lambda j, l, k, s: (s, j * nl + l))],
        out_specs=[pl.BlockSpec((1, tk, tn), lambda j, l, k, s: (j, k, l))],
        out_shape=[jax.ShapeDtypeStruct((J, K, n), bf16)],
        scratch_shapes=[pltpu.VMEM((tk, tn), f32)], rides=rides)
    return (res[0], landed) if rides else res[0]


def _relu2(v):
    r = jnp.maximum(v, 0.0)
    return r * r


HALO = 16


def _shift_down(z, k, halo):
    r = lax.broadcasted_iota(jnp.int32, z.shape, 0)
    y = pltpu.roll(z, k, 0)
    for q in range(k):
        y = jnp.where(r == q, halo[HALO - k + q:HALO - k + q + 1, :], y)
    return y


def _shift_up(z, k, halo):
    tm = z.shape[0]
    r = lax.broadcasted_iota(jnp.int32, z.shape, 0)
    y = pltpu.roll(z, tm - k, 0)
    for q in range(k):
        y = jnp.where(r == tm - k + q, halo[q:q + 1, :], y)
    return y


def _prev_halo(tm, cb):
    return pl.BlockSpec((HALO, CONV_WIDTH), lambda i: (jnp.maximum(i * (tm // HALO) - 1, 0), cb))


def _next_halo(tm, cb, nblk):
    return pl.BlockSpec((HALO, CONV_WIDTH), lambda i: (jnp.minimum((i + 1) * (tm // HALO), nblk - 1), cb))


def _f32(ref):
    return ref[...].astype(f32)


def conv_fwd(proj, conv_w, wco3, tm=TM):
    S = proj.shape[0]
    J, _, n = wco3.shape

    def body(cb_ref, cc_ref, ch_ref, cch_ref, chh_ref, w_ref, wo_ref, a_ref, y_ref):
        i = pl.program_id(0)
        z = _f32(cc_ref) * _f32(ch_ref)
        zh = jnp.where(i == 0, 0.0, _f32(cch_ref) * _f32(chh_ref))
        w = w_ref[...]
        dwz = w[0:1, :] * _shift_down(z, 2, zh) + w[1:2, :] * _shift_down(z, 1, zh) + w[2:3, :] * z
        a = (_f32(cb_ref) * dwz).astype(bf16)
        a_ref[...] = a
        for j in range(J):
            y_ref[:, j * n:(j + 1) * n] = jnp.dot(a, wo_ref[j], preferred_element_type=f32).astype(bf16)

    return pl.pallas_call(
        body, name="conv_fwd", grid=(S // tm,),
        in_specs=[_rows(tm, CONV_WIDTH, 6), _rows(tm, CONV_WIDTH, 7), _rows(tm, CONV_WIDTH, 8),
                  _prev_halo(tm, 7), _prev_halo(tm, 8), _full((3, CONV_WIDTH)), _full(wco3.shape)],
        out_specs=[_rows(tm, CONV_WIDTH), _rows(tm, J * n)],
        out_shape=[jax.ShapeDtypeStruct((S, CONV_WIDTH), bf16), jax.ShapeDtypeStruct((S, J * n), bf16)],
        compiler_params=_params())(proj, proj, proj, proj, proj, conv_w, wco3)


def conv_bwd(dy_a, a_conv, wco3, proj, conv_w, dproj, tm=TM):
    S = proj.shape[0]
    nt = S // tm
    J, _, n = wco3.shape

    def body(dy_ref, dyn_ref, a_ref, wo_ref, cb_ref, cc_ref, ch_ref, cch_ref, chh_ref, cbn_ref, w_ref, _, o_ref,
             dw_ref, dwo_ref, acc_ref):
        i = pl.program_id(0)

        def times_wo_t(dy):
            out = None
            for j in range(J):
                part = lax.dot_general(dy[:, j * n:(j + 1) * n], wo_ref[j], NT_DIMS, preferred_element_type=f32)
                out = part if out is None else out + part
            return out

        dyv = dy_ref[...]
        dav, dan = times_wo_t(dyv), times_wo_t(dyn_ref[...])
        part = lax.dot_general(a_ref[...], dyv, TN_DIMS, preferred_element_type=f32)

        @pl.when(i == 0)
        def _():
            acc_ref[...] = part

        @pl.when(i > 0)
        def _():
            acc_ref[...] += part

        @pl.when(i == nt - 1)
        def _():
            for j in range(J):
                dwo_ref[j] = acc_ref[:, j * n:(j + 1) * n].astype(bf16)

        cc, ch, cb = _f32(cc_ref), _f32(ch_ref), _f32(cb_ref)
        z = cc * ch
        zh = jnp.where(i == 0, 0.0, _f32(cch_ref) * _f32(chh_ref))
        w = w_ref[...]
        z1 = _shift_down(z, 1, zh)
        z2 = _shift_down(z, 2, zh)
        dwz = w[0:1, :] * z2 + w[1:2, :] * z1 + w[2:3, :] * z
        g = dav * cb
        gn = jnp.where(i == nt - 1, 0.0, dan * _f32(cbn_ref))
        dz = w[2:3, :] * g + w[1:2, :] * _shift_up(g, 1, gn) + w[0:1, :] * _shift_up(g, 2, gn)
        o_ref[:, 0:CONV_WIDTH] = (dav * dwz).astype(bf16)
        o_ref[:, CONV_WIDTH:2 * CONV_WIDTH] = (dz * ch).astype(bf16)
        o_ref[:, 2 * CONV_WIDTH:3 * CONV_WIDTH] = (dz * cc).astype(bf16)

        @pl.when(i == 0)
        def _():
            dw_ref[...] = jnp.zeros_like(dw_ref)

        dw_ref[0:1, :] += jnp.sum(g * z2, axis=0, keepdims=True)
        dw_ref[1:2, :] += jnp.sum(g * z1, axis=0, keepdims=True)
        dw_ref[2:3, :] += jnp.sum(g * z, axis=0, keepdims=True)

    return pl.pallas_call(
        body, name="conv_bwd", grid=(nt,),
        in_specs=[_rows(tm, J * n),
                  pl.BlockSpec((HALO, J * n), lambda i: (jnp.minimum((i + 1) * (tm // HALO), S // HALO - 1), 0)),
                  _rows(tm, CONV_WIDTH), _full(wco3.shape),
                  _rows(tm, CONV_WIDTH, 6), _rows(tm, CONV_WIDTH, 7), _rows(tm, CONV_WIDTH, 8),
                  _prev_halo(tm, 7), _prev_halo(tm, 8), _next_halo(tm, 6, S // HALO), _full((3, CONV_WIDTH)),
                  pl.BlockSpec(memory_space=pl.ANY)],
        out_specs=[_rows(tm, 3 * CONV_WIDTH, GATE_COLS // (3 * CONV_WIDTH)), _full((3, CONV_WIDTH)),
                   _full(wco3.shape)],
        out_shape=[jax.ShapeDtypeStruct(dproj.shape, bf16), jax.ShapeDtypeStruct((3, CONV_WIDTH), f32),
                   jax.ShapeDtypeStruct(wco3.shape, bf16)],
        scratch_shapes=[pltpu.VMEM((CONV_WIDTH, J * n), f32)], input_output_aliases={11: 0},
        compiler_params=_params())(dy_a, dy_a, a_conv, wco3, proj, proj, proj, proj, proj, proj, conv_w, dproj)


def _cmul(ar, ai, br, bi):
    return ar * br - ai * bi, ar * bi + ai * br


def ssm_prep(lr, li, ldt, bt_re, bt_im):
    def body(lr_ref, li_ref, ldt_ref, br_ref, bi_ref, ar_ref, ai_ref, kr_ref, ki_ref, bbr_ref, bbi_ref):
        lrv, liv = lr_ref[...], li_ref[...]
        dt = jnp.exp(ldt_ref[...])
        mag = jnp.exp(lrv * dt)
        ar = mag * jnp.cos(liv * dt)
        ai = mag * jnp.sin(liv * dt)
        den = lrv * lrv + liv * liv
        nr = ar - 1.0
        kr = (nr * lrv + ai * liv) / den
        ki = (ai * lrv - nr * liv) / den
        ar_ref[...], ai_ref[...], kr_ref[...], ki_ref[...] = ar, ai, kr, ki
        bbr_ref[...] = kr * br_ref[...] - ki * bi_ref[...]
        bbi_ref[...] = kr * bi_ref[...] + ki * br_ref[...]

    v = jax.ShapeDtypeStruct((1, N_STATES), f32)
    m = jax.ShapeDtypeStruct((SSM_GROUP, N_STATES), f32)
    return pl.pallas_call(body, name="ssm_prep", out_shape=[v, v, v, v, m, m])(lr, li, ldt, bt_re, bt_im)


def _pow_segment(ar, ai, seg_len):
    pr, pi = ar, ai
    for _ in range(int(math.log2(seg_len))):
        pr, pi = _cmul(pr, pi, pr, pi)
    return pr, pi


GELU_K = math.sqrt(2.0 / math.pi)
GELU_C = 0.044715


def _gelu(v):
    return 0.5 * v * (1.0 + jnp.tanh(GELU_K * (v + GELU_C * v * v * v)))


def _gelu_grad(v):
    t = jnp.tanh(GELU_K * (v + GELU_C * v * v * v))
    return 0.5 * (1.0 + t) + 0.5 * v * (1.0 - t * t) * GELU_K * (1.0 + 3.0 * GELU_C * v * v)


def ssm_scan_fwd(u_perm, bd_r, bd_i, ar, ai, *, final=None, rides=()):
    S = u_perm.shape[0]
    R = SCAN_ROWS
    nblk = S // R
    seg_len = S // N_SEG
    nstrip = N_STATES // LANE_STRIP
    store = final is not None

    def body(*refs):
        if store:
            (u_ref, bdr_ref, bdi_ref, ar_ref, ai_ref, cdr_ref, cdi_ref, d_ref, ir_ref, ii_ref,
             y_ref, ys_ref, sr_ref, si_ref, bur, bui, car, cai) = refs
        else:
            u_ref, bdr_ref, bdi_ref, ar_ref, ai_ref, or_ref, oi_ref, bur, bui, car, cai = refs
        i = pl.program_id(0)
        ub = _bf(u_ref[...])
        u = ub.astype(f32)
        for cb in range(4):
            us = ub[:, cb * 128:(cb + 1) * 128]
            bur[:, cb * 512:(cb + 1) * 512] = jnp.dot(us, bdr_ref[cb], preferred_element_type=f32)
            bui[:, cb * 512:(cb + 1) * 512] = jnp.dot(us, bdi_ref[cb], preferred_element_type=f32)

        @pl.when(i == 0)
        def _():
            if store:
                car[...] = ir_ref[...]
                cai[...] = ii_ref[...]
            else:
                car[...] = jnp.zeros_like(car)
                cai[...] = jnp.zeros_like(cai)

        for ls in range(nstrip):
            lanes = pl.ds(ls * LANE_STRIP, LANE_STRIP)
            a_r = jnp.broadcast_to(ar_ref[:, lanes], (N_SEG, LANE_STRIP))
            a_i = jnp.broadcast_to(ai_ref[:, lanes], (N_SEG, LANE_STRIP))

            def step(t, carry, lanes=lanes, a_r=a_r, a_i=a_i):
                s_r, s_i = carry
                row = pl.multiple_of(t * 8, 8)
                n_r = a_r * s_r - a_i * s_i + bur[pl.ds(row, 8), lanes]
                n_i = a_r * s_i + a_i * s_r + bui[pl.ds(row, 8), lanes]
                if store:
                    sr_ref[pl.ds(row, 8), lanes] = n_r
                    si_ref[pl.ds(row, 8), lanes] = n_i
                return n_r, n_i

            e_r, e_i = lax.fori_loop(0, R // 8, step, (car[:, lanes], cai[:, lanes]), unroll=True)
            car[:, lanes] = e_r
            cai[:, lanes] = e_i

        if store:
            for cb in range(4):
                st_r = sr_ref[:, cb * 512:(cb + 1) * 512].astype(bf16)
                st_i = si_ref[:, cb * 512:(cb + 1) * 512].astype(bf16)
                y = (jnp.dot(st_r, cdr_ref[cb], preferred_element_type=f32)
                     - jnp.dot(st_i, cdi_ref[cb], preferred_element_type=f32))
                cols = slice(cb * 128, (cb + 1) * 128)
                y = y + d_ref[:, cols] * u[:, cols]
                y_ref[:, cols] = y
                ys_ref[:, cols] = _gelu(y).astype(bf16)
        else:
            @pl.when(i == nblk - 1)
            def _():
                p_r, p_i = _pow_segment(ar_ref[...], ai_ref[...], seg_len)
                t_r, t_i = car[0:1, :], cai[0:1, :]
                or_ref[0:1, :] = jnp.zeros((1, N_STATES), f32)
                oi_ref[0:1, :] = jnp.zeros((1, N_STATES), f32)
                for j in range(1, N_SEG):
                    or_ref[j:j + 1, :] = t_r
                    oi_ref[j:j + 1, :] = t_i
                    m_r, m_i = _cmul(p_r, p_i, t_r, t_i)
                    t_r, t_i = car[j:j + 1, :] + m_r, cai[j:j + 1, :] + m_i

    blk = lambda w: pl.BlockSpec((R, w), lambda i: (i, 0))
    in_specs = [blk(SSM_WIDTH), _full((4, 128, 512)), _full((4, 128, 512)), _full((1, N_STATES)), _full((1, N_STATES))]
    args = [u_perm, bd_r, bd_i, ar, ai]
    scratch = [pltpu.VMEM((R, N_STATES), f32), pltpu.VMEM((R, N_STATES), f32),
               pltpu.VMEM((N_SEG, N_STATES), f32), pltpu.VMEM((N_SEG, N_STATES), f32)]
    if store:
        in_specs += [_full((4, 512, 128)), _full((4, 512, 128)), _full((1, SSM_WIDTH)),
                     _full((N_SEG, N_STATES)), _full((N_SEG, N_STATES))]
        args += list(final)
        out_specs = [blk(SSM_WIDTH), blk(SSM_WIDTH), blk(N_STATES), blk(N_STATES)]
        out_shape = [jax.ShapeDtypeStruct((S, SSM_WIDTH), f32), jax.ShapeDtypeStruct((S, SSM_WIDTH), bf16),
                     jax.ShapeDtypeStruct((S, N_STATES), f32), jax.ShapeDtypeStruct((S, N_STATES), f32)]
        name = "ssm_scan_fwd"
    else:
        out_specs = [_full((N_SEG, N_STATES)), _full((N_SEG, N_STATES))]
        out_shape = [jax.ShapeDtypeStruct((N_SEG, N_STATES), f32)] * 2
        name = "ssm_scan_fwd_carry"
    res, landed = _pcall(body, args, name=name, grid=(nblk,), in_specs=in_specs, out_specs=out_specs,
                         out_shape=out_shape, scratch_shapes=scratch, rides=rides)
    return (res, landed) if rides else res


def ssm_scan_bwd(dys_perm, y_perm, cdt_r, cdt_i, ar, ai, *, final=None, rides=()):
    S = dys_perm.shape[0]
    R = SCAN_ROWS
    nblk = S // R
    seg_len = S // N_SEG
    store = final is not None
    strip = LANE_STRIP // 2 if store else LANE_STRIP
    nstrip = N_STATES // strip

    def body(*refs):
        if store:
            (dys_ref, y_ref, cdr_ref, cdi_ref, ar_ref, ai_ref, u_ref, sr_ref, si_ref, bdr_ref, bdi_ref, d_ref, ir_ref,
             ii_ref, du_ref, dbr_ref, dbi_ref, dcr_ref, dci_ref, dar_ref, dai_ref, dd_ref, dsr, dsi, lmr, lmi, car,
             cai) = refs
        else:
            dys_ref, y_ref, cdr_ref, cdi_ref, ar_ref, ai_ref, or_ref, oi_ref, dsr, dsi, car, cai = refs
        i = pl.program_id(0)
        dy = dys_ref[...] * _gelu_grad(y_ref[...])
        dyb = dy.astype(bf16)
        for cb in range(4):
            ds_ = dyb[:, cb * 128:(cb + 1) * 128]
            dsr[:, cb * 512:(cb + 1) * 512] = jnp.dot(ds_, cdr_ref[cb], preferred_element_type=f32)
            dsi[:, cb * 512:(cb + 1) * 512] = -jnp.dot(ds_, cdi_ref[cb], preferred_element_type=f32)

        @pl.when(i == 0)
        def _():
            if store:
                car[...] = ir_ref[...]
                cai[...] = ii_ref[...]
                dar_ref[...] = jnp.zeros_like(dar_ref)
                dai_ref[...] = jnp.zeros_like(dai_ref)
                dbr_ref[...] = jnp.zeros_like(dbr_ref)
                dbi_ref[...] = jnp.zeros_like(dbi_ref)
                dcr_ref[...] = jnp.zeros_like(dcr_ref)
                dci_ref[...] = jnp.zeros_like(dci_ref)
                dd_ref[...] = jnp.zeros_like(dd_ref)
            else:
                car[...] = jnp.zeros_like(car)
                cai[...] = jnp.zeros_like(cai)

        for ls in range(nstrip):
            lanes = pl.ds(ls * strip, strip)
            a_r = jnp.broadcast_to(ar_ref[:, lanes], (N_SEG, strip))
            a_i = jnp.broadcast_to(ai_ref[:, lanes], (N_SEG, strip))
            zero = jnp.zeros((N_SEG, strip), f32)

            def step(k, carry, lanes=lanes, a_r=a_r, a_i=a_i):
                l_r, l_i, g_r, g_i = carry
                row = pl.multiple_of((R // 8 - 1 - k) * 8, 8)
                if store:
                    s_r = sr_ref[pl.ds(row, 8), lanes]
                    s_i = si_ref[pl.ds(row, 8), lanes]
                    g_r = g_r + l_r * s_r + l_i * s_i
                    g_i = g_i + l_i * s_r - l_r * s_i
                n_r = dsr[pl.ds(row, 8), lanes] + a_r * l_r + a_i * l_i
                n_i = dsi[pl.ds(row, 8), lanes] + a_r * l_i - a_i * l_r
                if store:
                    lmr[pl.ds(row, 8), lanes] = n_r
                    lmi[pl.ds(row, 8), lanes] = n_i
                return n_r, n_i, g_r, g_i

            e_r, e_i, g_r, g_i = lax.fori_loop(0, R // 8, step, (car[:, lanes], cai[:, lanes], zero, zero),
                                               unroll=True)
            car[:, lanes] = e_r
            cai[:, lanes] = e_i
            if store:
                dar_ref[:, lanes] += g_r
                dai_ref[:, lanes] += g_i

        if store:
            ub = _bf(u_ref[...])
            u = ub.astype(f32)
            for cb in range(4):
                cols = slice(cb * 128, (cb + 1) * 128)
                st = slice(cb * 512, (cb + 1) * 512)
                l_r = lmr[:, st].astype(bf16)
                l_i = lmi[:, st].astype(bf16)
                du = (jnp.dot(l_r, bdr_ref[cb], preferred_element_type=f32)
                      + jnp.dot(l_i, bdi_ref[cb], preferred_element_type=f32))
                du_ref[:, cols] = (du + d_ref[:, cols] * dy[:, cols]).astype(bf16)
                dbr_ref[cb] += lax.dot_general(l_r, ub[:, cols], TN_DIMS, preferred_element_type=f32)
                dbi_ref[cb] += lax.dot_general(l_i, ub[:, cols], TN_DIMS, preferred_element_type=f32)
                dcr_ref[cb] += lax.dot_general(dyb[:, cols], sr_ref[:, st].astype(bf16), TN_DIMS,
                                               preferred_element_type=f32)
                dci_ref[cb] += lax.dot_general(dyb[:, cols], si_ref[:, st].astype(bf16), TN_DIMS,
                                               preferred_element_type=f32)
            dd_ref[...] += jnp.sum(dy * u, axis=0, keepdims=True)
        else:
            @pl.when(i == nblk - 1)
            def _():
                p_r, p_i = _pow_segment(ar_ref[...], ai_ref[...], seg_len)
                p_i = -p_i
                t_r, t_i = car[N_SEG - 1:N_SEG, :], cai[N_SEG - 1:N_SEG, :]
                or_ref[N_SEG - 1:N_SEG, :] = jnp.zeros((1, N_STATES), f32)
                oi_ref[N_SEG - 1:N_SEG, :] = jnp.zeros((1, N_STATES), f32)
                for j in range(N_SEG - 2, -1, -1):
                    or_ref[j:j + 1, :] = t_r
                    oi_ref[j:j + 1, :] = t_i
                    m_r, m_i = _cmul(p_r, p_i, t_r, t_i)
                    t_r, t_i = car[j:j + 1, :] + m_r, cai[j:j + 1, :] + m_i

    blk = lambda w: pl.BlockSpec((R, w), lambda i: (nblk - 1 - i, 0))
    in_specs = [blk(SSM_WIDTH), blk(SSM_WIDTH), _full((4, 128, 512)), _full((4, 128, 512)), _full((1, N_STATES)),
                _full((1, N_STATES))]
    args = [dys_perm, y_perm, cdt_r, cdt_i, ar, ai]
    seg = jax.ShapeDtypeStruct((N_SEG, N_STATES), f32)
    if store:
        in_specs += [blk(SSM_WIDTH), blk(N_STATES), blk(N_STATES), _full((4, 512, 128)), _full((4, 512, 128)),
                     _full((1, SSM_WIDTH)), _full((N_SEG, N_STATES)), _full((N_SEG, N_STATES))]
        args += list(final)
        out_specs = [blk(SSM_WIDTH), _full((4, 512, 128)), _full((4, 512, 128)), _full((4, 128, 512)),
                     _full((4, 128, 512)), _full((N_SEG, N_STATES)), _full((N_SEG, N_STATES)), _full((1, SSM_WIDTH))]
        b_acc = jax.ShapeDtypeStruct((4, 512, 128), f32)
        c_acc = jax.ShapeDtypeStruct((4, 128, 512), f32)
        out_shape = [jax.ShapeDtypeStruct((S, SSM_WIDTH), bf16), b_acc, b_acc, c_acc, c_acc, seg, seg,
                     jax.ShapeDtypeStruct((1, SSM_WIDTH), f32)]
        scratch = [pltpu.VMEM((R, N_STATES), f32)] * 4 + [pltpu.VMEM((N_SEG, N_STATES), f32)] * 2
        name = "ssm_scan_bwd"
    else:
        out_specs = [_full((N_SEG, N_STATES)), _full((N_SEG, N_STATES))]
        out_shape = [seg, seg]
        scratch = [pltpu.VMEM((R, N_STATES), f32)] * 2 + [pltpu.VMEM((N_SEG, N_STATES), f32)] * 2
        name = "ssm_scan_bwd_carry"
    res, landed = _pcall(body, args, name=name, grid=(nblk,), in_specs=in_specs, out_specs=out_specs,
                         out_shape=out_shape, scratch_shapes=scratch, rides=rides)
    return (res, landed) if rides else res


def ssm_param_bwd(dbb_r, dbb_i, bt_re, bt_im, kr, ki, ar, ai, lr, li, ldt, da_r, da_i):
    def body(dbr_ref, dbi_ref, br_ref, bi_ref, kr_ref, ki_ref, ar_ref, ai_ref, lr_ref, li_ref, ldt_ref, dar_ref,
             dai_ref, obr_ref, obi_ref, olr_ref, oli_ref, odt_ref):
        dbr, dbi, b_r, b_i = dbr_ref[...], dbi_ref[...], br_ref[...], bi_ref[...]
        k_r, k_i, a_r, a_i = kr_ref[...], ki_ref[...], ar_ref[...], ai_ref[...]
        l_r, l_i = lr_ref[...], li_ref[...]
        dt = jnp.exp(ldt_ref[...])
        obr_ref[...] = k_r * dbr + k_i * dbi
        obi_ref[...] = k_r * dbi - k_i * dbr
        gk_r = jnp.sum(dbr * b_r + dbi * b_i, axis=0, keepdims=True)
        gk_i = jnp.sum(dbi * b_r - dbr * b_i, axis=0, keepdims=True)
        ga_r = jnp.sum(dar_ref[...], axis=0, keepdims=True)
        ga_i = jnp.sum(dai_ref[...], axis=0, keepdims=True)
        den = l_r * l_r + l_i * l_i
        c_r, c_i = l_r / den, l_i / den
        m_r, m_i = _cmul(c_r, c_i, gk_r, gk_i)
        g_r, g_i = ga_r + m_r, ga_i + m_i
        t1_r, t1_i = _cmul(dt * a_r, -dt * a_i, g_r, g_i)
        q_r, q_i = _cmul(k_r, k_i, c_r, -c_i)
        t2_r, t2_i = _cmul(-q_r, q_i, gk_r, gk_i)
        olr_ref[...] = t1_r + t2_r
        oli_ref[...] = t1_i + t2_i
        w_r, w_i = _cmul(l_r, l_i, a_r, a_i)
        odt_ref[...] = dt * (w_r * g_r + w_i * g_i)

    v = jax.ShapeDtypeStruct((1, N_STATES), f32)
    m = jax.ShapeDtypeStruct((SSM_GROUP, N_STATES), f32)
    return pl.pallas_call(body, name="ssm_param_bwd", out_shape=[m, m, v, v, v])(
        dbb_r, dbb_i, bt_re, bt_im, kr, ki, ar, ai, lr, li, ldt, da_r, da_i)


def group_sum(v):
    def body(v_ref, o_ref):
        o_ref[...] = jnp.sum(v_ref[...], axis=-1, keepdims=True)
    return pl.pallas_call(body, name="group_sum", out_shape=jax.ShapeDtypeStruct((v.shape[0], 1), f32))(v)


ATTN_SCALE = XATTN_HEAD_DIM ** -0.5


def _attn_probs(q_h, k_h):
    s = lax.dot_general(q_h, k_h, NT_DIMS, preferred_element_type=f32) * ATTN_SCALE
    e = jnp.exp(s - jnp.max(s, axis=-1, keepdims=True))
    return e / jnp.sum(e, axis=-1, keepdims=True)


def attn_fwd(proj, kv, wxo3, tm=TM):
    S = proj.shape[0]
    M = kv.shape[0]
    J, _, n = wxo3.shape

    def body(q_ref, kv_ref, w_ref, o_ref, y_ref):
        for h in range(XATTN_HEADS):
            cols = slice(h * XATTN_HEAD_DIM, (h + 1) * XATTN_HEAD_DIM)
            q_h = q_ref[:, cols].astype(bf16)
            k_h = kv_ref[:, cols]
            v_h = kv_ref[:, XATTN_WIDTH + h * XATTN_HEAD_DIM:XATTN_WIDTH + (h + 1) * XATTN_HEAD_DIM]
            p = _attn_probs(q_h, k_h)
            o_ref[:, cols] = jnp.dot(p.astype(bf16), v_h, preferred_element_type=f32).astype(bf16)
        o = o_ref[...]
        for j in range(J):
            y_ref[:, j * n:(j + 1) * n] = jnp.dot(o, w_ref[j], preferred_element_type=f32).astype(bf16)

    return pl.pallas_call(
        body, name="attn_fwd", grid=(S // tm,),
        in_specs=[_rows(tm, XATTN_WIDTH, 10), _full((M, 2 * XATTN_WIDTH)), _full(wxo3.shape)],
        out_specs=[_rows(tm, XATTN_WIDTH), _rows(tm, J * n)],
        out_shape=[jax.ShapeDtypeStruct((S, XATTN_WIDTH), bf16), jax.ShapeDtypeStruct((S, J * n), bf16)],
        compiler_params=_params())(proj, kv, wxo3)


def attn_bwd(dy_c, o_att, wxo3, proj, kv, dproj, tm=TM):
    S = proj.shape[0]
    M = kv.shape[0]
    J, _, n = wxo3.shape
    nt = S // tm

    def body(dy_ref, o_ref, w_ref, q_ref, kv_ref, _, dq_ref, dkv_ref, dwo_ref, acc_ref):
        i = pl.program_id(0)

        @pl.when(i == 0)
        def _():
            dkv_ref[...] = jnp.zeros_like(dkv_ref)

        part = lax.dot_general(o_ref[...], dy_ref[...], TN_DIMS, preferred_element_type=f32)

        @pl.when(i == 0)
        def _():
            acc_ref[...] = part

        @pl.when(i > 0)
        def _():
            acc_ref[...] += part

        @pl.when(i == nt - 1)
        def _():
            for j in range(J):
                dwo_ref[j] = acc_ref[:, j * n:(j + 1) * n].astype(bf16)

        do = None
        for j in range(J):
            part = lax.dot_general(dy_ref[:, j * n:(j + 1) * n], w_ref[j], NT_DIMS, preferred_element_type=f32)
            do = part if do is None else do + part
        do = do.astype(bf16)
        for h in range(XATTN_HEADS):
            cols = slice(h * XATTN_HEAD_DIM, (h + 1) * XATTN_HEAD_DIM)
            vcols = slice(XATTN_WIDTH + h * XATTN_HEAD_DIM, XATTN_WIDTH + (h + 1) * XATTN_HEAD_DIM)
            q_h = q_ref[:, cols].astype(bf16)
            k_h = kv_ref[:, cols]
            v_h = kv_ref[:, vcols]
            do_h = do[:, cols]
            p = _attn_probs(q_h, k_h)
            dp = lax.dot_general(do_h, v_h, NT_DIMS, preferred_element_type=f32)
            ds = (p * (dp - jnp.sum(dp * p, axis=-1, keepdims=True)) * ATTN_SCALE).astype(bf16)
            dq_ref[:, cols] = jnp.dot(ds, k_h, preferred_element_type=f32).astype(bf16)
            dkv_ref[:, cols] += lax.dot_general(ds, q_h, TN_DIMS, preferred_element_type=f32)
            dkv_ref[:, vcols] += lax.dot_general(p.astype(bf16), do_h, TN_DIMS, preferred_element_type=f32)

    return pl.pallas_call(
        body, name="attn_bwd", grid=(S // tm,),
        in_specs=[_rows(tm, J * n), _rows(tm, XATTN_WIDTH), _full(wxo3.shape), _rows(tm, XATTN_WIDTH, 10),
                  _full((M, 2 * XATTN_WIDTH)), pl.BlockSpec(memory_space=pl.ANY)],
        out_specs=[_rows(tm, XATTN_WIDTH, 10), _full((M, 2 * XATTN_WIDTH)), _full(wxo3.shape)],
        out_shape=[jax.ShapeDtypeStruct(dproj.shape, bf16), jax.ShapeDtypeStruct((M, 2 * XATTN_WIDTH), f32),
                   jax.ShapeDtypeStruct(wxo3.shape, bf16)],
        scratch_shapes=[pltpu.VMEM((XATTN_WIDTH, J * n), f32)], input_output_aliases={5: 0},
        compiler_params=_params())(dy_c, o_att, wxo3, proj, kv, dproj)


def _ln_stats(r):
    mu = jnp.mean(r, axis=-1, keepdims=True)
    xc = r - mu
    var = jnp.mean(xc * xc, axis=-1, keepdims=True)
    rstd = lax.rsqrt(var + LN_EPS)
    return xc * rstd, rstd


def _ln_bwd(dy, xhat, rstd, g):
    dxh = dy * g
    return rstd * (dxh - jnp.mean(dxh, axis=-1, keepdims=True) - xhat * jnp.mean(dxh * xhat, axis=-1, keepdims=True))


def _colsum(v):
    return jnp.sum(v, axis=0, keepdims=True)


def _ln2_loss_epi(t2, h1, target, b_down, g, b):
    xhat, rstd = _ln_stats(ALPHA * h1 + t2 + b_down)
    err = xhat * g + b - target
    dout = err * (1.0 / D_MODEL)
    dr = _ln_bwd(dout, xhat, rstd, g)
    return (dr, dr), (0.5 * _colsum(err * err) * (1.0 / D_MODEL), _colsum(dout * xhat), _colsum(dout), _colsum(dr))


def _merge_fwd_pro(g0, g1, g2, y_a, glu_a, glu_b, y_c, b):
    gs = [_sig(g.astype(f32) + b[:, k * D_MODEL:(k + 1) * D_MODEL]) for k, g in enumerate((g0, g1, g2))]
    y_b = glu_a.astype(f32) * _sig(glu_b.astype(f32))
    return gs[0] * y_a.astype(f32) + gs[1] * y_b + gs[2] * y_c.astype(f32)


def _merge_bwd_epi(dm, g0, g1, g2, y_a, glu_a, glu_b, y_c, b):
    ga, sb = glu_a.astype(f32), _sig(glu_b.astype(f32))
    ys = (y_a.astype(f32), ga * sb, y_c.astype(f32))
    gs = [_sig(g.astype(f32) + b[:, k * D_MODEL:(k + 1) * D_MODEL]) for k, g in enumerate((g0, g1, g2))]
    dpre = [dm * ys[k] * gs[k] * (1.0 - gs[k]) for k in range(3)]
    dyb = dm * gs[1]
    dglu = jnp.concatenate([dyb * sb, dyb * ga * sb * (1.0 - sb)], axis=1)
    return ((jnp.concatenate(dpre, axis=1), dm * gs[0], dglu, dm * gs[2]),
            (jnp.concatenate([_colsum(d) for d in dpre], axis=1),))


def _ln1_fwd_epi(t1, x, g, b):
    xhat, _ = _ln_stats(ALPHA * x + t1)
    h = xhat * g + b
    return (t1, h, h), ()


def _ln1_bwd_epi(t3, x, t1, dr2, g):
    xhat, rstd = _ln_stats(ALPHA * x + t1)
    dh = ALPHA * dr2 + t3
    dr = _ln_bwd(dh, xhat, rstd, g)
    return (dr, dr), (_colsum(dh * xhat), _colsum(dh))


def exchange(*rides, name):
    return _pcall(lambda: None, [], name=name, grid=(), in_specs=[], out_specs=[], out_shape=[], rides=rides)[1]


HBM_ONLY = pl.BlockSpec(memory_space=pltpu.HBM)
SEM_SPEC = pl.BlockSpec(memory_space=pltpu.SEMAPHORE)
DATAFLOW = pltpu.SideEffectType.DATAFLOW_SIDE_EFFECTING


def scatter_start(g):
    def body(g_ref, land_ref, s0, s1, s2, r0, r1, r2, g_thru, land_thru, token):
        del g_thru, land_thru
        x, y, c = lax.axis_index("x"), lax.axis_index("y"), lax.axis_index("c")
        me = 2 * x + y
        for k, (px, py) in enumerate(_xy_peers(x, y)):
            pltpu.make_async_remote_copy(src_ref=g_ref.at[2 * px + py], dst_ref=land_ref.at[me],
                                         send_sem=(s0, s1, s2)[k], recv_sem=(r0, r1, r2)[k],
                                         device_id=(px, py, c), device_id_type=MESH).start()
        token[...] = jnp.zeros_like(token)

    sem = pltpu.SemaphoreType.DMA(())
    return pl.pallas_call(
        body, name="scatter_start",
        out_shape=(sem,) * 6 + (pltpu.HBM(g.shape, g.dtype), pltpu.HBM(g.shape, g.dtype),
                                jax.ShapeDtypeStruct((1, D_MODEL), f32)),
        in_specs=(HBM_ONLY, HBM_ONLY), out_specs=(SEM_SPEC,) * 6 + (HBM_ONLY, HBM_ONLY, _full((1, D_MODEL))),
        input_output_aliases={0: 6, 1: 7}, compiler_params=pltpu.CompilerParams(has_side_effects=DATAFLOW),
    )(pltpu.with_memory_space_constraint(g, pltpu.HBM),
      pltpu.with_memory_space_constraint(lax.empty(g.shape, g.dtype), pltpu.HBM))


def scatter_wait(sems, g_thru, land_thru, after):
    def body(g_ref, land_ref, s0, s1, s2, r0, r1, r2, after_ref, g_out, land_out):
        del after_ref, g_out, land_out
        x, y, c = lax.axis_index("x"), lax.axis_index("y"), lax.axis_index("c")
        for k, (px, py) in enumerate(_xy_peers(x, y)):
            copy = pltpu.make_async_remote_copy(src_ref=g_ref.at[2 * px + py], dst_ref=land_ref.at[2 * px + py],
                                                send_sem=(s0, s1, s2)[k], recv_sem=(r0, r1, r2)[k],
                                                device_id=(px, py, c), device_id_type=MESH)
            copy.wait_send()
            copy.wait_recv()

    return pl.pallas_call(
        body, name="scatter_wait",
        out_shape=(pltpu.HBM(g_thru.shape, g_thru.dtype), pltpu.HBM(g_thru.shape, g_thru.dtype)),
        in_specs=(HBM_ONLY, HBM_ONLY) + (SEM_SPEC,) * 6 + (pl.BlockSpec(memory_space=pl.ANY),),
        out_specs=(HBM_ONLY, HBM_ONLY), input_output_aliases={0: 0, 1: 1},
        compiler_params=pltpu.CompilerParams(has_side_effects=DATAFLOW),
    )(g_thru, land_thru, *sems, after)


ROW_TILE = 256


def sum_slots_own(landed, g, me, name):
    n, R, C = landed.shape
    tm = min(R, ROW_TILE)

    def body(me_ref, l_ref, g_ref, o_ref):
        own = g_ref[0].astype(f32)
        acc = None
        for k in range(n):
            term = jnp.where(me_ref[0] == k, own, l_ref[k].astype(f32))
            acc = term if acc is None else acc + term
        o_ref[...] = acc

    return pl.pallas_call(
        body, name=name, out_shape=jax.ShapeDtypeStruct((R, C), f32), compiler_params=_params(),
        grid_spec=pltpu.PrefetchScalarGridSpec(
            num_scalar_prefetch=1, grid=(R // tm,),
            in_specs=[pl.BlockSpec((n, tm, C), lambda i, me_ref: (0, i, 0)),
                      pl.BlockSpec((1, tm, C), lambda i, me_ref: (me_ref[0], i, 0))],
            out_specs=pl.BlockSpec((tm, C), lambda i, me_ref: (i, 0))))(me, landed, g)


def sum_slots(recv, name):
    n, R, C = recv.shape
    tm = min(R, ROW_TILE)

    def body(r_ref, o_ref):
        acc = r_ref[0].astype(f32)
        for k in range(1, n):
            acc = acc + r_ref[k].astype(f32)
        o_ref[...] = acc

    return pl.pallas_call(
        body, name=name, grid=(R // tm,), in_specs=[pl.BlockSpec((n, tm, C), lambda i: (0, i, 0))],
        out_specs=_rows(tm, C), out_shape=jax.ShapeDtypeStruct((R, C), f32), compiler_params=_params())(recv)


def _adamw(w, g, m, v):
    m = ADAM_B1 * m + (1.0 - ADAM_B1) * g
    v = ADAM_B2 * v + (1.0 - ADAM_B2) * (g * g)
    m_hat = m / (1.0 - ADAM_B1 ** ADAM_STEP)
    v_hat = v / (1.0 - ADAM_B2 ** ADAM_STEP)
    delta = -ADAM_LR * (m_hat / (jnp.sqrt(v_hat) + ADAM_EPS) + ADAM_WD * w)
    return delta, m, v


def adam_pair(p, q, w, m, v, name):
    R, C = w.shape
    tm = min(R, ROW_TILE)

    def body(p_ref, q_ref, w_ref, m_ref, v_ref, g_ref, d_ref, nm_ref, nv_ref):
        g = p_ref[...] + q_ref[...]
        g_ref[...] = g
        d_ref[...], nm_ref[...], nv_ref[...] = _adamw(w_ref[...], g, m_ref[...], v_ref[...])

    o = jax.ShapeDtypeStruct((R, C), f32)
    return pl.pallas_call(body, name=name, grid=(R // tm,), in_specs=[_rows(tm, C)] * 5,
                          out_specs=[_rows(tm, C)] * 4, out_shape=[o] * 4, compiler_params=_params())(p, q, w, m, v)


SC_TILES = 32
SC_ROWS = 8


def adam_pair_sc(p, q, w, m, v, name):
    R, C = w.shape
    per_tile = R // SC_TILES

    def body(p_hbm, q_hbm, w_hbm, m_hbm, v_hbm, g_hbm, d_hbm, nm_hbm, nv_hbm, pb, qb, wb, mb, vb, gb, db, nmb, nvb):
        tile = lax.axis_index("sc_tile") * 2 + lax.axis_index("sc_core")

        @pl.loop(0, per_tile, step=SC_ROWS)
        def _(r0):
            rows = pl.ds(tile * per_tile + r0, SC_ROWS)
            for src, buf in ((p_hbm, pb), (q_hbm, qb), (w_hbm, wb), (m_hbm, mb), (v_hbm, vb)):
                pltpu.sync_copy(src.at[rows], buf)

            @pl.loop(0, SC_ROWS)
            def _(r):
                @pl.loop(0, C, step=16)
                def _(i):
                    at = (r, pl.ds(i, 16))
                    g = pb[at] + qb[at]
                    d, nm, nv = _adamw(wb[at], g, mb[at], vb[at])
                    gb[at], db[at], nmb[at], nvb[at] = g, d, nm, nv

            for buf, dst in ((gb, g_hbm), (db, d_hbm), (nmb, nm_hbm), (nvb, nv_hbm)):
                pltpu.sync_copy(buf, dst.at[rows])

    o = jax.ShapeDtypeStruct((R, C), f32)
    return pl.kernel(
        body, name=name, out_type=[o] * 4,
        mesh=plsc.VectorSubcoreMesh(core_axis_name="sc_core", subcore_axis_name="sc_tile"),
        scratch_types=[pltpu.VMEM((SC_ROWS, C), f32)] * 9)(p, q, w, m, v)


def adam_plain(w, g, m, v, name):
    def body(w_ref, g_ref, m_ref, v_ref, d_ref, nm_ref, nv_ref):
        d_ref[...], nm_ref[...], nv_ref[...] = _adamw(w_ref[...], g_ref[...], m_ref[...], v_ref[...])

    o = jax.ShapeDtypeStruct(w.shape, f32)
    return pl.pallas_call(body, name=name, out_shape=[o] * 3, compiler_params=_params())(w, g, m, v)


def adam_slots(recv, w, m, v, name):
    n = recv.shape[0]

    def body(r_ref, w_ref, m_ref, v_ref, g_ref, d_ref, nm_ref, nv_ref):
        g = r_ref[0]
        for k in range(1, n):
            g = g + r_ref[k]
        g_ref[...] = g
        d_ref[...], nm_ref[...], nv_ref[...] = _adamw(w_ref[...], g, m_ref[...], v_ref[...])

    o = jax.ShapeDtypeStruct(w.shape, f32)
    return pl.pallas_call(body, name=name, out_shape=[o] * 4, compiler_params=_params())(recv, w, m, v)


BIG = ("w_in", "w_conv_out", "w_glu", "w_xattn_out", "w_kv", "w_out", "w_up", "w_down")
MID = ("w_conv_out", "w_glu", "w_xattn_out", "w_kv", "w_out")
WIDE_MINOR = ("ssm_b_re", "ssm_b_im", "ssm_c_re", "ssm_c_im")
SMALL = ("b_gate", "ssm_lam_re", "ssm_lam_im", "ssm_log_dt", "ssm_b_re", "ssm_b_im", "ssm_c_re", "ssm_c_im", "ssm_d",
         "ln1_g", "ln1_b", "b_up", "b_down", "ln2_g", "ln2_b")
def _pad_flat(a, mult=1024):
    a = a.reshape(-1)
    return jnp.pad(a, (0, (-a.shape[0]) % mult))


def _perm(a):
    S, W = a.shape
    return a.reshape(N_SEG, S // N_SEG, W).transpose(1, 0, 2).reshape(S, W)


def _unperm(a):
    S, W = a.shape
    return a.reshape(S // N_SEG, N_SEG, W).transpose(1, 0, 2).reshape(S, W)


def _state_rows(a):
    return a.transpose(2, 0, 1).reshape(SSM_GROUP, N_STATES)


def _block_diag_b(bt):
    b4 = bt.reshape(SSM_GROUP, 4, 8, SSM_STATE)
    eye = jnp.eye(8, dtype=bt.dtype)
    return jnp.einsum("hcgp,gk->cghkp", b4, eye).reshape(4, 128, 512)


def _block_diag_c(c):
    c4 = c.reshape(4, 8, SSM_GROUP, SSM_STATE)
    eye = jnp.eye(8, dtype=c.dtype)
    return jnp.einsum("cghp,gk->cgpkh", c4, eye).reshape(4, 512, 128)


def _diag_b(acc):
    a = acc.reshape(4, 8, SSM_STATE, 8, SSM_GROUP)
    eye = jnp.eye(8, dtype=acc.dtype)
    return jnp.einsum("cgpkh,gk->hcgp", a, eye).reshape(SSM_GROUP, N_STATES)


def _diag_c(acc):
    a = acc.reshape(4, 8, SSM_GROUP, 8, SSM_STATE)
    eye = jnp.eye(8, dtype=acc.dtype)
    return jnp.einsum("cghkp,gk->cghp", a, eye).reshape(SSM_GROUPS, SSM_GROUP, SSM_STATE)


def kernel(x, mem, w_in, b_gate, conv_w, w_conv_out, ssm_lam_re, ssm_lam_im, ssm_log_dt, ssm_b_re, ssm_b_im, ssm_c_re, ssm_c_im, ssm_d, w_glu, w_kv, w_xattn_out, w_out, ln1_g, ln1_b, w_up, b_up, w_down, b_down, ln2_g, ln2_b, loss_target, m_w_in, m_b_gate, m_conv_w, m_w_conv_out, m_ssm_lam_re, m_ssm_lam_im, m_ssm_log_dt, m_ssm_b_re, m_ssm_b_im, m_ssm_c_re, m_ssm_c_im, m_ssm_d, m_w_glu, m_w_kv, m_w_xattn_out, m_w_out, m_ln1_g, m_ln1_b, m_w_up, m_b_up, m_w_down, m_b_down, m_ln2_g, m_ln2_b, v_w_in, v_b_gate, v_conv_w, v_w_conv_out, v_ssm_lam_re, v_ssm_lam_im, v_ssm_log_dt, v_ssm_b_re, v_ssm_b_im, v_ssm_c_re, v_ssm_c_im, v_ssm_d, v_w_glu, v_w_kv, v_w_xattn_out, v_w_out, v_ln1_g, v_ln1_b, v_w_up, v_b_up, v_w_down, v_b_down, v_ln2_g, v_ln2_b):
    W = dict(w_in=w_in, b_gate=b_gate, conv_w=conv_w, w_conv_out=w_conv_out, ssm_lam_re=ssm_lam_re,
             ssm_lam_im=ssm_lam_im, ssm_log_dt=ssm_log_dt, ssm_b_re=ssm_b_re, ssm_b_im=ssm_b_im, ssm_c_re=ssm_c_re,
             ssm_c_im=ssm_c_im, ssm_d=ssm_d, w_glu=w_glu, w_kv=w_kv, w_xattn_out=w_xattn_out, w_out=w_out,
             ln1_g=ln1_g, ln1_b=ln1_b, w_up=w_up, b_up=b_up, w_down=w_down, b_down=b_down, ln2_g=ln2_g, ln2_b=ln2_b)
    MOM = dict(w_in=m_w_in, b_gate=m_b_gate, conv_w=m_conv_w, w_conv_out=m_w_conv_out, ssm_lam_re=m_ssm_lam_re,
               ssm_lam_im=m_ssm_lam_im, ssm_log_dt=m_ssm_log_dt, ssm_b_re=m_ssm_b_re, ssm_b_im=m_ssm_b_im,
               ssm_c_re=m_ssm_c_re, ssm_c_im=m_ssm_c_im, ssm_d=m_ssm_d, w_glu=m_w_glu, w_kv=m_w_kv,
               w_xattn_out=m_w_xattn_out, w_out=m_w_out, ln1_g=m_ln1_g, ln1_b=m_ln1_b, w_up=m_w_up, b_up=m_b_up,
               w_down=m_w_down, b_down=m_b_down, ln2_g=m_ln2_g, ln2_b=m_ln2_b)
    VEL = dict(w_in=v_w_in, b_gate=v_b_gate, conv_w=v_conv_w, w_conv_out=v_w_conv_out, ssm_lam_re=v_ssm_lam_re,
               ssm_lam_im=v_ssm_lam_im, ssm_log_dt=v_ssm_log_dt, ssm_b_re=v_ssm_b_re, ssm_b_im=v_ssm_b_im,
               ssm_c_re=v_ssm_c_re, ssm_c_im=v_ssm_c_im, ssm_d=v_ssm_d, w_glu=v_w_glu, w_kv=v_w_kv,
               w_xattn_out=v_w_xattn_out, w_out=v_w_out, ln1_g=v_ln1_g, ln1_b=v_ln1_b, w_up=v_w_up, b_up=v_b_up,
               w_down=v_w_down, b_down=v_b_down, ln2_g=v_ln2_g, ln2_b=v_ln2_b)
    names = list(W)
    xy = 2 * lax.axis_index("x") + lax.axis_index("y")

    xs = x[0]
    S = xs.shape[0]
    mems = mem[0]
    tgt = loss_target[0]

    shard_bf = {n: W[n][0].astype(bf16) for n in BIG}

    lr = ssm_lam_re.reshape(1, N_STATES)
    li = ssm_lam_im.reshape(1, N_STATES)
    ldt = jnp.repeat(ssm_log_dt.reshape(SSM_GROUPS), SSM_STATE).reshape(1, N_STATES)
    bt_re, bt_im = _state_rows(ssm_b_re[0]), _state_rows(ssm_b_im[0])
    ar, ai, kr, ki, bbt_r, bbt_i = ssm_prep(lr, li, ldt, bt_re, bt_im)
    bd_r, bd_i = _block_diag_b(bbt_r).astype(bf16), _block_diag_b(bbt_i).astype(bf16)
    cd_r, cd_i = _block_diag_c(ssm_c_re[0]).astype(bf16), _block_diag_c(ssm_c_im[0]).astype(bf16)
    bdt_r, bdt_i = bd_r.transpose(0, 2, 1), bd_i.transpose(0, 2, 1)
    cdt_r, cdt_i = cd_r.transpose(0, 2, 1), cd_i.transpose(0, 2, 1)
    d_skip = ssm_d.reshape(1, SSM_WIDTH)

    mem_bf = mems.astype(bf16)
    u_col = GATE_COLS + 3 * CONV_WIDTH
    me_xy = jnp.reshape(xy, (1,)).astype(jnp.int32)
    proj, x_bf, (win3,) = proj_own(xs, shard_bf["w_in"], me_xy, 4, rides=(("gather2", shard_bf["w_in"]),))
    proj, (wco3, wglu3, wxo3, wkv4, wout4, convw4) = proj_rest(
        x_bf, win3, me_xy, proj, rides=tuple(("gather", shard_bf[n]) for n in MID) + (("gather", conv_w[0]),))
    wkv3 = wkv4.reshape(1, D_MODEL, 2 * XATTN_WIDTH)
    wout3 = wout4.reshape(1, D_MODEL, D_MODEL)
    convw_full = convw4.transpose(1, 0, 2).reshape(3, CONV_WIDTH)
    a_conv, y_a = conv_fwd(proj, convw_full, wco3)
    u_perm = _perm(proj[:, u_col:u_col + SSM_WIDTH])
    (init_r, init_i), (wup3,) = ssm_scan_fwd(u_perm, bd_r, bd_i, ar, ai, rides=(("gather2", shard_bf["w_up"]),))
    ysm_perm, ys_perm, st_r, st_i = ssm_scan_fwd(u_perm, bd_r, bd_i, ar, ai,
                                                 final=(cd_r, cd_i, d_skip, init_r, init_i))
    y_s = _unperm(ys_perm)
    glu = mm_fwd_small(y_s, wglu3, "glu", out_dtype=bf16)
    kv = mm_fwd_small(mem_bf, wkv3, "kv", out_dtype=bf16)
    o_att, y_c = attn_fwd(proj, kv, wxo3)
    merge_rows = ((proj, D_MODEL, 0), (proj, D_MODEL, 1), (proj, D_MODEL, 2), y_a, (glu, D_MODEL, 0),
                  (glu, D_MODEL, 1), y_c)
    merged, t1, h1, h1_bf = mm_rows_fused(
        (_merge_fwd_pro, merge_rows, (b_gate,)), wout3, "merge_w_out_ln1", transposed=False, tm=256,
        epi=_ln1_fwd_epi, extras=(xs,), consts=(ln1_g, ln1_b), outs=(f32, f32, bf16))
    (r_up, hdn), (wdn4,) = mm_fwd(h1_bf, wup3, "w_up", bias=b_up, outs=(bf16, bf16), second=_relu2,
                                  first=lambda v: jnp.maximum(v, 0.0), rides=(("gather2", shard_bf["w_down"]),))
    wdn3 = wdn4.reshape(1, D_FF, D_MODEL)

    dr2, dr2_bf, loss_cols, d_ln2_g, d_ln2_b, d_b_down = mm_rows_fused(
        hdn, wdn3, "w_down_ln2_loss", transposed=False, tm=512, epi=_ln2_loss_epi, extras=(h1, tgt),
        consts=(b_down, ln2_g, ln2_b), outs=(f32, bf16), nsums=4)
    part, other = {}, {}
    g_w_down = mm_bwd_w(hdn, dr2_bf, 1, "dw_down", tk=1024).reshape(4, -1, D_MODEL)
    dup, d_b_up = mm_bwd_x(dr2_bf, wdn3, "dup", tm=512, extras=(r_up,), colsum=True, out_dtype=bf16,
                           epi=lambda acc, r: acc * (2.0 * r.astype(f32)))
    g_w_up = mm_bwd_w(h1_bf, dup, 4, "dw_up")
    (dr1, dr1_bf, d_ln1_g, d_ln1_b), (recv_dn,) = mm_rows_fused(
        dup, wup3, "dh1_ln1_bwd", transposed=True, tm=512, epi=_ln1_bwd_epi, extras=(xs, t1, dr2), consts=(ln1_g,),
        outs=(f32, bf16), nsums=2, rides=(("scatter", g_w_down),))
    part["w_down"] = sum_slots(recv_dn, "sum_w_down")
    g_w_out = mm_bwd_w(merged, dr1_bf, 1, "dw_out").reshape(4, -1, D_MODEL)
    (dproj, dy_a, dglu, dy_c, d_b_gate), (recv_up,) = mm_rows_fused(
        dr1_bf, wout3, "dmerged_merge_bwd", transposed=True, tm=256, epi=_merge_bwd_epi, extras=merge_rows,
        consts=(b_gate,), outs=((bf16, GATE_COLS, IN_COLS), bf16, (bf16, 2 * D_MODEL, 2 * D_MODEL), bf16),
        sum_widths=(GATE_COLS,), rides=(("scatter", g_w_up),))
    part["w_up"] = sum_slots(recv_up, "sum_w_up")

    dproj, d_conv_w, g_w_co = conv_bwd(dy_a, a_conv, wco3, proj, convw_full, dproj)

    g_w_glu = mm_bwd_w_small(y_s, dglu, 4, "dw_glu")
    dys_perm = _perm(mm_bwd_x(dglu, wglu3, "dy_s", tm=1024))
    linit_r, linit_i = ssm_scan_bwd(dys_perm, ysm_perm, cdt_r, cdt_i, ar, ai)
    (du_perm, dbacc_r, dbacc_i, dcacc_r, dcacc_i, da_r, da_i, d_ssm_d), (recv_co, recv_glu, recv_out) = ssm_scan_bwd(
        dys_perm, ysm_perm, cdt_r, cdt_i, ar, ai,
        final=(u_perm, st_r, st_i, bdt_r, bdt_i, d_skip, linit_r, linit_i),
        rides=(("scatter", g_w_co), ("scatter", g_w_glu), ("scatter", g_w_out)))
    part["w_out"] = sum_slots(recv_out, "sum_w_out")
    part["w_conv_out"] = sum_slots(recv_co, "sum_w_conv_out")
    part["w_glu"] = sum_slots(recv_glu, "sum_w_glu")
    dbt_re, dbt_im, d_lr, d_li, d_ldt_state = ssm_param_bwd(
        _diag_b(dbacc_r), _diag_b(dbacc_i), bt_re, bt_im, kr, ki, ar, ai, lr, li, ldt, da_r, da_i)
    d_log_dt = group_sum(d_ldt_state.reshape(SSM_GROUPS, SSM_STATE))
    d_b_re = dbt_re.reshape(SSM_GROUP, SSM_GROUPS, SSM_STATE).transpose(1, 2, 0)
    d_b_im = dbt_im.reshape(SSM_GROUP, SSM_GROUPS, SSM_STATE).transpose(1, 2, 0)
    d_c_re = _diag_c(dcacc_r)
    d_c_im = -_diag_c(dcacc_i)

    dproj, dkv, g_w_xo = attn_bwd(dy_c, o_att, wxo3, proj, kv, dproj)
    g_w_kv = mm_bwd_w(mem_bf, dkv, 1, "dw_kv").reshape(4, -1, D_MODEL)

    dproj = lax.dynamic_update_slice(dproj, _unperm(du_perm), (0, u_col))
    g_small = {"b_gate": d_b_gate, "ssm_lam_re": d_lr, "ssm_lam_im": d_li, "ssm_log_dt": d_log_dt, "ssm_b_re": d_b_re,
               "ssm_b_im": d_b_im, "ssm_c_re": d_c_re, "ssm_c_im": d_c_im, "ssm_d": d_ssm_d, "ln1_g": d_ln1_g,
               "ln1_b": d_ln1_b, "b_up": d_b_up, "b_down": d_b_down, "ln2_g": d_ln2_g, "ln2_b": d_ln2_b}
    small_names = SMALL + ("conv_w",)
    g_small["conv_w"] = d_conv_w
    sizes = {n: (g_small[n].size + 1023) // 1024 * 1024 for n in small_names}
    pack = lambda d: jnp.concatenate([_pad_flat(d[n]) for n in small_names]).reshape(-1, 128)
    early = ("w_down", "w_up", "w_out", "w_conv_out", "w_glu")
    g_w_in, landed = mm_bwd_w(
        x_bf, dproj, 4, "dw_in", rides=(("scatter", g_w_xo), ("scatter", g_w_kv), ("all", pack(g_small)))
        + tuple(("pair", part[n]) for n in early))
    recv_xo, recv_kv, srecv = landed[:3]
    other.update(zip(early, landed[3:]))
    part["w_xattn_out"] = sum_slots(recv_xo, "sum_w_xattn_out")
    part["w_kv"] = sum_slots(recv_kv, "sum_w_kv")
    *sems, g_thru, land_thru, token = scatter_start(g_w_in)
    res = [{}, {}, {}, {}]
    for n in early:
        for k, r in enumerate(adam_pair_sc(part[n], other[n], W[n][0], MOM[n][0], VEL[n][0], "adam_sc_" + n)):
            res[k][n] = r[None]
    dx, = mm_rows_fused(dproj, win3, "dx", transposed=True, tm=512, extras=(dr1,), consts=(token,),
                        epi=lambda acc, d, tk: ((acc + ALPHA * d + tk,), ()), outs=(f32,))
    g_own, landed_in = scatter_wait(sems, g_thru, land_thru, dx)
    late = ("w_in", "w_xattn_out", "w_kv")
    part["w_in"] = sum_slots_own(landed_in, g_own, me_xy, "sum_w_in")
    other.update(zip(late, exchange(*[("pair", part[n]) for n in late], name="swap_late")))
    for n in late:
        for k, r in enumerate(adam_pair(part[n], other[n], W[n][0], MOM[n][0], VEL[n][0], "adam_" + n)):
            res[k][n] = r[None]
    conv_zero = jnp.zeros((3, CONV_WIDTH), f32)
    in_pack = lambda d: pack({**{n: (jnp.zeros(W[n].shape, f32) if n in WIDE_MINOR else d[n]) for n in SMALL},
                              "conv_w": conv_zero})
    gs, ds_, ms, vs = adam_slots(srecv, in_pack(W), in_pack(MOM), in_pack(VEL), "adam_small")

    def unpack_small(buf, skip=()):
        flat = buf.reshape(-1)
        out, r = {}, 0
        for n in small_names:
            ref = g_small[n] if n == "conv_w" else W[n]
            if n not in skip:
                out[n] = flat[r:r + ref.size].reshape(ref.shape)
            r += sizes[n]
        return out

    res_s = [unpack_small(gs)] + [unpack_small(b, WIDE_MINOR) for b in (ds_, ms, vs)]
    for n in WIDE_MINOR:
        two_d = (-1, W[n].shape[-1])
        outs_n = adam_plain(W[n].reshape(two_d), res_s[0][n].reshape(two_d), MOM[n].reshape(two_d),
                            VEL[n].reshape(two_d), "adam_" + n)
        for k, r in enumerate(outs_n):
            res_s[k + 1][n] = r.reshape(W[n].shape)
    g_conv = lax.dynamic_slice(res_s[0]["conv_w"], (0, xy * 128), (3, 128))
    conv_slots = g_conv.reshape(1, 3, 128)
    cg, cd, cm, cv = adam_slots(conv_slots, conv_w[0], m_conv_w[0], v_conv_w[0], "adam_conv")
    conv_res = [cg, cd, cm, cv]

    loss = lax.psum(jnp.sum(loss_cols), ("x", "y", "c"))
    outs = [loss, dx.reshape(x.shape)]
    for k in range(4):
        for n in names:
            if n == "conv_w":
                outs.append(conv_res[k].reshape(conv_w.shape))
            elif n in BIG:
                outs.append(res[k][n])
            else:
                outs.append(res_s[k][n])
    return tuple(outs)
```
